```python
import math
import jax, jax.numpy as jnp
from jax import lax
import numpy as np

D_MODEL = 1024
BATCH = 8
SEQ = 4096
DEPTH = 2

CHUNK = 64
N_MEM = 256
MEM_HEADS = 4
MEM_HEAD_DIM = D_MODEL // MEM_HEADS
D_MIX = D_MODEL
POOL_WINDOWS = (2, 4, 8, 16)
POOL_WIDTH = D_MIX // 4
POOL_GROUP = POOL_WIDTH // len(POOL_WINDOWS)
QK_NOPE = 128
QK_ROPE = 64
V_HEAD = 128
MLA_HEADS = (D_MIX - POOL_WIDTH) // V_HEAD
Q_LORA = 256
KV_LORA = 128
ROPE_BASE = 10000.0
D_FF = 2816
Q_BLOCK = 128
D_IN = POOL_WIDTH + Q_LORA + KV_LORA + QK_ROPE
ALPHA = (2 * DEPTH) ** 0.25
BETA = (8 * DEPTH) ** -0.25
LN_EPS = 1e-5
RMS_EPS = 1e-6
NEG_INF = -1e30

kernel_name = 'hybrid_pool_mla_macaron_deepnorm'


def layer_norm(x, g, b):
    xf = x.astype(jnp.float32)
    mu = jnp.mean(xf, axis=-1, keepdims=True)
    var = jnp.mean(jnp.square(xf - mu), axis=-1, keepdims=True)
    y = (xf - mu) * lax.rsqrt(var + LN_EPS) * g.astype(jnp.float32) + b.astype(jnp.float32)
    return y.astype(x.dtype)


def rms_norm(x, g):
    xf = x.astype(jnp.float32)
    y = xf * lax.rsqrt(jnp.mean(jnp.square(xf), axis=-1, keepdims=True) + RMS_EPS)
    return (y * g.astype(jnp.float32)).astype(x.dtype)


def swiglu(x, w13, w2):
    gate, up = jnp.split(x @ w13, 2, axis=-1)
    return (jax.nn.silu(gate) * up) @ w2


def rope(x, positions):
    half = QK_ROPE // 2
    inv_freq = ROPE_BASE ** (-jnp.arange(half, dtype=jnp.float32) / half)
    ang = positions.astype(jnp.float32)[..., None] * inv_freq
    ang = ang.reshape(ang.shape[:2] + (1,) * (x.ndim - 3) + (half,))
    cos, sin = jnp.cos(ang), jnp.sin(ang)
    xf = x.astype(jnp.float32)
    x1, x2 = xf[..., :half], xf[..., half:]
    out = jnp.concatenate([x1 * cos - x2 * sin, x2 * cos + x1 * sin], axis=-1)
    return out.astype(x.dtype)


def pool_mixer(u, pool_w, pool_scale):
    B, S, _ = u.shape
    uf = u.astype(jnp.float32)
    cs = jnp.cumsum(uf, axis=1)
    t = jnp.arange(S)
    means = []
    for g, w in enumerate(POOL_WINDOWS):
        csg = cs[..., g * POOL_GROUP:(g + 1) * POOL_GROUP]
        prev = jnp.pad(csg[:, :S - w], ((0, 0), (w, 0), (0, 0)))
        cnt = jnp.minimum(t + 1, w).astype(jnp.float32)[None, :, None]
        means.append((csg - prev) / cnt)
    d = (jnp.concatenate(means, axis=-1) - uf).astype(u.dtype)
    d = d.reshape(B, S, len(POOL_WINDOWS), POOL_GROUP)
    y = jnp.einsum('bsgc,gcd->bsgd', d, pool_w).reshape(B, S, POOL_WIDTH)
    return y * pool_scale


def mla_mixer(c_q, c_kv, k_pe, positions, q_norm_g, w_uq, kv_norm_g, w_ukv):
    B, S, _ = c_q.shape
    H = MLA_HEADS
    q = (rms_norm(c_q, q_norm_g) @ w_uq).reshape(B, S, H, QK_NOPE + QK_ROPE)
    q = jnp.concatenate([q[..., :QK_NOPE], rope(q[..., QK_NOPE:], positions)], axis=-1)
    kv = (rms_norm(c_kv, kv_norm_g) @ w_ukv).reshape(B, S, H, QK_NOPE + V_HEAD)
    k_nope, v = kv[..., :QK_NOPE], kv[..., QK_NOPE:]
    k_rot = rope(k_pe, positions)
    k = jnp.concatenate([k_nope, jnp.broadcast_to(k_rot[:, :, None, :], (B, S, H, QK_ROPE))], axis=-1)
    scale = (QK_NOPE + QK_ROPE) ** -0.5
    nb = S // Q_BLOCK
    q_blocks = q.reshape(B, nb, Q_BLOCK, H, QK_NOPE + QK_ROPE).transpose(1, 0, 2, 3, 4)
    q_idx = jnp.arange(S).reshape(nb, Q_BLOCK)
    k_chunk = jnp.arange(S) // CHUNK

    def attend(args):
        qb, qi = args
        s = jnp.einsum('bqhd,bkhd->bhqk', qb, k, preferred_element_type=jnp.float32) * scale
        mask = (qi[:, None] // CHUNK) >= k_chunk[None, :]
        s = jnp.where(mask[None, None], s, NEG_INF)
        p = jax.nn.softmax(s, axis=-1).astype(v.dtype)
        return jnp.einsum('bhqk,bkhd->bqhd', p, v)

    o = lax.map(attend, (q_blocks, q_idx))
    return o.transpose(1, 0, 2, 3, 4).reshape(B, S, H * V_HEAD)


def memory_cross_attention(x, mem, wq, wkv, wo):
    B, S, _ = x.shape
    q = (x @ wq).reshape(B, S, MEM_HEADS, MEM_HEAD_DIM)
    k, v = jnp.split(mem @ wkv, 2, axis=-1)
    k = k.reshape(B, mem.shape[1], MEM_HEADS, MEM_HEAD_DIM)
    v = v.reshape(B, mem.shape[1], MEM_HEADS, MEM_HEAD_DIM)
    s = jnp.einsum('bshd,bmhd->bhsm', q, k, preferred_element_type=jnp.float32) * MEM_HEAD_DIM ** -0.5
    p = jax.nn.softmax(s, axis=-1).astype(v.dtype)
    o = jnp.einsum('bhsm,bmhd->bshd', p, v).reshape(B, S, D_MODEL)
    return o @ wo


def _fwd_setup_inputs(seed: int = 0) -> dict:
    key = jax.random.key(seed)
    ks = jax.random.split(key, 24)

    def nrm(k, shape, scale):
        return jax.random.normal(k, shape, jnp.float32) * scale

    x = nrm(ks[0], (BATCH, SEQ, D_MODEL), 1.0)
    mem = nrm(ks[1], (BATCH, N_MEM, D_MODEL), 1.0)
    start = jax.random.randint(ks[2], (BATCH, 1), 0, 8192, dtype=jnp.int32)
    positions = (start + jnp.arange(SEQ, dtype=jnp.int32)[None, :]).astype(jnp.int32)
    L = DEPTH
    return {
        'x': x,
        'mem': mem,
        'positions': positions,
        'ln_g': 1.0 + nrm(ks[3], (L, 4, D_MODEL), 0.05),
        'ln_b': nrm(ks[4], (L, 4, D_MODEL), 0.02),
        'ffn1_w13': nrm(ks[5], (L, D_MODEL, 2 * D_FF), D_MODEL ** -0.5),
        'ffn1_w2': nrm(ks[6], (L, D_FF, D_MODEL), BETA * D_FF ** -0.5),
        'w_in': nrm(ks[7], (L, D_MODEL, D_IN), D_MODEL ** -0.5),
        'pool_w': nrm(ks[8], (L, len(POOL_WINDOWS), POOL_GROUP, POOL_GROUP), POOL_GROUP ** -0.5),
        'pool_scale': 1.0 + nrm(ks[9], (L, POOL_WIDTH), 0.1),
        'q_norm_g': 1.0 + nrm(ks[10], (L, Q_LORA), 0.05),
        'w_uq': nrm(ks[11], (L, Q_LORA, MLA_HEADS * (QK_NOPE + QK_ROPE)), Q_LORA ** -0.5),
        'kv_norm_g': 1.0 + nrm(ks[12], (L, KV_LORA), 0.05),
        'w_ukv': nrm(ks[13], (L, KV_LORA, MLA_HEADS * (QK_NOPE + V_HEAD)), KV_LORA ** -0.5),
        'w_out': nrm(ks[14], (L, D_MIX, D_MODEL), BETA * D_MIX ** -0.5),
        'mem_wq': nrm(ks[15], (L, D_MODEL, D_MODEL), D_MODEL ** -0.5),
        'mem_wkv': nrm(ks[16], (L, D_MODEL, 2 * D_MODEL), D_MODEL ** -0.5),
        'mem_wo': nrm(ks[17], (L, D_MODEL, D_MODEL), BETA * D_MODEL ** -0.5),
        'ffn2_w13': nrm(ks[18], (L, D_MODEL, 2 * D_FF), D_MODEL ** -0.5),
        'ffn2_w2': nrm(ks[19], (L, D_FF, D_MODEL), BETA * D_FF ** -0.5),
    }


def _fwd_reference(x, mem, positions, ln_g, ln_b, ffn1_w13, ffn1_w2, w_in, pool_w, pool_scale,
              q_norm_g, w_uq, kv_norm_g, w_ukv, w_out, mem_wq, mem_wkv, mem_wo,
              ffn2_w13, ffn2_w2):
    for l in range(DEPTH):
        x = layer_norm(ALPHA * x + 0.5 * swiglu(x, ffn1_w13[l], ffn1_w2[l]), ln_g[l, 0], ln_b[l, 0])
        h = x @ w_in[l]
        o0 = POOL_WIDTH
        o1 = o0 + Q_LORA
        o2 = o1 + KV_LORA
        u_pool, c_q, c_kv, k_pe = h[..., :o0], h[..., o0:o1], h[..., o1:o2], h[..., o2:]
        y_pool = pool_mixer(u_pool, pool_w[l], pool_scale[l])
        y_mla = mla_mixer(c_q, c_kv, k_pe, positions, q_norm_g[l], w_uq[l],
                          kv_norm_g[l], w_ukv[l])
        y_mix = jnp.concatenate([y_pool, y_mla], axis=-1) @ w_out[l]
        x = layer_norm(ALPHA * x + y_mix, ln_g[l, 1], ln_b[l, 1])
        y_mem = memory_cross_attention(x, mem, mem_wq[l], mem_wkv[l], mem_wo[l])
        x = layer_norm(ALPHA * x + y_mem, ln_g[l, 2], ln_b[l, 2])
        x = layer_norm(ALPHA * x + 0.5 * swiglu(x, ffn2_w13[l], ffn2_w2[l]), ln_g[l, 3], ln_b[l, 3])
    return x


import jax as _jax
import jax.numpy as _jnp

TWIN_FORMAT = 'train_step'
FWD_PARAMS = ['x', 'mem', 'positions', 'ln_g', 'ln_b', 'ffn1_w13', 'ffn1_w2', 'w_in', 'pool_w', 'pool_scale', 'q_norm_g', 'w_uq', 'kv_norm_g', 'w_ukv', 'w_out', 'mem_wq', 'mem_wkv', 'mem_wo', 'ffn2_w13', 'ffn2_w2']
TWIN_WEIGHTS = ['ln_g', 'ln_b', 'ffn1_w13', 'ffn1_w2', 'w_in', 'pool_w', 'pool_scale', 'q_norm_g', 'w_uq', 'kv_norm_g', 'w_ukv', 'w_out', 'mem_wq', 'mem_wkv', 'mem_wo', 'ffn2_w13', 'ffn2_w2']
TWIN_DIFF_INPUT = 'x'
TWIN_INPUTS = ['x', 'mem', 'positions', 'ln_g', 'ln_b', 'ffn1_w13', 'ffn1_w2', 'w_in', 'pool_w', 'pool_scale', 'q_norm_g', 'w_uq', 'kv_norm_g', 'w_ukv', 'w_out', 'mem_wq', 'mem_wkv', 'mem_wo', 'ffn2_w13', 'ffn2_w2', 'loss_target', 'm_ln_g', 'm_ln_b', 'm_ffn1_w13', 'm_ffn1_w2', 'm_w_in', 'm_pool_w', 'm_pool_scale', 'm_q_norm_g', 'm_w_uq', 'm_kv_norm_g', 'm_w_ukv', 'm_w_out', 'm_mem_wq', 'm_mem_wkv', 'm_mem_wo', 'm_ffn2_w13', 'm_ffn2_w2', 'v_ln_g', 'v_ln_b', 'v_ffn1_w13', 'v_ffn1_w2', 'v_w_in', 'v_pool_w', 'v_pool_scale', 'v_q_norm_g', 'v_w_uq', 'v_kv_norm_g', 'v_w_ukv', 'v_w_out', 'v_mem_wq', 'v_mem_wkv', 'v_mem_wo', 'v_ffn2_w13', 'v_ffn2_w2']
TWIN_OUTPUTS = ['loss', 'grad_x', 'grad_ln_g', 'grad_ln_b', 'grad_ffn1_w13', 'grad_ffn1_w2', 'grad_w_in', 'grad_pool_w', 'grad_pool_scale', 'grad_q_norm_g', 'grad_w_uq', 'grad_kv_norm_g', 'grad_w_ukv', 'grad_w_out', 'grad_mem_wq', 'grad_mem_wkv', 'grad_mem_wo', 'grad_ffn2_w13', 'grad_ffn2_w2', 'delta_ln_g', 'delta_ln_b', 'delta_ffn1_w13', 'delta_ffn1_w2', 'delta_w_in', 'delta_pool_w', 'delta_pool_scale', 'delta_q_norm_g', 'delta_w_uq', 'delta_kv_norm_g', 'delta_w_ukv', 'delta_w_out', 'delta_mem_wq', 'delta_mem_wkv', 'delta_mem_wo', 'delta_ffn2_w13', 'delta_ffn2_w2', 'new_m_ln_g', 'new_m_ln_b', 'new_m_ffn1_w13', 'new_m_ffn1_w2', 'new_m_w_in', 'new_m_pool_w', 'new_m_pool_scale', 'new_m_q_norm_g', 'new_m_w_uq', 'new_m_kv_norm_g', 'new_m_w_ukv', 'new_m_w_out', 'new_m_mem_wq', 'new_m_mem_wkv', 'new_m_mem_wo', 'new_m_ffn2_w13', 'new_m_ffn2_w2', 'new_v_ln_g', 'new_v_ln_b', 'new_v_ffn1_w13', 'new_v_ffn1_w2', 'new_v_w_in', 'new_v_pool_w', 'new_v_pool_scale', 'new_v_q_norm_g', 'new_v_w_uq', 'new_v_kv_norm_g', 'new_v_w_ukv', 'new_v_w_out', 'new_v_mem_wq', 'new_v_mem_wkv', 'new_v_mem_wo', 'new_v_ffn2_w13', 'new_v_ffn2_w2']
TWIN_LEAF_KINDS = {'loss': 'loss', 'grad_x': 'grad_x', 'grad_ln_g': 'grad_w', 'grad_ln_b': 'grad_w', 'grad_ffn1_w13': 'grad_w', 'grad_ffn1_w2': 'grad_w', 'grad_w_in': 'grad_w', 'grad_pool_w': 'grad_w', 'grad_pool_scale': 'grad_w', 'grad_q_norm_g': 'grad_w', 'grad_w_uq': 'grad_w', 'grad_kv_norm_g': 'grad_w', 'grad_w_ukv': 'grad_w', 'grad_w_out': 'grad_w', 'grad_mem_wq': 'grad_w', 'grad_mem_wkv': 'grad_w', 'grad_mem_wo': 'grad_w', 'grad_ffn2_w13': 'grad_w', 'grad_ffn2_w2': 'grad_w', 'delta_ln_g': 'delta_w', 'delta_ln_b': 'delta_w', 'delta_ffn1_w13': 'delta_w', 'delta_ffn1_w2': 'delta_w', 'delta_w_in': 'delta_w', 'delta_pool_w': 'delta_w', 'delta_pool_scale': 'delta_w', 'delta_q_norm_g': 'delta_w', 'delta_w_uq': 'delta_w', 'delta_kv_norm_g': 'delta_w', 'delta_w_ukv': 'delta_w', 'delta_w_out': 'delta_w', 'delta_mem_wq': 'delta_w', 'delta_mem_wkv': 'delta_w', 'delta_mem_wo': 'delta_w', 'delta_ffn2_w13': 'delta_w', 'delta_ffn2_w2': 'delta_w', 'new_m_ln_g': 'new_m', 'new_m_ln_b': 'new_m', 'new_m_ffn1_w13': 'new_m', 'new_m_ffn1_w2': 'new_m', 'new_m_w_in': 'new_m', 'new_m_pool_w': 'new_m', 'new_m_pool_scale': 'new_m', 'new_m_q_norm_g': 'new_m', 'new_m_w_uq': 'new_m', 'new_m_kv_norm_g': 'new_m', 'new_m_w_ukv': 'new_m', 'new_m_w_out': 'new_m', 'new_m_mem_wq': 'new_m', 'new_m_mem_wkv': 'new_m', 'new_m_mem_wo': 'new_m', 'new_m_ffn2_w13': 'new_m', 'new_m_ffn2_w2': 'new_m', 'new_v_ln_g': 'new_v', 'new_v_ln_b': 'new_v', 'new_v_ffn1_w13': 'new_v', 'new_v_ffn1_w2': 'new_v', 'new_v_w_in': 'new_v', 'new_v_pool_w': 'new_v', 'new_v_pool_scale': 'new_v', 'new_v_q_norm_g': 'new_v', 'new_v_w_uq': 'new_v', 'new_v_kv_norm_g': 'new_v', 'new_v_w_ukv': 'new_v', 'new_v_w_out': 'new_v', 'new_v_mem_wq': 'new_v', 'new_v_mem_wkv': 'new_v', 'new_v_mem_wo': 'new_v', 'new_v_ffn2_w13': 'new_v', 'new_v_ffn2_w2': 'new_v'}


def _forward(args):
    return _fwd_reference(*[args[k] for k in FWD_PARAMS])


def _output_shape():
    out = _jax.eval_shape(lambda: _forward(_fwd_setup_inputs(0)))
    return out.shape, out.dtype

N_MICROBATCH = 1
ADAM_LR = 0.001
ADAM_B1 = 0.9
ADAM_B2 = 0.999
ADAM_EPS = 1e-08
ADAM_WD = 0.01
ADAM_STEP = 10
PER_EXAMPLE_BATCH_AXIS = {'x': 0, 'mem': 0, 'positions': 0, 'loss_target': 0}
SHARED_INPUTS = []
_WEIGHT_DTYPES = {'ln_g': _jnp.float32, 'ln_b': _jnp.float32, 'ffn1_w13': _jnp.float32, 'ffn1_w2': _jnp.float32, 'w_in': _jnp.float32, 'pool_w': _jnp.float32, 'pool_scale': _jnp.float32, 'q_norm_g': _jnp.float32, 'w_uq': _jnp.float32, 'kv_norm_g': _jnp.float32, 'w_ukv': _jnp.float32, 'w_out': _jnp.float32, 'mem_wq': _jnp.float32, 'mem_wkv': _jnp.float32, 'mem_wo': _jnp.float32, 'ffn2_w13': _jnp.float32, 'ffn2_w2': _jnp.float32}
MOMENT_SCALE = {'ln_g': 1.199758e+01, 'ln_b': 7.049663e-01, 'ffn1_w13': 1.161869e-02, 'ffn1_w2': 3.796650e-02, 'w_in': 3.742522e-02, 'pool_w': 5.746766e-02, 'pool_scale': 5.550974e-02, 'q_norm_g': 1.717187e-02, 'w_uq': 8.183049e-03, 'kv_norm_g': 3.880083e-02, 'w_ukv': 9.932966e-03, 'w_out': 5.812993e-02, 'mem_wq': 6.690580e-03, 'mem_wkv': 7.318677e-03, 'mem_wo': 1.585082e-02, 'ffn2_w13': 1.154704e-02, 'ffn2_w2': 3.769139e-02}


def _to_microbatches(a, axis):
    t = _jnp.moveaxis(a, axis, 0)
    t = t.reshape((N_MICROBATCH, t.shape[0] // N_MICROBATCH) + t.shape[1:])
    return _jnp.moveaxis(t, 1, axis + 1)


def setup_inputs(seed: int = 0) -> dict:
    inp = _fwd_setup_inputs(seed)
    key = _jax.random.fold_in(_jax.random.key(seed), 7919)
    shape, _ = _output_shape()
    out = dict(inp)
    out["loss_target"] = _jax.random.normal(_jax.random.fold_in(key, 0), shape, _jnp.float32)
    for i, name in enumerate(TWIN_WEIGHTS):
        w = inp[name].astype(_jnp.float32)
        if MOMENT_SCALE is None:
            s = _jnp.sqrt(_jnp.mean(_jnp.square(w)) + 1e-30)
        else:
            s = MOMENT_SCALE[name]
        km, kv = _jax.random.split(_jax.random.fold_in(key, i + 1))
        out[name] = w
        out["m_" + name] = s * _jax.random.normal(km, w.shape, _jnp.float32)
        out["v_" + name] = (s * s) * _jax.random.uniform(kv, w.shape, _jnp.float32, 0.5, 1.5)
    if N_MICROBATCH > 1:
        for name, axis in PER_EXAMPLE_BATCH_AXIS.items():
            out[name] = _to_microbatches(out[name], axis)
    return {'x': out['x'], 'mem': out['mem'], 'positions': out['positions'], 'ln_g': out['ln_g'], 'ln_b': out['ln_b'], 'ffn1_w13': out['ffn1_w13'], 'ffn1_w2': out['ffn1_w2'], 'w_in': out['w_in'], 'pool_w': out['pool_w'], 'pool_scale': out['pool_scale'], 'q_norm_g': out['q_norm_g'], 'w_uq': out['w_uq'], 'kv_norm_g': out['kv_norm_g'], 'w_ukv': out['w_ukv'], 'w_out': out['w_out'], 'mem_wq': out['mem_wq'], 'mem_wkv': out['mem_wkv'], 'mem_wo': out['mem_wo'], 'ffn2_w13': out['ffn2_w13'], 'ffn2_w2': out['ffn2_w2'], 'loss_target': out['loss_target'], 'm_ln_g': out['m_ln_g'], 'm_ln_b': out['m_ln_b'], 'm_ffn1_w13': out['m_ffn1_w13'], 'm_ffn1_w2': out['m_ffn1_w2'], 'm_w_in': out['m_w_in'], 'm_pool_w': out['m_pool_w'], 'm_pool_scale': out['m_pool_scale'], 'm_q_norm_g': out['m_q_norm_g'], 'm_w_uq': out['m_w_uq'], 'm_kv_norm_g': out['m_kv_norm_g'], 'm_w_ukv': out['m_w_ukv'], 'm_w_out': out['m_w_out'], 'm_mem_wq': out['m_mem_wq'], 'm_mem_wkv': out['m_mem_wkv'], 'm_mem_wo': out['m_mem_wo'], 'm_ffn2_w13': out['m_ffn2_w13'], 'm_ffn2_w2': out['m_ffn2_w2'], 'v_ln_g': out['v_ln_g'], 'v_ln_b': out['v_ln_b'], 'v_ffn1_w13': out['v_ffn1_w13'], 'v_ffn1_w2': out['v_ffn1_w2'], 'v_w_in': out['v_w_in'], 'v_pool_w': out['v_pool_w'], 'v_pool_scale': out['v_pool_scale'], 'v_q_norm_g': out['v_q_norm_g'], 'v_w_uq': out['v_w_uq'], 'v_kv_norm_g': out['v_kv_norm_g'], 'v_w_ukv': out['v_w_ukv'], 'v_w_out': out['v_w_out'], 'v_mem_wq': out['v_mem_wq'], 'v_mem_wkv': out['v_mem_wkv'], 'v_mem_wo': out['v_mem_wo'], 'v_ffn2_w13': out['v_ffn2_w13'], 'v_ffn2_w2': out['v_ffn2_w2']}


def _loss(weights, diff, rest, loss_target):
    with _jax.named_scope("forward"):
        args = {**rest, TWIN_DIFF_INPUT: diff, **{k: w.astype(_WEIGHT_DTYPES[k]) for k, w in weights.items()}}
        y = _forward(args)
    with _jax.named_scope("loss_head"):
        err = _jnp.square(y.astype(_jnp.float32) - loss_target)
        return 0.5 * _jnp.sum(_jnp.mean(err, axis=-1)) if err.ndim else 0.5 * err


def _adamw(w, g, m, v):
    m = ADAM_B1 * m + (1.0 - ADAM_B1) * g
    v = ADAM_B2 * v + (1.0 - ADAM_B2) * _jnp.square(g)
    m_hat = m / (1.0 - ADAM_B1 ** ADAM_STEP)
    v_hat = v / (1.0 - ADAM_B2 ** ADAM_STEP)
    delta = -ADAM_LR * (m_hat / (_jnp.sqrt(v_hat) + ADAM_EPS) + ADAM_WD * w)
    return delta, m, v


def reference(x, mem, positions, ln_g, ln_b, ffn1_w13, ffn1_w2, w_in, pool_w, pool_scale, q_norm_g, w_uq, kv_norm_g, w_ukv, w_out, mem_wq, mem_wkv, mem_wo, ffn2_w13, ffn2_w2, loss_target, m_ln_g, m_ln_b, m_ffn1_w13, m_ffn1_w2, m_w_in, m_pool_w, m_pool_scale, m_q_norm_g, m_w_uq, m_kv_norm_g, m_w_ukv, m_w_out, m_mem_wq, m_mem_wkv, m_mem_wo, m_ffn2_w13, m_ffn2_w2, v_ln_g, v_ln_b, v_ffn1_w13, v_ffn1_w2, v_w_in, v_pool_w, v_pool_scale, v_q_norm_g, v_w_uq, v_kv_norm_g, v_w_ukv, v_w_out, v_mem_wq, v_mem_wkv, v_mem_wo, v_ffn2_w13, v_ffn2_w2):
    given = dict(x=x, mem=mem, positions=positions, ln_g=ln_g, ln_b=ln_b, ffn1_w13=ffn1_w13, ffn1_w2=ffn1_w2, w_in=w_in, pool_w=pool_w, pool_scale=pool_scale, q_norm_g=q_norm_g, w_uq=w_uq, kv_norm_g=kv_norm_g, w_ukv=w_ukv, w_out=w_out, mem_wq=mem_wq, mem_wkv=mem_wkv, mem_wo=mem_wo, ffn2_w13=ffn2_w13, ffn2_w2=ffn2_w2, loss_target=loss_target, m_ln_g=m_ln_g, m_ln_b=m_ln_b, m_ffn1_w13=m_ffn1_w13, m_ffn1_w2=m_ffn1_w2, m_w_in=m_w_in, m_pool_w=m_pool_w, m_pool_scale=m_pool_scale, m_q_norm_g=m_q_norm_g, m_w_uq=m_w_uq, m_kv_norm_g=m_kv_norm_g, m_w_ukv=m_w_ukv, m_w_out=m_w_out, m_mem_wq=m_mem_wq, m_mem_wkv=m_mem_wkv, m_mem_wo=m_mem_wo, m_ffn2_w13=m_ffn2_w13, m_ffn2_w2=m_ffn2_w2, v_ln_g=v_ln_g, v_ln_b=v_ln_b, v_ffn1_w13=v_ffn1_w13, v_ffn1_w2=v_ffn1_w2, v_w_in=v_w_in, v_pool_w=v_pool_w, v_pool_scale=v_pool_scale, v_q_norm_g=v_q_norm_g, v_w_uq=v_w_uq, v_kv_norm_g=v_kv_norm_g, v_w_ukv=v_w_ukv, v_w_out=v_w_out, v_mem_wq=v_mem_wq, v_mem_wkv=v_mem_wkv, v_mem_wo=v_mem_wo, v_ffn2_w13=v_ffn2_w13, v_ffn2_w2=v_ffn2_w2)
    weights = {n: given[n] for n in TWIN_WEIGHTS}
    shared = {n: given[n] for n in SHARED_INPUTS}
    per_example = {n: given[n] for n in ['x', 'mem', 'positions']}
    grad_fn = _jax.value_and_grad(_loss, argnums=(0, 1))

    def one_microbatch(ex, loss_target):
        ex = dict(ex)
        diff = ex.pop(TWIN_DIFF_INPUT)
        return grad_fn(weights, diff, {**shared, **ex}, loss_target)

    if N_MICROBATCH == 1:
        loss, (grad_w, grad_x) = one_microbatch(per_example, given["loss_target"])
    else:
        def body(carry, xs):
            loss_sum, grad_sum = carry
            l_k, (gw_k, gx_k) = one_microbatch(xs[0], xs[1])
            with _jax.named_scope("update"):
                return (loss_sum + l_k, _jax.tree.map(_jnp.add, grad_sum, gw_k)), gx_k

        init = (_jnp.zeros((), _jnp.float32), _jax.tree.map(_jnp.zeros_like, weights))
        (loss, grad_w), grad_x = _jax.lax.scan(body, init, (per_example, given["loss_target"]))
    with _jax.named_scope("update"):
        delta_w, new_m, new_v = {}, {}, {}
        for n in TWIN_WEIGHTS:
            delta_w[n], new_m[n], new_v[n] = _adamw(weights[n], grad_w[n], given["m_" + n], given["v_" + n])
    return (loss, grad_x, *[grad_w[n] for n in TWIN_WEIGHTS], *[delta_w[n] for n in TWIN_WEIGHTS],
            *[new_m[n] for n in TWIN_WEIGHTS], *[new_v[n] for n in TWIN_WEIGHTS])
```

```python
import functools
import math

import jax
import jax.numpy as jnp
from jax import lax
from jax.experimental import pallas as pl
from jax.experimental.pallas import tpu as pltpu

F32 = jnp.float32
BF16 = jnp.bfloat16
MESH = pl.DeviceIdType.MESH

CHUNK = 64
MEM_HEADS = 4
POOL_WINDOWS = (2, 4, 8, 16)
QK_NOPE = 128
QK_ROPE = 64
V_HEAD = 128
Q_LORA = 256
KV_LORA = 128
ROPE_BASE = 10000.0
LN_EPS = 1e-5
RMS_EPS = 1e-6
NEG_INF = -1e30
ADAM_LR = 0.001
ADAM_B1 = 0.9
ADAM_B2 = 0.999
ADAM_EPS = 1e-08
ADAM_WD = 0.01
ADAM_STEP = 10

N_DEV = 8
LANE = 128
HEAD_PAD = 2 * LANE
POOL_HALO = 16
VMEM_CAP = 56 * 1024 * 1024
VMEM_FLOOR = 32 * 1024 * 1024


def _tile(n, pref, mult):
    t = (min(pref, n) // mult) * mult
    while t >= mult:
        if n % t == 0:
            return t
        t -= mult
    return n


def _params(sem, est_bytes):
    limit = int(min(max(2 * est_bytes + (8 << 20), VMEM_FLOOR), VMEM_CAP))
    return pltpu.CompilerParams(dimension_semantics=sem, vmem_limit_bytes=limit)


def _nbytes(shape, dtype):
    return math.prod(shape) * jnp.dtype(dtype).itemsize


def _dg(a, b, ca, cb):
    return lax.dot_general(a.astype(BF16), b.astype(BF16), (((ca,), (cb,)), ((), ())),
                           preferred_element_type=F32)


@jax.custom_vjp
def _bdot_nn(a, b):
    return _dg(a, b, 1, 0)


def _bdot_nn_fwd(a, b):
    return _dg(a, b, 1, 0), (a, b)


def _bdot_nn_bwd(res, ct):
    a, b = res
    return _dg(ct, b, 1, 1).astype(a.dtype), _dg(a, ct, 0, 0).astype(b.dtype)


_bdot_nn.defvjp(_bdot_nn_fwd, _bdot_nn_bwd)


@jax.custom_vjp
def _bdot_nt(a, b):
    return _dg(a, b, 1, 1)


def _bdot_nt_fwd(a, b):
    return _dg(a, b, 1, 1), (a, b)


def _bdot_nt_bwd(res, ct):
    a, b = res
    return _dg(ct, b, 1, 0).astype(a.dtype), _dg(ct, a, 0, 0).astype(b.dtype)


_bdot_nt.defvjp(_bdot_nt_fwd, _bdot_nt_bwd)


@functools.partial(jax.custom_vjp, nondiff_argnums=(1,))
def _lane_roll(x, shift):
    return pltpu.roll(x, shift % x.shape[1], axis=1)


def _lane_roll_fwd(x, shift):
    return _lane_roll(x, shift), None


def _lane_roll_bwd(shift, _, ct):
    return (_lane_roll(ct, -shift),)


_lane_roll.defvjp(_lane_roll_fwd, _lane_roll_bwd)


@functools.partial(jax.custom_vjp, nondiff_argnums=(1, 2))
def _cols(x, lo, hi):
    return x[:, lo:hi]


def _cols_fwd(x, lo, hi):
    return x[:, lo:hi], x.shape[1]


def _cols_bwd(lo, hi, width, ct):
    parts = []
    if lo > 0:
        parts.append(jnp.zeros((ct.shape[0], lo), ct.dtype))
    parts.append(ct)
    if hi < width:
        parts.append(jnp.zeros((ct.shape[0], width - hi), ct.dtype))
    return (jnp.concatenate(parts, axis=1) if len(parts) > 1 else ct,)


_cols.defvjp(_cols_fwd, _cols_bwd)


def _mm(a, b, *, name, ta=False, tb=False, out_dtype=F32, lead=None, add=None, add_scale=1.0,
        tm=512, tn=512, tk=512):
    if ta:
        K, M = a.shape
    else:
        M, K = a.shape
    bshape = b.shape[1:] if lead is not None else b.shape
    if tb:
        N, Kb = bshape
    else:
        Kb, N = bshape
    assert K == Kb, (name, a.shape, b.shape)
    tm = _tile(M, tm, LANE if ta else 16)
    tn = _tile(N, tn, LANE)
    tk = _tile(K, tk, LANE)
    nk = K // tk
    ca = 0 if ta else 1
    cb = 1 if tb else 0

    def body(*refs):
        if add is None:
            a_ref, b_ref, o_ref, acc_ref = refs
            add_ref = None
        else:
            a_ref, b_ref, add_ref, o_ref, acc_ref = refs
        k = pl.program_id(2)

        @pl.when(k == 0)
        def _():
            acc_ref[...] = jnp.zeros_like(acc_ref)

        acc_ref[...] += _dg(a_ref[...], b_ref[...], ca, cb)

        @pl.when(k == nk - 1)
        def _():
            r = acc_ref[...]
            if add_ref is not None:
                r = r + add_scale * add_ref[...].astype(F32)
            o_ref[...] = r.astype(o_ref.dtype)

    a_blk = (tk, tm) if ta else (tm, tk)
    a_map = (lambda i, j, k: (k, i)) if ta else (lambda i, j, k: (i, k))
    b_blk = (tn, tk) if tb else (tk, tn)
    if lead is None:
        b_map = (lambda i, j, k: (j, k)) if tb else (lambda i, j, k: (k, j))
        b_spec = pl.BlockSpec(b_blk, b_map)
    else:
        b_map = (lambda i, j, k: (lead, j, k)) if tb else (lambda i, j, k: (lead, k, j))
        b_spec = pl.BlockSpec((None,) + b_blk, b_map)
    in_specs = [pl.BlockSpec(a_blk, a_map), b_spec]
    args = [a, b]
    est = _nbytes(a_blk, a.dtype) + _nbytes(b_blk, b.dtype) + _nbytes((tm, tn), out_dtype) + _nbytes((tm, tn), F32)
    if add is not None:
        in_specs.append(pl.BlockSpec((tm, tn), lambda i, j, k: (i, j)))
        args.append(add)
        est += _nbytes((tm, tn), add.dtype)
    return pl.pallas_call(
        body, name=name,
        grid=(M // tm, N // tn, nk),
        in_specs=in_specs,
        out_specs=pl.BlockSpec((tm, tn), lambda i, j, k: (i, j)),
        out_shape=jax.ShapeDtypeStruct((M, N), out_dtype),
        scratch_shapes=[pltpu.VMEM((tm, tn), F32)],
        compiler_params=_params(("parallel", "parallel", "arbitrary"), est),
    )(*args)


def _rowwise(fn, tiles, params, tile_outs, acc_outs=(), *, tm, name):
    tile_arrays, tile_specs = [], []
    for t in tiles:
        if isinstance(t, tuple):
            tile_arrays.append(t[0])
            tile_specs.append(t[1])
        else:
            tile_arrays.append(t)
            tile_specs.append(pl.BlockSpec((tm, t.shape[1]), lambda i: (i, 0)))
    T = tile_arrays[0].shape[0]
    nt, np_, nto, nao = len(tile_arrays), len(params), len(tile_outs), len(acc_outs)

    def body(*refs):
        i = pl.program_id(0)
        tvals = [r[...] for r in refs[:nt]]
        pvals = [r[...] for r in refs[nt:nt + np_]]
        to_refs = refs[nt + np_:nt + np_ + nto]
        ao_refs = refs[nt + np_ + nto:]
        touts, aouts = fn(i, tvals, pvals)
        for r, v in zip(to_refs, touts):
            r[...] = v.astype(r.dtype)
        if nao:
            @pl.when(i == 0)
            def _():
                for r in ao_refs:
                    r[...] = jnp.zeros_like(r)
            for r, v in zip(ao_refs, aouts):
                r[...] += v.astype(r.dtype)

    in_specs = tile_specs + [pl.BlockSpec(p.shape, lambda i: (0, 0)) for p in params]
    out_specs = [pl.BlockSpec((tm, c), lambda i: (i, 0)) for c, _ in tile_outs]
    out_specs += [pl.BlockSpec(s, lambda i: (0, 0)) for s, _ in acc_outs]
    out_shape = [jax.ShapeDtypeStruct((T, c), d) for c, d in tile_outs]
    out_shape += [jax.ShapeDtypeStruct(s, d) for s, d in acc_outs]
    width = sum(s.block_shape[-1] for s in tile_specs) + sum(c for c, _ in tile_outs)
    est = 6 * tm * width * 4 + sum(_nbytes(p.shape, F32) for p in params) * 4
    return pl.pallas_call(
        body, name=name, grid=(T // tm,),
        in_specs=in_specs, out_specs=out_specs, out_shape=out_shape,
        compiler_params=_params(("arbitrary",) if nao else ("parallel",), est),
    )(*tile_arrays, *params)


def _ln_fn(alpha, s, xres, y, g, b):
    z = alpha * xres.astype(F32) + s * y.astype(F32)
    mu = jnp.mean(z, axis=-1, keepdims=True)
    zc = z - mu
    var = jnp.mean(zc * zc, axis=-1, keepdims=True)
    return zc * lax.rsqrt(var + LN_EPS) * g + b


def _ln_fwd(xres, y, g, b, *, alpha, s, name):
    T, D = xres.shape
    tm = _tile(T, 256, 16)

    def fn(i, tv, pv):
        out = _ln_fn(alpha, s, tv[0], tv[1], pv[0], pv[1])
        return (out, out), ()

    return _rowwise(fn, [xres, y], [g, b], [(D, F32), (D, BF16)], tm=tm, name=name)


def _ln_bwd(xres, y, g, b, dout, *, alpha, s, name):
    T, D = xres.shape
    tm = _tile(T, 256, 16)

    def fn(i, tv, pv):
        _, vjp = jax.vjp(functools.partial(_ln_fn, alpha, s), tv[0], tv[1], pv[0], pv[1])
        dx, dy, dg, db = vjp(tv[2].astype(F32))
        return (dx, dy), (dg, db)

    return _rowwise(fn, [xres, y, dout], [g, b], [(D, F32), (D, BF16)],
                    [((1, D), F32), ((1, D), F32)], tm=tm, name=name)


def _swiglu_fwd(h13, *, name):
    T, F2 = h13.shape
    F = F2 // 2
    tm = _tile(T, 512, 16)
    tc = _tile(F, 512, LANE)
    nf = F // tc

    def body(g_ref, u_ref, a_ref):
        g = g_ref[...].astype(F32)
        u = u_ref[...].astype(F32)
        a_ref[...] = (g * jax.nn.sigmoid(g) * u).astype(a_ref.dtype)

    return pl.pallas_call(
        body, name=name, grid=(T // tm, nf),
        in_specs=[pl.BlockSpec((tm, tc), lambda i, j: (i, j)),
                  pl.BlockSpec((tm, tc), lambda i, j: (i, j + nf))],
        out_specs=pl.BlockSpec((tm, tc), lambda i, j: (i, j)),
        out_shape=jax.ShapeDtypeStruct((T, F), BF16),
        compiler_params=_params(("parallel", "parallel"), 8 * tm * tc * 4),
    )(h13, h13)


def _swiglu_bwd(h13, da, *, name):
    T, F2 = h13.shape
    F = F2 // 2
    tm = _tile(T, 512, 16)
    tc = _tile(F, 512, LANE)
    nf = F // tc

    def body(g_ref, u_ref, da_ref, dh_ref):
        j = pl.program_id(1)
        g = g_ref[...].astype(F32)
        u = u_ref[...].astype(F32)
        d = da_ref[...].astype(F32)
        sig = jax.nn.sigmoid(g)
        dgate = d * u * sig * (1.0 + g * (1.0 - sig))
        dup = d * g * sig
        dh_ref[...] = jnp.where(j < nf, dgate, dup).astype(dh_ref.dtype)

    return pl.pallas_call(
        body, name=name, grid=(T // tm, 2 * nf),
        in_specs=[pl.BlockSpec((tm, tc), lambda i, j: (i, j % nf)),
                  pl.BlockSpec((tm, tc), lambda i, j: (i, j % nf + nf)),
                  pl.BlockSpec((tm, tc), lambda i, j: (i, j % nf))],
        out_specs=pl.BlockSpec((tm, tc), lambda i, j: (i, j)),
        out_shape=jax.ShapeDtypeStruct((T, F2), BF16),
        compiler_params=_params(("parallel", "parallel"), 12 * tm * tc * 4),
    )(h13, h13, da)


def _pool_select(parts, pw):
    pg = pw // len(POOL_WINDOWS)
    grp = lax.broadcasted_iota(jnp.int32, parts[0].shape, 1) // pg
    out = parts[3]
    for g in (2, 1, 0):
        out = jnp.where(grp == g, parts[g], out)
    return out


def _pool_count(t0, rows, pw):
    pg = pw // len(POOL_WINDOWS)
    grp = lax.broadcasted_iota(jnp.int32, (rows, pw), 1) // pg
    win = jnp.where(grp == 0, POOL_WINDOWS[0],
                    jnp.where(grp == 1, POOL_WINDOWS[1],
                              jnp.where(grp == 2, POOL_WINDOWS[2], POOL_WINDOWS[3])))
    t = t0 + lax.broadcasted_iota(jnp.int32, (rows, pw), 0)
    return jnp.minimum(t + 1, win).astype(F32)


def _window_sums(ext, up):
    n = ext.shape[0]
    sums, cur, k = [], ext, 1
    for _ in POOL_WINDOWS:
        cur = cur + pltpu.roll(cur, (n - k) if up else k, axis=0)
        sums.append(cur)
        k *= 2
    return sums


def _pool_delta(u, halo, t0):
    tm, pw = u.shape
    ext = jnp.concatenate([halo, u], axis=0)
    sums = [s[POOL_HALO:, :] for s in _window_sums(ext, up=False)]
    return _pool_select(sums, pw) / _pool_count(t0, tm, pw) - u


def _pool_fwd(hin, wbd, scale, *, name):
    T = hin.shape[0]
    pw = wbd.shape[0]
    tm = _tile(T, 256, POOL_HALO)
    per = tm // POOL_HALO

    def body(u_ref, halo_ref, w_ref, s_ref, y_ref):
        i = pl.program_id(0)
        halo = jnp.where(i > 0, halo_ref[...], 0.0)
        d = _pool_delta(u_ref[...], halo, i * tm)
        y_ref[...] = (_dg(d, w_ref[...], 1, 0) * s_ref[...]).astype(y_ref.dtype)

    return pl.pallas_call(
        body, name=name, grid=(T // tm,),
        in_specs=[pl.BlockSpec((tm, pw), lambda i: (i, 0)),
                  pl.BlockSpec((POOL_HALO, pw), lambda i: (jnp.maximum(i * per - 1, 0), 0)),
                  pl.BlockSpec((pw, pw), lambda i: (0, 0)),
                  pl.BlockSpec((1, pw), lambda i: (0, 0))],
        out_specs=pl.BlockSpec((tm, pw), lambda i: (i, 0)),
        out_shape=jax.ShapeDtypeStruct((T, pw), BF16),
        compiler_params=_params(("parallel",), 16 * tm * pw * 4),
    )(hin, hin, wbd, scale)


def _pool_bwd(hin, dcat, wbd, scale, *, name):
    T = hin.shape[0]
    pw = wbd.shape[0]
    tm = _tile(T, 256, POOL_HALO)
    per = tm // POOL_HALO
    nt = T // tm

    def body(u_ref, halo_ref, dy_ref, dyn_ref, w_ref, s_ref, du_ref, dw_ref, ds_ref):
        i = pl.program_id(0)

        @pl.when(i == 0)
        def _():
            dw_ref[...] = jnp.zeros_like(dw_ref)
            ds_ref[...] = jnp.zeros_like(ds_ref)

        halo = jnp.where(i > 0, halo_ref[...], 0.0)
        d = _pool_delta(u_ref[...], halo, i * tm)
        w = w_ref[...]
        sc = s_ref[...]
        dy = dy_ref[...]
        dyn = jnp.where(i < nt - 1, dyn_ref[...], 0.0)
        ds_ref[...] += jnp.sum(dy * _dg(d, w, 1, 0), axis=0, keepdims=True)
        dys = dy * sc
        dw_ref[...] += _dg(d, dys, 0, 0)
        dys_ext = jnp.concatenate([dys, dyn * sc], axis=0)
        dd_ext = _dg(dys_ext, w, 1, 1)
        ddp = dd_ext / _pool_count(i * tm, tm + POOL_HALO, pw)
        sums = [s[:tm, :] for s in _window_sums(ddp, up=True)]
        du_ref[...] = _pool_select(sums, pw) - dd_ext[:tm, :]

    return pl.pallas_call(
        body, name=name, grid=(nt,),
        in_specs=[pl.BlockSpec((tm, pw), lambda i: (i, 0)),
                  pl.BlockSpec((POOL_HALO, pw), lambda i: (jnp.maximum(i * per - 1, 0), 0)),
                  pl.BlockSpec((tm, pw), lambda i: (i, 0)),
                  pl.BlockSpec((POOL_HALO, pw), lambda i: (jnp.minimum((i + 1) * per, nt * per - 1), 0)),
                  pl.BlockSpec((pw, pw), lambda i: (0, 0)),
                  pl.BlockSpec((1, pw), lambda i: (0, 0))],
        out_specs=[pl.BlockSpec((tm, pw), lambda i: (i, 0)),
                   pl.BlockSpec((pw, pw), lambda i: (0, 0)),
                   pl.BlockSpec((1, pw), lambda i: (0, 0))],
        out_shape=[jax.ShapeDtypeStruct((T, pw), F32),
                   jax.ShapeDtypeStruct((pw, pw), F32),
                   jax.ShapeDtypeStruct((1, pw), F32)],
        compiler_params=_params(("arbitrary",), 24 * tm * pw * 4),
    )(hin, hin, dcat, dcat, wbd, scale)


def _rms(x, g):
    return x * lax.rsqrt(jnp.mean(x * x, axis=-1, keepdims=True) + RMS_EPS) * g


def _norms_fn(pw, h, gq, gkv):
    o1 = pw + Q_LORA
    o2 = o1 + KV_LORA
    return (_rms(_cols(h, pw, o1), gq), _rms(_cols(h, o1, o2), gkv), _cols(h, o2, h.shape[1]))


def _norms_fwd(hin, gq, gkv, *, pw, name):
    tm = _tile(hin.shape[0], 256, 16)

    def fn(i, tv, pv):
        return _norms_fn(pw, tv[0], pv[0], pv[1]), ()

    return _rowwise(fn, [hin], [gq, gkv], [(Q_LORA, BF16), (KV_LORA, BF16), (LANE, F32)], tm=tm, name=name)


def _norms_bwd(hin, gq, gkv, dcq, dckv, dkpe, du, *, pw, name):
    tm = _tile(hin.shape[0], 256, 16)
    dinp = hin.shape[1]

    def fn(i, tv, pv):
        _, vjp = jax.vjp(functools.partial(_norms_fn, pw), tv[0], pv[0], pv[1])
        dh, dgq, dgkv = vjp((tv[1].astype(F32), tv[2].astype(F32), tv[3].astype(F32)))
        dh = jnp.concatenate([tv[4], dh[:, pw:]], axis=1)
        return (dh,), (dgq, dgkv)

    return _rowwise(fn, [hin, dcq, dckv, dkpe, du], [gq, gkv], [(dinp, BF16)],
                    [((1, Q_LORA), F32), ((1, KV_LORA), F32)], tm=tm, name=name)


def _heads_fn(H, qraw, kv, kpe, rc, rs1, rs2):
    half = QK_ROPE // 2

    def rope(blk):
        return blk * rc + _lane_roll(blk, -half) * rs1 + _lane_roll(blk, half) * rs2

    krot = rope(kpe)
    qs, ks, vs = [], [], []
    for h in range(H):
        lo = h * HEAD_PAD
        qs += [_cols(qraw, lo, lo + LANE), rope(_cols(qraw, lo + LANE, lo + HEAD_PAD))]
        ks += [_cols(kv, lo, lo + LANE), krot]
        vs += [_cols(kv, lo + LANE, lo + HEAD_PAD)]
    return jnp.concatenate(qs, axis=1), jnp.concatenate(ks, axis=1), jnp.concatenate(vs, axis=1)


def _heads_fwd(qraw, kv, kpe, tabs, *, H, name):
    tm = _tile(qraw.shape[0], 256, 16)

    def fn(i, tv, pv):
        return _heads_fn(H, *tv), ()

    return _rowwise(fn, [qraw, kv, kpe, *tabs], [],
                    [(H * HEAD_PAD, BF16), (H * HEAD_PAD, BF16), (H * V_HEAD, BF16)], tm=tm, name=name)


def _heads_bwd(dq, dk, dv, tabs, *, H, name):
    tm = _tile(dq.shape[0], 256, 16)

    def fn(i, tv, pv):
        z = jnp.zeros((tm, H * HEAD_PAD), F32)
        zk = jnp.zeros((tm, LANE), F32)
        rc, rs1, rs2 = tv[3], tv[4], tv[5]
        _, vjp = jax.vjp(lambda a, b, c: _heads_fn(H, a, b, c, rc, rs1, rs2), z, z, zk)
        return vjp((tv[0], tv[1], tv[2])), ()

    return _rowwise(fn, [dq, dk, dv, *tabs], [],
                    [(H * HEAD_PAD, BF16), (H * HEAD_PAD, BF16), (LANE, F32)], tm=tm, name=name)


def _diag_mask(t):
    r = lax.broadcasted_iota(jnp.int32, (t, t), 0) // CHUNK
    c = lax.broadcasted_iota(jnp.int32, (t, t), 1) // CHUNK
    return r >= c


def _flash_fwd(qh, kh, vh, *, H, name):
    T = qh.shape[0]
    t = _tile(T, 256, CHUNK)
    scale = (QK_NOPE + QK_ROPE) ** -0.5

    def body(q_ref, k_ref, v_ref, o_ref, lse_ref):
        i = pl.program_id(1)
        q = q_ref[...]

        def blk(j, carry, masked):
            m, l, acc = carry
            rows = pl.ds(pl.multiple_of(j * t, t), t)
            s = _dg(q, k_ref[rows, :], 1, 1) * scale
            if masked:
                s = jnp.where(_diag_mask(t), s, NEG_INF)
            mn = jnp.maximum(m, jnp.max(s, axis=1, keepdims=True))
            p = jnp.exp(s - mn)
            corr = jnp.exp(m - mn)
            l = corr * l + jnp.sum(p, axis=1, keepdims=True)
            acc = corr * acc + _dg(p, v_ref[rows, :], 1, 0)
            return mn, l, acc

        init = (jnp.full((t, 1), NEG_INF, F32), jnp.zeros((t, 1), F32), jnp.zeros((t, V_HEAD), F32))
        carry = lax.fori_loop(0, i, lambda j, c: blk(j, c, False), init)
        m, l, acc = blk(i, carry, True)
        o_ref[...] = (acc / l).astype(o_ref.dtype)
        lse_ref[...] = jnp.broadcast_to(m + jnp.log(l), (t, V_HEAD))

    est = 2 * T * (HEAD_PAD + V_HEAD) * 2 + 8 * t * t * 4
    return pl.pallas_call(
        body, name=name, grid=(H, T // t),
        in_specs=[pl.BlockSpec((t, HEAD_PAD), lambda h, i: (i, h)),
                  pl.BlockSpec((T, HEAD_PAD), lambda h, i: (0, h)),
                  pl.BlockSpec((T, V_HEAD), lambda h, i: (0, h))],
        out_specs=[pl.BlockSpec((t, V_HEAD), lambda h, i: (i, h)),
                   pl.BlockSpec((t, V_HEAD), lambda h, i: (i, h))],
        out_shape=[jax.ShapeDtypeStruct((T, H * V_HEAD), BF16),
                   jax.ShapeDtypeStruct((T, H * V_HEAD), F32)],
        compiler_params=_params(("parallel", "parallel"), est),
    )(qh, kh, vh)


def _flash_bwd(qh, kh, vh, cat, dcat, lse, *, H, pw, name):
    T = qh.shape[0]
    t = _tile(T, 256, CHUNK)
    nb = T // t
    off = pw // V_HEAD
    scale = (QK_NOPE + QK_ROPE) ** -0.5

    def body(q_ref, k_ref, v_ref, o_ref, do_ref, lse_ref, dq_ref, dk_ref, dv_ref):
        j = pl.program_id(1)

        @pl.when(j == 0)
        def _():
            dq_ref[...] = jnp.zeros_like(dq_ref)

        kj = k_ref[...]
        vj = v_ref[...]

        def blk(i, carry, masked):
            dk, dv = carry
            rows = pl.ds(pl.multiple_of(i * t, t), t)
            qi = q_ref[rows, :]
            doi = do_ref[rows, :]
            oi = o_ref[rows, :].astype(F32)
            lsei = lse_ref[rows, :][:, :1]
            s = _dg(qi, kj, 1, 1) * scale
            if masked:
                s = jnp.where(_diag_mask(t), s, NEG_INF)
            p = jnp.exp(s - lsei)
            dv = dv + _dg(p, doi, 0, 0)
            dp = _dg(doi, vj, 1, 1)
            di = jnp.sum(doi * oi, axis=1, keepdims=True)
            ds = p * (dp - di) * scale
            dk = dk + _dg(ds, qi, 0, 0)
            dq_ref[rows, :] += _dg(ds, kj, 1, 0)
            return dk, dv

        carry = blk(j, (jnp.zeros((t, HEAD_PAD), F32), jnp.zeros((t, V_HEAD), F32)), True)
        dk, dv = lax.fori_loop(j + 1, nb, lambda i, c: blk(i, c, False), carry)
        dk_ref[...] = dk
        dv_ref[...] = dv

    est = T * (HEAD_PAD * 2 + V_HEAD * 2 + V_HEAD * 4 + V_HEAD * 4 + HEAD_PAD * 4) + 10 * t * t * 4
    return pl.pallas_call(
        body, name=name, grid=(H, nb),
        in_specs=[pl.BlockSpec((T, HEAD_PAD), lambda h, j: (0, h)),
                  pl.BlockSpec((t, HEAD_PAD), lambda h, j: (j, h)),
                  pl.BlockSpec((t, V_HEAD), lambda h, j: (j, h)),
                  pl.BlockSpec((T, V_HEAD), lambda h, j: (0, off + h)),
                  pl.BlockSpec((T, V_HEAD), lambda h, j: (0, off + h)),
                  pl.BlockSpec((T, V_HEAD), lambda h, j: (0, h))],
        out_specs=[pl.BlockSpec((T, HEAD_PAD), lambda h, j: (0, h)),
                   pl.BlockSpec((t, HEAD_PAD), lambda h, j: (j, h)),
                   pl.BlockSpec((t, V_HEAD), lambda h, j: (j, h))],
        out_shape=[jax.ShapeDtypeStruct((T, H * HEAD_PAD), F32),
                   jax.ShapeDtypeStruct((T, H * HEAD_PAD), F32),
                   jax.ShapeDtypeStruct((T, H * V_HEAD), F32)],
        compiler_params=_params(("arbitrary", "arbitrary"), est),
    )(qh, kh, vh, cat, dcat, lse)


def _mem_fn(q, k, v):
    hd = q.shape[1] // MEM_HEADS
    outs = []
    for h in range(MEM_HEADS):
        lo, hi = h * hd, (h + 1) * hd
        s = _bdot_nt(_cols(q, lo, hi), _cols(k, lo, hi)) * hd ** -0.5
        e = jnp.exp(s - lax.stop_gradient(jnp.max(s, axis=1, keepdims=True)))
        p = e / jnp.sum(e, axis=1, keepdims=True)
        outs.append(_bdot_nn(p, _cols(v, lo, hi)))
    return jnp.concatenate(outs, axis=1)


def _mem_fwd(q, k, v, *, name):
    T, D = q.shape
    tm = _tile(T, 256, 16)

    def fn(i, tv, pv):
        return (_mem_fn(tv[0], pv[0], pv[1]),), ()

    return _rowwise(fn, [q], [k, v], [(D, BF16)], tm=tm, name=name)[0]


def _mem_bwd(q, k, v, do, *, name):
    T, D = q.shape
    tm = _tile(T, 256, 16)

    def fn(i, tv, pv):
        _, vjp = jax.vjp(_mem_fn, tv[0], pv[0], pv[1])
        dq, dk, dv = vjp(tv[1].astype(F32))
        return (dq,), (dk, dv)

    return _rowwise(fn, [q, do], [k, v], [(D, BF16)], [(k.shape, F32), (v.shape, F32)], tm=tm, name=name)


def _loss_head(y, target, *, name):
    T, D = y.shape
    tm = _tile(T, 256, 16)

    def fn(i, tv, pv):
        err = tv[0] - tv[1]
        part = 0.5 * jnp.sum(jnp.sum(err * err, axis=1, keepdims=True) / D, axis=0, keepdims=True)
        return (err / D,), (jnp.broadcast_to(part, (8, LANE)),)

    return _rowwise(fn, [y, target], [], [(D, F32)], [((8, LANE), F32)], tm=tm, name=name)


def _adamw(w, g, m, v, *, name):
    shape = w.shape
    C = shape[-1]
    R = math.prod(shape[:-1])
    tr = _tile(R, 512, 8)
    b1c = 1.0 - ADAM_B1 ** ADAM_STEP
    b2c = 1.0 - ADAM_B2 ** ADAM_STEP

    def body(w_ref, g_ref, m_ref, v_ref, d_ref, mo_ref, vo_ref):
        gg = g_ref[...]
        mn = ADAM_B1 * m_ref[...] + (1.0 - ADAM_B1) * gg
        vn = ADAM_B2 * v_ref[...] + (1.0 - ADAM_B2) * (gg * gg)
        d_ref[...] = -ADAM_LR * ((mn / b1c) / (jnp.sqrt(vn / b2c) + ADAM_EPS) + ADAM_WD * w_ref[...])
        mo_ref[...] = mn
        vo_ref[...] = vn

    spec = pl.BlockSpec((tr, C), lambda i: (i, 0))
    outs = pl.pallas_call(
        body, name=name, grid=(R // tr,),
        in_specs=[spec] * 4, out_specs=[spec] * 3,
        out_shape=[jax.ShapeDtypeStruct((R, C), F32)] * 3,
        compiler_params=_params(("parallel",), 7 * tr * C * 4),
    )(*[a.reshape(R, C) for a in (w, g, m, v)])
    return [o.reshape(shape) for o in outs]


def _pair_sum(a, b, *, name):
    _, R, C = a.shape
    tr = _tile(R, 256, 16)

    def body(a_ref, b_ref, o_ref):
        o_ref[...] = (a_ref[...].astype(F32) + b_ref[...].astype(F32)).astype(o_ref.dtype)

    spec = pl.BlockSpec((4, tr, C), lambda i: (0, i, 0))
    return pl.pallas_call(
        body, name=name, grid=(R // tr,), in_specs=[spec, spec], out_specs=spec,
        out_shape=jax.ShapeDtypeStruct(a.shape, BF16),
        compiler_params=_params(("parallel",), 3 * 4 * tr * C * 4),
    )(a, b)


def _quad_sum(a, *, name):
    _, R, C = a.shape
    tr = _tile(R, 256, 16)

    def body(a_ref, o_ref):
        x = a_ref[...].astype(F32)
        o_ref[...] = ((x[0] + x[1]) + x[2]) + x[3]

    return pl.pallas_call(
        body, name=name, grid=(R // tr,),
        in_specs=[pl.BlockSpec((4, tr, C), lambda i: (0, i, 0))],
        out_specs=pl.BlockSpec((tr, C), lambda i: (i, 0)),
        out_shape=jax.ShapeDtypeStruct((R, C), F32),
        compiler_params=_params(("parallel",), 6 * tr * C * 4),
    )(a)


def _place():
    x, y, c = lax.axis_index("x"), lax.axis_index("y"), lax.axis_index("c")
    return x, y, c


ANY = pl.BlockSpec(memory_space=pl.ANY)


def _all_gather(shards, *, name):
    n = len(shards)

    def body(*refs):
        ins, outs = refs[:n], refs[n:2 * n]
        send_sems, recv_sems, local_sems = refs[2 * n:]
        x, y, c = _place()
        me, sib = (x, y, c), (x, y, 1 - c)
        chips = [(1 - x, y), (x, 1 - y), (1 - x, 1 - y)]

        def rows(w, p):
            r = shards[w].shape[1]
            idx = 4 * p[0] + 2 * p[1] + p[2]
            return outs[w].at[:, pl.ds(pl.multiple_of(idx * r, 8), r), :]

        def copy(w, k, block, to, src=None):
            return pltpu.make_async_remote_copy(
                src_ref=rows(w, block) if src is None else src, dst_ref=rows(w, block),
                send_sem=send_sems.at[w * 7 + k], recv_sem=recv_sems.at[w * 7 + k],
                device_id=to, device_id_type=MESH)

        mine = [pltpu.make_async_copy(ins[w], rows(w, me), local_sems.at[w]) for w in range(n)]
        for cp in mine:
            cp.start()
        first = []
        for w in range(n):
            first.append(copy(w, 0, me, sib, src=ins[w]))
            for j, chip in enumerate(chips):
                first.append(copy(w, 1 + j, me, (*chip, c), src=ins[w]))
        for cp in first:
            cp.start()
        passed = []
        for j, chip in enumerate(chips):
            for w in range(n):
                copy(w, 1 + j, (*chip, c), me).wait_recv()
                fwd = copy(w, 4 + j, (*chip, c), sib)
                fwd.start()
                passed.append(fwd)
        for w in range(n):
            copy(w, 0, sib, me).wait_recv()
            for j, chip in enumerate(chips):
                copy(w, 4 + j, (*chip, 1 - c), me).wait_recv()
        for cp in first + passed:
            cp.wait_send()
        for cp in mine:
            cp.wait()

    return pl.pallas_call(
        body, name=name,
        in_specs=[ANY] * n, out_specs=[ANY] * n,
        out_shape=[jax.ShapeDtypeStruct((s.shape[0], N_DEV * s.shape[1], s.shape[2]), s.dtype) for s in shards],
        scratch_shapes=[pltpu.SemaphoreType.DMA((7 * n,)), pltpu.SemaphoreType.DMA((7 * n,)),
                        pltpu.SemaphoreType.DMA((n,))],
    )(*shards)


def _class_layout(grads, classes):
    heights = [0] * len(set(classes))
    offs = []
    for g, cl in zip(grads, classes):
        offs.append(heights[cl])
        heights[cl] += g.shape[0] // N_DEV
    return offs, heights


def _rs_to_sibling(grads, classes, *, name):
    n = len(grads)
    offs, heights = _class_layout(grads, classes)
    ncl = len(heights)
    cols = [next(g.shape[1] for g, cl in zip(grads, classes) if cl == k) for k in range(ncl)]

    def body(*refs):
        gs = refs[:n]
        mine, land = refs[n:n + ncl], refs[n + ncl:n + 2 * ncl]
        send_sems, recv_sems, local_sems = refs[n + 2 * ncl:]
        x, y, c = _place()
        sib = (x, y, 1 - c)
        started, local = [], []
        for p in range(4):
            px, py = p // 2, p % 2
            for w in range(n):
                r = grads[w].shape[0] // N_DEV
                cl = classes[w]
                sem = cl * 4 + p
                there = gs[w].at[pl.ds(pl.multiple_of((4 * px + 2 * py + 1 - c) * r, 8), r), :]
                here = gs[w].at[pl.ds(pl.multiple_of((4 * px + 2 * py + c) * r, 8), r), :]
                cp = pltpu.make_async_remote_copy(
                    src_ref=there, dst_ref=land[cl].at[p, pl.ds(offs[w], r), :],
                    send_sem=send_sems.at[sem], recv_sem=recv_sems.at[sem],
                    device_id=sib, device_id_type=MESH)
                cp.start()
                lc = pltpu.make_async_copy(here, mine[cl].at[p, pl.ds(offs[w], r), :], local_sems.at[sem])
                lc.start()
        for cl in range(ncl):
            for p in range(4):
                sem = cl * 4 + p
                slab = pltpu.make_async_remote_copy(
                    src_ref=mine[cl].at[p], dst_ref=land[cl].at[p],
                    send_sem=send_sems.at[sem], recv_sem=recv_sems.at[sem],
                    device_id=sib, device_id_type=MESH)
                slab.wait_send()
                slab.wait_recv()
                pltpu.make_async_copy(mine[cl].at[p], mine[cl].at[p], local_sems.at[sem]).wait()

    out_shape = [jax.ShapeDtypeStruct((4, heights[k], cols[k]), BF16) for k in range(ncl)] * 2
    outs = pl.pallas_call(
        body, name=name,
        in_specs=[ANY] * n, out_specs=[ANY] * (2 * ncl), out_shape=out_shape,
        scratch_shapes=[pltpu.SemaphoreType.DMA((4 * ncl,))] * 3,
    )(*grads)
    return outs[:ncl], outs[ncl:]


def _rs_to_chips(parts, *, name):
    ncl = len(parts)

    def body(*refs):
        ins, outs = refs[:ncl], refs[ncl:2 * ncl]
        send_sems, recv_sems, local_sems = refs[2 * ncl:]
        x, y, c = _place()
        chips = [(1 - x, y), (x, 1 - y), (1 - x, 1 - y)]
        here = 2 * x + y
        local, sent = [], []
        for k in range(ncl):
            lc = pltpu.make_async_copy(ins[k].at[here], outs[k].at[here], local_sems.at[k])
            lc.start()
            local.append(lc)
            for j, (cx, cy) in enumerate(chips):
                cp = pltpu.make_async_remote_copy(
                    src_ref=ins[k].at[2 * cx + cy], dst_ref=outs[k].at[here],
                    send_sem=send_sems.at[3 * k + j], recv_sem=recv_sems.at[3 * k + j],
                    device_id=(cx, cy, c), device_id_type=MESH)
                cp.start()
                sent.append(cp)
        for k in range(ncl):
            for j, (cx, cy) in enumerate(chips):
                pltpu.make_async_remote_copy(
                    src_ref=ins[k].at[here], dst_ref=outs[k].at[2 * cx + cy],
                    send_sem=send_sems.at[3 * k + j], recv_sem=recv_sems.at[3 * k + j],
                    device_id=(cx, cy, c), device_id_type=MESH).wait_recv()
        for cp in sent:
            cp.wait_send()
        for lc in local:
            lc.wait()

    return pl.pallas_call(
        body, name=name,
        in_specs=[ANY] * ncl, out_specs=[ANY] * ncl,
        out_shape=[jax.ShapeDtypeStruct(p.shape, p.dtype) for p in parts],
        scratch_shapes=[pltpu.SemaphoreType.DMA((3 * ncl,)), pltpu.SemaphoreType.DMA((3 * ncl,)),
                        pltpu.SemaphoreType.DMA((ncl,))],
    )(*parts)


def _all_reduce_small(v, *, name):
    R = v.shape[0]

    def body(v_ref, o_ref, buf, send_sems, recv_sems):
        x, y, c = _place()
        me = 4 * x + 2 * y + c
        buf[me] = v_ref[...]
        copies = []
        for k in range(1, N_DEV):
            fx, fy, fc = (k >> 2) & 1, (k >> 1) & 1, k & 1
            to = (x ^ fx, y ^ fy, c ^ fc)
            cp = pltpu.make_async_remote_copy(
                src_ref=v_ref, dst_ref=buf.at[me],
                send_sem=send_sems.at[k - 1], recv_sem=recv_sems.at[k - 1],
                device_id=to, device_id_type=MESH)
            cp.start()
            copies.append(cp)
        for k in range(1, N_DEV):
            fx, fy, fc = (k >> 2) & 1, (k >> 1) & 1, k & 1
            frm = 4 * (x ^ fx) + 2 * (y ^ fy) + (c ^ fc)
            pltpu.make_async_remote_copy(
                src_ref=v_ref, dst_ref=buf.at[frm],
                send_sem=send_sems.at[k - 1], recv_sem=recv_sems.at[k - 1],
                device_id=(x ^ fx, y ^ fy, c ^ fc), device_id_type=MESH).wait_recv()
        for cp in copies:
            cp.wait_send()
        acc = buf[0]
        for d in range(1, N_DEV):
            acc = acc + buf[d]
        o_ref[...] = acc

    vm = pl.BlockSpec(memory_space=pltpu.VMEM)
    return pl.pallas_call(
        body, name=name, in_specs=[vm], out_specs=vm,
        out_shape=jax.ShapeDtypeStruct((R, LANE), F32),
        scratch_shapes=[pltpu.VMEM((N_DEV, R, LANE), F32),
                        pltpu.SemaphoreType.DMA((N_DEV - 1,)), pltpu.SemaphoreType.DMA((N_DEV - 1,))],
        compiler_params=pltpu.CompilerParams(vmem_limit_bytes=VMEM_FLOOR),
    )(v)


def _rope_tables(positions):
    half = QK_ROPE // 2
    inv_freq = ROPE_BASE ** (-jnp.arange(half, dtype=F32) / half)
    ang = positions.astype(F32)[:, None] * inv_freq
    cos, sin = jnp.cos(ang), jnp.sin(ang)
    z = jnp.zeros_like(cos)
    z2 = jnp.zeros((positions.shape[0], LANE - QK_ROPE), F32)
    rc = jnp.concatenate([cos, cos, z2], axis=1)
    rs1 = jnp.concatenate([-sin, z, z2], axis=1)
    rs2 = jnp.concatenate([z, sin, z2], axis=1)
    return rc, rs1, rs2


def _block_diag(pool_w):
    G, pg, _ = pool_w.shape
    out = jnp.zeros((G * pg, G * pg), pool_w.dtype)
    for g in range(G):
        out = lax.dynamic_update_slice(out, pool_w[g], (g * pg, g * pg))
    return out


def kernel(x, mem, positions, ln_g, ln_b, ffn1_w13, ffn1_w2, w_in, pool_w, pool_scale, q_norm_g, w_uq, kv_norm_g, w_ukv, w_out, mem_wq, mem_wkv, mem_wo, ffn2_w13, ffn2_w2, loss_target, m_ln_g, m_ln_b, m_ffn1_w13, m_ffn1_w2, m_w_in, m_pool_w, m_pool_scale, m_q_norm_g, m_w_uq, m_kv_norm_g, m_w_ukv, m_w_out, m_mem_wq, m_mem_wkv, m_mem_wo, m_ffn2_w13, m_ffn2_w2, v_ln_g, v_ln_b, v_ffn1_w13, v_ffn1_w2, v_w_in, v_pool_w, v_pool_scale, v_q_norm_g, v_w_uq, v_kv_norm_g, v_w_ukv, v_w_out, v_mem_wq, v_mem_wkv, v_mem_wo, v_ffn2_w13, v_ffn2_w2):
    L = ln_g.shape[0]
    T, D = x.shape[1], x.shape[2]
    F = ffn1_w2.shape[1] * N_DEV
    PW = D // 4
    H = (D - PW) // V_HEAD
    DIN = w_in.shape[2]
    DINP = PW + Q_LORA + KV_LORA + LANE
    QW = QK_NOPE + QK_ROPE
    alpha = (2 * L) ** 0.25
    x2d = x.reshape(T, D)
    memb = mem.reshape(mem.shape[1], D).astype(BF16)
    target = loss_target.reshape(T, D)
    tabs = _rope_tables(positions.reshape(T))

    tr = lambda w: jnp.swapaxes(w, 1, 2)
    shards = [
        jnp.concatenate([tr(ffn1_w13), tr(ffn2_w13)], axis=0).astype(BF16),
        jnp.concatenate([ffn1_w2, ffn2_w2], axis=0).astype(BF16),
        jnp.concatenate([w_out, mem_wq, mem_wo], axis=0).astype(BF16),
        tr(mem_wkv).astype(BF16),
        jnp.pad(w_in, ((0, 0), (0, 0), (0, DINP - DIN))).astype(BF16),
        tr(w_uq).astype(BF16),
        tr(w_ukv).astype(BF16),
        jnp.concatenate([ln_g.reshape(1, 4 * L, -1), ln_b.reshape(1, 4 * L, -1)], axis=1),
    ]
    w13T, w2, wsq, wkvT, winp, wuqT, wukvT, lnp = _all_gather(shards, name="ag_weights")
    wuqT = jnp.pad(wuqT.reshape(L, H, QW, Q_LORA), ((0, 0), (0, 0), (0, HEAD_PAD - QW), (0, 0)))
    wuqT = wuqT.reshape(L, H * HEAD_PAD, Q_LORA)
    lnp = jnp.moveaxis(lnp.reshape(N_DEV, 2, L, 4, D // N_DEV), 0, 3).reshape(2, L, 4, D)
    lng, lnb = lnp[0], lnp[1]
    wbd = [_block_diag(pool_w[l]).astype(BF16) for l in range(L)]

    def ffn_fwd(l, which, xres, xb, k):
        idx = which * L + l
        h13 = _mm(xb, w13T, lead=idx, tb=True, out_dtype=BF16, name=f"l{l}_ffn{which}_h13")
        a = _swiglu_fwd(h13, name=f"l{l}_ffn{which}_act")
        y = _mm(a, w2, lead=idx, name=f"l{l}_ffn{which}_y", tk=F)
        xo, xob = _ln_fwd(xres, y, lng[l, k:k + 1], lnb[l, k:k + 1], alpha=alpha, s=0.5, name=f"l{l}_ln{k}")
        return dict(xres=xres, xb=xb, h13=h13, a=a, y=y), xo, xob

    saved = []
    xres, xb = x2d, x2d.astype(BF16)
    for l in range(L):
        sv = {}
        sv["ffn1"], x1, x1b = ffn_fwd(l, 0, xres, xb, 0)
        hin = _mm(x1b, winp, lead=l, name=f"l{l}_hin")
        pscale = pool_scale[l].reshape(1, PW)
        gq, gkv = q_norm_g[l].reshape(1, Q_LORA), kv_norm_g[l].reshape(1, KV_LORA)
        ypool = _pool_fwd(hin, wbd[l], pscale, name=f"l{l}_pool")
        cqn, ckvn, kpe = _norms_fwd(hin, gq, gkv, pw=PW, name=f"l{l}_norms")
        qraw = _mm(cqn, wuqT, lead=l, tb=True, name=f"l{l}_qraw")
        kv = _mm(ckvn, wukvT, lead=l, tb=True, name=f"l{l}_kv")
        qh, kh, vh = _heads_fwd(qraw, kv, kpe, tabs, H=H, name=f"l{l}_heads")
        o, lse = _flash_fwd(qh, kh, vh, H=H, name=f"l{l}_flash")
        cat = jnp.concatenate([ypool, o], axis=1)
        ymix = _mm(cat, wsq, lead=l, name=f"l{l}_ymix")
        x2, x2b = _ln_fwd(x1, ymix, lng[l, 1:2], lnb[l, 1:2], alpha=alpha, s=1.0, name=f"l{l}_ln1")
        qm = _mm(x2b, wsq, lead=L + l, out_dtype=BF16, name=f"l{l}_qm")
        kvm = _mm(memb, wkvT, lead=l, tb=True, name=f"l{l}_kvm")
        km, vm = kvm[:, :D], kvm[:, D:]
        om = _mem_fwd(qm, km, vm, name=f"l{l}_memattn")
        ymem = _mm(om, wsq, lead=2 * L + l, name=f"l{l}_ymem")
        x3, x3b = _ln_fwd(x2, ymem, lng[l, 2:3], lnb[l, 2:3], alpha=alpha, s=1.0, name=f"l{l}_ln2")
        sv["ffn2"], x4, x4b = ffn_fwd(l, 1, x3, x3b, 3)
        sv.update(x1=x1, x1b=x1b, hin=hin, pscale=pscale, gq=gq, gkv=gkv, cqn=cqn, ckvn=ckvn,
                  qh=qh, kh=kh, vh=vh, lse=lse, cat=cat, ymix=ymix, x2=x2, x2b=x2b, qm=qm, km=km, vm=vm,
                  om=om, ymem=ymem)
        saved.append(sv)
        xres, xb = x4, x4b

    dx, loss_blk = _loss_head(xres, target, name="loss_head")
    loss = lax.psum(loss_blk[0, 0], ("x", "y", "c"))

    gW = {}
    gS = {}

    def ffn_bwd(l, which, sv, dx, k):
        idx = which * L + l
        tag = f"l{l}_ffn{which}"
        dxres, dyb, dg, db = _ln_bwd(sv["xres"], sv["y"], lng[l, k:k + 1], lnb[l, k:k + 1], dx,
                                     alpha=alpha, s=0.5, name=f"l{l}_ln{k}_bwd")
        da = _mm(dyb, w2, lead=idx, tb=True, out_dtype=BF16, name=f"{tag}_da")
        gW[("w2", which, l)] = _mm(sv["a"], dyb, ta=True, out_dtype=BF16, name=f"{tag}_dw2", tn=D)
        dh = _swiglu_bwd(sv["h13"], da, name=f"{tag}_act_bwd")
        dxn = _mm(dh, w13T, lead=idx, add=dxres, name=f"{tag}_dx", tn=D)
        gW[("w13", which, l)] = _mm(dh, sv["xb"], ta=True, out_dtype=BF16, name=f"{tag}_dw13", tn=D)
        gS[("ln_g", l, k)], gS[("ln_b", l, k)] = dg, db
        return dxn

    for l in reversed(range(L)):
        sv = saved[l]
        dx = ffn_bwd(l, 1, sv["ffn2"], dx, 3)
        dxres, dyb, dg, db = _ln_bwd(sv["x2"], sv["ymem"], lng[l, 2:3], lnb[l, 2:3], dx,
                                     alpha=alpha, s=1.0, name=f"l{l}_ln2_bwd")
        gS[("ln_g", l, 2)], gS[("ln_b", l, 2)] = dg, db
        dom = _mm(dyb, wsq, lead=2 * L + l, tb=True, out_dtype=BF16, name=f"l{l}_dom")
        gW[("mem_wo", l)] = _mm(sv["om"], dyb, ta=True, out_dtype=BF16, name=f"l{l}_dwo", tn=D)
        dqm, dkm, dvm = _mem_bwd(sv["qm"], sv["km"], sv["vm"], dom, name=f"l{l}_memattn_bwd")
        dx = _mm(dqm, wsq, lead=L + l, tb=True, add=dxres, name=f"l{l}_dx2", tn=D)
        gW[("mem_wq", l)] = _mm(sv["x2b"], dqm, ta=True, out_dtype=BF16, name=f"l{l}_dwq", tn=D)
        dkvm = jnp.concatenate([dkm, dvm], axis=1).astype(BF16)
        gW[("mem_wkv", l)] = _mm(dkvm, memb, ta=True, out_dtype=BF16, name=f"l{l}_dwkv", tn=D)
        dxres, dyb, dg, db = _ln_bwd(sv["x1"], sv["ymix"], lng[l, 1:2], lnb[l, 1:2], dx,
                                     alpha=alpha, s=1.0, name=f"l{l}_ln1_bwd")
        gS[("ln_g", l, 1)], gS[("ln_b", l, 1)] = dg, db
        dcat = _mm(dyb, wsq, lead=l, tb=True, name=f"l{l}_dcat", tn=D)
        gW[("w_out", l)] = _mm(sv["cat"], dyb, ta=True, out_dtype=BF16, name=f"l{l}_dwout", tn=D)
        dqh, dkh, dvh = _flash_bwd(sv["qh"], sv["kh"], sv["vh"], sv["cat"], dcat, sv["lse"], H=H, pw=PW,
                                   name=f"l{l}_flash_bwd")
        dqraw, dkv, dkpe = _heads_bwd(dqh, dkh, dvh, tabs, H=H, name=f"l{l}_heads_bwd")
        dcq = _mm(dqraw, wuqT, lead=l, name=f"l{l}_dcq")
        gW[("w_uq", l)] = _mm(dqraw, sv["cqn"], ta=True, out_dtype=BF16, name=f"l{l}_dwuq")
        dckv = _mm(dkv, wukvT, lead=l, name=f"l{l}_dckv")
        gW[("w_ukv", l)] = _mm(dkv, sv["ckvn"], ta=True, out_dtype=BF16, name=f"l{l}_dwukv")
        du, dwbd, dps = _pool_bwd(sv["hin"], dcat, wbd[l], sv["pscale"], name=f"l{l}_pool_bwd")
        dhin, dgq, dgkv = _norms_bwd(sv["hin"], sv["gq"], sv["gkv"], dcq, dckv, dkpe, du, pw=PW,
                                     name=f"l{l}_norms_bwd")
        pg = PW // len(POOL_WINDOWS)
        gS[("pool_w", l)] = jnp.stack([dwbd[g * pg:(g + 1) * pg, g * pg:(g + 1) * pg]
                                       for g in range(len(POOL_WINDOWS))])
        gS[("pool_scale", l)], gS[("q_norm_g", l)], gS[("kv_norm_g", l)] = dps, dgq, dgkv
        dx = _mm(dhin, winp, lead=l, tb=True, add=dxres, name=f"l{l}_dx1", tn=D)
        gW[("w_in", l)] = _mm(sv["x1b"], dhin, ta=True, out_dtype=BF16, name=f"l{l}_dwin", tn=DINP)
        dx = ffn_bwd(l, 0, sv["ffn1"], dx, 0)
    grad_x = dx.reshape(1, T, D)

    rs_list, rs_class = [], []
    for l in range(L):
        for key, cl in ((("w13", 0, l), 0), (("w13", 1, l), 0), (("w2", 0, l), 0), (("w2", 1, l), 0),
                        (("w_out", l), 0), (("mem_wq", l), 0), (("mem_wo", l), 0), (("mem_wkv", l), 0),
                        (("w_in", l), 1), (("w_uq", l), 2), (("w_ukv", l), 3)):
            g = gW[key]
            if key[0] == "w_uq":
                g = g.reshape(H, HEAD_PAD, Q_LORA)[:, :QW, :].reshape(H * QW, Q_LORA)
            rs_list.append((key, g))
            rs_class.append(cl)
    garrs = [g for _, g in rs_list]
    mine, landed = _rs_to_sibling(garrs, rs_class, name="rs_sibling")
    parts = [_pair_sum(a, b, name=f"rs_pair_sum{k}") for k, (a, b) in enumerate(zip(mine, landed))]
    gathered = _rs_to_chips(parts, name="rs_chips")
    sums = [_quad_sum(a, name=f"rs_quad_sum{k}") for k, a in enumerate(gathered)]
    offs, _ = _class_layout(garrs, rs_class)
    gsh = {}
    for (key, g), cl, off in zip(rs_list, rs_class, offs):
        gsh[key] = sums[cl][off:off + g.shape[0] // N_DEV, :]

    small_keys = []
    for l in range(L):
        small_keys += [("pool_w", l), ("pool_scale", l), ("q_norm_g", l), ("kv_norm_g", l)]
        small_keys += [("ln_g", l, k) for k in range(4)] + [("ln_b", l, k) for k in range(4)]
    flat = jnp.concatenate([gS[k].reshape(-1) for k in small_keys])
    n_small = flat.shape[0]
    rows = -(-n_small // (8 * LANE)) * 8
    flat = jnp.pad(flat, (0, rows * LANE - n_small)).reshape(rows, LANE)
    red = _all_reduce_small(flat, name="ar_small").reshape(-1)
    gsm, pos = {}, 0
    for k in small_keys:
        size = math.prod(gS[k].shape)
        gsm[k] = red[pos:pos + size].reshape(gS[k].shape)
        pos += size

    me = 4 * lax.axis_index("x") + 2 * lax.axis_index("y") + lax.axis_index("c")
    dsh = D // N_DEV
    stack = lambda f: jnp.stack([f(l) for l in range(L)])
    g_ln_g = stack(lambda l: jnp.concatenate([gsm[("ln_g", l, k)] for k in range(4)], axis=0))
    g_ln_b = stack(lambda l: jnp.concatenate([gsm[("ln_b", l, k)] for k in range(4)], axis=0))
    grads = {
        "ln_g": lax.dynamic_slice_in_dim(g_ln_g, me * dsh, dsh, axis=2),
        "ln_b": lax.dynamic_slice_in_dim(g_ln_b, me * dsh, dsh, axis=2),
        "ffn1_w13": stack(lambda l: gsh[("w13", 0, l)].T),
        "ffn1_w2": stack(lambda l: gsh[("w2", 0, l)]),
        "w_in": stack(lambda l: gsh[("w_in", l)][:, :DIN]),
        "pool_w": stack(lambda l: gsm[("pool_w", l)]),
        "pool_scale": stack(lambda l: gsm[("pool_scale", l)].reshape(PW)),
        "q_norm_g": stack(lambda l: gsm[("q_norm_g", l)].reshape(Q_LORA)),
        "w_uq": stack(lambda l: gsh[("w_uq", l)].T),
        "kv_norm_g": stack(lambda l: gsm[("kv_norm_g", l)].reshape(KV_LORA)),
        "w_ukv": stack(lambda l: gsh[("w_ukv", l)].T),
        "w_out": stack(lambda l: gsh[("w_out", l)]),
        "mem_wq": stack(lambda l: gsh[("mem_wq", l)]),
        "mem_wkv": stack(lambda l: gsh[("mem_wkv", l)].T),
        "mem_wo": stack(lambda l: gsh[("mem_wo", l)]),
        "ffn2_w13": stack(lambda l: gsh[("w13", 1, l)].T),
        "ffn2_w2": stack(lambda l: gsh[("w2", 1, l)]),
    }

    names = ["ln_g", "ln_b", "ffn1_w13", "ffn1_w2", "w_in", "pool_w", "pool_scale", "q_norm_g", "w_uq",
             "kv_norm_g", "w_ukv", "w_out", "mem_wq", "mem_wkv", "mem_wo", "ffn2_w13", "ffn2_w2"]
    weights = dict(ln_g=ln_g, ln_b=ln_b, ffn1_w13=ffn1_w13, ffn1_w2=ffn1_w2, w_in=w_in, pool_w=pool_w,
                   pool_scale=pool_scale, q_norm_g=q_norm_g, w_uq=w_uq, kv_norm_g=kv_norm_g, w_ukv=w_ukv,
                   w_out=w_out, mem_wq=mem_wq, mem_wkv=mem_wkv, mem_wo=mem_wo, ffn2_w13=ffn2_w13,
                   ffn2_w2=ffn2_w2)
    ms = dict(ln_g=m_ln_g, ln_b=m_ln_b, ffn1_w13=m_ffn1_w13, ffn1_w2=m_ffn1_w2, w_in=m_w_in, pool_w=m_pool_w,
              pool_scale=m_pool_scale, q_norm_g=m_q_norm_g, w_uq=m_w_uq, kv_norm_g=m_kv_norm_g,
              w_ukv=m_w_ukv, w_out=m_w_out, mem_wq=m_mem_wq, mem_wkv=m_mem_wkv, mem_wo=m_mem_wo,
              ffn2_w13=m_ffn2_w13, ffn2_w2=m_ffn2_w2)
    vs = dict(ln_g=v_ln_g, ln_b=v_ln_b, ffn1_w13=v_ffn1_w13, ffn1_w2=v_ffn1_w2, w_in=v_w_in, pool_w=v_pool_w,
              pool_scale=v_pool_scale, q_norm_g=v_q_norm_g, w_uq=v_w_uq, kv_norm_g=v_kv_norm_g,
              w_ukv=v_w_ukv, w_out=v_w_out, mem_wq=v_mem_wq, mem_wkv=v_mem_wkv, mem_wo=v_mem_wo,
              ffn2_w13=v_ffn2_w13, ffn2_w2=v_ffn2_w2)
    deltas, new_m, new_v = [], [], []
    for nme in names:
        d, mn, vn = _adamw(weights[nme], grads[nme], ms[nme], vs[nme], name=f"adamw_{nme}")
        deltas.append(d)
        new_m.append(mn)
        new_v.append(vn)
    return (loss, grad_x, *[grads[nme] for nme in names], *deltas, *new_m, *new_v)
```

```python
import functools
import math

import jax
import jax.numpy as jnp
from jax import lax
from jax.experimental import pallas as pl
from jax.experimental.pallas import tpu as pltpu

F32 = jnp.float32
BF16 = jnp.bfloat16
MESH = pl.DeviceIdType.MESH

CHUNK = 64
MEM_HEADS = 4
POOL_WINDOWS = (2, 4, 8, 16)
QK_NOPE = 128
QK_ROPE = 64
V_HEAD = 128
Q_LORA = 256
KV_LORA = 128
ROPE_BASE = 10000.0
LN_EPS = 1e-5
RMS_EPS = 1e-6
NEG_INF = -1e30
ADAM_LR = 0.001
ADAM_B1 = 0.9
ADAM_B2 = 0.999
ADAM_EPS = 1e-08
ADAM_WD = 0.01
ADAM_STEP = 10

N_DEV = 8
LANE = 128
HEAD_PAD = 2 * LANE
POOL_HALO = 16
VMEM_CAP = 56 * 1024 * 1024
VMEM_FLOOR = 32 * 1024 * 1024


def _tile(n, pref, mult):
    t = (min(pref, n) // mult) * mult
    while t >= mult:
        if n % t == 0:
            return t
        t -= mult
    return n


def _params(sem, est_bytes):
    limit = int(min(max(2 * est_bytes + (8 << 20), VMEM_FLOOR), VMEM_CAP))
    return pltpu.CompilerParams(dimension_semantics=sem, vmem_limit_bytes=limit)


def _nbytes(shape, dtype):
    return math.prod(shape) * jnp.dtype(dtype).itemsize


def _dg(a, b, ca, cb):
    return lax.dot_general(a.astype(BF16), b.astype(BF16), (((ca,), (cb,)), ((), ())),
                           preferred_element_type=F32)


@jax.custom_vjp
def _bdot_nn(a, b):
    return _dg(a, b, 1, 0)


def _bdot_nn_fwd(a, b):
    return _dg(a, b, 1, 0), (a, b)


def _bdot_nn_bwd(res, ct):
    a, b = res
    return _dg(ct, b, 1, 1).astype(a.dtype), _dg(a, ct, 0, 0).astype(b.dtype)


_bdot_nn.defvjp(_bdot_nn_fwd, _bdot_nn_bwd)


@jax.custom_vjp
def _bdot_nt(a, b):
    return _dg(a, b, 1, 1)


def _bdot_nt_fwd(a, b):
    return _dg(a, b, 1, 1), (a, b)


def _bdot_nt_bwd(res, ct):
    a, b = res
    return _dg(ct, b, 1, 0).astype(a.dtype), _dg(ct, a, 0, 0).astype(b.dtype)


_bdot_nt.defvjp(_bdot_nt_fwd, _bdot_nt_bwd)


@functools.partial(jax.custom_vjp, nondiff_argnums=(1,))
def _lane_roll(x, shift):
    return pltpu.roll(x, shift % x.shape[1], axis=1)


def _lane_roll_fwd(x, shift):
    return _lane_roll(x, shift), None


def _lane_roll_bwd(shift, _, ct):
    return (_lane_roll(ct, -shift),)


_lane_roll.defvjp(_lane_roll_fwd, _lane_roll_bwd)


@functools.partial(jax.custom_vjp, nondiff_argnums=(1, 2))
def _cols(x, lo, hi):
    return x[:, lo:hi]


def _cols_fwd(x, lo, hi):
    return x[:, lo:hi], x.shape[1]


def _cols_bwd(lo, hi, width, ct):
    parts = []
    if lo > 0:
        parts.append(jnp.zeros((ct.shape[0], lo), ct.dtype))
    parts.append(ct)
    if hi < width:
        parts.append(jnp.zeros((ct.shape[0], width - hi), ct.dtype))
    return (jnp.concatenate(parts, axis=1) if len(parts) > 1 else ct,)


_cols.defvjp(_cols_fwd, _cols_bwd)


MM_VMEM_BUDGET = 20 * 1024 * 1024


def _mm(a, b, *, name, ta=False, tb=False, out_dtype=F32, lead=None, add=None, add_scale=1.0,
        tm=1024, tn=1024, tk=2816):
    if ta:
        K, M = a.shape
    else:
        M, K = a.shape
    bshape = b.shape[1:] if lead is not None else b.shape
    if tb:
        N, Kb = bshape
    else:
        Kb, N = bshape
    assert K == Kb, (name, a.shape, b.shape)

    def blocks(tm, tn, tk):
        tm = _tile(M, tm, LANE if ta else 16)
        tn = _tile(N, tn, LANE)
        tk = _tile(K, tk, LANE)
        nbytes = (tm * tk * a.dtype.itemsize + tk * tn * b.dtype.itemsize
                  + tm * tn * (jnp.dtype(out_dtype).itemsize + (4 if K // tk > 1 else 0)
                               + (add.dtype.itemsize if add is not None else 0)))
        return tm, tn, tk, nbytes

    tm, tn, tk, est = blocks(tm, tn, tk)
    for shrink in ("m", "k", "m", "k", "n"):
        if est <= MM_VMEM_BUDGET:
            break
        if shrink == "m":
            tm, tn, tk, est = blocks(max(tm // 2, LANE), tn, tk)
        elif shrink == "k":
            tm, tn, tk, est = blocks(tm, tn, max(tk // 2, LANE))
        else:
            tm, tn, tk, est = blocks(tm, max(tn // 2, LANE), tk)
    nk = K // tk
    ca = 0 if ta else 1
    cb = 1 if tb else 0

    def body(*refs):
        a_ref, b_ref = refs[0], refs[1]
        add_ref = refs[2] if add is not None else None
        o_ref = refs[3] if add is not None else refs[2]

        def finish(r):
            if add_ref is not None:
                r = r + add_scale * add_ref[...].astype(F32)
            o_ref[...] = r.astype(o_ref.dtype)

        if nk == 1:
            finish(_dg(a_ref[...], b_ref[...], ca, cb))
            return
        acc_ref = refs[-1]
        k = pl.program_id(2)

        @pl.when(k == 0)
        def _():
            acc_ref[...] = jnp.zeros_like(acc_ref)

        acc_ref[...] += _dg(a_ref[...], b_ref[...], ca, cb)

        @pl.when(k == nk - 1)
        def _():
            finish(acc_ref[...])

    a_blk = (tk, tm) if ta else (tm, tk)
    a_map = (lambda i, j, k: (k, i)) if ta else (lambda i, j, k: (i, k))
    b_blk = (tn, tk) if tb else (tk, tn)
    if lead is None:
        b_map = (lambda i, j, k: (j, k)) if tb else (lambda i, j, k: (k, j))
        b_spec = pl.BlockSpec(b_blk, b_map)
    else:
        b_map = (lambda i, j, k: (lead, j, k)) if tb else (lambda i, j, k: (lead, k, j))
        b_spec = pl.BlockSpec((None,) + b_blk, b_map)
    in_specs = [pl.BlockSpec(a_blk, a_map), b_spec]
    args = [a, b]
    if add is not None:
        in_specs.append(pl.BlockSpec((tm, tn), lambda i, j, k: (i, j)))
        args.append(add)
    return pl.pallas_call(
        body, name=name,
        grid=(M // tm, N // tn, nk),
        in_specs=in_specs,
        out_specs=pl.BlockSpec((tm, tn), lambda i, j, k: (i, j)),
        out_shape=jax.ShapeDtypeStruct((M, N), out_dtype),
        scratch_shapes=[pltpu.VMEM((tm, tn), F32)] if nk > 1 else [],
        compiler_params=_params(("parallel", "parallel", "arbitrary"), est + tm * tn * 4),
    )(*args)


def _rowwise(fn, tiles, params, tile_outs, acc_outs=(), *, tm, name):
    tile_arrays, tile_specs = [], []
    for t in tiles:
        if isinstance(t, tuple):
            tile_arrays.append(t[0])
            tile_specs.append(t[1])
        else:
            tile_arrays.append(t)
            tile_specs.append(pl.BlockSpec((tm, t.shape[1]), lambda i: (i, 0)))
    T = tile_arrays[0].shape[0]
    nt, np_, nto, nao = len(tile_arrays), len(params), len(tile_outs), len(acc_outs)

    def body(*refs):
        i = pl.program_id(0)
        tvals = [r[...] for r in refs[:nt]]
        pvals = [r[...] for r in refs[nt:nt + np_]]
        to_refs = refs[nt + np_:nt + np_ + nto]
        ao_refs = refs[nt + np_ + nto:]
        touts, aouts = fn(i, tvals, pvals)
        for r, v in zip(to_refs, touts):
            r[...] = v.astype(r.dtype)
        if nao:
            @pl.when(i == 0)
            def _():
                for r in ao_refs:
                    r[...] = jnp.zeros_like(r)
            for r, v in zip(ao_refs, aouts):
                r[...] += v.astype(r.dtype)

    in_specs = tile_specs + [pl.BlockSpec(p.shape, lambda i: (0, 0)) for p in params]
    out_specs = [pl.BlockSpec((tm, c), lambda i: (i, 0)) for c, _ in tile_outs]
    out_specs += [pl.BlockSpec(s, lambda i: (0, 0)) for s, _ in acc_outs]
    out_shape = [jax.ShapeDtypeStruct((T, c), d) for c, d in tile_outs]
    out_shape += [jax.ShapeDtypeStruct(s, d) for s, d in acc_outs]
    width = sum(s.block_shape[-1] for s in tile_specs) + sum(c for c, _ in tile_outs)
    est = 6 * tm * width * 4 + sum(_nbytes(p.shape, F32) for p in params) * 4
    return pl.pallas_call(
        body, name=name, grid=(T // tm,),
        in_specs=in_specs, out_specs=out_specs, out_shape=out_shape,
        compiler_params=_params(("arbitrary",) if nao else ("parallel",), est),
    )(*tile_arrays, *params)


def _ln_fn(alpha, s, xres, y, g, b):
    z = alpha * xres.astype(F32) + s * y.astype(F32)
    mu = jnp.mean(z, axis=-1, keepdims=True)
    zc = z - mu
    var = jnp.mean(zc * zc, axis=-1, keepdims=True)
    return zc * lax.rsqrt(var + LN_EPS) * g + b


def _ln_fwd(xres, y, g, b, *, alpha, s, name):
    T, D = xres.shape
    tm = _tile(T, 256, 16)

    def fn(i, tv, pv):
        out = _ln_fn(alpha, s, tv[0], tv[1], pv[0], pv[1])
        return (out, out), ()

    return _rowwise(fn, [xres, y], [g, b], [(D, F32), (D, BF16)], tm=tm, name=name)


def _ln_bwd(xres, y, g, b, dout, *, alpha, s, name):
    T, D = xres.shape
    tm = _tile(T, 256, 16)

    def fn(i, tv, pv):
        _, vjp = jax.vjp(functools.partial(_ln_fn, alpha, s), tv[0], tv[1], pv[0], pv[1])
        dx, dy, dg, db = vjp(tv[2].astype(F32))
        return (dx, dy), (dg, db)

    return _rowwise(fn, [xres, y, dout], [g, b], [(D, F32), (D, BF16)],
                    [((1, D), F32), ((1, D), F32)], tm=tm, name=name)


def _swiglu_fwd(h13, *, name):
    T, F2 = h13.shape
    F = F2 // 2
    tm = _tile(T, 512, 16)
    tc = _tile(F, 512, LANE)
    nf = F // tc

    def body(g_ref, u_ref, a_ref):
        g = g_ref[...].astype(F32)
        u = u_ref[...].astype(F32)
        a_ref[...] = (g * jax.nn.sigmoid(g) * u).astype(a_ref.dtype)

    return pl.pallas_call(
        body, name=name, grid=(T // tm, nf),
        in_specs=[pl.BlockSpec((tm, tc), lambda i, j: (i, j)),
                  pl.BlockSpec((tm, tc), lambda i, j: (i, j + nf))],
        out_specs=pl.BlockSpec((tm, tc), lambda i, j: (i, j)),
        out_shape=jax.ShapeDtypeStruct((T, F), BF16),
        compiler_params=_params(("parallel", "parallel"), 8 * tm * tc * 4),
    )(h13, h13)


def _swiglu_bwd(h13, da, *, name):
    T, F2 = h13.shape
    F = F2 // 2
    tm = _tile(T, 512, 16)
    tc = _tile(F, 512, LANE)
    nf = F // tc

    def body(g_ref, u_ref, da_ref, dh_ref):
        j = pl.program_id(1)
        g = g_ref[...].astype(F32)
        u = u_ref[...].astype(F32)
        d = da_ref[...].astype(F32)
        sig = jax.nn.sigmoid(g)
        dgate = d * u * sig * (1.0 + g * (1.0 - sig))
        dup = d * g * sig
        dh_ref[...] = jnp.where(j < nf, dgate, dup).astype(dh_ref.dtype)

    return pl.pallas_call(
        body, name=name, grid=(T // tm, 2 * nf),
        in_specs=[pl.BlockSpec((tm, tc), lambda i, j: (i, j % nf)),
                  pl.BlockSpec((tm, tc), lambda i, j: (i, j % nf + nf)),
                  pl.BlockSpec((tm, tc), lambda i, j: (i, j % nf))],
        out_specs=pl.BlockSpec((tm, tc), lambda i, j: (i, j)),
        out_shape=jax.ShapeDtypeStruct((T, F2), BF16),
        compiler_params=_params(("parallel", "parallel"), 12 * tm * tc * 4),
    )(h13, h13, da)


def _pool_select(parts, pw):
    pg = pw // len(POOL_WINDOWS)
    grp = lax.broadcasted_iota(jnp.int32, parts[0].shape, 1) // pg
    out = parts[3]
    for g in (2, 1, 0):
        out = jnp.where(grp == g, parts[g], out)
    return out


def _pool_count(t0, rows, pw):
    pg = pw // len(POOL_WINDOWS)
    grp = lax.broadcasted_iota(jnp.int32, (rows, pw), 1) // pg
    win = jnp.where(grp == 0, POOL_WINDOWS[0],
                    jnp.where(grp == 1, POOL_WINDOWS[1],
                              jnp.where(grp == 2, POOL_WINDOWS[2], POOL_WINDOWS[3])))
    t = t0 + lax.broadcasted_iota(jnp.int32, (rows, pw), 0)
    return jnp.minimum(t + 1, win).astype(F32)


def _window_sums(ext, up):
    n = ext.shape[0]
    sums, cur, k = [], ext, 1
    for _ in POOL_WINDOWS:
        cur = cur + pltpu.roll(cur, (n - k) if up else k, axis=0)
        sums.append(cur)
        k *= 2
    return sums


def _pool_delta(u, halo, t0):
    tm, pw = u.shape
    ext = jnp.concatenate([halo, u], axis=0)
    sums = [s[POOL_HALO:, :] for s in _window_sums(ext, up=False)]
    return _pool_select(sums, pw) / _pool_count(t0, tm, pw) - u


def _pool_fwd(hin, wbd, scale, *, name):
    T = hin.shape[0]
    pw = wbd.shape[0]
    tm = _tile(T, 256, POOL_HALO)
    per = tm // POOL_HALO

    def body(u_ref, halo_ref, w_ref, s_ref, y_ref):
        i = pl.program_id(0)
        halo = jnp.where(i > 0, halo_ref[...], 0.0)
        d = _pool_delta(u_ref[...], halo, i * tm)
        y_ref[...] = (_dg(d, w_ref[...], 1, 0) * s_ref[...]).astype(y_ref.dtype)

    return pl.pallas_call(
        body, name=name, grid=(T // tm,),
        in_specs=[pl.BlockSpec((tm, pw), lambda i: (i, 0)),
                  pl.BlockSpec((POOL_HALO, pw), lambda i: (jnp.maximum(i * per - 1, 0), 0)),
                  pl.BlockSpec((pw, pw), lambda i: (0, 0)),
                  pl.BlockSpec((1, pw), lambda i: (0, 0))],
        out_specs=pl.BlockSpec((tm, pw), lambda i: (i, 0)),
        out_shape=jax.ShapeDtypeStruct((T, pw), BF16),
        compiler_params=_params(("parallel",), 16 * tm * pw * 4),
    )(hin, hin, wbd, scale)


def _pool_bwd(hin, dcat, wbd, scale, *, name):
    T = hin.shape[0]
    pw = wbd.shape[0]
    tm = _tile(T, 256, POOL_HALO)
    per = tm // POOL_HALO
    nt = T // tm

    def body(u_ref, halo_ref, dy_ref, dyn_ref, w_ref, s_ref, du_ref, dw_ref, ds_ref):
        i = pl.program_id(0)

        @pl.when(i == 0)
        def _():
            dw_ref[...] = jnp.zeros_like(dw_ref)
            ds_ref[...] = jnp.zeros_like(ds_ref)

        halo = jnp.where(i > 0, halo_ref[...], 0.0)
        d = _pool_delta(u_ref[...], halo, i * tm)
        w = w_ref[...]
        sc = s_ref[...]
        dy = dy_ref[...]
        dyn = jnp.where(i < nt - 1, dyn_ref[...], 0.0)
        ds_ref[...] += jnp.sum(dy * _dg(d, w, 1, 0), axis=0, keepdims=True)
        dys = dy * sc
        dw_ref[...] += _dg(d, dys, 0, 0)
        dys_ext = jnp.concatenate([dys, dyn * sc], axis=0)
        dd_ext = _dg(dys_ext, w, 1, 1)
        ddp = dd_ext / _pool_count(i * tm, tm + POOL_HALO, pw)
        sums = [s[:tm, :] for s in _window_sums(ddp, up=True)]
        du_ref[...] = _pool_select(sums, pw) - dd_ext[:tm, :]

    return pl.pallas_call(
        body, name=name, grid=(nt,),
        in_specs=[pl.BlockSpec((tm, pw), lambda i: (i, 0)),
                  pl.BlockSpec((POOL_HALO, pw), lambda i: (jnp.maximum(i * per - 1, 0), 0)),
                  pl.BlockSpec((tm, pw), lambda i: (i, 0)),
                  pl.BlockSpec((POOL_HALO, pw), lambda i: (jnp.minimum((i + 1) * per, nt * per - 1), 0)),
                  pl.BlockSpec((pw, pw), lambda i: (0, 0)),
                  pl.BlockSpec((1, pw), lambda i: (0, 0))],
        out_specs=[pl.BlockSpec((tm, pw), lambda i: (i, 0)),
                   pl.BlockSpec((pw, pw), lambda i: (0, 0)),
                   pl.BlockSpec((1, pw), lambda i: (0, 0))],
        out_shape=[jax.ShapeDtypeStruct((T, pw), F32),
                   jax.ShapeDtypeStruct((pw, pw), F32),
                   jax.ShapeDtypeStruct((1, pw), F32)],
        compiler_params=_params(("arbitrary",), 24 * tm * pw * 4),
    )(hin, hin, dcat, dcat, wbd, scale)


def _rms(x, g):
    return x * lax.rsqrt(jnp.mean(x * x, axis=-1, keepdims=True) + RMS_EPS) * g


def _norms_fn(pw, h, gq, gkv):
    o1 = pw + Q_LORA
    o2 = o1 + KV_LORA
    return (_rms(_cols(h, pw, o1), gq), _rms(_cols(h, o1, o2), gkv), _cols(h, o2, h.shape[1]))


def _norms_fwd(hin, gq, gkv, *, pw, name):
    tm = _tile(hin.shape[0], 256, 16)

    def fn(i, tv, pv):
        return _norms_fn(pw, tv[0], pv[0], pv[1]), ()

    return _rowwise(fn, [hin], [gq, gkv], [(Q_LORA, BF16), (KV_LORA, BF16), (LANE, F32)], tm=tm, name=name)


def _norms_bwd(hin, gq, gkv, dcq, dckv, dkpe, du, *, pw, name):
    tm = _tile(hin.shape[0], 256, 16)
    dinp = hin.shape[1]

    def fn(i, tv, pv):
        _, vjp = jax.vjp(functools.partial(_norms_fn, pw), tv[0], pv[0], pv[1])
        dh, dgq, dgkv = vjp((tv[1].astype(F32), tv[2].astype(F32), tv[3].astype(F32)))
        dh = jnp.concatenate([tv[4], dh[:, pw:]], axis=1)
        return (dh,), (dgq, dgkv)

    return _rowwise(fn, [hin, dcq, dckv, dkpe, du], [gq, gkv], [(dinp, BF16)],
                    [((1, Q_LORA), F32), ((1, KV_LORA), F32)], tm=tm, name=name)


def _heads_fn(H, qraw, kv, kpe, rc, rs1, rs2):
    half = QK_ROPE // 2

    def rope(blk):
        return blk * rc + _lane_roll(blk, -half) * rs1 + _lane_roll(blk, half) * rs2

    krot = rope(kpe)
    qs, ks, vs = [], [], []
    for h in range(H):
        lo = h * HEAD_PAD
        qs += [_cols(qraw, lo, lo + LANE), rope(_cols(qraw, lo + LANE, lo + HEAD_PAD))]
        ks += [_cols(kv, lo, lo + LANE), krot]
        vs += [_cols(kv, lo + LANE, lo + HEAD_PAD)]
    return jnp.concatenate(qs, axis=1), jnp.concatenate(ks, axis=1), jnp.concatenate(vs, axis=1)


def _heads_fwd(qraw, kv, kpe, tabs, *, H, name):
    tm = _tile(qraw.shape[0], 256, 16)

    def fn(i, tv, pv):
        return _heads_fn(H, *tv), ()

    return _rowwise(fn, [qraw, kv, kpe, *tabs], [],
                    [(H * HEAD_PAD, BF16), (H * HEAD_PAD, BF16), (H * V_HEAD, BF16)], tm=tm, name=name)


def _heads_bwd(dq, dk, dv, tabs, *, H, name):
    tm = _tile(dq.shape[0], 256, 16)

    def fn(i, tv, pv):
        z = jnp.zeros((tm, H * HEAD_PAD), F32)
        zk = jnp.zeros((tm, LANE), F32)
        rc, rs1, rs2 = tv[3], tv[4], tv[5]
        _, vjp = jax.vjp(lambda a, b, c: _heads_fn(H, a, b, c, rc, rs1, rs2), z, z, zk)
        return vjp((tv[0], tv[1], tv[2])), ()

    return _rowwise(fn, [dq, dk, dv, *tabs], [],
                    [(H * HEAD_PAD, BF16), (H * HEAD_PAD, BF16), (LANE, F32)], tm=tm, name=name)


def _diag_mask(t):
    r = lax.broadcasted_iota(jnp.int32, (t, t), 0) // CHUNK
    c = lax.broadcasted_iota(jnp.int32, (t, t), 1) // CHUNK
    return r >= c


def _flash_fwd(qh, kh, vh, *, H, name):
    T = qh.shape[0]
    t = _tile(T, 512, CHUNK)
    scale = (QK_NOPE + QK_ROPE) ** -0.5

    def body(q_ref, k_ref, v_ref, o_ref, lse_ref):
        i = pl.program_id(1)
        q = q_ref[...]

        def blk(j, carry, masked):
            m, l, acc = carry
            rows = pl.ds(pl.multiple_of(j * t, t), t)
            s = _dg(q, k_ref[rows, :], 1, 1) * scale
            if masked:
                s = jnp.where(_diag_mask(t), s, NEG_INF)
            mn = jnp.maximum(m, jnp.max(s, axis=1, keepdims=True))
            p = jnp.exp(s - mn)
            corr = jnp.exp(m - mn)
            l = corr * l + jnp.sum(p, axis=1, keepdims=True)
            acc = corr * acc + _dg(p, v_ref[rows, :], 1, 0)
            return mn, l, acc

        init = (jnp.full((t, 1), NEG_INF, F32), jnp.zeros((t, 1), F32), jnp.zeros((t, V_HEAD), F32))
        carry = lax.fori_loop(0, i, lambda j, c: blk(j, c, False), init)
        m, l, acc = blk(i, carry, True)
        o_ref[...] = (acc / l).astype(o_ref.dtype)
        lse_ref[...] = jnp.broadcast_to(m + jnp.log(l), (t, V_HEAD))

    est = 2 * T * (HEAD_PAD + V_HEAD) * 2 + 8 * t * t * 4
    return pl.pallas_call(
        body, name=name, grid=(H, T // t),
        in_specs=[pl.BlockSpec((t, HEAD_PAD), lambda h, i: (i, h)),
                  pl.BlockSpec((T, HEAD_PAD), lambda h, i: (0, h)),
                  pl.BlockSpec((T, V_HEAD), lambda h, i: (0, h))],
        out_specs=[pl.BlockSpec((t, V_HEAD), lambda h, i: (i, h)),
                   pl.BlockSpec((t, V_HEAD), lambda h, i: (i, h))],
        out_shape=[jax.ShapeDtypeStruct((T, H * V_HEAD), BF16),
                   jax.ShapeDtypeStruct((T, H * V_HEAD), F32)],
        compiler_params=_params(("parallel", "parallel"), est),
    )(qh, kh, vh)


def _flash_bwd(qh, kh, vh, cat, dcat, lse, *, H, pw, name):
    T = qh.shape[0]
    t = _tile(T, 512, CHUNK)
    nb = T // t
    off = pw // V_HEAD
    scale = (QK_NOPE + QK_ROPE) ** -0.5

    def body(q_ref, k_ref, v_ref, o_ref, do_ref, lse_ref, dq_ref, dk_ref, dv_ref):
        j = pl.program_id(1)

        @pl.when(j == 0)
        def _():
            dq_ref[...] = jnp.zeros_like(dq_ref)

        kj = k_ref[...]
        vj = v_ref[...]

        def blk(i, carry, masked):
            dk, dv = carry
            rows = pl.ds(pl.multiple_of(i * t, t), t)
            qi = q_ref[rows, :]
            doi = do_ref[rows, :]
            oi = o_ref[rows, :].astype(F32)
            lsei = lse_ref[rows, :][:, :1]
            s = _dg(qi, kj, 1, 1) * scale
            if masked:
                s = jnp.where(_diag_mask(t), s, NEG_INF)
            p = jnp.exp(s - lsei)
            dv = dv + _dg(p, doi, 0, 0)
            dp = _dg(doi, vj, 1, 1)
            di = jnp.sum(doi * oi, axis=1, keepdims=True)
            ds = p * (dp - di) * scale
            dk = dk + _dg(ds, qi, 0, 0)
            dq_ref[rows, :] += _dg(ds, kj, 1, 0)
            return dk, dv

        carry = blk(j, (jnp.zeros((t, HEAD_PAD), F32), jnp.zeros((t, V_HEAD), F32)), True)
        dk, dv = lax.fori_loop(j + 1, nb, lambda i, c: blk(i, c, False), carry)
        dk_ref[...] = dk
        dv_ref[...] = dv

    est = T * (HEAD_PAD * 2 + V_HEAD * 2 + V_HEAD * 4 + V_HEAD * 4 + HEAD_PAD * 4) + 10 * t * t * 4
    return pl.pallas_call(
        body, name=name, grid=(H, nb),
        in_specs=[pl.BlockSpec((T, HEAD_PAD), lambda h, j: (0, h)),
                  pl.BlockSpec((t, HEAD_PAD), lambda h, j: (j, h)),
                  pl.BlockSpec((t, V_HEAD), lambda h, j: (j, h)),
                  pl.BlockSpec((T, V_HEAD), lambda h, j: (0, off + h)),
                  pl.BlockSpec((T, V_HEAD), lambda h, j: (0, off + h)),
                  pl.BlockSpec((T, V_HEAD), lambda h, j: (0, h))],
        out_specs=[pl.BlockSpec((T, HEAD_PAD), lambda h, j: (0, h)),
                   pl.BlockSpec((t, HEAD_PAD), lambda h, j: (j, h)),
                   pl.BlockSpec((t, V_HEAD), lambda h, j: (j, h))],
        out_shape=[jax.ShapeDtypeStruct((T, H * HEAD_PAD), F32),
                   jax.ShapeDtypeStruct((T, H * HEAD_PAD), F32),
                   jax.ShapeDtypeStruct((T, H * V_HEAD), F32)],
        compiler_params=_params(("arbitrary", "arbitrary"), est),
    )(qh, kh, vh, cat, dcat, lse)


def _mem_fn(q, k, v):
    hd = q.shape[1] // MEM_HEADS
    outs = []
    for h in range(MEM_HEADS):
        lo, hi = h * hd, (h + 1) * hd
        s = _bdot_nt(_cols(q, lo, hi), _cols(k, lo, hi)) * hd ** -0.5
        e = jnp.exp(s - lax.stop_gradient(jnp.max(s, axis=1, keepdims=True)))
        p = e / jnp.sum(e, axis=1, keepdims=True)
        outs.append(_bdot_nn(p, _cols(v, lo, hi)))
    return jnp.concatenate(outs, axis=1)


def _mem_fwd(q, k, v, *, name):
    T, D = q.shape
    tm = _tile(T, 256, 16)

    def fn(i, tv, pv):
        return (_mem_fn(tv[0], pv[0], pv[1]),), ()

    return _rowwise(fn, [q], [k, v], [(D, BF16)], tm=tm, name=name)[0]


def _mem_bwd(q, k, v, do, *, name):
    T, D = q.shape
    tm = _tile(T, 256, 16)

    def fn(i, tv, pv):
        _, vjp = jax.vjp(_mem_fn, tv[0], pv[0], pv[1])
        dq, dk, dv = vjp(tv[1].astype(F32))
        return (dq,), (dk, dv)

    return _rowwise(fn, [q, do], [k, v], [(D, BF16)], [(k.shape, F32), (v.shape, F32)], tm=tm, name=name)


def _loss_head(y, target, *, name):
    T, D = y.shape
    tm = _tile(T, 256, 16)

    def fn(i, tv, pv):
        err = tv[0] - tv[1]
        part = 0.5 * jnp.sum(jnp.sum(err * err, axis=1, keepdims=True) / D, axis=0, keepdims=True)
        return (err / D,), (jnp.broadcast_to(part, (8, LANE)),)

    return _rowwise(fn, [y, target], [], [(D, F32)], [((8, LANE), F32)], tm=tm, name=name)


def _adamw(w, g, m, v, *, name):
    shape = w.shape
    C = shape[-1]
    R = math.prod(shape[:-1])
    tr = _tile(R, 512, 8)
    b1c = 1.0 - ADAM_B1 ** ADAM_STEP
    b2c = 1.0 - ADAM_B2 ** ADAM_STEP

    def body(w_ref, g_ref, m_ref, v_ref, d_ref, mo_ref, vo_ref):
        gg = g_ref[...]
        mn = ADAM_B1 * m_ref[...] + (1.0 - ADAM_B1) * gg
        vn = ADAM_B2 * v_ref[...] + (1.0 - ADAM_B2) * (gg * gg)
        d_ref[...] = -ADAM_LR * ((mn / b1c) / (jnp.sqrt(vn / b2c) + ADAM_EPS) + ADAM_WD * w_ref[...])
        mo_ref[...] = mn
        vo_ref[...] = vn

    spec = pl.BlockSpec((tr, C), lambda i: (i, 0))
    outs = pl.pallas_call(
        body, name=name, grid=(R // tr,),
        in_specs=[spec] * 4, out_specs=[spec] * 3,
        out_shape=[jax.ShapeDtypeStruct((R, C), F32)] * 3,
        compiler_params=_params(("parallel",), 7 * tr * C * 4),
    )(*[a.reshape(R, C) for a in (w, g, m, v)])
    return [o.reshape(shape) for o in outs]


def _pair_sum(core, g, landed, off, tr, *, name):
    rows = g.shape[0] // N_DEV
    C = g.shape[1]
    per = rows // tr

    def body(core_ref, g_ref, l_ref, o_ref):
        o_ref[...] = (g_ref[...].astype(F32) + l_ref[...].astype(F32)).astype(o_ref.dtype)

    slab = pl.BlockSpec((None, tr, C), lambda p, i, core_ref: (p, off // tr + i, 0))
    return pl.pallas_call(
        body, name=name,
        grid_spec=pltpu.PrefetchScalarGridSpec(
            num_scalar_prefetch=1, grid=(4, per),
            in_specs=[pl.BlockSpec((tr, C), lambda p, i, core_ref: ((2 * p + core_ref[0]) * per + i, 0)), slab],
            out_specs=slab),
        out_shape=jax.ShapeDtypeStruct(landed.shape, landed.dtype),
        input_output_aliases={2: 0},
        compiler_params=_params(("arbitrary", "arbitrary"), 4 * tr * C * 4),
    )(core, g, landed)


def _quad_sum(chip, part, gathered, used, *, name):
    C = part.shape[2]
    R = used
    tr = _tile(math.gcd(used, part.shape[1]), 256, 16)

    def body(chip_ref, own_ref, a_ref, b_ref, c_ref, o_ref):
        o_ref[...] = ((own_ref[...].astype(F32) + a_ref[...].astype(F32)) + b_ref[...].astype(F32)) \
            + c_ref[...].astype(F32)

    def other(k):
        return pl.BlockSpec((None, tr, C), lambda i, chip_ref: (chip_ref[0] ^ k, i, 0))

    return pl.pallas_call(
        body, name=name,
        grid_spec=pltpu.PrefetchScalarGridSpec(
            num_scalar_prefetch=1, grid=(R // tr,),
            in_specs=[pl.BlockSpec((None, tr, C), lambda i, chip_ref: (chip_ref[0], i, 0)),
                      other(1), other(2), other(3)],
            out_specs=pl.BlockSpec((tr, C), lambda i, chip_ref: (i, 0))),
        out_shape=jax.ShapeDtypeStruct((R, C), F32),
        compiler_params=_params(("arbitrary",), 8 * tr * C * 4),
    )(chip, part, gathered, gathered, gathered)


def _place():
    x, y, c = lax.axis_index("x"), lax.axis_index("y"), lax.axis_index("c")
    return x, y, c


ANY = pl.BlockSpec(memory_space=pl.ANY)


def _all_gather(shards, *, name):
    n = len(shards)

    def body(*refs):
        ins, outs = refs[:n], refs[n:2 * n]
        send_sems, recv_sems, local_sems = refs[2 * n:]
        x, y, c = _place()
        me, sib = (x, y, c), (x, y, 1 - c)
        chips = [(1 - x, y), (x, 1 - y), (1 - x, 1 - y)]

        def rows(w, p):
            r = shards[w].shape[1]
            idx = 4 * p[0] + 2 * p[1] + p[2]
            return outs[w].at[:, pl.ds(pl.multiple_of(idx * r, 8), r), :]

        def copy(w, k, block, to, src=None):
            return pltpu.make_async_remote_copy(
                src_ref=rows(w, block) if src is None else src, dst_ref=rows(w, block),
                send_sem=send_sems.at[w * 7 + k], recv_sem=recv_sems.at[w * 7 + k],
                device_id=to, device_id_type=MESH)

        mine = [pltpu.make_async_copy(ins[w], rows(w, me), local_sems.at[w]) for w in range(n)]
        for cp in mine:
            cp.start()
        first = []
        for w in range(n):
            first.append(copy(w, 0, me, sib, src=ins[w]))
            for j, chip in enumerate(chips):
                first.append(copy(w, 1 + j, me, (*chip, c), src=ins[w]))
        for cp in first:
            cp.start()
        passed = []
        for j, chip in enumerate(chips):
            for w in range(n):
                copy(w, 1 + j, (*chip, c), me).wait_recv()
                fwd = copy(w, 4 + j, (*chip, c), sib)
                fwd.start()
                passed.append(fwd)
        for w in range(n):
            copy(w, 0, sib, me).wait_recv()
            for j, chip in enumerate(chips):
                copy(w, 4 + j, (*chip, 1 - c), me).wait_recv()
        for cp in first + passed:
            cp.wait_send()
        for cp in mine:
            cp.wait()

    return pl.pallas_call(
        body, name=name,
        in_specs=[ANY] * n, out_specs=[ANY] * n,
        out_shape=[jax.ShapeDtypeStruct((s.shape[0], N_DEV * s.shape[1], s.shape[2]), s.dtype) for s in shards],
        scratch_shapes=[pltpu.SemaphoreType.DMA((7 * n,)), pltpu.SemaphoreType.DMA((7 * n,)),
                        pltpu.SemaphoreType.DMA((n,))],
    )(*shards)


def _class_layout(grads, classes):
    used = [0] * len(set(classes))
    offs, tiles = [], []
    for g, cl in zip(grads, classes):
        rows = g.shape[0] // N_DEV
        offs.append(used[cl])
        tiles.append(math.gcd(rows, used[cl]) if used[cl] else rows)
        used[cl] += rows
    heights = []
    for k, u in enumerate(used):
        step = math.lcm(*[t for t, cl in zip(tiles, classes) if cl == k])
        heights.append(-(-u // step) * step)
    return offs, tiles, used, heights


def _rs_to_sibling(grads, classes, *, name):
    n = len(grads)
    offs, _, used, heights = _class_layout(grads, classes)
    ncl = len(heights)
    cols = [next(g.shape[1] for g, cl in zip(grads, classes) if cl == k) for k in range(ncl)]

    def body(*refs):
        gs, land = refs[:n], refs[n:n + ncl]
        send_sems, recv_sems = refs[n + ncl:]
        x, y, c = _place()
        sib = (x, y, 1 - c)
        for p in range(4):
            for w in range(n):
                r = grads[w].shape[0] // N_DEV
                cl = classes[w]
                there = gs[w].at[pl.ds(pl.multiple_of((2 * p + 1 - c) * r, 8), r), :]
                pltpu.make_async_remote_copy(
                    src_ref=there, dst_ref=land[cl].at[p, pl.ds(offs[w], r), :],
                    send_sem=send_sems.at[cl * 4 + p], recv_sem=recv_sems.at[cl * 4 + p],
                    device_id=sib, device_id_type=MESH).start()
        for cl in range(ncl):
            for p in range(4):
                rows_used = land[cl].at[p, pl.ds(0, used[cl]), :]
                slab = pltpu.make_async_remote_copy(
                    src_ref=rows_used, dst_ref=rows_used,
                    send_sem=send_sems.at[cl * 4 + p], recv_sem=recv_sems.at[cl * 4 + p],
                    device_id=sib, device_id_type=MESH)
                slab.wait_send()
                slab.wait_recv()

    return pl.pallas_call(
        body, name=name,
        in_specs=[ANY] * n, out_specs=[ANY] * ncl,
        out_shape=[jax.ShapeDtypeStruct((4, heights[k], cols[k]), BF16) for k in range(ncl)],
        scratch_shapes=[pltpu.SemaphoreType.DMA((4 * ncl,))] * 2,
    )(*grads)


def _rs_to_chips(parts, used, *, name):
    ncl = len(parts)

    def body(*refs):
        ins, outs = refs[:ncl], refs[ncl:2 * ncl]
        send_sems, recv_sems = refs[2 * ncl:]
        x, y, c = _place()
        chips = [(1 - x, y), (x, 1 - y), (1 - x, 1 - y)]
        here = 2 * x + y
        sent = []
        for k in range(ncl):
            rows = pl.ds(0, used[k])
            for j, (cx, cy) in enumerate(chips):
                cp = pltpu.make_async_remote_copy(
                    src_ref=ins[k].at[2 * cx + cy, rows, :], dst_ref=outs[k].at[here, rows, :],
                    send_sem=send_sems.at[3 * k + j], recv_sem=recv_sems.at[3 * k + j],
                    device_id=(cx, cy, c), device_id_type=MESH)
                cp.start()
                sent.append(cp)
        for k in range(ncl):
            rows = pl.ds(0, used[k])
            for j, (cx, cy) in enumerate(chips):
                pltpu.make_async_remote_copy(
                    src_ref=ins[k].at[here, rows, :], dst_ref=outs[k].at[2 * cx + cy, rows, :],
                    send_sem=send_sems.at[3 * k + j], recv_sem=recv_sems.at[3 * k + j],
                    device_id=(cx, cy, c), device_id_type=MESH).wait_recv()
        for cp in sent:
            cp.wait_send()

    return pl.pallas_call(
        body, name=name,
        in_specs=[ANY] * ncl, out_specs=[ANY] * ncl,
        out_shape=[jax.ShapeDtypeStruct(p.shape, p.dtype) for p in parts],
        scratch_shapes=[pltpu.SemaphoreType.DMA((3 * ncl,)), pltpu.SemaphoreType.DMA((3 * ncl,))],
    )(*parts)


def _all_reduce_small(v, *, name):
    R = v.shape[0]

    def body(v_ref, o_ref, buf, send_sems, recv_sems):
        x, y, c = _place()
        me = 4 * x + 2 * y + c
        buf[me] = v_ref[...]
        copies = []
        for k in range(1, N_DEV):
            fx, fy, fc = (k >> 2) & 1, (k >> 1) & 1, k & 1
            to = (x ^ fx, y ^ fy, c ^ fc)
            cp = pltpu.make_async_remote_copy(
                src_ref=v_ref, dst_ref=buf.at[me],
                send_sem=send_sems.at[k - 1], recv_sem=recv_sems.at[k - 1],
                device_id=to, device_id_type=MESH)
            cp.start()
            copies.append(cp)
        for k in range(1, N_DEV):
            fx, fy, fc = (k >> 2) & 1, (k >> 1) & 1, k & 1
            frm = 4 * (x ^ fx) + 2 * (y ^ fy) + (c ^ fc)
            pltpu.make_async_remote_copy(
                src_ref=v_ref, dst_ref=buf.at[frm],
                send_sem=send_sems.at[k - 1], recv_sem=recv_sems.at[k - 1],
                device_id=(x ^ fx, y ^ fy, c ^ fc), device_id_type=MESH).wait_recv()
        for cp in copies:
            cp.wait_send()
        acc = buf[0]
        for d in range(1, N_DEV):
            acc = acc + buf[d]
        o_ref[...] = acc

    vm = pl.BlockSpec(memory_space=pltpu.VMEM)
    return pl.pallas_call(
        body, name=name, in_specs=[vm], out_specs=vm,
        out_shape=jax.ShapeDtypeStruct((R, LANE), F32),
        scratch_shapes=[pltpu.VMEM((N_DEV, R, LANE), F32),
                        pltpu.SemaphoreType.DMA((N_DEV - 1,)), pltpu.SemaphoreType.DMA((N_DEV - 1,))],
        compiler_params=pltpu.CompilerParams(vmem_limit_bytes=VMEM_FLOOR),
    )(v)


def _rope_tables(positions):
    half = QK_ROPE // 2
    inv_freq = ROPE_BASE ** (-jnp.arange(half, dtype=F32) / half)
    ang = positions.astype(F32)[:, None] * inv_freq
    cos, sin = jnp.cos(ang), jnp.sin(ang)
    z = jnp.zeros_like(cos)
    z2 = jnp.zeros((positions.shape[0], LANE - QK_ROPE), F32)
    rc = jnp.concatenate([cos, cos, z2], axis=1)
    rs1 = jnp.concatenate([-sin, z, z2], axis=1)
    rs2 = jnp.concatenate([z, sin, z2], axis=1)
    return rc, rs1, rs2


def _block_diag(pool_w):
    G, pg, _ = pool_w.shape
    out = jnp.zeros((G * pg, G * pg), pool_w.dtype)
    for g in range(G):
        out = lax.dynamic_update_slice(out, pool_w[g], (g * pg, g * pg))
    return out


def kernel(x, mem, positions, ln_g, ln_b, ffn1_w13, ffn1_w2, w_in, pool_w, pool_scale, q_norm_g, w_uq, kv_norm_g, w_ukv, w_out, mem_wq, mem_wkv, mem_wo, ffn2_w13, ffn2_w2, loss_target, m_ln_g, m_ln_b, m_ffn1_w13, m_ffn1_w2, m_w_in, m_pool_w, m_pool_scale, m_q_norm_g, m_w_uq, m_kv_norm_g, m_w_ukv, m_w_out, m_mem_wq, m_mem_wkv, m_mem_wo, m_ffn2_w13, m_ffn2_w2, v_ln_g, v_ln_b, v_ffn1_w13, v_ffn1_w2, v_w_in, v_pool_w, v_pool_scale, v_q_norm_g, v_w_uq, v_kv_norm_g, v_w_ukv, v_w_out, v_mem_wq, v_mem_wkv, v_mem_wo, v_ffn2_w13, v_ffn2_w2):
    L = ln_g.shape[0]
    T, D = x.shape[1], x.shape[2]
    F = ffn1_w2.shape[1] * N_DEV
    PW = D // 4
    H = (D - PW) // V_HEAD
    DIN = w_in.shape[2]
    DINP = PW + Q_LORA + KV_LORA + LANE
    QW = QK_NOPE + QK_ROPE
    alpha = (2 * L) ** 0.25
    x2d = x.reshape(T, D)
    memb = mem.reshape(mem.shape[1], D).astype(BF16)
    target = loss_target.reshape(T, D)
    tabs = _rope_tables(positions.reshape(T))

    tr = lambda w: jnp.swapaxes(w, 1, 2)
    shards = [
        jnp.concatenate([tr(ffn1_w13), tr(ffn2_w13)], axis=0).astype(BF16),
        jnp.concatenate([ffn1_w2, ffn2_w2], axis=0).astype(BF16),
        jnp.concatenate([w_out, mem_wq, mem_wo], axis=0).astype(BF16),
        tr(mem_wkv).astype(BF16),
        jnp.pad(w_in, ((0, 0), (0, 0), (0, DINP - DIN))).astype(BF16),
        tr(w_uq).astype(BF16),
        tr(w_ukv).astype(BF16),
        jnp.concatenate([ln_g.reshape(1, 4 * L, -1), ln_b.reshape(1, 4 * L, -1)], axis=1),
    ]
    w13T, w2, wsq, wkvT, winp, wuqT, wukvT, lnp = _all_gather(shards, name="ag_weights")
    wuqT = jnp.pad(wuqT.reshape(L, H, QW, Q_LORA), ((0, 0), (0, 0), (0, HEAD_PAD - QW), (0, 0)))
    wuqT = wuqT.reshape(L, H * HEAD_PAD, Q_LORA)
    lnp = jnp.moveaxis(lnp.reshape(N_DEV, 2, L, 4, D // N_DEV), 0, 3).reshape(2, L, 4, D)
    lng, lnb = lnp[0], lnp[1]
    wbd = [_block_diag(pool_w[l]).astype(BF16) for l in range(L)]

    def ffn_fwd(l, which, xres, xb, k):
        idx = which * L + l
        h13 = _mm(xb, w13T, lead=idx, tb=True, out_dtype=BF16, name=f"l{l}_ffn{which}_h13")
        a = _swiglu_fwd(h13, name=f"l{l}_ffn{which}_act")
        y = _mm(a, w2, lead=idx, name=f"l{l}_ffn{which}_y", tk=F)
        xo, xob = _ln_fwd(xres, y, lng[l, k:k + 1], lnb[l, k:k + 1], alpha=alpha, s=0.5, name=f"l{l}_ln{k}")
        return dict(xres=xres, xb=xb, h13=h13, a=a, y=y), xo, xob

    saved = []
    xres, xb = x2d, x2d.astype(BF16)
    for l in range(L):
        sv = {}
        sv["ffn1"], x1, x1b = ffn_fwd(l, 0, xres, xb, 0)
        hin = _mm(x1b, winp, lead=l, name=f"l{l}_hin")
        pscale = pool_scale[l].reshape(1, PW)
        gq, gkv = q_norm_g[l].reshape(1, Q_LORA), kv_norm_g[l].reshape(1, KV_LORA)
        ypool = _pool_fwd(hin, wbd[l], pscale, name=f"l{l}_pool")
        cqn, ckvn, kpe = _norms_fwd(hin, gq, gkv, pw=PW, name=f"l{l}_norms")
        qraw = _mm(cqn, wuqT, lead=l, tb=True, name=f"l{l}_qraw")
        kv = _mm(ckvn, wukvT, lead=l, tb=True, name=f"l{l}_kv")
        qh, kh, vh = _heads_fwd(qraw, kv, kpe, tabs, H=H, name=f"l{l}_heads")
        o, lse = _flash_fwd(qh, kh, vh, H=H, name=f"l{l}_flash")
        cat = jnp.concatenate([ypool, o], axis=1)
        ymix = _mm(cat, wsq, lead=l, name=f"l{l}_ymix")
        x2, x2b = _ln_fwd(x1, ymix, lng[l, 1:2], lnb[l, 1:2], alpha=alpha, s=1.0, name=f"l{l}_ln1")
        qm = _mm(x2b, wsq, lead=L + l, out_dtype=BF16, name=f"l{l}_qm")
        kvm = _mm(memb, wkvT, lead=l, tb=True, name=f"l{l}_kvm")
        km, vm = kvm[:, :D], kvm[:, D:]
        om = _mem_fwd(qm, km, vm, name=f"l{l}_memattn")
        ymem = _mm(om, wsq, lead=2 * L + l, name=f"l{l}_ymem")
        x3, x3b = _ln_fwd(x2, ymem, lng[l, 2:3], lnb[l, 2:3], alpha=alpha, s=1.0, name=f"l{l}_ln2")
        sv["ffn2"], x4, x4b = ffn_fwd(l, 1, x3, x3b, 3)
        sv.update(x1=x1, x1b=x1b, hin=hin, pscale=pscale, gq=gq, gkv=gkv, cqn=cqn, ckvn=ckvn,
                  qh=qh, kh=kh, vh=vh, lse=lse, cat=cat, ymix=ymix, x2=x2, x2b=x2b, qm=qm, km=km, vm=vm,
                  om=om, ymem=ymem)
        saved.append(sv)
        xres, xb = x4, x4b

    dx, loss_blk = _loss_head(xres, target, name="loss_head")
    loss = lax.psum(loss_blk[0, 0], ("x", "y", "c"))

    gW = {}
    gS = {}

    def ffn_bwd(l, which, sv, dx, k):
        idx = which * L + l
        tag = f"l{l}_ffn{which}"
        dxres, dyb, dg, db = _ln_bwd(sv["xres"], sv["y"], lng[l, k:k + 1], lnb[l, k:k + 1], dx,
                                     alpha=alpha, s=0.5, name=f"l{l}_ln{k}_bwd")
        da = _mm(dyb, w2, lead=idx, tb=True, out_dtype=BF16, name=f"{tag}_da")
        gW[("w2", which, l)] = _mm(sv["a"], dyb, ta=True, out_dtype=BF16, name=f"{tag}_dw2", tn=D)
        dh = _swiglu_bwd(sv["h13"], da, name=f"{tag}_act_bwd")
        dxn = _mm(dh, w13T, lead=idx, add=dxres, name=f"{tag}_dx", tn=D)
        gW[("w13", which, l)] = _mm(dh, sv["xb"], ta=True, out_dtype=BF16, name=f"{tag}_dw13", tn=D)
        gS[("ln_g", l, k)], gS[("ln_b", l, k)] = dg, db
        return dxn

    for l in reversed(range(L)):
        sv = saved[l]
        dx = ffn_bwd(l, 1, sv["ffn2"], dx, 3)
        dxres, dyb, dg, db = _ln_bwd(sv["x2"], sv["ymem"], lng[l, 2:3], lnb[l, 2:3], dx,
                                     alpha=alpha, s=1.0, name=f"l{l}_ln2_bwd")
        gS[("ln_g", l, 2)], gS[("ln_b", l, 2)] = dg, db
        dom = _mm(dyb, wsq, lead=2 * L + l, tb=True, out_dtype=BF16, name=f"l{l}_dom")
        gW[("mem_wo", l)] = _mm(sv["om"], dyb, ta=True, out_dtype=BF16, name=f"l{l}_dwo", tn=D)
        dqm, dkm, dvm = _mem_bwd(sv["qm"], sv["km"], sv["vm"], dom, name=f"l{l}_memattn_bwd")
        dx = _mm(dqm, wsq, lead=L + l, tb=True, add=dxres, name=f"l{l}_dx2", tn=D)
        gW[("mem_wq", l)] = _mm(sv["x2b"], dqm, ta=True, out_dtype=BF16, name=f"l{l}_dwq", tn=D)
        dkvm = jnp.concatenate([dkm, dvm], axis=1).astype(BF16)
        gW[("mem_wkv", l)] = _mm(dkvm, memb, ta=True, out_dtype=BF16, name=f"l{l}_dwkv", tn=D)
        dxres, dyb, dg, db = _ln_bwd(sv["x1"], sv["ymix"], lng[l, 1:2], lnb[l, 1:2], dx,
                                     alpha=alpha, s=1.0, name=f"l{l}_ln1_bwd")
        gS[("ln_g", l, 1)], gS[("ln_b", l, 1)] = dg, db
        dcat = _mm(dyb, wsq, lead=l, tb=True, name=f"l{l}_dcat", tn=D)
        gW[("w_out", l)] = _mm(sv["cat"], dyb, ta=True, out_dtype=BF16, name=f"l{l}_dwout", tn=D)
        dqh, dkh, dvh = _flash_bwd(sv["qh"], sv["kh"], sv["vh"], sv["cat"], dcat, sv["lse"], H=H, pw=PW,
                                   name=f"l{l}_flash_bwd")
        dqraw, dkv, dkpe = _heads_bwd(dqh, dkh, dvh, tabs, H=H, name=f"l{l}_heads_bwd")
        dcq = _mm(dqraw, wuqT, lead=l, name=f"l{l}_dcq")
        gW[("w_uq", l)] = _mm(dqraw, sv["cqn"], ta=True, out_dtype=BF16, name=f"l{l}_dwuq")
        dckv = _mm(dkv, wukvT, lead=l, name=f"l{l}_dckv")
        gW[("w_ukv", l)] = _mm(dkv, sv["ckvn"], ta=True, out_dtype=BF16, name=f"l{l}_dwukv")
        du, dwbd, dps = _pool_bwd(sv["hin"], dcat, wbd[l], sv["pscale"], name=f"l{l}_pool_bwd")
        dhin, dgq, dgkv = _norms_bwd(sv["hin"], sv["gq"], sv["gkv"], dcq, dckv, dkpe, du, pw=PW,
                                     name=f"l{l}_norms_bwd")
        pg = PW // len(POOL_WINDOWS)
        gS[("pool_w", l)] = jnp.stack([dwbd[g * pg:(g + 1) * pg, g * pg:(g + 1) * pg]
                                       for g in range(len(POOL_WINDOWS))])
        gS[("pool_scale", l)], gS[("q_norm_g", l)], gS[("kv_norm_g", l)] = dps, dgq, dgkv
        dx = _mm(dhin, winp, lead=l, tb=True, add=dxres, name=f"l{l}_dx1", tn=D)
        gW[("w_in", l)] = _mm(sv["x1b"], dhin, ta=True, out_dtype=BF16, name=f"l{l}_dwin", tn=DINP)
        dx = ffn_bwd(l, 0, sv["ffn1"], dx, 0)
    grad_x = dx.reshape(1, T, D)

    rs_list, rs_class = [], []
    for keys, cl in (([("w13", f, l) for f in range(2) for l in range(L)], 0),
                     ([("w2", f, l) for f in range(2) for l in range(L)], 0),
                     ([("mem_wkv", l) for l in range(L)], 0),
                     ([(nme, l) for nme in ("w_out", "mem_wq", "mem_wo") for l in range(L)], 0),
                     ([("w_in", l) for l in range(L)], 1),
                     ([("w_uq", l) for l in range(L)], 2),
                     ([("w_ukv", l) for l in range(L)], 3)):
        for key in keys:
            g = gW[key]
            if key[0] == "w_uq":
                g = g.reshape(H, HEAD_PAD, Q_LORA)[:, :QW, :].reshape(H * QW, Q_LORA)
            rs_list.append((key, g))
            rs_class.append(cl)
    garrs = [g for _, g in rs_list]
    offs, tiles, used, _ = _class_layout(garrs, rs_class)
    core = lax.axis_index("c").astype(jnp.int32).reshape(1)
    chip = (2 * lax.axis_index("x") + lax.axis_index("y")).astype(jnp.int32).reshape(1)
    parts = list(_rs_to_sibling(garrs, rs_class, name="rs_sibling"))
    for w, (g, cl, off, tr) in enumerate(zip(garrs, rs_class, offs, tiles)):
        parts[cl] = _pair_sum(core, g, parts[cl], off, tr, name=f"rs_pair_sum_{w}")
    gathered = _rs_to_chips(parts, used, name="rs_chips")
    sums = [_quad_sum(chip, p, a, u, name=f"rs_quad_sum{k}")
            for k, (p, a, u) in enumerate(zip(parts, gathered, used))]
    gsh = {}
    for (key, g), cl, off in zip(rs_list, rs_class, offs):
        gsh[key] = sums[cl][off:off + g.shape[0] // N_DEV, :]

    small_keys = []
    for l in range(L):
        small_keys += [("pool_w", l), ("pool_scale", l), ("q_norm_g", l), ("kv_norm_g", l)]
        small_keys += [("ln_g", l, k) for k in range(4)] + [("ln_b", l, k) for k in range(4)]
    flat = jnp.concatenate([gS[k].reshape(-1) for k in small_keys])
    n_small = flat.shape[0]
    rows = -(-n_small // (8 * LANE)) * 8
    flat = jnp.pad(flat, (0, rows * LANE - n_small)).reshape(rows, LANE)
    red = _all_reduce_small(flat, name="ar_small").reshape(-1)
    gsm, pos = {}, 0
    for k in small_keys:
        size = math.prod(gS[k].shape)
        gsm[k] = red[pos:pos + size].reshape(gS[k].shape)
        pos += size

    me = 4 * lax.axis_index("x") + 2 * lax.axis_index("y") + lax.axis_index("c")
    dsh = D // N_DEV
    stack = lambda f: jnp.stack([f(l) for l in range(L)])
    g_ln_g = stack(lambda l: jnp.concatenate([gsm[("ln_g", l, k)] for k in range(4)], axis=0))
    g_ln_b = stack(lambda l: jnp.concatenate([gsm[("ln_b", l, k)] for k in range(4)], axis=0))
    grads = {
        "ln_g": lax.dynamic_slice_in_dim(g_ln_g, me * dsh, dsh, axis=2),
        "ln_b": lax.dynamic_slice_in_dim(g_ln_b, me * dsh, dsh, axis=2),
        "ffn1_w13": stack(lambda l: gsh[("w13", 0, l)].T),
        "ffn1_w2": stack(lambda l: gsh[("w2", 0, l)]),
        "w_in": stack(lambda l: gsh[("w_in", l)][:, :DIN]),
        "pool_w": stack(lambda l: gsm[("pool_w", l)]),
        "pool_scale": stack(lambda l: gsm[("pool_scale", l)].reshape(PW)),
        "q_norm_g": stack(lambda l: gsm[("q_norm_g", l)].reshape(Q_LORA)),
        "w_uq": stack(lambda l: gsh[("w_uq", l)].T),
        "kv_norm_g": stack(lambda l: gsm[("kv_norm_g", l)].reshape(KV_LORA)),
        "w_ukv": stack(lambda l: gsh[("w_ukv", l)].T),
        "w_out": stack(lambda l: gsh[("w_out", l)]),
        "mem_wq": stack(lambda l: gsh[("mem_wq", l)]),
        "mem_wkv": stack(lambda l: gsh[("mem_wkv", l)].T),
        "mem_wo": stack(lambda l: gsh[("mem_wo", l)]),
        "ffn2_w13": stack(lambda l: gsh[("w13", 1, l)].T),
        "ffn2_w2": stack(lambda l: gsh[("w2", 1, l)]),
    }

    names = ["ln_g", "ln_b", "ffn1_w13", "ffn1_w2", "w_in", "pool_w", "pool_scale", "q_norm_g", "w_uq",
             "kv_norm_g", "w_ukv", "w_out", "mem_wq", "mem_wkv", "mem_wo", "ffn2_w13", "ffn2_w2"]
    weights = dict(ln_g=ln_g, ln_b=ln_b, ffn1_w13=ffn1_w13, ffn1_w2=ffn1_w2, w_in=w_in, pool_w=pool_w,
                   pool_scale=pool_scale, q_norm_g=q_norm_g, w_uq=w_uq, kv_norm_g=kv_norm_g, w_ukv=w_ukv,
                   w_out=w_out, mem_wq=mem_wq, mem_wkv=mem_wkv, mem_wo=mem_wo, ffn2_w13=ffn2_w13,
                   ffn2_w2=ffn2_w2)
    ms = dict(ln_g=m_ln_g, ln_b=m_ln_b, ffn1_w13=m_ffn1_w13, ffn1_w2=m_ffn1_w2, w_in=m_w_in, pool_w=m_pool_w,
              pool_scale=m_pool_scale, q_norm_g=m_q_norm_g, w_uq=m_w_uq, kv_norm_g=m_kv_norm_g,
              w_ukv=m_w_ukv, w_out=m_w_out, mem_wq=m_mem_wq, mem_wkv=m_mem_wkv, mem_wo=m_mem_wo,
              ffn2_w13=m_ffn2_w13, ffn2_w2=m_ffn2_w2)
    vs = dict(ln_g=v_ln_g, ln_b=v_ln_b, ffn1_w13=v_ffn1_w13, ffn1_w2=v_ffn1_w2, w_in=v_w_in, pool_w=v_pool_w,
              pool_scale=v_pool_scale, q_norm_g=v_q_norm_g, w_uq=v_w_uq, kv_norm_g=v_kv_norm_g,
              w_ukv=v_w_ukv, w_out=v_w_out, mem_wq=v_mem_wq, mem_wkv=v_mem_wkv, mem_wo=v_mem_wo,
              ffn2_w13=v_ffn2_w13, ffn2_w2=v_ffn2_w2)
    deltas, new_m, new_v = [], [], []
    for nme in names:
        d, mn, vn = _adamw(weights[nme], grads[nme], ms[nme], vs[nme], name=f"adamw_{nme}")
        deltas.append(d)
        new_m.append(mn)
        new_v.append(vn)
    return (loss, grad_x, *[grads[nme] for nme in names], *deltas, *new_m, *new_v)
```

```python
import functools
import math

import jax
import jax.numpy as jnp
from jax import lax
from jax.experimental import pallas as pl
from jax.experimental.pallas import tpu as pltpu

F32 = jnp.float32
BF16 = jnp.bfloat16
MESH = pl.DeviceIdType.MESH

CHUNK = 64
MEM_HEADS = 4
POOL_WINDOWS = (2, 4, 8, 16)
QK_NOPE = 128
QK_ROPE = 64
V_HEAD = 128
Q_LORA = 256
KV_LORA = 128
ROPE_BASE = 10000.0
LN_EPS = 1e-5
RMS_EPS = 1e-6
NEG_INF = -1e30
ADAM_LR = 0.001
ADAM_B1 = 0.9
ADAM_B2 = 0.999
ADAM_EPS = 1e-08
ADAM_WD = 0.01
ADAM_STEP = 10

N_DEV = 8
LANE = 128
HEAD_PAD = 2 * LANE
POOL_HALO = 16
VMEM_CAP = 56 * 1024 * 1024
VMEM_FLOOR = 32 * 1024 * 1024


def _tile(n, pref, mult):
    t = (min(pref, n) // mult) * mult
    while t >= mult:
        if n % t == 0:
            return t
        t -= mult
    return n


def _params(sem, est_bytes):
    limit = int(min(max(2 * est_bytes + (8 << 20), VMEM_FLOOR), VMEM_CAP))
    return pltpu.CompilerParams(dimension_semantics=sem, vmem_limit_bytes=limit)


def _nbytes(shape, dtype):
    return math.prod(shape) * jnp.dtype(dtype).itemsize


def _hbm(x):
    return pltpu.with_memory_space_constraint(x, pltpu.HBM)


def _dg(a, b, ca, cb):
    return lax.dot_general(a.astype(BF16), b.astype(BF16), (((ca,), (cb,)), ((), ())),
                           preferred_element_type=F32)


@jax.custom_vjp
def _bdot_nn(a, b):
    return _dg(a, b, 1, 0)


def _bdot_nn_fwd(a, b):
    return _dg(a, b, 1, 0), (a, b)


def _bdot_nn_bwd(res, ct):
    a, b = res
    return _dg(ct, b, 1, 1).astype(a.dtype), _dg(a, ct, 0, 0).astype(b.dtype)


_bdot_nn.defvjp(_bdot_nn_fwd, _bdot_nn_bwd)


@jax.custom_vjp
def _bdot_nt(a, b):
    return _dg(a, b, 1, 1)


def _bdot_nt_fwd(a, b):
    return _dg(a, b, 1, 1), (a, b)


def _bdot_nt_bwd(res, ct):
    a, b = res
    return _dg(ct, b, 1, 0).astype(a.dtype), _dg(ct, a, 0, 0).astype(b.dtype)


_bdot_nt.defvjp(_bdot_nt_fwd, _bdot_nt_bwd)


@functools.partial(jax.custom_vjp, nondiff_argnums=(1,))
def _lane_roll(x, shift):
    return pltpu.roll(x, shift % x.shape[1], axis=1)


def _lane_roll_fwd(x, shift):
    return _lane_roll(x, shift), None


def _lane_roll_bwd(shift, _, ct):
    return (_lane_roll(ct, -shift),)


_lane_roll.defvjp(_lane_roll_fwd, _lane_roll_bwd)


@functools.partial(jax.custom_vjp, nondiff_argnums=(1, 2))
def _cols(x, lo, hi):
    return x[:, lo:hi]


def _cols_fwd(x, lo, hi):
    return x[:, lo:hi], x.shape[1]


def _cols_bwd(lo, hi, width, ct):
    parts = []
    if lo > 0:
        parts.append(jnp.zeros((ct.shape[0], lo), ct.dtype))
    parts.append(ct)
    if hi < width:
        parts.append(jnp.zeros((ct.shape[0], width - hi), ct.dtype))
    return (jnp.concatenate(parts, axis=1) if len(parts) > 1 else ct,)


_cols.defvjp(_cols_fwd, _cols_bwd)


MM_VMEM_BUDGET = 20 * 1024 * 1024


def _mm(a, b, *, name, ta=False, tb=False, out_dtype=F32, lead=None, add=None, add_scale=1.0,
        tm=1024, tn=1024, tk=2816):
    if ta:
        K, M = a.shape
    else:
        M, K = a.shape
    bshape = b.shape[1:] if lead is not None else b.shape
    if tb:
        N, Kb = bshape
    else:
        Kb, N = bshape
    assert K == Kb, (name, a.shape, b.shape)

    def blocks(tm, tn, tk):
        tm = _tile(M, tm, LANE if ta else 16)
        tn = _tile(N, tn, LANE)
        tk = _tile(K, tk, LANE)
        nbytes = (tm * tk * a.dtype.itemsize + tk * tn * b.dtype.itemsize
                  + tm * tn * (jnp.dtype(out_dtype).itemsize + (4 if K // tk > 1 else 0)
                               + (add.dtype.itemsize if add is not None else 0)))
        return tm, tn, tk, nbytes

    tm, tn, tk, est = blocks(tm, tn, tk)
    for shrink in ("m", "k", "m", "k", "n"):
        if est <= MM_VMEM_BUDGET:
            break
        if shrink == "m":
            tm, tn, tk, est = blocks(max(tm // 2, LANE), tn, tk)
        elif shrink == "k":
            tm, tn, tk, est = blocks(tm, tn, max(tk // 2, LANE))
        else:
            tm, tn, tk, est = blocks(tm, max(tn // 2, LANE), tk)
    nk = K // tk
    ca = 0 if ta else 1
    cb = 1 if tb else 0

    def body(*refs):
        a_ref, b_ref = refs[0], refs[1]
        add_ref = refs[2] if add is not None else None
        o_ref = refs[3] if add is not None else refs[2]

        def finish(r):
            if add_ref is not None:
                r = r + add_scale * add_ref[...].astype(F32)
            o_ref[...] = r.astype(o_ref.dtype)

        if nk == 1:
            finish(_dg(a_ref[...], b_ref[...], ca, cb))
            return
        acc_ref = refs[-1]
        k = pl.program_id(2)

        @pl.when(k == 0)
        def _():
            acc_ref[...] = jnp.zeros_like(acc_ref)

        acc_ref[...] += _dg(a_ref[...], b_ref[...], ca, cb)

        @pl.when(k == nk - 1)
        def _():
            finish(acc_ref[...])

    a_blk = (tk, tm) if ta else (tm, tk)
    a_map = (lambda i, j, k: (k, i)) if ta else (lambda i, j, k: (i, k))
    b_blk = (tn, tk) if tb else (tk, tn)
    if lead is None:
        b_map = (lambda i, j, k: (j, k)) if tb else (lambda i, j, k: (k, j))
        b_spec = pl.BlockSpec(b_blk, b_map)
    else:
        b_map = (lambda i, j, k: (lead, j, k)) if tb else (lambda i, j, k: (lead, k, j))
        b_spec = pl.BlockSpec((None,) + b_blk, b_map)
    in_specs = [pl.BlockSpec(a_blk, a_map), b_spec]
    args = [a, b]
    if add is not None:
        in_specs.append(pl.BlockSpec((tm, tn), lambda i, j, k: (i, j)))
        args.append(add)
    return pl.pallas_call(
        body, name=name,
        grid=(M // tm, N // tn, nk),
        in_specs=in_specs,
        out_specs=pl.BlockSpec((tm, tn), lambda i, j, k: (i, j)),
        out_shape=jax.ShapeDtypeStruct((M, N), out_dtype),
        scratch_shapes=[pltpu.VMEM((tm, tn), F32)] if nk > 1 else [],
        compiler_params=_params(("parallel", "parallel", "arbitrary"), est + tm * tn * 4),
    )(*[_hbm(v) for v in args])


def _rowwise(fn, tiles, params, tile_outs, acc_outs=(), *, tm, name):
    tile_arrays, tile_specs = [], []
    for t in tiles:
        if isinstance(t, tuple):
            tile_arrays.append(t[0])
            tile_specs.append(t[1])
        else:
            tile_arrays.append(t)
            tile_specs.append(pl.BlockSpec((tm, t.shape[1]), lambda i: (i, 0)))
    T = tile_arrays[0].shape[0]
    nt, np_, nto, nao = len(tile_arrays), len(params), len(tile_outs), len(acc_outs)

    def body(*refs):
        i = pl.program_id(0)
        tvals = [r[...] for r in refs[:nt]]
        pvals = [r[...] for r in refs[nt:nt + np_]]
        to_refs = refs[nt + np_:nt + np_ + nto]
        ao_refs = refs[nt + np_ + nto:]
        touts, aouts = fn(i, tvals, pvals)
        for r, v in zip(to_refs, touts):
            r[...] = v.astype(r.dtype)
        if nao:
            @pl.when(i == 0)
            def _():
                for r in ao_refs:
                    r[...] = jnp.zeros_like(r)
            for r, v in zip(ao_refs, aouts):
                r[...] += v.astype(r.dtype)

    in_specs = tile_specs + [pl.BlockSpec(p.shape, lambda i: (0, 0)) for p in params]
    out_specs = [pl.BlockSpec((tm, c), lambda i: (i, 0)) for c, _ in tile_outs]
    out_specs += [pl.BlockSpec(s, lambda i: (0, 0)) for s, _ in acc_outs]
    out_shape = [jax.ShapeDtypeStruct((T, c), d) for c, d in tile_outs]
    out_shape += [jax.ShapeDtypeStruct(s, d) for s, d in acc_outs]
    width = sum(s.block_shape[-1] for s in tile_specs) + sum(c for c, _ in tile_outs)
    est = 6 * tm * width * 4 + sum(_nbytes(p.shape, F32) for p in params) * 4
    return pl.pallas_call(
        body, name=name, grid=(T // tm,),
        in_specs=in_specs, out_specs=out_specs, out_shape=out_shape,
        compiler_params=_params(("arbitrary",) if nao else ("parallel",), est),
    )(*[_hbm(v) for v in tile_arrays], *params)


def _ln_fn(alpha, s, xres, y, g, b):
    z = alpha * xres.astype(F32) + s * y.astype(F32)
    mu = jnp.mean(z, axis=-1, keepdims=True)
    zc = z - mu
    var = jnp.mean(zc * zc, axis=-1, keepdims=True)
    return zc * lax.rsqrt(var + LN_EPS) * g + b


def _ln_fwd(xres, y, g, b, *, alpha, s, name):
    T, D = xres.shape
    tm = _tile(T, 256, 16)

    def fn(i, tv, pv):
        out = _ln_fn(alpha, s, tv[0], tv[1], pv[0], pv[1])
        return (out, out), ()

    return _rowwise(fn, [xres, y], [g, b], [(D, F32), (D, BF16)], tm=tm, name=name)


def _ln_bwd(xres, y, g, b, dout, *, alpha, s, name):
    T, D = xres.shape
    tm = _tile(T, 256, 16)

    def fn(i, tv, pv):
        _, vjp = jax.vjp(functools.partial(_ln_fn, alpha, s), tv[0], tv[1], pv[0], pv[1])
        dx, dy, dg, db = vjp(tv[2].astype(F32))
        return (dx, dy), (dg, db)

    return _rowwise(fn, [xres, y, dout], [g, b], [(D, F32), (D, BF16)],
                    [((1, D), F32), ((1, D), F32)], tm=tm, name=name)


FFN_TILE = 256


def _interleave(w, axis):
    n = w.shape[axis] // (2 * FFN_TILE)
    shp = w.shape[:axis] + (2, n, FFN_TILE) + w.shape[axis + 1:]
    return jnp.swapaxes(w.reshape(shp), axis, axis + 1).reshape(w.shape)


def _deinterleave(w, axis):
    n = w.shape[axis] // (2 * FFN_TILE)
    shp = w.shape[:axis] + (n, 2, FFN_TILE) + w.shape[axis + 1:]
    return jnp.swapaxes(w.reshape(shp), axis, axis + 1).reshape(w.shape)


def _ffn_up(xb, w13t, lead, *, name):
    T, D = xb.shape
    F = w13t.shape[1] // 2
    tc = FFN_TILE
    tm = _tile(T, 1024, 16)

    def body(x_ref, w_ref, h_ref, a_ref):
        h = _dg(x_ref[...], w_ref[...], 1, 1)
        g, u = h[:, :tc], h[:, tc:]
        h_ref[...] = h.astype(h_ref.dtype)
        a_ref[...] = (g * jax.nn.sigmoid(g) * u).astype(a_ref.dtype)

    est = (tm * D + 2 * tc * D + 3 * tm * tc) * 2 + 3 * tm * tc * 4
    return pl.pallas_call(
        body, name=name, grid=(T // tm, F // tc),
        in_specs=[pl.BlockSpec((tm, D), lambda i, j: (i, 0)),
                  pl.BlockSpec((None, 2 * tc, D), lambda i, j: (lead, j, 0))],
        out_specs=[pl.BlockSpec((tm, 2 * tc), lambda i, j: (i, j)),
                   pl.BlockSpec((tm, tc), lambda i, j: (i, j))],
        out_shape=[jax.ShapeDtypeStruct((T, 2 * F), BF16), jax.ShapeDtypeStruct((T, F), BF16)],
        compiler_params=_params(("parallel", "parallel"), est),
    )(_hbm(xb), _hbm(w13t))


def _ffn_down_bwd(dyb, w2, lead, h13, *, name):
    T, D = dyb.shape
    F = w2.shape[1]
    tc = FFN_TILE
    tm = _tile(T, 1024, 16)

    def body(dy_ref, w_ref, h_ref, dh_ref):
        d = _dg(dy_ref[...], w_ref[...], 1, 1)
        h = h_ref[...].astype(F32)
        g, u = h[:, :tc], h[:, tc:]
        sig = jax.nn.sigmoid(g)
        gs = g * sig
        dh_ref[...] = jnp.concatenate([d * u * (sig + gs * (1.0 - sig)), d * gs], axis=1).astype(dh_ref.dtype)

    est = (tm * D + tc * D + 4 * tm * tc) * 2 + 6 * tm * tc * 4
    return pl.pallas_call(
        body, name=name, grid=(T // tm, F // tc),
        in_specs=[pl.BlockSpec((tm, D), lambda i, j: (i, 0)),
                  pl.BlockSpec((None, tc, D), lambda i, j: (lead, j, 0)),
                  pl.BlockSpec((tm, 2 * tc), lambda i, j: (i, j))],
        out_specs=pl.BlockSpec((tm, 2 * tc), lambda i, j: (i, j)),
        out_shape=jax.ShapeDtypeStruct((T, 2 * F), BF16),
        compiler_params=_params(("parallel", "parallel"), est),
    )(_hbm(dyb), _hbm(w2), _hbm(h13))


def _pool_select(parts, pw):
    pg = pw // len(POOL_WINDOWS)
    grp = lax.broadcasted_iota(jnp.int32, parts[0].shape, 1) // pg
    out = parts[3]
    for g in (2, 1, 0):
        out = jnp.where(grp == g, parts[g], out)
    return out


def _pool_count(t0, rows, pw):
    pg = pw // len(POOL_WINDOWS)
    grp = lax.broadcasted_iota(jnp.int32, (rows, pw), 1) // pg
    win = jnp.where(grp == 0, POOL_WINDOWS[0],
                    jnp.where(grp == 1, POOL_WINDOWS[1],
                              jnp.where(grp == 2, POOL_WINDOWS[2], POOL_WINDOWS[3])))
    t = t0 + lax.broadcasted_iota(jnp.int32, (rows, pw), 0)
    return jnp.minimum(t + 1, win).astype(F32)


def _window_sums(ext, up):
    n = ext.shape[0]
    sums, cur, k = [], ext, 1
    for _ in POOL_WINDOWS:
        cur = cur + pltpu.roll(cur, (n - k) if up else k, axis=0)
        sums.append(cur)
        k *= 2
    return sums


def _pool_delta(u, halo, t0):
    tm, pw = u.shape
    ext = jnp.concatenate([halo, u], axis=0)
    sums = [s[POOL_HALO:, :] for s in _window_sums(ext, up=False)]
    return _pool_select(sums, pw) / _pool_count(t0, tm, pw) - u


def _pool_fwd(hin, wbd, scale, *, name):
    T = hin.shape[0]
    pw = wbd.shape[0]
    tm = _tile(T, 256, POOL_HALO)
    per = tm // POOL_HALO

    def body(u_ref, halo_ref, w_ref, s_ref, y_ref):
        i = pl.program_id(0)
        halo = jnp.where(i > 0, halo_ref[...], 0.0)
        d = _pool_delta(u_ref[...], halo, i * tm)
        y_ref[...] = (_dg(d, w_ref[...], 1, 0) * s_ref[...]).astype(y_ref.dtype)

    return pl.pallas_call(
        body, name=name, grid=(T // tm,),
        in_specs=[pl.BlockSpec((tm, pw), lambda i: (i, 0)),
                  pl.BlockSpec((POOL_HALO, pw), lambda i: (jnp.maximum(i * per - 1, 0), 0)),
                  pl.BlockSpec((pw, pw), lambda i: (0, 0)),
                  pl.BlockSpec((1, pw), lambda i: (0, 0))],
        out_specs=pl.BlockSpec((tm, pw), lambda i: (i, 0)),
        out_shape=jax.ShapeDtypeStruct((T, pw), BF16),
        compiler_params=_params(("parallel",), 16 * tm * pw * 4),
    )(_hbm(hin), _hbm(hin), wbd, scale)


def _pool_bwd(hin, dcat, wbd, scale, *, name):
    T = hin.shape[0]
    pw = wbd.shape[0]
    tm = _tile(T, 256, POOL_HALO)
    per = tm // POOL_HALO
    nt = T // tm

    def body(u_ref, halo_ref, dy_ref, dyn_ref, w_ref, s_ref, du_ref, dw_ref, ds_ref):
        i = pl.program_id(0)

        @pl.when(i == 0)
        def _():
            dw_ref[...] = jnp.zeros_like(dw_ref)
            ds_ref[...] = jnp.zeros_like(ds_ref)

        halo = jnp.where(i > 0, halo_ref[...], 0.0)
        d = _pool_delta(u_ref[...], halo, i * tm)
        w = w_ref[...]
        sc = s_ref[...]
        dy = dy_ref[...]
        dyn = jnp.where(i < nt - 1, dyn_ref[...], 0.0)
        ds_ref[...] += jnp.sum(dy * _dg(d, w, 1, 0), axis=0, keepdims=True)
        dys = dy * sc
        dw_ref[...] += _dg(d, dys, 0, 0)
        dys_ext = jnp.concatenate([dys, dyn * sc], axis=0)
        dd_ext = _dg(dys_ext, w, 1, 1)
        ddp = dd_ext / _pool_count(i * tm, tm + POOL_HALO, pw)
        sums = [s[:tm, :] for s in _window_sums(ddp, up=True)]
        du_ref[...] = _pool_select(sums, pw) - dd_ext[:tm, :]

    return pl.pallas_call(
        body, name=name, grid=(nt,),
        in_specs=[pl.BlockSpec((tm, pw), lambda i: (i, 0)),
                  pl.BlockSpec((POOL_HALO, pw), lambda i: (jnp.maximum(i * per - 1, 0), 0)),
                  pl.BlockSpec((tm, pw), lambda i: (i, 0)),
                  pl.BlockSpec((POOL_HALO, pw), lambda i: (jnp.minimum((i + 1) * per, nt * per - 1), 0)),
                  pl.BlockSpec((pw, pw), lambda i: (0, 0)),
                  pl.BlockSpec((1, pw), lambda i: (0, 0))],
        out_specs=[pl.BlockSpec((tm, pw), lambda i: (i, 0)),
                   pl.BlockSpec((pw, pw), lambda i: (0, 0)),
                   pl.BlockSpec((1, pw), lambda i: (0, 0))],
        out_shape=[jax.ShapeDtypeStruct((T, pw), F32),
                   jax.ShapeDtypeStruct((pw, pw), F32),
                   jax.ShapeDtypeStruct((1, pw), F32)],
        compiler_params=_params(("arbitrary",), 24 * tm * pw * 4),
    )(_hbm(hin), _hbm(hin), _hbm(dcat), _hbm(dcat), wbd, scale)


def _rms(x, g):
    return x * lax.rsqrt(jnp.mean(x * x, axis=-1, keepdims=True) + RMS_EPS) * g


def _norms_fn(pw, h, gq, gkv):
    o1 = pw + Q_LORA
    o2 = o1 + KV_LORA
    return (_rms(_cols(h, pw, o1), gq), _rms(_cols(h, o1, o2), gkv), _cols(h, o2, h.shape[1]))


def _norms_fwd(hin, gq, gkv, *, pw, name):
    tm = _tile(hin.shape[0], 256, 16)

    def fn(i, tv, pv):
        return _norms_fn(pw, tv[0], pv[0], pv[1]), ()

    return _rowwise(fn, [hin], [gq, gkv], [(Q_LORA, BF16), (KV_LORA, BF16), (LANE, F32)], tm=tm, name=name)


def _norms_bwd(hin, gq, gkv, dcq, dckv, dkpe, du, *, pw, name):
    tm = _tile(hin.shape[0], 256, 16)
    dinp = hin.shape[1]

    def fn(i, tv, pv):
        _, vjp = jax.vjp(functools.partial(_norms_fn, pw), tv[0], pv[0], pv[1])
        dh, dgq, dgkv = vjp((tv[1].astype(F32), tv[2].astype(F32), tv[3].astype(F32)))
        dh = jnp.concatenate([tv[4], dh[:, pw:]], axis=1)
        return (dh,), (dgq, dgkv)

    return _rowwise(fn, [hin, dcq, dckv, dkpe, du], [gq, gkv], [(dinp, BF16)],
                    [((1, Q_LORA), F32), ((1, KV_LORA), F32)], tm=tm, name=name)


def _heads_fn(H, qraw, kv, kpe, rc, rs1, rs2):
    half = QK_ROPE // 2

    def rope(blk):
        return blk * rc + _lane_roll(blk, -half) * rs1 + _lane_roll(blk, half) * rs2

    krot = rope(kpe)
    qs, ks, vs = [], [], []
    for h in range(H):
        lo = h * HEAD_PAD
        qs += [_cols(qraw, lo, lo + LANE), rope(_cols(qraw, lo + LANE, lo + HEAD_PAD))]
        ks += [_cols(kv, lo, lo + LANE), krot]
        vs += [_cols(kv, lo + LANE, lo + HEAD_PAD)]
    return jnp.concatenate(qs, axis=1), jnp.concatenate(ks, axis=1), jnp.concatenate(vs, axis=1)


def _heads_fwd(qraw, kv, kpe, tabs, *, H, name):
    tm = _tile(qraw.shape[0], 256, 16)

    def fn(i, tv, pv):
        return _heads_fn(H, *tv), ()

    return _rowwise(fn, [qraw, kv, kpe, *tabs], [],
                    [(H * HEAD_PAD, BF16), (H * HEAD_PAD, BF16), (H * V_HEAD, BF16)], tm=tm, name=name)


def _heads_bwd(dq, dk, dv, tabs, *, H, name):
    tm = _tile(dq.shape[0], 256, 16)

    def fn(i, tv, pv):
        z = jnp.zeros((tm, H * HEAD_PAD), F32)
        zk = jnp.zeros((tm, LANE), F32)
        rc, rs1, rs2 = tv[3], tv[4], tv[5]
        _, vjp = jax.vjp(lambda a, b, c: _heads_fn(H, a, b, c, rc, rs1, rs2), z, z, zk)
        return vjp((tv[0], tv[1], tv[2])), ()

    return _rowwise(fn, [dq, dk, dv, *tabs], [],
                    [(H * HEAD_PAD, BF16), (H * HEAD_PAD, BF16), (LANE, F32)], tm=tm, name=name)


def _diag_mask(t):
    r = lax.broadcasted_iota(jnp.int32, (t, t), 0) // CHUNK
    c = lax.broadcasted_iota(jnp.int32, (t, t), 1) // CHUNK
    return r >= c


def _flash_fwd(qh, kh, vh, *, H, name):
    T = qh.shape[0]
    t = _tile(T, 512, CHUNK)
    scale = (QK_NOPE + QK_ROPE) ** -0.5

    def body(q_ref, k_ref, v_ref, o_ref, lse_ref):
        i = pl.program_id(1)
        q = q_ref[...]

        def blk(j, carry, masked):
            m, l, acc = carry
            rows = pl.ds(pl.multiple_of(j * t, t), t)
            s = _dg(q, k_ref[rows, :], 1, 1) * scale
            if masked:
                s = jnp.where(_diag_mask(t), s, NEG_INF)
            mn = jnp.maximum(m, jnp.max(s, axis=1, keepdims=True))
            p = jnp.exp(s - mn)
            corr = jnp.exp(m - mn)
            l = corr * l + jnp.sum(p, axis=1, keepdims=True)
            acc = corr * acc + _dg(p, v_ref[rows, :], 1, 0)
            return mn, l, acc

        init = (jnp.full((t, 1), NEG_INF, F32), jnp.zeros((t, 1), F32), jnp.zeros((t, V_HEAD), F32))
        carry = lax.fori_loop(0, i, lambda j, c: blk(j, c, False), init)
        m, l, acc = blk(i, carry, True)
        o_ref[...] = (acc / l).astype(o_ref.dtype)
        lse_ref[...] = jnp.broadcast_to(m + jnp.log(l), (t, V_HEAD))

    est = 2 * T * (HEAD_PAD + V_HEAD) * 2 + 8 * t * t * 4
    return pl.pallas_call(
        body, name=name, grid=(H, T // t),
        in_specs=[pl.BlockSpec((t, HEAD_PAD), lambda h, i: (i, h)),
                  pl.BlockSpec((T, HEAD_PAD), lambda h, i: (0, h)),
                  pl.BlockSpec((T, V_HEAD), lambda h, i: (0, h))],
        out_specs=[pl.BlockSpec((t, V_HEAD), lambda h, i: (i, h)),
                   pl.BlockSpec((t, V_HEAD), lambda h, i: (i, h))],
        out_shape=[jax.ShapeDtypeStruct((T, H * V_HEAD), BF16),
                   jax.ShapeDtypeStruct((T, H * V_HEAD), F32)],
        compiler_params=_params(("parallel", "parallel"), est),
    )(_hbm(qh), _hbm(kh), _hbm(vh))


def _flash_bwd(qh, kh, vh, cat, dcat, lse, *, H, pw, name):
    T = qh.shape[0]
    t = _tile(T, 512, CHUNK)
    nb = T // t
    off = pw // V_HEAD
    scale = (QK_NOPE + QK_ROPE) ** -0.5

    def body(q_ref, k_ref, v_ref, o_ref, do_ref, lse_ref, dq_ref, dk_ref, dv_ref):
        j = pl.program_id(1)

        @pl.when(j == 0)
        def _():
            dq_ref[...] = jnp.zeros_like(dq_ref)

        kj = k_ref[...]
        vj = v_ref[...]

        def blk(i, carry, masked):
            dk, dv = carry
            rows = pl.ds(pl.multiple_of(i * t, t), t)
            qi = q_ref[rows, :]
            doi = do_ref[rows, :]
            oi = o_ref[rows, :].astype(F32)
            lsei = lse_ref[rows, :][:, :1]
            s = _dg(qi, kj, 1, 1) * scale
            if masked:
                s = jnp.where(_diag_mask(t), s, NEG_INF)
            p = jnp.exp(s - lsei)
            dv = dv + _dg(p, doi, 0, 0)
            dp = _dg(doi, vj, 1, 1)
            di = jnp.sum(doi * oi, axis=1, keepdims=True)
            ds = p * (dp - di) * scale
            dk = dk + _dg(ds, qi, 0, 0)
            dq_ref[rows, :] += _dg(ds, kj, 1, 0)
            return dk, dv

        carry = blk(j, (jnp.zeros((t, HEAD_PAD), F32), jnp.zeros((t, V_HEAD), F32)), True)
        dk, dv = lax.fori_loop(j + 1, nb, lambda i, c: blk(i, c, False), carry)
        dk_ref[...] = dk
        dv_ref[...] = dv

    est = T * (HEAD_PAD * 2 + V_HEAD * 2 + V_HEAD * 4 + V_HEAD * 4 + HEAD_PAD * 4) + 10 * t * t * 4
    return pl.pallas_call(
        body, name=name, grid=(H, nb),
        in_specs=[pl.BlockSpec((T, HEAD_PAD), lambda h, j: (0, h)),
                  pl.BlockSpec((t, HEAD_PAD), lambda h, j: (j, h)),
                  pl.BlockSpec((t, V_HEAD), lambda h, j: (j, h)),
                  pl.BlockSpec((T, V_HEAD), lambda h, j: (0, off + h)),
                  pl.BlockSpec((T, V_HEAD), lambda h, j: (0, off + h)),
                  pl.BlockSpec((T, V_HEAD), lambda h, j: (0, h))],
        out_specs=[pl.BlockSpec((T, HEAD_PAD), lambda h, j: (0, h)),
                   pl.BlockSpec((t, HEAD_PAD), lambda h, j: (j, h)),
                   pl.BlockSpec((t, V_HEAD), lambda h, j: (j, h))],
        out_shape=[jax.ShapeDtypeStruct((T, H * HEAD_PAD), F32),
                   jax.ShapeDtypeStruct((T, H * HEAD_PAD), F32),
                   jax.ShapeDtypeStruct((T, H * V_HEAD), F32)],
        compiler_params=_params(("arbitrary", "arbitrary"), est),
    )(*[_hbm(v) for v in (qh, kh, vh, cat, dcat, lse)])


def _mem_fn(q, k, v):
    hd = q.shape[1] // MEM_HEADS
    outs = []
    for h in range(MEM_HEADS):
        lo, hi = h * hd, (h + 1) * hd
        s = _bdot_nt(_cols(q, lo, hi), _cols(k, lo, hi)) * hd ** -0.5
        e = jnp.exp(s - lax.stop_gradient(jnp.max(s, axis=1, keepdims=True)))
        p = e / jnp.sum(e, axis=1, keepdims=True)
        outs.append(_bdot_nn(p, _cols(v, lo, hi)))
    return jnp.concatenate(outs, axis=1)


def _mem_fwd(q, k, v, *, name):
    T, D = q.shape
    tm = _tile(T, 256, 16)

    def fn(i, tv, pv):
        return (_mem_fn(tv[0], pv[0], pv[1]),), ()

    return _rowwise(fn, [q], [k, v], [(D, BF16)], tm=tm, name=name)[0]


def _mem_bwd(q, k, v, do, *, name):
    T, D = q.shape
    tm = _tile(T, 256, 16)

    def fn(i, tv, pv):
        _, vjp = jax.vjp(_mem_fn, tv[0], pv[0], pv[1])
        dq, dk, dv = vjp(tv[1].astype(F32))
        return (dq,), (dk, dv)

    return _rowwise(fn, [q, do], [k, v], [(D, BF16)], [(k.shape, F32), (v.shape, F32)], tm=tm, name=name)


def _loss_head(y, target, *, name):
    T, D = y.shape
    tm = _tile(T, 256, 16)

    def fn(i, tv, pv):
        err = tv[0] - tv[1]
        part = 0.5 * jnp.sum(jnp.sum(err * err, axis=1, keepdims=True) / D, axis=0, keepdims=True)
        return (err / D,), (jnp.broadcast_to(part, (8, LANE)),)

    return _rowwise(fn, [y, target], [], [(D, F32)], [((8, LANE), F32)], tm=tm, name=name)


def _adamw(w, g, m, v, *, name):
    shape = w.shape
    C = shape[-1]
    R = math.prod(shape[:-1])
    tr = _tile(R, 512, 8)
    b1c = 1.0 - ADAM_B1 ** ADAM_STEP
    b2c = 1.0 - ADAM_B2 ** ADAM_STEP

    def body(w_ref, g_ref, m_ref, v_ref, d_ref, mo_ref, vo_ref):
        gg = g_ref[...]
        mn = ADAM_B1 * m_ref[...] + (1.0 - ADAM_B1) * gg
        vn = ADAM_B2 * v_ref[...] + (1.0 - ADAM_B2) * (gg * gg)
        d_ref[...] = -ADAM_LR * ((mn / b1c) / (jnp.sqrt(vn / b2c) + ADAM_EPS) + ADAM_WD * w_ref[...])
        mo_ref[...] = mn
        vo_ref[...] = vn

    spec = pl.BlockSpec((tr, C), lambda i: (i, 0))
    outs = pl.pallas_call(
        body, name=name, grid=(R // tr,),
        in_specs=[spec] * 4, out_specs=[spec] * 3,
        out_shape=[jax.ShapeDtypeStruct((R, C), F32)] * 3,
        compiler_params=_params(("parallel",), 7 * tr * C * 4),
    )(*[_hbm(a.reshape(R, C)) for a in (w, g, m, v)])
    return [o.reshape(shape) for o in outs]


def _pair_sum(core, g, landed, off, tr, *, name):
    rows = g.shape[0] // N_DEV
    C = g.shape[1]
    per = rows // tr

    def body(core_ref, g_ref, l_ref, o_ref):
        o_ref[...] = (g_ref[...].astype(F32) + l_ref[...].astype(F32)).astype(o_ref.dtype)

    slab = pl.BlockSpec((None, tr, C), lambda p, i, core_ref: (p, off // tr + i, 0))
    return pl.pallas_call(
        body, name=name,
        grid_spec=pltpu.PrefetchScalarGridSpec(
            num_scalar_prefetch=1, grid=(4, per),
            in_specs=[pl.BlockSpec((tr, C), lambda p, i, core_ref: ((2 * p + core_ref[0]) * per + i, 0)), slab],
            out_specs=slab),
        out_shape=jax.ShapeDtypeStruct(landed.shape, landed.dtype),
        input_output_aliases={2: 0},
        compiler_params=_params(("arbitrary", "arbitrary"), 4 * tr * C * 4),
    )(core, _hbm(g), _hbm(landed))


def _quad_sum(chip, part, gathered, used, *, name):
    C = part.shape[2]
    R = used
    tr = _tile(math.gcd(used, part.shape[1]), 256, 16)

    def body(chip_ref, own_ref, a_ref, b_ref, c_ref, o_ref):
        o_ref[...] = ((own_ref[...].astype(F32) + a_ref[...].astype(F32)) + b_ref[...].astype(F32)) \
            + c_ref[...].astype(F32)

    def other(k):
        return pl.BlockSpec((None, tr, C), lambda i, chip_ref: (chip_ref[0] ^ k, i, 0))

    return pl.pallas_call(
        body, name=name,
        grid_spec=pltpu.PrefetchScalarGridSpec(
            num_scalar_prefetch=1, grid=(R // tr,),
            in_specs=[pl.BlockSpec((None, tr, C), lambda i, chip_ref: (chip_ref[0], i, 0)),
                      other(1), other(2), other(3)],
            out_specs=pl.BlockSpec((tr, C), lambda i, chip_ref: (i, 0))),
        out_shape=jax.ShapeDtypeStruct((R, C), F32),
        compiler_params=_params(("arbitrary",), 8 * tr * C * 4),
    )(chip, _hbm(part), _hbm(gathered), _hbm(gathered), _hbm(gathered))


def _place():
    x, y, c = lax.axis_index("x"), lax.axis_index("y"), lax.axis_index("c")
    return x, y, c


ANY = pl.BlockSpec(memory_space=pl.ANY)


def _all_gather(shards, *, name):
    n = len(shards)

    def body(*refs):
        ins, outs = refs[:n], refs[n:2 * n]
        send_sems, recv_sems, local_sems = refs[2 * n:]
        x, y, c = _place()
        me, sib = (x, y, c), (x, y, 1 - c)
        chips = [(1 - x, y), (x, 1 - y), (1 - x, 1 - y)]

        def rows(w, p):
            r = shards[w].shape[1]
            idx = 4 * p[0] + 2 * p[1] + p[2]
            return outs[w].at[:, pl.ds(pl.multiple_of(idx * r, 8), r), :]

        def copy(w, k, block, to, src=None):
            return pltpu.make_async_remote_copy(
                src_ref=rows(w, block) if src is None else src, dst_ref=rows(w, block),
                send_sem=send_sems.at[w * 7 + k], recv_sem=recv_sems.at[w * 7 + k],
                device_id=to, device_id_type=MESH)

        mine = [pltpu.make_async_copy(ins[w], rows(w, me), local_sems.at[w]) for w in range(n)]
        for cp in mine:
            cp.start()
        first = []
        for w in range(n):
            first.append(copy(w, 0, me, sib, src=ins[w]))
            for j, chip in enumerate(chips):
                first.append(copy(w, 1 + j, me, (*chip, c), src=ins[w]))
        for cp in first:
            cp.start()
        passed = []
        for j, chip in enumerate(chips):
            for w in range(n):
                copy(w, 1 + j, (*chip, c), me).wait_recv()
                fwd = copy(w, 4 + j, (*chip, c), sib)
                fwd.start()
                passed.append(fwd)
        for w in range(n):
            copy(w, 0, sib, me).wait_recv()
            for j, chip in enumerate(chips):
                copy(w, 4 + j, (*chip, 1 - c), me).wait_recv()
        for cp in first + passed:
            cp.wait_send()
        for cp in mine:
            cp.wait()

    return pl.pallas_call(
        body, name=name,
        in_specs=[ANY] * n, out_specs=[ANY] * n,
        out_shape=[jax.ShapeDtypeStruct((s.shape[0], N_DEV * s.shape[1], s.shape[2]), s.dtype) for s in shards],
        scratch_shapes=[pltpu.SemaphoreType.DMA((7 * n,)), pltpu.SemaphoreType.DMA((7 * n,)),
                        pltpu.SemaphoreType.DMA((n,))],
    )(*[_hbm(s) for s in shards])


def _class_layout(grads, classes):
    used = [0] * len(set(classes))
    offs, tiles = [], []
    for g, cl in zip(grads, classes):
        rows = g.shape[0] // N_DEV
        offs.append(used[cl])
        tiles.append(math.gcd(rows, used[cl]) if used[cl] else rows)
        used[cl] += rows
    heights = []
    for k, u in enumerate(used):
        step = math.lcm(*[t for t, cl in zip(tiles, classes) if cl == k])
        heights.append(-(-u // step) * step)
    return offs, tiles, used, heights


def _rs_to_sibling(grads, classes, *, name):
    n = len(grads)
    offs, _, used, heights = _class_layout(grads, classes)
    ncl = len(heights)
    cols = [next(g.shape[1] for g, cl in zip(grads, classes) if cl == k) for k in range(ncl)]

    def body(*refs):
        gs, land = refs[:n], refs[n:n + ncl]
        send_sems, recv_sems = refs[n + ncl:]
        x, y, c = _place()
        sib = (x, y, 1 - c)
        for p in range(4):
            for w in range(n):
                r = grads[w].shape[0] // N_DEV
                cl = classes[w]
                there = gs[w].at[pl.ds(pl.multiple_of((2 * p + 1 - c) * r, 8), r), :]
                pltpu.make_async_remote_copy(
                    src_ref=there, dst_ref=land[cl].at[p, pl.ds(offs[w], r), :],
                    send_sem=send_sems.at[cl * 4 + p], recv_sem=recv_sems.at[cl * 4 + p],
                    device_id=sib, device_id_type=MESH).start()
        for cl in range(ncl):
            for p in range(4):
                rows_used = land[cl].at[p, pl.ds(0, used[cl]), :]
                slab = pltpu.make_async_remote_copy(
                    src_ref=rows_used, dst_ref=rows_used,
                    send_sem=send_sems.at[cl * 4 + p], recv_sem=recv_sems.at[cl * 4 + p],
                    device_id=sib, device_id_type=MESH)
                slab.wait_send()
                slab.wait_recv()

    return pl.pallas_call(
        body, name=name,
        in_specs=[ANY] * n, out_specs=[ANY] * ncl,
        out_shape=[jax.ShapeDtypeStruct((4, heights[k], cols[k]), BF16) for k in range(ncl)],
        scratch_shapes=[pltpu.SemaphoreType.DMA((4 * ncl,))] * 2,
    )(*[_hbm(g) for g in grads])


def _rs_to_chips(parts, used, *, name):
    ncl = len(parts)

    def body(*refs):
        ins, outs = refs[:ncl], refs[ncl:2 * ncl]
        send_sems, recv_sems = refs[2 * ncl:]
        x, y, c = _place()
        chips = [(1 - x, y), (x, 1 - y), (1 - x, 1 - y)]
        here = 2 * x + y
        sent = []
        for k in range(ncl):
            rows = pl.ds(0, used[k])
            for j, (cx, cy) in enumerate(chips):
                cp = pltpu.make_async_remote_copy(
                    src_ref=ins[k].at[2 * cx + cy, rows, :], dst_ref=outs[k].at[here, rows, :],
                    send_sem=send_sems.at[3 * k + j], recv_sem=recv_sems.at[3 * k + j],
                    device_id=(cx, cy, c), device_id_type=MESH)
                cp.start()
                sent.append(cp)
        for k in range(ncl):
            rows = pl.ds(0, used[k])
            for j, (cx, cy) in enumerate(chips):
                pltpu.make_async_remote_copy(
                    src_ref=ins[k].at[here, rows, :], dst_ref=outs[k].at[2 * cx + cy, rows, :],
                    send_sem=send_sems.at[3 * k + j], recv_sem=recv_sems.at[3 * k + j],
                    device_id=(cx, cy, c), device_id_type=MESH).wait_recv()
        for cp in sent:
            cp.wait_send()

    return pl.pallas_call(
        body, name=name,
        in_specs=[ANY] * ncl, out_specs=[ANY] * ncl,
        out_shape=[jax.ShapeDtypeStruct(p.shape, p.dtype) for p in parts],
        scratch_shapes=[pltpu.SemaphoreType.DMA((3 * ncl,)), pltpu.SemaphoreType.DMA((3 * ncl,))],
    )(*[_hbm(p) for p in parts])


def _all_reduce_small(v, *, name):
    R = v.shape[0]

    def body(v_ref, o_ref, buf, send_sems, recv_sems):
        x, y, c = _place()
        me = 4 * x + 2 * y + c
        buf[me] = v_ref[...]
        copies = []
        for k in range(1, N_DEV):
            fx, fy, fc = (k >> 2) & 1, (k >> 1) & 1, k & 1
            to = (x ^ fx, y ^ fy, c ^ fc)
            cp = pltpu.make_async_remote_copy(
                src_ref=v_ref, dst_ref=buf.at[me],
                send_sem=send_sems.at[k - 1], recv_sem=recv_sems.at[k - 1],
                device_id=to, device_id_type=MESH)
            cp.start()
            copies.append(cp)
        for k in range(1, N_DEV):
            fx, fy, fc = (k >> 2) & 1, (k >> 1) & 1, k & 1
            frm = 4 * (x ^ fx) + 2 * (y ^ fy) + (c ^ fc)
            pltpu.make_async_remote_copy(
                src_ref=v_ref, dst_ref=buf.at[frm],
                send_sem=send_sems.at[k - 1], recv_sem=recv_sems.at[k - 1],
                device_id=(x ^ fx, y ^ fy, c ^ fc), device_id_type=MESH).wait_recv()
        for cp in copies:
            cp.wait_send()
        acc = buf[0]
        for d in range(1, N_DEV):
            acc = acc + buf[d]
        o_ref[...] = acc

    vm = pl.BlockSpec(memory_space=pltpu.VMEM)
    return pl.pallas_call(
        body, name=name, in_specs=[vm], out_specs=vm,
        out_shape=jax.ShapeDtypeStruct((R, LANE), F32),
        scratch_shapes=[pltpu.VMEM((N_DEV, R, LANE), F32),
                        pltpu.SemaphoreType.DMA((N_DEV - 1,)), pltpu.SemaphoreType.DMA((N_DEV - 1,))],
        compiler_params=pltpu.CompilerParams(vmem_limit_bytes=VMEM_FLOOR),
    )(v)


def _rope_tables(positions):
    half = QK_ROPE // 2
    inv_freq = ROPE_BASE ** (-jnp.arange(half, dtype=F32) / half)
    ang = positions.astype(F32)[:, None] * inv_freq
    cos, sin = jnp.cos(ang), jnp.sin(ang)
    z = jnp.zeros_like(cos)
    z2 = jnp.zeros((positions.shape[0], LANE - QK_ROPE), F32)
    rc = jnp.concatenate([cos, cos, z2], axis=1)
    rs1 = jnp.concatenate([-sin, z, z2], axis=1)
    rs2 = jnp.concatenate([z, sin, z2], axis=1)
    return rc, rs1, rs2


def _block_diag(pool_w):
    G, pg, _ = pool_w.shape
    out = jnp.zeros((G * pg, G * pg), pool_w.dtype)
    for g in range(G):
        out = lax.dynamic_update_slice(out, pool_w[g], (g * pg, g * pg))
    return out


def kernel(x, mem, positions, ln_g, ln_b, ffn1_w13, ffn1_w2, w_in, pool_w, pool_scale, q_norm_g, w_uq, kv_norm_g, w_ukv, w_out, mem_wq, mem_wkv, mem_wo, ffn2_w13, ffn2_w2, loss_target, m_ln_g, m_ln_b, m_ffn1_w13, m_ffn1_w2, m_w_in, m_pool_w, m_pool_scale, m_q_norm_g, m_w_uq, m_kv_norm_g, m_w_ukv, m_w_out, m_mem_wq, m_mem_wkv, m_mem_wo, m_ffn2_w13, m_ffn2_w2, v_ln_g, v_ln_b, v_ffn1_w13, v_ffn1_w2, v_w_in, v_pool_w, v_pool_scale, v_q_norm_g, v_w_uq, v_kv_norm_g, v_w_ukv, v_w_out, v_mem_wq, v_mem_wkv, v_mem_wo, v_ffn2_w13, v_ffn2_w2):
    L = ln_g.shape[0]
    T, D = x.shape[1], x.shape[2]
    F = ffn1_w2.shape[1] * N_DEV
    PW = D // 4
    H = (D - PW) // V_HEAD
    DIN = w_in.shape[2]
    DINP = PW + Q_LORA + KV_LORA + LANE
    QW = QK_NOPE + QK_ROPE
    alpha = (2 * L) ** 0.25
    x2d = x.reshape(T, D)
    memb = mem.reshape(mem.shape[1], D).astype(BF16)
    target = loss_target.reshape(T, D)
    tabs = _rope_tables(positions.reshape(T))

    tr = lambda w: jnp.swapaxes(w, 1, 2)
    shards = [
        jnp.concatenate([tr(ffn1_w13), tr(ffn2_w13)], axis=0).astype(BF16),
        jnp.concatenate([ffn1_w2, ffn2_w2], axis=0).astype(BF16),
        jnp.concatenate([w_out, mem_wq, mem_wo], axis=0).astype(BF16),
        tr(mem_wkv).astype(BF16),
        jnp.pad(w_in, ((0, 0), (0, 0), (0, DINP - DIN))).astype(BF16),
        tr(w_uq).astype(BF16),
        tr(w_ukv).astype(BF16),
        jnp.concatenate([ln_g.reshape(1, 4 * L, -1), ln_b.reshape(1, 4 * L, -1)], axis=1),
    ]
    w13T, w2, wsq, wkvT, winp, wuqT, wukvT, lnp = _all_gather(shards, name="ag_weights")
    w13T = _interleave(w13T, 1)
    wuqT = jnp.pad(wuqT.reshape(L, H, QW, Q_LORA), ((0, 0), (0, 0), (0, HEAD_PAD - QW), (0, 0)))
    wuqT = wuqT.reshape(L, H * HEAD_PAD, Q_LORA)
    lnp = jnp.moveaxis(lnp.reshape(N_DEV, 2, L, 4, D // N_DEV), 0, 3).reshape(2, L, 4, D)
    lng, lnb = lnp[0], lnp[1]
    wbd = [_block_diag(pool_w[l]).astype(BF16) for l in range(L)]

    def ffn_fwd(l, which, xres, xb, k):
        idx = which * L + l
        h13, a = _ffn_up(xb, w13T, idx, name=f"l{l}_ffn{which}_up")
        y = _mm(a, w2, lead=idx, name=f"l{l}_ffn{which}_y", tk=F)
        xo, xob = _ln_fwd(xres, y, lng[l, k:k + 1], lnb[l, k:k + 1], alpha=alpha, s=0.5, name=f"l{l}_ln{k}")
        return dict(xres=xres, xb=xb, h13=h13, a=a, y=y), xo, xob

    saved = []
    xres, xb = x2d, x2d.astype(BF16)
    for l in range(L):
        sv = {}
        sv["ffn1"], x1, x1b = ffn_fwd(l, 0, xres, xb, 0)
        hin = _mm(x1b, winp, lead=l, name=f"l{l}_hin")
        pscale = pool_scale[l].reshape(1, PW)
        gq, gkv = q_norm_g[l].reshape(1, Q_LORA), kv_norm_g[l].reshape(1, KV_LORA)
        ypool = _pool_fwd(hin, wbd[l], pscale, name=f"l{l}_pool")
        cqn, ckvn, kpe = _norms_fwd(hin, gq, gkv, pw=PW, name=f"l{l}_norms")
        qraw = _mm(cqn, wuqT, lead=l, tb=True, name=f"l{l}_qraw")
        kv = _mm(ckvn, wukvT, lead=l, tb=True, name=f"l{l}_kv")
        qh, kh, vh = _heads_fwd(qraw, kv, kpe, tabs, H=H, name=f"l{l}_heads")
        o, lse = _flash_fwd(qh, kh, vh, H=H, name=f"l{l}_flash")
        cat = jnp.concatenate([ypool, o], axis=1)
        ymix = _mm(cat, wsq, lead=l, name=f"l{l}_ymix")
        x2, x2b = _ln_fwd(x1, ymix, lng[l, 1:2], lnb[l, 1:2], alpha=alpha, s=1.0, name=f"l{l}_ln1")
        qm = _mm(x2b, wsq, lead=L + l, out_dtype=BF16, name=f"l{l}_qm")
        kvm = _mm(memb, wkvT, lead=l, tb=True, name=f"l{l}_kvm")
        km, vm = kvm[:, :D], kvm[:, D:]
        om = _mem_fwd(qm, km, vm, name=f"l{l}_memattn")
        ymem = _mm(om, wsq, lead=2 * L + l, name=f"l{l}_ymem")
        x3, x3b = _ln_fwd(x2, ymem, lng[l, 2:3], lnb[l, 2:3], alpha=alpha, s=1.0, name=f"l{l}_ln2")
        sv["ffn2"], x4, x4b = ffn_fwd(l, 1, x3, x3b, 3)
        sv.update(x1=x1, x1b=x1b, hin=hin, pscale=pscale, gq=gq, gkv=gkv, cqn=cqn, ckvn=ckvn,
                  qh=qh, kh=kh, vh=vh, lse=lse, cat=cat, ymix=ymix, x2=x2, x2b=x2b, qm=qm, km=km, vm=vm,
                  om=om, ymem=ymem)
        saved.append(sv)
        xres, xb = x4, x4b

    dx, loss_blk = _loss_head(xres, target, name="loss_head")
    loss = lax.psum(loss_blk[0, 0], ("x", "y", "c"))

    gW = {}
    gS = {}

    def ffn_bwd(l, which, sv, dx, k):
        idx = which * L + l
        tag = f"l{l}_ffn{which}"
        dxres, dyb, dg, db = _ln_bwd(sv["xres"], sv["y"], lng[l, k:k + 1], lnb[l, k:k + 1], dx,
                                     alpha=alpha, s=0.5, name=f"l{l}_ln{k}_bwd")
        gW[("w2", which, l)] = _mm(sv["a"], dyb, ta=True, out_dtype=BF16, name=f"{tag}_dw2", tn=D)
        dh = _ffn_down_bwd(dyb, w2, idx, sv["h13"], name=f"{tag}_dh")
        dxn = _mm(dh, w13T, lead=idx, add=dxres, name=f"{tag}_dx", tn=D)
        dw13 = _mm(dh, sv["xb"], ta=True, out_dtype=BF16, name=f"{tag}_dw13", tn=D)
        gW[("w13", which, l)] = _deinterleave(dw13, 0)
        gS[("ln_g", l, k)], gS[("ln_b", l, k)] = dg, db
        return dxn

    for l in reversed(range(L)):
        sv = saved[l]
        dx = ffn_bwd(l, 1, sv["ffn2"], dx, 3)
        dxres, dyb, dg, db = _ln_bwd(sv["x2"], sv["ymem"], lng[l, 2:3], lnb[l, 2:3], dx,
                                     alpha=alpha, s=1.0, name=f"l{l}_ln2_bwd")
        gS[("ln_g", l, 2)], gS[("ln_b", l, 2)] = dg, db
        dom = _mm(dyb, wsq, lead=2 * L + l, tb=True, out_dtype=BF16, name=f"l{l}_dom")
        gW[("mem_wo", l)] = _mm(sv["om"], dyb, ta=True, out_dtype=BF16, name=f"l{l}_dwo", tn=D)
        dqm, dkm, dvm = _mem_bwd(sv["qm"], sv["km"], sv["vm"], dom, name=f"l{l}_memattn_bwd")
        dx = _mm(dqm, wsq, lead=L + l, tb=True, add=dxres, name=f"l{l}_dx2", tn=D)
        gW[("mem_wq", l)] = _mm(sv["x2b"], dqm, ta=True, out_dtype=BF16, name=f"l{l}_dwq", tn=D)
        dkvm = jnp.concatenate([dkm, dvm], axis=1).astype(BF16)
        gW[("mem_wkv", l)] = _mm(dkvm, memb, ta=True, out_dtype=BF16, name=f"l{l}_dwkv", tn=D)
        dxres, dyb, dg, db = _ln_bwd(sv["x1"], sv["ymix"], lng[l, 1:2], lnb[l, 1:2], dx,
                                     alpha=alpha, s=1.0, name=f"l{l}_ln1_bwd")
        gS[("ln_g", l, 1)], gS[("ln_b", l, 1)] = dg, db
        dcat = _mm(dyb, wsq, lead=l, tb=True, name=f"l{l}_dcat", tn=D)
        gW[("w_out", l)] = _mm(sv["cat"], dyb, ta=True, out_dtype=BF16, name=f"l{l}_dwout", tn=D)
        dqh, dkh, dvh = _flash_bwd(sv["qh"], sv["kh"], sv["vh"], sv["cat"], dcat, sv["lse"], H=H, pw=PW,
                                   name=f"l{l}_flash_bwd")
        dqraw, dkv, dkpe = _heads_bwd(dqh, dkh, dvh, tabs, H=H, name=f"l{l}_heads_bwd")
        dcq = _mm(dqraw, wuqT, lead=l, name=f"l{l}_dcq")
        gW[("w_uq", l)] = _mm(dqraw, sv["cqn"], ta=True, out_dtype=BF16, name=f"l{l}_dwuq")
        dckv = _mm(dkv, wukvT, lead=l, name=f"l{l}_dckv")
        gW[("w_ukv", l)] = _mm(dkv, sv["ckvn"], ta=True, out_dtype=BF16, name=f"l{l}_dwukv")
        du, dwbd, dps = _pool_bwd(sv["hin"], dcat, wbd[l], sv["pscale"], name=f"l{l}_pool_bwd")
        dhin, dgq, dgkv = _norms_bwd(sv["hin"], sv["gq"], sv["gkv"], dcq, dckv, dkpe, du, pw=PW,
                                     name=f"l{l}_norms_bwd")
        pg = PW // len(POOL_WINDOWS)
        gS[("pool_w", l)] = jnp.stack([dwbd[g * pg:(g + 1) * pg, g * pg:(g + 1) * pg]
                                       for g in range(len(POOL_WINDOWS))])
        gS[("pool_scale", l)], gS[("q_norm_g", l)], gS[("kv_norm_g", l)] = dps, dgq, dgkv
        dx = _mm(dhin, winp, lead=l, tb=True, add=dxres, name=f"l{l}_dx1", tn=D)
        gW[("w_in", l)] = _mm(sv["x1b"], dhin, ta=True, out_dtype=BF16, name=f"l{l}_dwin", tn=DINP)
        dx = ffn_bwd(l, 0, sv["ffn1"], dx, 0)
    grad_x = dx.reshape(1, T, D)

    rs_list, rs_class = [], []
    for keys, cl in (([("w13", f, l) for f in range(2) for l in range(L)], 0),
                     ([("w2", f, l) for f in range(2) for l in range(L)], 0),
                     ([("mem_wkv", l) for l in range(L)], 0),
                     ([(nme, l) for nme in ("w_out", "mem_wq", "mem_wo") for l in range(L)], 0),
                     ([("w_in", l) for l in range(L)], 1),
                     ([("w_uq", l) for l in range(L)], 2),
                     ([("w_ukv", l) for l in range(L)], 3)):
        for key in keys:
            g = gW[key]
            if key[0] == "w_uq":
                g = g.reshape(H, HEAD_PAD, Q_LORA)[:, :QW, :].reshape(H * QW, Q_LORA)
            rs_list.append((key, g))
            rs_class.append(cl)
    garrs = [g for _, g in rs_list]
    offs, tiles, used, _ = _class_layout(garrs, rs_class)
    core = lax.axis_index("c").astype(jnp.int32).reshape(1)
    chip = (2 * lax.axis_index("x") + lax.axis_index("y")).astype(jnp.int32).reshape(1)
    parts = list(_rs_to_sibling(garrs, rs_class, name="rs_sibling"))
    for w, (g, cl, off, tr) in enumerate(zip(garrs, rs_class, offs, tiles)):
        parts[cl] = _pair_sum(core, g, parts[cl], off, tr, name=f"rs_pair_sum_{w}")
    gathered = _rs_to_chips(parts, used, name="rs_chips")
    sums = [_quad_sum(chip, p, a, u, name=f"rs_quad_sum{k}")
            for k, (p, a, u) in enumerate(zip(parts, gathered, used))]
    gsh = {}
    for (key, g), cl, off in zip(rs_list, rs_class, offs):
        gsh[key] = sums[cl][off:off + g.shape[0] // N_DEV, :]

    small_keys = []
    for l in range(L):
        small_keys += [("pool_w", l), ("pool_scale", l), ("q_norm_g", l), ("kv_norm_g", l)]
        small_keys += [("ln_g", l, k) for k in range(4)] + [("ln_b", l, k) for k in range(4)]
    flat = jnp.concatenate([gS[k].reshape(-1) for k in small_keys])
    n_small = flat.shape[0]
    rows = -(-n_small // (8 * LANE)) * 8
    flat = jnp.pad(flat, (0, rows * LANE - n_small)).reshape(rows, LANE)
    red = _all_reduce_small(flat, name="ar_small").reshape(-1)
    gsm, pos = {}, 0
    for k in small_keys:
        size = math.prod(gS[k].shape)
        gsm[k] = red[pos:pos + size].reshape(gS[k].shape)
        pos += size

    me = 4 * lax.axis_index("x") + 2 * lax.axis_index("y") + lax.axis_index("c")
    dsh = D // N_DEV
    stack = lambda f: jnp.stack([f(l) for l in range(L)])
    g_ln_g = stack(lambda l: jnp.concatenate([gsm[("ln_g", l, k)] for k in range(4)], axis=0))
    g_ln_b = stack(lambda l: jnp.concatenate([gsm[("ln_b", l, k)] for k in range(4)], axis=0))
    grads = {
        "ln_g": lax.dynamic_slice_in_dim(g_ln_g, me * dsh, dsh, axis=2),
        "ln_b": lax.dynamic_slice_in_dim(g_ln_b, me * dsh, dsh, axis=2),
        "ffn1_w13": stack(lambda l: gsh[("w13", 0, l)].T),
        "ffn1_w2": stack(lambda l: gsh[("w2", 0, l)]),
        "w_in": stack(lambda l: gsh[("w_in", l)][:, :DIN]),
        "pool_w": stack(lambda l: gsm[("pool_w", l)]),
        "pool_scale": stack(lambda l: gsm[("pool_scale", l)].reshape(PW)),
        "q_norm_g": stack(lambda l: gsm[("q_norm_g", l)].reshape(Q_LORA)),
        "w_uq": stack(lambda l: gsh[("w_uq", l)].T),
        "kv_norm_g": stack(lambda l: gsm[("kv_norm_g", l)].reshape(KV_LORA)),
        "w_ukv": stack(lambda l: gsh[("w_ukv", l)].T),
        "w_out": stack(lambda l: gsh[("w_out", l)]),
        "mem_wq": stack(lambda l: gsh[("mem_wq", l)]),
        "mem_wkv": stack(lambda l: gsh[("mem_wkv", l)].T),
        "mem_wo": stack(lambda l: gsh[("mem_wo", l)]),
        "ffn2_w13": stack(lambda l: gsh[("w13", 1, l)].T),
        "ffn2_w2": stack(lambda l: gsh[("w2", 1, l)]),
    }

    names = ["ln_g", "ln_b", "ffn1_w13", "ffn1_w2", "w_in", "pool_w", "pool_scale", "q_norm_g", "w_uq",
             "kv_norm_g", "w_ukv", "w_out", "mem_wq", "mem_wkv", "mem_wo", "ffn2_w13", "ffn2_w2"]
    weights = dict(ln_g=ln_g, ln_b=ln_b, ffn1_w13=ffn1_w13, ffn1_w2=ffn1_w2, w_in=w_in, pool_w=pool_w,
                   pool_scale=pool_scale, q_norm_g=q_norm_g, w_uq=w_uq, kv_norm_g=kv_norm_g, w_ukv=w_ukv,
                   w_out=w_out, mem_wq=mem_wq, mem_wkv=mem_wkv, mem_wo=mem_wo, ffn2_w13=ffn2_w13,
                   ffn2_w2=ffn2_w2)
    ms = dict(ln_g=m_ln_g, ln_b=m_ln_b, ffn1_w13=m_ffn1_w13, ffn1_w2=m_ffn1_w2, w_in=m_w_in, pool_w=m_pool_w,
              pool_scale=m_pool_scale, q_norm_g=m_q_norm_g, w_uq=m_w_uq, kv_norm_g=m_kv_norm_g,
              w_ukv=m_w_ukv, w_out=m_w_out, mem_wq=m_mem_wq, mem_wkv=m_mem_wkv, mem_wo=m_mem_wo,
              ffn2_w13=m_ffn2_w13, ffn2_w2=m_ffn2_w2)
    vs = dict(ln_g=v_ln_g, ln_b=v_ln_b, ffn1_w13=v_ffn1_w13, ffn1_w2=v_ffn1_w2, w_in=v_w_in, pool_w=v_pool_w,
              pool_scale=v_pool_scale, q_norm_g=v_q_norm_g, w_uq=v_w_uq, kv_norm_g=v_kv_norm_g,
              w_ukv=v_w_ukv, w_out=v_w_out, mem_wq=v_mem_wq, mem_wkv=v_mem_wkv, mem_wo=v_mem_wo,
              ffn2_w13=v_ffn2_w13, ffn2_w2=v_ffn2_w2)
    deltas, new_m, new_v = [], [], []
    for nme in names:
        d, mn, vn = _adamw(weights[nme], grads[nme], ms[nme], vs[nme], name=f"adamw_{nme}")
        deltas.append(d)
        new_m.append(mn)
        new_v.append(vn)
    return (loss, grad_x, *[grads[nme] for nme in names], *deltas, *new_m, *new_v)
```

```python
import functools
import math

import jax
import jax.numpy as jnp
from jax import lax
from jax.experimental import pallas as pl
from jax.experimental.pallas import tpu as pltpu

F32 = jnp.float32
BF16 = jnp.bfloat16
MESH = pl.DeviceIdType.MESH

CHUNK = 64
MEM_HEADS = 4
POOL_WINDOWS = (2, 4, 8, 16)
QK_NOPE = 128
QK_ROPE = 64
V_HEAD = 128
Q_LORA = 256
KV_LORA = 128
ROPE_BASE = 10000.0
LN_EPS = 1e-5
RMS_EPS = 1e-6
NEG_INF = -1e30
ADAM_LR = 0.001
ADAM_B1 = 0.9
ADAM_B2 = 0.999
ADAM_EPS = 1e-08
ADAM_WD = 0.01
ADAM_STEP = 10

N_DEV = 8
LANE = 128
HEAD_PAD = 2 * LANE
POOL_HALO = 16
VMEM_CAP = 56 * 1024 * 1024
VMEM_FLOOR = 32 * 1024 * 1024


def _tile(n, pref, mult):
    t = (min(pref, n) // mult) * mult
    while t >= mult:
        if n % t == 0:
            return t
        t -= mult
    return n


def _params(sem, est_bytes):
    limit = int(min(max(2 * est_bytes + (8 << 20), VMEM_FLOOR), VMEM_CAP))
    return pltpu.CompilerParams(dimension_semantics=sem, vmem_limit_bytes=limit)


def _nbytes(shape, dtype):
    return math.prod(shape) * jnp.dtype(dtype).itemsize


def _hbm(x):
    return pltpu.with_memory_space_constraint(x, pltpu.HBM)


def _dg(a, b, ca, cb):
    return lax.dot_general(a.astype(BF16), b.astype(BF16), (((ca,), (cb,)), ((), ())),
                           preferred_element_type=F32)


@jax.custom_vjp
def _bdot_nn(a, b):
    return _dg(a, b, 1, 0)


def _bdot_nn_fwd(a, b):
    return _dg(a, b, 1, 0), (a, b)


def _bdot_nn_bwd(res, ct):
    a, b = res
    return _dg(ct, b, 1, 1).astype(a.dtype), _dg(a, ct, 0, 0).astype(b.dtype)


_bdot_nn.defvjp(_bdot_nn_fwd, _bdot_nn_bwd)


@jax.custom_vjp
def _bdot_nt(a, b):
    return _dg(a, b, 1, 1)


def _bdot_nt_fwd(a, b):
    return _dg(a, b, 1, 1), (a, b)


def _bdot_nt_bwd(res, ct):
    a, b = res
    return _dg(ct, b, 1, 0).astype(a.dtype), _dg(ct, a, 0, 0).astype(b.dtype)


_bdot_nt.defvjp(_bdot_nt_fwd, _bdot_nt_bwd)


@functools.partial(jax.custom_vjp, nondiff_argnums=(1,))
def _lane_roll(x, shift):
    return pltpu.roll(x, shift % x.shape[1], axis=1)


def _lane_roll_fwd(x, shift):
    return _lane_roll(x, shift), None


def _lane_roll_bwd(shift, _, ct):
    return (_lane_roll(ct, -shift),)


_lane_roll.defvjp(_lane_roll_fwd, _lane_roll_bwd)


@functools.partial(jax.custom_vjp, nondiff_argnums=(1, 2))
def _cols(x, lo, hi):
    return x[:, lo:hi]


def _cols_fwd(x, lo, hi):
    return x[:, lo:hi], x.shape[1]


def _cols_bwd(lo, hi, width, ct):
    parts = []
    if lo > 0:
        parts.append(jnp.zeros((ct.shape[0], lo), ct.dtype))
    parts.append(ct)
    if hi < width:
        parts.append(jnp.zeros((ct.shape[0], width - hi), ct.dtype))
    return (jnp.concatenate(parts, axis=1) if len(parts) > 1 else ct,)


_cols.defvjp(_cols_fwd, _cols_bwd)


MM_VMEM_BUDGET = 20 * 1024 * 1024


def _mm(a, b, *, name, ta=False, tb=False, out_dtype=F32, lead=None, add=None, add_scale=1.0,
        tm=1024, tn=1024, tk=2816):
    if ta:
        K, M = a.shape
    else:
        M, K = a.shape
    bshape = b.shape[1:] if lead is not None else b.shape
    if tb:
        N, Kb = bshape
    else:
        Kb, N = bshape
    assert K == Kb, (name, a.shape, b.shape)

    def blocks(tm, tn, tk):
        tm = _tile(M, tm, LANE if ta else 16)
        tn = _tile(N, tn, LANE)
        tk = _tile(K, tk, LANE)
        nbytes = (tm * tk * a.dtype.itemsize + tk * tn * b.dtype.itemsize
                  + tm * tn * (jnp.dtype(out_dtype).itemsize + (4 if K // tk > 1 else 0)
                               + (add.dtype.itemsize if add is not None else 0)))
        return tm, tn, tk, nbytes

    tm, tn, tk, est = blocks(tm, tn, tk)
    for shrink in ("m", "k", "m", "k", "n"):
        if est <= MM_VMEM_BUDGET:
            break
        if shrink == "m":
            tm, tn, tk, est = blocks(max(tm // 2, LANE), tn, tk)
        elif shrink == "k":
            tm, tn, tk, est = blocks(tm, tn, max(tk // 2, LANE))
        else:
            tm, tn, tk, est = blocks(tm, max(tn // 2, LANE), tk)
    nk = K // tk
    ca = 0 if ta else 1
    cb = 1 if tb else 0

    def body(*refs):
        a_ref, b_ref = refs[0], refs[1]
        add_ref = refs[2] if add is not None else None
        o_ref = refs[3] if add is not None else refs[2]

        def finish(r):
            if add_ref is not None:
                r = r + add_scale * add_ref[...].astype(F32)
            o_ref[...] = r.astype(o_ref.dtype)

        if nk == 1:
            finish(_dg(a_ref[...], b_ref[...], ca, cb))
            return
        acc_ref = refs[-1]
        k = pl.program_id(2)

        @pl.when(k == 0)
        def _():
            acc_ref[...] = jnp.zeros_like(acc_ref)

        acc_ref[...] += _dg(a_ref[...], b_ref[...], ca, cb)

        @pl.when(k == nk - 1)
        def _():
            finish(acc_ref[...])

    a_blk = (tk, tm) if ta else (tm, tk)
    a_map = (lambda i, j, k: (k, i)) if ta else (lambda i, j, k: (i, k))
    b_blk = (tn, tk) if tb else (tk, tn)
    if lead is None:
        b_map = (lambda i, j, k: (j, k)) if tb else (lambda i, j, k: (k, j))
        b_spec = pl.BlockSpec(b_blk, b_map)
    else:
        b_map = (lambda i, j, k: (lead, j, k)) if tb else (lambda i, j, k: (lead, k, j))
        b_spec = pl.BlockSpec((None,) + b_blk, b_map)
    in_specs = [pl.BlockSpec(a_blk, a_map), b_spec]
    args = [a, b]
    if add is not None:
        in_specs.append(pl.BlockSpec((tm, tn), lambda i, j, k: (i, j)))
        args.append(add)
    return pl.pallas_call(
        body, name=name,
        grid=(M // tm, N // tn, nk),
        in_specs=in_specs,
        out_specs=pl.BlockSpec((tm, tn), lambda i, j, k: (i, j)),
        out_shape=jax.ShapeDtypeStruct((M, N), out_dtype),
        scratch_shapes=[pltpu.VMEM((tm, tn), F32)] if nk > 1 else [],
        compiler_params=_params(("parallel", "parallel", "arbitrary"), est + tm * tn * 4),
    )(*[_hbm(v) for v in args])


def _rowwise(fn, tiles, params, tile_outs, acc_outs=(), *, tm, name):
    tile_arrays, tile_specs = [], []
    for t in tiles:
        if isinstance(t, tuple):
            tile_arrays.append(t[0])
            tile_specs.append(t[1])
        else:
            tile_arrays.append(t)
            tile_specs.append(pl.BlockSpec((tm, t.shape[1]), lambda i: (i, 0)))
    T = tile_arrays[0].shape[0]
    nt, np_, nto, nao = len(tile_arrays), len(params), len(tile_outs), len(acc_outs)

    def body(*refs):
        i = pl.program_id(0)
        tvals = [r[...] for r in refs[:nt]]
        pvals = [r[...] for r in refs[nt:nt + np_]]
        to_refs = refs[nt + np_:nt + np_ + nto]
        ao_refs = refs[nt + np_ + nto:]
        touts, aouts = fn(i, tvals, pvals)
        for r, v in zip(to_refs, touts):
            r[...] = v.astype(r.dtype)
        if nao:
            @pl.when(i == 0)
            def _():
                for r in ao_refs:
                    r[...] = jnp.zeros_like(r)
            for r, v in zip(ao_refs, aouts):
                r[...] += v.astype(r.dtype)

    in_specs = tile_specs + [pl.BlockSpec(p.shape, lambda i: (0, 0)) for p in params]
    out_specs = [pl.BlockSpec((tm, c), lambda i: (i, 0)) for c, _ in tile_outs]
    out_specs += [pl.BlockSpec(s, lambda i: (0, 0)) for s, _ in acc_outs]
    out_shape = [jax.ShapeDtypeStruct((T, c), d) for c, d in tile_outs]
    out_shape += [jax.ShapeDtypeStruct(s, d) for s, d in acc_outs]
    width = sum(s.block_shape[-1] for s in tile_specs) + sum(c for c, _ in tile_outs)
    est = 6 * tm * width * 4 + sum(_nbytes(p.shape, F32) for p in params) * 4
    return pl.pallas_call(
        body, name=name, grid=(T // tm,),
        in_specs=in_specs, out_specs=out_specs, out_shape=out_shape,
        compiler_params=_params(("arbitrary",) if nao else ("parallel",), est),
    )(*[_hbm(v) for v in tile_arrays], *params)


def _ln_fn(alpha, s, xres, y, g, b):
    z = alpha * xres.astype(F32) + s * y.astype(F32)
    mu = jnp.mean(z, axis=-1, keepdims=True)
    zc = z - mu
    var = jnp.mean(zc * zc, axis=-1, keepdims=True)
    return zc * lax.rsqrt(var + LN_EPS) * g + b


def _ln_fwd(xres, y, g, b, *, alpha, s, name):
    T, D = xres.shape
    tm = _tile(T, 256, 16)

    def fn(i, tv, pv):
        out = _ln_fn(alpha, s, tv[0], tv[1], pv[0], pv[1])
        return (out, out), ()

    return _rowwise(fn, [xres, y], [g, b], [(D, F32), (D, BF16)], tm=tm, name=name)


def _ln_bwd(xres, y, g, b, dout, *, alpha, s, name):
    T, D = xres.shape
    tm = _tile(T, 256, 16)

    def fn(i, tv, pv):
        _, vjp = jax.vjp(functools.partial(_ln_fn, alpha, s), tv[0], tv[1], pv[0], pv[1])
        dx, dy, dg, db = vjp(tv[2].astype(F32))
        return (dx, dy), (dg, db)

    return _rowwise(fn, [xres, y, dout], [g, b], [(D, F32), (D, BF16)],
                    [((1, D), F32), ((1, D), F32)], tm=tm, name=name)


FFN_TILE = 256


def _interleave(w, axis):
    n = w.shape[axis] // (2 * FFN_TILE)
    shp = w.shape[:axis] + (2, n, FFN_TILE) + w.shape[axis + 1:]
    return jnp.swapaxes(w.reshape(shp), axis, axis + 1).reshape(w.shape)


def _deinterleave(w, axis):
    n = w.shape[axis] // (2 * FFN_TILE)
    shp = w.shape[:axis] + (n, 2, FFN_TILE) + w.shape[axis + 1:]
    return jnp.swapaxes(w.reshape(shp), axis, axis + 1).reshape(w.shape)


def _ffn_up(xb, w13t, lead, *, name, ride=None):
    T, D = xb.shape
    F = w13t.shape[1] // 2
    tc = FFN_TILE
    tm = _tile(T, 1024, 16)

    def body(x_ref, w_ref, h_ref, a_ref):
        h = _dg(x_ref[...], w_ref[...], 1, 1)
        g, u = h[:, :tc], h[:, tc:]
        h_ref[...] = h.astype(h_ref.dtype)
        a_ref[...] = (g * jax.nn.sigmoid(g) * u).astype(a_ref.dtype)

    est = (tm * D + 2 * tc * D + 3 * tm * tc) * 2 + 3 * tm * tc * 4
    (h13, a), gathered = _host_call(
        body, name=name, grid=(T // tm, F // tc),
        in_specs=[pl.BlockSpec((tm, D), lambda i, j: (i, 0)),
                  pl.BlockSpec((None, 2 * tc, D), lambda i, j: (lead, j, 0))],
        out_specs=[pl.BlockSpec((tm, 2 * tc), lambda i, j: (i, j)),
                   pl.BlockSpec((tm, tc), lambda i, j: (i, j))],
        out_shape=[jax.ShapeDtypeStruct((T, 2 * F), BF16), jax.ShapeDtypeStruct((T, F), BF16)],
        args=[_hbm(xb), _hbm(w13t)], sem=("parallel", "parallel"), est=est, ride=ride)
    return h13, a, gathered


def _ffn_down_bwd(dyb, w2, lead, h13, *, name):
    T, D = dyb.shape
    F = w2.shape[1]
    tc = FFN_TILE
    tm = _tile(T, 1024, 16)

    def body(dy_ref, w_ref, h_ref, dh_ref):
        d = _dg(dy_ref[...], w_ref[...], 1, 1)
        h = h_ref[...].astype(F32)
        g, u = h[:, :tc], h[:, tc:]
        sig = jax.nn.sigmoid(g)
        gs = g * sig
        dh_ref[...] = jnp.concatenate([d * u * (sig + gs * (1.0 - sig)), d * gs], axis=1).astype(dh_ref.dtype)

    est = (tm * D + tc * D + 4 * tm * tc) * 2 + 6 * tm * tc * 4
    return pl.pallas_call(
        body, name=name, grid=(T // tm, F // tc),
        in_specs=[pl.BlockSpec((tm, D), lambda i, j: (i, 0)),
                  pl.BlockSpec((None, tc, D), lambda i, j: (lead, j, 0)),
                  pl.BlockSpec((tm, 2 * tc), lambda i, j: (i, j))],
        out_specs=pl.BlockSpec((tm, 2 * tc), lambda i, j: (i, j)),
        out_shape=jax.ShapeDtypeStruct((T, 2 * F), BF16),
        compiler_params=_params(("parallel", "parallel"), est),
    )(_hbm(dyb), _hbm(w2), _hbm(h13))


def _pool_select(parts, pw):
    pg = pw // len(POOL_WINDOWS)
    grp = lax.broadcasted_iota(jnp.int32, parts[0].shape, 1) // pg
    out = parts[3]
    for g in (2, 1, 0):
        out = jnp.where(grp == g, parts[g], out)
    return out


def _pool_count(t0, rows, pw):
    pg = pw // len(POOL_WINDOWS)
    grp = lax.broadcasted_iota(jnp.int32, (rows, pw), 1) // pg
    win = jnp.where(grp == 0, POOL_WINDOWS[0],
                    jnp.where(grp == 1, POOL_WINDOWS[1],
                              jnp.where(grp == 2, POOL_WINDOWS[2], POOL_WINDOWS[3])))
    t = t0 + lax.broadcasted_iota(jnp.int32, (rows, pw), 0)
    return jnp.minimum(t + 1, win).astype(F32)


def _window_sums(ext, up):
    n = ext.shape[0]
    sums, cur, k = [], ext, 1
    for _ in POOL_WINDOWS:
        cur = cur + pltpu.roll(cur, (n - k) if up else k, axis=0)
        sums.append(cur)
        k *= 2
    return sums


def _pool_delta(u, halo, t0):
    tm, pw = u.shape
    ext = jnp.concatenate([halo, u], axis=0)
    sums = [s[POOL_HALO:, :] for s in _window_sums(ext, up=False)]
    return _pool_select(sums, pw) / _pool_count(t0, tm, pw) - u


def _pool_fwd(hin, wbd, scale, *, name):
    T = hin.shape[0]
    pw = wbd.shape[0]
    tm = _tile(T, 256, POOL_HALO)
    per = tm // POOL_HALO

    def body(u_ref, halo_ref, w_ref, s_ref, y_ref):
        i = pl.program_id(0)
        halo = jnp.where(i > 0, halo_ref[...], 0.0)
        d = _pool_delta(u_ref[...], halo, i * tm)
        y_ref[...] = (_dg(d, w_ref[...], 1, 0) * s_ref[...]).astype(y_ref.dtype)

    return pl.pallas_call(
        body, name=name, grid=(T // tm,),
        in_specs=[pl.BlockSpec((tm, pw), lambda i: (i, 0)),
                  pl.BlockSpec((POOL_HALO, pw), lambda i: (jnp.maximum(i * per - 1, 0), 0)),
                  pl.BlockSpec((pw, pw), lambda i: (0, 0)),
                  pl.BlockSpec((1, pw), lambda i: (0, 0))],
        out_specs=pl.BlockSpec((tm, pw), lambda i: (i, 0)),
        out_shape=jax.ShapeDtypeStruct((T, pw), BF16),
        compiler_params=_params(("parallel",), 16 * tm * pw * 4),
    )(_hbm(hin), _hbm(hin), wbd, scale)


def _pool_bwd(hin, dcat, wbd, scale, *, name):
    T = hin.shape[0]
    pw = wbd.shape[0]
    tm = _tile(T, 256, POOL_HALO)
    per = tm // POOL_HALO
    nt = T // tm

    def body(u_ref, halo_ref, dy_ref, dyn_ref, w_ref, s_ref, du_ref, dw_ref, ds_ref):
        i = pl.program_id(0)

        @pl.when(i == 0)
        def _():
            dw_ref[...] = jnp.zeros_like(dw_ref)
            ds_ref[...] = jnp.zeros_like(ds_ref)

        halo = jnp.where(i > 0, halo_ref[...], 0.0)
        d = _pool_delta(u_ref[...], halo, i * tm)
        w = w_ref[...]
        sc = s_ref[...]
        dy = dy_ref[...]
        dyn = jnp.where(i < nt - 1, dyn_ref[...], 0.0)
        ds_ref[...] += jnp.sum(dy * _dg(d, w, 1, 0), axis=0, keepdims=True)
        dys = dy * sc
        dw_ref[...] += _dg(d, dys, 0, 0)
        dys_ext = jnp.concatenate([dys, dyn * sc], axis=0)
        dd_ext = _dg(dys_ext, w, 1, 1)
        ddp = dd_ext / _pool_count(i * tm, tm + POOL_HALO, pw)
        sums = [s[:tm, :] for s in _window_sums(ddp, up=True)]
        du_ref[...] = _pool_select(sums, pw) - dd_ext[:tm, :]

    return pl.pallas_call(
        body, name=name, grid=(nt,),
        in_specs=[pl.BlockSpec((tm, pw), lambda i: (i, 0)),
                  pl.BlockSpec((POOL_HALO, pw), lambda i: (jnp.maximum(i * per - 1, 0), 0)),
                  pl.BlockSpec((tm, pw), lambda i: (i, 0)),
                  pl.BlockSpec((POOL_HALO, pw), lambda i: (jnp.minimum((i + 1) * per, nt * per - 1), 0)),
                  pl.BlockSpec((pw, pw), lambda i: (0, 0)),
                  pl.BlockSpec((1, pw), lambda i: (0, 0))],
        out_specs=[pl.BlockSpec((tm, pw), lambda i: (i, 0)),
                   pl.BlockSpec((pw, pw), lambda i: (0, 0)),
                   pl.BlockSpec((1, pw), lambda i: (0, 0))],
        out_shape=[jax.ShapeDtypeStruct((T, pw), F32),
                   jax.ShapeDtypeStruct((pw, pw), F32),
                   jax.ShapeDtypeStruct((1, pw), F32)],
        compiler_params=_params(("arbitrary",), 24 * tm * pw * 4),
    )(_hbm(hin), _hbm(hin), _hbm(dcat), _hbm(dcat), wbd, scale)


def _rms(x, g):
    return x * lax.rsqrt(jnp.mean(x * x, axis=-1, keepdims=True) + RMS_EPS) * g


def _norms_fn(pw, h, gq, gkv):
    o1 = pw + Q_LORA
    o2 = o1 + KV_LORA
    return (_rms(_cols(h, pw, o1), gq), _rms(_cols(h, o1, o2), gkv), _cols(h, o2, h.shape[1]))


def _norms_fwd(hin, gq, gkv, *, pw, name):
    tm = _tile(hin.shape[0], 256, 16)

    def fn(i, tv, pv):
        return _norms_fn(pw, tv[0], pv[0], pv[1]), ()

    return _rowwise(fn, [hin], [gq, gkv], [(Q_LORA, BF16), (KV_LORA, BF16), (LANE, F32)], tm=tm, name=name)


def _norms_bwd(hin, gq, gkv, dcq, dckv, dkpe, du, *, pw, name):
    tm = _tile(hin.shape[0], 256, 16)
    dinp = hin.shape[1]

    def fn(i, tv, pv):
        _, vjp = jax.vjp(functools.partial(_norms_fn, pw), tv[0], pv[0], pv[1])
        dh, dgq, dgkv = vjp((tv[1].astype(F32), tv[2].astype(F32), tv[3].astype(F32)))
        dh = jnp.concatenate([tv[4], dh[:, pw:]], axis=1)
        return (dh,), (dgq, dgkv)

    return _rowwise(fn, [hin, dcq, dckv, dkpe, du], [gq, gkv], [(dinp, BF16)],
                    [((1, Q_LORA), F32), ((1, KV_LORA), F32)], tm=tm, name=name)


def _heads_fn(H, qraw, kv, kpe, rc, rs1, rs2):
    half = QK_ROPE // 2
    scale = (QK_NOPE + QK_ROPE) ** -0.5

    def rope(blk):
        return blk * rc + _lane_roll(blk, -half) * rs1 + _lane_roll(blk, half) * rs2

    krot = rope(kpe)
    qs, ks, vs = [], [], []
    for h in range(H):
        lo = h * HEAD_PAD
        qs += [_cols(qraw, lo, lo + LANE) * scale, rope(_cols(qraw, lo + LANE, lo + HEAD_PAD)) * scale]
        ks += [_cols(kv, lo, lo + LANE), krot]
        vs += [_cols(kv, lo + LANE, lo + HEAD_PAD)]
    return jnp.concatenate(qs, axis=1), jnp.concatenate(ks, axis=1), jnp.concatenate(vs, axis=1)


def _heads_fwd(qraw, kv, kpe, tabs, *, H, name):
    tm = _tile(qraw.shape[0], 256, 16)

    def fn(i, tv, pv):
        return _heads_fn(H, *tv), ()

    return _rowwise(fn, [qraw, kv, kpe, *tabs], [],
                    [(H * HEAD_PAD, BF16), (H * HEAD_PAD, BF16), (H * V_HEAD, BF16)], tm=tm, name=name)


def _heads_bwd(dq, dk, dv, tabs, *, H, name):
    tm = _tile(dq.shape[0], 256, 16)

    def fn(i, tv, pv):
        z = jnp.zeros((tm, H * HEAD_PAD), F32)
        zk = jnp.zeros((tm, LANE), F32)
        rc, rs1, rs2 = tv[3], tv[4], tv[5]
        _, vjp = jax.vjp(lambda a, b, c: _heads_fn(H, a, b, c, rc, rs1, rs2), z, z, zk)
        return vjp((tv[0], tv[1], tv[2])), ()

    return _rowwise(fn, [dq, dk, dv, *tabs], [],
                    [(H * HEAD_PAD, BF16), (H * HEAD_PAD, BF16), (LANE, F32)], tm=tm, name=name)


def _diag_mask(t):
    r = lax.broadcasted_iota(jnp.int32, (t, t), 0) // CHUNK
    c = lax.broadcasted_iota(jnp.int32, (t, t), 1) // CHUNK
    return r >= c


def _flash_fwd(qh, kh, vh, *, H, name, ride=None):
    T = qh.shape[0]
    t = _tile(T, 512, CHUNK)

    def body(q_ref, k_ref, v_ref, o_ref, lse_ref):
        i = pl.program_id(1)
        q = q_ref[...]

        def blk(j, carry, masked):
            m, l, acc = carry
            rows = pl.ds(pl.multiple_of(j * t, t), t)
            s = _dg(q, k_ref[rows, :], 1, 1)
            if masked:
                s = jnp.where(_diag_mask(t), s, NEG_INF)
            mn = jnp.maximum(m, jnp.max(s, axis=1, keepdims=True))
            p = jnp.exp(s - mn)
            corr = jnp.exp(m - mn)
            l = corr * l + jnp.sum(p, axis=1, keepdims=True)
            acc = corr * acc + _dg(p, v_ref[rows, :], 1, 0)
            return mn, l, acc

        init = (jnp.full((t, 1), NEG_INF, F32), jnp.zeros((t, 1), F32), jnp.zeros((t, V_HEAD), F32))
        carry = lax.fori_loop(0, i, lambda j, c: blk(j, c, False), init)
        m, l, acc = blk(i, carry, True)
        o_ref[...] = (acc / l).astype(o_ref.dtype)
        lse_ref[...] = jnp.broadcast_to(m + jnp.log(l), (t, V_HEAD))

    est = 2 * T * (HEAD_PAD + V_HEAD) * 2 + 8 * t * t * 4
    (o, lse), gathered = _host_call(
        body, name=name, grid=(H, T // t),
        in_specs=[pl.BlockSpec((t, HEAD_PAD), lambda h, i: (i, h)),
                  pl.BlockSpec((T, HEAD_PAD), lambda h, i: (0, h)),
                  pl.BlockSpec((T, V_HEAD), lambda h, i: (0, h))],
        out_specs=[pl.BlockSpec((t, V_HEAD), lambda h, i: (i, h)),
                   pl.BlockSpec((t, V_HEAD), lambda h, i: (i, h))],
        out_shape=[jax.ShapeDtypeStruct((T, H * V_HEAD), BF16),
                   jax.ShapeDtypeStruct((T, H * V_HEAD), F32)],
        args=[_hbm(qh), _hbm(kh), _hbm(vh)], sem=("parallel", "parallel"), est=est, ride=ride)
    return o, lse, gathered


def _flash_bwd(qh, kh, vh, cat, dcat, lse, *, H, pw, name, ride=None):
    T = qh.shape[0]
    t = _tile(T, 512, CHUNK)
    nb = T // t
    off = pw // V_HEAD

    def body(q_ref, k_ref, v_ref, o_ref, do_ref, lse_ref, dq_ref, dk_ref, dv_ref):
        j = pl.program_id(1)

        @pl.when(j == 0)
        def _():
            dq_ref[...] = jnp.zeros_like(dq_ref)

        kj = k_ref[...]
        vj = v_ref[...]

        def blk(i, carry, masked):
            dk, dv = carry
            rows = pl.ds(pl.multiple_of(i * t, t), t)
            qi = q_ref[rows, :]
            doi = do_ref[rows, :]
            oi = o_ref[rows, :].astype(F32)
            lsei = lse_ref[rows, :][:, :1]
            s = _dg(qi, kj, 1, 1)
            if masked:
                s = jnp.where(_diag_mask(t), s, NEG_INF)
            p = jnp.exp(s - lsei)
            dv = dv + _dg(p, doi, 0, 0)
            dp = _dg(doi, vj, 1, 1)
            di = jnp.sum(doi * oi, axis=1, keepdims=True)
            ds = p * (dp - di)
            dk = dk + _dg(ds, qi, 0, 0)
            dq_ref[rows, :] += _dg(ds, kj, 1, 0)
            return dk, dv

        carry = blk(j, (jnp.zeros((t, HEAD_PAD), F32), jnp.zeros((t, V_HEAD), F32)), True)
        dk, dv = lax.fori_loop(j + 1, nb, lambda i, c: blk(i, c, False), carry)
        dk_ref[...] = dk
        dv_ref[...] = dv

    est = T * (HEAD_PAD * 2 + V_HEAD * 2 + V_HEAD * 4 + V_HEAD * 4 + HEAD_PAD * 4) + 10 * t * t * 4
    (dq, dk, dv), gathered = _host_call(
        body, name=name, grid=(H, nb),
        in_specs=[pl.BlockSpec((T, HEAD_PAD), lambda h, j: (0, h)),
                  pl.BlockSpec((t, HEAD_PAD), lambda h, j: (j, h)),
                  pl.BlockSpec((t, V_HEAD), lambda h, j: (j, h)),
                  pl.BlockSpec((T, V_HEAD), lambda h, j: (0, off + h)),
                  pl.BlockSpec((T, V_HEAD), lambda h, j: (0, off + h)),
                  pl.BlockSpec((T, V_HEAD), lambda h, j: (0, h))],
        out_specs=[pl.BlockSpec((T, HEAD_PAD), lambda h, j: (0, h)),
                   pl.BlockSpec((t, HEAD_PAD), lambda h, j: (j, h)),
                   pl.BlockSpec((t, V_HEAD), lambda h, j: (j, h))],
        out_shape=[jax.ShapeDtypeStruct((T, H * HEAD_PAD), F32),
                   jax.ShapeDtypeStruct((T, H * HEAD_PAD), F32),
                   jax.ShapeDtypeStruct((T, H * V_HEAD), F32)],
        args=[_hbm(v) for v in (qh, kh, vh, cat, dcat, lse)], sem=("arbitrary", "arbitrary"), est=est,
        ride=ride)
    return dq, dk, dv, gathered


def _mem_fn(q, k, v):
    hd = q.shape[1] // MEM_HEADS
    outs = []
    for h in range(MEM_HEADS):
        lo, hi = h * hd, (h + 1) * hd
        s = _bdot_nt(_cols(q, lo, hi), _cols(k, lo, hi)) * hd ** -0.5
        e = jnp.exp(s - lax.stop_gradient(jnp.max(s, axis=1, keepdims=True)))
        p = e / jnp.sum(e, axis=1, keepdims=True)
        outs.append(_bdot_nn(p, _cols(v, lo, hi)))
    return jnp.concatenate(outs, axis=1)


def _mem_fwd(q, k, v, *, name):
    T, D = q.shape
    tm = _tile(T, 256, 16)

    def fn(i, tv, pv):
        return (_mem_fn(tv[0], pv[0], pv[1]),), ()

    return _rowwise(fn, [q], [k, v], [(D, BF16)], tm=tm, name=name)[0]


def _mem_bwd(q, k, v, do, *, name):
    T, D = q.shape
    tm = _tile(T, 256, 16)

    def fn(i, tv, pv):
        _, vjp = jax.vjp(_mem_fn, tv[0], pv[0], pv[1])
        dq, dk, dv = vjp(tv[1].astype(F32))
        return (dq,), (dk, dv)

    return _rowwise(fn, [q, do], [k, v], [(D, BF16)], [(k.shape, F32), (v.shape, F32)], tm=tm, name=name)


def _loss_head(y, target, *, name):
    T, D = y.shape
    tm = _tile(T, 256, 16)

    def fn(i, tv, pv):
        err = tv[0] - tv[1]
        part = 0.5 * jnp.sum(jnp.sum(err * err, axis=1, keepdims=True) / D, axis=0, keepdims=True)
        return (err / D,), (jnp.broadcast_to(part, (8, LANE)),)

    return _rowwise(fn, [y, target], [], [(D, F32)], [((8, LANE), F32)], tm=tm, name=name)


def _adamw(w, g, m, v, *, name):
    shape = w.shape
    C = shape[-1]
    R = math.prod(shape[:-1])
    tr = _tile(R, 512, 8)
    b1c = 1.0 - ADAM_B1 ** ADAM_STEP
    b2c = 1.0 - ADAM_B2 ** ADAM_STEP

    def body(w_ref, g_ref, m_ref, v_ref, d_ref, mo_ref, vo_ref):
        gg = g_ref[...]
        mn = ADAM_B1 * m_ref[...] + (1.0 - ADAM_B1) * gg
        vn = ADAM_B2 * v_ref[...] + (1.0 - ADAM_B2) * (gg * gg)
        d_ref[...] = -ADAM_LR * ((mn / b1c) / (jnp.sqrt(vn / b2c) + ADAM_EPS) + ADAM_WD * w_ref[...])
        mo_ref[...] = mn
        vo_ref[...] = vn

    spec = pl.BlockSpec((tr, C), lambda i: (i, 0))
    outs = pl.pallas_call(
        body, name=name, grid=(R // tr,),
        in_specs=[spec] * 4, out_specs=[spec] * 3,
        out_shape=[jax.ShapeDtypeStruct((R, C), F32)] * 3,
        compiler_params=_params(("parallel",), 7 * tr * C * 4),
    )(*[_hbm(a.reshape(R, C)) for a in (w, g, m, v)])
    return [o.reshape(shape) for o in outs]


def _pair_sum(core, g, landed, off, tr, *, name):
    rows = g.shape[0] // N_DEV
    C = g.shape[1]
    per = rows // tr

    def body(core_ref, g_ref, l_ref, o_ref):
        o_ref[...] = (g_ref[...].astype(F32) + l_ref[...].astype(F32)).astype(o_ref.dtype)

    slab = pl.BlockSpec((None, tr, C), lambda p, i, core_ref: (p, off // tr + i, 0))
    return pl.pallas_call(
        body, name=name,
        grid_spec=pltpu.PrefetchScalarGridSpec(
            num_scalar_prefetch=1, grid=(4, per),
            in_specs=[pl.BlockSpec((tr, C), lambda p, i, core_ref: ((2 * p + core_ref[0]) * per + i, 0)), slab],
            out_specs=slab),
        out_shape=jax.ShapeDtypeStruct(landed.shape, landed.dtype),
        input_output_aliases={2: 0},
        compiler_params=_params(("arbitrary", "arbitrary"), 4 * tr * C * 4),
    )(core, _hbm(g), _hbm(landed))


def _quad_sum(chip, part, gathered, used, *, name):
    C = part.shape[2]
    R = used
    tr = _tile(math.gcd(used, part.shape[1]), 256, 16)

    def body(chip_ref, own_ref, a_ref, b_ref, c_ref, o_ref):
        o_ref[...] = ((own_ref[...].astype(F32) + a_ref[...].astype(F32)) + b_ref[...].astype(F32)) \
            + c_ref[...].astype(F32)

    def other(k):
        return pl.BlockSpec((None, tr, C), lambda i, chip_ref: (chip_ref[0] ^ k, i, 0))

    return pl.pallas_call(
        body, name=name,
        grid_spec=pltpu.PrefetchScalarGridSpec(
            num_scalar_prefetch=1, grid=(R // tr,),
            in_specs=[pl.BlockSpec((None, tr, C), lambda i, chip_ref: (chip_ref[0], i, 0)),
                      other(1), other(2), other(3)],
            out_specs=pl.BlockSpec((tr, C), lambda i, chip_ref: (i, 0))),
        out_shape=jax.ShapeDtypeStruct((R, C), F32),
        compiler_params=_params(("arbitrary",), 8 * tr * C * 4),
    )(chip, _hbm(part), _hbm(gathered), _hbm(gathered), _hbm(gathered))


def _place():
    x, y, c = lax.axis_index("x"), lax.axis_index("y"), lax.axis_index("c")
    return x, y, c


ANY = pl.BlockSpec(memory_space=pl.ANY)


class _Gather:
    def __init__(self, shards):
        self.shards = list(shards)
        self.n = len(self.shards)
        self.out_shape = [jax.ShapeDtypeStruct((s.shape[0], N_DEV * s.shape[1], s.shape[2]), s.dtype)
                          for s in self.shards]
        self.scratch = [pltpu.SemaphoreType.DMA((7 * self.n,)), pltpu.SemaphoreType.DMA((7 * self.n,)),
                        pltpu.SemaphoreType.DMA((self.n,))]
        self.operands = [_hbm(s) for s in self.shards]

    def _bind(self, refs):
        n = self.n
        ins, outs = refs[:n], refs[n:2 * n]
        send_sems, recv_sems, local_sems = refs[2 * n:]
        x, y, c = _place()
        me, sib = (x, y, c), (x, y, 1 - c)
        chips = [(1 - x, y), (x, 1 - y), (1 - x, 1 - y)]

        def rows(w, p):
            r = self.shards[w].shape[1]
            idx = 4 * p[0] + 2 * p[1] + p[2]
            return outs[w].at[:, pl.ds(pl.multiple_of(idx * r, 8), r), :]

        def copy(w, k, block, to, src=None):
            return pltpu.make_async_remote_copy(
                src_ref=rows(w, block) if src is None else src, dst_ref=rows(w, block),
                send_sem=send_sems.at[w * 7 + k], recv_sem=recv_sems.at[w * 7 + k],
                device_id=to, device_id_type=MESH)

        mine = [pltpu.make_async_copy(ins[w], rows(w, me), local_sems.at[w]) for w in range(n)]
        first = []
        for w in range(n):
            first.append(copy(w, 0, me, sib, src=ins[w]))
            for j, chip in enumerate(chips):
                first.append(copy(w, 1 + j, me, (*chip, c), src=ins[w]))
        passed = [copy(w, 4 + j, (*chip, c), sib) for j, chip in enumerate(chips) for w in range(n)]
        landed = [copy(w, 1 + j, (*chip, c), me) for j, chip in enumerate(chips) for w in range(n)]
        last = []
        for w in range(n):
            last.append(copy(w, 0, sib, me))
            for j, chip in enumerate(chips):
                last.append(copy(w, 4 + j, (*chip, 1 - c), me))
        return mine, first, landed, passed, last

    def start(self, refs):
        mine, first, _, _, _ = self._bind(refs)
        for cp in mine + first:
            cp.start()

    def forward(self, refs):
        _, _, landed, passed, _ = self._bind(refs)
        for arrived, fwd in zip(landed, passed):
            arrived.wait_recv()
            fwd.start()

    def finish(self, refs):
        mine, first, _, passed, last = self._bind(refs)
        for cp in last:
            cp.wait_recv()
        for cp in first + passed:
            cp.wait_send()
        for cp in mine:
            cp.wait()


class _ChipExchange:
    def __init__(self, parts, used):
        self.ncl = len(parts)
        self.used = list(used)
        self.out_shape = [jax.ShapeDtypeStruct(p.shape, p.dtype) for p in parts]
        self.scratch = [pltpu.SemaphoreType.DMA((3 * self.ncl,)), pltpu.SemaphoreType.DMA((3 * self.ncl,))]
        self.operands = [_hbm(p) for p in parts]
        self.n = self.ncl

    def _bind(self, refs):
        ncl = self.ncl
        ins, outs = refs[:ncl], refs[ncl:2 * ncl]
        send_sems, recv_sems = refs[2 * ncl:]
        x, y, c = _place()
        chips = [(1 - x, y), (x, 1 - y), (1 - x, 1 - y)]
        here = 2 * x + y
        sent, arriving = [], []
        for k in range(ncl):
            rows = pl.ds(0, self.used[k])
            for j, (cx, cy) in enumerate(chips):
                sems = dict(send_sem=send_sems.at[3 * k + j], recv_sem=recv_sems.at[3 * k + j],
                            device_id=(cx, cy, c), device_id_type=MESH)
                sent.append(pltpu.make_async_remote_copy(
                    src_ref=ins[k].at[2 * cx + cy, rows, :], dst_ref=outs[k].at[here, rows, :], **sems))
                arriving.append(pltpu.make_async_remote_copy(
                    src_ref=ins[k].at[here, rows, :], dst_ref=outs[k].at[2 * cx + cy, rows, :], **sems))
        return sent, arriving

    def start(self, refs):
        for cp in self._bind(refs)[0]:
            cp.start()

    def forward(self, refs):
        pass

    def finish(self, refs):
        sent, arriving = self._bind(refs)
        for cp in arriving:
            cp.wait_recv()
        for cp in sent:
            cp.wait_send()


def _exchange_alone(ex, *, name):
    def body(*refs):
        ex.start(refs)
        ex.forward(refs)
        ex.finish(refs)

    return pl.pallas_call(
        body, name=name, in_specs=[ANY] * ex.n, out_specs=[ANY] * ex.n,
        out_shape=ex.out_shape, scratch_shapes=ex.scratch,
    )(*ex.operands)


def _host_call(body, *, name, grid, in_specs, out_specs, out_shape, args, sem, est, ride=None):
    if ride is None:
        outs = pl.pallas_call(body, name=name, grid=grid, in_specs=in_specs, out_specs=out_specs,
                              out_shape=out_shape, compiler_params=_params(sem, est))(*args)
        return list(outs), []
    n_in, n_out, n = len(in_specs), len(out_specs), ride.n

    def full(*refs):
        ins, rin = refs[:n_in], refs[n_in:n_in + n]
        outs, rout = refs[n_in + n:n_in + n + n_out], refs[n_in + n + n_out:n_in + 2 * n + n_out]
        rrefs = (*rin, *rout, *refs[n_in + 2 * n + n_out:])
        step, total = _ride(ride, rrefs, grid)
        body(*ins, *outs)
        _ride_end(ride, rrefs, step, total)

    outs = pl.pallas_call(
        full, name=name, grid=grid,
        in_specs=list(in_specs) + [ANY] * n, out_specs=list(out_specs) + [ANY] * n,
        out_shape=list(out_shape) + ride.out_shape, scratch_shapes=ride.scratch,
        compiler_params=_params(("arbitrary",) * len(grid), est),
    )(*args, *ride.operands)
    return list(outs[:n_out]), list(outs[n_out:])


def _ride(ex, refs, grid):
    total = math.prod(grid)
    step = pl.program_id(0)
    for axis in range(1, len(grid)):
        step = step * grid[axis] + pl.program_id(axis)
    pl.when(step == 0)(lambda: ex.start(refs))
    return step, total


def _ride_end(ex, refs, step, total):
    pl.when(step == (3 * total) // 4)(lambda: ex.forward(refs))
    pl.when(step == total - 1)(lambda: ex.finish(refs))


def _class_layout(grads, classes):
    used = [0] * len(set(classes))
    offs, tiles = [], []
    for g, cl in zip(grads, classes):
        rows = g.shape[0] // N_DEV
        offs.append(used[cl])
        tiles.append(math.gcd(rows, used[cl]) if used[cl] else rows)
        used[cl] += rows
    heights = []
    for k, u in enumerate(used):
        step = math.lcm(*[t for t, cl in zip(tiles, classes) if cl == k])
        heights.append(-(-u // step) * step)
    return offs, tiles, used, heights


def _rs_to_sibling(grads, classes, *, name):
    n = len(grads)
    offs, _, used, heights = _class_layout(grads, classes)
    ncl = len(heights)
    cols = [next(g.shape[1] for g, cl in zip(grads, classes) if cl == k) for k in range(ncl)]

    def body(*refs):
        gs, land = refs[:n], refs[n:n + ncl]
        send_sems, recv_sems = refs[n + ncl:]
        x, y, c = _place()
        sib = (x, y, 1 - c)
        for p in range(4):
            for w in range(n):
                r = grads[w].shape[0] // N_DEV
                cl = classes[w]
                there = gs[w].at[pl.ds(pl.multiple_of((2 * p + 1 - c) * r, 8), r), :]
                pltpu.make_async_remote_copy(
                    src_ref=there, dst_ref=land[cl].at[p, pl.ds(offs[w], r), :],
                    send_sem=send_sems.at[cl * 4 + p], recv_sem=recv_sems.at[cl * 4 + p],
                    device_id=sib, device_id_type=MESH).start()
        for cl in range(ncl):
            for p in range(4):
                rows_used = land[cl].at[p, pl.ds(0, used[cl]), :]
                slab = pltpu.make_async_remote_copy(
                    src_ref=rows_used, dst_ref=rows_used,
                    send_sem=send_sems.at[cl * 4 + p], recv_sem=recv_sems.at[cl * 4 + p],
                    device_id=sib, device_id_type=MESH)
                slab.wait_send()
                slab.wait_recv()

    return pl.pallas_call(
        body, name=name,
        in_specs=[ANY] * n, out_specs=[ANY] * ncl,
        out_shape=[jax.ShapeDtypeStruct((4, heights[k], cols[k]), BF16) for k in range(ncl)],
        scratch_shapes=[pltpu.SemaphoreType.DMA((4 * ncl,))] * 2,
    )(*[_hbm(g) for g in grads])


def _all_reduce_small(v, *, name):
    R = v.shape[0]

    def body(v_ref, o_ref, buf, send_sems, recv_sems):
        x, y, c = _place()
        me = 4 * x + 2 * y + c
        buf[me] = v_ref[...]
        copies = []
        for k in range(1, N_DEV):
            fx, fy, fc = (k >> 2) & 1, (k >> 1) & 1, k & 1
            to = (x ^ fx, y ^ fy, c ^ fc)
            cp = pltpu.make_async_remote_copy(
                src_ref=v_ref, dst_ref=buf.at[me],
                send_sem=send_sems.at[k - 1], recv_sem=recv_sems.at[k - 1],
                device_id=to, device_id_type=MESH)
            cp.start()
            copies.append(cp)
        for k in range(1, N_DEV):
            fx, fy, fc = (k >> 2) & 1, (k >> 1) & 1, k & 1
            frm = 4 * (x ^ fx) + 2 * (y ^ fy) + (c ^ fc)
            pltpu.make_async_remote_copy(
                src_ref=v_ref, dst_ref=buf.at[frm],
                send_sem=send_sems.at[k - 1], recv_sem=recv_sems.at[k - 1],
                device_id=(x ^ fx, y ^ fy, c ^ fc), device_id_type=MESH).wait_recv()
        for cp in copies:
            cp.wait_send()
        acc = buf[0]
        for d in range(1, N_DEV):
            acc = acc + buf[d]
        o_ref[...] = acc

    vm = pl.BlockSpec(memory_space=pltpu.VMEM)
    return pl.pallas_call(
        body, name=name, in_specs=[vm], out_specs=vm,
        out_shape=jax.ShapeDtypeStruct((R, LANE), F32),
        scratch_shapes=[pltpu.VMEM((N_DEV, R, LANE), F32),
                        pltpu.SemaphoreType.DMA((N_DEV - 1,)), pltpu.SemaphoreType.DMA((N_DEV - 1,))],
        compiler_params=pltpu.CompilerParams(vmem_limit_bytes=VMEM_FLOOR),
    )(v)


def _rope_tables(positions):
    half = QK_ROPE // 2
    inv_freq = ROPE_BASE ** (-jnp.arange(half, dtype=F32) / half)
    ang = positions.astype(F32)[:, None] * inv_freq
    cos, sin = jnp.cos(ang), jnp.sin(ang)
    z = jnp.zeros_like(cos)
    z2 = jnp.zeros((positions.shape[0], LANE - QK_ROPE), F32)
    rc = jnp.concatenate([cos, cos, z2], axis=1)
    rs1 = jnp.concatenate([-sin, z, z2], axis=1)
    rs2 = jnp.concatenate([z, sin, z2], axis=1)
    return rc, rs1, rs2


def _block_diag(pool_w):
    G, pg, _ = pool_w.shape
    out = jnp.zeros((G * pg, G * pg), pool_w.dtype)
    for g in range(G):
        out = lax.dynamic_update_slice(out, pool_w[g], (g * pg, g * pg))
    return out


def kernel(x, mem, positions, ln_g, ln_b, ffn1_w13, ffn1_w2, w_in, pool_w, pool_scale, q_norm_g, w_uq, kv_norm_g, w_ukv, w_out, mem_wq, mem_wkv, mem_wo, ffn2_w13, ffn2_w2, loss_target, m_ln_g, m_ln_b, m_ffn1_w13, m_ffn1_w2, m_w_in, m_pool_w, m_pool_scale, m_q_norm_g, m_w_uq, m_kv_norm_g, m_w_ukv, m_w_out, m_mem_wq, m_mem_wkv, m_mem_wo, m_ffn2_w13, m_ffn2_w2, v_ln_g, v_ln_b, v_ffn1_w13, v_ffn1_w2, v_w_in, v_pool_w, v_pool_scale, v_q_norm_g, v_w_uq, v_kv_norm_g, v_w_ukv, v_w_out, v_mem_wq, v_mem_wkv, v_mem_wo, v_ffn2_w13, v_ffn2_w2):
    L = ln_g.shape[0]
    T, D = x.shape[1], x.shape[2]
    F = ffn1_w2.shape[1] * N_DEV
    PW = D // 4
    H = (D - PW) // V_HEAD
    DIN = w_in.shape[2]
    DINP = PW + Q_LORA + KV_LORA + LANE
    QW = QK_NOPE + QK_ROPE
    alpha = (2 * L) ** 0.25
    x2d = x.reshape(T, D)
    memb = mem.reshape(mem.shape[1], D).astype(BF16)
    target = loss_target.reshape(T, D)
    tabs = _rope_tables(positions.reshape(T))

    def shards_of(l):
        return dict(
            w13a=ffn1_w13[l].T[None].astype(BF16),
            w13b=ffn2_w13[l].T[None].astype(BF16),
            w2=jnp.stack([ffn1_w2[l], ffn2_w2[l]]).astype(BF16),
            wsq=jnp.stack([w_out[l], mem_wq[l], mem_wo[l]]).astype(BF16),
            wkvT=mem_wkv[l].T[None].astype(BF16),
            winp=jnp.pad(w_in[l], ((0, 0), (0, DINP - DIN)))[None].astype(BF16),
            wuqT=w_uq[l].T[None].astype(BF16),
            wukvT=w_ukv[l].T[None].astype(BF16),
        )

    MID = ("w2", "wsq", "wkvT", "winp", "wuqT", "wukvT")

    def finish_weights(g):
        wuq = jnp.pad(g["wuqT"].reshape(H, QW, Q_LORA), ((0, 0), (0, HEAD_PAD - QW), (0, 0)))
        return dict(g, w13T=[_interleave(g["w13a"], 1), _interleave(g["w13b"], 1)],
                    wuqT=wuq.reshape(1, H * HEAD_PAD, Q_LORA))

    names0 = ["w13a", "w13b", *MID]
    sh0 = shards_of(0)
    ln_shard = jnp.concatenate([ln_g.reshape(1, 4 * L, -1), ln_b.reshape(1, 4 * L, -1)], axis=1)
    got = _exchange_alone(_Gather([sh0[k] for k in names0] + [ln_shard]), name="ag_layer0")
    W = [finish_weights(dict(zip(names0, got[:-1])))]
    lnp = jnp.moveaxis(got[-1].reshape(N_DEV, 2, L, 4, D // N_DEV), 0, 3).reshape(2, L, 4, D)
    lng, lnb = lnp[0], lnp[1]
    wbd = [_block_diag(pool_w[l]).astype(BF16) for l in range(L)]

    def ffn_fwd(l, which, xres, xb, k, ride):
        h13, a, rode = _ffn_up(xb, W[l]["w13T"][which], 0, name=f"l{l}_ffn{which}_up", ride=ride)
        y = _mm(a, W[l]["w2"], lead=which, name=f"l{l}_ffn{which}_y", tk=F)
        xo, xob = _ln_fwd(xres, y, lng[l, k:k + 1], lnb[l, k:k + 1], alpha=alpha, s=0.5, name=f"l{l}_ln{k}")
        return dict(xres=xres, xb=xb, h13=h13, a=a, y=y), xo, xob, rode

    saved = []
    xres, xb = x2d, x2d.astype(BF16)
    for l in range(L):
        sv = {}
        nxt = shards_of(l + 1) if l + 1 < L else None
        Wl = W[l]
        sv["ffn1"], x1, x1b, got_a = ffn_fwd(l, 0, xres, xb, 0, _Gather([nxt["w13a"]]) if nxt else None)
        hin = _mm(x1b, Wl["winp"], lead=0, name=f"l{l}_hin")
        pscale = pool_scale[l].reshape(1, PW)
        gq, gkv = q_norm_g[l].reshape(1, Q_LORA), kv_norm_g[l].reshape(1, KV_LORA)
        ypool = _pool_fwd(hin, wbd[l], pscale, name=f"l{l}_pool")
        cqn, ckvn, kpe = _norms_fwd(hin, gq, gkv, pw=PW, name=f"l{l}_norms")
        qraw = _mm(cqn, Wl["wuqT"], lead=0, tb=True, name=f"l{l}_qraw")
        kv = _mm(ckvn, Wl["wukvT"], lead=0, tb=True, name=f"l{l}_kv")
        qh, kh, vh = _heads_fwd(qraw, kv, kpe, tabs, H=H, name=f"l{l}_heads")
        o, lse, got_mid = _flash_fwd(qh, kh, vh, H=H, name=f"l{l}_flash",
                                     ride=_Gather([nxt[k] for k in MID]) if nxt else None)
        cat = jnp.concatenate([ypool, o], axis=1)
        ymix = _mm(cat, Wl["wsq"], lead=0, name=f"l{l}_ymix")
        x2, x2b = _ln_fwd(x1, ymix, lng[l, 1:2], lnb[l, 1:2], alpha=alpha, s=1.0, name=f"l{l}_ln1")
        qm = _mm(x2b, Wl["wsq"], lead=1, out_dtype=BF16, name=f"l{l}_qm")
        kvm = _mm(memb, Wl["wkvT"], lead=0, tb=True, name=f"l{l}_kvm")
        km, vm = kvm[:, :D], kvm[:, D:]
        om = _mem_fwd(qm, km, vm, name=f"l{l}_memattn")
        ymem = _mm(om, Wl["wsq"], lead=2, name=f"l{l}_ymem")
        x3, x3b = _ln_fwd(x2, ymem, lng[l, 2:3], lnb[l, 2:3], alpha=alpha, s=1.0, name=f"l{l}_ln2")
        sv["ffn2"], x4, x4b, got_b = ffn_fwd(l, 1, x3, x3b, 3, _Gather([nxt["w13b"]]) if nxt else None)
        if nxt:
            W.append(finish_weights(dict(w13a=got_a[0], w13b=got_b[0], **dict(zip(MID, got_mid)))))
        sv.update(x1=x1, x1b=x1b, hin=hin, pscale=pscale, gq=gq, gkv=gkv, cqn=cqn, ckvn=ckvn,
                  qh=qh, kh=kh, vh=vh, lse=lse, cat=cat, ymix=ymix, x2=x2, x2b=x2b, qm=qm, km=km, vm=vm,
                  om=om, ymem=ymem)
        saved.append(sv)
        xres, xb = x4, x4b

    dx, loss_blk = _loss_head(xres, target, name="loss_head")
    loss = lax.psum(loss_blk[0, 0], ("x", "y", "c"))

    gW = {}
    gS = {}

    def ffn_bwd(l, which, sv, dx, k):
        tag = f"l{l}_ffn{which}"
        dxres, dyb, dg, db = _ln_bwd(sv["xres"], sv["y"], lng[l, k:k + 1], lnb[l, k:k + 1], dx,
                                     alpha=alpha, s=0.5, name=f"l{l}_ln{k}_bwd")
        gW[("w2", which, l)] = _mm(sv["a"], dyb, ta=True, out_dtype=BF16, name=f"{tag}_dw2", tn=D)
        dh = _ffn_down_bwd(dyb, W[l]["w2"], which, sv["h13"], name=f"{tag}_dh")
        dxn = _mm(dh, W[l]["w13T"][which], lead=0, add=dxres, name=f"{tag}_dx", tn=D)
        dw13 = _mm(dh, sv["xb"], ta=True, out_dtype=BF16, name=f"{tag}_dw13", tn=D)
        gW[("w13", which, l)] = _deinterleave(dw13, 0)
        gS[("ln_g", l, k)], gS[("ln_b", l, k)] = dg, db
        return dxn

    core = lax.axis_index("c").astype(jnp.int32).reshape(1)
    chip = (2 * lax.axis_index("x") + lax.axis_index("y")).astype(jnp.int32).reshape(1)
    gsh = {}

    def rs_first_level(l):
        keys = [("w13", 0, l), ("w13", 1, l), ("w2", 0, l), ("w2", 1, l), ("mem_wkv", l),
                ("w_out", l), ("mem_wq", l), ("mem_wo", l), ("w_in", l), ("w_uq", l), ("w_ukv", l)]
        classes = [0] * 8 + [1, 2, 3]
        garrs = []
        for key in keys:
            g = gW[key]
            if key[0] == "w_uq":
                g = g.reshape(H, HEAD_PAD, Q_LORA)[:, :QW, :].reshape(H * QW, Q_LORA)
            garrs.append(g)
        offs, tiles, used, _ = _class_layout(garrs, classes)
        parts = list(_rs_to_sibling(garrs, classes, name=f"l{l}_rs_sibling"))
        for w, (g, cl, off, tr) in enumerate(zip(garrs, classes, offs, tiles)):
            parts[cl] = _pair_sum(core, g, parts[cl], off, tr, name=f"l{l}_rs_pair_sum_{w}")
        return dict(l=l, keys=keys, garrs=garrs, classes=classes, offs=offs, used=used, parts=parts,
                    ex=_ChipExchange(parts, used))

    def rs_last_level(st, gathered):
        sums = [_quad_sum(chip, p, a, u, name=f"l{st['l']}_rs_quad_sum{k}")
                for k, (p, a, u) in enumerate(zip(st["parts"], gathered, st["used"]))]
        for key, g, cl, off in zip(st["keys"], st["garrs"], st["classes"], st["offs"]):
            gsh[key] = sums[cl][off:off + g.shape[0] // N_DEV, :]

    pending = None
    for l in reversed(range(L)):
        sv = saved[l]
        Wl = W[l]
        dx = ffn_bwd(l, 1, sv["ffn2"], dx, 3)
        dxres, dyb, dg, db = _ln_bwd(sv["x2"], sv["ymem"], lng[l, 2:3], lnb[l, 2:3], dx,
                                     alpha=alpha, s=1.0, name=f"l{l}_ln2_bwd")
        gS[("ln_g", l, 2)], gS[("ln_b", l, 2)] = dg, db
        dom = _mm(dyb, Wl["wsq"], lead=2, tb=True, out_dtype=BF16, name=f"l{l}_dom")
        gW[("mem_wo", l)] = _mm(sv["om"], dyb, ta=True, out_dtype=BF16, name=f"l{l}_dwo", tn=D)
        dqm, dkm, dvm = _mem_bwd(sv["qm"], sv["km"], sv["vm"], dom, name=f"l{l}_memattn_bwd")
        dx = _mm(dqm, Wl["wsq"], lead=1, tb=True, add=dxres, name=f"l{l}_dx2", tn=D)
        gW[("mem_wq", l)] = _mm(sv["x2b"], dqm, ta=True, out_dtype=BF16, name=f"l{l}_dwq", tn=D)
        dkvm = jnp.concatenate([dkm, dvm], axis=1).astype(BF16)
        gW[("mem_wkv", l)] = _mm(dkvm, memb, ta=True, out_dtype=BF16, name=f"l{l}_dwkv", tn=D)
        dxres, dyb, dg, db = _ln_bwd(sv["x1"], sv["ymix"], lng[l, 1:2], lnb[l, 1:2], dx,
                                     alpha=alpha, s=1.0, name=f"l{l}_ln1_bwd")
        gS[("ln_g", l, 1)], gS[("ln_b", l, 1)] = dg, db
        dcat = _mm(dyb, Wl["wsq"], lead=0, tb=True, name=f"l{l}_dcat", tn=D)
        gW[("w_out", l)] = _mm(sv["cat"], dyb, ta=True, out_dtype=BF16, name=f"l{l}_dwout", tn=D)
        dqh, dkh, dvh, rode = _flash_bwd(sv["qh"], sv["kh"], sv["vh"], sv["cat"], dcat, sv["lse"], H=H, pw=PW,
                                         name=f"l{l}_flash_bwd", ride=pending["ex"] if pending else None)
        if pending:
            rs_last_level(pending, rode)
        dqraw, dkv, dkpe = _heads_bwd(dqh, dkh, dvh, tabs, H=H, name=f"l{l}_heads_bwd")
        dcq = _mm(dqraw, Wl["wuqT"], lead=0, name=f"l{l}_dcq")
        gW[("w_uq", l)] = _mm(dqraw, sv["cqn"], ta=True, out_dtype=BF16, name=f"l{l}_dwuq")
        dckv = _mm(dkv, Wl["wukvT"], lead=0, name=f"l{l}_dckv")
        gW[("w_ukv", l)] = _mm(dkv, sv["ckvn"], ta=True, out_dtype=BF16, name=f"l{l}_dwukv")
        du, dwbd, dps = _pool_bwd(sv["hin"], dcat, wbd[l], sv["pscale"], name=f"l{l}_pool_bwd")
        dhin, dgq, dgkv = _norms_bwd(sv["hin"], sv["gq"], sv["gkv"], dcq, dckv, dkpe, du, pw=PW,
                                     name=f"l{l}_norms_bwd")
        pg = PW // len(POOL_WINDOWS)
        gS[("pool_w", l)] = jnp.stack([dwbd[g * pg:(g + 1) * pg, g * pg:(g + 1) * pg]
                                       for g in range(len(POOL_WINDOWS))])
        gS[("pool_scale", l)], gS[("q_norm_g", l)], gS[("kv_norm_g", l)] = dps, dgq, dgkv
        dx = _mm(dhin, Wl["winp"], lead=0, tb=True, add=dxres, name=f"l{l}_dx1", tn=D)
        gW[("w_in", l)] = _mm(sv["x1b"], dhin, ta=True, out_dtype=BF16, name=f"l{l}_dwin", tn=DINP)
        dx = ffn_bwd(l, 0, sv["ffn1"], dx, 0)
        pending = rs_first_level(l)
    rs_last_level(pending, _exchange_alone(pending["ex"], name="rs_chips_last"))
    grad_x = dx.reshape(1, T, D)

    small_keys = []
    for l in range(L):
        small_keys += [("pool_w", l), ("pool_scale", l), ("q_norm_g", l), ("kv_norm_g", l)]
        small_keys += [("ln_g", l, k) for k in range(4)] + [("ln_b", l, k) for k in range(4)]
    flat = jnp.concatenate([gS[k].reshape(-1) for k in small_keys])
    n_small = flat.shape[0]
    rows = -(-n_small // (8 * LANE)) * 8
    flat = jnp.pad(flat, (0, rows * LANE - n_small)).reshape(rows, LANE)
    red = _all_reduce_small(flat, name="ar_small").reshape(-1)
    gsm, pos = {}, 0
    for k in small_keys:
        size = math.prod(gS[k].shape)
        gsm[k] = red[pos:pos + size].reshape(gS[k].shape)
        pos += size

    me = 4 * lax.axis_index("x") + 2 * lax.axis_index("y") + lax.axis_index("c")
    dsh = D // N_DEV
    stack = lambda f: jnp.stack([f(l) for l in range(L)])
    g_ln_g = stack(lambda l: jnp.concatenate([gsm[("ln_g", l, k)] for k in range(4)], axis=0))
    g_ln_b = stack(lambda l: jnp.concatenate([gsm[("ln_b", l, k)] for k in range(4)], axis=0))
    grads = {
        "ln_g": lax.dynamic_slice_in_dim(g_ln_g, me * dsh, dsh, axis=2),
        "ln_b": lax.dynamic_slice_in_dim(g_ln_b, me * dsh, dsh, axis=2),
        "ffn1_w13": stack(lambda l: gsh[("w13", 0, l)].T),
        "ffn1_w2": stack(lambda l: gsh[("w2", 0, l)]),
        "w_in": stack(lambda l: gsh[("w_in", l)][:, :DIN]),
        "pool_w": stack(lambda l: gsm[("pool_w", l)]),
        "pool_scale": stack(lambda l: gsm[("pool_scale", l)].reshape(PW)),
        "q_norm_g": stack(lambda l: gsm[("q_norm_g", l)].reshape(Q_LORA)),
        "w_uq": stack(lambda l: gsh[("w_uq", l)].T),
        "kv_norm_g": stack(lambda l: gsm[("kv_norm_g", l)].reshape(KV_LORA)),
        "w_ukv": stack(lambda l: gsh[("w_ukv", l)].T),
        "w_out": stack(lambda l: gsh[("w_out", l)]),
        "mem_wq": stack(lambda l: gsh[("mem_wq", l)]),
        "mem_wkv": stack(lambda l: gsh[("mem_wkv", l)].T),
        "mem_wo": stack(lambda l: gsh[("mem_wo", l)]),
        "ffn2_w13": stack(lambda l: gsh[("w13", 1, l)].T),
        "ffn2_w2": stack(lambda l: gsh[("w2", 1, l)]),
    }

    names = ["ln_g", "ln_b", "ffn1_w13", "ffn1_w2", "w_in", "pool_w", "pool_scale", "q_norm_g", "w_uq",
             "kv_norm_g", "w_ukv", "w_out", "mem_wq", "mem_wkv", "mem_wo", "ffn2_w13", "ffn2_w2"]
    weights = dict(ln_g=ln_g, ln_b=ln_b, ffn1_w13=ffn1_w13, ffn1_w2=ffn1_w2, w_in=w_in, pool_w=pool_w,
                   pool_scale=pool_scale, q_norm_g=q_norm_g, w_uq=w_uq, kv_norm_g=kv_norm_g, w_ukv=w_ukv,
                   w_out=w_out, mem_wq=mem_wq, mem_wkv=mem_wkv, mem_wo=mem_wo, ffn2_w13=ffn2_w13,
                   ffn2_w2=ffn2_w2)
    ms = dict(ln_g=m_ln_g, ln_b=m_ln_b, ffn1_w13=m_ffn1_w13, ffn1_w2=m_ffn1_w2, w_in=m_w_in, pool_w=m_pool_w,
              pool_scale=m_pool_scale, q_norm_g=m_q_norm_g, w_uq=m_w_uq, kv_norm_g=m_kv_norm_g,
              w_ukv=m_w_ukv, w_out=m_w_out, mem_wq=m_mem_wq, mem_wkv=m_mem_wkv, mem_wo=m_mem_wo,
              ffn2_w13=m_ffn2_w13, ffn2_w2=m_ffn2_w2)
    vs = dict(ln_g=v_ln_g, ln_b=v_ln_b, ffn1_w13=v_ffn1_w13, ffn1_w2=v_ffn1_w2, w_in=v_w_in, pool_w=v_pool_w,
              pool_scale=v_pool_scale, q_norm_g=v_q_norm_g, w_uq=v_w_uq, kv_norm_g=v_kv_norm_g,
              w_ukv=v_w_ukv, w_out=v_w_out, mem_wq=v_mem_wq, mem_wkv=v_mem_wkv, mem_wo=v_mem_wo,
              ffn2_w13=v_ffn2_w13, ffn2_w2=v_ffn2_w2)
    deltas, new_m, new_v = [], [], []
    for nme in names:
        d, mn, vn = _adamw(weights[nme], grads[nme], ms[nme], vs[nme], name=f"adamw_{nme}")
        deltas.append(d)
        new_m.append(mn)
        new_v.append(vn)
    return (loss, grad_x, *[grads[nme] for nme in names], *deltas, *new_m, *new_v)
```

```python
import functools
import math

import jax
import jax.numpy as jnp
from jax import lax
from jax.experimental import pallas as pl
from jax.experimental.pallas import tpu as pltpu

F32 = jnp.float32
BF16 = jnp.bfloat16
MESH = pl.DeviceIdType.MESH

CHUNK = 64
MEM_HEADS = 4
POOL_WINDOWS = (2, 4, 8, 16)
QK_NOPE = 128
QK_ROPE = 64
V_HEAD = 128
Q_LORA = 256
KV_LORA = 128
ROPE_BASE = 10000.0
LN_EPS = 1e-5
RMS_EPS = 1e-6
NEG_INF = -1e30
ADAM_LR = 0.001
ADAM_B1 = 0.9
ADAM_B2 = 0.999
ADAM_EPS = 1e-08
ADAM_WD = 0.01
ADAM_STEP = 10

N_DEV = 8
LANE = 128
HEAD_PAD = 2 * LANE
POOL_HALO = 16
VMEM_CAP = 56 * 1024 * 1024
VMEM_FLOOR = 32 * 1024 * 1024


def _tile(n, pref, mult):
    t = (min(pref, n) // mult) * mult
    while t >= mult:
        if n % t == 0:
            return t
        t -= mult
    return n


def _params(sem, est_bytes):
    limit = int(min(max(2 * est_bytes + (8 << 20), VMEM_FLOOR), VMEM_CAP))
    return pltpu.CompilerParams(dimension_semantics=sem, vmem_limit_bytes=limit)


def _nbytes(shape, dtype):
    return math.prod(shape) * jnp.dtype(dtype).itemsize


def _hbm(x):
    return pltpu.with_memory_space_constraint(x, pltpu.HBM)


def _dg(a, b, ca, cb):
    return lax.dot_general(a.astype(BF16), b.astype(BF16), (((ca,), (cb,)), ((), ())),
                           preferred_element_type=F32)


@jax.custom_vjp
def _bdot_nn(a, b):
    return _dg(a, b, 1, 0)


def _bdot_nn_fwd(a, b):
    return _dg(a, b, 1, 0), (a, b)


def _bdot_nn_bwd(res, ct):
    a, b = res
    return _dg(ct, b, 1, 1).astype(a.dtype), _dg(a, ct, 0, 0).astype(b.dtype)


_bdot_nn.defvjp(_bdot_nn_fwd, _bdot_nn_bwd)


@jax.custom_vjp
def _bdot_nt(a, b):
    return _dg(a, b, 1, 1)


def _bdot_nt_fwd(a, b):
    return _dg(a, b, 1, 1), (a, b)


def _bdot_nt_bwd(res, ct):
    a, b = res
    return _dg(ct, b, 1, 0).astype(a.dtype), _dg(ct, a, 0, 0).astype(b.dtype)


_bdot_nt.defvjp(_bdot_nt_fwd, _bdot_nt_bwd)


@functools.partial(jax.custom_vjp, nondiff_argnums=(1,))
def _lane_roll(x, shift):
    return pltpu.roll(x, shift % x.shape[1], axis=1)


def _lane_roll_fwd(x, shift):
    return _lane_roll(x, shift), None


def _lane_roll_bwd(shift, _, ct):
    return (_lane_roll(ct, -shift),)


_lane_roll.defvjp(_lane_roll_fwd, _lane_roll_bwd)


@functools.partial(jax.custom_vjp, nondiff_argnums=(1, 2))
def _cols(x, lo, hi):
    return x[:, lo:hi]


def _cols_fwd(x, lo, hi):
    return x[:, lo:hi], x.shape[1]


def _cols_bwd(lo, hi, width, ct):
    parts = []
    if lo > 0:
        parts.append(jnp.zeros((ct.shape[0], lo), ct.dtype))
    parts.append(ct)
    if hi < width:
        parts.append(jnp.zeros((ct.shape[0], width - hi), ct.dtype))
    return (jnp.concatenate(parts, axis=1) if len(parts) > 1 else ct,)


_cols.defvjp(_cols_fwd, _cols_bwd)


MM_VMEM_BUDGET = 20 * 1024 * 1024


def _mm(a, b, *, name, ta=False, tb=False, out_dtype=F32, lead=None, add=None, add_scale=1.0,
        tm=1024, tn=1024, tk=2816):
    if ta:
        K, M = a.shape
    else:
        M, K = a.shape
    bshape = b.shape[1:] if lead is not None else b.shape
    if tb:
        N, Kb = bshape
    else:
        Kb, N = bshape
    assert K == Kb, (name, a.shape, b.shape)

    def blocks(tm, tn, tk):
        tm = _tile(M, tm, LANE if ta else 16)
        tn = _tile(N, tn, LANE)
        tk = _tile(K, tk, LANE)
        nbytes = (tm * tk * a.dtype.itemsize + tk * tn * b.dtype.itemsize
                  + tm * tn * (jnp.dtype(out_dtype).itemsize + (4 if K // tk > 1 else 0)
                               + (add.dtype.itemsize if add is not None else 0)))
        return tm, tn, tk, nbytes

    tm, tn, tk, est = blocks(tm, tn, tk)
    for shrink in ("m", "k", "m", "k", "n"):
        if est <= MM_VMEM_BUDGET:
            break
        if shrink == "m":
            tm, tn, tk, est = blocks(max(tm // 2, LANE), tn, tk)
        elif shrink == "k":
            tm, tn, tk, est = blocks(tm, tn, max(tk // 2, LANE))
        else:
            tm, tn, tk, est = blocks(tm, max(tn // 2, LANE), tk)
    nk = K // tk
    ca = 0 if ta else 1
    cb = 1 if tb else 0

    def body(*refs):
        a_ref, b_ref = refs[0], refs[1]
        add_ref = refs[2] if add is not None else None
        o_ref = refs[3] if add is not None else refs[2]

        def finish(r):
            if add_ref is not None:
                r = r + add_scale * add_ref[...].astype(F32)
            o_ref[...] = r.astype(o_ref.dtype)

        if nk == 1:
            finish(_dg(a_ref[...], b_ref[...], ca, cb))
            return
        acc_ref = refs[-1]
        k = pl.program_id(2)

        @pl.when(k == 0)
        def _():
            acc_ref[...] = jnp.zeros_like(acc_ref)

        acc_ref[...] += _dg(a_ref[...], b_ref[...], ca, cb)

        @pl.when(k == nk - 1)
        def _():
            finish(acc_ref[...])

    a_blk = (tk, tm) if ta else (tm, tk)
    a_map = (lambda i, j, k: (k, i)) if ta else (lambda i, j, k: (i, k))
    b_blk = (tn, tk) if tb else (tk, tn)
    if lead is None:
        b_map = (lambda i, j, k: (j, k)) if tb else (lambda i, j, k: (k, j))
        b_spec = pl.BlockSpec(b_blk, b_map)
    else:
        b_map = (lambda i, j, k: (lead, j, k)) if tb else (lambda i, j, k: (lead, k, j))
        b_spec = pl.BlockSpec((None,) + b_blk, b_map)
    in_specs = [pl.BlockSpec(a_blk, a_map), b_spec]
    args = [a, b]
    if add is not None:
        in_specs.append(pl.BlockSpec((tm, tn), lambda i, j, k: (i, j)))
        args.append(add)
    return pl.pallas_call(
        body, name=name,
        grid=(M // tm, N // tn, nk),
        in_specs=in_specs,
        out_specs=pl.BlockSpec((tm, tn), lambda i, j, k: (i, j)),
        out_shape=jax.ShapeDtypeStruct((M, N), out_dtype),
        scratch_shapes=[pltpu.VMEM((tm, tn), F32)] if nk > 1 else [],
        compiler_params=_params(("parallel", "parallel", "arbitrary"), est + tm * tn * 4),
    )(*[_hbm(v) for v in args])


def _rowwise(fn, tiles, params, tile_outs, acc_outs=(), *, tm, name):
    tile_arrays, tile_specs = [], []
    for t in tiles:
        if isinstance(t, tuple):
            tile_arrays.append(t[0])
            tile_specs.append(t[1])
        else:
            tile_arrays.append(t)
            tile_specs.append(pl.BlockSpec((tm, t.shape[1]), lambda i: (i, 0)))
    T = tile_arrays[0].shape[0]
    nt, np_, nto, nao = len(tile_arrays), len(params), len(tile_outs), len(acc_outs)

    def body(*refs):
        i = pl.program_id(0)
        tvals = [r[...] for r in refs[:nt]]
        pvals = [r[...] for r in refs[nt:nt + np_]]
        to_refs = refs[nt + np_:nt + np_ + nto]
        ao_refs = refs[nt + np_ + nto:]
        touts, aouts = fn(i, tvals, pvals)
        for r, v in zip(to_refs, touts):
            r[...] = v.astype(r.dtype)
        if nao:
            @pl.when(i == 0)
            def _():
                for r in ao_refs:
                    r[...] = jnp.zeros_like(r)
            for r, v in zip(ao_refs, aouts):
                r[...] += v.astype(r.dtype)

    in_specs = tile_specs + [pl.BlockSpec(p.shape, lambda i: (0, 0)) for p in params]
    out_specs = [pl.BlockSpec((tm, c), lambda i: (i, 0)) for c, _ in tile_outs]
    out_specs += [pl.BlockSpec(s, lambda i: (0, 0)) for s, _ in acc_outs]
    out_shape = [jax.ShapeDtypeStruct((T, c), d) for c, d in tile_outs]
    out_shape += [jax.ShapeDtypeStruct(s, d) for s, d in acc_outs]
    width = sum(s.block_shape[-1] for s in tile_specs) + sum(c for c, _ in tile_outs)
    est = 6 * tm * width * 4 + sum(_nbytes(p.shape, F32) for p in params) * 4
    return pl.pallas_call(
        body, name=name, grid=(T // tm,),
        in_specs=in_specs, out_specs=out_specs, out_shape=out_shape,
        compiler_params=_params(("arbitrary",) if nao else ("parallel",), est),
    )(*[_hbm(v) for v in tile_arrays], *params)


def _ln_fn(alpha, s, xres, y, g, b):
    z = alpha * xres.astype(F32) + s * y.astype(F32)
    mu = jnp.mean(z, axis=-1, keepdims=True)
    zc = z - mu
    var = jnp.mean(zc * zc, axis=-1, keepdims=True)
    return zc * lax.rsqrt(var + LN_EPS) * g + b


def _mm_ln(a, b, lead, xres, g, bias, *, alpha, s, name):
    M, K = a.shape
    N = b.shape[2]
    tm = _tile(M, 256, 16)

    def body(a_ref, b_ref, x_ref, g_ref, bias_ref, y_ref, xo_ref, xb_ref):
        y = _dg(a_ref[...], b_ref[...], 1, 0)
        y_ref[...] = y
        out = _ln_fn(alpha, s, x_ref[...], y, g_ref[...], bias_ref[...])
        xo_ref[...] = out
        xb_ref[...] = out.astype(BF16)

    row = pl.BlockSpec((tm, N), lambda i: (i, 0))
    vec = pl.BlockSpec((1, N), lambda i: (0, 0))
    est = tm * K * 2 + K * N * 2 + tm * N * (4 + 4 + 4 + 2 + 8)
    return pl.pallas_call(
        body, name=name, grid=(M // tm,),
        in_specs=[pl.BlockSpec((tm, K), lambda i: (i, 0)), pl.BlockSpec((None, K, N), lambda i: (lead, 0, 0)),
                  row, vec, vec],
        out_specs=[row, row, row],
        out_shape=[jax.ShapeDtypeStruct((M, N), F32), jax.ShapeDtypeStruct((M, N), F32),
                   jax.ShapeDtypeStruct((M, N), BF16)],
        compiler_params=_params(("parallel",), est),
    )(_hbm(a), _hbm(b), _hbm(xres), g, bias)


def _ln_bwd(xres, y, g, dout, *, alpha, s, name):
    T, D = xres.shape
    tm = _tile(T, 256, 16)

    def body(x_ref, y_ref, d_ref, g_ref, dx_ref, dy_ref, dg_ref, db_ref):
        @pl.when(pl.program_id(0) == 0)
        def _():
            dg_ref[...] = jnp.zeros_like(dg_ref)
            db_ref[...] = jnp.zeros_like(db_ref)

        z = alpha * x_ref[...] + s * y_ref[...]
        zc = z - jnp.mean(z, axis=-1, keepdims=True)
        r = lax.rsqrt(jnp.mean(zc * zc, axis=-1, keepdims=True) + LN_EPS)
        xh = zc * r
        d = d_ref[...]
        dxh = d * g_ref[...]
        dz = r * (dxh - jnp.mean(dxh, axis=-1, keepdims=True) - xh * jnp.mean(dxh * xh, axis=-1, keepdims=True))
        dx_ref[...] = alpha * dz
        dy_ref[...] = (s * dz).astype(dy_ref.dtype)
        dg_ref[...] += jnp.sum(d * xh, axis=0, keepdims=True)
        db_ref[...] += jnp.sum(d, axis=0, keepdims=True)

    row = pl.BlockSpec((tm, D), lambda i: (i, 0))
    vec = pl.BlockSpec((1, D), lambda i: (0, 0))
    return pl.pallas_call(
        body, name=name, grid=(T // tm,),
        in_specs=[row, row, row, vec], out_specs=[row, row, vec, vec],
        out_shape=[jax.ShapeDtypeStruct((T, D), F32), jax.ShapeDtypeStruct((T, D), BF16),
                   jax.ShapeDtypeStruct((1, D), F32), jax.ShapeDtypeStruct((1, D), F32)],
        compiler_params=_params(("arbitrary",), 12 * tm * D * 4),
    )(_hbm(xres), _hbm(y), _hbm(dout), g)


FFN_TILE = 256


def _interleave(w, axis):
    n = w.shape[axis] // (2 * FFN_TILE)
    shp = w.shape[:axis] + (2, n, FFN_TILE) + w.shape[axis + 1:]
    return jnp.swapaxes(w.reshape(shp), axis, axis + 1).reshape(w.shape)


def _deinterleave(w, axis):
    n = w.shape[axis] // (2 * FFN_TILE)
    shp = w.shape[:axis] + (n, 2, FFN_TILE) + w.shape[axis + 1:]
    return jnp.swapaxes(w.reshape(shp), axis, axis + 1).reshape(w.shape)


def _ffn_up(xb, w13t, lead, *, name, ride=None):
    T, D = xb.shape
    F = w13t.shape[1] // 2
    tc = FFN_TILE
    tm = _tile(T, 1024, 16)

    def body(x_ref, w_ref, h_ref, a_ref):
        h = _dg(x_ref[...], w_ref[...], 1, 1)
        g, u = h[:, :tc], h[:, tc:]
        h_ref[...] = h.astype(h_ref.dtype)
        a_ref[...] = (g * jax.nn.sigmoid(g) * u).astype(a_ref.dtype)

    est = (tm * D + 2 * tc * D + 3 * tm * tc) * 2 + 3 * tm * tc * 4
    (h13, a), gathered = _host_call(
        body, name=name, grid=(T // tm, F // tc),
        in_specs=[pl.BlockSpec((tm, D), lambda i, j: (i, 0)),
                  pl.BlockSpec((None, 2 * tc, D), lambda i, j: (lead, j, 0))],
        out_specs=[pl.BlockSpec((tm, 2 * tc), lambda i, j: (i, j)),
                   pl.BlockSpec((tm, tc), lambda i, j: (i, j))],
        out_shape=[jax.ShapeDtypeStruct((T, 2 * F), BF16), jax.ShapeDtypeStruct((T, F), BF16)],
        args=[_hbm(xb), _hbm(w13t)], sem=("parallel", "parallel"), est=est, ride=ride)
    return h13, a, gathered


def _ffn_down_bwd(dyb, w2, lead, h13, *, name):
    T, D = dyb.shape
    F = w2.shape[1]
    tc = FFN_TILE
    tm = _tile(T, 1024, 16)

    def body(dy_ref, w_ref, h_ref, dh_ref):
        d = _dg(dy_ref[...], w_ref[...], 1, 1)
        h = h_ref[...].astype(F32)
        g, u = h[:, :tc], h[:, tc:]
        sig = jax.nn.sigmoid(g)
        gs = g * sig
        dh_ref[...] = jnp.concatenate([d * u * (sig + gs * (1.0 - sig)), d * gs], axis=1).astype(dh_ref.dtype)

    est = (tm * D + tc * D + 4 * tm * tc) * 2 + 6 * tm * tc * 4
    return pl.pallas_call(
        body, name=name, grid=(T // tm, F // tc),
        in_specs=[pl.BlockSpec((tm, D), lambda i, j: (i, 0)),
                  pl.BlockSpec((None, tc, D), lambda i, j: (lead, j, 0)),
                  pl.BlockSpec((tm, 2 * tc), lambda i, j: (i, j))],
        out_specs=pl.BlockSpec((tm, 2 * tc), lambda i, j: (i, j)),
        out_shape=jax.ShapeDtypeStruct((T, 2 * F), BF16),
        compiler_params=_params(("parallel", "parallel"), est),
    )(_hbm(dyb), _hbm(w2), _hbm(h13))


def _pool_select(parts, pw):
    pg = pw // len(POOL_WINDOWS)
    grp = lax.broadcasted_iota(jnp.int32, parts[0].shape, 1) // pg
    out = parts[3]
    for g in (2, 1, 0):
        out = jnp.where(grp == g, parts[g], out)
    return out


def _pool_count(t0, rows, pw):
    pg = pw // len(POOL_WINDOWS)
    grp = lax.broadcasted_iota(jnp.int32, (rows, pw), 1) // pg
    win = jnp.where(grp == 0, POOL_WINDOWS[0],
                    jnp.where(grp == 1, POOL_WINDOWS[1],
                              jnp.where(grp == 2, POOL_WINDOWS[2], POOL_WINDOWS[3])))
    t = t0 + lax.broadcasted_iota(jnp.int32, (rows, pw), 0)
    return jnp.minimum(t + 1, win).astype(F32)


def _window_sums(ext, up):
    n = ext.shape[0]
    sums, cur, k = [], ext, 1
    for _ in POOL_WINDOWS:
        cur = cur + pltpu.roll(cur, (n - k) if up else k, axis=0)
        sums.append(cur)
        k *= 2
    return sums


def _pool_delta(u, halo, t0):
    tm, pw = u.shape
    ext = jnp.concatenate([halo, u], axis=0)
    sums = [s[POOL_HALO:, :] for s in _window_sums(ext, up=False)]
    return _pool_select(sums, pw) / _pool_count(t0, tm, pw) - u


def _pool_fwd(hin, wbd, scale, *, name):
    T = hin.shape[0]
    pw = wbd.shape[0]
    tm = _tile(T, 256, POOL_HALO)
    per = tm // POOL_HALO

    def body(u_ref, halo_ref, w_ref, s_ref, y_ref):
        i = pl.program_id(0)
        halo = jnp.where(i > 0, halo_ref[...], 0.0)
        d = _pool_delta(u_ref[...], halo, i * tm)
        y_ref[...] = (_dg(d, w_ref[...], 1, 0) * s_ref[...]).astype(y_ref.dtype)

    return pl.pallas_call(
        body, name=name, grid=(T // tm,),
        in_specs=[pl.BlockSpec((tm, pw), lambda i: (i, 0)),
                  pl.BlockSpec((POOL_HALO, pw), lambda i: (jnp.maximum(i * per - 1, 0), 0)),
                  pl.BlockSpec((pw, pw), lambda i: (0, 0)),
                  pl.BlockSpec((1, pw), lambda i: (0, 0))],
        out_specs=pl.BlockSpec((tm, pw), lambda i: (i, 0)),
        out_shape=jax.ShapeDtypeStruct((T, pw), BF16),
        compiler_params=_params(("parallel",), 16 * tm * pw * 4),
    )(_hbm(hin), _hbm(hin), wbd, scale)


def _pool_bwd(hin, dcat, wbd, scale, *, name):
    T = hin.shape[0]
    pw = wbd.shape[0]
    tm = _tile(T, 256, POOL_HALO)
    per = tm // POOL_HALO
    nt = T // tm

    def body(u_ref, halo_ref, dy_ref, dyn_ref, w_ref, s_ref, du_ref, dw_ref, ds_ref):
        i = pl.program_id(0)

        @pl.when(i == 0)
        def _():
            dw_ref[...] = jnp.zeros_like(dw_ref)
            ds_ref[...] = jnp.zeros_like(ds_ref)

        halo = jnp.where(i > 0, halo_ref[...], 0.0)
        d = _pool_delta(u_ref[...], halo, i * tm)
        w = w_ref[...]
        sc = s_ref[...]
        dy = dy_ref[...]
        dyn = jnp.where(i < nt - 1, dyn_ref[...], 0.0)
        ds_ref[...] += jnp.sum(dy * _dg(d, w, 1, 0), axis=0, keepdims=True)
        dys = dy * sc
        dw_ref[...] += _dg(d, dys, 0, 0)
        dys_ext = jnp.concatenate([dys, dyn * sc], axis=0)
        dd_ext = _dg(dys_ext, w, 1, 1)
        ddp = dd_ext / _pool_count(i * tm, tm + POOL_HALO, pw)
        sums = [s[:tm, :] for s in _window_sums(ddp, up=True)]
        du_ref[...] = _pool_select(sums, pw) - dd_ext[:tm, :]

    return pl.pallas_call(
        body, name=name, grid=(nt,),
        in_specs=[pl.BlockSpec((tm, pw), lambda i: (i, 0)),
                  pl.BlockSpec((POOL_HALO, pw), lambda i: (jnp.maximum(i * per - 1, 0), 0)),
                  pl.BlockSpec((tm, pw), lambda i: (i, 0)),
                  pl.BlockSpec((POOL_HALO, pw), lambda i: (jnp.minimum((i + 1) * per, nt * per - 1), 0)),
                  pl.BlockSpec((pw, pw), lambda i: (0, 0)),
                  pl.BlockSpec((1, pw), lambda i: (0, 0))],
        out_specs=[pl.BlockSpec((tm, pw), lambda i: (i, 0)),
                   pl.BlockSpec((pw, pw), lambda i: (0, 0)),
                   pl.BlockSpec((1, pw), lambda i: (0, 0))],
        out_shape=[jax.ShapeDtypeStruct((T, pw), F32),
                   jax.ShapeDtypeStruct((pw, pw), F32),
                   jax.ShapeDtypeStruct((1, pw), F32)],
        compiler_params=_params(("arbitrary",), 24 * tm * pw * 4),
    )(_hbm(hin), _hbm(hin), _hbm(dcat), _hbm(dcat), wbd, scale)


def _rms(x, g):
    return x * lax.rsqrt(jnp.mean(x * x, axis=-1, keepdims=True) + RMS_EPS) * g


def _norms_fn(pw, h, gq, gkv):
    o1 = pw + Q_LORA
    o2 = o1 + KV_LORA
    return (_rms(_cols(h, pw, o1), gq), _rms(_cols(h, o1, o2), gkv), _cols(h, o2, h.shape[1]))


def _norms_fwd(hin, gq, gkv, *, pw, name):
    tm = _tile(hin.shape[0], 256, 16)

    def fn(i, tv, pv):
        return _norms_fn(pw, tv[0], pv[0], pv[1]), ()

    return _rowwise(fn, [hin], [gq, gkv], [(Q_LORA, BF16), (KV_LORA, BF16), (LANE, F32)], tm=tm, name=name)


def _norms_bwd(hin, gq, gkv, dcq, dckv, dkpe, du, *, pw, name):
    tm = _tile(hin.shape[0], 256, 16)
    dinp = hin.shape[1]

    def fn(i, tv, pv):
        _, vjp = jax.vjp(functools.partial(_norms_fn, pw), tv[0], pv[0], pv[1])
        dh, dgq, dgkv = vjp((tv[1].astype(F32), tv[2].astype(F32), tv[3].astype(F32)))
        dh = jnp.concatenate([tv[4], dh[:, pw:]], axis=1)
        return (dh,), (dgq, dgkv)

    return _rowwise(fn, [hin, dcq, dckv, dkpe, du], [gq, gkv], [(dinp, BF16)],
                    [((1, Q_LORA), F32), ((1, KV_LORA), F32)], tm=tm, name=name)


def _heads_fn(H, qraw, kv, kpe, rc, rs1, rs2):
    half = QK_ROPE // 2
    scale = (QK_NOPE + QK_ROPE) ** -0.5

    def rope(blk):
        return blk * rc + _lane_roll(blk, -half) * rs1 + _lane_roll(blk, half) * rs2

    krot = rope(kpe)
    qs, ks, vs = [], [], []
    for h in range(H):
        lo = h * HEAD_PAD
        qs += [_cols(qraw, lo, lo + LANE) * scale, rope(_cols(qraw, lo + LANE, lo + HEAD_PAD)) * scale]
        ks += [_cols(kv, lo, lo + LANE), krot]
        vs += [_cols(kv, lo + LANE, lo + HEAD_PAD)]
    return jnp.concatenate(qs, axis=1), jnp.concatenate(ks, axis=1), jnp.concatenate(vs, axis=1)


def _heads_fwd(qraw, kv, kpe, tabs, *, H, name):
    tm = _tile(qraw.shape[0], 256, 16)

    def fn(i, tv, pv):
        return _heads_fn(H, *tv), ()

    return _rowwise(fn, [qraw, kv, kpe, *tabs], [],
                    [(H * HEAD_PAD, BF16), (H * HEAD_PAD, BF16), (H * V_HEAD, BF16)], tm=tm, name=name)


def _heads_bwd(dq, dk, dv, tabs, *, H, name):
    tm = _tile(dq.shape[0], 256, 16)

    def fn(i, tv, pv):
        z = jnp.zeros((tm, H * HEAD_PAD), F32)
        zk = jnp.zeros((tm, LANE), F32)
        rc, rs1, rs2 = tv[3], tv[4], tv[5]
        _, vjp = jax.vjp(lambda a, b, c: _heads_fn(H, a, b, c, rc, rs1, rs2), z, z, zk)
        return vjp((tv[0], tv[1], tv[2])), ()

    return _rowwise(fn, [dq, dk, dv, *tabs], [],
                    [(H * HEAD_PAD, BF16), (H * HEAD_PAD, BF16), (LANE, F32)], tm=tm, name=name)


def _diag_mask(t):
    r = lax.broadcasted_iota(jnp.int32, (t, t), 0) // CHUNK
    c = lax.broadcasted_iota(jnp.int32, (t, t), 1) // CHUNK
    return r >= c


def _flash_fwd(qh, kh, vh, *, H, name, ride=None):
    T = qh.shape[0]
    t = _tile(T, 512, CHUNK)

    def body(q_ref, k_ref, v_ref, o_ref, lse_ref):
        i = pl.program_id(1)
        q = q_ref[...]

        def blk(j, carry, masked):
            m, l, acc = carry
            rows = pl.ds(pl.multiple_of(j * t, t), t)
            s = _dg(q, k_ref[rows, :], 1, 1)
            if masked:
                s = jnp.where(_diag_mask(t), s, NEG_INF)
            mn = jnp.maximum(m, jnp.max(s, axis=1, keepdims=True))
            p = jnp.exp(s - mn)
            corr = jnp.exp(m - mn)
            l = corr * l + jnp.sum(p, axis=1, keepdims=True)
            acc = corr * acc + _dg(p, v_ref[rows, :], 1, 0)
            return mn, l, acc

        init = (jnp.full((t, 1), NEG_INF, F32), jnp.zeros((t, 1), F32), jnp.zeros((t, V_HEAD), F32))
        carry = lax.fori_loop(0, i, lambda j, c: blk(j, c, False), init)
        m, l, acc = blk(i, carry, True)
        o_ref[...] = (acc / l).astype(o_ref.dtype)
        lse_ref[...] = jnp.broadcast_to(m + jnp.log(l), (t, V_HEAD))

    est = 2 * T * (HEAD_PAD + V_HEAD) * 2 + 8 * t * t * 4
    (o, lse), gathered = _host_call(
        body, name=name, grid=(H, T // t),
        in_specs=[pl.BlockSpec((t, HEAD_PAD), lambda h, i: (i, h)),
                  pl.BlockSpec((T, HEAD_PAD), lambda h, i: (0, h)),
                  pl.BlockSpec((T, V_HEAD), lambda h, i: (0, h))],
        out_specs=[pl.BlockSpec((t, V_HEAD), lambda h, i: (i, h)),
                   pl.BlockSpec((t, V_HEAD), lambda h, i: (i, h))],
        out_shape=[jax.ShapeDtypeStruct((T, H * V_HEAD), BF16),
                   jax.ShapeDtypeStruct((T, H * V_HEAD), F32)],
        args=[_hbm(qh), _hbm(kh), _hbm(vh)], sem=("parallel", "parallel"), est=est, ride=ride)
    return o, lse, gathered


def _flash_bwd(qh, kh, vh, cat, dcat, lse, *, H, pw, name, ride=None):
    T = qh.shape[0]
    t = _tile(T, 512, CHUNK)
    nb = T // t
    off = pw // V_HEAD

    def body(q_ref, k_ref, v_ref, o_ref, do_ref, lse_ref, dq_ref, dk_ref, dv_ref):
        j = pl.program_id(1)

        @pl.when(j == 0)
        def _():
            dq_ref[...] = jnp.zeros_like(dq_ref)

        kj = k_ref[...]
        vj = v_ref[...]

        def blk(i, carry, masked):
            dk, dv = carry
            rows = pl.ds(pl.multiple_of(i * t, t), t)
            qi = q_ref[rows, :]
            doi = do_ref[rows, :]
            oi = o_ref[rows, :].astype(F32)
            lsei = lse_ref[rows, :][:, :1]
            s = _dg(qi, kj, 1, 1)
            if masked:
                s = jnp.where(_diag_mask(t), s, NEG_INF)
            p = jnp.exp(s - lsei)
            dv = dv + _dg(p, doi, 0, 0)
            dp = _dg(doi, vj, 1, 1)
            di = jnp.sum(doi * oi, axis=1, keepdims=True)
            ds = p * (dp - di)
            dk = dk + _dg(ds, qi, 0, 0)
            dq_ref[rows, :] += _dg(ds, kj, 1, 0)
            return dk, dv

        carry = blk(j, (jnp.zeros((t, HEAD_PAD), F32), jnp.zeros((t, V_HEAD), F32)), True)
        dk, dv = lax.fori_loop(j + 1, nb, lambda i, c: blk(i, c, False), carry)
        dk_ref[...] = dk
        dv_ref[...] = dv

    est = T * (HEAD_PAD * 2 + V_HEAD * 2 + V_HEAD * 4 + V_HEAD * 4 + HEAD_PAD * 4) + 10 * t * t * 4
    (dq, dk, dv), gathered = _host_call(
        body, name=name, grid=(H, nb),
        in_specs=[pl.BlockSpec((T, HEAD_PAD), lambda h, j: (0, h)),
                  pl.BlockSpec((t, HEAD_PAD), lambda h, j: (j, h)),
                  pl.BlockSpec((t, V_HEAD), lambda h, j: (j, h)),
                  pl.BlockSpec((T, V_HEAD), lambda h, j: (0, off + h)),
                  pl.BlockSpec((T, V_HEAD), lambda h, j: (0, off + h)),
                  pl.BlockSpec((T, V_HEAD), lambda h, j: (0, h))],
        out_specs=[pl.BlockSpec((T, HEAD_PAD), lambda h, j: (0, h)),
                   pl.BlockSpec((t, HEAD_PAD), lambda h, j: (j, h)),
                   pl.BlockSpec((t, V_HEAD), lambda h, j: (j, h))],
        out_shape=[jax.ShapeDtypeStruct((T, H * HEAD_PAD), F32),
                   jax.ShapeDtypeStruct((T, H * HEAD_PAD), F32),
                   jax.ShapeDtypeStruct((T, H * V_HEAD), F32)],
        args=[_hbm(v) for v in (qh, kh, vh, cat, dcat, lse)], sem=("arbitrary", "arbitrary"), est=est,
        ride=ride)
    return dq, dk, dv, gathered


def _mem_fn(q, k, v):
    hd = q.shape[1] // MEM_HEADS
    outs = []
    for h in range(MEM_HEADS):
        lo, hi = h * hd, (h + 1) * hd
        s = _bdot_nt(_cols(q, lo, hi), _cols(k, lo, hi)) * hd ** -0.5
        e = jnp.exp(s - lax.stop_gradient(jnp.max(s, axis=1, keepdims=True)))
        p = e / jnp.sum(e, axis=1, keepdims=True)
        outs.append(_bdot_nn(p, _cols(v, lo, hi)))
    return jnp.concatenate(outs, axis=1)


def _mem_fwd(q, k, v, *, name):
    T, D = q.shape
    tm = _tile(T, 256, 16)

    def fn(i, tv, pv):
        return (_mem_fn(tv[0], pv[0], pv[1]),), ()

    return _rowwise(fn, [q], [k, v], [(D, BF16)], tm=tm, name=name)[0]


def _mem_bwd(q, k, v, do, *, name):
    T, D = q.shape
    tm = _tile(T, 256, 16)

    def fn(i, tv, pv):
        _, vjp = jax.vjp(_mem_fn, tv[0], pv[0], pv[1])
        dq, dk, dv = vjp(tv[1].astype(F32))
        return (dq,), (dk, dv)

    return _rowwise(fn, [q, do], [k, v], [(D, BF16)], [(k.shape, F32), (v.shape, F32)], tm=tm, name=name)


def _loss_head(y, target, *, name):
    T, D = y.shape
    tm = _tile(T, 256, 16)

    def fn(i, tv, pv):
        err = tv[0] - tv[1]
        part = 0.5 * jnp.sum(jnp.sum(err * err, axis=1, keepdims=True) / D, axis=0, keepdims=True)
        return (err / D,), (jnp.broadcast_to(part, (8, LANE)),)

    return _rowwise(fn, [y, target], [], [(D, F32)], [((8, LANE), F32)], tm=tm, name=name)


def _adamw(w, g, m, v, *, name):
    shape = w.shape
    if w.ndim != 3:
        lead3 = (1, math.prod(shape[:-1]), shape[-1])
        return [o.reshape(shape) for o in _adamw(*[a.reshape(lead3) for a in (w, g, m, v)], name=name)]
    Lw, R, C = shape
    tr = _tile(R, 512, 8)
    b1c = 1.0 - ADAM_B1 ** ADAM_STEP
    b2c = 1.0 - ADAM_B2 ** ADAM_STEP

    def body(w_ref, g_ref, m_ref, v_ref, d_ref, mo_ref, vo_ref):
        gg = g_ref[...]
        mn = ADAM_B1 * m_ref[...] + (1.0 - ADAM_B1) * gg
        vn = ADAM_B2 * v_ref[...] + (1.0 - ADAM_B2) * (gg * gg)
        d_ref[...] = -ADAM_LR * ((mn / b1c) / (jnp.sqrt(vn / b2c) + ADAM_EPS) + ADAM_WD * w_ref[...])
        mo_ref[...] = mn
        vo_ref[...] = vn

    spec = pl.BlockSpec((None, tr, C), lambda l, i: (l, i, 0))
    return pl.pallas_call(
        body, name=name, grid=(Lw, R // tr),
        in_specs=[spec] * 4, out_specs=[spec] * 3,
        out_shape=[jax.ShapeDtypeStruct(shape, F32)] * 3,
        compiler_params=_params(("parallel", "parallel"), 7 * tr * C * 4),
    )(*[_hbm(a) for a in (w, g, m, v)])


def _pair_sum(core, g, landed, off, tr, *, name):
    rows = g.shape[0] // N_DEV
    C = g.shape[1]
    per = rows // tr

    def body(core_ref, g_ref, l_ref, o_ref):
        o_ref[...] = (g_ref[...].astype(F32) + l_ref[...].astype(F32)).astype(o_ref.dtype)

    slab = pl.BlockSpec((None, tr, C), lambda p, i, core_ref: (p, off // tr + i, 0))
    return pl.pallas_call(
        body, name=name,
        grid_spec=pltpu.PrefetchScalarGridSpec(
            num_scalar_prefetch=1, grid=(4, per),
            in_specs=[pl.BlockSpec((tr, C), lambda p, i, core_ref: ((2 * p + core_ref[0]) * per + i, 0)), slab],
            out_specs=slab),
        out_shape=jax.ShapeDtypeStruct(landed.shape, landed.dtype),
        input_output_aliases={2: 0},
        compiler_params=_params(("arbitrary", "arbitrary"), 4 * tr * C * 4),
    )(core, _hbm(g), _hbm(landed))


def _quad_sum(chip, part, gathered, used, *, name):
    C = part.shape[2]
    R = used
    tr = _tile(math.gcd(used, part.shape[1]), 256, 16)

    def body(chip_ref, own_ref, a_ref, b_ref, c_ref, o_ref):
        o_ref[...] = ((own_ref[...].astype(F32) + a_ref[...].astype(F32)) + b_ref[...].astype(F32)) \
            + c_ref[...].astype(F32)

    def other(k):
        return pl.BlockSpec((None, tr, C), lambda i, chip_ref: (chip_ref[0] ^ k, i, 0))

    return pl.pallas_call(
        body, name=name,
        grid_spec=pltpu.PrefetchScalarGridSpec(
            num_scalar_prefetch=1, grid=(R // tr,),
            in_specs=[pl.BlockSpec((None, tr, C), lambda i, chip_ref: (chip_ref[0], i, 0)),
                      other(1), other(2), other(3)],
            out_specs=pl.BlockSpec((tr, C), lambda i, chip_ref: (i, 0))),
        out_shape=jax.ShapeDtypeStruct((R, C), F32),
        compiler_params=_params(("arbitrary",), 8 * tr * C * 4),
    )(chip, _hbm(part), _hbm(gathered), _hbm(gathered), _hbm(gathered))


def _place():
    x, y, c = lax.axis_index("x"), lax.axis_index("y"), lax.axis_index("c")
    return x, y, c


ANY = pl.BlockSpec(memory_space=pl.ANY)


class _Gather:
    def __init__(self, shards):
        self.shards = list(shards)
        self.n = len(self.shards)
        self.out_shape = [jax.ShapeDtypeStruct((s.shape[0], N_DEV * s.shape[1], s.shape[2]), s.dtype)
                          for s in self.shards]
        self.scratch = [pltpu.SemaphoreType.DMA((7 * self.n,)), pltpu.SemaphoreType.DMA((7 * self.n,)),
                        pltpu.SemaphoreType.DMA((self.n,))]
        self.operands = [_hbm(s) for s in self.shards]

    def _bind(self, refs):
        n = self.n
        ins, outs = refs[:n], refs[n:2 * n]
        send_sems, recv_sems, local_sems = refs[2 * n:]
        x, y, c = _place()
        me, sib = (x, y, c), (x, y, 1 - c)
        chips = [(1 - x, y), (x, 1 - y), (1 - x, 1 - y)]

        def rows(w, p):
            r = self.shards[w].shape[1]
            idx = 4 * p[0] + 2 * p[1] + p[2]
            return outs[w].at[:, pl.ds(pl.multiple_of(idx * r, 8), r), :]

        def copy(w, k, block, to, src=None):
            return pltpu.make_async_remote_copy(
                src_ref=rows(w, block) if src is None else src, dst_ref=rows(w, block),
                send_sem=send_sems.at[w * 7 + k], recv_sem=recv_sems.at[w * 7 + k],
                device_id=to, device_id_type=MESH)

        mine = [pltpu.make_async_copy(ins[w], rows(w, me), local_sems.at[w]) for w in range(n)]
        first = []
        for w in range(n):
            first.append(copy(w, 0, me, sib, src=ins[w]))
            for j, chip in enumerate(chips):
                first.append(copy(w, 1 + j, me, (*chip, c), src=ins[w]))
        passed = [copy(w, 4 + j, (*chip, c), sib) for j, chip in enumerate(chips) for w in range(n)]
        landed = [copy(w, 1 + j, (*chip, c), me) for j, chip in enumerate(chips) for w in range(n)]
        last = []
        for w in range(n):
            last.append(copy(w, 0, sib, me))
            for j, chip in enumerate(chips):
                last.append(copy(w, 4 + j, (*chip, 1 - c), me))
        return mine, first, landed, passed, last

    def start(self, refs):
        mine, first, _, _, _ = self._bind(refs)
        for cp in mine + first:
            cp.start()

    def forward(self, refs):
        _, _, landed, passed, _ = self._bind(refs)
        for arrived, fwd in zip(landed, passed):
            arrived.wait_recv()
            fwd.start()

    def finish(self, refs):
        mine, first, _, passed, last = self._bind(refs)
        for cp in last:
            cp.wait_recv()
        for cp in first + passed:
            cp.wait_send()
        for cp in mine:
            cp.wait()


class _ChipExchange:
    def __init__(self, parts, used):
        self.ncl = len(parts)
        self.used = list(used)
        self.out_shape = [jax.ShapeDtypeStruct(p.shape, p.dtype) for p in parts]
        self.scratch = [pltpu.SemaphoreType.DMA((3 * self.ncl,)), pltpu.SemaphoreType.DMA((3 * self.ncl,))]
        self.operands = [_hbm(p) for p in parts]
        self.n = self.ncl

    def _bind(self, refs):
        ncl = self.ncl
        ins, outs = refs[:ncl], refs[ncl:2 * ncl]
        send_sems, recv_sems = refs[2 * ncl:]
        x, y, c = _place()
        chips = [(1 - x, y), (x, 1 - y), (1 - x, 1 - y)]
        here = 2 * x + y
        sent, arriving = [], []
        for k in range(ncl):
            rows = pl.ds(0, self.used[k])
            for j, (cx, cy) in enumerate(chips):
                sems = dict(send_sem=send_sems.at[3 * k + j], recv_sem=recv_sems.at[3 * k + j],
                            device_id=(cx, cy, c), device_id_type=MESH)
                sent.append(pltpu.make_async_remote_copy(
                    src_ref=ins[k].at[2 * cx + cy, rows, :], dst_ref=outs[k].at[here, rows, :], **sems))
                arriving.append(pltpu.make_async_remote_copy(
                    src_ref=ins[k].at[here, rows, :], dst_ref=outs[k].at[2 * cx + cy, rows, :], **sems))
        return sent, arriving

    def start(self, refs):
        for cp in self._bind(refs)[0]:
            cp.start()

    def forward(self, refs):
        pass

    def finish(self, refs):
        sent, arriving = self._bind(refs)
        for cp in arriving:
            cp.wait_recv()
        for cp in sent:
            cp.wait_send()


def _exchange_alone(ex, *, name):
    def body(*refs):
        ex.start(refs)
        ex.forward(refs)
        ex.finish(refs)

    return pl.pallas_call(
        body, name=name, in_specs=[ANY] * ex.n, out_specs=[ANY] * ex.n,
        out_shape=ex.out_shape, scratch_shapes=ex.scratch,
    )(*ex.operands)


def _host_call(body, *, name, grid, in_specs, out_specs, out_shape, args, sem, est, ride=None):
    if ride is None:
        outs = pl.pallas_call(body, name=name, grid=grid, in_specs=in_specs, out_specs=out_specs,
                              out_shape=out_shape, compiler_params=_params(sem, est))(*args)
        return list(outs), []
    n_in, n_out, n = len(in_specs), len(out_specs), ride.n

    def full(*refs):
        ins, rin = refs[:n_in], refs[n_in:n_in + n]
        outs, rout = refs[n_in + n:n_in + n + n_out], refs[n_in + n + n_out:n_in + 2 * n + n_out]
        rrefs = (*rin, *rout, *refs[n_in + 2 * n + n_out:])
        step, total = _ride(ride, rrefs, grid)
        body(*ins, *outs)
        _ride_end(ride, rrefs, step, total)

    outs = pl.pallas_call(
        full, name=name, grid=grid,
        in_specs=list(in_specs) + [ANY] * n, out_specs=list(out_specs) + [ANY] * n,
        out_shape=list(out_shape) + ride.out_shape, scratch_shapes=ride.scratch,
        compiler_params=_params(("arbitrary",) * len(grid), est),
    )(*args, *ride.operands)
    return list(outs[:n_out]), list(outs[n_out:])


def _ride(ex, refs, grid):
    total = math.prod(grid)
    step = pl.program_id(0)
    for axis in range(1, len(grid)):
        step = step * grid[axis] + pl.program_id(axis)
    pl.when(step == 0)(lambda: ex.start(refs))
    return step, total


def _ride_end(ex, refs, step, total):
    pl.when(step == (3 * total) // 4)(lambda: ex.forward(refs))
    pl.when(step == total - 1)(lambda: ex.finish(refs))


def _class_layout(grads, classes):
    used = [0] * len(set(classes))
    offs, tiles = [], []
    for g, cl in zip(grads, classes):
        rows = g.shape[0] // N_DEV
        offs.append(used[cl])
        tiles.append(math.gcd(rows, used[cl]) if used[cl] else rows)
        used[cl] += rows
    heights = []
    for k, u in enumerate(used):
        step = math.lcm(*[t for t, cl in zip(tiles, classes) if cl == k])
        heights.append(-(-u // step) * step)
    return offs, tiles, used, heights


def _rs_to_sibling(grads, classes, *, name):
    n = len(grads)
    offs, _, used, heights = _class_layout(grads, classes)
    ncl = len(heights)
    cols = [next(g.shape[1] for g, cl in zip(grads, classes) if cl == k) for k in range(ncl)]

    def body(*refs):
        gs, land = refs[:n], refs[n:n + ncl]
        send_sems, recv_sems = refs[n + ncl:]
        x, y, c = _place()
        sib = (x, y, 1 - c)
        for p in range(4):
            for w in range(n):
                r = grads[w].shape[0] // N_DEV
                cl = classes[w]
                there = gs[w].at[pl.ds(pl.multiple_of((2 * p + 1 - c) * r, 8), r), :]
                pltpu.make_async_remote_copy(
                    src_ref=there, dst_ref=land[cl].at[p, pl.ds(offs[w], r), :],
                    send_sem=send_sems.at[cl * 4 + p], recv_sem=recv_sems.at[cl * 4 + p],
                    device_id=sib, device_id_type=MESH).start()
        for cl in range(ncl):
            for p in range(4):
                rows_used = land[cl].at[p, pl.ds(0, used[cl]), :]
                slab = pltpu.make_async_remote_copy(
                    src_ref=rows_used, dst_ref=rows_used,
                    send_sem=send_sems.at[cl * 4 + p], recv_sem=recv_sems.at[cl * 4 + p],
                    device_id=sib, device_id_type=MESH)
                slab.wait_send()
                slab.wait_recv()

    return pl.pallas_call(
        body, name=name,
        in_specs=[ANY] * n, out_specs=[ANY] * ncl,
        out_shape=[jax.ShapeDtypeStruct((4, heights[k], cols[k]), BF16) for k in range(ncl)],
        scratch_shapes=[pltpu.SemaphoreType.DMA((4 * ncl,))] * 2,
    )(*[_hbm(g) for g in grads])


def _all_reduce_small(v, *, name):
    R = v.shape[0]

    def body(v_ref, o_ref, buf, send_sems, recv_sems):
        x, y, c = _place()
        me = 4 * x + 2 * y + c
        buf[me] = v_ref[...]
        copies = []
        for k in range(1, N_DEV):
            fx, fy, fc = (k >> 2) & 1, (k >> 1) & 1, k & 1
            to = (x ^ fx, y ^ fy, c ^ fc)
            cp = pltpu.make_async_remote_copy(
                src_ref=v_ref, dst_ref=buf.at[me],
                send_sem=send_sems.at[k - 1], recv_sem=recv_sems.at[k - 1],
                device_id=to, device_id_type=MESH)
            cp.start()
            copies.append(cp)
        for k in range(1, N_DEV):
            fx, fy, fc = (k >> 2) & 1, (k >> 1) & 1, k & 1
            frm = 4 * (x ^ fx) + 2 * (y ^ fy) + (c ^ fc)
            pltpu.make_async_remote_copy(
                src_ref=v_ref, dst_ref=buf.at[frm],
                send_sem=send_sems.at[k - 1], recv_sem=recv_sems.at[k - 1],
                device_id=(x ^ fx, y ^ fy, c ^ fc), device_id_type=MESH).wait_recv()
        for cp in copies:
            cp.wait_send()
        acc = buf[0]
        for d in range(1, N_DEV):
            acc = acc + buf[d]
        o_ref[...] = acc

    vm = pl.BlockSpec(memory_space=pltpu.VMEM)
    return pl.pallas_call(
        body, name=name, in_specs=[vm], out_specs=vm,
        out_shape=jax.ShapeDtypeStruct((R, LANE), F32),
        scratch_shapes=[pltpu.VMEM((N_DEV, R, LANE), F32),
                        pltpu.SemaphoreType.DMA((N_DEV - 1,)), pltpu.SemaphoreType.DMA((N_DEV - 1,))],
        compiler_params=pltpu.CompilerParams(vmem_limit_bytes=VMEM_FLOOR),
    )(v)


def _rope_tables(positions):
    half = QK_ROPE // 2
    inv_freq = ROPE_BASE ** (-jnp.arange(half, dtype=F32) / half)
    ang = positions.astype(F32)[:, None] * inv_freq
    cos, sin = jnp.cos(ang), jnp.sin(ang)
    z = jnp.zeros_like(cos)
    z2 = jnp.zeros((positions.shape[0], LANE - QK_ROPE), F32)
    rc = jnp.concatenate([cos, cos, z2], axis=1)
    rs1 = jnp.concatenate([-sin, z, z2], axis=1)
    rs2 = jnp.concatenate([z, sin, z2], axis=1)
    return rc, rs1, rs2


def _block_diag(pool_w):
    G, pg, _ = pool_w.shape
    out = jnp.zeros((G * pg, G * pg), pool_w.dtype)
    for g in range(G):
        out = lax.dynamic_update_slice(out, pool_w[g], (g * pg, g * pg))
    return out


def kernel(x, mem, positions, ln_g, ln_b, ffn1_w13, ffn1_w2, w_in, pool_w, pool_scale, q_norm_g, w_uq, kv_norm_g, w_ukv, w_out, mem_wq, mem_wkv, mem_wo, ffn2_w13, ffn2_w2, loss_target, m_ln_g, m_ln_b, m_ffn1_w13, m_ffn1_w2, m_w_in, m_pool_w, m_pool_scale, m_q_norm_g, m_w_uq, m_kv_norm_g, m_w_ukv, m_w_out, m_mem_wq, m_mem_wkv, m_mem_wo, m_ffn2_w13, m_ffn2_w2, v_ln_g, v_ln_b, v_ffn1_w13, v_ffn1_w2, v_w_in, v_pool_w, v_pool_scale, v_q_norm_g, v_w_uq, v_kv_norm_g, v_w_ukv, v_w_out, v_mem_wq, v_mem_wkv, v_mem_wo, v_ffn2_w13, v_ffn2_w2):
    L = ln_g.shape[0]
    T, D = x.shape[1], x.shape[2]
    F = ffn1_w2.shape[1] * N_DEV
    PW = D // 4
    H = (D - PW) // V_HEAD
    DIN = w_in.shape[2]
    DINP = PW + Q_LORA + KV_LORA + LANE
    QW = QK_NOPE + QK_ROPE
    alpha = (2 * L) ** 0.25
    x2d = x.reshape(T, D)
    memb = mem.reshape(mem.shape[1], D).astype(BF16)
    target = loss_target.reshape(T, D)
    tabs = _rope_tables(positions.reshape(T))

    def shards_of(l):
        return dict(
            w13a=ffn1_w13[l].T[None].astype(BF16),
            w13b=ffn2_w13[l].T[None].astype(BF16),
            w2=jnp.stack([ffn1_w2[l], ffn2_w2[l]]).astype(BF16),
            wsq=jnp.stack([w_out[l], mem_wq[l], mem_wo[l]]).astype(BF16),
            wkvT=mem_wkv[l].T[None].astype(BF16),
            winp=jnp.pad(w_in[l], ((0, 0), (0, DINP - DIN)))[None].astype(BF16),
            wuqT=w_uq[l].T[None].astype(BF16),
            wukvT=w_ukv[l].T[None].astype(BF16),
        )

    MID = ("w2", "wsq", "wkvT", "winp", "wuqT", "wukvT")

    def finish_weights(g):
        wuq = jnp.pad(g["wuqT"].reshape(H, QW, Q_LORA), ((0, 0), (0, HEAD_PAD - QW), (0, 0)))
        return dict(g, w13T=[_interleave(g["w13a"], 1), _interleave(g["w13b"], 1)],
                    wuqT=wuq.reshape(1, H * HEAD_PAD, Q_LORA))

    names0 = ["w13a", "w13b", *MID]
    sh0 = shards_of(0)
    ln_shard = jnp.concatenate([ln_g.reshape(1, 4 * L, -1), ln_b.reshape(1, 4 * L, -1)], axis=1)
    got = _exchange_alone(_Gather([sh0[k] for k in names0] + [ln_shard]), name="ag_layer0")
    W = [finish_weights(dict(zip(names0, got[:-1])))]
    lnp = jnp.moveaxis(got[-1].reshape(N_DEV, 2, L, 4, D // N_DEV), 0, 3).reshape(2, L, 4, D)
    lng, lnb = lnp[0], lnp[1]
    wbd = [_block_diag(pool_w[l]).astype(BF16) for l in range(L)]

    def ffn_fwd(l, which, xres, xb, k, ride):
        h13, a, rode = _ffn_up(xb, W[l]["w13T"][which], 0, name=f"l{l}_ffn{which}_up", ride=ride)
        y, xo, xob = _mm_ln(a, W[l]["w2"], which, xres, lng[l, k:k + 1], lnb[l, k:k + 1], alpha=alpha, s=0.5,
                            name=f"l{l}_ffn{which}_y_ln{k}")
        return dict(xres=xres, xb=xb, h13=h13, a=a, y=y), xo, xob, rode

    saved = []
    xres, xb = x2d, x2d.astype(BF16)
    for l in range(L):
        sv = {}
        nxt = shards_of(l + 1) if l + 1 < L else None
        Wl = W[l]
        sv["ffn1"], x1, x1b, got_a = ffn_fwd(l, 0, xres, xb, 0, _Gather([nxt["w13a"]]) if nxt else None)
        hin = _mm(x1b, Wl["winp"], lead=0, name=f"l{l}_hin")
        pscale = pool_scale[l].reshape(1, PW)
        gq, gkv = q_norm_g[l].reshape(1, Q_LORA), kv_norm_g[l].reshape(1, KV_LORA)
        ypool = _pool_fwd(hin, wbd[l], pscale, name=f"l{l}_pool")
        cqn, ckvn, kpe = _norms_fwd(hin, gq, gkv, pw=PW, name=f"l{l}_norms")
        qraw = _mm(cqn, Wl["wuqT"], lead=0, tb=True, name=f"l{l}_qraw")
        kv = _mm(ckvn, Wl["wukvT"], lead=0, tb=True, name=f"l{l}_kv")
        qh, kh, vh = _heads_fwd(qraw, kv, kpe, tabs, H=H, name=f"l{l}_heads")
        o, lse, got_mid = _flash_fwd(qh, kh, vh, H=H, name=f"l{l}_flash",
                                     ride=_Gather([nxt[k] for k in MID]) if nxt else None)
        cat = jnp.concatenate([ypool, o], axis=1)
        ymix, x2, x2b = _mm_ln(cat, Wl["wsq"], 0, x1, lng[l, 1:2], lnb[l, 1:2], alpha=alpha, s=1.0,
                               name=f"l{l}_ymix_ln1")
        qm = _mm(x2b, Wl["wsq"], lead=1, out_dtype=BF16, name=f"l{l}_qm")
        kvm = _mm(memb, Wl["wkvT"], lead=0, tb=True, name=f"l{l}_kvm")
        km, vm = kvm[:, :D], kvm[:, D:]
        om = _mem_fwd(qm, km, vm, name=f"l{l}_memattn")
        ymem, x3, x3b = _mm_ln(om, Wl["wsq"], 2, x2, lng[l, 2:3], lnb[l, 2:3], alpha=alpha, s=1.0,
                               name=f"l{l}_ymem_ln2")
        sv["ffn2"], x4, x4b, got_b = ffn_fwd(l, 1, x3, x3b, 3, _Gather([nxt["w13b"]]) if nxt else None)
        if nxt:
            W.append(finish_weights(dict(w13a=got_a[0], w13b=got_b[0], **dict(zip(MID, got_mid)))))
        sv.update(x1=x1, x1b=x1b, hin=hin, pscale=pscale, gq=gq, gkv=gkv, cqn=cqn, ckvn=ckvn,
                  qh=qh, kh=kh, vh=vh, lse=lse, cat=cat, ymix=ymix, x2=x2, x2b=x2b, qm=qm, km=km, vm=vm,
                  om=om, ymem=ymem)
        saved.append(sv)
        xres, xb = x4, x4b

    dx, loss_blk = _loss_head(xres, target, name="loss_head")
    loss = lax.psum(loss_blk[0, 0], ("x", "y", "c"))

    gW = {}
    gS = {}

    def ffn_bwd(l, which, sv, dx, k):
        tag = f"l{l}_ffn{which}"
        dxres, dyb, dg, db = _ln_bwd(sv["xres"], sv["y"], lng[l, k:k + 1], dx,
                                     alpha=alpha, s=0.5, name=f"l{l}_ln{k}_bwd")
        gW[("w2", which, l)] = _mm(sv["a"], dyb, ta=True, out_dtype=BF16, name=f"{tag}_dw2", tn=D)
        dh = _ffn_down_bwd(dyb, W[l]["w2"], which, sv["h13"], name=f"{tag}_dh")
        dxn = _mm(dh, W[l]["w13T"][which], lead=0, add=dxres, name=f"{tag}_dx", tn=D)
        dw13 = _mm(dh, sv["xb"], ta=True, out_dtype=BF16, name=f"{tag}_dw13", tn=D)
        gW[("w13", which, l)] = _deinterleave(dw13, 0)
        gS[("ln_g", l, k)], gS[("ln_b", l, k)] = dg, db
        return dxn

    core = lax.axis_index("c").astype(jnp.int32).reshape(1)
    chip = (2 * lax.axis_index("x") + lax.axis_index("y")).astype(jnp.int32).reshape(1)
    gsh = {}

    def rs_first_level(l):
        keys = [("w13", 0, l), ("w13", 1, l), ("w2", 0, l), ("w2", 1, l), ("mem_wkv", l),
                ("w_out", l), ("mem_wq", l), ("mem_wo", l), ("w_in", l), ("w_uq", l), ("w_ukv", l)]
        classes = [0] * 8 + [1, 2, 3]
        garrs = []
        for key in keys:
            g = gW[key]
            if key[0] == "w_uq":
                g = g.reshape(H, HEAD_PAD, Q_LORA)[:, :QW, :].reshape(H * QW, Q_LORA)
            garrs.append(g)
        offs, tiles, used, _ = _class_layout(garrs, classes)
        parts = list(_rs_to_sibling(garrs, classes, name=f"l{l}_rs_sibling"))
        for w, (g, cl, off, tr) in enumerate(zip(garrs, classes, offs, tiles)):
            parts[cl] = _pair_sum(core, g, parts[cl], off, tr, name=f"l{l}_rs_pair_sum_{w}")
        return dict(l=l, keys=keys, garrs=garrs, classes=classes, offs=offs, used=used, parts=parts,
                    ex=_ChipExchange(parts, used))

    def rs_last_level(st, gathered):
        sums = [_quad_sum(chip, p, a, u, name=f"l{st['l']}_rs_quad_sum{k}")
                for k, (p, a, u) in enumerate(zip(st["parts"], gathered, st["used"]))]
        for key, g, cl, off in zip(st["keys"], st["garrs"], st["classes"], st["offs"]):
            gsh[key] = sums[cl][off:off + g.shape[0] // N_DEV, :]

    pending = None
    for l in reversed(range(L)):
        sv = saved[l]
        Wl = W[l]
        dx = ffn_bwd(l, 1, sv["ffn2"], dx, 3)
        dxres, dyb, dg, db = _ln_bwd(sv["x2"], sv["ymem"], lng[l, 2:3], dx,
                                     alpha=alpha, s=1.0, name=f"l{l}_ln2_bwd")
        gS[("ln_g", l, 2)], gS[("ln_b", l, 2)] = dg, db
        dom = _mm(dyb, Wl["wsq"], lead=2, tb=True, out_dtype=BF16, name=f"l{l}_dom")
        gW[("mem_wo", l)] = _mm(sv["om"], dyb, ta=True, out_dtype=BF16, name=f"l{l}_dwo", tn=D)
        dqm, dkm, dvm = _mem_bwd(sv["qm"], sv["km"], sv["vm"], dom, name=f"l{l}_memattn_bwd")
        dx = _mm(dqm, Wl["wsq"], lead=1, tb=True, add=dxres, name=f"l{l}_dx2", tn=D)
        gW[("mem_wq", l)] = _mm(sv["x2b"], dqm, ta=True, out_dtype=BF16, name=f"l{l}_dwq", tn=D)
        dkvm = jnp.concatenate([dkm, dvm], axis=1).astype(BF16)
        gW[("mem_wkv", l)] = _mm(dkvm, memb, ta=True, out_dtype=BF16, name=f"l{l}_dwkv", tn=D)
        dxres, dyb, dg, db = _ln_bwd(sv["x1"], sv["ymix"], lng[l, 1:2], dx,
                                     alpha=alpha, s=1.0, name=f"l{l}_ln1_bwd")
        gS[("ln_g", l, 1)], gS[("ln_b", l, 1)] = dg, db
        dcat = _mm(dyb, Wl["wsq"], lead=0, tb=True, name=f"l{l}_dcat", tn=D)
        gW[("w_out", l)] = _mm(sv["cat"], dyb, ta=True, out_dtype=BF16, name=f"l{l}_dwout", tn=D)
        dqh, dkh, dvh, rode = _flash_bwd(sv["qh"], sv["kh"], sv["vh"], sv["cat"], dcat, sv["lse"], H=H, pw=PW,
                                         name=f"l{l}_flash_bwd", ride=pending["ex"] if pending else None)
        if pending:
            rs_last_level(pending, rode)
        dqraw, dkv, dkpe = _heads_bwd(dqh, dkh, dvh, tabs, H=H, name=f"l{l}_heads_bwd")
        dcq = _mm(dqraw, Wl["wuqT"], lead=0, name=f"l{l}_dcq")
        gW[("w_uq", l)] = _mm(dqraw, sv["cqn"], ta=True, out_dtype=BF16, name=f"l{l}_dwuq")
        dckv = _mm(dkv, Wl["wukvT"], lead=0, name=f"l{l}_dckv")
        gW[("w_ukv", l)] = _mm(dkv, sv["ckvn"], ta=True, out_dtype=BF16, name=f"l{l}_dwukv")
        du, dwbd, dps = _pool_bwd(sv["hin"], dcat, wbd[l], sv["pscale"], name=f"l{l}_pool_bwd")
        dhin, dgq, dgkv = _norms_bwd(sv["hin"], sv["gq"], sv["gkv"], dcq, dckv, dkpe, du, pw=PW,
                                     name=f"l{l}_norms_bwd")
        pg = PW // len(POOL_WINDOWS)
        gS[("pool_w", l)] = jnp.stack([dwbd[g * pg:(g + 1) * pg, g * pg:(g + 1) * pg]
                                       for g in range(len(POOL_WINDOWS))])
        gS[("pool_scale", l)], gS[("q_norm_g", l)], gS[("kv_norm_g", l)] = dps, dgq, dgkv
        dx = _mm(dhin, Wl["winp"], lead=0, tb=True, add=dxres, name=f"l{l}_dx1", tn=D)
        gW[("w_in", l)] = _mm(sv["x1b"], dhin, ta=True, out_dtype=BF16, name=f"l{l}_dwin", tn=DINP)
        dx = ffn_bwd(l, 0, sv["ffn1"], dx, 0)
        pending = rs_first_level(l)
    rs_last_level(pending, _exchange_alone(pending["ex"], name="rs_chips_last"))
    grad_x = dx.reshape(1, T, D)

    small_keys = []
    for l in range(L):
        small_keys += [("pool_w", l), ("pool_scale", l), ("q_norm_g", l), ("kv_norm_g", l)]
        small_keys += [("ln_g", l, k) for k in range(4)] + [("ln_b", l, k) for k in range(4)]
    flat = jnp.concatenate([gS[k].reshape(-1) for k in small_keys])
    n_small = flat.shape[0]
    rows = -(-n_small // (8 * LANE)) * 8
    flat = jnp.pad(flat, (0, rows * LANE - n_small)).reshape(rows, LANE)
    red = _all_reduce_small(flat, name="ar_small").reshape(-1)
    gsm, pos = {}, 0
    for k in small_keys:
        size = math.prod(gS[k].shape)
        gsm[k] = red[pos:pos + size].reshape(gS[k].shape)
        pos += size

    me = 4 * lax.axis_index("x") + 2 * lax.axis_index("y") + lax.axis_index("c")
    dsh = D // N_DEV
    stack = lambda f: jnp.stack([f(l) for l in range(L)])
    g_ln_g = stack(lambda l: jnp.concatenate([gsm[("ln_g", l, k)] for k in range(4)], axis=0))
    g_ln_b = stack(lambda l: jnp.concatenate([gsm[("ln_b", l, k)] for k in range(4)], axis=0))
    grads = {
        "ln_g": lax.dynamic_slice_in_dim(g_ln_g, me * dsh, dsh, axis=2),
        "ln_b": lax.dynamic_slice_in_dim(g_ln_b, me * dsh, dsh, axis=2),
        "ffn1_w13": stack(lambda l: gsh[("w13", 0, l)].T),
        "ffn1_w2": stack(lambda l: gsh[("w2", 0, l)]),
        "w_in": stack(lambda l: gsh[("w_in", l)][:, :DIN]),
        "pool_w": stack(lambda l: gsm[("pool_w", l)]),
        "pool_scale": stack(lambda l: gsm[("pool_scale", l)].reshape(PW)),
        "q_norm_g": stack(lambda l: gsm[("q_norm_g", l)].reshape(Q_LORA)),
        "w_uq": stack(lambda l: gsh[("w_uq", l)].T),
        "kv_norm_g": stack(lambda l: gsm[("kv_norm_g", l)].reshape(KV_LORA)),
        "w_ukv": stack(lambda l: gsh[("w_ukv", l)].T),
        "w_out": stack(lambda l: gsh[("w_out", l)]),
        "mem_wq": stack(lambda l: gsh[("mem_wq", l)]),
        "mem_wkv": stack(lambda l: gsh[("mem_wkv", l)].T),
        "mem_wo": stack(lambda l: gsh[("mem_wo", l)]),
        "ffn2_w13": stack(lambda l: gsh[("w13", 1, l)].T),
        "ffn2_w2": stack(lambda l: gsh[("w2", 1, l)]),
    }

    names = ["ln_g", "ln_b", "ffn1_w13", "ffn1_w2", "w_in", "pool_w", "pool_scale", "q_norm_g", "w_uq",
             "kv_norm_g", "w_ukv", "w_out", "mem_wq", "mem_wkv", "mem_wo", "ffn2_w13", "ffn2_w2"]
    weights = dict(ln_g=ln_g, ln_b=ln_b, ffn1_w13=ffn1_w13, ffn1_w2=ffn1_w2, w_in=w_in, pool_w=pool_w,
                   pool_scale=pool_scale, q_norm_g=q_norm_g, w_uq=w_uq, kv_norm_g=kv_norm_g, w_ukv=w_ukv,
                   w_out=w_out, mem_wq=mem_wq, mem_wkv=mem_wkv, mem_wo=mem_wo, ffn2_w13=ffn2_w13,
                   ffn2_w2=ffn2_w2)
    ms = dict(ln_g=m_ln_g, ln_b=m_ln_b, ffn1_w13=m_ffn1_w13, ffn1_w2=m_ffn1_w2, w_in=m_w_in, pool_w=m_pool_w,
              pool_scale=m_pool_scale, q_norm_g=m_q_norm_g, w_uq=m_w_uq, kv_norm_g=m_kv_norm_g,
              w_ukv=m_w_ukv, w_out=m_w_out, mem_wq=m_mem_wq, mem_wkv=m_mem_wkv, mem_wo=m_mem_wo,
              ffn2_w13=m_ffn2_w13, ffn2_w2=m_ffn2_w2)
    vs = dict(ln_g=v_ln_g, ln_b=v_ln_b, ffn1_w13=v_ffn1_w13, ffn1_w2=v_ffn1_w2, w_in=v_w_in, pool_w=v_pool_w,
              pool_scale=v_pool_scale, q_norm_g=v_q_norm_g, w_uq=v_w_uq, kv_norm_g=v_kv_norm_g,
              w_ukv=v_w_ukv, w_out=v_w_out, mem_wq=v_mem_wq, mem_wkv=v_mem_wkv, mem_wo=v_mem_wo,
              ffn2_w13=v_ffn2_w13, ffn2_w2=v_ffn2_w2)
    deltas, new_m, new_v = [], [], []
    for nme in names:
        d, mn, vn = _adamw(weights[nme], grads[nme], ms[nme], vs[nme], name=f"adamw_{nme}")
        deltas.append(d)
        new_m.append(mn)
        new_v.append(vn)
    return (loss, grad_x, *[grads[nme] for nme in names], *deltas, *new_m, *new_v)
```

```python
import functools
import math

import jax
import jax.numpy as jnp
from jax import lax
from jax.experimental import pallas as pl
from jax.experimental.pallas import tpu as pltpu

F32 = jnp.float32
BF16 = jnp.bfloat16
MESH = pl.DeviceIdType.MESH

CHUNK = 64
MEM_HEADS = 4
POOL_WINDOWS = (2, 4, 8, 16)
QK_NOPE = 128
QK_ROPE = 64
V_HEAD = 128
Q_LORA = 256
KV_LORA = 128
ROPE_BASE = 10000.0
LN_EPS = 1e-5
RMS_EPS = 1e-6
NEG_INF = -1e30
ADAM_LR = 0.001
ADAM_B1 = 0.9
ADAM_B2 = 0.999
ADAM_EPS = 1e-08
ADAM_WD = 0.01
ADAM_STEP = 10

N_DEV = 8
LANE = 128
HEAD_PAD = 2 * LANE
POOL_HALO = 16
VMEM_CAP = 56 * 1024 * 1024
VMEM_FLOOR = 32 * 1024 * 1024


def _tile(n, pref, mult):
    t = (min(pref, n) // mult) * mult
    while t >= mult:
        if n % t == 0:
            return t
        t -= mult
    return n


def _params(sem, est_bytes):
    limit = int(min(max(2 * est_bytes + (8 << 20), VMEM_FLOOR), VMEM_CAP))
    return pltpu.CompilerParams(dimension_semantics=sem, vmem_limit_bytes=limit)


def _nbytes(shape, dtype):
    return math.prod(shape) * jnp.dtype(dtype).itemsize


def _hbm(x):
    return pltpu.with_memory_space_constraint(x, pltpu.HBM)


def _dg(a, b, ca, cb):
    return lax.dot_general(a.astype(BF16), b.astype(BF16), (((ca,), (cb,)), ((), ())),
                           preferred_element_type=F32)


@jax.custom_vjp
def _bdot_nn(a, b):
    return _dg(a, b, 1, 0)


def _bdot_nn_fwd(a, b):
    return _dg(a, b, 1, 0), (a, b)


def _bdot_nn_bwd(res, ct):
    a, b = res
    return _dg(ct, b, 1, 1).astype(a.dtype), _dg(a, ct, 0, 0).astype(b.dtype)


_bdot_nn.defvjp(_bdot_nn_fwd, _bdot_nn_bwd)


@jax.custom_vjp
def _bdot_nt(a, b):
    return _dg(a, b, 1, 1)


def _bdot_nt_fwd(a, b):
    return _dg(a, b, 1, 1), (a, b)


def _bdot_nt_bwd(res, ct):
    a, b = res
    return _dg(ct, b, 1, 0).astype(a.dtype), _dg(ct, a, 0, 0).astype(b.dtype)


_bdot_nt.defvjp(_bdot_nt_fwd, _bdot_nt_bwd)


@functools.partial(jax.custom_vjp, nondiff_argnums=(1,))
def _lane_roll(x, shift):
    return pltpu.roll(x, shift % x.shape[1], axis=1)


def _lane_roll_fwd(x, shift):
    return _lane_roll(x, shift), None


def _lane_roll_bwd(shift, _, ct):
    return (_lane_roll(ct, -shift),)


_lane_roll.defvjp(_lane_roll_fwd, _lane_roll_bwd)


@functools.partial(jax.custom_vjp, nondiff_argnums=(1, 2))
def _cols(x, lo, hi):
    return x[:, lo:hi]


def _cols_fwd(x, lo, hi):
    return x[:, lo:hi], x.shape[1]


def _cols_bwd(lo, hi, width, ct):
    parts = []
    if lo > 0:
        parts.append(jnp.zeros((ct.shape[0], lo), ct.dtype))
    parts.append(ct)
    if hi < width:
        parts.append(jnp.zeros((ct.shape[0], width - hi), ct.dtype))
    return (jnp.concatenate(parts, axis=1) if len(parts) > 1 else ct,)


_cols.defvjp(_cols_fwd, _cols_bwd)


MM_VMEM_BUDGET = 20 * 1024 * 1024


def _mm(a, b, *, name, ta=False, tb=False, out_dtype=F32, lead=None, add=None, add_scale=1.0,
        tm=1024, tn=1024, tk=2816):
    if ta:
        K, M = a.shape
    else:
        M, K = a.shape
    bshape = b.shape[1:] if lead is not None else b.shape
    if tb:
        N, Kb = bshape
    else:
        Kb, N = bshape
    assert K == Kb, (name, a.shape, b.shape)

    def blocks(tm, tn, tk):
        tm = _tile(M, tm, LANE if ta else 16)
        tn = _tile(N, tn, LANE)
        tk = _tile(K, tk, LANE)
        nbytes = (tm * tk * a.dtype.itemsize + tk * tn * b.dtype.itemsize
                  + tm * tn * (jnp.dtype(out_dtype).itemsize + (4 if K // tk > 1 else 0)
                               + (add.dtype.itemsize if add is not None else 0)))
        return tm, tn, tk, nbytes

    tm, tn, tk, est = blocks(tm, tn, tk)
    for shrink in ("m", "k", "m", "k", "n"):
        if est <= MM_VMEM_BUDGET:
            break
        if shrink == "m":
            tm, tn, tk, est = blocks(max(tm // 2, LANE), tn, tk)
        elif shrink == "k":
            tm, tn, tk, est = blocks(tm, tn, max(tk // 2, LANE))
        else:
            tm, tn, tk, est = blocks(tm, max(tn // 2, LANE), tk)
    nk = K // tk
    ca = 0 if ta else 1
    cb = 1 if tb else 0

    def body(*refs):
        a_ref, b_ref = refs[0], refs[1]
        add_ref = refs[2] if add is not None else None
        o_ref = refs[3] if add is not None else refs[2]

        def finish(r):
            if add_ref is not None:
                r = r + add_scale * add_ref[...].astype(F32)
            o_ref[...] = r.astype(o_ref.dtype)

        if nk == 1:
            finish(_dg(a_ref[...], b_ref[...], ca, cb))
            return
        acc_ref = refs[-1]
        k = pl.program_id(2)

        @pl.when(k == 0)
        def _():
            acc_ref[...] = jnp.zeros_like(acc_ref)

        acc_ref[...] += _dg(a_ref[...], b_ref[...], ca, cb)

        @pl.when(k == nk - 1)
        def _():
            finish(acc_ref[...])

    a_blk = (tk, tm) if ta else (tm, tk)
    a_map = (lambda i, j, k: (k, i)) if ta else (lambda i, j, k: (i, k))
    b_blk = (tn, tk) if tb else (tk, tn)
    if lead is None:
        b_map = (lambda i, j, k: (j, k)) if tb else (lambda i, j, k: (k, j))
        b_spec = pl.BlockSpec(b_blk, b_map)
    else:
        b_map = (lambda i, j, k: (lead, j, k)) if tb else (lambda i, j, k: (lead, k, j))
        b_spec = pl.BlockSpec((None,) + b_blk, b_map)
    in_specs = [pl.BlockSpec(a_blk, a_map), b_spec]
    args = [a, b]
    if add is not None:
        in_specs.append(pl.BlockSpec((tm, tn), lambda i, j, k: (i, j)))
        args.append(add)
    return pl.pallas_call(
        body, name=name,
        grid=(M // tm, N // tn, nk),
        in_specs=in_specs,
        out_specs=pl.BlockSpec((tm, tn), lambda i, j, k: (i, j)),
        out_shape=jax.ShapeDtypeStruct((M, N), out_dtype),
        scratch_shapes=[pltpu.VMEM((tm, tn), F32)] if nk > 1 else [],
        compiler_params=_params(("parallel", "parallel", "arbitrary"), est + tm * tn * 4),
    )(*[_hbm(v) for v in args])


def _rowwise(fn, tiles, params, tile_outs, acc_outs=(), *, tm, name):
    tile_arrays, tile_specs = [], []
    for t in tiles:
        if isinstance(t, tuple):
            tile_arrays.append(t[0])
            tile_specs.append(t[1])
        else:
            tile_arrays.append(t)
            tile_specs.append(pl.BlockSpec((tm, t.shape[1]), lambda i: (i, 0)))
    T = tile_arrays[0].shape[0]
    nt, np_, nto, nao = len(tile_arrays), len(params), len(tile_outs), len(acc_outs)

    def body(*refs):
        i = pl.program_id(0)
        tvals = [r[...] for r in refs[:nt]]
        pvals = [r[...] for r in refs[nt:nt + np_]]
        to_refs = refs[nt + np_:nt + np_ + nto]
        ao_refs = refs[nt + np_ + nto:]
        touts, aouts = fn(i, tvals, pvals)
        for r, v in zip(to_refs, touts):
            r[...] = v.astype(r.dtype)
        if nao:
            @pl.when(i == 0)
            def _():
                for r in ao_refs:
                    r[...] = jnp.zeros_like(r)
            for r, v in zip(ao_refs, aouts):
                r[...] += v.astype(r.dtype)

    in_specs = tile_specs + [pl.BlockSpec(p.shape, lambda i: (0, 0)) for p in params]
    out_specs = [pl.BlockSpec((tm, c), lambda i: (i, 0)) for c, _ in tile_outs]
    out_specs += [pl.BlockSpec(s, lambda i: (0, 0)) for s, _ in acc_outs]
    out_shape = [jax.ShapeDtypeStruct((T, c), d) for c, d in tile_outs]
    out_shape += [jax.ShapeDtypeStruct(s, d) for s, d in acc_outs]
    width = sum(s.block_shape[-1] for s in tile_specs) + sum(c for c, _ in tile_outs)
    est = 6 * tm * width * 4 + sum(_nbytes(p.shape, F32) for p in params) * 4
    return pl.pallas_call(
        body, name=name, grid=(T // tm,),
        in_specs=in_specs, out_specs=out_specs, out_shape=out_shape,
        compiler_params=_params(("arbitrary",) if nao else ("parallel",), est),
    )(*[_hbm(v) for v in tile_arrays], *params)


def _ln_fn(alpha, s, xres, y, g, b):
    z = alpha * xres.astype(F32) + s * y.astype(F32)
    mu = jnp.mean(z, axis=-1, keepdims=True)
    zc = z - mu
    var = jnp.mean(zc * zc, axis=-1, keepdims=True)
    return zc * lax.rsqrt(var + LN_EPS) * g + b


def _mm_ln(a, b, lead, xres, g, bias, *, alpha, s, name):
    M, K = a.shape
    N = b.shape[2]
    tm = _tile(M, 256, 16)

    def body(a_ref, b_ref, x_ref, g_ref, bias_ref, y_ref, xo_ref, xb_ref):
        y = _dg(a_ref[...], b_ref[...], 1, 0)
        y_ref[...] = y
        out = _ln_fn(alpha, s, x_ref[...], y, g_ref[...], bias_ref[...])
        xo_ref[...] = out
        xb_ref[...] = out.astype(BF16)

    row = pl.BlockSpec((tm, N), lambda i: (i, 0))
    vec = pl.BlockSpec((1, N), lambda i: (0, 0))
    est = tm * K * 2 + K * N * 2 + tm * N * (4 + 4 + 4 + 2 + 8)
    return pl.pallas_call(
        body, name=name, grid=(M // tm,),
        in_specs=[pl.BlockSpec((tm, K), lambda i: (i, 0)), pl.BlockSpec((None, K, N), lambda i: (lead, 0, 0)),
                  row, vec, vec],
        out_specs=[row, row, row],
        out_shape=[jax.ShapeDtypeStruct((M, N), F32), jax.ShapeDtypeStruct((M, N), F32),
                   jax.ShapeDtypeStruct((M, N), BF16)],
        compiler_params=_params(("parallel",), est),
    )(_hbm(a), _hbm(b), _hbm(xres), g, bias)


def _ln_bwd(xres, y, g, dout, *, alpha, s, name):
    T, D = xres.shape
    tm = _tile(T, 256, 16)

    def body(x_ref, y_ref, d_ref, g_ref, dx_ref, dy_ref, dg_ref, db_ref):
        @pl.when(pl.program_id(0) == 0)
        def _():
            dg_ref[...] = jnp.zeros_like(dg_ref)
            db_ref[...] = jnp.zeros_like(db_ref)

        z = alpha * x_ref[...] + s * y_ref[...]
        zc = z - jnp.mean(z, axis=-1, keepdims=True)
        r = lax.rsqrt(jnp.mean(zc * zc, axis=-1, keepdims=True) + LN_EPS)
        xh = zc * r
        d = d_ref[...]
        dxh = d * g_ref[...]
        dz = r * (dxh - jnp.mean(dxh, axis=-1, keepdims=True) - xh * jnp.mean(dxh * xh, axis=-1, keepdims=True))
        dx_ref[...] = alpha * dz
        dy_ref[...] = (s * dz).astype(dy_ref.dtype)
        dg_ref[...] += jnp.sum(d * xh, axis=0, keepdims=True)
        db_ref[...] += jnp.sum(d, axis=0, keepdims=True)

    row = pl.BlockSpec((tm, D), lambda i: (i, 0))
    vec = pl.BlockSpec((1, D), lambda i: (0, 0))
    return pl.pallas_call(
        body, name=name, grid=(T // tm,),
        in_specs=[row, row, row, vec], out_specs=[row, row, vec, vec],
        out_shape=[jax.ShapeDtypeStruct((T, D), F32), jax.ShapeDtypeStruct((T, D), BF16),
                   jax.ShapeDtypeStruct((1, D), F32), jax.ShapeDtypeStruct((1, D), F32)],
        compiler_params=_params(("arbitrary",), 12 * tm * D * 4),
    )(_hbm(xres), _hbm(y), _hbm(dout), g)


FFN_TILE = 256


def _interleave(w, axis):
    n = w.shape[axis] // (2 * FFN_TILE)
    shp = w.shape[:axis] + (2, n, FFN_TILE) + w.shape[axis + 1:]
    return jnp.swapaxes(w.reshape(shp), axis, axis + 1).reshape(w.shape)


def _deinterleave(w, axis):
    n = w.shape[axis] // (2 * FFN_TILE)
    shp = w.shape[:axis] + (n, 2, FFN_TILE) + w.shape[axis + 1:]
    return jnp.swapaxes(w.reshape(shp), axis, axis + 1).reshape(w.shape)


def _ffn_up(xb, w13t, lead, *, name, ride=None):
    T, D = xb.shape
    F = w13t.shape[1] // 2
    tc = FFN_TILE
    tm = _tile(T, 1024, 16)

    def body(x_ref, w_ref, h_ref, a_ref):
        h = _dg(x_ref[...], w_ref[...], 1, 1)
        g, u = h[:, :tc], h[:, tc:]
        h_ref[...] = h.astype(h_ref.dtype)
        a_ref[...] = (g * jax.nn.sigmoid(g) * u).astype(a_ref.dtype)

    est = (tm * D + 2 * tc * D + 3 * tm * tc) * 2 + 3 * tm * tc * 4
    (h13, a), gathered = _host_call(
        body, name=name, grid=(T // tm, F // tc),
        in_specs=[pl.BlockSpec((tm, D), lambda i, j: (i, 0)),
                  pl.BlockSpec((None, 2 * tc, D), lambda i, j: (lead, j, 0))],
        out_specs=[pl.BlockSpec((tm, 2 * tc), lambda i, j: (i, j)),
                   pl.BlockSpec((tm, tc), lambda i, j: (i, j))],
        out_shape=[jax.ShapeDtypeStruct((T, 2 * F), BF16), jax.ShapeDtypeStruct((T, F), BF16)],
        args=[_hbm(xb), _hbm(w13t)], sem=("parallel", "parallel"), est=est, ride=ride)
    return h13, a, gathered


def _ffn_down_bwd(dyb, w2, lead, h13, *, name):
    T, D = dyb.shape
    F = w2.shape[1]
    tc = FFN_TILE
    tm = _tile(T, 1024, 16)

    def body(dy_ref, w_ref, h_ref, dh_ref):
        d = _dg(dy_ref[...], w_ref[...], 1, 1)
        h = h_ref[...].astype(F32)
        g, u = h[:, :tc], h[:, tc:]
        sig = jax.nn.sigmoid(g)
        gs = g * sig
        dh_ref[...] = jnp.concatenate([d * u * (sig + gs * (1.0 - sig)), d * gs], axis=1).astype(dh_ref.dtype)

    est = (tm * D + tc * D + 4 * tm * tc) * 2 + 6 * tm * tc * 4
    return pl.pallas_call(
        body, name=name, grid=(T // tm, F // tc),
        in_specs=[pl.BlockSpec((tm, D), lambda i, j: (i, 0)),
                  pl.BlockSpec((None, tc, D), lambda i, j: (lead, j, 0)),
                  pl.BlockSpec((tm, 2 * tc), lambda i, j: (i, j))],
        out_specs=pl.BlockSpec((tm, 2 * tc), lambda i, j: (i, j)),
        out_shape=jax.ShapeDtypeStruct((T, 2 * F), BF16),
        compiler_params=_params(("parallel", "parallel"), est),
    )(_hbm(dyb), _hbm(w2), _hbm(h13))


def _pool_select(parts, pw):
    pg = pw // len(POOL_WINDOWS)
    grp = lax.broadcasted_iota(jnp.int32, parts[0].shape, 1) // pg
    out = parts[3]
    for g in (2, 1, 0):
        out = jnp.where(grp == g, parts[g], out)
    return out


def _pool_count(t0, rows, pw):
    pg = pw // len(POOL_WINDOWS)
    grp = lax.broadcasted_iota(jnp.int32, (rows, pw), 1) // pg
    win = jnp.where(grp == 0, POOL_WINDOWS[0],
                    jnp.where(grp == 1, POOL_WINDOWS[1],
                              jnp.where(grp == 2, POOL_WINDOWS[2], POOL_WINDOWS[3])))
    t = t0 + lax.broadcasted_iota(jnp.int32, (rows, pw), 0)
    return jnp.minimum(t + 1, win).astype(F32)


def _window_sums(ext, up):
    n = ext.shape[0]
    sums, cur, k = [], ext, 1
    for _ in POOL_WINDOWS:
        cur = cur + pltpu.roll(cur, (n - k) if up else k, axis=0)
        sums.append(cur)
        k *= 2
    return sums


def _pool_delta(u, halo, t0):
    tm, pw = u.shape
    ext = jnp.concatenate([halo, u], axis=0)
    sums = [s[POOL_HALO:, :] for s in _window_sums(ext, up=False)]
    return _pool_select(sums, pw) / _pool_count(t0, tm, pw) - u


def _pool_fwd(hin, wbd, scale, *, name):
    T = hin.shape[0]
    pw = wbd.shape[0]
    tm = _tile(T, 256, POOL_HALO)
    per = tm // POOL_HALO

    def body(u_ref, halo_ref, w_ref, s_ref, y_ref):
        i = pl.program_id(0)
        halo = jnp.where(i > 0, halo_ref[...], 0.0)
        d = _pool_delta(u_ref[...], halo, i * tm)
        y_ref[...] = (_dg(d, w_ref[...], 1, 0) * s_ref[...]).astype(y_ref.dtype)

    return pl.pallas_call(
        body, name=name, grid=(T // tm,),
        in_specs=[pl.BlockSpec((tm, pw), lambda i: (i, 0)),
                  pl.BlockSpec((POOL_HALO, pw), lambda i: (jnp.maximum(i * per - 1, 0), 0)),
                  pl.BlockSpec((pw, pw), lambda i: (0, 0)),
                  pl.BlockSpec((1, pw), lambda i: (0, 0))],
        out_specs=pl.BlockSpec((tm, pw), lambda i: (i, 0)),
        out_shape=jax.ShapeDtypeStruct((T, pw), BF16),
        compiler_params=_params(("parallel",), 16 * tm * pw * 4),
    )(_hbm(hin), _hbm(hin), wbd, scale)


def _pool_bwd(hin, dcat, wbd, scale, *, name):
    T = hin.shape[0]
    pw = wbd.shape[0]
    tm = _tile(T, 256, POOL_HALO)
    per = tm // POOL_HALO
    nt = T // tm

    def body(u_ref, halo_ref, dy_ref, dyn_ref, w_ref, s_ref, du_ref, dw_ref, ds_ref):
        i = pl.program_id(0)

        @pl.when(i == 0)
        def _():
            dw_ref[...] = jnp.zeros_like(dw_ref)
            ds_ref[...] = jnp.zeros_like(ds_ref)

        halo = jnp.where(i > 0, halo_ref[...], 0.0)
        d = _pool_delta(u_ref[...], halo, i * tm)
        w = w_ref[...]
        sc = s_ref[...]
        dy = dy_ref[...]
        dyn = jnp.where(i < nt - 1, dyn_ref[...], 0.0)
        ds_ref[...] += jnp.sum(dy * _dg(d, w, 1, 0), axis=0, keepdims=True)
        dys = dy * sc
        dw_ref[...] += _dg(d, dys, 0, 0)
        dys_ext = jnp.concatenate([dys, dyn * sc], axis=0)
        dd_ext = _dg(dys_ext, w, 1, 1)
        ddp = dd_ext / _pool_count(i * tm, tm + POOL_HALO, pw)
        sums = [s[:tm, :] for s in _window_sums(ddp, up=True)]
        du_ref[...] = _pool_select(sums, pw) - dd_ext[:tm, :]

    return pl.pallas_call(
        body, name=name, grid=(nt,),
        in_specs=[pl.BlockSpec((tm, pw), lambda i: (i, 0)),
                  pl.BlockSpec((POOL_HALO, pw), lambda i: (jnp.maximum(i * per - 1, 0), 0)),
                  pl.BlockSpec((tm, pw), lambda i: (i, 0)),
                  pl.BlockSpec((POOL_HALO, pw), lambda i: (jnp.minimum((i + 1) * per, nt * per - 1), 0)),
                  pl.BlockSpec((pw, pw), lambda i: (0, 0)),
                  pl.BlockSpec((1, pw), lambda i: (0, 0))],
        out_specs=[pl.BlockSpec((tm, pw), lambda i: (i, 0)),
                   pl.BlockSpec((pw, pw), lambda i: (0, 0)),
                   pl.BlockSpec((1, pw), lambda i: (0, 0))],
        out_shape=[jax.ShapeDtypeStruct((T, pw), F32),
                   jax.ShapeDtypeStruct((pw, pw), F32),
                   jax.ShapeDtypeStruct((1, pw), F32)],
        compiler_params=_params(("arbitrary",), 24 * tm * pw * 4),
    )(_hbm(hin), _hbm(hin), _hbm(dcat), _hbm(dcat), wbd, scale)


def _rms(x, g):
    return x * lax.rsqrt(jnp.mean(x * x, axis=-1, keepdims=True) + RMS_EPS) * g


def _norms_fn(pw, h, gq, gkv):
    o1 = pw + Q_LORA
    o2 = o1 + KV_LORA
    return (_rms(_cols(h, pw, o1), gq), _rms(_cols(h, o1, o2), gkv), _cols(h, o2, h.shape[1]))


def _norms_fwd(hin, gq, gkv, *, pw, name):
    tm = _tile(hin.shape[0], 256, 16)

    def fn(i, tv, pv):
        return _norms_fn(pw, tv[0], pv[0], pv[1]), ()

    return _rowwise(fn, [hin], [gq, gkv], [(Q_LORA, BF16), (KV_LORA, BF16), (LANE, F32)], tm=tm, name=name)


def _norms_bwd(hin, gq, gkv, dcq, dckv, dkpe, du, *, pw, name):
    tm = _tile(hin.shape[0], 256, 16)
    dinp = hin.shape[1]

    def fn(i, tv, pv):
        _, vjp = jax.vjp(functools.partial(_norms_fn, pw), tv[0], pv[0], pv[1])
        dh, dgq, dgkv = vjp((tv[1].astype(F32), tv[2].astype(F32), tv[3].astype(F32)))
        dh = jnp.concatenate([tv[4], dh[:, pw:]], axis=1)
        return (dh,), (dgq, dgkv)

    return _rowwise(fn, [hin, dcq, dckv, dkpe, du], [gq, gkv], [(dinp, BF16)],
                    [((1, Q_LORA), F32), ((1, KV_LORA), F32)], tm=tm, name=name)


def _heads_fn(H, qraw, kv, kpe, rc, rs1, rs2):
    half = QK_ROPE // 2
    scale = (QK_NOPE + QK_ROPE) ** -0.5

    def rope(blk):
        return blk * rc + _lane_roll(blk, -half) * rs1 + _lane_roll(blk, half) * rs2

    krot = rope(kpe)
    qs, ks, vs = [], [], []
    for h in range(H):
        lo = h * HEAD_PAD
        qs += [_cols(qraw, lo, lo + LANE) * scale, rope(_cols(qraw, lo + LANE, lo + HEAD_PAD)) * scale]
        ks += [_cols(kv, lo, lo + LANE), krot]
        vs += [_cols(kv, lo + LANE, lo + HEAD_PAD)]
    return jnp.concatenate(qs, axis=1), jnp.concatenate(ks, axis=1), jnp.concatenate(vs, axis=1)


def _heads_fwd(qraw, kv, kpe, tabs, *, H, name):
    tm = _tile(qraw.shape[0], 256, 16)

    def fn(i, tv, pv):
        q, k, v = _heads_fn(H, *tv)
        ones = (lax.broadcasted_iota(jnp.int32, (tm, LANE), 1) == 0).astype(F32)
        vs = []
        for h in range(H):
            vs += [v[:, h * V_HEAD:(h + 1) * V_HEAD], ones]
        return (q, k, jnp.concatenate(vs, axis=1)), ()

    return _rowwise(fn, [qraw, kv, kpe, *tabs], [],
                    [(H * HEAD_PAD, BF16), (H * HEAD_PAD, BF16), (H * HEAD_PAD, BF16)], tm=tm, name=name)


def _heads_bwd(dq, dk, dv, tabs, *, H, name):
    tm = _tile(dq.shape[0], 256, 16)

    def fn(i, tv, pv):
        z = jnp.zeros((tm, H * HEAD_PAD), F32)
        zk = jnp.zeros((tm, LANE), F32)
        rc, rs1, rs2 = tv[3], tv[4], tv[5]
        _, vjp = jax.vjp(lambda a, b, c: _heads_fn(H, a, b, c, rc, rs1, rs2), z, z, zk)
        return vjp((tv[0], tv[1], tv[2])), ()

    return _rowwise(fn, [dq, dk, dv, *tabs], [],
                    [(H * HEAD_PAD, BF16), (H * HEAD_PAD, BF16), (LANE, F32)], tm=tm, name=name)


FLASH_GROUPS = 2


def _diag_mask(rows, cols, row0):
    r = (row0 + lax.broadcasted_iota(jnp.int32, (rows, cols), 0)) // CHUNK
    c = lax.broadcasted_iota(jnp.int32, (rows, cols), 1) // CHUNK
    return r >= c


def _flash_fwd(qh, kh, vh, *, H, name, ride=None):
    T = qh.shape[0]
    t = _tile(T, 512, CHUNK)

    groups = FLASH_GROUPS if t % (FLASH_GROUPS * CHUNK) == 0 else 1
    tg = t // groups

    def body(q_ref, k_ref, v_ref, o_ref, lse_ref):
        i = pl.program_id(1)
        q = q_ref[...]

        def scores(j):
            return _dg(q, k_ref[pl.ds(pl.multiple_of(j * t, t), t), :], 1, 1)

        def update(j, m, acc, s):
            mn = jnp.maximum(m, jnp.max(s, axis=1, keepdims=True))
            p = jnp.exp((s - mn).astype(BF16))
            return mn, jnp.exp(m - mn) * acc + _dg(p, v_ref[pl.ds(pl.multiple_of(j * t, t), t), :], 1, 0)

        def step(j, carry):
            m, acc, s = carry
            s_next = scores(j + 1)
            m, acc = update(j, m, acc, s)
            return m, acc, s_next

        init = (jnp.full((t, 1), NEG_INF, F32), jnp.zeros((t, HEAD_PAD), F32), scores(0))
        m, acc, s = lax.fori_loop(0, i, step, init)
        m, acc = update(i, m, acc, jnp.where(_diag_mask(t, t, 0), s, NEG_INF))
        l = acc[:, V_HEAD:V_HEAD + 1]
        o_ref[...] = (acc[:, :V_HEAD] / l).astype(o_ref.dtype)
        lse_ref[...] = jnp.broadcast_to(m + jnp.log(l), (t, V_HEAD))

    est = 2 * T * (HEAD_PAD + HEAD_PAD) * 2 + 8 * t * t * 4
    (o, lse), gathered = _host_call(
        body, name=name, grid=(H, T // t),
        in_specs=[pl.BlockSpec((t, HEAD_PAD), lambda h, i: (i, h)),
                  pl.BlockSpec((T, HEAD_PAD), lambda h, i: (0, h)),
                  pl.BlockSpec((T, HEAD_PAD), lambda h, i: (0, h))],
        out_specs=[pl.BlockSpec((t, V_HEAD), lambda h, i: (i, h)),
                   pl.BlockSpec((t, V_HEAD), lambda h, i: (i, h))],
        out_shape=[jax.ShapeDtypeStruct((T, H * V_HEAD), BF16),
                   jax.ShapeDtypeStruct((T, H * V_HEAD), F32)],
        args=[_hbm(qh), _hbm(kh), _hbm(vh)], sem=("parallel", "parallel"), est=est, ride=ride)
    return o, lse, gathered


def _flash_bwd(qh, kh, vh, cat, dcat, lse, *, H, pw, name, ride=None):
    T = qh.shape[0]
    t = _tile(T, 512, CHUNK)
    nb = T // t
    off = pw // V_HEAD
    groups = FLASH_GROUPS if t % (FLASH_GROUPS * CHUNK) == 0 else 1
    tg = t // groups

    def body(q_ref, k_ref, v_ref, o_ref, do_ref, lse_ref, dq_ref, dk_ref, dv_ref):
        j = pl.program_id(1)

        @pl.when(j == 0)
        def _():
            dq_ref[...] = jnp.zeros_like(dq_ref)

        kj = k_ref[...]
        vj = v_ref[...]

        def rows_of(i):
            return pl.ds(pl.multiple_of(i * t, t), t)

        def first_stage(i):
            rows = rows_of(i)
            return _dg(q_ref[rows, :], kj, 1, 1), _dg(do_ref[rows, :], vj, 1, 1)

        def second_stage(i, dk, dv, s, dp):
            rows = rows_of(i)
            qi = q_ref[rows, :]
            doi = do_ref[rows, :]
            p = jnp.exp((s - lse_ref[rows, :][:, :1]).astype(BF16))
            dv = dv + _dg(p, doi, 0, 0)
            di = jnp.sum(doi * o_ref[rows, :].astype(F32), axis=1, keepdims=True)
            ds = p * (dp - di).astype(BF16)
            dk = dk + _dg(ds, qi, 0, 0)
            dq_ref[rows, :] += _dg(ds, kj, 1, 0)
            return dk, dv

        def step(i, carry):
            dk, dv, s, dp = carry
            s_next, dp_next = first_stage(i + 1)
            dk, dv = second_stage(i, dk, dv, s, dp)
            return dk, dv, s_next, dp_next

        s, dp = first_stage(j)
        s = jnp.where(_diag_mask(t, t, 0), s, NEG_INF)
        init = (jnp.zeros((t, HEAD_PAD), F32), jnp.zeros((t, V_HEAD), F32), s, dp)
        dk, dv, s, dp = lax.fori_loop(j, nb - 1, step, init)
        dk, dv = second_stage(nb - 1, dk, dv, s, dp)
        dk_ref[...] = dk
        dv_ref[...] = dv

    est = T * (HEAD_PAD * 2 + V_HEAD * 2 + V_HEAD * 4 + V_HEAD * 4 + HEAD_PAD * 4) + 10 * t * t * 4
    (dq, dk, dv), gathered = _host_call(
        body, name=name, grid=(H, nb),
        in_specs=[pl.BlockSpec((T, HEAD_PAD), lambda h, j: (0, h)),
                  pl.BlockSpec((t, HEAD_PAD), lambda h, j: (j, h)),
                  pl.BlockSpec((t, V_HEAD), lambda h, j: (j, 2 * h)),
                  pl.BlockSpec((T, V_HEAD), lambda h, j: (0, off + h)),
                  pl.BlockSpec((T, V_HEAD), lambda h, j: (0, off + h)),
                  pl.BlockSpec((T, V_HEAD), lambda h, j: (0, h))],
        out_specs=[pl.BlockSpec((T, HEAD_PAD), lambda h, j: (0, h)),
                   pl.BlockSpec((t, HEAD_PAD), lambda h, j: (j, h)),
                   pl.BlockSpec((t, V_HEAD), lambda h, j: (j, h))],
        out_shape=[jax.ShapeDtypeStruct((T, H * HEAD_PAD), F32),
                   jax.ShapeDtypeStruct((T, H * HEAD_PAD), F32),
                   jax.ShapeDtypeStruct((T, H * V_HEAD), F32)],
        args=[_hbm(v) for v in (qh, kh, vh, cat, dcat, lse)], sem=("arbitrary", "arbitrary"), est=est,
        ride=ride)
    return dq, dk, dv, gathered


def _mem_fn(q, k, v):
    hd = q.shape[1] // MEM_HEADS
    outs = []
    for h in range(MEM_HEADS):
        lo, hi = h * hd, (h + 1) * hd
        s = _bdot_nt(_cols(q, lo, hi), _cols(k, lo, hi)) * hd ** -0.5
        e = jnp.exp(s - lax.stop_gradient(jnp.max(s, axis=1, keepdims=True)))
        p = e / jnp.sum(e, axis=1, keepdims=True)
        outs.append(_bdot_nn(p, _cols(v, lo, hi)))
    return jnp.concatenate(outs, axis=1)


def _mem_fwd(q, k, v, *, name):
    T, D = q.shape
    tm = _tile(T, 256, 16)

    def fn(i, tv, pv):
        return (_mem_fn(tv[0], pv[0], pv[1]),), ()

    return _rowwise(fn, [q], [k, v], [(D, BF16)], tm=tm, name=name)[0]


def _mem_bwd(q, k, v, do, *, name):
    T, D = q.shape
    tm = _tile(T, 256, 16)

    def fn(i, tv, pv):
        _, vjp = jax.vjp(_mem_fn, tv[0], pv[0], pv[1])
        dq, dk, dv = vjp(tv[1].astype(F32))
        return (dq,), (dk, dv)

    return _rowwise(fn, [q, do], [k, v], [(D, BF16)], [(k.shape, F32), (v.shape, F32)], tm=tm, name=name)


def _loss_head(y, target, *, name):
    T, D = y.shape
    tm = _tile(T, 256, 16)

    def fn(i, tv, pv):
        err = tv[0] - tv[1]
        part = 0.5 * jnp.sum(jnp.sum(err * err, axis=1, keepdims=True) / D, axis=0, keepdims=True)
        return (err / D,), (jnp.broadcast_to(part, (8, LANE)),)

    return _rowwise(fn, [y, target], [], [(D, F32)], [((8, LANE), F32)], tm=tm, name=name)


def _adamw(w, g, m, v, *, name):
    shape = w.shape
    if w.ndim != 3:
        lead3 = (1, math.prod(shape[:-1]), shape[-1])
        return [o.reshape(shape) for o in _adamw(*[a.reshape(lead3) for a in (w, g, m, v)], name=name)]
    Lw, R, C = shape
    tr = _tile(R, 512, 8)
    b1c = 1.0 - ADAM_B1 ** ADAM_STEP
    b2c = 1.0 - ADAM_B2 ** ADAM_STEP

    def body(w_ref, g_ref, m_ref, v_ref, d_ref, mo_ref, vo_ref):
        gg = g_ref[...]
        mn = ADAM_B1 * m_ref[...] + (1.0 - ADAM_B1) * gg
        vn = ADAM_B2 * v_ref[...] + (1.0 - ADAM_B2) * (gg * gg)
        d_ref[...] = -ADAM_LR * ((mn / b1c) / (jnp.sqrt(vn / b2c) + ADAM_EPS) + ADAM_WD * w_ref[...])
        mo_ref[...] = mn
        vo_ref[...] = vn

    spec = pl.BlockSpec((None, tr, C), lambda l, i: (l, i, 0))
    return pl.pallas_call(
        body, name=name, grid=(Lw, R // tr),
        in_specs=[spec] * 4, out_specs=[spec] * 3,
        out_shape=[jax.ShapeDtypeStruct(shape, F32)] * 3,
        compiler_params=_params(("parallel", "parallel"), 7 * tr * C * 4),
    )(*[_hbm(a) for a in (w, g, m, v)])


def _pair_sum(core, g, landed, off, tr, *, name):
    rows = g.shape[0] // N_DEV
    C = g.shape[1]
    per = rows // tr

    def body(core_ref, g_ref, l_ref, o_ref):
        o_ref[...] = (g_ref[...].astype(F32) + l_ref[...].astype(F32)).astype(o_ref.dtype)

    slab = pl.BlockSpec((None, tr, C), lambda p, i, core_ref: (p, off // tr + i, 0))
    return pl.pallas_call(
        body, name=name,
        grid_spec=pltpu.PrefetchScalarGridSpec(
            num_scalar_prefetch=1, grid=(4, per),
            in_specs=[pl.BlockSpec((tr, C), lambda p, i, core_ref: ((2 * p + core_ref[0]) * per + i, 0)), slab],
            out_specs=slab),
        out_shape=jax.ShapeDtypeStruct(landed.shape, landed.dtype),
        input_output_aliases={2: 0},
        compiler_params=_params(("arbitrary", "arbitrary"), 4 * tr * C * 4),
    )(core, _hbm(g), _hbm(landed))


def _quad_sum(chip, part, gathered, used, *, name):
    C = part.shape[2]
    R = used
    tr = _tile(math.gcd(used, part.shape[1]), 256, 16)

    def body(chip_ref, own_ref, a_ref, b_ref, c_ref, o_ref):
        o_ref[...] = ((own_ref[...].astype(F32) + a_ref[...].astype(F32)) + b_ref[...].astype(F32)) \
            + c_ref[...].astype(F32)

    def other(k):
        return pl.BlockSpec((None, tr, C), lambda i, chip_ref: (chip_ref[0] ^ k, i, 0))

    return pl.pallas_call(
        body, name=name,
        grid_spec=pltpu.PrefetchScalarGridSpec(
            num_scalar_prefetch=1, grid=(R // tr,),
            in_specs=[pl.BlockSpec((None, tr, C), lambda i, chip_ref: (chip_ref[0], i, 0)),
                      other(1), other(2), other(3)],
            out_specs=pl.BlockSpec((tr, C), lambda i, chip_ref: (i, 0))),
        out_shape=jax.ShapeDtypeStruct((R, C), F32),
        compiler_params=_params(("arbitrary",), 8 * tr * C * 4),
    )(chip, _hbm(part), _hbm(gathered), _hbm(gathered), _hbm(gathered))


def _place():
    x, y, c = lax.axis_index("x"), lax.axis_index("y"), lax.axis_index("c")
    return x, y, c


ANY = pl.BlockSpec(memory_space=pl.ANY)


class _Gather:
    def __init__(self, shards):
        self.shards = list(shards)
        self.n = len(self.shards)
        self.out_shape = [jax.ShapeDtypeStruct((s.shape[0], N_DEV * s.shape[1], s.shape[2]), s.dtype)
                          for s in self.shards]
        self.scratch = [pltpu.SemaphoreType.DMA((7 * self.n,)), pltpu.SemaphoreType.DMA((7 * self.n,)),
                        pltpu.SemaphoreType.DMA((self.n,))]
        self.operands = [_hbm(s) for s in self.shards]

    def _bind(self, refs):
        n = self.n
        ins, outs = refs[:n], refs[n:2 * n]
        send_sems, recv_sems, local_sems = refs[2 * n:]
        x, y, c = _place()
        me, sib = (x, y, c), (x, y, 1 - c)
        chips = [(1 - x, y), (x, 1 - y), (1 - x, 1 - y)]

        def rows(w, p):
            r = self.shards[w].shape[1]
            idx = 4 * p[0] + 2 * p[1] + p[2]
            return outs[w].at[:, pl.ds(pl.multiple_of(idx * r, 8), r), :]

        def copy(w, k, block, to, src=None):
            return pltpu.make_async_remote_copy(
                src_ref=rows(w, block) if src is None else src, dst_ref=rows(w, block),
                send_sem=send_sems.at[w * 7 + k], recv_sem=recv_sems.at[w * 7 + k],
                device_id=to, device_id_type=MESH)

        def mine():
            return [pltpu.make_async_copy(ins[w], rows(w, me), local_sems.at[w]) for w in range(n)]

        def first():
            out = []
            for w in range(n):
                out.append(copy(w, 0, me, sib, src=ins[w]))
                out += [copy(w, 1 + j, me, (*chip, c), src=ins[w]) for j, chip in enumerate(chips)]
            return out

        def passed():
            return [copy(w, 4 + j, (*chip, c), sib) for j, chip in enumerate(chips) for w in range(n)]

        def landed():
            return [copy(w, 1 + j, (*chip, c), me) for j, chip in enumerate(chips) for w in range(n)]

        def last():
            out = []
            for w in range(n):
                out.append(copy(w, 0, sib, me))
                out += [copy(w, 4 + j, (*chip, 1 - c), me) for j, chip in enumerate(chips)]
            return out

        return mine, first, landed, passed, last

    def start(self, refs):
        mine, first, _, _, _ = self._bind(refs)
        for cp in mine() + first():
            cp.start()

    def forward(self, refs):
        _, _, landed, passed, _ = self._bind(refs)
        for arrived, fwd in zip(landed(), passed()):
            arrived.wait_recv()
            fwd.start()

    def finish(self, refs):
        mine, first, _, passed, last = self._bind(refs)
        for cp in last():
            cp.wait_recv()
        for cp in first() + passed():
            cp.wait_send()
        for cp in mine():
            cp.wait()


class _ChipExchange:
    def __init__(self, parts, used):
        self.ncl = len(parts)
        self.used = list(used)
        self.out_shape = [jax.ShapeDtypeStruct(p.shape, p.dtype) for p in parts]
        self.scratch = [pltpu.SemaphoreType.DMA((3 * self.ncl,)), pltpu.SemaphoreType.DMA((3 * self.ncl,))]
        self.operands = [_hbm(p) for p in parts]
        self.n = self.ncl

    def _bind(self, refs):
        ncl = self.ncl
        ins, outs = refs[:ncl], refs[ncl:2 * ncl]
        send_sems, recv_sems = refs[2 * ncl:]
        x, y, c = _place()
        chips = [(1 - x, y), (x, 1 - y), (1 - x, 1 - y)]
        here = 2 * x + y

        def copies(outgoing):
            out = []
            for k in range(ncl):
                rows = pl.ds(0, self.used[k])
                for j, (cx, cy) in enumerate(chips):
                    there = 2 * cx + cy
                    src, dst = (there, here) if outgoing else (here, there)
                    out.append(pltpu.make_async_remote_copy(
                        src_ref=ins[k].at[src, rows, :], dst_ref=outs[k].at[dst, rows, :],
                        send_sem=send_sems.at[3 * k + j], recv_sem=recv_sems.at[3 * k + j],
                        device_id=(cx, cy, c), device_id_type=MESH))
            return out

        return copies

    def start(self, refs):
        for cp in self._bind(refs)(True):
            cp.start()

    def forward(self, refs):
        pass

    def finish(self, refs):
        copies = self._bind(refs)
        for cp in copies(False):
            cp.wait_recv()
        for cp in copies(True):
            cp.wait_send()


def _exchange_alone(ex, *, name):
    def body(*refs):
        ex.start(refs)
        ex.forward(refs)
        ex.finish(refs)

    return pl.pallas_call(
        body, name=name, in_specs=[ANY] * ex.n, out_specs=[ANY] * ex.n,
        out_shape=ex.out_shape, scratch_shapes=ex.scratch,
    )(*ex.operands)


def _host_call(body, *, name, grid, in_specs, out_specs, out_shape, args, sem, est, ride=None):
    if ride is None:
        outs = pl.pallas_call(body, name=name, grid=grid, in_specs=in_specs, out_specs=out_specs,
                              out_shape=out_shape, compiler_params=_params(sem, est))(*args)
        return list(outs), []
    n_in, n_out, n = len(in_specs), len(out_specs), ride.n

    def full(*refs):
        ins, rin = refs[:n_in], refs[n_in:n_in + n]
        outs, rout = refs[n_in + n:n_in + n + n_out], refs[n_in + n + n_out:n_in + 2 * n + n_out]
        rrefs = (*rin, *rout, *refs[n_in + 2 * n + n_out:])
        step, total = _ride(ride, rrefs, grid)
        body(*ins, *outs)
        _ride_end(ride, rrefs, step, total)

    outs = pl.pallas_call(
        full, name=name, grid=grid,
        in_specs=list(in_specs) + [ANY] * n, out_specs=list(out_specs) + [ANY] * n,
        out_shape=list(out_shape) + ride.out_shape, scratch_shapes=ride.scratch,
        compiler_params=_params(("arbitrary",) * len(grid), est),
    )(*args, *ride.operands)
    return list(outs[:n_out]), list(outs[n_out:])


def _ride(ex, refs, grid):
    total = math.prod(grid)
    step = pl.program_id(0)
    for axis in range(1, len(grid)):
        step = step * grid[axis] + pl.program_id(axis)
    pl.when(step == 0)(lambda: ex.start(refs))
    return step, total


def _ride_end(ex, refs, step, total):
    pl.when(step == (3 * total) // 4)(lambda: ex.forward(refs))
    pl.when(step == total - 1)(lambda: ex.finish(refs))


def _class_layout(grads, classes):
    used = [0] * len(set(classes))
    offs, tiles = [], []
    for g, cl in zip(grads, classes):
        rows = g.shape[0] // N_DEV
        offs.append(used[cl])
        tiles.append(math.gcd(rows, used[cl]) if used[cl] else rows)
        used[cl] += rows
    heights = []
    for k, u in enumerate(used):
        step = math.lcm(*[t for t, cl in zip(tiles, classes) if cl == k])
        heights.append(-(-u // step) * step)
    return offs, tiles, used, heights


def _rs_to_sibling(grads, classes, *, name):
    n = len(grads)
    offs, _, used, heights = _class_layout(grads, classes)
    ncl = len(heights)
    cols = [next(g.shape[1] for g, cl in zip(grads, classes) if cl == k) for k in range(ncl)]

    def body(*refs):
        gs, land = refs[:n], refs[n:n + ncl]
        send_sems, recv_sems = refs[n + ncl:]
        x, y, c = _place()
        sib = (x, y, 1 - c)
        for p in range(4):
            for w in range(n):
                r = grads[w].shape[0] // N_DEV
                cl = classes[w]
                there = gs[w].at[pl.ds(pl.multiple_of((2 * p + 1 - c) * r, 8), r), :]
                pltpu.make_async_remote_copy(
                    src_ref=there, dst_ref=land[cl].at[p, pl.ds(offs[w], r), :],
                    send_sem=send_sems.at[cl * 4 + p], recv_sem=recv_sems.at[cl * 4 + p],
                    device_id=sib, device_id_type=MESH).start()
        for cl in range(ncl):
            for p in range(4):
                rows_used = land[cl].at[p, pl.ds(0, used[cl]), :]
                slab = pltpu.make_async_remote_copy(
                    src_ref=rows_used, dst_ref=rows_used,
                    send_sem=send_sems.at[cl * 4 + p], recv_sem=recv_sems.at[cl * 4 + p],
                    device_id=sib, device_id_type=MESH)
                slab.wait_send()
                slab.wait_recv()

    return pl.pallas_call(
        body, name=name,
        in_specs=[ANY] * n, out_specs=[ANY] * ncl,
        out_shape=[jax.ShapeDtypeStruct((4, heights[k], cols[k]), BF16) for k in range(ncl)],
        scratch_shapes=[pltpu.SemaphoreType.DMA((4 * ncl,))] * 2,
    )(*[_hbm(g) for g in grads])


def _all_reduce_small(v, *, name):
    R = v.shape[0]

    def body(v_ref, o_ref, buf, send_sems, recv_sems):
        x, y, c = _place()
        me = 4 * x + 2 * y + c
        buf[me] = v_ref[...]
        copies = []
        for k in range(1, N_DEV):
            fx, fy, fc = (k >> 2) & 1, (k >> 1) & 1, k & 1
            to = (x ^ fx, y ^ fy, c ^ fc)
            cp = pltpu.make_async_remote_copy(
                src_ref=v_ref, dst_ref=buf.at[me],
                send_sem=send_sems.at[k - 1], recv_sem=recv_sems.at[k - 1],
                device_id=to, device_id_type=MESH)
            cp.start()
            copies.append(cp)
        for k in range(1, N_DEV):
            fx, fy, fc = (k >> 2) & 1, (k >> 1) & 1, k & 1
            frm = 4 * (x ^ fx) + 2 * (y ^ fy) + (c ^ fc)
            pltpu.make_async_remote_copy(
                src_ref=v_ref, dst_ref=buf.at[frm],
                send_sem=send_sems.at[k - 1], recv_sem=recv_sems.at[k - 1],
                device_id=(x ^ fx, y ^ fy, c ^ fc), device_id_type=MESH).wait_recv()
        for cp in copies:
            cp.wait_send()
        acc = buf[0]
        for d in range(1, N_DEV):
            acc = acc + buf[d]
        o_ref[...] = acc

    vm = pl.BlockSpec(memory_space=pltpu.VMEM)
    return pl.pallas_call(
        body, name=name, in_specs=[vm], out_specs=vm,
        out_shape=jax.ShapeDtypeStruct((R, LANE), F32),
        scratch_shapes=[pltpu.VMEM((N_DEV, R, LANE), F32),
                        pltpu.SemaphoreType.DMA((N_DEV - 1,)), pltpu.SemaphoreType.DMA((N_DEV - 1,))],
        compiler_params=pltpu.CompilerParams(vmem_limit_bytes=VMEM_FLOOR),
    )(v)


def _rope_tables(positions):
    half = QK_ROPE // 2
    inv_freq = ROPE_BASE ** (-jnp.arange(half, dtype=F32) / half)
    ang = positions.astype(F32)[:, None] * inv_freq
    cos, sin = jnp.cos(ang), jnp.sin(ang)
    z = jnp.zeros_like(cos)
    z2 = jnp.zeros((positions.shape[0], LANE - QK_ROPE), F32)
    rc = jnp.concatenate([cos, cos, z2], axis=1)
    rs1 = jnp.concatenate([-sin, z, z2], axis=1)
    rs2 = jnp.concatenate([z, sin, z2], axis=1)
    return rc, rs1, rs2


def _block_diag(pool_w):
    G, pg, _ = pool_w.shape
    out = jnp.zeros((G * pg, G * pg), pool_w.dtype)
    for g in range(G):
        out = lax.dynamic_update_slice(out, pool_w[g], (g * pg, g * pg))
    return out


def kernel(x, mem, positions, ln_g, ln_b, ffn1_w13, ffn1_w2, w_in, pool_w, pool_scale, q_norm_g, w_uq, kv_norm_g, w_ukv, w_out, mem_wq, mem_wkv, mem_wo, ffn2_w13, ffn2_w2, loss_target, m_ln_g, m_ln_b, m_ffn1_w13, m_ffn1_w2, m_w_in, m_pool_w, m_pool_scale, m_q_norm_g, m_w_uq, m_kv_norm_g, m_w_ukv, m_w_out, m_mem_wq, m_mem_wkv, m_mem_wo, m_ffn2_w13, m_ffn2_w2, v_ln_g, v_ln_b, v_ffn1_w13, v_ffn1_w2, v_w_in, v_pool_w, v_pool_scale, v_q_norm_g, v_w_uq, v_kv_norm_g, v_w_ukv, v_w_out, v_mem_wq, v_mem_wkv, v_mem_wo, v_ffn2_w13, v_ffn2_w2):
    L = ln_g.shape[0]
    T, D = x.shape[1], x.shape[2]
    F = ffn1_w2.shape[1] * N_DEV
    PW = D // 4
    H = (D - PW) // V_HEAD
    DIN = w_in.shape[2]
    DINP = PW + Q_LORA + KV_LORA + LANE
    QW = QK_NOPE + QK_ROPE
    alpha = (2 * L) ** 0.25
    x2d = x.reshape(T, D)
    memb = mem.reshape(mem.shape[1], D).astype(BF16)
    target = loss_target.reshape(T, D)
    tabs = _rope_tables(positions.reshape(T))

    def shards_of(l):
        return dict(
            w13a=ffn1_w13[l].T[None].astype(BF16),
            w13b=ffn2_w13[l].T[None].astype(BF16),
            w2=jnp.stack([ffn1_w2[l], ffn2_w2[l]]).astype(BF16),
            wsq=jnp.stack([w_out[l], mem_wq[l], mem_wo[l]]).astype(BF16),
            wkvT=mem_wkv[l].T[None].astype(BF16),
            winp=jnp.pad(w_in[l], ((0, 0), (0, DINP - DIN)))[None].astype(BF16),
            wuqT=w_uq[l].T[None].astype(BF16),
            wukvT=w_ukv[l].T[None].astype(BF16),
        )

    MID = ("w2", "wsq", "wkvT", "winp", "wuqT", "wukvT")

    def finish_weights(g):
        wuq = jnp.pad(g["wuqT"].reshape(H, QW, Q_LORA), ((0, 0), (0, HEAD_PAD - QW), (0, 0)))
        return dict(g, w13T=[_interleave(g["w13a"], 1), _interleave(g["w13b"], 1)],
                    wuqT=wuq.reshape(1, H * HEAD_PAD, Q_LORA))

    names0 = ["w13a", "w13b", *MID]
    sh0 = shards_of(0)
    ln_shard = jnp.concatenate([ln_g.reshape(1, 4 * L, -1), ln_b.reshape(1, 4 * L, -1)], axis=1)
    got = _exchange_alone(_Gather([sh0[k] for k in names0] + [ln_shard]), name="ag_layer0")
    W = [finish_weights(dict(zip(names0, got[:-1])))]
    lnp = jnp.moveaxis(got[-1].reshape(N_DEV, 2, L, 4, D // N_DEV), 0, 3).reshape(2, L, 4, D)
    lng, lnb = lnp[0], lnp[1]
    wbd = [_block_diag(pool_w[l]).astype(BF16) for l in range(L)]

    def ffn_fwd(l, which, xres, xb, k, ride):
        h13, a, rode = _ffn_up(xb, W[l]["w13T"][which], 0, name=f"l{l}_ffn{which}_up", ride=ride)
        y, xo, xob = _mm_ln(a, W[l]["w2"], which, xres, lng[l, k:k + 1], lnb[l, k:k + 1], alpha=alpha, s=0.5,
                            name=f"l{l}_ffn{which}_y_ln{k}")
        return dict(xres=xres, xb=xb, h13=h13, a=a, y=y), xo, xob, rode

    saved = []
    xres, xb = x2d, x2d.astype(BF16)
    for l in range(L):
        sv = {}
        nxt = shards_of(l + 1) if l + 1 < L else None
        Wl = W[l]
        sv["ffn1"], x1, x1b, got_a = ffn_fwd(l, 0, xres, xb, 0, _Gather([nxt["w13a"]]) if nxt else None)
        hin = _mm(x1b, Wl["winp"], lead=0, name=f"l{l}_hin")
        pscale = pool_scale[l].reshape(1, PW)
        gq, gkv = q_norm_g[l].reshape(1, Q_LORA), kv_norm_g[l].reshape(1, KV_LORA)
        ypool = _pool_fwd(hin, wbd[l], pscale, name=f"l{l}_pool")
        cqn, ckvn, kpe = _norms_fwd(hin, gq, gkv, pw=PW, name=f"l{l}_norms")
        qraw = _mm(cqn, Wl["wuqT"], lead=0, tb=True, name=f"l{l}_qraw")
        kv = _mm(ckvn, Wl["wukvT"], lead=0, tb=True, name=f"l{l}_kv")
        qh, kh, vh = _heads_fwd(qraw, kv, kpe, tabs, H=H, name=f"l{l}_heads")
        o, lse, got_mid = _flash_fwd(qh, kh, vh, H=H, name=f"l{l}_flash",
                                     ride=_Gather([nxt[k] for k in MID]) if nxt else None)
        cat = jnp.concatenate([ypool, o], axis=1)
        ymix, x2, x2b = _mm_ln(cat, Wl["wsq"], 0, x1, lng[l, 1:2], lnb[l, 1:2], alpha=alpha, s=1.0,
                               name=f"l{l}_ymix_ln1")
        qm = _mm(x2b, Wl["wsq"], lead=1, out_dtype=BF16, name=f"l{l}_qm")
        kvm = _mm(memb, Wl["wkvT"], lead=0, tb=True, name=f"l{l}_kvm")
        km, vm = kvm[:, :D], kvm[:, D:]
        om = _mem_fwd(qm, km, vm, name=f"l{l}_memattn")
        ymem, x3, x3b = _mm_ln(om, Wl["wsq"], 2, x2, lng[l, 2:3], lnb[l, 2:3], alpha=alpha, s=1.0,
                               name=f"l{l}_ymem_ln2")
        sv["ffn2"], x4, x4b, got_b = ffn_fwd(l, 1, x3, x3b, 3, _Gather([nxt["w13b"]]) if nxt else None)
        if nxt:
            W.append(finish_weights(dict(w13a=got_a[0], w13b=got_b[0], **dict(zip(MID, got_mid)))))
        sv.update(x1=x1, x1b=x1b, hin=hin, pscale=pscale, gq=gq, gkv=gkv, cqn=cqn, ckvn=ckvn,
                  qh=qh, kh=kh, vh=vh, lse=lse, cat=cat, ymix=ymix, x2=x2, x2b=x2b, qm=qm, km=km, vm=vm,
                  om=om, ymem=ymem)
        saved.append(sv)
        xres, xb = x4, x4b

    dx, loss_blk = _loss_head(xres, target, name="loss_head")
    loss = lax.psum(loss_blk[0, 0], ("x", "y", "c"))

    gW = {}
    gS = {}

    def ffn_bwd(l, which, sv, dx, k):
        tag = f"l{l}_ffn{which}"
        dxres, dyb, dg, db = _ln_bwd(sv["xres"], sv["y"], lng[l, k:k + 1], dx,
                                     alpha=alpha, s=0.5, name=f"l{l}_ln{k}_bwd")
        gW[("w2", which, l)] = _mm(sv["a"], dyb, ta=True, out_dtype=BF16, name=f"{tag}_dw2", tn=D)
        dh = _ffn_down_bwd(dyb, W[l]["w2"], which, sv["h13"], name=f"{tag}_dh")
        dxn = _mm(dh, W[l]["w13T"][which], lead=0, add=dxres, name=f"{tag}_dx", tn=D)
        dw13 = _mm(dh, sv["xb"], ta=True, out_dtype=BF16, name=f"{tag}_dw13", tn=D)
        gW[("w13", which, l)] = _deinterleave(dw13, 0)
        gS[("ln_g", l, k)], gS[("ln_b", l, k)] = dg, db
        return dxn

    core = lax.axis_index("c").astype(jnp.int32).reshape(1)
    chip = (2 * lax.axis_index("x") + lax.axis_index("y")).astype(jnp.int32).reshape(1)
    gsh = {}

    def rs_first_level(l):
        keys = [("w13", 0, l), ("w13", 1, l), ("w2", 0, l), ("w2", 1, l), ("mem_wkv", l),
                ("w_out", l), ("mem_wq", l), ("mem_wo", l), ("w_in", l), ("w_uq", l), ("w_ukv", l)]
        classes = [0] * 8 + [1, 2, 3]
        garrs = []
        for key in keys:
            g = gW[key]
            if key[0] == "w_uq":
                g = g.reshape(H, HEAD_PAD, Q_LORA)[:, :QW, :].reshape(H * QW, Q_LORA)
            garrs.append(g)
        offs, tiles, used, _ = _class_layout(garrs, classes)
        parts = list(_rs_to_sibling(garrs, classes, name=f"l{l}_rs_sibling"))
        for w, (g, cl, off, tr) in enumerate(zip(garrs, classes, offs, tiles)):
            parts[cl] = _pair_sum(core, g, parts[cl], off, tr, name=f"l{l}_rs_pair_sum_{w}")
        return dict(l=l, keys=keys, garrs=garrs, classes=classes, offs=offs, used=used, parts=parts,
                    ex=_ChipExchange(parts, used))

    def rs_last_level(st, gathered):
        sums = [_quad_sum(chip, p, a, u, name=f"l{st['l']}_rs_quad_sum{k}")
                for k, (p, a, u) in enumerate(zip(st["parts"], gathered, st["used"]))]
        for key, g, cl, off in zip(st["keys"], st["garrs"], st["classes"], st["offs"]):
            gsh[key] = sums[cl][off:off + g.shape[0] // N_DEV, :]

    pending = None
    for l in reversed(range(L)):
        sv = saved[l]
        Wl = W[l]
        dx = ffn_bwd(l, 1, sv["ffn2"], dx, 3)
        dxres, dyb, dg, db = _ln_bwd(sv["x2"], sv["ymem"], lng[l, 2:3], dx,
                                     alpha=alpha, s=1.0, name=f"l{l}_ln2_bwd")
        gS[("ln_g", l, 2)], gS[("ln_b", l, 2)] = dg, db
        dom = _mm(dyb, Wl["wsq"], lead=2, tb=True, out_dtype=BF16, name=f"l{l}_dom")
        gW[("mem_wo", l)] = _mm(sv["om"], dyb, ta=True, out_dtype=BF16, name=f"l{l}_dwo", tn=D)
        dqm, dkm, dvm = _mem_bwd(sv["qm"], sv["km"], sv["vm"], dom, name=f"l{l}_memattn_bwd")
        dx = _mm(dqm, Wl["wsq"], lead=1, tb=True, add=dxres, name=f"l{l}_dx2", tn=D)
        gW[("mem_wq", l)] = _mm(sv["x2b"], dqm, ta=True, out_dtype=BF16, name=f"l{l}_dwq", tn=D)
        dkvm = jnp.concatenate([dkm, dvm], axis=1).astype(BF16)
        gW[("mem_wkv", l)] = _mm(dkvm, memb, ta=True, out_dtype=BF16, name=f"l{l}_dwkv", tn=D)
        dxres, dyb, dg, db = _ln_bwd(sv["x1"], sv["ymix"], lng[l, 1:2], dx,
                                     alpha=alpha, s=1.0, name=f"l{l}_ln1_bwd")
        gS[("ln_g", l, 1)], gS[("ln_b", l, 1)] = dg, db
        dcat = _mm(dyb, Wl["wsq"], lead=0, tb=True, name=f"l{l}_dcat", tn=D)
        gW[("w_out", l)] = _mm(sv["cat"], dyb, ta=True, out_dtype=BF16, name=f"l{l}_dwout", tn=D)
        dqh, dkh, dvh, rode = _flash_bwd(sv["qh"], sv["kh"], sv["vh"], sv["cat"], dcat, sv["lse"], H=H, pw=PW,
                                         name=f"l{l}_flash_bwd", ride=pending["ex"] if pending else None)
        if pending:
            rs_last_level(pending, rode)
        dqraw, dkv, dkpe = _heads_bwd(dqh, dkh, dvh, tabs, H=H, name=f"l{l}_heads_bwd")
        dcq = _mm(dqraw, Wl["wuqT"], lead=0, name=f"l{l}_dcq")
        gW[("w_uq", l)] = _mm(dqraw, sv["cqn"], ta=True, out_dtype=BF16, name=f"l{l}_dwuq")
        dckv = _mm(dkv, Wl["wukvT"], lead=0, name=f"l{l}_dckv")
        gW[("w_ukv", l)] = _mm(dkv, sv["ckvn"], ta=True, out_dtype=BF16, name=f"l{l}_dwukv")
        du, dwbd, dps = _pool_bwd(sv["hin"], dcat, wbd[l], sv["pscale"], name=f"l{l}_pool_bwd")
        dhin, dgq, dgkv = _norms_bwd(sv["hin"], sv["gq"], sv["gkv"], dcq, dckv, dkpe, du, pw=PW,
                                     name=f"l{l}_norms_bwd")
        pg = PW // len(POOL_WINDOWS)
        gS[("pool_w", l)] = jnp.stack([dwbd[g * pg:(g + 1) * pg, g * pg:(g + 1) * pg]
                                       for g in range(len(POOL_WINDOWS))])
        gS[("pool_scale", l)], gS[("q_norm_g", l)], gS[("kv_norm_g", l)] = dps, dgq, dgkv
        dx = _mm(dhin, Wl["winp"], lead=0, tb=True, add=dxres, name=f"l{l}_dx1", tn=D)
        gW[("w_in", l)] = _mm(sv["x1b"], dhin, ta=True, out_dtype=BF16, name=f"l{l}_dwin", tn=DINP)
        dx = ffn_bwd(l, 0, sv["ffn1"], dx, 0)
        pending = rs_first_level(l)
    rs_last_level(pending, _exchange_alone(pending["ex"], name="rs_chips_last"))
    grad_x = dx.reshape(1, T, D)

    small_keys = []
    for l in range(L):
        small_keys += [("pool_w", l), ("pool_scale", l), ("q_norm_g", l), ("kv_norm_g", l)]
        small_keys += [("ln_g", l, k) for k in range(4)] + [("ln_b", l, k) for k in range(4)]
    flat = jnp.concatenate([gS[k].reshape(-1) for k in small_keys])
    n_small = flat.shape[0]
    rows = -(-n_small // (8 * LANE)) * 8
    flat = jnp.pad(flat, (0, rows * LANE - n_small)).reshape(rows, LANE)
    red = _all_reduce_small(flat, name="ar_small").reshape(-1)
    gsm, pos = {}, 0
    for k in small_keys:
        size = math.prod(gS[k].shape)
        gsm[k] = red[pos:pos + size].reshape(gS[k].shape)
        pos += size

    me = 4 * lax.axis_index("x") + 2 * lax.axis_index("y") + lax.axis_index("c")
    dsh = D // N_DEV
    stack = lambda f: jnp.stack([f(l) for l in range(L)])
    g_ln_g = stack(lambda l: jnp.concatenate([gsm[("ln_g", l, k)] for k in range(4)], axis=0))
    g_ln_b = stack(lambda l: jnp.concatenate([gsm[("ln_b", l, k)] for k in range(4)], axis=0))
    grads = {
        "ln_g": lax.dynamic_slice_in_dim(g_ln_g, me * dsh, dsh, axis=2),
        "ln_b": lax.dynamic_slice_in_dim(g_ln_b, me * dsh, dsh, axis=2),
        "ffn1_w13": stack(lambda l: gsh[("w13", 0, l)].T),
        "ffn1_w2": stack(lambda l: gsh[("w2", 0, l)]),
        "w_in": stack(lambda l: gsh[("w_in", l)][:, :DIN]),
        "pool_w": stack(lambda l: gsm[("pool_w", l)]),
        "pool_scale": stack(lambda l: gsm[("pool_scale", l)].reshape(PW)),
        "q_norm_g": stack(lambda l: gsm[("q_norm_g", l)].reshape(Q_LORA)),
        "w_uq": stack(lambda l: gsh[("w_uq", l)].T),
        "kv_norm_g": stack(lambda l: gsm[("kv_norm_g", l)].reshape(KV_LORA)),
        "w_ukv": stack(lambda l: gsh[("w_ukv", l)].T),
        "w_out": stack(lambda l: gsh[("w_out", l)]),
        "mem_wq": stack(lambda l: gsh[("mem_wq", l)]),
        "mem_wkv": stack(lambda l: gsh[("mem_wkv", l)].T),
        "mem_wo": stack(lambda l: gsh[("mem_wo", l)]),
        "ffn2_w13": stack(lambda l: gsh[("w13", 1, l)].T),
        "ffn2_w2": stack(lambda l: gsh[("w2", 1, l)]),
    }

    names = ["ln_g", "ln_b", "ffn1_w13", "ffn1_w2", "w_in", "pool_w", "pool_scale", "q_norm_g", "w_uq",
             "kv_norm_g", "w_ukv", "w_out", "mem_wq", "mem_wkv", "mem_wo", "ffn2_w13", "ffn2_w2"]
    weights = dict(ln_g=ln_g, ln_b=ln_b, ffn1_w13=ffn1_w13, ffn1_w2=ffn1_w2, w_in=w_in, pool_w=pool_w,
                   pool_scale=pool_scale, q_norm_g=q_norm_g, w_uq=w_uq, kv_norm_g=kv_norm_g, w_ukv=w_ukv,
                   w_out=w_out, mem_wq=mem_wq, mem_wkv=mem_wkv, mem_wo=mem_wo, ffn2_w13=ffn2_w13,
                   ffn2_w2=ffn2_w2)
    ms = dict(ln_g=m_ln_g, ln_b=m_ln_b, ffn1_w13=m_ffn1_w13, ffn1_w2=m_ffn1_w2, w_in=m_w_in, pool_w=m_pool_w,
              pool_scale=m_pool_scale, q_norm_g=m_q_norm_g, w_uq=m_w_uq, kv_norm_g=m_kv_norm_g,
              w_ukv=m_w_ukv, w_out=m_w_out, mem_wq=m_mem_wq, mem_wkv=m_mem_wkv, mem_wo=m_mem_wo,
              ffn2_w13=m_ffn2_w13, ffn2_w2=m_ffn2_w2)
    vs = dict(ln_g=v_ln_g, ln_b=v_ln_b, ffn1_w13=v_ffn1_w13, ffn1_w2=v_ffn1_w2, w_in=v_w_in, pool_w=v_pool_w,
              pool_scale=v_pool_scale, q_norm_g=v_q_norm_g, w_uq=v_w_uq, kv_norm_g=v_kv_norm_g,
              w_ukv=v_w_ukv, w_out=v_w_out, mem_wq=v_mem_wq, mem_wkv=v_mem_wkv, mem_wo=v_mem_wo,
              ffn2_w13=v_ffn2_w13, ffn2_w2=v_ffn2_w2)
    deltas, new_m, new_v = [], [], []
    for nme in names:
        d, mn, vn = _adamw(weights[nme], grads[nme], ms[nme], vs[nme], name=f"adamw_{nme}")
        deltas.append(d)
        new_m.append(mn)
        new_v.append(vn)
    return (loss, grad_x, *[grads[nme] for nme in names], *deltas, *new_m, *new_v)
```

```python
import functools
import math

import jax
import jax.numpy as jnp
from jax import lax
from jax.experimental import pallas as pl
from jax.experimental.pallas import tpu as pltpu

F32 = jnp.float32
BF16 = jnp.bfloat16
MESH = pl.DeviceIdType.MESH

CHUNK = 64
MEM_HEADS = 4
POOL_WINDOWS = (2, 4, 8, 16)
QK_NOPE = 128
QK_ROPE = 64
V_HEAD = 128
Q_LORA = 256
KV_LORA = 128
ROPE_BASE = 10000.0
LN_EPS = 1e-5
RMS_EPS = 1e-6
NEG_INF = -1e30
ADAM_LR = 0.001
ADAM_B1 = 0.9
ADAM_B2 = 0.999
ADAM_EPS = 1e-08
ADAM_WD = 0.01
ADAM_STEP = 10

N_DEV = 8
LANE = 128
HEAD_PAD = 2 * LANE
POOL_HALO = 16
VMEM_CAP = 56 * 1024 * 1024
VMEM_FLOOR = 32 * 1024 * 1024


def _tile(n, pref, mult):
    t = (min(pref, n) // mult) * mult
    while t >= mult:
        if n % t == 0:
            return t
        t -= mult
    return n


def _params(sem, est_bytes):
    limit = int(min(max(2 * est_bytes + (8 << 20), VMEM_FLOOR), VMEM_CAP))
    return pltpu.CompilerParams(dimension_semantics=sem, vmem_limit_bytes=limit)


def _nbytes(shape, dtype):
    return math.prod(shape) * jnp.dtype(dtype).itemsize


def _hbm(x):
    return pltpu.with_memory_space_constraint(x, pltpu.HBM)


def _dg(a, b, ca, cb):
    return lax.dot_general(a.astype(BF16), b.astype(BF16), (((ca,), (cb,)), ((), ())),
                           preferred_element_type=F32)


@jax.custom_vjp
def _bdot_nn(a, b):
    return _dg(a, b, 1, 0)


def _bdot_nn_fwd(a, b):
    return _dg(a, b, 1, 0), (a, b)


def _bdot_nn_bwd(res, ct):
    a, b = res
    return _dg(ct, b, 1, 1).astype(a.dtype), _dg(a, ct, 0, 0).astype(b.dtype)


_bdot_nn.defvjp(_bdot_nn_fwd, _bdot_nn_bwd)


@jax.custom_vjp
def _bdot_nt(a, b):
    return _dg(a, b, 1, 1)


def _bdot_nt_fwd(a, b):
    return _dg(a, b, 1, 1), (a, b)


def _bdot_nt_bwd(res, ct):
    a, b = res
    return _dg(ct, b, 1, 0).astype(a.dtype), _dg(ct, a, 0, 0).astype(b.dtype)


_bdot_nt.defvjp(_bdot_nt_fwd, _bdot_nt_bwd)


@functools.partial(jax.custom_vjp, nondiff_argnums=(1,))
def _lane_roll(x, shift):
    return pltpu.roll(x, shift % x.shape[1], axis=1)


def _lane_roll_fwd(x, shift):
    return _lane_roll(x, shift), None


def _lane_roll_bwd(shift, _, ct):
    return (_lane_roll(ct, -shift),)


_lane_roll.defvjp(_lane_roll_fwd, _lane_roll_bwd)


@functools.partial(jax.custom_vjp, nondiff_argnums=(1, 2))
def _cols(x, lo, hi):
    return x[:, lo:hi]


def _cols_fwd(x, lo, hi):
    return x[:, lo:hi], x.shape[1]


def _cols_bwd(lo, hi, width, ct):
    parts = []
    if lo > 0:
        parts.append(jnp.zeros((ct.shape[0], lo), ct.dtype))
    parts.append(ct)
    if hi < width:
        parts.append(jnp.zeros((ct.shape[0], width - hi), ct.dtype))
    return (jnp.concatenate(parts, axis=1) if len(parts) > 1 else ct,)


_cols.defvjp(_cols_fwd, _cols_bwd)


MM_VMEM_BUDGET = 20 * 1024 * 1024


def _mm(a, b, *, name, ta=False, tb=False, out_dtype=F32, lead=None, add=None, add_scale=1.0,
        tm=1024, tn=1024, tk=2816):
    if ta:
        K, M = a.shape
    else:
        M, K = a.shape
    bshape = b.shape[1:] if lead is not None else b.shape
    if tb:
        N, Kb = bshape
    else:
        Kb, N = bshape
    assert K == Kb, (name, a.shape, b.shape)

    def blocks(tm, tn, tk):
        tm = _tile(M, tm, LANE if ta else 16)
        tn = _tile(N, tn, LANE)
        tk = _tile(K, tk, LANE)
        nbytes = (tm * tk * a.dtype.itemsize + tk * tn * b.dtype.itemsize
                  + tm * tn * (jnp.dtype(out_dtype).itemsize + (4 if K // tk > 1 else 0)
                               + (add.dtype.itemsize if add is not None else 0)))
        return tm, tn, tk, nbytes

    tm, tn, tk, est = blocks(tm, tn, tk)
    for shrink in ("m", "k", "m", "k", "n"):
        if est <= MM_VMEM_BUDGET:
            break
        if shrink == "m":
            tm, tn, tk, est = blocks(max(tm // 2, LANE), tn, tk)
        elif shrink == "k":
            tm, tn, tk, est = blocks(tm, tn, max(tk // 2, LANE))
        else:
            tm, tn, tk, est = blocks(tm, max(tn // 2, LANE), tk)
    nk = K // tk
    ca = 0 if ta else 1
    cb = 1 if tb else 0

    def body(*refs):
        a_ref, b_ref = refs[0], refs[1]
        add_ref = refs[2] if add is not None else None
        o_ref = refs[3] if add is not None else refs[2]

        def finish(r):
            if add_ref is not None:
                r = r + add_scale * add_ref[...].astype(F32)
            o_ref[...] = r.astype(o_ref.dtype)

        if nk == 1:
            finish(_dg(a_ref[...], b_ref[...], ca, cb))
            return
        acc_ref = refs[-1]
        k = pl.program_id(2)

        @pl.when(k == 0)
        def _():
            acc_ref[...] = jnp.zeros_like(acc_ref)

        acc_ref[...] += _dg(a_ref[...], b_ref[...], ca, cb)

        @pl.when(k == nk - 1)
        def _():
            finish(acc_ref[...])

    a_blk = (tk, tm) if ta else (tm, tk)
    a_map = (lambda i, j, k: (k, i)) if ta else (lambda i, j, k: (i, k))
    b_blk = (tn, tk) if tb else (tk, tn)
    if lead is None:
        b_map = (lambda i, j, k: (j, k)) if tb else (lambda i, j, k: (k, j))
        b_spec = pl.BlockSpec(b_blk, b_map)
    else:
        b_map = (lambda i, j, k: (lead, j, k)) if tb else (lambda i, j, k: (lead, k, j))
        b_spec = pl.BlockSpec((None,) + b_blk, b_map)
    in_specs = [pl.BlockSpec(a_blk, a_map), b_spec]
    args = [a, b]
    if add is not None:
        in_specs.append(pl.BlockSpec((tm, tn), lambda i, j, k: (i, j)))
        args.append(add)
    return pl.pallas_call(
        body, name=name,
        grid=(M // tm, N // tn, nk),
        in_specs=in_specs,
        out_specs=pl.BlockSpec((tm, tn), lambda i, j, k: (i, j)),
        out_shape=jax.ShapeDtypeStruct((M, N), out_dtype),
        scratch_shapes=[pltpu.VMEM((tm, tn), F32)] if nk > 1 else [],
        compiler_params=_params(("parallel", "parallel", "arbitrary"), est + tm * tn * 4),
    )(*[_hbm(v) for v in args])


def _rowwise(fn, tiles, params, tile_outs, acc_outs=(), *, tm, name):
    tile_arrays, tile_specs = [], []
    for t in tiles:
        if isinstance(t, tuple):
            tile_arrays.append(t[0])
            tile_specs.append(t[1])
        else:
            tile_arrays.append(t)
            tile_specs.append(pl.BlockSpec((tm, t.shape[1]), lambda i: (i, 0)))
    T = tile_arrays[0].shape[0]
    nt, np_, nto, nao = len(tile_arrays), len(params), len(tile_outs), len(acc_outs)

    def body(*refs):
        i = pl.program_id(0)
        tvals = [r[...] for r in refs[:nt]]
        pvals = [r[...] for r in refs[nt:nt + np_]]
        to_refs = refs[nt + np_:nt + np_ + nto]
        ao_refs = refs[nt + np_ + nto:]
        touts, aouts = fn(i, tvals, pvals)
        for r, v in zip(to_refs, touts):
            r[...] = v.astype(r.dtype)
        if nao:
            @pl.when(i == 0)
            def _():
                for r in ao_refs:
                    r[...] = jnp.zeros_like(r)
            for r, v in zip(ao_refs, aouts):
                r[...] += v.astype(r.dtype)

    in_specs = tile_specs + [pl.BlockSpec(p.shape, lambda i: (0, 0)) for p in params]
    out_specs = [pl.BlockSpec((tm, c), lambda i: (i, 0)) for c, _ in tile_outs]
    out_specs += [pl.BlockSpec(s, lambda i: (0, 0)) for s, _ in acc_outs]
    out_shape = [jax.ShapeDtypeStruct((T, c), d) for c, d in tile_outs]
    out_shape += [jax.ShapeDtypeStruct(s, d) for s, d in acc_outs]
    width = sum(s.block_shape[-1] for s in tile_specs) + sum(c for c, _ in tile_outs)
    est = 6 * tm * width * 4 + sum(_nbytes(p.shape, F32) for p in params) * 4
    return pl.pallas_call(
        body, name=name, grid=(T // tm,),
        in_specs=in_specs, out_specs=out_specs, out_shape=out_shape,
        compiler_params=_params(("arbitrary",) if nao else ("parallel",), est),
    )(*[_hbm(v) for v in tile_arrays], *params)


def _ln_fn(alpha, s, xres, y, g, b):
    z = alpha * xres.astype(F32) + s * y.astype(F32)
    mu = jnp.mean(z, axis=-1, keepdims=True)
    zc = z - mu
    var = jnp.mean(zc * zc, axis=-1, keepdims=True)
    return zc * lax.rsqrt(var + LN_EPS) * g + b


def _mm_ln(a, b, lead, xres, g, bias, *, alpha, s, name):
    M, K = a.shape
    N = b.shape[2]
    tm = _tile(M, 256, 16)

    def body(a_ref, b_ref, x_ref, g_ref, bias_ref, y_ref, xo_ref, xb_ref):
        y = _dg(a_ref[...], b_ref[...], 1, 0)
        y_ref[...] = y
        out = _ln_fn(alpha, s, x_ref[...], y, g_ref[...], bias_ref[...])
        xo_ref[...] = out
        xb_ref[...] = out.astype(BF16)

    row = pl.BlockSpec((tm, N), lambda i: (i, 0))
    vec = pl.BlockSpec((1, N), lambda i: (0, 0))
    est = tm * K * 2 + K * N * 2 + tm * N * (4 + 4 + 4 + 2 + 8)
    return pl.pallas_call(
        body, name=name, grid=(M // tm,),
        in_specs=[pl.BlockSpec((tm, K), lambda i: (i, 0)), pl.BlockSpec((None, K, N), lambda i: (lead, 0, 0)),
                  row, vec, vec],
        out_specs=[row, row, row],
        out_shape=[jax.ShapeDtypeStruct((M, N), F32), jax.ShapeDtypeStruct((M, N), F32),
                   jax.ShapeDtypeStruct((M, N), BF16)],
        compiler_params=_params(("parallel",), est),
    )(_hbm(a), _hbm(b), _hbm(xres), g, bias)


def _ln_bwd(xres, y, g, dout, *, alpha, s, name):
    T, D = xres.shape
    tm = _tile(T, 256, 16)

    def body(x_ref, y_ref, d_ref, g_ref, dx_ref, dy_ref, dg_ref, db_ref):
        @pl.when(pl.program_id(0) == 0)
        def _():
            dg_ref[...] = jnp.zeros_like(dg_ref)
            db_ref[...] = jnp.zeros_like(db_ref)

        z = alpha * x_ref[...] + s * y_ref[...]
        zc = z - jnp.mean(z, axis=-1, keepdims=True)
        r = lax.rsqrt(jnp.mean(zc * zc, axis=-1, keepdims=True) + LN_EPS)
        xh = zc * r
        d = d_ref[...]
        dxh = d * g_ref[...]
        dz = r * (dxh - jnp.mean(dxh, axis=-1, keepdims=True) - xh * jnp.mean(dxh * xh, axis=-1, keepdims=True))
        dx_ref[...] = alpha * dz
        dy_ref[...] = (s * dz).astype(dy_ref.dtype)
        dg_ref[...] += jnp.sum(d * xh, axis=0, keepdims=True)
        db_ref[...] += jnp.sum(d, axis=0, keepdims=True)

    row = pl.BlockSpec((tm, D), lambda i: (i, 0))
    vec = pl.BlockSpec((1, D), lambda i: (0, 0))
    return pl.pallas_call(
        body, name=name, grid=(T // tm,),
        in_specs=[row, row, row, vec], out_specs=[row, row, vec, vec],
        out_shape=[jax.ShapeDtypeStruct((T, D), F32), jax.ShapeDtypeStruct((T, D), BF16),
                   jax.ShapeDtypeStruct((1, D), F32), jax.ShapeDtypeStruct((1, D), F32)],
        compiler_params=_params(("arbitrary",), 12 * tm * D * 4),
    )(_hbm(xres), _hbm(y), _hbm(dout), g)


FFN_TILE = 256


def _interleave(w, axis):
    n = w.shape[axis] // (2 * FFN_TILE)
    shp = w.shape[:axis] + (2, n, FFN_TILE) + w.shape[axis + 1:]
    return jnp.swapaxes(w.reshape(shp), axis, axis + 1).reshape(w.shape)


def _deinterleave(w, axis):
    n = w.shape[axis] // (2 * FFN_TILE)
    shp = w.shape[:axis] + (n, 2, FFN_TILE) + w.shape[axis + 1:]
    return jnp.swapaxes(w.reshape(shp), axis, axis + 1).reshape(w.shape)


def _ffn_up(xb, w13t, lead, *, name, ride=None):
    T, D = xb.shape
    F = w13t.shape[1] // 2
    tc = FFN_TILE
    tm = _tile(T, 1024, 16)

    def body(x_ref, w_ref, h_ref, a_ref):
        h = _dg(x_ref[...], w_ref[...], 1, 1)
        g, u = h[:, :tc], h[:, tc:]
        h_ref[...] = h.astype(h_ref.dtype)
        a_ref[...] = (g * jax.nn.sigmoid(g) * u).astype(a_ref.dtype)

    est = (tm * D + 2 * tc * D + 3 * tm * tc) * 2 + 3 * tm * tc * 4
    (h13, a), gathered = _host_call(
        body, name=name, grid=(T // tm, F // tc),
        in_specs=[pl.BlockSpec((tm, D), lambda i, j: (i, 0)),
                  pl.BlockSpec((None, 2 * tc, D), lambda i, j: (lead, j, 0))],
        out_specs=[pl.BlockSpec((tm, 2 * tc), lambda i, j: (i, j)),
                   pl.BlockSpec((tm, tc), lambda i, j: (i, j))],
        out_shape=[jax.ShapeDtypeStruct((T, 2 * F), BF16), jax.ShapeDtypeStruct((T, F), BF16)],
        args=[_hbm(xb), _hbm(w13t)], sem=("parallel", "parallel"), est=est, ride=ride)
    return h13, a, gathered


def _ffn_down_bwd(dyb, w2, lead, h13, *, name, ride=None):
    T, D = dyb.shape
    F = w2.shape[1]
    tc = FFN_TILE
    tm = _tile(T, 1024, 16)

    def body(dy_ref, w_ref, h_ref, dh_ref):
        d = _dg(dy_ref[...], w_ref[...], 1, 1)
        h = h_ref[...].astype(F32)
        g, u = h[:, :tc], h[:, tc:]
        sig = jax.nn.sigmoid(g)
        gs = g * sig
        dh_ref[...] = jnp.concatenate([d * u * (sig + gs * (1.0 - sig)), d * gs], axis=1).astype(dh_ref.dtype)

    est = (tm * D + tc * D + 4 * tm * tc) * 2 + 6 * tm * tc * 4
    (dh,), rode = _host_call(
        body, name=name, grid=(T // tm, F // tc),
        in_specs=[pl.BlockSpec((tm, D), lambda i, j: (i, 0)),
                  pl.BlockSpec((None, tc, D), lambda i, j: (lead, j, 0)),
                  pl.BlockSpec((tm, 2 * tc), lambda i, j: (i, j))],
        out_specs=[pl.BlockSpec((tm, 2 * tc), lambda i, j: (i, j))],
        out_shape=[jax.ShapeDtypeStruct((T, 2 * F), BF16)],
        args=[_hbm(dyb), _hbm(w2), _hbm(h13)], sem=("parallel", "parallel"), est=est, ride=ride)
    return dh, rode


def _pool_select(parts, pw):
    pg = pw // len(POOL_WINDOWS)
    grp = lax.broadcasted_iota(jnp.int32, parts[0].shape, 1) // pg
    out = parts[3]
    for g in (2, 1, 0):
        out = jnp.where(grp == g, parts[g], out)
    return out


def _pool_count(t0, rows, pw):
    pg = pw // len(POOL_WINDOWS)
    grp = lax.broadcasted_iota(jnp.int32, (rows, pw), 1) // pg
    win = jnp.where(grp == 0, POOL_WINDOWS[0],
                    jnp.where(grp == 1, POOL_WINDOWS[1],
                              jnp.where(grp == 2, POOL_WINDOWS[2], POOL_WINDOWS[3])))
    t = t0 + lax.broadcasted_iota(jnp.int32, (rows, pw), 0)
    return jnp.minimum(t + 1, win).astype(F32)


def _window_sums(ext, up):
    n = ext.shape[0]
    sums, cur, k = [], ext, 1
    for _ in POOL_WINDOWS:
        cur = cur + pltpu.roll(cur, (n - k) if up else k, axis=0)
        sums.append(cur)
        k *= 2
    return sums


def _pool_delta(u, halo, t0):
    tm, pw = u.shape
    ext = jnp.concatenate([halo, u], axis=0)
    sums = [s[POOL_HALO:, :] for s in _window_sums(ext, up=False)]
    return _pool_select(sums, pw) / _pool_count(t0, tm, pw) - u


def _pool_fwd(hin, wbd, scale, *, name):
    T = hin.shape[0]
    pw = wbd.shape[0]
    tm = _tile(T, 256, POOL_HALO)
    per = tm // POOL_HALO

    def body(u_ref, halo_ref, w_ref, s_ref, y_ref):
        i = pl.program_id(0)
        halo = jnp.where(i > 0, halo_ref[...], 0.0)
        d = _pool_delta(u_ref[...], halo, i * tm)
        y_ref[...] = (_dg(d, w_ref[...], 1, 0) * s_ref[...]).astype(y_ref.dtype)

    return pl.pallas_call(
        body, name=name, grid=(T // tm,),
        in_specs=[pl.BlockSpec((tm, pw), lambda i: (i, 0)),
                  pl.BlockSpec((POOL_HALO, pw), lambda i: (jnp.maximum(i * per - 1, 0), 0)),
                  pl.BlockSpec((pw, pw), lambda i: (0, 0)),
                  pl.BlockSpec((1, pw), lambda i: (0, 0))],
        out_specs=pl.BlockSpec((tm, pw), lambda i: (i, 0)),
        out_shape=jax.ShapeDtypeStruct((T, pw), BF16),
        compiler_params=_params(("parallel",), 16 * tm * pw * 4),
    )(_hbm(hin), _hbm(hin), wbd, scale)


def _pool_bwd(hin, dcat, wbd, scale, *, name):
    T = hin.shape[0]
    pw = wbd.shape[0]
    tm = _tile(T, 256, POOL_HALO)
    per = tm // POOL_HALO
    nt = T // tm

    def body(u_ref, halo_ref, dy_ref, dyn_ref, w_ref, s_ref, du_ref, dw_ref, ds_ref):
        i = pl.program_id(0)

        @pl.when(i == 0)
        def _():
            dw_ref[...] = jnp.zeros_like(dw_ref)
            ds_ref[...] = jnp.zeros_like(ds_ref)

        halo = jnp.where(i > 0, halo_ref[...], 0.0)
        d = _pool_delta(u_ref[...], halo, i * tm)
        w = w_ref[...]
        sc = s_ref[...]
        dy = dy_ref[...]
        dyn = jnp.where(i < nt - 1, dyn_ref[...], 0.0)
        ds_ref[...] += jnp.sum(dy * _dg(d, w, 1, 0), axis=0, keepdims=True)
        dys = dy * sc
        dw_ref[...] += _dg(d, dys, 0, 0)
        dys_ext = jnp.concatenate([dys, dyn * sc], axis=0)
        dd_ext = _dg(dys_ext, w, 1, 1)
        ddp = dd_ext / _pool_count(i * tm, tm + POOL_HALO, pw)
        sums = [s[:tm, :] for s in _window_sums(ddp, up=True)]
        du_ref[...] = _pool_select(sums, pw) - dd_ext[:tm, :]

    return pl.pallas_call(
        body, name=name, grid=(nt,),
        in_specs=[pl.BlockSpec((tm, pw), lambda i: (i, 0)),
                  pl.BlockSpec((POOL_HALO, pw), lambda i: (jnp.maximum(i * per - 1, 0), 0)),
                  pl.BlockSpec((tm, pw), lambda i: (i, 0)),
                  pl.BlockSpec((POOL_HALO, pw), lambda i: (jnp.minimum((i + 1) * per, nt * per - 1), 0)),
                  pl.BlockSpec((pw, pw), lambda i: (0, 0)),
                  pl.BlockSpec((1, pw), lambda i: (0, 0))],
        out_specs=[pl.BlockSpec((tm, pw), lambda i: (i, 0)),
                   pl.BlockSpec((pw, pw), lambda i: (0, 0)),
                   pl.BlockSpec((1, pw), lambda i: (0, 0))],
        out_shape=[jax.ShapeDtypeStruct((T, pw), F32),
                   jax.ShapeDtypeStruct((pw, pw), F32),
                   jax.ShapeDtypeStruct((1, pw), F32)],
        compiler_params=_params(("arbitrary",), 24 * tm * pw * 4),
    )(_hbm(hin), _hbm(hin), _hbm(dcat), _hbm(dcat), wbd, scale)


def _rms(x, g):
    return x * lax.rsqrt(jnp.mean(x * x, axis=-1, keepdims=True) + RMS_EPS) * g


def _norms_fn(pw, h, gq, gkv):
    o1 = pw + Q_LORA
    o2 = o1 + KV_LORA
    return (_rms(_cols(h, pw, o1), gq), _rms(_cols(h, o1, o2), gkv), _cols(h, o2, h.shape[1]))


def _norms_fwd(hin, gq, gkv, *, pw, name):
    tm = _tile(hin.shape[0], 256, 16)

    def fn(i, tv, pv):
        return _norms_fn(pw, tv[0], pv[0], pv[1]), ()

    return _rowwise(fn, [hin], [gq, gkv], [(Q_LORA, BF16), (KV_LORA, BF16), (LANE, F32)], tm=tm, name=name)


def _norms_bwd(hin, gq, gkv, dcq, dckv, dkpe, du, *, pw, name):
    tm = _tile(hin.shape[0], 256, 16)
    dinp = hin.shape[1]

    def fn(i, tv, pv):
        _, vjp = jax.vjp(functools.partial(_norms_fn, pw), tv[0], pv[0], pv[1])
        dh, dgq, dgkv = vjp((tv[1].astype(F32), tv[2].astype(F32), tv[3].astype(F32)))
        dh = jnp.concatenate([tv[4], dh[:, pw:]], axis=1)
        return (dh,), (dgq, dgkv)

    return _rowwise(fn, [hin, dcq, dckv, dkpe, du], [gq, gkv], [(dinp, BF16)],
                    [((1, Q_LORA), F32), ((1, KV_LORA), F32)], tm=tm, name=name)


def _heads_fn(H, qraw, kv, kpe, rc, rs1, rs2):
    half = QK_ROPE // 2
    scale = (QK_NOPE + QK_ROPE) ** -0.5

    def rope(blk):
        return blk * rc + _lane_roll(blk, -half) * rs1 + _lane_roll(blk, half) * rs2

    krot = rope(kpe)
    qs, ks, vs = [], [], []
    for h in range(H):
        lo = h * HEAD_PAD
        qs += [_cols(qraw, lo, lo + LANE) * scale, rope(_cols(qraw, lo + LANE, lo + HEAD_PAD)) * scale]
        ks += [_cols(kv, lo, lo + LANE), krot]
        vs += [_cols(kv, lo + LANE, lo + HEAD_PAD)]
    return jnp.concatenate(qs, axis=1), jnp.concatenate(ks, axis=1), jnp.concatenate(vs, axis=1)


def _heads_fwd(qraw, kv, kpe, tabs, *, H, name):
    tm = _tile(qraw.shape[0], 256, 16)

    def fn(i, tv, pv):
        return _heads_fn(H, *tv), ()

    return _rowwise(fn, [qraw, kv, kpe, *tabs], [],
                    [(H * HEAD_PAD, BF16), (H * HEAD_PAD, BF16), (H * V_HEAD, BF16)], tm=tm, name=name)


def _heads_bwd(dq, dk, dv, tabs, *, H, name):
    tm = _tile(dq.shape[0], 256, 16)

    def fn(i, tv, pv):
        z = jnp.zeros((tm, H * HEAD_PAD), F32)
        zk = jnp.zeros((tm, LANE), F32)
        rc, rs1, rs2 = tv[3], tv[4], tv[5]
        _, vjp = jax.vjp(lambda a, b, c: _heads_fn(H, a, b, c, rc, rs1, rs2), z, z, zk)
        return vjp((tv[0], tv[1], tv[2])), ()

    return _rowwise(fn, [dq, dk, dv, *tabs], [],
                    [(H * HEAD_PAD, BF16), (H * HEAD_PAD, BF16), (LANE, F32)], tm=tm, name=name)


def _diag_mask(rows, cols, row0):
    r = (row0 + lax.broadcasted_iota(jnp.int32, (rows, cols), 0)) // CHUNK
    c = lax.broadcasted_iota(jnp.int32, (rows, cols), 1) // CHUNK
    return r >= c


def _flash_fwd(qh, kh, vh, *, H, name, ride=None):
    T = qh.shape[0]
    t = _tile(T, 512, CHUNK)


    def body(q_ref, k_ref, v_ref, o_ref, lse_ref):
        i = pl.program_id(1)
        q = q_ref[...]

        def blk(j, carry, masked):
            m, l, acc = carry
            rows = pl.ds(pl.multiple_of(j * t, t), t)
            s = _dg(q, k_ref[rows, :], 1, 1)
            if masked:
                s = jnp.where(_diag_mask(t, t, 0), s, NEG_INF)
            mn = jnp.maximum(m, jnp.max(s, axis=1, keepdims=True))
            p = jnp.exp(s - mn)
            corr = jnp.exp(m - mn)
            l = corr * l + jnp.sum(p, axis=1, keepdims=True)
            acc = corr * acc + _dg(p, v_ref[rows, :], 1, 0)
            return mn, l, acc

        init = (jnp.full((t, 1), NEG_INF, F32), jnp.zeros((t, 1), F32), jnp.zeros((t, V_HEAD), F32))
        carry = lax.fori_loop(0, i, lambda j, c: blk(j, c, False), init)
        m, l, acc = blk(i, carry, True)
        o_ref[...] = (acc / l).astype(o_ref.dtype)
        lse_ref[...] = jnp.broadcast_to(m + jnp.log(l), (t, V_HEAD))

    est = 2 * T * (HEAD_PAD + V_HEAD) * 2 + 8 * t * t * 4
    (o, lse), gathered = _host_call(
        body, name=name, grid=(H, T // t),
        in_specs=[pl.BlockSpec((t, HEAD_PAD), lambda h, i: (i, h)),
                  pl.BlockSpec((T, HEAD_PAD), lambda h, i: (0, h)),
                  pl.BlockSpec((T, V_HEAD), lambda h, i: (0, h))],
        out_specs=[pl.BlockSpec((t, V_HEAD), lambda h, i: (i, h)),
                   pl.BlockSpec((t, V_HEAD), lambda h, i: (i, h))],
        out_shape=[jax.ShapeDtypeStruct((T, H * V_HEAD), BF16),
                   jax.ShapeDtypeStruct((T, H * V_HEAD), F32)],
        args=[_hbm(qh), _hbm(kh), _hbm(vh)], sem=("parallel", "parallel"), est=est, ride=ride)
    return o, lse, gathered


def _flash_bwd(qh, kh, vh, cat, dcat, lse, *, H, pw, name, ride=None):
    T = qh.shape[0]
    t = _tile(T, 512, CHUNK)
    nb = T // t
    off = pw // V_HEAD

    def body(q_ref, k_ref, v_ref, o_ref, do_ref, lse_ref, dq_ref, dk_ref, dv_ref):
        j = pl.program_id(1)

        @pl.when(j == 0)
        def _():
            dq_ref[...] = jnp.zeros_like(dq_ref)

        kj = k_ref[...]
        vj = v_ref[...]

        def blk(i, carry, masked):
            dk, dv = carry
            rows = pl.ds(pl.multiple_of(i * t, t), t)
            qi = q_ref[rows, :]
            doi = do_ref[rows, :]
            oi = o_ref[rows, :].astype(F32)
            lsei = lse_ref[rows, :][:, :1]
            s = _dg(qi, kj, 1, 1)
            if masked:
                s = jnp.where(_diag_mask(t, t, 0), s, NEG_INF)
            p = jnp.exp(s - lsei)
            dv = dv + _dg(p, doi, 0, 0)
            dp = _dg(doi, vj, 1, 1)
            di = jnp.sum(doi * oi, axis=1, keepdims=True)
            ds = p * (dp - di)
            dk = dk + _dg(ds, qi, 0, 0)
            dq_ref[rows, :] += _dg(ds, kj, 1, 0)
            return dk, dv

        carry = blk(j, (jnp.zeros((t, HEAD_PAD), F32), jnp.zeros((t, V_HEAD), F32)), True)
        dk, dv = lax.fori_loop(j + 1, nb, lambda i, c: blk(i, c, False), carry)
        dk_ref[...] = dk
        dv_ref[...] = dv

    est = T * (HEAD_PAD * 2 + V_HEAD * 2 + V_HEAD * 4 + V_HEAD * 4 + HEAD_PAD * 4) + 10 * t * t * 4
    (dq, dk, dv), gathered = _host_call(
        body, name=name, grid=(H, nb),
        in_specs=[pl.BlockSpec((T, HEAD_PAD), lambda h, j: (0, h)),
                  pl.BlockSpec((t, HEAD_PAD), lambda h, j: (j, h)),
                  pl.BlockSpec((t, V_HEAD), lambda h, j: (j, h)),
                  pl.BlockSpec((T, V_HEAD), lambda h, j: (0, off + h)),
                  pl.BlockSpec((T, V_HEAD), lambda h, j: (0, off + h)),
                  pl.BlockSpec((T, V_HEAD), lambda h, j: (0, h))],
        out_specs=[pl.BlockSpec((T, HEAD_PAD), lambda h, j: (0, h)),
                   pl.BlockSpec((t, HEAD_PAD), lambda h, j: (j, h)),
                   pl.BlockSpec((t, V_HEAD), lambda h, j: (j, h))],
        out_shape=[jax.ShapeDtypeStruct((T, H * HEAD_PAD), F32),
                   jax.ShapeDtypeStruct((T, H * HEAD_PAD), F32),
                   jax.ShapeDtypeStruct((T, H * V_HEAD), F32)],
        args=[_hbm(v) for v in (qh, kh, vh, cat, dcat, lse)], sem=("arbitrary", "arbitrary"), est=est,
        ride=ride)
    return dq, dk, dv, gathered


def _mem_fn(q, k, v):
    hd = q.shape[1] // MEM_HEADS
    outs = []
    for h in range(MEM_HEADS):
        lo, hi = h * hd, (h + 1) * hd
        s = _bdot_nt(_cols(q, lo, hi), _cols(k, lo, hi)) * hd ** -0.5
        e = jnp.exp(s - lax.stop_gradient(jnp.max(s, axis=1, keepdims=True)))
        p = e / jnp.sum(e, axis=1, keepdims=True)
        outs.append(_bdot_nn(p, _cols(v, lo, hi)))
    return jnp.concatenate(outs, axis=1)


def _mem_fwd(q, k, v, *, name):
    T, D = q.shape
    tm = _tile(T, 256, 16)

    def fn(i, tv, pv):
        return (_mem_fn(tv[0], pv[0], pv[1]),), ()

    return _rowwise(fn, [q], [k, v], [(D, BF16)], tm=tm, name=name)[0]


def _mem_bwd(q, k, v, do, *, name):
    T, D = q.shape
    tm = _tile(T, 256, 16)

    def fn(i, tv, pv):
        _, vjp = jax.vjp(_mem_fn, tv[0], pv[0], pv[1])
        dq, dk, dv = vjp(tv[1].astype(F32))
        return (dq,), (dk, dv)

    return _rowwise(fn, [q, do], [k, v], [(D, BF16)], [(k.shape, F32), (v.shape, F32)], tm=tm, name=name)


def _loss_head(y, target, *, name):
    T, D = y.shape
    tm = _tile(T, 256, 16)

    def fn(i, tv, pv):
        err = tv[0] - tv[1]
        part = 0.5 * jnp.sum(jnp.sum(err * err, axis=1, keepdims=True) / D, axis=0, keepdims=True)
        return (err / D,), (jnp.broadcast_to(part, (8, LANE)),)

    return _rowwise(fn, [y, target], [], [(D, F32)], [((8, LANE), F32)], tm=tm, name=name)


def _adamw(w, g, m, v, *, name):
    shape = w.shape
    if w.ndim != 3:
        lead3 = (1, math.prod(shape[:-1]), shape[-1])
        return [o.reshape(shape) for o in _adamw(*[a.reshape(lead3) for a in (w, g, m, v)], name=name)]
    Lw, R, C = shape
    tr = _tile(R, 512, 8)
    b1c = 1.0 - ADAM_B1 ** ADAM_STEP
    b2c = 1.0 - ADAM_B2 ** ADAM_STEP

    def body(w_ref, g_ref, m_ref, v_ref, d_ref, mo_ref, vo_ref):
        gg = g_ref[...]
        mn = ADAM_B1 * m_ref[...] + (1.0 - ADAM_B1) * gg
        vn = ADAM_B2 * v_ref[...] + (1.0 - ADAM_B2) * (gg * gg)
        d_ref[...] = -ADAM_LR * ((mn / b1c) / (jnp.sqrt(vn / b2c) + ADAM_EPS) + ADAM_WD * w_ref[...])
        mo_ref[...] = mn
        vo_ref[...] = vn

    spec = pl.BlockSpec((None, tr, C), lambda l, i: (l, i, 0))
    return pl.pallas_call(
        body, name=name, grid=(Lw, R // tr),
        in_specs=[spec] * 4, out_specs=[spec] * 3,
        out_shape=[jax.ShapeDtypeStruct(shape, F32)] * 3,
        compiler_params=_params(("parallel", "parallel"), 7 * tr * C * 4),
    )(*[_hbm(a) for a in (w, g, m, v)])


def _pair_sum(core, g, landed, off, tr, *, name):
    rows = g.shape[0] // N_DEV
    C = g.shape[1]
    per = rows // tr

    def body(core_ref, g_ref, l_ref, o_ref):
        o_ref[...] = (g_ref[...].astype(F32) + l_ref[...].astype(F32)).astype(o_ref.dtype)

    slab = pl.BlockSpec((None, tr, C), lambda p, i, core_ref: (p, off // tr + i, 0))
    return pl.pallas_call(
        body, name=name,
        grid_spec=pltpu.PrefetchScalarGridSpec(
            num_scalar_prefetch=1, grid=(4, per),
            in_specs=[pl.BlockSpec((tr, C), lambda p, i, core_ref: ((2 * p + core_ref[0]) * per + i, 0)), slab],
            out_specs=slab),
        out_shape=jax.ShapeDtypeStruct(landed.shape, landed.dtype),
        input_output_aliases={2: 0},
        compiler_params=_params(("arbitrary", "arbitrary"), 4 * tr * C * 4),
    )(core, _hbm(g), _hbm(landed))


def _quad_sum(chip, part, gathered, used, *, name):
    C = part.shape[2]
    R = used
    tr = _tile(math.gcd(used, part.shape[1]), 256, 16)

    def body(chip_ref, own_ref, a_ref, b_ref, c_ref, o_ref):
        o_ref[...] = ((own_ref[...].astype(F32) + a_ref[...].astype(F32)) + b_ref[...].astype(F32)) \
            + c_ref[...].astype(F32)

    def other(k):
        return pl.BlockSpec((None, tr, C), lambda i, chip_ref: (chip_ref[0] ^ k, i, 0))

    return pl.pallas_call(
        body, name=name,
        grid_spec=pltpu.PrefetchScalarGridSpec(
            num_scalar_prefetch=1, grid=(R // tr,),
            in_specs=[pl.BlockSpec((None, tr, C), lambda i, chip_ref: (chip_ref[0], i, 0)),
                      other(1), other(2), other(3)],
            out_specs=pl.BlockSpec((tr, C), lambda i, chip_ref: (i, 0))),
        out_shape=jax.ShapeDtypeStruct((R, C), F32),
        compiler_params=_params(("arbitrary",), 8 * tr * C * 4),
    )(chip, _hbm(part), _hbm(gathered), _hbm(gathered), _hbm(gathered))


def _place():
    x, y, c = lax.axis_index("x"), lax.axis_index("y"), lax.axis_index("c")
    return x, y, c


ANY = pl.BlockSpec(memory_space=pl.ANY)


class _Gather:
    def __init__(self, shards):
        self.shards = list(shards)
        self.n = len(self.shards)
        self.out_shape = [jax.ShapeDtypeStruct((s.shape[0], N_DEV * s.shape[1], s.shape[2]), s.dtype)
                          for s in self.shards]
        self.scratch = [pltpu.SemaphoreType.DMA((7 * self.n,)), pltpu.SemaphoreType.DMA((7 * self.n,)),
                        pltpu.SemaphoreType.DMA((self.n,))]
        self.operands = [_hbm(s) for s in self.shards]

    def _bind(self, refs):
        n = self.n
        ins, outs = refs[:n], refs[n:2 * n]
        send_sems, recv_sems, local_sems = refs[2 * n:]
        x, y, c = _place()
        me, sib = (x, y, c), (x, y, 1 - c)
        chips = [(1 - x, y), (x, 1 - y), (1 - x, 1 - y)]

        def rows(w, p):
            r = self.shards[w].shape[1]
            idx = 4 * p[0] + 2 * p[1] + p[2]
            return outs[w].at[:, pl.ds(pl.multiple_of(idx * r, 8), r), :]

        def copy(w, k, block, to, src=None):
            return pltpu.make_async_remote_copy(
                src_ref=rows(w, block) if src is None else src, dst_ref=rows(w, block),
                send_sem=send_sems.at[w * 7 + k], recv_sem=recv_sems.at[w * 7 + k],
                device_id=to, device_id_type=MESH)

        def mine():
            return [pltpu.make_async_copy(ins[w], rows(w, me), local_sems.at[w]) for w in range(n)]

        def first():
            out = []
            for w in range(n):
                out.append(copy(w, 0, me, sib, src=ins[w]))
                out += [copy(w, 1 + j, me, (*chip, c), src=ins[w]) for j, chip in enumerate(chips)]
            return out

        def passed():
            return [copy(w, 4 + j, (*chip, c), sib) for j, chip in enumerate(chips) for w in range(n)]

        def landed():
            return [copy(w, 1 + j, (*chip, c), me) for j, chip in enumerate(chips) for w in range(n)]

        def last():
            out = []
            for w in range(n):
                out.append(copy(w, 0, sib, me))
                out += [copy(w, 4 + j, (*chip, 1 - c), me) for j, chip in enumerate(chips)]
            return out

        return mine, first, landed, passed, last

    def start(self, refs):
        mine, first, _, _, _ = self._bind(refs)
        for cp in mine() + first():
            cp.start()

    def forward(self, refs):
        _, _, landed, passed, _ = self._bind(refs)
        for arrived, fwd in zip(landed(), passed()):
            arrived.wait_recv()
            fwd.start()

    def finish(self, refs):
        mine, first, _, passed, last = self._bind(refs)
        for cp in last():
            cp.wait_recv()
        for cp in first() + passed():
            cp.wait_send()
        for cp in mine():
            cp.wait()


class _ChipExchange:
    def __init__(self, parts, used):
        self.ncl = len(parts)
        self.used = list(used)
        self.out_shape = [jax.ShapeDtypeStruct(p.shape, p.dtype) for p in parts]
        self.scratch = [pltpu.SemaphoreType.DMA((3 * self.ncl,)), pltpu.SemaphoreType.DMA((3 * self.ncl,))]
        self.operands = [_hbm(p) for p in parts]
        self.n = self.ncl

    def _bind(self, refs):
        ncl = self.ncl
        ins, outs = refs[:ncl], refs[ncl:2 * ncl]
        send_sems, recv_sems = refs[2 * ncl:]
        x, y, c = _place()
        chips = [(1 - x, y), (x, 1 - y), (1 - x, 1 - y)]
        here = 2 * x + y

        def copies(outgoing):
            out = []
            for k in range(ncl):
                rows = pl.ds(0, self.used[k])
                for j, (cx, cy) in enumerate(chips):
                    there = 2 * cx + cy
                    src, dst = (there, here) if outgoing else (here, there)
                    out.append(pltpu.make_async_remote_copy(
                        src_ref=ins[k].at[src, rows, :], dst_ref=outs[k].at[dst, rows, :],
                        send_sem=send_sems.at[3 * k + j], recv_sem=recv_sems.at[3 * k + j],
                        device_id=(cx, cy, c), device_id_type=MESH))
            return out

        return copies

    def start(self, refs):
        for cp in self._bind(refs)(True):
            cp.start()

    def forward(self, refs):
        pass

    def finish(self, refs):
        copies = self._bind(refs)
        for cp in copies(False):
            cp.wait_recv()
        for cp in copies(True):
            cp.wait_send()


class _Both:
    def __init__(self, members):
        self.members = list(members)
        self.n = sum(m.n for m in self.members)
        self.out_shape = [s for m in self.members for s in m.out_shape]
        self.scratch = [s for m in self.members for s in m.scratch]
        self.operands = [o for m in self.members for o in m.operands]

    def split(self, arrays):
        out, a = [], 0
        for m in self.members:
            out.append(list(arrays[a:a + m.n]))
            a += m.n
        return out

    def _refs(self, refs):
        ins, outs = self.split(refs[:self.n]), self.split(refs[self.n:2 * self.n])
        scr, b = [], 2 * self.n
        for m in self.members:
            scr.append(list(refs[b:b + len(m.scratch)]))
            b += len(m.scratch)
        return [(*i, *o, *s) for i, o, s in zip(ins, outs, scr)]

    def start(self, refs):
        for m, r in zip(self.members, self._refs(refs)):
            m.start(r)

    def forward(self, refs):
        for m, r in zip(self.members, self._refs(refs)):
            m.forward(r)

    def finish(self, refs):
        for m, r in zip(self.members, self._refs(refs)):
            m.finish(r)


def _exchange_alone(ex, *, name):
    def body(*refs):
        ex.start(refs)
        ex.forward(refs)
        ex.finish(refs)

    return pl.pallas_call(
        body, name=name, in_specs=[ANY] * ex.n, out_specs=[ANY] * ex.n,
        out_shape=ex.out_shape, scratch_shapes=ex.scratch,
    )(*ex.operands)


def _host_call(body, *, name, grid, in_specs, out_specs, out_shape, args, sem, est, ride=None):
    if ride is None:
        outs = pl.pallas_call(body, name=name, grid=grid, in_specs=in_specs, out_specs=out_specs,
                              out_shape=out_shape, compiler_params=_params(sem, est))(*args)
        return list(outs), []
    n_in, n_out, n = len(in_specs), len(out_specs), ride.n

    def full(*refs):
        ins, rin = refs[:n_in], refs[n_in:n_in + n]
        outs, rout = refs[n_in + n:n_in + n + n_out], refs[n_in + n + n_out:n_in + 2 * n + n_out]
        rrefs = (*rin, *rout, *refs[n_in + 2 * n + n_out:])
        step, total = _ride(ride, rrefs, grid)
        body(*ins, *outs)
        _ride_end(ride, rrefs, step, total)

    outs = pl.pallas_call(
        full, name=name, grid=grid,
        in_specs=list(in_specs) + [ANY] * n, out_specs=list(out_specs) + [ANY] * n,
        out_shape=list(out_shape) + ride.out_shape, scratch_shapes=ride.scratch,
        compiler_params=_params(("arbitrary",) * len(grid), est),
    )(*args, *ride.operands)
    return list(outs[:n_out]), list(outs[n_out:])


def _ride(ex, refs, grid):
    total = math.prod(grid)
    step = pl.program_id(0)
    for axis in range(1, len(grid)):
        step = step * grid[axis] + pl.program_id(axis)
    pl.when(step == 0)(lambda: ex.start(refs))
    return step, total


def _ride_end(ex, refs, step, total):
    pl.when(step == (3 * total) // 4)(lambda: ex.forward(refs))
    pl.when(step == total - 1)(lambda: ex.finish(refs))


def _class_layout(grads, classes):
    used = [0] * len(set(classes))
    offs, tiles = [], []
    for g, cl in zip(grads, classes):
        rows = g.shape[0] // N_DEV
        offs.append(used[cl])
        tiles.append(math.gcd(rows, used[cl]) if used[cl] else rows)
        used[cl] += rows
    heights = []
    for k, u in enumerate(used):
        step = math.lcm(*[t for t, cl in zip(tiles, classes) if cl == k])
        heights.append(-(-u // step) * step)
    return offs, tiles, used, heights


def _rs_to_sibling(grads, classes, *, name):
    n = len(grads)
    offs, _, used, heights = _class_layout(grads, classes)
    ncl = len(heights)
    cols = [next(g.shape[1] for g, cl in zip(grads, classes) if cl == k) for k in range(ncl)]

    def body(*refs):
        gs, land = refs[:n], refs[n:n + ncl]
        send_sems, recv_sems = refs[n + ncl:]
        x, y, c = _place()
        sib = (x, y, 1 - c)
        for p in range(4):
            for w in range(n):
                r = grads[w].shape[0] // N_DEV
                cl = classes[w]
                there = gs[w].at[pl.ds(pl.multiple_of((2 * p + 1 - c) * r, 8), r), :]
                pltpu.make_async_remote_copy(
                    src_ref=there, dst_ref=land[cl].at[p, pl.ds(offs[w], r), :],
                    send_sem=send_sems.at[cl * 4 + p], recv_sem=recv_sems.at[cl * 4 + p],
                    device_id=sib, device_id_type=MESH).start()
        for cl in range(ncl):
            for p in range(4):
                rows_used = land[cl].at[p, pl.ds(0, used[cl]), :]
                slab = pltpu.make_async_remote_copy(
                    src_ref=rows_used, dst_ref=rows_used,
                    send_sem=send_sems.at[cl * 4 + p], recv_sem=recv_sems.at[cl * 4 + p],
                    device_id=sib, device_id_type=MESH)
                slab.wait_send()
                slab.wait_recv()

    return pl.pallas_call(
        body, name=name,
        in_specs=[ANY] * n, out_specs=[ANY] * ncl,
        out_shape=[jax.ShapeDtypeStruct((4, heights[k], cols[k]), BF16) for k in range(ncl)],
        scratch_shapes=[pltpu.SemaphoreType.DMA((4 * ncl,))] * 2,
    )(*[_hbm(g) for g in grads])


def _all_reduce_small(v, *, name):
    R = v.shape[0]

    def body(v_ref, o_ref, buf, send_sems, recv_sems):
        x, y, c = _place()
        me = 4 * x + 2 * y + c
        buf[me] = v_ref[...]
        copies = []
        for k in range(1, N_DEV):
            fx, fy, fc = (k >> 2) & 1, (k >> 1) & 1, k & 1
            to = (x ^ fx, y ^ fy, c ^ fc)
            cp = pltpu.make_async_remote_copy(
                src_ref=v_ref, dst_ref=buf.at[me],
                send_sem=send_sems.at[k - 1], recv_sem=recv_sems.at[k - 1],
                device_id=to, device_id_type=MESH)
            cp.start()
            copies.append(cp)
        for k in range(1, N_DEV):
            fx, fy, fc = (k >> 2) & 1, (k >> 1) & 1, k & 1
            frm = 4 * (x ^ fx) + 2 * (y ^ fy) + (c ^ fc)
            pltpu.make_async_remote_copy(
                src_ref=v_ref, dst_ref=buf.at[frm],
                send_sem=send_sems.at[k - 1], recv_sem=recv_sems.at[k - 1],
                device_id=(x ^ fx, y ^ fy, c ^ fc), device_id_type=MESH).wait_recv()
        for cp in copies:
            cp.wait_send()
        acc = buf[0]
        for d in range(1, N_DEV):
            acc = acc + buf[d]
        o_ref[...] = acc

    vm = pl.BlockSpec(memory_space=pltpu.VMEM)
    return pl.pallas_call(
        body, name=name, in_specs=[vm], out_specs=vm,
        out_shape=jax.ShapeDtypeStruct((R, LANE), F32),
        scratch_shapes=[pltpu.VMEM((N_DEV, R, LANE), F32),
                        pltpu.SemaphoreType.DMA((N_DEV - 1,)), pltpu.SemaphoreType.DMA((N_DEV - 1,))],
        compiler_params=pltpu.CompilerParams(vmem_limit_bytes=VMEM_FLOOR),
    )(v)


def _rope_tables(positions):
    half = QK_ROPE // 2
    inv_freq = ROPE_BASE ** (-jnp.arange(half, dtype=F32) / half)
    ang = positions.astype(F32)[:, None] * inv_freq
    cos, sin = jnp.cos(ang), jnp.sin(ang)
    z = jnp.zeros_like(cos)
    z2 = jnp.zeros((positions.shape[0], LANE - QK_ROPE), F32)
    rc = jnp.concatenate([cos, cos, z2], axis=1)
    rs1 = jnp.concatenate([-sin, z, z2], axis=1)
    rs2 = jnp.concatenate([z, sin, z2], axis=1)
    return rc, rs1, rs2


def _block_diag(pool_w):
    G, pg, _ = pool_w.shape
    out = jnp.zeros((G * pg, G * pg), pool_w.dtype)
    for g in range(G):
        out = lax.dynamic_update_slice(out, pool_w[g], (g * pg, g * pg))
    return out


def kernel(x, mem, positions, ln_g, ln_b, ffn1_w13, ffn1_w2, w_in, pool_w, pool_scale, q_norm_g, w_uq, kv_norm_g, w_ukv, w_out, mem_wq, mem_wkv, mem_wo, ffn2_w13, ffn2_w2, loss_target, m_ln_g, m_ln_b, m_ffn1_w13, m_ffn1_w2, m_w_in, m_pool_w, m_pool_scale, m_q_norm_g, m_w_uq, m_kv_norm_g, m_w_ukv, m_w_out, m_mem_wq, m_mem_wkv, m_mem_wo, m_ffn2_w13, m_ffn2_w2, v_ln_g, v_ln_b, v_ffn1_w13, v_ffn1_w2, v_w_in, v_pool_w, v_pool_scale, v_q_norm_g, v_w_uq, v_kv_norm_g, v_w_ukv, v_w_out, v_mem_wq, v_mem_wkv, v_mem_wo, v_ffn2_w13, v_ffn2_w2):
    L = ln_g.shape[0]
    T, D = x.shape[1], x.shape[2]
    F = ffn1_w2.shape[1] * N_DEV
    PW = D // 4
    H = (D - PW) // V_HEAD
    DIN = w_in.shape[2]
    DINP = PW + Q_LORA + KV_LORA + LANE
    QW = QK_NOPE + QK_ROPE
    alpha = (2 * L) ** 0.25
    x2d = x.reshape(T, D)
    memb = mem.reshape(mem.shape[1], D).astype(BF16)
    target = loss_target.reshape(T, D)
    tabs = _rope_tables(positions.reshape(T))

    def shards_of(l):
        return dict(
            w13a=ffn1_w13[l].T[None].astype(BF16),
            w13b=ffn2_w13[l].T[None].astype(BF16),
            w2a=ffn1_w2[l][None].astype(BF16),
            w2b=ffn2_w2[l][None].astype(BF16),
            wsq=jnp.stack([w_out[l], mem_wq[l], mem_wo[l]]).astype(BF16),
            wkvT=mem_wkv[l].T[None].astype(BF16),
            winp=jnp.pad(w_in[l], ((0, 0), (0, DINP - DIN)))[None].astype(BF16),
            wuqT=w_uq[l].T[None].astype(BF16),
            wukvT=w_ukv[l].T[None].astype(BF16),
        )

    SMALL = ("winp", "wuqT", "wukvT")
    shards = [shards_of(l) for l in range(L)]
    W = [dict() for _ in range(L)]

    def rider(spec):
        return _Gather([shards[l][n] for l, n in spec]) if spec else None

    def arrived(spec, arrays):
        for (l, n), a in zip(spec, arrays):
            if n in ("w13a", "w13b"):
                a = _interleave(a, 1)
            elif n == "wuqT":
                a = jnp.pad(a.reshape(H, QW, Q_LORA), ((0, 0), (0, HEAD_PAD - QW), (0, 0)))
                a = a.reshape(1, H * HEAD_PAD, Q_LORA)
            W[l][n] = a

    ln_shard = jnp.concatenate([ln_g.reshape(1, 4 * L, -1), ln_b.reshape(1, 4 * L, -1)], axis=1)
    spec0 = [(0, "w13a"), (0, "w2a")]
    got = _exchange_alone(_Gather([shards[l][n] for l, n in spec0] + [ln_shard]), name="ag_first")
    arrived(spec0, got[:-1])
    lnp = jnp.moveaxis(got[-1].reshape(N_DEV, 2, L, 4, D // N_DEV), 0, 3).reshape(2, L, 4, D)
    lng, lnb = lnp[0], lnp[1]
    wbd = [_block_diag(pool_w[l]).astype(BF16) for l in range(L)]

    def ffn_fwd(l, which, xres, xb, k, spec):
        ab = "ab"[which]
        h13, a, rode = _ffn_up(xb, W[l]["w13" + ab], 0, name=f"l{l}_ffn{which}_up", ride=rider(spec))
        arrived(spec, rode)
        y, xo, xob = _mm_ln(a, W[l]["w2" + ab], 0, xres, lng[l, k:k + 1], lnb[l, k:k + 1], alpha=alpha, s=0.5,
                            name=f"l{l}_ffn{which}_y_ln{k}")
        return dict(xres=xres, xb=xb, h13=h13, a=a, y=y), xo, xob

    saved = []
    xres, xb = x2d, x2d.astype(BF16)
    for l in range(L):
        sv = {}
        more = l + 1 < L
        Wl = W[l]
        spec = [(l, n) for n in (SMALL if l == 0 else ())] + [(l, "wsq"), (l, "wkvT")]
        sv["ffn1"], x1, x1b = ffn_fwd(l, 0, xres, xb, 0, spec)
        hin = _mm(x1b, Wl["winp"], lead=0, name=f"l{l}_hin")
        pscale = pool_scale[l].reshape(1, PW)
        gq, gkv = q_norm_g[l].reshape(1, Q_LORA), kv_norm_g[l].reshape(1, KV_LORA)
        ypool = _pool_fwd(hin, wbd[l], pscale, name=f"l{l}_pool")
        cqn, ckvn, kpe = _norms_fwd(hin, gq, gkv, pw=PW, name=f"l{l}_norms")
        qraw = _mm(cqn, Wl["wuqT"], lead=0, tb=True, name=f"l{l}_qraw")
        kv = _mm(ckvn, Wl["wukvT"], lead=0, tb=True, name=f"l{l}_kv")
        qh, kh, vh = _heads_fwd(qraw, kv, kpe, tabs, H=H, name=f"l{l}_heads")
        spec = [(l, "w13b"), (l, "w2b")] + ([(l + 1, "w13a")] if more else [])
        o, lse, rode = _flash_fwd(qh, kh, vh, H=H, name=f"l{l}_flash", ride=rider(spec))
        arrived(spec, rode)
        cat = jnp.concatenate([ypool, o], axis=1)
        ymix, x2, x2b = _mm_ln(cat, Wl["wsq"], 0, x1, lng[l, 1:2], lnb[l, 1:2], alpha=alpha, s=1.0,
                               name=f"l{l}_ymix_ln1")
        qm = _mm(x2b, Wl["wsq"], lead=1, out_dtype=BF16, name=f"l{l}_qm")
        kvm = _mm(memb, Wl["wkvT"], lead=0, tb=True, name=f"l{l}_kvm")
        km, vm = kvm[:, :D], kvm[:, D:]
        om = _mem_fwd(qm, km, vm, name=f"l{l}_memattn")
        ymem, x3, x3b = _mm_ln(om, Wl["wsq"], 2, x2, lng[l, 2:3], lnb[l, 2:3], alpha=alpha, s=1.0,
                               name=f"l{l}_ymem_ln2")
        spec = [(l + 1, n) for n in ("w2a", *SMALL)] if more else []
        sv["ffn2"], x4, x4b = ffn_fwd(l, 1, x3, x3b, 3, spec)
        sv.update(x1=x1, x1b=x1b, hin=hin, pscale=pscale, gq=gq, gkv=gkv, cqn=cqn, ckvn=ckvn,
                  qh=qh, kh=kh, vh=vh, lse=lse, cat=cat, ymix=ymix, x2=x2, x2b=x2b, qm=qm, km=km, vm=vm,
                  om=om, ymem=ymem)
        saved.append(sv)
        xres, xb = x4, x4b

    dx, loss_blk = _loss_head(xres, target, name="loss_head")
    loss = lax.psum(loss_blk[0, 0], ("x", "y", "c"))

    gW = {}
    gS = {}

    def ffn_bwd(l, which, sv, dx, k, ride):
        tag = f"l{l}_ffn{which}"
        dxres, dyb, dg, db = _ln_bwd(sv["xres"], sv["y"], lng[l, k:k + 1], dx,
                                     alpha=alpha, s=0.5, name=f"l{l}_ln{k}_bwd")
        gW[("w2", which, l)] = _mm(sv["a"], dyb, ta=True, out_dtype=BF16, name=f"{tag}_dw2", tn=D)
        dh, rode = _ffn_down_bwd(dyb, W[l]["w2" + "ab"[which]], 0, sv["h13"], name=f"{tag}_dh", ride=ride)
        dxn = _mm(dh, W[l]["w13" + "ab"[which]], lead=0, add=dxres, name=f"{tag}_dx", tn=D)
        dw13 = _mm(dh, sv["xb"], ta=True, out_dtype=BF16, name=f"{tag}_dw13", tn=D)
        gW[("w13", which, l)] = _deinterleave(dw13, 0)
        gS[("ln_g", l, k)], gS[("ln_b", l, k)] = dg, db
        return dxn, rode

    core = lax.axis_index("c").astype(jnp.int32).reshape(1)
    chip = (2 * lax.axis_index("x") + lax.axis_index("y")).astype(jnp.int32).reshape(1)
    gsh = {}

    def rs_first_level(l, group):
        keys, classes = {
            "a": ([("w13", 1, l), ("w2", 1, l), ("mem_wkv", l), ("mem_wq", l), ("mem_wo", l)], [0] * 5),
            "b": ([("w_out", l), ("w_in", l), ("w_uq", l), ("w_ukv", l)], [0, 1, 2, 3]),
            "c": ([("w13", 0, l), ("w2", 0, l)], [0, 0]),
        }[group]
        tag = f"l{l}{group}"
        garrs = []
        for key in keys:
            g = gW[key]
            if key[0] == "w_uq":
                g = g.reshape(H, HEAD_PAD, Q_LORA)[:, :QW, :].reshape(H * QW, Q_LORA)
            garrs.append(g)
        offs, tiles, used, _ = _class_layout(garrs, classes)
        parts = list(_rs_to_sibling(garrs, classes, name=f"{tag}_rs_sibling"))
        for w, (g, cl, off, tr) in enumerate(zip(garrs, classes, offs, tiles)):
            parts[cl] = _pair_sum(core, g, parts[cl], off, tr, name=f"{tag}_rs_pair_sum_{w}")
        return dict(tag=tag, keys=keys, garrs=garrs, classes=classes, offs=offs, used=used, parts=parts,
                    ex=_ChipExchange(parts, used))

    def rs_last_level(st, gathered):
        sums = [_quad_sum(chip, p, a, u, name=f"{st['tag']}_rs_quad_sum{k}")
                for k, (p, a, u) in enumerate(zip(st["parts"], gathered, st["used"]))]
        for key, g, cl, off in zip(st["keys"], st["garrs"], st["classes"], st["offs"]):
            gsh[key] = sums[cl][off:off + g.shape[0] // N_DEV, :]

    above = None
    for l in reversed(range(L)):
        sv = saved[l]
        Wl = W[l]
        dx, _ = ffn_bwd(l, 1, sv["ffn2"], dx, 3, None)
        dxres, dyb, dg, db = _ln_bwd(sv["x2"], sv["ymem"], lng[l, 2:3], dx,
                                     alpha=alpha, s=1.0, name=f"l{l}_ln2_bwd")
        gS[("ln_g", l, 2)], gS[("ln_b", l, 2)] = dg, db
        dom = _mm(dyb, Wl["wsq"], lead=2, tb=True, out_dtype=BF16, name=f"l{l}_dom")
        gW[("mem_wo", l)] = _mm(sv["om"], dyb, ta=True, out_dtype=BF16, name=f"l{l}_dwo", tn=D)
        dqm, dkm, dvm = _mem_bwd(sv["qm"], sv["km"], sv["vm"], dom, name=f"l{l}_memattn_bwd")
        dx = _mm(dqm, Wl["wsq"], lead=1, tb=True, add=dxres, name=f"l{l}_dx2", tn=D)
        gW[("mem_wq", l)] = _mm(sv["x2b"], dqm, ta=True, out_dtype=BF16, name=f"l{l}_dwq", tn=D)
        dkvm = jnp.concatenate([dkm, dvm], axis=1).astype(BF16)
        gW[("mem_wkv", l)] = _mm(dkvm, memb, ta=True, out_dtype=BF16, name=f"l{l}_dwkv", tn=D)
        dxres, dyb, dg, db = _ln_bwd(sv["x1"], sv["ymix"], lng[l, 1:2], dx,
                                     alpha=alpha, s=1.0, name=f"l{l}_ln1_bwd")
        gS[("ln_g", l, 1)], gS[("ln_b", l, 1)] = dg, db
        dcat = _mm(dyb, Wl["wsq"], lead=0, tb=True, name=f"l{l}_dcat", tn=D)
        gW[("w_out", l)] = _mm(sv["cat"], dyb, ta=True, out_dtype=BF16, name=f"l{l}_dwout", tn=D)
        riding = [rs_first_level(l, "a")] + ([above] if above else [])
        both = _Both([st["ex"] for st in riding])
        dqh, dkh, dvh, rode = _flash_bwd(sv["qh"], sv["kh"], sv["vh"], sv["cat"], dcat, sv["lse"], H=H, pw=PW,
                                         name=f"l{l}_flash_bwd", ride=both)
        for st, got in zip(riding, both.split(rode)):
            rs_last_level(st, got)
        dqraw, dkv, dkpe = _heads_bwd(dqh, dkh, dvh, tabs, H=H, name=f"l{l}_heads_bwd")
        dcq = _mm(dqraw, Wl["wuqT"], lead=0, name=f"l{l}_dcq")
        gW[("w_uq", l)] = _mm(dqraw, sv["cqn"], ta=True, out_dtype=BF16, name=f"l{l}_dwuq")
        dckv = _mm(dkv, Wl["wukvT"], lead=0, name=f"l{l}_dckv")
        gW[("w_ukv", l)] = _mm(dkv, sv["ckvn"], ta=True, out_dtype=BF16, name=f"l{l}_dwukv")
        du, dwbd, dps = _pool_bwd(sv["hin"], dcat, wbd[l], sv["pscale"], name=f"l{l}_pool_bwd")
        dhin, dgq, dgkv = _norms_bwd(sv["hin"], sv["gq"], sv["gkv"], dcq, dckv, dkpe, du, pw=PW,
                                     name=f"l{l}_norms_bwd")
        pg = PW // len(POOL_WINDOWS)
        gS[("pool_w", l)] = jnp.stack([dwbd[g * pg:(g + 1) * pg, g * pg:(g + 1) * pg]
                                       for g in range(len(POOL_WINDOWS))])
        gS[("pool_scale", l)], gS[("q_norm_g", l)], gS[("kv_norm_g", l)] = dps, dgq, dgkv
        dx = _mm(dhin, Wl["winp"], lead=0, tb=True, add=dxres, name=f"l{l}_dx1", tn=D)
        gW[("w_in", l)] = _mm(sv["x1b"], dhin, ta=True, out_dtype=BF16, name=f"l{l}_dwin", tn=DINP)
        heads = rs_first_level(l, "b")
        dx, rode = ffn_bwd(l, 0, sv["ffn1"], dx, 0, heads["ex"])
        rs_last_level(heads, rode)
        above = rs_first_level(l, "c")
    rs_last_level(above, _exchange_alone(above["ex"], name="rs_chips_last"))
    grad_x = dx.reshape(1, T, D)

    small_keys = []
    for l in range(L):
        small_keys += [("pool_w", l), ("pool_scale", l), ("q_norm_g", l), ("kv_norm_g", l)]
        small_keys += [("ln_g", l, k) for k in range(4)] + [("ln_b", l, k) for k in range(4)]
    flat = jnp.concatenate([gS[k].reshape(-1) for k in small_keys])
    n_small = flat.shape[0]
    rows = -(-n_small // (8 * LANE)) * 8
    flat = jnp.pad(flat, (0, rows * LANE - n_small)).reshape(rows, LANE)
    red = _all_reduce_small(flat, name="ar_small").reshape(-1)
    gsm, pos = {}, 0
    for k in small_keys:
        size = math.prod(gS[k].shape)
        gsm[k] = red[pos:pos + size].reshape(gS[k].shape)
        pos += size

    me = 4 * lax.axis_index("x") + 2 * lax.axis_index("y") + lax.axis_index("c")
    dsh = D // N_DEV
    stack = lambda f: jnp.stack([f(l) for l in range(L)])
    g_ln_g = stack(lambda l: jnp.concatenate([gsm[("ln_g", l, k)] for k in range(4)], axis=0))
    g_ln_b = stack(lambda l: jnp.concatenate([gsm[("ln_b", l, k)] for k in range(4)], axis=0))
    grads = {
        "ln_g": lax.dynamic_slice_in_dim(g_ln_g, me * dsh, dsh, axis=2),
        "ln_b": lax.dynamic_slice_in_dim(g_ln_b, me * dsh, dsh, axis=2),
        "ffn1_w13": stack(lambda l: gsh[("w13", 0, l)].T),
        "ffn1_w2": stack(lambda l: gsh[("w2", 0, l)]),
        "w_in": stack(lambda l: gsh[("w_in", l)][:, :DIN]),
        "pool_w": stack(lambda l: gsm[("pool_w", l)]),
        "pool_scale": stack(lambda l: gsm[("pool_scale", l)].reshape(PW)),
        "q_norm_g": stack(lambda l: gsm[("q_norm_g", l)].reshape(Q_LORA)),
        "w_uq": stack(lambda l: gsh[("w_uq", l)].T),
        "kv_norm_g": stack(lambda l: gsm[("kv_norm_g", l)].reshape(KV_LORA)),
        "w_ukv": stack(lambda l: gsh[("w_ukv", l)].T),
        "w_out": stack(lambda l: gsh[("w_out", l)]),
        "mem_wq": stack(lambda l: gsh[("mem_wq", l)]),
        "mem_wkv": stack(lambda l: gsh[("mem_wkv", l)].T),
        "mem_wo": stack(lambda l: gsh[("mem_wo", l)]),
        "ffn2_w13": stack(lambda l: gsh[("w13", 1, l)].T),
        "ffn2_w2": stack(lambda l: gsh[("w2", 1, l)]),
    }

    names = ["ln_g", "ln_b", "ffn1_w13", "ffn1_w2", "w_in", "pool_w", "pool_scale", "q_norm_g", "w_uq",
             "kv_norm_g", "w_ukv", "w_out", "mem_wq", "mem_wkv", "mem_wo", "ffn2_w13", "ffn2_w2"]
    weights = dict(ln_g=ln_g, ln_b=ln_b, ffn1_w13=ffn1_w13, ffn1_w2=ffn1_w2, w_in=w_in, pool_w=pool_w,
                   pool_scale=pool_scale, q_norm_g=q_norm_g, w_uq=w_uq, kv_norm_g=kv_norm_g, w_ukv=w_ukv,
                   w_out=w_out, mem_wq=mem_wq, mem_wkv=mem_wkv, mem_wo=mem_wo, ffn2_w13=ffn2_w13,
                   ffn2_w2=ffn2_w2)
    ms = dict(ln_g=m_ln_g, ln_b=m_ln_b, ffn1_w13=m_ffn1_w13, ffn1_w2=m_ffn1_w2, w_in=m_w_in, pool_w=m_pool_w,
              pool_scale=m_pool_scale, q_norm_g=m_q_norm_g, w_uq=m_w_uq, kv_norm_g=m_kv_norm_g,
              w_ukv=m_w_ukv, w_out=m_w_out, mem_wq=m_mem_wq, mem_wkv=m_mem_wkv, mem_wo=m_mem_wo,
              ffn2_w13=m_ffn2_w13, ffn2_w2=m_ffn2_w2)
    vs = dict(ln_g=v_ln_g, ln_b=v_ln_b, ffn1_w13=v_ffn1_w13, ffn1_w2=v_ffn1_w2, w_in=v_w_in, pool_w=v_pool_w,
              pool_scale=v_pool_scale, q_norm_g=v_q_norm_g, w_uq=v_w_uq, kv_norm_g=v_kv_norm_g,
              w_ukv=v_w_ukv, w_out=v_w_out, mem_wq=v_mem_wq, mem_wkv=v_mem_wkv, mem_wo=v_mem_wo,
              ffn2_w13=v_ffn2_w13, ffn2_w2=v_ffn2_w2)
    deltas, new_m, new_v = [], [], []
    for nme in names:
        d, mn, vn = _adamw(weights[nme], grads[nme], ms[nme], vs[nme], name=f"adamw_{nme}")
        deltas.append(d)
        new_m.append(mn)
        new_v.append(vn)
    return (loss, grad_x, *[grads[nme] for nme in names], *deltas, *new_m, *new_v)
```

```python
import functools
import math

import jax
import jax.numpy as jnp
from jax import lax
from jax.experimental import pallas as pl
from jax.experimental.pallas import tpu as pltpu

F32 = jnp.float32
BF16 = jnp.bfloat16
MESH = pl.DeviceIdType.MESH

CHUNK = 64
MEM_HEADS = 4
POOL_WINDOWS = (2, 4, 8, 16)
QK_NOPE = 128
QK_ROPE = 64
V_HEAD = 128
Q_LORA = 256
KV_LORA = 128
ROPE_BASE = 10000.0
LN_EPS = 1e-5
RMS_EPS = 1e-6
NEG_INF = -1e30
ADAM_LR = 0.001
ADAM_B1 = 0.9
ADAM_B2 = 0.999
ADAM_EPS = 1e-08
ADAM_WD = 0.01
ADAM_STEP = 10

N_DEV = 8
LANE = 128
HEAD_PAD = 2 * LANE
POOL_HALO = 16
VMEM_CAP = 56 * 1024 * 1024
VMEM_FLOOR = 32 * 1024 * 1024


def _tile(n, pref, mult):
    t = (min(pref, n) // mult) * mult
    while t >= mult:
        if n % t == 0:
            return t
        t -= mult
    return n


def _params(sem, est_bytes):
    limit = int(min(max(2 * est_bytes + (8 << 20), VMEM_FLOOR), VMEM_CAP))
    return pltpu.CompilerParams(dimension_semantics=sem, vmem_limit_bytes=limit)


def _nbytes(shape, dtype):
    return math.prod(shape) * jnp.dtype(dtype).itemsize


def _hbm(x):
    return pltpu.with_memory_space_constraint(x, pltpu.HBM)


def _dg(a, b, ca, cb):
    return lax.dot_general(a.astype(BF16), b.astype(BF16), (((ca,), (cb,)), ((), ())),
                           preferred_element_type=F32)


@jax.custom_vjp
def _bdot_nn(a, b):
    return _dg(a, b, 1, 0)


def _bdot_nn_fwd(a, b):
    return _dg(a, b, 1, 0), (a, b)


def _bdot_nn_bwd(res, ct):
    a, b = res
    return _dg(ct, b, 1, 1).astype(a.dtype), _dg(a, ct, 0, 0).astype(b.dtype)


_bdot_nn.defvjp(_bdot_nn_fwd, _bdot_nn_bwd)


@jax.custom_vjp
def _bdot_nt(a, b):
    return _dg(a, b, 1, 1)


def _bdot_nt_fwd(a, b):
    return _dg(a, b, 1, 1), (a, b)


def _bdot_nt_bwd(res, ct):
    a, b = res
    return _dg(ct, b, 1, 0).astype(a.dtype), _dg(ct, a, 0, 0).astype(b.dtype)


_bdot_nt.defvjp(_bdot_nt_fwd, _bdot_nt_bwd)


@functools.partial(jax.custom_vjp, nondiff_argnums=(1,))
def _lane_roll(x, shift):
    return pltpu.roll(x, shift % x.shape[1], axis=1)


def _lane_roll_fwd(x, shift):
    return _lane_roll(x, shift), None


def _lane_roll_bwd(shift, _, ct):
    return (_lane_roll(ct, -shift),)


_lane_roll.defvjp(_lane_roll_fwd, _lane_roll_bwd)


@functools.partial(jax.custom_vjp, nondiff_argnums=(1, 2))
def _cols(x, lo, hi):
    return x[:, lo:hi]


def _cols_fwd(x, lo, hi):
    return x[:, lo:hi], x.shape[1]


def _cols_bwd(lo, hi, width, ct):
    parts = []
    if lo > 0:
        parts.append(jnp.zeros((ct.shape[0], lo), ct.dtype))
    parts.append(ct)
    if hi < width:
        parts.append(jnp.zeros((ct.shape[0], width - hi), ct.dtype))
    return (jnp.concatenate(parts, axis=1) if len(parts) > 1 else ct,)


_cols.defvjp(_cols_fwd, _cols_bwd)


MM_VMEM_BUDGET = 20 * 1024 * 1024


def _mm(a, b, *, name, ta=False, tb=False, out_dtype=F32, lead=None, add=None, add_scale=1.0,
        tm=1024, tn=1024, tk=2816):
    if ta:
        K, M = a.shape
    else:
        M, K = a.shape
    bshape = b.shape[1:] if lead is not None else b.shape
    if tb:
        N, Kb = bshape
    else:
        Kb, N = bshape
    assert K == Kb, (name, a.shape, b.shape)

    def blocks(tm, tn, tk):
        tm = _tile(M, tm, LANE if ta else 16)
        tn = _tile(N, tn, LANE)
        tk = _tile(K, tk, LANE)
        nbytes = (tm * tk * a.dtype.itemsize + tk * tn * b.dtype.itemsize
                  + tm * tn * (jnp.dtype(out_dtype).itemsize + (4 if K // tk > 1 else 0)
                               + (add.dtype.itemsize if add is not None else 0)))
        return tm, tn, tk, nbytes

    tm, tn, tk, est = blocks(tm, tn, tk)
    for shrink in ("m", "k", "m", "k", "n"):
        if est <= MM_VMEM_BUDGET:
            break
        if shrink == "m":
            tm, tn, tk, est = blocks(max(tm // 2, LANE), tn, tk)
        elif shrink == "k":
            tm, tn, tk, est = blocks(tm, tn, max(tk // 2, LANE))
        else:
            tm, tn, tk, est = blocks(tm, max(tn // 2, LANE), tk)
    nk = K // tk
    ca = 0 if ta else 1
    cb = 1 if tb else 0

    def body(*refs):
        a_ref, b_ref = refs[0], refs[1]
        add_ref = refs[2] if add is not None else None
        o_ref = refs[3] if add is not None else refs[2]

        def finish(r):
            if add_ref is not None:
                r = r + add_scale * add_ref[...].astype(F32)
            o_ref[...] = r.astype(o_ref.dtype)

        if nk == 1:
            finish(_dg(a_ref[...], b_ref[...], ca, cb))
            return
        acc_ref = refs[-1]
        k = pl.program_id(2)

        @pl.when(k == 0)
        def _():
            acc_ref[...] = jnp.zeros_like(acc_ref)

        acc_ref[...] += _dg(a_ref[...], b_ref[...], ca, cb)

        @pl.when(k == nk - 1)
        def _():
            finish(acc_ref[...])

    a_blk = (tk, tm) if ta else (tm, tk)
    a_map = (lambda i, j, k: (k, i)) if ta else (lambda i, j, k: (i, k))
    b_blk = (tn, tk) if tb else (tk, tn)
    if lead is None:
        b_map = (lambda i, j, k: (j, k)) if tb else (lambda i, j, k: (k, j))
        b_spec = pl.BlockSpec(b_blk, b_map)
    else:
        b_map = (lambda i, j, k: (lead, j, k)) if tb else (lambda i, j, k: (lead, k, j))
        b_spec = pl.BlockSpec((None,) + b_blk, b_map)
    in_specs = [pl.BlockSpec(a_blk, a_map), b_spec]
    args = [a, b]
    if add is not None:
        in_specs.append(pl.BlockSpec((tm, tn), lambda i, j, k: (i, j)))
        args.append(add)
    return pl.pallas_call(
        body, name=name,
        grid=(M // tm, N // tn, nk),
        in_specs=in_specs,
        out_specs=pl.BlockSpec((tm, tn), lambda i, j, k: (i, j)),
        out_shape=jax.ShapeDtypeStruct((M, N), out_dtype),
        scratch_shapes=[pltpu.VMEM((tm, tn), F32)] if nk > 1 else [],
        compiler_params=_params(("parallel", "parallel", "arbitrary"), est + tm * tn * 4),
    )(*[_hbm(v) for v in args])


def _rowwise(fn, tiles, params, tile_outs, acc_outs=(), *, tm, name):
    tile_arrays, tile_specs = [], []
    for t in tiles:
        if isinstance(t, tuple):
            tile_arrays.append(t[0])
            tile_specs.append(t[1])
        else:
            tile_arrays.append(t)
            tile_specs.append(pl.BlockSpec((tm, t.shape[1]), lambda i: (i, 0)))
    T = tile_arrays[0].shape[0]
    nt, np_, nto, nao = len(tile_arrays), len(params), len(tile_outs), len(acc_outs)

    def body(*refs):
        i = pl.program_id(0)
        tvals = [r[...] for r in refs[:nt]]
        pvals = [r[...] for r in refs[nt:nt + np_]]
        to_refs = refs[nt + np_:nt + np_ + nto]
        ao_refs = refs[nt + np_ + nto:]
        touts, aouts = fn(i, tvals, pvals)
        for r, v in zip(to_refs, touts):
            r[...] = v.astype(r.dtype)
        if nao:
            @pl.when(i == 0)
            def _():
                for r in ao_refs:
                    r[...] = jnp.zeros_like(r)
            for r, v in zip(ao_refs, aouts):
                r[...] += v.astype(r.dtype)

    in_specs = tile_specs + [pl.BlockSpec(p.shape, lambda i: (0, 0)) for p in params]
    out_specs = [pl.BlockSpec((tm, c), lambda i: (i, 0)) for c, _ in tile_outs]
    out_specs += [pl.BlockSpec(s, lambda i: (0, 0)) for s, _ in acc_outs]
    out_shape = [jax.ShapeDtypeStruct((T, c), d) for c, d in tile_outs]
    out_shape += [jax.ShapeDtypeStruct(s, d) for s, d in acc_outs]
    width = sum(s.block_shape[-1] for s in tile_specs) + sum(c for c, _ in tile_outs)
    est = 6 * tm * width * 4 + sum(_nbytes(p.shape, F32) for p in params) * 4
    return pl.pallas_call(
        body, name=name, grid=(T // tm,),
        in_specs=in_specs, out_specs=out_specs, out_shape=out_shape,
        compiler_params=_params(("arbitrary",) if nao else ("parallel",), est),
    )(*[_hbm(v) for v in tile_arrays], *[_hbm(p) for p in params])


def _ln_fn(alpha, s, xres, y, g, b):
    z = alpha * xres.astype(F32) + s * y.astype(F32)
    mu = jnp.mean(z, axis=-1, keepdims=True)
    zc = z - mu
    var = jnp.mean(zc * zc, axis=-1, keepdims=True)
    return zc * lax.rsqrt(var + LN_EPS) * g + b


def _mm_ln(a, b, lead, xres, g, bias, *, alpha, s, name):
    M, K = a.shape
    N = b.shape[2]
    tm = _tile(M, 256, 16)

    def body(a_ref, b_ref, x_ref, g_ref, bias_ref, y_ref, xo_ref, xb_ref):
        y = _dg(a_ref[...], b_ref[...], 1, 0)
        y_ref[...] = y
        out = _ln_fn(alpha, s, x_ref[...], y, g_ref[...], bias_ref[...])
        xo_ref[...] = out
        xb_ref[...] = out.astype(BF16)

    row = pl.BlockSpec((tm, N), lambda i: (i, 0))
    vec = pl.BlockSpec((1, N), lambda i: (0, 0))
    est = tm * K * 2 + K * N * 2 + tm * N * (4 + 4 + 4 + 2 + 8)
    return pl.pallas_call(
        body, name=name, grid=(M // tm,),
        in_specs=[pl.BlockSpec((tm, K), lambda i: (i, 0)), pl.BlockSpec((None, K, N), lambda i: (lead, 0, 0)),
                  row, vec, vec],
        out_specs=[row, row, row],
        out_shape=[jax.ShapeDtypeStruct((M, N), F32), jax.ShapeDtypeStruct((M, N), F32),
                   jax.ShapeDtypeStruct((M, N), BF16)],
        compiler_params=_params(("parallel",), est),
    )(_hbm(a), _hbm(b), _hbm(xres), _hbm(g), _hbm(bias))


def _ln_bwd(xres, y, g, dout, *, alpha, s, name):
    T, D = xres.shape
    tm = _tile(T, 256, 16)

    def body(x_ref, y_ref, d_ref, g_ref, dx_ref, dy_ref, dg_ref, db_ref):
        @pl.when(pl.program_id(0) == 0)
        def _():
            dg_ref[...] = jnp.zeros_like(dg_ref)
            db_ref[...] = jnp.zeros_like(db_ref)

        z = alpha * x_ref[...] + s * y_ref[...]
        zc = z - jnp.mean(z, axis=-1, keepdims=True)
        r = lax.rsqrt(jnp.mean(zc * zc, axis=-1, keepdims=True) + LN_EPS)
        xh = zc * r
        d = d_ref[...]
        dxh = d * g_ref[...]
        dz = r * (dxh - jnp.mean(dxh, axis=-1, keepdims=True) - xh * jnp.mean(dxh * xh, axis=-1, keepdims=True))
        dx_ref[...] = alpha * dz
        dy_ref[...] = (s * dz).astype(dy_ref.dtype)
        dg_ref[...] += jnp.sum(d * xh, axis=0, keepdims=True)
        db_ref[...] += jnp.sum(d, axis=0, keepdims=True)

    row = pl.BlockSpec((tm, D), lambda i: (i, 0))
    vec = pl.BlockSpec((1, D), lambda i: (0, 0))
    return pl.pallas_call(
        body, name=name, grid=(T // tm,),
        in_specs=[row, row, row, vec], out_specs=[row, row, vec, vec],
        out_shape=[jax.ShapeDtypeStruct((T, D), F32), jax.ShapeDtypeStruct((T, D), BF16),
                   jax.ShapeDtypeStruct((1, D), F32), jax.ShapeDtypeStruct((1, D), F32)],
        compiler_params=_params(("arbitrary",), 12 * tm * D * 4),
    )(_hbm(xres), _hbm(y), _hbm(dout), _hbm(g))


FFN_TILE = 256


def _interleave(w, axis):
    n = w.shape[axis] // (2 * FFN_TILE)
    shp = w.shape[:axis] + (2, n, FFN_TILE) + w.shape[axis + 1:]
    return jnp.swapaxes(w.reshape(shp), axis, axis + 1).reshape(w.shape)


def _deinterleave(w, axis):
    n = w.shape[axis] // (2 * FFN_TILE)
    shp = w.shape[:axis] + (n, 2, FFN_TILE) + w.shape[axis + 1:]
    return jnp.swapaxes(w.reshape(shp), axis, axis + 1).reshape(w.shape)


def _ffn_up(xb, w13t, lead, *, name, ride=None):
    T, D = xb.shape
    F = w13t.shape[1] // 2
    tc = FFN_TILE
    tm = _tile(T, 1024, 16)

    def body(x_ref, w_ref, h_ref, a_ref):
        h = _dg(x_ref[...], w_ref[...], 1, 1)
        g, u = h[:, :tc], h[:, tc:]
        h_ref[...] = h.astype(h_ref.dtype)
        a_ref[...] = (g * jax.nn.sigmoid(g) * u).astype(a_ref.dtype)

    est = (tm * D + 2 * tc * D + 3 * tm * tc) * 2 + 3 * tm * tc * 4
    (h13, a), gathered = _host_call(
        body, name=name, grid=(T // tm, F // tc),
        in_specs=[pl.BlockSpec((tm, D), lambda i, j: (i, 0)),
                  pl.BlockSpec((None, 2 * tc, D), lambda i, j: (lead, j, 0))],
        out_specs=[pl.BlockSpec((tm, 2 * tc), lambda i, j: (i, j)),
                   pl.BlockSpec((tm, tc), lambda i, j: (i, j))],
        out_shape=[jax.ShapeDtypeStruct((T, 2 * F), BF16), jax.ShapeDtypeStruct((T, F), BF16)],
        args=[_hbm(xb), _hbm(w13t)], sem=("parallel", "parallel"), est=est, ride=ride)
    return h13, a, gathered


def _ffn_down_bwd(dyb, w2, lead, h13, *, name, ride=None):
    T, D = dyb.shape
    F = w2.shape[1]
    tc = FFN_TILE
    tm = _tile(T, 1024, 16)

    def body(dy_ref, w_ref, h_ref, dh_ref):
        d = _dg(dy_ref[...], w_ref[...], 1, 1)
        h = h_ref[...].astype(F32)
        g, u = h[:, :tc], h[:, tc:]
        sig = jax.nn.sigmoid(g)
        gs = g * sig
        dh_ref[...] = jnp.concatenate([d * u * (sig + gs * (1.0 - sig)), d * gs], axis=1).astype(dh_ref.dtype)

    est = (tm * D + tc * D + 4 * tm * tc) * 2 + 6 * tm * tc * 4
    (dh,), rode = _host_call(
        body, name=name, grid=(T // tm, F // tc),
        in_specs=[pl.BlockSpec((tm, D), lambda i, j: (i, 0)),
                  pl.BlockSpec((None, tc, D), lambda i, j: (lead, j, 0)),
                  pl.BlockSpec((tm, 2 * tc), lambda i, j: (i, j))],
        out_specs=[pl.BlockSpec((tm, 2 * tc), lambda i, j: (i, j))],
        out_shape=[jax.ShapeDtypeStruct((T, 2 * F), BF16)],
        args=[_hbm(dyb), _hbm(w2), _hbm(h13)], sem=("parallel", "parallel"), est=est, ride=ride)
    return dh, rode


def _pool_select(parts, pw):
    pg = pw // len(POOL_WINDOWS)
    grp = lax.broadcasted_iota(jnp.int32, parts[0].shape, 1) // pg
    out = parts[3]
    for g in (2, 1, 0):
        out = jnp.where(grp == g, parts[g], out)
    return out


def _pool_count(t0, rows, pw):
    pg = pw // len(POOL_WINDOWS)
    grp = lax.broadcasted_iota(jnp.int32, (rows, pw), 1) // pg
    win = jnp.where(grp == 0, POOL_WINDOWS[0],
                    jnp.where(grp == 1, POOL_WINDOWS[1],
                              jnp.where(grp == 2, POOL_WINDOWS[2], POOL_WINDOWS[3])))
    t = t0 + lax.broadcasted_iota(jnp.int32, (rows, pw), 0)
    return jnp.minimum(t + 1, win).astype(F32)


def _window_sums(ext, up):
    n = ext.shape[0]
    sums, cur, k = [], ext, 1
    for _ in POOL_WINDOWS:
        cur = cur + pltpu.roll(cur, (n - k) if up else k, axis=0)
        sums.append(cur)
        k *= 2
    return sums


def _pool_delta(u, halo, t0):
    tm, pw = u.shape
    ext = jnp.concatenate([halo, u], axis=0)
    sums = [s[POOL_HALO:, :] for s in _window_sums(ext, up=False)]
    return _pool_select(sums, pw) / _pool_count(t0, tm, pw) - u


def _pool_fwd(hin, wbd, scale, *, name):
    T = hin.shape[0]
    pw = wbd.shape[0]
    tm = _tile(T, 256, POOL_HALO)
    per = tm // POOL_HALO

    def body(u_ref, halo_ref, w_ref, s_ref, y_ref):
        i = pl.program_id(0)
        halo = jnp.where(i > 0, halo_ref[...], 0.0)
        d = _pool_delta(u_ref[...], halo, i * tm)
        y_ref[...] = (_dg(d, w_ref[...], 1, 0) * s_ref[...]).astype(y_ref.dtype)

    return pl.pallas_call(
        body, name=name, grid=(T // tm,),
        in_specs=[pl.BlockSpec((tm, pw), lambda i: (i, 0)),
                  pl.BlockSpec((POOL_HALO, pw), lambda i: (jnp.maximum(i * per - 1, 0), 0)),
                  pl.BlockSpec((pw, pw), lambda i: (0, 0)),
                  pl.BlockSpec((1, pw), lambda i: (0, 0))],
        out_specs=pl.BlockSpec((tm, pw), lambda i: (i, 0)),
        out_shape=jax.ShapeDtypeStruct((T, pw), BF16),
        compiler_params=_params(("parallel",), 16 * tm * pw * 4),
    )(_hbm(hin), _hbm(hin), _hbm(wbd), _hbm(scale))


def _pool_bwd(hin, dcat, wbd, scale, *, name):
    T = hin.shape[0]
    pw = wbd.shape[0]
    tm = _tile(T, 256, POOL_HALO)
    per = tm // POOL_HALO
    nt = T // tm

    def body(u_ref, halo_ref, dy_ref, dyn_ref, w_ref, s_ref, du_ref, dw_ref, ds_ref):
        i = pl.program_id(0)

        @pl.when(i == 0)
        def _():
            dw_ref[...] = jnp.zeros_like(dw_ref)
            ds_ref[...] = jnp.zeros_like(ds_ref)

        halo = jnp.where(i > 0, halo_ref[...], 0.0)
        d = _pool_delta(u_ref[...], halo, i * tm)
        w = w_ref[...]
        sc = s_ref[...]
        dy = dy_ref[...]
        dyn = jnp.where(i < nt - 1, dyn_ref[...], 0.0)
        ds_ref[...] += jnp.sum(dy * _dg(d, w, 1, 0), axis=0, keepdims=True)
        dys = dy * sc
        dw_ref[...] += _dg(d, dys, 0, 0)
        dys_ext = jnp.concatenate([dys, dyn * sc], axis=0)
        dd_ext = _dg(dys_ext, w, 1, 1)
        ddp = dd_ext / _pool_count(i * tm, tm + POOL_HALO, pw)
        sums = [s[:tm, :] for s in _window_sums(ddp, up=True)]
        du_ref[...] = _pool_select(sums, pw) - dd_ext[:tm, :]

    return pl.pallas_call(
        body, name=name, grid=(nt,),
        in_specs=[pl.BlockSpec((tm, pw), lambda i: (i, 0)),
                  pl.BlockSpec((POOL_HALO, pw), lambda i: (jnp.maximum(i * per - 1, 0), 0)),
                  pl.BlockSpec((tm, pw), lambda i: (i, 0)),
                  pl.BlockSpec((POOL_HALO, pw), lambda i: (jnp.minimum((i + 1) * per, nt * per - 1), 0)),
                  pl.BlockSpec((pw, pw), lambda i: (0, 0)),
                  pl.BlockSpec((1, pw), lambda i: (0, 0))],
        out_specs=[pl.BlockSpec((tm, pw), lambda i: (i, 0)),
                   pl.BlockSpec((pw, pw), lambda i: (0, 0)),
                   pl.BlockSpec((1, pw), lambda i: (0, 0))],
        out_shape=[jax.ShapeDtypeStruct((T, pw), F32),
                   jax.ShapeDtypeStruct((pw, pw), F32),
                   jax.ShapeDtypeStruct((1, pw), F32)],
        compiler_params=_params(("arbitrary",), 24 * tm * pw * 4),
    )(_hbm(hin), _hbm(hin), _hbm(dcat), _hbm(dcat), _hbm(wbd), _hbm(scale))


def _rms(x, g):
    return x * lax.rsqrt(jnp.mean(x * x, axis=-1, keepdims=True) + RMS_EPS) * g


def _norms_fn(pw, h, gq, gkv):
    o1 = pw + Q_LORA
    o2 = o1 + KV_LORA
    return (_rms(_cols(h, pw, o1), gq), _rms(_cols(h, o1, o2), gkv), _cols(h, o2, h.shape[1]))


def _norms_fwd(hin, gq, gkv, *, pw, name):
    tm = _tile(hin.shape[0], 256, 16)

    def fn(i, tv, pv):
        return _norms_fn(pw, tv[0], pv[0], pv[1]), ()

    return _rowwise(fn, [hin], [gq, gkv], [(Q_LORA, BF16), (KV_LORA, BF16), (LANE, F32)], tm=tm, name=name)


def _norms_bwd(hin, gq, gkv, dcq, dckv, dkpe, du, *, pw, name):
    tm = _tile(hin.shape[0], 256, 16)
    dinp = hin.shape[1]

    def fn(i, tv, pv):
        _, vjp = jax.vjp(functools.partial(_norms_fn, pw), tv[0], pv[0], pv[1])
        dh, dgq, dgkv = vjp((tv[1].astype(F32), tv[2].astype(F32), tv[3].astype(F32)))
        dh = jnp.concatenate([tv[4], dh[:, pw:]], axis=1)
        return (dh,), (dgq, dgkv)

    return _rowwise(fn, [hin, dcq, dckv, dkpe, du], [gq, gkv], [(dinp, BF16)],
                    [((1, Q_LORA), F32), ((1, KV_LORA), F32)], tm=tm, name=name)


def _heads_fn(H, qraw, kv, kpe, rc, rs1, rs2):
    half = QK_ROPE // 2
    scale = (QK_NOPE + QK_ROPE) ** -0.5

    def rope(blk):
        return blk * rc + _lane_roll(blk, -half) * rs1 + _lane_roll(blk, half) * rs2

    krot = rope(kpe)
    qs, ks, vs = [], [], []
    for h in range(H):
        lo = h * HEAD_PAD
        qs += [_cols(qraw, lo, lo + LANE) * scale, rope(_cols(qraw, lo + LANE, lo + HEAD_PAD)) * scale]
        ks += [_cols(kv, lo, lo + LANE), krot]
        vs += [_cols(kv, lo + LANE, lo + HEAD_PAD)]
    return jnp.concatenate(qs, axis=1), jnp.concatenate(ks, axis=1), jnp.concatenate(vs, axis=1)


def _heads_fwd(qraw, kv, kpe, tabs, *, H, name):
    tm = _tile(qraw.shape[0], 256, 16)

    def fn(i, tv, pv):
        return _heads_fn(H, *tv), ()

    return _rowwise(fn, [qraw, kv, kpe, *tabs], [],
                    [(H * HEAD_PAD, BF16), (H * HEAD_PAD, BF16), (H * V_HEAD, BF16)], tm=tm, name=name)


def _heads_bwd(dq, dk, dv, tabs, *, H, name):
    tm = _tile(dq.shape[0], 256, 16)

    def fn(i, tv, pv):
        z = jnp.zeros((tm, H * HEAD_PAD), F32)
        zk = jnp.zeros((tm, LANE), F32)
        rc, rs1, rs2 = tv[3], tv[4], tv[5]
        _, vjp = jax.vjp(lambda a, b, c: _heads_fn(H, a, b, c, rc, rs1, rs2), z, z, zk)
        return vjp((tv[0], tv[1], tv[2])), ()

    return _rowwise(fn, [dq, dk, dv, *tabs], [],
                    [(H * HEAD_PAD, BF16), (H * HEAD_PAD, BF16), (LANE, F32)], tm=tm, name=name)


def _diag_mask(rows, cols, row0):
    r = (row0 + lax.broadcasted_iota(jnp.int32, (rows, cols), 0)) // CHUNK
    c = lax.broadcasted_iota(jnp.int32, (rows, cols), 1) // CHUNK
    return r >= c


def _flash_fwd(qh, kh, vh, *, H, name, ride=None):
    T = qh.shape[0]
    t = _tile(T, 512, CHUNK)


    def body(q_ref, k_ref, v_ref, o_ref, lse_ref):
        i = pl.program_id(1)
        q = q_ref[...]

        def blk(j, carry, masked):
            m, l, acc = carry
            rows = pl.ds(pl.multiple_of(j * t, t), t)
            s = _dg(q, k_ref[rows, :], 1, 1)
            if masked:
                s = jnp.where(_diag_mask(t, t, 0), s, NEG_INF)
            mn = jnp.maximum(m, jnp.max(s, axis=1, keepdims=True))
            p = jnp.exp(s - mn)
            corr = jnp.exp(m - mn)
            l = corr * l + jnp.sum(p, axis=1, keepdims=True)
            acc = corr * acc + _dg(p, v_ref[rows, :], 1, 0)
            return mn, l, acc

        init = (jnp.full((t, 1), NEG_INF, F32), jnp.zeros((t, 1), F32), jnp.zeros((t, V_HEAD), F32))
        carry = lax.fori_loop(0, i, lambda j, c: blk(j, c, False), init)
        m, l, acc = blk(i, carry, True)
        o_ref[...] = (acc / l).astype(o_ref.dtype)
        lse_ref[...] = jnp.broadcast_to(m + jnp.log(l), (t, V_HEAD))

    est = 2 * T * (HEAD_PAD + V_HEAD) * 2 + 8 * t * t * 4
    (o, lse), gathered = _host_call(
        body, name=name, grid=(H, T // t),
        in_specs=[pl.BlockSpec((t, HEAD_PAD), lambda h, i: (i, h)),
                  pl.BlockSpec((T, HEAD_PAD), lambda h, i: (0, h)),
                  pl.BlockSpec((T, V_HEAD), lambda h, i: (0, h))],
        out_specs=[pl.BlockSpec((t, V_HEAD), lambda h, i: (i, h)),
                   pl.BlockSpec((t, V_HEAD), lambda h, i: (i, h))],
        out_shape=[jax.ShapeDtypeStruct((T, H * V_HEAD), BF16),
                   jax.ShapeDtypeStruct((T, H * V_HEAD), F32)],
        args=[_hbm(qh), _hbm(kh), _hbm(vh)], sem=("parallel", "parallel"), est=est, ride=ride)
    return o, lse, gathered


def _flash_bwd(qh, kh, vh, cat, dcat, lse, *, H, pw, name, ride=None):
    T = qh.shape[0]
    t = _tile(T, 512, CHUNK)
    nb = T // t
    off = pw // V_HEAD

    def body(q_ref, k_ref, v_ref, o_ref, do_ref, lse_ref, dq_ref, dk_ref, dv_ref):
        j = pl.program_id(1)

        @pl.when(j == 0)
        def _():
            dq_ref[...] = jnp.zeros_like(dq_ref)

        kj = k_ref[...]
        vj = v_ref[...]

        def blk(i, carry, masked):
            dk, dv = carry
            rows = pl.ds(pl.multiple_of(i * t, t), t)
            qi = q_ref[rows, :]
            doi = do_ref[rows, :]
            oi = o_ref[rows, :].astype(F32)
            lsei = lse_ref[rows, :][:, :1]
            s = _dg(qi, kj, 1, 1)
            if masked:
                s = jnp.where(_diag_mask(t, t, 0), s, NEG_INF)
            p = jnp.exp(s - lsei)
            dv = dv + _dg(p, doi, 0, 0)
            dp = _dg(doi, vj, 1, 1)
            di = jnp.sum(doi * oi, axis=1, keepdims=True)
            ds = p * (dp - di)
            dk = dk + _dg(ds, qi, 0, 0)
            dq_ref[rows, :] += _dg(ds, kj, 1, 0)
            return dk, dv

        carry = blk(j, (jnp.zeros((t, HEAD_PAD), F32), jnp.zeros((t, V_HEAD), F32)), True)
        dk, dv = lax.fori_loop(j + 1, nb, lambda i, c: blk(i, c, False), carry)
        dk_ref[...] = dk
        dv_ref[...] = dv

    est = T * (HEAD_PAD * 2 + V_HEAD * 2 + V_HEAD * 4 + V_HEAD * 4 + HEAD_PAD * 4) + 10 * t * t * 4
    (dq, dk, dv), gathered = _host_call(
        body, name=name, grid=(H, nb),
        in_specs=[pl.BlockSpec((T, HEAD_PAD), lambda h, j: (0, h)),
                  pl.BlockSpec((t, HEAD_PAD), lambda h, j: (j, h)),
                  pl.BlockSpec((t, V_HEAD), lambda h, j: (j, h)),
                  pl.BlockSpec((T, V_HEAD), lambda h, j: (0, off + h)),
                  pl.BlockSpec((T, V_HEAD), lambda h, j: (0, off + h)),
                  pl.BlockSpec((T, V_HEAD), lambda h, j: (0, h))],
        out_specs=[pl.BlockSpec((T, HEAD_PAD), lambda h, j: (0, h)),
                   pl.BlockSpec((t, HEAD_PAD), lambda h, j: (j, h)),
                   pl.BlockSpec((t, V_HEAD), lambda h, j: (j, h))],
        out_shape=[jax.ShapeDtypeStruct((T, H * HEAD_PAD), F32),
                   jax.ShapeDtypeStruct((T, H * HEAD_PAD), F32),
                   jax.ShapeDtypeStruct((T, H * V_HEAD), F32)],
        args=[_hbm(v) for v in (qh, kh, vh, cat, dcat, lse)], sem=("arbitrary", "arbitrary"), est=est,
        ride=ride)
    return dq, dk, dv, gathered


def _mem_fn(q, k, v):
    hd = q.shape[1] // MEM_HEADS
    outs = []
    for h in range(MEM_HEADS):
        lo, hi = h * hd, (h + 1) * hd
        s = _bdot_nt(_cols(q, lo, hi), _cols(k, lo, hi)) * hd ** -0.5
        e = jnp.exp(s - lax.stop_gradient(jnp.max(s, axis=1, keepdims=True)))
        p = e / jnp.sum(e, axis=1, keepdims=True)
        outs.append(_bdot_nn(p, _cols(v, lo, hi)))
    return jnp.concatenate(outs, axis=1)


def _mem_fwd(q, k, v, *, name):
    T, D = q.shape
    tm = _tile(T, 256, 16)

    def fn(i, tv, pv):
        return (_mem_fn(tv[0], pv[0], pv[1]),), ()

    return _rowwise(fn, [q], [k, v], [(D, BF16)], tm=tm, name=name)[0]


def _mem_bwd(q, k, v, do, *, name):
    T, D = q.shape
    tm = _tile(T, 256, 16)

    def fn(i, tv, pv):
        _, vjp = jax.vjp(_mem_fn, tv[0], pv[0], pv[1])
        dq, dk, dv = vjp(tv[1].astype(F32))
        return (dq,), (dk, dv)

    return _rowwise(fn, [q, do], [k, v], [(D, BF16)], [(k.shape, F32), (v.shape, F32)], tm=tm, name=name)


def _loss_head(y, target, *, name):
    T, D = y.shape
    tm = _tile(T, 256, 16)

    def fn(i, tv, pv):
        err = tv[0] - tv[1]
        part = 0.5 * jnp.sum(jnp.sum(err * err, axis=1, keepdims=True) / D, axis=0, keepdims=True)
        return (err / D,), (jnp.broadcast_to(part, (8, LANE)),)

    return _rowwise(fn, [y, target], [], [(D, F32)], [((8, LANE), F32)], tm=tm, name=name)


def _adamw(w, g, m, v, *, name):
    shape = w.shape
    if w.ndim != 3:
        lead3 = (1, math.prod(shape[:-1]), shape[-1])
        return [o.reshape(shape) for o in _adamw(*[a.reshape(lead3) for a in (w, g, m, v)], name=name)]
    Lw, R, C = shape
    tr = _tile(R, 512, 8)
    b1c = 1.0 - ADAM_B1 ** ADAM_STEP
    b2c = 1.0 - ADAM_B2 ** ADAM_STEP

    def body(w_ref, g_ref, m_ref, v_ref, d_ref, mo_ref, vo_ref):
        gg = g_ref[...]
        mn = ADAM_B1 * m_ref[...] + (1.0 - ADAM_B1) * gg
        vn = ADAM_B2 * v_ref[...] + (1.0 - ADAM_B2) * (gg * gg)
        d_ref[...] = -ADAM_LR * ((mn / b1c) / (jnp.sqrt(vn / b2c) + ADAM_EPS) + ADAM_WD * w_ref[...])
        mo_ref[...] = mn
        vo_ref[...] = vn

    spec = pl.BlockSpec((None, tr, C), lambda l, i: (l, i, 0))
    return pl.pallas_call(
        body, name=name, grid=(Lw, R // tr),
        in_specs=[spec] * 4, out_specs=[spec] * 3,
        out_shape=[jax.ShapeDtypeStruct(shape, F32)] * 3,
        compiler_params=_params(("parallel", "parallel"), 7 * tr * C * 4),
    )(*[_hbm(a) for a in (w, g, m, v)])


def _pair_sum(core, gs, landed, offs, *, name):
    n = len(gs)
    _, R, C = landed.shape
    rows = [g.shape[0] // N_DEV for g in gs]

    def body(core_ref, *refs):
        g_refs, l_ref, o_ref = refs[:n], refs[n], refs[n + 1]
        for g_ref, off, r in zip(g_refs, offs, rows):
            o_ref[off:off + r, :] = (g_ref[...].astype(F32) + l_ref[off:off + r, :].astype(F32)).astype(o_ref.dtype)

    slab = pl.BlockSpec((None, R, C), lambda p, core_ref: (p, 0, 0))
    own = [pl.BlockSpec((r, C), lambda p, core_ref: (2 * p + core_ref[0], 0)) for r in rows]
    return pl.pallas_call(
        body, name=name,
        grid_spec=pltpu.PrefetchScalarGridSpec(
            num_scalar_prefetch=1, grid=(4,), in_specs=own + [slab], out_specs=slab),
        out_shape=jax.ShapeDtypeStruct(landed.shape, landed.dtype),
        input_output_aliases={n + 1: 0},
        compiler_params=_params(("arbitrary",), 3 * R * C * 2 + R * C * 8),
    )(core, *[_hbm(g) for g in gs], _hbm(landed))


def _quad_sum(chip, part, gathered, used, *, name):
    C = part.shape[2]
    R = used
    tr = _tile(R, 256, 16)

    def body(chip_ref, own_ref, a_ref, b_ref, c_ref, o_ref):
        o_ref[...] = ((own_ref[...].astype(F32) + a_ref[...].astype(F32)) + b_ref[...].astype(F32)) \
            + c_ref[...].astype(F32)

    def other(k):
        return pl.BlockSpec((None, tr, C), lambda i, chip_ref: (chip_ref[0] ^ k, i, 0))

    return pl.pallas_call(
        body, name=name,
        grid_spec=pltpu.PrefetchScalarGridSpec(
            num_scalar_prefetch=1, grid=(R // tr,),
            in_specs=[pl.BlockSpec((None, tr, C), lambda i, chip_ref: (chip_ref[0], i, 0)),
                      other(1), other(2), other(3)],
            out_specs=pl.BlockSpec((tr, C), lambda i, chip_ref: (i, 0))),
        out_shape=jax.ShapeDtypeStruct((R, C), F32),
        compiler_params=_params(("arbitrary",), 8 * tr * C * 4),
    )(chip, _hbm(part), _hbm(gathered), _hbm(gathered), _hbm(gathered))


def _place():
    x, y, c = lax.axis_index("x"), lax.axis_index("y"), lax.axis_index("c")
    return x, y, c


ANY = pl.BlockSpec(memory_space=pl.ANY)


class _Gather:
    def __init__(self, shards):
        self.shards = list(shards)
        self.n = len(self.shards)
        self.out_shape = [jax.ShapeDtypeStruct((s.shape[0], N_DEV * s.shape[1], s.shape[2]), s.dtype)
                          for s in self.shards]
        self.scratch = [pltpu.SemaphoreType.DMA((7 * self.n,)), pltpu.SemaphoreType.DMA((7 * self.n,)),
                        pltpu.SemaphoreType.DMA((self.n,))]
        self.operands = [_hbm(s) for s in self.shards]

    def _bind(self, refs):
        n = self.n
        ins, outs = refs[:n], refs[n:2 * n]
        send_sems, recv_sems, local_sems = refs[2 * n:]
        x, y, c = _place()
        me, sib = (x, y, c), (x, y, 1 - c)
        chips = [(1 - x, y), (x, 1 - y), (1 - x, 1 - y)]

        def rows(w, p):
            r = self.shards[w].shape[1]
            idx = 4 * p[0] + 2 * p[1] + p[2]
            return outs[w].at[:, pl.ds(pl.multiple_of(idx * r, 8), r), :]

        def copy(w, k, block, to, src=None):
            return pltpu.make_async_remote_copy(
                src_ref=rows(w, block) if src is None else src, dst_ref=rows(w, block),
                send_sem=send_sems.at[w * 7 + k], recv_sem=recv_sems.at[w * 7 + k],
                device_id=to, device_id_type=MESH)

        def mine():
            return [pltpu.make_async_copy(ins[w], rows(w, me), local_sems.at[w]) for w in range(n)]

        def first():
            out = []
            for w in range(n):
                out.append(copy(w, 0, me, sib, src=ins[w]))
                out += [copy(w, 1 + j, me, (*chip, c), src=ins[w]) for j, chip in enumerate(chips)]
            return out

        def passed():
            return [copy(w, 4 + j, (*chip, c), sib) for j, chip in enumerate(chips) for w in range(n)]

        def landed():
            return [copy(w, 1 + j, (*chip, c), me) for j, chip in enumerate(chips) for w in range(n)]

        def last():
            out = []
            for w in range(n):
                out.append(copy(w, 0, sib, me))
                out += [copy(w, 4 + j, (*chip, 1 - c), me) for j, chip in enumerate(chips)]
            return out

        return mine, first, landed, passed, last

    def start(self, refs):
        mine, first, _, _, _ = self._bind(refs)
        for cp in mine() + first():
            cp.start()

    def forward(self, refs):
        _, _, landed, passed, _ = self._bind(refs)
        for arrived, fwd in zip(landed(), passed()):
            arrived.wait_recv()
            fwd.start()

    def finish(self, refs):
        mine, first, _, passed, last = self._bind(refs)
        for cp in last():
            cp.wait_recv()
        for cp in first() + passed():
            cp.wait_send()
        for cp in mine():
            cp.wait()


class _ChipExchange:
    def __init__(self, parts, used):
        self.ncl = len(parts)
        self.used = list(used)
        self.out_shape = [jax.ShapeDtypeStruct(p.shape, p.dtype) for p in parts]
        self.scratch = [pltpu.SemaphoreType.DMA((3 * self.ncl,)), pltpu.SemaphoreType.DMA((3 * self.ncl,))]
        self.operands = [_hbm(p) for p in parts]
        self.n = self.ncl

    def _bind(self, refs):
        ncl = self.ncl
        ins, outs = refs[:ncl], refs[ncl:2 * ncl]
        send_sems, recv_sems = refs[2 * ncl:]
        x, y, c = _place()
        chips = [(1 - x, y), (x, 1 - y), (1 - x, 1 - y)]
        here = 2 * x + y

        def copies(outgoing):
            out = []
            for k in range(ncl):
                rows = pl.ds(0, self.used[k])
                for j, (cx, cy) in enumerate(chips):
                    there = 2 * cx + cy
                    src, dst = (there, here) if outgoing else (here, there)
                    out.append(pltpu.make_async_remote_copy(
                        src_ref=ins[k].at[src, rows, :], dst_ref=outs[k].at[dst, rows, :],
                        send_sem=send_sems.at[3 * k + j], recv_sem=recv_sems.at[3 * k + j],
                        device_id=(cx, cy, c), device_id_type=MESH))
            return out

        return copies

    def start(self, refs):
        for cp in self._bind(refs)(True):
            cp.start()

    def forward(self, refs):
        pass

    def finish(self, refs):
        copies = self._bind(refs)
        for cp in copies(False):
            cp.wait_recv()
        for cp in copies(True):
            cp.wait_send()


class _Both:
    def __init__(self, members):
        self.members = list(members)
        self.n = sum(m.n for m in self.members)
        self.out_shape = [s for m in self.members for s in m.out_shape]
        self.scratch = [s for m in self.members for s in m.scratch]
        self.operands = [o for m in self.members for o in m.operands]

    def split(self, arrays):
        out, a = [], 0
        for m in self.members:
            out.append(list(arrays[a:a + m.n]))
            a += m.n
        return out

    def _refs(self, refs):
        ins, outs = self.split(refs[:self.n]), self.split(refs[self.n:2 * self.n])
        scr, b = [], 2 * self.n
        for m in self.members:
            scr.append(list(refs[b:b + len(m.scratch)]))
            b += len(m.scratch)
        return [(*i, *o, *s) for i, o, s in zip(ins, outs, scr)]

    def start(self, refs):
        for m, r in zip(self.members, self._refs(refs)):
            m.start(r)

    def forward(self, refs):
        for m, r in zip(self.members, self._refs(refs)):
            m.forward(r)

    def finish(self, refs):
        for m, r in zip(self.members, self._refs(refs)):
            m.finish(r)


def _exchange_alone(ex, *, name):
    def body(*refs):
        ex.start(refs)
        ex.forward(refs)
        ex.finish(refs)

    return pl.pallas_call(
        body, name=name, in_specs=[ANY] * ex.n, out_specs=[ANY] * ex.n,
        out_shape=ex.out_shape, scratch_shapes=ex.scratch,
    )(*ex.operands)


def _host_call(body, *, name, grid, in_specs, out_specs, out_shape, args, sem, est, ride=None):
    if ride is None:
        outs = pl.pallas_call(body, name=name, grid=grid, in_specs=in_specs, out_specs=out_specs,
                              out_shape=out_shape, compiler_params=_params(sem, est))(*args)
        return list(outs), []
    n_in, n_out, n = len(in_specs), len(out_specs), ride.n

    def full(*refs):
        ins, rin = refs[:n_in], refs[n_in:n_in + n]
        outs, rout = refs[n_in + n:n_in + n + n_out], refs[n_in + n + n_out:n_in + 2 * n + n_out]
        rrefs = (*rin, *rout, *refs[n_in + 2 * n + n_out:])
        step, total = _ride(ride, rrefs, grid)
        body(*ins, *outs)
        _ride_end(ride, rrefs, step, total)

    outs = pl.pallas_call(
        full, name=name, grid=grid,
        in_specs=list(in_specs) + [ANY] * n, out_specs=list(out_specs) + [ANY] * n,
        out_shape=list(out_shape) + ride.out_shape, scratch_shapes=ride.scratch,
        compiler_params=_params(("arbitrary",) * len(grid), est),
    )(*args, *ride.operands)
    return list(outs[:n_out]), list(outs[n_out:])


def _ride(ex, refs, grid):
    total = math.prod(grid)
    step = pl.program_id(0)
    for axis in range(1, len(grid)):
        step = step * grid[axis] + pl.program_id(axis)
    pl.when(step == 0)(lambda: ex.start(refs))
    return step, total


def _ride_end(ex, refs, step, total):
    pl.when(step == (3 * total) // 4)(lambda: ex.forward(refs))
    pl.when(step == total - 1)(lambda: ex.finish(refs))


def _class_layout(grads, classes):
    used = [0] * len(set(classes))
    offs = []
    for g, cl in zip(grads, classes):
        offs.append(used[cl])
        used[cl] += g.shape[0] // N_DEV
    return offs, used


def _rs_to_sibling(grads, classes, *, name):
    n = len(grads)
    offs, used = _class_layout(grads, classes)
    heights = used
    ncl = len(heights)
    cols = [next(g.shape[1] for g, cl in zip(grads, classes) if cl == k) for k in range(ncl)]

    def body(*refs):
        gs, land = refs[:n], refs[n:n + ncl]
        send_sems, recv_sems = refs[n + ncl:]
        x, y, c = _place()
        sib = (x, y, 1 - c)
        for p in range(4):
            for w in range(n):
                r = grads[w].shape[0] // N_DEV
                cl = classes[w]
                there = gs[w].at[pl.ds(pl.multiple_of((2 * p + 1 - c) * r, 8), r), :]
                pltpu.make_async_remote_copy(
                    src_ref=there, dst_ref=land[cl].at[p, pl.ds(offs[w], r), :],
                    send_sem=send_sems.at[cl * 4 + p], recv_sem=recv_sems.at[cl * 4 + p],
                    device_id=sib, device_id_type=MESH).start()
        for cl in range(ncl):
            for p in range(4):
                rows_used = land[cl].at[p, pl.ds(0, used[cl]), :]
                slab = pltpu.make_async_remote_copy(
                    src_ref=rows_used, dst_ref=rows_used,
                    send_sem=send_sems.at[cl * 4 + p], recv_sem=recv_sems.at[cl * 4 + p],
                    device_id=sib, device_id_type=MESH)
                slab.wait_send()
                slab.wait_recv()

    return pl.pallas_call(
        body, name=name,
        in_specs=[ANY] * n, out_specs=[ANY] * ncl,
        out_shape=[jax.ShapeDtypeStruct((4, heights[k], cols[k]), BF16) for k in range(ncl)],
        scratch_shapes=[pltpu.SemaphoreType.DMA((4 * ncl,))] * 2,
    )(*[_hbm(g) for g in grads])


def _all_reduce_small(v, *, name):
    R = v.shape[0]

    def body(v_ref, o_ref, buf, send_sems, recv_sems):
        x, y, c = _place()
        me = 4 * x + 2 * y + c
        buf[me] = v_ref[...]
        copies = []
        for k in range(1, N_DEV):
            fx, fy, fc = (k >> 2) & 1, (k >> 1) & 1, k & 1
            to = (x ^ fx, y ^ fy, c ^ fc)
            cp = pltpu.make_async_remote_copy(
                src_ref=v_ref, dst_ref=buf.at[me],
                send_sem=send_sems.at[k - 1], recv_sem=recv_sems.at[k - 1],
                device_id=to, device_id_type=MESH)
            cp.start()
            copies.append(cp)
        for k in range(1, N_DEV):
            fx, fy, fc = (k >> 2) & 1, (k >> 1) & 1, k & 1
            frm = 4 * (x ^ fx) + 2 * (y ^ fy) + (c ^ fc)
            pltpu.make_async_remote_copy(
                src_ref=v_ref, dst_ref=buf.at[frm],
                send_sem=send_sems.at[k - 1], recv_sem=recv_sems.at[k - 1],
                device_id=(x ^ fx, y ^ fy, c ^ fc), device_id_type=MESH).wait_recv()
        for cp in copies:
            cp.wait_send()
        acc = buf[0]
        for d in range(1, N_DEV):
            acc = acc + buf[d]
        o_ref[...] = acc

    vm = pl.BlockSpec(memory_space=pltpu.VMEM)
    return pl.pallas_call(
        body, name=name, in_specs=[vm], out_specs=vm,
        out_shape=jax.ShapeDtypeStruct((R, LANE), F32),
        scratch_shapes=[pltpu.VMEM((N_DEV, R, LANE), F32),
                        pltpu.SemaphoreType.DMA((N_DEV - 1,)), pltpu.SemaphoreType.DMA((N_DEV - 1,))],
        compiler_params=pltpu.CompilerParams(vmem_limit_bytes=VMEM_FLOOR),
    )(v)


def _rope_tables(positions):
    half = QK_ROPE // 2
    inv_freq = ROPE_BASE ** (-jnp.arange(half, dtype=F32) / half)
    ang = positions.astype(F32)[:, None] * inv_freq
    cos, sin = jnp.cos(ang), jnp.sin(ang)
    z = jnp.zeros_like(cos)
    z2 = jnp.zeros((positions.shape[0], LANE - QK_ROPE), F32)
    rc = jnp.concatenate([cos, cos, z2], axis=1)
    rs1 = jnp.concatenate([-sin, z, z2], axis=1)
    rs2 = jnp.concatenate([z, sin, z2], axis=1)
    return rc, rs1, rs2


def _block_diag(pool_w):
    G, pg, _ = pool_w.shape
    out = jnp.zeros((G * pg, G * pg), pool_w.dtype)
    for g in range(G):
        out = lax.dynamic_update_slice(out, pool_w[g], (g * pg, g * pg))
    return out


def kernel(x, mem, positions, ln_g, ln_b, ffn1_w13, ffn1_w2, w_in, pool_w, pool_scale, q_norm_g, w_uq, kv_norm_g, w_ukv, w_out, mem_wq, mem_wkv, mem_wo, ffn2_w13, ffn2_w2, loss_target, m_ln_g, m_ln_b, m_ffn1_w13, m_ffn1_w2, m_w_in, m_pool_w, m_pool_scale, m_q_norm_g, m_w_uq, m_kv_norm_g, m_w_ukv, m_w_out, m_mem_wq, m_mem_wkv, m_mem_wo, m_ffn2_w13, m_ffn2_w2, v_ln_g, v_ln_b, v_ffn1_w13, v_ffn1_w2, v_w_in, v_pool_w, v_pool_scale, v_q_norm_g, v_w_uq, v_kv_norm_g, v_w_ukv, v_w_out, v_mem_wq, v_mem_wkv, v_mem_wo, v_ffn2_w13, v_ffn2_w2):
    L = ln_g.shape[0]
    T, D = x.shape[1], x.shape[2]
    F = ffn1_w2.shape[1] * N_DEV
    PW = D // 4
    H = (D - PW) // V_HEAD
    DIN = w_in.shape[2]
    DINP = PW + Q_LORA + KV_LORA + LANE
    QW = QK_NOPE + QK_ROPE
    alpha = (2 * L) ** 0.25
    x2d = x.reshape(T, D)
    memb = mem.reshape(mem.shape[1], D).astype(BF16)
    target = loss_target.reshape(T, D)
    tabs = _rope_tables(positions.reshape(T))

    def shards_of(l):
        return dict(
            w13a=ffn1_w13[l].T[None].astype(BF16),
            w13b=ffn2_w13[l].T[None].astype(BF16),
            w2a=ffn1_w2[l][None].astype(BF16),
            w2b=ffn2_w2[l][None].astype(BF16),
            wsq=jnp.stack([w_out[l], mem_wq[l], mem_wo[l]]).astype(BF16),
            wkvT=mem_wkv[l].T[None].astype(BF16),
            winp=jnp.pad(w_in[l], ((0, 0), (0, DINP - DIN)))[None].astype(BF16),
            wuqT=w_uq[l].T[None].astype(BF16),
            wukvT=w_ukv[l].T[None].astype(BF16),
        )

    SMALL = ("winp", "wuqT", "wukvT")
    shards = [shards_of(l) for l in range(L)]
    W = [dict() for _ in range(L)]

    def rider(spec):
        return _Gather([shards[l][n] for l, n in spec]) if spec else None

    def arrived(spec, arrays):
        for (l, n), a in zip(spec, arrays):
            if n in ("w13a", "w13b"):
                a = _interleave(a, 1)
            elif n == "wuqT":
                a = jnp.pad(a.reshape(H, QW, Q_LORA), ((0, 0), (0, HEAD_PAD - QW), (0, 0)))
                a = a.reshape(1, H * HEAD_PAD, Q_LORA)
            W[l][n] = a

    ln_shard = jnp.concatenate([ln_g.reshape(1, 4 * L, -1), ln_b.reshape(1, 4 * L, -1)], axis=1)
    spec0 = [(0, "w13a"), (0, "w2a")]
    got = _exchange_alone(_Gather([shards[l][n] for l, n in spec0] + [ln_shard]), name="ag_first")
    arrived(spec0, got[:-1])
    lnp = jnp.moveaxis(got[-1].reshape(N_DEV, 2, L, 4, D // N_DEV), 0, 3).reshape(2, L, 4, D)
    lng, lnb = lnp[0], lnp[1]
    wbd = [_block_diag(pool_w[l]).astype(BF16) for l in range(L)]

    def ffn_fwd(l, which, xres, xb, k, spec):
        ab = "ab"[which]
        h13, a, rode = _ffn_up(xb, W[l]["w13" + ab], 0, name=f"l{l}_ffn{which}_up", ride=rider(spec))
        arrived(spec, rode)
        y, xo, xob = _mm_ln(a, W[l]["w2" + ab], 0, xres, lng[l, k:k + 1], lnb[l, k:k + 1], alpha=alpha, s=0.5,
                            name=f"l{l}_ffn{which}_y_ln{k}")
        return dict(xres=xres, xb=xb, h13=h13, a=a, y=y), xo, xob

    saved = []
    xres, xb = x2d, x2d.astype(BF16)
    for l in range(L):
        sv = {}
        more = l + 1 < L
        Wl = W[l]
        spec = [(l, n) for n in (SMALL if l == 0 else ())] + [(l, "wsq"), (l, "wkvT")]
        sv["ffn1"], x1, x1b = ffn_fwd(l, 0, xres, xb, 0, spec)
        hin = _mm(x1b, Wl["winp"], lead=0, name=f"l{l}_hin")
        pscale = pool_scale[l].reshape(1, PW)
        gq, gkv = q_norm_g[l].reshape(1, Q_LORA), kv_norm_g[l].reshape(1, KV_LORA)
        ypool = _pool_fwd(hin, wbd[l], pscale, name=f"l{l}_pool")
        cqn, ckvn, kpe = _norms_fwd(hin, gq, gkv, pw=PW, name=f"l{l}_norms")
        qraw = _mm(cqn, Wl["wuqT"], lead=0, tb=True, name=f"l{l}_qraw")
        kv = _mm(ckvn, Wl["wukvT"], lead=0, tb=True, name=f"l{l}_kv")
        qh, kh, vh = _heads_fwd(qraw, kv, kpe, tabs, H=H, name=f"l{l}_heads")
        spec = [(l, "w13b"), (l, "w2b")] + ([(l + 1, "w13a")] if more else [])
        o, lse, rode = _flash_fwd(qh, kh, vh, H=H, name=f"l{l}_flash", ride=rider(spec))
        arrived(spec, rode)
        cat = jnp.concatenate([ypool, o], axis=1)
        ymix, x2, x2b = _mm_ln(cat, Wl["wsq"], 0, x1, lng[l, 1:2], lnb[l, 1:2], alpha=alpha, s=1.0,
                               name=f"l{l}_ymix_ln1")
        qm = _mm(x2b, Wl["wsq"], lead=1, out_dtype=BF16, name=f"l{l}_qm")
        kvm = _mm(memb, Wl["wkvT"], lead=0, tb=True, name=f"l{l}_kvm")
        km, vm = kvm[:, :D], kvm[:, D:]
        om = _mem_fwd(qm, km, vm, name=f"l{l}_memattn")
        ymem, x3, x3b = _mm_ln(om, Wl["wsq"], 2, x2, lng[l, 2:3], lnb[l, 2:3], alpha=alpha, s=1.0,
                               name=f"l{l}_ymem_ln2")
        spec = [(l + 1, n) for n in ("w2a", *SMALL)] if more else []
        sv["ffn2"], x4, x4b = ffn_fwd(l, 1, x3, x3b, 3, spec)
        sv.update(x1=x1, x1b=x1b, hin=hin, pscale=pscale, gq=gq, gkv=gkv, cqn=cqn, ckvn=ckvn,
                  qh=qh, kh=kh, vh=vh, lse=lse, cat=cat, ymix=ymix, x2=x2, x2b=x2b, qm=qm, km=km, vm=vm,
                  om=om, ymem=ymem)
        saved.append(sv)
        xres, xb = x4, x4b

    dx, loss_blk = _loss_head(xres, target, name="loss_head")
    loss = lax.psum(loss_blk[0, 0], ("x", "y", "c"))

    gW = {}
    gS = {}

    def ffn_bwd(l, which, sv, dx, k, ride):
        tag = f"l{l}_ffn{which}"
        dxres, dyb, dg, db = _ln_bwd(sv["xres"], sv["y"], lng[l, k:k + 1], dx,
                                     alpha=alpha, s=0.5, name=f"l{l}_ln{k}_bwd")
        gW[("w2", which, l)] = _mm(sv["a"], dyb, ta=True, out_dtype=BF16, name=f"{tag}_dw2", tn=D)
        dh, rode = _ffn_down_bwd(dyb, W[l]["w2" + "ab"[which]], 0, sv["h13"], name=f"{tag}_dh", ride=ride)
        dxn = _mm(dh, W[l]["w13" + "ab"[which]], lead=0, add=dxres, name=f"{tag}_dx", tn=D)
        dw13 = _mm(dh, sv["xb"], ta=True, out_dtype=BF16, name=f"{tag}_dw13", tn=D)
        gW[("w13", which, l)] = _deinterleave(dw13, 0)
        gS[("ln_g", l, k)], gS[("ln_b", l, k)] = dg, db
        return dxn, rode

    core = lax.axis_index("c").astype(jnp.int32).reshape(1)
    chip = (2 * lax.axis_index("x") + lax.axis_index("y")).astype(jnp.int32).reshape(1)
    gsh = {}

    def rs_first_level(l, group):
        keys, classes = {
            "a": ([("w13", 1, l), ("w2", 1, l), ("mem_wkv", l), ("mem_wq", l), ("mem_wo", l)], [0] * 5),
            "b": ([("w_out", l), ("w_in", l), ("w_uq", l), ("w_ukv", l)], [0, 1, 2, 3]),
            "c": ([("w13", 0, l), ("w2", 0, l)], [0, 0]),
        }[group]
        tag = f"l{l}{group}"
        garrs = []
        for key in keys:
            g = gW[key]
            if key[0] == "w_uq":
                g = g.reshape(H, HEAD_PAD, Q_LORA)[:, :QW, :].reshape(H * QW, Q_LORA)
            garrs.append(g)
        offs, used = _class_layout(garrs, classes)
        parts = list(_rs_to_sibling(garrs, classes, name=f"{tag}_rs_sibling"))
        for cl in range(len(parts)):
            mine = [w for w, c in enumerate(classes) if c == cl]
            parts[cl] = _pair_sum(core, [garrs[w] for w in mine], parts[cl], [offs[w] for w in mine],
                                  name=f"{tag}_rs_pair_sum{cl}")
        return dict(tag=tag, keys=keys, garrs=garrs, classes=classes, offs=offs, used=used, parts=parts,
                    ex=_ChipExchange(parts, used))

    def rs_last_level(st, gathered):
        sums = [_quad_sum(chip, p, a, u, name=f"{st['tag']}_rs_quad_sum{k}")
                for k, (p, a, u) in enumerate(zip(st["parts"], gathered, st["used"]))]
        for key, g, cl, off in zip(st["keys"], st["garrs"], st["classes"], st["offs"]):
            gsh[key] = sums[cl][off:off + g.shape[0] // N_DEV, :]

    above = None
    for l in reversed(range(L)):
        sv = saved[l]
        Wl = W[l]
        dx, _ = ffn_bwd(l, 1, sv["ffn2"], dx, 3, None)
        dxres, dyb, dg, db = _ln_bwd(sv["x2"], sv["ymem"], lng[l, 2:3], dx,
                                     alpha=alpha, s=1.0, name=f"l{l}_ln2_bwd")
        gS[("ln_g", l, 2)], gS[("ln_b", l, 2)] = dg, db
        dom = _mm(dyb, Wl["wsq"], lead=2, tb=True, out_dtype=BF16, name=f"l{l}_dom")
        gW[("mem_wo", l)] = _mm(sv["om"], dyb, ta=True, out_dtype=BF16, name=f"l{l}_dwo", tn=D)
        dqm, dkm, dvm = _mem_bwd(sv["qm"], sv["km"], sv["vm"], dom, name=f"l{l}_memattn_bwd")
        dx = _mm(dqm, Wl["wsq"], lead=1, tb=True, add=dxres, name=f"l{l}_dx2", tn=D)
        gW[("mem_wq", l)] = _mm(sv["x2b"], dqm, ta=True, out_dtype=BF16, name=f"l{l}_dwq", tn=D)
        dkvm = jnp.concatenate([dkm, dvm], axis=1).astype(BF16)
        gW[("mem_wkv", l)] = _mm(dkvm, memb, ta=True, out_dtype=BF16, name=f"l{l}_dwkv", tn=D)
        dxres, dyb, dg, db = _ln_bwd(sv["x1"], sv["ymix"], lng[l, 1:2], dx,
                                     alpha=alpha, s=1.0, name=f"l{l}_ln1_bwd")
        gS[("ln_g", l, 1)], gS[("ln_b", l, 1)] = dg, db
        dcat = _mm(dyb, Wl["wsq"], lead=0, tb=True, name=f"l{l}_dcat", tn=D)
        gW[("w_out", l)] = _mm(sv["cat"], dyb, ta=True, out_dtype=BF16, name=f"l{l}_dwout", tn=D)
        riding = [rs_first_level(l, "a")] + ([above] if above else [])
        both = _Both([st["ex"] for st in riding])
        dqh, dkh, dvh, rode = _flash_bwd(sv["qh"], sv["kh"], sv["vh"], sv["cat"], dcat, sv["lse"], H=H, pw=PW,
                                         name=f"l{l}_flash_bwd", ride=both)
        for st, got in zip(riding, both.split(rode)):
            rs_last_level(st, got)
        dqraw, dkv, dkpe = _heads_bwd(dqh, dkh, dvh, tabs, H=H, name=f"l{l}_heads_bwd")
        dcq = _mm(dqraw, Wl["wuqT"], lead=0, name=f"l{l}_dcq")
        gW[("w_uq", l)] = _mm(dqraw, sv["cqn"], ta=True, out_dtype=BF16, name=f"l{l}_dwuq")
        dckv = _mm(dkv, Wl["wukvT"], lead=0, name=f"l{l}_dckv")
        gW[("w_ukv", l)] = _mm(dkv, sv["ckvn"], ta=True, out_dtype=BF16, name=f"l{l}_dwukv")
        du, dwbd, dps = _pool_bwd(sv["hin"], dcat, wbd[l], sv["pscale"], name=f"l{l}_pool_bwd")
        dhin, dgq, dgkv = _norms_bwd(sv["hin"], sv["gq"], sv["gkv"], dcq, dckv, dkpe, du, pw=PW,
                                     name=f"l{l}_norms_bwd")
        pg = PW // len(POOL_WINDOWS)
        gS[("pool_w", l)] = jnp.stack([dwbd[g * pg:(g + 1) * pg, g * pg:(g + 1) * pg]
                                       for g in range(len(POOL_WINDOWS))])
        gS[("pool_scale", l)], gS[("q_norm_g", l)], gS[("kv_norm_g", l)] = dps, dgq, dgkv
        dx = _mm(dhin, Wl["winp"], lead=0, tb=True, add=dxres, name=f"l{l}_dx1", tn=D)
        gW[("w_in", l)] = _mm(sv["x1b"], dhin, ta=True, out_dtype=BF16, name=f"l{l}_dwin", tn=DINP)
        heads = rs_first_level(l, "b")
        dx, rode = ffn_bwd(l, 0, sv["ffn1"], dx, 0, heads["ex"])
        rs_last_level(heads, rode)
        above = rs_first_level(l, "c")
    rs_last_level(above, _exchange_alone(above["ex"], name="rs_chips_last"))
    grad_x = dx.reshape(1, T, D)

    small_keys = []
    for l in range(L):
        small_keys += [("pool_w", l), ("pool_scale", l), ("q_norm_g", l), ("kv_norm_g", l)]
        small_keys += [("ln_g", l, k) for k in range(4)] + [("ln_b", l, k) for k in range(4)]
    flat = jnp.concatenate([gS[k].reshape(-1) for k in small_keys])
    n_small = flat.shape[0]
    rows = -(-n_small // (8 * LANE)) * 8
    flat = jnp.pad(flat, (0, rows * LANE - n_small)).reshape(rows, LANE)
    red = _all_reduce_small(flat, name="ar_small").reshape(-1)
    gsm, pos = {}, 0
    for k in small_keys:
        size = math.prod(gS[k].shape)
        gsm[k] = red[pos:pos + size].reshape(gS[k].shape)
        pos += size

    me = 4 * lax.axis_index("x") + 2 * lax.axis_index("y") + lax.axis_index("c")
    dsh = D // N_DEV
    stack = lambda f: jnp.stack([f(l) for l in range(L)])
    g_ln_g = stack(lambda l: jnp.concatenate([gsm[("ln_g", l, k)] for k in range(4)], axis=0))
    g_ln_b = stack(lambda l: jnp.concatenate([gsm[("ln_b", l, k)] for k in range(4)], axis=0))
    grads = {
        "ln_g": lax.dynamic_slice_in_dim(g_ln_g, me * dsh, dsh, axis=2),
        "ln_b": lax.dynamic_slice_in_dim(g_ln_b, me * dsh, dsh, axis=2),
        "ffn1_w13": stack(lambda l: gsh[("w13", 0, l)].T),
        "ffn1_w2": stack(lambda l: gsh[("w2", 0, l)]),
        "w_in": stack(lambda l: gsh[("w_in", l)][:, :DIN]),
        "pool_w": stack(lambda l: gsm[("pool_w", l)]),
        "pool_scale": stack(lambda l: gsm[("pool_scale", l)].reshape(PW)),
        "q_norm_g": stack(lambda l: gsm[("q_norm_g", l)].reshape(Q_LORA)),
        "w_uq": stack(lambda l: gsh[("w_uq", l)].T),
        "kv_norm_g": stack(lambda l: gsm[("kv_norm_g", l)].reshape(KV_LORA)),
        "w_ukv": stack(lambda l: gsh[("w_ukv", l)].T),
        "w_out": stack(lambda l: gsh[("w_out", l)]),
        "mem_wq": stack(lambda l: gsh[("mem_wq", l)]),
        "mem_wkv": stack(lambda l: gsh[("mem_wkv", l)].T),
        "mem_wo": stack(lambda l: gsh[("mem_wo", l)]),
        "ffn2_w13": stack(lambda l: gsh[("w13", 1, l)].T),
        "ffn2_w2": stack(lambda l: gsh[("w2", 1, l)]),
    }

    names = ["ln_g", "ln_b", "ffn1_w13", "ffn1_w2", "w_in", "pool_w", "pool_scale", "q_norm_g", "w_uq",
             "kv_norm_g", "w_ukv", "w_out", "mem_wq", "mem_wkv", "mem_wo", "ffn2_w13", "ffn2_w2"]
    weights = dict(ln_g=ln_g, ln_b=ln_b, ffn1_w13=ffn1_w13, ffn1_w2=ffn1_w2, w_in=w_in, pool_w=pool_w,
                   pool_scale=pool_scale, q_norm_g=q_norm_g, w_uq=w_uq, kv_norm_g=kv_norm_g, w_ukv=w_ukv,
                   w_out=w_out, mem_wq=mem_wq, mem_wkv=mem_wkv, mem_wo=mem_wo, ffn2_w13=ffn2_w13,
                   ffn2_w2=ffn2_w2)
    ms = dict(ln_g=m_ln_g, ln_b=m_ln_b, ffn1_w13=m_ffn1_w13, ffn1_w2=m_ffn1_w2, w_in=m_w_in, pool_w=m_pool_w,
              pool_scale=m_pool_scale, q_norm_g=m_q_norm_g, w_uq=m_w_uq, kv_norm_g=m_kv_norm_g,
              w_ukv=m_w_ukv, w_out=m_w_out, mem_wq=m_mem_wq, mem_wkv=m_mem_wkv, mem_wo=m_mem_wo,
              ffn2_w13=m_ffn2_w13, ffn2_w2=m_ffn2_w2)
    vs = dict(ln_g=v_ln_g, ln_b=v_ln_b, ffn1_w13=v_ffn1_w13, ffn1_w2=v_ffn1_w2, w_in=v_w_in, pool_w=v_pool_w,
              pool_scale=v_pool_scale, q_norm_g=v_q_norm_g, w_uq=v_w_uq, kv_norm_g=v_kv_norm_g,
              w_ukv=v_w_ukv, w_out=v_w_out, mem_wq=v_mem_wq, mem_wkv=v_mem_wkv, mem_wo=v_mem_wo,
              ffn2_w13=v_ffn2_w13, ffn2_w2=v_ffn2_w2)
    deltas, new_m, new_v = [], [], []
    for nme in names:
        d, mn, vn = _adamw(weights[nme], grads[nme], ms[nme], vs[nme], name=f"adamw_{nme}")
        deltas.append(d)
        new_m.append(mn)
        new_v.append(vn)
    return (loss, grad_x, *[grads[nme] for nme in names], *deltas, *new_m, *new_v)
```

```python
import functools
import math

import jax
import jax.numpy as jnp
from jax import lax
from jax.experimental import pallas as pl
from jax.experimental.pallas import tpu as pltpu

F32 = jnp.float32
BF16 = jnp.bfloat16
MESH = pl.DeviceIdType.MESH

CHUNK = 64
MEM_HEADS = 4
POOL_WINDOWS = (2, 4, 8, 16)
QK_NOPE = 128
QK_ROPE = 64
V_HEAD = 128
Q_LORA = 256
KV_LORA = 128
ROPE_BASE = 10000.0
LN_EPS = 1e-5
RMS_EPS = 1e-6
NEG_INF = -1e30
ADAM_LR = 0.001
ADAM_B1 = 0.9
ADAM_B2 = 0.999
ADAM_EPS = 1e-08
ADAM_WD = 0.01
ADAM_STEP = 10

N_DEV = 8
LANE = 128
HEAD_PAD = 2 * LANE
POOL_HALO = 16
VMEM_CAP = 56 * 1024 * 1024
VMEM_FLOOR = 32 * 1024 * 1024


def _tile(n, pref, mult):
    t = (min(pref, n) // mult) * mult
    while t >= mult:
        if n % t == 0:
            return t
        t -= mult
    return n


def _params(sem, est_bytes):
    limit = int(min(max(2 * est_bytes + (8 << 20), VMEM_FLOOR), VMEM_CAP))
    return pltpu.CompilerParams(dimension_semantics=sem, vmem_limit_bytes=limit)


def _nbytes(shape, dtype):
    return math.prod(shape) * jnp.dtype(dtype).itemsize


def _hbm(x):
    return pltpu.with_memory_space_constraint(x, pltpu.HBM)


def _out(shape, dtype):
    return pltpu.HBM(tuple(shape), dtype)


def _dg(a, b, ca, cb):
    return lax.dot_general(a.astype(BF16), b.astype(BF16), (((ca,), (cb,)), ((), ())),
                           preferred_element_type=F32)


@jax.custom_vjp
def _bdot_nn(a, b):
    return _dg(a, b, 1, 0)


def _bdot_nn_fwd(a, b):
    return _dg(a, b, 1, 0), (a, b)


def _bdot_nn_bwd(res, ct):
    a, b = res
    return _dg(ct, b, 1, 1).astype(a.dtype), _dg(a, ct, 0, 0).astype(b.dtype)


_bdot_nn.defvjp(_bdot_nn_fwd, _bdot_nn_bwd)


@jax.custom_vjp
def _bdot_nt(a, b):
    return _dg(a, b, 1, 1)


def _bdot_nt_fwd(a, b):
    return _dg(a, b, 1, 1), (a, b)


def _bdot_nt_bwd(res, ct):
    a, b = res
    return _dg(ct, b, 1, 0).astype(a.dtype), _dg(ct, a, 0, 0).astype(b.dtype)


_bdot_nt.defvjp(_bdot_nt_fwd, _bdot_nt_bwd)


@functools.partial(jax.custom_vjp, nondiff_argnums=(1,))
def _lane_roll(x, shift):
    return pltpu.roll(x, shift % x.shape[1], axis=1)


def _lane_roll_fwd(x, shift):
    return _lane_roll(x, shift), None


def _lane_roll_bwd(shift, _, ct):
    return (_lane_roll(ct, -shift),)


_lane_roll.defvjp(_lane_roll_fwd, _lane_roll_bwd)


@functools.partial(jax.custom_vjp, nondiff_argnums=(1, 2))
def _cols(x, lo, hi):
    return x[:, lo:hi]


def _cols_fwd(x, lo, hi):
    return x[:, lo:hi], x.shape[1]


def _cols_bwd(lo, hi, width, ct):
    parts = []
    if lo > 0:
        parts.append(jnp.zeros((ct.shape[0], lo), ct.dtype))
    parts.append(ct)
    if hi < width:
        parts.append(jnp.zeros((ct.shape[0], width - hi), ct.dtype))
    return (jnp.concatenate(parts, axis=1) if len(parts) > 1 else ct,)


_cols.defvjp(_cols_fwd, _cols_bwd)


MM_VMEM_BUDGET = 20 * 1024 * 1024


def _mm(a, b, *, name, ta=False, tb=False, out_dtype=F32, lead=None, add=None, add_scale=1.0,
        tm=1024, tn=1024, tk=2816, ride=None):
    if ta:
        K, M = a.shape
    else:
        M, K = a.shape
    bshape = b.shape[1:] if lead is not None else b.shape
    if tb:
        N, Kb = bshape
    else:
        Kb, N = bshape
    assert K == Kb, (name, a.shape, b.shape)

    def blocks(tm, tn, tk):
        tm = _tile(M, tm, LANE if ta else 16)
        tn = _tile(N, tn, LANE)
        tk = _tile(K, tk, LANE)
        nbytes = (tm * tk * a.dtype.itemsize + tk * tn * b.dtype.itemsize
                  + tm * tn * (jnp.dtype(out_dtype).itemsize + (4 if K // tk > 1 else 0)
                               + (add.dtype.itemsize if add is not None else 0)))
        return tm, tn, tk, nbytes

    tm, tn, tk, est = blocks(tm, tn, tk)
    for shrink in ("m", "k", "m", "k", "n"):
        if est <= MM_VMEM_BUDGET:
            break
        if shrink == "m":
            tm, tn, tk, est = blocks(max(tm // 2, LANE), tn, tk)
        elif shrink == "k":
            tm, tn, tk, est = blocks(tm, tn, max(tk // 2, LANE))
        else:
            tm, tn, tk, est = blocks(tm, max(tn // 2, LANE), tk)
    nk = K // tk
    ca = 0 if ta else 1
    cb = 1 if tb else 0

    def body(*refs):
        a_ref, b_ref = refs[0], refs[1]
        add_ref = refs[2] if add is not None else None
        o_ref = refs[3] if add is not None else refs[2]

        def finish(r):
            if add_ref is not None:
                r = r + add_scale * add_ref[...].astype(F32)
            o_ref[...] = r.astype(o_ref.dtype)

        if nk == 1:
            finish(_dg(a_ref[...], b_ref[...], ca, cb))
            return
        acc_ref = refs[-1]
        k = pl.program_id(2)

        @pl.when(k == 0)
        def _():
            acc_ref[...] = jnp.zeros_like(acc_ref)

        acc_ref[...] += _dg(a_ref[...], b_ref[...], ca, cb)

        @pl.when(k == nk - 1)
        def _():
            finish(acc_ref[...])

    a_blk = (tk, tm) if ta else (tm, tk)
    a_map = (lambda i, j, k: (k, i)) if ta else (lambda i, j, k: (i, k))
    b_blk = (tn, tk) if tb else (tk, tn)
    if lead is None:
        b_map = (lambda i, j, k: (j, k)) if tb else (lambda i, j, k: (k, j))
        b_spec = pl.BlockSpec(b_blk, b_map)
    else:
        b_map = (lambda i, j, k: (lead, j, k)) if tb else (lambda i, j, k: (lead, k, j))
        b_spec = pl.BlockSpec((None,) + b_blk, b_map)
    in_specs = [pl.BlockSpec(a_blk, a_map), b_spec]
    args = [a, b]
    if add is not None:
        in_specs.append(pl.BlockSpec((tm, tn), lambda i, j, k: (i, j)))
        args.append(add)
    (out,), rode = _host_call(
        body, name=name,
        grid=(M // tm, N // tn, nk),
        in_specs=in_specs,
        out_specs=[pl.BlockSpec((tm, tn), lambda i, j, k: (i, j))],
        out_shape=[_out((M, N), out_dtype)],
        scratch=[pltpu.VMEM((tm, tn), F32)] if nk > 1 else [],
        args=[_hbm(v) for v in args], sem=("parallel", "parallel", "arbitrary"), est=est + tm * tn * 4,
        ride=ride)
    return out if ride is None else (out, rode)


def _rowwise(fn, tiles, params, tile_outs, acc_outs=(), *, tm, name):
    tile_arrays, tile_specs = [], []
    for t in tiles:
        if isinstance(t, tuple):
            tile_arrays.append(t[0])
            tile_specs.append(t[1])
        else:
            tile_arrays.append(t)
            tile_specs.append(pl.BlockSpec((tm, t.shape[1]), lambda i: (i, 0)))
    T = tile_arrays[0].shape[0]
    nt, np_, nto, nao = len(tile_arrays), len(params), len(tile_outs), len(acc_outs)

    def body(*refs):
        i = pl.program_id(0)
        tvals = [r[...] for r in refs[:nt]]
        pvals = [r[...] for r in refs[nt:nt + np_]]
        to_refs = refs[nt + np_:nt + np_ + nto]
        ao_refs = refs[nt + np_ + nto:]
        touts, aouts = fn(i, tvals, pvals)
        for r, v in zip(to_refs, touts):
            r[...] = v.astype(r.dtype)
        if nao:
            @pl.when(i == 0)
            def _():
                for r in ao_refs:
                    r[...] = jnp.zeros_like(r)
            for r, v in zip(ao_refs, aouts):
                r[...] += v.astype(r.dtype)

    in_specs = tile_specs + [pl.BlockSpec(p.shape, lambda i: (0, 0)) for p in params]
    out_specs = [pl.BlockSpec((tm, c), lambda i: (i, 0)) for c, _ in tile_outs]
    out_specs += [pl.BlockSpec(s, lambda i: (0, 0)) for s, _ in acc_outs]
    out_shape = [_out((T, c), d) for c, d in tile_outs]
    out_shape += [_out(s, d) for s, d in acc_outs]
    width = sum(s.block_shape[-1] for s in tile_specs) + sum(c for c, _ in tile_outs)
    est = 6 * tm * width * 4 + sum(_nbytes(p.shape, F32) for p in params) * 4
    return pl.pallas_call(
        body, name=name, grid=(T // tm,),
        in_specs=in_specs, out_specs=out_specs, out_shape=out_shape,
        compiler_params=_params(("arbitrary",) if nao else ("parallel",), est),
    )(*[_hbm(v) for v in tile_arrays], *[_hbm(p) for p in params])


def _ln_fn(alpha, s, xres, y, g, b):
    z = alpha * xres.astype(F32) + s * y.astype(F32)
    mu = jnp.mean(z, axis=-1, keepdims=True)
    zc = z - mu
    var = jnp.mean(zc * zc, axis=-1, keepdims=True)
    return zc * lax.rsqrt(var + LN_EPS) * g + b


def _mm_ln(a, b, lead, xres, g, bias, *, alpha, s, name):
    M, K = a.shape
    N = b.shape[2]
    tm = _tile(M, 256, 16)

    def body(a_ref, b_ref, x_ref, g_ref, bias_ref, y_ref, xo_ref, xb_ref):
        y = _dg(a_ref[...], b_ref[...], 1, 0)
        y_ref[...] = y.astype(y_ref.dtype)
        out = _ln_fn(alpha, s, x_ref[...], y, g_ref[...], bias_ref[...])
        xo_ref[...] = out
        xb_ref[...] = out.astype(BF16)

    row = pl.BlockSpec((tm, N), lambda i: (i, 0))
    vec = pl.BlockSpec((1, N), lambda i: (0, 0))
    est = tm * K * 2 + K * N * 2 + tm * N * (4 + 4 + 4 + 2 + 8)
    return pl.pallas_call(
        body, name=name, grid=(M // tm,),
        in_specs=[pl.BlockSpec((tm, K), lambda i: (i, 0)), pl.BlockSpec((None, K, N), lambda i: (lead, 0, 0)),
                  row, vec, vec],
        out_specs=[row, row, row],
        out_shape=[_out((M, N), BF16), _out((M, N), F32), _out((M, N), BF16)],
        compiler_params=_params(("parallel",), est),
    )(_hbm(a), _hbm(b), _hbm(xres), _hbm(g), _hbm(bias))


def _ln_bwd(xres, y, g, dout, *, alpha, s, name):
    T, D = xres.shape
    tm = _tile(T, 256, 16)

    def body(x_ref, y_ref, d_ref, g_ref, dx_ref, dy_ref, dg_ref, db_ref):
        @pl.when(pl.program_id(0) == 0)
        def _():
            dg_ref[...] = jnp.zeros_like(dg_ref)
            db_ref[...] = jnp.zeros_like(db_ref)

        z = alpha * x_ref[...] + s * y_ref[...].astype(F32)
        zc = z - jnp.mean(z, axis=-1, keepdims=True)
        r = lax.rsqrt(jnp.mean(zc * zc, axis=-1, keepdims=True) + LN_EPS)
        xh = zc * r
        d = d_ref[...]
        dxh = d * g_ref[...]
        dz = r * (dxh - jnp.mean(dxh, axis=-1, keepdims=True) - xh * jnp.mean(dxh * xh, axis=-1, keepdims=True))
        dx_ref[...] = alpha * dz
        dy_ref[...] = (s * dz).astype(dy_ref.dtype)
        dg_ref[...] += jnp.sum(d * xh, axis=0, keepdims=True)
        db_ref[...] += jnp.sum(d, axis=0, keepdims=True)

    row = pl.BlockSpec((tm, D), lambda i: (i, 0))
    vec = pl.BlockSpec((1, D), lambda i: (0, 0))
    return pl.pallas_call(
        body, name=name, grid=(T // tm,),
        in_specs=[row, row, row, vec], out_specs=[row, row, vec, vec],
        out_shape=[_out((T, D), F32), _out((T, D), BF16),
                   _out((1, D), F32), _out((1, D), F32)],
        compiler_params=_params(("arbitrary",), 12 * tm * D * 4),
    )(_hbm(xres), _hbm(y), _hbm(dout), _hbm(g))


FFN_TILE = 256


def _interleave(w, axis):
    n = w.shape[axis] // (2 * FFN_TILE)
    shp = w.shape[:axis] + (2, n, FFN_TILE) + w.shape[axis + 1:]
    return jnp.swapaxes(w.reshape(shp), axis, axis + 1).reshape(w.shape)


def _deinterleave(w, axis):
    n = w.shape[axis] // (2 * FFN_TILE)
    shp = w.shape[:axis] + (n, 2, FFN_TILE) + w.shape[axis + 1:]
    return jnp.swapaxes(w.reshape(shp), axis, axis + 1).reshape(w.shape)


def _ffn_up(xb, w13t, lead, *, name, ride=None):
    T, D = xb.shape
    F = w13t.shape[1] // 2
    tc = FFN_TILE
    tm = _tile(T, 1024, 16)

    def body(x_ref, w_ref, h_ref, a_ref):
        h = _dg(x_ref[...], w_ref[...], 1, 1)
        g, u = h[:, :tc], h[:, tc:]
        h_ref[...] = h.astype(h_ref.dtype)
        a_ref[...] = (g * jax.nn.sigmoid(g) * u).astype(a_ref.dtype)

    est = (tm * D + 2 * tc * D + 3 * tm * tc) * 2 + 3 * tm * tc * 4
    (h13, a), gathered = _host_call(
        body, name=name, grid=(T // tm, F // tc),
        in_specs=[pl.BlockSpec((tm, D), lambda i, j: (i, 0)),
                  pl.BlockSpec((None, 2 * tc, D), lambda i, j: (lead, j, 0))],
        out_specs=[pl.BlockSpec((tm, 2 * tc), lambda i, j: (i, j)),
                   pl.BlockSpec((tm, tc), lambda i, j: (i, j))],
        out_shape=[_out((T, 2 * F), BF16), _out((T, F), BF16)],
        args=[_hbm(xb), _hbm(w13t)], sem=("parallel", "parallel"), est=est, ride=ride)
    return h13, a, gathered


def _ffn_down_bwd(dyb, w2, lead, h13, *, name, ride=None):
    T, D = dyb.shape
    F = w2.shape[1]
    tc = FFN_TILE
    tm = _tile(T, 1024, 16)

    def body(dy_ref, w_ref, h_ref, dh_ref):
        d = _dg(dy_ref[...], w_ref[...], 1, 1)
        h = h_ref[...].astype(F32)
        g, u = h[:, :tc], h[:, tc:]
        sig = jax.nn.sigmoid(g)
        gs = g * sig
        dh_ref[...] = jnp.concatenate([d * u * (sig + gs * (1.0 - sig)), d * gs], axis=1).astype(dh_ref.dtype)

    est = (tm * D + tc * D + 4 * tm * tc) * 2 + 6 * tm * tc * 4
    (dh,), rode = _host_call(
        body, name=name, grid=(T // tm, F // tc),
        in_specs=[pl.BlockSpec((tm, D), lambda i, j: (i, 0)),
                  pl.BlockSpec((None, tc, D), lambda i, j: (lead, j, 0)),
                  pl.BlockSpec((tm, 2 * tc), lambda i, j: (i, j))],
        out_specs=[pl.BlockSpec((tm, 2 * tc), lambda i, j: (i, j))],
        out_shape=[_out((T, 2 * F), BF16)],
        args=[_hbm(dyb), _hbm(w2), _hbm(h13)], sem=("parallel", "parallel"), est=est, ride=ride)
    return dh, rode


def _pool_select(parts, pw):
    pg = pw // len(POOL_WINDOWS)
    grp = lax.broadcasted_iota(jnp.int32, parts[0].shape, 1) // pg
    out = parts[3]
    for g in (2, 1, 0):
        out = jnp.where(grp == g, parts[g], out)
    return out


def _pool_count(t0, rows, pw):
    pg = pw // len(POOL_WINDOWS)
    grp = lax.broadcasted_iota(jnp.int32, (rows, pw), 1) // pg
    win = jnp.where(grp == 0, POOL_WINDOWS[0],
                    jnp.where(grp == 1, POOL_WINDOWS[1],
                              jnp.where(grp == 2, POOL_WINDOWS[2], POOL_WINDOWS[3])))
    t = t0 + lax.broadcasted_iota(jnp.int32, (rows, pw), 0)
    return jnp.minimum(t + 1, win).astype(F32)


def _window_sums(ext, up):
    n = ext.shape[0]
    sums, cur, k = [], ext, 1
    for _ in POOL_WINDOWS:
        cur = cur + pltpu.roll(cur, (n - k) if up else k, axis=0)
        sums.append(cur)
        k *= 2
    return sums


def _pool_delta(u, halo, t0):
    tm, pw = u.shape
    ext = jnp.concatenate([halo, u], axis=0)
    sums = [s[POOL_HALO:, :] for s in _window_sums(ext, up=False)]
    return _pool_select(sums, pw) / _pool_count(t0, tm, pw) - u


def _pool_fwd(hin, wbd, scale, *, name):
    T = hin.shape[0]
    pw = wbd.shape[0]
    tm = _tile(T, 256, POOL_HALO)
    per = tm // POOL_HALO

    def body(u_ref, halo_ref, w_ref, s_ref, y_ref):
        i = pl.program_id(0)
        halo = jnp.where(i > 0, halo_ref[...], 0.0)
        d = _pool_delta(u_ref[...], halo, i * tm)
        y_ref[...] = (_dg(d, w_ref[...], 1, 0) * s_ref[...]).astype(y_ref.dtype)

    return pl.pallas_call(
        body, name=name, grid=(T // tm,),
        in_specs=[pl.BlockSpec((tm, pw), lambda i: (i, 0)),
                  pl.BlockSpec((POOL_HALO, pw), lambda i: (jnp.maximum(i * per - 1, 0), 0)),
                  pl.BlockSpec((pw, pw), lambda i: (0, 0)),
                  pl.BlockSpec((1, pw), lambda i: (0, 0))],
        out_specs=pl.BlockSpec((tm, pw), lambda i: (i, 0)),
        out_shape=_out((T, pw), BF16),
        compiler_params=_params(("parallel",), 16 * tm * pw * 4),
    )(_hbm(hin), _hbm(hin), _hbm(wbd), _hbm(scale))


def _pool_bwd(hin, dcat, wbd, scale, *, name):
    T = hin.shape[0]
    pw = wbd.shape[0]
    tm = _tile(T, 256, POOL_HALO)
    per = tm // POOL_HALO
    nt = T // tm

    def body(u_ref, halo_ref, dy_ref, dyn_ref, w_ref, s_ref, du_ref, dw_ref, ds_ref):
        i = pl.program_id(0)

        @pl.when(i == 0)
        def _():
            dw_ref[...] = jnp.zeros_like(dw_ref)
            ds_ref[...] = jnp.zeros_like(ds_ref)

        halo = jnp.where(i > 0, halo_ref[...], 0.0)
        d = _pool_delta(u_ref[...], halo, i * tm)
        w = w_ref[...]
        sc = s_ref[...]
        dy = dy_ref[...]
        dyn = jnp.where(i < nt - 1, dyn_ref[...], 0.0)
        ds_ref[...] += jnp.sum(dy * _dg(d, w, 1, 0), axis=0, keepdims=True)
        dys = dy * sc
        dw_ref[...] += _dg(d, dys, 0, 0)
        dys_ext = jnp.concatenate([dys, dyn * sc], axis=0)
        dd_ext = _dg(dys_ext, w, 1, 1)
        ddp = dd_ext / _pool_count(i * tm, tm + POOL_HALO, pw)
        sums = [s[:tm, :] for s in _window_sums(ddp, up=True)]
        du_ref[...] = _pool_select(sums, pw) - dd_ext[:tm, :]

    return pl.pallas_call(
        body, name=name, grid=(nt,),
        in_specs=[pl.BlockSpec((tm, pw), lambda i: (i, 0)),
                  pl.BlockSpec((POOL_HALO, pw), lambda i: (jnp.maximum(i * per - 1, 0), 0)),
                  pl.BlockSpec((tm, pw), lambda i: (i, 0)),
                  pl.BlockSpec((POOL_HALO, pw), lambda i: (jnp.minimum((i + 1) * per, nt * per - 1), 0)),
                  pl.BlockSpec((pw, pw), lambda i: (0, 0)),
                  pl.BlockSpec((1, pw), lambda i: (0, 0))],
        out_specs=[pl.BlockSpec((tm, pw), lambda i: (i, 0)),
                   pl.BlockSpec((pw, pw), lambda i: (0, 0)),
                   pl.BlockSpec((1, pw), lambda i: (0, 0))],
        out_shape=[_out((T, pw), F32),
                   _out((pw, pw), F32),
                   _out((1, pw), F32)],
        compiler_params=_params(("arbitrary",), 24 * tm * pw * 4),
    )(_hbm(hin), _hbm(hin), _hbm(dcat), _hbm(dcat), _hbm(wbd), _hbm(scale))


def _rms(x, g):
    return x * lax.rsqrt(jnp.mean(x * x, axis=-1, keepdims=True) + RMS_EPS) * g


def _norms_fn(pw, h, gq, gkv):
    o1 = pw + Q_LORA
    o2 = o1 + KV_LORA
    return (_rms(_cols(h, pw, o1), gq), _rms(_cols(h, o1, o2), gkv), _cols(h, o2, h.shape[1]))


def _norms_fwd(hin, gq, gkv, *, pw, name):
    tm = _tile(hin.shape[0], 256, 16)

    def fn(i, tv, pv):
        return _norms_fn(pw, tv[0], pv[0], pv[1]), ()

    return _rowwise(fn, [hin], [gq, gkv], [(Q_LORA, BF16), (KV_LORA, BF16), (LANE, F32)], tm=tm, name=name)


def _norms_bwd(hin, gq, gkv, dcq, dckv, dkpe, du, *, pw, name):
    tm = _tile(hin.shape[0], 256, 16)
    dinp = hin.shape[1]

    def fn(i, tv, pv):
        _, vjp = jax.vjp(functools.partial(_norms_fn, pw), tv[0], pv[0], pv[1])
        dh, dgq, dgkv = vjp((tv[1].astype(F32), tv[2].astype(F32), tv[3].astype(F32)))
        dh = jnp.concatenate([tv[4], dh[:, pw:]], axis=1)
        return (dh,), (dgq, dgkv)

    return _rowwise(fn, [hin, dcq, dckv, dkpe, du], [gq, gkv], [(dinp, BF16)],
                    [((1, Q_LORA), F32), ((1, KV_LORA), F32)], tm=tm, name=name)


def _heads_fn(H, qraw, kv, kpe, rc, rs1, rs2):
    half = QK_ROPE // 2
    scale = (QK_NOPE + QK_ROPE) ** -0.5

    def rope(blk):
        return blk * rc + _lane_roll(blk, -half) * rs1 + _lane_roll(blk, half) * rs2

    krot = rope(kpe)
    qs, ks, vs = [], [], []
    for h in range(H):
        lo = h * HEAD_PAD
        qs += [_cols(qraw, lo, lo + LANE) * scale, rope(_cols(qraw, lo + LANE, lo + HEAD_PAD)) * scale]
        ks += [_cols(kv, lo, lo + LANE), krot]
        vs += [_cols(kv, lo + LANE, lo + HEAD_PAD)]
    return jnp.concatenate(qs, axis=1), jnp.concatenate(ks, axis=1), jnp.concatenate(vs, axis=1)


def _heads_fwd(qraw, kv, kpe, tabs, *, H, name):
    tm = _tile(qraw.shape[0], 256, 16)

    def fn(i, tv, pv):
        return _heads_fn(H, *tv), ()

    return _rowwise(fn, [qraw, kv, kpe, *tabs], [],
                    [(H * HEAD_PAD, BF16), (H * HEAD_PAD, BF16), (H * V_HEAD, BF16)], tm=tm, name=name)


def _heads_bwd(dq, dk, dv, tabs, *, H, name):
    tm = _tile(dq.shape[0], 256, 16)

    def fn(i, tv, pv):
        z = jnp.zeros((tm, H * HEAD_PAD), F32)
        zk = jnp.zeros((tm, LANE), F32)
        rc, rs1, rs2 = tv[3], tv[4], tv[5]
        _, vjp = jax.vjp(lambda a, b, c: _heads_fn(H, a, b, c, rc, rs1, rs2), z, z, zk)
        return vjp((tv[0], tv[1], tv[2])), ()

    return _rowwise(fn, [dq, dk, dv, *tabs], [],
                    [(H * HEAD_PAD, BF16), (H * HEAD_PAD, BF16), (LANE, F32)], tm=tm, name=name)


def _diag_mask(rows, cols, row0):
    r = (row0 + lax.broadcasted_iota(jnp.int32, (rows, cols), 0)) // CHUNK
    c = lax.broadcasted_iota(jnp.int32, (rows, cols), 1) // CHUNK
    return r >= c


def _flash_fwd(qh, kh, vh, *, H, name, ride=None):
    T = qh.shape[0]
    t = _tile(T, 512, CHUNK)


    def body(q_ref, k_ref, v_ref, o_ref, lse_ref):
        i = pl.program_id(1)
        q = q_ref[...]

        def blk(j, carry, masked):
            m, l, acc = carry
            rows = pl.ds(pl.multiple_of(j * t, t), t)
            s = _dg(q, k_ref[rows, :], 1, 1)
            if masked:
                s = jnp.where(_diag_mask(t, t, 0), s, NEG_INF)
            mn = jnp.maximum(m, jnp.max(s, axis=1, keepdims=True))
            p = jnp.exp(s - mn)
            corr = jnp.exp(m - mn)
            l = corr * l + jnp.sum(p, axis=1, keepdims=True)
            acc = corr * acc + _dg(p, v_ref[rows, :], 1, 0)
            return mn, l, acc

        init = (jnp.full((t, 1), NEG_INF, F32), jnp.zeros((t, 1), F32), jnp.zeros((t, V_HEAD), F32))
        carry = lax.fori_loop(0, i, lambda j, c: blk(j, c, False), init)
        m, l, acc = blk(i, carry, True)
        o_ref[...] = (acc / l).astype(o_ref.dtype)
        lse_ref[...] = jnp.broadcast_to(m + jnp.log(l), (t, V_HEAD))

    est = 2 * T * (HEAD_PAD + V_HEAD) * 2 + 8 * t * t * 4
    (o, lse), gathered = _host_call(
        body, name=name, grid=(H, T // t),
        in_specs=[pl.BlockSpec((t, HEAD_PAD), lambda h, i: (i, h)),
                  pl.BlockSpec((T, HEAD_PAD), lambda h, i: (0, h)),
                  pl.BlockSpec((T, V_HEAD), lambda h, i: (0, h))],
        out_specs=[pl.BlockSpec((t, V_HEAD), lambda h, i: (i, h)),
                   pl.BlockSpec((t, V_HEAD), lambda h, i: (i, h))],
        out_shape=[_out((T, H * V_HEAD), BF16),
                   _out((T, H * V_HEAD), F32)],
        args=[_hbm(qh), _hbm(kh), _hbm(vh)], sem=("parallel", "parallel"), est=est, ride=ride)
    return o, lse, gathered


def _flash_bwd(qh, kh, vh, cat, dcat, lse, *, H, pw, name, ride=None):
    T = qh.shape[0]
    t = _tile(T, 512, CHUNK)
    nb = T // t
    off = pw // V_HEAD

    def body(q_ref, k_ref, v_ref, o_ref, do_ref, lse_ref, dq_ref, dk_ref, dv_ref):
        j = pl.program_id(1)

        @pl.when(j == 0)
        def _():
            dq_ref[...] = jnp.zeros_like(dq_ref)

        kj = k_ref[...]
        vj = v_ref[...]

        def blk(i, carry, masked):
            dk, dv = carry
            rows = pl.ds(pl.multiple_of(i * t, t), t)
            qi = q_ref[rows, :]
            doi = do_ref[rows, :]
            oi = o_ref[rows, :].astype(F32)
            lsei = lse_ref[rows, :][:, :1]
            s = _dg(qi, kj, 1, 1)
            if masked:
                s = jnp.where(_diag_mask(t, t, 0), s, NEG_INF)
            p = jnp.exp(s - lsei)
            dv = dv + _dg(p, doi, 0, 0)
            dp = _dg(doi, vj, 1, 1)
            di = jnp.sum(doi * oi, axis=1, keepdims=True)
            ds = p * (dp - di)
            dk = dk + _dg(ds, qi, 0, 0)
            dq_ref[rows, :] += _dg(ds, kj, 1, 0)
            return dk, dv

        carry = blk(j, (jnp.zeros((t, HEAD_PAD), F32), jnp.zeros((t, V_HEAD), F32)), True)
        dk, dv = lax.fori_loop(j + 1, nb, lambda i, c: blk(i, c, False), carry)
        dk_ref[...] = dk
        dv_ref[...] = dv

    est = T * (HEAD_PAD * 2 + V_HEAD * 2 + V_HEAD * 4 + V_HEAD * 4 + HEAD_PAD * 4) + 10 * t * t * 4
    (dq, dk, dv), gathered = _host_call(
        body, name=name, grid=(H, nb),
        in_specs=[pl.BlockSpec((T, HEAD_PAD), lambda h, j: (0, h)),
                  pl.BlockSpec((t, HEAD_PAD), lambda h, j: (j, h)),
                  pl.BlockSpec((t, V_HEAD), lambda h, j: (j, h)),
                  pl.BlockSpec((T, V_HEAD), lambda h, j: (0, off + h)),
                  pl.BlockSpec((T, V_HEAD), lambda h, j: (0, off + h)),
                  pl.BlockSpec((T, V_HEAD), lambda h, j: (0, h))],
        out_specs=[pl.BlockSpec((T, HEAD_PAD), lambda h, j: (0, h)),
                   pl.BlockSpec((t, HEAD_PAD), lambda h, j: (j, h)),
                   pl.BlockSpec((t, V_HEAD), lambda h, j: (j, h))],
        out_shape=[_out((T, H * HEAD_PAD), F32),
                   _out((T, H * HEAD_PAD), F32),
                   _out((T, H * V_HEAD), F32)],
        args=[_hbm(v) for v in (qh, kh, vh, cat, dcat, lse)], sem=("arbitrary", "arbitrary"), est=est,
        ride=ride)
    return dq, dk, dv, gathered


def _mem_fn(q, k, v):
    hd = q.shape[1] // MEM_HEADS
    outs = []
    for h in range(MEM_HEADS):
        lo, hi = h * hd, (h + 1) * hd
        s = _bdot_nt(_cols(q, lo, hi), _cols(k, lo, hi)) * hd ** -0.5
        e = jnp.exp(s - lax.stop_gradient(jnp.max(s, axis=1, keepdims=True)))
        p = e / jnp.sum(e, axis=1, keepdims=True)
        outs.append(_bdot_nn(p, _cols(v, lo, hi)))
    return jnp.concatenate(outs, axis=1)


def _mem_fwd(q, k, v, *, name):
    T, D = q.shape
    tm = _tile(T, 256, 16)

    def fn(i, tv, pv):
        return (_mem_fn(tv[0], pv[0], pv[1]),), ()

    return _rowwise(fn, [q], [k, v], [(D, BF16)], tm=tm, name=name)[0]


def _mem_bwd(q, k, v, do, *, name):
    T, D = q.shape
    tm = _tile(T, 256, 16)

    def fn(i, tv, pv):
        _, vjp = jax.vjp(_mem_fn, tv[0], pv[0], pv[1])
        dq, dk, dv = vjp(tv[1].astype(F32))
        return (dq,), (dk, dv)

    return _rowwise(fn, [q, do], [k, v], [(D, BF16)], [(k.shape, F32), (v.shape, F32)], tm=tm, name=name)


def _loss_head(y, target, *, name):
    T, D = y.shape
    tm = _tile(T, 256, 16)

    def fn(i, tv, pv):
        err = tv[0] - tv[1]
        part = 0.5 * jnp.sum(jnp.sum(err * err, axis=1, keepdims=True) / D, axis=0, keepdims=True)
        return (err / D,), (jnp.broadcast_to(part, (8, LANE)),)

    return _rowwise(fn, [y, target], [], [(D, F32)], [((8, LANE), F32)], tm=tm, name=name)


def _adamw(w, g, m, v, *, name):
    shape = w.shape
    if w.ndim != 3:
        lead3 = (1, math.prod(shape[:-1]), shape[-1])
        return [o.reshape(shape) for o in _adamw(*[a.reshape(lead3) for a in (w, g, m, v)], name=name)]
    Lw, R, C = shape
    tr = _tile(R, 512, 8)
    b1c = 1.0 - ADAM_B1 ** ADAM_STEP
    b2c = 1.0 - ADAM_B2 ** ADAM_STEP

    def body(w_ref, g_ref, m_ref, v_ref, d_ref, mo_ref, vo_ref):
        gg = g_ref[...]
        mn = ADAM_B1 * m_ref[...] + (1.0 - ADAM_B1) * gg
        vn = ADAM_B2 * v_ref[...] + (1.0 - ADAM_B2) * (gg * gg)
        d_ref[...] = -ADAM_LR * ((mn / b1c) / (jnp.sqrt(vn / b2c) + ADAM_EPS) + ADAM_WD * w_ref[...])
        mo_ref[...] = mn
        vo_ref[...] = vn

    spec = pl.BlockSpec((None, tr, C), lambda l, i: (l, i, 0))
    return pl.pallas_call(
        body, name=name, grid=(Lw, R // tr),
        in_specs=[spec] * 4, out_specs=[spec] * 3,
        out_shape=[_out(shape, F32)] * 3,
        compiler_params=_params(("parallel", "parallel"), 7 * tr * C * 4),
    )(*[_hbm(a) for a in (w, g, m, v)])


def _pair_sum(core, gs, landed, offs, *, name):
    n = len(gs)
    _, R, C = landed.shape
    rows = [g.shape[0] // N_DEV for g in gs]

    def body(core_ref, *refs):
        g_refs, l_ref, o_ref = refs[:n], refs[n], refs[n + 1]
        for g_ref, off, r in zip(g_refs, offs, rows):
            o_ref[off:off + r, :] = (g_ref[...].astype(F32) + l_ref[off:off + r, :].astype(F32)).astype(o_ref.dtype)

    slab = pl.BlockSpec((None, R, C), lambda p, core_ref: (p, 0, 0))
    own = [pl.BlockSpec((r, C), lambda p, core_ref: (2 * p + core_ref[0], 0)) for r in rows]
    return pl.pallas_call(
        body, name=name,
        grid_spec=pltpu.PrefetchScalarGridSpec(
            num_scalar_prefetch=1, grid=(4,), in_specs=own + [slab], out_specs=slab),
        out_shape=_out(landed.shape, landed.dtype),
        input_output_aliases={n + 1: 0},
        compiler_params=_params(("arbitrary",), 3 * R * C * 2 + R * C * 8),
    )(core, *[_hbm(g) for g in gs], _hbm(landed))


def _quad_sum(chip, part, gathered, used, *, name):
    C = part.shape[2]
    R = used
    tr = _tile(R, 256, 16)

    def body(chip_ref, own_ref, a_ref, b_ref, c_ref, o_ref):
        o_ref[...] = ((own_ref[...].astype(F32) + a_ref[...].astype(F32)) + b_ref[...].astype(F32)) \
            + c_ref[...].astype(F32)

    def other(k):
        return pl.BlockSpec((None, tr, C), lambda i, chip_ref: (chip_ref[0] ^ k, i, 0))

    return pl.pallas_call(
        body, name=name,
        grid_spec=pltpu.PrefetchScalarGridSpec(
            num_scalar_prefetch=1, grid=(R // tr,),
            in_specs=[pl.BlockSpec((None, tr, C), lambda i, chip_ref: (chip_ref[0], i, 0)),
                      other(1), other(2), other(3)],
            out_specs=pl.BlockSpec((tr, C), lambda i, chip_ref: (i, 0))),
        out_shape=_out((R, C), F32),
        compiler_params=_params(("arbitrary",), 8 * tr * C * 4),
    )(chip, _hbm(part), _hbm(gathered), _hbm(gathered), _hbm(gathered))


def _place():
    x, y, c = lax.axis_index("x"), lax.axis_index("y"), lax.axis_index("c")
    return x, y, c


ANY = pl.BlockSpec(memory_space=pl.ANY)


class _Gather:
    def __init__(self, shards):
        self.shards = list(shards)
        self.n = len(self.shards)
        self.out_shape = [_out((s.shape[0], N_DEV * s.shape[1], s.shape[2]), s.dtype)
                          for s in self.shards]
        self.scratch = [pltpu.SemaphoreType.DMA((7 * self.n,)), pltpu.SemaphoreType.DMA((7 * self.n,)),
                        pltpu.SemaphoreType.DMA((self.n,))]
        self.operands = [_hbm(s) for s in self.shards]

    def _bind(self, refs):
        n = self.n
        ins, outs = refs[:n], refs[n:2 * n]
        send_sems, recv_sems, local_sems = refs[2 * n:]
        x, y, c = _place()
        me, sib = (x, y, c), (x, y, 1 - c)
        chips = [(1 - x, y), (x, 1 - y), (1 - x, 1 - y)]

        def rows(w, p):
            r = self.shards[w].shape[1]
            idx = 4 * p[0] + 2 * p[1] + p[2]
            return outs[w].at[:, pl.ds(pl.multiple_of(idx * r, 8), r), :]

        def copy(w, k, block, to, src=None):
            return pltpu.make_async_remote_copy(
                src_ref=rows(w, block) if src is None else src, dst_ref=rows(w, block),
                send_sem=send_sems.at[w * 7 + k], recv_sem=recv_sems.at[w * 7 + k],
                device_id=to, device_id_type=MESH)

        def mine():
            return [pltpu.make_async_copy(ins[w], rows(w, me), local_sems.at[w]) for w in range(n)]

        def first():
            out = []
            for w in range(n):
                out.append(copy(w, 0, me, sib, src=ins[w]))
                out += [copy(w, 1 + j, me, (*chip, c), src=ins[w]) for j, chip in enumerate(chips)]
            return out

        def passed():
            return [copy(w, 4 + j, (*chip, c), sib) for j, chip in enumerate(chips) for w in range(n)]

        def landed():
            return [copy(w, 1 + j, (*chip, c), me) for j, chip in enumerate(chips) for w in range(n)]

        def last():
            out = []
            for w in range(n):
                out.append(copy(w, 0, sib, me))
                out += [copy(w, 4 + j, (*chip, 1 - c), me) for j, chip in enumerate(chips)]
            return out

        return mine, first, landed, passed, last

    def start(self, refs):
        mine, first, _, _, _ = self._bind(refs)
        for cp in mine() + first():
            cp.start()

    def forward(self, refs):
        _, _, landed, passed, _ = self._bind(refs)
        for arrived, fwd in zip(landed(), passed()):
            arrived.wait_recv()
            fwd.start()

    def finish(self, refs):
        mine, first, _, passed, last = self._bind(refs)
        for cp in last():
            cp.wait_recv()
        for cp in first() + passed():
            cp.wait_send()
        for cp in mine():
            cp.wait()


class _ChipExchange:
    def __init__(self, parts, used):
        self.ncl = len(parts)
        self.used = list(used)
        self.out_shape = [_out(p.shape, p.dtype) for p in parts]
        self.scratch = [pltpu.SemaphoreType.DMA((3 * self.ncl,)), pltpu.SemaphoreType.DMA((3 * self.ncl,))]
        self.operands = [_hbm(p) for p in parts]
        self.n = self.ncl

    def _bind(self, refs):
        ncl = self.ncl
        ins, outs = refs[:ncl], refs[ncl:2 * ncl]
        send_sems, recv_sems = refs[2 * ncl:]
        x, y, c = _place()
        chips = [(1 - x, y), (x, 1 - y), (1 - x, 1 - y)]
        here = 2 * x + y

        def copies(outgoing):
            out = []
            for k in range(ncl):
                rows = pl.ds(0, self.used[k])
                for j, (cx, cy) in enumerate(chips):
                    there = 2 * cx + cy
                    src, dst = (there, here) if outgoing else (here, there)
                    out.append(pltpu.make_async_remote_copy(
                        src_ref=ins[k].at[src, rows, :], dst_ref=outs[k].at[dst, rows, :],
                        send_sem=send_sems.at[3 * k + j], recv_sem=recv_sems.at[3 * k + j],
                        device_id=(cx, cy, c), device_id_type=MESH))
            return out

        return copies

    def start(self, refs):
        for cp in self._bind(refs)(True):
            cp.start()

    def forward(self, refs):
        pass

    def finish(self, refs):
        copies = self._bind(refs)
        for cp in copies(False):
            cp.wait_recv()
        for cp in copies(True):
            cp.wait_send()


class _Both:
    def __init__(self, members):
        self.members = list(members)
        self.n = sum(m.n for m in self.members)
        self.out_shape = [s for m in self.members for s in m.out_shape]
        self.scratch = [s for m in self.members for s in m.scratch]
        self.operands = [o for m in self.members for o in m.operands]

    def split(self, arrays):
        out, a = [], 0
        for m in self.members:
            out.append(list(arrays[a:a + m.n]))
            a += m.n
        return out

    def _refs(self, refs):
        ins, outs = self.split(refs[:self.n]), self.split(refs[self.n:2 * self.n])
        scr, b = [], 2 * self.n
        for m in self.members:
            scr.append(list(refs[b:b + len(m.scratch)]))
            b += len(m.scratch)
        return [(*i, *o, *s) for i, o, s in zip(ins, outs, scr)]

    def start(self, refs):
        for m, r in zip(self.members, self._refs(refs)):
            m.start(r)

    def forward(self, refs):
        for m, r in zip(self.members, self._refs(refs)):
            m.forward(r)

    def finish(self, refs):
        for m, r in zip(self.members, self._refs(refs)):
            m.finish(r)


def _exchange_alone(ex, *, name):
    def body(*refs):
        ex.start(refs)
        ex.forward(refs)
        ex.finish(refs)

    return pl.pallas_call(
        body, name=name, in_specs=[ANY] * ex.n, out_specs=[ANY] * ex.n,
        out_shape=ex.out_shape, scratch_shapes=ex.scratch,
    )(*ex.operands)


def _host_call(body, *, name, grid, in_specs, out_specs, out_shape, args, sem, est, ride=None, scratch=()):
    scratch = list(scratch)
    if ride is None:
        outs = pl.pallas_call(body, name=name, grid=grid, in_specs=in_specs, out_specs=out_specs,
                              out_shape=out_shape, scratch_shapes=scratch,
                              compiler_params=_params(sem, est))(*args)
        return list(outs), []
    n_in, n_out, n, n_scr = len(in_specs), len(out_specs), ride.n, len(scratch)

    def full(*refs):
        ins, rin = refs[:n_in], refs[n_in:n_in + n]
        outs, rout = refs[n_in + n:n_in + n + n_out], refs[n_in + n + n_out:n_in + 2 * n + n_out]
        own = refs[n_in + 2 * n + n_out:n_in + 2 * n + n_out + n_scr]
        rrefs = (*rin, *rout, *refs[n_in + 2 * n + n_out + n_scr:])
        step, total = _ride(ride, rrefs, grid)
        body(*ins, *outs, *own)
        _ride_end(ride, rrefs, step, total)

    outs = pl.pallas_call(
        full, name=name, grid=grid,
        in_specs=list(in_specs) + [ANY] * n, out_specs=list(out_specs) + [ANY] * n,
        out_shape=list(out_shape) + ride.out_shape, scratch_shapes=scratch + ride.scratch,
        compiler_params=_params(("arbitrary",) * len(grid), est),
    )(*args, *ride.operands)
    return list(outs[:n_out]), list(outs[n_out:])


def _ride(ex, refs, grid):
    total = math.prod(grid)
    step = pl.program_id(0)
    for axis in range(1, len(grid)):
        step = step * grid[axis] + pl.program_id(axis)
    pl.when(step == 0)(lambda: ex.start(refs))
    return step, total


def _ride_end(ex, refs, step, total):
    pl.when(step == (3 * total) // 4)(lambda: ex.forward(refs))
    pl.when(step == total - 1)(lambda: ex.finish(refs))


def _class_layout(grads, classes):
    used = [0] * len(set(classes))
    offs = []
    for g, cl in zip(grads, classes):
        offs.append(used[cl])
        used[cl] += g.shape[0] // N_DEV
    return offs, used


def _rs_to_sibling(grads, classes, *, name):
    n = len(grads)
    offs, used = _class_layout(grads, classes)
    heights = used
    ncl = len(heights)
    cols = [next(g.shape[1] for g, cl in zip(grads, classes) if cl == k) for k in range(ncl)]

    def body(*refs):
        gs, land = refs[:n], refs[n:n + ncl]
        send_sems, recv_sems = refs[n + ncl:]
        x, y, c = _place()
        sib = (x, y, 1 - c)
        for p in range(4):
            for w in range(n):
                r = grads[w].shape[0] // N_DEV
                cl = classes[w]
                there = gs[w].at[pl.ds(pl.multiple_of((2 * p + 1 - c) * r, 8), r), :]
                pltpu.make_async_remote_copy(
                    src_ref=there, dst_ref=land[cl].at[p, pl.ds(offs[w], r), :],
                    send_sem=send_sems.at[cl * 4 + p], recv_sem=recv_sems.at[cl * 4 + p],
                    device_id=sib, device_id_type=MESH).start()
        for cl in range(ncl):
            for p in range(4):
                rows_used = land[cl].at[p, pl.ds(0, used[cl]), :]
                slab = pltpu.make_async_remote_copy(
                    src_ref=rows_used, dst_ref=rows_used,
                    send_sem=send_sems.at[cl * 4 + p], recv_sem=recv_sems.at[cl * 4 + p],
                    device_id=sib, device_id_type=MESH)
                slab.wait_send()
                slab.wait_recv()

    return pl.pallas_call(
        body, name=name,
        in_specs=[ANY] * n, out_specs=[ANY] * ncl,
        out_shape=[_out((4, heights[k], cols[k]), BF16) for k in range(ncl)],
        scratch_shapes=[pltpu.SemaphoreType.DMA((4 * ncl,))] * 2,
    )(*[_hbm(g) for g in grads])


def _all_reduce_small(v, *, name):
    R = v.shape[0]

    def body(v_ref, o_ref, buf, send_sems, recv_sems):
        x, y, c = _place()
        me = 4 * x + 2 * y + c
        buf[me] = v_ref[...]
        copies = []
        for k in range(1, N_DEV):
            fx, fy, fc = (k >> 2) & 1, (k >> 1) & 1, k & 1
            to = (x ^ fx, y ^ fy, c ^ fc)
            cp = pltpu.make_async_remote_copy(
                src_ref=v_ref, dst_ref=buf.at[me],
                send_sem=send_sems.at[k - 1], recv_sem=recv_sems.at[k - 1],
                device_id=to, device_id_type=MESH)
            cp.start()
            copies.append(cp)
        for k in range(1, N_DEV):
            fx, fy, fc = (k >> 2) & 1, (k >> 1) & 1, k & 1
            frm = 4 * (x ^ fx) + 2 * (y ^ fy) + (c ^ fc)
            pltpu.make_async_remote_copy(
                src_ref=v_ref, dst_ref=buf.at[frm],
                send_sem=send_sems.at[k - 1], recv_sem=recv_sems.at[k - 1],
                device_id=(x ^ fx, y ^ fy, c ^ fc), device_id_type=MESH).wait_recv()
        for cp in copies:
            cp.wait_send()
        acc = buf[0]
        for d in range(1, N_DEV):
            acc = acc + buf[d]
        o_ref[...] = acc

    vm = pl.BlockSpec(memory_space=pltpu.VMEM)
    return pl.pallas_call(
        body, name=name, in_specs=[vm], out_specs=vm,
        out_shape=jax.ShapeDtypeStruct((R, LANE), F32),
        scratch_shapes=[pltpu.VMEM((N_DEV, R, LANE), F32),
                        pltpu.SemaphoreType.DMA((N_DEV - 1,)), pltpu.SemaphoreType.DMA((N_DEV - 1,))],
        compiler_params=pltpu.CompilerParams(vmem_limit_bytes=VMEM_FLOOR),
    )(v)


def _rope_tables(positions):
    half = QK_ROPE // 2
    inv_freq = ROPE_BASE ** (-jnp.arange(half, dtype=F32) / half)
    ang = positions.astype(F32)[:, None] * inv_freq
    cos, sin = jnp.cos(ang), jnp.sin(ang)
    z = jnp.zeros_like(cos)
    z2 = jnp.zeros((positions.shape[0], LANE - QK_ROPE), F32)
    rc = jnp.concatenate([cos, cos, z2], axis=1)
    rs1 = jnp.concatenate([-sin, z, z2], axis=1)
    rs2 = jnp.concatenate([z, sin, z2], axis=1)
    return rc, rs1, rs2


def _block_diag(pool_w):
    G, pg, _ = pool_w.shape
    out = jnp.zeros((G * pg, G * pg), pool_w.dtype)
    for g in range(G):
        out = lax.dynamic_update_slice(out, pool_w[g], (g * pg, g * pg))
    return out


def kernel(x, mem, positions, ln_g, ln_b, ffn1_w13, ffn1_w2, w_in, pool_w, pool_scale, q_norm_g, w_uq, kv_norm_g, w_ukv, w_out, mem_wq, mem_wkv, mem_wo, ffn2_w13, ffn2_w2, loss_target, m_ln_g, m_ln_b, m_ffn1_w13, m_ffn1_w2, m_w_in, m_pool_w, m_pool_scale, m_q_norm_g, m_w_uq, m_kv_norm_g, m_w_ukv, m_w_out, m_mem_wq, m_mem_wkv, m_mem_wo, m_ffn2_w13, m_ffn2_w2, v_ln_g, v_ln_b, v_ffn1_w13, v_ffn1_w2, v_w_in, v_pool_w, v_pool_scale, v_q_norm_g, v_w_uq, v_kv_norm_g, v_w_ukv, v_w_out, v_mem_wq, v_mem_wkv, v_mem_wo, v_ffn2_w13, v_ffn2_w2):
    L = ln_g.shape[0]
    T, D = x.shape[1], x.shape[2]
    F = ffn1_w2.shape[1] * N_DEV
    PW = D // 4
    H = (D - PW) // V_HEAD
    DIN = w_in.shape[2]
    DINP = PW + Q_LORA + KV_LORA + LANE
    QW = QK_NOPE + QK_ROPE
    alpha = (2 * L) ** 0.25
    x2d = x.reshape(T, D)
    memb = mem.reshape(mem.shape[1], D).astype(BF16)
    target = loss_target.reshape(T, D)
    tabs = _rope_tables(positions.reshape(T))

    def shards_of(l):
        return dict(
            w13a=ffn1_w13[l].T[None].astype(BF16),
            w13b=ffn2_w13[l].T[None].astype(BF16),
            w2a=ffn1_w2[l][None].astype(BF16),
            w2b=ffn2_w2[l][None].astype(BF16),
            wsq=jnp.stack([w_out[l], mem_wq[l], mem_wo[l]]).astype(BF16),
            wkvT=mem_wkv[l].T[None].astype(BF16),
            winp=jnp.pad(w_in[l], ((0, 0), (0, DINP - DIN)))[None].astype(BF16),
            wuqT=w_uq[l].T[None].astype(BF16),
            wukvT=w_ukv[l].T[None].astype(BF16),
        )

    SMALL = ("winp", "wuqT", "wukvT")
    shards = [shards_of(l) for l in range(L)]
    W = [dict() for _ in range(L)]

    def rider(spec):
        return _Gather([shards[l][n] for l, n in spec]) if spec else None

    def arrived(spec, arrays):
        for (l, n), a in zip(spec, arrays):
            if n in ("w13a", "w13b"):
                a = _interleave(a, 1)
            elif n == "wuqT":
                a = jnp.pad(a.reshape(H, QW, Q_LORA), ((0, 0), (0, HEAD_PAD - QW), (0, 0)))
                a = a.reshape(1, H * HEAD_PAD, Q_LORA)
            W[l][n] = a

    ln_shard = jnp.concatenate([ln_g.reshape(1, 4 * L, -1), ln_b.reshape(1, 4 * L, -1)], axis=1)
    spec0 = [(0, "w13a"), (0, "w2a")]
    got = _exchange_alone(_Gather([shards[l][n] for l, n in spec0] + [ln_shard]), name="ag_first")
    arrived(spec0, got[:-1])
    lnp = jnp.moveaxis(got[-1].reshape(N_DEV, 2, L, 4, D // N_DEV), 0, 3).reshape(2, L, 4, D)
    lng, lnb = lnp[0], lnp[1]
    wbd = [_block_diag(pool_w[l]).astype(BF16) for l in range(L)]

    def ffn_fwd(l, which, xres, xb, k, spec):
        ab = "ab"[which]
        h13, a, rode = _ffn_up(xb, W[l]["w13" + ab], 0, name=f"l{l}_ffn{which}_up", ride=rider(spec))
        arrived(spec, rode)
        y, xo, xob = _mm_ln(a, W[l]["w2" + ab], 0, xres, lng[l, k:k + 1], lnb[l, k:k + 1], alpha=alpha, s=0.5,
                            name=f"l{l}_ffn{which}_y_ln{k}")
        return dict(xres=xres, xb=xb, h13=h13, a=a, y=y), xo, xob

    saved = []
    xres, xb = x2d, x2d.astype(BF16)
    for l in range(L):
        sv = {}
        more = l + 1 < L
        Wl = W[l]
        spec = [(l, n) for n in (SMALL if l == 0 else ())] + [(l, "wsq"), (l, "wkvT")]
        sv["ffn1"], x1, x1b = ffn_fwd(l, 0, xres, xb, 0, spec)
        hin = _mm(x1b, Wl["winp"], lead=0, name=f"l{l}_hin")
        pscale = pool_scale[l].reshape(1, PW)
        gq, gkv = q_norm_g[l].reshape(1, Q_LORA), kv_norm_g[l].reshape(1, KV_LORA)
        ypool = _pool_fwd(hin, wbd[l], pscale, name=f"l{l}_pool")
        cqn, ckvn, kpe = _norms_fwd(hin, gq, gkv, pw=PW, name=f"l{l}_norms")
        qraw = _mm(cqn, Wl["wuqT"], lead=0, tb=True, name=f"l{l}_qraw")
        kv = _mm(ckvn, Wl["wukvT"], lead=0, tb=True, name=f"l{l}_kv")
        qh, kh, vh = _heads_fwd(qraw, kv, kpe, tabs, H=H, name=f"l{l}_heads")
        spec = [(l, "w13b"), (l, "w2b")] + ([(l + 1, "w13a")] if more else [])
        o, lse, rode = _flash_fwd(qh, kh, vh, H=H, name=f"l{l}_flash", ride=rider(spec))
        arrived(spec, rode)
        cat = jnp.concatenate([ypool, o], axis=1)
        ymix, x2, x2b = _mm_ln(cat, Wl["wsq"], 0, x1, lng[l, 1:2], lnb[l, 1:2], alpha=alpha, s=1.0,
                               name=f"l{l}_ymix_ln1")
        qm = _mm(x2b, Wl["wsq"], lead=1, out_dtype=BF16, name=f"l{l}_qm")
        kvm = _mm(memb, Wl["wkvT"], lead=0, tb=True, name=f"l{l}_kvm")
        km, vm = kvm[:, :D], kvm[:, D:]
        om = _mem_fwd(qm, km, vm, name=f"l{l}_memattn")
        ymem, x3, x3b = _mm_ln(om, Wl["wsq"], 2, x2, lng[l, 2:3], lnb[l, 2:3], alpha=alpha, s=1.0,
                               name=f"l{l}_ymem_ln2")
        spec = [(l + 1, n) for n in ("w2a", *SMALL)] if more else []
        sv["ffn2"], x4, x4b = ffn_fwd(l, 1, x3, x3b, 3, spec)
        sv.update(x1=x1, x1b=x1b, hin=hin, pscale=pscale, gq=gq, gkv=gkv, cqn=cqn, ckvn=ckvn,
                  qh=qh, kh=kh, vh=vh, lse=lse, cat=cat, ymix=ymix, x2=x2, x2b=x2b, qm=qm, km=km, vm=vm,
                  om=om, ymem=ymem)
        saved.append(sv)
        xres, xb = x4, x4b

    dx, loss_blk = _loss_head(xres, target, name="loss_head")
    loss = lax.psum(loss_blk[0, 0], ("x", "y", "c"))

    gW = {}
    gS = {}

    def ffn_bwd(l, which, sv, dx, k, ride):
        tag = f"l{l}_ffn{which}"
        dxres, dyb, dg, db = _ln_bwd(sv["xres"], sv["y"], lng[l, k:k + 1], dx,
                                     alpha=alpha, s=0.5, name=f"l{l}_ln{k}_bwd")
        gW[("w2", which, l)] = _mm(sv["a"], dyb, ta=True, out_dtype=BF16, name=f"{tag}_dw2", tn=D)
        dh, rode = _ffn_down_bwd(dyb, W[l]["w2" + "ab"[which]], 0, sv["h13"], name=f"{tag}_dh", ride=ride)
        dw13 = _mm(dh, sv["xb"], ta=True, out_dtype=BF16, name=f"{tag}_dw13", tn=D)
        gW[("w13", which, l)] = _deinterleave(dw13, 0)
        gS[("ln_g", l, k)], gS[("ln_b", l, k)] = dg, db
        last = rs_first_level(l, "c") if (l == 0 and which == 0) else None
        dxn = _mm(dh, W[l]["w13" + "ab"[which]], lead=0, add=dxres, name=f"{tag}_dx", tn=D,
                  ride=last["ex"] if last else None)
        if last:
            dxn, got = dxn
            rs_last_level(last, got)
        return dxn, rode

    core = lax.axis_index("c").astype(jnp.int32).reshape(1)
    chip = (2 * lax.axis_index("x") + lax.axis_index("y")).astype(jnp.int32).reshape(1)
    gsh = {}

    def rs_first_level(l, group):
        keys, classes = {
            "a": ([("w13", 1, l), ("w2", 1, l), ("mem_wkv", l), ("mem_wq", l), ("mem_wo", l)], [0] * 5),
            "b": ([("w_out", l), ("w_in", l), ("w_uq", l), ("w_ukv", l)], [0, 1, 2, 3]),
            "c": ([("w13", 0, l), ("w2", 0, l)], [0, 0]),
        }[group]
        tag = f"l{l}{group}"
        garrs = []
        for key in keys:
            g = gW[key]
            if key[0] == "w_uq":
                g = g.reshape(H, HEAD_PAD, Q_LORA)[:, :QW, :].reshape(H * QW, Q_LORA)
            garrs.append(g)
        offs, used = _class_layout(garrs, classes)
        parts = list(_rs_to_sibling(garrs, classes, name=f"{tag}_rs_sibling"))
        for cl in range(len(parts)):
            mine = [w for w, c in enumerate(classes) if c == cl]
            parts[cl] = _pair_sum(core, [garrs[w] for w in mine], parts[cl], [offs[w] for w in mine],
                                  name=f"{tag}_rs_pair_sum{cl}")
        return dict(tag=tag, keys=keys, garrs=garrs, classes=classes, offs=offs, used=used, parts=parts,
                    ex=_ChipExchange(parts, used))

    def rs_last_level(st, gathered):
        sums = [_quad_sum(chip, p, a, u, name=f"{st['tag']}_rs_quad_sum{k}")
                for k, (p, a, u) in enumerate(zip(st["parts"], gathered, st["used"]))]
        for key, g, cl, off in zip(st["keys"], st["garrs"], st["classes"], st["offs"]):
            gsh[key] = sums[cl][off:off + g.shape[0] // N_DEV, :]

    above = None
    for l in reversed(range(L)):
        sv = saved[l]
        Wl = W[l]
        dx, _ = ffn_bwd(l, 1, sv["ffn2"], dx, 3, None)
        dxres, dyb, dg, db = _ln_bwd(sv["x2"], sv["ymem"], lng[l, 2:3], dx,
                                     alpha=alpha, s=1.0, name=f"l{l}_ln2_bwd")
        gS[("ln_g", l, 2)], gS[("ln_b", l, 2)] = dg, db
        dom = _mm(dyb, Wl["wsq"], lead=2, tb=True, out_dtype=BF16, name=f"l{l}_dom")
        gW[("mem_wo", l)] = _mm(sv["om"], dyb, ta=True, out_dtype=BF16, name=f"l{l}_dwo", tn=D)
        dqm, dkm, dvm = _mem_bwd(sv["qm"], sv["km"], sv["vm"], dom, name=f"l{l}_memattn_bwd")
        dx = _mm(dqm, Wl["wsq"], lead=1, tb=True, add=dxres, name=f"l{l}_dx2", tn=D)
        gW[("mem_wq", l)] = _mm(sv["x2b"], dqm, ta=True, out_dtype=BF16, name=f"l{l}_dwq", tn=D)
        dkvm = jnp.concatenate([dkm, dvm], axis=1).astype(BF16)
        gW[("mem_wkv", l)] = _mm(dkvm, memb, ta=True, out_dtype=BF16, name=f"l{l}_dwkv", tn=D)
        dxres, dyb, dg, db = _ln_bwd(sv["x1"], sv["ymix"], lng[l, 1:2], dx,
                                     alpha=alpha, s=1.0, name=f"l{l}_ln1_bwd")
        gS[("ln_g", l, 1)], gS[("ln_b", l, 1)] = dg, db
        dcat = _mm(dyb, Wl["wsq"], lead=0, tb=True, name=f"l{l}_dcat", tn=D)
        gW[("w_out", l)] = _mm(sv["cat"], dyb, ta=True, out_dtype=BF16, name=f"l{l}_dwout", tn=D)
        riding = [rs_first_level(l, "a")] + ([above] if above else [])
        both = _Both([st["ex"] for st in riding])
        dqh, dkh, dvh, rode = _flash_bwd(sv["qh"], sv["kh"], sv["vh"], sv["cat"], dcat, sv["lse"], H=H, pw=PW,
                                         name=f"l{l}_flash_bwd", ride=both)
        for st, got in zip(riding, both.split(rode)):
            rs_last_level(st, got)
        dqraw, dkv, dkpe = _heads_bwd(dqh, dkh, dvh, tabs, H=H, name=f"l{l}_heads_bwd")
        dcq = _mm(dqraw, Wl["wuqT"], lead=0, name=f"l{l}_dcq")
        gW[("w_uq", l)] = _mm(dqraw, sv["cqn"], ta=True, out_dtype=BF16, name=f"l{l}_dwuq")
        dckv = _mm(dkv, Wl["wukvT"], lead=0, name=f"l{l}_dckv")
        gW[("w_ukv", l)] = _mm(dkv, sv["ckvn"], ta=True, out_dtype=BF16, name=f"l{l}_dwukv")
        du, dwbd, dps = _pool_bwd(sv["hin"], dcat, wbd[l], sv["pscale"], name=f"l{l}_pool_bwd")
        dhin, dgq, dgkv = _norms_bwd(sv["hin"], sv["gq"], sv["gkv"], dcq, dckv, dkpe, du, pw=PW,
                                     name=f"l{l}_norms_bwd")
        pg = PW // len(POOL_WINDOWS)
        gS[("pool_w", l)] = jnp.stack([dwbd[g * pg:(g + 1) * pg, g * pg:(g + 1) * pg]
                                       for g in range(len(POOL_WINDOWS))])
        gS[("pool_scale", l)], gS[("q_norm_g", l)], gS[("kv_norm_g", l)] = dps, dgq, dgkv
        dx = _mm(dhin, Wl["winp"], lead=0, tb=True, add=dxres, name=f"l{l}_dx1", tn=D)
        gW[("w_in", l)] = _mm(sv["x1b"], dhin, ta=True, out_dtype=BF16, name=f"l{l}_dwin", tn=DINP)
        heads = rs_first_level(l, "b")
        dx, rode = ffn_bwd(l, 0, sv["ffn1"], dx, 0, heads["ex"])
        rs_last_level(heads, rode)
        above = rs_first_level(l, "c") if l > 0 else None
    grad_x = dx.reshape(1, T, D)

    small_keys = []
    for l in range(L):
        small_keys += [("pool_w", l), ("pool_scale", l), ("q_norm_g", l), ("kv_norm_g", l)]
        small_keys += [("ln_g", l, k) for k in range(4)] + [("ln_b", l, k) for k in range(4)]
    flat = jnp.concatenate([gS[k].reshape(-1) for k in small_keys])
    n_small = flat.shape[0]
    rows = -(-n_small // (8 * LANE)) * 8
    flat = jnp.pad(flat, (0, rows * LANE - n_small)).reshape(rows, LANE)
    red = _all_reduce_small(flat, name="ar_small").reshape(-1)
    gsm, pos = {}, 0
    for k in small_keys:
        size = math.prod(gS[k].shape)
        gsm[k] = red[pos:pos + size].reshape(gS[k].shape)
        pos += size

    me = 4 * lax.axis_index("x") + 2 * lax.axis_index("y") + lax.axis_index("c")
    dsh = D // N_DEV
    stack = lambda f: jnp.stack([f(l) for l in range(L)])
    g_ln_g = stack(lambda l: jnp.concatenate([gsm[("ln_g", l, k)] for k in range(4)], axis=0))
    g_ln_b = stack(lambda l: jnp.concatenate([gsm[("ln_b", l, k)] for k in range(4)], axis=0))
    grads = {
        "ln_g": lax.dynamic_slice_in_dim(g_ln_g, me * dsh, dsh, axis=2),
        "ln_b": lax.dynamic_slice_in_dim(g_ln_b, me * dsh, dsh, axis=2),
        "ffn1_w13": stack(lambda l: gsh[("w13", 0, l)].T),
        "ffn1_w2": stack(lambda l: gsh[("w2", 0, l)]),
        "w_in": stack(lambda l: gsh[("w_in", l)][:, :DIN]),
        "pool_w": stack(lambda l: gsm[("pool_w", l)]),
        "pool_scale": stack(lambda l: gsm[("pool_scale", l)].reshape(PW)),
        "q_norm_g": stack(lambda l: gsm[("q_norm_g", l)].reshape(Q_LORA)),
        "w_uq": stack(lambda l: gsh[("w_uq", l)].T),
        "kv_norm_g": stack(lambda l: gsm[("kv_norm_g", l)].reshape(KV_LORA)),
        "w_ukv": stack(lambda l: gsh[("w_ukv", l)].T),
        "w_out": stack(lambda l: gsh[("w_out", l)]),
        "mem_wq": stack(lambda l: gsh[("mem_wq", l)]),
        "mem_wkv": stack(lambda l: gsh[("mem_wkv", l)].T),
        "mem_wo": stack(lambda l: gsh[("mem_wo", l)]),
        "ffn2_w13": stack(lambda l: gsh[("w13", 1, l)].T),
        "ffn2_w2": stack(lambda l: gsh[("w2", 1, l)]),
    }

    names = ["ln_g", "ln_b", "ffn1_w13", "ffn1_w2", "w_in", "pool_w", "pool_scale", "q_norm_g", "w_uq",
             "kv_norm_g", "w_ukv", "w_out", "mem_wq", "mem_wkv", "mem_wo", "ffn2_w13", "ffn2_w2"]
    weights = dict(ln_g=ln_g, ln_b=ln_b, ffn1_w13=ffn1_w13, ffn1_w2=ffn1_w2, w_in=w_in, pool_w=pool_w,
                   pool_scale=pool_scale, q_norm_g=q_norm_g, w_uq=w_uq, kv_norm_g=kv_norm_g, w_ukv=w_ukv,
                   w_out=w_out, mem_wq=mem_wq, mem_wkv=mem_wkv, mem_wo=mem_wo, ffn2_w13=ffn2_w13,
                   ffn2_w2=ffn2_w2)
    ms = dict(ln_g=m_ln_g, ln_b=m_ln_b, ffn1_w13=m_ffn1_w13, ffn1_w2=m_ffn1_w2, w_in=m_w_in, pool_w=m_pool_w,
              pool_scale=m_pool_scale, q_norm_g=m_q_norm_g, w_uq=m_w_uq, kv_norm_g=m_kv_norm_g,
              w_ukv=m_w_ukv, w_out=m_w_out, mem_wq=m_mem_wq, mem_wkv=m_mem_wkv, mem_wo=m_mem_wo,
              ffn2_w13=m_ffn2_w13, ffn2_w2=m_ffn2_w2)
    vs = dict(ln_g=v_ln_g, ln_b=v_ln_b, ffn1_w13=v_ffn1_w13, ffn1_w2=v_ffn1_w2, w_in=v_w_in, pool_w=v_pool_w,
              pool_scale=v_pool_scale, q_norm_g=v_q_norm_g, w_uq=v_w_uq, kv_norm_g=v_kv_norm_g,
              w_ukv=v_w_ukv, w_out=v_w_out, mem_wq=v_mem_wq, mem_wkv=v_mem_wkv, mem_wo=v_mem_wo,
              ffn2_w13=v_ffn2_w13, ffn2_w2=v_ffn2_w2)
    deltas, new_m, new_v = [], [], []
    for nme in names:
        d, mn, vn = _adamw(weights[nme], grads[nme], ms[nme], vs[nme], name=f"adamw_{nme}")
        deltas.append(d)
        new_m.append(mn)
        new_v.append(vn)
    return (loss, grad_x, *[grads[nme] for nme in names], *deltas, *new_m, *new_v)
```

```python
import functools
import math

import jax
import jax.numpy as jnp
from jax import lax
from jax.experimental import pallas as pl
from jax.experimental.pallas import tpu as pltpu

F32 = jnp.float32
BF16 = jnp.bfloat16
MESH = pl.DeviceIdType.MESH

CHUNK = 64
MEM_HEADS = 4
POOL_WINDOWS = (2, 4, 8, 16)
QK_NOPE = 128
QK_ROPE = 64
V_HEAD = 128
Q_LORA = 256
KV_LORA = 128
ROPE_BASE = 10000.0
LN_EPS = 1e-5
RMS_EPS = 1e-6
NEG_INF = -1e30
ADAM_LR = 0.001
ADAM_B1 = 0.9
ADAM_B2 = 0.999
ADAM_EPS = 1e-08
ADAM_WD = 0.01
ADAM_STEP = 10

N_DEV = 8
LANE = 128
HEAD_PAD = 2 * LANE
POOL_HALO = 16
VMEM_CAP = 56 * 1024 * 1024
VMEM_FLOOR = 32 * 1024 * 1024


def _tile(n, pref, mult):
    t = (min(pref, n) // mult) * mult
    while t >= mult:
        if n % t == 0:
            return t
        t -= mult
    return n


def _params(sem, est_bytes):
    limit = int(min(max(2 * est_bytes + (8 << 20), VMEM_FLOOR), VMEM_CAP))
    return pltpu.CompilerParams(dimension_semantics=sem, vmem_limit_bytes=limit)


def _nbytes(shape, dtype):
    return math.prod(shape) * jnp.dtype(dtype).itemsize


def _hbm(x):
    return pltpu.with_memory_space_constraint(x, pltpu.HBM)


def _out(shape, dtype):
    return pltpu.HBM(tuple(shape), dtype)


def _dg(a, b, ca, cb):
    return lax.dot_general(a.astype(BF16), b.astype(BF16), (((ca,), (cb,)), ((), ())),
                           preferred_element_type=F32)


@jax.custom_vjp
def _bdot_nn(a, b):
    return _dg(a, b, 1, 0)


def _bdot_nn_fwd(a, b):
    return _dg(a, b, 1, 0), (a, b)


def _bdot_nn_bwd(res, ct):
    a, b = res
    return _dg(ct, b, 1, 1).astype(a.dtype), _dg(a, ct, 0, 0).astype(b.dtype)


_bdot_nn.defvjp(_bdot_nn_fwd, _bdot_nn_bwd)


@jax.custom_vjp
def _bdot_nt(a, b):
    return _dg(a, b, 1, 1)


def _bdot_nt_fwd(a, b):
    return _dg(a, b, 1, 1), (a, b)


def _bdot_nt_bwd(res, ct):
    a, b = res
    return _dg(ct, b, 1, 0).astype(a.dtype), _dg(ct, a, 0, 0).astype(b.dtype)


_bdot_nt.defvjp(_bdot_nt_fwd, _bdot_nt_bwd)


@functools.partial(jax.custom_vjp, nondiff_argnums=(1,))
def _lane_roll(x, shift):
    return pltpu.roll(x, shift % x.shape[1], axis=1)


def _lane_roll_fwd(x, shift):
    return _lane_roll(x, shift), None


def _lane_roll_bwd(shift, _, ct):
    return (_lane_roll(ct, -shift),)


_lane_roll.defvjp(_lane_roll_fwd, _lane_roll_bwd)


@functools.partial(jax.custom_vjp, nondiff_argnums=(1, 2))
def _cols(x, lo, hi):
    return x[:, lo:hi]


def _cols_fwd(x, lo, hi):
    return x[:, lo:hi], x.shape[1]


def _cols_bwd(lo, hi, width, ct):
    parts = []
    if lo > 0:
        parts.append(jnp.zeros((ct.shape[0], lo), ct.dtype))
    parts.append(ct)
    if hi < width:
        parts.append(jnp.zeros((ct.shape[0], width - hi), ct.dtype))
    return (jnp.concatenate(parts, axis=1) if len(parts) > 1 else ct,)


_cols.defvjp(_cols_fwd, _cols_bwd)


MM_VMEM_BUDGET = 20 * 1024 * 1024


def _mm(a, b, *, name, ta=False, tb=False, out_dtype=F32, lead=None, add=None, add_scale=1.0,
        tm=1024, tn=1024, tk=2816, ride=None, row_map=None):
    if ta:
        K, M = a.shape
    else:
        M, K = a.shape
    bshape = b.shape[1:] if lead is not None else b.shape
    if tb:
        N, Kb = bshape
    else:
        Kb, N = bshape
    assert K == Kb, (name, a.shape, b.shape)

    def blocks(tm, tn, tk):
        tm = _tile(M, tm, LANE if ta else 16)
        tn = _tile(N, tn, LANE)
        tk = _tile(K, tk, LANE)
        nbytes = (tm * tk * a.dtype.itemsize + tk * tn * b.dtype.itemsize
                  + tm * tn * (jnp.dtype(out_dtype).itemsize + (4 if K // tk > 1 else 0)
                               + (add.dtype.itemsize if add is not None else 0)))
        return tm, tn, tk, nbytes

    tm, tn, tk, est = blocks(tm, tn, tk)
    for shrink in ("m", "k", "m", "k", "n"):
        if est <= MM_VMEM_BUDGET:
            break
        if shrink == "m":
            tm, tn, tk, est = blocks(max(tm // 2, LANE), tn, tk)
        elif shrink == "k":
            tm, tn, tk, est = blocks(tm, tn, max(tk // 2, LANE))
        else:
            tm, tn, tk, est = blocks(tm, max(tn // 2, LANE), tk)
    nk = K // tk
    ca = 0 if ta else 1
    cb = 1 if tb else 0

    def body(*refs):
        a_ref, b_ref = refs[0], refs[1]
        add_ref = refs[2] if add is not None else None
        o_ref = refs[3] if add is not None else refs[2]

        def finish(r):
            if add_ref is not None:
                r = r + add_scale * add_ref[...].astype(F32)
            o_ref[...] = r.astype(o_ref.dtype)

        if nk == 1:
            finish(_dg(a_ref[...], b_ref[...], ca, cb))
            return
        acc_ref = refs[-1]
        k = pl.program_id(2)

        @pl.when(k == 0)
        def _():
            acc_ref[...] = jnp.zeros_like(acc_ref)

        acc_ref[...] += _dg(a_ref[...], b_ref[...], ca, cb)

        @pl.when(k == nk - 1)
        def _():
            finish(acc_ref[...])

    a_blk = (tk, tm) if ta else (tm, tk)
    a_map = (lambda i, j, k: (k, i)) if ta else (lambda i, j, k: (i, k))
    b_blk = (tn, tk) if tb else (tk, tn)
    if lead is None:
        b_map = (lambda i, j, k: (j, k)) if tb else (lambda i, j, k: (k, j))
        b_spec = pl.BlockSpec(b_blk, b_map)
    else:
        b_map = (lambda i, j, k: (lead, j, k)) if tb else (lambda i, j, k: (lead, k, j))
        b_spec = pl.BlockSpec((None,) + b_blk, b_map)
    in_specs = [pl.BlockSpec(a_blk, a_map), b_spec]
    args = [a, b]
    if add is not None:
        in_specs.append(pl.BlockSpec((tm, tn), lambda i, j, k: (i, j)))
        args.append(add)
    if row_map is not None:
        assert tm == row_map[0], (name, tm)
    out_row = row_map[1] if row_map is not None else (lambda i: i)
    (out,), rode = _host_call(
        body, name=name,
        grid=(M // tm, N // tn, nk),
        in_specs=in_specs,
        out_specs=[pl.BlockSpec((tm, tn), lambda i, j, k: (out_row(i), j))],
        out_shape=[_out((M, N), out_dtype)],
        scratch=[pltpu.VMEM((tm, tn), F32)] if nk > 1 else [],
        args=[_hbm(v) for v in args], sem=("parallel", "parallel", "arbitrary"), est=est + tm * tn * 4,
        ride=ride)
    return out if ride is None else (out, rode)


def _rowwise(fn, tiles, params, tile_outs, acc_outs=(), *, tm, name):
    tile_arrays, tile_specs = [], []
    for t in tiles:
        if isinstance(t, tuple):
            tile_arrays.append(t[0])
            tile_specs.append(t[1])
        else:
            tile_arrays.append(t)
            tile_specs.append(pl.BlockSpec((tm, t.shape[1]), lambda i: (i, 0)))
    T = tile_arrays[0].shape[0]
    nt, np_, nto, nao = len(tile_arrays), len(params), len(tile_outs), len(acc_outs)

    def body(*refs):
        i = pl.program_id(0)
        tvals = [r[...] for r in refs[:nt]]
        pvals = [r[...] for r in refs[nt:nt + np_]]
        to_refs = refs[nt + np_:nt + np_ + nto]
        ao_refs = refs[nt + np_ + nto:]
        touts, aouts = fn(i, tvals, pvals)
        for r, v in zip(to_refs, touts):
            r[...] = v.astype(r.dtype)
        if nao:
            @pl.when(i == 0)
            def _():
                for r in ao_refs:
                    r[...] = jnp.zeros_like(r)
            for r, v in zip(ao_refs, aouts):
                r[...] += v.astype(r.dtype)

    in_specs = tile_specs + [pl.BlockSpec(p.shape, lambda i: (0, 0)) for p in params]
    out_specs = [pl.BlockSpec((tm, c), lambda i: (i, 0)) for c, _ in tile_outs]
    out_specs += [pl.BlockSpec(s, lambda i: (0, 0)) for s, _ in acc_outs]
    out_shape = [_out((T, c), d) for c, d in tile_outs]
    out_shape += [_out(s, d) for s, d in acc_outs]
    width = sum(s.block_shape[-1] for s in tile_specs) + sum(c for c, _ in tile_outs)
    est = 6 * tm * width * 4 + sum(_nbytes(p.shape, F32) for p in params) * 4
    return pl.pallas_call(
        body, name=name, grid=(T // tm,),
        in_specs=in_specs, out_specs=out_specs, out_shape=out_shape,
        compiler_params=_params(("arbitrary",) if nao else ("parallel",), est),
    )(*[_hbm(v) for v in tile_arrays], *[_hbm(p) for p in params])


def _ln_fn(alpha, s, xres, y, g, b):
    z = alpha * xres.astype(F32) + s * y.astype(F32)
    mu = jnp.mean(z, axis=-1, keepdims=True)
    zc = z - mu
    var = jnp.mean(zc * zc, axis=-1, keepdims=True)
    return zc * lax.rsqrt(var + LN_EPS) * g + b


def _mm_ln(a, b, lead, xres, g, bias, *, alpha, s, name):
    M, K = a.shape
    N = b.shape[2]
    tm = _tile(M, 256, 16)

    def body(a_ref, b_ref, x_ref, g_ref, bias_ref, y_ref, xo_ref, xb_ref):
        y = _dg(a_ref[...], b_ref[...], 1, 0)
        y_ref[...] = y.astype(y_ref.dtype)
        out = _ln_fn(alpha, s, x_ref[...], y, g_ref[...], bias_ref[...])
        xo_ref[...] = out
        xb_ref[...] = out.astype(BF16)

    row = pl.BlockSpec((tm, N), lambda i: (i, 0))
    vec = pl.BlockSpec((1, N), lambda i: (0, 0))
    est = tm * K * 2 + K * N * 2 + tm * N * (4 + 4 + 4 + 2 + 8)
    return pl.pallas_call(
        body, name=name, grid=(M // tm,),
        in_specs=[pl.BlockSpec((tm, K), lambda i: (i, 0)), pl.BlockSpec((None, K, N), lambda i: (lead, 0, 0)),
                  row, vec, vec],
        out_specs=[row, row, row],
        out_shape=[_out((M, N), BF16), _out((M, N), F32), _out((M, N), BF16)],
        compiler_params=_params(("parallel",), est),
    )(_hbm(a), _hbm(b), _hbm(xres), _hbm(g), _hbm(bias))


def _ln_bwd(xres, y, g, dout, *, alpha, s, name):
    T, D = xres.shape
    tm = _tile(T, 256, 16)

    def body(x_ref, y_ref, d_ref, g_ref, dx_ref, dy_ref, dg_ref, db_ref):
        @pl.when(pl.program_id(0) == 0)
        def _():
            dg_ref[...] = jnp.zeros_like(dg_ref)
            db_ref[...] = jnp.zeros_like(db_ref)

        z = alpha * x_ref[...] + s * y_ref[...].astype(F32)
        zc = z - jnp.mean(z, axis=-1, keepdims=True)
        r = lax.rsqrt(jnp.mean(zc * zc, axis=-1, keepdims=True) + LN_EPS)
        xh = zc * r
        d = d_ref[...]
        dxh = d * g_ref[...]
        dz = r * (dxh - jnp.mean(dxh, axis=-1, keepdims=True) - xh * jnp.mean(dxh * xh, axis=-1, keepdims=True))
        dx_ref[...] = alpha * dz
        dy_ref[...] = (s * dz).astype(dy_ref.dtype)
        dg_ref[...] += jnp.sum(d * xh, axis=0, keepdims=True)
        db_ref[...] += jnp.sum(d, axis=0, keepdims=True)

    row = pl.BlockSpec((tm, D), lambda i: (i, 0))
    vec = pl.BlockSpec((1, D), lambda i: (0, 0))
    return pl.pallas_call(
        body, name=name, grid=(T // tm,),
        in_specs=[row, row, row, vec], out_specs=[row, row, vec, vec],
        out_shape=[_out((T, D), F32), _out((T, D), BF16),
                   _out((1, D), F32), _out((1, D), F32)],
        compiler_params=_params(("arbitrary",), 12 * tm * D * 4),
    )(_hbm(xres), _hbm(y), _hbm(dout), _hbm(g))


FFN_TILE = 256


def _interleave(w, axis):
    n = w.shape[axis] // (2 * FFN_TILE)
    shp = w.shape[:axis] + (2, n, FFN_TILE) + w.shape[axis + 1:]
    return jnp.swapaxes(w.reshape(shp), axis, axis + 1).reshape(w.shape)


def _deinterleave(w, axis):
    n = w.shape[axis] // (2 * FFN_TILE)
    shp = w.shape[:axis] + (n, 2, FFN_TILE) + w.shape[axis + 1:]
    return jnp.swapaxes(w.reshape(shp), axis, axis + 1).reshape(w.shape)


def _ffn_up(xb, w13t, lead, *, name, ride=None):
    T, D = xb.shape
    F = w13t.shape[1] // 2
    tc = FFN_TILE
    tm = _tile(T, 1024, 16)

    def body(x_ref, w_ref, h_ref, a_ref):
        h = _dg(x_ref[...], w_ref[...], 1, 1)
        g, u = h[:, :tc], h[:, tc:]
        h_ref[...] = h.astype(h_ref.dtype)
        a_ref[...] = (g * jax.nn.sigmoid(g) * u).astype(a_ref.dtype)

    est = (tm * D + 2 * tc * D + 3 * tm * tc) * 2 + 3 * tm * tc * 4
    (h13, a), gathered = _host_call(
        body, name=name, grid=(T // tm, F // tc),
        in_specs=[pl.BlockSpec((tm, D), lambda i, j: (i, 0)),
                  pl.BlockSpec((None, 2 * tc, D), lambda i, j: (lead, j, 0))],
        out_specs=[pl.BlockSpec((tm, 2 * tc), lambda i, j: (i, j)),
                   pl.BlockSpec((tm, tc), lambda i, j: (i, j))],
        out_shape=[_out((T, 2 * F), BF16), _out((T, F), BF16)],
        args=[_hbm(xb), _hbm(w13t)], sem=("parallel", "parallel"), est=est, ride=ride)
    return h13, a, gathered


def _ffn_down_bwd(dyb, w2, lead, h13, *, name, ride=None):
    T, D = dyb.shape
    F = w2.shape[1]
    tc = FFN_TILE
    tm = _tile(T, 1024, 16)

    def body(dy_ref, w_ref, h_ref, dh_ref):
        d = _dg(dy_ref[...], w_ref[...], 1, 1)
        h = h_ref[...].astype(F32)
        g, u = h[:, :tc], h[:, tc:]
        sig = jax.nn.sigmoid(g)
        gs = g * sig
        dh_ref[...] = jnp.concatenate([d * u * (sig + gs * (1.0 - sig)), d * gs], axis=1).astype(dh_ref.dtype)

    est = (tm * D + tc * D + 4 * tm * tc) * 2 + 6 * tm * tc * 4
    (dh,), rode = _host_call(
        body, name=name, grid=(T // tm, F // tc),
        in_specs=[pl.BlockSpec((tm, D), lambda i, j: (i, 0)),
                  pl.BlockSpec((None, tc, D), lambda i, j: (lead, j, 0)),
                  pl.BlockSpec((tm, 2 * tc), lambda i, j: (i, j))],
        out_specs=[pl.BlockSpec((tm, 2 * tc), lambda i, j: (i, j))],
        out_shape=[_out((T, 2 * F), BF16)],
        args=[_hbm(dyb), _hbm(w2), _hbm(h13)], sem=("parallel", "parallel"), est=est, ride=ride)
    return dh, rode


def _pool_select(parts, pw):
    pg = pw // len(POOL_WINDOWS)
    grp = lax.broadcasted_iota(jnp.int32, parts[0].shape, 1) // pg
    out = parts[3]
    for g in (2, 1, 0):
        out = jnp.where(grp == g, parts[g], out)
    return out


def _pool_count(t0, rows, pw):
    pg = pw // len(POOL_WINDOWS)
    grp = lax.broadcasted_iota(jnp.int32, (rows, pw), 1) // pg
    win = jnp.where(grp == 0, POOL_WINDOWS[0],
                    jnp.where(grp == 1, POOL_WINDOWS[1],
                              jnp.where(grp == 2, POOL_WINDOWS[2], POOL_WINDOWS[3])))
    t = t0 + lax.broadcasted_iota(jnp.int32, (rows, pw), 0)
    return jnp.minimum(t + 1, win).astype(F32)


def _window_sums(ext, up):
    n = ext.shape[0]
    sums, cur, k = [], ext, 1
    for _ in POOL_WINDOWS:
        cur = cur + pltpu.roll(cur, (n - k) if up else k, axis=0)
        sums.append(cur)
        k *= 2
    return sums


def _pool_delta(u, halo, t0):
    tm, pw = u.shape
    ext = jnp.concatenate([halo, u], axis=0)
    sums = [s[POOL_HALO:, :] for s in _window_sums(ext, up=False)]
    return _pool_select(sums, pw) / _pool_count(t0, tm, pw) - u


def _pool_fwd(hin, wbd, scale, cat, *, name):
    T = hin.shape[0]
    pw = wbd.shape[0]
    tm = _tile(T, 256, POOL_HALO)
    per = tm // POOL_HALO

    def body(u_ref, halo_ref, w_ref, s_ref, cat_ref, y_ref):
        i = pl.program_id(0)
        halo = jnp.where(i > 0, halo_ref[...], 0.0)
        d = _pool_delta(u_ref[...], halo, i * tm)
        y_ref[...] = (_dg(d, w_ref[...], 1, 0) * s_ref[...]).astype(y_ref.dtype)

    return pl.pallas_call(
        body, name=name, grid=(T // tm,),
        in_specs=[pl.BlockSpec((tm, pw), lambda i: (i, 0)),
                  pl.BlockSpec((POOL_HALO, pw), lambda i: (jnp.maximum(i * per - 1, 0), 0)),
                  pl.BlockSpec((pw, pw), lambda i: (0, 0)),
                  pl.BlockSpec((1, pw), lambda i: (0, 0)),
                  ANY],
        out_specs=pl.BlockSpec((tm, pw), lambda i: (i, 0)),
        out_shape=_out(cat.shape, cat.dtype),
        input_output_aliases={4: 0},
        compiler_params=_params(("parallel",), 16 * tm * pw * 4),
    )(_hbm(hin), _hbm(hin), _hbm(wbd), _hbm(scale), _hbm(cat))


def _pool_bwd(hin, dcat, wbd, scale, *, name):
    T = hin.shape[0]
    pw = wbd.shape[0]
    tm = _tile(T, 256, POOL_HALO)
    per = tm // POOL_HALO
    nt = T // tm

    def body(u_ref, halo_ref, dy_ref, dyn_ref, w_ref, s_ref, du_ref, dw_ref, ds_ref):
        i = pl.program_id(0)

        @pl.when(i == 0)
        def _():
            dw_ref[...] = jnp.zeros_like(dw_ref)
            ds_ref[...] = jnp.zeros_like(ds_ref)

        halo = jnp.where(i > 0, halo_ref[...], 0.0)
        d = _pool_delta(u_ref[...], halo, i * tm)
        w = w_ref[...]
        sc = s_ref[...]
        dy = dy_ref[...]
        dyn = jnp.where(i < nt - 1, dyn_ref[...], 0.0)
        ds_ref[...] += jnp.sum(dy * _dg(d, w, 1, 0), axis=0, keepdims=True)
        dys = dy * sc
        dw_ref[...] += _dg(d, dys, 0, 0)
        dys_ext = jnp.concatenate([dys, dyn * sc], axis=0)
        dd_ext = _dg(dys_ext, w, 1, 1)
        ddp = dd_ext / _pool_count(i * tm, tm + POOL_HALO, pw)
        sums = [s[:tm, :] for s in _window_sums(ddp, up=True)]
        du_ref[...] = _pool_select(sums, pw) - dd_ext[:tm, :]

    return pl.pallas_call(
        body, name=name, grid=(nt,),
        in_specs=[pl.BlockSpec((tm, pw), lambda i: (i, 0)),
                  pl.BlockSpec((POOL_HALO, pw), lambda i: (jnp.maximum(i * per - 1, 0), 0)),
                  pl.BlockSpec((tm, pw), lambda i: (i, 0)),
                  pl.BlockSpec((POOL_HALO, pw), lambda i: (jnp.minimum((i + 1) * per, nt * per - 1), 0)),
                  pl.BlockSpec((pw, pw), lambda i: (0, 0)),
                  pl.BlockSpec((1, pw), lambda i: (0, 0))],
        out_specs=[pl.BlockSpec((tm, pw), lambda i: (i, 0)),
                   pl.BlockSpec((pw, pw), lambda i: (0, 0)),
                   pl.BlockSpec((1, pw), lambda i: (0, 0))],
        out_shape=[_out((T, pw), F32),
                   _out((pw, pw), F32),
                   _out((1, pw), F32)],
        compiler_params=_params(("arbitrary",), 24 * tm * pw * 4),
    )(_hbm(hin), _hbm(hin), _hbm(dcat), _hbm(dcat), _hbm(wbd), _hbm(scale))


def _rms(x, g):
    return x * lax.rsqrt(jnp.mean(x * x, axis=-1, keepdims=True) + RMS_EPS) * g


def _norms_fn(pw, h, gq, gkv):
    o1 = pw + Q_LORA
    o2 = o1 + KV_LORA
    return (_rms(_cols(h, pw, o1), gq), _rms(_cols(h, o1, o2), gkv), _cols(h, o2, h.shape[1]))


def _norms_fwd(hin, gq, gkv, *, pw, name):
    tm = _tile(hin.shape[0], 256, 16)

    def fn(i, tv, pv):
        return _norms_fn(pw, tv[0], pv[0], pv[1]), ()

    return _rowwise(fn, [hin], [gq, gkv], [(Q_LORA, BF16), (KV_LORA, BF16), (LANE, F32)], tm=tm, name=name)


def _norms_bwd(hin, gq, gkv, dcq, dckv, dkpe, du, *, pw, name):
    tm = _tile(hin.shape[0], 256, 16)
    dinp = hin.shape[1]

    def fn(i, tv, pv):
        _, vjp = jax.vjp(functools.partial(_norms_fn, pw), tv[0], pv[0], pv[1])
        dh, dgq, dgkv = vjp((tv[1].astype(F32), tv[2].astype(F32), tv[3].astype(F32)))
        dh = jnp.concatenate([tv[4], dh[:, pw:]], axis=1)
        return (dh,), (dgq, dgkv)

    return _rowwise(fn, [hin, dcq, dckv, dkpe, du], [gq, gkv], [(dinp, BF16)],
                    [((1, Q_LORA), F32), ((1, KV_LORA), F32)], tm=tm, name=name)


def _heads_fn(H, qraw, kv, kpe, rc, rs1, rs2):
    half = QK_ROPE // 2
    scale = (QK_NOPE + QK_ROPE) ** -0.5

    def rope(blk):
        return blk * rc + _lane_roll(blk, -half) * rs1 + _lane_roll(blk, half) * rs2

    krot = rope(kpe)
    qs, ks, vs = [], [], []
    for h in range(H):
        lo = h * HEAD_PAD
        qs += [_cols(qraw, lo, lo + LANE) * scale, rope(_cols(qraw, lo + LANE, lo + HEAD_PAD)) * scale]
        ks += [_cols(kv, lo, lo + LANE), krot]
        vs += [_cols(kv, lo + LANE, lo + HEAD_PAD)]
    return jnp.concatenate(qs, axis=1), jnp.concatenate(ks, axis=1), jnp.concatenate(vs, axis=1)


def _heads_fwd(qraw, kv, kpe, tabs, *, H, name):
    tm = _tile(qraw.shape[0], 256, 16)

    def fn(i, tv, pv):
        return _heads_fn(H, *tv), ()

    return _rowwise(fn, [qraw, kv, kpe, *tabs], [],
                    [(H * HEAD_PAD, BF16), (H * HEAD_PAD, BF16), (H * V_HEAD, BF16)], tm=tm, name=name)


def _heads_bwd(dq, dk, dv, tabs, *, H, name):
    tm = _tile(dq.shape[0], 256, 16)

    def fn(i, tv, pv):
        z = jnp.zeros((tm, H * HEAD_PAD), F32)
        zk = jnp.zeros((tm, LANE), F32)
        rc, rs1, rs2 = tv[3], tv[4], tv[5]
        _, vjp = jax.vjp(lambda a, b, c: _heads_fn(H, a, b, c, rc, rs1, rs2), z, z, zk)
        return vjp((tv[0], tv[1], tv[2])), ()

    return _rowwise(fn, [dq, dk, dv, *tabs], [],
                    [(H * HEAD_PAD, BF16), (H * HEAD_PAD, BF16), (LANE, F32)], tm=tm, name=name)


def _diag_mask(rows, cols, row0):
    r = (row0 + lax.broadcasted_iota(jnp.int32, (rows, cols), 0)) // CHUNK
    c = lax.broadcasted_iota(jnp.int32, (rows, cols), 1) // CHUNK
    return r >= c


def _flash_fwd(qh, kh, vh, *, H, pw, name, ride=None):
    T = qh.shape[0]
    t = _tile(T, 512, CHUNK)
    off = pw // V_HEAD


    def body(q_ref, k_ref, v_ref, o_ref, lse_ref):
        i = pl.program_id(1)
        q = q_ref[...]

        def blk(j, carry, masked):
            m, l, acc = carry
            rows = pl.ds(pl.multiple_of(j * t, t), t)
            s = _dg(q, k_ref[rows, :], 1, 1)
            if masked:
                s = jnp.where(_diag_mask(t, t, 0), s, NEG_INF)
            mn = jnp.maximum(m, jnp.max(s, axis=1, keepdims=True))
            p = jnp.exp(s - mn)
            corr = jnp.exp(m - mn)
            l = corr * l + jnp.sum(p, axis=1, keepdims=True)
            acc = corr * acc + _dg(p, v_ref[rows, :], 1, 0)
            return mn, l, acc

        init = (jnp.full((t, 1), NEG_INF, F32), jnp.zeros((t, 1), F32), jnp.zeros((t, V_HEAD), F32))
        carry = lax.fori_loop(0, i, lambda j, c: blk(j, c, False), init)
        m, l, acc = blk(i, carry, True)
        o_ref[...] = (acc / l).astype(o_ref.dtype)
        lse_ref[...] = jnp.broadcast_to(m + jnp.log(l), (t, V_HEAD))

    est = 2 * T * (HEAD_PAD + V_HEAD) * 2 + 8 * t * t * 4
    (o, lse), gathered = _host_call(
        body, name=name, grid=(H, T // t),
        in_specs=[pl.BlockSpec((t, HEAD_PAD), lambda h, i: (i, h)),
                  pl.BlockSpec((T, HEAD_PAD), lambda h, i: (0, h)),
                  pl.BlockSpec((T, V_HEAD), lambda h, i: (0, h))],
        out_specs=[pl.BlockSpec((t, V_HEAD), lambda h, i: (i, off + h)),
                   pl.BlockSpec((t, V_HEAD), lambda h, i: (i, h))],
        out_shape=[_out((T, pw + H * V_HEAD), BF16),
                   _out((T, H * V_HEAD), F32)],
        args=[_hbm(qh), _hbm(kh), _hbm(vh)], sem=("parallel", "parallel"), est=est, ride=ride)
    return o, lse, gathered


def _flash_bwd(qh, kh, vh, cat, dcat, lse, *, H, pw, name, ride=None):
    T = qh.shape[0]
    t = _tile(T, 512, CHUNK)
    nb = T // t
    off = pw // V_HEAD

    def body(q_ref, k_ref, v_ref, o_ref, do_ref, lse_ref, dq_ref, dk_ref, dv_ref):
        j = pl.program_id(1)

        @pl.when(j == 0)
        def _():
            dq_ref[...] = jnp.zeros_like(dq_ref)

        kj = k_ref[...]
        vj = v_ref[...]

        def blk(i, carry, masked):
            dk, dv = carry
            rows = pl.ds(pl.multiple_of(i * t, t), t)
            qi = q_ref[rows, :]
            doi = do_ref[rows, :]
            oi = o_ref[rows, :].astype(F32)
            lsei = lse_ref[rows, :][:, :1]
            s = _dg(qi, kj, 1, 1)
            if masked:
                s = jnp.where(_diag_mask(t, t, 0), s, NEG_INF)
            p = jnp.exp(s - lsei)
            dv = dv + _dg(p, doi, 0, 0)
            dp = _dg(doi, vj, 1, 1)
            di = jnp.sum(doi * oi, axis=1, keepdims=True)
            ds = p * (dp - di)
            dk = dk + _dg(ds, qi, 0, 0)
            dq_ref[rows, :] += _dg(ds, kj, 1, 0)
            return dk, dv

        carry = blk(j, (jnp.zeros((t, HEAD_PAD), F32), jnp.zeros((t, V_HEAD), F32)), True)
        dk, dv = lax.fori_loop(j + 1, nb, lambda i, c: blk(i, c, False), carry)
        dk_ref[...] = dk
        dv_ref[...] = dv

    est = T * (HEAD_PAD * 2 + V_HEAD * 2 + V_HEAD * 4 + V_HEAD * 4 + HEAD_PAD * 4) + 10 * t * t * 4
    (dq, dk, dv), gathered = _host_call(
        body, name=name, grid=(H, nb),
        in_specs=[pl.BlockSpec((T, HEAD_PAD), lambda h, j: (0, h)),
                  pl.BlockSpec((t, HEAD_PAD), lambda h, j: (j, h)),
                  pl.BlockSpec((t, V_HEAD), lambda h, j: (j, h)),
                  pl.BlockSpec((T, V_HEAD), lambda h, j: (0, off + h)),
                  pl.BlockSpec((T, V_HEAD), lambda h, j: (0, off + h)),
                  pl.BlockSpec((T, V_HEAD), lambda h, j: (0, h))],
        out_specs=[pl.BlockSpec((T, HEAD_PAD), lambda h, j: (0, h)),
                   pl.BlockSpec((t, HEAD_PAD), lambda h, j: (j, h)),
                   pl.BlockSpec((t, V_HEAD), lambda h, j: (j, h))],
        out_shape=[_out((T, H * HEAD_PAD), F32),
                   _out((T, H * HEAD_PAD), F32),
                   _out((T, H * V_HEAD), F32)],
        args=[_hbm(v) for v in (qh, kh, vh, cat, dcat, lse)], sem=("arbitrary", "arbitrary"), est=est,
        ride=ride)
    return dq, dk, dv, gathered


def _mem_fn(q, k, v):
    hd = q.shape[1] // MEM_HEADS
    outs = []
    for h in range(MEM_HEADS):
        lo, hi = h * hd, (h + 1) * hd
        s = _bdot_nt(_cols(q, lo, hi), _cols(k, lo, hi)) * hd ** -0.5
        e = jnp.exp(s - lax.stop_gradient(jnp.max(s, axis=1, keepdims=True)))
        p = e / jnp.sum(e, axis=1, keepdims=True)
        outs.append(_bdot_nn(p, _cols(v, lo, hi)))
    return jnp.concatenate(outs, axis=1)


def _mem_fwd(q, k, v, *, name):
    T, D = q.shape
    tm = _tile(T, 256, 16)

    def fn(i, tv, pv):
        return (_mem_fn(tv[0], pv[0], pv[1]),), ()

    return _rowwise(fn, [q], [k, v], [(D, BF16)], tm=tm, name=name)[0]


def _mem_bwd(q, k, v, do, *, name):
    T, D = q.shape
    tm = _tile(T, 256, 16)

    def fn(i, tv, pv):
        _, vjp = jax.vjp(_mem_fn, tv[0], pv[0], pv[1])
        dq, dk, dv = vjp(tv[1].astype(F32))
        return (dq,), (dk, dv)

    return _rowwise(fn, [q, do], [k, v], [(D, BF16)], [(k.shape, F32), (v.shape, F32)], tm=tm, name=name)


def _loss_head(y, target, *, name):
    T, D = y.shape
    tm = _tile(T, 256, 16)

    def fn(i, tv, pv):
        err = tv[0] - tv[1]
        part = 0.5 * jnp.sum(jnp.sum(err * err, axis=1, keepdims=True) / D, axis=0, keepdims=True)
        return (err / D,), (jnp.broadcast_to(part, (8, LANE)),)

    return _rowwise(fn, [y, target], [], [(D, F32)], [((8, LANE), F32)], tm=tm, name=name)


def _adamw(w, g, m, v, *, name):
    shape = w.shape
    if w.ndim != 3:
        lead3 = (1, math.prod(shape[:-1]), shape[-1])
        return [o.reshape(shape) for o in _adamw(*[a.reshape(lead3) for a in (w, g, m, v)], name=name)]
    Lw, R, C = shape
    tr = _tile(R, 512, 8)
    b1c = 1.0 - ADAM_B1 ** ADAM_STEP
    b2c = 1.0 - ADAM_B2 ** ADAM_STEP

    def body(w_ref, g_ref, m_ref, v_ref, d_ref, mo_ref, vo_ref):
        gg = g_ref[...]
        mn = ADAM_B1 * m_ref[...] + (1.0 - ADAM_B1) * gg
        vn = ADAM_B2 * v_ref[...] + (1.0 - ADAM_B2) * (gg * gg)
        d_ref[...] = -ADAM_LR * ((mn / b1c) / (jnp.sqrt(vn / b2c) + ADAM_EPS) + ADAM_WD * w_ref[...])
        mo_ref[...] = mn
        vo_ref[...] = vn

    spec = pl.BlockSpec((None, tr, C), lambda l, i: (l, i, 0))
    return pl.pallas_call(
        body, name=name, grid=(Lw, R // tr),
        in_specs=[spec] * 4, out_specs=[spec] * 3,
        out_shape=[_out(shape, F32)] * 3,
        compiler_params=_params(("parallel", "parallel"), 7 * tr * C * 4),
    )(*[_hbm(a) for a in (w, g, m, v)])


def _pair_sum(core, gs, landed, offs, *, name):
    n = len(gs)
    _, R, C = landed.shape
    rows = [g.shape[0] // N_DEV for g in gs]

    def body(core_ref, *refs):
        g_refs, l_ref, o_ref = refs[:n], refs[n], refs[n + 1]
        for g_ref, off, r in zip(g_refs, offs, rows):
            o_ref[off:off + r, :] = (g_ref[...].astype(F32) + l_ref[off:off + r, :].astype(F32)).astype(o_ref.dtype)

    slab = pl.BlockSpec((None, R, C), lambda p, core_ref: (p, 0, 0))
    own = [pl.BlockSpec((r, C), lambda p, core_ref: (2 * p + core_ref[0], 0)) for r in rows]
    return pl.pallas_call(
        body, name=name,
        grid_spec=pltpu.PrefetchScalarGridSpec(
            num_scalar_prefetch=1, grid=(4,), in_specs=own + [slab], out_specs=slab),
        out_shape=_out(landed.shape, landed.dtype),
        input_output_aliases={n + 1: 0},
        compiler_params=_params(("arbitrary",), 3 * R * C * 2 + R * C * 8),
    )(core, *[_hbm(g) for g in gs], _hbm(landed))


def _quad_sum(chip, part, gathered, used, *, name):
    C = part.shape[2]
    R = used
    tr = _tile(R, 256, 16)

    def body(chip_ref, own_ref, a_ref, b_ref, c_ref, o_ref):
        o_ref[...] = ((own_ref[...].astype(F32) + a_ref[...].astype(F32)) + b_ref[...].astype(F32)) \
            + c_ref[...].astype(F32)

    def other(k):
        return pl.BlockSpec((None, tr, C), lambda i, chip_ref: (chip_ref[0] ^ k, i, 0))

    return pl.pallas_call(
        body, name=name,
        grid_spec=pltpu.PrefetchScalarGridSpec(
            num_scalar_prefetch=1, grid=(R // tr,),
            in_specs=[pl.BlockSpec((None, tr, C), lambda i, chip_ref: (chip_ref[0], i, 0)),
                      other(1), other(2), other(3)],
            out_specs=pl.BlockSpec((tr, C), lambda i, chip_ref: (i, 0))),
        out_shape=_out((R, C), F32),
        compiler_params=_params(("arbitrary",), 8 * tr * C * 4),
    )(chip, _hbm(part), _hbm(gathered), _hbm(gathered), _hbm(gathered))


def _place():
    x, y, c = lax.axis_index("x"), lax.axis_index("y"), lax.axis_index("c")
    return x, y, c


ANY = pl.BlockSpec(memory_space=pl.ANY)


class _Gather:
    def __init__(self, shards):
        self.shards = list(shards)
        self.n = len(self.shards)
        self.out_shape = [_out((s.shape[0], N_DEV * s.shape[1], s.shape[2]), s.dtype)
                          for s in self.shards]
        self.scratch = [pltpu.SemaphoreType.DMA((7 * self.n,)), pltpu.SemaphoreType.DMA((7 * self.n,)),
                        pltpu.SemaphoreType.DMA((self.n,))]
        self.operands = [_hbm(s) for s in self.shards]

    def _bind(self, refs):
        n = self.n
        ins, outs = refs[:n], refs[n:2 * n]
        send_sems, recv_sems, local_sems = refs[2 * n:]
        x, y, c = _place()
        me, sib = (x, y, c), (x, y, 1 - c)
        chips = [(1 - x, y), (x, 1 - y), (1 - x, 1 - y)]

        def rows(w, p):
            r = self.shards[w].shape[1]
            idx = 4 * p[0] + 2 * p[1] + p[2]
            return outs[w].at[:, pl.ds(pl.multiple_of(idx * r, 8), r), :]

        def copy(w, k, block, to, src=None):
            return pltpu.make_async_remote_copy(
                src_ref=rows(w, block) if src is None else src, dst_ref=rows(w, block),
                send_sem=send_sems.at[w * 7 + k], recv_sem=recv_sems.at[w * 7 + k],
                device_id=to, device_id_type=MESH)

        def mine():
            return [pltpu.make_async_copy(ins[w], rows(w, me), local_sems.at[w]) for w in range(n)]

        def first():
            out = []
            for w in range(n):
                out.append(copy(w, 0, me, sib, src=ins[w]))
                out += [copy(w, 1 + j, me, (*chip, c), src=ins[w]) for j, chip in enumerate(chips)]
            return out

        def passed():
            return [copy(w, 4 + j, (*chip, c), sib) for j, chip in enumerate(chips) for w in range(n)]

        def landed():
            return [copy(w, 1 + j, (*chip, c), me) for j, chip in enumerate(chips) for w in range(n)]

        def last():
            out = []
            for w in range(n):
                out.append(copy(w, 0, sib, me))
                out += [copy(w, 4 + j, (*chip, 1 - c), me) for j, chip in enumerate(chips)]
            return out

        return mine, first, landed, passed, last

    def start(self, refs):
        mine, first, _, _, _ = self._bind(refs)
        for cp in mine() + first():
            cp.start()

    def forward(self, refs):
        _, _, landed, passed, _ = self._bind(refs)
        for arrived, fwd in zip(landed(), passed()):
            arrived.wait_recv()
            fwd.start()

    def finish(self, refs):
        mine, first, _, passed, last = self._bind(refs)
        for cp in last():
            cp.wait_recv()
        for cp in first() + passed():
            cp.wait_send()
        for cp in mine():
            cp.wait()


class _ChipExchange:
    def __init__(self, parts, used):
        self.ncl = len(parts)
        self.used = list(used)
        self.out_shape = [_out(p.shape, p.dtype) for p in parts]
        self.scratch = [pltpu.SemaphoreType.DMA((3 * self.ncl,)), pltpu.SemaphoreType.DMA((3 * self.ncl,))]
        self.operands = [_hbm(p) for p in parts]
        self.n = self.ncl

    def _bind(self, refs):
        ncl = self.ncl
        ins, outs = refs[:ncl], refs[ncl:2 * ncl]
        send_sems, recv_sems = refs[2 * ncl:]
        x, y, c = _place()
        chips = [(1 - x, y), (x, 1 - y), (1 - x, 1 - y)]
        here = 2 * x + y

        def copies(outgoing):
            out = []
            for k in range(ncl):
                rows = pl.ds(0, self.used[k])
                for j, (cx, cy) in enumerate(chips):
                    there = 2 * cx + cy
                    src, dst = (there, here) if outgoing else (here, there)
                    out.append(pltpu.make_async_remote_copy(
                        src_ref=ins[k].at[src, rows, :], dst_ref=outs[k].at[dst, rows, :],
                        send_sem=send_sems.at[3 * k + j], recv_sem=recv_sems.at[3 * k + j],
                        device_id=(cx, cy, c), device_id_type=MESH))
            return out

        return copies

    def start(self, refs):
        for cp in self._bind(refs)(True):
            cp.start()

    def forward(self, refs):
        pass

    def finish(self, refs):
        copies = self._bind(refs)
        for cp in copies(False):
            cp.wait_recv()
        for cp in copies(True):
            cp.wait_send()


class _Both:
    def __init__(self, members):
        self.members = list(members)
        self.n = sum(m.n for m in self.members)
        self.out_shape = [s for m in self.members for s in m.out_shape]
        self.scratch = [s for m in self.members for s in m.scratch]
        self.operands = [o for m in self.members for o in m.operands]

    def split(self, arrays):
        out, a = [], 0
        for m in self.members:
            out.append(list(arrays[a:a + m.n]))
            a += m.n
        return out

    def _refs(self, refs):
        ins, outs = self.split(refs[:self.n]), self.split(refs[self.n:2 * self.n])
        scr, b = [], 2 * self.n
        for m in self.members:
            scr.append(list(refs[b:b + len(m.scratch)]))
            b += len(m.scratch)
        return [(*i, *o, *s) for i, o, s in zip(ins, outs, scr)]

    def start(self, refs):
        for m, r in zip(self.members, self._refs(refs)):
            m.start(r)

    def forward(self, refs):
        for m, r in zip(self.members, self._refs(refs)):
            m.forward(r)

    def finish(self, refs):
        for m, r in zip(self.members, self._refs(refs)):
            m.finish(r)


def _exchange_alone(ex, *, name):
    def body(*refs):
        ex.start(refs)
        ex.forward(refs)
        ex.finish(refs)

    return pl.pallas_call(
        body, name=name, in_specs=[ANY] * ex.n, out_specs=[ANY] * ex.n,
        out_shape=ex.out_shape, scratch_shapes=ex.scratch,
    )(*ex.operands)


def _host_call(body, *, name, grid, in_specs, out_specs, out_shape, args, sem, est, ride=None, scratch=()):
    scratch = list(scratch)
    if ride is None:
        outs = pl.pallas_call(body, name=name, grid=grid, in_specs=in_specs, out_specs=out_specs,
                              out_shape=out_shape, scratch_shapes=scratch,
                              compiler_params=_params(sem, est))(*args)
        return list(outs), []
    n_in, n_out, n, n_scr = len(in_specs), len(out_specs), ride.n, len(scratch)

    def full(*refs):
        ins, rin = refs[:n_in], refs[n_in:n_in + n]
        outs, rout = refs[n_in + n:n_in + n + n_out], refs[n_in + n + n_out:n_in + 2 * n + n_out]
        own = refs[n_in + 2 * n + n_out:n_in + 2 * n + n_out + n_scr]
        rrefs = (*rin, *rout, *refs[n_in + 2 * n + n_out + n_scr:])
        step, total = _ride(ride, rrefs, grid)
        body(*ins, *outs, *own)
        _ride_end(ride, rrefs, step, total)

    outs = pl.pallas_call(
        full, name=name, grid=grid,
        in_specs=list(in_specs) + [ANY] * n, out_specs=list(out_specs) + [ANY] * n,
        out_shape=list(out_shape) + ride.out_shape, scratch_shapes=scratch + ride.scratch,
        compiler_params=_params(("arbitrary",) * len(grid), est),
    )(*args, *ride.operands)
    return list(outs[:n_out]), list(outs[n_out:])


def _ride(ex, refs, grid):
    total = math.prod(grid)
    step = pl.program_id(0)
    for axis in range(1, len(grid)):
        step = step * grid[axis] + pl.program_id(axis)
    pl.when(step == 0)(lambda: ex.start(refs))
    return step, total


def _ride_end(ex, refs, step, total):
    pl.when(step == (3 * total) // 4)(lambda: ex.forward(refs))
    pl.when(step == total - 1)(lambda: ex.finish(refs))


def _class_layout(grads, classes):
    used = [0] * len(set(classes))
    offs = []
    for g, cl in zip(grads, classes):
        offs.append(used[cl])
        used[cl] += g.shape[0] // N_DEV
    return offs, used


def _rs_to_sibling(grads, classes, *, name):
    n = len(grads)
    offs, used = _class_layout(grads, classes)
    heights = used
    ncl = len(heights)
    cols = [next(g.shape[1] for g, cl in zip(grads, classes) if cl == k) for k in range(ncl)]

    def body(*refs):
        gs, land = refs[:n], refs[n:n + ncl]
        send_sems, recv_sems = refs[n + ncl:]
        x, y, c = _place()
        sib = (x, y, 1 - c)
        for p in range(4):
            for w in range(n):
                r = grads[w].shape[0] // N_DEV
                cl = classes[w]
                there = gs[w].at[pl.ds(pl.multiple_of((2 * p + 1 - c) * r, 8), r), :]
                pltpu.make_async_remote_copy(
                    src_ref=there, dst_ref=land[cl].at[p, pl.ds(offs[w], r), :],
                    send_sem=send_sems.at[cl * 4 + p], recv_sem=recv_sems.at[cl * 4 + p],
                    device_id=sib, device_id_type=MESH).start()
        for cl in range(ncl):
            for p in range(4):
                rows_used = land[cl].at[p, pl.ds(0, used[cl]), :]
                slab = pltpu.make_async_remote_copy(
                    src_ref=rows_used, dst_ref=rows_used,
                    send_sem=send_sems.at[cl * 4 + p], recv_sem=recv_sems.at[cl * 4 + p],
                    device_id=sib, device_id_type=MESH)
                slab.wait_send()
                slab.wait_recv()

    return pl.pallas_call(
        body, name=name,
        in_specs=[ANY] * n, out_specs=[ANY] * ncl,
        out_shape=[_out((4, heights[k], cols[k]), BF16) for k in range(ncl)],
        scratch_shapes=[pltpu.SemaphoreType.DMA((4 * ncl,))] * 2,
    )(*[_hbm(g) for g in grads])


def _all_reduce_small(v, *, name):
    R = v.shape[0]

    def body(v_ref, o_ref, buf, send_sems, recv_sems):
        x, y, c = _place()
        me = 4 * x + 2 * y + c
        buf[me] = v_ref[...]
        copies = []
        for k in range(1, N_DEV):
            fx, fy, fc = (k >> 2) & 1, (k >> 1) & 1, k & 1
            to = (x ^ fx, y ^ fy, c ^ fc)
            cp = pltpu.make_async_remote_copy(
                src_ref=v_ref, dst_ref=buf.at[me],
                send_sem=send_sems.at[k - 1], recv_sem=recv_sems.at[k - 1],
                device_id=to, device_id_type=MESH)
            cp.start()
            copies.append(cp)
        for k in range(1, N_DEV):
            fx, fy, fc = (k >> 2) & 1, (k >> 1) & 1, k & 1
            frm = 4 * (x ^ fx) + 2 * (y ^ fy) + (c ^ fc)
            pltpu.make_async_remote_copy(
                src_ref=v_ref, dst_ref=buf.at[frm],
                send_sem=send_sems.at[k - 1], recv_sem=recv_sems.at[k - 1],
                device_id=(x ^ fx, y ^ fy, c ^ fc), device_id_type=MESH).wait_recv()
        for cp in copies:
            cp.wait_send()
        acc = buf[0]
        for d in range(1, N_DEV):
            acc = acc + buf[d]
        o_ref[...] = acc

    vm = pl.BlockSpec(memory_space=pltpu.VMEM)
    return pl.pallas_call(
        body, name=name, in_specs=[vm], out_specs=vm,
        out_shape=jax.ShapeDtypeStruct((R, LANE), F32),
        scratch_shapes=[pltpu.VMEM((N_DEV, R, LANE), F32),
                        pltpu.SemaphoreType.DMA((N_DEV - 1,)), pltpu.SemaphoreType.DMA((N_DEV - 1,))],
        compiler_params=pltpu.CompilerParams(vmem_limit_bytes=VMEM_FLOOR),
    )(v)


def _rope_tables(positions):
    half = QK_ROPE // 2
    inv_freq = ROPE_BASE ** (-jnp.arange(half, dtype=F32) / half)
    ang = positions.astype(F32)[:, None] * inv_freq
    cos, sin = jnp.cos(ang), jnp.sin(ang)
    z = jnp.zeros_like(cos)
    z2 = jnp.zeros((positions.shape[0], LANE - QK_ROPE), F32)
    rc = jnp.concatenate([cos, cos, z2], axis=1)
    rs1 = jnp.concatenate([-sin, z, z2], axis=1)
    rs2 = jnp.concatenate([z, sin, z2], axis=1)
    return rc, rs1, rs2


def _block_diag(pool_w):
    G, pg, _ = pool_w.shape
    out = jnp.zeros((G * pg, G * pg), pool_w.dtype)
    for g in range(G):
        out = lax.dynamic_update_slice(out, pool_w[g], (g * pg, g * pg))
    return out


def kernel(x, mem, positions, ln_g, ln_b, ffn1_w13, ffn1_w2, w_in, pool_w, pool_scale, q_norm_g, w_uq, kv_norm_g, w_ukv, w_out, mem_wq, mem_wkv, mem_wo, ffn2_w13, ffn2_w2, loss_target, m_ln_g, m_ln_b, m_ffn1_w13, m_ffn1_w2, m_w_in, m_pool_w, m_pool_scale, m_q_norm_g, m_w_uq, m_kv_norm_g, m_w_ukv, m_w_out, m_mem_wq, m_mem_wkv, m_mem_wo, m_ffn2_w13, m_ffn2_w2, v_ln_g, v_ln_b, v_ffn1_w13, v_ffn1_w2, v_w_in, v_pool_w, v_pool_scale, v_q_norm_g, v_w_uq, v_kv_norm_g, v_w_ukv, v_w_out, v_mem_wq, v_mem_wkv, v_mem_wo, v_ffn2_w13, v_ffn2_w2):
    L = ln_g.shape[0]
    T, D = x.shape[1], x.shape[2]
    F = ffn1_w2.shape[1] * N_DEV
    PW = D // 4
    H = (D - PW) // V_HEAD
    DIN = w_in.shape[2]
    DINP = PW + Q_LORA + KV_LORA + LANE
    QW = QK_NOPE + QK_ROPE
    alpha = (2 * L) ** 0.25
    x2d = x.reshape(T, D)
    memb = mem.reshape(mem.shape[1], D).astype(BF16)
    target = loss_target.reshape(T, D)
    tabs = _rope_tables(positions.reshape(T))

    def shards_of(l):
        return dict(
            w13a=ffn1_w13[l].T[None].astype(BF16),
            w13b=ffn2_w13[l].T[None].astype(BF16),
            w2a=ffn1_w2[l][None].astype(BF16),
            w2b=ffn2_w2[l][None].astype(BF16),
            wsq=jnp.stack([w_out[l], mem_wq[l], mem_wo[l]]).astype(BF16),
            wkvT=mem_wkv[l].T[None].astype(BF16),
            winp=jnp.pad(w_in[l], ((0, 0), (0, DINP - DIN)))[None].astype(BF16),
            wuqT=w_uq[l].T[None].astype(BF16),
            wukvT=w_ukv[l].T[None].astype(BF16),
        )

    SMALL = ("winp", "wuqT", "wukvT")
    shards = [shards_of(l) for l in range(L)]
    W = [dict() for _ in range(L)]

    def rider(spec):
        return _Gather([shards[l][n] for l, n in spec]) if spec else None

    def arrived(spec, arrays):
        for (l, n), a in zip(spec, arrays):
            if n in ("w13a", "w13b"):
                a = _interleave(a, 1)
            elif n == "wuqT":
                a = jnp.pad(a.reshape(H, QW, Q_LORA), ((0, 0), (0, HEAD_PAD - QW), (0, 0)))
                a = a.reshape(1, H * HEAD_PAD, Q_LORA)
            elif n == "ln":
                a = jnp.moveaxis(a.reshape(N_DEV, 2, L, 4, D // N_DEV), 0, 3).reshape(2, L, 4, D)
                LN["g"], LN["b"] = a[0], a[1]
            W[l][n] = a

    LN = {}
    shards[0]["ln"] = jnp.concatenate([ln_g.reshape(1, 4 * L, -1), ln_b.reshape(1, 4 * L, -1)], axis=1)
    spec0 = [(0, "w13a")]
    arrived(spec0, _exchange_alone(rider(spec0), name="ag_first"))
    wbd = [_block_diag(pool_w[l]).astype(BF16) for l in range(L)]

    def ffn_fwd(l, which, xres, xb, k, spec):
        ab = "ab"[which]
        h13, a, rode = _ffn_up(xb, W[l]["w13" + ab], 0, name=f"l{l}_ffn{which}_up", ride=rider(spec))
        arrived(spec, rode)
        y, xo, xob = _mm_ln(a, W[l]["w2" + ab], 0, xres, LN["g"][l,k:k + 1], LN["b"][l,k:k + 1], alpha=alpha, s=0.5,
                            name=f"l{l}_ffn{which}_y_ln{k}")
        return dict(xres=xres, xb=xb, h13=h13, a=a, y=y), xo, xob

    saved = []
    xres, xb = x2d, x2d.astype(BF16)
    for l in range(L):
        sv = {}
        more = l + 1 < L
        Wl = W[l]
        spec = [(l, n) for n in (("w2a", "ln", *SMALL) if l == 0 else ())] + [(l, "wsq"), (l, "wkvT")]
        sv["ffn1"], x1, x1b = ffn_fwd(l, 0, xres, xb, 0, spec)
        hin = _mm(x1b, Wl["winp"], lead=0, name=f"l{l}_hin")
        pscale = pool_scale[l].reshape(1, PW)
        gq, gkv = q_norm_g[l].reshape(1, Q_LORA), kv_norm_g[l].reshape(1, KV_LORA)
        cqn, ckvn, kpe = _norms_fwd(hin, gq, gkv, pw=PW, name=f"l{l}_norms")
        qraw = _mm(cqn, Wl["wuqT"], lead=0, tb=True, name=f"l{l}_qraw")
        kv = _mm(ckvn, Wl["wukvT"], lead=0, tb=True, name=f"l{l}_kv")
        qh, kh, vh = _heads_fwd(qraw, kv, kpe, tabs, H=H, name=f"l{l}_heads")
        spec = [(l, "w13b"), (l, "w2b")] + ([(l + 1, "w13a")] if more else [])
        cat, lse, rode = _flash_fwd(qh, kh, vh, H=H, pw=PW, name=f"l{l}_flash", ride=rider(spec))
        arrived(spec, rode)
        cat = _pool_fwd(hin, wbd[l], pscale, cat, name=f"l{l}_pool")
        ymix, x2, x2b = _mm_ln(cat, Wl["wsq"], 0, x1, LN["g"][l,1:2], LN["b"][l,1:2], alpha=alpha, s=1.0,
                               name=f"l{l}_ymix_ln1")
        qm = _mm(x2b, Wl["wsq"], lead=1, out_dtype=BF16, name=f"l{l}_qm")
        kvm = _mm(memb, Wl["wkvT"], lead=0, tb=True, name=f"l{l}_kvm")
        km, vm = kvm[:, :D], kvm[:, D:]
        om = _mem_fwd(qm, km, vm, name=f"l{l}_memattn")
        ymem, x3, x3b = _mm_ln(om, Wl["wsq"], 2, x2, LN["g"][l,2:3], LN["b"][l,2:3], alpha=alpha, s=1.0,
                               name=f"l{l}_ymem_ln2")
        spec = [(l + 1, n) for n in ("w2a", *SMALL)] if more else []
        sv["ffn2"], x4, x4b = ffn_fwd(l, 1, x3, x3b, 3, spec)
        sv.update(x1=x1, x1b=x1b, hin=hin, pscale=pscale, gq=gq, gkv=gkv, cqn=cqn, ckvn=ckvn,
                  qh=qh, kh=kh, vh=vh, lse=lse, cat=cat, ymix=ymix, x2=x2, x2b=x2b, qm=qm, km=km, vm=vm,
                  om=om, ymem=ymem)
        saved.append(sv)
        xres, xb = x4, x4b

    dx, loss_blk = _loss_head(xres, target, name="loss_head")
    loss = lax.psum(loss_blk[0, 0], ("x", "y", "c"))

    gW = {}
    gS = {}

    def ffn_bwd(l, which, sv, dx, k, ride):
        tag = f"l{l}_ffn{which}"
        dxres, dyb, dg, db = _ln_bwd(sv["xres"], sv["y"], LN["g"][l,k:k + 1], dx,
                                     alpha=alpha, s=0.5, name=f"l{l}_ln{k}_bwd")
        gW[("w2", which, l)] = _mm(sv["a"], dyb, ta=True, out_dtype=BF16, name=f"{tag}_dw2", tn=D)
        dh, rode = _ffn_down_bwd(dyb, W[l]["w2" + "ab"[which]], 0, sv["h13"], name=f"{tag}_dh", ride=ride)
        nf = F // FFN_TILE
        gW[("w13", which, l)] = _mm(dh, sv["xb"], ta=True, out_dtype=BF16, name=f"{tag}_dw13", tn=D,
                                    tm=FFN_TILE, row_map=(FFN_TILE, lambda i: (i % 2) * nf + i // 2))
        gS[("ln_g", l, k)], gS[("ln_b", l, k)] = dg, db
        last = rs_first_level(l, "c") if (l == 0 and which == 0) else None
        dxn = _mm(dh, W[l]["w13" + "ab"[which]], lead=0, add=dxres, name=f"{tag}_dx", tn=D,
                  ride=last["ex"] if last else None)
        if last:
            dxn, got = dxn
            rs_last_level(last, got)
        return dxn, rode

    core = lax.axis_index("c").astype(jnp.int32).reshape(1)
    chip = (2 * lax.axis_index("x") + lax.axis_index("y")).astype(jnp.int32).reshape(1)
    gsh = {}

    def rs_first_level(l, group):
        keys, classes = {
            "a": ([("w13", 1, l), ("w2", 1, l), ("mem_wkv", l), ("mem_wq", l), ("mem_wo", l)], [0] * 5),
            "b": ([("w_out", l), ("w_in", l), ("w_uq", l), ("w_ukv", l)], [0, 1, 2, 3]),
            "c": ([("w13", 0, l), ("w2", 0, l)], [0, 0]),
        }[group]
        tag = f"l{l}{group}"
        garrs = []
        for key in keys:
            g = gW[key]
            if key[0] == "w_uq":
                g = g.reshape(H, HEAD_PAD, Q_LORA)[:, :QW, :].reshape(H * QW, Q_LORA)
            garrs.append(g)
        offs, used = _class_layout(garrs, classes)
        parts = list(_rs_to_sibling(garrs, classes, name=f"{tag}_rs_sibling"))
        for cl in range(len(parts)):
            mine = [w for w, c in enumerate(classes) if c == cl]
            parts[cl] = _pair_sum(core, [garrs[w] for w in mine], parts[cl], [offs[w] for w in mine],
                                  name=f"{tag}_rs_pair_sum{cl}")
        return dict(tag=tag, keys=keys, garrs=garrs, classes=classes, offs=offs, used=used, parts=parts,
                    ex=_ChipExchange(parts, used))

    def rs_last_level(st, gathered):
        sums = [_quad_sum(chip, p, a, u, name=f"{st['tag']}_rs_quad_sum{k}")
                for k, (p, a, u) in enumerate(zip(st["parts"], gathered, st["used"]))]
        for key, g, cl, off in zip(st["keys"], st["garrs"], st["classes"], st["offs"]):
            gsh[key] = sums[cl][off:off + g.shape[0] // N_DEV, :]

    above = None
    for l in reversed(range(L)):
        sv = saved[l]
        Wl = W[l]
        dx, _ = ffn_bwd(l, 1, sv["ffn2"], dx, 3, None)
        dxres, dyb, dg, db = _ln_bwd(sv["x2"], sv["ymem"], LN["g"][l,2:3], dx,
                                     alpha=alpha, s=1.0, name=f"l{l}_ln2_bwd")
        gS[("ln_g", l, 2)], gS[("ln_b", l, 2)] = dg, db
        dom = _mm(dyb, Wl["wsq"], lead=2, tb=True, out_dtype=BF16, name=f"l{l}_dom")
        gW[("mem_wo", l)] = _mm(sv["om"], dyb, ta=True, out_dtype=BF16, name=f"l{l}_dwo", tn=D)
        dqm, dkm, dvm = _mem_bwd(sv["qm"], sv["km"], sv["vm"], dom, name=f"l{l}_memattn_bwd")
        dx = _mm(dqm, Wl["wsq"], lead=1, tb=True, add=dxres, name=f"l{l}_dx2", tn=D)
        gW[("mem_wq", l)] = _mm(sv["x2b"], dqm, ta=True, out_dtype=BF16, name=f"l{l}_dwq", tn=D)
        dkvm = jnp.concatenate([dkm, dvm], axis=1).astype(BF16)
        gW[("mem_wkv", l)] = _mm(dkvm, memb, ta=True, out_dtype=BF16, name=f"l{l}_dwkv", tn=D)
        dxres, dyb, dg, db = _ln_bwd(sv["x1"], sv["ymix"], LN["g"][l,1:2], dx,
                                     alpha=alpha, s=1.0, name=f"l{l}_ln1_bwd")
        gS[("ln_g", l, 1)], gS[("ln_b", l, 1)] = dg, db
        dcat = _mm(dyb, Wl["wsq"], lead=0, tb=True, name=f"l{l}_dcat", tn=D)
        gW[("w_out", l)] = _mm(sv["cat"], dyb, ta=True, out_dtype=BF16, name=f"l{l}_dwout", tn=D)
        riding = [rs_first_level(l, "a")] + ([above] if above else [])
        both = _Both([st["ex"] for st in riding])
        dqh, dkh, dvh, rode = _flash_bwd(sv["qh"], sv["kh"], sv["vh"], sv["cat"], dcat, sv["lse"], H=H, pw=PW,
                                         name=f"l{l}_flash_bwd", ride=both)
        for st, got in zip(riding, both.split(rode)):
            rs_last_level(st, got)
        dqraw, dkv, dkpe = _heads_bwd(dqh, dkh, dvh, tabs, H=H, name=f"l{l}_heads_bwd")
        dcq = _mm(dqraw, Wl["wuqT"], lead=0, name=f"l{l}_dcq")
        gW[("w_uq", l)] = _mm(dqraw, sv["cqn"], ta=True, out_dtype=BF16, name=f"l{l}_dwuq")
        dckv = _mm(dkv, Wl["wukvT"], lead=0, name=f"l{l}_dckv")
        gW[("w_ukv", l)] = _mm(dkv, sv["ckvn"], ta=True, out_dtype=BF16, name=f"l{l}_dwukv")
        du, dwbd, dps = _pool_bwd(sv["hin"], dcat, wbd[l], sv["pscale"], name=f"l{l}_pool_bwd")
        dhin, dgq, dgkv = _norms_bwd(sv["hin"], sv["gq"], sv["gkv"], dcq, dckv, dkpe, du, pw=PW,
                                     name=f"l{l}_norms_bwd")
        pg = PW // len(POOL_WINDOWS)
        gS[("pool_w", l)] = jnp.stack([dwbd[g * pg:(g + 1) * pg, g * pg:(g + 1) * pg]
                                       for g in range(len(POOL_WINDOWS))])
        gS[("pool_scale", l)], gS[("q_norm_g", l)], gS[("kv_norm_g", l)] = dps, dgq, dgkv
        dx = _mm(dhin, Wl["winp"], lead=0, tb=True, add=dxres, name=f"l{l}_dx1", tn=D)
        gW[("w_in", l)] = _mm(sv["x1b"], dhin, ta=True, out_dtype=BF16, name=f"l{l}_dwin", tn=DINP)
        heads = rs_first_level(l, "b")
        dx, rode = ffn_bwd(l, 0, sv["ffn1"], dx, 0, heads["ex"])
        rs_last_level(heads, rode)
        above = rs_first_level(l, "c") if l > 0 else None
    grad_x = dx.reshape(1, T, D)

    small_keys = []
    for l in range(L):
        small_keys += [("pool_w", l), ("pool_scale", l), ("q_norm_g", l), ("kv_norm_g", l)]
        small_keys += [("ln_g", l, k) for k in range(4)] + [("ln_b", l, k) for k in range(4)]
    flat = jnp.concatenate([gS[k].reshape(-1) for k in small_keys])
    n_small = flat.shape[0]
    rows = -(-n_small // (8 * LANE)) * 8
    flat = jnp.pad(flat, (0, rows * LANE - n_small)).reshape(rows, LANE)
    red = _all_reduce_small(flat, name="ar_small").reshape(-1)
    gsm, pos = {}, 0
    for k in small_keys:
        size = math.prod(gS[k].shape)
        gsm[k] = red[pos:pos + size].reshape(gS[k].shape)
        pos += size

    me = 4 * lax.axis_index("x") + 2 * lax.axis_index("y") + lax.axis_index("c")
    dsh = D // N_DEV
    stack = lambda f: jnp.stack([f(l) for l in range(L)])
    g_ln_g = stack(lambda l: jnp.concatenate([gsm[("ln_g", l, k)] for k in range(4)], axis=0))
    g_ln_b = stack(lambda l: jnp.concatenate([gsm[("ln_b", l, k)] for k in range(4)], axis=0))
    grads = {
        "ln_g": lax.dynamic_slice_in_dim(g_ln_g, me * dsh, dsh, axis=2),
        "ln_b": lax.dynamic_slice_in_dim(g_ln_b, me * dsh, dsh, axis=2),
        "ffn1_w13": stack(lambda l: gsh[("w13", 0, l)].T),
        "ffn1_w2": stack(lambda l: gsh[("w2", 0, l)]),
        "w_in": stack(lambda l: gsh[("w_in", l)][:, :DIN]),
        "pool_w": stack(lambda l: gsm[("pool_w", l)]),
        "pool_scale": stack(lambda l: gsm[("pool_scale", l)].reshape(PW)),
        "q_norm_g": stack(lambda l: gsm[("q_norm_g", l)].reshape(Q_LORA)),
        "w_uq": stack(lambda l: gsh[("w_uq", l)].T),
        "kv_norm_g": stack(lambda l: gsm[("kv_norm_g", l)].reshape(KV_LORA)),
        "w_ukv": stack(lambda l: gsh[("w_ukv", l)].T),
        "w_out": stack(lambda l: gsh[("w_out", l)]),
        "mem_wq": stack(lambda l: gsh[("mem_wq", l)]),
        "mem_wkv": stack(lambda l: gsh[("mem_wkv", l)].T),
        "mem_wo": stack(lambda l: gsh[("mem_wo", l)]),
        "ffn2_w13": stack(lambda l: gsh[("w13", 1, l)].T),
        "ffn2_w2": stack(lambda l: gsh[("w2", 1, l)]),
    }

    names = ["ln_g", "ln_b", "ffn1_w13", "ffn1_w2", "w_in", "pool_w", "pool_scale", "q_norm_g", "w_uq",
             "kv_norm_g", "w_ukv", "w_out", "mem_wq", "mem_wkv", "mem_wo", "ffn2_w13", "ffn2_w2"]
    weights = dict(ln_g=ln_g, ln_b=ln_b, ffn1_w13=ffn1_w13, ffn1_w2=ffn1_w2, w_in=w_in, pool_w=pool_w,
                   pool_scale=pool_scale, q_norm_g=q_norm_g, w_uq=w_uq, kv_norm_g=kv_norm_g, w_ukv=w_ukv,
                   w_out=w_out, mem_wq=mem_wq, mem_wkv=mem_wkv, mem_wo=mem_wo, ffn2_w13=ffn2_w13,
                   ffn2_w2=ffn2_w2)
    ms = dict(ln_g=m_ln_g, ln_b=m_ln_b, ffn1_w13=m_ffn1_w13, ffn1_w2=m_ffn1_w2, w_in=m_w_in, pool_w=m_pool_w,
              pool_scale=m_pool_scale, q_norm_g=m_q_norm_g, w_uq=m_w_uq, kv_norm_g=m_kv_norm_g,
              w_ukv=m_w_ukv, w_out=m_w_out, mem_wq=m_mem_wq, mem_wkv=m_mem_wkv, mem_wo=m_mem_wo,
              ffn2_w13=m_ffn2_w13, ffn2_w2=m_ffn2_w2)
    vs = dict(ln_g=v_ln_g, ln_b=v_ln_b, ffn1_w13=v_ffn1_w13, ffn1_w2=v_ffn1_w2, w_in=v_w_in, pool_w=v_pool_w,
              pool_scale=v_pool_scale, q_norm_g=v_q_norm_g, w_uq=v_w_uq, kv_norm_g=v_kv_norm_g,
              w_ukv=v_w_ukv, w_out=v_w_out, mem_wq=v_mem_wq, mem_wkv=v_mem_wkv, mem_wo=v_mem_wo,
              ffn2_w13=v_ffn2_w13, ffn2_w2=v_ffn2_w2)
    deltas, new_m, new_v = [], [], []
    for nme in names:
        d, mn, vn = _adamw(weights[nme], grads[nme], ms[nme], vs[nme], name=f"adamw_{nme}")
        deltas.append(d)
        new_m.append(mn)
        new_v.append(vn)
    return (loss, grad_x, *[grads[nme] for nme in names], *deltas, *new_m, *new_v)
```

```python
import functools
import math

import jax
import jax.numpy as jnp
from jax import lax
from jax.experimental import pallas as pl
from jax.experimental.pallas import tpu as pltpu

F32 = jnp.float32
BF16 = jnp.bfloat16
MESH = pl.DeviceIdType.MESH

CHUNK = 64
MEM_HEADS = 4
POOL_WINDOWS = (2, 4, 8, 16)
QK_NOPE = 128
QK_ROPE = 64
V_HEAD = 128
Q_LORA = 256
KV_LORA = 128
ROPE_BASE = 10000.0
LN_EPS = 1e-5
RMS_EPS = 1e-6
NEG_INF = -1e30
ADAM_LR = 0.001
ADAM_B1 = 0.9
ADAM_B2 = 0.999
ADAM_EPS = 1e-08
ADAM_WD = 0.01
ADAM_STEP = 10

N_DEV = 8
LANE = 128
HEAD_PAD = 2 * LANE
POOL_HALO = 16
VMEM_CAP = 56 * 1024 * 1024
VMEM_FLOOR = 32 * 1024 * 1024


def _tile(n, pref, mult):
    t = (min(pref, n) // mult) * mult
    while t >= mult:
        if n % t == 0:
            return t
        t -= mult
    return n


def _params(sem, est_bytes):
    limit = int(min(max(2 * est_bytes + (8 << 20), VMEM_FLOOR), VMEM_CAP))
    return pltpu.CompilerParams(dimension_semantics=sem, vmem_limit_bytes=limit)


def _nbytes(shape, dtype):
    return math.prod(shape) * jnp.dtype(dtype).itemsize


def _hbm(x):
    return pltpu.with_memory_space_constraint(x, pltpu.HBM)


def _out(shape, dtype):
    return pltpu.HBM(tuple(shape), dtype)


def _dg(a, b, ca, cb):
    return lax.dot_general(a.astype(BF16), b.astype(BF16), (((ca,), (cb,)), ((), ())),
                           preferred_element_type=F32)


@jax.custom_vjp
def _bdot_nn(a, b):
    return _dg(a, b, 1, 0)


def _bdot_nn_fwd(a, b):
    return _dg(a, b, 1, 0), (a, b)


def _bdot_nn_bwd(res, ct):
    a, b = res
    return _dg(ct, b, 1, 1).astype(a.dtype), _dg(a, ct, 0, 0).astype(b.dtype)


_bdot_nn.defvjp(_bdot_nn_fwd, _bdot_nn_bwd)


@jax.custom_vjp
def _bdot_nt(a, b):
    return _dg(a, b, 1, 1)


def _bdot_nt_fwd(a, b):
    return _dg(a, b, 1, 1), (a, b)


def _bdot_nt_bwd(res, ct):
    a, b = res
    return _dg(ct, b, 1, 0).astype(a.dtype), _dg(ct, a, 0, 0).astype(b.dtype)


_bdot_nt.defvjp(_bdot_nt_fwd, _bdot_nt_bwd)


@functools.partial(jax.custom_vjp, nondiff_argnums=(1,))
def _lane_roll(x, shift):
    return pltpu.roll(x, shift % x.shape[1], axis=1)


def _lane_roll_fwd(x, shift):
    return _lane_roll(x, shift), None


def _lane_roll_bwd(shift, _, ct):
    return (_lane_roll(ct, -shift),)


_lane_roll.defvjp(_lane_roll_fwd, _lane_roll_bwd)


@functools.partial(jax.custom_vjp, nondiff_argnums=(1, 2))
def _cols(x, lo, hi):
    return x[:, lo:hi]


def _cols_fwd(x, lo, hi):
    return x[:, lo:hi], x.shape[1]


def _cols_bwd(lo, hi, width, ct):
    parts = []
    if lo > 0:
        parts.append(jnp.zeros((ct.shape[0], lo), ct.dtype))
    parts.append(ct)
    if hi < width:
        parts.append(jnp.zeros((ct.shape[0], width - hi), ct.dtype))
    return (jnp.concatenate(parts, axis=1) if len(parts) > 1 else ct,)


_cols.defvjp(_cols_fwd, _cols_bwd)


MM_VMEM_BUDGET = 20 * 1024 * 1024


def _mm(a, b, *, name, ta=False, tb=False, out_dtype=F32, lead=None, add=None, add_scale=1.0,
        tm=1024, tn=1024, tk=2816, ride=None):
    if ta:
        K, M = a.shape
    else:
        M, K = a.shape
    bshape = b.shape[1:] if lead is not None else b.shape
    if tb:
        N, Kb = bshape
    else:
        Kb, N = bshape
    assert K == Kb, (name, a.shape, b.shape)

    def blocks(tm, tn, tk):
        tm = _tile(M, tm, LANE if ta else 16)
        tn = _tile(N, tn, LANE)
        tk = _tile(K, tk, LANE)
        nbytes = (tm * tk * a.dtype.itemsize + tk * tn * b.dtype.itemsize
                  + tm * tn * (jnp.dtype(out_dtype).itemsize + (4 if K // tk > 1 else 0)
                               + (add.dtype.itemsize if add is not None else 0)))
        return tm, tn, tk, nbytes

    tm, tn, tk, est = blocks(tm, tn, tk)
    for shrink in ("m", "k", "m", "k", "n"):
        if est <= MM_VMEM_BUDGET:
            break
        if shrink == "m":
            tm, tn, tk, est = blocks(max(tm // 2, LANE), tn, tk)
        elif shrink == "k":
            tm, tn, tk, est = blocks(tm, tn, max(tk // 2, LANE))
        else:
            tm, tn, tk, est = blocks(tm, max(tn // 2, LANE), tk)
    nk = K // tk
    ca = 0 if ta else 1
    cb = 1 if tb else 0

    def body(*refs):
        a_ref, b_ref = refs[0], refs[1]
        add_ref = refs[2] if add is not None else None
        o_ref = refs[3] if add is not None else refs[2]

        def finish(r):
            if add_ref is not None:
                r = r + add_scale * add_ref[...].astype(F32)
            o_ref[...] = r.astype(o_ref.dtype)

        if nk == 1:
            finish(_dg(a_ref[...], b_ref[...], ca, cb))
            return
        acc_ref = refs[-1]
        k = pl.program_id(2)

        @pl.when(k == 0)
        def _():
            acc_ref[...] = jnp.zeros_like(acc_ref)

        acc_ref[...] += _dg(a_ref[...], b_ref[...], ca, cb)

        @pl.when(k == nk - 1)
        def _():
            finish(acc_ref[...])

    a_blk = (tk, tm) if ta else (tm, tk)
    a_map = (lambda i, j, k: (k, i)) if ta else (lambda i, j, k: (i, k))
    b_blk = (tn, tk) if tb else (tk, tn)
    if lead is None:
        b_map = (lambda i, j, k: (j, k)) if tb else (lambda i, j, k: (k, j))
        b_spec = pl.BlockSpec(b_blk, b_map)
    else:
        b_map = (lambda i, j, k: (lead, j, k)) if tb else (lambda i, j, k: (lead, k, j))
        b_spec = pl.BlockSpec((None,) + b_blk, b_map)
    in_specs = [pl.BlockSpec(a_blk, a_map), b_spec]
    args = [a, b]
    if add is not None:
        in_specs.append(pl.BlockSpec((tm, tn), lambda i, j, k: (i, j)))
        args.append(add)
    (out,), rode = _host_call(
        body, name=name,
        grid=(M // tm, N // tn, nk),
        in_specs=in_specs,
        out_specs=[pl.BlockSpec((tm, tn), lambda i, j, k: (i, j))],
        out_shape=[_out((M, N), out_dtype)],
        scratch=[pltpu.VMEM((tm, tn), F32)] if nk > 1 else [],
        args=[_hbm(v) for v in args], sem=("parallel", "parallel", "arbitrary"), est=est + tm * tn * 4,
        ride=ride)
    return out if ride is None else (out, rode)


def _rowwise(fn, tiles, params, tile_outs, acc_outs=(), *, tm, name):
    tile_arrays, tile_specs = [], []
    for t in tiles:
        if isinstance(t, tuple):
            tile_arrays.append(t[0])
            tile_specs.append(t[1])
        else:
            tile_arrays.append(t)
            tile_specs.append(pl.BlockSpec((tm, t.shape[1]), lambda i: (i, 0)))
    T = tile_arrays[0].shape[0]
    nt, np_, nto, nao = len(tile_arrays), len(params), len(tile_outs), len(acc_outs)

    def body(*refs):
        i = pl.program_id(0)
        tvals = [r[...] for r in refs[:nt]]
        pvals = [r[...] for r in refs[nt:nt + np_]]
        to_refs = refs[nt + np_:nt + np_ + nto]
        ao_refs = refs[nt + np_ + nto:]
        touts, aouts = fn(i, tvals, pvals)
        for r, v in zip(to_refs, touts):
            r[...] = v.astype(r.dtype)
        if nao:
            @pl.when(i == 0)
            def _():
                for r in ao_refs:
                    r[...] = jnp.zeros_like(r)
            for r, v in zip(ao_refs, aouts):
                r[...] += v.astype(r.dtype)

    in_specs = tile_specs + [pl.BlockSpec(p.shape, lambda i: (0, 0)) for p in params]
    out_specs = [pl.BlockSpec((tm, c), lambda i: (i, 0)) for c, _ in tile_outs]
    out_specs += [pl.BlockSpec(s, lambda i: (0, 0)) for s, _ in acc_outs]
    out_shape = [_out((T, c), d) for c, d in tile_outs]
    out_shape += [_out(s, d) for s, d in acc_outs]
    width = sum(s.block_shape[-1] for s in tile_specs) + sum(c for c, _ in tile_outs)
    est = 6 * tm * width * 4 + sum(_nbytes(p.shape, F32) for p in params) * 4
    return pl.pallas_call(
        body, name=name, grid=(T // tm,),
        in_specs=in_specs, out_specs=out_specs, out_shape=out_shape,
        compiler_params=_params(("arbitrary",) if nao else ("parallel",), est),
    )(*[_hbm(v) for v in tile_arrays], *[_hbm(p) for p in params])


def _ln_fn(alpha, s, xres, y, g, b):
    z = alpha * xres.astype(F32) + s * y.astype(F32)
    mu = jnp.mean(z, axis=-1, keepdims=True)
    zc = z - mu
    var = jnp.mean(zc * zc, axis=-1, keepdims=True)
    return zc * lax.rsqrt(var + LN_EPS) * g + b


def _mm_ln(a, b, lead, xres, g, bias, *, alpha, s, name):
    M, K = a.shape
    N = b.shape[2]
    tm = _tile(M, 256, 16)

    def body(a_ref, b_ref, x_ref, g_ref, bias_ref, y_ref, xo_ref, xb_ref):
        y = _dg(a_ref[...], b_ref[...], 1, 0)
        y_ref[...] = y.astype(y_ref.dtype)
        out = _ln_fn(alpha, s, x_ref[...], y, g_ref[...], bias_ref[...])
        xo_ref[...] = out
        xb_ref[...] = out.astype(BF16)

    row = pl.BlockSpec((tm, N), lambda i: (i, 0))
    vec = pl.BlockSpec((1, N), lambda i: (0, 0))
    est = tm * K * 2 + K * N * 2 + tm * N * (4 + 4 + 4 + 2 + 8)
    return pl.pallas_call(
        body, name=name, grid=(M // tm,),
        in_specs=[pl.BlockSpec((tm, K), lambda i: (i, 0)), pl.BlockSpec((None, K, N), lambda i: (lead, 0, 0)),
                  row, vec, vec],
        out_specs=[row, row, row],
        out_shape=[_out((M, N), BF16), _out((M, N), F32), _out((M, N), BF16)],
        compiler_params=_params(("parallel",), est),
    )(_hbm(a), _hbm(b), _hbm(xres), _hbm(g), _hbm(bias))


def _ln_bwd(xres, y, g, dout, *, alpha, s, name):
    T, D = xres.shape
    tm = _tile(T, 256, 16)

    def body(x_ref, y_ref, d_ref, g_ref, dx_ref, dy_ref, dg_ref, db_ref):
        @pl.when(pl.program_id(0) == 0)
        def _():
            dg_ref[...] = jnp.zeros_like(dg_ref)
            db_ref[...] = jnp.zeros_like(db_ref)

        z = alpha * x_ref[...] + s * y_ref[...].astype(F32)
        zc = z - jnp.mean(z, axis=-1, keepdims=True)
        r = lax.rsqrt(jnp.mean(zc * zc, axis=-1, keepdims=True) + LN_EPS)
        xh = zc * r
        d = d_ref[...]
        dxh = d * g_ref[...]
        dz = r * (dxh - jnp.mean(dxh, axis=-1, keepdims=True) - xh * jnp.mean(dxh * xh, axis=-1, keepdims=True))
        dx_ref[...] = alpha * dz
        dy_ref[...] = (s * dz).astype(dy_ref.dtype)
        dg_ref[...] += jnp.sum(d * xh, axis=0, keepdims=True)
        db_ref[...] += jnp.sum(d, axis=0, keepdims=True)

    row = pl.BlockSpec((tm, D), lambda i: (i, 0))
    vec = pl.BlockSpec((1, D), lambda i: (0, 0))
    return pl.pallas_call(
        body, name=name, grid=(T // tm,),
        in_specs=[row, row, row, vec], out_specs=[row, row, vec, vec],
        out_shape=[_out((T, D), F32), _out((T, D), BF16),
                   _out((1, D), F32), _out((1, D), F32)],
        compiler_params=_params(("arbitrary",), 12 * tm * D * 4),
    )(_hbm(xres), _hbm(y), _hbm(dout), _hbm(g))


FFN_TILE = 256


def _interleave(w, axis):
    n = w.shape[axis] // (2 * FFN_TILE)
    shp = w.shape[:axis] + (2, n, FFN_TILE) + w.shape[axis + 1:]
    return jnp.swapaxes(w.reshape(shp), axis, axis + 1).reshape(w.shape)


def _deinterleave(w, axis):
    n = w.shape[axis] // (2 * FFN_TILE)
    shp = w.shape[:axis] + (n, 2, FFN_TILE) + w.shape[axis + 1:]
    return jnp.swapaxes(w.reshape(shp), axis, axis + 1).reshape(w.shape)


def _ffn_up(xb, w13t, lead, *, name, ride=None):
    T, D = xb.shape
    F = w13t.shape[1] // 2
    tc = FFN_TILE
    tm = _tile(T, 1024, 16)

    def body(x_ref, w_ref, h_ref, a_ref):
        h = _dg(x_ref[...], w_ref[...], 1, 1)
        g, u = h[:, :tc], h[:, tc:]
        h_ref[...] = h.astype(h_ref.dtype)
        a_ref[...] = (g * jax.nn.sigmoid(g) * u).astype(a_ref.dtype)

    est = (tm * D + 2 * tc * D + 3 * tm * tc) * 2 + 3 * tm * tc * 4
    (h13, a), gathered = _host_call(
        body, name=name, grid=(T // tm, F // tc),
        in_specs=[pl.BlockSpec((tm, D), lambda i, j: (i, 0)),
                  pl.BlockSpec((None, 2 * tc, D), lambda i, j: (lead, j, 0))],
        out_specs=[pl.BlockSpec((tm, 2 * tc), lambda i, j: (i, j)),
                   pl.BlockSpec((tm, tc), lambda i, j: (i, j))],
        out_shape=[_out((T, 2 * F), BF16), _out((T, F), BF16)],
        args=[_hbm(xb), _hbm(w13t)], sem=("parallel", "parallel"), est=est, ride=ride)
    return h13, a, gathered


def _ffn_down_bwd(dyb, w2, lead, h13, *, name, ride=None):
    T, D = dyb.shape
    F = w2.shape[1]
    tc = FFN_TILE
    tm = _tile(T, 1024, 16)

    def body(dy_ref, w_ref, h_ref, dh_ref):
        d = _dg(dy_ref[...], w_ref[...], 1, 1)
        h = h_ref[...].astype(F32)
        g, u = h[:, :tc], h[:, tc:]
        sig = jax.nn.sigmoid(g)
        gs = g * sig
        dh_ref[...] = jnp.concatenate([d * u * (sig + gs * (1.0 - sig)), d * gs], axis=1).astype(dh_ref.dtype)

    est = (tm * D + tc * D + 4 * tm * tc) * 2 + 6 * tm * tc * 4
    (dh,), rode = _host_call(
        body, name=name, grid=(T // tm, F // tc),
        in_specs=[pl.BlockSpec((tm, D), lambda i, j: (i, 0)),
                  pl.BlockSpec((None, tc, D), lambda i, j: (lead, j, 0)),
                  pl.BlockSpec((tm, 2 * tc), lambda i, j: (i, j))],
        out_specs=[pl.BlockSpec((tm, 2 * tc), lambda i, j: (i, j))],
        out_shape=[_out((T, 2 * F), BF16)],
        args=[_hbm(dyb), _hbm(w2), _hbm(h13)], sem=("parallel", "parallel"), est=est, ride=ride)
    return dh, rode


def _pool_select(parts, pw):
    pg = pw // len(POOL_WINDOWS)
    grp = lax.broadcasted_iota(jnp.int32, parts[0].shape, 1) // pg
    out = parts[3]
    for g in (2, 1, 0):
        out = jnp.where(grp == g, parts[g], out)
    return out


def _pool_count(t0, rows, pw):
    pg = pw // len(POOL_WINDOWS)
    grp = lax.broadcasted_iota(jnp.int32, (rows, pw), 1) // pg
    win = jnp.where(grp == 0, POOL_WINDOWS[0],
                    jnp.where(grp == 1, POOL_WINDOWS[1],
                              jnp.where(grp == 2, POOL_WINDOWS[2], POOL_WINDOWS[3])))
    t = t0 + lax.broadcasted_iota(jnp.int32, (rows, pw), 0)
    return jnp.minimum(t + 1, win).astype(F32)


def _window_sums(ext, up):
    n = ext.shape[0]
    sums, cur, k = [], ext, 1
    for _ in POOL_WINDOWS:
        cur = cur + pltpu.roll(cur, (n - k) if up else k, axis=0)
        sums.append(cur)
        k *= 2
    return sums


def _pool_delta(u, halo, t0):
    tm, pw = u.shape
    ext = jnp.concatenate([halo, u], axis=0)
    sums = [s[POOL_HALO:, :] for s in _window_sums(ext, up=False)]
    return _pool_select(sums, pw) / _pool_count(t0, tm, pw) - u


def _pool_fwd(hin, wbd, scale, cat, *, name):
    T = hin.shape[0]
    pw = wbd.shape[0]
    tm = _tile(T, 256, POOL_HALO)
    per = tm // POOL_HALO

    def body(u_ref, halo_ref, w_ref, s_ref, cat_ref, y_ref):
        i = pl.program_id(0)
        halo = jnp.where(i > 0, halo_ref[...], 0.0)
        d = _pool_delta(u_ref[...], halo, i * tm)
        y_ref[...] = (_dg(d, w_ref[...], 1, 0) * s_ref[...]).astype(y_ref.dtype)

    return pl.pallas_call(
        body, name=name, grid=(T // tm,),
        in_specs=[pl.BlockSpec((tm, pw), lambda i: (i, 0)),
                  pl.BlockSpec((POOL_HALO, pw), lambda i: (jnp.maximum(i * per - 1, 0), 0)),
                  pl.BlockSpec((pw, pw), lambda i: (0, 0)),
                  pl.BlockSpec((1, pw), lambda i: (0, 0)),
                  ANY],
        out_specs=pl.BlockSpec((tm, pw), lambda i: (i, 0)),
        out_shape=_out(cat.shape, cat.dtype),
        input_output_aliases={4: 0},
        compiler_params=_params(("parallel",), 16 * tm * pw * 4),
    )(_hbm(hin), _hbm(hin), _hbm(wbd), _hbm(scale), _hbm(cat))


def _pool_bwd(hin, dcat, wbd, scale, *, name):
    T = hin.shape[0]
    pw = wbd.shape[0]
    tm = _tile(T, 256, POOL_HALO)
    per = tm // POOL_HALO
    nt = T // tm

    def body(u_ref, halo_ref, dy_ref, dyn_ref, w_ref, s_ref, du_ref, dw_ref, ds_ref):
        i = pl.program_id(0)

        @pl.when(i == 0)
        def _():
            dw_ref[...] = jnp.zeros_like(dw_ref)
            ds_ref[...] = jnp.zeros_like(ds_ref)

        halo = jnp.where(i > 0, halo_ref[...], 0.0)
        d = _pool_delta(u_ref[...], halo, i * tm)
        w = w_ref[...]
        sc = s_ref[...]
        dy = dy_ref[...]
        dyn = jnp.where(i < nt - 1, dyn_ref[...], 0.0)
        ds_ref[...] += jnp.sum(dy * _dg(d, w, 1, 0), axis=0, keepdims=True)
        dys = dy * sc
        dw_ref[...] += _dg(d, dys, 0, 0)
        dys_ext = jnp.concatenate([dys, dyn * sc], axis=0)
        dd_ext = _dg(dys_ext, w, 1, 1)
        ddp = dd_ext / _pool_count(i * tm, tm + POOL_HALO, pw)
        sums = [s[:tm, :] for s in _window_sums(ddp, up=True)]
        du_ref[...] = _pool_select(sums, pw) - dd_ext[:tm, :]

    return pl.pallas_call(
        body, name=name, grid=(nt,),
        in_specs=[pl.BlockSpec((tm, pw), lambda i: (i, 0)),
                  pl.BlockSpec((POOL_HALO, pw), lambda i: (jnp.maximum(i * per - 1, 0), 0)),
                  pl.BlockSpec((tm, pw), lambda i: (i, 0)),
                  pl.BlockSpec((POOL_HALO, pw), lambda i: (jnp.minimum((i + 1) * per, nt * per - 1), 0)),
                  pl.BlockSpec((pw, pw), lambda i: (0, 0)),
                  pl.BlockSpec((1, pw), lambda i: (0, 0))],
        out_specs=[pl.BlockSpec((tm, pw), lambda i: (i, 0)),
                   pl.BlockSpec((pw, pw), lambda i: (0, 0)),
                   pl.BlockSpec((1, pw), lambda i: (0, 0))],
        out_shape=[_out((T, pw), F32),
                   _out((pw, pw), F32),
                   _out((1, pw), F32)],
        compiler_params=_params(("arbitrary",), 24 * tm * pw * 4),
    )(_hbm(hin), _hbm(hin), _hbm(dcat), _hbm(dcat), _hbm(wbd), _hbm(scale))


def _rms(x, g):
    return x * lax.rsqrt(jnp.mean(x * x, axis=-1, keepdims=True) + RMS_EPS) * g


def _norms_fn(pw, h, gq, gkv):
    o1 = pw + Q_LORA
    o2 = o1 + KV_LORA
    return (_rms(_cols(h, pw, o1), gq), _rms(_cols(h, o1, o2), gkv), _cols(h, o2, h.shape[1]))


def _norms_fwd(hin, gq, gkv, *, pw, name):
    tm = _tile(hin.shape[0], 256, 16)

    def fn(i, tv, pv):
        return _norms_fn(pw, tv[0], pv[0], pv[1]), ()

    return _rowwise(fn, [hin], [gq, gkv], [(Q_LORA, BF16), (KV_LORA, BF16), (LANE, F32)], tm=tm, name=name)


def _norms_bwd(hin, gq, gkv, dcq, dckv, dkpe, du, *, pw, name):
    tm = _tile(hin.shape[0], 256, 16)
    dinp = hin.shape[1]

    def fn(i, tv, pv):
        _, vjp = jax.vjp(functools.partial(_norms_fn, pw), tv[0], pv[0], pv[1])
        dh, dgq, dgkv = vjp((tv[1].astype(F32), tv[2].astype(F32), tv[3].astype(F32)))
        dh = jnp.concatenate([tv[4], dh[:, pw:]], axis=1)
        return (dh,), (dgq, dgkv)

    return _rowwise(fn, [hin, dcq, dckv, dkpe, du], [gq, gkv], [(dinp, BF16)],
                    [((1, Q_LORA), F32), ((1, KV_LORA), F32)], tm=tm, name=name)


def _heads_fn(H, qraw, kv, kpe, rc, rs1, rs2):
    half = QK_ROPE // 2
    scale = (QK_NOPE + QK_ROPE) ** -0.5

    def rope(blk):
        return blk * rc + _lane_roll(blk, -half) * rs1 + _lane_roll(blk, half) * rs2

    krot = rope(kpe)
    qs, ks, vs = [], [], []
    for h in range(H):
        lo = h * HEAD_PAD
        qs += [_cols(qraw, lo, lo + LANE) * scale, rope(_cols(qraw, lo + LANE, lo + HEAD_PAD)) * scale]
        ks += [_cols(kv, lo, lo + LANE), krot]
        vs += [_cols(kv, lo + LANE, lo + HEAD_PAD)]
    return jnp.concatenate(qs, axis=1), jnp.concatenate(ks, axis=1), jnp.concatenate(vs, axis=1)


def _heads_fwd(qraw, kv, kpe, tabs, *, H, name):
    tm = _tile(qraw.shape[0], 256, 16)

    def fn(i, tv, pv):
        return _heads_fn(H, *tv), ()

    return _rowwise(fn, [qraw, kv, kpe, *tabs], [],
                    [(H * HEAD_PAD, BF16), (H * HEAD_PAD, BF16), (H * V_HEAD, BF16)], tm=tm, name=name)


def _heads_bwd(dq, dk, dv, tabs, *, H, name):
    tm = _tile(dq.shape[0], 256, 16)

    def fn(i, tv, pv):
        z = jnp.zeros((tm, H * HEAD_PAD), F32)
        zk = jnp.zeros((tm, LANE), F32)
        rc, rs1, rs2 = tv[3], tv[4], tv[5]
        _, vjp = jax.vjp(lambda a, b, c: _heads_fn(H, a, b, c, rc, rs1, rs2), z, z, zk)
        return vjp((tv[0], tv[1], tv[2])), ()

    return _rowwise(fn, [dq, dk, dv, *tabs], [],
                    [(H * HEAD_PAD, BF16), (H * HEAD_PAD, BF16), (LANE, F32)], tm=tm, name=name)


def _diag_mask(rows, cols, row0):
    r = (row0 + lax.broadcasted_iota(jnp.int32, (rows, cols), 0)) // CHUNK
    c = lax.broadcasted_iota(jnp.int32, (rows, cols), 1) // CHUNK
    return r >= c


def _flash_fwd(qh, kh, vh, *, H, pw, name, ride=None):
    T = qh.shape[0]
    t = _tile(T, 512, CHUNK)
    off = pw // V_HEAD


    def body(q_ref, k_ref, v_ref, o_ref, lse_ref):
        i = pl.program_id(1)
        q = q_ref[...]

        def blk(j, carry, masked):
            m, l, acc = carry
            rows = pl.ds(pl.multiple_of(j * t, t), t)
            s = _dg(q, k_ref[rows, :], 1, 1)
            if masked:
                s = jnp.where(_diag_mask(t, t, 0), s, NEG_INF)
            mn = jnp.maximum(m, jnp.max(s, axis=1, keepdims=True))
            p = jnp.exp(s - mn)
            corr = jnp.exp(m - mn)
            l = corr * l + jnp.sum(p, axis=1, keepdims=True)
            acc = corr * acc + _dg(p, v_ref[rows, :], 1, 0)
            return mn, l, acc

        init = (jnp.full((t, 1), NEG_INF, F32), jnp.zeros((t, 1), F32), jnp.zeros((t, V_HEAD), F32))
        carry = lax.fori_loop(0, i, lambda j, c: blk(j, c, False), init)
        m, l, acc = blk(i, carry, True)
        o_ref[...] = (acc / l).astype(o_ref.dtype)
        lse_ref[...] = jnp.broadcast_to(m + jnp.log(l), (t, V_HEAD))

    est = 2 * T * (HEAD_PAD + V_HEAD) * 2 + 8 * t * t * 4
    (o, lse), gathered = _host_call(
        body, name=name, grid=(H, T // t),
        in_specs=[pl.BlockSpec((t, HEAD_PAD), lambda h, i: (i, h)),
                  pl.BlockSpec((T, HEAD_PAD), lambda h, i: (0, h)),
                  pl.BlockSpec((T, V_HEAD), lambda h, i: (0, h))],
        out_specs=[pl.BlockSpec((t, V_HEAD), lambda h, i: (i, off + h)),
                   pl.BlockSpec((t, V_HEAD), lambda h, i: (i, h))],
        out_shape=[_out((T, pw + H * V_HEAD), BF16),
                   _out((T, H * V_HEAD), F32)],
        args=[_hbm(qh), _hbm(kh), _hbm(vh)], sem=("parallel", "parallel"), est=est, ride=ride)
    return o, lse, gathered


def _flash_bwd(qh, kh, vh, cat, dcat, lse, *, H, pw, name, ride=None):
    T = qh.shape[0]
    t = _tile(T, 512, CHUNK)
    nb = T // t
    off = pw // V_HEAD

    def body(q_ref, k_ref, v_ref, o_ref, do_ref, lse_ref, dq_ref, dk_ref, dv_ref):
        j = pl.program_id(1)

        @pl.when(j == 0)
        def _():
            dq_ref[...] = jnp.zeros_like(dq_ref)

        kj = k_ref[...]
        vj = v_ref[...]

        def blk(i, carry, masked):
            dk, dv = carry
            rows = pl.ds(pl.multiple_of(i * t, t), t)
            qi = q_ref[rows, :]
            doi = do_ref[rows, :]
            oi = o_ref[rows, :].astype(F32)
            lsei = lse_ref[rows, :][:, :1]
            s = _dg(qi, kj, 1, 1)
            if masked:
                s = jnp.where(_diag_mask(t, t, 0), s, NEG_INF)
            p = jnp.exp(s - lsei)
            dv = dv + _dg(p, doi, 0, 0)
            dp = _dg(doi, vj, 1, 1)
            di = jnp.sum(doi * oi, axis=1, keepdims=True)
            ds = p * (dp - di)
            dk = dk + _dg(ds, qi, 0, 0)
            dq_ref[rows, :] += _dg(ds, kj, 1, 0)
            return dk, dv

        carry = blk(j, (jnp.zeros((t, HEAD_PAD), F32), jnp.zeros((t, V_HEAD), F32)), True)
        dk, dv = lax.fori_loop(j + 1, nb, lambda i, c: blk(i, c, False), carry)
        dk_ref[...] = dk
        dv_ref[...] = dv

    est = T * (HEAD_PAD * 2 + V_HEAD * 2 + V_HEAD * 4 + V_HEAD * 4 + HEAD_PAD * 4) + 10 * t * t * 4
    (dq, dk, dv), gathered = _host_call(
        body, name=name, grid=(H, nb),
        in_specs=[pl.BlockSpec((T, HEAD_PAD), lambda h, j: (0, h)),
                  pl.BlockSpec((t, HEAD_PAD), lambda h, j: (j, h)),
                  pl.BlockSpec((t, V_HEAD), lambda h, j: (j, h)),
                  pl.BlockSpec((T, V_HEAD), lambda h, j: (0, off + h)),
                  pl.BlockSpec((T, V_HEAD), lambda h, j: (0, off + h)),
                  pl.BlockSpec((T, V_HEAD), lambda h, j: (0, h))],
        out_specs=[pl.BlockSpec((T, HEAD_PAD), lambda h, j: (0, h)),
                   pl.BlockSpec((t, HEAD_PAD), lambda h, j: (j, h)),
                   pl.BlockSpec((t, V_HEAD), lambda h, j: (j, h))],
        out_shape=[_out((T, H * HEAD_PAD), F32),
                   _out((T, H * HEAD_PAD), F32),
                   _out((T, H * V_HEAD), F32)],
        args=[_hbm(v) for v in (qh, kh, vh, cat, dcat, lse)], sem=("arbitrary", "arbitrary"), est=est,
        ride=ride)
    return dq, dk, dv, gathered


def _mem_fn(q, k, v):
    hd = q.shape[1] // MEM_HEADS
    outs = []
    for h in range(MEM_HEADS):
        lo, hi = h * hd, (h + 1) * hd
        s = _bdot_nt(_cols(q, lo, hi), _cols(k, lo, hi)) * hd ** -0.5
        e = jnp.exp(s - lax.stop_gradient(jnp.max(s, axis=1, keepdims=True)))
        p = e / jnp.sum(e, axis=1, keepdims=True)
        outs.append(_bdot_nn(p, _cols(v, lo, hi)))
    return jnp.concatenate(outs, axis=1)


def _mem_fwd(q, k, v, *, name):
    T, D = q.shape
    tm = _tile(T, 256, 16)

    def fn(i, tv, pv):
        return (_mem_fn(tv[0], pv[0], pv[1]),), ()

    return _rowwise(fn, [q], [k, v], [(D, BF16)], tm=tm, name=name)[0]


def _mem_bwd(q, k, v, do, *, name):
    T, D = q.shape
    tm = _tile(T, 256, 16)

    def fn(i, tv, pv):
        _, vjp = jax.vjp(_mem_fn, tv[0], pv[0], pv[1])
        dq, dk, dv = vjp(tv[1].astype(F32))
        return (dq,), (dk, dv)

    return _rowwise(fn, [q, do], [k, v], [(D, BF16)], [(k.shape, F32), (v.shape, F32)], tm=tm, name=name)


def _loss_head(y, target, *, name):
    T, D = y.shape
    tm = _tile(T, 256, 16)

    def fn(i, tv, pv):
        err = tv[0] - tv[1]
        part = 0.5 * jnp.sum(jnp.sum(err * err, axis=1, keepdims=True) / D, axis=0, keepdims=True)
        return (err / D,), (jnp.broadcast_to(part, (8, LANE)),)

    return _rowwise(fn, [y, target], [], [(D, F32)], [((8, LANE), F32)], tm=tm, name=name)


def _adamw(w, g, m, v, *, name):
    shape = w.shape
    if w.ndim != 3:
        lead3 = (1, math.prod(shape[:-1]), shape[-1])
        return [o.reshape(shape) for o in _adamw(*[a.reshape(lead3) for a in (w, g, m, v)], name=name)]
    Lw, R, C = shape
    tr = _tile(R, 512, 8)
    b1c = 1.0 - ADAM_B1 ** ADAM_STEP
    b2c = 1.0 - ADAM_B2 ** ADAM_STEP

    def body(w_ref, g_ref, m_ref, v_ref, d_ref, mo_ref, vo_ref):
        gg = g_ref[...]
        mn = ADAM_B1 * m_ref[...] + (1.0 - ADAM_B1) * gg
        vn = ADAM_B2 * v_ref[...] + (1.0 - ADAM_B2) * (gg * gg)
        d_ref[...] = -ADAM_LR * ((mn / b1c) / (jnp.sqrt(vn / b2c) + ADAM_EPS) + ADAM_WD * w_ref[...])
        mo_ref[...] = mn
        vo_ref[...] = vn

    spec = pl.BlockSpec((None, tr, C), lambda l, i: (l, i, 0))
    return pl.pallas_call(
        body, name=name, grid=(Lw, R // tr),
        in_specs=[spec] * 4, out_specs=[spec] * 3,
        out_shape=[_out(shape, F32)] * 3,
        compiler_params=_params(("parallel", "parallel"), 7 * tr * C * 4),
    )(*[_hbm(a) for a in (w, g, m, v)])


def _pair_sum(core, gs, landed, offs, *, name):
    n = len(gs)
    _, R, C = landed.shape
    rows = [g.shape[0] // N_DEV for g in gs]

    def body(core_ref, *refs):
        g_refs, l_ref, o_ref = refs[:n], refs[n], refs[n + 1]
        for g_ref, off, r in zip(g_refs, offs, rows):
            o_ref[off:off + r, :] = (g_ref[...].astype(F32) + l_ref[off:off + r, :].astype(F32)).astype(o_ref.dtype)

    slab = pl.BlockSpec((None, R, C), lambda p, core_ref: (p, 0, 0))
    own = [pl.BlockSpec((r, C), lambda p, core_ref: (2 * p + core_ref[0], 0)) for r in rows]
    return pl.pallas_call(
        body, name=name,
        grid_spec=pltpu.PrefetchScalarGridSpec(
            num_scalar_prefetch=1, grid=(4,), in_specs=own + [slab], out_specs=slab),
        out_shape=_out(landed.shape, landed.dtype),
        input_output_aliases={n + 1: 0},
        compiler_params=_params(("arbitrary",), 3 * R * C * 2 + R * C * 8),
    )(core, *[_hbm(g) for g in gs], _hbm(landed))


def _quad_sum(chip, part, gathered, used, *, name):
    C = part.shape[2]
    R = used
    tr = _tile(R, 256, 16)

    def body(chip_ref, own_ref, a_ref, b_ref, c_ref, o_ref):
        o_ref[...] = ((own_ref[...].astype(F32) + a_ref[...].astype(F32)) + b_ref[...].astype(F32)) \
            + c_ref[...].astype(F32)

    def other(k):
        return pl.BlockSpec((None, tr, C), lambda i, chip_ref: (chip_ref[0] ^ k, i, 0))

    return pl.pallas_call(
        body, name=name,
        grid_spec=pltpu.PrefetchScalarGridSpec(
            num_scalar_prefetch=1, grid=(R // tr,),
            in_specs=[pl.BlockSpec((None, tr, C), lambda i, chip_ref: (chip_ref[0], i, 0)),
                      other(1), other(2), other(3)],
            out_specs=pl.BlockSpec((tr, C), lambda i, chip_ref: (i, 0))),
        out_shape=_out((R, C), F32),
        compiler_params=_params(("arbitrary",), 8 * tr * C * 4),
    )(chip, _hbm(part), _hbm(gathered), _hbm(gathered), _hbm(gathered))


def _place():
    x, y, c = lax.axis_index("x"), lax.axis_index("y"), lax.axis_index("c")
    return x, y, c


ANY = pl.BlockSpec(memory_space=pl.ANY)


class _Gather:
    def __init__(self, shards):
        self.shards = list(shards)
        self.n = len(self.shards)
        self.out_shape = [_out((s.shape[0], N_DEV * s.shape[1], s.shape[2]), s.dtype)
                          for s in self.shards]
        self.scratch = [pltpu.SemaphoreType.DMA((7 * self.n,)), pltpu.SemaphoreType.DMA((7 * self.n,)),
                        pltpu.SemaphoreType.DMA((self.n,))]
        self.operands = [_hbm(s) for s in self.shards]

    def _bind(self, refs):
        n = self.n
        ins, outs = refs[:n], refs[n:2 * n]
        send_sems, recv_sems, local_sems = refs[2 * n:]
        x, y, c = _place()
        me, sib = (x, y, c), (x, y, 1 - c)
        chips = [(1 - x, y), (x, 1 - y), (1 - x, 1 - y)]

        def rows(w, p):
            r = self.shards[w].shape[1]
            idx = 4 * p[0] + 2 * p[1] + p[2]
            return outs[w].at[:, pl.ds(pl.multiple_of(idx * r, 8), r), :]

        def copy(w, k, block, to, src=None):
            return pltpu.make_async_remote_copy(
                src_ref=rows(w, block) if src is None else src, dst_ref=rows(w, block),
                send_sem=send_sems.at[w * 7 + k], recv_sem=recv_sems.at[w * 7 + k],
                device_id=to, device_id_type=MESH)

        def mine():
            return [pltpu.make_async_copy(ins[w], rows(w, me), local_sems.at[w]) for w in range(n)]

        def first():
            out = []
            for w in range(n):
                out.append(copy(w, 0, me, sib, src=ins[w]))
                out += [copy(w, 1 + j, me, (*chip, c), src=ins[w]) for j, chip in enumerate(chips)]
            return out

        def passed():
            return [copy(w, 4 + j, (*chip, c), sib) for j, chip in enumerate(chips) for w in range(n)]

        def landed():
            return [copy(w, 1 + j, (*chip, c), me) for j, chip in enumerate(chips) for w in range(n)]

        def last():
            out = []
            for w in range(n):
                out.append(copy(w, 0, sib, me))
                out += [copy(w, 4 + j, (*chip, 1 - c), me) for j, chip in enumerate(chips)]
            return out

        return mine, first, landed, passed, last

    def start(self, refs):
        mine, first, _, _, _ = self._bind(refs)
        for cp in mine() + first():
            cp.start()

    def forward(self, refs):
        _, _, landed, passed, _ = self._bind(refs)
        for arrived, fwd in zip(landed(), passed()):
            arrived.wait_recv()
            fwd.start()

    def finish(self, refs):
        mine, first, _, passed, last = self._bind(refs)
        for cp in last():
            cp.wait_recv()
        for cp in first() + passed():
            cp.wait_send()
        for cp in mine():
            cp.wait()


class _ChipExchange:
    def __init__(self, parts, used):
        self.ncl = len(parts)
        self.used = list(used)
        self.out_shape = [_out(p.shape, p.dtype) for p in parts]
        self.scratch = [pltpu.SemaphoreType.DMA((3 * self.ncl,)), pltpu.SemaphoreType.DMA((3 * self.ncl,))]
        self.operands = [_hbm(p) for p in parts]
        self.n = self.ncl

    def _bind(self, refs):
        ncl = self.ncl
        ins, outs = refs[:ncl], refs[ncl:2 * ncl]
        send_sems, recv_sems = refs[2 * ncl:]
        x, y, c = _place()
        chips = [(1 - x, y), (x, 1 - y), (1 - x, 1 - y)]
        here = 2 * x + y

        def copies(outgoing):
            out = []
            for k in range(ncl):
                rows = pl.ds(0, self.used[k])
                for j, (cx, cy) in enumerate(chips):
                    there = 2 * cx + cy
                    src, dst = (there, here) if outgoing else (here, there)
                    out.append(pltpu.make_async_remote_copy(
                        src_ref=ins[k].at[src, rows, :], dst_ref=outs[k].at[dst, rows, :],
                        send_sem=send_sems.at[3 * k + j], recv_sem=recv_sems.at[3 * k + j],
                        device_id=(cx, cy, c), device_id_type=MESH))
            return out

        return copies

    def start(self, refs):
        for cp in self._bind(refs)(True):
            cp.start()

    def forward(self, refs):
        pass

    def finish(self, refs):
        copies = self._bind(refs)
        for cp in copies(False):
            cp.wait_recv()
        for cp in copies(True):
            cp.wait_send()


class _Both:
    def __init__(self, members):
        self.members = list(members)
        self.n = sum(m.n for m in self.members)
        self.out_shape = [s for m in self.members for s in m.out_shape]
        self.scratch = [s for m in self.members for s in m.scratch]
        self.operands = [o for m in self.members for o in m.operands]

    def split(self, arrays):
        out, a = [], 0
        for m in self.members:
            out.append(list(arrays[a:a + m.n]))
            a += m.n
        return out

    def _refs(self, refs):
        ins, outs = self.split(refs[:self.n]), self.split(refs[self.n:2 * self.n])
        scr, b = [], 2 * self.n
        for m in self.members:
            scr.append(list(refs[b:b + len(m.scratch)]))
            b += len(m.scratch)
        return [(*i, *o, *s) for i, o, s in zip(ins, outs, scr)]

    def start(self, refs):
        for m, r in zip(self.members, self._refs(refs)):
            m.start(r)

    def forward(self, refs):
        for m, r in zip(self.members, self._refs(refs)):
            m.forward(r)

    def finish(self, refs):
        for m, r in zip(self.members, self._refs(refs)):
            m.finish(r)


def _exchange_alone(ex, *, name):
    def body(*refs):
        ex.start(refs)
        ex.forward(refs)
        ex.finish(refs)

    return pl.pallas_call(
        body, name=name, in_specs=[ANY] * ex.n, out_specs=[ANY] * ex.n,
        out_shape=ex.out_shape, scratch_shapes=ex.scratch,
    )(*ex.operands)


def _host_call(body, *, name, grid, in_specs, out_specs, out_shape, args, sem, est, ride=None, scratch=()):
    scratch = list(scratch)
    if ride is None:
        outs = pl.pallas_call(body, name=name, grid=grid, in_specs=in_specs, out_specs=out_specs,
                              out_shape=out_shape, scratch_shapes=scratch,
                              compiler_params=_params(sem, est))(*args)
        return list(outs), []
    n_in, n_out, n, n_scr = len(in_specs), len(out_specs), ride.n, len(scratch)

    def full(*refs):
        ins, rin = refs[:n_in], refs[n_in:n_in + n]
        outs, rout = refs[n_in + n:n_in + n + n_out], refs[n_in + n + n_out:n_in + 2 * n + n_out]
        own = refs[n_in + 2 * n + n_out:n_in + 2 * n + n_out + n_scr]
        rrefs = (*rin, *rout, *refs[n_in + 2 * n + n_out + n_scr:])
        step, total = _ride(ride, rrefs, grid)
        body(*ins, *outs, *own)
        _ride_end(ride, rrefs, step, total)

    outs = pl.pallas_call(
        full, name=name, grid=grid,
        in_specs=list(in_specs) + [ANY] * n, out_specs=list(out_specs) + [ANY] * n,
        out_shape=list(out_shape) + ride.out_shape, scratch_shapes=scratch + ride.scratch,
        compiler_params=_params(("arbitrary",) * len(grid), est),
    )(*args, *ride.operands)
    return list(outs[:n_out]), list(outs[n_out:])


def _ride(ex, refs, grid):
    total = math.prod(grid)
    step = pl.program_id(0)
    for axis in range(1, len(grid)):
        step = step * grid[axis] + pl.program_id(axis)
    pl.when(step == 0)(lambda: ex.start(refs))
    return step, total


def _ride_end(ex, refs, step, total):
    pl.when(step == (3 * total) // 4)(lambda: ex.forward(refs))
    pl.when(step == total - 1)(lambda: ex.finish(refs))


def _class_layout(grads, classes):
    used = [0] * len(set(classes))
    offs = []
    for g, cl in zip(grads, classes):
        offs.append(used[cl])
        used[cl] += g.shape[0] // N_DEV
    return offs, used


def _rs_to_sibling(grads, classes, *, name):
    n = len(grads)
    offs, used = _class_layout(grads, classes)
    heights = used
    ncl = len(heights)
    cols = [next(g.shape[1] for g, cl in zip(grads, classes) if cl == k) for k in range(ncl)]

    def body(*refs):
        gs, land = refs[:n], refs[n:n + ncl]
        send_sems, recv_sems = refs[n + ncl:]
        x, y, c = _place()
        sib = (x, y, 1 - c)
        for p in range(4):
            for w in range(n):
                r = grads[w].shape[0] // N_DEV
                cl = classes[w]
                there = gs[w].at[pl.ds(pl.multiple_of((2 * p + 1 - c) * r, 8), r), :]
                pltpu.make_async_remote_copy(
                    src_ref=there, dst_ref=land[cl].at[p, pl.ds(offs[w], r), :],
                    send_sem=send_sems.at[cl * 4 + p], recv_sem=recv_sems.at[cl * 4 + p],
                    device_id=sib, device_id_type=MESH).start()
        for cl in range(ncl):
            for p in range(4):
                rows_used = land[cl].at[p, pl.ds(0, used[cl]), :]
                slab = pltpu.make_async_remote_copy(
                    src_ref=rows_used, dst_ref=rows_used,
                    send_sem=send_sems.at[cl * 4 + p], recv_sem=recv_sems.at[cl * 4 + p],
                    device_id=sib, device_id_type=MESH)
                slab.wait_send()
                slab.wait_recv()

    return pl.pallas_call(
        body, name=name,
        in_specs=[ANY] * n, out_specs=[ANY] * ncl,
        out_shape=[_out((4, heights[k], cols[k]), BF16) for k in range(ncl)],
        scratch_shapes=[pltpu.SemaphoreType.DMA((4 * ncl,))] * 2,
    )(*[_hbm(g) for g in grads])


def _all_reduce_small(v, *, name):
    R = v.shape[0]

    def body(v_ref, o_ref, buf, send_sems, recv_sems):
        x, y, c = _place()
        me = 4 * x + 2 * y + c
        buf[me] = v_ref[...]
        copies = []
        for k in range(1, N_DEV):
            fx, fy, fc = (k >> 2) & 1, (k >> 1) & 1, k & 1
            to = (x ^ fx, y ^ fy, c ^ fc)
            cp = pltpu.make_async_remote_copy(
                src_ref=v_ref, dst_ref=buf.at[me],
                send_sem=send_sems.at[k - 1], recv_sem=recv_sems.at[k - 1],
                device_id=to, device_id_type=MESH)
            cp.start()
            copies.append(cp)
        for k in range(1, N_DEV):
            fx, fy, fc = (k >> 2) & 1, (k >> 1) & 1, k & 1
            frm = 4 * (x ^ fx) + 2 * (y ^ fy) + (c ^ fc)
            pltpu.make_async_remote_copy(
                src_ref=v_ref, dst_ref=buf.at[frm],
                send_sem=send_sems.at[k - 1], recv_sem=recv_sems.at[k - 1],
                device_id=(x ^ fx, y ^ fy, c ^ fc), device_id_type=MESH).wait_recv()
        for cp in copies:
            cp.wait_send()
        acc = buf[0]
        for d in range(1, N_DEV):
            acc = acc + buf[d]
        o_ref[...] = acc

    vm = pl.BlockSpec(memory_space=pltpu.VMEM)
    return pl.pallas_call(
        body, name=name, in_specs=[vm], out_specs=vm,
        out_shape=jax.ShapeDtypeStruct((R, LANE), F32),
        scratch_shapes=[pltpu.VMEM((N_DEV, R, LANE), F32),
                        pltpu.SemaphoreType.DMA((N_DEV - 1,)), pltpu.SemaphoreType.DMA((N_DEV - 1,))],
        compiler_params=pltpu.CompilerParams(vmem_limit_bytes=VMEM_FLOOR),
    )(v)


def _rope_tables(positions):
    half = QK_ROPE // 2
    inv_freq = ROPE_BASE ** (-jnp.arange(half, dtype=F32) / half)
    ang = positions.astype(F32)[:, None] * inv_freq
    cos, sin = jnp.cos(ang), jnp.sin(ang)
    z = jnp.zeros_like(cos)
    z2 = jnp.zeros((positions.shape[0], LANE - QK_ROPE), F32)
    rc = jnp.concatenate([cos, cos, z2], axis=1)
    rs1 = jnp.concatenate([-sin, z, z2], axis=1)
    rs2 = jnp.concatenate([z, sin, z2], axis=1)
    return rc, rs1, rs2


def _block_diag(pool_w):
    G, pg, _ = pool_w.shape
    out = jnp.zeros((G * pg, G * pg), pool_w.dtype)
    for g in range(G):
        out = lax.dynamic_update_slice(out, pool_w[g], (g * pg, g * pg))
    return out


def kernel(x, mem, positions, ln_g, ln_b, ffn1_w13, ffn1_w2, w_in, pool_w, pool_scale, q_norm_g, w_uq, kv_norm_g, w_ukv, w_out, mem_wq, mem_wkv, mem_wo, ffn2_w13, ffn2_w2, loss_target, m_ln_g, m_ln_b, m_ffn1_w13, m_ffn1_w2, m_w_in, m_pool_w, m_pool_scale, m_q_norm_g, m_w_uq, m_kv_norm_g, m_w_ukv, m_w_out, m_mem_wq, m_mem_wkv, m_mem_wo, m_ffn2_w13, m_ffn2_w2, v_ln_g, v_ln_b, v_ffn1_w13, v_ffn1_w2, v_w_in, v_pool_w, v_pool_scale, v_q_norm_g, v_w_uq, v_kv_norm_g, v_w_ukv, v_w_out, v_mem_wq, v_mem_wkv, v_mem_wo, v_ffn2_w13, v_ffn2_w2):
    L = ln_g.shape[0]
    T, D = x.shape[1], x.shape[2]
    F = ffn1_w2.shape[1] * N_DEV
    PW = D // 4
    H = (D - PW) // V_HEAD
    DIN = w_in.shape[2]
    DINP = PW + Q_LORA + KV_LORA + LANE
    QW = QK_NOPE + QK_ROPE
    alpha = (2 * L) ** 0.25
    x2d = x.reshape(T, D)
    memb = mem.reshape(mem.shape[1], D).astype(BF16)
    target = loss_target.reshape(T, D)
    tabs = _rope_tables(positions.reshape(T))

    def shards_of(l):
        return dict(
            w13a=ffn1_w13[l].T[None].astype(BF16),
            w13b=ffn2_w13[l].T[None].astype(BF16),
            w2a=ffn1_w2[l][None].astype(BF16),
            w2b=ffn2_w2[l][None].astype(BF16),
            wsq=jnp.stack([w_out[l], mem_wq[l], mem_wo[l]]).astype(BF16),
            wkvT=mem_wkv[l].T[None].astype(BF16),
            winp=jnp.pad(w_in[l], ((0, 0), (0, DINP - DIN)))[None].astype(BF16),
            wuqT=w_uq[l].T[None].astype(BF16),
            wukvT=w_ukv[l].T[None].astype(BF16),
        )

    SMALL = ("winp", "wuqT", "wukvT")
    shards = [shards_of(l) for l in range(L)]
    W = [dict() for _ in range(L)]

    def rider(spec):
        return _Gather([shards[l][n] for l, n in spec]) if spec else None

    def arrived(spec, arrays):
        for (l, n), a in zip(spec, arrays):
            if n in ("w13a", "w13b"):
                a = _interleave(a, 1)
            elif n == "wuqT":
                a = jnp.pad(a.reshape(H, QW, Q_LORA), ((0, 0), (0, HEAD_PAD - QW), (0, 0)))
                a = a.reshape(1, H * HEAD_PAD, Q_LORA)
            elif n == "ln":
                a = jnp.moveaxis(a.reshape(N_DEV, 2, L, 4, D // N_DEV), 0, 3).reshape(2, L, 4, D)
                LN["g"], LN["b"] = a[0], a[1]
            W[l][n] = a

    LN = {}
    shards[0]["ln"] = jnp.concatenate([ln_g.reshape(1, 4 * L, -1), ln_b.reshape(1, 4 * L, -1)], axis=1)
    spec0 = [(0, "w13a")]
    arrived(spec0, _exchange_alone(rider(spec0), name="ag_first"))
    wbd = [_block_diag(pool_w[l]).astype(BF16) for l in range(L)]

    def ffn_fwd(l, which, xres, xb, k, spec):
        ab = "ab"[which]
        h13, a, rode = _ffn_up(xb, W[l]["w13" + ab], 0, name=f"l{l}_ffn{which}_up", ride=rider(spec))
        arrived(spec, rode)
        y, xo, xob = _mm_ln(a, W[l]["w2" + ab], 0, xres, LN["g"][l,k:k + 1], LN["b"][l,k:k + 1], alpha=alpha, s=0.5,
                            name=f"l{l}_ffn{which}_y_ln{k}")
        return dict(xres=xres, xb=xb, h13=h13, a=a, y=y), xo, xob

    saved = []
    xres, xb = x2d, x2d.astype(BF16)
    for l in range(L):
        sv = {}
        more = l + 1 < L
        Wl = W[l]
        spec = ([(0, "w2a"), (0, "ln"), *[(0, n) for n in SMALL], (0, "wkvT")] if l == 0
                else [(l, "wsq"), (l, "wkvT")])
        sv["ffn1"], x1, x1b = ffn_fwd(l, 0, xres, xb, 0, spec)
        hin = _mm(x1b, Wl["winp"], lead=0, name=f"l{l}_hin")
        pscale = pool_scale[l].reshape(1, PW)
        gq, gkv = q_norm_g[l].reshape(1, Q_LORA), kv_norm_g[l].reshape(1, KV_LORA)
        cqn, ckvn, kpe = _norms_fwd(hin, gq, gkv, pw=PW, name=f"l{l}_norms")
        qraw = _mm(cqn, Wl["wuqT"], lead=0, tb=True, name=f"l{l}_qraw")
        kv = _mm(ckvn, Wl["wukvT"], lead=0, tb=True, name=f"l{l}_kv")
        qh, kh, vh = _heads_fwd(qraw, kv, kpe, tabs, H=H, name=f"l{l}_heads")
        spec = [(l, "w13b"), (l, "w2b")] + ([(0, "wsq")] if l == 0 else []) + ([(l + 1, "w13a")] if more else [])
        cat, lse, rode = _flash_fwd(qh, kh, vh, H=H, pw=PW, name=f"l{l}_flash", ride=rider(spec))
        arrived(spec, rode)
        cat = _pool_fwd(hin, wbd[l], pscale, cat, name=f"l{l}_pool")
        ymix, x2, x2b = _mm_ln(cat, Wl["wsq"], 0, x1, LN["g"][l,1:2], LN["b"][l,1:2], alpha=alpha, s=1.0,
                               name=f"l{l}_ymix_ln1")
        qm = _mm(x2b, Wl["wsq"], lead=1, out_dtype=BF16, name=f"l{l}_qm")
        kvm = _mm(memb, Wl["wkvT"], lead=0, tb=True, name=f"l{l}_kvm")
        km, vm = kvm[:, :D], kvm[:, D:]
        om = _mem_fwd(qm, km, vm, name=f"l{l}_memattn")
        ymem, x3, x3b = _mm_ln(om, Wl["wsq"], 2, x2, LN["g"][l,2:3], LN["b"][l,2:3], alpha=alpha, s=1.0,
                               name=f"l{l}_ymem_ln2")
        spec = [(l + 1, n) for n in ("w2a", *SMALL)] if more else []
        sv["ffn2"], x4, x4b = ffn_fwd(l, 1, x3, x3b, 3, spec)
        sv.update(x1=x1, x1b=x1b, hin=hin, pscale=pscale, gq=gq, gkv=gkv, cqn=cqn, ckvn=ckvn,
                  qh=qh, kh=kh, vh=vh, lse=lse, cat=cat, ymix=ymix, x2=x2, x2b=x2b, qm=qm, km=km, vm=vm,
                  om=om, ymem=ymem)
        saved.append(sv)
        xres, xb = x4, x4b

    dx, loss_blk = _loss_head(xres, target, name="loss_head")
    loss = lax.psum(loss_blk[0, 0], ("x", "y", "c"))

    gW = {}
    gS = {}

    def ffn_bwd(l, which, sv, dx, k, ride):
        tag = f"l{l}_ffn{which}"
        dxres, dyb, dg, db = _ln_bwd(sv["xres"], sv["y"], LN["g"][l,k:k + 1], dx,
                                     alpha=alpha, s=0.5, name=f"l{l}_ln{k}_bwd")
        gW[("w2", which, l)] = _mm(sv["a"], dyb, ta=True, out_dtype=BF16, name=f"{tag}_dw2", tn=D)
        dh, rode = _ffn_down_bwd(dyb, W[l]["w2" + "ab"[which]], 0, sv["h13"], name=f"{tag}_dh", ride=ride)
        dw13 = _mm(dh, sv["xb"], ta=True, out_dtype=BF16, name=f"{tag}_dw13", tn=D)
        gW[("w13", which, l)] = _deinterleave(dw13, 0)
        gS[("ln_g", l, k)], gS[("ln_b", l, k)] = dg, db
        last = rs_first_level(l, "c") if (l == 0 and which == 0) else None
        dxn = _mm(dh, W[l]["w13" + "ab"[which]], lead=0, add=dxres, name=f"{tag}_dx", tn=D,
                  ride=last["ex"] if last else None)
        if last:
            dxn, got = dxn
            rs_last_level(last, got)
        return dxn, rode

    core = lax.axis_index("c").astype(jnp.int32).reshape(1)
    chip = (2 * lax.axis_index("x") + lax.axis_index("y")).astype(jnp.int32).reshape(1)
    gsh = {}

    def rs_first_level(l, group):
        keys, classes = {
            "a": ([("w13", 1, l), ("w2", 1, l), ("mem_wkv", l), ("mem_wq", l), ("mem_wo", l)], [0] * 5),
            "b": ([("w_out", l), ("w_in", l), ("w_uq", l), ("w_ukv", l)], [0, 1, 2, 3]),
            "c": ([("w13", 0, l), ("w2", 0, l)], [0, 0]),
        }[group]
        tag = f"l{l}{group}"
        garrs = []
        for key in keys:
            g = gW[key]
            if key[0] == "w_uq":
                g = g.reshape(H, HEAD_PAD, Q_LORA)[:, :QW, :].reshape(H * QW, Q_LORA)
            garrs.append(g)
        offs, used = _class_layout(garrs, classes)
        parts = list(_rs_to_sibling(garrs, classes, name=f"{tag}_rs_sibling"))
        for cl in range(len(parts)):
            mine = [w for w, c in enumerate(classes) if c == cl]
            parts[cl] = _pair_sum(core, [garrs[w] for w in mine], parts[cl], [offs[w] for w in mine],
                                  name=f"{tag}_rs_pair_sum{cl}")
        return dict(tag=tag, keys=keys, garrs=garrs, classes=classes, offs=offs, used=used, parts=parts,
                    ex=_ChipExchange(parts, used))

    def rs_last_level(st, gathered):
        sums = [_quad_sum(chip, p, a, u, name=f"{st['tag']}_rs_quad_sum{k}")
                for k, (p, a, u) in enumerate(zip(st["parts"], gathered, st["used"]))]
        for key, g, cl, off in zip(st["keys"], st["garrs"], st["classes"], st["offs"]):
            gsh[key] = sums[cl][off:off + g.shape[0] // N_DEV, :]

    above = None
    for l in reversed(range(L)):
        sv = saved[l]
        Wl = W[l]
        dx, _ = ffn_bwd(l, 1, sv["ffn2"], dx, 3, None)
        dxres, dyb, dg, db = _ln_bwd(sv["x2"], sv["ymem"], LN["g"][l,2:3], dx,
                                     alpha=alpha, s=1.0, name=f"l{l}_ln2_bwd")
        gS[("ln_g", l, 2)], gS[("ln_b", l, 2)] = dg, db
        dom = _mm(dyb, Wl["wsq"], lead=2, tb=True, out_dtype=BF16, name=f"l{l}_dom")
        gW[("mem_wo", l)] = _mm(sv["om"], dyb, ta=True, out_dtype=BF16, name=f"l{l}_dwo", tn=D)
        dqm, dkm, dvm = _mem_bwd(sv["qm"], sv["km"], sv["vm"], dom, name=f"l{l}_memattn_bwd")
        dx = _mm(dqm, Wl["wsq"], lead=1, tb=True, add=dxres, name=f"l{l}_dx2", tn=D)
        gW[("mem_wq", l)] = _mm(sv["x2b"], dqm, ta=True, out_dtype=BF16, name=f"l{l}_dwq", tn=D)
        dkvm = jnp.concatenate([dkm, dvm], axis=1).astype(BF16)
        gW[("mem_wkv", l)] = _mm(dkvm, memb, ta=True, out_dtype=BF16, name=f"l{l}_dwkv", tn=D)
        dxres, dyb, dg, db = _ln_bwd(sv["x1"], sv["ymix"], LN["g"][l,1:2], dx,
                                     alpha=alpha, s=1.0, name=f"l{l}_ln1_bwd")
        gS[("ln_g", l, 1)], gS[("ln_b", l, 1)] = dg, db
        dcat = _mm(dyb, Wl["wsq"], lead=0, tb=True, name=f"l{l}_dcat", tn=D)
        gW[("w_out", l)] = _mm(sv["cat"], dyb, ta=True, out_dtype=BF16, name=f"l{l}_dwout", tn=D)
        riding = [rs_first_level(l, "a")] + ([above] if above else [])
        both = _Both([st["ex"] for st in riding])
        dqh, dkh, dvh, rode = _flash_bwd(sv["qh"], sv["kh"], sv["vh"], sv["cat"], dcat, sv["lse"], H=H, pw=PW,
                                         name=f"l{l}_flash_bwd", ride=both)
        for st, got in zip(riding, both.split(rode)):
            rs_last_level(st, got)
        dqraw, dkv, dkpe = _heads_bwd(dqh, dkh, dvh, tabs, H=H, name=f"l{l}_heads_bwd")
        dcq = _mm(dqraw, Wl["wuqT"], lead=0, name=f"l{l}_dcq")
        gW[("w_uq", l)] = _mm(dqraw, sv["cqn"], ta=True, out_dtype=BF16, name=f"l{l}_dwuq")
        dckv = _mm(dkv, Wl["wukvT"], lead=0, name=f"l{l}_dckv")
        gW[("w_ukv", l)] = _mm(dkv, sv["ckvn"], ta=True, out_dtype=BF16, name=f"l{l}_dwukv")
        du, dwbd, dps = _pool_bwd(sv["hin"], dcat, wbd[l], sv["pscale"], name=f"l{l}_pool_bwd")
        dhin, dgq, dgkv = _norms_bwd(sv["hin"], sv["gq"], sv["gkv"], dcq, dckv, dkpe, du, pw=PW,
                                     name=f"l{l}_norms_bwd")
        pg = PW // len(POOL_WINDOWS)
        gS[("pool_w", l)] = jnp.stack([dwbd[g * pg:(g + 1) * pg, g * pg:(g + 1) * pg]
                                       for g in range(len(POOL_WINDOWS))])
        gS[("pool_scale", l)], gS[("q_norm_g", l)], gS[("kv_norm_g", l)] = dps, dgq, dgkv
        dx = _mm(dhin, Wl["winp"], lead=0, tb=True, add=dxres, name=f"l{l}_dx1", tn=D)
        gW[("w_in", l)] = _mm(sv["x1b"], dhin, ta=True, out_dtype=BF16, name=f"l{l}_dwin", tn=DINP)
        heads = rs_first_level(l, "b")
        dx, rode = ffn_bwd(l, 0, sv["ffn1"], dx, 0, heads["ex"])
        rs_last_level(heads, rode)
        above = rs_first_level(l, "c") if l > 0 else None
    grad_x = dx.reshape(1, T, D)

    small_keys = []
    for l in range(L):
        small_keys += [("pool_w", l), ("pool_scale", l), ("q_norm_g", l), ("kv_norm_g", l)]
        small_keys += [("ln_g", l, k) for k in range(4)] + [("ln_b", l, k) for k in range(4)]
    flat = jnp.concatenate([gS[k].reshape(-1) for k in small_keys])
    n_small = flat.shape[0]
    rows = -(-n_small // (8 * LANE)) * 8
    flat = jnp.pad(flat, (0, rows * LANE - n_small)).reshape(rows, LANE)
    red = _all_reduce_small(flat, name="ar_small").reshape(-1)
    gsm, pos = {}, 0
    for k in small_keys:
        size = math.prod(gS[k].shape)
        gsm[k] = red[pos:pos + size].reshape(gS[k].shape)
        pos += size

    me = 4 * lax.axis_index("x") + 2 * lax.axis_index("y") + lax.axis_index("c")
    dsh = D // N_DEV
    stack = lambda f: jnp.stack([f(l) for l in range(L)])
    g_ln_g = stack(lambda l: jnp.concatenate([gsm[("ln_g", l, k)] for k in range(4)], axis=0))
    g_ln_b = stack(lambda l: jnp.concatenate([gsm[("ln_b", l, k)] for k in range(4)], axis=0))
    grads = {
        "ln_g": lax.dynamic_slice_in_dim(g_ln_g, me * dsh, dsh, axis=2),
        "ln_b": lax.dynamic_slice_in_dim(g_ln_b, me * dsh, dsh, axis=2),
        "ffn1_w13": stack(lambda l: gsh[("w13", 0, l)].T),
        "ffn1_w2": stack(lambda l: gsh[("w2", 0, l)]),
        "w_in": stack(lambda l: gsh[("w_in", l)][:, :DIN]),
        "pool_w": stack(lambda l: gsm[("pool_w", l)]),
        "pool_scale": stack(lambda l: gsm[("pool_scale", l)].reshape(PW)),
        "q_norm_g": stack(lambda l: gsm[("q_norm_g", l)].reshape(Q_LORA)),
        "w_uq": stack(lambda l: gsh[("w_uq", l)].T),
        "kv_norm_g": stack(lambda l: gsm[("kv_norm_g", l)].reshape(KV_LORA)),
        "w_ukv": stack(lambda l: gsh[("w_ukv", l)].T),
        "w_out": stack(lambda l: gsh[("w_out", l)]),
        "mem_wq": stack(lambda l: gsh[("mem_wq", l)]),
        "mem_wkv": stack(lambda l: gsh[("mem_wkv", l)].T),
        "mem_wo": stack(lambda l: gsh[("mem_wo", l)]),
        "ffn2_w13": stack(lambda l: gsh[("w13", 1, l)].T),
        "ffn2_w2": stack(lambda l: gsh[("w2", 1, l)]),
    }

    names = ["ln_g", "ln_b", "ffn1_w13", "ffn1_w2", "w_in", "pool_w", "pool_scale", "q_norm_g", "w_uq",
             "kv_norm_g", "w_ukv", "w_out", "mem_wq", "mem_wkv", "mem_wo", "ffn2_w13", "ffn2_w2"]
    weights = dict(ln_g=ln_g, ln_b=ln_b, ffn1_w13=ffn1_w13, ffn1_w2=ffn1_w2, w_in=w_in, pool_w=pool_w,
                   pool_scale=pool_scale, q_norm_g=q_norm_g, w_uq=w_uq, kv_norm_g=kv_norm_g, w_ukv=w_ukv,
                   w_out=w_out, mem_wq=mem_wq, mem_wkv=mem_wkv, mem_wo=mem_wo, ffn2_w13=ffn2_w13,
                   ffn2_w2=ffn2_w2)
    ms = dict(ln_g=m_ln_g, ln_b=m_ln_b, ffn1_w13=m_ffn1_w13, ffn1_w2=m_ffn1_w2, w_in=m_w_in, pool_w=m_pool_w,
              pool_scale=m_pool_scale, q_norm_g=m_q_norm_g, w_uq=m_w_uq, kv_norm_g=m_kv_norm_g,
              w_ukv=m_w_ukv, w_out=m_w_out, mem_wq=m_mem_wq, mem_wkv=m_mem_wkv, mem_wo=m_mem_wo,
              ffn2_w13=m_ffn2_w13, ffn2_w2=m_ffn2_w2)
    vs = dict(ln_g=v_ln_g, ln_b=v_ln_b, ffn1_w13=v_ffn1_w13, ffn1_w2=v_ffn1_w2, w_in=v_w_in, pool_w=v_pool_w,
              pool_scale=v_pool_scale, q_norm_g=v_q_norm_g, w_uq=v_w_uq, kv_norm_g=v_kv_norm_g,
              w_ukv=v_w_ukv, w_out=v_w_out, mem_wq=v_mem_wq, mem_wkv=v_mem_wkv, mem_wo=v_mem_wo,
              ffn2_w13=v_ffn2_w13, ffn2_w2=v_ffn2_w2)
    deltas, new_m, new_v = [], [], []
    for nme in names:
        d, mn, vn = _adamw(weights[nme], grads[nme], ms[nme], vs[nme], name=f"adamw_{nme}")
        deltas.append(d)
        new_m.append(mn)
        new_v.append(vn)
    return (loss, grad_x, *[grads[nme] for nme in names], *deltas, *new_m, *new_v)
```

```python
import functools
import math

import jax
import jax.numpy as jnp
from jax import lax
from jax.experimental import pallas as pl
from jax.experimental.pallas import tpu as pltpu

F32 = jnp.float32
BF16 = jnp.bfloat16
MESH = pl.DeviceIdType.MESH

CHUNK = 64
MEM_HEADS = 4
POOL_WINDOWS = (2, 4, 8, 16)
QK_NOPE = 128
QK_ROPE = 64
V_HEAD = 128
Q_LORA = 256
KV_LORA = 128
ROPE_BASE = 10000.0
LN_EPS = 1e-5
RMS_EPS = 1e-6
NEG_INF = -1e30
ADAM_LR = 0.001
ADAM_B1 = 0.9
ADAM_B2 = 0.999
ADAM_EPS = 1e-08
ADAM_WD = 0.01
ADAM_STEP = 10

N_DEV = 8
LANE = 128
HEAD_PAD = 2 * LANE
POOL_HALO = 16
VMEM_CAP = 56 * 1024 * 1024
VMEM_FLOOR = 32 * 1024 * 1024


def _tile(n, pref, mult):
    t = (min(pref, n) // mult) * mult
    while t >= mult:
        if n % t == 0:
            return t
        t -= mult
    return n


def _params(sem, est_bytes):
    limit = int(min(max(2 * est_bytes + (8 << 20), VMEM_FLOOR), VMEM_CAP))
    return pltpu.CompilerParams(dimension_semantics=sem, vmem_limit_bytes=limit)


def _nbytes(shape, dtype):
    return math.prod(shape) * jnp.dtype(dtype).itemsize


def _hbm(x):
    return pltpu.with_memory_space_constraint(x, pltpu.HBM)


def _out(shape, dtype):
    return pltpu.HBM(tuple(shape), dtype)


def _dg(a, b, ca, cb):
    return lax.dot_general(a.astype(BF16), b.astype(BF16), (((ca,), (cb,)), ((), ())),
                           preferred_element_type=F32)


@jax.custom_vjp
def _bdot_nn(a, b):
    return _dg(a, b, 1, 0)


def _bdot_nn_fwd(a, b):
    return _dg(a, b, 1, 0), (a, b)


def _bdot_nn_bwd(res, ct):
    a, b = res
    return _dg(ct, b, 1, 1).astype(a.dtype), _dg(a, ct, 0, 0).astype(b.dtype)


_bdot_nn.defvjp(_bdot_nn_fwd, _bdot_nn_bwd)


@jax.custom_vjp
def _bdot_nt(a, b):
    return _dg(a, b, 1, 1)


def _bdot_nt_fwd(a, b):
    return _dg(a, b, 1, 1), (a, b)


def _bdot_nt_bwd(res, ct):
    a, b = res
    return _dg(ct, b, 1, 0).astype(a.dtype), _dg(ct, a, 0, 0).astype(b.dtype)


_bdot_nt.defvjp(_bdot_nt_fwd, _bdot_nt_bwd)


@functools.partial(jax.custom_vjp, nondiff_argnums=(1,))
def _lane_roll(x, shift):
    return pltpu.roll(x, shift % x.shape[1], axis=1)


def _lane_roll_fwd(x, shift):
    return _lane_roll(x, shift), None


def _lane_roll_bwd(shift, _, ct):
    return (_lane_roll(ct, -shift),)


_lane_roll.defvjp(_lane_roll_fwd, _lane_roll_bwd)


@functools.partial(jax.custom_vjp, nondiff_argnums=(1, 2))
def _cols(x, lo, hi):
    return x[:, lo:hi]


def _cols_fwd(x, lo, hi):
    return x[:, lo:hi], x.shape[1]


def _cols_bwd(lo, hi, width, ct):
    parts = []
    if lo > 0:
        parts.append(jnp.zeros((ct.shape[0], lo), ct.dtype))
    parts.append(ct)
    if hi < width:
        parts.append(jnp.zeros((ct.shape[0], width - hi), ct.dtype))
    return (jnp.concatenate(parts, axis=1) if len(parts) > 1 else ct,)


_cols.defvjp(_cols_fwd, _cols_bwd)


MM_VMEM_BUDGET = 20 * 1024 * 1024


def _mm(a, b, *, name, ta=False, tb=False, out_dtype=F32, lead=None, add=None, add_scale=1.0,
        tm=1024, tn=1024, tk=2816, ride=None):
    if ta:
        K, M = a.shape
    else:
        M, K = a.shape
    bshape = b.shape[1:] if lead is not None else b.shape
    if tb:
        N, Kb = bshape
    else:
        Kb, N = bshape
    assert K == Kb, (name, a.shape, b.shape)

    def blocks(tm, tn, tk):
        tm = _tile(M, tm, LANE if ta else 16)
        tn = _tile(N, tn, LANE)
        tk = _tile(K, tk, LANE)
        nbytes = (tm * tk * a.dtype.itemsize + tk * tn * b.dtype.itemsize
                  + tm * tn * (jnp.dtype(out_dtype).itemsize + (4 if K // tk > 1 else 0)
                               + (add.dtype.itemsize if add is not None else 0)))
        return tm, tn, tk, nbytes

    tm, tn, tk, est = blocks(tm, tn, tk)
    for shrink in ("m", "k", "m", "k", "n"):
        if est <= MM_VMEM_BUDGET:
            break
        if shrink == "m":
            tm, tn, tk, est = blocks(max(tm // 2, LANE), tn, tk)
        elif shrink == "k":
            tm, tn, tk, est = blocks(tm, tn, max(tk // 2, LANE))
        else:
            tm, tn, tk, est = blocks(tm, max(tn // 2, LANE), tk)
    nk = K // tk
    ca = 0 if ta else 1
    cb = 1 if tb else 0

    def body(*refs):
        a_ref, b_ref = refs[0], refs[1]
        add_ref = refs[2] if add is not None else None
        o_ref = refs[3] if add is not None else refs[2]

        def finish(r):
            if add_ref is not None:
                r = r + add_scale * add_ref[...].astype(F32)
            o_ref[...] = r.astype(o_ref.dtype)

        if nk == 1:
            finish(_dg(a_ref[...], b_ref[...], ca, cb))
            return
        acc_ref = refs[-1]
        k = pl.program_id(2)

        @pl.when(k == 0)
        def _():
            acc_ref[...] = jnp.zeros_like(acc_ref)

        acc_ref[...] += _dg(a_ref[...], b_ref[...], ca, cb)

        @pl.when(k == nk - 1)
        def _():
            finish(acc_ref[...])

    a_blk = (tk, tm) if ta else (tm, tk)
    a_map = (lambda i, j, k: (k, i)) if ta else (lambda i, j, k: (i, k))
    b_blk = (tn, tk) if tb else (tk, tn)
    if lead is None:
        b_map = (lambda i, j, k: (j, k)) if tb else (lambda i, j, k: (k, j))
        b_spec = pl.BlockSpec(b_blk, b_map)
    else:
        b_map = (lambda i, j, k: (lead, j, k)) if tb else (lambda i, j, k: (lead, k, j))
        b_spec = pl.BlockSpec((None,) + b_blk, b_map)
    in_specs = [pl.BlockSpec(a_blk, a_map), b_spec]
    args = [a, b]
    if add is not None:
        in_specs.append(pl.BlockSpec((tm, tn), lambda i, j, k: (i, j)))
        args.append(add)
    (out,), rode = _host_call(
        body, name=name,
        grid=(M // tm, N // tn, nk),
        in_specs=in_specs,
        out_specs=[pl.BlockSpec((tm, tn), lambda i, j, k: (i, j))],
        out_shape=[_out((M, N), out_dtype)],
        scratch=[pltpu.VMEM((tm, tn), F32)] if nk > 1 else [],
        args=[_hbm(v) for v in args], sem=("parallel", "parallel", "arbitrary"), est=est + tm * tn * 4,
        ride=ride)
    return out if ride is None else (out, rode)


def _rowwise(fn, tiles, params, tile_outs, acc_outs=(), *, tm, name):
    tile_arrays, tile_specs = [], []
    for t in tiles:
        if isinstance(t, tuple):
            tile_arrays.append(t[0])
            tile_specs.append(t[1])
        else:
            tile_arrays.append(t)
            tile_specs.append(pl.BlockSpec((tm, t.shape[1]), lambda i: (i, 0)))
    T = tile_arrays[0].shape[0]
    nt, np_, nto, nao = len(tile_arrays), len(params), len(tile_outs), len(acc_outs)

    def body(*refs):
        i = pl.program_id(0)
        tvals = [r[...] for r in refs[:nt]]
        pvals = [r[...] for r in refs[nt:nt + np_]]
        to_refs = refs[nt + np_:nt + np_ + nto]
        ao_refs = refs[nt + np_ + nto:]
        touts, aouts = fn(i, tvals, pvals)
        for r, v in zip(to_refs, touts):
            r[...] = v.astype(r.dtype)
        if nao:
            @pl.when(i == 0)
            def _():
                for r in ao_refs:
                    r[...] = jnp.zeros_like(r)
            for r, v in zip(ao_refs, aouts):
                r[...] += v.astype(r.dtype)

    in_specs = tile_specs + [pl.BlockSpec(p.shape, lambda i: (0, 0)) for p in params]
    out_specs = [pl.BlockSpec((tm, c), lambda i: (i, 0)) for c, _ in tile_outs]
    out_specs += [pl.BlockSpec(s, lambda i: (0, 0)) for s, _ in acc_outs]
    out_shape = [_out((T, c), d) for c, d in tile_outs]
    out_shape += [_out(s, d) for s, d in acc_outs]
    width = sum(s.block_shape[-1] for s in tile_specs) + sum(c for c, _ in tile_outs)
    est = 6 * tm * width * 4 + sum(_nbytes(p.shape, F32) for p in params) * 4
    return pl.pallas_call(
        body, name=name, grid=(T // tm,),
        in_specs=in_specs, out_specs=out_specs, out_shape=out_shape,
        compiler_params=_params(("arbitrary",) if nao else ("parallel",), est),
    )(*[_hbm(v) for v in tile_arrays], *[_hbm(p) for p in params])


def _ln_fn(alpha, s, xres, y, g, b):
    z = alpha * xres.astype(F32) + s * y.astype(F32)
    mu = jnp.mean(z, axis=-1, keepdims=True)
    zc = z - mu
    var = jnp.mean(zc * zc, axis=-1, keepdims=True)
    return zc * lax.rsqrt(var + LN_EPS) * g + b


def _mm_ln(a, b, lead, xres, g, bias, *, alpha, s, name):
    M, K = a.shape
    N = b.shape[2]
    tm = _tile(M, 256, 16)

    def body(a_ref, b_ref, x_ref, g_ref, bias_ref, y_ref, xo_ref, xb_ref):
        y = _dg(a_ref[...], b_ref[...], 1, 0)
        y_ref[...] = y.astype(y_ref.dtype)
        out = _ln_fn(alpha, s, x_ref[...], y, g_ref[...], bias_ref[...])
        xo_ref[...] = out
        xb_ref[...] = out.astype(BF16)

    row = pl.BlockSpec((tm, N), lambda i: (i, 0))
    vec = pl.BlockSpec((1, N), lambda i: (0, 0))
    est = tm * K * 2 + K * N * 2 + tm * N * (4 + 4 + 4 + 2 + 8)
    return pl.pallas_call(
        body, name=name, grid=(M // tm,),
        in_specs=[pl.BlockSpec((tm, K), lambda i: (i, 0)), pl.BlockSpec((None, K, N), lambda i: (lead, 0, 0)),
                  row, vec, vec],
        out_specs=[row, row, row],
        out_shape=[_out((M, N), BF16), _out((M, N), F32), _out((M, N), BF16)],
        compiler_params=_params(("parallel",), est),
    )(_hbm(a), _hbm(b), _hbm(xres), _hbm(g), _hbm(bias))


def _ln_bwd_math(alpha, s, x, y, g, d):
    z = alpha * x + s * y.astype(F32)
    zc = z - jnp.mean(z, axis=-1, keepdims=True)
    r = lax.rsqrt(jnp.mean(zc * zc, axis=-1, keepdims=True) + LN_EPS)
    xh = zc * r
    dxh = d * g
    dz = r * (dxh - jnp.mean(dxh, axis=-1, keepdims=True) - xh * jnp.mean(dxh * xh, axis=-1, keepdims=True))
    return alpha * dz, s * dz, jnp.sum(d * xh, axis=0, keepdims=True), jnp.sum(d, axis=0, keepdims=True)


def _mm_ln_bwd(a, b, lead, tb, add, xres, y, g, *, alpha, s, name):
    M, K = a.shape
    N = b.shape[1] if tb else b.shape[2]
    tm = _tile(M, 512, 16)
    tk = _tile(K, 2816, LANE)
    nk = K // tk
    cb = 1 if tb else 0

    def body(a_ref, b_ref, add_ref, x_ref, y_ref, g_ref, dx_ref, dy_ref, dg_ref, db_ref, *scratch):
        i, k = pl.program_id(0), pl.program_id(1)

        def finish(d):
            @pl.when(i == 0)
            def _():
                dg_ref[...] = jnp.zeros_like(dg_ref)
                db_ref[...] = jnp.zeros_like(db_ref)

            dx, dy, dg, db = _ln_bwd_math(alpha, s, x_ref[...], y_ref[...], g_ref[...], d + add_ref[...])
            dx_ref[...] = dx
            dy_ref[...] = dy.astype(dy_ref.dtype)
            dg_ref[...] += dg
            db_ref[...] += db

        if nk == 1:
            finish(_dg(a_ref[...], b_ref[...], 1, cb))
            return
        acc_ref = scratch[0]

        @pl.when(k == 0)
        def _():
            acc_ref[...] = jnp.zeros_like(acc_ref)

        acc_ref[...] += _dg(a_ref[...], b_ref[...], 1, cb)

        @pl.when(k == nk - 1)
        def _():
            finish(acc_ref[...])

    row = pl.BlockSpec((tm, N), lambda i, k: (i, 0))
    vec = pl.BlockSpec((1, N), lambda i, k: (0, 0))
    b_spec = (pl.BlockSpec((None, N, tk), lambda i, k: (lead, 0, k)) if tb
              else pl.BlockSpec((None, tk, N), lambda i, k: (lead, k, 0)))
    est = tm * tk * 2 + tk * N * 2 + tm * N * (4 + 4 + 2 + 4 + 2 + 4 + 12)
    return pl.pallas_call(
        body, name=name, grid=(M // tm, nk),
        in_specs=[pl.BlockSpec((tm, tk), lambda i, k: (i, k)), b_spec, row, row, row, vec],
        out_specs=[row, row, vec, vec],
        out_shape=[_out((M, N), F32), _out((M, N), BF16), _out((1, N), F32), _out((1, N), F32)],
        scratch_shapes=[pltpu.VMEM((tm, N), F32)] if nk > 1 else [],
        compiler_params=_params(("arbitrary", "arbitrary"), est),
    )(_hbm(a), _hbm(b), _hbm(add), _hbm(xres), _hbm(y), _hbm(g))


def _ln_bwd(xres, y, g, dout, *, alpha, s, name):
    T, D = xres.shape
    tm = _tile(T, 256, 16)

    def body(x_ref, y_ref, d_ref, g_ref, dx_ref, dy_ref, dg_ref, db_ref):
        @pl.when(pl.program_id(0) == 0)
        def _():
            dg_ref[...] = jnp.zeros_like(dg_ref)
            db_ref[...] = jnp.zeros_like(db_ref)

        dx, dy, dg, db = _ln_bwd_math(alpha, s, x_ref[...], y_ref[...], g_ref[...], d_ref[...])
        dx_ref[...] = dx
        dy_ref[...] = dy.astype(dy_ref.dtype)
        dg_ref[...] += dg
        db_ref[...] += db

    row = pl.BlockSpec((tm, D), lambda i: (i, 0))
    vec = pl.BlockSpec((1, D), lambda i: (0, 0))
    return pl.pallas_call(
        body, name=name, grid=(T // tm,),
        in_specs=[row, row, row, vec], out_specs=[row, row, vec, vec],
        out_shape=[_out((T, D), F32), _out((T, D), BF16),
                   _out((1, D), F32), _out((1, D), F32)],
        compiler_params=_params(("arbitrary",), 12 * tm * D * 4),
    )(_hbm(xres), _hbm(y), _hbm(dout), _hbm(g))


FFN_TILE = 256


def _interleave(w, axis):
    n = w.shape[axis] // (2 * FFN_TILE)
    shp = w.shape[:axis] + (2, n, FFN_TILE) + w.shape[axis + 1:]
    return jnp.swapaxes(w.reshape(shp), axis, axis + 1).reshape(w.shape)


def _deinterleave(w, axis):
    n = w.shape[axis] // (2 * FFN_TILE)
    shp = w.shape[:axis] + (n, 2, FFN_TILE) + w.shape[axis + 1:]
    return jnp.swapaxes(w.reshape(shp), axis, axis + 1).reshape(w.shape)


def _ffn_up(xb, w13t, lead, *, name, ride=None):
    T, D = xb.shape
    F = w13t.shape[1] // 2
    tc = FFN_TILE
    tm = _tile(T, 1024, 16)

    def body(x_ref, w_ref, h_ref, a_ref):
        h = _dg(x_ref[...], w_ref[...], 1, 1)
        g, u = h[:, :tc], h[:, tc:]
        h_ref[...] = h.astype(h_ref.dtype)
        a_ref[...] = (g * jax.nn.sigmoid(g) * u).astype(a_ref.dtype)

    est = (tm * D + 2 * tc * D + 3 * tm * tc) * 2 + 3 * tm * tc * 4
    (h13, a), gathered = _host_call(
        body, name=name, grid=(T // tm, F // tc),
        in_specs=[pl.BlockSpec((tm, D), lambda i, j: (i, 0)),
                  pl.BlockSpec((None, 2 * tc, D), lambda i, j: (lead, j, 0))],
        out_specs=[pl.BlockSpec((tm, 2 * tc), lambda i, j: (i, j)),
                   pl.BlockSpec((tm, tc), lambda i, j: (i, j))],
        out_shape=[_out((T, 2 * F), BF16), _out((T, F), BF16)],
        args=[_hbm(xb), _hbm(w13t)], sem=("parallel", "parallel"), est=est, ride=ride)
    return h13, a, gathered


def _ffn_down_bwd(dyb, w2, lead, h13, *, name, ride=None):
    T, D = dyb.shape
    F = w2.shape[1]
    tc = FFN_TILE
    tm = _tile(T, 1024, 16)

    def body(dy_ref, w_ref, h_ref, dh_ref):
        d = _dg(dy_ref[...], w_ref[...], 1, 1)
        h = h_ref[...].astype(F32)
        g, u = h[:, :tc], h[:, tc:]
        sig = jax.nn.sigmoid(g)
        gs = g * sig
        dh_ref[...] = jnp.concatenate([d * u * (sig + gs * (1.0 - sig)), d * gs], axis=1).astype(dh_ref.dtype)

    est = (tm * D + tc * D + 4 * tm * tc) * 2 + 6 * tm * tc * 4
    (dh,), rode = _host_call(
        body, name=name, grid=(T // tm, F // tc),
        in_specs=[pl.BlockSpec((tm, D), lambda i, j: (i, 0)),
                  pl.BlockSpec((None, tc, D), lambda i, j: (lead, j, 0)),
                  pl.BlockSpec((tm, 2 * tc), lambda i, j: (i, j))],
        out_specs=[pl.BlockSpec((tm, 2 * tc), lambda i, j: (i, j))],
        out_shape=[_out((T, 2 * F), BF16)],
        args=[_hbm(dyb), _hbm(w2), _hbm(h13)], sem=("parallel", "parallel"), est=est, ride=ride)
    return dh, rode


def _pool_select(parts, pw):
    pg = pw // len(POOL_WINDOWS)
    grp = lax.broadcasted_iota(jnp.int32, parts[0].shape, 1) // pg
    out = parts[3]
    for g in (2, 1, 0):
        out = jnp.where(grp == g, parts[g], out)
    return out


def _pool_count(t0, rows, pw):
    pg = pw // len(POOL_WINDOWS)
    grp = lax.broadcasted_iota(jnp.int32, (rows, pw), 1) // pg
    win = jnp.where(grp == 0, POOL_WINDOWS[0],
                    jnp.where(grp == 1, POOL_WINDOWS[1],
                              jnp.where(grp == 2, POOL_WINDOWS[2], POOL_WINDOWS[3])))
    t = t0 + lax.broadcasted_iota(jnp.int32, (rows, pw), 0)
    return jnp.minimum(t + 1, win).astype(F32)


def _window_sums(ext, up):
    n = ext.shape[0]
    sums, cur, k = [], ext, 1
    for _ in POOL_WINDOWS:
        cur = cur + pltpu.roll(cur, (n - k) if up else k, axis=0)
        sums.append(cur)
        k *= 2
    return sums


def _pool_delta(u, halo, t0):
    tm, pw = u.shape
    ext = jnp.concatenate([halo, u], axis=0)
    sums = [s[POOL_HALO:, :] for s in _window_sums(ext, up=False)]
    return _pool_select(sums, pw) / _pool_count(t0, tm, pw) - u


def _pool_fwd(hin, wbd, scale, cat, *, name):
    T = hin.shape[0]
    pw = wbd.shape[0]
    tm = _tile(T, 256, POOL_HALO)
    per = tm // POOL_HALO

    def body(u_ref, halo_ref, w_ref, s_ref, cat_ref, y_ref):
        i = pl.program_id(0)
        halo = jnp.where(i > 0, halo_ref[...], 0.0)
        d = _pool_delta(u_ref[...], halo, i * tm)
        y_ref[...] = (_dg(d, w_ref[...], 1, 0) * s_ref[...]).astype(y_ref.dtype)

    return pl.pallas_call(
        body, name=name, grid=(T // tm,),
        in_specs=[pl.BlockSpec((tm, pw), lambda i: (i, 0)),
                  pl.BlockSpec((POOL_HALO, pw), lambda i: (jnp.maximum(i * per - 1, 0), 0)),
                  pl.BlockSpec((pw, pw), lambda i: (0, 0)),
                  pl.BlockSpec((1, pw), lambda i: (0, 0)),
                  ANY],
        out_specs=pl.BlockSpec((tm, pw), lambda i: (i, 0)),
        out_shape=_out(cat.shape, cat.dtype),
        input_output_aliases={4: 0},
        compiler_params=_params(("parallel",), 16 * tm * pw * 4),
    )(_hbm(hin), _hbm(hin), _hbm(wbd), _hbm(scale), _hbm(cat))


def _pool_bwd(hin, dcat, wbd, scale, *, name):
    T = hin.shape[0]
    pw = wbd.shape[0]
    tm = _tile(T, 256, POOL_HALO)
    per = tm // POOL_HALO
    nt = T // tm

    def body(u_ref, halo_ref, dy_ref, dyn_ref, w_ref, s_ref, du_ref, dw_ref, ds_ref):
        i = pl.program_id(0)

        @pl.when(i == 0)
        def _():
            dw_ref[...] = jnp.zeros_like(dw_ref)
            ds_ref[...] = jnp.zeros_like(ds_ref)

        halo = jnp.where(i > 0, halo_ref[...], 0.0)
        d = _pool_delta(u_ref[...], halo, i * tm)
        w = w_ref[...]
        sc = s_ref[...]
        dy = dy_ref[...]
        dyn = jnp.where(i < nt - 1, dyn_ref[...], 0.0)
        ds_ref[...] += jnp.sum(dy * _dg(d, w, 1, 0), axis=0, keepdims=True)
        dys = dy * sc
        dw_ref[...] += _dg(d, dys, 0, 0)
        dys_ext = jnp.concatenate([dys, dyn * sc], axis=0)
        dd_ext = _dg(dys_ext, w, 1, 1)
        ddp = dd_ext / _pool_count(i * tm, tm + POOL_HALO, pw)
        sums = [s[:tm, :] for s in _window_sums(ddp, up=True)]
        du_ref[...] = _pool_select(sums, pw) - dd_ext[:tm, :]

    return pl.pallas_call(
        body, name=name, grid=(nt,),
        in_specs=[pl.BlockSpec((tm, pw), lambda i: (i, 0)),
                  pl.BlockSpec((POOL_HALO, pw), lambda i: (jnp.maximum(i * per - 1, 0), 0)),
                  pl.BlockSpec((tm, pw), lambda i: (i, 0)),
                  pl.BlockSpec((POOL_HALO, pw), lambda i: (jnp.minimum((i + 1) * per, nt * per - 1), 0)),
                  pl.BlockSpec((pw, pw), lambda i: (0, 0)),
                  pl.BlockSpec((1, pw), lambda i: (0, 0))],
        out_specs=[pl.BlockSpec((tm, pw), lambda i: (i, 0)),
                   pl.BlockSpec((pw, pw), lambda i: (0, 0)),
                   pl.BlockSpec((1, pw), lambda i: (0, 0))],
        out_shape=[_out((T, pw), F32),
                   _out((pw, pw), F32),
                   _out((1, pw), F32)],
        compiler_params=_params(("arbitrary",), 24 * tm * pw * 4),
    )(_hbm(hin), _hbm(hin), _hbm(dcat), _hbm(dcat), _hbm(wbd), _hbm(scale))


def _rms(x, g):
    return x * lax.rsqrt(jnp.mean(x * x, axis=-1, keepdims=True) + RMS_EPS) * g


def _norms_fn(pw, h, gq, gkv):
    o1 = pw + Q_LORA
    o2 = o1 + KV_LORA
    return (_rms(_cols(h, pw, o1), gq), _rms(_cols(h, o1, o2), gkv), _cols(h, o2, h.shape[1]))


def _norms_fwd(hin, gq, gkv, *, pw, name):
    tm = _tile(hin.shape[0], 256, 16)

    def fn(i, tv, pv):
        return _norms_fn(pw, tv[0], pv[0], pv[1]), ()

    return _rowwise(fn, [hin], [gq, gkv], [(Q_LORA, BF16), (KV_LORA, BF16), (LANE, F32)], tm=tm, name=name)


def _norms_bwd(hin, gq, gkv, dcq, dckv, dkpe, du, *, pw, name):
    tm = _tile(hin.shape[0], 256, 16)
    dinp = hin.shape[1]

    def fn(i, tv, pv):
        _, vjp = jax.vjp(functools.partial(_norms_fn, pw), tv[0], pv[0], pv[1])
        dh, dgq, dgkv = vjp((tv[1].astype(F32), tv[2].astype(F32), tv[3].astype(F32)))
        dh = jnp.concatenate([tv[4], dh[:, pw:]], axis=1)
        return (dh,), (dgq, dgkv)

    return _rowwise(fn, [hin, dcq, dckv, dkpe, du], [gq, gkv], [(dinp, BF16)],
                    [((1, Q_LORA), F32), ((1, KV_LORA), F32)], tm=tm, name=name)


def _heads_fn(H, qraw, kv, kpe, rc, rs1, rs2):
    half = QK_ROPE // 2
    scale = (QK_NOPE + QK_ROPE) ** -0.5

    def rope(blk):
        return blk * rc + _lane_roll(blk, -half) * rs1 + _lane_roll(blk, half) * rs2

    krot = rope(kpe)
    qs, ks, vs = [], [], []
    for h in range(H):
        lo = h * HEAD_PAD
        qs += [_cols(qraw, lo, lo + LANE) * scale, rope(_cols(qraw, lo + LANE, lo + HEAD_PAD)) * scale]
        ks += [_cols(kv, lo, lo + LANE), krot]
        vs += [_cols(kv, lo + LANE, lo + HEAD_PAD)]
    return jnp.concatenate(qs, axis=1), jnp.concatenate(ks, axis=1), jnp.concatenate(vs, axis=1)


def _heads_fwd(qraw, kv, kpe, tabs, *, H, name):
    tm = _tile(qraw.shape[0], 256, 16)

    def fn(i, tv, pv):
        return _heads_fn(H, *tv), ()

    return _rowwise(fn, [qraw, kv, kpe, *tabs], [],
                    [(H * HEAD_PAD, BF16), (H * HEAD_PAD, BF16), (H * V_HEAD, BF16)], tm=tm, name=name)


def _heads_bwd(dq, dk, dv, tabs, *, H, name):
    tm = _tile(dq.shape[0], 256, 16)

    def fn(i, tv, pv):
        z = jnp.zeros((tm, H * HEAD_PAD), F32)
        zk = jnp.zeros((tm, LANE), F32)
        rc, rs1, rs2 = tv[3], tv[4], tv[5]
        _, vjp = jax.vjp(lambda a, b, c: _heads_fn(H, a, b, c, rc, rs1, rs2), z, z, zk)
        return vjp((tv[0], tv[1], tv[2])), ()

    return _rowwise(fn, [dq, dk, dv, *tabs], [],
                    [(H * HEAD_PAD, BF16), (H * HEAD_PAD, BF16), (LANE, F32)], tm=tm, name=name)


def _diag_mask(rows, cols, row0):
    r = (row0 + lax.broadcasted_iota(jnp.int32, (rows, cols), 0)) // CHUNK
    c = lax.broadcasted_iota(jnp.int32, (rows, cols), 1) // CHUNK
    return r >= c


def _flash_fwd(qh, kh, vh, *, H, pw, name, ride=None):
    T = qh.shape[0]
    t = _tile(T, 512, CHUNK)
    off = pw // V_HEAD


    def body(q_ref, k_ref, v_ref, o_ref, lse_ref):
        i = pl.program_id(1)
        q = q_ref[...]

        def blk(j, carry, masked):
            m, l, acc = carry
            rows = pl.ds(pl.multiple_of(j * t, t), t)
            s = _dg(q, k_ref[rows, :], 1, 1)
            if masked:
                s = jnp.where(_diag_mask(t, t, 0), s, NEG_INF)
            mn = jnp.maximum(m, jnp.max(s, axis=1, keepdims=True))
            p = jnp.exp(s - mn)
            corr = jnp.exp(m - mn)
            l = corr * l + jnp.sum(p, axis=1, keepdims=True)
            acc = corr * acc + _dg(p, v_ref[rows, :], 1, 0)
            return mn, l, acc

        init = (jnp.full((t, 1), NEG_INF, F32), jnp.zeros((t, 1), F32), jnp.zeros((t, V_HEAD), F32))
        carry = lax.fori_loop(0, i, lambda j, c: blk(j, c, False), init)
        m, l, acc = blk(i, carry, True)
        o_ref[...] = (acc / l).astype(o_ref.dtype)
        lse_ref[...] = jnp.broadcast_to(m + jnp.log(l), (t, V_HEAD))

    est = 2 * T * (HEAD_PAD + V_HEAD) * 2 + 8 * t * t * 4
    (o, lse), gathered = _host_call(
        body, name=name, grid=(H, T // t),
        in_specs=[pl.BlockSpec((t, HEAD_PAD), lambda h, i: (i, h)),
                  pl.BlockSpec((T, HEAD_PAD), lambda h, i: (0, h)),
                  pl.BlockSpec((T, V_HEAD), lambda h, i: (0, h))],
        out_specs=[pl.BlockSpec((t, V_HEAD), lambda h, i: (i, off + h)),
                   pl.BlockSpec((t, V_HEAD), lambda h, i: (i, h))],
        out_shape=[_out((T, pw + H * V_HEAD), BF16),
                   _out((T, H * V_HEAD), F32)],
        args=[_hbm(qh), _hbm(kh), _hbm(vh)], sem=("parallel", "parallel"), est=est, ride=ride)
    return o, lse, gathered


def _flash_bwd(qh, kh, vh, cat, dcat, lse, *, H, pw, name, ride=None):
    T = qh.shape[0]
    t = _tile(T, 512, CHUNK)
    nb = T // t
    off = pw // V_HEAD

    def body(q_ref, k_ref, v_ref, o_ref, do_ref, lse_ref, dq_ref, dk_ref, dv_ref):
        j = pl.program_id(1)

        @pl.when(j == 0)
        def _():
            dq_ref[...] = jnp.zeros_like(dq_ref)

        kj = k_ref[...]
        vj = v_ref[...]

        def blk(i, carry, masked):
            dk, dv = carry
            rows = pl.ds(pl.multiple_of(i * t, t), t)
            qi = q_ref[rows, :]
            doi = do_ref[rows, :]
            oi = o_ref[rows, :].astype(F32)
            lsei = lse_ref[rows, :][:, :1]
            s = _dg(qi, kj, 1, 1)
            if masked:
                s = jnp.where(_diag_mask(t, t, 0), s, NEG_INF)
            p = jnp.exp(s - lsei)
            dv = dv + _dg(p, doi, 0, 0)
            dp = _dg(doi, vj, 1, 1)
            di = jnp.sum(doi * oi, axis=1, keepdims=True)
            ds = p * (dp - di)
            dk = dk + _dg(ds, qi, 0, 0)
            dq_ref[rows, :] += _dg(ds, kj, 1, 0)
            return dk, dv

        carry = blk(j, (jnp.zeros((t, HEAD_PAD), F32), jnp.zeros((t, V_HEAD), F32)), True)
        dk, dv = lax.fori_loop(j + 1, nb, lambda i, c: blk(i, c, False), carry)
        dk_ref[...] = dk
        dv_ref[...] = dv

    est = T * (HEAD_PAD * 2 + V_HEAD * 2 + V_HEAD * 4 + V_HEAD * 4 + HEAD_PAD * 4) + 10 * t * t * 4
    (dq, dk, dv), gathered = _host_call(
        body, name=name, grid=(H, nb),
        in_specs=[pl.BlockSpec((T, HEAD_PAD), lambda h, j: (0, h)),
                  pl.BlockSpec((t, HEAD_PAD), lambda h, j: (j, h)),
                  pl.BlockSpec((t, V_HEAD), lambda h, j: (j, h)),
                  pl.BlockSpec((T, V_HEAD), lambda h, j: (0, off + h)),
                  pl.BlockSpec((T, V_HEAD), lambda h, j: (0, off + h)),
                  pl.BlockSpec((T, V_HEAD), lambda h, j: (0, h))],
        out_specs=[pl.BlockSpec((T, HEAD_PAD), lambda h, j: (0, h)),
                   pl.BlockSpec((t, HEAD_PAD), lambda h, j: (j, h)),
                   pl.BlockSpec((t, V_HEAD), lambda h, j: (j, h))],
        out_shape=[_out((T, H * HEAD_PAD), F32),
                   _out((T, H * HEAD_PAD), F32),
                   _out((T, H * V_HEAD), F32)],
        args=[_hbm(v) for v in (qh, kh, vh, cat, dcat, lse)], sem=("arbitrary", "arbitrary"), est=est,
        ride=ride)
    return dq, dk, dv, gathered


def _mem_fn(q, k, v):
    hd = q.shape[1] // MEM_HEADS
    outs = []
    for h in range(MEM_HEADS):
        lo, hi = h * hd, (h + 1) * hd
        s = _bdot_nt(_cols(q, lo, hi), _cols(k, lo, hi)) * hd ** -0.5
        e = jnp.exp(s - lax.stop_gradient(jnp.max(s, axis=1, keepdims=True)))
        p = e / jnp.sum(e, axis=1, keepdims=True)
        outs.append(_bdot_nn(p, _cols(v, lo, hi)))
    return jnp.concatenate(outs, axis=1)


def _mem_fwd(q, k, v, *, name):
    T, D = q.shape
    tm = _tile(T, 256, 16)

    def fn(i, tv, pv):
        return (_mem_fn(tv[0], pv[0], pv[1]),), ()

    return _rowwise(fn, [q], [k, v], [(D, BF16)], tm=tm, name=name)[0]


def _mem_bwd(q, k, v, do, *, name):
    T, D = q.shape
    tm = _tile(T, 256, 16)

    def fn(i, tv, pv):
        _, vjp = jax.vjp(_mem_fn, tv[0], pv[0], pv[1])
        dq, dk, dv = vjp(tv[1].astype(F32))
        return (dq,), (dk, dv)

    return _rowwise(fn, [q, do], [k, v], [(D, BF16)], [(k.shape, F32), (v.shape, F32)], tm=tm, name=name)


def _loss_head(y, target, *, name):
    T, D = y.shape
    tm = _tile(T, 256, 16)

    def fn(i, tv, pv):
        err = tv[0] - tv[1]
        part = 0.5 * jnp.sum(jnp.sum(err * err, axis=1, keepdims=True) / D, axis=0, keepdims=True)
        return (err / D,), (jnp.broadcast_to(part, (8, LANE)),)

    return _rowwise(fn, [y, target], [], [(D, F32)], [((8, LANE), F32)], tm=tm, name=name)


def _adamw(w, g, m, v, *, name):
    shape = w.shape
    if w.ndim != 3:
        lead3 = (1, math.prod(shape[:-1]), shape[-1])
        return [o.reshape(shape) for o in _adamw(*[a.reshape(lead3) for a in (w, g, m, v)], name=name)]
    Lw, R, C = shape
    tr = _tile(R, 512, 8)
    b1c = 1.0 - ADAM_B1 ** ADAM_STEP
    b2c = 1.0 - ADAM_B2 ** ADAM_STEP

    def body(w_ref, g_ref, m_ref, v_ref, d_ref, mo_ref, vo_ref):
        gg = g_ref[...]
        mn = ADAM_B1 * m_ref[...] + (1.0 - ADAM_B1) * gg
        vn = ADAM_B2 * v_ref[...] + (1.0 - ADAM_B2) * (gg * gg)
        d_ref[...] = -ADAM_LR * ((mn / b1c) / (jnp.sqrt(vn / b2c) + ADAM_EPS) + ADAM_WD * w_ref[...])
        mo_ref[...] = mn
        vo_ref[...] = vn

    spec = pl.BlockSpec((None, tr, C), lambda l, i: (l, i, 0))
    return pl.pallas_call(
        body, name=name, grid=(Lw, R // tr),
        in_specs=[spec] * 4, out_specs=[spec] * 3,
        out_shape=[_out(shape, F32)] * 3,
        compiler_params=_params(("parallel", "parallel"), 7 * tr * C * 4),
    )(*[_hbm(a) for a in (w, g, m, v)])


def _pair_sum(core, gs, landed, offs, *, name):
    n = len(gs)
    _, R, C = landed.shape
    rows = [g.shape[0] // N_DEV for g in gs]

    def body(core_ref, *refs):
        g_refs, l_ref, o_ref = refs[:n], refs[n], refs[n + 1]
        for g_ref, off, r in zip(g_refs, offs, rows):
            o_ref[off:off + r, :] = (g_ref[...].astype(F32) + l_ref[off:off + r, :].astype(F32)).astype(o_ref.dtype)

    slab = pl.BlockSpec((None, R, C), lambda p, core_ref: (p, 0, 0))
    own = [pl.BlockSpec((r, C), lambda p, core_ref: (2 * p + core_ref[0], 0)) for r in rows]
    return pl.pallas_call(
        body, name=name,
        grid_spec=pltpu.PrefetchScalarGridSpec(
            num_scalar_prefetch=1, grid=(4,), in_specs=own + [slab], out_specs=slab),
        out_shape=_out(landed.shape, landed.dtype),
        input_output_aliases={n + 1: 0},
        compiler_params=_params(("arbitrary",), 3 * R * C * 2 + R * C * 8),
    )(core, *[_hbm(g) for g in gs], _hbm(landed))


def _quad_sum(chip, part, gathered, used, *, name):
    C = part.shape[2]
    R = used
    tr = _tile(R, 256, 16)

    def body(chip_ref, own_ref, a_ref, b_ref, c_ref, o_ref):
        o_ref[...] = ((own_ref[...].astype(F32) + a_ref[...].astype(F32)) + b_ref[...].astype(F32)) \
            + c_ref[...].astype(F32)

    def other(k):
        return pl.BlockSpec((None, tr, C), lambda i, chip_ref: (chip_ref[0] ^ k, i, 0))

    return pl.pallas_call(
        body, name=name,
        grid_spec=pltpu.PrefetchScalarGridSpec(
            num_scalar_prefetch=1, grid=(R // tr,),
            in_specs=[pl.BlockSpec((None, tr, C), lambda i, chip_ref: (chip_ref[0], i, 0)),
                      other(1), other(2), other(3)],
            out_specs=pl.BlockSpec((tr, C), lambda i, chip_ref: (i, 0))),
        out_shape=_out((R, C), F32),
        compiler_params=_params(("arbitrary",), 8 * tr * C * 4),
    )(chip, _hbm(part), _hbm(gathered), _hbm(gathered), _hbm(gathered))


def _place():
    x, y, c = lax.axis_index("x"), lax.axis_index("y"), lax.axis_index("c")
    return x, y, c


ANY = pl.BlockSpec(memory_space=pl.ANY)


class _Gather:
    def __init__(self, shards):
        self.shards = list(shards)
        self.n = len(self.shards)
        self.out_shape = [_out((s.shape[0], N_DEV * s.shape[1], s.shape[2]), s.dtype)
                          for s in self.shards]
        self.scratch = [pltpu.SemaphoreType.DMA((7 * self.n,)), pltpu.SemaphoreType.DMA((7 * self.n,)),
                        pltpu.SemaphoreType.DMA((self.n,))]
        self.operands = [_hbm(s) for s in self.shards]

    def _bind(self, refs):
        n = self.n
        ins, outs = refs[:n], refs[n:2 * n]
        send_sems, recv_sems, local_sems = refs[2 * n:]
        x, y, c = _place()
        me, sib = (x, y, c), (x, y, 1 - c)
        chips = [(1 - x, y), (x, 1 - y), (1 - x, 1 - y)]

        def rows(w, p):
            r = self.shards[w].shape[1]
            idx = 4 * p[0] + 2 * p[1] + p[2]
            return outs[w].at[:, pl.ds(pl.multiple_of(idx * r, 8), r), :]

        def copy(w, k, block, to, src=None):
            return pltpu.make_async_remote_copy(
                src_ref=rows(w, block) if src is None else src, dst_ref=rows(w, block),
                send_sem=send_sems.at[w * 7 + k], recv_sem=recv_sems.at[w * 7 + k],
                device_id=to, device_id_type=MESH)

        def mine():
            return [pltpu.make_async_copy(ins[w], rows(w, me), local_sems.at[w]) for w in range(n)]

        def first():
            out = []
            for w in range(n):
                out.append(copy(w, 0, me, sib, src=ins[w]))
                out += [copy(w, 1 + j, me, (*chip, c), src=ins[w]) for j, chip in enumerate(chips)]
            return out

        def passed():
            return [copy(w, 4 + j, (*chip, c), sib) for j, chip in enumerate(chips) for w in range(n)]

        def landed():
            return [copy(w, 1 + j, (*chip, c), me) for j, chip in enumerate(chips) for w in range(n)]

        def last():
            out = []
            for w in range(n):
                out.append(copy(w, 0, sib, me))
                out += [copy(w, 4 + j, (*chip, 1 - c), me) for j, chip in enumerate(chips)]
            return out

        return mine, first, landed, passed, last

    def start(self, refs):
        mine, first, _, _, _ = self._bind(refs)
        for cp in mine() + first():
            cp.start()

    def forward(self, refs):
        _, _, landed, passed, _ = self._bind(refs)
        for arrived, fwd in zip(landed(), passed()):
            arrived.wait_recv()
            fwd.start()

    def finish(self, refs):
        mine, first, _, passed, last = self._bind(refs)
        for cp in last():
            cp.wait_recv()
        for cp in first() + passed():
            cp.wait_send()
        for cp in mine():
            cp.wait()


class _ChipExchange:
    def __init__(self, parts, used):
        self.ncl = len(parts)
        self.used = list(used)
        self.out_shape = [_out(p.shape, p.dtype) for p in parts]
        self.scratch = [pltpu.SemaphoreType.DMA((3 * self.ncl,)), pltpu.SemaphoreType.DMA((3 * self.ncl,))]
        self.operands = [_hbm(p) for p in parts]
        self.n = self.ncl

    def _bind(self, refs):
        ncl = self.ncl
        ins, outs = refs[:ncl], refs[ncl:2 * ncl]
        send_sems, recv_sems = refs[2 * ncl:]
        x, y, c = _place()
        chips = [(1 - x, y), (x, 1 - y), (1 - x, 1 - y)]
        here = 2 * x + y

        def copies(outgoing):
            out = []
            for k in range(ncl):
                rows = pl.ds(0, self.used[k])
                for j, (cx, cy) in enumerate(chips):
                    there = 2 * cx + cy
                    src, dst = (there, here) if outgoing else (here, there)
                    out.append(pltpu.make_async_remote_copy(
                        src_ref=ins[k].at[src, rows, :], dst_ref=outs[k].at[dst, rows, :],
                        send_sem=send_sems.at[3 * k + j], recv_sem=recv_sems.at[3 * k + j],
                        device_id=(cx, cy, c), device_id_type=MESH))
            return out

        return copies

    def start(self, refs):
        for cp in self._bind(refs)(True):
            cp.start()

    def forward(self, refs):
        pass

    def finish(self, refs):
        copies = self._bind(refs)
        for cp in copies(False):
            cp.wait_recv()
        for cp in copies(True):
            cp.wait_send()


class _Both:
    def __init__(self, members):
        self.members = list(members)
        self.n = sum(m.n for m in self.members)
        self.out_shape = [s for m in self.members for s in m.out_shape]
        self.scratch = [s for m in self.members for s in m.scratch]
        self.operands = [o for m in self.members for o in m.operands]

    def split(self, arrays):
        out, a = [], 0
        for m in self.members:
            out.append(list(arrays[a:a + m.n]))
            a += m.n
        return out

    def _refs(self, refs):
        ins, outs = self.split(refs[:self.n]), self.split(refs[self.n:2 * self.n])
        scr, b = [], 2 * self.n
        for m in self.members:
            scr.append(list(refs[b:b + len(m.scratch)]))
            b += len(m.scratch)
        return [(*i, *o, *s) for i, o, s in zip(ins, outs, scr)]

    def start(self, refs):
        for m, r in zip(self.members, self._refs(refs)):
            m.start(r)

    def forward(self, refs):
        for m, r in zip(self.members, self._refs(refs)):
            m.forward(r)

    def finish(self, refs):
        for m, r in zip(self.members, self._refs(refs)):
            m.finish(r)


def _exchange_alone(ex, *, name):
    def body(*refs):
        ex.start(refs)
        ex.forward(refs)
        ex.finish(refs)

    return pl.pallas_call(
        body, name=name, in_specs=[ANY] * ex.n, out_specs=[ANY] * ex.n,
        out_shape=ex.out_shape, scratch_shapes=ex.scratch,
    )(*ex.operands)


def _host_call(body, *, name, grid, in_specs, out_specs, out_shape, args, sem, est, ride=None, scratch=()):
    scratch = list(scratch)
    if ride is None:
        outs = pl.pallas_call(body, name=name, grid=grid, in_specs=in_specs, out_specs=out_specs,
                              out_shape=out_shape, scratch_shapes=scratch,
                              compiler_params=_params(sem, est))(*args)
        return list(outs), []
    n_in, n_out, n, n_scr = len(in_specs), len(out_specs), ride.n, len(scratch)

    def full(*refs):
        ins, rin = refs[:n_in], refs[n_in:n_in + n]
        outs, rout = refs[n_in + n:n_in + n + n_out], refs[n_in + n + n_out:n_in + 2 * n + n_out]
        own = refs[n_in + 2 * n + n_out:n_in + 2 * n + n_out + n_scr]
        rrefs = (*rin, *rout, *refs[n_in + 2 * n + n_out + n_scr:])
        step, total = _ride(ride, rrefs, grid)
        body(*ins, *outs, *own)
        _ride_end(ride, rrefs, step, total)

    outs = pl.pallas_call(
        full, name=name, grid=grid,
        in_specs=list(in_specs) + [ANY] * n, out_specs=list(out_specs) + [ANY] * n,
        out_shape=list(out_shape) + ride.out_shape, scratch_shapes=scratch + ride.scratch,
        compiler_params=_params(("arbitrary",) * len(grid), est),
    )(*args, *ride.operands)
    return list(outs[:n_out]), list(outs[n_out:])


def _ride(ex, refs, grid):
    total = math.prod(grid)
    step = pl.program_id(0)
    for axis in range(1, len(grid)):
        step = step * grid[axis] + pl.program_id(axis)
    pl.when(step == 0)(lambda: ex.start(refs))
    return step, total


def _ride_end(ex, refs, step, total):
    pl.when(step == (3 * total) // 4)(lambda: ex.forward(refs))
    pl.when(step == total - 1)(lambda: ex.finish(refs))


def _class_layout(grads, classes):
    used = [0] * len(set(classes))
    offs = []
    for g, cl in zip(grads, classes):
        offs.append(used[cl])
        used[cl] += g.shape[0] // N_DEV
    return offs, used


def _rs_to_sibling(grads, classes, *, name):
    n = len(grads)
    offs, used = _class_layout(grads, classes)
    heights = used
    ncl = len(heights)
    cols = [next(g.shape[1] for g, cl in zip(grads, classes) if cl == k) for k in range(ncl)]

    def body(*refs):
        gs, land = refs[:n], refs[n:n + ncl]
        send_sems, recv_sems = refs[n + ncl:]
        x, y, c = _place()
        sib = (x, y, 1 - c)
        for p in range(4):
            for w in range(n):
                r = grads[w].shape[0] // N_DEV
                cl = classes[w]
                there = gs[w].at[pl.ds(pl.multiple_of((2 * p + 1 - c) * r, 8), r), :]
                pltpu.make_async_remote_copy(
                    src_ref=there, dst_ref=land[cl].at[p, pl.ds(offs[w], r), :],
                    send_sem=send_sems.at[cl * 4 + p], recv_sem=recv_sems.at[cl * 4 + p],
                    device_id=sib, device_id_type=MESH).start()
        for cl in range(ncl):
            for p in range(4):
                rows_used = land[cl].at[p, pl.ds(0, used[cl]), :]
                slab = pltpu.make_async_remote_copy(
                    src_ref=rows_used, dst_ref=rows_used,
                    send_sem=send_sems.at[cl * 4 + p], recv_sem=recv_sems.at[cl * 4 + p],
                    device_id=sib, device_id_type=MESH)
                slab.wait_send()
                slab.wait_recv()

    return pl.pallas_call(
        body, name=name,
        in_specs=[ANY] * n, out_specs=[ANY] * ncl,
        out_shape=[_out((4, heights[k], cols[k]), BF16) for k in range(ncl)],
        scratch_shapes=[pltpu.SemaphoreType.DMA((4 * ncl,))] * 2,
    )(*[_hbm(g) for g in grads])


def _all_reduce_small(v, *, name):
    R = v.shape[0]

    def body(v_ref, o_ref, buf, send_sems, recv_sems):
        x, y, c = _place()
        me = 4 * x + 2 * y + c
        buf[me] = v_ref[...]
        copies = []
        for k in range(1, N_DEV):
            fx, fy, fc = (k >> 2) & 1, (k >> 1) & 1, k & 1
            to = (x ^ fx, y ^ fy, c ^ fc)
            cp = pltpu.make_async_remote_copy(
                src_ref=v_ref, dst_ref=buf.at[me],
                send_sem=send_sems.at[k - 1], recv_sem=recv_sems.at[k - 1],
                device_id=to, device_id_type=MESH)
            cp.start()
            copies.append(cp)
        for k in range(1, N_DEV):
            fx, fy, fc = (k >> 2) & 1, (k >> 1) & 1, k & 1
            frm = 4 * (x ^ fx) + 2 * (y ^ fy) + (c ^ fc)
            pltpu.make_async_remote_copy(
                src_ref=v_ref, dst_ref=buf.at[frm],
                send_sem=send_sems.at[k - 1], recv_sem=recv_sems.at[k - 1],
                device_id=(x ^ fx, y ^ fy, c ^ fc), device_id_type=MESH).wait_recv()
        for cp in copies:
            cp.wait_send()
        acc = buf[0]
        for d in range(1, N_DEV):
            acc = acc + buf[d]
        o_ref[...] = acc

    vm = pl.BlockSpec(memory_space=pltpu.VMEM)
    return pl.pallas_call(
        body, name=name, in_specs=[vm], out_specs=vm,
        out_shape=jax.ShapeDtypeStruct((R, LANE), F32),
        scratch_shapes=[pltpu.VMEM((N_DEV, R, LANE), F32),
                        pltpu.SemaphoreType.DMA((N_DEV - 1,)), pltpu.SemaphoreType.DMA((N_DEV - 1,))],
        compiler_params=pltpu.CompilerParams(vmem_limit_bytes=VMEM_FLOOR),
    )(v)


def _rope_tables(positions):
    half = QK_ROPE // 2
    inv_freq = ROPE_BASE ** (-jnp.arange(half, dtype=F32) / half)
    ang = positions.astype(F32)[:, None] * inv_freq
    cos, sin = jnp.cos(ang), jnp.sin(ang)
    z = jnp.zeros_like(cos)
    z2 = jnp.zeros((positions.shape[0], LANE - QK_ROPE), F32)
    rc = jnp.concatenate([cos, cos, z2], axis=1)
    rs1 = jnp.concatenate([-sin, z, z2], axis=1)
    rs2 = jnp.concatenate([z, sin, z2], axis=1)
    return rc, rs1, rs2


def _block_diag(pool_w):
    G, pg, _ = pool_w.shape
    out = jnp.zeros((G * pg, G * pg), pool_w.dtype)
    for g in range(G):
        out = lax.dynamic_update_slice(out, pool_w[g], (g * pg, g * pg))
    return out


def kernel(x, mem, positions, ln_g, ln_b, ffn1_w13, ffn1_w2, w_in, pool_w, pool_scale, q_norm_g, w_uq, kv_norm_g, w_ukv, w_out, mem_wq, mem_wkv, mem_wo, ffn2_w13, ffn2_w2, loss_target, m_ln_g, m_ln_b, m_ffn1_w13, m_ffn1_w2, m_w_in, m_pool_w, m_pool_scale, m_q_norm_g, m_w_uq, m_kv_norm_g, m_w_ukv, m_w_out, m_mem_wq, m_mem_wkv, m_mem_wo, m_ffn2_w13, m_ffn2_w2, v_ln_g, v_ln_b, v_ffn1_w13, v_ffn1_w2, v_w_in, v_pool_w, v_pool_scale, v_q_norm_g, v_w_uq, v_kv_norm_g, v_w_ukv, v_w_out, v_mem_wq, v_mem_wkv, v_mem_wo, v_ffn2_w13, v_ffn2_w2):
    L = ln_g.shape[0]
    T, D = x.shape[1], x.shape[2]
    F = ffn1_w2.shape[1] * N_DEV
    PW = D // 4
    H = (D - PW) // V_HEAD
    DIN = w_in.shape[2]
    DINP = PW + Q_LORA + KV_LORA + LANE
    QW = QK_NOPE + QK_ROPE
    alpha = (2 * L) ** 0.25
    x2d = x.reshape(T, D)
    memb = mem.reshape(mem.shape[1], D).astype(BF16)
    target = loss_target.reshape(T, D)
    tabs = _rope_tables(positions.reshape(T))

    def shards_of(l):
        return dict(
            w13a=ffn1_w13[l].T[None].astype(BF16),
            w13b=ffn2_w13[l].T[None].astype(BF16),
            w2a=ffn1_w2[l][None].astype(BF16),
            w2b=ffn2_w2[l][None].astype(BF16),
            wsq=jnp.stack([w_out[l], mem_wq[l], mem_wo[l]]).astype(BF16),
            wkvT=mem_wkv[l].T[None].astype(BF16),
            winp=jnp.pad(w_in[l], ((0, 0), (0, DINP - DIN)))[None].astype(BF16),
            wuqT=w_uq[l].T[None].astype(BF16),
            wukvT=w_ukv[l].T[None].astype(BF16),
        )

    SMALL = ("winp", "wuqT", "wukvT")
    shards = [shards_of(l) for l in range(L)]
    W = [dict() for _ in range(L)]

    def rider(spec):
        return _Gather([shards[l][n] for l, n in spec]) if spec else None

    def arrived(spec, arrays):
        for (l, n), a in zip(spec, arrays):
            if n in ("w13a", "w13b"):
                a = _interleave(a, 1)
            elif n == "wuqT":
                a = jnp.pad(a.reshape(H, QW, Q_LORA), ((0, 0), (0, HEAD_PAD - QW), (0, 0)))
                a = a.reshape(1, H * HEAD_PAD, Q_LORA)
            elif n == "ln":
                a = jnp.moveaxis(a.reshape(N_DEV, 2, L, 4, D // N_DEV), 0, 3).reshape(2, L, 4, D)
                LN["g"], LN["b"] = a[0], a[1]
            W[l][n] = a

    LN = {}
    shards[0]["ln"] = jnp.concatenate([ln_g.reshape(1, 4 * L, -1), ln_b.reshape(1, 4 * L, -1)], axis=1)
    spec0 = [(0, "w13a")]
    arrived(spec0, _exchange_alone(rider(spec0), name="ag_first"))
    wbd = [_block_diag(pool_w[l]).astype(BF16) for l in range(L)]

    def ffn_fwd(l, which, xres, xb, k, spec):
        ab = "ab"[which]
        h13, a, rode = _ffn_up(xb, W[l]["w13" + ab], 0, name=f"l{l}_ffn{which}_up", ride=rider(spec))
        arrived(spec, rode)
        y, xo, xob = _mm_ln(a, W[l]["w2" + ab], 0, xres, LN["g"][l,k:k + 1], LN["b"][l,k:k + 1], alpha=alpha, s=0.5,
                            name=f"l{l}_ffn{which}_y_ln{k}")
        return dict(xres=xres, xb=xb, h13=h13, a=a, y=y), xo, xob

    saved = []
    xres, xb = x2d, x2d.astype(BF16)
    for l in range(L):
        sv = {}
        more = l + 1 < L
        Wl = W[l]
        spec = ([(0, "w2a"), (0, "ln"), *[(0, n) for n in SMALL], (0, "wkvT")] if l == 0
                else [(l, "wsq"), (l, "wkvT")])
        sv["ffn1"], x1, x1b = ffn_fwd(l, 0, xres, xb, 0, spec)
        hin = _mm(x1b, Wl["winp"], lead=0, name=f"l{l}_hin")
        pscale = pool_scale[l].reshape(1, PW)
        gq, gkv = q_norm_g[l].reshape(1, Q_LORA), kv_norm_g[l].reshape(1, KV_LORA)
        cqn, ckvn, kpe = _norms_fwd(hin, gq, gkv, pw=PW, name=f"l{l}_norms")
        qraw = _mm(cqn, Wl["wuqT"], lead=0, tb=True, name=f"l{l}_qraw")
        kv = _mm(ckvn, Wl["wukvT"], lead=0, tb=True, name=f"l{l}_kv")
        qh, kh, vh = _heads_fwd(qraw, kv, kpe, tabs, H=H, name=f"l{l}_heads")
        spec = [(l, "w13b"), (l, "w2b")] + ([(0, "wsq")] if l == 0 else []) + ([(l + 1, "w13a")] if more else [])
        cat, lse, rode = _flash_fwd(qh, kh, vh, H=H, pw=PW, name=f"l{l}_flash", ride=rider(spec))
        arrived(spec, rode)
        cat = _pool_fwd(hin, wbd[l], pscale, cat, name=f"l{l}_pool")
        ymix, x2, x2b = _mm_ln(cat, Wl["wsq"], 0, x1, LN["g"][l,1:2], LN["b"][l,1:2], alpha=alpha, s=1.0,
                               name=f"l{l}_ymix_ln1")
        qm = _mm(x2b, Wl["wsq"], lead=1, out_dtype=BF16, name=f"l{l}_qm")
        kvm = _mm(memb, Wl["wkvT"], lead=0, tb=True, name=f"l{l}_kvm")
        km, vm = kvm[:, :D], kvm[:, D:]
        om = _mem_fwd(qm, km, vm, name=f"l{l}_memattn")
        ymem, x3, x3b = _mm_ln(om, Wl["wsq"], 2, x2, LN["g"][l,2:3], LN["b"][l,2:3], alpha=alpha, s=1.0,
                               name=f"l{l}_ymem_ln2")
        spec = [(l + 1, n) for n in ("w2a", *SMALL)] if more else []
        sv["ffn2"], x4, x4b = ffn_fwd(l, 1, x3, x3b, 3, spec)
        sv.update(x1=x1, x1b=x1b, hin=hin, pscale=pscale, gq=gq, gkv=gkv, cqn=cqn, ckvn=ckvn,
                  qh=qh, kh=kh, vh=vh, lse=lse, cat=cat, ymix=ymix, x2=x2, x2b=x2b, qm=qm, km=km, vm=vm,
                  om=om, ymem=ymem)
        saved.append(sv)
        xres, xb = x4, x4b

    dx, loss_blk = _loss_head(xres, target, name="loss_head")
    loss = lax.psum(loss_blk[0, 0], ("x", "y", "c"))

    gW = {}
    gS = {}

    def ln_of(l, k):
        sv = saved[l]
        x, y, s = {0: (sv["ffn1"]["xres"], sv["ffn1"]["y"], 0.5), 1: (sv["x1"], sv["ymix"], 1.0),
                   2: (sv["x2"], sv["ymem"], 1.0), 3: (sv["ffn2"]["xres"], sv["ffn2"]["y"], 0.5)}[k]
        return x, y, LN["g"][l, k:k + 1], s

    def dx_through_ln(a, b, lead, tb, add, into, name):
        x, y, g, s = ln_of(*into)
        dxres, dyb, dg, db = _mm_ln_bwd(a, b, lead, tb, add, x, y, g, alpha=alpha, s=s, name=name)
        gS[("ln_g", *into)], gS[("ln_b", *into)] = dg, db
        return dxres, dyb

    def ffn_bwd(l, which, sv, dxres, dyb, ride, into):
        tag = f"l{l}_ffn{which}"
        gW[("w2", which, l)] = _mm(sv["a"], dyb, ta=True, out_dtype=BF16, name=f"{tag}_dw2", tn=D)
        dh, rode = _ffn_down_bwd(dyb, W[l]["w2" + "ab"[which]], 0, sv["h13"], name=f"{tag}_dh", ride=ride)
        dw13 = _mm(dh, sv["xb"], ta=True, out_dtype=BF16, name=f"{tag}_dw13", tn=D)
        gW[("w13", which, l)] = _deinterleave(dw13, 0)
        w13 = W[l]["w13" + "ab"[which]]
        if into is not None:
            return dx_through_ln(dh, w13, 0, False, dxres, into, f"{tag}_dx"), rode
        last = rs_first_level(l, "c")
        dxn, got = _mm(dh, w13, lead=0, add=dxres, name=f"{tag}_dx", tn=D, ride=last["ex"])
        rs_last_level(last, got)
        return dxn, rode

    core = lax.axis_index("c").astype(jnp.int32).reshape(1)
    chip = (2 * lax.axis_index("x") + lax.axis_index("y")).astype(jnp.int32).reshape(1)
    gsh = {}

    def rs_first_level(l, group):
        keys, classes = {
            "a": ([("w13", 1, l), ("w2", 1, l), ("mem_wkv", l), ("mem_wq", l), ("mem_wo", l)], [0] * 5),
            "b": ([("w_out", l), ("w_in", l), ("w_uq", l), ("w_ukv", l)], [0, 1, 2, 3]),
            "c": ([("w13", 0, l), ("w2", 0, l)], [0, 0]),
        }[group]
        tag = f"l{l}{group}"
        garrs = []
        for key in keys:
            g = gW[key]
            if key[0] == "w_uq":
                g = g.reshape(H, HEAD_PAD, Q_LORA)[:, :QW, :].reshape(H * QW, Q_LORA)
            garrs.append(g)
        offs, used = _class_layout(garrs, classes)
        parts = list(_rs_to_sibling(garrs, classes, name=f"{tag}_rs_sibling"))
        for cl in range(len(parts)):
            mine = [w for w, c in enumerate(classes) if c == cl]
            parts[cl] = _pair_sum(core, [garrs[w] for w in mine], parts[cl], [offs[w] for w in mine],
                                  name=f"{tag}_rs_pair_sum{cl}")
        return dict(tag=tag, keys=keys, garrs=garrs, classes=classes, offs=offs, used=used, parts=parts,
                    ex=_ChipExchange(parts, used))

    def rs_last_level(st, gathered):
        sums = [_quad_sum(chip, p, a, u, name=f"{st['tag']}_rs_quad_sum{k}")
                for k, (p, a, u) in enumerate(zip(st["parts"], gathered, st["used"]))]
        for key, g, cl, off in zip(st["keys"], st["garrs"], st["classes"], st["offs"]):
            gsh[key] = sums[cl][off:off + g.shape[0] // N_DEV, :]

    top = (L - 1, 3)
    x_top, y_top, g_top, s_top = ln_of(*top)
    dxres, dyb, gS[("ln_g", *top)], gS[("ln_b", *top)] = _ln_bwd(x_top, y_top, g_top, dx, alpha=alpha, s=s_top,
                                                                 name="ln_top_bwd")
    above = None
    for l in reversed(range(L)):
        sv = saved[l]
        Wl = W[l]
        (dxres, dyb), _ = ffn_bwd(l, 1, sv["ffn2"], dxres, dyb, None, (l, 2))
        dom = _mm(dyb, Wl["wsq"], lead=2, tb=True, out_dtype=BF16, name=f"l{l}_dom")
        gW[("mem_wo", l)] = _mm(sv["om"], dyb, ta=True, out_dtype=BF16, name=f"l{l}_dwo", tn=D)
        dqm, dkm, dvm = _mem_bwd(sv["qm"], sv["km"], sv["vm"], dom, name=f"l{l}_memattn_bwd")
        dxres, dyb = dx_through_ln(dqm, Wl["wsq"], 1, True, dxres, (l, 1), f"l{l}_dx2")
        gW[("mem_wq", l)] = _mm(sv["x2b"], dqm, ta=True, out_dtype=BF16, name=f"l{l}_dwq", tn=D)
        dkvm = jnp.concatenate([dkm, dvm], axis=1).astype(BF16)
        gW[("mem_wkv", l)] = _mm(dkvm, memb, ta=True, out_dtype=BF16, name=f"l{l}_dwkv", tn=D)
        dcat = _mm(dyb, Wl["wsq"], lead=0, tb=True, name=f"l{l}_dcat", tn=D)
        gW[("w_out", l)] = _mm(sv["cat"], dyb, ta=True, out_dtype=BF16, name=f"l{l}_dwout", tn=D)
        riding = [rs_first_level(l, "a")] + ([above] if above else [])
        both = _Both([st["ex"] for st in riding])
        dqh, dkh, dvh, rode = _flash_bwd(sv["qh"], sv["kh"], sv["vh"], sv["cat"], dcat, sv["lse"], H=H, pw=PW,
                                         name=f"l{l}_flash_bwd", ride=both)
        for st, got in zip(riding, both.split(rode)):
            rs_last_level(st, got)
        dqraw, dkv, dkpe = _heads_bwd(dqh, dkh, dvh, tabs, H=H, name=f"l{l}_heads_bwd")
        dcq = _mm(dqraw, Wl["wuqT"], lead=0, name=f"l{l}_dcq")
        gW[("w_uq", l)] = _mm(dqraw, sv["cqn"], ta=True, out_dtype=BF16, name=f"l{l}_dwuq")
        dckv = _mm(dkv, Wl["wukvT"], lead=0, name=f"l{l}_dckv")
        gW[("w_ukv", l)] = _mm(dkv, sv["ckvn"], ta=True, out_dtype=BF16, name=f"l{l}_dwukv")
        du, dwbd, dps = _pool_bwd(sv["hin"], dcat, wbd[l], sv["pscale"], name=f"l{l}_pool_bwd")
        dhin, dgq, dgkv = _norms_bwd(sv["hin"], sv["gq"], sv["gkv"], dcq, dckv, dkpe, du, pw=PW,
                                     name=f"l{l}_norms_bwd")
        pg = PW // len(POOL_WINDOWS)
        gS[("pool_w", l)] = jnp.stack([dwbd[g * pg:(g + 1) * pg, g * pg:(g + 1) * pg]
                                       for g in range(len(POOL_WINDOWS))])
        gS[("pool_scale", l)], gS[("q_norm_g", l)], gS[("kv_norm_g", l)] = dps, dgq, dgkv
        dxres, dyb = dx_through_ln(dhin, Wl["winp"], 0, True, dxres, (l, 0), f"l{l}_dx1")
        gW[("w_in", l)] = _mm(sv["x1b"], dhin, ta=True, out_dtype=BF16, name=f"l{l}_dwin", tn=DINP)
        heads = rs_first_level(l, "b")
        below, rode = ffn_bwd(l, 0, sv["ffn1"], dxres, dyb, heads["ex"], (l - 1, 3) if l > 0 else None)
        rs_last_level(heads, rode)
        if l > 0:
            dxres, dyb = below
            above = rs_first_level(l, "c")
    grad_x = below.reshape(1, T, D)

    small_keys = []
    for l in range(L):
        small_keys += [("pool_w", l), ("pool_scale", l), ("q_norm_g", l), ("kv_norm_g", l)]
        small_keys += [("ln_g", l, k) for k in range(4)] + [("ln_b", l, k) for k in range(4)]
    flat = jnp.concatenate([gS[k].reshape(-1) for k in small_keys])
    n_small = flat.shape[0]
    rows = -(-n_small // (8 * LANE)) * 8
    flat = jnp.pad(flat, (0, rows * LANE - n_small)).reshape(rows, LANE)
    red = _all_reduce_small(flat, name="ar_small").reshape(-1)
    gsm, pos = {}, 0
    for k in small_keys:
        size = math.prod(gS[k].shape)
        gsm[k] = red[pos:pos + size].reshape(gS[k].shape)
        pos += size

    me = 4 * lax.axis_index("x") + 2 * lax.axis_index("y") + lax.axis_index("c")
    dsh = D // N_DEV
    stack = lambda f: jnp.stack([f(l) for l in range(L)])
    g_ln_g = stack(lambda l: jnp.concatenate([gsm[("ln_g", l, k)] for k in range(4)], axis=0))
    g_ln_b = stack(lambda l: jnp.concatenate([gsm[("ln_b", l, k)] for k in range(4)], axis=0))
    grads = {
        "ln_g": lax.dynamic_slice_in_dim(g_ln_g, me * dsh, dsh, axis=2),
        "ln_b": lax.dynamic_slice_in_dim(g_ln_b, me * dsh, dsh, axis=2),
        "ffn1_w13": stack(lambda l: gsh[("w13", 0, l)].T),
        "ffn1_w2": stack(lambda l: gsh[("w2", 0, l)]),
        "w_in": stack(lambda l: gsh[("w_in", l)][:, :DIN]),
        "pool_w": stack(lambda l: gsm[("pool_w", l)]),
        "pool_scale": stack(lambda l: gsm[("pool_scale", l)].reshape(PW)),
        "q_norm_g": stack(lambda l: gsm[("q_norm_g", l)].reshape(Q_LORA)),
        "w_uq": stack(lambda l: gsh[("w_uq", l)].T),
        "kv_norm_g": stack(lambda l: gsm[("kv_norm_g", l)].reshape(KV_LORA)),
        "w_ukv": stack(lambda l: gsh[("w_ukv", l)].T),
        "w_out": stack(lambda l: gsh[("w_out", l)]),
        "mem_wq": stack(lambda l: gsh[("mem_wq", l)]),
        "mem_wkv": stack(lambda l: gsh[("mem_wkv", l)].T),
        "mem_wo": stack(lambda l: gsh[("mem_wo", l)]),
        "ffn2_w13": stack(lambda l: gsh[("w13", 1, l)].T),
        "ffn2_w2": stack(lambda l: gsh[("w2", 1, l)]),
    }

    names = ["ln_g", "ln_b", "ffn1_w13", "ffn1_w2", "w_in", "pool_w", "pool_scale", "q_norm_g", "w_uq",
             "kv_norm_g", "w_ukv", "w_out", "mem_wq", "mem_wkv", "mem_wo", "ffn2_w13", "ffn2_w2"]
    weights = dict(ln_g=ln_g, ln_b=ln_b, ffn1_w13=ffn1_w13, ffn1_w2=ffn1_w2, w_in=w_in, pool_w=pool_w,
                   pool_scale=pool_scale, q_norm_g=q_norm_g, w_uq=w_uq, kv_norm_g=kv_norm_g, w_ukv=w_ukv,
                   w_out=w_out, mem_wq=mem_wq, mem_wkv=mem_wkv, mem_wo=mem_wo, ffn2_w13=ffn2_w13,
                   ffn2_w2=ffn2_w2)
    ms = dict(ln_g=m_ln_g, ln_b=m_ln_b, ffn1_w13=m_ffn1_w13, ffn1_w2=m_ffn1_w2, w_in=m_w_in, pool_w=m_pool_w,
              pool_scale=m_pool_scale, q_norm_g=m_q_norm_g, w_uq=m_w_uq, kv_norm_g=m_kv_norm_g,
              w_ukv=m_w_ukv, w_out=m_w_out, mem_wq=m_mem_wq, mem_wkv=m_mem_wkv, mem_wo=m_mem_wo,
              ffn2_w13=m_ffn2_w13, ffn2_w2=m_ffn2_w2)
    vs = dict(ln_g=v_ln_g, ln_b=v_ln_b, ffn1_w13=v_ffn1_w13, ffn1_w2=v_ffn1_w2, w_in=v_w_in, pool_w=v_pool_w,
              pool_scale=v_pool_scale, q_norm_g=v_q_norm_g, w_uq=v_w_uq, kv_norm_g=v_kv_norm_g,
              w_ukv=v_w_ukv, w_out=v_w_out, mem_wq=v_mem_wq, mem_wkv=v_mem_wkv, mem_wo=v_mem_wo,
              ffn2_w13=v_ffn2_w13, ffn2_w2=v_ffn2_w2)
    deltas, new_m, new_v = [], [], []
    for nme in names:
        d, mn, vn = _adamw(weights[nme], grads[nme], ms[nme], vs[nme], name=f"adamw_{nme}")
        deltas.append(d)
        new_m.append(mn)
        new_v.append(vn)
    return (loss, grad_x, *[grads[nme] for nme in names], *deltas, *new_m, *new_v)
```

```python
import functools
import math

import jax
import jax.numpy as jnp
from jax import lax
from jax.experimental import pallas as pl
from jax.experimental.pallas import tpu as pltpu

F32 = jnp.float32
BF16 = jnp.bfloat16
MESH = pl.DeviceIdType.MESH

CHUNK = 64
MEM_HEADS = 4
POOL_WINDOWS = (2, 4, 8, 16)
QK_NOPE = 128
QK_ROPE = 64
V_HEAD = 128
Q_LORA = 256
KV_LORA = 128
ROPE_BASE = 10000.0
LN_EPS = 1e-5
RMS_EPS = 1e-6
NEG_INF = -1e30
ADAM_LR = 0.001
ADAM_B1 = 0.9
ADAM_B2 = 0.999
ADAM_EPS = 1e-08
ADAM_WD = 0.01
ADAM_STEP = 10

N_DEV = 8
LANE = 128
HEAD_PAD = 2 * LANE
POOL_HALO = 16
VMEM_CAP = 56 * 1024 * 1024
VMEM_FLOOR = 32 * 1024 * 1024


def _tile(n, pref, mult):
    t = (min(pref, n) // mult) * mult
    while t >= mult:
        if n % t == 0:
            return t
        t -= mult
    return n


def _params(sem, est_bytes):
    limit = int(min(max(2 * est_bytes + (8 << 20), VMEM_FLOOR), VMEM_CAP))
    return pltpu.CompilerParams(dimension_semantics=sem, vmem_limit_bytes=limit)


def _nbytes(shape, dtype):
    return math.prod(shape) * jnp.dtype(dtype).itemsize


def _hbm(x):
    return pltpu.with_memory_space_constraint(x, pltpu.HBM)


def _out(shape, dtype):
    return pltpu.HBM(tuple(shape), dtype)


def _dg(a, b, ca, cb):
    return lax.dot_general(a.astype(BF16), b.astype(BF16), (((ca,), (cb,)), ((), ())),
                           preferred_element_type=F32)


@jax.custom_vjp
def _bdot_nn(a, b):
    return _dg(a, b, 1, 0)


def _bdot_nn_fwd(a, b):
    return _dg(a, b, 1, 0), (a, b)


def _bdot_nn_bwd(res, ct):
    a, b = res
    return _dg(ct, b, 1, 1).astype(a.dtype), _dg(a, ct, 0, 0).astype(b.dtype)


_bdot_nn.defvjp(_bdot_nn_fwd, _bdot_nn_bwd)


@jax.custom_vjp
def _bdot_nt(a, b):
    return _dg(a, b, 1, 1)


def _bdot_nt_fwd(a, b):
    return _dg(a, b, 1, 1), (a, b)


def _bdot_nt_bwd(res, ct):
    a, b = res
    return _dg(ct, b, 1, 0).astype(a.dtype), _dg(ct, a, 0, 0).astype(b.dtype)


_bdot_nt.defvjp(_bdot_nt_fwd, _bdot_nt_bwd)


@functools.partial(jax.custom_vjp, nondiff_argnums=(1,))
def _lane_roll(x, shift):
    return pltpu.roll(x, shift % x.shape[1], axis=1)


def _lane_roll_fwd(x, shift):
    return _lane_roll(x, shift), None


def _lane_roll_bwd(shift, _, ct):
    return (_lane_roll(ct, -shift),)


_lane_roll.defvjp(_lane_roll_fwd, _lane_roll_bwd)


@functools.partial(jax.custom_vjp, nondiff_argnums=(1, 2))
def _cols(x, lo, hi):
    return x[:, lo:hi]


def _cols_fwd(x, lo, hi):
    return x[:, lo:hi], x.shape[1]


def _cols_bwd(lo, hi, width, ct):
    parts = []
    if lo > 0:
        parts.append(jnp.zeros((ct.shape[0], lo), ct.dtype))
    parts.append(ct)
    if hi < width:
        parts.append(jnp.zeros((ct.shape[0], width - hi), ct.dtype))
    return (jnp.concatenate(parts, axis=1) if len(parts) > 1 else ct,)


_cols.defvjp(_cols_fwd, _cols_bwd)


MM_VMEM_BUDGET = 20 * 1024 * 1024


def _mm(a, b, *, name, ta=False, tb=False, out_dtype=F32, lead=None, add=None, add_scale=1.0,
        tm=1024, tn=1024, tk=2816, ride=None):
    if ta:
        K, M = a.shape
    else:
        M, K = a.shape
    bshape = b.shape[1:] if lead is not None else b.shape
    if tb:
        N, Kb = bshape
    else:
        Kb, N = bshape
    assert K == Kb, (name, a.shape, b.shape)

    def blocks(tm, tn, tk):
        tm = _tile(M, tm, LANE if ta else 16)
        tn = _tile(N, tn, LANE)
        tk = _tile(K, tk, LANE)
        nbytes = (tm * tk * a.dtype.itemsize + tk * tn * b.dtype.itemsize
                  + tm * tn * (jnp.dtype(out_dtype).itemsize + (4 if K // tk > 1 else 0)
                               + (add.dtype.itemsize if add is not None else 0)))
        return tm, tn, tk, nbytes

    tm, tn, tk, est = blocks(tm, tn, tk)
    for shrink in ("m", "k", "m", "k", "n"):
        if est <= MM_VMEM_BUDGET:
            break
        if shrink == "m":
            tm, tn, tk, est = blocks(max(tm // 2, LANE), tn, tk)
        elif shrink == "k":
            tm, tn, tk, est = blocks(tm, tn, max(tk // 2, LANE))
        else:
            tm, tn, tk, est = blocks(tm, max(tn // 2, LANE), tk)
    nk = K // tk
    ca = 0 if ta else 1
    cb = 1 if tb else 0

    def body(*refs):
        a_ref, b_ref = refs[0], refs[1]
        add_ref = refs[2] if add is not None else None
        o_ref = refs[3] if add is not None else refs[2]

        def finish(r):
            if add_ref is not None:
                r = r + add_scale * add_ref[...].astype(F32)
            o_ref[...] = r.astype(o_ref.dtype)

        if nk == 1:
            finish(_dg(a_ref[...], b_ref[...], ca, cb))
            return
        acc_ref = refs[-1]
        k = pl.program_id(2)

        @pl.when(k == 0)
        def _():
            acc_ref[...] = jnp.zeros_like(acc_ref)

        acc_ref[...] += _dg(a_ref[...], b_ref[...], ca, cb)

        @pl.when(k == nk - 1)
        def _():
            finish(acc_ref[...])

    a_blk = (tk, tm) if ta else (tm, tk)
    a_map = (lambda i, j, k: (k, i)) if ta else (lambda i, j, k: (i, k))
    b_blk = (tn, tk) if tb else (tk, tn)
    if lead is None:
        b_map = (lambda i, j, k: (j, k)) if tb else (lambda i, j, k: (k, j))
        b_spec = pl.BlockSpec(b_blk, b_map)
    else:
        b_map = (lambda i, j, k: (lead, j, k)) if tb else (lambda i, j, k: (lead, k, j))
        b_spec = pl.BlockSpec((None,) + b_blk, b_map)
    in_specs = [pl.BlockSpec(a_blk, a_map), b_spec]
    args = [a, b]
    if add is not None:
        in_specs.append(pl.BlockSpec((tm, tn), lambda i, j, k: (i, j)))
        args.append(add)
    (out,), rode = _host_call(
        body, name=name,
        grid=(M // tm, N // tn, nk),
        in_specs=in_specs,
        out_specs=[pl.BlockSpec((tm, tn), lambda i, j, k: (i, j))],
        out_shape=[_out((M, N), out_dtype)],
        scratch=[pltpu.VMEM((tm, tn), F32)] if nk > 1 else [],
        args=[_hbm(v) for v in args], sem=("parallel", "parallel", "arbitrary"), est=est + tm * tn * 4,
        ride=ride)
    return out if ride is None else (out, rode)


def _rowwise(fn, tiles, params, tile_outs, acc_outs=(), *, tm, name):
    tile_arrays, tile_specs = [], []
    for t in tiles:
        if isinstance(t, tuple):
            tile_arrays.append(t[0])
            tile_specs.append(t[1])
        else:
            tile_arrays.append(t)
            tile_specs.append(pl.BlockSpec((tm, t.shape[1]), lambda i: (i, 0)))
    T = tile_arrays[0].shape[0]
    nt, np_, nto, nao = len(tile_arrays), len(params), len(tile_outs), len(acc_outs)

    def body(*refs):
        i = pl.program_id(0)
        tvals = [r[...] for r in refs[:nt]]
        pvals = [r[...] for r in refs[nt:nt + np_]]
        to_refs = refs[nt + np_:nt + np_ + nto]
        ao_refs = refs[nt + np_ + nto:]
        touts, aouts = fn(i, tvals, pvals)
        for r, v in zip(to_refs, touts):
            r[...] = v.astype(r.dtype)
        if nao:
            @pl.when(i == 0)
            def _():
                for r in ao_refs:
                    r[...] = jnp.zeros_like(r)
            for r, v in zip(ao_refs, aouts):
                r[...] += v.astype(r.dtype)

    in_specs = tile_specs + [pl.BlockSpec(p.shape, lambda i: (0, 0)) for p in params]
    out_specs = [pl.BlockSpec((tm, c), lambda i: (i, 0)) for c, _ in tile_outs]
    out_specs += [pl.BlockSpec(s, lambda i: (0, 0)) for s, _ in acc_outs]
    out_shape = [_out((T, c), d) for c, d in tile_outs]
    out_shape += [_out(s, d) for s, d in acc_outs]
    width = sum(s.block_shape[-1] for s in tile_specs) + sum(c for c, _ in tile_outs)
    est = 6 * tm * width * 4 + sum(_nbytes(p.shape, F32) for p in params) * 4
    return pl.pallas_call(
        body, name=name, grid=(T // tm,),
        in_specs=in_specs, out_specs=out_specs, out_shape=out_shape,
        compiler_params=_params(("arbitrary",) if nao else ("parallel",), est),
    )(*[_hbm(v) for v in tile_arrays], *[_hbm(p) for p in params])


def _ln_fn(alpha, s, xres, y, g, b):
    z = alpha * xres.astype(F32) + s * y.astype(F32)
    mu = jnp.mean(z, axis=-1, keepdims=True)
    zc = z - mu
    var = jnp.mean(zc * zc, axis=-1, keepdims=True)
    return zc * lax.rsqrt(var + LN_EPS) * g + b


def _mm_ln(a, b, lead, xres, g, bias, *, alpha, s, name):
    M, K = a.shape
    N = b.shape[2]
    tm = _tile(M, 256, 16)

    def body(a_ref, b_ref, x_ref, g_ref, bias_ref, y_ref, xo_ref, xb_ref):
        y = _dg(a_ref[...], b_ref[...], 1, 0)
        y_ref[...] = y.astype(y_ref.dtype)
        out = _ln_fn(alpha, s, x_ref[...], y, g_ref[...], bias_ref[...])
        xo_ref[...] = out
        xb_ref[...] = out.astype(BF16)

    row = pl.BlockSpec((tm, N), lambda i: (i, 0))
    vec = pl.BlockSpec((1, N), lambda i: (0, 0))
    est = tm * K * 2 + K * N * 2 + tm * N * (4 + 4 + 4 + 2 + 8)
    return pl.pallas_call(
        body, name=name, grid=(M // tm,),
        in_specs=[pl.BlockSpec((tm, K), lambda i: (i, 0)), pl.BlockSpec((None, K, N), lambda i: (lead, 0, 0)),
                  row, vec, vec],
        out_specs=[row, row, row],
        out_shape=[_out((M, N), BF16), _out((M, N), F32), _out((M, N), BF16)],
        compiler_params=_params(("parallel",), est),
    )(_hbm(a), _hbm(b), _hbm(xres), _hbm(g), _hbm(bias))


def _ln_bwd_math(alpha, s, x, y, g, d):
    z = alpha * x + s * y.astype(F32)
    zc = z - jnp.mean(z, axis=-1, keepdims=True)
    r = lax.rsqrt(jnp.mean(zc * zc, axis=-1, keepdims=True) + LN_EPS)
    xh = zc * r
    dxh = d * g
    dz = r * (dxh - jnp.mean(dxh, axis=-1, keepdims=True) - xh * jnp.mean(dxh * xh, axis=-1, keepdims=True))
    return alpha * dz, s * dz, jnp.sum(d * xh, axis=0, keepdims=True), jnp.sum(d, axis=0, keepdims=True)


def _mm_ln_bwd(a, b, lead, tb, add, xres, y, g, *, alpha, s, name):
    M, K = a.shape
    N = b.shape[1] if tb else b.shape[2]
    tm = _tile(M, 512, 16)
    tk = _tile(K, 2816, LANE)
    nk = K // tk
    cb = 1 if tb else 0

    def body(a_ref, b_ref, add_ref, x_ref, y_ref, g_ref, dx_ref, dy_ref, dg_ref, db_ref, *scratch):
        i, k = pl.program_id(0), pl.program_id(1)

        def finish(d):
            @pl.when(i == 0)
            def _():
                dg_ref[...] = jnp.zeros_like(dg_ref)
                db_ref[...] = jnp.zeros_like(db_ref)

            dx, dy, dg, db = _ln_bwd_math(alpha, s, x_ref[...], y_ref[...], g_ref[...], d + add_ref[...])
            dx_ref[...] = dx
            dy_ref[...] = dy.astype(dy_ref.dtype)
            dg_ref[...] += dg
            db_ref[...] += db

        if nk == 1:
            finish(_dg(a_ref[...], b_ref[...], 1, cb))
            return
        acc_ref = scratch[0]

        @pl.when(k == 0)
        def _():
            acc_ref[...] = jnp.zeros_like(acc_ref)

        acc_ref[...] += _dg(a_ref[...], b_ref[...], 1, cb)

        @pl.when(k == nk - 1)
        def _():
            finish(acc_ref[...])

    row = pl.BlockSpec((tm, N), lambda i, k: (i, 0))
    vec = pl.BlockSpec((1, N), lambda i, k: (0, 0))
    b_spec = (pl.BlockSpec((None, N, tk), lambda i, k: (lead, 0, k)) if tb
              else pl.BlockSpec((None, tk, N), lambda i, k: (lead, k, 0)))
    est = tm * tk * 2 + tk * N * 2 + tm * N * (4 + 4 + 2 + 4 + 2 + 4 + 12)
    return pl.pallas_call(
        body, name=name, grid=(M // tm, nk),
        in_specs=[pl.BlockSpec((tm, tk), lambda i, k: (i, k)), b_spec, row, row, row, vec],
        out_specs=[row, row, vec, vec],
        out_shape=[_out((M, N), F32), _out((M, N), BF16), _out((1, N), F32), _out((1, N), F32)],
        scratch_shapes=[pltpu.VMEM((tm, N), F32)] if nk > 1 else [],
        compiler_params=_params(("arbitrary", "arbitrary"), est),
    )(_hbm(a), _hbm(b), _hbm(add), _hbm(xres), _hbm(y), _hbm(g))


def _loss_ln_bwd(xres, y, g, out, target, *, alpha, s, name):
    T, D = xres.shape
    tm = _tile(T, 256, 16)

    def body(x_ref, y_ref, o_ref, t_ref, g_ref, dx_ref, dy_ref, dg_ref, db_ref, loss_ref):
        @pl.when(pl.program_id(0) == 0)
        def _():
            dg_ref[...] = jnp.zeros_like(dg_ref)
            db_ref[...] = jnp.zeros_like(db_ref)
            loss_ref[...] = jnp.zeros_like(loss_ref)

        err = o_ref[...] - t_ref[...]
        part = 0.5 * jnp.sum(jnp.sum(err * err, axis=1, keepdims=True) / D, axis=0, keepdims=True)
        loss_ref[...] += jnp.broadcast_to(part, loss_ref.shape)
        dx, dy, dg, db = _ln_bwd_math(alpha, s, x_ref[...], y_ref[...], g_ref[...], err / D)
        dx_ref[...] = dx
        dy_ref[...] = dy.astype(dy_ref.dtype)
        dg_ref[...] += dg
        db_ref[...] += db

    row = pl.BlockSpec((tm, D), lambda i: (i, 0))
    vec = pl.BlockSpec((1, D), lambda i: (0, 0))
    return pl.pallas_call(
        body, name=name, grid=(T // tm,),
        in_specs=[row, row, row, row, vec],
        out_specs=[row, row, vec, vec, pl.BlockSpec((8, LANE), lambda i: (0, 0))],
        out_shape=[_out((T, D), F32), _out((T, D), BF16), _out((1, D), F32), _out((1, D), F32),
                   _out((8, LANE), F32)],
        compiler_params=_params(("arbitrary",), 14 * tm * D * 4),
    )(_hbm(xres), _hbm(y), _hbm(out), _hbm(target), _hbm(g))


FFN_TILE = 256


def _interleave(w, axis):
    n = w.shape[axis] // (2 * FFN_TILE)
    shp = w.shape[:axis] + (2, n, FFN_TILE) + w.shape[axis + 1:]
    return jnp.swapaxes(w.reshape(shp), axis, axis + 1).reshape(w.shape)


def _deinterleave(w, axis):
    n = w.shape[axis] // (2 * FFN_TILE)
    shp = w.shape[:axis] + (n, 2, FFN_TILE) + w.shape[axis + 1:]
    return jnp.swapaxes(w.reshape(shp), axis, axis + 1).reshape(w.shape)


def _ffn_up(xb, w13t, lead, *, name, ride=None):
    T, D = xb.shape
    F = w13t.shape[1] // 2
    tc = FFN_TILE
    tm = _tile(T, 1024, 16)

    def body(x_ref, w_ref, h_ref, a_ref):
        h = _dg(x_ref[...], w_ref[...], 1, 1)
        g, u = h[:, :tc], h[:, tc:]
        h_ref[...] = h.astype(h_ref.dtype)
        a_ref[...] = (g * jax.nn.sigmoid(g) * u).astype(a_ref.dtype)

    est = (tm * D + 2 * tc * D + 3 * tm * tc) * 2 + 3 * tm * tc * 4
    (h13, a), gathered = _host_call(
        body, name=name, grid=(T // tm, F // tc),
        in_specs=[pl.BlockSpec((tm, D), lambda i, j: (i, 0)),
                  pl.BlockSpec((None, 2 * tc, D), lambda i, j: (lead, j, 0))],
        out_specs=[pl.BlockSpec((tm, 2 * tc), lambda i, j: (i, j)),
                   pl.BlockSpec((tm, tc), lambda i, j: (i, j))],
        out_shape=[_out((T, 2 * F), BF16), _out((T, F), BF16)],
        args=[_hbm(xb), _hbm(w13t)], sem=("parallel", "parallel"), est=est, ride=ride)
    return h13, a, gathered


def _ffn_down_bwd(dyb, w2, lead, h13, *, name, ride=None):
    T, D = dyb.shape
    F = w2.shape[1]
    tc = FFN_TILE
    tm = _tile(T, 1024, 16)

    def body(dy_ref, w_ref, h_ref, dh_ref):
        d = _dg(dy_ref[...], w_ref[...], 1, 1)
        h = h_ref[...].astype(F32)
        g, u = h[:, :tc], h[:, tc:]
        sig = jax.nn.sigmoid(g)
        gs = g * sig
        dh_ref[...] = jnp.concatenate([d * u * (sig + gs * (1.0 - sig)), d * gs], axis=1).astype(dh_ref.dtype)

    est = (tm * D + tc * D + 4 * tm * tc) * 2 + 6 * tm * tc * 4
    (dh,), rode = _host_call(
        body, name=name, grid=(T // tm, F // tc),
        in_specs=[pl.BlockSpec((tm, D), lambda i, j: (i, 0)),
                  pl.BlockSpec((None, tc, D), lambda i, j: (lead, j, 0)),
                  pl.BlockSpec((tm, 2 * tc), lambda i, j: (i, j))],
        out_specs=[pl.BlockSpec((tm, 2 * tc), lambda i, j: (i, j))],
        out_shape=[_out((T, 2 * F), BF16)],
        args=[_hbm(dyb), _hbm(w2), _hbm(h13)], sem=("parallel", "parallel"), est=est, ride=ride)
    return dh, rode


def _pool_select(parts, pw):
    pg = pw // len(POOL_WINDOWS)
    grp = lax.broadcasted_iota(jnp.int32, parts[0].shape, 1) // pg
    out = parts[3]
    for g in (2, 1, 0):
        out = jnp.where(grp == g, parts[g], out)
    return out


def _pool_count(t0, rows, pw):
    pg = pw // len(POOL_WINDOWS)
    grp = lax.broadcasted_iota(jnp.int32, (rows, pw), 1) // pg
    win = jnp.where(grp == 0, POOL_WINDOWS[0],
                    jnp.where(grp == 1, POOL_WINDOWS[1],
                              jnp.where(grp == 2, POOL_WINDOWS[2], POOL_WINDOWS[3])))
    t = t0 + lax.broadcasted_iota(jnp.int32, (rows, pw), 0)
    return jnp.minimum(t + 1, win).astype(F32)


def _window_sums(ext, up):
    n = ext.shape[0]
    sums, cur, k = [], ext, 1
    for _ in POOL_WINDOWS:
        cur = cur + pltpu.roll(cur, (n - k) if up else k, axis=0)
        sums.append(cur)
        k *= 2
    return sums


def _pool_delta(u, halo, t0):
    tm, pw = u.shape
    ext = jnp.concatenate([halo, u], axis=0)
    sums = [s[POOL_HALO:, :] for s in _window_sums(ext, up=False)]
    return _pool_select(sums, pw) / _pool_count(t0, tm, pw) - u


def _pool_fwd(hin, wbd, scale, cat, *, name):
    T = hin.shape[0]
    pw = wbd.shape[0]
    tm = _tile(T, 256, POOL_HALO)
    per = tm // POOL_HALO

    def body(u_ref, halo_ref, w_ref, s_ref, cat_ref, y_ref):
        i = pl.program_id(0)
        halo = jnp.where(i > 0, halo_ref[...], 0.0)
        d = _pool_delta(u_ref[...], halo, i * tm)
        y_ref[...] = (_dg(d, w_ref[...], 1, 0) * s_ref[...]).astype(y_ref.dtype)

    return pl.pallas_call(
        body, name=name, grid=(T // tm,),
        in_specs=[pl.BlockSpec((tm, pw), lambda i: (i, 0)),
                  pl.BlockSpec((POOL_HALO, pw), lambda i: (jnp.maximum(i * per - 1, 0), 0)),
                  pl.BlockSpec((pw, pw), lambda i: (0, 0)),
                  pl.BlockSpec((1, pw), lambda i: (0, 0)),
                  ANY],
        out_specs=pl.BlockSpec((tm, pw), lambda i: (i, 0)),
        out_shape=_out(cat.shape, cat.dtype),
        input_output_aliases={4: 0},
        compiler_params=_params(("parallel",), 16 * tm * pw * 4),
    )(_hbm(hin), _hbm(hin), _hbm(wbd), _hbm(scale), _hbm(cat))


def _pool_bwd(hin, dcat, wbd, scale, *, name):
    T = hin.shape[0]
    pw = wbd.shape[0]
    tm = _tile(T, 256, POOL_HALO)
    per = tm // POOL_HALO
    nt = T // tm

    def body(u_ref, halo_ref, dy_ref, dyn_ref, w_ref, s_ref, du_ref, dw_ref, ds_ref):
        i = pl.program_id(0)

        @pl.when(i == 0)
        def _():
            dw_ref[...] = jnp.zeros_like(dw_ref)
            ds_ref[...] = jnp.zeros_like(ds_ref)

        halo = jnp.where(i > 0, halo_ref[...], 0.0)
        d = _pool_delta(u_ref[...], halo, i * tm)
        w = w_ref[...]
        sc = s_ref[...]
        dy = dy_ref[...]
        dyn = jnp.where(i < nt - 1, dyn_ref[...], 0.0)
        ds_ref[...] += jnp.sum(dy * _dg(d, w, 1, 0), axis=0, keepdims=True)
        dys = dy * sc
        dw_ref[...] += _dg(d, dys, 0, 0)
        dys_ext = jnp.concatenate([dys, dyn * sc], axis=0)
        dd_ext = _dg(dys_ext, w, 1, 1)
        ddp = dd_ext / _pool_count(i * tm, tm + POOL_HALO, pw)
        sums = [s[:tm, :] for s in _window_sums(ddp, up=True)]
        du_ref[...] = _pool_select(sums, pw) - dd_ext[:tm, :]

    return pl.pallas_call(
        body, name=name, grid=(nt,),
        in_specs=[pl.BlockSpec((tm, pw), lambda i: (i, 0)),
                  pl.BlockSpec((POOL_HALO, pw), lambda i: (jnp.maximum(i * per - 1, 0), 0)),
                  pl.BlockSpec((tm, pw), lambda i: (i, 0)),
                  pl.BlockSpec((POOL_HALO, pw), lambda i: (jnp.minimum((i + 1) * per, nt * per - 1), 0)),
                  pl.BlockSpec((pw, pw), lambda i: (0, 0)),
                  pl.BlockSpec((1, pw), lambda i: (0, 0))],
        out_specs=[pl.BlockSpec((tm, pw), lambda i: (i, 0)),
                   pl.BlockSpec((pw, pw), lambda i: (0, 0)),
                   pl.BlockSpec((1, pw), lambda i: (0, 0))],
        out_shape=[_out((T, pw), F32),
                   _out((pw, pw), F32),
                   _out((1, pw), F32)],
        compiler_params=_params(("arbitrary",), 24 * tm * pw * 4),
    )(_hbm(hin), _hbm(hin), _hbm(dcat), _hbm(dcat), _hbm(wbd), _hbm(scale))


def _rms(x, g):
    return x * lax.rsqrt(jnp.mean(x * x, axis=-1, keepdims=True) + RMS_EPS) * g


def _norms_fn(pw, h, gq, gkv):
    o1 = pw + Q_LORA
    o2 = o1 + KV_LORA
    return (_rms(_cols(h, pw, o1), gq), _rms(_cols(h, o1, o2), gkv), _cols(h, o2, h.shape[1]))


def _norms_fwd(hin, gq, gkv, *, pw, name):
    tm = _tile(hin.shape[0], 256, 16)

    def fn(i, tv, pv):
        return _norms_fn(pw, tv[0], pv[0], pv[1]), ()

    return _rowwise(fn, [hin], [gq, gkv], [(Q_LORA, BF16), (KV_LORA, BF16), (LANE, F32)], tm=tm, name=name)


def _norms_bwd(hin, gq, gkv, dcq, dckv, dkpe, du, *, pw, name):
    tm = _tile(hin.shape[0], 256, 16)
    dinp = hin.shape[1]

    def fn(i, tv, pv):
        _, vjp = jax.vjp(functools.partial(_norms_fn, pw), tv[0], pv[0], pv[1])
        dh, dgq, dgkv = vjp((tv[1].astype(F32), tv[2].astype(F32), tv[3].astype(F32)))
        dh = jnp.concatenate([tv[4], dh[:, pw:]], axis=1)
        return (dh,), (dgq, dgkv)

    return _rowwise(fn, [hin, dcq, dckv, dkpe, du], [gq, gkv], [(dinp, BF16)],
                    [((1, Q_LORA), F32), ((1, KV_LORA), F32)], tm=tm, name=name)


def _heads_fn(H, qraw, kv, kpe, rc, rs1, rs2):
    half = QK_ROPE // 2
    scale = (QK_NOPE + QK_ROPE) ** -0.5

    def rope(blk):
        return blk * rc + _lane_roll(blk, -half) * rs1 + _lane_roll(blk, half) * rs2

    krot = rope(kpe)
    qs, ks, vs = [], [], []
    for h in range(H):
        lo = h * HEAD_PAD
        qs += [_cols(qraw, lo, lo + LANE) * scale, rope(_cols(qraw, lo + LANE, lo + HEAD_PAD)) * scale]
        ks += [_cols(kv, lo, lo + LANE), krot]
        vs += [_cols(kv, lo + LANE, lo + HEAD_PAD)]
    return jnp.concatenate(qs, axis=1), jnp.concatenate(ks, axis=1), jnp.concatenate(vs, axis=1)


def _heads_fwd(cqn, ckvn, kpe, tabs, wuq, wukv, *, H, name):
    tm = _tile(cqn.shape[0], 256, 16)

    def fn(i, tv, pv):
        qraw = _dg(tv[0], pv[0], 1, 1)
        kv = _dg(tv[1], pv[1], 1, 1)
        return _heads_fn(H, qraw, kv, *tv[2:]), ()

    return _rowwise(fn, [cqn, ckvn, kpe, *tabs], [wuq, wukv],
                    [(H * HEAD_PAD, BF16), (H * HEAD_PAD, BF16), (H * V_HEAD, BF16)], tm=tm, name=name)


def _heads_bwd(dq, dk, dv, tabs, *, H, name):
    tm = _tile(dq.shape[0], 256, 16)

    def fn(i, tv, pv):
        z = jnp.zeros((tm, H * HEAD_PAD), F32)
        zk = jnp.zeros((tm, LANE), F32)
        rc, rs1, rs2 = tv[3], tv[4], tv[5]
        _, vjp = jax.vjp(lambda a, b, c: _heads_fn(H, a, b, c, rc, rs1, rs2), z, z, zk)
        return vjp((tv[0].astype(F32), tv[1].astype(F32), tv[2].astype(F32))), ()

    return _rowwise(fn, [dq, dk, dv, *tabs], [],
                    [(H * HEAD_PAD, BF16), (H * HEAD_PAD, BF16), (LANE, F32)], tm=tm, name=name)


def _diag_mask(rows, cols, row0):
    r = (row0 + lax.broadcasted_iota(jnp.int32, (rows, cols), 0)) // CHUNK
    c = lax.broadcasted_iota(jnp.int32, (rows, cols), 1) // CHUNK
    return r >= c


def _flash_fwd(qh, kh, vh, *, H, pw, name, ride=None):
    T = qh.shape[0]
    t = _tile(T, 512, CHUNK)
    off = pw // V_HEAD


    def body(q_ref, k_ref, v_ref, o_ref, lse_ref):
        i = pl.program_id(1)
        q = q_ref[...]

        def blk(j, carry, masked):
            m, l, acc = carry
            rows = pl.ds(pl.multiple_of(j * t, t), t)
            s = _dg(q, k_ref[rows, :], 1, 1)
            if masked:
                s = jnp.where(_diag_mask(t, t, 0), s, NEG_INF)
            mn = jnp.maximum(m, jnp.max(s, axis=1, keepdims=True))
            p = jnp.exp(s - mn)
            corr = jnp.exp(m - mn)
            l = corr * l + jnp.sum(p, axis=1, keepdims=True)
            acc = corr * acc + _dg(p, v_ref[rows, :], 1, 0)
            return mn, l, acc

        init = (jnp.full((t, 1), NEG_INF, F32), jnp.zeros((t, 1), F32), jnp.zeros((t, V_HEAD), F32))
        carry = lax.fori_loop(0, i, lambda j, c: blk(j, c, False), init)
        m, l, acc = blk(i, carry, True)
        o_ref[...] = (acc / l).astype(o_ref.dtype)
        lse_ref[...] = jnp.broadcast_to(m + jnp.log(l), (t, V_HEAD))

    est = 2 * T * (HEAD_PAD + V_HEAD) * 2 + 8 * t * t * 4
    (o, lse), gathered = _host_call(
        body, name=name, grid=(H, T // t),
        in_specs=[pl.BlockSpec((t, HEAD_PAD), lambda h, i: (i, h)),
                  pl.BlockSpec((T, HEAD_PAD), lambda h, i: (0, h)),
                  pl.BlockSpec((T, V_HEAD), lambda h, i: (0, h))],
        out_specs=[pl.BlockSpec((t, V_HEAD), lambda h, i: (i, off + h)),
                   pl.BlockSpec((t, V_HEAD), lambda h, i: (i, h))],
        out_shape=[_out((T, pw + H * V_HEAD), BF16),
                   _out((T, H * V_HEAD), F32)],
        args=[_hbm(qh), _hbm(kh), _hbm(vh)], sem=("parallel", "parallel"), est=est, ride=ride)
    return o, lse, gathered


def _flash_bwd(qh, kh, vh, cat, dcat, lse, *, H, pw, name, ride=None):
    T = qh.shape[0]
    t = _tile(T, 512, CHUNK)
    nb = T // t
    off = pw // V_HEAD

    def body(q_ref, k_ref, v_ref, o_ref, do_ref, lse_ref, dq_out_ref, dk_ref, dv_ref, dq_ref):
        j = pl.program_id(1)

        @pl.when(j == 0)
        def _():
            dq_ref[...] = jnp.zeros_like(dq_ref)

        kj = k_ref[...]
        vj = v_ref[...]

        def blk(i, carry, masked):
            dk, dv = carry
            rows = pl.ds(pl.multiple_of(i * t, t), t)
            qi = q_ref[rows, :]
            doi = do_ref[rows, :]
            oi = o_ref[rows, :].astype(F32)
            lsei = lse_ref[rows, :][:, :1]
            s = _dg(qi, kj, 1, 1)
            if masked:
                s = jnp.where(_diag_mask(t, t, 0), s, NEG_INF)
            p = jnp.exp(s - lsei)
            dv = dv + _dg(p, doi, 0, 0)
            dp = _dg(doi, vj, 1, 1)
            di = jnp.sum(doi * oi, axis=1, keepdims=True)
            ds = p * (dp - di)
            dk = dk + _dg(ds, qi, 0, 0)
            dq_ref[rows, :] += _dg(ds, kj, 1, 0)
            return dk, dv

        carry = blk(j, (jnp.zeros((t, HEAD_PAD), F32), jnp.zeros((t, V_HEAD), F32)), True)
        dk, dv = lax.fori_loop(j + 1, nb, lambda i, c: blk(i, c, False), carry)
        dk_ref[...] = dk.astype(dk_ref.dtype)
        dv_ref[...] = dv.astype(dv_ref.dtype)

        @pl.when(j == nb - 1)
        def _():
            dq_out_ref[...] = dq_ref[...].astype(dq_out_ref.dtype)

    est = T * (HEAD_PAD * 2 + V_HEAD * 2 + V_HEAD * 4 + V_HEAD * 4 + HEAD_PAD * 4) + 10 * t * t * 4
    (dq, dk, dv), gathered = _host_call(
        body, name=name, grid=(H, nb),
        in_specs=[pl.BlockSpec((T, HEAD_PAD), lambda h, j: (0, h)),
                  pl.BlockSpec((t, HEAD_PAD), lambda h, j: (j, h)),
                  pl.BlockSpec((t, V_HEAD), lambda h, j: (j, h)),
                  pl.BlockSpec((T, V_HEAD), lambda h, j: (0, off + h)),
                  pl.BlockSpec((T, V_HEAD), lambda h, j: (0, off + h)),
                  pl.BlockSpec((T, V_HEAD), lambda h, j: (0, h))],
        out_specs=[pl.BlockSpec((T, HEAD_PAD), lambda h, j: (0, h)),
                   pl.BlockSpec((t, HEAD_PAD), lambda h, j: (j, h)),
                   pl.BlockSpec((t, V_HEAD), lambda h, j: (j, h))],
        out_shape=[_out((T, H * HEAD_PAD), BF16),
                   _out((T, H * HEAD_PAD), BF16),
                   _out((T, H * V_HEAD), BF16)],
        scratch=[pltpu.VMEM((T, HEAD_PAD), F32)],
        args=[_hbm(v) for v in (qh, kh, vh, cat, dcat, lse)], sem=("arbitrary", "arbitrary"), est=est,
        ride=ride)
    return dq, dk, dv, gathered


def _mem_fn(q, k, v):
    hd = q.shape[1] // MEM_HEADS
    outs = []
    for h in range(MEM_HEADS):
        lo, hi = h * hd, (h + 1) * hd
        s = _bdot_nt(_cols(q, lo, hi), _cols(k, lo, hi)) * hd ** -0.5
        e = jnp.exp(s - lax.stop_gradient(jnp.max(s, axis=1, keepdims=True)))
        p = e / jnp.sum(e, axis=1, keepdims=True)
        outs.append(_bdot_nn(p, _cols(v, lo, hi)))
    return jnp.concatenate(outs, axis=1)


def _mem_fwd(q, k, v, *, name):
    T, D = q.shape
    tm = _tile(T, 256, 16)

    def fn(i, tv, pv):
        return (_mem_fn(tv[0], pv[0], pv[1]),), ()

    return _rowwise(fn, [q], [k, v], [(D, BF16)], tm=tm, name=name)[0]


def _mem_bwd(q, k, v, do, *, name):
    T, D = q.shape
    tm = _tile(T, 256, 16)

    def fn(i, tv, pv):
        _, vjp = jax.vjp(_mem_fn, tv[0], pv[0], pv[1])
        dq, dk, dv = vjp(tv[1].astype(F32))
        return (dq,), (dk, dv)

    return _rowwise(fn, [q, do], [k, v], [(D, BF16)], [(k.shape, F32), (v.shape, F32)], tm=tm, name=name)


def _adamw(w, g, m, v, *, name):
    shape = w.shape
    if w.ndim != 3:
        lead3 = (1, math.prod(shape[:-1]), shape[-1])
        return [o.reshape(shape) for o in _adamw(*[a.reshape(lead3) for a in (w, g, m, v)], name=name)]
    Lw, R, C = shape
    tr = _tile(R, 512, 8)
    b1c = 1.0 - ADAM_B1 ** ADAM_STEP
    b2c = 1.0 - ADAM_B2 ** ADAM_STEP

    def body(w_ref, g_ref, m_ref, v_ref, d_ref, mo_ref, vo_ref):
        gg = g_ref[...]
        mn = ADAM_B1 * m_ref[...] + (1.0 - ADAM_B1) * gg
        vn = ADAM_B2 * v_ref[...] + (1.0 - ADAM_B2) * (gg * gg)
        d_ref[...] = -ADAM_LR * ((mn / b1c) / (jnp.sqrt(vn / b2c) + ADAM_EPS) + ADAM_WD * w_ref[...])
        mo_ref[...] = mn
        vo_ref[...] = vn

    spec = pl.BlockSpec((None, tr, C), lambda l, i: (l, i, 0))
    return pl.pallas_call(
        body, name=name, grid=(Lw, R // tr),
        in_specs=[spec] * 4, out_specs=[spec] * 3,
        out_shape=[_out(shape, F32)] * 3,
        compiler_params=_params(("parallel", "parallel"), 7 * tr * C * 4),
    )(*[_hbm(a) for a in (w, g, m, v)])


def _pair_sum(core, gs, landed, offs, *, name):
    n = len(gs)
    _, R, C = landed.shape
    rows = [g.shape[0] // N_DEV for g in gs]

    def body(core_ref, *refs):
        g_refs, l_ref, o_ref = refs[:n], refs[n], refs[n + 1]
        for g_ref, off, r in zip(g_refs, offs, rows):
            o_ref[off:off + r, :] = (g_ref[...].astype(F32) + l_ref[off:off + r, :].astype(F32)).astype(o_ref.dtype)

    slab = pl.BlockSpec((None, R, C), lambda p, core_ref: (p, 0, 0))
    own = [pl.BlockSpec((r, C), lambda p, core_ref: (2 * p + core_ref[0], 0)) for r in rows]
    return pl.pallas_call(
        body, name=name,
        grid_spec=pltpu.PrefetchScalarGridSpec(
            num_scalar_prefetch=1, grid=(4,), in_specs=own + [slab], out_specs=slab),
        out_shape=_out(landed.shape, landed.dtype),
        input_output_aliases={n + 1: 0},
        compiler_params=_params(("arbitrary",), 3 * R * C * 2 + R * C * 8),
    )(core, *[_hbm(g) for g in gs], _hbm(landed))


def _quad_sum(chip, part, gathered, used, *, name):
    C = part.shape[2]
    R = used
    tr = _tile(R, 256, 16)

    def body(chip_ref, own_ref, a_ref, b_ref, c_ref, o_ref):
        o_ref[...] = ((own_ref[...].astype(F32) + a_ref[...].astype(F32)) + b_ref[...].astype(F32)) \
            + c_ref[...].astype(F32)

    def other(k):
        return pl.BlockSpec((None, tr, C), lambda i, chip_ref: (chip_ref[0] ^ k, i, 0))

    return pl.pallas_call(
        body, name=name,
        grid_spec=pltpu.PrefetchScalarGridSpec(
            num_scalar_prefetch=1, grid=(R // tr,),
            in_specs=[pl.BlockSpec((None, tr, C), lambda i, chip_ref: (chip_ref[0], i, 0)),
                      other(1), other(2), other(3)],
            out_specs=pl.BlockSpec((tr, C), lambda i, chip_ref: (i, 0))),
        out_shape=_out((R, C), F32),
        compiler_params=_params(("arbitrary",), 8 * tr * C * 4),
    )(chip, _hbm(part), _hbm(gathered), _hbm(gathered), _hbm(gathered))


def _place():
    x, y, c = lax.axis_index("x"), lax.axis_index("y"), lax.axis_index("c")
    return x, y, c


ANY = pl.BlockSpec(memory_space=pl.ANY)


class _Gather:
    def __init__(self, shards):
        self.shards = list(shards)
        self.n = len(self.shards)
        self.out_shape = [_out((s.shape[0], N_DEV * s.shape[1], s.shape[2]), s.dtype)
                          for s in self.shards]
        self.scratch = [pltpu.SemaphoreType.DMA((7 * self.n,)), pltpu.SemaphoreType.DMA((7 * self.n,)),
                        pltpu.SemaphoreType.DMA((self.n,))]
        self.operands = [_hbm(s) for s in self.shards]

    def _bind(self, refs):
        n = self.n
        ins, outs = refs[:n], refs[n:2 * n]
        send_sems, recv_sems, local_sems = refs[2 * n:]
        x, y, c = _place()
        me, sib = (x, y, c), (x, y, 1 - c)
        chips = [(1 - x, y), (x, 1 - y), (1 - x, 1 - y)]

        def rows(w, p):
            r = self.shards[w].shape[1]
            idx = 4 * p[0] + 2 * p[1] + p[2]
            return outs[w].at[:, pl.ds(pl.multiple_of(idx * r, 8), r), :]

        def copy(w, k, block, to, src=None):
            return pltpu.make_async_remote_copy(
                src_ref=rows(w, block) if src is None else src, dst_ref=rows(w, block),
                send_sem=send_sems.at[w * 7 + k], recv_sem=recv_sems.at[w * 7 + k],
                device_id=to, device_id_type=MESH)

        def mine():
            return [pltpu.make_async_copy(ins[w], rows(w, me), local_sems.at[w]) for w in range(n)]

        def first():
            out = []
            for w in range(n):
                out.append(copy(w, 0, me, sib, src=ins[w]))
                out += [copy(w, 1 + j, me, (*chip, c), src=ins[w]) for j, chip in enumerate(chips)]
            return out

        def passed():
            return [copy(w, 4 + j, (*chip, c), sib) for j, chip in enumerate(chips) for w in range(n)]

        def landed():
            return [copy(w, 1 + j, (*chip, c), me) for j, chip in enumerate(chips) for w in range(n)]

        def last():
            out = []
            for w in range(n):
                out.append(copy(w, 0, sib, me))
                out += [copy(w, 4 + j, (*chip, 1 - c), me) for j, chip in enumerate(chips)]
            return out

        return mine, first, landed, passed, last

    def start(self, refs):
        mine, first, _, _, _ = self._bind(refs)
        for cp in mine() + first():
            cp.start()

    def forward(self, refs):
        _, _, landed, passed, _ = self._bind(refs)
        for arrived, fwd in zip(landed(), passed()):
            arrived.wait_recv()
            fwd.start()

    def finish(self, refs):
        mine, first, _, passed, last = self._bind(refs)
        for cp in last():
            cp.wait_recv()
        for cp in first() + passed():
            cp.wait_send()
        for cp in mine():
            cp.wait()


class _ChipExchange:
    def __init__(self, parts, used):
        self.ncl = len(parts)
        self.used = list(used)
        self.out_shape = [_out(p.shape, p.dtype) for p in parts]
        self.scratch = [pltpu.SemaphoreType.DMA((3 * self.ncl,)), pltpu.SemaphoreType.DMA((3 * self.ncl,))]
        self.operands = [_hbm(p) for p in parts]
        self.n = self.ncl

    def _bind(self, refs):
        ncl = self.ncl
        ins, outs = refs[:ncl], refs[ncl:2 * ncl]
        send_sems, recv_sems = refs[2 * ncl:]
        x, y, c = _place()
        chips = [(1 - x, y), (x, 1 - y), (1 - x, 1 - y)]
        here = 2 * x + y

        def copies(outgoing):
            out = []
            for k in range(ncl):
                rows = pl.ds(0, self.used[k])
                for j, (cx, cy) in enumerate(chips):
                    there = 2 * cx + cy
                    src, dst = (there, here) if outgoing else (here, there)
                    out.append(pltpu.make_async_remote_copy(
                        src_ref=ins[k].at[src, rows, :], dst_ref=outs[k].at[dst, rows, :],
                        send_sem=send_sems.at[3 * k + j], recv_sem=recv_sems.at[3 * k + j],
                        device_id=(cx, cy, c), device_id_type=MESH))
            return out

        return copies

    def start(self, refs):
        for cp in self._bind(refs)(True):
            cp.start()

    def forward(self, refs):
        pass

    def finish(self, refs):
        copies = self._bind(refs)
        for cp in copies(False):
            cp.wait_recv()
        for cp in copies(True):
            cp.wait_send()


class _Both:
    def __init__(self, members):
        self.members = list(members)
        self.n = sum(m.n for m in self.members)
        self.out_shape = [s for m in self.members for s in m.out_shape]
        self.scratch = [s for m in self.members for s in m.scratch]
        self.operands = [o for m in self.members for o in m.operands]

    def split(self, arrays):
        out, a = [], 0
        for m in self.members:
            out.append(list(arrays[a:a + m.n]))
            a += m.n
        return out

    def _refs(self, refs):
        ins, outs = self.split(refs[:self.n]), self.split(refs[self.n:2 * self.n])
        scr, b = [], 2 * self.n
        for m in self.members:
            scr.append(list(refs[b:b + len(m.scratch)]))
            b += len(m.scratch)
        return [(*i, *o, *s) for i, o, s in zip(ins, outs, scr)]

    def start(self, refs):
        for m, r in zip(self.members, self._refs(refs)):
            m.start(r)

    def forward(self, refs):
        for m, r in zip(self.members, self._refs(refs)):
            m.forward(r)

    def finish(self, refs):
        for m, r in zip(self.members, self._refs(refs)):
            m.finish(r)


def _exchange_alone(ex, *, name):
    def body(*refs):
        ex.start(refs)
        ex.forward(refs)
        ex.finish(refs)

    return pl.pallas_call(
        body, name=name, in_specs=[ANY] * ex.n, out_specs=[ANY] * ex.n,
        out_shape=ex.out_shape, scratch_shapes=ex.scratch,
    )(*ex.operands)


def _host_call(body, *, name, grid, in_specs, out_specs, out_shape, args, sem, est, ride=None, scratch=()):
    scratch = list(scratch)
    if ride is None:
        outs = pl.pallas_call(body, name=name, grid=grid, in_specs=in_specs, out_specs=out_specs,
                              out_shape=out_shape, scratch_shapes=scratch,
                              compiler_params=_params(sem, est))(*args)
        return list(outs), []
    n_in, n_out, n, n_scr = len(in_specs), len(out_specs), ride.n, len(scratch)

    def full(*refs):
        ins, rin = refs[:n_in], refs[n_in:n_in + n]
        outs, rout = refs[n_in + n:n_in + n + n_out], refs[n_in + n + n_out:n_in + 2 * n + n_out]
        own = refs[n_in + 2 * n + n_out:n_in + 2 * n + n_out + n_scr]
        rrefs = (*rin, *rout, *refs[n_in + 2 * n + n_out + n_scr:])
        step, total = _ride(ride, rrefs, grid)
        body(*ins, *outs, *own)
        _ride_end(ride, rrefs, step, total)

    outs = pl.pallas_call(
        full, name=name, grid=grid,
        in_specs=list(in_specs) + [ANY] * n, out_specs=list(out_specs) + [ANY] * n,
        out_shape=list(out_shape) + ride.out_shape, scratch_shapes=scratch + ride.scratch,
        compiler_params=_params(("arbitrary",) * len(grid), est),
    )(*args, *ride.operands)
    return list(outs[:n_out]), list(outs[n_out:])


def _ride(ex, refs, grid):
    total = math.prod(grid)
    step = pl.program_id(0)
    for axis in range(1, len(grid)):
        step = step * grid[axis] + pl.program_id(axis)
    pl.when(step == 0)(lambda: ex.start(refs))
    return step, total


def _ride_end(ex, refs, step, total):
    pl.when(step == (3 * total) // 4)(lambda: ex.forward(refs))
    pl.when(step == total - 1)(lambda: ex.finish(refs))


def _class_layout(grads, classes):
    used = [0] * len(set(classes))
    offs = []
    for g, cl in zip(grads, classes):
        offs.append(used[cl])
        used[cl] += g.shape[0] // N_DEV
    return offs, used


def _rs_to_sibling(grads, classes, *, name):
    n = len(grads)
    offs, used = _class_layout(grads, classes)
    heights = used
    ncl = len(heights)
    cols = [next(g.shape[1] for g, cl in zip(grads, classes) if cl == k) for k in range(ncl)]

    def body(*refs):
        gs, land = refs[:n], refs[n:n + ncl]
        send_sems, recv_sems = refs[n + ncl:]
        x, y, c = _place()
        sib = (x, y, 1 - c)
        for p in range(4):
            for w in range(n):
                r = grads[w].shape[0] // N_DEV
                cl = classes[w]
                there = gs[w].at[pl.ds(pl.multiple_of((2 * p + 1 - c) * r, 8), r), :]
                pltpu.make_async_remote_copy(
                    src_ref=there, dst_ref=land[cl].at[p, pl.ds(offs[w], r), :],
                    send_sem=send_sems.at[cl * 4 + p], recv_sem=recv_sems.at[cl * 4 + p],
                    device_id=sib, device_id_type=MESH).start()
        for cl in range(ncl):
            for p in range(4):
                rows_used = land[cl].at[p, pl.ds(0, used[cl]), :]
                slab = pltpu.make_async_remote_copy(
                    src_ref=rows_used, dst_ref=rows_used,
                    send_sem=send_sems.at[cl * 4 + p], recv_sem=recv_sems.at[cl * 4 + p],
                    device_id=sib, device_id_type=MESH)
                slab.wait_send()
                slab.wait_recv()

    return pl.pallas_call(
        body, name=name,
        in_specs=[ANY] * n, out_specs=[ANY] * ncl,
        out_shape=[_out((4, heights[k], cols[k]), BF16) for k in range(ncl)],
        scratch_shapes=[pltpu.SemaphoreType.DMA((4 * ncl,))] * 2,
    )(*[_hbm(g) for g in grads])


def _all_reduce_small(v, *, name):
    R = v.shape[0]

    def body(v_ref, o_ref, buf, send_sems, recv_sems):
        x, y, c = _place()
        me = 4 * x + 2 * y + c
        buf[me] = v_ref[...]
        copies = []
        for k in range(1, N_DEV):
            fx, fy, fc = (k >> 2) & 1, (k >> 1) & 1, k & 1
            to = (x ^ fx, y ^ fy, c ^ fc)
            cp = pltpu.make_async_remote_copy(
                src_ref=v_ref, dst_ref=buf.at[me],
                send_sem=send_sems.at[k - 1], recv_sem=recv_sems.at[k - 1],
                device_id=to, device_id_type=MESH)
            cp.start()
            copies.append(cp)
        for k in range(1, N_DEV):
            fx, fy, fc = (k >> 2) & 1, (k >> 1) & 1, k & 1
            frm = 4 * (x ^ fx) + 2 * (y ^ fy) + (c ^ fc)
            pltpu.make_async_remote_copy(
                src_ref=v_ref, dst_ref=buf.at[frm],
                send_sem=send_sems.at[k - 1], recv_sem=recv_sems.at[k - 1],
                device_id=(x ^ fx, y ^ fy, c ^ fc), device_id_type=MESH).wait_recv()
        for cp in copies:
            cp.wait_send()
        acc = buf[0]
        for d in range(1, N_DEV):
            acc = acc + buf[d]
        o_ref[...] = acc

    vm = pl.BlockSpec(memory_space=pltpu.VMEM)
    return pl.pallas_call(
        body, name=name, in_specs=[vm], out_specs=vm,
        out_shape=jax.ShapeDtypeStruct((R, LANE), F32),
        scratch_shapes=[pltpu.VMEM((N_DEV, R, LANE), F32),
                        pltpu.SemaphoreType.DMA((N_DEV - 1,)), pltpu.SemaphoreType.DMA((N_DEV - 1,))],
        compiler_params=pltpu.CompilerParams(vmem_limit_bytes=VMEM_FLOOR),
    )(v)


def _rope_tables(positions):
    half = QK_ROPE // 2
    inv_freq = ROPE_BASE ** (-jnp.arange(half, dtype=F32) / half)
    ang = positions.astype(F32)[:, None] * inv_freq
    cos, sin = jnp.cos(ang), jnp.sin(ang)
    z = jnp.zeros_like(cos)
    z2 = jnp.zeros((positions.shape[0], LANE - QK_ROPE), F32)
    rc = jnp.concatenate([cos, cos, z2], axis=1)
    rs1 = jnp.concatenate([-sin, z, z2], axis=1)
    rs2 = jnp.concatenate([z, sin, z2], axis=1)
    return rc, rs1, rs2


def _block_diag(pool_w):
    G, pg, _ = pool_w.shape
    out = jnp.zeros((G * pg, G * pg), pool_w.dtype)
    for g in range(G):
        out = lax.dynamic_update_slice(out, pool_w[g], (g * pg, g * pg))
    return out


def kernel(x, mem, positions, ln_g, ln_b, ffn1_w13, ffn1_w2, w_in, pool_w, pool_scale, q_norm_g, w_uq, kv_norm_g, w_ukv, w_out, mem_wq, mem_wkv, mem_wo, ffn2_w13, ffn2_w2, loss_target, m_ln_g, m_ln_b, m_ffn1_w13, m_ffn1_w2, m_w_in, m_pool_w, m_pool_scale, m_q_norm_g, m_w_uq, m_kv_norm_g, m_w_ukv, m_w_out, m_mem_wq, m_mem_wkv, m_mem_wo, m_ffn2_w13, m_ffn2_w2, v_ln_g, v_ln_b, v_ffn1_w13, v_ffn1_w2, v_w_in, v_pool_w, v_pool_scale, v_q_norm_g, v_w_uq, v_kv_norm_g, v_w_ukv, v_w_out, v_mem_wq, v_mem_wkv, v_mem_wo, v_ffn2_w13, v_ffn2_w2):
    L = ln_g.shape[0]
    T, D = x.shape[1], x.shape[2]
    F = ffn1_w2.shape[1] * N_DEV
    PW = D // 4
    H = (D - PW) // V_HEAD
    DIN = w_in.shape[2]
    DINP = PW + Q_LORA + KV_LORA + LANE
    QW = QK_NOPE + QK_ROPE
    alpha = (2 * L) ** 0.25
    x2d = x.reshape(T, D)
    memb = mem.reshape(mem.shape[1], D).astype(BF16)
    target = loss_target.reshape(T, D)
    tabs = _rope_tables(positions.reshape(T))

    def shards_of(l):
        return dict(
            w13a=ffn1_w13[l].T[None].astype(BF16),
            w13b=ffn2_w13[l].T[None].astype(BF16),
            w2a=ffn1_w2[l][None].astype(BF16),
            w2b=ffn2_w2[l][None].astype(BF16),
            wsq=jnp.stack([w_out[l], mem_wq[l], mem_wo[l]]).astype(BF16),
            wkvT=mem_wkv[l].T[None].astype(BF16),
            winp=jnp.pad(w_in[l], ((0, 0), (0, DINP - DIN)))[None].astype(BF16),
            wuqT=w_uq[l].T[None].astype(BF16),
            wukvT=w_ukv[l].T[None].astype(BF16),
        )

    SMALL = ("winp", "wuqT", "wukvT")
    shards = [shards_of(l) for l in range(L)]
    W = [dict() for _ in range(L)]

    def rider(spec):
        return _Gather([shards[l][n] for l, n in spec]) if spec else None

    def arrived(spec, arrays):
        for (l, n), a in zip(spec, arrays):
            if n in ("w13a", "w13b"):
                a = _interleave(a, 1)
            elif n == "wuqT":
                a = jnp.pad(a.reshape(H, QW, Q_LORA), ((0, 0), (0, HEAD_PAD - QW), (0, 0)))
                a = a.reshape(1, H * HEAD_PAD, Q_LORA)
            elif n == "ln":
                a = jnp.moveaxis(a.reshape(N_DEV, 2, L, 4, D // N_DEV), 0, 3).reshape(2, L, 4, D)
                LN["g"], LN["b"] = a[0], a[1]
            W[l][n] = a

    LN = {}
    shards[0]["ln"] = jnp.concatenate([ln_g.reshape(1, 4 * L, -1), ln_b.reshape(1, 4 * L, -1)], axis=1)
    spec0 = [(0, "w13a")]
    arrived(spec0, _exchange_alone(rider(spec0), name="ag_first"))
    wbd = [_block_diag(pool_w[l]).astype(BF16) for l in range(L)]

    def ffn_fwd(l, which, xres, xb, k, spec):
        ab = "ab"[which]
        h13, a, rode = _ffn_up(xb, W[l]["w13" + ab], 0, name=f"l{l}_ffn{which}_up", ride=rider(spec))
        arrived(spec, rode)
        y, xo, xob = _mm_ln(a, W[l]["w2" + ab], 0, xres, LN["g"][l,k:k + 1], LN["b"][l,k:k + 1], alpha=alpha, s=0.5,
                            name=f"l{l}_ffn{which}_y_ln{k}")
        return dict(xres=xres, xb=xb, h13=h13, a=a, y=y), xo, xob

    saved = []
    xres, xb = x2d, x2d.astype(BF16)
    for l in range(L):
        sv = {}
        more = l + 1 < L
        Wl = W[l]
        spec = ([(0, "w2a"), (0, "ln"), *[(0, n) for n in SMALL], (0, "wkvT")] if l == 0
                else [(l, "wsq"), (l, "wkvT")])
        sv["ffn1"], x1, x1b = ffn_fwd(l, 0, xres, xb, 0, spec)
        hin = _mm(x1b, Wl["winp"], lead=0, name=f"l{l}_hin")
        pscale = pool_scale[l].reshape(1, PW)
        gq, gkv = q_norm_g[l].reshape(1, Q_LORA), kv_norm_g[l].reshape(1, KV_LORA)
        cqn, ckvn, kpe = _norms_fwd(hin, gq, gkv, pw=PW, name=f"l{l}_norms")
        qh, kh, vh = _heads_fwd(cqn, ckvn, kpe, tabs, Wl["wuqT"][0], Wl["wukvT"][0], H=H, name=f"l{l}_heads")
        spec = [(l, "w13b"), (l, "w2b")] + ([(0, "wsq")] if l == 0 else []) + ([(l + 1, "w13a")] if more else [])
        cat, lse, rode = _flash_fwd(qh, kh, vh, H=H, pw=PW, name=f"l{l}_flash", ride=rider(spec))
        arrived(spec, rode)
        cat = _pool_fwd(hin, wbd[l], pscale, cat, name=f"l{l}_pool")
        ymix, x2, x2b = _mm_ln(cat, Wl["wsq"], 0, x1, LN["g"][l,1:2], LN["b"][l,1:2], alpha=alpha, s=1.0,
                               name=f"l{l}_ymix_ln1")
        qm = _mm(x2b, Wl["wsq"], lead=1, out_dtype=BF16, name=f"l{l}_qm")
        kvm = _mm(memb, Wl["wkvT"], lead=0, tb=True, name=f"l{l}_kvm")
        km, vm = kvm[:, :D], kvm[:, D:]
        om = _mem_fwd(qm, km, vm, name=f"l{l}_memattn")
        ymem, x3, x3b = _mm_ln(om, Wl["wsq"], 2, x2, LN["g"][l,2:3], LN["b"][l,2:3], alpha=alpha, s=1.0,
                               name=f"l{l}_ymem_ln2")
        spec = [(l + 1, n) for n in ("w2a", *SMALL)] if more else []
        sv["ffn2"], x4, x4b = ffn_fwd(l, 1, x3, x3b, 3, spec)
        sv.update(x1=x1, x1b=x1b, hin=hin, pscale=pscale, gq=gq, gkv=gkv, cqn=cqn, ckvn=ckvn,
                  qh=qh, kh=kh, vh=vh, lse=lse, cat=cat, ymix=ymix, x2=x2, x2b=x2b, qm=qm, km=km, vm=vm,
                  om=om, ymem=ymem)
        saved.append(sv)
        xres, xb = x4, x4b


    gW = {}
    gS = {}

    def ln_of(l, k):
        sv = saved[l]
        x, y, s = {0: (sv["ffn1"]["xres"], sv["ffn1"]["y"], 0.5), 1: (sv["x1"], sv["ymix"], 1.0),
                   2: (sv["x2"], sv["ymem"], 1.0), 3: (sv["ffn2"]["xres"], sv["ffn2"]["y"], 0.5)}[k]
        return x, y, LN["g"][l, k:k + 1], s

    def dx_through_ln(a, b, lead, tb, add, into, name):
        x, y, g, s = ln_of(*into)
        dxres, dyb, dg, db = _mm_ln_bwd(a, b, lead, tb, add, x, y, g, alpha=alpha, s=s, name=name)
        gS[("ln_g", *into)], gS[("ln_b", *into)] = dg, db
        return dxres, dyb

    def ffn_bwd(l, which, sv, dxres, dyb, ride, into):
        tag = f"l{l}_ffn{which}"
        gW[("w2", which, l)] = _mm(sv["a"], dyb, ta=True, out_dtype=BF16, name=f"{tag}_dw2", tn=D)
        dh, rode = _ffn_down_bwd(dyb, W[l]["w2" + "ab"[which]], 0, sv["h13"], name=f"{tag}_dh", ride=ride)
        dw13 = _mm(dh, sv["xb"], ta=True, out_dtype=BF16, name=f"{tag}_dw13", tn=D)
        gW[("w13", which, l)] = _deinterleave(dw13, 0)
        w13 = W[l]["w13" + "ab"[which]]
        if into is not None:
            return dx_through_ln(dh, w13, 0, False, dxres, into, f"{tag}_dx"), rode
        last = rs_first_level(l, "c")
        dxn, got = _mm(dh, w13, lead=0, add=dxres, name=f"{tag}_dx", tn=D, ride=last["ex"])
        rs_last_level(last, got)
        return dxn, rode

    core = lax.axis_index("c").astype(jnp.int32).reshape(1)
    chip = (2 * lax.axis_index("x") + lax.axis_index("y")).astype(jnp.int32).reshape(1)
    gsh = {}

    def rs_first_level(l, group):
        keys, classes = {
            "a": ([("w13", 1, l), ("w2", 1, l), ("mem_wkv", l), ("mem_wq", l), ("mem_wo", l)], [0] * 5),
            "b": ([("w_out", l), ("w_in", l), ("w_uq", l), ("w_ukv", l)], [0, 1, 2, 3]),
            "c": ([("w13", 0, l), ("w2", 0, l)], [0, 0]),
        }[group]
        tag = f"l{l}{group}"
        garrs = []
        for key in keys:
            g = gW[key]
            if key[0] == "w_uq":
                g = g.reshape(H, HEAD_PAD, Q_LORA)[:, :QW, :].reshape(H * QW, Q_LORA)
            garrs.append(g)
        offs, used = _class_layout(garrs, classes)
        parts = list(_rs_to_sibling(garrs, classes, name=f"{tag}_rs_sibling"))
        for cl in range(len(parts)):
            mine = [w for w, c in enumerate(classes) if c == cl]
            parts[cl] = _pair_sum(core, [garrs[w] for w in mine], parts[cl], [offs[w] for w in mine],
                                  name=f"{tag}_rs_pair_sum{cl}")
        return dict(tag=tag, keys=keys, garrs=garrs, classes=classes, offs=offs, used=used, parts=parts,
                    ex=_ChipExchange(parts, used))

    def rs_last_level(st, gathered):
        sums = [_quad_sum(chip, p, a, u, name=f"{st['tag']}_rs_quad_sum{k}")
                for k, (p, a, u) in enumerate(zip(st["parts"], gathered, st["used"]))]
        for key, g, cl, off in zip(st["keys"], st["garrs"], st["classes"], st["offs"]):
            gsh[key] = sums[cl][off:off + g.shape[0] // N_DEV, :]

    top = (L - 1, 3)
    x_top, y_top, g_top, s_top = ln_of(*top)
    dxres, dyb, gS[("ln_g", *top)], gS[("ln_b", *top)], loss_blk = _loss_ln_bwd(
        x_top, y_top, g_top, xres, target, alpha=alpha, s=s_top, name="loss_ln_top_bwd")
    loss = lax.psum(loss_blk[0, 0], ("x", "y", "c"))
    above = None
    for l in reversed(range(L)):
        sv = saved[l]
        Wl = W[l]
        (dxres, dyb), _ = ffn_bwd(l, 1, sv["ffn2"], dxres, dyb, None, (l, 2))
        dom = _mm(dyb, Wl["wsq"], lead=2, tb=True, out_dtype=BF16, name=f"l{l}_dom")
        gW[("mem_wo", l)] = _mm(sv["om"], dyb, ta=True, out_dtype=BF16, name=f"l{l}_dwo", tn=D)
        dqm, dkm, dvm = _mem_bwd(sv["qm"], sv["km"], sv["vm"], dom, name=f"l{l}_memattn_bwd")
        dxres, dyb = dx_through_ln(dqm, Wl["wsq"], 1, True, dxres, (l, 1), f"l{l}_dx2")
        gW[("mem_wq", l)] = _mm(sv["x2b"], dqm, ta=True, out_dtype=BF16, name=f"l{l}_dwq", tn=D)
        dkvm = jnp.concatenate([dkm, dvm], axis=1).astype(BF16)
        gW[("mem_wkv", l)] = _mm(dkvm, memb, ta=True, out_dtype=BF16, name=f"l{l}_dwkv", tn=D)
        dcat = _mm(dyb, Wl["wsq"], lead=0, tb=True, name=f"l{l}_dcat", tn=D)
        gW[("w_out", l)] = _mm(sv["cat"], dyb, ta=True, out_dtype=BF16, name=f"l{l}_dwout", tn=D)
        riding = [rs_first_level(l, "a")] + ([above] if above else [])
        both = _Both([st["ex"] for st in riding])
        dqh, dkh, dvh, rode = _flash_bwd(sv["qh"], sv["kh"], sv["vh"], sv["cat"], dcat, sv["lse"], H=H, pw=PW,
                                         name=f"l{l}_flash_bwd", ride=both)
        for st, got in zip(riding, both.split(rode)):
            rs_last_level(st, got)
        dqraw, dkv, dkpe = _heads_bwd(dqh, dkh, dvh, tabs, H=H, name=f"l{l}_heads_bwd")
        dcq = _mm(dqraw, Wl["wuqT"], lead=0, name=f"l{l}_dcq")
        gW[("w_uq", l)] = _mm(dqraw, sv["cqn"], ta=True, out_dtype=BF16, name=f"l{l}_dwuq")
        dckv = _mm(dkv, Wl["wukvT"], lead=0, name=f"l{l}_dckv")
        gW[("w_ukv", l)] = _mm(dkv, sv["ckvn"], ta=True, out_dtype=BF16, name=f"l{l}_dwukv")
        du, dwbd, dps = _pool_bwd(sv["hin"], dcat, wbd[l], sv["pscale"], name=f"l{l}_pool_bwd")
        dhin, dgq, dgkv = _norms_bwd(sv["hin"], sv["gq"], sv["gkv"], dcq, dckv, dkpe, du, pw=PW,
                                     name=f"l{l}_norms_bwd")
        pg = PW // len(POOL_WINDOWS)
        gS[("pool_w", l)] = jnp.stack([dwbd[g * pg:(g + 1) * pg, g * pg:(g + 1) * pg]
                                       for g in range(len(POOL_WINDOWS))])
        gS[("pool_scale", l)], gS[("q_norm_g", l)], gS[("kv_norm_g", l)] = dps, dgq, dgkv
        dxres, dyb = dx_through_ln(dhin, Wl["winp"], 0, True, dxres, (l, 0), f"l{l}_dx1")
        gW[("w_in", l)] = _mm(sv["x1b"], dhin, ta=True, out_dtype=BF16, name=f"l{l}_dwin", tn=DINP)
        heads = rs_first_level(l, "b")
        below, rode = ffn_bwd(l, 0, sv["ffn1"], dxres, dyb, heads["ex"], (l - 1, 3) if l > 0 else None)
        rs_last_level(heads, rode)
        if l > 0:
            dxres, dyb = below
            above = rs_first_level(l, "c")
    grad_x = below.reshape(1, T, D)

    small_keys = []
    for l in range(L):
        small_keys += [("pool_w", l), ("pool_scale", l), ("q_norm_g", l), ("kv_norm_g", l)]
        small_keys += [("ln_g", l, k) for k in range(4)] + [("ln_b", l, k) for k in range(4)]
    flat = jnp.concatenate([gS[k].reshape(-1) for k in small_keys])
    n_small = flat.shape[0]
    rows = -(-n_small // (8 * LANE)) * 8
    flat = jnp.pad(flat, (0, rows * LANE - n_small)).reshape(rows, LANE)
    red = _all_reduce_small(flat, name="ar_small").reshape(-1)
    gsm, pos = {}, 0
    for k in small_keys:
        size = math.prod(gS[k].shape)
        gsm[k] = red[pos:pos + size].reshape(gS[k].shape)
        pos += size

    me = 4 * lax.axis_index("x") + 2 * lax.axis_index("y") + lax.axis_index("c")
    dsh = D // N_DEV
    stack = lambda f: jnp.stack([f(l) for l in range(L)])
    g_ln_g = stack(lambda l: jnp.concatenate([gsm[("ln_g", l, k)] for k in range(4)], axis=0))
    g_ln_b = stack(lambda l: jnp.concatenate([gsm[("ln_b", l, k)] for k in range(4)], axis=0))
    grads = {
        "ln_g": lax.dynamic_slice_in_dim(g_ln_g, me * dsh, dsh, axis=2),
        "ln_b": lax.dynamic_slice_in_dim(g_ln_b, me * dsh, dsh, axis=2),
        "ffn1_w13": stack(lambda l: gsh[("w13", 0, l)].T),
        "ffn1_w2": stack(lambda l: gsh[("w2", 0, l)]),
        "w_in": stack(lambda l: gsh[("w_in", l)][:, :DIN]),
        "pool_w": stack(lambda l: gsm[("pool_w", l)]),
        "pool_scale": stack(lambda l: gsm[("pool_scale", l)].reshape(PW)),
        "q_norm_g": stack(lambda l: gsm[("q_norm_g", l)].reshape(Q_LORA)),
        "w_uq": stack(lambda l: gsh[("w_uq", l)].T),
        "kv_norm_g": stack(lambda l: gsm[("kv_norm_g", l)].reshape(KV_LORA)),
        "w_ukv": stack(lambda l: gsh[("w_ukv", l)].T),
        "w_out": stack(lambda l: gsh[("w_out", l)]),
        "mem_wq": stack(lambda l: gsh[("mem_wq", l)]),
        "mem_wkv": stack(lambda l: gsh[("mem_wkv", l)].T),
        "mem_wo": stack(lambda l: gsh[("mem_wo", l)]),
        "ffn2_w13": stack(lambda l: gsh[("w13", 1, l)].T),
        "ffn2_w2": stack(lambda l: gsh[("w2", 1, l)]),
    }

    names = ["ln_g", "ln_b", "ffn1_w13", "ffn1_w2", "w_in", "pool_w", "pool_scale", "q_norm_g", "w_uq",
             "kv_norm_g", "w_ukv", "w_out", "mem_wq", "mem_wkv", "mem_wo", "ffn2_w13", "ffn2_w2"]
    weights = dict(ln_g=ln_g, ln_b=ln_b, ffn1_w13=ffn1_w13, ffn1_w2=ffn1_w2, w_in=w_in, pool_w=pool_w,
                   pool_scale=pool_scale, q_norm_g=q_norm_g, w_uq=w_uq, kv_norm_g=kv_norm_g, w_ukv=w_ukv,
                   w_out=w_out, mem_wq=mem_wq, mem_wkv=mem_wkv, mem_wo=mem_wo, ffn2_w13=ffn2_w13,
                   ffn2_w2=ffn2_w2)
    ms = dict(ln_g=m_ln_g, ln_b=m_ln_b, ffn1_w13=m_ffn1_w13, ffn1_w2=m_ffn1_w2, w_in=m_w_in, pool_w=m_pool_w,
              pool_scale=m_pool_scale, q_norm_g=m_q_norm_g, w_uq=m_w_uq, kv_norm_g=m_kv_norm_g,
              w_ukv=m_w_ukv, w_out=m_w_out, mem_wq=m_mem_wq, mem_wkv=m_mem_wkv, mem_wo=m_mem_wo,
              ffn2_w13=m_ffn2_w13, ffn2_w2=m_ffn2_w2)
    vs = dict(ln_g=v_ln_g, ln_b=v_ln_b, ffn1_w13=v_ffn1_w13, ffn1_w2=v_ffn1_w2, w_in=v_w_in, pool_w=v_pool_w,
              pool_scale=v_pool_scale, q_norm_g=v_q_norm_g, w_uq=v_w_uq, kv_norm_g=v_kv_norm_g,
              w_ukv=v_w_ukv, w_out=v_w_out, mem_wq=v_mem_wq, mem_wkv=v_mem_wkv, mem_wo=v_mem_wo,
              ffn2_w13=v_ffn2_w13, ffn2_w2=v_ffn2_w2)
    deltas, new_m, new_v = [], [], []
    for nme in names:
        d, mn, vn = _adamw(weights[nme], grads[nme], ms[nme], vs[nme], name=f"adamw_{nme}")
        deltas.append(d)
        new_m.append(mn)
        new_v.append(vn)
    return (loss, grad_x, *[grads[nme] for nme in names], *deltas, *new_m, *new_v)
```

```python
import functools
import math

import jax
import jax.numpy as jnp
from jax import lax
from jax.experimental import pallas as pl
from jax.experimental.pallas import tpu as pltpu

F32 = jnp.float32
BF16 = jnp.bfloat16
MESH = pl.DeviceIdType.MESH

CHUNK = 64
MEM_HEADS = 4
POOL_WINDOWS = (2, 4, 8, 16)
QK_NOPE = 128
QK_ROPE = 64
V_HEAD = 128
Q_LORA = 256
KV_LORA = 128
ROPE_BASE = 10000.0
LN_EPS = 1e-5
RMS_EPS = 1e-6
NEG_INF = -1e30
ADAM_LR = 0.001
ADAM_B1 = 0.9
ADAM_B2 = 0.999
ADAM_EPS = 1e-08
ADAM_WD = 0.01
ADAM_STEP = 10

N_DEV = 8
LANE = 128
HEAD_PAD = 2 * LANE
POOL_HALO = 16
VMEM_CAP = 56 * 1024 * 1024
VMEM_FLOOR = 32 * 1024 * 1024


def _tile(n, pref, mult):
    t = (min(pref, n) // mult) * mult
    while t >= mult:
        if n % t == 0:
            return t
        t -= mult
    return n


def _params(sem, est_bytes):
    limit = int(min(max(2 * est_bytes + (8 << 20), VMEM_FLOOR), VMEM_CAP))
    return pltpu.CompilerParams(dimension_semantics=sem, vmem_limit_bytes=limit)


def _nbytes(shape, dtype):
    return math.prod(shape) * jnp.dtype(dtype).itemsize


def _hbm(x):
    return pltpu.with_memory_space_constraint(x, pltpu.HBM)


def _out(shape, dtype):
    return pltpu.HBM(tuple(shape), dtype)


def _dg(a, b, ca, cb):
    return lax.dot_general(a.astype(BF16), b.astype(BF16), (((ca,), (cb,)), ((), ())),
                           preferred_element_type=F32)


@jax.custom_vjp
def _bdot_nn(a, b):
    return _dg(a, b, 1, 0)


def _bdot_nn_fwd(a, b):
    return _dg(a, b, 1, 0), (a, b)


def _bdot_nn_bwd(res, ct):
    a, b = res
    return _dg(ct, b, 1, 1).astype(a.dtype), _dg(a, ct, 0, 0).astype(b.dtype)


_bdot_nn.defvjp(_bdot_nn_fwd, _bdot_nn_bwd)


@jax.custom_vjp
def _bdot_nt(a, b):
    return _dg(a, b, 1, 1)


def _bdot_nt_fwd(a, b):
    return _dg(a, b, 1, 1), (a, b)


def _bdot_nt_bwd(res, ct):
    a, b = res
    return _dg(ct, b, 1, 0).astype(a.dtype), _dg(ct, a, 0, 0).astype(b.dtype)


_bdot_nt.defvjp(_bdot_nt_fwd, _bdot_nt_bwd)


@functools.partial(jax.custom_vjp, nondiff_argnums=(1,))
def _lane_roll(x, shift):
    return pltpu.roll(x, shift % x.shape[1], axis=1)


def _lane_roll_fwd(x, shift):
    return _lane_roll(x, shift), None


def _lane_roll_bwd(shift, _, ct):
    return (_lane_roll(ct, -shift),)


_lane_roll.defvjp(_lane_roll_fwd, _lane_roll_bwd)


@functools.partial(jax.custom_vjp, nondiff_argnums=(1, 2))
def _cols(x, lo, hi):
    return x[:, lo:hi]


def _cols_fwd(x, lo, hi):
    return x[:, lo:hi], x.shape[1]


def _cols_bwd(lo, hi, width, ct):
    parts = []
    if lo > 0:
        parts.append(jnp.zeros((ct.shape[0], lo), ct.dtype))
    parts.append(ct)
    if hi < width:
        parts.append(jnp.zeros((ct.shape[0], width - hi), ct.dtype))
    return (jnp.concatenate(parts, axis=1) if len(parts) > 1 else ct,)


_cols.defvjp(_cols_fwd, _cols_bwd)


MM_VMEM_BUDGET = 20 * 1024 * 1024


def _mm(a, b, *, name, ta=False, tb=False, out_dtype=F32, lead=None, add=None, add_scale=1.0,
        tm=1024, tn=1024, tk=2816, ride=None):
    if ta:
        K, M = a.shape
    else:
        M, K = a.shape
    bshape = b.shape[1:] if lead is not None else b.shape
    if tb:
        N, Kb = bshape
    else:
        Kb, N = bshape
    assert K == Kb, (name, a.shape, b.shape)

    def blocks(tm, tn, tk):
        tm = _tile(M, tm, LANE if ta else 16)
        tn = _tile(N, tn, LANE)
        tk = _tile(K, tk, LANE)
        nbytes = (tm * tk * a.dtype.itemsize + tk * tn * b.dtype.itemsize
                  + tm * tn * (jnp.dtype(out_dtype).itemsize + (4 if K // tk > 1 else 0)
                               + (add.dtype.itemsize if add is not None else 0)))
        return tm, tn, tk, nbytes

    tm, tn, tk, est = blocks(tm, tn, tk)
    for shrink in ("m", "k", "m", "k", "n"):
        if est <= MM_VMEM_BUDGET:
            break
        if shrink == "m":
            tm, tn, tk, est = blocks(max(tm // 2, LANE), tn, tk)
        elif shrink == "k":
            tm, tn, tk, est = blocks(tm, tn, max(tk // 2, LANE))
        else:
            tm, tn, tk, est = blocks(tm, max(tn // 2, LANE), tk)
    nk = K // tk
    ca = 0 if ta else 1
    cb = 1 if tb else 0

    def body(*refs):
        a_ref, b_ref = refs[0], refs[1]
        add_ref = refs[2] if add is not None else None
        o_ref = refs[3] if add is not None else refs[2]

        def finish(r):
            if add_ref is not None:
                r = r + add_scale * add_ref[...].astype(F32)
            o_ref[...] = r.astype(o_ref.dtype)

        if nk == 1:
            finish(_dg(a_ref[...], b_ref[...], ca, cb))
            return
        acc_ref = refs[-1]
        k = pl.program_id(2)

        @pl.when(k == 0)
        def _():
            acc_ref[...] = jnp.zeros_like(acc_ref)

        acc_ref[...] += _dg(a_ref[...], b_ref[...], ca, cb)

        @pl.when(k == nk - 1)
        def _():
            finish(acc_ref[...])

    a_blk = (tk, tm) if ta else (tm, tk)
    a_map = (lambda i, j, k: (k, i)) if ta else (lambda i, j, k: (i, k))
    b_blk = (tn, tk) if tb else (tk, tn)
    if lead is None:
        b_map = (lambda i, j, k: (j, k)) if tb else (lambda i, j, k: (k, j))
        b_spec = pl.BlockSpec(b_blk, b_map)
    else:
        b_map = (lambda i, j, k: (lead, j, k)) if tb else (lambda i, j, k: (lead, k, j))
        b_spec = pl.BlockSpec((None,) + b_blk, b_map)
    in_specs = [pl.BlockSpec(a_blk, a_map), b_spec]
    args = [a, b]
    if add is not None:
        in_specs.append(pl.BlockSpec((tm, tn), lambda i, j, k: (i, j)))
        args.append(add)
    (out,), rode = _host_call(
        body, name=name,
        grid=(M // tm, N // tn, nk),
        in_specs=in_specs,
        out_specs=[pl.BlockSpec((tm, tn), lambda i, j, k: (i, j))],
        out_shape=[_out((M, N), out_dtype)],
        scratch=[pltpu.VMEM((tm, tn), F32)] if nk > 1 else [],
        args=[_hbm(v) for v in args], sem=("parallel", "parallel", "arbitrary"), est=est + tm * tn * 4,
        ride=ride)
    return out if ride is None else (out, rode)


def _rowwise(fn, tiles, params, tile_outs, acc_outs=(), *, tm, name):
    tile_arrays, tile_specs = [], []
    for t in tiles:
        if isinstance(t, tuple):
            tile_arrays.append(t[0])
            tile_specs.append(t[1])
        else:
            tile_arrays.append(t)
            tile_specs.append(pl.BlockSpec((tm, t.shape[1]), lambda i: (i, 0)))
    T = tile_arrays[0].shape[0]
    nt, np_, nto, nao = len(tile_arrays), len(params), len(tile_outs), len(acc_outs)

    def body(*refs):
        i = pl.program_id(0)
        tvals = [r[...] for r in refs[:nt]]
        pvals = [r[...] for r in refs[nt:nt + np_]]
        to_refs = refs[nt + np_:nt + np_ + nto]
        ao_refs = refs[nt + np_ + nto:]
        touts, aouts = fn(i, tvals, pvals)
        for r, v in zip(to_refs, touts):
            r[...] = v.astype(r.dtype)
        if nao:
            @pl.when(i == 0)
            def _():
                for r in ao_refs:
                    r[...] = jnp.zeros_like(r)
            for r, v in zip(ao_refs, aouts):
                r[...] += v.astype(r.dtype)

    in_specs = tile_specs + [pl.BlockSpec(p.shape, lambda i: (0, 0)) for p in params]
    out_specs = [pl.BlockSpec((tm, c), lambda i: (i, 0)) for c, _ in tile_outs]
    out_specs += [pl.BlockSpec(s, lambda i: (0, 0)) for s, _ in acc_outs]
    out_shape = [_out((T, c), d) for c, d in tile_outs]
    out_shape += [_out(s, d) for s, d in acc_outs]
    width = sum(s.block_shape[-1] for s in tile_specs) + sum(c for c, _ in tile_outs)
    est = 6 * tm * width * 4 + sum(_nbytes(p.shape, F32) for p in params) * 4
    return pl.pallas_call(
        body, name=name, grid=(T // tm,),
        in_specs=in_specs, out_specs=out_specs, out_shape=out_shape,
        compiler_params=_params(("arbitrary",) if nao else ("parallel",), est),
    )(*[_hbm(v) for v in tile_arrays], *[_hbm(p) for p in params])


def _ln_fn(alpha, s, xres, y, g, b):
    z = alpha * xres.astype(F32) + s * y.astype(F32)
    mu = jnp.mean(z, axis=-1, keepdims=True)
    zc = z - mu
    var = jnp.mean(zc * zc, axis=-1, keepdims=True)
    return zc * lax.rsqrt(var + LN_EPS) * g + b


def _mm_ln(a, b, lead, xres, g, bias, *, alpha, s, name):
    M, K = a.shape
    N = b.shape[2]
    tm = _tile(M, 256, 16)

    def body(a_ref, b_ref, x_ref, g_ref, bias_ref, y_ref, xo_ref, xb_ref):
        y = _dg(a_ref[...], b_ref[...], 1, 0)
        y_ref[...] = y.astype(y_ref.dtype)
        out = _ln_fn(alpha, s, x_ref[...], y, g_ref[...], bias_ref[...])
        xo_ref[...] = out
        xb_ref[...] = out.astype(BF16)

    row = pl.BlockSpec((tm, N), lambda i: (i, 0))
    vec = pl.BlockSpec((1, N), lambda i: (0, 0))
    est = tm * K * 2 + K * N * 2 + tm * N * (4 + 4 + 4 + 2 + 8)
    return pl.pallas_call(
        body, name=name, grid=(M // tm,),
        in_specs=[pl.BlockSpec((tm, K), lambda i: (i, 0)), pl.BlockSpec((None, K, N), lambda i: (lead, 0, 0)),
                  row, vec, vec],
        out_specs=[row, row, row],
        out_shape=[_out((M, N), BF16), _out((M, N), F32), _out((M, N), BF16)],
        compiler_params=_params(("parallel",), est),
    )(_hbm(a), _hbm(b), _hbm(xres), _hbm(g), _hbm(bias))


def _ln_bwd_math(alpha, s, x, y, g, d):
    z = alpha * x + s * y.astype(F32)
    zc = z - jnp.mean(z, axis=-1, keepdims=True)
    r = lax.rsqrt(jnp.mean(zc * zc, axis=-1, keepdims=True) + LN_EPS)
    xh = zc * r
    dxh = d * g
    dz = r * (dxh - jnp.mean(dxh, axis=-1, keepdims=True) - xh * jnp.mean(dxh * xh, axis=-1, keepdims=True))
    return alpha * dz, s * dz, jnp.sum(d * xh, axis=0, keepdims=True), jnp.sum(d, axis=0, keepdims=True)


def _mm_ln_bwd(a, b, lead, tb, add, xres, y, g, *, alpha, s, name):
    M, K = a.shape
    N = b.shape[1] if tb else b.shape[2]
    tm = _tile(M, 512, 16)
    tk = _tile(K, 2816, LANE)
    nk = K // tk
    cb = 1 if tb else 0

    def body(a_ref, b_ref, add_ref, x_ref, y_ref, g_ref, dx_ref, dy_ref, dg_ref, db_ref, *scratch):
        i, k = pl.program_id(0), pl.program_id(1)

        def finish(d):
            @pl.when(i == 0)
            def _():
                dg_ref[...] = jnp.zeros_like(dg_ref)
                db_ref[...] = jnp.zeros_like(db_ref)

            dx, dy, dg, db = _ln_bwd_math(alpha, s, x_ref[...], y_ref[...], g_ref[...], d + add_ref[...])
            dx_ref[...] = dx
            dy_ref[...] = dy.astype(dy_ref.dtype)
            dg_ref[...] += dg
            db_ref[...] += db

        if nk == 1:
            finish(_dg(a_ref[...], b_ref[...], 1, cb))
            return
        acc_ref = scratch[0]

        @pl.when(k == 0)
        def _():
            acc_ref[...] = jnp.zeros_like(acc_ref)

        acc_ref[...] += _dg(a_ref[...], b_ref[...], 1, cb)

        @pl.when(k == nk - 1)
        def _():
            finish(acc_ref[...])

    row = pl.BlockSpec((tm, N), lambda i, k: (i, 0))
    vec = pl.BlockSpec((1, N), lambda i, k: (0, 0))
    b_spec = (pl.BlockSpec((None, N, tk), lambda i, k: (lead, 0, k)) if tb
              else pl.BlockSpec((None, tk, N), lambda i, k: (lead, k, 0)))
    est = tm * tk * 2 + tk * N * 2 + tm * N * (4 + 4 + 2 + 4 + 2 + 4 + 12)
    return pl.pallas_call(
        body, name=name, grid=(M // tm, nk),
        in_specs=[pl.BlockSpec((tm, tk), lambda i, k: (i, k)), b_spec, row, row, row, vec],
        out_specs=[row, row, vec, vec],
        out_shape=[_out((M, N), F32), _out((M, N), BF16), _out((1, N), F32), _out((1, N), F32)],
        scratch_shapes=[pltpu.VMEM((tm, N), F32)] if nk > 1 else [],
        compiler_params=_params(("arbitrary", "arbitrary"), est),
    )(_hbm(a), _hbm(b), _hbm(add), _hbm(xres), _hbm(y), _hbm(g))


def _loss_ln_bwd(xres, y, g, out, target, *, alpha, s, name):
    T, D = xres.shape
    tm = _tile(T, 256, 16)

    def body(x_ref, y_ref, o_ref, t_ref, g_ref, dx_ref, dy_ref, dg_ref, db_ref, loss_ref):
        @pl.when(pl.program_id(0) == 0)
        def _():
            dg_ref[...] = jnp.zeros_like(dg_ref)
            db_ref[...] = jnp.zeros_like(db_ref)
            loss_ref[...] = jnp.zeros_like(loss_ref)

        err = o_ref[...] - t_ref[...]
        part = 0.5 * jnp.sum(jnp.sum(err * err, axis=1, keepdims=True) / D, axis=0, keepdims=True)
        loss_ref[...] += jnp.broadcast_to(part, loss_ref.shape)
        dx, dy, dg, db = _ln_bwd_math(alpha, s, x_ref[...], y_ref[...], g_ref[...], err / D)
        dx_ref[...] = dx
        dy_ref[...] = dy.astype(dy_ref.dtype)
        dg_ref[...] += dg
        db_ref[...] += db

    row = pl.BlockSpec((tm, D), lambda i: (i, 0))
    vec = pl.BlockSpec((1, D), lambda i: (0, 0))
    return pl.pallas_call(
        body, name=name, grid=(T // tm,),
        in_specs=[row, row, row, row, vec],
        out_specs=[row, row, vec, vec, pl.BlockSpec((8, LANE), lambda i: (0, 0))],
        out_shape=[_out((T, D), F32), _out((T, D), BF16), _out((1, D), F32), _out((1, D), F32),
                   _out((8, LANE), F32)],
        compiler_params=_params(("arbitrary",), 14 * tm * D * 4),
    )(_hbm(xres), _hbm(y), _hbm(out), _hbm(target), _hbm(g))


FFN_TILE = 256


def _interleave(w, axis):
    n = w.shape[axis] // (2 * FFN_TILE)
    shp = w.shape[:axis] + (2, n, FFN_TILE) + w.shape[axis + 1:]
    return jnp.swapaxes(w.reshape(shp), axis, axis + 1).reshape(w.shape)


def _deinterleave(w, axis):
    n = w.shape[axis] // (2 * FFN_TILE)
    shp = w.shape[:axis] + (n, 2, FFN_TILE) + w.shape[axis + 1:]
    return jnp.swapaxes(w.reshape(shp), axis, axis + 1).reshape(w.shape)


def _ffn_up(xb, w13t, lead, *, name, ride=None):
    T, D = xb.shape
    F = w13t.shape[1] // 2
    tc = FFN_TILE
    tm = _tile(T, 1024, 16)

    def body(x_ref, w_ref, h_ref, a_ref):
        h = _dg(x_ref[...], w_ref[...], 1, 1)
        g, u = h[:, :tc], h[:, tc:]
        h_ref[...] = h.astype(h_ref.dtype)
        a_ref[...] = (g * jax.nn.sigmoid(g) * u).astype(a_ref.dtype)

    est = (tm * D + 2 * tc * D + 3 * tm * tc) * 2 + 3 * tm * tc * 4
    (h13, a), gathered = _host_call(
        body, name=name, grid=(T // tm, F // tc),
        in_specs=[pl.BlockSpec((tm, D), lambda i, j: (i, 0)),
                  pl.BlockSpec((None, 2 * tc, D), lambda i, j: (lead, j, 0))],
        out_specs=[pl.BlockSpec((tm, 2 * tc), lambda i, j: (i, j)),
                   pl.BlockSpec((tm, tc), lambda i, j: (i, j))],
        out_shape=[_out((T, 2 * F), BF16), _out((T, F), BF16)],
        args=[_hbm(xb), _hbm(w13t)], sem=("parallel", "parallel"), est=est, ride=ride)
    return h13, a, gathered


def _ffn_down_bwd(dyb, w2, lead, h13, *, name, ride=None):
    T, D = dyb.shape
    F = w2.shape[1]
    tc = FFN_TILE
    tm = _tile(T, 1024, 16)

    def body(dy_ref, w_ref, h_ref, dh_ref):
        d = _dg(dy_ref[...], w_ref[...], 1, 1)
        h = h_ref[...].astype(F32)
        g, u = h[:, :tc], h[:, tc:]
        sig = jax.nn.sigmoid(g)
        gs = g * sig
        dh_ref[...] = jnp.concatenate([d * u * (sig + gs * (1.0 - sig)), d * gs], axis=1).astype(dh_ref.dtype)

    est = (tm * D + tc * D + 4 * tm * tc) * 2 + 6 * tm * tc * 4
    (dh,), rode = _host_call(
        body, name=name, grid=(T // tm, F // tc),
        in_specs=[pl.BlockSpec((tm, D), lambda i, j: (i, 0)),
                  pl.BlockSpec((None, tc, D), lambda i, j: (lead, j, 0)),
                  pl.BlockSpec((tm, 2 * tc), lambda i, j: (i, j))],
        out_specs=[pl.BlockSpec((tm, 2 * tc), lambda i, j: (i, j))],
        out_shape=[_out((T, 2 * F), BF16)],
        args=[_hbm(dyb), _hbm(w2), _hbm(h13)], sem=("parallel", "parallel"), est=est, ride=ride)
    return dh, rode


def _pool_select(parts, pw):
    pg = pw // len(POOL_WINDOWS)
    grp = lax.broadcasted_iota(jnp.int32, parts[0].shape, 1) // pg
    out = parts[3]
    for g in (2, 1, 0):
        out = jnp.where(grp == g, parts[g], out)
    return out


def _pool_count(t0, rows, pw):
    pg = pw // len(POOL_WINDOWS)
    grp = lax.broadcasted_iota(jnp.int32, (rows, pw), 1) // pg
    win = jnp.where(grp == 0, POOL_WINDOWS[0],
                    jnp.where(grp == 1, POOL_WINDOWS[1],
                              jnp.where(grp == 2, POOL_WINDOWS[2], POOL_WINDOWS[3])))
    t = t0 + lax.broadcasted_iota(jnp.int32, (rows, pw), 0)
    return jnp.minimum(t + 1, win).astype(F32)


def _window_sums(ext, up):
    n = ext.shape[0]
    sums, cur, k = [], ext, 1
    for _ in POOL_WINDOWS:
        cur = cur + pltpu.roll(cur, (n - k) if up else k, axis=0)
        sums.append(cur)
        k *= 2
    return sums


def _pool_delta(u, halo, t0):
    tm, pw = u.shape
    ext = jnp.concatenate([halo, u], axis=0)
    sums = [s[POOL_HALO:, :] for s in _window_sums(ext, up=False)]
    return _pool_select(sums, pw) / _pool_count(t0, tm, pw) - u


def _pool_fwd(hin, wbd, scale, cat, *, name):
    T = hin.shape[0]
    pw = wbd.shape[0]
    tm = _tile(T, 256, POOL_HALO)
    per = tm // POOL_HALO

    def body(u_ref, halo_ref, w_ref, s_ref, cat_ref, y_ref):
        i = pl.program_id(0)
        halo = jnp.where(i > 0, halo_ref[...], 0.0)
        d = _pool_delta(u_ref[...], halo, i * tm)
        y_ref[...] = (_dg(d, w_ref[...], 1, 0) * s_ref[...]).astype(y_ref.dtype)

    return pl.pallas_call(
        body, name=name, grid=(T // tm,),
        in_specs=[pl.BlockSpec((tm, pw), lambda i: (i, 0)),
                  pl.BlockSpec((POOL_HALO, pw), lambda i: (jnp.maximum(i * per - 1, 0), 0)),
                  pl.BlockSpec((pw, pw), lambda i: (0, 0)),
                  pl.BlockSpec((1, pw), lambda i: (0, 0)),
                  ANY],
        out_specs=pl.BlockSpec((tm, pw), lambda i: (i, 0)),
        out_shape=_out(cat.shape, cat.dtype),
        input_output_aliases={4: 0},
        compiler_params=_params(("parallel",), 16 * tm * pw * 4),
    )(_hbm(hin), _hbm(hin), _hbm(wbd), _hbm(scale), _hbm(cat))


def _pool_bwd(hin, dcat, wbd, scale, *, name):
    T = hin.shape[0]
    pw = wbd.shape[0]
    tm = _tile(T, 256, POOL_HALO)
    per = tm // POOL_HALO
    nt = T // tm

    def body(u_ref, halo_ref, dy_ref, dyn_ref, w_ref, s_ref, du_ref, dw_ref, ds_ref):
        i = pl.program_id(0)

        @pl.when(i == 0)
        def _():
            dw_ref[...] = jnp.zeros_like(dw_ref)
            ds_ref[...] = jnp.zeros_like(ds_ref)

        halo = jnp.where(i > 0, halo_ref[...], 0.0)
        d = _pool_delta(u_ref[...], halo, i * tm)
        w = w_ref[...]
        sc = s_ref[...]
        dy = dy_ref[...]
        dyn = jnp.where(i < nt - 1, dyn_ref[...], 0.0)
        ds_ref[...] += jnp.sum(dy * _dg(d, w, 1, 0), axis=0, keepdims=True)
        dys = dy * sc
        dw_ref[...] += _dg(d, dys, 0, 0)
        dys_ext = jnp.concatenate([dys, dyn * sc], axis=0)
        dd_ext = _dg(dys_ext, w, 1, 1)
        ddp = dd_ext / _pool_count(i * tm, tm + POOL_HALO, pw)
        sums = [s[:tm, :] for s in _window_sums(ddp, up=True)]
        du_ref[...] = _pool_select(sums, pw) - dd_ext[:tm, :]

    return pl.pallas_call(
        body, name=name, grid=(nt,),
        in_specs=[pl.BlockSpec((tm, pw), lambda i: (i, 0)),
                  pl.BlockSpec((POOL_HALO, pw), lambda i: (jnp.maximum(i * per - 1, 0), 0)),
                  pl.BlockSpec((tm, pw), lambda i: (i, 0)),
                  pl.BlockSpec((POOL_HALO, pw), lambda i: (jnp.minimum((i + 1) * per, nt * per - 1), 0)),
                  pl.BlockSpec((pw, pw), lambda i: (0, 0)),
                  pl.BlockSpec((1, pw), lambda i: (0, 0))],
        out_specs=[pl.BlockSpec((tm, pw), lambda i: (i, 0)),
                   pl.BlockSpec((pw, pw), lambda i: (0, 0)),
                   pl.BlockSpec((1, pw), lambda i: (0, 0))],
        out_shape=[_out((T, pw), F32),
                   _out((pw, pw), F32),
                   _out((1, pw), F32)],
        compiler_params=_params(("arbitrary",), 24 * tm * pw * 4),
    )(_hbm(hin), _hbm(hin), _hbm(dcat), _hbm(dcat), _hbm(wbd), _hbm(scale))


def _rms(x, g):
    return x * lax.rsqrt(jnp.mean(x * x, axis=-1, keepdims=True) + RMS_EPS) * g


def _norms_fn(pw, h, gq, gkv):
    o1 = pw + Q_LORA
    o2 = o1 + KV_LORA
    return (_rms(_cols(h, pw, o1), gq), _rms(_cols(h, o1, o2), gkv), _cols(h, o2, h.shape[1]))


def _norms_fwd(hin, gq, gkv, *, pw, name):
    tm = _tile(hin.shape[0], 256, 16)

    def fn(i, tv, pv):
        return _norms_fn(pw, tv[0], pv[0], pv[1]), ()

    return _rowwise(fn, [hin], [gq, gkv], [(Q_LORA, BF16), (KV_LORA, BF16), (LANE, F32)], tm=tm, name=name)


def _norms_bwd(hin, gq, gkv, dcq, dckv, dkpe, du, *, pw, name):
    tm = _tile(hin.shape[0], 256, 16)
    dinp = hin.shape[1]

    def fn(i, tv, pv):
        _, vjp = jax.vjp(functools.partial(_norms_fn, pw), tv[0], pv[0], pv[1])
        dh, dgq, dgkv = vjp((tv[1].astype(F32), tv[2].astype(F32), tv[3].astype(F32)))
        dh = jnp.concatenate([tv[4], dh[:, pw:]], axis=1)
        return (dh,), (dgq, dgkv)

    return _rowwise(fn, [hin, dcq, dckv, dkpe, du], [gq, gkv], [(dinp, BF16)],
                    [((1, Q_LORA), F32), ((1, KV_LORA), F32)], tm=tm, name=name)


def _heads_fn(H, qraw, kv, kpe, rc, rs1, rs2):
    half = QK_ROPE // 2
    scale = (QK_NOPE + QK_ROPE) ** -0.5

    def rope(blk):
        return blk * rc + _lane_roll(blk, -half) * rs1 + _lane_roll(blk, half) * rs2

    krot = rope(kpe)
    qs, ks, vs = [], [], []
    for h in range(H):
        lo = h * HEAD_PAD
        qs += [_cols(qraw, lo, lo + LANE) * scale, rope(_cols(qraw, lo + LANE, lo + HEAD_PAD)) * scale]
        ks += [_cols(kv, lo, lo + LANE), krot]
        vs += [_cols(kv, lo + LANE, lo + HEAD_PAD)]
    return jnp.concatenate(qs, axis=1), jnp.concatenate(ks, axis=1), jnp.concatenate(vs, axis=1)


def _heads_fwd(cqn, ckvn, kpe, tabs, wuq, wukv, *, H, name):
    tm = _tile(cqn.shape[0], 256, 16)

    def fn(i, tv, pv):
        qraw = _dg(tv[0], pv[0], 1, 1)
        kv = _dg(tv[1], pv[1], 1, 1)
        return _heads_fn(H, qraw, kv, *tv[2:]), ()

    return _rowwise(fn, [cqn, ckvn, kpe, *tabs], [wuq, wukv],
                    [(H * HEAD_PAD, BF16), (H * HEAD_PAD, BF16), (H * V_HEAD, BF16)], tm=tm, name=name)


def _heads_bwd(dq, dk, dv, tabs, *, H, name):
    tm = _tile(dq.shape[0], 256, 16)

    def fn(i, tv, pv):
        z = jnp.zeros((tm, H * HEAD_PAD), F32)
        zk = jnp.zeros((tm, LANE), F32)
        rc, rs1, rs2 = tv[3], tv[4], tv[5]
        _, vjp = jax.vjp(lambda a, b, c: _heads_fn(H, a, b, c, rc, rs1, rs2), z, z, zk)
        return vjp((tv[0].astype(F32), tv[1].astype(F32), tv[2].astype(F32))), ()

    return _rowwise(fn, [dq, dk, dv, *tabs], [],
                    [(H * HEAD_PAD, BF16), (H * HEAD_PAD, BF16), (LANE, F32)], tm=tm, name=name)


def _diag_mask(rows, cols, row0):
    r = (row0 + lax.broadcasted_iota(jnp.int32, (rows, cols), 0)) // CHUNK
    c = lax.broadcasted_iota(jnp.int32, (rows, cols), 1) // CHUNK
    return r >= c


def _flash_fwd(qh, kh, vh, *, H, pw, name, ride=None):
    T = qh.shape[0]
    t = _tile(T, 512, CHUNK)
    off = pw // V_HEAD


    def body(q_ref, k_ref, v_ref, o_ref, lse_ref):
        i = pl.program_id(1)
        q = q_ref[...]

        def blk(j, carry, masked):
            m, l, acc = carry
            rows = pl.ds(pl.multiple_of(j * t, t), t)
            s = _dg(q, k_ref[rows, :], 1, 1)
            if masked:
                s = jnp.where(_diag_mask(t, t, 0), s, NEG_INF)
            mn = jnp.maximum(m, jnp.max(s, axis=1, keepdims=True))
            p = jnp.exp(s - mn)
            corr = jnp.exp(m - mn)
            l = corr * l + jnp.sum(p, axis=1, keepdims=True)
            acc = corr * acc + _dg(p, v_ref[rows, :], 1, 0)
            return mn, l, acc

        init = (jnp.full((t, 1), NEG_INF, F32), jnp.zeros((t, 1), F32), jnp.zeros((t, V_HEAD), F32))
        carry = lax.fori_loop(0, i, lambda j, c: blk(j, c, False), init)
        m, l, acc = blk(i, carry, True)
        o_ref[...] = (acc / l).astype(o_ref.dtype)
        lse_ref[...] = jnp.broadcast_to(m + jnp.log(l), (t, V_HEAD))

    est = 2 * T * (HEAD_PAD + V_HEAD) * 2 + 8 * t * t * 4
    (o, lse), gathered = _host_call(
        body, name=name, grid=(H, T // t),
        in_specs=[pl.BlockSpec((t, HEAD_PAD), lambda h, i: (i, h)),
                  pl.BlockSpec((T, HEAD_PAD), lambda h, i: (0, h)),
                  pl.BlockSpec((T, V_HEAD), lambda h, i: (0, h))],
        out_specs=[pl.BlockSpec((t, V_HEAD), lambda h, i: (i, off + h)),
                   pl.BlockSpec((t, V_HEAD), lambda h, i: (i, h))],
        out_shape=[_out((T, pw + H * V_HEAD), BF16),
                   _out((T, H * V_HEAD), F32)],
        args=[_hbm(qh), _hbm(kh), _hbm(vh)], sem=("parallel", "parallel"), est=est, ride=ride)
    return o, lse, gathered


def _flash_bwd(qh, kh, vh, cat, dcat, lse, *, H, pw, name, ride=None):
    T = qh.shape[0]
    t = _tile(T, 512, CHUNK)
    nb = T // t
    off = pw // V_HEAD

    def body(q_ref, k_ref, v_ref, o_ref, do_ref, lse_ref, dq_out_ref, dk_ref, dv_ref, dq_ref):
        j = pl.program_id(1)

        @pl.when(j == 0)
        def _():
            dq_ref[...] = jnp.zeros_like(dq_ref)

        kj = k_ref[...]
        vj = v_ref[...]

        def blk(i, carry, masked):
            dk, dv = carry
            rows = pl.ds(pl.multiple_of(i * t, t), t)
            qi = q_ref[rows, :]
            doi = do_ref[rows, :]
            oi = o_ref[rows, :].astype(F32)
            lsei = lse_ref[rows, :][:, :1]
            s = _dg(qi, kj, 1, 1)
            if masked:
                s = jnp.where(_diag_mask(t, t, 0), s, NEG_INF)
            p = jnp.exp(s - lsei)
            dv = dv + _dg(p, doi, 0, 0)
            dp = _dg(doi, vj, 1, 1)
            di = jnp.sum(doi * oi, axis=1, keepdims=True)
            ds = p * (dp - di)
            dk = dk + _dg(ds, qi, 0, 0)
            dq_ref[rows, :] += _dg(ds, kj, 1, 0)
            return dk, dv

        carry = blk(j, (jnp.zeros((t, HEAD_PAD), F32), jnp.zeros((t, V_HEAD), F32)), True)
        dk, dv = lax.fori_loop(j + 1, nb, lambda i, c: blk(i, c, False), carry)
        dk_ref[...] = dk.astype(dk_ref.dtype)
        dv_ref[...] = dv.astype(dv_ref.dtype)

        @pl.when(j == nb - 1)
        def _():
            dq_out_ref[...] = dq_ref[...].astype(dq_out_ref.dtype)

    est = T * (HEAD_PAD * 2 + V_HEAD * 2 + V_HEAD * 4 + V_HEAD * 4 + HEAD_PAD * 4) + 10 * t * t * 4
    (dq, dk, dv), gathered = _host_call(
        body, name=name, grid=(H, nb),
        in_specs=[pl.BlockSpec((T, HEAD_PAD), lambda h, j: (0, h)),
                  pl.BlockSpec((t, HEAD_PAD), lambda h, j: (j, h)),
                  pl.BlockSpec((t, V_HEAD), lambda h, j: (j, h)),
                  pl.BlockSpec((T, V_HEAD), lambda h, j: (0, off + h)),
                  pl.BlockSpec((T, V_HEAD), lambda h, j: (0, off + h)),
                  pl.BlockSpec((T, V_HEAD), lambda h, j: (0, h))],
        out_specs=[pl.BlockSpec((T, HEAD_PAD), lambda h, j: (0, h)),
                   pl.BlockSpec((t, HEAD_PAD), lambda h, j: (j, h)),
                   pl.BlockSpec((t, V_HEAD), lambda h, j: (j, h))],
        out_shape=[_out((T, H * HEAD_PAD), BF16),
                   _out((T, H * HEAD_PAD), BF16),
                   _out((T, H * V_HEAD), BF16)],
        scratch=[pltpu.VMEM((T, HEAD_PAD), F32)],
        args=[_hbm(v) for v in (qh, kh, vh, cat, dcat, lse)], sem=("arbitrary", "arbitrary"), est=est,
        ride=ride)
    return dq, dk, dv, gathered


def _mem_fn(q, k, v):
    hd = q.shape[1] // MEM_HEADS
    outs = []
    for h in range(MEM_HEADS):
        lo, hi = h * hd, (h + 1) * hd
        s = _bdot_nt(_cols(q, lo, hi), _cols(k, lo, hi)) * hd ** -0.5
        e = jnp.exp(s - lax.stop_gradient(jnp.max(s, axis=1, keepdims=True)))
        p = e / jnp.sum(e, axis=1, keepdims=True)
        outs.append(_bdot_nn(p, _cols(v, lo, hi)))
    return jnp.concatenate(outs, axis=1)


def _mem_fwd(q, k, v, *, name):
    T, D = q.shape
    tm = _tile(T, 256, 16)

    def fn(i, tv, pv):
        return (_mem_fn(tv[0], pv[0], pv[1]),), ()

    return _rowwise(fn, [q], [k, v], [(D, BF16)], tm=tm, name=name)[0]


def _mem_bwd(q, k, v, do, *, name):
    T, D = q.shape
    tm = _tile(T, 256, 16)

    def fn(i, tv, pv):
        _, vjp = jax.vjp(_mem_fn, tv[0], pv[0], pv[1])
        dq, dk, dv = vjp(tv[1].astype(F32))
        return (dq,), (dk, dv)

    return _rowwise(fn, [q, do], [k, v], [(D, BF16)], [(k.shape, F32), (v.shape, F32)], tm=tm, name=name)


def _adamw(w, g, m, v, *, name):
    shape = w.shape
    if w.ndim != 3:
        lead3 = (1, math.prod(shape[:-1]), shape[-1])
        return [o.reshape(shape) for o in _adamw(*[a.reshape(lead3) for a in (w, g, m, v)], name=name)]
    Lw, R, C = shape
    tr = _tile(R, 512, 8)
    b1c = 1.0 - ADAM_B1 ** ADAM_STEP
    b2c = 1.0 - ADAM_B2 ** ADAM_STEP

    def body(w_ref, g_ref, m_ref, v_ref, d_ref, mo_ref, vo_ref):
        gg = g_ref[...]
        mn = ADAM_B1 * m_ref[...] + (1.0 - ADAM_B1) * gg
        vn = ADAM_B2 * v_ref[...] + (1.0 - ADAM_B2) * (gg * gg)
        d_ref[...] = -ADAM_LR * ((mn / b1c) / (jnp.sqrt(vn / b2c) + ADAM_EPS) + ADAM_WD * w_ref[...])
        mo_ref[...] = mn
        vo_ref[...] = vn

    spec = pl.BlockSpec((None, tr, C), lambda l, i: (l, i, 0))
    return pl.pallas_call(
        body, name=name, grid=(Lw, R // tr),
        in_specs=[spec] * 4, out_specs=[spec] * 3,
        out_shape=[_out(shape, F32)] * 3,
        compiler_params=_params(("parallel", "parallel"), 7 * tr * C * 4),
    )(*[_hbm(a) for a in (w, g, m, v)])


def _pair_sum(core, gs, landed, offs, *, name):
    n = len(gs)
    _, R, C = landed.shape
    rows = [g.shape[0] // N_DEV for g in gs]

    def body(core_ref, *refs):
        g_refs, l_ref, o_ref = refs[:n], refs[n], refs[n + 1]
        for g_ref, off, r in zip(g_refs, offs, rows):
            o_ref[off:off + r, :] = (g_ref[...].astype(F32) + l_ref[off:off + r, :].astype(F32)).astype(o_ref.dtype)

    slab = pl.BlockSpec((None, R, C), lambda p, core_ref: (p, 0, 0))
    own = [pl.BlockSpec((r, C), lambda p, core_ref: (2 * p + core_ref[0], 0)) for r in rows]
    return pl.pallas_call(
        body, name=name,
        grid_spec=pltpu.PrefetchScalarGridSpec(
            num_scalar_prefetch=1, grid=(4,), in_specs=own + [slab], out_specs=slab),
        out_shape=_out(landed.shape, landed.dtype),
        input_output_aliases={n + 1: 0},
        compiler_params=_params(("arbitrary",), 3 * R * C * 2 + R * C * 8),
    )(core, *[_hbm(g) for g in gs], _hbm(landed))


def _quad_sum(chip, part, gathered, used, *, name):
    C = part.shape[2]
    R = used
    tr = _tile(R, 256, 16)

    def body(chip_ref, own_ref, a_ref, b_ref, c_ref, o_ref):
        o_ref[...] = ((own_ref[...].astype(F32) + a_ref[...].astype(F32)) + b_ref[...].astype(F32)) \
            + c_ref[...].astype(F32)

    def other(k):
        return pl.BlockSpec((None, tr, C), lambda i, chip_ref: (chip_ref[0] ^ k, i, 0))

    return pl.pallas_call(
        body, name=name,
        grid_spec=pltpu.PrefetchScalarGridSpec(
            num_scalar_prefetch=1, grid=(R // tr,),
            in_specs=[pl.BlockSpec((None, tr, C), lambda i, chip_ref: (chip_ref[0], i, 0)),
                      other(1), other(2), other(3)],
            out_specs=pl.BlockSpec((tr, C), lambda i, chip_ref: (i, 0))),
        out_shape=_out((R, C), F32),
        compiler_params=_params(("arbitrary",), 8 * tr * C * 4),
    )(chip, _hbm(part), _hbm(gathered), _hbm(gathered), _hbm(gathered))


def _place():
    x, y, c = lax.axis_index("x"), lax.axis_index("y"), lax.axis_index("c")
    return x, y, c


ANY = pl.BlockSpec(memory_space=pl.ANY)


class _Gather:
    def __init__(self, shards):
        self.shards = list(shards)
        self.n = len(self.shards)
        self.out_shape = [_out((s.shape[0], N_DEV * s.shape[1], s.shape[2]), s.dtype)
                          for s in self.shards]
        self.scratch = [pltpu.SemaphoreType.DMA((7 * self.n,)), pltpu.SemaphoreType.DMA((7 * self.n,)),
                        pltpu.SemaphoreType.DMA((self.n,))]
        self.operands = [_hbm(s) for s in self.shards]

    def _bind(self, refs):
        n = self.n
        ins, outs = refs[:n], refs[n:2 * n]
        send_sems, recv_sems, local_sems = refs[2 * n:]
        x, y, c = _place()
        me, sib = (x, y, c), (x, y, 1 - c)
        chips = [(1 - x, y), (x, 1 - y), (1 - x, 1 - y)]

        def rows(w, p):
            r = self.shards[w].shape[1]
            idx = 4 * p[0] + 2 * p[1] + p[2]
            return outs[w].at[:, pl.ds(pl.multiple_of(idx * r, 8), r), :]

        def copy(w, k, block, to, src=None):
            return pltpu.make_async_remote_copy(
                src_ref=rows(w, block) if src is None else src, dst_ref=rows(w, block),
                send_sem=send_sems.at[w * 7 + k], recv_sem=recv_sems.at[w * 7 + k],
                device_id=to, device_id_type=MESH)

        def mine():
            return [pltpu.make_async_copy(ins[w], rows(w, me), local_sems.at[w]) for w in range(n)]

        def first():
            out = []
            for w in range(n):
                out.append(copy(w, 0, me, sib, src=ins[w]))
                out += [copy(w, 1 + j, me, (*chip, c), src=ins[w]) for j, chip in enumerate(chips)]
            return out

        def passed():
            return [copy(w, 4 + j, (*chip, c), sib) for j, chip in enumerate(chips) for w in range(n)]

        def landed():
            return [copy(w, 1 + j, (*chip, c), me) for j, chip in enumerate(chips) for w in range(n)]

        def last():
            out = []
            for w in range(n):
                out.append(copy(w, 0, sib, me))
                out += [copy(w, 4 + j, (*chip, 1 - c), me) for j, chip in enumerate(chips)]
            return out

        return mine, first, landed, passed, last

    def start(self, refs):
        mine, first, _, _, _ = self._bind(refs)
        for cp in mine() + first():
            cp.start()

    def forward(self, refs):
        _, _, landed, passed, _ = self._bind(refs)
        for arrived, fwd in zip(landed(), passed()):
            arrived.wait_recv()
            fwd.start()

    def finish(self, refs):
        mine, first, _, passed, last = self._bind(refs)
        for cp in last():
            cp.wait_recv()
        for cp in first() + passed():
            cp.wait_send()
        for cp in mine():
            cp.wait()


class _ChipExchange:
    def __init__(self, parts, used):
        self.ncl = len(parts)
        self.used = list(used)
        self.out_shape = [_out(p.shape, p.dtype) for p in parts]
        self.scratch = [pltpu.SemaphoreType.DMA((3 * self.ncl,)), pltpu.SemaphoreType.DMA((3 * self.ncl,))]
        self.operands = [_hbm(p) for p in parts]
        self.n = self.ncl

    def _bind(self, refs):
        ncl = self.ncl
        ins, outs = refs[:ncl], refs[ncl:2 * ncl]
        send_sems, recv_sems = refs[2 * ncl:]
        x, y, c = _place()
        chips = [(1 - x, y), (x, 1 - y), (1 - x, 1 - y)]
        here = 2 * x + y

        def copies(outgoing):
            out = []
            for k in range(ncl):
                rows = pl.ds(0, self.used[k])
                for j, (cx, cy) in enumerate(chips):
                    there = 2 * cx + cy
                    src, dst = (there, here) if outgoing else (here, there)
                    out.append(pltpu.make_async_remote_copy(
                        src_ref=ins[k].at[src, rows, :], dst_ref=outs[k].at[dst, rows, :],
                        send_sem=send_sems.at[3 * k + j], recv_sem=recv_sems.at[3 * k + j],
                        device_id=(cx, cy, c), device_id_type=MESH))
            return out

        return copies

    def start(self, refs):
        for cp in self._bind(refs)(True):
            cp.start()

    def forward(self, refs):
        pass

    def finish(self, refs):
        copies = self._bind(refs)
        for cp in copies(False):
            cp.wait_recv()
        for cp in copies(True):
            cp.wait_send()


class _Both:
    def __init__(self, members):
        self.members = list(members)
        self.n = sum(m.n for m in self.members)
        self.out_shape = [s for m in self.members for s in m.out_shape]
        self.scratch = [s for m in self.members for s in m.scratch]
        self.operands = [o for m in self.members for o in m.operands]

    def split(self, arrays):
        out, a = [], 0
        for m in self.members:
            out.append(list(arrays[a:a + m.n]))
            a += m.n
        return out

    def _refs(self, refs):
        ins, outs = self.split(refs[:self.n]), self.split(refs[self.n:2 * self.n])
        scr, b = [], 2 * self.n
        for m in self.members:
            scr.append(list(refs[b:b + len(m.scratch)]))
            b += len(m.scratch)
        return [(*i, *o, *s) for i, o, s in zip(ins, outs, scr)]

    def start(self, refs):
        for m, r in zip(self.members, self._refs(refs)):
            m.start(r)

    def forward(self, refs):
        for m, r in zip(self.members, self._refs(refs)):
            m.forward(r)

    def finish(self, refs):
        for m, r in zip(self.members, self._refs(refs)):
            m.finish(r)


def _exchange_alone(ex, *, name):
    def body(*refs):
        ex.start(refs)
        ex.forward(refs)
        ex.finish(refs)

    return pl.pallas_call(
        body, name=name, in_specs=[ANY] * ex.n, out_specs=[ANY] * ex.n,
        out_shape=ex.out_shape, scratch_shapes=ex.scratch,
    )(*ex.operands)


def _host_call(body, *, name, grid, in_specs, out_specs, out_shape, args, sem, est, ride=None, scratch=()):
    scratch = list(scratch)
    if ride is None:
        outs = pl.pallas_call(body, name=name, grid=grid, in_specs=in_specs, out_specs=out_specs,
                              out_shape=out_shape, scratch_shapes=scratch,
                              compiler_params=_params(sem, est))(*args)
        return list(outs), []
    n_in, n_out, n, n_scr = len(in_specs), len(out_specs), ride.n, len(scratch)

    def full(*refs):
        ins, rin = refs[:n_in], refs[n_in:n_in + n]
        outs, rout = refs[n_in + n:n_in + n + n_out], refs[n_in + n + n_out:n_in + 2 * n + n_out]
        own = refs[n_in + 2 * n + n_out:n_in + 2 * n + n_out + n_scr]
        rrefs = (*rin, *rout, *refs[n_in + 2 * n + n_out + n_scr:])
        step, total = _ride(ride, rrefs, grid)
        body(*ins, *outs, *own)
        _ride_end(ride, rrefs, step, total)

    outs = pl.pallas_call(
        full, name=name, grid=grid,
        in_specs=list(in_specs) + [ANY] * n, out_specs=list(out_specs) + [ANY] * n,
        out_shape=list(out_shape) + ride.out_shape, scratch_shapes=scratch + ride.scratch,
        compiler_params=_params(("arbitrary",) * len(grid), est),
    )(*args, *ride.operands)
    return list(outs[:n_out]), list(outs[n_out:])


def _ride(ex, refs, grid):
    total = math.prod(grid)
    step = pl.program_id(0)
    for axis in range(1, len(grid)):
        step = step * grid[axis] + pl.program_id(axis)
    pl.when(step == 0)(lambda: ex.start(refs))
    return step, total


def _ride_end(ex, refs, step, total):
    pl.when(step == (3 * total) // 4)(lambda: ex.forward(refs))
    pl.when(step == total - 1)(lambda: ex.finish(refs))


def _class_layout(grads, classes):
    used = [0] * len(set(classes))
    offs = []
    for g, cl in zip(grads, classes):
        offs.append(used[cl])
        used[cl] += g.shape[0] // N_DEV
    return offs, used


def _rs_to_sibling(grads, classes, *, name):
    n = len(grads)
    offs, used = _class_layout(grads, classes)
    heights = used
    ncl = len(heights)
    cols = [next(g.shape[1] for g, cl in zip(grads, classes) if cl == k) for k in range(ncl)]

    def body(*refs):
        gs, land = refs[:n], refs[n:n + ncl]
        send_sems, recv_sems = refs[n + ncl:]
        x, y, c = _place()
        sib = (x, y, 1 - c)
        for p in range(4):
            for w in range(n):
                r = grads[w].shape[0] // N_DEV
                cl = classes[w]
                there = gs[w].at[pl.ds(pl.multiple_of((2 * p + 1 - c) * r, 8), r), :]
                pltpu.make_async_remote_copy(
                    src_ref=there, dst_ref=land[cl].at[p, pl.ds(offs[w], r), :],
                    send_sem=send_sems.at[cl * 4 + p], recv_sem=recv_sems.at[cl * 4 + p],
                    device_id=sib, device_id_type=MESH).start()
        for cl in range(ncl):
            for p in range(4):
                rows_used = land[cl].at[p, pl.ds(0, used[cl]), :]
                slab = pltpu.make_async_remote_copy(
                    src_ref=rows_used, dst_ref=rows_used,
                    send_sem=send_sems.at[cl * 4 + p], recv_sem=recv_sems.at[cl * 4 + p],
                    device_id=sib, device_id_type=MESH)
                slab.wait_send()
                slab.wait_recv()

    return pl.pallas_call(
        body, name=name,
        in_specs=[ANY] * n, out_specs=[ANY] * ncl,
        out_shape=[_out((4, heights[k], cols[k]), BF16) for k in range(ncl)],
        scratch_shapes=[pltpu.SemaphoreType.DMA((4 * ncl,))] * 2,
    )(*[_hbm(g) for g in grads])


def _all_reduce_small(v, *, name):
    R = v.shape[0]

    def body(v_ref, o_ref, buf, send_sems, recv_sems):
        x, y, c = _place()
        me = 4 * x + 2 * y + c
        buf[me] = v_ref[...]
        copies = []
        for k in range(1, N_DEV):
            fx, fy, fc = (k >> 2) & 1, (k >> 1) & 1, k & 1
            to = (x ^ fx, y ^ fy, c ^ fc)
            cp = pltpu.make_async_remote_copy(
                src_ref=v_ref, dst_ref=buf.at[me],
                send_sem=send_sems.at[k - 1], recv_sem=recv_sems.at[k - 1],
                device_id=to, device_id_type=MESH)
            cp.start()
            copies.append(cp)
        for k in range(1, N_DEV):
            fx, fy, fc = (k >> 2) & 1, (k >> 1) & 1, k & 1
            frm = 4 * (x ^ fx) + 2 * (y ^ fy) + (c ^ fc)
            pltpu.make_async_remote_copy(
                src_ref=v_ref, dst_ref=buf.at[frm],
                send_sem=send_sems.at[k - 1], recv_sem=recv_sems.at[k - 1],
                device_id=(x ^ fx, y ^ fy, c ^ fc), device_id_type=MESH).wait_recv()
        for cp in copies:
            cp.wait_send()
        acc = buf[0]
        for d in range(1, N_DEV):
            acc = acc + buf[d]
        o_ref[...] = acc

    vm = pl.BlockSpec(memory_space=pltpu.VMEM)
    return pl.pallas_call(
        body, name=name, in_specs=[vm], out_specs=vm,
        out_shape=jax.ShapeDtypeStruct((R, LANE), F32),
        scratch_shapes=[pltpu.VMEM((N_DEV, R, LANE), F32),
                        pltpu.SemaphoreType.DMA((N_DEV - 1,)), pltpu.SemaphoreType.DMA((N_DEV - 1,))],
        compiler_params=pltpu.CompilerParams(vmem_limit_bytes=VMEM_FLOOR),
    )(v)


def _rope_tables(positions):
    half = QK_ROPE // 2
    inv_freq = ROPE_BASE ** (-jnp.arange(half, dtype=F32) / half)
    ang = positions.astype(F32)[:, None] * inv_freq
    cos, sin = jnp.cos(ang), jnp.sin(ang)
    z = jnp.zeros_like(cos)
    z2 = jnp.zeros((positions.shape[0], LANE - QK_ROPE), F32)
    rc = jnp.concatenate([cos, cos, z2], axis=1)
    rs1 = jnp.concatenate([-sin, z, z2], axis=1)
    rs2 = jnp.concatenate([z, sin, z2], axis=1)
    return rc, rs1, rs2


def _block_diag(pool_w):
    G, pg, _ = pool_w.shape
    out = jnp.zeros((G * pg, G * pg), pool_w.dtype)
    for g in range(G):
        out = lax.dynamic_update_slice(out, pool_w[g], (g * pg, g * pg))
    return out


def kernel(x, mem, positions, ln_g, ln_b, ffn1_w13, ffn1_w2, w_in, pool_w, pool_scale, q_norm_g, w_uq, kv_norm_g, w_ukv, w_out, mem_wq, mem_wkv, mem_wo, ffn2_w13, ffn2_w2, loss_target, m_ln_g, m_ln_b, m_ffn1_w13, m_ffn1_w2, m_w_in, m_pool_w, m_pool_scale, m_q_norm_g, m_w_uq, m_kv_norm_g, m_w_ukv, m_w_out, m_mem_wq, m_mem_wkv, m_mem_wo, m_ffn2_w13, m_ffn2_w2, v_ln_g, v_ln_b, v_ffn1_w13, v_ffn1_w2, v_w_in, v_pool_w, v_pool_scale, v_q_norm_g, v_w_uq, v_kv_norm_g, v_w_ukv, v_w_out, v_mem_wq, v_mem_wkv, v_mem_wo, v_ffn2_w13, v_ffn2_w2):
    L = ln_g.shape[0]
    T, D = x.shape[1], x.shape[2]
    F = ffn1_w2.shape[1] * N_DEV
    PW = D // 4
    H = (D - PW) // V_HEAD
    DIN = w_in.shape[2]
    DINP = PW + Q_LORA + KV_LORA + LANE
    QW = QK_NOPE + QK_ROPE
    alpha = (2 * L) ** 0.25
    x2d = x.reshape(T, D)
    memb = mem.reshape(mem.shape[1], D).astype(BF16)
    target = loss_target.reshape(T, D)
    tabs = _rope_tables(positions.reshape(T))

    def shards_of(l):
        return dict(
            w13a=ffn1_w13[l].T[None].astype(BF16),
            w13b=ffn2_w13[l].T[None].astype(BF16),
            w2a=ffn1_w2[l][None].astype(BF16),
            w2b=ffn2_w2[l][None].astype(BF16),
            wsq=jnp.stack([w_out[l], mem_wq[l], mem_wo[l]]).astype(BF16),
            wkvT=mem_wkv[l].T[None].astype(BF16),
            winp=jnp.pad(w_in[l], ((0, 0), (0, DINP - DIN)))[None].astype(BF16),
            wuqT=w_uq[l].T[None].astype(BF16),
            wukvT=w_ukv[l].T[None].astype(BF16),
        )

    SMALL = ("winp", "wuqT", "wukvT")
    shards = [shards_of(l) for l in range(L)]
    W = [dict() for _ in range(L)]

    def rider(spec):
        return _Gather([shards[l][n] for l, n in spec]) if spec else None

    def arrived(spec, arrays):
        for (l, n), a in zip(spec, arrays):
            if n in ("w13a", "w13b"):
                a = _interleave(a, 1)
            elif n == "wuqT":
                a = jnp.pad(a.reshape(H, QW, Q_LORA), ((0, 0), (0, HEAD_PAD - QW), (0, 0)))
                a = a.reshape(1, H * HEAD_PAD, Q_LORA)
            elif n == "ln":
                a = jnp.moveaxis(a.reshape(N_DEV, 2, L, 4, D // N_DEV), 0, 3).reshape(2, L, 4, D)
                LN["g"], LN["b"] = a[0], a[1]
            W[l][n] = a

    LN = {}
    shards[0]["ln"] = jnp.concatenate([ln_g.reshape(1, 4 * L, -1), ln_b.reshape(1, 4 * L, -1)], axis=1)
    spec0 = [(0, "w13a")]
    arrived(spec0, _exchange_alone(rider(spec0), name="ag_first"))
    wbd = [_block_diag(pool_w[l]).astype(BF16) for l in range(L)]

    def ffn_fwd(l, which, xres, xb, k, spec):
        ab = "ab"[which]
        h13, a, rode = _ffn_up(xb, W[l]["w13" + ab], 0, name=f"l{l}_ffn{which}_up", ride=rider(spec))
        arrived(spec, rode)
        y, xo, xob = _mm_ln(a, W[l]["w2" + ab], 0, xres, LN["g"][l,k:k + 1], LN["b"][l,k:k + 1], alpha=alpha, s=0.5,
                            name=f"l{l}_ffn{which}_y_ln{k}")
        return dict(xres=xres, xb=xb, h13=h13, a=a, y=y), xo, xob

    saved = []
    xres, xb = x2d, x2d.astype(BF16)
    for l in range(L):
        sv = {}
        more = l + 1 < L
        Wl = W[l]
        spec = ([(0, "w2a"), (0, "ln"), *[(0, n) for n in SMALL], (0, "wkvT")] if l == 0
                else [(l, "wsq"), (l, "wkvT")])
        sv["ffn1"], x1, x1b = ffn_fwd(l, 0, xres, xb, 0, spec)
        hin = _mm(x1b, Wl["winp"], lead=0, name=f"l{l}_hin")
        pscale = pool_scale[l].reshape(1, PW)
        gq, gkv = q_norm_g[l].reshape(1, Q_LORA), kv_norm_g[l].reshape(1, KV_LORA)
        cqn, ckvn, kpe = _norms_fwd(hin, gq, gkv, pw=PW, name=f"l{l}_norms")
        qh, kh, vh = _heads_fwd(cqn, ckvn, kpe, tabs, Wl["wuqT"][0], Wl["wukvT"][0], H=H, name=f"l{l}_heads")
        spec = [(l, "w13b"), (l, "w2b")] + ([(0, "wsq")] if l == 0 else []) + ([(l + 1, "w13a")] if more else [])
        cat, lse, rode = _flash_fwd(qh, kh, vh, H=H, pw=PW, name=f"l{l}_flash", ride=rider(spec))
        arrived(spec, rode)
        cat = _pool_fwd(hin, wbd[l], pscale, cat, name=f"l{l}_pool")
        ymix, x2, x2b = _mm_ln(cat, Wl["wsq"], 0, x1, LN["g"][l,1:2], LN["b"][l,1:2], alpha=alpha, s=1.0,
                               name=f"l{l}_ymix_ln1")
        qm = _mm(x2b, Wl["wsq"], lead=1, out_dtype=BF16, name=f"l{l}_qm")
        kvm = _mm(memb, Wl["wkvT"], lead=0, tb=True, name=f"l{l}_kvm")
        km, vm = kvm[:, :D], kvm[:, D:]
        om = _mem_fwd(qm, km, vm, name=f"l{l}_memattn")
        ymem, x3, x3b = _mm_ln(om, Wl["wsq"], 2, x2, LN["g"][l,2:3], LN["b"][l,2:3], alpha=alpha, s=1.0,
                               name=f"l{l}_ymem_ln2")
        spec = [(l + 1, n) for n in ("w2a", *SMALL)] if more else []
        sv["ffn2"], x4, x4b = ffn_fwd(l, 1, x3, x3b, 3, spec)
        sv.update(x1=x1, x1b=x1b, hin=hin, pscale=pscale, gq=gq, gkv=gkv, cqn=cqn, ckvn=ckvn,
                  qh=qh, kh=kh, vh=vh, lse=lse, cat=cat, ymix=ymix, x2=x2, x2b=x2b, qm=qm, km=km, vm=vm,
                  om=om, ymem=ymem)
        saved.append(sv)
        xres, xb = x4, x4b


    gW = {}
    gS = {}

    def ln_of(l, k):
        sv = saved[l]
        x, y, s = {0: (sv["ffn1"]["xres"], sv["ffn1"]["y"], 0.5), 1: (sv["x1"], sv["ymix"], 1.0),
                   2: (sv["x2"], sv["ymem"], 1.0), 3: (sv["ffn2"]["xres"], sv["ffn2"]["y"], 0.5)}[k]
        return x, y, LN["g"][l, k:k + 1], s

    def dx_through_ln(a, b, lead, tb, add, into, name):
        x, y, g, s = ln_of(*into)
        dxres, dyb, dg, db = _mm_ln_bwd(a, b, lead, tb, add, x, y, g, alpha=alpha, s=s, name=name)
        gS[("ln_g", *into)], gS[("ln_b", *into)] = dg, db
        return dxres, dyb

    def ffn_bwd(l, which, sv, dxres, dyb, ride, into):
        tag = f"l{l}_ffn{which}"
        gW[("w2", which, l)] = _mm(sv["a"], dyb, ta=True, out_dtype=BF16, name=f"{tag}_dw2", tn=D)
        dh, rode = _ffn_down_bwd(dyb, W[l]["w2" + "ab"[which]], 0, sv["h13"], name=f"{tag}_dh", ride=ride)
        dw13 = _mm(dh, sv["xb"], ta=True, out_dtype=BF16, name=f"{tag}_dw13", tn=D)
        gW[("w13", which, l)] = _deinterleave(dw13, 0)
        w13 = W[l]["w13" + "ab"[which]]
        if into is not None:
            return dx_through_ln(dh, w13, 0, False, dxres, into, f"{tag}_dx"), rode
        last = rs_first_level(l, "c")
        dxn, got = _mm(dh, w13, lead=0, add=dxres, name=f"{tag}_dx", tn=D, ride=last["ex"])
        rs_last_level(last, got)
        return dxn, rode

    core = lax.axis_index("c").astype(jnp.int32).reshape(1)
    chip = (2 * lax.axis_index("x") + lax.axis_index("y")).astype(jnp.int32).reshape(1)
    gsh = {}

    def rs_first_level(l, group):
        keys, classes = {
            "a": ([("w13", 1, l), ("w2", 1, l), ("mem_wkv", l), ("mem_wq", l), ("mem_wo", l)], [0] * 5),
            "b": ([("w_out", l), ("w_in", l), ("w_uq", l), ("w_ukv", l)], [0, 1, 2, 3]),
            "c": ([("w13", 0, l), ("w2", 0, l)], [0, 0]),
        }[group]
        tag = f"l{l}{group}"
        garrs = []
        for key in keys:
            g = gW[key]
            if key[0] == "w_uq":
                g = g.reshape(H, HEAD_PAD, Q_LORA)[:, :QW, :].reshape(H * QW, Q_LORA)
            garrs.append(g)
        offs, used = _class_layout(garrs, classes)
        parts = list(_rs_to_sibling(garrs, classes, name=f"{tag}_rs_sibling"))
        for cl in range(len(parts)):
            mine = [w for w, c in enumerate(classes) if c == cl]
            parts[cl] = _pair_sum(core, [garrs[w] for w in mine], parts[cl], [offs[w] for w in mine],
                                  name=f"{tag}_rs_pair_sum{cl}")
        return dict(tag=tag, keys=keys, garrs=garrs, classes=classes, offs=offs, used=used, parts=parts,
                    ex=_ChipExchange(parts, used))

    def rs_last_level(st, gathered):
        sums = [_quad_sum(chip, p, a, u, name=f"{st['tag']}_rs_quad_sum{k}")
                for k, (p, a, u) in enumerate(zip(st["parts"], gathered, st["used"]))]
        for key, g, cl, off in zip(st["keys"], st["garrs"], st["classes"], st["offs"]):
            gsh[key] = sums[cl][off:off + g.shape[0] // N_DEV, :]

    top = (L - 1, 3)
    x_top, y_top, g_top, s_top = ln_of(*top)
    dxres, dyb, gS[("ln_g", *top)], gS[("ln_b", *top)], loss_blk = _loss_ln_bwd(
        x_top, y_top, g_top, xres, target, alpha=alpha, s=s_top, name="loss_ln_top_bwd")
    loss = lax.psum(loss_blk[0, 0], ("x", "y", "c"))
    above = None
    for l in reversed(range(L)):
        sv = saved[l]
        Wl = W[l]
        (dxres, dyb), _ = ffn_bwd(l, 1, sv["ffn2"], dxres, dyb, None, (l, 2))
        dom = _mm(dyb, Wl["wsq"], lead=2, tb=True, out_dtype=BF16, name=f"l{l}_dom")
        gW[("mem_wo", l)] = _mm(sv["om"], dyb, ta=True, out_dtype=BF16, name=f"l{l}_dwo", tn=D)
        dqm, dkm, dvm = _mem_bwd(sv["qm"], sv["km"], sv["vm"], dom, name=f"l{l}_memattn_bwd")
        dxres, dyb = dx_through_ln(dqm, Wl["wsq"], 1, True, dxres, (l, 1), f"l{l}_dx2")
        gW[("mem_wq", l)] = _mm(sv["x2b"], dqm, ta=True, out_dtype=BF16, name=f"l{l}_dwq", tn=D)
        dkvm = jnp.concatenate([dkm, dvm], axis=1).astype(BF16)
        gW[("mem_wkv", l)] = _mm(dkvm, memb, ta=True, out_dtype=BF16, name=f"l{l}_dwkv", tn=D)
        dcat = _mm(dyb, Wl["wsq"], lead=0, tb=True, name=f"l{l}_dcat", tn=D)
        gW[("w_out", l)] = _mm(sv["cat"], dyb, ta=True, out_dtype=BF16, name=f"l{l}_dwout", tn=D)
        riding = [rs_first_level(l, "a")] + ([above] if above else [])
        both = _Both([st["ex"] for st in riding])
        dqh, dkh, dvh, rode = _flash_bwd(sv["qh"], sv["kh"], sv["vh"], sv["cat"], dcat, sv["lse"], H=H, pw=PW,
                                         name=f"l{l}_flash_bwd", ride=both)
        for st, got in zip(riding, both.split(rode)):
            rs_last_level(st, got)
        dqraw, dkv, dkpe = _heads_bwd(dqh, dkh, dvh, tabs, H=H, name=f"l{l}_heads_bwd")
        dcq = _mm(dqraw, Wl["wuqT"], lead=0, name=f"l{l}_dcq")
        gW[("w_uq", l)] = _mm(dqraw, sv["cqn"], ta=True, out_dtype=BF16, name=f"l{l}_dwuq")
        dckv = _mm(dkv, Wl["wukvT"], lead=0, name=f"l{l}_dckv")
        gW[("w_ukv", l)] = _mm(dkv, sv["ckvn"], ta=True, out_dtype=BF16, name=f"l{l}_dwukv")
        du, dwbd, dps = _pool_bwd(sv["hin"], dcat, wbd[l], sv["pscale"], name=f"l{l}_pool_bwd")
        dhin, dgq, dgkv = _norms_bwd(sv["hin"], sv["gq"], sv["gkv"], dcq, dckv, dkpe, du, pw=PW,
                                     name=f"l{l}_norms_bwd")
        pg = PW // len(POOL_WINDOWS)
        gS[("pool_w", l)] = jnp.stack([dwbd[g * pg:(g + 1) * pg, g * pg:(g + 1) * pg]
                                       for g in range(len(POOL_WINDOWS))])
        gS[("pool_scale", l)], gS[("q_norm_g", l)], gS[("kv_norm_g", l)] = dps, dgq, dgkv
        dxres, dyb = dx_through_ln(dhin, Wl["winp"], 0, True, dxres, (l, 0), f"l{l}_dx1")
        gW[("w_in", l)] = _mm(sv["x1b"], dhin, ta=True, out_dtype=BF16, name=f"l{l}_dwin", tn=DINP)
        heads = rs_first_level(l, "b")
        below, rode = ffn_bwd(l, 0, sv["ffn1"], dxres, dyb, heads["ex"], (l - 1, 3) if l > 0 else None)
        rs_last_level(heads, rode)
        if l > 0:
            dxres, dyb = below
            above = rs_first_level(l, "c")
    grad_x = below.reshape(1, T, D)

    small_keys = []
    for l in range(L):
        small_keys += [("pool_w", l), ("pool_scale", l), ("q_norm_g", l), ("kv_norm_g", l)]
        small_keys += [("ln_g", l, k) for k in range(4)] + [("ln_b", l, k) for k in range(4)]
    flat = jnp.concatenate([gS[k].reshape(-1) for k in small_keys])
    n_small = flat.shape[0]
    rows = -(-n_small // (8 * LANE)) * 8
    flat = jnp.pad(flat, (0, rows * LANE - n_small)).reshape(rows, LANE)
    red = _all_reduce_small(flat, name="ar_small").reshape(-1)
    gsm, pos = {}, 0
    for k in small_keys:
        size = math.prod(gS[k].shape)
        gsm[k] = red[pos:pos + size].reshape(gS[k].shape)
        pos += size

    me = 4 * lax.axis_index("x") + 2 * lax.axis_index("y") + lax.axis_index("c")
    dsh = D // N_DEV
    stack = lambda f: jnp.stack([f(l) for l in range(L)])
    g_ln_g = stack(lambda l: jnp.concatenate([gsm[("ln_g", l, k)] for k in range(4)], axis=0))
    g_ln_b = stack(lambda l: jnp.concatenate([gsm[("ln_b", l, k)] for k in range(4)], axis=0))
    swapped = {
        "ffn1_w13": stack(lambda l: gsh[("w13", 0, l)]),
        "ffn2_w13": stack(lambda l: gsh[("w13", 1, l)]),
        "w_in": stack(lambda l: gsh[("w_in", l)][:, :DIN].T),
        "w_uq": stack(lambda l: gsh[("w_uq", l)]),
        "w_ukv": stack(lambda l: gsh[("w_ukv", l)]),
    }
    swap = lambda a: jnp.swapaxes(a, 1, 2)
    grads = {
        "ln_g": lax.dynamic_slice_in_dim(g_ln_g, me * dsh, dsh, axis=2),
        "ln_b": lax.dynamic_slice_in_dim(g_ln_b, me * dsh, dsh, axis=2),
        "ffn1_w2": stack(lambda l: gsh[("w2", 0, l)]),
        "pool_w": stack(lambda l: gsm[("pool_w", l)]),
        "pool_scale": stack(lambda l: gsm[("pool_scale", l)].reshape(PW)),
        "q_norm_g": stack(lambda l: gsm[("q_norm_g", l)].reshape(Q_LORA)),
        "kv_norm_g": stack(lambda l: gsm[("kv_norm_g", l)].reshape(KV_LORA)),
        "w_out": stack(lambda l: gsh[("w_out", l)]),
        "mem_wq": stack(lambda l: gsh[("mem_wq", l)]),
        "mem_wkv": stack(lambda l: gsh[("mem_wkv", l)].T),
        "mem_wo": stack(lambda l: gsh[("mem_wo", l)]),
        "ffn2_w2": stack(lambda l: gsh[("w2", 1, l)]),
        **{nme: swap(g) for nme, g in swapped.items()},
    }

    names = ["ln_g", "ln_b", "ffn1_w13", "ffn1_w2", "w_in", "pool_w", "pool_scale", "q_norm_g", "w_uq",
             "kv_norm_g", "w_ukv", "w_out", "mem_wq", "mem_wkv", "mem_wo", "ffn2_w13", "ffn2_w2"]
    weights = dict(ln_g=ln_g, ln_b=ln_b, ffn1_w13=ffn1_w13, ffn1_w2=ffn1_w2, w_in=w_in, pool_w=pool_w,
                   pool_scale=pool_scale, q_norm_g=q_norm_g, w_uq=w_uq, kv_norm_g=kv_norm_g, w_ukv=w_ukv,
                   w_out=w_out, mem_wq=mem_wq, mem_wkv=mem_wkv, mem_wo=mem_wo, ffn2_w13=ffn2_w13,
                   ffn2_w2=ffn2_w2)
    ms = dict(ln_g=m_ln_g, ln_b=m_ln_b, ffn1_w13=m_ffn1_w13, ffn1_w2=m_ffn1_w2, w_in=m_w_in, pool_w=m_pool_w,
              pool_scale=m_pool_scale, q_norm_g=m_q_norm_g, w_uq=m_w_uq, kv_norm_g=m_kv_norm_g,
              w_ukv=m_w_ukv, w_out=m_w_out, mem_wq=m_mem_wq, mem_wkv=m_mem_wkv, mem_wo=m_mem_wo,
              ffn2_w13=m_ffn2_w13, ffn2_w2=m_ffn2_w2)
    vs = dict(ln_g=v_ln_g, ln_b=v_ln_b, ffn1_w13=v_ffn1_w13, ffn1_w2=v_ffn1_w2, w_in=v_w_in, pool_w=v_pool_w,
              pool_scale=v_pool_scale, q_norm_g=v_q_norm_g, w_uq=v_w_uq, kv_norm_g=v_kv_norm_g,
              w_ukv=v_w_ukv, w_out=v_w_out, mem_wq=v_mem_wq, mem_wkv=v_mem_wkv, mem_wo=v_mem_wo,
              ffn2_w13=v_ffn2_w13, ffn2_w2=v_ffn2_w2)
    deltas, new_m, new_v = [], [], []
    for nme in names:
        if nme in swapped:
            d, mn, vn = [swap(o) for o in _adamw(swap(weights[nme]), swapped[nme], swap(ms[nme]), swap(vs[nme]),
                                                 name=f"adamw_{nme}")]
        else:
            d, mn, vn = _adamw(weights[nme], grads[nme], ms[nme], vs[nme], name=f"adamw_{nme}")
        deltas.append(d)
        new_m.append(mn)
        new_v.append(vn)
    return (loss, grad_x, *[grads[nme] for nme in names], *deltas, *new_m, *new_v)
```

```python
import functools
import math

import jax
import jax.numpy as jnp
from jax import lax
from jax.experimental import pallas as pl
from jax.experimental.pallas import tpu as pltpu

F32 = jnp.float32
BF16 = jnp.bfloat16
MESH = pl.DeviceIdType.MESH

CHUNK = 64
MEM_HEADS = 4
POOL_WINDOWS = (2, 4, 8, 16)
QK_NOPE = 128
QK_ROPE = 64
V_HEAD = 128
Q_LORA = 256
KV_LORA = 128
ROPE_BASE = 10000.0
LN_EPS = 1e-5
RMS_EPS = 1e-6
NEG_INF = -1e30
ADAM_LR = 0.001
ADAM_B1 = 0.9
ADAM_B2 = 0.999
ADAM_EPS = 1e-08
ADAM_WD = 0.01
ADAM_STEP = 10

N_DEV = 8
LANE = 128
HEAD_PAD = 2 * LANE
POOL_HALO = 16
VMEM_CAP = 56 * 1024 * 1024
VMEM_FLOOR = 32 * 1024 * 1024


def _tile(n, pref, mult):
    t = (min(pref, n) // mult) * mult
    while t >= mult:
        if n % t == 0:
            return t
        t -= mult
    return n


def _params(sem, est_bytes):
    limit = int(min(max(2 * est_bytes + (8 << 20), VMEM_FLOOR), VMEM_CAP))
    return pltpu.CompilerParams(dimension_semantics=sem, vmem_limit_bytes=limit)


def _nbytes(shape, dtype):
    return math.prod(shape) * jnp.dtype(dtype).itemsize


def _hbm(x):
    return pltpu.with_memory_space_constraint(x, pltpu.HBM)


def _out(shape, dtype):
    return pltpu.HBM(tuple(shape), dtype)


def _dg(a, b, ca, cb):
    return lax.dot_general(a.astype(BF16), b.astype(BF16), (((ca,), (cb,)), ((), ())),
                           preferred_element_type=F32)


@jax.custom_vjp
def _bdot_nn(a, b):
    return _dg(a, b, 1, 0)


def _bdot_nn_fwd(a, b):
    return _dg(a, b, 1, 0), (a, b)


def _bdot_nn_bwd(res, ct):
    a, b = res
    return _dg(ct, b, 1, 1).astype(a.dtype), _dg(a, ct, 0, 0).astype(b.dtype)


_bdot_nn.defvjp(_bdot_nn_fwd, _bdot_nn_bwd)


@jax.custom_vjp
def _bdot_nt(a, b):
    return _dg(a, b, 1, 1)


def _bdot_nt_fwd(a, b):
    return _dg(a, b, 1, 1), (a, b)


def _bdot_nt_bwd(res, ct):
    a, b = res
    return _dg(ct, b, 1, 0).astype(a.dtype), _dg(ct, a, 0, 0).astype(b.dtype)


_bdot_nt.defvjp(_bdot_nt_fwd, _bdot_nt_bwd)


@functools.partial(jax.custom_vjp, nondiff_argnums=(1,))
def _lane_roll(x, shift):
    return pltpu.roll(x, shift % x.shape[1], axis=1)


def _lane_roll_fwd(x, shift):
    return _lane_roll(x, shift), None


def _lane_roll_bwd(shift, _, ct):
    return (_lane_roll(ct, -shift),)


_lane_roll.defvjp(_lane_roll_fwd, _lane_roll_bwd)


@functools.partial(jax.custom_vjp, nondiff_argnums=(1, 2))
def _cols(x, lo, hi):
    return x[:, lo:hi]


def _cols_fwd(x, lo, hi):
    return x[:, lo:hi], x.shape[1]


def _cols_bwd(lo, hi, width, ct):
    parts = []
    if lo > 0:
        parts.append(jnp.zeros((ct.shape[0], lo), ct.dtype))
    parts.append(ct)
    if hi < width:
        parts.append(jnp.zeros((ct.shape[0], width - hi), ct.dtype))
    return (jnp.concatenate(parts, axis=1) if len(parts) > 1 else ct,)


_cols.defvjp(_cols_fwd, _cols_bwd)


MM_VMEM_BUDGET = 22 * 1024 * 1024


def _mm(a, b, *, name, ta=False, tb=False, out_dtype=F32, lead=None, add=None, add_scale=1.0,
        tm=1024, tn=1024, tk=8192, ride=None):
    if ta:
        K, M = a.shape
    else:
        M, K = a.shape
    bshape = b.shape[1:] if lead is not None else b.shape
    if tb:
        N, Kb = bshape
    else:
        Kb, N = bshape
    assert K == Kb, (name, a.shape, b.shape)

    def blocks(tm, tn, tk):
        tm = _tile(M, tm, LANE if ta else 16)
        tn = _tile(N, tn, LANE)
        tk = _tile(K, tk, LANE)
        nbytes = (tm * tk * a.dtype.itemsize + tk * tn * b.dtype.itemsize
                  + tm * tn * (jnp.dtype(out_dtype).itemsize + (4 if K // tk > 1 else 0)
                               + (add.dtype.itemsize if add is not None else 0)))
        return tm, tn, tk, nbytes

    tm, tn, tk, est = blocks(tm, tn, tk)
    for shrink in ("m", "k", "m", "k", "n"):
        if est <= MM_VMEM_BUDGET:
            break
        if shrink == "m":
            tm, tn, tk, est = blocks(max(tm // 2, LANE), tn, tk)
        elif shrink == "k":
            tm, tn, tk, est = blocks(tm, tn, max(tk // 2, LANE))
        else:
            tm, tn, tk, est = blocks(tm, max(tn // 2, LANE), tk)
    nk = K // tk
    ca = 0 if ta else 1
    cb = 1 if tb else 0

    def body(*refs):
        a_ref, b_ref = refs[0], refs[1]
        add_ref = refs[2] if add is not None else None
        o_ref = refs[3] if add is not None else refs[2]

        def finish(r):
            if add_ref is not None:
                r = r + add_scale * add_ref[...].astype(F32)
            o_ref[...] = r.astype(o_ref.dtype)

        if nk == 1:
            finish(_dg(a_ref[...], b_ref[...], ca, cb))
            return
        acc_ref = refs[-1]
        k = pl.program_id(2)

        @pl.when(k == 0)
        def _():
            acc_ref[...] = jnp.zeros_like(acc_ref)

        acc_ref[...] += _dg(a_ref[...], b_ref[...], ca, cb)

        @pl.when(k == nk - 1)
        def _():
            finish(acc_ref[...])

    a_blk = (tk, tm) if ta else (tm, tk)
    a_map = (lambda i, j, k: (k, i)) if ta else (lambda i, j, k: (i, k))
    b_blk = (tn, tk) if tb else (tk, tn)
    if lead is None:
        b_map = (lambda i, j, k: (j, k)) if tb else (lambda i, j, k: (k, j))
        b_spec = pl.BlockSpec(b_blk, b_map)
    else:
        b_map = (lambda i, j, k: (lead, j, k)) if tb else (lambda i, j, k: (lead, k, j))
        b_spec = pl.BlockSpec((None,) + b_blk, b_map)
    in_specs = [pl.BlockSpec(a_blk, a_map), b_spec]
    args = [a, b]
    if add is not None:
        in_specs.append(pl.BlockSpec((tm, tn), lambda i, j, k: (i, j)))
        args.append(add)
    (out,), rode = _host_call(
        body, name=name,
        grid=(M // tm, N // tn, nk),
        in_specs=in_specs,
        out_specs=[pl.BlockSpec((tm, tn), lambda i, j, k: (i, j))],
        out_shape=[_out((M, N), out_dtype)],
        scratch=[pltpu.VMEM((tm, tn), F32)] if nk > 1 else [],
        args=[_hbm(v) for v in args], sem=("parallel", "parallel", "arbitrary"), est=est + tm * tn * 4,
        ride=ride)
    return out if ride is None else (out, rode)


def _rowwise(fn, tiles, params, tile_outs, acc_outs=(), *, tm, name):
    tile_arrays, tile_specs = [], []
    for t in tiles:
        if isinstance(t, tuple):
            tile_arrays.append(t[0])
            tile_specs.append(t[1])
        else:
            tile_arrays.append(t)
            tile_specs.append(pl.BlockSpec((tm, t.shape[1]), lambda i: (i, 0)))
    T = tile_arrays[0].shape[0]
    nt, np_, nto, nao = len(tile_arrays), len(params), len(tile_outs), len(acc_outs)

    def body(*refs):
        i = pl.program_id(0)
        tvals = [r[...] for r in refs[:nt]]
        pvals = [r[...] for r in refs[nt:nt + np_]]
        to_refs = refs[nt + np_:nt + np_ + nto]
        ao_refs = refs[nt + np_ + nto:]
        touts, aouts = fn(i, tvals, pvals)
        for r, v in zip(to_refs, touts):
            r[...] = v.astype(r.dtype)
        if nao:
            @pl.when(i == 0)
            def _():
                for r in ao_refs:
                    r[...] = jnp.zeros_like(r)
            for r, v in zip(ao_refs, aouts):
                r[...] += v.astype(r.dtype)

    in_specs = tile_specs + [pl.BlockSpec(p.shape, lambda i: (0, 0)) for p in params]
    out_specs = [pl.BlockSpec((tm, c), lambda i: (i, 0)) for c, _ in tile_outs]
    out_specs += [pl.BlockSpec(s, lambda i: (0, 0)) for s, _ in acc_outs]
    out_shape = [_out((T, c), d) for c, d in tile_outs]
    out_shape += [_out(s, d) for s, d in acc_outs]
    width = sum(s.block_shape[-1] for s in tile_specs) + sum(c for c, _ in tile_outs)
    est = 6 * tm * width * 4 + sum(_nbytes(p.shape, F32) for p in params) * 4
    return pl.pallas_call(
        body, name=name, grid=(T // tm,),
        in_specs=in_specs, out_specs=out_specs, out_shape=out_shape,
        compiler_params=_params(("arbitrary",) if nao else ("parallel",), est),
    )(*[_hbm(v) for v in tile_arrays], *[_hbm(p) for p in params])


def _ln_fn(alpha, s, xres, y, g, b):
    z = alpha * xres.astype(F32) + s * y.astype(F32)
    mu = jnp.mean(z, axis=-1, keepdims=True)
    zc = z - mu
    var = jnp.mean(zc * zc, axis=-1, keepdims=True)
    return zc * lax.rsqrt(var + LN_EPS) * g + b


def _mm_ln(a, b, lead, xres, g, bias, *, alpha, s, name):
    M, K = a.shape
    N = b.shape[2]
    tm = _tile(M, 256, 16)

    def body(a_ref, b_ref, x_ref, g_ref, bias_ref, y_ref, xo_ref, xb_ref):
        y = _dg(a_ref[...], b_ref[...], 1, 0)
        y_ref[...] = y.astype(y_ref.dtype)
        out = _ln_fn(alpha, s, x_ref[...], y, g_ref[...], bias_ref[...])
        xo_ref[...] = out
        xb_ref[...] = out.astype(BF16)

    row = pl.BlockSpec((tm, N), lambda i: (i, 0))
    vec = pl.BlockSpec((1, N), lambda i: (0, 0))
    est = tm * K * 2 + K * N * 2 + tm * N * (4 + 4 + 4 + 2 + 8)
    return pl.pallas_call(
        body, name=name, grid=(M // tm,),
        in_specs=[pl.BlockSpec((tm, K), lambda i: (i, 0)), pl.BlockSpec((None, K, N), lambda i: (lead, 0, 0)),
                  row, vec, vec],
        out_specs=[row, row, row],
        out_shape=[_out((M, N), BF16), _out((M, N), F32), _out((M, N), BF16)],
        compiler_params=_params(("parallel",), est),
    )(_hbm(a), _hbm(b), _hbm(xres), _hbm(g), _hbm(bias))


def _ln_bwd_math(alpha, s, x, y, g, d):
    z = alpha * x + s * y.astype(F32)
    zc = z - jnp.mean(z, axis=-1, keepdims=True)
    r = lax.rsqrt(jnp.mean(zc * zc, axis=-1, keepdims=True) + LN_EPS)
    xh = zc * r
    dxh = d * g
    dz = r * (dxh - jnp.mean(dxh, axis=-1, keepdims=True) - xh * jnp.mean(dxh * xh, axis=-1, keepdims=True))
    return alpha * dz, s * dz, jnp.sum(d * xh, axis=0, keepdims=True), jnp.sum(d, axis=0, keepdims=True)


def _mm_ln_bwd(a, b, lead, tb, add, xres, y, g, *, alpha, s, name):
    M, K = a.shape
    N = b.shape[1] if tb else b.shape[2]
    tk = K if K * N * 2 <= MM_VMEM_BUDGET * 3 // 5 else _tile(K, 2816, LANE)
    tm = _tile(M, 512 if K * N * 2 <= MM_VMEM_BUDGET // 4 else 256, 16)
    nk = K // tk
    cb = 1 if tb else 0

    def body(a_ref, b_ref, add_ref, x_ref, y_ref, g_ref, dx_ref, dy_ref, dg_ref, db_ref, *scratch):
        i, k = pl.program_id(0), pl.program_id(1)

        def finish(d):
            @pl.when(i == 0)
            def _():
                dg_ref[...] = jnp.zeros_like(dg_ref)
                db_ref[...] = jnp.zeros_like(db_ref)

            dx, dy, dg, db = _ln_bwd_math(alpha, s, x_ref[...], y_ref[...], g_ref[...], d + add_ref[...])
            dx_ref[...] = dx
            dy_ref[...] = dy.astype(dy_ref.dtype)
            dg_ref[...] += dg
            db_ref[...] += db

        if nk == 1:
            finish(_dg(a_ref[...], b_ref[...], 1, cb))
            return
        acc_ref = scratch[0]

        @pl.when(k == 0)
        def _():
            acc_ref[...] = jnp.zeros_like(acc_ref)

        acc_ref[...] += _dg(a_ref[...], b_ref[...], 1, cb)

        @pl.when(k == nk - 1)
        def _():
            finish(acc_ref[...])

    row = pl.BlockSpec((tm, N), lambda i, k: (i, 0))
    vec = pl.BlockSpec((1, N), lambda i, k: (0, 0))
    b_spec = (pl.BlockSpec((None, N, tk), lambda i, k: (lead, 0, k)) if tb
              else pl.BlockSpec((None, tk, N), lambda i, k: (lead, k, 0)))
    est = tm * tk * 2 + tk * N * 2 + tm * N * (4 + 4 + 2 + 4 + 2 + 4 + 12)
    return pl.pallas_call(
        body, name=name, grid=(M // tm, nk),
        in_specs=[pl.BlockSpec((tm, tk), lambda i, k: (i, k)), b_spec, row, row, row, vec],
        out_specs=[row, row, vec, vec],
        out_shape=[_out((M, N), F32), _out((M, N), BF16), _out((1, N), F32), _out((1, N), F32)],
        scratch_shapes=[pltpu.VMEM((tm, N), F32)] if nk > 1 else [],
        compiler_params=_params(("arbitrary", "arbitrary"), est),
    )(_hbm(a), _hbm(b), _hbm(add), _hbm(xres), _hbm(y), _hbm(g))


def _loss_ln_bwd(xres, y, g, out, target, *, alpha, s, name):
    T, D = xres.shape
    tm = _tile(T, 256, 16)

    def body(x_ref, y_ref, o_ref, t_ref, g_ref, dx_ref, dy_ref, dg_ref, db_ref, loss_ref):
        @pl.when(pl.program_id(0) == 0)
        def _():
            dg_ref[...] = jnp.zeros_like(dg_ref)
            db_ref[...] = jnp.zeros_like(db_ref)
            loss_ref[...] = jnp.zeros_like(loss_ref)

        err = o_ref[...] - t_ref[...]
        part = 0.5 * jnp.sum(jnp.sum(err * err, axis=1, keepdims=True) / D, axis=0, keepdims=True)
        loss_ref[...] += jnp.broadcast_to(part, loss_ref.shape)
        dx, dy, dg, db = _ln_bwd_math(alpha, s, x_ref[...], y_ref[...], g_ref[...], err / D)
        dx_ref[...] = dx
        dy_ref[...] = dy.astype(dy_ref.dtype)
        dg_ref[...] += dg
        db_ref[...] += db

    row = pl.BlockSpec((tm, D), lambda i: (i, 0))
    vec = pl.BlockSpec((1, D), lambda i: (0, 0))
    return pl.pallas_call(
        body, name=name, grid=(T // tm,),
        in_specs=[row, row, row, row, vec],
        out_specs=[row, row, vec, vec, pl.BlockSpec((8, LANE), lambda i: (0, 0))],
        out_shape=[_out((T, D), F32), _out((T, D), BF16), _out((1, D), F32), _out((1, D), F32),
                   _out((8, LANE), F32)],
        compiler_params=_params(("arbitrary",), 14 * tm * D * 4),
    )(_hbm(xres), _hbm(y), _hbm(out), _hbm(target), _hbm(g))


FFN_TILE = 256


def _interleave(w, axis):
    n = w.shape[axis] // (2 * FFN_TILE)
    shp = w.shape[:axis] + (2, n, FFN_TILE) + w.shape[axis + 1:]
    return jnp.swapaxes(w.reshape(shp), axis, axis + 1).reshape(w.shape)


def _deinterleave(w, axis):
    n = w.shape[axis] // (2 * FFN_TILE)
    shp = w.shape[:axis] + (n, 2, FFN_TILE) + w.shape[axis + 1:]
    return jnp.swapaxes(w.reshape(shp), axis, axis + 1).reshape(w.shape)


def _ffn_up(xb, w13t, lead, *, name, ride=None):
    T, D = xb.shape
    F = w13t.shape[1] // 2
    tc = FFN_TILE
    tm = _tile(T, 2048, 16)

    def body(x_ref, w_ref, h_ref, a_ref):
        h = _dg(x_ref[...], w_ref[...], 1, 1)
        g, u = h[:, :tc], h[:, tc:]
        h_ref[...] = h.astype(h_ref.dtype)
        a_ref[...] = (g * jax.nn.sigmoid(g) * u).astype(a_ref.dtype)

    est = (tm * D + 2 * tc * D + 3 * tm * tc) * 2 + 3 * tm * tc * 4
    (h13, a), gathered = _host_call(
        body, name=name, grid=(T // tm, F // tc),
        in_specs=[pl.BlockSpec((tm, D), lambda i, j: (i, 0)),
                  pl.BlockSpec((None, 2 * tc, D), lambda i, j: (lead, j, 0))],
        out_specs=[pl.BlockSpec((tm, 2 * tc), lambda i, j: (i, j)),
                   pl.BlockSpec((tm, tc), lambda i, j: (i, j))],
        out_shape=[_out((T, 2 * F), BF16), _out((T, F), BF16)],
        args=[_hbm(xb), _hbm(w13t)], sem=("parallel", "parallel"), est=est, ride=ride)
    return h13, a, gathered


def _ffn_down_bwd(dyb, w2, lead, h13, *, name, ride=None):
    T, D = dyb.shape
    F = w2.shape[1]
    tc = FFN_TILE
    tm = _tile(T, 2048, 16)

    def body(dy_ref, w_ref, h_ref, dh_ref):
        d = _dg(dy_ref[...], w_ref[...], 1, 1)
        h = h_ref[...].astype(F32)
        g, u = h[:, :tc], h[:, tc:]
        sig = jax.nn.sigmoid(g)
        gs = g * sig
        dh_ref[...] = jnp.concatenate([d * u * (sig + gs * (1.0 - sig)), d * gs], axis=1).astype(dh_ref.dtype)

    est = (tm * D + tc * D + 4 * tm * tc) * 2 + 6 * tm * tc * 4
    (dh,), rode = _host_call(
        body, name=name, grid=(T // tm, F // tc),
        in_specs=[pl.BlockSpec((tm, D), lambda i, j: (i, 0)),
                  pl.BlockSpec((None, tc, D), lambda i, j: (lead, j, 0)),
                  pl.BlockSpec((tm, 2 * tc), lambda i, j: (i, j))],
        out_specs=[pl.BlockSpec((tm, 2 * tc), lambda i, j: (i, j))],
        out_shape=[_out((T, 2 * F), BF16)],
        args=[_hbm(dyb), _hbm(w2), _hbm(h13)], sem=("parallel", "parallel"), est=est, ride=ride)
    return dh, rode


def _pool_select(parts, pw):
    pg = pw // len(POOL_WINDOWS)
    grp = lax.broadcasted_iota(jnp.int32, parts[0].shape, 1) // pg
    out = parts[3]
    for g in (2, 1, 0):
        out = jnp.where(grp == g, parts[g], out)
    return out


def _pool_count(t0, rows, pw):
    pg = pw // len(POOL_WINDOWS)
    grp = lax.broadcasted_iota(jnp.int32, (rows, pw), 1) // pg
    win = jnp.where(grp == 0, POOL_WINDOWS[0],
                    jnp.where(grp == 1, POOL_WINDOWS[1],
                              jnp.where(grp == 2, POOL_WINDOWS[2], POOL_WINDOWS[3])))
    t = t0 + lax.broadcasted_iota(jnp.int32, (rows, pw), 0)
    return jnp.minimum(t + 1, win).astype(F32)


def _window_sums(ext, up):
    n = ext.shape[0]
    sums, cur, k = [], ext, 1
    for _ in POOL_WINDOWS:
        cur = cur + pltpu.roll(cur, (n - k) if up else k, axis=0)
        sums.append(cur)
        k *= 2
    return sums


def _pool_delta(u, halo, t0):
    tm, pw = u.shape
    ext = jnp.concatenate([halo, u], axis=0)
    sums = [s[POOL_HALO:, :] for s in _window_sums(ext, up=False)]
    return _pool_select(sums, pw) / _pool_count(t0, tm, pw) - u


def _pool_fwd(hin, wbd, scale, cat, *, name):
    T = hin.shape[0]
    pw = wbd.shape[0]
    tm = _tile(T, 256, POOL_HALO)
    per = tm // POOL_HALO

    def body(u_ref, halo_ref, w_ref, s_ref, cat_ref, y_ref):
        i = pl.program_id(0)
        halo = jnp.where(i > 0, halo_ref[...], 0.0)
        d = _pool_delta(u_ref[...], halo, i * tm)
        y_ref[...] = (_dg(d, w_ref[...], 1, 0) * s_ref[...]).astype(y_ref.dtype)

    return pl.pallas_call(
        body, name=name, grid=(T // tm,),
        in_specs=[pl.BlockSpec((tm, pw), lambda i: (i, 0)),
                  pl.BlockSpec((POOL_HALO, pw), lambda i: (jnp.maximum(i * per - 1, 0), 0)),
                  pl.BlockSpec((pw, pw), lambda i: (0, 0)),
                  pl.BlockSpec((1, pw), lambda i: (0, 0)),
                  ANY],
        out_specs=pl.BlockSpec((tm, pw), lambda i: (i, 0)),
        out_shape=_out(cat.shape, cat.dtype),
        input_output_aliases={4: 0},
        compiler_params=_params(("parallel",), 16 * tm * pw * 4),
    )(_hbm(hin), _hbm(hin), _hbm(wbd), _hbm(scale), _hbm(cat))


def _pool_bwd(hin, dcat, wbd, scale, *, name):
    T = hin.shape[0]
    pw = wbd.shape[0]
    tm = _tile(T, 256, POOL_HALO)
    per = tm // POOL_HALO
    nt = T // tm

    def body(u_ref, halo_ref, dy_ref, dyn_ref, w_ref, s_ref, du_ref, dw_ref, ds_ref):
        i = pl.program_id(0)

        @pl.when(i == 0)
        def _():
            dw_ref[...] = jnp.zeros_like(dw_ref)
            ds_ref[...] = jnp.zeros_like(ds_ref)

        halo = jnp.where(i > 0, halo_ref[...], 0.0)
        d = _pool_delta(u_ref[...], halo, i * tm)
        w = w_ref[...]
        sc = s_ref[...]
        dy = dy_ref[...]
        dyn = jnp.where(i < nt - 1, dyn_ref[...], 0.0)
        ds_ref[...] += jnp.sum(dy * _dg(d, w, 1, 0), axis=0, keepdims=True)
        dys = dy * sc
        dw_ref[...] += _dg(d, dys, 0, 0)
        dys_ext = jnp.concatenate([dys, dyn * sc], axis=0)
        dd_ext = _dg(dys_ext, w, 1, 1)
        ddp = dd_ext / _pool_count(i * tm, tm + POOL_HALO, pw)
        sums = [s[:tm, :] for s in _window_sums(ddp, up=True)]
        du_ref[...] = _pool_select(sums, pw) - dd_ext[:tm, :]

    return pl.pallas_call(
        body, name=name, grid=(nt,),
        in_specs=[pl.BlockSpec((tm, pw), lambda i: (i, 0)),
                  pl.BlockSpec((POOL_HALO, pw), lambda i: (jnp.maximum(i * per - 1, 0), 0)),
                  pl.BlockSpec((tm, pw), lambda i: (i, 0)),
                  pl.BlockSpec((POOL_HALO, pw), lambda i: (jnp.minimum((i + 1) * per, nt * per - 1), 0)),
                  pl.BlockSpec((pw, pw), lambda i: (0, 0)),
                  pl.BlockSpec((1, pw), lambda i: (0, 0))],
        out_specs=[pl.BlockSpec((tm, pw), lambda i: (i, 0)),
                   pl.BlockSpec((pw, pw), lambda i: (0, 0)),
                   pl.BlockSpec((1, pw), lambda i: (0, 0))],
        out_shape=[_out((T, pw), F32),
                   _out((pw, pw), F32),
                   _out((1, pw), F32)],
        compiler_params=_params(("arbitrary",), 24 * tm * pw * 4),
    )(_hbm(hin), _hbm(hin), _hbm(dcat), _hbm(dcat), _hbm(wbd), _hbm(scale))


def _rms(x, g):
    return x * lax.rsqrt(jnp.mean(x * x, axis=-1, keepdims=True) + RMS_EPS) * g


def _norms_fn(pw, h, gq, gkv):
    o1 = pw + Q_LORA
    o2 = o1 + KV_LORA
    return (_rms(_cols(h, pw, o1), gq), _rms(_cols(h, o1, o2), gkv), _cols(h, o2, h.shape[1]))


def _norms_fwd(hin, gq, gkv, *, pw, name):
    tm = _tile(hin.shape[0], 256, 16)

    def fn(i, tv, pv):
        return _norms_fn(pw, tv[0], pv[0], pv[1]), ()

    return _rowwise(fn, [hin], [gq, gkv], [(Q_LORA, BF16), (KV_LORA, BF16), (LANE, F32)], tm=tm, name=name)


def _norms_bwd(hin, gq, gkv, dcq, dckv, dkpe, du, *, pw, name):
    tm = _tile(hin.shape[0], 256, 16)
    dinp = hin.shape[1]

    def fn(i, tv, pv):
        _, vjp = jax.vjp(functools.partial(_norms_fn, pw), tv[0], pv[0], pv[1])
        dh, dgq, dgkv = vjp((tv[1].astype(F32), tv[2].astype(F32), tv[3].astype(F32)))
        dh = jnp.concatenate([tv[4], dh[:, pw:]], axis=1)
        return (dh,), (dgq, dgkv)

    return _rowwise(fn, [hin, dcq, dckv, dkpe, du], [gq, gkv], [(dinp, BF16)],
                    [((1, Q_LORA), F32), ((1, KV_LORA), F32)], tm=tm, name=name)


def _heads_fn(H, qraw, kv, kpe, rc, rs1, rs2):
    half = QK_ROPE // 2
    scale = (QK_NOPE + QK_ROPE) ** -0.5

    def rope(blk):
        return blk * rc + _lane_roll(blk, -half) * rs1 + _lane_roll(blk, half) * rs2

    krot = rope(kpe)
    qs, ks, vs = [], [], []
    for h in range(H):
        lo = h * HEAD_PAD
        qs += [_cols(qraw, lo, lo + LANE) * scale, rope(_cols(qraw, lo + LANE, lo + HEAD_PAD)) * scale]
        ks += [_cols(kv, lo, lo + LANE), krot]
        vs += [_cols(kv, lo + LANE, lo + HEAD_PAD)]
    return jnp.concatenate(qs, axis=1), jnp.concatenate(ks, axis=1), jnp.concatenate(vs, axis=1)


def _heads_fwd(cqn, ckvn, kpe, tabs, wuq, wukv, *, H, name):
    tm = _tile(cqn.shape[0], 256, 16)

    def fn(i, tv, pv):
        qraw = _dg(tv[0], pv[0], 1, 1)
        kv = _dg(tv[1], pv[1], 1, 1)
        return _heads_fn(H, qraw, kv, *tv[2:]), ()

    return _rowwise(fn, [cqn, ckvn, kpe, *tabs], [wuq, wukv],
                    [(H * HEAD_PAD, BF16), (H * HEAD_PAD, BF16), (H * V_HEAD, BF16)], tm=tm, name=name)


def _heads_bwd(dq, dk, dv, tabs, *, H, name):
    tm = _tile(dq.shape[0], 256, 16)

    def fn(i, tv, pv):
        z = jnp.zeros((tm, H * HEAD_PAD), F32)
        zk = jnp.zeros((tm, LANE), F32)
        rc, rs1, rs2 = tv[3], tv[4], tv[5]
        _, vjp = jax.vjp(lambda a, b, c: _heads_fn(H, a, b, c, rc, rs1, rs2), z, z, zk)
        return vjp((tv[0].astype(F32), tv[1].astype(F32), tv[2].astype(F32))), ()

    return _rowwise(fn, [dq, dk, dv, *tabs], [],
                    [(H * HEAD_PAD, BF16), (H * HEAD_PAD, BF16), (LANE, F32)], tm=tm, name=name)


def _diag_mask(rows, cols, row0):
    r = (row0 + lax.broadcasted_iota(jnp.int32, (rows, cols), 0)) // CHUNK
    c = lax.broadcasted_iota(jnp.int32, (rows, cols), 1) // CHUNK
    return r >= c


def _flash_fwd(qh, kh, vh, *, H, pw, name, ride=None):
    T = qh.shape[0]
    t = _tile(T, 512, CHUNK)
    off = pw // V_HEAD


    def body(q_ref, k_ref, v_ref, o_ref, lse_ref):
        i = pl.program_id(1)
        q = q_ref[...]

        def blk(j, carry, masked):
            m, l, acc = carry
            rows = pl.ds(pl.multiple_of(j * t, t), t)
            s = _dg(q, k_ref[rows, :], 1, 1)
            if masked:
                s = jnp.where(_diag_mask(t, t, 0), s, NEG_INF)
            mn = jnp.maximum(m, jnp.max(s, axis=1, keepdims=True))
            p = jnp.exp(s - mn)
            corr = jnp.exp(m - mn)
            l = corr * l + jnp.sum(p, axis=1, keepdims=True)
            acc = corr * acc + _dg(p, v_ref[rows, :], 1, 0)
            return mn, l, acc

        init = (jnp.full((t, 1), NEG_INF, F32), jnp.zeros((t, 1), F32), jnp.zeros((t, V_HEAD), F32))
        carry = lax.fori_loop(0, i, lambda j, c: blk(j, c, False), init)
        m, l, acc = blk(i, carry, True)
        o_ref[...] = (acc / l).astype(o_ref.dtype)
        lse_ref[...] = jnp.broadcast_to(m + jnp.log(l), (t, V_HEAD))

    est = 2 * T * (HEAD_PAD + V_HEAD) * 2 + 8 * t * t * 4
    (o, lse), gathered = _host_call(
        body, name=name, grid=(H, T // t),
        in_specs=[pl.BlockSpec((t, HEAD_PAD), lambda h, i: (i, h)),
                  pl.BlockSpec((T, HEAD_PAD), lambda h, i: (0, h)),
                  pl.BlockSpec((T, V_HEAD), lambda h, i: (0, h))],
        out_specs=[pl.BlockSpec((t, V_HEAD), lambda h, i: (i, off + h)),
                   pl.BlockSpec((t, V_HEAD), lambda h, i: (i, h))],
        out_shape=[_out((T, pw + H * V_HEAD), BF16),
                   _out((T, H * V_HEAD), F32)],
        args=[_hbm(qh), _hbm(kh), _hbm(vh)], sem=("parallel", "parallel"), est=est, ride=ride)
    return o, lse, gathered


def _flash_bwd(qh, kh, vh, cat, dcat, lse, *, H, pw, name, ride=None):
    T = qh.shape[0]
    t = _tile(T, 512, CHUNK)
    nb = T // t
    off = pw // V_HEAD

    def body(q_ref, k_ref, v_ref, o_ref, do_ref, lse_ref, dq_out_ref, dk_ref, dv_ref, dq_ref):
        j = pl.program_id(1)

        @pl.when(j == 0)
        def _():
            dq_ref[...] = jnp.zeros_like(dq_ref)

        kj = k_ref[...]
        vj = v_ref[...]

        def blk(i, carry, masked):
            dk, dv = carry
            rows = pl.ds(pl.multiple_of(i * t, t), t)
            qi = q_ref[rows, :]
            doi = do_ref[rows, :]
            oi = o_ref[rows, :].astype(F32)
            lsei = lse_ref[rows, :][:, :1]
            s = _dg(qi, kj, 1, 1)
            if masked:
                s = jnp.where(_diag_mask(t, t, 0), s, NEG_INF)
            p = jnp.exp(s - lsei)
            dv = dv + _dg(p, doi, 0, 0)
            dp = _dg(doi, vj, 1, 1)
            di = jnp.sum(doi * oi, axis=1, keepdims=True)
            ds = p * (dp - di)
            dk = dk + _dg(ds, qi, 0, 0)
            dq_ref[rows, :] += _dg(ds, kj, 1, 0)
            return dk, dv

        carry = blk(j, (jnp.zeros((t, HEAD_PAD), F32), jnp.zeros((t, V_HEAD), F32)), True)
        dk, dv = lax.fori_loop(j + 1, nb, lambda i, c: blk(i, c, False), carry)
        dk_ref[...] = dk.astype(dk_ref.dtype)
        dv_ref[...] = dv.astype(dv_ref.dtype)

        @pl.when(j == nb - 1)
        def _():
            dq_out_ref[...] = dq_ref[...].astype(dq_out_ref.dtype)

    est = T * (HEAD_PAD * 2 + V_HEAD * 2 + V_HEAD * 4 + V_HEAD * 4 + HEAD_PAD * 4) + 10 * t * t * 4
    (dq, dk, dv), gathered = _host_call(
        body, name=name, grid=(H, nb),
        in_specs=[pl.BlockSpec((T, HEAD_PAD), lambda h, j: (0, h)),
                  pl.BlockSpec((t, HEAD_PAD), lambda h, j: (j, h)),
                  pl.BlockSpec((t, V_HEAD), lambda h, j: (j, h)),
                  pl.BlockSpec((T, V_HEAD), lambda h, j: (0, off + h)),
                  pl.BlockSpec((T, V_HEAD), lambda h, j: (0, off + h)),
                  pl.BlockSpec((T, V_HEAD), lambda h, j: (0, h))],
        out_specs=[pl.BlockSpec((T, HEAD_PAD), lambda h, j: (0, h)),
                   pl.BlockSpec((t, HEAD_PAD), lambda h, j: (j, h)),
                   pl.BlockSpec((t, V_HEAD), lambda h, j: (j, h))],
        out_shape=[_out((T, H * HEAD_PAD), BF16),
                   _out((T, H * HEAD_PAD), BF16),
                   _out((T, H * V_HEAD), BF16)],
        scratch=[pltpu.VMEM((T, HEAD_PAD), F32)],
        args=[_hbm(v) for v in (qh, kh, vh, cat, dcat, lse)], sem=("arbitrary", "arbitrary"), est=est,
        ride=ride)
    return dq, dk, dv, gathered


def _mem_fn(q, k, v):
    hd = q.shape[1] // MEM_HEADS
    outs = []
    for h in range(MEM_HEADS):
        lo, hi = h * hd, (h + 1) * hd
        s = _bdot_nt(_cols(q, lo, hi), _cols(k, lo, hi)) * hd ** -0.5
        e = jnp.exp(s - lax.stop_gradient(jnp.max(s, axis=1, keepdims=True)))
        p = e / jnp.sum(e, axis=1, keepdims=True)
        outs.append(_bdot_nn(p, _cols(v, lo, hi)))
    return jnp.concatenate(outs, axis=1)


def _mem_fwd(q, k, v, *, name):
    T, D = q.shape
    tm = _tile(T, 256, 16)

    def fn(i, tv, pv):
        return (_mem_fn(tv[0], pv[0], pv[1]),), ()

    return _rowwise(fn, [q], [k, v], [(D, BF16)], tm=tm, name=name)[0]


def _mem_bwd(q, k, v, do, *, name):
    T, D = q.shape
    tm = _tile(T, 256, 16)

    def fn(i, tv, pv):
        _, vjp = jax.vjp(_mem_fn, tv[0], pv[0], pv[1])
        dq, dk, dv = vjp(tv[1].astype(F32))
        return (dq,), (dk, dv)

    return _rowwise(fn, [q, do], [k, v], [(D, BF16)], [(k.shape, F32), (v.shape, F32)], tm=tm, name=name)


def _adamw(w, g, m, v, *, name):
    shape = w.shape
    if w.ndim != 3:
        lead3 = (1, math.prod(shape[:-1]), shape[-1])
        return [o.reshape(shape) for o in _adamw(*[a.reshape(lead3) for a in (w, g, m, v)], name=name)]
    Lw, R, C = shape
    tr = _tile(R, 512, 8)
    b1c = 1.0 - ADAM_B1 ** ADAM_STEP
    b2c = 1.0 - ADAM_B2 ** ADAM_STEP

    def body(w_ref, g_ref, m_ref, v_ref, d_ref, mo_ref, vo_ref):
        gg = g_ref[...]
        mn = ADAM_B1 * m_ref[...] + (1.0 - ADAM_B1) * gg
        vn = ADAM_B2 * v_ref[...] + (1.0 - ADAM_B2) * (gg * gg)
        d_ref[...] = -ADAM_LR * ((mn / b1c) / (jnp.sqrt(vn / b2c) + ADAM_EPS) + ADAM_WD * w_ref[...])
        mo_ref[...] = mn
        vo_ref[...] = vn

    spec = pl.BlockSpec((None, tr, C), lambda l, i: (l, i, 0))
    return pl.pallas_call(
        body, name=name, grid=(Lw, R // tr),
        in_specs=[spec] * 4, out_specs=[spec] * 3,
        out_shape=[_out(shape, F32)] * 3,
        compiler_params=_params(("parallel", "parallel"), 7 * tr * C * 4),
    )(*[_hbm(a) for a in (w, g, m, v)])


def _pair_sum(core, gs, landed, offs, *, name):
    n = len(gs)
    _, R, C = landed.shape
    rows = [g.shape[0] // N_DEV for g in gs]

    def body(core_ref, *refs):
        g_refs, l_ref, o_ref = refs[:n], refs[n], refs[n + 1]
        for g_ref, off, r in zip(g_refs, offs, rows):
            o_ref[off:off + r, :] = (g_ref[...].astype(F32) + l_ref[off:off + r, :].astype(F32)).astype(o_ref.dtype)

    slab = pl.BlockSpec((None, R, C), lambda p, core_ref: (p, 0, 0))
    own = [pl.BlockSpec((r, C), lambda p, core_ref: (2 * p + core_ref[0], 0)) for r in rows]
    return pl.pallas_call(
        body, name=name,
        grid_spec=pltpu.PrefetchScalarGridSpec(
            num_scalar_prefetch=1, grid=(4,), in_specs=own + [slab], out_specs=slab),
        out_shape=_out(landed.shape, landed.dtype),
        input_output_aliases={n + 1: 0},
        compiler_params=_params(("arbitrary",), 3 * R * C * 2 + R * C * 8),
    )(core, *[_hbm(g) for g in gs], _hbm(landed))


def _quad_sum(chip, part, gathered, used, *, name):
    C = part.shape[2]
    R = used
    tr = _tile(R, 256, 16)

    def body(chip_ref, own_ref, a_ref, b_ref, c_ref, o_ref):
        o_ref[...] = ((own_ref[...].astype(F32) + a_ref[...].astype(F32)) + b_ref[...].astype(F32)) \
            + c_ref[...].astype(F32)

    def other(k):
        return pl.BlockSpec((None, tr, C), lambda i, chip_ref: (chip_ref[0] ^ k, i, 0))

    return pl.pallas_call(
        body, name=name,
        grid_spec=pltpu.PrefetchScalarGridSpec(
            num_scalar_prefetch=1, grid=(R // tr,),
            in_specs=[pl.BlockSpec((None, tr, C), lambda i, chip_ref: (chip_ref[0], i, 0)),
                      other(1), other(2), other(3)],
            out_specs=pl.BlockSpec((tr, C), lambda i, chip_ref: (i, 0))),
        out_shape=_out((R, C), F32),
        compiler_params=_params(("arbitrary",), 8 * tr * C * 4),
    )(chip, _hbm(part), _hbm(gathered), _hbm(gathered), _hbm(gathered))


def _place():
    x, y, c = lax.axis_index("x"), lax.axis_index("y"), lax.axis_index("c")
    return x, y, c


ANY = pl.BlockSpec(memory_space=pl.ANY)


class _Gather:
    def __init__(self, shards):
        self.shards = list(shards)
        self.n = len(self.shards)
        self.out_shape = [_out((s.shape[0], N_DEV * s.shape[1], s.shape[2]), s.dtype)
                          for s in self.shards]
        self.scratch = [pltpu.SemaphoreType.DMA((7 * self.n,)), pltpu.SemaphoreType.DMA((7 * self.n,)),
                        pltpu.SemaphoreType.DMA((self.n,))]
        self.operands = [_hbm(s) for s in self.shards]

    def _bind(self, refs):
        n = self.n
        ins, outs = refs[:n], refs[n:2 * n]
        send_sems, recv_sems, local_sems = refs[2 * n:]
        x, y, c = _place()
        me, sib = (x, y, c), (x, y, 1 - c)
        chips = [(1 - x, y), (x, 1 - y), (1 - x, 1 - y)]

        def rows(w, p):
            r = self.shards[w].shape[1]
            idx = 4 * p[0] + 2 * p[1] + p[2]
            return outs[w].at[:, pl.ds(pl.multiple_of(idx * r, 8), r), :]

        def copy(w, k, block, to, src=None):
            return pltpu.make_async_remote_copy(
                src_ref=rows(w, block) if src is None else src, dst_ref=rows(w, block),
                send_sem=send_sems.at[w * 7 + k], recv_sem=recv_sems.at[w * 7 + k],
                device_id=to, device_id_type=MESH)

        def mine():
            return [pltpu.make_async_copy(ins[w], rows(w, me), local_sems.at[w]) for w in range(n)]

        def first():
            out = []
            for w in range(n):
                out.append(copy(w, 0, me, sib, src=ins[w]))
                out += [copy(w, 1 + j, me, (*chip, c), src=ins[w]) for j, chip in enumerate(chips)]
            return out

        def passed():
            return [copy(w, 4 + j, (*chip, c), sib) for j, chip in enumerate(chips) for w in range(n)]

        def landed():
            return [copy(w, 1 + j, (*chip, c), me) for j, chip in enumerate(chips) for w in range(n)]

        def last():
            out = []
            for w in range(n):
                out.append(copy(w, 0, sib, me))
                out += [copy(w, 4 + j, (*chip, 1 - c), me) for j, chip in enumerate(chips)]
            return out

        return mine, first, landed, passed, last

    def start(self, refs):
        mine, first, _, _, _ = self._bind(refs)
        for cp in mine() + first():
            cp.start()

    def forward(self, refs):
        _, _, landed, passed, _ = self._bind(refs)
        for arrived, fwd in zip(landed(), passed()):
            arrived.wait_recv()
            fwd.start()

    def finish(self, refs):
        mine, first, _, passed, last = self._bind(refs)
        for cp in last():
            cp.wait_recv()
        for cp in first() + passed():
            cp.wait_send()
        for cp in mine():
            cp.wait()


class _ChipExchange:
    def __init__(self, parts, used):
        self.ncl = len(parts)
        self.used = list(used)
        self.out_shape = [_out(p.shape, p.dtype) for p in parts]
        self.scratch = [pltpu.SemaphoreType.DMA((3 * self.ncl,)), pltpu.SemaphoreType.DMA((3 * self.ncl,))]
        self.operands = [_hbm(p) for p in parts]
        self.n = self.ncl

    def _bind(self, refs):
        ncl = self.ncl
        ins, outs = refs[:ncl], refs[ncl:2 * ncl]
        send_sems, recv_sems = refs[2 * ncl:]
        x, y, c = _place()
        chips = [(1 - x, y), (x, 1 - y), (1 - x, 1 - y)]
        here = 2 * x + y

        def copies(outgoing):
            out = []
            for k in range(ncl):
                rows = pl.ds(0, self.used[k])
                for j, (cx, cy) in enumerate(chips):
                    there = 2 * cx + cy
                    src, dst = (there, here) if outgoing else (here, there)
                    out.append(pltpu.make_async_remote_copy(
                        src_ref=ins[k].at[src, rows, :], dst_ref=outs[k].at[dst, rows, :],
                        send_sem=send_sems.at[3 * k + j], recv_sem=recv_sems.at[3 * k + j],
                        device_id=(cx, cy, c), device_id_type=MESH))
            return out

        return copies

    def start(self, refs):
        for cp in self._bind(refs)(True):
            cp.start()

    def forward(self, refs):
        pass

    def finish(self, refs):
        copies = self._bind(refs)
        for cp in copies(False):
            cp.wait_recv()
        for cp in copies(True):
            cp.wait_send()


class _Both:
    def __init__(self, members):
        self.members = list(members)
        self.n = sum(m.n for m in self.members)
        self.out_shape = [s for m in self.members for s in m.out_shape]
        self.scratch = [s for m in self.members for s in m.scratch]
        self.operands = [o for m in self.members for o in m.operands]

    def split(self, arrays):
        out, a = [], 0
        for m in self.members:
            out.append(list(arrays[a:a + m.n]))
            a += m.n
        return out

    def _refs(self, refs):
        ins, outs = self.split(refs[:self.n]), self.split(refs[self.n:2 * self.n])
        scr, b = [], 2 * self.n
        for m in self.members:
            scr.append(list(refs[b:b + len(m.scratch)]))
            b += len(m.scratch)
        return [(*i, *o, *s) for i, o, s in zip(ins, outs, scr)]

    def start(self, refs):
        for m, r in zip(self.members, self._refs(refs)):
            m.start(r)

    def forward(self, refs):
        for m, r in zip(self.members, self._refs(refs)):
            m.forward(r)

    def finish(self, refs):
        for m, r in zip(self.members, self._refs(refs)):
            m.finish(r)


def _exchange_alone(ex, *, name):
    def body(*refs):
        ex.start(refs)
        ex.forward(refs)
        ex.finish(refs)

    return pl.pallas_call(
        body, name=name, in_specs=[ANY] * ex.n, out_specs=[ANY] * ex.n,
        out_shape=ex.out_shape, scratch_shapes=ex.scratch,
    )(*ex.operands)


def _host_call(body, *, name, grid, in_specs, out_specs, out_shape, args, sem, est, ride=None, scratch=()):
    scratch = list(scratch)
    if ride is None:
        outs = pl.pallas_call(body, name=name, grid=grid, in_specs=in_specs, out_specs=out_specs,
                              out_shape=out_shape, scratch_shapes=scratch,
                              compiler_params=_params(sem, est))(*args)
        return list(outs), []
    n_in, n_out, n, n_scr = len(in_specs), len(out_specs), ride.n, len(scratch)

    def full(*refs):
        ins, rin = refs[:n_in], refs[n_in:n_in + n]
        outs, rout = refs[n_in + n:n_in + n + n_out], refs[n_in + n + n_out:n_in + 2 * n + n_out]
        own = refs[n_in + 2 * n + n_out:n_in + 2 * n + n_out + n_scr]
        rrefs = (*rin, *rout, *refs[n_in + 2 * n + n_out + n_scr:])
        step, total = _ride(ride, rrefs, grid)
        body(*ins, *outs, *own)
        _ride_end(ride, rrefs, step, total)

    outs = pl.pallas_call(
        full, name=name, grid=grid,
        in_specs=list(in_specs) + [ANY] * n, out_specs=list(out_specs) + [ANY] * n,
        out_shape=list(out_shape) + ride.out_shape, scratch_shapes=scratch + ride.scratch,
        compiler_params=_params(("arbitrary",) * len(grid), est),
    )(*args, *ride.operands)
    return list(outs[:n_out]), list(outs[n_out:])


def _ride(ex, refs, grid):
    total = math.prod(grid)
    step = pl.program_id(0)
    for axis in range(1, len(grid)):
        step = step * grid[axis] + pl.program_id(axis)
    pl.when(step == 0)(lambda: ex.start(refs))
    return step, total


def _ride_end(ex, refs, step, total):
    pl.when(step == (3 * total) // 4)(lambda: ex.forward(refs))
    pl.when(step == total - 1)(lambda: ex.finish(refs))


def _class_layout(grads, classes):
    used = [0] * len(set(classes))
    offs = []
    for g, cl in zip(grads, classes):
        offs.append(used[cl])
        used[cl] += g.shape[0] // N_DEV
    return offs, used


def _rs_to_sibling(grads, classes, *, name):
    n = len(grads)
    offs, used = _class_layout(grads, classes)
    heights = used
    ncl = len(heights)
    cols = [next(g.shape[1] for g, cl in zip(grads, classes) if cl == k) for k in range(ncl)]

    def body(*refs):
        gs, land = refs[:n], refs[n:n + ncl]
        send_sems, recv_sems = refs[n + ncl:]
        x, y, c = _place()
        sib = (x, y, 1 - c)
        for p in range(4):
            for w in range(n):
                r = grads[w].shape[0] // N_DEV
                cl = classes[w]
                there = gs[w].at[pl.ds(pl.multiple_of((2 * p + 1 - c) * r, 8), r), :]
                pltpu.make_async_remote_copy(
                    src_ref=there, dst_ref=land[cl].at[p, pl.ds(offs[w], r), :],
                    send_sem=send_sems.at[cl * 4 + p], recv_sem=recv_sems.at[cl * 4 + p],
                    device_id=sib, device_id_type=MESH).start()
        for cl in range(ncl):
            for p in range(4):
                rows_used = land[cl].at[p, pl.ds(0, used[cl]), :]
                slab = pltpu.make_async_remote_copy(
                    src_ref=rows_used, dst_ref=rows_used,
                    send_sem=send_sems.at[cl * 4 + p], recv_sem=recv_sems.at[cl * 4 + p],
                    device_id=sib, device_id_type=MESH)
                slab.wait_send()
                slab.wait_recv()

    return pl.pallas_call(
        body, name=name,
        in_specs=[ANY] * n, out_specs=[ANY] * ncl,
        out_shape=[_out((4, heights[k], cols[k]), BF16) for k in range(ncl)],
        scratch_shapes=[pltpu.SemaphoreType.DMA((4 * ncl,))] * 2,
    )(*[_hbm(g) for g in grads])


def _all_reduce_small(v, *, name):
    R = v.shape[0]

    def body(v_ref, o_ref, buf, send_sems, recv_sems):
        x, y, c = _place()
        me = 4 * x + 2 * y + c
        buf[me] = v_ref[...]
        copies = []
        for k in range(1, N_DEV):
            fx, fy, fc = (k >> 2) & 1, (k >> 1) & 1, k & 1
            to = (x ^ fx, y ^ fy, c ^ fc)
            cp = pltpu.make_async_remote_copy(
                src_ref=v_ref, dst_ref=buf.at[me],
                send_sem=send_sems.at[k - 1], recv_sem=recv_sems.at[k - 1],
                device_id=to, device_id_type=MESH)
            cp.start()
            copies.append(cp)
        for k in range(1, N_DEV):
            fx, fy, fc = (k >> 2) & 1, (k >> 1) & 1, k & 1
            frm = 4 * (x ^ fx) + 2 * (y ^ fy) + (c ^ fc)
            pltpu.make_async_remote_copy(
                src_ref=v_ref, dst_ref=buf.at[frm],
                send_sem=send_sems.at[k - 1], recv_sem=recv_sems.at[k - 1],
                device_id=(x ^ fx, y ^ fy, c ^ fc), device_id_type=MESH).wait_recv()
        for cp in copies:
            cp.wait_send()
        acc = buf[0]
        for d in range(1, N_DEV):
            acc = acc + buf[d]
        o_ref[...] = acc

    vm = pl.BlockSpec(memory_space=pltpu.VMEM)
    return pl.pallas_call(
        body, name=name, in_specs=[vm], out_specs=vm,
        out_shape=jax.ShapeDtypeStruct((R, LANE), F32),
        scratch_shapes=[pltpu.VMEM((N_DEV, R, LANE), F32),
                        pltpu.SemaphoreType.DMA((N_DEV - 1,)), pltpu.SemaphoreType.DMA((N_DEV - 1,))],
        compiler_params=pltpu.CompilerParams(vmem_limit_bytes=VMEM_FLOOR),
    )(v)


def _rope_tables(positions):
    half = QK_ROPE // 2
    inv_freq = ROPE_BASE ** (-jnp.arange(half, dtype=F32) / half)
    ang = positions.astype(F32)[:, None] * inv_freq
    cos, sin = jnp.cos(ang), jnp.sin(ang)
    z = jnp.zeros_like(cos)
    z2 = jnp.zeros((positions.shape[0], LANE - QK_ROPE), F32)
    rc = jnp.concatenate([cos, cos, z2], axis=1)
    rs1 = jnp.concatenate([-sin, z, z2], axis=1)
    rs2 = jnp.concatenate([z, sin, z2], axis=1)
    return rc, rs1, rs2


def _block_diag(pool_w):
    G, pg, _ = pool_w.shape
    out = jnp.zeros((G * pg, G * pg), pool_w.dtype)
    for g in range(G):
        out = lax.dynamic_update_slice(out, pool_w[g], (g * pg, g * pg))
    return out


def kernel(x, mem, positions, ln_g, ln_b, ffn1_w13, ffn1_w2, w_in, pool_w, pool_scale, q_norm_g, w_uq, kv_norm_g, w_ukv, w_out, mem_wq, mem_wkv, mem_wo, ffn2_w13, ffn2_w2, loss_target, m_ln_g, m_ln_b, m_ffn1_w13, m_ffn1_w2, m_w_in, m_pool_w, m_pool_scale, m_q_norm_g, m_w_uq, m_kv_norm_g, m_w_ukv, m_w_out, m_mem_wq, m_mem_wkv, m_mem_wo, m_ffn2_w13, m_ffn2_w2, v_ln_g, v_ln_b, v_ffn1_w13, v_ffn1_w2, v_w_in, v_pool_w, v_pool_scale, v_q_norm_g, v_w_uq, v_kv_norm_g, v_w_ukv, v_w_out, v_mem_wq, v_mem_wkv, v_mem_wo, v_ffn2_w13, v_ffn2_w2):
    L = ln_g.shape[0]
    T, D = x.shape[1], x.shape[2]
    F = ffn1_w2.shape[1] * N_DEV
    PW = D // 4
    H = (D - PW) // V_HEAD
    DIN = w_in.shape[2]
    DINP = PW + Q_LORA + KV_LORA + LANE
    QW = QK_NOPE + QK_ROPE
    alpha = (2 * L) ** 0.25
    x2d = x.reshape(T, D)
    memb = mem.reshape(mem.shape[1], D).astype(BF16)
    target = loss_target.reshape(T, D)
    tabs = _rope_tables(positions.reshape(T))

    def shards_of(l):
        return dict(
            w13a=ffn1_w13[l].T[None].astype(BF16),
            w13b=ffn2_w13[l].T[None].astype(BF16),
            w2a=ffn1_w2[l][None].astype(BF16),
            w2b=ffn2_w2[l][None].astype(BF16),
            wsq=jnp.stack([w_out[l], mem_wq[l], mem_wo[l]]).astype(BF16),
            wkvT=mem_wkv[l].T[None].astype(BF16),
            winp=jnp.pad(w_in[l], ((0, 0), (0, DINP - DIN)))[None].astype(BF16),
            wuqT=w_uq[l].T[None].astype(BF16),
            wukvT=w_ukv[l].T[None].astype(BF16),
        )

    SMALL = ("winp", "wuqT", "wukvT")
    shards = [shards_of(l) for l in range(L)]
    W = [dict() for _ in range(L)]

    def rider(spec):
        return _Gather([shards[l][n] for l, n in spec]) if spec else None

    def arrived(spec, arrays):
        for (l, n), a in zip(spec, arrays):
            if n in ("w13a", "w13b"):
                a = _interleave(a, 1)
            elif n == "wuqT":
                a = jnp.pad(a.reshape(H, QW, Q_LORA), ((0, 0), (0, HEAD_PAD - QW), (0, 0)))
                a = a.reshape(1, H * HEAD_PAD, Q_LORA)
            elif n == "ln":
                a = jnp.moveaxis(a.reshape(N_DEV, 2, L, 4, D // N_DEV), 0, 3).reshape(2, L, 4, D)
                LN["g"], LN["b"] = a[0], a[1]
            W[l][n] = a

    LN = {}
    shards[0]["ln"] = jnp.concatenate([ln_g.reshape(1, 4 * L, -1), ln_b.reshape(1, 4 * L, -1)], axis=1)
    spec0 = [(0, "w13a")]
    arrived(spec0, _exchange_alone(rider(spec0), name="ag_first"))
    wbd = [_block_diag(pool_w[l]).astype(BF16) for l in range(L)]

    def ffn_fwd(l, which, xres, xb, k, spec):
        ab = "ab"[which]
        h13, a, rode = _ffn_up(xb, W[l]["w13" + ab], 0, name=f"l{l}_ffn{which}_up", ride=rider(spec))
        arrived(spec, rode)
        y, xo, xob = _mm_ln(a, W[l]["w2" + ab], 0, xres, LN["g"][l,k:k + 1], LN["b"][l,k:k + 1], alpha=alpha, s=0.5,
                            name=f"l{l}_ffn{which}_y_ln{k}")
        return dict(xres=xres, xb=xb, h13=h13, a=a, y=y), xo, xob

    saved = []
    xres, xb = x2d, x2d.astype(BF16)
    for l in range(L):
        sv = {}
        more = l + 1 < L
        Wl = W[l]
        spec = ([(0, "w2a"), (0, "ln"), *[(0, n) for n in SMALL], (0, "wkvT")] if l == 0
                else [(l, "wsq"), (l, "wkvT")])
        sv["ffn1"], x1, x1b = ffn_fwd(l, 0, xres, xb, 0, spec)
        hin = _mm(x1b, Wl["winp"], lead=0, name=f"l{l}_hin")
        pscale = pool_scale[l].reshape(1, PW)
        gq, gkv = q_norm_g[l].reshape(1, Q_LORA), kv_norm_g[l].reshape(1, KV_LORA)
        cqn, ckvn, kpe = _norms_fwd(hin, gq, gkv, pw=PW, name=f"l{l}_norms")
        qh, kh, vh = _heads_fwd(cqn, ckvn, kpe, tabs, Wl["wuqT"][0], Wl["wukvT"][0], H=H, name=f"l{l}_heads")
        spec = [(l, "w13b"), (l, "w2b")] + ([(0, "wsq")] if l == 0 else []) + ([(l + 1, "w13a")] if more else [])
        cat, lse, rode = _flash_fwd(qh, kh, vh, H=H, pw=PW, name=f"l{l}_flash", ride=rider(spec))
        arrived(spec, rode)
        cat = _pool_fwd(hin, wbd[l], pscale, cat, name=f"l{l}_pool")
        ymix, x2, x2b = _mm_ln(cat, Wl["wsq"], 0, x1, LN["g"][l,1:2], LN["b"][l,1:2], alpha=alpha, s=1.0,
                               name=f"l{l}_ymix_ln1")
        qm = _mm(x2b, Wl["wsq"], lead=1, out_dtype=BF16, name=f"l{l}_qm")
        kvm = _mm(memb, Wl["wkvT"], lead=0, tb=True, name=f"l{l}_kvm")
        km, vm = kvm[:, :D], kvm[:, D:]
        om = _mem_fwd(qm, km, vm, name=f"l{l}_memattn")
        ymem, x3, x3b = _mm_ln(om, Wl["wsq"], 2, x2, LN["g"][l,2:3], LN["b"][l,2:3], alpha=alpha, s=1.0,
                               name=f"l{l}_ymem_ln2")
        spec = [(l + 1, n) for n in ("w2a", *SMALL)] if more else []
        sv["ffn2"], x4, x4b = ffn_fwd(l, 1, x3, x3b, 3, spec)
        sv.update(x1=x1, x1b=x1b, hin=hin, pscale=pscale, gq=gq, gkv=gkv, cqn=cqn, ckvn=ckvn,
                  qh=qh, kh=kh, vh=vh, lse=lse, cat=cat, ymix=ymix, x2=x2, x2b=x2b, qm=qm, km=km, vm=vm,
                  om=om, ymem=ymem)
        saved.append(sv)
        xres, xb = x4, x4b


    gW = {}
    gS = {}

    def ln_of(l, k):
        sv = saved[l]
        x, y, s = {0: (sv["ffn1"]["xres"], sv["ffn1"]["y"], 0.5), 1: (sv["x1"], sv["ymix"], 1.0),
                   2: (sv["x2"], sv["ymem"], 1.0), 3: (sv["ffn2"]["xres"], sv["ffn2"]["y"], 0.5)}[k]
        return x, y, LN["g"][l, k:k + 1], s

    def dx_through_ln(a, b, lead, tb, add, into, name):
        x, y, g, s = ln_of(*into)
        dxres, dyb, dg, db = _mm_ln_bwd(a, b, lead, tb, add, x, y, g, alpha=alpha, s=s, name=name)
        gS[("ln_g", *into)], gS[("ln_b", *into)] = dg, db
        return dxres, dyb

    def ffn_bwd(l, which, sv, dxres, dyb, ride, into):
        tag = f"l{l}_ffn{which}"
        gW[("w2", which, l)] = _mm(sv["a"], dyb, ta=True, out_dtype=BF16, name=f"{tag}_dw2", tn=D)
        dh, rode = _ffn_down_bwd(dyb, W[l]["w2" + "ab"[which]], 0, sv["h13"], name=f"{tag}_dh", ride=ride)
        dw13 = _mm(dh, sv["xb"], ta=True, out_dtype=BF16, name=f"{tag}_dw13", tn=D)
        gW[("w13", which, l)] = _deinterleave(dw13, 0)
        w13 = W[l]["w13" + "ab"[which]]
        if into is not None:
            return dx_through_ln(dh, w13, 0, False, dxres, into, f"{tag}_dx"), rode
        last = rs_first_level(l, "c")
        dxn, got = _mm(dh, w13, lead=0, add=dxres, name=f"{tag}_dx", tn=D, ride=last["ex"])
        rs_last_level(last, got)
        return dxn, rode

    core = lax.axis_index("c").astype(jnp.int32).reshape(1)
    chip = (2 * lax.axis_index("x") + lax.axis_index("y")).astype(jnp.int32).reshape(1)
    gsh = {}

    def rs_first_level(l, group):
        keys, classes = {
            "a": ([("w13", 1, l), ("w2", 1, l), ("mem_wkv", l), ("mem_wq", l), ("mem_wo", l)], [0] * 5),
            "b": ([("w_out", l), ("w_in", l), ("w_uq", l), ("w_ukv", l)], [0, 1, 2, 3]),
            "c": ([("w13", 0, l), ("w2", 0, l)], [0, 0]),
        }[group]
        tag = f"l{l}{group}"
        garrs = []
        for key in keys:
            g = gW[key]
            if key[0] == "w_uq":
                g = g.reshape(H, HEAD_PAD, Q_LORA)[:, :QW, :].reshape(H * QW, Q_LORA)
            garrs.append(g)
        offs, used = _class_layout(garrs, classes)
        parts = list(_rs_to_sibling(garrs, classes, name=f"{tag}_rs_sibling"))
        for cl in range(len(parts)):
            mine = [w for w, c in enumerate(classes) if c == cl]
            parts[cl] = _pair_sum(core, [garrs[w] for w in mine], parts[cl], [offs[w] for w in mine],
                                  name=f"{tag}_rs_pair_sum{cl}")
        return dict(tag=tag, keys=keys, garrs=garrs, classes=classes, offs=offs, used=used, parts=parts,
                    ex=_ChipExchange(parts, used))

    def rs_last_level(st, gathered):
        sums = [_quad_sum(chip, p, a, u, name=f"{st['tag']}_rs_quad_sum{k}")
                for k, (p, a, u) in enumerate(zip(st["parts"], gathered, st["used"]))]
        for key, g, cl, off in zip(st["keys"], st["garrs"], st["classes"], st["offs"]):
            gsh[key] = sums[cl][off:off + g.shape[0] // N_DEV, :]

    top = (L - 1, 3)
    x_top, y_top, g_top, s_top = ln_of(*top)
    dxres, dyb, gS[("ln_g", *top)], gS[("ln_b", *top)], loss_blk = _loss_ln_bwd(
        x_top, y_top, g_top, xres, target, alpha=alpha, s=s_top, name="loss_ln_top_bwd")
    loss = lax.psum(loss_blk[0, 0], ("x", "y", "c"))
    above = None
    for l in reversed(range(L)):
        sv = saved[l]
        Wl = W[l]
        (dxres, dyb), _ = ffn_bwd(l, 1, sv["ffn2"], dxres, dyb, None, (l, 2))
        dom = _mm(dyb, Wl["wsq"], lead=2, tb=True, out_dtype=BF16, name=f"l{l}_dom")
        gW[("mem_wo", l)] = _mm(sv["om"], dyb, ta=True, out_dtype=BF16, name=f"l{l}_dwo", tn=D)
        dqm, dkm, dvm = _mem_bwd(sv["qm"], sv["km"], sv["vm"], dom, name=f"l{l}_memattn_bwd")
        dxres, dyb = dx_through_ln(dqm, Wl["wsq"], 1, True, dxres, (l, 1), f"l{l}_dx2")
        gW[("mem_wq", l)] = _mm(sv["x2b"], dqm, ta=True, out_dtype=BF16, name=f"l{l}_dwq", tn=D)
        dkvm = jnp.concatenate([dkm, dvm], axis=1).astype(BF16)
        gW[("mem_wkv", l)] = _mm(dkvm, memb, ta=True, out_dtype=BF16, name=f"l{l}_dwkv", tn=D)
        dcat = _mm(dyb, Wl["wsq"], lead=0, tb=True, name=f"l{l}_dcat", tn=D)
        gW[("w_out", l)] = _mm(sv["cat"], dyb, ta=True, out_dtype=BF16, name=f"l{l}_dwout", tn=D)
        riding = [rs_first_level(l, "a")] + ([above] if above else [])
        both = _Both([st["ex"] for st in riding])
        dqh, dkh, dvh, rode = _flash_bwd(sv["qh"], sv["kh"], sv["vh"], sv["cat"], dcat, sv["lse"], H=H, pw=PW,
                                         name=f"l{l}_flash_bwd", ride=both)
        for st, got in zip(riding, both.split(rode)):
            rs_last_level(st, got)
        dqraw, dkv, dkpe = _heads_bwd(dqh, dkh, dvh, tabs, H=H, name=f"l{l}_heads_bwd")
        dcq = _mm(dqraw, Wl["wuqT"], lead=0, name=f"l{l}_dcq")
        gW[("w_uq", l)] = _mm(dqraw, sv["cqn"], ta=True, out_dtype=BF16, name=f"l{l}_dwuq")
        dckv = _mm(dkv, Wl["wukvT"], lead=0, name=f"l{l}_dckv")
        gW[("w_ukv", l)] = _mm(dkv, sv["ckvn"], ta=True, out_dtype=BF16, name=f"l{l}_dwukv")
        du, dwbd, dps = _pool_bwd(sv["hin"], dcat, wbd[l], sv["pscale"], name=f"l{l}_pool_bwd")
        dhin, dgq, dgkv = _norms_bwd(sv["hin"], sv["gq"], sv["gkv"], dcq, dckv, dkpe, du, pw=PW,
                                     name=f"l{l}_norms_bwd")
        pg = PW // len(POOL_WINDOWS)
        gS[("pool_w", l)] = jnp.stack([dwbd[g * pg:(g + 1) * pg, g * pg:(g + 1) * pg]
                                       for g in range(len(POOL_WINDOWS))])
        gS[("pool_scale", l)], gS[("q_norm_g", l)], gS[("kv_norm_g", l)] = dps, dgq, dgkv
        dxres, dyb = dx_through_ln(dhin, Wl["winp"], 0, True, dxres, (l, 0), f"l{l}_dx1")
        gW[("w_in", l)] = _mm(sv["x1b"], dhin, ta=True, out_dtype=BF16, name=f"l{l}_dwin", tn=DINP)
        heads = rs_first_level(l, "b")
        below, rode = ffn_bwd(l, 0, sv["ffn1"], dxres, dyb, heads["ex"], (l - 1, 3) if l > 0 else None)
        rs_last_level(heads, rode)
        if l > 0:
            dxres, dyb = below
            above = rs_first_level(l, "c")
    grad_x = below.reshape(1, T, D)

    small_keys = []
    for l in range(L):
        small_keys += [("pool_w", l), ("pool_scale", l), ("q_norm_g", l), ("kv_norm_g", l)]
        small_keys += [("ln_g", l, k) for k in range(4)] + [("ln_b", l, k) for k in range(4)]
    flat = jnp.concatenate([gS[k].reshape(-1) for k in small_keys])
    n_small = flat.shape[0]
    rows = -(-n_small // (8 * LANE)) * 8
    flat = jnp.pad(flat, (0, rows * LANE - n_small)).reshape(rows, LANE)
    red = _all_reduce_small(flat, name="ar_small").reshape(-1)
    gsm, pos = {}, 0
    for k in small_keys:
        size = math.prod(gS[k].shape)
        gsm[k] = red[pos:pos + size].reshape(gS[k].shape)
        pos += size

    me = 4 * lax.axis_index("x") + 2 * lax.axis_index("y") + lax.axis_index("c")
    dsh = D // N_DEV
    stack = lambda f: jnp.stack([f(l) for l in range(L)])
    g_ln_g = stack(lambda l: jnp.concatenate([gsm[("ln_g", l, k)] for k in range(4)], axis=0))
    g_ln_b = stack(lambda l: jnp.concatenate([gsm[("ln_b", l, k)] for k in range(4)], axis=0))
    swapped = {
        "ffn1_w13": stack(lambda l: gsh[("w13", 0, l)]),
        "ffn2_w13": stack(lambda l: gsh[("w13", 1, l)]),
        "w_in": stack(lambda l: gsh[("w_in", l)][:, :DIN].T),
        "w_uq": stack(lambda l: gsh[("w_uq", l)]),
        "w_ukv": stack(lambda l: gsh[("w_ukv", l)]),
    }
    swap = lambda a: jnp.swapaxes(a, 1, 2)
    grads = {
        "ln_g": lax.dynamic_slice_in_dim(g_ln_g, me * dsh, dsh, axis=2),
        "ln_b": lax.dynamic_slice_in_dim(g_ln_b, me * dsh, dsh, axis=2),
        "ffn1_w2": stack(lambda l: gsh[("w2", 0, l)]),
        "pool_w": stack(lambda l: gsm[("pool_w", l)]),
        "pool_scale": stack(lambda l: gsm[("pool_scale", l)].reshape(PW)),
        "q_norm_g": stack(lambda l: gsm[("q_norm_g", l)].reshape(Q_LORA)),
        "kv_norm_g": stack(lambda l: gsm[("kv_norm_g", l)].reshape(KV_LORA)),
        "w_out": stack(lambda l: gsh[("w_out", l)]),
        "mem_wq": stack(lambda l: gsh[("mem_wq", l)]),
        "mem_wkv": stack(lambda l: gsh[("mem_wkv", l)].T),
        "mem_wo": stack(lambda l: gsh[("mem_wo", l)]),
        "ffn2_w2": stack(lambda l: gsh[("w2", 1, l)]),
        **{nme: swap(g) for nme, g in swapped.items()},
    }

    names = ["ln_g", "ln_b", "ffn1_w13", "ffn1_w2", "w_in", "pool_w", "pool_scale", "q_norm_g", "w_uq",
             "kv_norm_g", "w_ukv", "w_out", "mem_wq", "mem_wkv", "mem_wo", "ffn2_w13", "ffn2_w2"]
    weights = dict(ln_g=ln_g, ln_b=ln_b, ffn1_w13=ffn1_w13, ffn1_w2=ffn1_w2, w_in=w_in, pool_w=pool_w,
                   pool_scale=pool_scale, q_norm_g=q_norm_g, w_uq=w_uq, kv_norm_g=kv_norm_g, w_ukv=w_ukv,
                   w_out=w_out, mem_wq=mem_wq, mem_wkv=mem_wkv, mem_wo=mem_wo, ffn2_w13=ffn2_w13,
                   ffn2_w2=ffn2_w2)
    ms = dict(ln_g=m_ln_g, ln_b=m_ln_b, ffn1_w13=m_ffn1_w13, ffn1_w2=m_ffn1_w2, w_in=m_w_in, pool_w=m_pool_w,
              pool_scale=m_pool_scale, q_norm_g=m_q_norm_g, w_uq=m_w_uq, kv_norm_g=m_kv_norm_g,
              w_ukv=m_w_ukv, w_out=m_w_out, mem_wq=m_mem_wq, mem_wkv=m_mem_wkv, mem_wo=m_mem_wo,
              ffn2_w13=m_ffn2_w13, ffn2_w2=m_ffn2_w2)
    vs = dict(ln_g=v_ln_g, ln_b=v_ln_b, ffn1_w13=v_ffn1_w13, ffn1_w2=v_ffn1_w2, w_in=v_w_in, pool_w=v_pool_w,
              pool_scale=v_pool_scale, q_norm_g=v_q_norm_g, w_uq=v_w_uq, kv_norm_g=v_kv_norm_g,
              w_ukv=v_w_ukv, w_out=v_w_out, mem_wq=v_mem_wq, mem_wkv=v_mem_wkv, mem_wo=v_mem_wo,
              ffn2_w13=v_ffn2_w13, ffn2_w2=v_ffn2_w2)
    deltas, new_m, new_v = [], [], []
    for nme in names:
        if nme in swapped:
            d, mn, vn = [swap(o) for o in _adamw(swap(weights[nme]), swapped[nme], swap(ms[nme]), swap(vs[nme]),
                                                 name=f"adamw_{nme}")]
        else:
            d, mn, vn = _adamw(weights[nme], grads[nme], ms[nme], vs[nme], name=f"adamw_{nme}")
        deltas.append(d)
        new_m.append(mn)
        new_v.append(vn)
    return (loss, grad_x, *[grads[nme] for nme in names], *deltas, *new_m, *new_v)
```

```python
import functools
import math

import jax
import jax.numpy as jnp
from jax import lax
from jax.experimental import pallas as pl
from jax.experimental.pallas import tpu as pltpu

F32 = jnp.float32
BF16 = jnp.bfloat16
MESH = pl.DeviceIdType.MESH

CHUNK = 64
MEM_HEADS = 4
POOL_WINDOWS = (2, 4, 8, 16)
QK_NOPE = 128
QK_ROPE = 64
V_HEAD = 128
Q_LORA = 256
KV_LORA = 128
ROPE_BASE = 10000.0
LN_EPS = 1e-5
RMS_EPS = 1e-6
NEG_INF = -1e30
ADAM_LR = 0.001
ADAM_B1 = 0.9
ADAM_B2 = 0.999
ADAM_EPS = 1e-08
ADAM_WD = 0.01
ADAM_STEP = 10

N_DEV = 8
LANE = 128
HEAD_PAD = 2 * LANE
POOL_HALO = 16
VMEM_CAP = 56 * 1024 * 1024
VMEM_FLOOR = 32 * 1024 * 1024


def _tile(n, pref, mult):
    t = (min(pref, n) // mult) * mult
    while t >= mult:
        if n % t == 0:
            return t
        t -= mult
    return n


def _params(sem, est_bytes):
    limit = int(min(max(2 * est_bytes + (8 << 20), VMEM_FLOOR), VMEM_CAP))
    return pltpu.CompilerParams(dimension_semantics=sem, vmem_limit_bytes=limit)


def _nbytes(shape, dtype):
    return math.prod(shape) * jnp.dtype(dtype).itemsize


def _hbm(x):
    return pltpu.with_memory_space_constraint(x, pltpu.HBM)


def _out(shape, dtype):
    return pltpu.HBM(tuple(shape), dtype)


def _dg(a, b, ca, cb):
    return lax.dot_general(a.astype(BF16), b.astype(BF16), (((ca,), (cb,)), ((), ())),
                           preferred_element_type=F32)


@jax.custom_vjp
def _bdot_nn(a, b):
    return _dg(a, b, 1, 0)


def _bdot_nn_fwd(a, b):
    return _dg(a, b, 1, 0), (a, b)


def _bdot_nn_bwd(res, ct):
    a, b = res
    return _dg(ct, b, 1, 1).astype(a.dtype), _dg(a, ct, 0, 0).astype(b.dtype)


_bdot_nn.defvjp(_bdot_nn_fwd, _bdot_nn_bwd)


@jax.custom_vjp
def _bdot_nt(a, b):
    return _dg(a, b, 1, 1)


def _bdot_nt_fwd(a, b):
    return _dg(a, b, 1, 1), (a, b)


def _bdot_nt_bwd(res, ct):
    a, b = res
    return _dg(ct, b, 1, 0).astype(a.dtype), _dg(ct, a, 0, 0).astype(b.dtype)


_bdot_nt.defvjp(_bdot_nt_fwd, _bdot_nt_bwd)


@functools.partial(jax.custom_vjp, nondiff_argnums=(1,))
def _lane_roll(x, shift):
    return pltpu.roll(x, shift % x.shape[1], axis=1)


def _lane_roll_fwd(x, shift):
    return _lane_roll(x, shift), None


def _lane_roll_bwd(shift, _, ct):
    return (_lane_roll(ct, -shift),)


_lane_roll.defvjp(_lane_roll_fwd, _lane_roll_bwd)


@functools.partial(jax.custom_vjp, nondiff_argnums=(1, 2))
def _cols(x, lo, hi):
    return x[:, lo:hi]


def _cols_fwd(x, lo, hi):
    return x[:, lo:hi], x.shape[1]


def _cols_bwd(lo, hi, width, ct):
    parts = []
    if lo > 0:
        parts.append(jnp.zeros((ct.shape[0], lo), ct.dtype))
    parts.append(ct)
    if hi < width:
        parts.append(jnp.zeros((ct.shape[0], width - hi), ct.dtype))
    return (jnp.concatenate(parts, axis=1) if len(parts) > 1 else ct,)


_cols.defvjp(_cols_fwd, _cols_bwd)


MM_VMEM_BUDGET = 22 * 1024 * 1024


def _mm(a, b, *, name, ta=False, tb=False, out_dtype=F32, lead=None, add=None, add_scale=1.0,
        tm=512, tn=1024, tk=8192, ride=None):
    if ta:
        K, M = a.shape
    else:
        M, K = a.shape
    bshape = b.shape[1:] if lead is not None else b.shape
    if tb:
        N, Kb = bshape
    else:
        Kb, N = bshape
    assert K == Kb, (name, a.shape, b.shape)

    def blocks(tm, tn, tk):
        tm = _tile(M, tm, LANE if ta else 16)
        tn = _tile(N, tn, LANE)
        tk = _tile(K, tk, LANE)
        nbytes = (tm * tk * a.dtype.itemsize + tk * tn * b.dtype.itemsize
                  + tm * tn * (jnp.dtype(out_dtype).itemsize + (4 if K // tk > 1 else 0)
                               + (add.dtype.itemsize if add is not None else 0)))
        return tm, tn, tk, nbytes

    tm, tn, tk, est = blocks(tm, tn, tk)
    for shrink in ("m", "k", "m", "k", "n"):
        if est <= MM_VMEM_BUDGET:
            break
        if shrink == "m":
            tm, tn, tk, est = blocks(max(tm // 2, LANE), tn, tk)
        elif shrink == "k":
            tm, tn, tk, est = blocks(tm, tn, max(tk // 2, LANE))
        else:
            tm, tn, tk, est = blocks(tm, max(tn // 2, LANE), tk)
    nk = K // tk
    ca = 0 if ta else 1
    cb = 1 if tb else 0

    def body(*refs):
        a_ref, b_ref = refs[0], refs[1]
        add_ref = refs[2] if add is not None else None
        o_ref = refs[3] if add is not None else refs[2]

        def finish(r):
            if add_ref is not None:
                r = r + add_scale * add_ref[...].astype(F32)
            o_ref[...] = r.astype(o_ref.dtype)

        if nk == 1:
            finish(_dg(a_ref[...], b_ref[...], ca, cb))
            return
        acc_ref = refs[-1]
        k = pl.program_id(2)

        @pl.when(k == 0)
        def _():
            acc_ref[...] = jnp.zeros_like(acc_ref)

        acc_ref[...] += _dg(a_ref[...], b_ref[...], ca, cb)

        @pl.when(k == nk - 1)
        def _():
            finish(acc_ref[...])

    a_blk = (tk, tm) if ta else (tm, tk)
    a_map = (lambda i, j, k: (k, i)) if ta else (lambda i, j, k: (i, k))
    b_blk = (tn, tk) if tb else (tk, tn)
    if lead is None:
        b_map = (lambda i, j, k: (j, k)) if tb else (lambda i, j, k: (k, j))
        b_spec = pl.BlockSpec(b_blk, b_map)
    else:
        b_map = (lambda i, j, k: (lead, j, k)) if tb else (lambda i, j, k: (lead, k, j))
        b_spec = pl.BlockSpec((None,) + b_blk, b_map)
    in_specs = [pl.BlockSpec(a_blk, a_map), b_spec]
    args = [a, b]
    if add is not None:
        in_specs.append(pl.BlockSpec((tm, tn), lambda i, j, k: (i, j)))
        args.append(add)
    (out,), rode = _host_call(
        body, name=name,
        grid=(M // tm, N // tn, nk),
        in_specs=in_specs,
        out_specs=[pl.BlockSpec((tm, tn), lambda i, j, k: (i, j))],
        out_shape=[_out((M, N), out_dtype)],
        scratch=[pltpu.VMEM((tm, tn), F32)] if nk > 1 else [],
        args=[_hbm(v) for v in args], sem=("parallel", "parallel", "arbitrary"), est=est + tm * tn * 4,
        ride=ride)
    return out if ride is None else (out, rode)


def _rowwise(fn, tiles, params, tile_outs, acc_outs=(), *, tm, name):
    tile_arrays, tile_specs = [], []
    for t in tiles:
        if isinstance(t, tuple):
            tile_arrays.append(t[0])
            tile_specs.append(t[1])
        else:
            tile_arrays.append(t)
            tile_specs.append(pl.BlockSpec((tm, t.shape[1]), lambda i: (i, 0)))
    T = tile_arrays[0].shape[0]
    nt, np_, nto, nao = len(tile_arrays), len(params), len(tile_outs), len(acc_outs)

    def body(*refs):
        i = pl.program_id(0)
        tvals = [r[...] for r in refs[:nt]]
        pvals = [r[...] for r in refs[nt:nt + np_]]
        to_refs = refs[nt + np_:nt + np_ + nto]
        ao_refs = refs[nt + np_ + nto:]
        touts, aouts = fn(i, tvals, pvals)
        for r, v in zip(to_refs, touts):
            r[...] = v.astype(r.dtype)
        if nao:
            @pl.when(i == 0)
            def _():
                for r in ao_refs:
                    r[...] = jnp.zeros_like(r)
            for r, v in zip(ao_refs, aouts):
                r[...] += v.astype(r.dtype)

    in_specs = tile_specs + [pl.BlockSpec(p.shape, lambda i: (0, 0)) for p in params]
    out_specs = [pl.BlockSpec((tm, c), lambda i: (i, 0)) for c, _ in tile_outs]
    out_specs += [pl.BlockSpec(s, lambda i: (0, 0)) for s, _ in acc_outs]
    out_shape = [_out((T, c), d) for c, d in tile_outs]
    out_shape += [_out(s, d) for s, d in acc_outs]
    width = sum(s.block_shape[-1] for s in tile_specs) + sum(c for c, _ in tile_outs)
    est = 6 * tm * width * 4 + sum(_nbytes(p.shape, F32) for p in params) * 4
    return pl.pallas_call(
        body, name=name, grid=(T // tm,),
        in_specs=in_specs, out_specs=out_specs, out_shape=out_shape,
        compiler_params=_params(("arbitrary",) if nao else ("parallel",), est),
    )(*[_hbm(v) for v in tile_arrays], *[_hbm(p) for p in params])


def _ln_fn(alpha, s, xres, y, g, b):
    z = alpha * xres.astype(F32) + s * y.astype(F32)
    mu = jnp.mean(z, axis=-1, keepdims=True)
    zc = z - mu
    var = jnp.mean(zc * zc, axis=-1, keepdims=True)
    return zc * lax.rsqrt(var + LN_EPS) * g + b


def _mm_ln(a, b, lead, xres, g, bias, *, alpha, s, name):
    M, K = a.shape
    N = b.shape[2]
    tm = _tile(M, 512, 16)

    def body(a_ref, b_ref, x_ref, g_ref, bias_ref, y_ref, xo_ref, xb_ref):
        y = _dg(a_ref[...], b_ref[...], 1, 0)
        y_ref[...] = y.astype(y_ref.dtype)
        out = _ln_fn(alpha, s, x_ref[...], y, g_ref[...], bias_ref[...])
        xo_ref[...] = out
        xb_ref[...] = out.astype(BF16)

    row = pl.BlockSpec((tm, N), lambda i: (i, 0))
    vec = pl.BlockSpec((1, N), lambda i: (0, 0))
    est = tm * K * 2 + K * N * 2 + tm * N * (4 + 4 + 4 + 2 + 8)
    return pl.pallas_call(
        body, name=name, grid=(M // tm,),
        in_specs=[pl.BlockSpec((tm, K), lambda i: (i, 0)), pl.BlockSpec((None, K, N), lambda i: (lead, 0, 0)),
                  row, vec, vec],
        out_specs=[row, row, row],
        out_shape=[_out((M, N), BF16), _out((M, N), F32), _out((M, N), BF16)],
        compiler_params=_params(("parallel",), est),
    )(_hbm(a), _hbm(b), _hbm(xres), _hbm(g), _hbm(bias))


def _ln_bwd_math(alpha, s, x, y, g, d):
    z = alpha * x + s * y.astype(F32)
    zc = z - jnp.mean(z, axis=-1, keepdims=True)
    r = lax.rsqrt(jnp.mean(zc * zc, axis=-1, keepdims=True) + LN_EPS)
    xh = zc * r
    dxh = d * g
    dz = r * (dxh - jnp.mean(dxh, axis=-1, keepdims=True) - xh * jnp.mean(dxh * xh, axis=-1, keepdims=True))
    return alpha * dz, s * dz, jnp.sum(d * xh, axis=0, keepdims=True), jnp.sum(d, axis=0, keepdims=True)


def _mm_ln_bwd(a, b, lead, tb, add, xres, y, g, *, alpha, s, name):
    M, K = a.shape
    N = b.shape[1] if tb else b.shape[2]
    tk = K if K * N * 2 <= MM_VMEM_BUDGET * 3 // 5 else _tile(K, 2816, LANE)
    tm = _tile(M, 256, 16)
    nk = K // tk
    cb = 1 if tb else 0

    def body(a_ref, b_ref, add_ref, x_ref, y_ref, g_ref, dx_ref, dy_ref, dg_ref, db_ref, *scratch):
        i, k = pl.program_id(0), pl.program_id(1)

        def finish(d):
            @pl.when(i == 0)
            def _():
                dg_ref[...] = jnp.zeros_like(dg_ref)
                db_ref[...] = jnp.zeros_like(db_ref)

            dx, dy, dg, db = _ln_bwd_math(alpha, s, x_ref[...], y_ref[...], g_ref[...], d + add_ref[...])
            dx_ref[...] = dx
            dy_ref[...] = dy.astype(dy_ref.dtype)
            dg_ref[...] += dg
            db_ref[...] += db

        if nk == 1:
            finish(_dg(a_ref[...], b_ref[...], 1, cb))
            return
        acc_ref = scratch[0]

        @pl.when(k == 0)
        def _():
            acc_ref[...] = jnp.zeros_like(acc_ref)

        acc_ref[...] += _dg(a_ref[...], b_ref[...], 1, cb)

        @pl.when(k == nk - 1)
        def _():
            finish(acc_ref[...])

    row = pl.BlockSpec((tm, N), lambda i, k: (i, 0))
    vec = pl.BlockSpec((1, N), lambda i, k: (0, 0))
    b_spec = (pl.BlockSpec((None, N, tk), lambda i, k: (lead, 0, k)) if tb
              else pl.BlockSpec((None, tk, N), lambda i, k: (lead, k, 0)))
    est = tm * tk * 2 + tk * N * 2 + tm * N * (4 + 4 + 2 + 4 + 2 + 4 + 12)
    return pl.pallas_call(
        body, name=name, grid=(M // tm, nk),
        in_specs=[pl.BlockSpec((tm, tk), lambda i, k: (i, k)), b_spec, row, row, row, vec],
        out_specs=[row, row, vec, vec],
        out_shape=[_out((M, N), F32), _out((M, N), BF16), _out((1, N), F32), _out((1, N), F32)],
        scratch_shapes=[pltpu.VMEM((tm, N), F32)] if nk > 1 else [],
        compiler_params=_params(("arbitrary", "arbitrary"), est),
    )(_hbm(a), _hbm(b), _hbm(add), _hbm(xres), _hbm(y), _hbm(g))


def _loss_ln_bwd(xres, y, g, out, target, *, alpha, s, name):
    T, D = xres.shape
    tm = _tile(T, 256, 16)

    def body(x_ref, y_ref, o_ref, t_ref, g_ref, dx_ref, dy_ref, dg_ref, db_ref, loss_ref):
        @pl.when(pl.program_id(0) == 0)
        def _():
            dg_ref[...] = jnp.zeros_like(dg_ref)
            db_ref[...] = jnp.zeros_like(db_ref)
            loss_ref[...] = jnp.zeros_like(loss_ref)

        err = o_ref[...] - t_ref[...]
        part = 0.5 * jnp.sum(jnp.sum(err * err, axis=1, keepdims=True) / D, axis=0, keepdims=True)
        loss_ref[...] += jnp.broadcast_to(part, loss_ref.shape)
        dx, dy, dg, db = _ln_bwd_math(alpha, s, x_ref[...], y_ref[...], g_ref[...], err / D)
        dx_ref[...] = dx
        dy_ref[...] = dy.astype(dy_ref.dtype)
        dg_ref[...] += dg
        db_ref[...] += db

    row = pl.BlockSpec((tm, D), lambda i: (i, 0))
    vec = pl.BlockSpec((1, D), lambda i: (0, 0))
    return pl.pallas_call(
        body, name=name, grid=(T // tm,),
        in_specs=[row, row, row, row, vec],
        out_specs=[row, row, vec, vec, pl.BlockSpec((8, LANE), lambda i: (0, 0))],
        out_shape=[_out((T, D), F32), _out((T, D), BF16), _out((1, D), F32), _out((1, D), F32),
                   _out((8, LANE), F32)],
        compiler_params=_params(("arbitrary",), 14 * tm * D * 4),
    )(_hbm(xres), _hbm(y), _hbm(out), _hbm(target), _hbm(g))


FFN_TILE = 256


def _interleave(w, axis):
    n = w.shape[axis] // (2 * FFN_TILE)
    shp = w.shape[:axis] + (2, n, FFN_TILE) + w.shape[axis + 1:]
    return jnp.swapaxes(w.reshape(shp), axis, axis + 1).reshape(w.shape)


def _deinterleave(w, axis):
    n = w.shape[axis] // (2 * FFN_TILE)
    shp = w.shape[:axis] + (n, 2, FFN_TILE) + w.shape[axis + 1:]
    return jnp.swapaxes(w.reshape(shp), axis, axis + 1).reshape(w.shape)


def _ffn_up(xb, w13t, lead, *, name, ride=None):
    T, D = xb.shape
    F = w13t.shape[1] // 2
    tc = FFN_TILE
    tm = _tile(T, 2048, 16)

    def body(x_ref, w_ref, h_ref, a_ref):
        h = _dg(x_ref[...], w_ref[...], 1, 1)
        g, u = h[:, :tc], h[:, tc:]
        h_ref[...] = h.astype(h_ref.dtype)
        a_ref[...] = (g * jax.nn.sigmoid(g) * u).astype(a_ref.dtype)

    est = (tm * D + 2 * tc * D + 3 * tm * tc) * 2 + 3 * tm * tc * 4
    (h13, a), gathered = _host_call(
        body, name=name, grid=(T // tm, F // tc),
        in_specs=[pl.BlockSpec((tm, D), lambda i, j: (i, 0)),
                  pl.BlockSpec((None, 2 * tc, D), lambda i, j: (lead, j, 0))],
        out_specs=[pl.BlockSpec((tm, 2 * tc), lambda i, j: (i, j)),
                   pl.BlockSpec((tm, tc), lambda i, j: (i, j))],
        out_shape=[_out((T, 2 * F), BF16), _out((T, F), BF16)],
        args=[_hbm(xb), _hbm(w13t)], sem=("parallel", "parallel"), est=est, ride=ride)
    return h13, a, gathered


def _ffn_down_bwd(dyb, w2, lead, h13, *, name, ride=None):
    T, D = dyb.shape
    F = w2.shape[1]
    tc = FFN_TILE
    tm = _tile(T, 2048, 16)

    def body(dy_ref, w_ref, h_ref, dh_ref):
        d = _dg(dy_ref[...], w_ref[...], 1, 1)
        h = h_ref[...].astype(F32)
        g, u = h[:, :tc], h[:, tc:]
        sig = jax.nn.sigmoid(g)
        gs = g * sig
        dh_ref[...] = jnp.concatenate([d * u * (sig + gs * (1.0 - sig)), d * gs], axis=1).astype(dh_ref.dtype)

    est = (tm * D + tc * D + 4 * tm * tc) * 2 + 6 * tm * tc * 4
    (dh,), rode = _host_call(
        body, name=name, grid=(T // tm, F // tc),
        in_specs=[pl.BlockSpec((tm, D), lambda i, j: (i, 0)),
                  pl.BlockSpec((None, tc, D), lambda i, j: (lead, j, 0)),
                  pl.BlockSpec((tm, 2 * tc), lambda i, j: (i, j))],
        out_specs=[pl.BlockSpec((tm, 2 * tc), lambda i, j: (i, j))],
        out_shape=[_out((T, 2 * F), BF16)],
        args=[_hbm(dyb), _hbm(w2), _hbm(h13)], sem=("parallel", "parallel"), est=est, ride=ride)
    return dh, rode


def _pool_select(parts, pw):
    pg = pw // len(POOL_WINDOWS)
    grp = lax.broadcasted_iota(jnp.int32, parts[0].shape, 1) // pg
    out = parts[3]
    for g in (2, 1, 0):
        out = jnp.where(grp == g, parts[g], out)
    return out


def _pool_count(t0, rows, pw):
    pg = pw // len(POOL_WINDOWS)
    grp = lax.broadcasted_iota(jnp.int32, (rows, pw), 1) // pg
    win = jnp.where(grp == 0, POOL_WINDOWS[0],
                    jnp.where(grp == 1, POOL_WINDOWS[1],
                              jnp.where(grp == 2, POOL_WINDOWS[2], POOL_WINDOWS[3])))
    t = t0 + lax.broadcasted_iota(jnp.int32, (rows, pw), 0)
    return jnp.minimum(t + 1, win).astype(F32)


def _window_sums(ext, up):
    n = ext.shape[0]
    sums, cur, k = [], ext, 1
    for _ in POOL_WINDOWS:
        cur = cur + pltpu.roll(cur, (n - k) if up else k, axis=0)
        sums.append(cur)
        k *= 2
    return sums


def _pool_delta(u, halo, t0):
    tm, pw = u.shape
    ext = jnp.concatenate([halo, u], axis=0)
    sums = [s[POOL_HALO:, :] for s in _window_sums(ext, up=False)]
    return _pool_select(sums, pw) / _pool_count(t0, tm, pw) - u


def _pool_fwd(hin, wbd, scale, cat, *, name):
    T = hin.shape[0]
    pw = wbd.shape[0]
    tm = _tile(T, 256, POOL_HALO)
    per = tm // POOL_HALO

    def body(u_ref, halo_ref, w_ref, s_ref, cat_ref, y_ref):
        i = pl.program_id(0)
        halo = jnp.where(i > 0, halo_ref[...], 0.0)
        d = _pool_delta(u_ref[...], halo, i * tm)
        y_ref[...] = (_dg(d, w_ref[...], 1, 0) * s_ref[...]).astype(y_ref.dtype)

    return pl.pallas_call(
        body, name=name, grid=(T // tm,),
        in_specs=[pl.BlockSpec((tm, pw), lambda i: (i, 0)),
                  pl.BlockSpec((POOL_HALO, pw), lambda i: (jnp.maximum(i * per - 1, 0), 0)),
                  pl.BlockSpec((pw, pw), lambda i: (0, 0)),
                  pl.BlockSpec((1, pw), lambda i: (0, 0)),
                  ANY],
        out_specs=pl.BlockSpec((tm, pw), lambda i: (i, 0)),
        out_shape=_out(cat.shape, cat.dtype),
        input_output_aliases={4: 0},
        compiler_params=_params(("parallel",), 16 * tm * pw * 4),
    )(_hbm(hin), _hbm(hin), _hbm(wbd), _hbm(scale), _hbm(cat))


def _pool_bwd(hin, dcat, wbd, scale, *, name):
    T = hin.shape[0]
    pw = wbd.shape[0]
    tm = _tile(T, 256, POOL_HALO)
    per = tm // POOL_HALO
    nt = T // tm

    def body(u_ref, halo_ref, dy_ref, dyn_ref, w_ref, s_ref, du_ref, dw_ref, ds_ref):
        i = pl.program_id(0)

        @pl.when(i == 0)
        def _():
            dw_ref[...] = jnp.zeros_like(dw_ref)
            ds_ref[...] = jnp.zeros_like(ds_ref)

        halo = jnp.where(i > 0, halo_ref[...], 0.0)
        d = _pool_delta(u_ref[...], halo, i * tm)
        w = w_ref[...]
        sc = s_ref[...]
        dy = dy_ref[...]
        dyn = jnp.where(i < nt - 1, dyn_ref[...], 0.0)
        ds_ref[...] += jnp.sum(dy * _dg(d, w, 1, 0), axis=0, keepdims=True)
        dys = dy * sc
        dw_ref[...] += _dg(d, dys, 0, 0)
        dys_ext = jnp.concatenate([dys, dyn * sc], axis=0)
        dd_ext = _dg(dys_ext, w, 1, 1)
        ddp = dd_ext / _pool_count(i * tm, tm + POOL_HALO, pw)
        sums = [s[:tm, :] for s in _window_sums(ddp, up=True)]
        du_ref[...] = _pool_select(sums, pw) - dd_ext[:tm, :]

    return pl.pallas_call(
        body, name=name, grid=(nt,),
        in_specs=[pl.BlockSpec((tm, pw), lambda i: (i, 0)),
                  pl.BlockSpec((POOL_HALO, pw), lambda i: (jnp.maximum(i * per - 1, 0), 0)),
                  pl.BlockSpec((tm, pw), lambda i: (i, 0)),
                  pl.BlockSpec((POOL_HALO, pw), lambda i: (jnp.minimum((i + 1) * per, nt * per - 1), 0)),
                  pl.BlockSpec((pw, pw), lambda i: (0, 0)),
                  pl.BlockSpec((1, pw), lambda i: (0, 0))],
        out_specs=[pl.BlockSpec((tm, pw), lambda i: (i, 0)),
                   pl.BlockSpec((pw, pw), lambda i: (0, 0)),
                   pl.BlockSpec((1, pw), lambda i: (0, 0))],
        out_shape=[_out((T, pw), F32),
                   _out((pw, pw), F32),
                   _out((1, pw), F32)],
        compiler_params=_params(("arbitrary",), 24 * tm * pw * 4),
    )(_hbm(hin), _hbm(hin), _hbm(dcat), _hbm(dcat), _hbm(wbd), _hbm(scale))


def _rms(x, g):
    return x * lax.rsqrt(jnp.mean(x * x, axis=-1, keepdims=True) + RMS_EPS) * g


def _norms_fn(pw, h, gq, gkv):
    o1 = pw + Q_LORA
    o2 = o1 + KV_LORA
    return (_rms(_cols(h, pw, o1), gq), _rms(_cols(h, o1, o2), gkv), _cols(h, o2, h.shape[1]))


def _norms_fwd(hin, gq, gkv, *, pw, name):
    tm = _tile(hin.shape[0], 256, 16)

    def fn(i, tv, pv):
        return _norms_fn(pw, tv[0], pv[0], pv[1]), ()

    return _rowwise(fn, [hin], [gq, gkv], [(Q_LORA, BF16), (KV_LORA, BF16), (LANE, F32)], tm=tm, name=name)


def _norms_bwd(hin, gq, gkv, dcq, dckv, dkpe, du, *, pw, name):
    tm = _tile(hin.shape[0], 256, 16)
    dinp = hin.shape[1]

    def fn(i, tv, pv):
        _, vjp = jax.vjp(functools.partial(_norms_fn, pw), tv[0], pv[0], pv[1])
        dh, dgq, dgkv = vjp((tv[1].astype(F32), tv[2].astype(F32), tv[3].astype(F32)))
        dh = jnp.concatenate([tv[4], dh[:, pw:]], axis=1)
        return (dh,), (dgq, dgkv)

    return _rowwise(fn, [hin, dcq, dckv, dkpe, du], [gq, gkv], [(dinp, BF16)],
                    [((1, Q_LORA), F32), ((1, KV_LORA), F32)], tm=tm, name=name)


def _heads_fn(H, qraw, kv, kpe, rc, rs1, rs2):
    half = QK_ROPE // 2
    scale = (QK_NOPE + QK_ROPE) ** -0.5

    def rope(blk):
        return blk * rc + _lane_roll(blk, -half) * rs1 + _lane_roll(blk, half) * rs2

    krot = rope(kpe)
    qs, ks, vs = [], [], []
    for h in range(H):
        lo = h * HEAD_PAD
        qs += [_cols(qraw, lo, lo + LANE) * scale, rope(_cols(qraw, lo + LANE, lo + HEAD_PAD)) * scale]
        ks += [_cols(kv, lo, lo + LANE), krot]
        vs += [_cols(kv, lo + LANE, lo + HEAD_PAD)]
    return jnp.concatenate(qs, axis=1), jnp.concatenate(ks, axis=1), jnp.concatenate(vs, axis=1)


def _heads_fwd(cqn, ckvn, kpe, tabs, wuq, wukv, *, H, name):
    tm = _tile(cqn.shape[0], 256, 16)

    def fn(i, tv, pv):
        qraw = _dg(tv[0], pv[0], 1, 1)
        kv = _dg(tv[1], pv[1], 1, 1)
        return _heads_fn(H, qraw, kv, *tv[2:]), ()

    return _rowwise(fn, [cqn, ckvn, kpe, *tabs], [wuq, wukv],
                    [(H * HEAD_PAD, BF16), (H * HEAD_PAD, BF16), (H * V_HEAD, BF16)], tm=tm, name=name)


def _heads_bwd(dq, dk, dv, tabs, *, H, name):
    tm = _tile(dq.shape[0], 256, 16)

    def fn(i, tv, pv):
        z = jnp.zeros((tm, H * HEAD_PAD), F32)
        zk = jnp.zeros((tm, LANE), F32)
        rc, rs1, rs2 = tv[3], tv[4], tv[5]
        _, vjp = jax.vjp(lambda a, b, c: _heads_fn(H, a, b, c, rc, rs1, rs2), z, z, zk)
        return vjp((tv[0].astype(F32), tv[1].astype(F32), tv[2].astype(F32))), ()

    return _rowwise(fn, [dq, dk, dv, *tabs], [],
                    [(H * HEAD_PAD, BF16), (H * HEAD_PAD, BF16), (LANE, F32)], tm=tm, name=name)


def _diag_mask(rows, cols, row0):
    r = (row0 + lax.broadcasted_iota(jnp.int32, (rows, cols), 0)) // CHUNK
    c = lax.broadcasted_iota(jnp.int32, (rows, cols), 1) // CHUNK
    return r >= c


def _flash_fwd(qh, kh, vh, *, H, pw, name, ride=None):
    T = qh.shape[0]
    t = _tile(T, 512, CHUNK)
    off = pw // V_HEAD


    def body(q_ref, k_ref, v_ref, o_ref, lse_ref):
        i = pl.program_id(1)
        q = q_ref[...]

        def blk(j, carry, masked):
            m, l, acc = carry
            rows = pl.ds(pl.multiple_of(j * t, t), t)
            s = _dg(q, k_ref[rows, :], 1, 1)
            if masked:
                s = jnp.where(_diag_mask(t, t, 0), s, NEG_INF)
            mn = jnp.maximum(m, jnp.max(s, axis=1, keepdims=True))
            p = jnp.exp(s - mn)
            corr = jnp.exp(m - mn)
            l = corr * l + jnp.sum(p, axis=1, keepdims=True)
            acc = corr * acc + _dg(p, v_ref[rows, :], 1, 0)
            return mn, l, acc

        init = (jnp.full((t, 1), NEG_INF, F32), jnp.zeros((t, 1), F32), jnp.zeros((t, V_HEAD), F32))
        carry = lax.fori_loop(0, i, lambda j, c: blk(j, c, False), init)
        m, l, acc = blk(i, carry, True)
        o_ref[...] = (acc / l).astype(o_ref.dtype)
        lse_ref[...] = jnp.broadcast_to(m + jnp.log(l), (t, V_HEAD))

    est = 2 * T * (HEAD_PAD + V_HEAD) * 2 + 8 * t * t * 4
    (o, lse), gathered = _host_call(
        body, name=name, grid=(H, T // t),
        in_specs=[pl.BlockSpec((t, HEAD_PAD), lambda h, i: (i, h)),
                  pl.BlockSpec((T, HEAD_PAD), lambda h, i: (0, h)),
                  pl.BlockSpec((T, V_HEAD), lambda h, i: (0, h))],
        out_specs=[pl.BlockSpec((t, V_HEAD), lambda h, i: (i, off + h)),
                   pl.BlockSpec((t, V_HEAD), lambda h, i: (i, h))],
        out_shape=[_out((T, pw + H * V_HEAD), BF16),
                   _out((T, H * V_HEAD), F32)],
        args=[_hbm(qh), _hbm(kh), _hbm(vh)], sem=("parallel", "parallel"), est=est, ride=ride)
    return o, lse, gathered


def _flash_bwd(qh, kh, vh, cat, dcat, lse, *, H, pw, name, ride=None):
    T = qh.shape[0]
    t = _tile(T, 512, CHUNK)
    nb = T // t
    off = pw // V_HEAD

    def body(q_ref, k_ref, v_ref, o_ref, do_ref, lse_ref, dq_out_ref, dk_ref, dv_ref, dq_ref):
        j = pl.program_id(1)

        @pl.when(j == 0)
        def _():
            dq_ref[...] = jnp.zeros_like(dq_ref)

        kj = k_ref[...]
        vj = v_ref[...]

        def blk(i, carry, masked):
            dk, dv = carry
            rows = pl.ds(pl.multiple_of(i * t, t), t)
            qi = q_ref[rows, :]
            doi = do_ref[rows, :]
            oi = o_ref[rows, :].astype(F32)
            lsei = lse_ref[rows, :][:, :1]
            s = _dg(qi, kj, 1, 1)
            if masked:
                s = jnp.where(_diag_mask(t, t, 0), s, NEG_INF)
            p = jnp.exp(s - lsei)
            dv = dv + _dg(p, doi, 0, 0)
            dp = _dg(doi, vj, 1, 1)
            di = jnp.sum(doi * oi, axis=1, keepdims=True)
            ds = p * (dp - di)
            dk = dk + _dg(ds, qi, 0, 0)
            dq_ref[rows, :] += _dg(ds, kj, 1, 0)
            return dk, dv

        carry = blk(j, (jnp.zeros((t, HEAD_PAD), F32), jnp.zeros((t, V_HEAD), F32)), True)
        dk, dv = lax.fori_loop(j + 1, nb, lambda i, c: blk(i, c, False), carry)
        dk_ref[...] = dk.astype(dk_ref.dtype)
        dv_ref[...] = dv.astype(dv_ref.dtype)

        @pl.when(j == nb - 1)
        def _():
            dq_out_ref[...] = dq_ref[...].astype(dq_out_ref.dtype)

    est = T * (HEAD_PAD * 2 + V_HEAD * 2 + V_HEAD * 4 + V_HEAD * 4 + HEAD_PAD * 4) + 10 * t * t * 4
    (dq, dk, dv), gathered = _host_call(
        body, name=name, grid=(H, nb),
        in_specs=[pl.BlockSpec((T, HEAD_PAD), lambda h, j: (0, h)),
                  pl.BlockSpec((t, HEAD_PAD), lambda h, j: (j, h)),
                  pl.BlockSpec((t, V_HEAD), lambda h, j: (j, h)),
                  pl.BlockSpec((T, V_HEAD), lambda h, j: (0, off + h)),
                  pl.BlockSpec((T, V_HEAD), lambda h, j: (0, off + h)),
                  pl.BlockSpec((T, V_HEAD), lambda h, j: (0, h))],
        out_specs=[pl.BlockSpec((T, HEAD_PAD), lambda h, j: (0, h)),
                   pl.BlockSpec((t, HEAD_PAD), lambda h, j: (j, h)),
                   pl.BlockSpec((t, V_HEAD), lambda h, j: (j, h))],
        out_shape=[_out((T, H * HEAD_PAD), BF16),
                   _out((T, H * HEAD_PAD), BF16),
                   _out((T, H * V_HEAD), BF16)],
        scratch=[pltpu.VMEM((T, HEAD_PAD), F32)],
        args=[_hbm(v) for v in (qh, kh, vh, cat, dcat, lse)], sem=("arbitrary", "arbitrary"), est=est,
        ride=ride)
    return dq, dk, dv, gathered


def _mem_fn(q, k, v):
    hd = q.shape[1] // MEM_HEADS
    outs = []
    for h in range(MEM_HEADS):
        lo, hi = h * hd, (h + 1) * hd
        s = _bdot_nt(_cols(q, lo, hi), _cols(k, lo, hi)) * hd ** -0.5
        e = jnp.exp(s - lax.stop_gradient(jnp.max(s, axis=1, keepdims=True)))
        p = e / jnp.sum(e, axis=1, keepdims=True)
        outs.append(_bdot_nn(p, _cols(v, lo, hi)))
    return jnp.concatenate(outs, axis=1)


def _mem_fwd(q, k, v, *, name):
    T, D = q.shape
    tm = _tile(T, 256, 16)

    def fn(i, tv, pv):
        return (_mem_fn(tv[0], pv[0], pv[1]),), ()

    return _rowwise(fn, [q], [k, v], [(D, BF16)], tm=tm, name=name)[0]


def _mem_bwd(q, k, v, do, *, name):
    T, D = q.shape
    tm = _tile(T, 256, 16)

    def fn(i, tv, pv):
        _, vjp = jax.vjp(_mem_fn, tv[0], pv[0], pv[1])
        dq, dk, dv = vjp(tv[1].astype(F32))
        return (dq,), (dk, dv)

    return _rowwise(fn, [q, do], [k, v], [(D, BF16)], [(k.shape, F32), (v.shape, F32)], tm=tm, name=name)


def _adamw(w, g, m, v, *, name):
    shape = w.shape
    if w.ndim != 3:
        lead3 = (1, math.prod(shape[:-1]), shape[-1])
        return [o.reshape(shape) for o in _adamw(*[a.reshape(lead3) for a in (w, g, m, v)], name=name)]
    Lw, R, C = shape
    tr = _tile(R, 512, 8)
    b1c = 1.0 - ADAM_B1 ** ADAM_STEP
    b2c = 1.0 - ADAM_B2 ** ADAM_STEP

    def body(w_ref, g_ref, m_ref, v_ref, d_ref, mo_ref, vo_ref):
        gg = g_ref[...]
        mn = ADAM_B1 * m_ref[...] + (1.0 - ADAM_B1) * gg
        vn = ADAM_B2 * v_ref[...] + (1.0 - ADAM_B2) * (gg * gg)
        d_ref[...] = -ADAM_LR * ((mn / b1c) / (jnp.sqrt(vn / b2c) + ADAM_EPS) + ADAM_WD * w_ref[...])
        mo_ref[...] = mn
        vo_ref[...] = vn

    spec = pl.BlockSpec((None, tr, C), lambda l, i: (l, i, 0))
    return pl.pallas_call(
        body, name=name, grid=(Lw, R // tr),
        in_specs=[spec] * 4, out_specs=[spec] * 3,
        out_shape=[_out(shape, F32)] * 3,
        compiler_params=_params(("parallel", "parallel"), 7 * tr * C * 4),
    )(*[_hbm(a) for a in (w, g, m, v)])


def _pair_sum(core, gs, landed, offs, *, name):
    n = len(gs)
    _, R, C = landed.shape
    rows = [g.shape[0] // N_DEV for g in gs]

    def body(core_ref, *refs):
        g_refs, l_ref, o_ref = refs[:n], refs[n], refs[n + 1]
        for g_ref, off, r in zip(g_refs, offs, rows):
            o_ref[off:off + r, :] = (g_ref[...].astype(F32) + l_ref[off:off + r, :].astype(F32)).astype(o_ref.dtype)

    slab = pl.BlockSpec((None, R, C), lambda p, core_ref: (p, 0, 0))
    own = [pl.BlockSpec((r, C), lambda p, core_ref: (2 * p + core_ref[0], 0)) for r in rows]
    return pl.pallas_call(
        body, name=name,
        grid_spec=pltpu.PrefetchScalarGridSpec(
            num_scalar_prefetch=1, grid=(4,), in_specs=own + [slab], out_specs=slab),
        out_shape=_out(landed.shape, landed.dtype),
        input_output_aliases={n + 1: 0},
        compiler_params=_params(("arbitrary",), 3 * R * C * 2 + R * C * 8),
    )(core, *[_hbm(g) for g in gs], _hbm(landed))


def _quad_sum(chip, part, gathered, used, *, name):
    C = part.shape[2]
    R = used
    tr = _tile(R, 256, 16)

    def body(chip_ref, own_ref, a_ref, b_ref, c_ref, o_ref):
        o_ref[...] = ((own_ref[...].astype(F32) + a_ref[...].astype(F32)) + b_ref[...].astype(F32)) \
            + c_ref[...].astype(F32)

    def other(k):
        return pl.BlockSpec((None, tr, C), lambda i, chip_ref: (chip_ref[0] ^ k, i, 0))

    return pl.pallas_call(
        body, name=name,
        grid_spec=pltpu.PrefetchScalarGridSpec(
            num_scalar_prefetch=1, grid=(R // tr,),
            in_specs=[pl.BlockSpec((None, tr, C), lambda i, chip_ref: (chip_ref[0], i, 0)),
                      other(1), other(2), other(3)],
            out_specs=pl.BlockSpec((tr, C), lambda i, chip_ref: (i, 0))),
        out_shape=_out((R, C), F32),
        compiler_params=_params(("arbitrary",), 8 * tr * C * 4),
    )(chip, _hbm(part), _hbm(gathered), _hbm(gathered), _hbm(gathered))


def _place():
    x, y, c = lax.axis_index("x"), lax.axis_index("y"), lax.axis_index("c")
    return x, y, c


ANY = pl.BlockSpec(memory_space=pl.ANY)


class _Gather:
    def __init__(self, shards):
        self.shards = list(shards)
        self.n = len(self.shards)
        self.out_shape = [_out((s.shape[0], N_DEV * s.shape[1], s.shape[2]), s.dtype)
                          for s in self.shards]
        self.scratch = [pltpu.SemaphoreType.DMA((7 * self.n,)), pltpu.SemaphoreType.DMA((7 * self.n,)),
                        pltpu.SemaphoreType.DMA((self.n,))]
        self.operands = [_hbm(s) for s in self.shards]

    def _bind(self, refs):
        n = self.n
        ins, outs = refs[:n], refs[n:2 * n]
        send_sems, recv_sems, local_sems = refs[2 * n:]
        x, y, c = _place()
        me, sib = (x, y, c), (x, y, 1 - c)
        chips = [(1 - x, y), (x, 1 - y), (1 - x, 1 - y)]

        def rows(w, p):
            r = self.shards[w].shape[1]
            idx = 4 * p[0] + 2 * p[1] + p[2]
            return outs[w].at[:, pl.ds(pl.multiple_of(idx * r, 8), r), :]

        def copy(w, k, block, to, src=None):
            return pltpu.make_async_remote_copy(
                src_ref=rows(w, block) if src is None else src, dst_ref=rows(w, block),
                send_sem=send_sems.at[w * 7 + k], recv_sem=recv_sems.at[w * 7 + k],
                device_id=to, device_id_type=MESH)

        def mine():
            return [pltpu.make_async_copy(ins[w], rows(w, me), local_sems.at[w]) for w in range(n)]

        def first():
            out = []
            for w in range(n):
                out.append(copy(w, 0, me, sib, src=ins[w]))
                out += [copy(w, 1 + j, me, (*chip, c), src=ins[w]) for j, chip in enumerate(chips)]
            return out

        def passed():
            return [copy(w, 4 + j, (*chip, c), sib) for j, chip in enumerate(chips) for w in range(n)]

        def landed():
            return [copy(w, 1 + j, (*chip, c), me) for j, chip in enumerate(chips) for w in range(n)]

        def last():
            out = []
            for w in range(n):
                out.append(copy(w, 0, sib, me))
                out += [copy(w, 4 + j, (*chip, 1 - c), me) for j, chip in enumerate(chips)]
            return out

        return mine, first, landed, passed, last

    def start(self, refs):
        mine, first, _, _, _ = self._bind(refs)
        for cp in mine() + first():
            cp.start()

    def forward(self, refs):
        _, _, landed, passed, _ = self._bind(refs)
        for arrived, fwd in zip(landed(), passed()):
            arrived.wait_recv()
            fwd.start()

    def finish(self, refs):
        mine, first, _, passed, last = self._bind(refs)
        for cp in last():
            cp.wait_recv()
        for cp in first() + passed():
            cp.wait_send()
        for cp in mine():
            cp.wait()


class _ChipExchange:
    def __init__(self, parts, used):
        self.ncl = len(parts)
        self.used = list(used)
        self.out_shape = [_out(p.shape, p.dtype) for p in parts]
        self.scratch = [pltpu.SemaphoreType.DMA((3 * self.ncl,)), pltpu.SemaphoreType.DMA((3 * self.ncl,))]
        self.operands = [_hbm(p) for p in parts]
        self.n = self.ncl

    def _bind(self, refs):
        ncl = self.ncl
        ins, outs = refs[:ncl], refs[ncl:2 * ncl]
        send_sems, recv_sems = refs[2 * ncl:]
        x, y, c = _place()
        chips = [(1 - x, y), (x, 1 - y), (1 - x, 1 - y)]
        here = 2 * x + y

        def copies(outgoing):
            out = []
            for k in range(ncl):
                rows = pl.ds(0, self.used[k])
                for j, (cx, cy) in enumerate(chips):
                    there = 2 * cx + cy
                    src, dst = (there, here) if outgoing else (here, there)
                    out.append(pltpu.make_async_remote_copy(
                        src_ref=ins[k].at[src, rows, :], dst_ref=outs[k].at[dst, rows, :],
                        send_sem=send_sems.at[3 * k + j], recv_sem=recv_sems.at[3 * k + j],
                        device_id=(cx, cy, c), device_id_type=MESH))
            return out

        return copies

    def start(self, refs):
        for cp in self._bind(refs)(True):
            cp.start()

    def forward(self, refs):
        pass

    def finish(self, refs):
        copies = self._bind(refs)
        for cp in copies(False):
            cp.wait_recv()
        for cp in copies(True):
            cp.wait_send()


class _Both:
    def __init__(self, members):
        self.members = list(members)
        self.n = sum(m.n for m in self.members)
        self.out_shape = [s for m in self.members for s in m.out_shape]
        self.scratch = [s for m in self.members for s in m.scratch]
        self.operands = [o for m in self.members for o in m.operands]

    def split(self, arrays):
        out, a = [], 0
        for m in self.members:
            out.append(list(arrays[a:a + m.n]))
            a += m.n
        return out

    def _refs(self, refs):
        ins, outs = self.split(refs[:self.n]), self.split(refs[self.n:2 * self.n])
        scr, b = [], 2 * self.n
        for m in self.members:
            scr.append(list(refs[b:b + len(m.scratch)]))
            b += len(m.scratch)
        return [(*i, *o, *s) for i, o, s in zip(ins, outs, scr)]

    def start(self, refs):
        for m, r in zip(self.members, self._refs(refs)):
            m.start(r)

    def forward(self, refs):
        for m, r in zip(self.members, self._refs(refs)):
            m.forward(r)

    def finish(self, refs):
        for m, r in zip(self.members, self._refs(refs)):
            m.finish(r)


def _exchange_alone(ex, *, name):
    def body(*refs):
        ex.start(refs)
        ex.forward(refs)
        ex.finish(refs)

    return pl.pallas_call(
        body, name=name, in_specs=[ANY] * ex.n, out_specs=[ANY] * ex.n,
        out_shape=ex.out_shape, scratch_shapes=ex.scratch,
    )(*ex.operands)


def _host_call(body, *, name, grid, in_specs, out_specs, out_shape, args, sem, est, ride=None, scratch=()):
    scratch = list(scratch)
    if ride is None:
        outs = pl.pallas_call(body, name=name, grid=grid, in_specs=in_specs, out_specs=out_specs,
                              out_shape=out_shape, scratch_shapes=scratch,
                              compiler_params=_params(sem, est))(*args)
        return list(outs), []
    n_in, n_out, n, n_scr = len(in_specs), len(out_specs), ride.n, len(scratch)

    def full(*refs):
        ins, rin = refs[:n_in], refs[n_in:n_in + n]
        outs, rout = refs[n_in + n:n_in + n + n_out], refs[n_in + n + n_out:n_in + 2 * n + n_out]
        own = refs[n_in + 2 * n + n_out:n_in + 2 * n + n_out + n_scr]
        rrefs = (*rin, *rout, *refs[n_in + 2 * n + n_out + n_scr:])
        step, total = _ride(ride, rrefs, grid)
        body(*ins, *outs, *own)
        _ride_end(ride, rrefs, step, total)

    outs = pl.pallas_call(
        full, name=name, grid=grid,
        in_specs=list(in_specs) + [ANY] * n, out_specs=list(out_specs) + [ANY] * n,
        out_shape=list(out_shape) + ride.out_shape, scratch_shapes=scratch + ride.scratch,
        compiler_params=_params(("arbitrary",) * len(grid), est),
    )(*args, *ride.operands)
    return list(outs[:n_out]), list(outs[n_out:])


def _ride(ex, refs, grid):
    total = math.prod(grid)
    step = pl.program_id(0)
    for axis in range(1, len(grid)):
        step = step * grid[axis] + pl.program_id(axis)
    pl.when(step == 0)(lambda: ex.start(refs))
    return step, total


def _ride_end(ex, refs, step, total):
    pl.when(step == (3 * total) // 4)(lambda: ex.forward(refs))
    pl.when(step == total - 1)(lambda: ex.finish(refs))


def _class_layout(grads, classes):
    used = [0] * len(set(classes))
    offs = []
    for g, cl in zip(grads, classes):
        offs.append(used[cl])
        used[cl] += g.shape[0] // N_DEV
    return offs, used


def _rs_to_sibling(grads, classes, *, name):
    n = len(grads)
    offs, used = _class_layout(grads, classes)
    heights = used
    ncl = len(heights)
    cols = [next(g.shape[1] for g, cl in zip(grads, classes) if cl == k) for k in range(ncl)]

    def body(*refs):
        gs, land = refs[:n], refs[n:n + ncl]
        send_sems, recv_sems = refs[n + ncl:]
        x, y, c = _place()
        sib = (x, y, 1 - c)
        for p in range(4):
            for w in range(n):
                r = grads[w].shape[0] // N_DEV
                cl = classes[w]
                there = gs[w].at[pl.ds(pl.multiple_of((2 * p + 1 - c) * r, 8), r), :]
                pltpu.make_async_remote_copy(
                    src_ref=there, dst_ref=land[cl].at[p, pl.ds(offs[w], r), :],
                    send_sem=send_sems.at[cl * 4 + p], recv_sem=recv_sems.at[cl * 4 + p],
                    device_id=sib, device_id_type=MESH).start()
        for cl in range(ncl):
            for p in range(4):
                rows_used = land[cl].at[p, pl.ds(0, used[cl]), :]
                slab = pltpu.make_async_remote_copy(
                    src_ref=rows_used, dst_ref=rows_used,
                    send_sem=send_sems.at[cl * 4 + p], recv_sem=recv_sems.at[cl * 4 + p],
                    device_id=sib, device_id_type=MESH)
                slab.wait_send()
                slab.wait_recv()

    return pl.pallas_call(
        body, name=name,
        in_specs=[ANY] * n, out_specs=[ANY] * ncl,
        out_shape=[_out((4, heights[k], cols[k]), BF16) for k in range(ncl)],
        scratch_shapes=[pltpu.SemaphoreType.DMA((4 * ncl,))] * 2,
    )(*[_hbm(g) for g in grads])


def _all_reduce_small(v, *, name):
    R = v.shape[0]

    def body(v_ref, o_ref, buf, send_sems, recv_sems):
        x, y, c = _place()
        me = 4 * x + 2 * y + c
        buf[me] = v_ref[...]
        copies = []
        for k in range(1, N_DEV):
            fx, fy, fc = (k >> 2) & 1, (k >> 1) & 1, k & 1
            to = (x ^ fx, y ^ fy, c ^ fc)
            cp = pltpu.make_async_remote_copy(
                src_ref=v_ref, dst_ref=buf.at[me],
                send_sem=send_sems.at[k - 1], recv_sem=recv_sems.at[k - 1],
                device_id=to, device_id_type=MESH)
            cp.start()
            copies.append(cp)
        for k in range(1, N_DEV):
            fx, fy, fc = (k >> 2) & 1, (k >> 1) & 1, k & 1
            frm = 4 * (x ^ fx) + 2 * (y ^ fy) + (c ^ fc)
            pltpu.make_async_remote_copy(
                src_ref=v_ref, dst_ref=buf.at[frm],
                send_sem=send_sems.at[k - 1], recv_sem=recv_sems.at[k - 1],
                device_id=(x ^ fx, y ^ fy, c ^ fc), device_id_type=MESH).wait_recv()
        for cp in copies:
            cp.wait_send()
        acc = buf[0]
        for d in range(1, N_DEV):
            acc = acc + buf[d]
        o_ref[...] = acc

    vm = pl.BlockSpec(memory_space=pltpu.VMEM)
    return pl.pallas_call(
        body, name=name, in_specs=[vm], out_specs=vm,
        out_shape=jax.ShapeDtypeStruct((R, LANE), F32),
        scratch_shapes=[pltpu.VMEM((N_DEV, R, LANE), F32),
                        pltpu.SemaphoreType.DMA((N_DEV - 1,)), pltpu.SemaphoreType.DMA((N_DEV - 1,))],
        compiler_params=pltpu.CompilerParams(vmem_limit_bytes=VMEM_FLOOR),
    )(v)


def _rope_tables(positions):
    half = QK_ROPE // 2
    inv_freq = ROPE_BASE ** (-jnp.arange(half, dtype=F32) / half)
    ang = positions.astype(F32)[:, None] * inv_freq
    cos, sin = jnp.cos(ang), jnp.sin(ang)
    z = jnp.zeros_like(cos)
    z2 = jnp.zeros((positions.shape[0], LANE - QK_ROPE), F32)
    rc = jnp.concatenate([cos, cos, z2], axis=1)
    rs1 = jnp.concatenate([-sin, z, z2], axis=1)
    rs2 = jnp.concatenate([z, sin, z2], axis=1)
    return rc, rs1, rs2


def _block_diag(pool_w):
    G, pg, _ = pool_w.shape
    out = jnp.zeros((G * pg, G * pg), pool_w.dtype)
    for g in range(G):
        out = lax.dynamic_update_slice(out, pool_w[g], (g * pg, g * pg))
    return out


def kernel(x, mem, positions, ln_g, ln_b, ffn1_w13, ffn1_w2, w_in, pool_w, pool_scale, q_norm_g, w_uq, kv_norm_g, w_ukv, w_out, mem_wq, mem_wkv, mem_wo, ffn2_w13, ffn2_w2, loss_target, m_ln_g, m_ln_b, m_ffn1_w13, m_ffn1_w2, m_w_in, m_pool_w, m_pool_scale, m_q_norm_g, m_w_uq, m_kv_norm_g, m_w_ukv, m_w_out, m_mem_wq, m_mem_wkv, m_mem_wo, m_ffn2_w13, m_ffn2_w2, v_ln_g, v_ln_b, v_ffn1_w13, v_ffn1_w2, v_w_in, v_pool_w, v_pool_scale, v_q_norm_g, v_w_uq, v_kv_norm_g, v_w_ukv, v_w_out, v_mem_wq, v_mem_wkv, v_mem_wo, v_ffn2_w13, v_ffn2_w2):
    L = ln_g.shape[0]
    T, D = x.shape[1], x.shape[2]
    F = ffn1_w2.shape[1] * N_DEV
    PW = D // 4
    H = (D - PW) // V_HEAD
    DIN = w_in.shape[2]
    DINP = PW + Q_LORA + KV_LORA + LANE
    QW = QK_NOPE + QK_ROPE
    alpha = (2 * L) ** 0.25
    x2d = x.reshape(T, D)
    memb = mem.reshape(mem.shape[1], D).astype(BF16)
    target = loss_target.reshape(T, D)
    tabs = _rope_tables(positions.reshape(T))

    def shards_of(l):
        return dict(
            w13a=ffn1_w13[l].T[None].astype(BF16),
            w13b=ffn2_w13[l].T[None].astype(BF16),
            w2a=ffn1_w2[l][None].astype(BF16),
            w2b=ffn2_w2[l][None].astype(BF16),
            wsq=jnp.stack([w_out[l], mem_wq[l], mem_wo[l]]).astype(BF16),
            wkvT=mem_wkv[l].T[None].astype(BF16),
            winp=jnp.pad(w_in[l], ((0, 0), (0, DINP - DIN)))[None].astype(BF16),
            wuqT=w_uq[l].T[None].astype(BF16),
            wukvT=w_ukv[l].T[None].astype(BF16),
        )

    SMALL = ("winp", "wuqT", "wukvT")
    shards = [shards_of(l) for l in range(L)]
    W = [dict() for _ in range(L)]

    def rider(spec):
        return _Gather([shards[l][n] for l, n in spec]) if spec else None

    def arrived(spec, arrays):
        for (l, n), a in zip(spec, arrays):
            if n in ("w13a", "w13b"):
                a = _interleave(a, 1)
            elif n == "wuqT":
                a = jnp.pad(a.reshape(H, QW, Q_LORA), ((0, 0), (0, HEAD_PAD - QW), (0, 0)))
                a = a.reshape(1, H * HEAD_PAD, Q_LORA)
            elif n == "ln":
                a = jnp.moveaxis(a.reshape(N_DEV, 2, L, 4, D // N_DEV), 0, 3).reshape(2, L, 4, D)
                LN["g"], LN["b"] = a[0], a[1]
            W[l][n] = a

    LN = {}
    shards[0]["ln"] = jnp.concatenate([ln_g.reshape(1, 4 * L, -1), ln_b.reshape(1, 4 * L, -1)], axis=1)
    spec0 = [(0, "w13a")]
    arrived(spec0, _exchange_alone(rider(spec0), name="ag_first"))
    wbd = [_block_diag(pool_w[l]).astype(BF16) for l in range(L)]

    def ffn_fwd(l, which, xres, xb, k, spec):
        ab = "ab"[which]
        h13, a, rode = _ffn_up(xb, W[l]["w13" + ab], 0, name=f"l{l}_ffn{which}_up", ride=rider(spec))
        arrived(spec, rode)
        y, xo, xob = _mm_ln(a, W[l]["w2" + ab], 0, xres, LN["g"][l,k:k + 1], LN["b"][l,k:k + 1], alpha=alpha, s=0.5,
                            name=f"l{l}_ffn{which}_y_ln{k}")
        return dict(xres=xres, xb=xb, h13=h13, a=a, y=y), xo, xob

    saved = []
    xres, xb = x2d, x2d.astype(BF16)
    for l in range(L):
        sv = {}
        more = l + 1 < L
        Wl = W[l]
        spec = ([(0, "w2a"), (0, "ln"), *[(0, n) for n in SMALL], (0, "wkvT")] if l == 0
                else [(l, "wsq"), (l, "wkvT")])
        sv["ffn1"], x1, x1b = ffn_fwd(l, 0, xres, xb, 0, spec)
        hin = _mm(x1b, Wl["winp"], lead=0, name=f"l{l}_hin")
        pscale = pool_scale[l].reshape(1, PW)
        gq, gkv = q_norm_g[l].reshape(1, Q_LORA), kv_norm_g[l].reshape(1, KV_LORA)
        cqn, ckvn, kpe = _norms_fwd(hin, gq, gkv, pw=PW, name=f"l{l}_norms")
        qh, kh, vh = _heads_fwd(cqn, ckvn, kpe, tabs, Wl["wuqT"][0], Wl["wukvT"][0], H=H, name=f"l{l}_heads")
        spec = [(l, "w13b"), (l, "w2b")] + ([(0, "wsq")] if l == 0 else []) + ([(l + 1, "w13a")] if more else [])
        cat, lse, rode = _flash_fwd(qh, kh, vh, H=H, pw=PW, name=f"l{l}_flash", ride=rider(spec))
        arrived(spec, rode)
        cat = _pool_fwd(hin, wbd[l], pscale, cat, name=f"l{l}_pool")
        ymix, x2, x2b = _mm_ln(cat, Wl["wsq"], 0, x1, LN["g"][l,1:2], LN["b"][l,1:2], alpha=alpha, s=1.0,
                               name=f"l{l}_ymix_ln1")
        qm = _mm(x2b, Wl["wsq"], lead=1, out_dtype=BF16, name=f"l{l}_qm")
        kvm = _mm(memb, Wl["wkvT"], lead=0, tb=True, name=f"l{l}_kvm")
        km, vm = kvm[:, :D], kvm[:, D:]
        om = _mem_fwd(qm, km, vm, name=f"l{l}_memattn")
        ymem, x3, x3b = _mm_ln(om, Wl["wsq"], 2, x2, LN["g"][l,2:3], LN["b"][l,2:3], alpha=alpha, s=1.0,
                               name=f"l{l}_ymem_ln2")
        spec = [(l + 1, n) for n in ("w2a", *SMALL)] if more else []
        sv["ffn2"], x4, x4b = ffn_fwd(l, 1, x3, x3b, 3, spec)
        sv.update(x1=x1, x1b=x1b, hin=hin, pscale=pscale, gq=gq, gkv=gkv, cqn=cqn, ckvn=ckvn,
                  qh=qh, kh=kh, vh=vh, lse=lse, cat=cat, ymix=ymix, x2=x2, x2b=x2b, qm=qm, km=km, vm=vm,
                  om=om, ymem=ymem)
        saved.append(sv)
        xres, xb = x4, x4b


    gW = {}
    gS = {}

    def ln_of(l, k):
        sv = saved[l]
        x, y, s = {0: (sv["ffn1"]["xres"], sv["ffn1"]["y"], 0.5), 1: (sv["x1"], sv["ymix"], 1.0),
                   2: (sv["x2"], sv["ymem"], 1.0), 3: (sv["ffn2"]["xres"], sv["ffn2"]["y"], 0.5)}[k]
        return x, y, LN["g"][l, k:k + 1], s

    def dx_through_ln(a, b, lead, tb, add, into, name):
        x, y, g, s = ln_of(*into)
        dxres, dyb, dg, db = _mm_ln_bwd(a, b, lead, tb, add, x, y, g, alpha=alpha, s=s, name=name)
        gS[("ln_g", *into)], gS[("ln_b", *into)] = dg, db
        return dxres, dyb

    def ffn_bwd(l, which, sv, dxres, dyb, ride, into):
        tag = f"l{l}_ffn{which}"
        gW[("w2", which, l)] = _mm(sv["a"], dyb, ta=True, out_dtype=BF16, name=f"{tag}_dw2", tn=D)
        dh, rode = _ffn_down_bwd(dyb, W[l]["w2" + "ab"[which]], 0, sv["h13"], name=f"{tag}_dh", ride=ride)
        dw13 = _mm(dh, sv["xb"], ta=True, out_dtype=BF16, name=f"{tag}_dw13", tn=D)
        gW[("w13", which, l)] = _deinterleave(dw13, 0)
        w13 = W[l]["w13" + "ab"[which]]
        if into is not None:
            return dx_through_ln(dh, w13, 0, False, dxres, into, f"{tag}_dx"), rode
        last = rs_first_level(l, "c")
        dxn, got = _mm(dh, w13, lead=0, add=dxres, name=f"{tag}_dx", tn=D, ride=last["ex"])
        rs_last_level(last, got)
        return dxn, rode

    core = lax.axis_index("c").astype(jnp.int32).reshape(1)
    chip = (2 * lax.axis_index("x") + lax.axis_index("y")).astype(jnp.int32).reshape(1)
    gsh = {}

    def rs_first_level(l, group):
        keys, classes = {
            "a": ([("w13", 1, l), ("w2", 1, l), ("mem_wkv", l), ("mem_wq", l), ("mem_wo", l)], [0] * 5),
            "b": ([("w_out", l), ("w_in", l), ("w_uq", l), ("w_ukv", l)], [0, 1, 2, 3]),
            "c": ([("w13", 0, l), ("w2", 0, l)], [0, 0]),
        }[group]
        tag = f"l{l}{group}"
        garrs = []
        for key in keys:
            g = gW[key]
            if key[0] == "w_uq":
                g = g.reshape(H, HEAD_PAD, Q_LORA)[:, :QW, :].reshape(H * QW, Q_LORA)
            garrs.append(g)
        offs, used = _class_layout(garrs, classes)
        parts = list(_rs_to_sibling(garrs, classes, name=f"{tag}_rs_sibling"))
        for cl in range(len(parts)):
            mine = [w for w, c in enumerate(classes) if c == cl]
            parts[cl] = _pair_sum(core, [garrs[w] for w in mine], parts[cl], [offs[w] for w in mine],
                                  name=f"{tag}_rs_pair_sum{cl}")
        return dict(tag=tag, keys=keys, garrs=garrs, classes=classes, offs=offs, used=used, parts=parts,
                    ex=_ChipExchange(parts, used))

    def rs_last_level(st, gathered):
        sums = [_quad_sum(chip, p, a, u, name=f"{st['tag']}_rs_quad_sum{k}")
                for k, (p, a, u) in enumerate(zip(st["parts"], gathered, st["used"]))]
        for key, g, cl, off in zip(st["keys"], st["garrs"], st["classes"], st["offs"]):
            gsh[key] = sums[cl][off:off + g.shape[0] // N_DEV, :]

    top = (L - 1, 3)
    x_top, y_top, g_top, s_top = ln_of(*top)
    dxres, dyb, gS[("ln_g", *top)], gS[("ln_b", *top)], loss_blk = _loss_ln_bwd(
        x_top, y_top, g_top, xres, target, alpha=alpha, s=s_top, name="loss_ln_top_bwd")
    loss = lax.psum(loss_blk[0, 0], ("x", "y", "c"))
    above = None
    for l in reversed(range(L)):
        sv = saved[l]
        Wl = W[l]
        (dxres, dyb), _ = ffn_bwd(l, 1, sv["ffn2"], dxres, dyb, None, (l, 2))
        dom = _mm(dyb, Wl["wsq"], lead=2, tb=True, out_dtype=BF16, name=f"l{l}_dom")
        gW[("mem_wo", l)] = _mm(sv["om"], dyb, ta=True, out_dtype=BF16, name=f"l{l}_dwo", tn=D)
        dqm, dkm, dvm = _mem_bwd(sv["qm"], sv["km"], sv["vm"], dom, name=f"l{l}_memattn_bwd")
        dxres, dyb = dx_through_ln(dqm, Wl["wsq"], 1, True, dxres, (l, 1), f"l{l}_dx2")
        gW[("mem_wq", l)] = _mm(sv["x2b"], dqm, ta=True, out_dtype=BF16, name=f"l{l}_dwq", tn=D)
        dkvm = jnp.concatenate([dkm, dvm], axis=1).astype(BF16)
        gW[("mem_wkv", l)] = _mm(dkvm, memb, ta=True, out_dtype=BF16, name=f"l{l}_dwkv", tn=D)
        dcat = _mm(dyb, Wl["wsq"], lead=0, tb=True, name=f"l{l}_dcat", tn=D)
        gW[("w_out", l)] = _mm(sv["cat"], dyb, ta=True, out_dtype=BF16, name=f"l{l}_dwout", tn=D)
        riding = [rs_first_level(l, "a")] + ([above] if above else [])
        both = _Both([st["ex"] for st in riding])
        dqh, dkh, dvh, rode = _flash_bwd(sv["qh"], sv["kh"], sv["vh"], sv["cat"], dcat, sv["lse"], H=H, pw=PW,
                                         name=f"l{l}_flash_bwd", ride=both)
        for st, got in zip(riding, both.split(rode)):
            rs_last_level(st, got)
        dqraw, dkv, dkpe = _heads_bwd(dqh, dkh, dvh, tabs, H=H, name=f"l{l}_heads_bwd")
        dcq = _mm(dqraw, Wl["wuqT"], lead=0, name=f"l{l}_dcq")
        gW[("w_uq", l)] = _mm(dqraw, sv["cqn"], ta=True, out_dtype=BF16, name=f"l{l}_dwuq")
        dckv = _mm(dkv, Wl["wukvT"], lead=0, name=f"l{l}_dckv")
        gW[("w_ukv", l)] = _mm(dkv, sv["ckvn"], ta=True, out_dtype=BF16, name=f"l{l}_dwukv")
        du, dwbd, dps = _pool_bwd(sv["hin"], dcat, wbd[l], sv["pscale"], name=f"l{l}_pool_bwd")
        dhin, dgq, dgkv = _norms_bwd(sv["hin"], sv["gq"], sv["gkv"], dcq, dckv, dkpe, du, pw=PW,
                                     name=f"l{l}_norms_bwd")
        pg = PW // len(POOL_WINDOWS)
        gS[("pool_w", l)] = jnp.stack([dwbd[g * pg:(g + 1) * pg, g * pg:(g + 1) * pg]
                                       for g in range(len(POOL_WINDOWS))])
        gS[("pool_scale", l)], gS[("q_norm_g", l)], gS[("kv_norm_g", l)] = dps, dgq, dgkv
        dxres, dyb = dx_through_ln(dhin, Wl["winp"], 0, True, dxres, (l, 0), f"l{l}_dx1")
        gW[("w_in", l)] = _mm(sv["x1b"], dhin, ta=True, out_dtype=BF16, name=f"l{l}_dwin", tn=DINP)
        heads = rs_first_level(l, "b")
        below, rode = ffn_bwd(l, 0, sv["ffn1"], dxres, dyb, heads["ex"], (l - 1, 3) if l > 0 else None)
        rs_last_level(heads, rode)
        if l > 0:
            dxres, dyb = below
            above = rs_first_level(l, "c")
    grad_x = below.reshape(1, T, D)

    small_keys = []
    for l in range(L):
        small_keys += [("pool_w", l), ("pool_scale", l), ("q_norm_g", l), ("kv_norm_g", l)]
        small_keys += [("ln_g", l, k) for k in range(4)] + [("ln_b", l, k) for k in range(4)]
    flat = jnp.concatenate([gS[k].reshape(-1) for k in small_keys])
    n_small = flat.shape[0]
    rows = -(-n_small // (8 * LANE)) * 8
    flat = jnp.pad(flat, (0, rows * LANE - n_small)).reshape(rows, LANE)
    red = _all_reduce_small(flat, name="ar_small").reshape(-1)
    gsm, pos = {}, 0
    for k in small_keys:
        size = math.prod(gS[k].shape)
        gsm[k] = red[pos:pos + size].reshape(gS[k].shape)
        pos += size

    me = 4 * lax.axis_index("x") + 2 * lax.axis_index("y") + lax.axis_index("c")
    dsh = D // N_DEV
    stack = lambda f: jnp.stack([f(l) for l in range(L)])
    g_ln_g = stack(lambda l: jnp.concatenate([gsm[("ln_g", l, k)] for k in range(4)], axis=0))
    g_ln_b = stack(lambda l: jnp.concatenate([gsm[("ln_b", l, k)] for k in range(4)], axis=0))
    swapped = {
        "ffn1_w13": stack(lambda l: gsh[("w13", 0, l)]),
        "ffn2_w13": stack(lambda l: gsh[("w13", 1, l)]),
        "w_in": stack(lambda l: gsh[("w_in", l)][:, :DIN].T),
        "w_uq": stack(lambda l: gsh[("w_uq", l)]),
        "w_ukv": stack(lambda l: gsh[("w_ukv", l)]),
    }
    swap = lambda a: jnp.swapaxes(a, 1, 2)
    grads = {
        "ln_g": lax.dynamic_slice_in_dim(g_ln_g, me * dsh, dsh, axis=2),
        "ln_b": lax.dynamic_slice_in_dim(g_ln_b, me * dsh, dsh, axis=2),
        "ffn1_w2": stack(lambda l: gsh[("w2", 0, l)]),
        "pool_w": stack(lambda l: gsm[("pool_w", l)]),
        "pool_scale": stack(lambda l: gsm[("pool_scale", l)].reshape(PW)),
        "q_norm_g": stack(lambda l: gsm[("q_norm_g", l)].reshape(Q_LORA)),
        "kv_norm_g": stack(lambda l: gsm[("kv_norm_g", l)].reshape(KV_LORA)),
        "w_out": stack(lambda l: gsh[("w_out", l)]),
        "mem_wq": stack(lambda l: gsh[("mem_wq", l)]),
        "mem_wkv": stack(lambda l: gsh[("mem_wkv", l)].T),
        "mem_wo": stack(lambda l: gsh[("mem_wo", l)]),
        "ffn2_w2": stack(lambda l: gsh[("w2", 1, l)]),
        **{nme: swap(g) for nme, g in swapped.items()},
    }

    names = ["ln_g", "ln_b", "ffn1_w13", "ffn1_w2", "w_in", "pool_w", "pool_scale", "q_norm_g", "w_uq",
             "kv_norm_g", "w_ukv", "w_out", "mem_wq", "mem_wkv", "mem_wo", "ffn2_w13", "ffn2_w2"]
    weights = dict(ln_g=ln_g, ln_b=ln_b, ffn1_w13=ffn1_w13, ffn1_w2=ffn1_w2, w_in=w_in, pool_w=pool_w,
                   pool_scale=pool_scale, q_norm_g=q_norm_g, w_uq=w_uq, kv_norm_g=kv_norm_g, w_ukv=w_ukv,
                   w_out=w_out, mem_wq=mem_wq, mem_wkv=mem_wkv, mem_wo=mem_wo, ffn2_w13=ffn2_w13,
                   ffn2_w2=ffn2_w2)
    ms = dict(ln_g=m_ln_g, ln_b=m_ln_b, ffn1_w13=m_ffn1_w13, ffn1_w2=m_ffn1_w2, w_in=m_w_in, pool_w=m_pool_w,
              pool_scale=m_pool_scale, q_norm_g=m_q_norm_g, w_uq=m_w_uq, kv_norm_g=m_kv_norm_g,
              w_ukv=m_w_ukv, w_out=m_w_out, mem_wq=m_mem_wq, mem_wkv=m_mem_wkv, mem_wo=m_mem_wo,
              ffn2_w13=m_ffn2_w13, ffn2_w2=m_ffn2_w2)
    vs = dict(ln_g=v_ln_g, ln_b=v_ln_b, ffn1_w13=v_ffn1_w13, ffn1_w2=v_ffn1_w2, w_in=v_w_in, pool_w=v_pool_w,
              pool_scale=v_pool_scale, q_norm_g=v_q_norm_g, w_uq=v_w_uq, kv_norm_g=v_kv_norm_g,
              w_ukv=v_w_ukv, w_out=v_w_out, mem_wq=v_mem_wq, mem_wkv=v_mem_wkv, mem_wo=v_mem_wo,
              ffn2_w13=v_ffn2_w13, ffn2_w2=v_ffn2_w2)
    deltas, new_m, new_v = [], [], []
    for nme in names:
        if nme in swapped:
            d, mn, vn = [swap(o) for o in _adamw(swap(weights[nme]), swapped[nme], swap(ms[nme]), swap(vs[nme]),
                                                 name=f"adamw_{nme}")]
        else:
            d, mn, vn = _adamw(weights[nme], grads[nme], ms[nme], vs[nme], name=f"adamw_{nme}")
        deltas.append(d)
        new_m.append(mn)
        new_v.append(vn)
    return (loss, grad_x, *[grads[nme] for nme in names], *deltas, *new_m, *new_v)
```

```python
import functools
import math

import jax
import jax.numpy as jnp
from jax import lax
from jax.experimental import pallas as pl
from jax.experimental.pallas import tpu as pltpu

F32 = jnp.float32
BF16 = jnp.bfloat16
MESH = pl.DeviceIdType.MESH

CHUNK = 64
MEM_HEADS = 4
POOL_WINDOWS = (2, 4, 8, 16)
QK_NOPE = 128
QK_ROPE = 64
V_HEAD = 128
Q_LORA = 256
KV_LORA = 128
ROPE_BASE = 10000.0
LN_EPS = 1e-5
RMS_EPS = 1e-6
NEG_INF = -1e30
ADAM_LR = 0.001
ADAM_B1 = 0.9
ADAM_B2 = 0.999
ADAM_EPS = 1e-08
ADAM_WD = 0.01
ADAM_STEP = 10

N_DEV = 8
LANE = 128
HEAD_PAD = 2 * LANE
POOL_HALO = 16
VMEM_CAP = 56 * 1024 * 1024
VMEM_FLOOR = 32 * 1024 * 1024


def _tile(n, pref, mult):
    t = (min(pref, n) // mult) * mult
    while t >= mult:
        if n % t == 0:
            return t
        t -= mult
    return n


def _params(sem, est_bytes):
    limit = int(min(max(2 * est_bytes + (8 << 20), VMEM_FLOOR), VMEM_CAP))
    return pltpu.CompilerParams(dimension_semantics=sem, vmem_limit_bytes=limit)


def _nbytes(shape, dtype):
    return math.prod(shape) * jnp.dtype(dtype).itemsize


def _hbm(x):
    return pltpu.with_memory_space_constraint(x, pltpu.HBM)


def _out(shape, dtype):
    return pltpu.HBM(tuple(shape), dtype)


def _dg(a, b, ca, cb):
    return lax.dot_general(a.astype(BF16), b.astype(BF16), (((ca,), (cb,)), ((), ())),
                           preferred_element_type=F32)


@jax.custom_vjp
def _bdot_nn(a, b):
    return _dg(a, b, 1, 0)


def _bdot_nn_fwd(a, b):
    return _dg(a, b, 1, 0), (a, b)


def _bdot_nn_bwd(res, ct):
    a, b = res
    return _dg(ct, b, 1, 1).astype(a.dtype), _dg(a, ct, 0, 0).astype(b.dtype)


_bdot_nn.defvjp(_bdot_nn_fwd, _bdot_nn_bwd)


@jax.custom_vjp
def _bdot_nt(a, b):
    return _dg(a, b, 1, 1)


def _bdot_nt_fwd(a, b):
    return _dg(a, b, 1, 1), (a, b)


def _bdot_nt_bwd(res, ct):
    a, b = res
    return _dg(ct, b, 1, 0).astype(a.dtype), _dg(ct, a, 0, 0).astype(b.dtype)


_bdot_nt.defvjp(_bdot_nt_fwd, _bdot_nt_bwd)


@functools.partial(jax.custom_vjp, nondiff_argnums=(1,))
def _lane_roll(x, shift):
    return pltpu.roll(x, shift % x.shape[1], axis=1)


def _lane_roll_fwd(x, shift):
    return _lane_roll(x, shift), None


def _lane_roll_bwd(shift, _, ct):
    return (_lane_roll(ct, -shift),)


_lane_roll.defvjp(_lane_roll_fwd, _lane_roll_bwd)


@functools.partial(jax.custom_vjp, nondiff_argnums=(1, 2))
def _cols(x, lo, hi):
    return x[:, lo:hi]


def _cols_fwd(x, lo, hi):
    return x[:, lo:hi], x.shape[1]


def _cols_bwd(lo, hi, width, ct):
    parts = []
    if lo > 0:
        parts.append(jnp.zeros((ct.shape[0], lo), ct.dtype))
    parts.append(ct)
    if hi < width:
        parts.append(jnp.zeros((ct.shape[0], width - hi), ct.dtype))
    return (jnp.concatenate(parts, axis=1) if len(parts) > 1 else ct,)


_cols.defvjp(_cols_fwd, _cols_bwd)


MM_VMEM_BUDGET = 22 * 1024 * 1024


def _mm(a, b, *, name, ta=False, tb=False, out_dtype=F32, lead=None, add=None, add_scale=1.0,
        tm=1024, tn=1024, tk=8192, ride=None):
    if ta:
        K, M = a.shape
    else:
        M, K = a.shape
    bshape = b.shape[1:] if lead is not None else b.shape
    if tb:
        N, Kb = bshape
    else:
        Kb, N = bshape
    assert K == Kb, (name, a.shape, b.shape)

    def blocks(tm, tn, tk):
        tm = _tile(M, tm, LANE if ta else 16)
        tn = _tile(N, tn, LANE)
        tk = _tile(K, tk, LANE)
        nbytes = (tm * tk * a.dtype.itemsize + tk * tn * b.dtype.itemsize
                  + tm * tn * (jnp.dtype(out_dtype).itemsize + (4 if K // tk > 1 else 0)
                               + (add.dtype.itemsize if add is not None else 0)))
        return tm, tn, tk, nbytes

    tm, tn, tk, est = blocks(tm, tn, tk)
    for shrink in ("m", "k", "m", "k", "n"):
        if est <= MM_VMEM_BUDGET:
            break
        if shrink == "m":
            tm, tn, tk, est = blocks(max(tm // 2, LANE), tn, tk)
        elif shrink == "k":
            tm, tn, tk, est = blocks(tm, tn, max(tk // 2, LANE))
        else:
            tm, tn, tk, est = blocks(tm, max(tn // 2, LANE), tk)
    nk = K // tk
    ca = 0 if ta else 1
    cb = 1 if tb else 0

    def body(*refs):
        a_ref, b_ref = refs[0], refs[1]
        add_ref = refs[2] if add is not None else None
        o_ref = refs[3] if add is not None else refs[2]

        def finish(r):
            if add_ref is not None:
                r = r + add_scale * add_ref[...].astype(F32)
            o_ref[...] = r.astype(o_ref.dtype)

        if nk == 1:
            finish(_dg(a_ref[...], b_ref[...], ca, cb))
            return
        acc_ref = refs[-1]
        k = pl.program_id(2)

        @pl.when(k == 0)
        def _():
            acc_ref[...] = jnp.zeros_like(acc_ref)

        acc_ref[...] += _dg(a_ref[...], b_ref[...], ca, cb)

        @pl.when(k == nk - 1)
        def _():
            finish(acc_ref[...])

    a_blk = (tk, tm) if ta else (tm, tk)
    a_map = (lambda i, j, k: (k, i)) if ta else (lambda i, j, k: (i, k))
    b_blk = (tn, tk) if tb else (tk, tn)
    if lead is None:
        b_map = (lambda i, j, k: (j, k)) if tb else (lambda i, j, k: (k, j))
        b_spec = pl.BlockSpec(b_blk, b_map)
    else:
        b_map = (lambda i, j, k: (lead, j, k)) if tb else (lambda i, j, k: (lead, k, j))
        b_spec = pl.BlockSpec((None,) + b_blk, b_map)
    in_specs = [pl.BlockSpec(a_blk, a_map), b_spec]
    args = [a, b]
    if add is not None:
        in_specs.append(pl.BlockSpec((tm, tn), lambda i, j, k: (i, j)))
        args.append(add)
    (out,), rode = _host_call(
        body, name=name,
        grid=(M // tm, N // tn, nk),
        in_specs=in_specs,
        out_specs=[pl.BlockSpec((tm, tn), lambda i, j, k: (i, j))],
        out_shape=[_out((M, N), out_dtype)],
        scratch=[pltpu.VMEM((tm, tn), F32)] if nk > 1 else [],
        args=[_hbm(v) for v in args], sem=("parallel", "parallel", "arbitrary"), est=est + tm * tn * 4,
        ride=ride)
    return out if ride is None else (out, rode)


def _rowwise(fn, tiles, params, tile_outs, acc_outs=(), *, tm, name):
    tile_arrays, tile_specs = [], []
    for t in tiles:
        if isinstance(t, tuple):
            tile_arrays.append(t[0])
            tile_specs.append(t[1])
        else:
            tile_arrays.append(t)
            tile_specs.append(pl.BlockSpec((tm, t.shape[1]), lambda i: (i, 0)))
    T = tile_arrays[0].shape[0]
    nt, np_, nto, nao = len(tile_arrays), len(params), len(tile_outs), len(acc_outs)

    def body(*refs):
        i = pl.program_id(0)
        tvals = [r[...] for r in refs[:nt]]
        pvals = [r[...] for r in refs[nt:nt + np_]]
        to_refs = refs[nt + np_:nt + np_ + nto]
        ao_refs = refs[nt + np_ + nto:]
        touts, aouts = fn(i, tvals, pvals)
        for r, v in zip(to_refs, touts):
            r[...] = v.astype(r.dtype)
        if nao:
            @pl.when(i == 0)
            def _():
                for r in ao_refs:
                    r[...] = jnp.zeros_like(r)
            for r, v in zip(ao_refs, aouts):
                r[...] += v.astype(r.dtype)

    in_specs = tile_specs + [pl.BlockSpec(p.shape, lambda i: (0, 0)) for p in params]
    out_specs = [pl.BlockSpec((tm, c), lambda i: (i, 0)) for c, _ in tile_outs]
    out_specs += [pl.BlockSpec(s, lambda i: (0, 0)) for s, _ in acc_outs]
    out_shape = [_out((T, c), d) for c, d in tile_outs]
    out_shape += [_out(s, d) for s, d in acc_outs]
    width = sum(s.block_shape[-1] for s in tile_specs) + sum(c for c, _ in tile_outs)
    est = 6 * tm * width * 4 + sum(_nbytes(p.shape, F32) for p in params) * 4
    return pl.pallas_call(
        body, name=name, grid=(T // tm,),
        in_specs=in_specs, out_specs=out_specs, out_shape=out_shape,
        compiler_params=_params(("arbitrary",) if nao else ("parallel",), est),
    )(*[_hbm(v) for v in tile_arrays], *[_hbm(p) for p in params])


def _ln_fn(alpha, s, xres, y, g, b):
    z = alpha * xres.astype(F32) + s * y.astype(F32)
    mu = jnp.mean(z, axis=-1, keepdims=True)
    zc = z - mu
    var = jnp.mean(zc * zc, axis=-1, keepdims=True)
    return zc * lax.rsqrt(var + LN_EPS) * g + b


def _mm_ln(a, b, lead, xres, g, bias, *, alpha, s, name):
    M, K = a.shape
    N = b.shape[2]
    tm = _tile(M, 512, 16)

    def body(a_ref, b_ref, x_ref, g_ref, bias_ref, y_ref, xo_ref, xb_ref):
        y = _dg(a_ref[...], b_ref[...], 1, 0)
        y_ref[...] = y.astype(y_ref.dtype)
        out = _ln_fn(alpha, s, x_ref[...], y, g_ref[...], bias_ref[...])
        xo_ref[...] = out
        xb_ref[...] = out.astype(BF16)

    row = pl.BlockSpec((tm, N), lambda i: (i, 0))
    vec = pl.BlockSpec((1, N), lambda i: (0, 0))
    est = tm * K * 2 + K * N * 2 + tm * N * (4 + 4 + 4 + 2 + 8)
    return pl.pallas_call(
        body, name=name, grid=(M // tm,),
        in_specs=[pl.BlockSpec((tm, K), lambda i: (i, 0)), pl.BlockSpec((None, K, N), lambda i: (lead, 0, 0)),
                  row, vec, vec],
        out_specs=[row, row, row],
        out_shape=[_out((M, N), BF16), _out((M, N), F32), _out((M, N), BF16)],
        compiler_params=_params(("parallel",), est),
    )(_hbm(a), _hbm(b), _hbm(xres), _hbm(g), _hbm(bias))


def _ln_bwd_math(alpha, s, x, y, g, d):
    z = alpha * x + s * y.astype(F32)
    zc = z - jnp.mean(z, axis=-1, keepdims=True)
    r = lax.rsqrt(jnp.mean(zc * zc, axis=-1, keepdims=True) + LN_EPS)
    xh = zc * r
    dxh = d * g
    dz = r * (dxh - jnp.mean(dxh, axis=-1, keepdims=True) - xh * jnp.mean(dxh * xh, axis=-1, keepdims=True))
    return alpha * dz, s * dz, jnp.sum(d * xh, axis=0, keepdims=True), jnp.sum(d, axis=0, keepdims=True)


def _mm_ln_bwd(a, b, lead, tb, add, xres, y, g, *, alpha, s, name):
    M, K = a.shape
    N = b.shape[1] if tb else b.shape[2]
    tk = K if K * N * 2 <= MM_VMEM_BUDGET * 3 // 5 else _tile(K, 2816, LANE)
    tm = _tile(M, 512 if K * N * 2 <= MM_VMEM_BUDGET // 4 else 256, 16)
    nk = K // tk
    cb = 1 if tb else 0

    def body(a_ref, b_ref, add_ref, x_ref, y_ref, g_ref, dx_ref, dy_ref, dg_ref, db_ref, *scratch):
        i, k = pl.program_id(0), pl.program_id(1)

        def finish(d):
            @pl.when(i == 0)
            def _():
                dg_ref[...] = jnp.zeros_like(dg_ref)
                db_ref[...] = jnp.zeros_like(db_ref)

            dx, dy, dg, db = _ln_bwd_math(alpha, s, x_ref[...], y_ref[...], g_ref[...], d + add_ref[...])
            dx_ref[...] = dx
            dy_ref[...] = dy.astype(dy_ref.dtype)
            dg_ref[...] += dg
            db_ref[...] += db

        if nk == 1:
            finish(_dg(a_ref[...], b_ref[...], 1, cb))
            return
        acc_ref = scratch[0]

        @pl.when(k == 0)
        def _():
            acc_ref[...] = jnp.zeros_like(acc_ref)

        acc_ref[...] += _dg(a_ref[...], b_ref[...], 1, cb)

        @pl.when(k == nk - 1)
        def _():
            finish(acc_ref[...])

    row = pl.BlockSpec((tm, N), lambda i, k: (i, 0))
    vec = pl.BlockSpec((1, N), lambda i, k: (0, 0))
    b_spec = (pl.BlockSpec((None, N, tk), lambda i, k: (lead, 0, k)) if tb
              else pl.BlockSpec((None, tk, N), lambda i, k: (lead, k, 0)))
    est = tm * tk * 2 + tk * N * 2 + tm * N * (4 + 4 + 2 + 4 + 2 + 4 + 12)
    return pl.pallas_call(
        body, name=name, grid=(M // tm, nk),
        in_specs=[pl.BlockSpec((tm, tk), lambda i, k: (i, k)), b_spec, row, row, row, vec],
        out_specs=[row, row, vec, vec],
        out_shape=[_out((M, N), F32), _out((M, N), BF16), _out((1, N), F32), _out((1, N), F32)],
        scratch_shapes=[pltpu.VMEM((tm, N), F32)] if nk > 1 else [],
        compiler_params=_params(("arbitrary", "arbitrary"), est),
    )(_hbm(a), _hbm(b), _hbm(add), _hbm(xres), _hbm(y), _hbm(g))


def _loss_ln_bwd(xres, y, g, out, target, *, alpha, s, name):
    T, D = xres.shape
    tm = _tile(T, 256, 16)

    def body(x_ref, y_ref, o_ref, t_ref, g_ref, dx_ref, dy_ref, dg_ref, db_ref, loss_ref):
        @pl.when(pl.program_id(0) == 0)
        def _():
            dg_ref[...] = jnp.zeros_like(dg_ref)
            db_ref[...] = jnp.zeros_like(db_ref)
            loss_ref[...] = jnp.zeros_like(loss_ref)

        err = o_ref[...] - t_ref[...]
        part = 0.5 * jnp.sum(jnp.sum(err * err, axis=1, keepdims=True) / D, axis=0, keepdims=True)
        loss_ref[...] += jnp.broadcast_to(part, loss_ref.shape)
        dx, dy, dg, db = _ln_bwd_math(alpha, s, x_ref[...], y_ref[...], g_ref[...], err / D)
        dx_ref[...] = dx
        dy_ref[...] = dy.astype(dy_ref.dtype)
        dg_ref[...] += dg
        db_ref[...] += db

    row = pl.BlockSpec((tm, D), lambda i: (i, 0))
    vec = pl.BlockSpec((1, D), lambda i: (0, 0))
    return pl.pallas_call(
        body, name=name, grid=(T // tm,),
        in_specs=[row, row, row, row, vec],
        out_specs=[row, row, vec, vec, pl.BlockSpec((8, LANE), lambda i: (0, 0))],
        out_shape=[_out((T, D), F32), _out((T, D), BF16), _out((1, D), F32), _out((1, D), F32),
                   _out((8, LANE), F32)],
        compiler_params=_params(("arbitrary",), 14 * tm * D * 4),
    )(_hbm(xres), _hbm(y), _hbm(out), _hbm(target), _hbm(g))


FFN_TILE = 256


def _interleave(w, axis):
    n = w.shape[axis] // (2 * FFN_TILE)
    shp = w.shape[:axis] + (2, n, FFN_TILE) + w.shape[axis + 1:]
    return jnp.swapaxes(w.reshape(shp), axis, axis + 1).reshape(w.shape)


def _deinterleave(w, axis):
    n = w.shape[axis] // (2 * FFN_TILE)
    shp = w.shape[:axis] + (n, 2, FFN_TILE) + w.shape[axis + 1:]
    return jnp.swapaxes(w.reshape(shp), axis, axis + 1).reshape(w.shape)


def _ffn_up(xb, w13t, lead, *, name, ride=None):
    T, D = xb.shape
    F = w13t.shape[1] // 2
    tc = FFN_TILE
    tm = _tile(T, 2048, 16)

    def body(x_ref, w_ref, h_ref, a_ref):
        h = _dg(x_ref[...], w_ref[...], 1, 1)
        g, u = h[:, :tc], h[:, tc:]
        h_ref[...] = h.astype(h_ref.dtype)
        a_ref[...] = (g * jax.nn.sigmoid(g) * u).astype(a_ref.dtype)

    est = (tm * D + 2 * tc * D + 3 * tm * tc) * 2 + 3 * tm * tc * 4
    (h13, a), gathered = _host_call(
        body, name=name, grid=(T // tm, F // tc),
        in_specs=[pl.BlockSpec((tm, D), lambda i, j: (i, 0)),
                  pl.BlockSpec((None, 2 * tc, D), lambda i, j: (lead, j, 0))],
        out_specs=[pl.BlockSpec((tm, 2 * tc), lambda i, j: (i, j)),
                   pl.BlockSpec((tm, tc), lambda i, j: (i, j))],
        out_shape=[_out((T, 2 * F), BF16), _out((T, F), BF16)],
        args=[_hbm(xb), _hbm(w13t)], sem=("parallel", "parallel"), est=est, ride=ride)
    return h13, a, gathered


def _ffn_down_bwd(dyb, w2, lead, h13, *, name, ride=None):
    T, D = dyb.shape
    F = w2.shape[1]
    tc = FFN_TILE
    tm = _tile(T, 2048, 16)

    def body(dy_ref, w_ref, h_ref, dh_ref):
        d = _dg(dy_ref[...], w_ref[...], 1, 1)
        h = h_ref[...].astype(F32)
        g, u = h[:, :tc], h[:, tc:]
        sig = jax.nn.sigmoid(g)
        gs = g * sig
        dh_ref[...] = jnp.concatenate([d * u * (sig + gs * (1.0 - sig)), d * gs], axis=1).astype(dh_ref.dtype)

    est = (tm * D + tc * D + 4 * tm * tc) * 2 + 6 * tm * tc * 4
    (dh,), rode = _host_call(
        body, name=name, grid=(T // tm, F // tc),
        in_specs=[pl.BlockSpec((tm, D), lambda i, j: (i, 0)),
                  pl.BlockSpec((None, tc, D), lambda i, j: (lead, j, 0)),
                  pl.BlockSpec((tm, 2 * tc), lambda i, j: (i, j))],
        out_specs=[pl.BlockSpec((tm, 2 * tc), lambda i, j: (i, j))],
        out_shape=[_out((T, 2 * F), BF16)],
        args=[_hbm(dyb), _hbm(w2), _hbm(h13)], sem=("parallel", "parallel"), est=est, ride=ride)
    return dh, rode


def _pool_select(parts, pw):
    pg = pw // len(POOL_WINDOWS)
    grp = lax.broadcasted_iota(jnp.int32, parts[0].shape, 1) // pg
    out = parts[3]
    for g in (2, 1, 0):
        out = jnp.where(grp == g, parts[g], out)
    return out


def _pool_count(t0, rows, pw):
    pg = pw // len(POOL_WINDOWS)
    grp = lax.broadcasted_iota(jnp.int32, (rows, pw), 1) // pg
    win = jnp.where(grp == 0, POOL_WINDOWS[0],
                    jnp.where(grp == 1, POOL_WINDOWS[1],
                              jnp.where(grp == 2, POOL_WINDOWS[2], POOL_WINDOWS[3])))
    t = t0 + lax.broadcasted_iota(jnp.int32, (rows, pw), 0)
    return jnp.minimum(t + 1, win).astype(F32)


def _window_sums(ext, up):
    n = ext.shape[0]
    sums, cur, k = [], ext, 1
    for _ in POOL_WINDOWS:
        cur = cur + pltpu.roll(cur, (n - k) if up else k, axis=0)
        sums.append(cur)
        k *= 2
    return sums


def _pool_delta(u, halo, t0):
    tm, pw = u.shape
    ext = jnp.concatenate([halo, u], axis=0)
    sums = [s[POOL_HALO:, :] for s in _window_sums(ext, up=False)]
    return _pool_select(sums, pw) / _pool_count(t0, tm, pw) - u


def _pool_fwd(hin, wbd, scale, cat, *, name):
    T = hin.shape[0]
    pw = wbd.shape[0]
    tm = _tile(T, 256, POOL_HALO)
    per = tm // POOL_HALO

    def body(u_ref, halo_ref, w_ref, s_ref, cat_ref, y_ref):
        i = pl.program_id(0)
        halo = jnp.where(i > 0, halo_ref[...], 0.0)
        d = _pool_delta(u_ref[...], halo, i * tm)
        y_ref[...] = (_dg(d, w_ref[...], 1, 0) * s_ref[...]).astype(y_ref.dtype)

    return pl.pallas_call(
        body, name=name, grid=(T // tm,),
        in_specs=[pl.BlockSpec((tm, pw), lambda i: (i, 0)),
                  pl.BlockSpec((POOL_HALO, pw), lambda i: (jnp.maximum(i * per - 1, 0), 0)),
                  pl.BlockSpec((pw, pw), lambda i: (0, 0)),
                  pl.BlockSpec((1, pw), lambda i: (0, 0)),
                  ANY],
        out_specs=pl.BlockSpec((tm, pw), lambda i: (i, 0)),
        out_shape=_out(cat.shape, cat.dtype),
        input_output_aliases={4: 0},
        compiler_params=_params(("parallel",), 16 * tm * pw * 4),
    )(_hbm(hin), _hbm(hin), _hbm(wbd), _hbm(scale), _hbm(cat))


def _pool_bwd(hin, dcat, wbd, scale, *, name):
    T = hin.shape[0]
    pw = wbd.shape[0]
    tm = _tile(T, 256, POOL_HALO)
    per = tm // POOL_HALO
    nt = T // tm

    def body(u_ref, halo_ref, dy_ref, dyn_ref, w_ref, s_ref, du_ref, dw_ref, ds_ref):
        i = pl.program_id(0)

        @pl.when(i == 0)
        def _():
            dw_ref[...] = jnp.zeros_like(dw_ref)
            ds_ref[...] = jnp.zeros_like(ds_ref)

        halo = jnp.where(i > 0, halo_ref[...], 0.0)
        d = _pool_delta(u_ref[...], halo, i * tm)
        w = w_ref[...]
        sc = s_ref[...]
        dy = dy_ref[...]
        dyn = jnp.where(i < nt - 1, dyn_ref[...], 0.0)
        ds_ref[...] += jnp.sum(dy * _dg(d, w, 1, 0), axis=0, keepdims=True)
        dys = dy * sc
        dw_ref[...] += _dg(d, dys, 0, 0)
        dys_ext = jnp.concatenate([dys, dyn * sc], axis=0)
        dd_ext = _dg(dys_ext, w, 1, 1)
        ddp = dd_ext / _pool_count(i * tm, tm + POOL_HALO, pw)
        sums = [s[:tm, :] for s in _window_sums(ddp, up=True)]
        du_ref[...] = _pool_select(sums, pw) - dd_ext[:tm, :]

    return pl.pallas_call(
        body, name=name, grid=(nt,),
        in_specs=[pl.BlockSpec((tm, pw), lambda i: (i, 0)),
                  pl.BlockSpec((POOL_HALO, pw), lambda i: (jnp.maximum(i * per - 1, 0), 0)),
                  pl.BlockSpec((tm, pw), lambda i: (i, 0)),
                  pl.BlockSpec((POOL_HALO, pw), lambda i: (jnp.minimum((i + 1) * per, nt * per - 1), 0)),
                  pl.BlockSpec((pw, pw), lambda i: (0, 0)),
                  pl.BlockSpec((1, pw), lambda i: (0, 0))],
        out_specs=[pl.BlockSpec((tm, pw), lambda i: (i, 0)),
                   pl.BlockSpec((pw, pw), lambda i: (0, 0)),
                   pl.BlockSpec((1, pw), lambda i: (0, 0))],
        out_shape=[_out((T, pw), F32),
                   _out((pw, pw), F32),
                   _out((1, pw), F32)],
        compiler_params=_params(("arbitrary",), 24 * tm * pw * 4),
    )(_hbm(hin), _hbm(hin), _hbm(dcat), _hbm(dcat), _hbm(wbd), _hbm(scale))


def _rms(x, g):
    return x * lax.rsqrt(jnp.mean(x * x, axis=-1, keepdims=True) + RMS_EPS) * g


def _norms_fn(pw, h, gq, gkv):
    o1 = pw + Q_LORA
    o2 = o1 + KV_LORA
    return (_rms(_cols(h, pw, o1), gq), _rms(_cols(h, o1, o2), gkv), _cols(h, o2, h.shape[1]))


def _norms_bwd(hin, gq, gkv, dcq, dckv, dkpe, du, *, pw, name):
    tm = _tile(hin.shape[0], 256, 16)
    dinp = hin.shape[1]

    def fn(i, tv, pv):
        _, vjp = jax.vjp(functools.partial(_norms_fn, pw), tv[0], pv[0], pv[1])
        dh, dgq, dgkv = vjp((tv[1].astype(F32), tv[2].astype(F32), tv[3].astype(F32)))
        dh = jnp.concatenate([tv[4], dh[:, pw:]], axis=1)
        return (dh,), (dgq, dgkv)

    return _rowwise(fn, [hin, dcq, dckv, dkpe, du], [gq, gkv], [(dinp, BF16)],
                    [((1, Q_LORA), F32), ((1, KV_LORA), F32)], tm=tm, name=name)


def _heads_fn(H, qraw, kv, kpe, rc, rs1, rs2):
    half = QK_ROPE // 2
    scale = (QK_NOPE + QK_ROPE) ** -0.5

    def rope(blk):
        return blk * rc + _lane_roll(blk, -half) * rs1 + _lane_roll(blk, half) * rs2

    krot = rope(kpe)
    qs, ks, vs = [], [], []
    for h in range(H):
        lo = h * HEAD_PAD
        qs += [_cols(qraw, lo, lo + LANE) * scale, rope(_cols(qraw, lo + LANE, lo + HEAD_PAD)) * scale]
        ks += [_cols(kv, lo, lo + LANE), krot]
        vs += [_cols(kv, lo + LANE, lo + HEAD_PAD)]
    return jnp.concatenate(qs, axis=1), jnp.concatenate(ks, axis=1), jnp.concatenate(vs, axis=1)


def _heads_fwd(hin, gq, gkv, tabs, wuq, wukv, *, H, pw, name):
    tm = _tile(hin.shape[0], 256, 16)

    def fn(i, tv, pv):
        cqn, ckvn, kpe = _norms_fn(pw, tv[0], pv[0], pv[1])
        qraw = _dg(cqn, pv[2], 1, 1)
        kv = _dg(ckvn, pv[3], 1, 1)
        return (*_heads_fn(H, qraw, kv, kpe, *tv[1:]), cqn, ckvn), ()

    return _rowwise(fn, [hin, *tabs], [gq, gkv, wuq, wukv],
                    [(H * HEAD_PAD, BF16), (H * HEAD_PAD, BF16), (H * V_HEAD, BF16), (Q_LORA, BF16),
                     (KV_LORA, BF16)], tm=tm, name=name)


def _heads_bwd(dq, dk, dv, tabs, *, H, name):
    tm = _tile(dq.shape[0], 256, 16)

    def fn(i, tv, pv):
        z = jnp.zeros((tm, H * HEAD_PAD), F32)
        zk = jnp.zeros((tm, LANE), F32)
        rc, rs1, rs2 = tv[3], tv[4], tv[5]
        _, vjp = jax.vjp(lambda a, b, c: _heads_fn(H, a, b, c, rc, rs1, rs2), z, z, zk)
        return vjp((tv[0].astype(F32), tv[1].astype(F32), tv[2].astype(F32))), ()

    return _rowwise(fn, [dq, dk, dv, *tabs], [],
                    [(H * HEAD_PAD, BF16), (H * HEAD_PAD, BF16), (LANE, F32)], tm=tm, name=name)


def _diag_mask(rows, cols, row0):
    r = (row0 + lax.broadcasted_iota(jnp.int32, (rows, cols), 0)) // CHUNK
    c = lax.broadcasted_iota(jnp.int32, (rows, cols), 1) // CHUNK
    return r >= c


def _flash_fwd(qh, kh, vh, *, H, pw, name, ride=None):
    T = qh.shape[0]
    t = _tile(T, 512, CHUNK)
    off = pw // V_HEAD


    half = t // 2 if t % (2 * CHUNK) == 0 else t

    def body(q_ref, k_ref, v_ref, o_ref, lse_ref):
        i = pl.program_id(1)
        q = q_ref[...]

        def update(carry, s, v):
            m, l, acc = carry
            mn = jnp.maximum(m, jnp.max(s, axis=1, keepdims=True))
            p = jnp.exp(s - mn)
            corr = jnp.exp(m - mn)
            return mn, corr * l + jnp.sum(p, axis=1, keepdims=True), corr * acc + _dg(p, v, 1, 0)

        def blk(j, carry):
            rows = pl.ds(pl.multiple_of(j * t, t), t)
            return update(carry, _dg(q, k_ref[rows, :], 1, 1), v_ref[rows, :])

        init = (jnp.full((t, 1), NEG_INF, F32), jnp.zeros((t, 1), F32), jnp.zeros((t, V_HEAD), F32))
        carry = lax.fori_loop(0, i, blk, init)
        done = []
        for r0 in range(0, t, half):
            keys = pl.ds(pl.multiple_of(i * t, t), r0 + half)
            s = _dg(q[r0:r0 + half, :], k_ref[keys, :], 1, 1)
            s = jnp.where(_diag_mask(half, r0 + half, r0), s, NEG_INF)
            done.append(update(tuple(c[r0:r0 + half] for c in carry), s, v_ref[keys, :]))
        m, l, acc = (jnp.concatenate(parts, axis=0) for parts in zip(*done))
        o_ref[...] = (acc / l).astype(o_ref.dtype)
        lse_ref[...] = jnp.broadcast_to(m + jnp.log(l), (t, V_HEAD))

    est = 2 * T * (HEAD_PAD + V_HEAD) * 2 + 8 * t * t * 4
    (o, lse), gathered = _host_call(
        body, name=name, grid=(H, T // t),
        in_specs=[pl.BlockSpec((t, HEAD_PAD), lambda h, i: (i, h)),
                  pl.BlockSpec((T, HEAD_PAD), lambda h, i: (0, h)),
                  pl.BlockSpec((T, V_HEAD), lambda h, i: (0, h))],
        out_specs=[pl.BlockSpec((t, V_HEAD), lambda h, i: (i, off + h)),
                   pl.BlockSpec((t, V_HEAD), lambda h, i: (i, h))],
        out_shape=[_out((T, pw + H * V_HEAD), BF16),
                   _out((T, H * V_HEAD), F32)],
        args=[_hbm(qh), _hbm(kh), _hbm(vh)], sem=("parallel", "parallel"), est=est, ride=ride)
    return o, lse, gathered


def _flash_bwd(qh, kh, vh, cat, dcat, lse, *, H, pw, name, ride=None):
    T = qh.shape[0]
    t = _tile(T, 512, CHUNK)
    nb = T // t
    off = pw // V_HEAD
    half = t // 2 if t % (2 * CHUNK) == 0 else t

    def body(q_ref, k_ref, v_ref, o_ref, do_ref, lse_ref, dq_out_ref, dk_ref, dv_ref, dq_ref):
        j = pl.program_id(1)

        @pl.when(j == 0)
        def _():
            dq_ref[...] = jnp.zeros_like(dq_ref)

        kj = k_ref[...]
        vj = v_ref[...]

        def pair(rows, kx, vx, mask):
            qi = q_ref[rows, :]
            doi = do_ref[rows, :]
            oi = o_ref[rows, :].astype(F32)
            lsei = lse_ref[rows, :][:, :1]
            s = _dg(qi, kx, 1, 1)
            if mask is not None:
                s = jnp.where(mask, s, NEG_INF)
            p = jnp.exp(s - lsei)
            dp = _dg(doi, vx, 1, 1)
            di = jnp.sum(doi * oi, axis=1, keepdims=True)
            ds = p * (dp - di)
            dq_ref[rows, :] += _dg(ds, kx, 1, 0)
            return _dg(ds, qi, 0, 0), _dg(p, doi, 0, 0)

        def blk(i, carry):
            dk, dv = pair(pl.ds(pl.multiple_of(i * t, t), t), kj, vj, None)
            return carry[0] + dk, carry[1] + dv

        dk, dv = jnp.zeros((t, HEAD_PAD), F32), jnp.zeros((t, V_HEAD), F32)
        for r0 in range(0, t, half):
            n = r0 + half
            dkp, dvp = pair(pl.ds(pl.multiple_of(j * t + r0, half), half), kj[:n], vj[:n],
                            _diag_mask(half, n, r0))
            if n < t:
                dkp = jnp.concatenate([dkp, jnp.zeros((t - n, HEAD_PAD), F32)], axis=0)
                dvp = jnp.concatenate([dvp, jnp.zeros((t - n, V_HEAD), F32)], axis=0)
            dk, dv = dk + dkp, dv + dvp
        dk, dv = lax.fori_loop(j + 1, nb, blk, (dk, dv))
        dk_ref[...] = dk.astype(dk_ref.dtype)
        dv_ref[...] = dv.astype(dv_ref.dtype)

        @pl.when(j == nb - 1)
        def _():
            dq_out_ref[...] = dq_ref[...].astype(dq_out_ref.dtype)

    est = T * (HEAD_PAD * 2 + V_HEAD * 2 + V_HEAD * 4 + V_HEAD * 4 + HEAD_PAD * 4) + 10 * t * t * 4
    (dq, dk, dv), gathered = _host_call(
        body, name=name, grid=(H, nb),
        in_specs=[pl.BlockSpec((T, HEAD_PAD), lambda h, j: (0, h)),
                  pl.BlockSpec((t, HEAD_PAD), lambda h, j: (j, h)),
                  pl.BlockSpec((t, V_HEAD), lambda h, j: (j, h)),
                  pl.BlockSpec((T, V_HEAD), lambda h, j: (0, off + h)),
                  pl.BlockSpec((T, V_HEAD), lambda h, j: (0, off + h)),
                  pl.BlockSpec((T, V_HEAD), lambda h, j: (0, h))],
        out_specs=[pl.BlockSpec((T, HEAD_PAD), lambda h, j: (0, h)),
                   pl.BlockSpec((t, HEAD_PAD), lambda h, j: (j, h)),
                   pl.BlockSpec((t, V_HEAD), lambda h, j: (j, h))],
        out_shape=[_out((T, H * HEAD_PAD), BF16),
                   _out((T, H * HEAD_PAD), BF16),
                   _out((T, H * V_HEAD), BF16)],
        scratch=[pltpu.VMEM((T, HEAD_PAD), F32)],
        args=[_hbm(v) for v in (qh, kh, vh, cat, dcat, lse)], sem=("arbitrary", "arbitrary"), est=est,
        ride=ride)
    return dq, dk, dv, gathered


def _mem_fn(q, k, v):
    hd = q.shape[1] // MEM_HEADS
    outs = []
    for h in range(MEM_HEADS):
        lo, hi = h * hd, (h + 1) * hd
        s = _bdot_nt(_cols(q, lo, hi), _cols(k, lo, hi)) * hd ** -0.5
        e = jnp.exp(s - lax.stop_gradient(jnp.max(s, axis=1, keepdims=True)))
        p = e / jnp.sum(e, axis=1, keepdims=True)
        outs.append(_bdot_nn(p, _cols(v, lo, hi)))
    return jnp.concatenate(outs, axis=1)


def _mem_fwd(q, k, v, *, name):
    T, D = q.shape
    tm = _tile(T, 256, 16)

    def fn(i, tv, pv):
        return (_mem_fn(tv[0], pv[0], pv[1]),), ()

    return _rowwise(fn, [q], [k, v], [(D, BF16)], tm=tm, name=name)[0]


def _mem_bwd(q, k, v, do, *, name):
    T, D = q.shape
    tm = _tile(T, 256, 16)

    def fn(i, tv, pv):
        _, vjp = jax.vjp(_mem_fn, tv[0], pv[0], pv[1])
        dq, dk, dv = vjp(tv[1].astype(F32))
        return (dq,), (dk, dv)

    return _rowwise(fn, [q, do], [k, v], [(D, BF16)], [(k.shape, F32), (v.shape, F32)], tm=tm, name=name)


def _adamw(w, g, m, v, *, name):
    shape = w.shape
    if w.ndim != 3:
        lead3 = (1, math.prod(shape[:-1]), shape[-1])
        return [o.reshape(shape) for o in _adamw(*[a.reshape(lead3) for a in (w, g, m, v)], name=name)]
    Lw, R, C = shape
    tr = _tile(R, 512, 8)
    b1c = 1.0 - ADAM_B1 ** ADAM_STEP
    b2c = 1.0 - ADAM_B2 ** ADAM_STEP

    def body(w_ref, g_ref, m_ref, v_ref, d_ref, mo_ref, vo_ref):
        gg = g_ref[...]
        mn = ADAM_B1 * m_ref[...] + (1.0 - ADAM_B1) * gg
        vn = ADAM_B2 * v_ref[...] + (1.0 - ADAM_B2) * (gg * gg)
        d_ref[...] = -ADAM_LR * ((mn / b1c) / (jnp.sqrt(vn / b2c) + ADAM_EPS) + ADAM_WD * w_ref[...])
        mo_ref[...] = mn
        vo_ref[...] = vn

    spec = pl.BlockSpec((None, tr, C), lambda l, i: (l, i, 0))
    return pl.pallas_call(
        body, name=name, grid=(Lw, R // tr),
        in_specs=[spec] * 4, out_specs=[spec] * 3,
        out_shape=[_out(shape, F32)] * 3,
        compiler_params=_params(("parallel", "parallel"), 7 * tr * C * 4),
    )(*[_hbm(a) for a in (w, g, m, v)])


def _pair_sum(core, gs, landed, offs, *, name):
    n = len(gs)
    _, R, C = landed.shape
    rows = [g.shape[0] // N_DEV for g in gs]

    def body(core_ref, *refs):
        g_refs, l_ref, o_ref = refs[:n], refs[n], refs[n + 1]
        for g_ref, off, r in zip(g_refs, offs, rows):
            o_ref[off:off + r, :] = (g_ref[...].astype(F32) + l_ref[off:off + r, :].astype(F32)).astype(o_ref.dtype)

    slab = pl.BlockSpec((None, R, C), lambda p, core_ref: (p, 0, 0))
    own = [pl.BlockSpec((r, C), lambda p, core_ref: (2 * p + core_ref[0], 0)) for r in rows]
    return pl.pallas_call(
        body, name=name,
        grid_spec=pltpu.PrefetchScalarGridSpec(
            num_scalar_prefetch=1, grid=(4,), in_specs=own + [slab], out_specs=slab),
        out_shape=_out(landed.shape, landed.dtype),
        input_output_aliases={n + 1: 0},
        compiler_params=_params(("arbitrary",), 3 * R * C * 2 + R * C * 8),
    )(core, *[_hbm(g) for g in gs], _hbm(landed))


def _quad_sum(chip, part, gathered, used, *, name):
    C = part.shape[2]
    R = used
    tr = _tile(R, 256, 16)

    def body(chip_ref, own_ref, a_ref, b_ref, c_ref, o_ref):
        o_ref[...] = ((own_ref[...].astype(F32) + a_ref[...].astype(F32)) + b_ref[...].astype(F32)) \
            + c_ref[...].astype(F32)

    def other(k):
        return pl.BlockSpec((None, tr, C), lambda i, chip_ref: (chip_ref[0] ^ k, i, 0))

    return pl.pallas_call(
        body, name=name,
        grid_spec=pltpu.PrefetchScalarGridSpec(
            num_scalar_prefetch=1, grid=(R // tr,),
            in_specs=[pl.BlockSpec((None, tr, C), lambda i, chip_ref: (chip_ref[0], i, 0)),
                      other(1), other(2), other(3)],
            out_specs=pl.BlockSpec((tr, C), lambda i, chip_ref: (i, 0))),
        out_shape=_out((R, C), F32),
        compiler_params=_params(("arbitrary",), 8 * tr * C * 4),
    )(chip, _hbm(part), _hbm(gathered), _hbm(gathered), _hbm(gathered))


def _place():
    x, y, c = lax.axis_index("x"), lax.axis_index("y"), lax.axis_index("c")
    return x, y, c


ANY = pl.BlockSpec(memory_space=pl.ANY)


class _Gather:
    def __init__(self, shards):
        self.shards = list(shards)
        self.n = len(self.shards)
        self.out_shape = [_out((s.shape[0], N_DEV * s.shape[1], s.shape[2]), s.dtype)
                          for s in self.shards]
        self.scratch = [pltpu.SemaphoreType.DMA((7 * self.n,)), pltpu.SemaphoreType.DMA((7 * self.n,)),
                        pltpu.SemaphoreType.DMA((self.n,))]
        self.operands = [_hbm(s) for s in self.shards]

    def _bind(self, refs):
        n = self.n
        ins, outs = refs[:n], refs[n:2 * n]
        send_sems, recv_sems, local_sems = refs[2 * n:]
        x, y, c = _place()
        me, sib = (x, y, c), (x, y, 1 - c)
        chips = [(1 - x, y), (x, 1 - y), (1 - x, 1 - y)]

        def rows(w, p):
            r = self.shards[w].shape[1]
            idx = 4 * p[0] + 2 * p[1] + p[2]
            return outs[w].at[:, pl.ds(pl.multiple_of(idx * r, 8), r), :]

        def copy(w, k, block, to, src=None):
            return pltpu.make_async_remote_copy(
                src_ref=rows(w, block) if src is None else src, dst_ref=rows(w, block),
                send_sem=send_sems.at[w * 7 + k], recv_sem=recv_sems.at[w * 7 + k],
                device_id=to, device_id_type=MESH)

        def mine():
            return [pltpu.make_async_copy(ins[w], rows(w, me), local_sems.at[w]) for w in range(n)]

        def first():
            out = []
            for w in range(n):
                out.append(copy(w, 0, me, sib, src=ins[w]))
                out += [copy(w, 1 + j, me, (*chip, c), src=ins[w]) for j, chip in enumerate(chips)]
            return out

        def passed():
            return [copy(w, 4 + j, (*chip, c), sib) for j, chip in enumerate(chips) for w in range(n)]

        def landed():
            return [copy(w, 1 + j, (*chip, c), me) for j, chip in enumerate(chips) for w in range(n)]

        def last():
            out = []
            for w in range(n):
                out.append(copy(w, 0, sib, me))
                out += [copy(w, 4 + j, (*chip, 1 - c), me) for j, chip in enumerate(chips)]
            return out

        return mine, first, landed, passed, last

    def start(self, refs):
        mine, first, _, _, _ = self._bind(refs)
        for cp in mine() + first():
            cp.start()

    def forward(self, refs):
        _, _, landed, passed, _ = self._bind(refs)
        for arrived, fwd in zip(landed(), passed()):
            arrived.wait_recv()
            fwd.start()

    def finish(self, refs):
        mine, first, _, passed, last = self._bind(refs)
        for cp in last():
            cp.wait_recv()
        for cp in first() + passed():
            cp.wait_send()
        for cp in mine():
            cp.wait()


class _ChipExchange:
    def __init__(self, parts, used):
        self.ncl = len(parts)
        self.used = list(used)
        self.out_shape = [_out(p.shape, p.dtype) for p in parts]
        self.scratch = [pltpu.SemaphoreType.DMA((3 * self.ncl,)), pltpu.SemaphoreType.DMA((3 * self.ncl,))]
        self.operands = [_hbm(p) for p in parts]
        self.n = self.ncl

    def _bind(self, refs):
        ncl = self.ncl
        ins, outs = refs[:ncl], refs[ncl:2 * ncl]
        send_sems, recv_sems = refs[2 * ncl:]
        x, y, c = _place()
        chips = [(1 - x, y), (x, 1 - y), (1 - x, 1 - y)]
        here = 2 * x + y

        def copies(outgoing):
            out = []
            for k in range(ncl):
                rows = pl.ds(0, self.used[k])
                for j, (cx, cy) in enumerate(chips):
                    there = 2 * cx + cy
                    src, dst = (there, here) if outgoing else (here, there)
                    out.append(pltpu.make_async_remote_copy(
                        src_ref=ins[k].at[src, rows, :], dst_ref=outs[k].at[dst, rows, :],
                        send_sem=send_sems.at[3 * k + j], recv_sem=recv_sems.at[3 * k + j],
                        device_id=(cx, cy, c), device_id_type=MESH))
            return out

        return copies

    def start(self, refs):
        for cp in self._bind(refs)(True):
            cp.start()

    def forward(self, refs):
        pass

    def finish(self, refs):
        copies = self._bind(refs)
        for cp in copies(False):
            cp.wait_recv()
        for cp in copies(True):
            cp.wait_send()


class _Both:
    def __init__(self, members):
        self.members = list(members)
        self.n = sum(m.n for m in self.members)
        self.out_shape = [s for m in self.members for s in m.out_shape]
        self.scratch = [s for m in self.members for s in m.scratch]
        self.operands = [o for m in self.members for o in m.operands]

    def split(self, arrays):
        out, a = [], 0
        for m in self.members:
            out.append(list(arrays[a:a + m.n]))
            a += m.n
        return out

    def _refs(self, refs):
        ins, outs = self.split(refs[:self.n]), self.split(refs[self.n:2 * self.n])
        scr, b = [], 2 * self.n
        for m in self.members:
            scr.append(list(refs[b:b + len(m.scratch)]))
            b += len(m.scratch)
        return [(*i, *o, *s) for i, o, s in zip(ins, outs, scr)]

    def start(self, refs):
        for m, r in zip(self.members, self._refs(refs)):
            m.start(r)

    def forward(self, refs):
        for m, r in zip(self.members, self._refs(refs)):
            m.forward(r)

    def finish(self, refs):
        for m, r in zip(self.members, self._refs(refs)):
            m.finish(r)


def _exchange_alone(ex, *, name):
    def body(*refs):
        ex.start(refs)
        ex.forward(refs)
        ex.finish(refs)

    return pl.pallas_call(
        body, name=name, in_specs=[ANY] * ex.n, out_specs=[ANY] * ex.n,
        out_shape=ex.out_shape, scratch_shapes=ex.scratch,
    )(*ex.operands)


def _host_call(body, *, name, grid, in_specs, out_specs, out_shape, args, sem, est, ride=None, scratch=()):
    scratch = list(scratch)
    if ride is None:
        outs = pl.pallas_call(body, name=name, grid=grid, in_specs=in_specs, out_specs=out_specs,
                              out_shape=out_shape, scratch_shapes=scratch,
                              compiler_params=_params(sem, est))(*args)
        return list(outs), []
    n_in, n_out, n, n_scr = len(in_specs), len(out_specs), ride.n, len(scratch)

    def full(*refs):
        ins, rin = refs[:n_in], refs[n_in:n_in + n]
        outs, rout = refs[n_in + n:n_in + n + n_out], refs[n_in + n + n_out:n_in + 2 * n + n_out]
        own = refs[n_in + 2 * n + n_out:n_in + 2 * n + n_out + n_scr]
        rrefs = (*rin, *rout, *refs[n_in + 2 * n + n_out + n_scr:])
        step, total = _ride(ride, rrefs, grid)
        body(*ins, *outs, *own)
        _ride_end(ride, rrefs, step, total)

    outs = pl.pallas_call(
        full, name=name, grid=grid,
        in_specs=list(in_specs) + [ANY] * n, out_specs=list(out_specs) + [ANY] * n,
        out_shape=list(out_shape) + ride.out_shape, scratch_shapes=scratch + ride.scratch,
        compiler_params=_params(("arbitrary",) * len(grid), est),
    )(*args, *ride.operands)
    return list(outs[:n_out]), list(outs[n_out:])


def _ride(ex, refs, grid):
    total = math.prod(grid)
    step = pl.program_id(0)
    for axis in range(1, len(grid)):
        step = step * grid[axis] + pl.program_id(axis)
    pl.when(step == 0)(lambda: ex.start(refs))
    return step, total


def _ride_end(ex, refs, step, total):
    pl.when(step == (3 * total) // 4)(lambda: ex.forward(refs))
    pl.when(step == total - 1)(lambda: ex.finish(refs))


def _class_layout(grads, classes):
    used = [0] * len(set(classes))
    offs = []
    for g, cl in zip(grads, classes):
        offs.append(used[cl])
        used[cl] += g.shape[0] // N_DEV
    return offs, used


def _rs_to_sibling(grads, classes, *, name):
    n = len(grads)
    offs, used = _class_layout(grads, classes)
    heights = used
    ncl = len(heights)
    cols = [next(g.shape[1] for g, cl in zip(grads, classes) if cl == k) for k in range(ncl)]

    def body(*refs):
        gs, land = refs[:n], refs[n:n + ncl]
        send_sems, recv_sems = refs[n + ncl:]
        x, y, c = _place()
        sib = (x, y, 1 - c)
        for p in range(4):
            for w in range(n):
                r = grads[w].shape[0] // N_DEV
                cl = classes[w]
                there = gs[w].at[pl.ds(pl.multiple_of((2 * p + 1 - c) * r, 8), r), :]
                pltpu.make_async_remote_copy(
                    src_ref=there, dst_ref=land[cl].at[p, pl.ds(offs[w], r), :],
                    send_sem=send_sems.at[cl * 4 + p], recv_sem=recv_sems.at[cl * 4 + p],
                    device_id=sib, device_id_type=MESH).start()
        for cl in range(ncl):
            for p in range(4):
                rows_used = land[cl].at[p, pl.ds(0, used[cl]), :]
                slab = pltpu.make_async_remote_copy(
                    src_ref=rows_used, dst_ref=rows_used,
                    send_sem=send_sems.at[cl * 4 + p], recv_sem=recv_sems.at[cl * 4 + p],
                    device_id=sib, device_id_type=MESH)
                slab.wait_send()
                slab.wait_recv()

    return pl.pallas_call(
        body, name=name,
        in_specs=[ANY] * n, out_specs=[ANY] * ncl,
        out_shape=[_out((4, heights[k], cols[k]), BF16) for k in range(ncl)],
        scratch_shapes=[pltpu.SemaphoreType.DMA((4 * ncl,))] * 2,
    )(*[_hbm(g) for g in grads])


def _all_reduce_small(v, *, name):
    R = v.shape[0]

    def body(v_ref, o_ref, buf, send_sems, recv_sems):
        x, y, c = _place()
        me = 4 * x + 2 * y + c
        buf[me] = v_ref[...]
        copies = []
        for k in range(1, N_DEV):
            fx, fy, fc = (k >> 2) & 1, (k >> 1) & 1, k & 1
            to = (x ^ fx, y ^ fy, c ^ fc)
            cp = pltpu.make_async_remote_copy(
                src_ref=v_ref, dst_ref=buf.at[me],
                send_sem=send_sems.at[k - 1], recv_sem=recv_sems.at[k - 1],
                device_id=to, device_id_type=MESH)
            cp.start()
            copies.append(cp)
        for k in range(1, N_DEV):
            fx, fy, fc = (k >> 2) & 1, (k >> 1) & 1, k & 1
            frm = 4 * (x ^ fx) + 2 * (y ^ fy) + (c ^ fc)
            pltpu.make_async_remote_copy(
                src_ref=v_ref, dst_ref=buf.at[frm],
                send_sem=send_sems.at[k - 1], recv_sem=recv_sems.at[k - 1],
                device_id=(x ^ fx, y ^ fy, c ^ fc), device_id_type=MESH).wait_recv()
        for cp in copies:
            cp.wait_send()
        acc = buf[0]
        for d in range(1, N_DEV):
            acc = acc + buf[d]
        o_ref[...] = acc

    vm = pl.BlockSpec(memory_space=pltpu.VMEM)
    return pl.pallas_call(
        body, name=name, in_specs=[vm], out_specs=vm,
        out_shape=jax.ShapeDtypeStruct((R, LANE), F32),
        scratch_shapes=[pltpu.VMEM((N_DEV, R, LANE), F32),
                        pltpu.SemaphoreType.DMA((N_DEV - 1,)), pltpu.SemaphoreType.DMA((N_DEV - 1,))],
        compiler_params=pltpu.CompilerParams(vmem_limit_bytes=VMEM_FLOOR),
    )(v)


def _rope_tables(positions):
    half = QK_ROPE // 2
    inv_freq = ROPE_BASE ** (-jnp.arange(half, dtype=F32) / half)
    ang = positions.astype(F32)[:, None] * inv_freq
    cos, sin = jnp.cos(ang), jnp.sin(ang)
    z = jnp.zeros_like(cos)
    z2 = jnp.zeros((positions.shape[0], LANE - QK_ROPE), F32)
    rc = jnp.concatenate([cos, cos, z2], axis=1)
    rs1 = jnp.concatenate([-sin, z, z2], axis=1)
    rs2 = jnp.concatenate([z, sin, z2], axis=1)
    return rc, rs1, rs2


def _block_diag(pool_w):
    G, pg, _ = pool_w.shape
    out = jnp.zeros((G * pg, G * pg), pool_w.dtype)
    for g in range(G):
        out = lax.dynamic_update_slice(out, pool_w[g], (g * pg, g * pg))
    return out


def kernel(x, mem, positions, ln_g, ln_b, ffn1_w13, ffn1_w2, w_in, pool_w, pool_scale, q_norm_g, w_uq, kv_norm_g, w_ukv, w_out, mem_wq, mem_wkv, mem_wo, ffn2_w13, ffn2_w2, loss_target, m_ln_g, m_ln_b, m_ffn1_w13, m_ffn1_w2, m_w_in, m_pool_w, m_pool_scale, m_q_norm_g, m_w_uq, m_kv_norm_g, m_w_ukv, m_w_out, m_mem_wq, m_mem_wkv, m_mem_wo, m_ffn2_w13, m_ffn2_w2, v_ln_g, v_ln_b, v_ffn1_w13, v_ffn1_w2, v_w_in, v_pool_w, v_pool_scale, v_q_norm_g, v_w_uq, v_kv_norm_g, v_w_ukv, v_w_out, v_mem_wq, v_mem_wkv, v_mem_wo, v_ffn2_w13, v_ffn2_w2):
    L = ln_g.shape[0]
    T, D = x.shape[1], x.shape[2]
    F = ffn1_w2.shape[1] * N_DEV
    PW = D // 4
    H = (D - PW) // V_HEAD
    DIN = w_in.shape[2]
    DINP = PW + Q_LORA + KV_LORA + LANE
    QW = QK_NOPE + QK_ROPE
    alpha = (2 * L) ** 0.25
    x2d = x.reshape(T, D)
    memb = mem.reshape(mem.shape[1], D).astype(BF16)
    target = loss_target.reshape(T, D)
    tabs = _rope_tables(positions.reshape(T))

    def shards_of(l):
        return dict(
            w13a=ffn1_w13[l].T[None].astype(BF16),
            w13b=ffn2_w13[l].T[None].astype(BF16),
            w2a=ffn1_w2[l][None].astype(BF16),
            w2b=ffn2_w2[l][None].astype(BF16),
            wsq=jnp.stack([w_out[l], mem_wq[l], mem_wo[l]]).astype(BF16),
            wkvT=mem_wkv[l].T[None].astype(BF16),
            winp=jnp.pad(w_in[l], ((0, 0), (0, DINP - DIN)))[None].astype(BF16),
            wuqT=w_uq[l].T[None].astype(BF16),
            wukvT=w_ukv[l].T[None].astype(BF16),
        )

    SMALL = ("winp", "wuqT", "wukvT")
    shards = [shards_of(l) for l in range(L)]
    W = [dict() for _ in range(L)]

    def rider(spec):
        return _Gather([shards[l][n] for l, n in spec]) if spec else None

    def arrived(spec, arrays):
        for (l, n), a in zip(spec, arrays):
            if n in ("w13a", "w13b"):
                a = _interleave(a, 1)
            elif n == "wuqT":
                a = jnp.pad(a.reshape(H, QW, Q_LORA), ((0, 0), (0, HEAD_PAD - QW), (0, 0)))
                a = a.reshape(1, H * HEAD_PAD, Q_LORA)
            elif n == "ln":
                a = jnp.moveaxis(a.reshape(N_DEV, 2, L, 4, D // N_DEV), 0, 3).reshape(2, L, 4, D)
                LN["g"], LN["b"] = a[0], a[1]
            W[l][n] = a

    LN = {}
    shards[0]["ln"] = jnp.concatenate([ln_g.reshape(1, 4 * L, -1), ln_b.reshape(1, 4 * L, -1)], axis=1)
    spec0 = [(0, "w13a")]
    arrived(spec0, _exchange_alone(rider(spec0), name="ag_first"))
    wbd = [_block_diag(pool_w[l]).astype(BF16) for l in range(L)]

    def ffn_fwd(l, which, xres, xb, k, spec):
        ab = "ab"[which]
        h13, a, rode = _ffn_up(xb, W[l]["w13" + ab], 0, name=f"l{l}_ffn{which}_up", ride=rider(spec))
        arrived(spec, rode)
        y, xo, xob = _mm_ln(a, W[l]["w2" + ab], 0, xres, LN["g"][l,k:k + 1], LN["b"][l,k:k + 1], alpha=alpha, s=0.5,
                            name=f"l{l}_ffn{which}_y_ln{k}")
        return dict(xres=xres, xb=xb, h13=h13, a=a, y=y), xo, xob

    saved = []
    xres, xb = x2d, x2d.astype(BF16)
    for l in range(L):
        sv = {}
        more = l + 1 < L
        Wl = W[l]
        spec = ([(0, "w2a"), (0, "ln"), *[(0, n) for n in SMALL], (0, "wkvT")] if l == 0
                else [(l, "wsq"), (l, "wkvT")])
        sv["ffn1"], x1, x1b = ffn_fwd(l, 0, xres, xb, 0, spec)
        hin = _mm(x1b, Wl["winp"], lead=0, name=f"l{l}_hin")
        pscale = pool_scale[l].reshape(1, PW)
        gq, gkv = q_norm_g[l].reshape(1, Q_LORA), kv_norm_g[l].reshape(1, KV_LORA)
        qh, kh, vh, cqn, ckvn = _heads_fwd(hin, gq, gkv, tabs, Wl["wuqT"][0], Wl["wukvT"][0], H=H, pw=PW,
                                           name=f"l{l}_heads")
        spec = [(l, "w13b"), (l, "w2b")] + ([(0, "wsq")] if l == 0 else []) + ([(l + 1, "w13a")] if more else [])
        cat, lse, rode = _flash_fwd(qh, kh, vh, H=H, pw=PW, name=f"l{l}_flash", ride=rider(spec))
        arrived(spec, rode)
        cat = _pool_fwd(hin, wbd[l], pscale, cat, name=f"l{l}_pool")
        ymix, x2, x2b = _mm_ln(cat, Wl["wsq"], 0, x1, LN["g"][l,1:2], LN["b"][l,1:2], alpha=alpha, s=1.0,
                               name=f"l{l}_ymix_ln1")
        qm = _mm(x2b, Wl["wsq"], lead=1, out_dtype=BF16, name=f"l{l}_qm")
        kvm = _mm(memb, Wl["wkvT"], lead=0, tb=True, name=f"l{l}_kvm")
        km, vm = kvm[:, :D], kvm[:, D:]
        om = _mem_fwd(qm, km, vm, name=f"l{l}_memattn")
        ymem, x3, x3b = _mm_ln(om, Wl["wsq"], 2, x2, LN["g"][l,2:3], LN["b"][l,2:3], alpha=alpha, s=1.0,
                               name=f"l{l}_ymem_ln2")
        spec = [(l + 1, n) for n in ("w2a", *SMALL)] if more else []
        sv["ffn2"], x4, x4b = ffn_fwd(l, 1, x3, x3b, 3, spec)
        sv.update(x1=x1, x1b=x1b, hin=hin, pscale=pscale, gq=gq, gkv=gkv, cqn=cqn, ckvn=ckvn,
                  qh=qh, kh=kh, vh=vh, lse=lse, cat=cat, ymix=ymix, x2=x2, x2b=x2b, qm=qm, km=km, vm=vm,
                  om=om, ymem=ymem)
        saved.append(sv)
        xres, xb = x4, x4b


    gW = {}
    gS = {}

    def ln_of(l, k):
        sv = saved[l]
        x, y, s = {0: (sv["ffn1"]["xres"], sv["ffn1"]["y"], 0.5), 1: (sv["x1"], sv["ymix"], 1.0),
                   2: (sv["x2"], sv["ymem"], 1.0), 3: (sv["ffn2"]["xres"], sv["ffn2"]["y"], 0.5)}[k]
        return x, y, LN["g"][l, k:k + 1], s

    def dx_through_ln(a, b, lead, tb, add, into, name):
        x, y, g, s = ln_of(*into)
        dxres, dyb, dg, db = _mm_ln_bwd(a, b, lead, tb, add, x, y, g, alpha=alpha, s=s, name=name)
        gS[("ln_g", *into)], gS[("ln_b", *into)] = dg, db
        return dxres, dyb

    def ffn_bwd(l, which, sv, dxres, dyb, ride, into):
        tag = f"l{l}_ffn{which}"
        gW[("w2", which, l)] = _mm(sv["a"], dyb, ta=True, out_dtype=BF16, name=f"{tag}_dw2", tn=D)
        dh, rode = _ffn_down_bwd(dyb, W[l]["w2" + "ab"[which]], 0, sv["h13"], name=f"{tag}_dh", ride=ride)
        dw13 = _mm(dh, sv["xb"], ta=True, out_dtype=BF16, name=f"{tag}_dw13", tn=D)
        gW[("w13", which, l)] = _deinterleave(dw13, 0)
        w13 = W[l]["w13" + "ab"[which]]
        if into is not None:
            return dx_through_ln(dh, w13, 0, False, dxres, into, f"{tag}_dx"), rode
        last = rs_first_level(l, "c")
        dxn, got = _mm(dh, w13, lead=0, add=dxres, name=f"{tag}_dx", tn=D, ride=last["ex"])
        rs_last_level(last, got)
        return dxn, rode

    core = lax.axis_index("c").astype(jnp.int32).reshape(1)
    chip = (2 * lax.axis_index("x") + lax.axis_index("y")).astype(jnp.int32).reshape(1)
    gsh = {}

    def rs_first_level(l, group):
        keys, classes = {
            "a": ([("w13", 1, l), ("w2", 1, l), ("mem_wkv", l), ("mem_wq", l), ("mem_wo", l)], [0] * 5),
            "b": ([("w_out", l), ("w_in", l), ("w_uq", l), ("w_ukv", l)], [0, 1, 2, 3]),
            "c": ([("w13", 0, l), ("w2", 0, l)], [0, 0]),
        }[group]
        tag = f"l{l}{group}"
        garrs = []
        for key in keys:
            g = gW[key]
            if key[0] == "w_uq":
                g = g.reshape(H, HEAD_PAD, Q_LORA)[:, :QW, :].reshape(H * QW, Q_LORA)
            garrs.append(g)
        offs, used = _class_layout(garrs, classes)
        parts = list(_rs_to_sibling(garrs, classes, name=f"{tag}_rs_sibling"))
        for cl in range(len(parts)):
            mine = [w for w, c in enumerate(classes) if c == cl]
            parts[cl] = _pair_sum(core, [garrs[w] for w in mine], parts[cl], [offs[w] for w in mine],
                                  name=f"{tag}_rs_pair_sum{cl}")
        return dict(tag=tag, keys=keys, garrs=garrs, classes=classes, offs=offs, used=used, parts=parts,
                    ex=_ChipExchange(parts, used))

    def rs_last_level(st, gathered):
        sums = [_quad_sum(chip, p, a, u, name=f"{st['tag']}_rs_quad_sum{k}")
                for k, (p, a, u) in enumerate(zip(st["parts"], gathered, st["used"]))]
        for key, g, cl, off in zip(st["keys"], st["garrs"], st["classes"], st["offs"]):
            gsh[key] = sums[cl][off:off + g.shape[0] // N_DEV, :]

    top = (L - 1, 3)
    x_top, y_top, g_top, s_top = ln_of(*top)
    dxres, dyb, gS[("ln_g", *top)], gS[("ln_b", *top)], loss_blk = _loss_ln_bwd(
        x_top, y_top, g_top, xres, target, alpha=alpha, s=s_top, name="loss_ln_top_bwd")
    loss = lax.psum(loss_blk[0, 0], ("x", "y", "c"))
    above = None
    for l in reversed(range(L)):
        sv = saved[l]
        Wl = W[l]
        (dxres, dyb), _ = ffn_bwd(l, 1, sv["ffn2"], dxres, dyb, None, (l, 2))
        dom = _mm(dyb, Wl["wsq"], lead=2, tb=True, out_dtype=BF16, name=f"l{l}_dom")
        gW[("mem_wo", l)] = _mm(sv["om"], dyb, ta=True, out_dtype=BF16, name=f"l{l}_dwo", tn=D)
        dqm, dkm, dvm = _mem_bwd(sv["qm"], sv["km"], sv["vm"], dom, name=f"l{l}_memattn_bwd")
        dxres, dyb = dx_through_ln(dqm, Wl["wsq"], 1, True, dxres, (l, 1), f"l{l}_dx2")
        gW[("mem_wq", l)] = _mm(sv["x2b"], dqm, ta=True, out_dtype=BF16, name=f"l{l}_dwq", tn=D)
        dkvm = jnp.concatenate([dkm, dvm], axis=1).astype(BF16)
        gW[("mem_wkv", l)] = _mm(dkvm, memb, ta=True, out_dtype=BF16, name=f"l{l}_dwkv", tn=D)
        dcat = _mm(dyb, Wl["wsq"], lead=0, tb=True, name=f"l{l}_dcat", tn=D)
        gW[("w_out", l)] = _mm(sv["cat"], dyb, ta=True, out_dtype=BF16, name=f"l{l}_dwout", tn=D)
        riding = [rs_first_level(l, "a")] + ([above] if above else [])
        both = _Both([st["ex"] for st in riding])
        dqh, dkh, dvh, rode = _flash_bwd(sv["qh"], sv["kh"], sv["vh"], sv["cat"], dcat, sv["lse"], H=H, pw=PW,
                                         name=f"l{l}_flash_bwd", ride=both)
        for st, got in zip(riding, both.split(rode)):
            rs_last_level(st, got)
        dqraw, dkv, dkpe = _heads_bwd(dqh, dkh, dvh, tabs, H=H, name=f"l{l}_heads_bwd")
        dcq = _mm(dqraw, Wl["wuqT"], lead=0, name=f"l{l}_dcq")
        gW[("w_uq", l)] = _mm(dqraw, sv["cqn"], ta=True, out_dtype=BF16, name=f"l{l}_dwuq")
        dckv = _mm(dkv, Wl["wukvT"], lead=0, name=f"l{l}_dckv")
        gW[("w_ukv", l)] = _mm(dkv, sv["ckvn"], ta=True, out_dtype=BF16, name=f"l{l}_dwukv")
        du, dwbd, dps = _pool_bwd(sv["hin"], dcat, wbd[l], sv["pscale"], name=f"l{l}_pool_bwd")
        dhin, dgq, dgkv = _norms_bwd(sv["hin"], sv["gq"], sv["gkv"], dcq, dckv, dkpe, du, pw=PW,
                                     name=f"l{l}_norms_bwd")
        pg = PW // len(POOL_WINDOWS)
        gS[("pool_w", l)] = jnp.stack([dwbd[g * pg:(g + 1) * pg, g * pg:(g + 1) * pg]
                                       for g in range(len(POOL_WINDOWS))])
        gS[("pool_scale", l)], gS[("q_norm_g", l)], gS[("kv_norm_g", l)] = dps, dgq, dgkv
        dxres, dyb = dx_through_ln(dhin, Wl["winp"], 0, True, dxres, (l, 0), f"l{l}_dx1")
        gW[("w_in", l)] = _mm(sv["x1b"], dhin, ta=True, out_dtype=BF16, name=f"l{l}_dwin", tn=DINP)
        heads = rs_first_level(l, "b")
        below, rode = ffn_bwd(l, 0, sv["ffn1"], dxres, dyb, heads["ex"], (l - 1, 3) if l > 0 else None)
        rs_last_level(heads, rode)
        if l > 0:
            dxres, dyb = below
            above = rs_first_level(l, "c")
    grad_x = below.reshape(1, T, D)

    small_keys = []
    for l in range(L):
        small_keys += [("pool_w", l), ("pool_scale", l), ("q_norm_g", l), ("kv_norm_g", l)]
        small_keys += [("ln_g", l, k) for k in range(4)] + [("ln_b", l, k) for k in range(4)]
    flat = jnp.concatenate([gS[k].reshape(-1) for k in small_keys])
    n_small = flat.shape[0]
    rows = -(-n_small // (8 * LANE)) * 8
    flat = jnp.pad(flat, (0, rows * LANE - n_small)).reshape(rows, LANE)
    red = _all_reduce_small(flat, name="ar_small").reshape(-1)
    gsm, pos = {}, 0
    for k in small_keys:
        size = math.prod(gS[k].shape)
        gsm[k] = red[pos:pos + size].reshape(gS[k].shape)
        pos += size

    me = 4 * lax.axis_index("x") + 2 * lax.axis_index("y") + lax.axis_index("c")
    dsh = D // N_DEV
    stack = lambda f: jnp.stack([f(l) for l in range(L)])
    g_ln_g = stack(lambda l: jnp.concatenate([gsm[("ln_g", l, k)] for k in range(4)], axis=0))
    g_ln_b = stack(lambda l: jnp.concatenate([gsm[("ln_b", l, k)] for k in range(4)], axis=0))
    swapped = {
        "ffn1_w13": stack(lambda l: gsh[("w13", 0, l)]),
        "ffn2_w13": stack(lambda l: gsh[("w13", 1, l)]),
        "w_in": stack(lambda l: gsh[("w_in", l)][:, :DIN].T),
        "w_uq": stack(lambda l: gsh[("w_uq", l)]),
        "w_ukv": stack(lambda l: gsh[("w_ukv", l)]),
    }
    swap = lambda a: jnp.swapaxes(a, 1, 2)
    grads = {
        "ln_g": lax.dynamic_slice_in_dim(g_ln_g, me * dsh, dsh, axis=2),
        "ln_b": lax.dynamic_slice_in_dim(g_ln_b, me * dsh, dsh, axis=2),
        "ffn1_w2": stack(lambda l: gsh[("w2", 0, l)]),
        "pool_w": stack(lambda l: gsm[("pool_w", l)]),
        "pool_scale": stack(lambda l: gsm[("pool_scale", l)].reshape(PW)),
        "q_norm_g": stack(lambda l: gsm[("q_norm_g", l)].reshape(Q_LORA)),
        "kv_norm_g": stack(lambda l: gsm[("kv_norm_g", l)].reshape(KV_LORA)),
        "w_out": stack(lambda l: gsh[("w_out", l)]),
        "mem_wq": stack(lambda l: gsh[("mem_wq", l)]),
        "mem_wkv": stack(lambda l: gsh[("mem_wkv", l)].T),
        "mem_wo": stack(lambda l: gsh[("mem_wo", l)]),
        "ffn2_w2": stack(lambda l: gsh[("w2", 1, l)]),
        **{nme: swap(g) for nme, g in swapped.items()},
    }

    names = ["ln_g", "ln_b", "ffn1_w13", "ffn1_w2", "w_in", "pool_w", "pool_scale", "q_norm_g", "w_uq",
             "kv_norm_g", "w_ukv", "w_out", "mem_wq", "mem_wkv", "mem_wo", "ffn2_w13", "ffn2_w2"]
    weights = dict(ln_g=ln_g, ln_b=ln_b, ffn1_w13=ffn1_w13, ffn1_w2=ffn1_w2, w_in=w_in, pool_w=pool_w,
                   pool_scale=pool_scale, q_norm_g=q_norm_g, w_uq=w_uq, kv_norm_g=kv_norm_g, w_ukv=w_ukv,
                   w_out=w_out, mem_wq=mem_wq, mem_wkv=mem_wkv, mem_wo=mem_wo, ffn2_w13=ffn2_w13,
                   ffn2_w2=ffn2_w2)
    ms = dict(ln_g=m_ln_g, ln_b=m_ln_b, ffn1_w13=m_ffn1_w13, ffn1_w2=m_ffn1_w2, w_in=m_w_in, pool_w=m_pool_w,
              pool_scale=m_pool_scale, q_norm_g=m_q_norm_g, w_uq=m_w_uq, kv_norm_g=m_kv_norm_g,
              w_ukv=m_w_ukv, w_out=m_w_out, mem_wq=m_mem_wq, mem_wkv=m_mem_wkv, mem_wo=m_mem_wo,
              ffn2_w13=m_ffn2_w13, ffn2_w2=m_ffn2_w2)
    vs = dict(ln_g=v_ln_g, ln_b=v_ln_b, ffn1_w13=v_ffn1_w13, ffn1_w2=v_ffn1_w2, w_in=v_w_in, pool_w=v_pool_w,
              pool_scale=v_pool_scale, q_norm_g=v_q_norm_g, w_uq=v_w_uq, kv_norm_g=v_kv_norm_g,
              w_ukv=v_w_ukv, w_out=v_w_out, mem_wq=v_mem_wq, mem_wkv=v_mem_wkv, mem_wo=v_mem_wo,
              ffn2_w13=v_ffn2_w13, ffn2_w2=v_ffn2_w2)
    deltas, new_m, new_v = [], [], []
    for nme in names:
        if nme in swapped:
            d, mn, vn = [swap(o) for o in _adamw(swap(weights[nme]), swapped[nme], swap(ms[nme]), swap(vs[nme]),
                                                 name=f"adamw_{nme}")]
        else:
            d, mn, vn = _adamw(weights[nme], grads[nme], ms[nme], vs[nme], name=f"adamw_{nme}")
        deltas.append(d)
        new_m.append(mn)
        new_v.append(vn)
    return (loss, grad_x, *[grads[nme] for nme in names], *deltas, *new_m, *new_v)
```

```python
import functools
import math

import jax
import jax.numpy as jnp
from jax import lax
from jax.experimental import pallas as pl
from jax.experimental.pallas import tpu as pltpu

F32 = jnp.float32
BF16 = jnp.bfloat16
MESH = pl.DeviceIdType.MESH

CHUNK = 64
MEM_HEADS = 4
POOL_WINDOWS = (2, 4, 8, 16)
QK_NOPE = 128
QK_ROPE = 64
V_HEAD = 128
Q_LORA = 256
KV_LORA = 128
ROPE_BASE = 10000.0
LN_EPS = 1e-5
RMS_EPS = 1e-6
NEG_INF = -1e30
ADAM_LR = 0.001
ADAM_B1 = 0.9
ADAM_B2 = 0.999
ADAM_EPS = 1e-08
ADAM_WD = 0.01
ADAM_STEP = 10

N_DEV = 8
LANE = 128
HEAD_PAD = 2 * LANE
POOL_HALO = 16
VMEM_CAP = 56 * 1024 * 1024
VMEM_FLOOR = 32 * 1024 * 1024


def _tile(n, pref, mult):
    t = (min(pref, n) // mult) * mult
    while t >= mult:
        if n % t == 0:
            return t
        t -= mult
    return n


def _params(sem, est_bytes):
    limit = int(min(max(2 * est_bytes + (8 << 20), VMEM_FLOOR), VMEM_CAP))
    return pltpu.CompilerParams(dimension_semantics=sem, vmem_limit_bytes=limit)


def _nbytes(shape, dtype):
    return math.prod(shape) * jnp.dtype(dtype).itemsize


def _hbm(x):
    return pltpu.with_memory_space_constraint(x, pltpu.HBM)


def _out(shape, dtype):
    return pltpu.HBM(tuple(shape), dtype)


def _dg(a, b, ca, cb):
    return lax.dot_general(a.astype(BF16), b.astype(BF16), (((ca,), (cb,)), ((), ())),
                           preferred_element_type=F32)


@jax.custom_vjp
def _bdot_nn(a, b):
    return _dg(a, b, 1, 0)


def _bdot_nn_fwd(a, b):
    return _dg(a, b, 1, 0), (a, b)


def _bdot_nn_bwd(res, ct):
    a, b = res
    return _dg(ct, b, 1, 1).astype(a.dtype), _dg(a, ct, 0, 0).astype(b.dtype)


_bdot_nn.defvjp(_bdot_nn_fwd, _bdot_nn_bwd)


@jax.custom_vjp
def _bdot_nt(a, b):
    return _dg(a, b, 1, 1)


def _bdot_nt_fwd(a, b):
    return _dg(a, b, 1, 1), (a, b)


def _bdot_nt_bwd(res, ct):
    a, b = res
    return _dg(ct, b, 1, 0).astype(a.dtype), _dg(ct, a, 0, 0).astype(b.dtype)


_bdot_nt.defvjp(_bdot_nt_fwd, _bdot_nt_bwd)


@functools.partial(jax.custom_vjp, nondiff_argnums=(1,))
def _lane_roll(x, shift):
    return pltpu.roll(x, shift % x.shape[1], axis=1)


def _lane_roll_fwd(x, shift):
    return _lane_roll(x, shift), None


def _lane_roll_bwd(shift, _, ct):
    return (_lane_roll(ct, -shift),)


_lane_roll.defvjp(_lane_roll_fwd, _lane_roll_bwd)


@functools.partial(jax.custom_vjp, nondiff_argnums=(1, 2))
def _cols(x, lo, hi):
    return x[:, lo:hi]


def _cols_fwd(x, lo, hi):
    return x[:, lo:hi], x.shape[1]


def _cols_bwd(lo, hi, width, ct):
    parts = []
    if lo > 0:
        parts.append(jnp.zeros((ct.shape[0], lo), ct.dtype))
    parts.append(ct)
    if hi < width:
        parts.append(jnp.zeros((ct.shape[0], width - hi), ct.dtype))
    return (jnp.concatenate(parts, axis=1) if len(parts) > 1 else ct,)


_cols.defvjp(_cols_fwd, _cols_bwd)


MM_VMEM_BUDGET = 22 * 1024 * 1024


def _mm(a, b, *, name, ta=False, tb=False, out_dtype=F32, lead=None, add=None, add_scale=1.0,
        tm=1024, tn=1024, tk=8192, ride=None):
    if ta:
        K, M = a.shape
    else:
        M, K = a.shape
    bshape = b.shape[1:] if lead is not None else b.shape
    if tb:
        N, Kb = bshape
    else:
        Kb, N = bshape
    assert K == Kb, (name, a.shape, b.shape)

    def blocks(tm, tn, tk):
        tm = _tile(M, tm, LANE if ta else 16)
        tn = _tile(N, tn, LANE)
        tk = _tile(K, tk, LANE)
        nbytes = (tm * tk * a.dtype.itemsize + tk * tn * b.dtype.itemsize
                  + tm * tn * (jnp.dtype(out_dtype).itemsize + (4 if K // tk > 1 else 0)
                               + (add.dtype.itemsize if add is not None else 0)))
        return tm, tn, tk, nbytes

    tm, tn, tk, est = blocks(tm, tn, tk)
    for shrink in ("m", "k", "m", "k", "n"):
        if est <= MM_VMEM_BUDGET:
            break
        if shrink == "m":
            tm, tn, tk, est = blocks(max(tm // 2, LANE), tn, tk)
        elif shrink == "k":
            tm, tn, tk, est = blocks(tm, tn, max(tk // 2, LANE))
        else:
            tm, tn, tk, est = blocks(tm, max(tn // 2, LANE), tk)
    nk = K // tk
    ca = 0 if ta else 1
    cb = 1 if tb else 0

    def body(*refs):
        a_ref, b_ref = refs[0], refs[1]
        add_ref = refs[2] if add is not None else None
        o_ref = refs[3] if add is not None else refs[2]

        def finish(r):
            if add_ref is not None:
                r = r + add_scale * add_ref[...].astype(F32)
            o_ref[...] = r.astype(o_ref.dtype)

        if nk == 1:
            finish(_dg(a_ref[...], b_ref[...], ca, cb))
            return
        acc_ref = refs[-1]
        k = pl.program_id(2)

        @pl.when(k == 0)
        def _():
            acc_ref[...] = jnp.zeros_like(acc_ref)

        acc_ref[...] += _dg(a_ref[...], b_ref[...], ca, cb)

        @pl.when(k == nk - 1)
        def _():
            finish(acc_ref[...])

    a_blk = (tk, tm) if ta else (tm, tk)
    a_map = (lambda i, j, k: (k, i)) if ta else (lambda i, j, k: (i, k))
    b_blk = (tn, tk) if tb else (tk, tn)
    if lead is None:
        b_map = (lambda i, j, k: (j, k)) if tb else (lambda i, j, k: (k, j))
        b_spec = pl.BlockSpec(b_blk, b_map)
    else:
        b_map = (lambda i, j, k: (lead, j, k)) if tb else (lambda i, j, k: (lead, k, j))
        b_spec = pl.BlockSpec((None,) + b_blk, b_map)
    in_specs = [pl.BlockSpec(a_blk, a_map), b_spec]
    args = [a, b]
    if add is not None:
        in_specs.append(pl.BlockSpec((tm, tn), lambda i, j, k: (i, j)))
        args.append(add)
    (out,), rode = _host_call(
        body, name=name,
        grid=(M // tm, N // tn, nk),
        in_specs=in_specs,
        out_specs=[pl.BlockSpec((tm, tn), lambda i, j, k: (i, j))],
        out_shape=[_out((M, N), out_dtype)],
        scratch=[pltpu.VMEM((tm, tn), F32)] if nk > 1 else [],
        args=[_hbm(v) for v in args], sem=("parallel", "parallel", "arbitrary"), est=est + tm * tn * 4,
        ride=ride)
    return out if ride is None else (out, rode)


def _rowwise(fn, tiles, params, tile_outs, acc_outs=(), *, tm, name):
    tile_arrays, tile_specs = [], []
    for t in tiles:
        if isinstance(t, tuple):
            tile_arrays.append(t[0])
            tile_specs.append(t[1])
        else:
            tile_arrays.append(t)
            tile_specs.append(pl.BlockSpec((tm, t.shape[1]), lambda i: (i, 0)))
    T = tile_arrays[0].shape[0]
    nt, np_, nto, nao = len(tile_arrays), len(params), len(tile_outs), len(acc_outs)

    def body(*refs):
        i = pl.program_id(0)
        tvals = [r[...] for r in refs[:nt]]
        pvals = [r[...] for r in refs[nt:nt + np_]]
        to_refs = refs[nt + np_:nt + np_ + nto]
        ao_refs = refs[nt + np_ + nto:]
        touts, aouts = fn(i, tvals, pvals)
        for r, v in zip(to_refs, touts):
            r[...] = v.astype(r.dtype)
        if nao:
            @pl.when(i == 0)
            def _():
                for r in ao_refs:
                    r[...] = jnp.zeros_like(r)
            for r, v in zip(ao_refs, aouts):
                r[...] += v.astype(r.dtype)

    in_specs = tile_specs + [pl.BlockSpec(p.shape, lambda i: (0, 0)) for p in params]
    out_specs = [pl.BlockSpec((tm, c), lambda i: (i, 0)) for c, _ in tile_outs]
    out_specs += [pl.BlockSpec(s, lambda i: (0, 0)) for s, _ in acc_outs]
    out_shape = [_out((T, c), d) for c, d in tile_outs]
    out_shape += [_out(s, d) for s, d in acc_outs]
    width = sum(s.block_shape[-1] for s in tile_specs) + sum(c for c, _ in tile_outs)
    est = 6 * tm * width * 4 + sum(_nbytes(p.shape, F32) for p in params) * 4
    return pl.pallas_call(
        body, name=name, grid=(T // tm,),
        in_specs=in_specs, out_specs=out_specs, out_shape=out_shape,
        compiler_params=_params(("arbitrary",) if nao else ("parallel",), est),
    )(*[_hbm(v) for v in tile_arrays], *[_hbm(p) for p in params])


def _ln_fn(alpha, s, xres, y, g, b):
    z = alpha * xres.astype(F32) + s * y.astype(F32)
    mu = jnp.mean(z, axis=-1, keepdims=True)
    zc = z - mu
    var = jnp.mean(zc * zc, axis=-1, keepdims=True)
    return zc * lax.rsqrt(var + LN_EPS) * g + b


def _mm_ln(a, b, lead, xres, g, bias, *, alpha, s, name):
    M, K = a.shape
    N = b.shape[2]
    tm = _tile(M, 512, 16)

    def body(a_ref, b_ref, x_ref, g_ref, bias_ref, y_ref, xo_ref, xb_ref):
        y = _dg(a_ref[...], b_ref[...], 1, 0)
        y_ref[...] = y.astype(y_ref.dtype)
        out = _ln_fn(alpha, s, x_ref[...], y, g_ref[...], bias_ref[...])
        xo_ref[...] = out
        xb_ref[...] = out.astype(BF16)

    row = pl.BlockSpec((tm, N), lambda i: (i, 0))
    vec = pl.BlockSpec((1, N), lambda i: (0, 0))
    est = tm * K * 2 + K * N * 2 + tm * N * (4 + 4 + 4 + 2 + 8)
    return pl.pallas_call(
        body, name=name, grid=(M // tm,),
        in_specs=[pl.BlockSpec((tm, K), lambda i: (i, 0)), pl.BlockSpec((None, K, N), lambda i: (lead, 0, 0)),
                  row, vec, vec],
        out_specs=[row, row, row],
        out_shape=[_out((M, N), BF16), _out((M, N), F32), _out((M, N), BF16)],
        compiler_params=_params(("parallel",), est),
    )(_hbm(a), _hbm(b), _hbm(xres), _hbm(g), _hbm(bias))


def _ln_bwd_math(alpha, s, x, y, g, d):
    z = alpha * x + s * y.astype(F32)
    zc = z - jnp.mean(z, axis=-1, keepdims=True)
    r = lax.rsqrt(jnp.mean(zc * zc, axis=-1, keepdims=True) + LN_EPS)
    xh = zc * r
    dxh = d * g
    dz = r * (dxh - jnp.mean(dxh, axis=-1, keepdims=True) - xh * jnp.mean(dxh * xh, axis=-1, keepdims=True))
    return alpha * dz, s * dz, jnp.sum(d * xh, axis=0, keepdims=True), jnp.sum(d, axis=0, keepdims=True)


def _mm_ln_bwd(a, b, lead, tb, add, xres, y, g, *, alpha, s, name):
    M, K = a.shape
    N = b.shape[1] if tb else b.shape[2]
    tk = K if K * N * 2 <= MM_VMEM_BUDGET * 3 // 5 else _tile(K, 2816, LANE)
    tm = _tile(M, 512 if K * N * 2 <= MM_VMEM_BUDGET // 4 else 256, 16)
    nk = K // tk
    cb = 1 if tb else 0

    def body(a_ref, b_ref, add_ref, x_ref, y_ref, g_ref, dx_ref, dy_ref, dg_ref, db_ref, *scratch):
        i, k = pl.program_id(0), pl.program_id(1)

        def finish(d):
            @pl.when(i == 0)
            def _():
                dg_ref[...] = jnp.zeros_like(dg_ref)
                db_ref[...] = jnp.zeros_like(db_ref)

            dx, dy, dg, db = _ln_bwd_math(alpha, s, x_ref[...], y_ref[...], g_ref[...], d + add_ref[...])
            dx_ref[...] = dx
            dy_ref[...] = dy.astype(dy_ref.dtype)
            dg_ref[...] += dg
            db_ref[...] += db

        if nk == 1:
            finish(_dg(a_ref[...], b_ref[...], 1, cb))
            return
        acc_ref = scratch[0]

        @pl.when(k == 0)
        def _():
            acc_ref[...] = jnp.zeros_like(acc_ref)

        acc_ref[...] += _dg(a_ref[...], b_ref[...], 1, cb)

        @pl.when(k == nk - 1)
        def _():
            finish(acc_ref[...])

    row = pl.BlockSpec((tm, N), lambda i, k: (i, 0))
    vec = pl.BlockSpec((1, N), lambda i, k: (0, 0))
    b_spec = (pl.BlockSpec((None, N, tk), lambda i, k: (lead, 0, k)) if tb
              else pl.BlockSpec((None, tk, N), lambda i, k: (lead, k, 0)))
    est = tm * tk * 2 + tk * N * 2 + tm * N * (4 + 4 + 2 + 4 + 2 + 4 + 12)
    return pl.pallas_call(
        body, name=name, grid=(M // tm, nk),
        in_specs=[pl.BlockSpec((tm, tk), lambda i, k: (i, k)), b_spec, row, row, row, vec],
        out_specs=[row, row, vec, vec],
        out_shape=[_out((M, N), F32), _out((M, N), BF16), _out((1, N), F32), _out((1, N), F32)],
        scratch_shapes=[pltpu.VMEM((tm, N), F32)] if nk > 1 else [],
        compiler_params=_params(("arbitrary", "arbitrary"), est),
    )(_hbm(a), _hbm(b), _hbm(add), _hbm(xres), _hbm(y), _hbm(g))


def _loss_ln_bwd(xres, y, g, out, target, *, alpha, s, name):
    T, D = xres.shape
    tm = _tile(T, 256, 16)

    def body(x_ref, y_ref, o_ref, t_ref, g_ref, dx_ref, dy_ref, dg_ref, db_ref, loss_ref):
        @pl.when(pl.program_id(0) == 0)
        def _():
            dg_ref[...] = jnp.zeros_like(dg_ref)
            db_ref[...] = jnp.zeros_like(db_ref)
            loss_ref[...] = jnp.zeros_like(loss_ref)

        err = o_ref[...] - t_ref[...]
        part = 0.5 * jnp.sum(jnp.sum(err * err, axis=1, keepdims=True) / D, axis=0, keepdims=True)
        loss_ref[...] += jnp.broadcast_to(part, loss_ref.shape)
        dx, dy, dg, db = _ln_bwd_math(alpha, s, x_ref[...], y_ref[...], g_ref[...], err / D)
        dx_ref[...] = dx
        dy_ref[...] = dy.astype(dy_ref.dtype)
        dg_ref[...] += dg
        db_ref[...] += db

    row = pl.BlockSpec((tm, D), lambda i: (i, 0))
    vec = pl.BlockSpec((1, D), lambda i: (0, 0))
    return pl.pallas_call(
        body, name=name, grid=(T // tm,),
        in_specs=[row, row, row, row, vec],
        out_specs=[row, row, vec, vec, pl.BlockSpec((8, LANE), lambda i: (0, 0))],
        out_shape=[_out((T, D), F32), _out((T, D), BF16), _out((1, D), F32), _out((1, D), F32),
                   _out((8, LANE), F32)],
        compiler_params=_params(("arbitrary",), 14 * tm * D * 4),
    )(_hbm(xres), _hbm(y), _hbm(out), _hbm(target), _hbm(g))


FFN_TILE = 256


def _interleave(w, axis):
    n = w.shape[axis] // (2 * FFN_TILE)
    shp = w.shape[:axis] + (2, n, FFN_TILE) + w.shape[axis + 1:]
    return jnp.swapaxes(w.reshape(shp), axis, axis + 1).reshape(w.shape)


def _deinterleave(w, axis):
    n = w.shape[axis] // (2 * FFN_TILE)
    shp = w.shape[:axis] + (n, 2, FFN_TILE) + w.shape[axis + 1:]
    return jnp.swapaxes(w.reshape(shp), axis, axis + 1).reshape(w.shape)


def _ffn_up(xb, w13t, lead, *, name, ride=None):
    T, D = xb.shape
    F = w13t.shape[1] // 2
    tc = FFN_TILE
    tm = _tile(T, 2048, 16)

    def body(x_ref, w_ref, h_ref, a_ref):
        h = _dg(x_ref[...], w_ref[...], 1, 1)
        g, u = h[:, :tc], h[:, tc:]
        h_ref[...] = h.astype(h_ref.dtype)
        a_ref[...] = (g * jax.nn.sigmoid(g) * u).astype(a_ref.dtype)

    est = (tm * D + 2 * tc * D + 3 * tm * tc) * 2 + 3 * tm * tc * 4
    (h13, a), gathered = _host_call(
        body, name=name, grid=(T // tm, F // tc),
        in_specs=[pl.BlockSpec((tm, D), lambda i, j: (i, 0)),
                  pl.BlockSpec((None, 2 * tc, D), lambda i, j: (lead, j, 0))],
        out_specs=[pl.BlockSpec((tm, 2 * tc), lambda i, j: (i, j)),
                   pl.BlockSpec((tm, tc), lambda i, j: (i, j))],
        out_shape=[_out((T, 2 * F), BF16), _out((T, F), BF16)],
        args=[_hbm(xb), _hbm(w13t)], sem=("parallel", "parallel"), est=est, ride=ride)
    return h13, a, gathered


def _ffn_down_bwd(dyb, w2, lead, h13, *, name, ride=None):
    T, D = dyb.shape
    F = w2.shape[1]
    tc = FFN_TILE
    tm = _tile(T, 2048, 16)

    def body(dy_ref, w_ref, h_ref, dh_ref):
        d = _dg(dy_ref[...], w_ref[...], 1, 1)
        h = h_ref[...].astype(F32)
        g, u = h[:, :tc], h[:, tc:]
        sig = jax.nn.sigmoid(g)
        gs = g * sig
        dh_ref[...] = jnp.concatenate([d * u * (sig + gs * (1.0 - sig)), d * gs], axis=1).astype(dh_ref.dtype)

    est = (tm * D + tc * D + 4 * tm * tc) * 2 + 6 * tm * tc * 4
    (dh,), rode = _host_call(
        body, name=name, grid=(T // tm, F // tc),
        in_specs=[pl.BlockSpec((tm, D), lambda i, j: (i, 0)),
                  pl.BlockSpec((None, tc, D), lambda i, j: (lead, j, 0)),
                  pl.BlockSpec((tm, 2 * tc), lambda i, j: (i, j))],
        out_specs=[pl.BlockSpec((tm, 2 * tc), lambda i, j: (i, j))],
        out_shape=[_out((T, 2 * F), BF16)],
        args=[_hbm(dyb), _hbm(w2), _hbm(h13)], sem=("parallel", "parallel"), est=est, ride=ride)
    return dh, rode


def _pool_select(parts, pw):
    pg = pw // len(POOL_WINDOWS)
    grp = lax.broadcasted_iota(jnp.int32, parts[0].shape, 1) // pg
    out = parts[3]
    for g in (2, 1, 0):
        out = jnp.where(grp == g, parts[g], out)
    return out


def _pool_count(t0, rows, pw):
    pg = pw // len(POOL_WINDOWS)
    grp = lax.broadcasted_iota(jnp.int32, (rows, pw), 1) // pg
    win = jnp.where(grp == 0, POOL_WINDOWS[0],
                    jnp.where(grp == 1, POOL_WINDOWS[1],
                              jnp.where(grp == 2, POOL_WINDOWS[2], POOL_WINDOWS[3])))
    t = t0 + lax.broadcasted_iota(jnp.int32, (rows, pw), 0)
    return jnp.minimum(t + 1, win).astype(F32)


def _window_sums(ext, up):
    n = ext.shape[0]
    sums, cur, k = [], ext, 1
    for _ in POOL_WINDOWS:
        cur = cur + pltpu.roll(cur, (n - k) if up else k, axis=0)
        sums.append(cur)
        k *= 2
    return sums


def _pool_delta(u, halo, t0):
    tm, pw = u.shape
    ext = jnp.concatenate([halo, u], axis=0)
    sums = [s[POOL_HALO:, :] for s in _window_sums(ext, up=False)]
    return _pool_select(sums, pw) / _pool_count(t0, tm, pw) - u


def _pool_fwd(hin, wbd, scale, cat, *, name):
    T = hin.shape[0]
    pw = wbd.shape[0]
    tm = _tile(T, 256, POOL_HALO)
    per = tm // POOL_HALO

    def body(u_ref, halo_ref, w_ref, s_ref, cat_ref, y_ref):
        i = pl.program_id(0)
        halo = jnp.where(i > 0, halo_ref[...], 0.0)
        d = _pool_delta(u_ref[...], halo, i * tm)
        y_ref[...] = (_dg(d, w_ref[...], 1, 0) * s_ref[...]).astype(y_ref.dtype)

    return pl.pallas_call(
        body, name=name, grid=(T // tm,),
        in_specs=[pl.BlockSpec((tm, pw), lambda i: (i, 0)),
                  pl.BlockSpec((POOL_HALO, pw), lambda i: (jnp.maximum(i * per - 1, 0), 0)),
                  pl.BlockSpec((pw, pw), lambda i: (0, 0)),
                  pl.BlockSpec((1, pw), lambda i: (0, 0)),
                  ANY],
        out_specs=pl.BlockSpec((tm, pw), lambda i: (i, 0)),
        out_shape=_out(cat.shape, cat.dtype),
        input_output_aliases={4: 0},
        compiler_params=_params(("parallel",), 16 * tm * pw * 4),
    )(_hbm(hin), _hbm(hin), _hbm(wbd), _hbm(scale), _hbm(cat))


def _pool_bwd(hin, dcat, wbd, scale, *, name):
    T = hin.shape[0]
    pw = wbd.shape[0]
    tm = _tile(T, 256, POOL_HALO)
    per = tm // POOL_HALO
    nt = T // tm

    def body(u_ref, halo_ref, dy_ref, dyn_ref, w_ref, s_ref, du_ref, dw_ref, ds_ref):
        i = pl.program_id(0)

        @pl.when(i == 0)
        def _():
            dw_ref[...] = jnp.zeros_like(dw_ref)
            ds_ref[...] = jnp.zeros_like(ds_ref)

        halo = jnp.where(i > 0, halo_ref[...], 0.0)
        d = _pool_delta(u_ref[...], halo, i * tm)
        w = w_ref[...]
        sc = s_ref[...]
        dy = dy_ref[...]
        dyn = jnp.where(i < nt - 1, dyn_ref[...], 0.0)
        ds_ref[...] += jnp.sum(dy * _dg(d, w, 1, 0), axis=0, keepdims=True)
        dys = dy * sc
        dw_ref[...] += _dg(d, dys, 0, 0)
        dys_ext = jnp.concatenate([dys, dyn * sc], axis=0)
        dd_ext = _dg(dys_ext, w, 1, 1)
        ddp = dd_ext / _pool_count(i * tm, tm + POOL_HALO, pw)
        sums = [s[:tm, :] for s in _window_sums(ddp, up=True)]
        du_ref[...] = _pool_select(sums, pw) - dd_ext[:tm, :]

    return pl.pallas_call(
        body, name=name, grid=(nt,),
        in_specs=[pl.BlockSpec((tm, pw), lambda i: (i, 0)),
                  pl.BlockSpec((POOL_HALO, pw), lambda i: (jnp.maximum(i * per - 1, 0), 0)),
                  pl.BlockSpec((tm, pw), lambda i: (i, 0)),
                  pl.BlockSpec((POOL_HALO, pw), lambda i: (jnp.minimum((i + 1) * per, nt * per - 1), 0)),
                  pl.BlockSpec((pw, pw), lambda i: (0, 0)),
                  pl.BlockSpec((1, pw), lambda i: (0, 0))],
        out_specs=[pl.BlockSpec((tm, pw), lambda i: (i, 0)),
                   pl.BlockSpec((pw, pw), lambda i: (0, 0)),
                   pl.BlockSpec((1, pw), lambda i: (0, 0))],
        out_shape=[_out((T, pw), F32),
                   _out((pw, pw), F32),
                   _out((1, pw), F32)],
        compiler_params=_params(("arbitrary",), 24 * tm * pw * 4),
    )(_hbm(hin), _hbm(hin), _hbm(dcat), _hbm(dcat), _hbm(wbd), _hbm(scale))


def _rms(x, g):
    return x * lax.rsqrt(jnp.mean(x * x, axis=-1, keepdims=True) + RMS_EPS) * g


def _norms_fn(pw, h, gq, gkv):
    o1 = pw + Q_LORA
    o2 = o1 + KV_LORA
    return (_rms(_cols(h, pw, o1), gq), _rms(_cols(h, o1, o2), gkv), _cols(h, o2, h.shape[1]))


def _norms_bwd(hin, gq, gkv, dcq, dckv, dkpe, du, *, pw, name):
    tm = _tile(hin.shape[0], 256, 16)
    dinp = hin.shape[1]

    def fn(i, tv, pv):
        _, vjp = jax.vjp(functools.partial(_norms_fn, pw), tv[0], pv[0], pv[1])
        dh, dgq, dgkv = vjp((tv[1].astype(F32), tv[2].astype(F32), tv[3].astype(F32)))
        dh = jnp.concatenate([tv[4], dh[:, pw:]], axis=1)
        return (dh,), (dgq, dgkv)

    return _rowwise(fn, [hin, dcq, dckv, dkpe, du], [gq, gkv], [(dinp, BF16)],
                    [((1, Q_LORA), F32), ((1, KV_LORA), F32)], tm=tm, name=name)


def _heads_fn(H, qraw, kv, kpe, rc, rs1, rs2):
    half = QK_ROPE // 2
    scale = (QK_NOPE + QK_ROPE) ** -0.5

    def rope(blk):
        return blk * rc + _lane_roll(blk, -half) * rs1 + _lane_roll(blk, half) * rs2

    krot = rope(kpe)
    qs, ks, vs = [], [], []
    for h in range(H):
        lo = h * HEAD_PAD
        qs += [_cols(qraw, lo, lo + LANE) * scale, rope(_cols(qraw, lo + LANE, lo + HEAD_PAD)) * scale]
        ks += [_cols(kv, lo, lo + LANE), krot]
        vs += [_cols(kv, lo + LANE, lo + HEAD_PAD)]
    return jnp.concatenate(qs, axis=1), jnp.concatenate(ks, axis=1), jnp.concatenate(vs, axis=1)


def _heads_fwd(hin, gq, gkv, tabs, wuq, wukv, *, H, pw, name):
    tm = _tile(hin.shape[0], 256, 16)

    def fn(i, tv, pv):
        cqn, ckvn, kpe = _norms_fn(pw, tv[0], pv[0], pv[1])
        qraw = _dg(cqn, pv[2], 1, 1)
        kv = _dg(ckvn, pv[3], 1, 1)
        return (*_heads_fn(H, qraw, kv, kpe, *tv[1:]), cqn, ckvn), ()

    return _rowwise(fn, [hin, *tabs], [gq, gkv, wuq, wukv],
                    [(H * HEAD_PAD, BF16), (H * HEAD_PAD, BF16), (H * V_HEAD, BF16), (Q_LORA, BF16),
                     (KV_LORA, BF16)], tm=tm, name=name)


def _heads_bwd(dq, dk, dv, tabs, *, H, name):
    tm = _tile(dq.shape[0], 256, 16)

    def fn(i, tv, pv):
        z = jnp.zeros((tm, H * HEAD_PAD), F32)
        zk = jnp.zeros((tm, LANE), F32)
        rc, rs1, rs2 = tv[3], tv[4], tv[5]
        _, vjp = jax.vjp(lambda a, b, c: _heads_fn(H, a, b, c, rc, rs1, rs2), z, z, zk)
        return vjp((tv[0].astype(F32), tv[1].astype(F32), tv[2].astype(F32))), ()

    return _rowwise(fn, [dq, dk, dv, *tabs], [],
                    [(H * HEAD_PAD, BF16), (H * HEAD_PAD, BF16), (LANE, F32)], tm=tm, name=name)


def _diag_mask(rows, cols, row0):
    r = (row0 + lax.broadcasted_iota(jnp.int32, (rows, cols), 0)) // CHUNK
    c = lax.broadcasted_iota(jnp.int32, (rows, cols), 1) // CHUNK
    return r >= c


def _flash_fwd(qh, kh, vh, *, H, pw, name, ride=None):
    T = qh.shape[0]
    t = _tile(T, 512, CHUNK)
    off = pw // V_HEAD


    half = t

    def body(q_ref, k_ref, v_ref, o_ref, lse_ref):
        i = pl.program_id(1)
        q = q_ref[...]

        def update(carry, s, v):
            m, l, acc = carry
            mn = jnp.maximum(m, jnp.max(s, axis=1, keepdims=True))
            p = jnp.exp(s - mn)
            corr = jnp.exp(m - mn)
            return mn, corr * l + jnp.sum(p, axis=1, keepdims=True), corr * acc + _dg(p, v, 1, 0)

        def blk(j, carry):
            rows = pl.ds(pl.multiple_of(j * t, t), t)
            return update(carry, _dg(q, k_ref[rows, :], 1, 1), v_ref[rows, :])

        init = (jnp.full((t, 1), NEG_INF, F32), jnp.zeros((t, 1), F32), jnp.zeros((t, V_HEAD), F32))
        carry = lax.fori_loop(0, i, blk, init)
        done = []
        for r0 in range(0, t, half):
            keys = pl.ds(pl.multiple_of(i * t, t), r0 + half)
            s = _dg(q[r0:r0 + half, :], k_ref[keys, :], 1, 1)
            s = jnp.where(_diag_mask(half, r0 + half, r0), s, NEG_INF)
            done.append(update(tuple(c[r0:r0 + half] for c in carry), s, v_ref[keys, :]))
        m, l, acc = (jnp.concatenate(parts, axis=0) for parts in zip(*done))
        o_ref[...] = (acc / l).astype(o_ref.dtype)
        lse_ref[...] = jnp.broadcast_to(m + jnp.log(l), (t, V_HEAD))

    est = 2 * T * (HEAD_PAD + V_HEAD) * 2 + 8 * t * t * 4
    (o, lse), gathered = _host_call(
        body, name=name, grid=(H, T // t),
        in_specs=[pl.BlockSpec((t, HEAD_PAD), lambda h, i: (i, h)),
                  pl.BlockSpec((T, HEAD_PAD), lambda h, i: (0, h)),
                  pl.BlockSpec((T, V_HEAD), lambda h, i: (0, h))],
        out_specs=[pl.BlockSpec((t, V_HEAD), lambda h, i: (i, off + h)),
                   pl.BlockSpec((t, V_HEAD), lambda h, i: (i, h))],
        out_shape=[_out((T, pw + H * V_HEAD), BF16),
                   _out((T, H * V_HEAD), F32)],
        args=[_hbm(qh), _hbm(kh), _hbm(vh)], sem=("parallel", "parallel"), est=est, ride=ride)
    return o, lse, gathered


def _flash_bwd(qh, kh, vh, cat, dcat, lse, *, H, pw, name, ride=None):
    T = qh.shape[0]
    t = _tile(T, 512, CHUNK)
    nb = T // t
    off = pw // V_HEAD
    half = t

    def body(q_ref, k_ref, v_ref, o_ref, do_ref, lse_ref, dq_out_ref, dk_ref, dv_ref, dq_ref):
        j = pl.program_id(1)

        @pl.when(j == 0)
        def _():
            dq_ref[...] = jnp.zeros_like(dq_ref)

        kj = k_ref[...]
        vj = v_ref[...]

        def pair(rows, kx, vx, mask):
            qi = q_ref[rows, :]
            doi = do_ref[rows, :]
            oi = o_ref[rows, :].astype(F32)
            lsei = lse_ref[rows, :][:, :1]
            s = _dg(qi, kx, 1, 1)
            if mask is not None:
                s = jnp.where(mask, s, NEG_INF)
            p = jnp.exp(s - lsei)
            dp = _dg(doi, vx, 1, 1)
            di = jnp.sum(doi * oi, axis=1, keepdims=True)
            ds = p * (dp - di)
            dq_ref[rows, :] += _dg(ds, kx, 1, 0)
            return _dg(ds, qi, 0, 0), _dg(p, doi, 0, 0)

        def blk(i, carry):
            dk, dv = pair(pl.ds(pl.multiple_of(i * t, t), t), kj, vj, None)
            return carry[0] + dk, carry[1] + dv

        dk, dv = jnp.zeros((t, HEAD_PAD), F32), jnp.zeros((t, V_HEAD), F32)
        for r0 in range(0, t, half):
            n = r0 + half
            dkp, dvp = pair(pl.ds(pl.multiple_of(j * t + r0, half), half), kj[:n], vj[:n],
                            _diag_mask(half, n, r0))
            if n < t:
                dkp = jnp.concatenate([dkp, jnp.zeros((t - n, HEAD_PAD), F32)], axis=0)
                dvp = jnp.concatenate([dvp, jnp.zeros((t - n, V_HEAD), F32)], axis=0)
            dk, dv = dk + dkp, dv + dvp
        dk, dv = lax.fori_loop(j + 1, nb, blk, (dk, dv))
        dk_ref[...] = dk.astype(dk_ref.dtype)
        dv_ref[...] = dv.astype(dv_ref.dtype)

        @pl.when(j == nb - 1)
        def _():
            dq_out_ref[...] = dq_ref[...].astype(dq_out_ref.dtype)

    est = T * (HEAD_PAD * 2 + V_HEAD * 2 + V_HEAD * 4 + V_HEAD * 4 + HEAD_PAD * 4) + 10 * t * t * 4
    (dq, dk, dv), gathered = _host_call(
        body, name=name, grid=(H, nb),
        in_specs=[pl.BlockSpec((T, HEAD_PAD), lambda h, j: (0, h)),
                  pl.BlockSpec((t, HEAD_PAD), lambda h, j: (j, h)),
                  pl.BlockSpec((t, V_HEAD), lambda h, j: (j, h)),
                  pl.BlockSpec((T, V_HEAD), lambda h, j: (0, off + h)),
                  pl.BlockSpec((T, V_HEAD), lambda h, j: (0, off + h)),
                  pl.BlockSpec((T, V_HEAD), lambda h, j: (0, h))],
        out_specs=[pl.BlockSpec((T, HEAD_PAD), lambda h, j: (0, h)),
                   pl.BlockSpec((t, HEAD_PAD), lambda h, j: (j, h)),
                   pl.BlockSpec((t, V_HEAD), lambda h, j: (j, h))],
        out_shape=[_out((T, H * HEAD_PAD), BF16),
                   _out((T, H * HEAD_PAD), BF16),
                   _out((T, H * V_HEAD), BF16)],
        scratch=[pltpu.VMEM((T, HEAD_PAD), F32)],
        args=[_hbm(v) for v in (qh, kh, vh, cat, dcat, lse)], sem=("arbitrary", "arbitrary"), est=est,
        ride=ride)
    return dq, dk, dv, gathered


def _mem_fn(q, k, v):
    hd = q.shape[1] // MEM_HEADS
    outs = []
    for h in range(MEM_HEADS):
        lo, hi = h * hd, (h + 1) * hd
        s = _bdot_nt(_cols(q, lo, hi), _cols(k, lo, hi)) * hd ** -0.5
        e = jnp.exp(s - lax.stop_gradient(jnp.max(s, axis=1, keepdims=True)))
        p = e / jnp.sum(e, axis=1, keepdims=True)
        outs.append(_bdot_nn(p, _cols(v, lo, hi)))
    return jnp.concatenate(outs, axis=1)


def _mem_fwd(q, k, v, *, name):
    T, D = q.shape
    tm = _tile(T, 256, 16)

    def fn(i, tv, pv):
        return (_mem_fn(tv[0], pv[0], pv[1]),), ()

    return _rowwise(fn, [q], [k, v], [(D, BF16)], tm=tm, name=name)[0]


def _mem_bwd(q, k, v, do, *, name):
    T, D = q.shape
    tm = _tile(T, 256, 16)

    def fn(i, tv, pv):
        _, vjp = jax.vjp(_mem_fn, tv[0], pv[0], pv[1])
        dq, dk, dv = vjp(tv[1].astype(F32))
        return (dq,), (dk, dv)

    return _rowwise(fn, [q, do], [k, v], [(D, BF16)], [(k.shape, F32), (v.shape, F32)], tm=tm, name=name)


def _adamw(w, g, m, v, *, name):
    shape = w.shape
    if w.ndim != 3:
        lead3 = (1, math.prod(shape[:-1]), shape[-1])
        return [o.reshape(shape) for o in _adamw(*[a.reshape(lead3) for a in (w, g, m, v)], name=name)]
    Lw, R, C = shape
    tr = _tile(R, 512, 8)
    b1c = 1.0 - ADAM_B1 ** ADAM_STEP
    b2c = 1.0 - ADAM_B2 ** ADAM_STEP

    def body(w_ref, g_ref, m_ref, v_ref, d_ref, mo_ref, vo_ref):
        gg = g_ref[...]
        mn = ADAM_B1 * m_ref[...] + (1.0 - ADAM_B1) * gg
        vn = ADAM_B2 * v_ref[...] + (1.0 - ADAM_B2) * (gg * gg)
        d_ref[...] = -ADAM_LR * ((mn / b1c) / (jnp.sqrt(vn / b2c) + ADAM_EPS) + ADAM_WD * w_ref[...])
        mo_ref[...] = mn
        vo_ref[...] = vn

    spec = pl.BlockSpec((None, tr, C), lambda l, i: (l, i, 0))
    return pl.pallas_call(
        body, name=name, grid=(Lw, R // tr),
        in_specs=[spec] * 4, out_specs=[spec] * 3,
        out_shape=[_out(shape, F32)] * 3,
        compiler_params=_params(("parallel", "parallel"), 7 * tr * C * 4),
    )(*[_hbm(a) for a in (w, g, m, v)])


def _pair_sum(core, gs, landed, offs, *, name):
    n = len(gs)
    _, R, C = landed.shape
    rows = [g.shape[0] // N_DEV for g in gs]

    def body(core_ref, *refs):
        g_refs, l_ref, o_ref = refs[:n], refs[n], refs[n + 1]
        for g_ref, off, r in zip(g_refs, offs, rows):
            o_ref[off:off + r, :] = (g_ref[...].astype(F32) + l_ref[off:off + r, :].astype(F32)).astype(o_ref.dtype)

    slab = pl.BlockSpec((None, R, C), lambda p, core_ref: (p, 0, 0))
    own = [pl.BlockSpec((r, C), lambda p, core_ref: (2 * p + core_ref[0], 0)) for r in rows]
    return pl.pallas_call(
        body, name=name,
        grid_spec=pltpu.PrefetchScalarGridSpec(
            num_scalar_prefetch=1, grid=(4,), in_specs=own + [slab], out_specs=slab),
        out_shape=_out(landed.shape, landed.dtype),
        input_output_aliases={n + 1: 0},
        compiler_params=_params(("arbitrary",), 3 * R * C * 2 + R * C * 8),
    )(core, *[_hbm(g) for g in gs], _hbm(landed))


def _quad_sum(chip, part, gathered, used, *, name):
    C = part.shape[2]
    R = used
    tr = _tile(R, 256, 16)

    def body(chip_ref, own_ref, a_ref, b_ref, c_ref, o_ref):
        o_ref[...] = ((own_ref[...].astype(F32) + a_ref[...].astype(F32)) + b_ref[...].astype(F32)) \
            + c_ref[...].astype(F32)

    def other(k):
        return pl.BlockSpec((None, tr, C), lambda i, chip_ref: (chip_ref[0] ^ k, i, 0))

    return pl.pallas_call(
        body, name=name,
        grid_spec=pltpu.PrefetchScalarGridSpec(
            num_scalar_prefetch=1, grid=(R // tr,),
            in_specs=[pl.BlockSpec((None, tr, C), lambda i, chip_ref: (chip_ref[0], i, 0)),
                      other(1), other(2), other(3)],
            out_specs=pl.BlockSpec((tr, C), lambda i, chip_ref: (i, 0))),
        out_shape=_out((R, C), F32),
        compiler_params=_params(("arbitrary",), 8 * tr * C * 4),
    )(chip, _hbm(part), _hbm(gathered), _hbm(gathered), _hbm(gathered))


def _place():
    x, y, c = lax.axis_index("x"), lax.axis_index("y"), lax.axis_index("c")
    return x, y, c


ANY = pl.BlockSpec(memory_space=pl.ANY)


class _Gather:
    def __init__(self, shards):
        self.shards = list(shards)
        self.n = len(self.shards)
        self.out_shape = [_out((s.shape[0], N_DEV * s.shape[1], s.shape[2]), s.dtype)
                          for s in self.shards]
        self.scratch = [pltpu.SemaphoreType.DMA((7 * self.n,)), pltpu.SemaphoreType.DMA((7 * self.n,)),
                        pltpu.SemaphoreType.DMA((self.n,))]
        self.operands = [_hbm(s) for s in self.shards]

    def _bind(self, refs):
        n = self.n
        ins, outs = refs[:n], refs[n:2 * n]
        send_sems, recv_sems, local_sems = refs[2 * n:]
        x, y, c = _place()
        me, sib = (x, y, c), (x, y, 1 - c)
        chips = [(1 - x, y), (x, 1 - y), (1 - x, 1 - y)]

        def rows(w, p):
            r = self.shards[w].shape[1]
            idx = 4 * p[0] + 2 * p[1] + p[2]
            return outs[w].at[:, pl.ds(pl.multiple_of(idx * r, 8), r), :]

        def copy(w, k, block, to, src=None):
            return pltpu.make_async_remote_copy(
                src_ref=rows(w, block) if src is None else src, dst_ref=rows(w, block),
                send_sem=send_sems.at[w * 7 + k], recv_sem=recv_sems.at[w * 7 + k],
                device_id=to, device_id_type=MESH)

        def mine():
            return [pltpu.make_async_copy(ins[w], rows(w, me), local_sems.at[w]) for w in range(n)]

        def first():
            out = []
            for w in range(n):
                out.append(copy(w, 0, me, sib, src=ins[w]))
                out += [copy(w, 1 + j, me, (*chip, c), src=ins[w]) for j, chip in enumerate(chips)]
            return out

        def passed():
            return [copy(w, 4 + j, (*chip, c), sib) for j, chip in enumerate(chips) for w in range(n)]

        def landed():
            return [copy(w, 1 + j, (*chip, c), me) for j, chip in enumerate(chips) for w in range(n)]

        def last():
            out = []
            for w in range(n):
                out.append(copy(w, 0, sib, me))
                out += [copy(w, 4 + j, (*chip, 1 - c), me) for j, chip in enumerate(chips)]
            return out

        return mine, first, landed, passed, last

    def start(self, refs):
        mine, first, _, _, _ = self._bind(refs)
        for cp in mine() + first():
            cp.start()

    def forward(self, refs):
        _, _, landed, passed, _ = self._bind(refs)
        for arrived, fwd in zip(landed(), passed()):
            arrived.wait_recv()
            fwd.start()

    def finish(self, refs):
        mine, first, _, passed, last = self._bind(refs)
        for cp in last():
            cp.wait_recv()
        for cp in first() + passed():
            cp.wait_send()
        for cp in mine():
            cp.wait()


class _ChipExchange:
    def __init__(self, parts, used):
        self.ncl = len(parts)
        self.used = list(used)
        self.out_shape = [_out(p.shape, p.dtype) for p in parts]
        self.scratch = [pltpu.SemaphoreType.DMA((3 * self.ncl,)), pltpu.SemaphoreType.DMA((3 * self.ncl,))]
        self.operands = [_hbm(p) for p in parts]
        self.n = self.ncl

    def _bind(self, refs):
        ncl = self.ncl
        ins, outs = refs[:ncl], refs[ncl:2 * ncl]
        send_sems, recv_sems = refs[2 * ncl:]
        x, y, c = _place()
        chips = [(1 - x, y), (x, 1 - y), (1 - x, 1 - y)]
        here = 2 * x + y

        def copies(outgoing):
            out = []
            for k in range(ncl):
                rows = pl.ds(0, self.used[k])
                for j, (cx, cy) in enumerate(chips):
                    there = 2 * cx + cy
                    src, dst = (there, here) if outgoing else (here, there)
                    out.append(pltpu.make_async_remote_copy(
                        src_ref=ins[k].at[src, rows, :], dst_ref=outs[k].at[dst, rows, :],
                        send_sem=send_sems.at[3 * k + j], recv_sem=recv_sems.at[3 * k + j],
                        device_id=(cx, cy, c), device_id_type=MESH))
            return out

        return copies

    def start(self, refs):
        for cp in self._bind(refs)(True):
            cp.start()

    def forward(self, refs):
        pass

    def finish(self, refs):
        copies = self._bind(refs)
        for cp in copies(False):
            cp.wait_recv()
        for cp in copies(True):
            cp.wait_send()


class _Both:
    def __init__(self, members):
        self.members = list(members)
        self.n = sum(m.n for m in self.members)
        self.out_shape = [s for m in self.members for s in m.out_shape]
        self.scratch = [s for m in self.members for s in m.scratch]
        self.operands = [o for m in self.members for o in m.operands]

    def split(self, arrays):
        out, a = [], 0
        for m in self.members:
            out.append(list(arrays[a:a + m.n]))
            a += m.n
        return out

    def _refs(self, refs):
        ins, outs = self.split(refs[:self.n]), self.split(refs[self.n:2 * self.n])
        scr, b = [], 2 * self.n
        for m in self.members:
            scr.append(list(refs[b:b + len(m.scratch)]))
            b += len(m.scratch)
        return [(*i, *o, *s) for i, o, s in zip(ins, outs, scr)]

    def start(self, refs):
        for m, r in zip(self.members, self._refs(refs)):
            m.start(r)

    def forward(self, refs):
        for m, r in zip(self.members, self._refs(refs)):
            m.forward(r)

    def finish(self, refs):
        for m, r in zip(self.members, self._refs(refs)):
            m.finish(r)


def _exchange_alone(ex, *, name):
    def body(*refs):
        ex.start(refs)
        ex.forward(refs)
        ex.finish(refs)

    return pl.pallas_call(
        body, name=name, in_specs=[ANY] * ex.n, out_specs=[ANY] * ex.n,
        out_shape=ex.out_shape, scratch_shapes=ex.scratch,
    )(*ex.operands)


def _host_call(body, *, name, grid, in_specs, out_specs, out_shape, args, sem, est, ride=None, scratch=()):
    scratch = list(scratch)
    if ride is None:
        outs = pl.pallas_call(body, name=name, grid=grid, in_specs=in_specs, out_specs=out_specs,
                              out_shape=out_shape, scratch_shapes=scratch,
                              compiler_params=_params(sem, est))(*args)
        return list(outs), []
    n_in, n_out, n, n_scr = len(in_specs), len(out_specs), ride.n, len(scratch)

    def full(*refs):
        ins, rin = refs[:n_in], refs[n_in:n_in + n]
        outs, rout = refs[n_in + n:n_in + n + n_out], refs[n_in + n + n_out:n_in + 2 * n + n_out]
        own = refs[n_in + 2 * n + n_out:n_in + 2 * n + n_out + n_scr]
        rrefs = (*rin, *rout, *refs[n_in + 2 * n + n_out + n_scr:])
        step, total = _ride(ride, rrefs, grid)
        body(*ins, *outs, *own)
        _ride_end(ride, rrefs, step, total)

    outs = pl.pallas_call(
        full, name=name, grid=grid,
        in_specs=list(in_specs) + [ANY] * n, out_specs=list(out_specs) + [ANY] * n,
        out_shape=list(out_shape) + ride.out_shape, scratch_shapes=scratch + ride.scratch,
        compiler_params=_params(("arbitrary",) * len(grid), est),
    )(*args, *ride.operands)
    return list(outs[:n_out]), list(outs[n_out:])


def _ride(ex, refs, grid):
    total = math.prod(grid)
    step = pl.program_id(0)
    for axis in range(1, len(grid)):
        step = step * grid[axis] + pl.program_id(axis)
    pl.when(step == 0)(lambda: ex.start(refs))
    return step, total


def _ride_end(ex, refs, step, total):
    pl.when(step == (3 * total) // 4)(lambda: ex.forward(refs))
    pl.when(step == total - 1)(lambda: ex.finish(refs))


def _class_layout(grads, classes):
    used = [0] * len(set(classes))
    offs = []
    for g, cl in zip(grads, classes):
        offs.append(used[cl])
        used[cl] += g.shape[0] // N_DEV
    return offs, used


def _rs_to_sibling(grads, classes, *, name):
    n = len(grads)
    offs, used = _class_layout(grads, classes)
    heights = used
    ncl = len(heights)
    cols = [next(g.shape[1] for g, cl in zip(grads, classes) if cl == k) for k in range(ncl)]

    def body(*refs):
        gs, land = refs[:n], refs[n:n + ncl]
        send_sems, recv_sems = refs[n + ncl:]
        x, y, c = _place()
        sib = (x, y, 1 - c)
        for p in range(4):
            for w in range(n):
                r = grads[w].shape[0] // N_DEV
                cl = classes[w]
                there = gs[w].at[pl.ds(pl.multiple_of((2 * p + 1 - c) * r, 8), r), :]
                pltpu.make_async_remote_copy(
                    src_ref=there, dst_ref=land[cl].at[p, pl.ds(offs[w], r), :],
                    send_sem=send_sems.at[cl * 4 + p], recv_sem=recv_sems.at[cl * 4 + p],
                    device_id=sib, device_id_type=MESH).start()
        for cl in range(ncl):
            for p in range(4):
                rows_used = land[cl].at[p, pl.ds(0, used[cl]), :]
                slab = pltpu.make_async_remote_copy(
                    src_ref=rows_used, dst_ref=rows_used,
                    send_sem=send_sems.at[cl * 4 + p], recv_sem=recv_sems.at[cl * 4 + p],
                    device_id=sib, device_id_type=MESH)
                slab.wait_send()
                slab.wait_recv()

    return pl.pallas_call(
        body, name=name,
        in_specs=[ANY] * n, out_specs=[ANY] * ncl,
        out_shape=[_out((4, heights[k], cols[k]), BF16) for k in range(ncl)],
        scratch_shapes=[pltpu.SemaphoreType.DMA((4 * ncl,))] * 2,
    )(*[_hbm(g) for g in grads])


def _sum_devices(g, *, name):
    R = g.shape[1]

    def body(g_ref, o_ref):
        acc = g_ref[0]
        for d in range(1, N_DEV):
            acc = acc + g_ref[d]
        o_ref[...] = acc

    vm = pl.BlockSpec(memory_space=pltpu.VMEM)
    return pl.pallas_call(
        body, name=name, in_specs=[vm], out_specs=vm,
        out_shape=jax.ShapeDtypeStruct((R, LANE), F32),
        compiler_params=pltpu.CompilerParams(vmem_limit_bytes=VMEM_FLOOR),
    )(g)


def _rope_tables(positions):
    half = QK_ROPE // 2
    inv_freq = ROPE_BASE ** (-jnp.arange(half, dtype=F32) / half)
    ang = positions.astype(F32)[:, None] * inv_freq
    cos, sin = jnp.cos(ang), jnp.sin(ang)
    z = jnp.zeros_like(cos)
    z2 = jnp.zeros((positions.shape[0], LANE - QK_ROPE), F32)
    rc = jnp.concatenate([cos, cos, z2], axis=1)
    rs1 = jnp.concatenate([-sin, z, z2], axis=1)
    rs2 = jnp.concatenate([z, sin, z2], axis=1)
    return rc, rs1, rs2


def _block_diag(pool_w):
    G, pg, _ = pool_w.shape
    out = jnp.zeros((G * pg, G * pg), pool_w.dtype)
    for g in range(G):
        out = lax.dynamic_update_slice(out, pool_w[g], (g * pg, g * pg))
    return out


def kernel(x, mem, positions, ln_g, ln_b, ffn1_w13, ffn1_w2, w_in, pool_w, pool_scale, q_norm_g, w_uq, kv_norm_g, w_ukv, w_out, mem_wq, mem_wkv, mem_wo, ffn2_w13, ffn2_w2, loss_target, m_ln_g, m_ln_b, m_ffn1_w13, m_ffn1_w2, m_w_in, m_pool_w, m_pool_scale, m_q_norm_g, m_w_uq, m_kv_norm_g, m_w_ukv, m_w_out, m_mem_wq, m_mem_wkv, m_mem_wo, m_ffn2_w13, m_ffn2_w2, v_ln_g, v_ln_b, v_ffn1_w13, v_ffn1_w2, v_w_in, v_pool_w, v_pool_scale, v_q_norm_g, v_w_uq, v_kv_norm_g, v_w_ukv, v_w_out, v_mem_wq, v_mem_wkv, v_mem_wo, v_ffn2_w13, v_ffn2_w2):
    L = ln_g.shape[0]
    T, D = x.shape[1], x.shape[2]
    F = ffn1_w2.shape[1] * N_DEV
    PW = D // 4
    H = (D - PW) // V_HEAD
    DIN = w_in.shape[2]
    DINP = PW + Q_LORA + KV_LORA + LANE
    QW = QK_NOPE + QK_ROPE
    alpha = (2 * L) ** 0.25
    x2d = x.reshape(T, D)
    memb = mem.reshape(mem.shape[1], D).astype(BF16)
    target = loss_target.reshape(T, D)
    tabs = _rope_tables(positions.reshape(T))

    def shards_of(l):
        return dict(
            w13a=ffn1_w13[l].T[None].astype(BF16),
            w13b=ffn2_w13[l].T[None].astype(BF16),
            w2a=ffn1_w2[l][None].astype(BF16),
            w2b=ffn2_w2[l][None].astype(BF16),
            wsq=jnp.stack([w_out[l], mem_wq[l], mem_wo[l]]).astype(BF16),
            wkvT=mem_wkv[l].T[None].astype(BF16),
            winp=jnp.pad(w_in[l], ((0, 0), (0, DINP - DIN)))[None].astype(BF16),
            wuqT=w_uq[l].T[None].astype(BF16),
            wukvT=w_ukv[l].T[None].astype(BF16),
        )

    SMALL = ("winp", "wuqT", "wukvT")
    shards = [shards_of(l) for l in range(L)]
    W = [dict() for _ in range(L)]

    def rider(spec):
        return _Gather([shards[l][n] for l, n in spec]) if spec else None

    def arrived(spec, arrays):
        for (l, n), a in zip(spec, arrays):
            if n in ("w13a", "w13b"):
                a = _interleave(a, 1)
            elif n == "wuqT":
                a = jnp.pad(a.reshape(H, QW, Q_LORA), ((0, 0), (0, HEAD_PAD - QW), (0, 0)))
                a = a.reshape(1, H * HEAD_PAD, Q_LORA)
            elif n == "ln":
                a = jnp.moveaxis(a.reshape(N_DEV, 2, L, 4, D // N_DEV), 0, 3).reshape(2, L, 4, D)
                LN["g"], LN["b"] = a[0], a[1]
            W[l][n] = a

    LN = {}
    shards[0]["ln"] = jnp.concatenate([ln_g.reshape(1, 4 * L, -1), ln_b.reshape(1, 4 * L, -1)], axis=1)
    spec0 = [(0, "w13a")]
    arrived(spec0, _exchange_alone(rider(spec0), name="ag_first"))
    wbd = [_block_diag(pool_w[l]).astype(BF16) for l in range(L)]

    def ffn_fwd(l, which, xres, xb, k, spec):
        ab = "ab"[which]
        h13, a, rode = _ffn_up(xb, W[l]["w13" + ab], 0, name=f"l{l}_ffn{which}_up", ride=rider(spec))
        arrived(spec, rode)
        y, xo, xob = _mm_ln(a, W[l]["w2" + ab], 0, xres, LN["g"][l,k:k + 1], LN["b"][l,k:k + 1], alpha=alpha, s=0.5,
                            name=f"l{l}_ffn{which}_y_ln{k}")
        return dict(xres=xres, xb=xb, h13=h13, a=a, y=y), xo, xob

    saved = []
    xres, xb = x2d, x2d.astype(BF16)
    for l in range(L):
        sv = {}
        more = l + 1 < L
        Wl = W[l]
        spec = ([(0, "w2a"), (0, "ln"), *[(0, n) for n in SMALL], (0, "wkvT")] if l == 0
                else [(l, "wsq"), (l, "wkvT")])
        sv["ffn1"], x1, x1b = ffn_fwd(l, 0, xres, xb, 0, spec)
        hin = _mm(x1b, Wl["winp"], lead=0, name=f"l{l}_hin")
        pscale = pool_scale[l].reshape(1, PW)
        gq, gkv = q_norm_g[l].reshape(1, Q_LORA), kv_norm_g[l].reshape(1, KV_LORA)
        qh, kh, vh, cqn, ckvn = _heads_fwd(hin, gq, gkv, tabs, Wl["wuqT"][0], Wl["wukvT"][0], H=H, pw=PW,
                                           name=f"l{l}_heads")
        spec = [(l, "w13b"), (l, "w2b")] + ([(0, "wsq")] if l == 0 else []) + ([(l + 1, "w13a")] if more else [])
        cat, lse, rode = _flash_fwd(qh, kh, vh, H=H, pw=PW, name=f"l{l}_flash", ride=rider(spec))
        arrived(spec, rode)
        cat = _pool_fwd(hin, wbd[l], pscale, cat, name=f"l{l}_pool")
        ymix, x2, x2b = _mm_ln(cat, Wl["wsq"], 0, x1, LN["g"][l,1:2], LN["b"][l,1:2], alpha=alpha, s=1.0,
                               name=f"l{l}_ymix_ln1")
        qm = _mm(x2b, Wl["wsq"], lead=1, out_dtype=BF16, name=f"l{l}_qm")
        kvm = _mm(memb, Wl["wkvT"], lead=0, tb=True, name=f"l{l}_kvm")
        km, vm = kvm[:, :D], kvm[:, D:]
        om = _mem_fwd(qm, km, vm, name=f"l{l}_memattn")
        ymem, x3, x3b = _mm_ln(om, Wl["wsq"], 2, x2, LN["g"][l,2:3], LN["b"][l,2:3], alpha=alpha, s=1.0,
                               name=f"l{l}_ymem_ln2")
        spec = [(l + 1, n) for n in ("w2a", *SMALL)] if more else []
        sv["ffn2"], x4, x4b = ffn_fwd(l, 1, x3, x3b, 3, spec)
        sv.update(x1=x1, x1b=x1b, hin=hin, pscale=pscale, gq=gq, gkv=gkv, cqn=cqn, ckvn=ckvn,
                  qh=qh, kh=kh, vh=vh, lse=lse, cat=cat, ymix=ymix, x2=x2, x2b=x2b, qm=qm, km=km, vm=vm,
                  om=om, ymem=ymem)
        saved.append(sv)
        xres, xb = x4, x4b


    gW = {}
    gS = {}

    def ln_of(l, k):
        sv = saved[l]
        x, y, s = {0: (sv["ffn1"]["xres"], sv["ffn1"]["y"], 0.5), 1: (sv["x1"], sv["ymix"], 1.0),
                   2: (sv["x2"], sv["ymem"], 1.0), 3: (sv["ffn2"]["xres"], sv["ffn2"]["y"], 0.5)}[k]
        return x, y, LN["g"][l, k:k + 1], s

    def dx_through_ln(a, b, lead, tb, add, into, name):
        x, y, g, s = ln_of(*into)
        dxres, dyb, dg, db = _mm_ln_bwd(a, b, lead, tb, add, x, y, g, alpha=alpha, s=s, name=name)
        gS[("ln_g", *into)], gS[("ln_b", *into)] = dg, db
        return dxres, dyb

    def ffn_bwd(l, which, sv, dxres, dyb, ride, into):
        tag = f"l{l}_ffn{which}"
        gW[("w2", which, l)] = _mm(sv["a"], dyb, ta=True, out_dtype=BF16, name=f"{tag}_dw2", tn=D)
        dh, rode = _ffn_down_bwd(dyb, W[l]["w2" + "ab"[which]], 0, sv["h13"], name=f"{tag}_dh", ride=ride)
        dw13 = _mm(dh, sv["xb"], ta=True, out_dtype=BF16, name=f"{tag}_dw13", tn=D)
        gW[("w13", which, l)] = _deinterleave(dw13, 0)
        w13 = W[l]["w13" + "ab"[which]]
        if into is not None:
            return dx_through_ln(dh, w13, 0, False, dxres, into, f"{tag}_dx"), rode
        last = rs_first_level(l, "c")
        dxn, got = _mm(dh, w13, lead=0, add=dxres, name=f"{tag}_dx", tn=D, ride=last["ex"])
        rs_last_level(last, got)
        return dxn, rode

    core = lax.axis_index("c").astype(jnp.int32).reshape(1)
    chip = (2 * lax.axis_index("x") + lax.axis_index("y")).astype(jnp.int32).reshape(1)
    gsh = {}

    def rs_first_level(l, group):
        keys, classes = {
            "a": ([("w13", 1, l), ("w2", 1, l), ("mem_wkv", l), ("mem_wq", l), ("mem_wo", l)], [0] * 5),
            "b": ([("w_out", l), ("w_in", l), ("w_uq", l), ("w_ukv", l)], [0, 1, 2, 3]),
            "c": ([("w13", 0, l), ("w2", 0, l)], [0, 0]),
        }[group]
        tag = f"l{l}{group}"
        garrs = []
        for key in keys:
            g = gW[key]
            if key[0] == "w_uq":
                g = g.reshape(H, HEAD_PAD, Q_LORA)[:, :QW, :].reshape(H * QW, Q_LORA)
            garrs.append(g)
        offs, used = _class_layout(garrs, classes)
        parts = list(_rs_to_sibling(garrs, classes, name=f"{tag}_rs_sibling"))
        for cl in range(len(parts)):
            mine = [w for w, c in enumerate(classes) if c == cl]
            parts[cl] = _pair_sum(core, [garrs[w] for w in mine], parts[cl], [offs[w] for w in mine],
                                  name=f"{tag}_rs_pair_sum{cl}")
        return dict(tag=tag, keys=keys, garrs=garrs, classes=classes, offs=offs, used=used, parts=parts,
                    ex=_ChipExchange(parts, used))

    def rs_last_level(st, gathered):
        sums = [_quad_sum(chip, p, a, u, name=f"{st['tag']}_rs_quad_sum{k}")
                for k, (p, a, u) in enumerate(zip(st["parts"], gathered, st["used"]))]
        for key, g, cl, off in zip(st["keys"], st["garrs"], st["classes"], st["offs"]):
            gsh[key] = sums[cl][off:off + g.shape[0] // N_DEV, :]

    top = (L - 1, 3)
    x_top, y_top, g_top, s_top = ln_of(*top)
    dxres, dyb, gS[("ln_g", *top)], gS[("ln_b", *top)], loss_blk = _loss_ln_bwd(
        x_top, y_top, g_top, xres, target, alpha=alpha, s=s_top, name="loss_ln_top_bwd")
    above = None
    for l in reversed(range(L)):
        sv = saved[l]
        Wl = W[l]
        (dxres, dyb), _ = ffn_bwd(l, 1, sv["ffn2"], dxres, dyb, None, (l, 2))
        dom = _mm(dyb, Wl["wsq"], lead=2, tb=True, out_dtype=BF16, name=f"l{l}_dom")
        gW[("mem_wo", l)] = _mm(sv["om"], dyb, ta=True, out_dtype=BF16, name=f"l{l}_dwo", tn=D)
        dqm, dkm, dvm = _mem_bwd(sv["qm"], sv["km"], sv["vm"], dom, name=f"l{l}_memattn_bwd")
        dxres, dyb = dx_through_ln(dqm, Wl["wsq"], 1, True, dxres, (l, 1), f"l{l}_dx2")
        gW[("mem_wq", l)] = _mm(sv["x2b"], dqm, ta=True, out_dtype=BF16, name=f"l{l}_dwq", tn=D)
        dkvm = jnp.concatenate([dkm, dvm], axis=1).astype(BF16)
        gW[("mem_wkv", l)] = _mm(dkvm, memb, ta=True, out_dtype=BF16, name=f"l{l}_dwkv", tn=D)
        dcat = _mm(dyb, Wl["wsq"], lead=0, tb=True, name=f"l{l}_dcat", tn=D)
        gW[("w_out", l)] = _mm(sv["cat"], dyb, ta=True, out_dtype=BF16, name=f"l{l}_dwout", tn=D)
        riding = [rs_first_level(l, "a")] + ([above] if above else [])
        both = _Both([st["ex"] for st in riding])
        dqh, dkh, dvh, rode = _flash_bwd(sv["qh"], sv["kh"], sv["vh"], sv["cat"], dcat, sv["lse"], H=H, pw=PW,
                                         name=f"l{l}_flash_bwd", ride=both)
        for st, got in zip(riding, both.split(rode)):
            rs_last_level(st, got)
        dqraw, dkv, dkpe = _heads_bwd(dqh, dkh, dvh, tabs, H=H, name=f"l{l}_heads_bwd")
        dcq = _mm(dqraw, Wl["wuqT"], lead=0, name=f"l{l}_dcq")
        gW[("w_uq", l)] = _mm(dqraw, sv["cqn"], ta=True, out_dtype=BF16, name=f"l{l}_dwuq")
        dckv = _mm(dkv, Wl["wukvT"], lead=0, name=f"l{l}_dckv")
        gW[("w_ukv", l)] = _mm(dkv, sv["ckvn"], ta=True, out_dtype=BF16, name=f"l{l}_dwukv")
        du, dwbd, dps = _pool_bwd(sv["hin"], dcat, wbd[l], sv["pscale"], name=f"l{l}_pool_bwd")
        dhin, dgq, dgkv = _norms_bwd(sv["hin"], sv["gq"], sv["gkv"], dcq, dckv, dkpe, du, pw=PW,
                                     name=f"l{l}_norms_bwd")
        pg = PW // len(POOL_WINDOWS)
        gS[("pool_w", l)] = jnp.stack([dwbd[g * pg:(g + 1) * pg, g * pg:(g + 1) * pg]
                                       for g in range(len(POOL_WINDOWS))])
        gS[("pool_scale", l)], gS[("q_norm_g", l)], gS[("kv_norm_g", l)] = dps, dgq, dgkv
        dxres, dyb = dx_through_ln(dhin, Wl["winp"], 0, True, dxres, (l, 0), f"l{l}_dx1")
        gW[("w_in", l)] = _mm(sv["x1b"], dhin, ta=True, out_dtype=BF16, name=f"l{l}_dwin", tn=DINP)
        heads = rs_first_level(l, "b")
        riding = [heads["ex"]]
        if l == 0:
            small_keys = []
            for ll in range(L):
                small_keys += [("pool_w", ll), ("pool_scale", ll), ("q_norm_g", ll), ("kv_norm_g", ll)]
                small_keys += [("ln_g", ll, k) for k in range(4)] + [("ln_b", ll, k) for k in range(4)]
            flat = jnp.concatenate([loss_blk[0, :1]] + [gS[k].reshape(-1) for k in small_keys])
            n_small = flat.shape[0]
            rows = -(-n_small // (8 * LANE)) * 8
            flat = jnp.pad(flat, (0, rows * LANE - n_small)).reshape(1, rows, LANE)
            riding.append(_Gather([flat]))
        both = _Both(riding)
        below, rode = ffn_bwd(l, 0, sv["ffn1"], dxres, dyb, both, (l - 1, 3) if l > 0 else None)
        rode = both.split(rode)
        rs_last_level(heads, rode[0])
        if l > 0:
            dxres, dyb = below
            above = rs_first_level(l, "c")
    grad_x = below.reshape(1, T, D)

    red = _sum_devices(rode[1][0].reshape(N_DEV, rows, LANE), name="sum_small").reshape(-1)
    loss = red[0]
    gsm, pos = {}, 1
    for k in small_keys:
        size = math.prod(gS[k].shape)
        gsm[k] = red[pos:pos + size].reshape(gS[k].shape)
        pos += size

    me = 4 * lax.axis_index("x") + 2 * lax.axis_index("y") + lax.axis_index("c")
    dsh = D // N_DEV
    stack = lambda f: jnp.stack([f(l) for l in range(L)])
    g_ln_g = stack(lambda l: jnp.concatenate([gsm[("ln_g", l, k)] for k in range(4)], axis=0))
    g_ln_b = stack(lambda l: jnp.concatenate([gsm[("ln_b", l, k)] for k in range(4)], axis=0))
    swapped = {
        "ffn1_w13": stack(lambda l: gsh[("w13", 0, l)]),
        "ffn2_w13": stack(lambda l: gsh[("w13", 1, l)]),
        "w_in": stack(lambda l: gsh[("w_in", l)][:, :DIN].T),
        "w_uq": stack(lambda l: gsh[("w_uq", l)]),
        "w_ukv": stack(lambda l: gsh[("w_ukv", l)]),
    }
    swap = lambda a: jnp.swapaxes(a, 1, 2)
    grads = {
        "ln_g": lax.dynamic_slice_in_dim(g_ln_g, me * dsh, dsh, axis=2),
        "ln_b": lax.dynamic_slice_in_dim(g_ln_b, me * dsh, dsh, axis=2),
        "ffn1_w2": stack(lambda l: gsh[("w2", 0, l)]),
        "pool_w": stack(lambda l: gsm[("pool_w", l)]),
        "pool_scale": stack(lambda l: gsm[("pool_scale", l)].reshape(PW)),
        "q_norm_g": stack(lambda l: gsm[("q_norm_g", l)].reshape(Q_LORA)),
        "kv_norm_g": stack(lambda l: gsm[("kv_norm_g", l)].reshape(KV_LORA)),
        "w_out": stack(lambda l: gsh[("w_out", l)]),
        "mem_wq": stack(lambda l: gsh[("mem_wq", l)]),
        "mem_wkv": stack(lambda l: gsh[("mem_wkv", l)].T),
        "mem_wo": stack(lambda l: gsh[("mem_wo", l)]),
        "ffn2_w2": stack(lambda l: gsh[("w2", 1, l)]),
        **{nme: swap(g) for nme, g in swapped.items()},
    }

    names = ["ln_g", "ln_b", "ffn1_w13", "ffn1_w2", "w_in", "pool_w", "pool_scale", "q_norm_g", "w_uq",
             "kv_norm_g", "w_ukv", "w_out", "mem_wq", "mem_wkv", "mem_wo", "ffn2_w13", "ffn2_w2"]
    weights = dict(ln_g=ln_g, ln_b=ln_b, ffn1_w13=ffn1_w13, ffn1_w2=ffn1_w2, w_in=w_in, pool_w=pool_w,
                   pool_scale=pool_scale, q_norm_g=q_norm_g, w_uq=w_uq, kv_norm_g=kv_norm_g, w_ukv=w_ukv,
                   w_out=w_out, mem_wq=mem_wq, mem_wkv=mem_wkv, mem_wo=mem_wo, ffn2_w13=ffn2_w13,
                   ffn2_w2=ffn2_w2)
    ms = dict(ln_g=m_ln_g, ln_b=m_ln_b, ffn1_w13=m_ffn1_w13, ffn1_w2=m_ffn1_w2, w_in=m_w_in, pool_w=m_pool_w,
              pool_scale=m_pool_scale, q_norm_g=m_q_norm_g, w_uq=m_w_uq, kv_norm_g=m_kv_norm_g,
              w_ukv=m_w_ukv, w_out=m_w_out, mem_wq=m_mem_wq, mem_wkv=m_mem_wkv, mem_wo=m_mem_wo,
              ffn2_w13=m_ffn2_w13, ffn2_w2=m_ffn2_w2)
    vs = dict(ln_g=v_ln_g, ln_b=v_ln_b, ffn1_w13=v_ffn1_w13, ffn1_w2=v_ffn1_w2, w_in=v_w_in, pool_w=v_pool_w,
              pool_scale=v_pool_scale, q_norm_g=v_q_norm_g, w_uq=v_w_uq, kv_norm_g=v_kv_norm_g,
              w_ukv=v_w_ukv, w_out=v_w_out, mem_wq=v_mem_wq, mem_wkv=v_mem_wkv, mem_wo=v_mem_wo,
              ffn2_w13=v_ffn2_w13, ffn2_w2=v_ffn2_w2)
    deltas, new_m, new_v = [], [], []
    for nme in names:
        if nme in swapped:
            d, mn, vn = [swap(o) for o in _adamw(swap(weights[nme]), swapped[nme], swap(ms[nme]), swap(vs[nme]),
                                                 name=f"adamw_{nme}")]
        else:
            d, mn, vn = _adamw(weights[nme], grads[nme], ms[nme], vs[nme], name=f"adamw_{nme}")
        deltas.append(d)
        new_m.append(mn)
        new_v.append(vn)
    return (loss, grad_x, *[grads[nme] for nme in names], *deltas, *new_m, *new_v)
```

```python
import functools
import math

import jax
import jax.numpy as jnp
from jax import lax
from jax.experimental import pallas as pl
from jax.experimental.pallas import tpu as pltpu

F32 = jnp.float32
BF16 = jnp.bfloat16
MESH = pl.DeviceIdType.MESH

CHUNK = 64
MEM_HEADS = 4
POOL_WINDOWS = (2, 4, 8, 16)
QK_NOPE = 128
QK_ROPE = 64
V_HEAD = 128
Q_LORA = 256
KV_LORA = 128
ROPE_BASE = 10000.0
LN_EPS = 1e-5
RMS_EPS = 1e-6
NEG_INF = -1e30
ADAM_LR = 0.001
ADAM_B1 = 0.9
ADAM_B2 = 0.999
ADAM_EPS = 1e-08
ADAM_WD = 0.01
ADAM_STEP = 10

N_DEV = 8
LANE = 128
HEAD_PAD = 2 * LANE
POOL_HALO = 16
VMEM_CAP = 56 * 1024 * 1024
VMEM_FLOOR = 32 * 1024 * 1024


def _tile(n, pref, mult):
    t = (min(pref, n) // mult) * mult
    while t >= mult:
        if n % t == 0:
            return t
        t -= mult
    return n


def _params(sem, est_bytes):
    limit = int(min(max(2 * est_bytes + (8 << 20), VMEM_FLOOR), VMEM_CAP))
    return pltpu.CompilerParams(dimension_semantics=sem, vmem_limit_bytes=limit)


def _nbytes(shape, dtype):
    return math.prod(shape) * jnp.dtype(dtype).itemsize


def _hbm(x):
    return pltpu.with_memory_space_constraint(x, pltpu.HBM)


def _out(shape, dtype):
    return pltpu.HBM(tuple(shape), dtype)


def _dg(a, b, ca, cb):
    return lax.dot_general(a.astype(BF16), b.astype(BF16), (((ca,), (cb,)), ((), ())),
                           preferred_element_type=F32)


@jax.custom_vjp
def _bdot_nn(a, b):
    return _dg(a, b, 1, 0)


def _bdot_nn_fwd(a, b):
    return _dg(a, b, 1, 0), (a, b)


def _bdot_nn_bwd(res, ct):
    a, b = res
    return _dg(ct, b, 1, 1).astype(a.dtype), _dg(a, ct, 0, 0).astype(b.dtype)


_bdot_nn.defvjp(_bdot_nn_fwd, _bdot_nn_bwd)


@jax.custom_vjp
def _bdot_nt(a, b):
    return _dg(a, b, 1, 1)


def _bdot_nt_fwd(a, b):
    return _dg(a, b, 1, 1), (a, b)


def _bdot_nt_bwd(res, ct):
    a, b = res
    return _dg(ct, b, 1, 0).astype(a.dtype), _dg(ct, a, 0, 0).astype(b.dtype)


_bdot_nt.defvjp(_bdot_nt_fwd, _bdot_nt_bwd)


@functools.partial(jax.custom_vjp, nondiff_argnums=(1,))
def _lane_roll(x, shift):
    return pltpu.roll(x, shift % x.shape[1], axis=1)


def _lane_roll_fwd(x, shift):
    return _lane_roll(x, shift), None


def _lane_roll_bwd(shift, _, ct):
    return (_lane_roll(ct, -shift),)


_lane_roll.defvjp(_lane_roll_fwd, _lane_roll_bwd)


@functools.partial(jax.custom_vjp, nondiff_argnums=(1, 2))
def _cols(x, lo, hi):
    return x[:, lo:hi]


def _cols_fwd(x, lo, hi):
    return x[:, lo:hi], x.shape[1]


def _cols_bwd(lo, hi, width, ct):
    parts = []
    if lo > 0:
        parts.append(jnp.zeros((ct.shape[0], lo), ct.dtype))
    parts.append(ct)
    if hi < width:
        parts.append(jnp.zeros((ct.shape[0], width - hi), ct.dtype))
    return (jnp.concatenate(parts, axis=1) if len(parts) > 1 else ct,)


_cols.defvjp(_cols_fwd, _cols_bwd)


MM_VMEM_BUDGET = 22 * 1024 * 1024


def _mm(a, b, *, name, ta=False, tb=False, out_dtype=F32, lead=None, add=None, add_scale=1.0,
        tm=1024, tn=1024, tk=8192, ride=None):
    if ta:
        K, M = a.shape
    else:
        M, K = a.shape
    bshape = b.shape[1:] if lead is not None else b.shape
    if tb:
        N, Kb = bshape
    else:
        Kb, N = bshape
    assert K == Kb, (name, a.shape, b.shape)

    def blocks(tm, tn, tk):
        tm = _tile(M, tm, LANE if ta else 16)
        tn = _tile(N, tn, LANE)
        tk = _tile(K, tk, LANE)
        nbytes = (tm * tk * a.dtype.itemsize + tk * tn * b.dtype.itemsize
                  + tm * tn * (jnp.dtype(out_dtype).itemsize + (4 if K // tk > 1 else 0)
                               + (add.dtype.itemsize if add is not None else 0)))
        return tm, tn, tk, nbytes

    if ta:
        tm = min(tm, max(LANE, M // 4))
    tm, tn, tk, est = blocks(tm, tn, tk)
    for shrink in ("m", "k", "m", "k", "n"):
        if est <= MM_VMEM_BUDGET:
            break
        if shrink == "m":
            tm, tn, tk, est = blocks(max(tm // 2, LANE), tn, tk)
        elif shrink == "k":
            tm, tn, tk, est = blocks(tm, tn, max(tk // 2, LANE))
        else:
            tm, tn, tk, est = blocks(tm, max(tn // 2, LANE), tk)
    nk = K // tk
    ca = 0 if ta else 1
    cb = 1 if tb else 0

    def body(*refs):
        a_ref, b_ref = refs[0], refs[1]
        add_ref = refs[2] if add is not None else None
        o_ref = refs[3] if add is not None else refs[2]

        def finish(r):
            if add_ref is not None:
                r = r + add_scale * add_ref[...].astype(F32)
            o_ref[...] = r.astype(o_ref.dtype)

        if nk == 1:
            finish(_dg(a_ref[...], b_ref[...], ca, cb))
            return
        acc_ref = refs[-1]
        k = pl.program_id(2)

        @pl.when(k == 0)
        def _():
            acc_ref[...] = jnp.zeros_like(acc_ref)

        acc_ref[...] += _dg(a_ref[...], b_ref[...], ca, cb)

        @pl.when(k == nk - 1)
        def _():
            finish(acc_ref[...])

    a_blk = (tk, tm) if ta else (tm, tk)
    a_map = (lambda i, j, k: (k, i)) if ta else (lambda i, j, k: (i, k))
    b_blk = (tn, tk) if tb else (tk, tn)
    if lead is None:
        b_map = (lambda i, j, k: (j, k)) if tb else (lambda i, j, k: (k, j))
        b_spec = pl.BlockSpec(b_blk, b_map)
    else:
        b_map = (lambda i, j, k: (lead, j, k)) if tb else (lambda i, j, k: (lead, k, j))
        b_spec = pl.BlockSpec((None,) + b_blk, b_map)
    in_specs = [pl.BlockSpec(a_blk, a_map), b_spec]
    args = [a, b]
    if add is not None:
        in_specs.append(pl.BlockSpec((tm, tn), lambda i, j, k: (i, j)))
        args.append(add)
    (out,), rode = _host_call(
        body, name=name,
        grid=(M // tm, N // tn, nk),
        in_specs=in_specs,
        out_specs=[pl.BlockSpec((tm, tn), lambda i, j, k: (i, j))],
        out_shape=[_out((M, N), out_dtype)],
        scratch=[pltpu.VMEM((tm, tn), F32)] if nk > 1 else [],
        args=[_hbm(v) for v in args], sem=("parallel", "parallel", "arbitrary"), est=est + tm * tn * 4,
        ride=ride)
    return out if ride is None else (out, rode)


def _rowwise(fn, tiles, params, tile_outs, acc_outs=(), *, tm, name):
    tile_arrays, tile_specs = [], []
    for t in tiles:
        if isinstance(t, tuple):
            tile_arrays.append(t[0])
            tile_specs.append(t[1])
        else:
            tile_arrays.append(t)
            tile_specs.append(pl.BlockSpec((tm, t.shape[1]), lambda i: (i, 0)))
    T = tile_arrays[0].shape[0]
    nt, np_, nto, nao = len(tile_arrays), len(params), len(tile_outs), len(acc_outs)

    def body(*refs):
        i = pl.program_id(0)
        tvals = [r[...] for r in refs[:nt]]
        pvals = [r[...] for r in refs[nt:nt + np_]]
        to_refs = refs[nt + np_:nt + np_ + nto]
        ao_refs = refs[nt + np_ + nto:]
        touts, aouts = fn(i, tvals, pvals)
        for r, v in zip(to_refs, touts):
            r[...] = v.astype(r.dtype)
        if nao:
            @pl.when(i == 0)
            def _():
                for r in ao_refs:
                    r[...] = jnp.zeros_like(r)
            for r, v in zip(ao_refs, aouts):
                r[...] += v.astype(r.dtype)

    in_specs = tile_specs + [pl.BlockSpec(p.shape, lambda i: (0, 0)) for p in params]
    out_specs = [pl.BlockSpec((tm, c), lambda i: (i, 0)) for c, _ in tile_outs]
    out_specs += [pl.BlockSpec(s, lambda i: (0, 0)) for s, _ in acc_outs]
    out_shape = [_out((T, c), d) for c, d in tile_outs]
    out_shape += [_out(s, d) for s, d in acc_outs]
    width = sum(s.block_shape[-1] for s in tile_specs) + sum(c for c, _ in tile_outs)
    est = 6 * tm * width * 4 + sum(_nbytes(p.shape, F32) for p in params) * 4
    return pl.pallas_call(
        body, name=name, grid=(T // tm,),
        in_specs=in_specs, out_specs=out_specs, out_shape=out_shape,
        compiler_params=_params(("arbitrary",) if nao else ("parallel",), est),
    )(*[_hbm(v) for v in tile_arrays], *[_hbm(p) for p in params])


def _ln_fn(alpha, s, xres, y, g, b):
    z = alpha * xres.astype(F32) + s * y.astype(F32)
    mu = jnp.mean(z, axis=-1, keepdims=True)
    zc = z - mu
    var = jnp.mean(zc * zc, axis=-1, keepdims=True)
    return zc * lax.rsqrt(var + LN_EPS) * g + b


def _mm_ln(a, b, lead, xres, g, bias, *, alpha, s, name):
    M, K = a.shape
    N = b.shape[2]
    tm = _tile(M, 512, 16)

    def body(a_ref, b_ref, x_ref, g_ref, bias_ref, y_ref, xo_ref, xb_ref):
        y = _dg(a_ref[...], b_ref[...], 1, 0)
        y_ref[...] = y.astype(y_ref.dtype)
        out = _ln_fn(alpha, s, x_ref[...], y, g_ref[...], bias_ref[...])
        xo_ref[...] = out
        xb_ref[...] = out.astype(BF16)

    row = pl.BlockSpec((tm, N), lambda i: (i, 0))
    vec = pl.BlockSpec((1, N), lambda i: (0, 0))
    est = tm * K * 2 + K * N * 2 + tm * N * (4 + 4 + 4 + 2 + 8)
    return pl.pallas_call(
        body, name=name, grid=(M // tm,),
        in_specs=[pl.BlockSpec((tm, K), lambda i: (i, 0)), pl.BlockSpec((None, K, N), lambda i: (lead, 0, 0)),
                  row, vec, vec],
        out_specs=[row, row, row],
        out_shape=[_out((M, N), BF16), _out((M, N), F32), _out((M, N), BF16)],
        compiler_params=_params(("parallel",), est),
    )(_hbm(a), _hbm(b), _hbm(xres), _hbm(g), _hbm(bias))


def _ln_bwd_math(alpha, s, x, y, g, d):
    z = alpha * x + s * y.astype(F32)
    zc = z - jnp.mean(z, axis=-1, keepdims=True)
    r = lax.rsqrt(jnp.mean(zc * zc, axis=-1, keepdims=True) + LN_EPS)
    xh = zc * r
    dxh = d * g
    dz = r * (dxh - jnp.mean(dxh, axis=-1, keepdims=True) - xh * jnp.mean(dxh * xh, axis=-1, keepdims=True))
    return alpha * dz, s * dz, jnp.sum(d * xh, axis=0, keepdims=True), jnp.sum(d, axis=0, keepdims=True)


def _mm_ln_bwd(a, b, lead, tb, add, xres, y, g, *, alpha, s, name):
    M, K = a.shape
    N = b.shape[1] if tb else b.shape[2]
    tk = K if K * N * 2 <= MM_VMEM_BUDGET * 3 // 5 else _tile(K, 2816, LANE)
    tm = _tile(M, 512 if K * N * 2 <= MM_VMEM_BUDGET // 4 else 256, 16)
    nk = K // tk
    cb = 1 if tb else 0

    def body(a_ref, b_ref, add_ref, x_ref, y_ref, g_ref, dx_ref, dy_ref, dg_ref, db_ref, *scratch):
        i, k = pl.program_id(0), pl.program_id(1)

        def finish(d):
            @pl.when(i == 0)
            def _():
                dg_ref[...] = jnp.zeros_like(dg_ref)
                db_ref[...] = jnp.zeros_like(db_ref)

            dx, dy, dg, db = _ln_bwd_math(alpha, s, x_ref[...], y_ref[...], g_ref[...], d + add_ref[...])
            dx_ref[...] = dx
            dy_ref[...] = dy.astype(dy_ref.dtype)
            dg_ref[...] += dg
            db_ref[...] += db

        if nk == 1:
            finish(_dg(a_ref[...], b_ref[...], 1, cb))
            return
        acc_ref = scratch[0]

        @pl.when(k == 0)
        def _():
            acc_ref[...] = jnp.zeros_like(acc_ref)

        acc_ref[...] += _dg(a_ref[...], b_ref[...], 1, cb)

        @pl.when(k == nk - 1)
        def _():
            finish(acc_ref[...])

    row = pl.BlockSpec((tm, N), lambda i, k: (i, 0))
    vec = pl.BlockSpec((1, N), lambda i, k: (0, 0))
    b_spec = (pl.BlockSpec((None, N, tk), lambda i, k: (lead, 0, k)) if tb
              else pl.BlockSpec((None, tk, N), lambda i, k: (lead, k, 0)))
    est = tm * tk * 2 + tk * N * 2 + tm * N * (4 + 4 + 2 + 4 + 2 + 4 + 12)
    return pl.pallas_call(
        body, name=name, grid=(M // tm, nk),
        in_specs=[pl.BlockSpec((tm, tk), lambda i, k: (i, k)), b_spec, row, row, row, vec],
        out_specs=[row, row, vec, vec],
        out_shape=[_out((M, N), F32), _out((M, N), BF16), _out((1, N), F32), _out((1, N), F32)],
        scratch_shapes=[pltpu.VMEM((tm, N), F32)] if nk > 1 else [],
        compiler_params=_params(("arbitrary", "arbitrary"), est),
    )(_hbm(a), _hbm(b), _hbm(add), _hbm(xres), _hbm(y), _hbm(g))


def _loss_ln_bwd(xres, y, g, out, target, *, alpha, s, name):
    T, D = xres.shape
    tm = _tile(T, 256, 16)

    def body(x_ref, y_ref, o_ref, t_ref, g_ref, dx_ref, dy_ref, dg_ref, db_ref, loss_ref):
        @pl.when(pl.program_id(0) == 0)
        def _():
            dg_ref[...] = jnp.zeros_like(dg_ref)
            db_ref[...] = jnp.zeros_like(db_ref)
            loss_ref[...] = jnp.zeros_like(loss_ref)

        err = o_ref[...] - t_ref[...]
        part = 0.5 * jnp.sum(jnp.sum(err * err, axis=1, keepdims=True) / D, axis=0, keepdims=True)
        loss_ref[...] += jnp.broadcast_to(part, loss_ref.shape)
        dx, dy, dg, db = _ln_bwd_math(alpha, s, x_ref[...], y_ref[...], g_ref[...], err / D)
        dx_ref[...] = dx
        dy_ref[...] = dy.astype(dy_ref.dtype)
        dg_ref[...] += dg
        db_ref[...] += db

    row = pl.BlockSpec((tm, D), lambda i: (i, 0))
    vec = pl.BlockSpec((1, D), lambda i: (0, 0))
    return pl.pallas_call(
        body, name=name, grid=(T // tm,),
        in_specs=[row, row, row, row, vec],
        out_specs=[row, row, vec, vec, pl.BlockSpec((8, LANE), lambda i: (0, 0))],
        out_shape=[_out((T, D), F32), _out((T, D), BF16), _out((1, D), F32), _out((1, D), F32),
                   _out((8, LANE), F32)],
        compiler_params=_params(("arbitrary",), 14 * tm * D * 4),
    )(_hbm(xres), _hbm(y), _hbm(out), _hbm(target), _hbm(g))


FFN_TILE = 256


def _interleave(w, axis):
    n = w.shape[axis] // (2 * FFN_TILE)
    shp = w.shape[:axis] + (2, n, FFN_TILE) + w.shape[axis + 1:]
    return jnp.swapaxes(w.reshape(shp), axis, axis + 1).reshape(w.shape)


def _deinterleave(w, axis):
    n = w.shape[axis] // (2 * FFN_TILE)
    shp = w.shape[:axis] + (n, 2, FFN_TILE) + w.shape[axis + 1:]
    return jnp.swapaxes(w.reshape(shp), axis, axis + 1).reshape(w.shape)


def _ffn_up(xb, w13t, lead, *, name, ride=None):
    T, D = xb.shape
    F = w13t.shape[1] // 2
    tc = FFN_TILE
    tm = _tile(T, 2048, 16)

    def body(x_ref, w_ref, h_ref, a_ref):
        h = _dg(x_ref[...], w_ref[...], 1, 1)
        g, u = h[:, :tc], h[:, tc:]
        h_ref[...] = h.astype(h_ref.dtype)
        a_ref[...] = (g * jax.nn.sigmoid(g) * u).astype(a_ref.dtype)

    est = (tm * D + 2 * tc * D + 3 * tm * tc) * 2 + 3 * tm * tc * 4
    (h13, a), gathered = _host_call(
        body, name=name, grid=(T // tm, F // tc),
        in_specs=[pl.BlockSpec((tm, D), lambda i, j: (i, 0)),
                  pl.BlockSpec((None, 2 * tc, D), lambda i, j: (lead, j, 0))],
        out_specs=[pl.BlockSpec((tm, 2 * tc), lambda i, j: (i, j)),
                   pl.BlockSpec((tm, tc), lambda i, j: (i, j))],
        out_shape=[_out((T, 2 * F), BF16), _out((T, F), BF16)],
        args=[_hbm(xb), _hbm(w13t)], sem=("parallel", "parallel"), est=est, ride=ride)
    return h13, a, gathered


def _ffn_down_bwd(dyb, w2, lead, h13, *, name, ride=None):
    T, D = dyb.shape
    F = w2.shape[1]
    tc = FFN_TILE
    tm = _tile(T, 2048, 16)

    def body(dy_ref, w_ref, h_ref, dh_ref):
        d = _dg(dy_ref[...], w_ref[...], 1, 1)
        h = h_ref[...].astype(F32)
        g, u = h[:, :tc], h[:, tc:]
        sig = jax.nn.sigmoid(g)
        gs = g * sig
        dh_ref[...] = jnp.concatenate([d * u * (sig + gs * (1.0 - sig)), d * gs], axis=1).astype(dh_ref.dtype)

    est = (tm * D + tc * D + 4 * tm * tc) * 2 + 6 * tm * tc * 4
    (dh,), rode = _host_call(
        body, name=name, grid=(T // tm, F // tc),
        in_specs=[pl.BlockSpec((tm, D), lambda i, j: (i, 0)),
                  pl.BlockSpec((None, tc, D), lambda i, j: (lead, j, 0)),
                  pl.BlockSpec((tm, 2 * tc), lambda i, j: (i, j))],
        out_specs=[pl.BlockSpec((tm, 2 * tc), lambda i, j: (i, j))],
        out_shape=[_out((T, 2 * F), BF16)],
        args=[_hbm(dyb), _hbm(w2), _hbm(h13)], sem=("parallel", "parallel"), est=est, ride=ride)
    return dh, rode


def _pool_select(parts, pw):
    pg = pw // len(POOL_WINDOWS)
    grp = lax.broadcasted_iota(jnp.int32, parts[0].shape, 1) // pg
    out = parts[3]
    for g in (2, 1, 0):
        out = jnp.where(grp == g, parts[g], out)
    return out


def _pool_count(t0, rows, pw):
    pg = pw // len(POOL_WINDOWS)
    grp = lax.broadcasted_iota(jnp.int32, (rows, pw), 1) // pg
    win = jnp.where(grp == 0, POOL_WINDOWS[0],
                    jnp.where(grp == 1, POOL_WINDOWS[1],
                              jnp.where(grp == 2, POOL_WINDOWS[2], POOL_WINDOWS[3])))
    t = t0 + lax.broadcasted_iota(jnp.int32, (rows, pw), 0)
    return jnp.minimum(t + 1, win).astype(F32)


def _window_sums(ext, up):
    n = ext.shape[0]
    sums, cur, k = [], ext, 1
    for _ in POOL_WINDOWS:
        cur = cur + pltpu.roll(cur, (n - k) if up else k, axis=0)
        sums.append(cur)
        k *= 2
    return sums


def _pool_delta(u, halo, t0):
    tm, pw = u.shape
    ext = jnp.concatenate([halo, u], axis=0)
    sums = [s[POOL_HALO:, :] for s in _window_sums(ext, up=False)]
    return _pool_select(sums, pw) / _pool_count(t0, tm, pw) - u


def _pool_fwd(hin, wbd, scale, cat, *, name):
    T = hin.shape[0]
    pw = wbd.shape[0]
    tm = _tile(T, 256, POOL_HALO)
    per = tm // POOL_HALO

    def body(u_ref, halo_ref, w_ref, s_ref, cat_ref, y_ref):
        i = pl.program_id(0)
        halo = jnp.where(i > 0, halo_ref[...], 0.0)
        d = _pool_delta(u_ref[...], halo, i * tm)
        y_ref[...] = (_dg(d, w_ref[...], 1, 0) * s_ref[...]).astype(y_ref.dtype)

    return pl.pallas_call(
        body, name=name, grid=(T // tm,),
        in_specs=[pl.BlockSpec((tm, pw), lambda i: (i, 0)),
                  pl.BlockSpec((POOL_HALO, pw), lambda i: (jnp.maximum(i * per - 1, 0), 0)),
                  pl.BlockSpec((pw, pw), lambda i: (0, 0)),
                  pl.BlockSpec((1, pw), lambda i: (0, 0)),
                  ANY],
        out_specs=pl.BlockSpec((tm, pw), lambda i: (i, 0)),
        out_shape=_out(cat.shape, cat.dtype),
        input_output_aliases={4: 0},
        compiler_params=_params(("parallel",), 16 * tm * pw * 4),
    )(_hbm(hin), _hbm(hin), _hbm(wbd), _hbm(scale), _hbm(cat))


def _pool_bwd(hin, dcat, wbd, scale, *, name):
    T = hin.shape[0]
    pw = wbd.shape[0]
    tm = _tile(T, 256, POOL_HALO)
    per = tm // POOL_HALO
    nt = T // tm

    def body(u_ref, halo_ref, dy_ref, dyn_ref, w_ref, s_ref, du_ref, dw_ref, ds_ref):
        i = pl.program_id(0)

        @pl.when(i == 0)
        def _():
            dw_ref[...] = jnp.zeros_like(dw_ref)
            ds_ref[...] = jnp.zeros_like(ds_ref)

        halo = jnp.where(i > 0, halo_ref[...], 0.0)
        d = _pool_delta(u_ref[...], halo, i * tm)
        w = w_ref[...]
        sc = s_ref[...]
        dy = dy_ref[...]
        dyn = jnp.where(i < nt - 1, dyn_ref[...], 0.0)
        ds_ref[...] += jnp.sum(dy * _dg(d, w, 1, 0), axis=0, keepdims=True)
        dys = dy * sc
        dw_ref[...] += _dg(d, dys, 0, 0)
        dys_ext = jnp.concatenate([dys, dyn * sc], axis=0)
        dd_ext = _dg(dys_ext, w, 1, 1)
        ddp = dd_ext / _pool_count(i * tm, tm + POOL_HALO, pw)
        sums = [s[:tm, :] for s in _window_sums(ddp, up=True)]
        du_ref[...] = _pool_select(sums, pw) - dd_ext[:tm, :]

    return pl.pallas_call(
        body, name=name, grid=(nt,),
        in_specs=[pl.BlockSpec((tm, pw), lambda i: (i, 0)),
                  pl.BlockSpec((POOL_HALO, pw), lambda i: (jnp.maximum(i * per - 1, 0), 0)),
                  pl.BlockSpec((tm, pw), lambda i: (i, 0)),
                  pl.BlockSpec((POOL_HALO, pw), lambda i: (jnp.minimum((i + 1) * per, nt * per - 1), 0)),
                  pl.BlockSpec((pw, pw), lambda i: (0, 0)),
                  pl.BlockSpec((1, pw), lambda i: (0, 0))],
        out_specs=[pl.BlockSpec((tm, pw), lambda i: (i, 0)),
                   pl.BlockSpec((pw, pw), lambda i: (0, 0)),
                   pl.BlockSpec((1, pw), lambda i: (0, 0))],
        out_shape=[_out((T, pw), F32),
                   _out((pw, pw), F32),
                   _out((1, pw), F32)],
        compiler_params=_params(("arbitrary",), 24 * tm * pw * 4),
    )(_hbm(hin), _hbm(hin), _hbm(dcat), _hbm(dcat), _hbm(wbd), _hbm(scale))


def _rms(x, g):
    return x * lax.rsqrt(jnp.mean(x * x, axis=-1, keepdims=True) + RMS_EPS) * g


def _norms_fn(pw, h, gq, gkv):
    o1 = pw + Q_LORA
    o2 = o1 + KV_LORA
    return (_rms(_cols(h, pw, o1), gq), _rms(_cols(h, o1, o2), gkv), _cols(h, o2, h.shape[1]))


def _norms_bwd(hin, gq, gkv, dcq, dckv, dkpe, du, *, pw, name):
    tm = _tile(hin.shape[0], 256, 16)
    dinp = hin.shape[1]

    def fn(i, tv, pv):
        _, vjp = jax.vjp(functools.partial(_norms_fn, pw), tv[0], pv[0], pv[1])
        dh, dgq, dgkv = vjp((tv[1].astype(F32), tv[2].astype(F32), tv[3].astype(F32)))
        dh = jnp.concatenate([tv[4], dh[:, pw:]], axis=1)
        return (dh,), (dgq, dgkv)

    return _rowwise(fn, [hin, dcq, dckv, dkpe, du], [gq, gkv], [(dinp, BF16)],
                    [((1, Q_LORA), F32), ((1, KV_LORA), F32)], tm=tm, name=name)


def _heads_fn(H, qraw, kv, kpe, rc, rs1, rs2):
    half = QK_ROPE // 2
    scale = (QK_NOPE + QK_ROPE) ** -0.5

    def rope(blk):
        return blk * rc + _lane_roll(blk, -half) * rs1 + _lane_roll(blk, half) * rs2

    krot = rope(kpe)
    qs, ks, vs = [], [], []
    for h in range(H):
        lo = h * HEAD_PAD
        qs += [_cols(qraw, lo, lo + LANE) * scale, rope(_cols(qraw, lo + LANE, lo + HEAD_PAD)) * scale]
        ks += [_cols(kv, lo, lo + LANE), krot]
        vs += [_cols(kv, lo + LANE, lo + HEAD_PAD)]
    return jnp.concatenate(qs, axis=1), jnp.concatenate(ks, axis=1), jnp.concatenate(vs, axis=1)


def _heads_fwd(hin, gq, gkv, tabs, wuq, wukv, *, H, pw, name):
    tm = _tile(hin.shape[0], 256, 16)

    def fn(i, tv, pv):
        cqn, ckvn, kpe = _norms_fn(pw, tv[0], pv[0], pv[1])
        qraw = _dg(cqn, pv[2], 1, 1)
        kv = _dg(ckvn, pv[3], 1, 1)
        return (*_heads_fn(H, qraw, kv, kpe, *tv[1:]), cqn, ckvn), ()

    return _rowwise(fn, [hin, *tabs], [gq, gkv, wuq, wukv],
                    [(H * HEAD_PAD, BF16), (H * HEAD_PAD, BF16), (H * V_HEAD, BF16), (Q_LORA, BF16),
                     (KV_LORA, BF16)], tm=tm, name=name)


def _heads_bwd(dq, dk, dv, tabs, *, H, name):
    tm = _tile(dq.shape[0], 256, 16)

    def fn(i, tv, pv):
        z = jnp.zeros((tm, H * HEAD_PAD), F32)
        zk = jnp.zeros((tm, LANE), F32)
        rc, rs1, rs2 = tv[3], tv[4], tv[5]
        _, vjp = jax.vjp(lambda a, b, c: _heads_fn(H, a, b, c, rc, rs1, rs2), z, z, zk)
        return vjp((tv[0].astype(F32), tv[1].astype(F32), tv[2].astype(F32))), ()

    return _rowwise(fn, [dq, dk, dv, *tabs], [],
                    [(H * HEAD_PAD, BF16), (H * HEAD_PAD, BF16), (LANE, F32)], tm=tm, name=name)


def _diag_mask(rows, cols, row0):
    r = (row0 + lax.broadcasted_iota(jnp.int32, (rows, cols), 0)) // CHUNK
    c = lax.broadcasted_iota(jnp.int32, (rows, cols), 1) // CHUNK
    return r >= c


def _flash_fwd(qh, kh, vh, *, H, pw, name, ride=None):
    T = qh.shape[0]
    t = _tile(T, 512, CHUNK)
    off = pw // V_HEAD


    half = t

    def body(q_ref, k_ref, v_ref, o_ref, lse_ref):
        i = pl.program_id(1)
        q = q_ref[...]

        def update(carry, s, v):
            m, l, acc = carry
            mn = jnp.maximum(m, jnp.max(s, axis=1, keepdims=True))
            p = jnp.exp(s - mn)
            corr = jnp.exp(m - mn)
            return mn, corr * l + jnp.sum(p, axis=1, keepdims=True), corr * acc + _dg(p, v, 1, 0)

        def blk(j, carry):
            rows = pl.ds(pl.multiple_of(j * t, t), t)
            return update(carry, _dg(q, k_ref[rows, :], 1, 1), v_ref[rows, :])

        init = (jnp.full((t, 1), NEG_INF, F32), jnp.zeros((t, 1), F32), jnp.zeros((t, V_HEAD), F32))
        carry = lax.fori_loop(0, i, blk, init)
        done = []
        for r0 in range(0, t, half):
            keys = pl.ds(pl.multiple_of(i * t, t), r0 + half)
            s = _dg(q[r0:r0 + half, :], k_ref[keys, :], 1, 1)
            s = jnp.where(_diag_mask(half, r0 + half, r0), s, NEG_INF)
            done.append(update(tuple(c[r0:r0 + half] for c in carry), s, v_ref[keys, :]))
        m, l, acc = (jnp.concatenate(parts, axis=0) for parts in zip(*done))
        o_ref[...] = (acc / l).astype(o_ref.dtype)
        lse_ref[...] = jnp.broadcast_to(m + jnp.log(l), (t, V_HEAD))

    est = 2 * T * (HEAD_PAD + V_HEAD) * 2 + 8 * t * t * 4
    (o, lse), gathered = _host_call(
        body, name=name, grid=(H, T // t),
        in_specs=[pl.BlockSpec((t, HEAD_PAD), lambda h, i: (i, h)),
                  pl.BlockSpec((T, HEAD_PAD), lambda h, i: (0, h)),
                  pl.BlockSpec((T, V_HEAD), lambda h, i: (0, h))],
        out_specs=[pl.BlockSpec((t, V_HEAD), lambda h, i: (i, off + h)),
                   pl.BlockSpec((t, V_HEAD), lambda h, i: (i, h))],
        out_shape=[_out((T, pw + H * V_HEAD), BF16),
                   _out((T, H * V_HEAD), F32)],
        args=[_hbm(qh), _hbm(kh), _hbm(vh)], sem=("parallel", "parallel"), est=est, ride=ride)
    return o, lse, gathered


def _flash_bwd(qh, kh, vh, cat, dcat, lse, *, H, pw, name, ride=None):
    T = qh.shape[0]
    t = _tile(T, 512, CHUNK)
    nb = T // t
    off = pw // V_HEAD
    half = t

    def body(q_ref, k_ref, v_ref, o_ref, do_ref, lse_ref, dq_out_ref, dk_ref, dv_ref, dq_ref):
        j = pl.program_id(1)

        @pl.when(j == 0)
        def _():
            dq_ref[...] = jnp.zeros_like(dq_ref)

        kj = k_ref[...]
        vj = v_ref[...]

        def pair(rows, kx, vx, mask):
            qi = q_ref[rows, :]
            doi = do_ref[rows, :]
            oi = o_ref[rows, :].astype(F32)
            lsei = lse_ref[rows, :][:, :1]
            s = _dg(qi, kx, 1, 1)
            if mask is not None:
                s = jnp.where(mask, s, NEG_INF)
            p = jnp.exp(s - lsei)
            dp = _dg(doi, vx, 1, 1)
            di = jnp.sum(doi * oi, axis=1, keepdims=True)
            ds = p * (dp - di)
            dq_ref[rows, :] += _dg(ds, kx, 1, 0)
            return _dg(ds, qi, 0, 0), _dg(p, doi, 0, 0)

        def blk(i, carry):
            dk, dv = pair(pl.ds(pl.multiple_of(i * t, t), t), kj, vj, None)
            return carry[0] + dk, carry[1] + dv

        dk, dv = jnp.zeros((t, HEAD_PAD), F32), jnp.zeros((t, V_HEAD), F32)
        for r0 in range(0, t, half):
            n = r0 + half
            dkp, dvp = pair(pl.ds(pl.multiple_of(j * t + r0, half), half), kj[:n], vj[:n],
                            _diag_mask(half, n, r0))
            if n < t:
                dkp = jnp.concatenate([dkp, jnp.zeros((t - n, HEAD_PAD), F32)], axis=0)
                dvp = jnp.concatenate([dvp, jnp.zeros((t - n, V_HEAD), F32)], axis=0)
            dk, dv = dk + dkp, dv + dvp
        dk, dv = lax.fori_loop(j + 1, nb, blk, (dk, dv))
        dk_ref[...] = dk.astype(dk_ref.dtype)
        dv_ref[...] = dv.astype(dv_ref.dtype)

        @pl.when(j == nb - 1)
        def _():
            dq_out_ref[...] = dq_ref[...].astype(dq_out_ref.dtype)

    est = T * (HEAD_PAD * 2 + V_HEAD * 2 + V_HEAD * 4 + V_HEAD * 4 + HEAD_PAD * 4) + 10 * t * t * 4
    (dq, dk, dv), gathered = _host_call(
        body, name=name, grid=(H, nb),
        in_specs=[pl.BlockSpec((T, HEAD_PAD), lambda h, j: (0, h)),
                  pl.BlockSpec((t, HEAD_PAD), lambda h, j: (j, h)),
                  pl.BlockSpec((t, V_HEAD), lambda h, j: (j, h)),
                  pl.BlockSpec((T, V_HEAD), lambda h, j: (0, off + h)),
                  pl.BlockSpec((T, V_HEAD), lambda h, j: (0, off + h)),
                  pl.BlockSpec((T, V_HEAD), lambda h, j: (0, h))],
        out_specs=[pl.BlockSpec((T, HEAD_PAD), lambda h, j: (0, h)),
                   pl.BlockSpec((t, HEAD_PAD), lambda h, j: (j, h)),
                   pl.BlockSpec((t, V_HEAD), lambda h, j: (j, h))],
        out_shape=[_out((T, H * HEAD_PAD), BF16),
                   _out((T, H * HEAD_PAD), BF16),
                   _out((T, H * V_HEAD), BF16)],
        scratch=[pltpu.VMEM((T, HEAD_PAD), F32)],
        args=[_hbm(v) for v in (qh, kh, vh, cat, dcat, lse)], sem=("arbitrary", "arbitrary"), est=est,
        ride=ride)
    return dq, dk, dv, gathered


def _mem_fn(q, k, v):
    hd = q.shape[1] // MEM_HEADS
    outs = []
    for h in range(MEM_HEADS):
        lo, hi = h * hd, (h + 1) * hd
        s = _bdot_nt(_cols(q, lo, hi), _cols(k, lo, hi)) * hd ** -0.5
        e = jnp.exp(s - lax.stop_gradient(jnp.max(s, axis=1, keepdims=True)))
        p = e / jnp.sum(e, axis=1, keepdims=True)
        outs.append(_bdot_nn(p, _cols(v, lo, hi)))
    return jnp.concatenate(outs, axis=1)


def _mem_fwd(q, k, v, *, name):
    T, D = q.shape
    tm = _tile(T, 256, 16)

    def fn(i, tv, pv):
        return (_mem_fn(tv[0], pv[0], pv[1]),), ()

    return _rowwise(fn, [q], [k, v], [(D, BF16)], tm=tm, name=name)[0]


def _mem_bwd(q, k, v, do, *, name):
    T, D = q.shape
    tm = _tile(T, 256, 16)

    def fn(i, tv, pv):
        _, vjp = jax.vjp(_mem_fn, tv[0], pv[0], pv[1])
        dq, dk, dv = vjp(tv[1].astype(F32))
        return (dq,), (dk, dv)

    return _rowwise(fn, [q, do], [k, v], [(D, BF16)], [(k.shape, F32), (v.shape, F32)], tm=tm, name=name)


def _adamw(w, g, m, v, *, name):
    shape = w.shape
    if w.ndim != 3:
        lead3 = (1, math.prod(shape[:-1]), shape[-1])
        return [o.reshape(shape) for o in _adamw(*[a.reshape(lead3) for a in (w, g, m, v)], name=name)]
    Lw, R, C = shape
    tr = _tile(R, 512, 8)
    b1c = 1.0 - ADAM_B1 ** ADAM_STEP
    b2c = 1.0 - ADAM_B2 ** ADAM_STEP

    def body(w_ref, g_ref, m_ref, v_ref, d_ref, mo_ref, vo_ref):
        gg = g_ref[...]
        mn = ADAM_B1 * m_ref[...] + (1.0 - ADAM_B1) * gg
        vn = ADAM_B2 * v_ref[...] + (1.0 - ADAM_B2) * (gg * gg)
        d_ref[...] = -ADAM_LR * ((mn / b1c) / (jnp.sqrt(vn / b2c) + ADAM_EPS) + ADAM_WD * w_ref[...])
        mo_ref[...] = mn
        vo_ref[...] = vn

    spec = pl.BlockSpec((None, tr, C), lambda l, i: (l, i, 0))
    return pl.pallas_call(
        body, name=name, grid=(Lw, R // tr),
        in_specs=[spec] * 4, out_specs=[spec] * 3,
        out_shape=[_out(shape, F32)] * 3,
        compiler_params=_params(("parallel", "parallel"), 7 * tr * C * 4),
    )(*[_hbm(a) for a in (w, g, m, v)])


def _pair_sum(core, gs, landed, offs, *, name):
    n = len(gs)
    _, R, C = landed.shape
    rows = [g.shape[0] // N_DEV for g in gs]

    def body(core_ref, *refs):
        g_refs, l_ref, o_ref = refs[:n], refs[n], refs[n + 1]
        for g_ref, off, r in zip(g_refs, offs, rows):
            o_ref[off:off + r, :] = (g_ref[...].astype(F32) + l_ref[off:off + r, :].astype(F32)).astype(o_ref.dtype)

    slab = pl.BlockSpec((None, R, C), lambda p, core_ref: (p, 0, 0))
    own = [pl.BlockSpec((r, C), lambda p, core_ref: (2 * p + core_ref[0], 0)) for r in rows]
    return pl.pallas_call(
        body, name=name,
        grid_spec=pltpu.PrefetchScalarGridSpec(
            num_scalar_prefetch=1, grid=(4,), in_specs=own + [slab], out_specs=slab),
        out_shape=_out(landed.shape, landed.dtype),
        input_output_aliases={n + 1: 0},
        compiler_params=_params(("arbitrary",), 3 * R * C * 2 + R * C * 8),
    )(core, *[_hbm(g) for g in gs], _hbm(landed))


def _quad_sum(chip, part, gathered, used, *, name):
    C = part.shape[2]
    R = used
    tr = _tile(R, 256, 16)

    def body(chip_ref, own_ref, a_ref, b_ref, c_ref, o_ref):
        o_ref[...] = ((own_ref[...].astype(F32) + a_ref[...].astype(F32)) + b_ref[...].astype(F32)) \
            + c_ref[...].astype(F32)

    def other(k):
        return pl.BlockSpec((None, tr, C), lambda i, chip_ref: (chip_ref[0] ^ k, i, 0))

    return pl.pallas_call(
        body, name=name,
        grid_spec=pltpu.PrefetchScalarGridSpec(
            num_scalar_prefetch=1, grid=(R // tr,),
            in_specs=[pl.BlockSpec((None, tr, C), lambda i, chip_ref: (chip_ref[0], i, 0)),
                      other(1), other(2), other(3)],
            out_specs=pl.BlockSpec((tr, C), lambda i, chip_ref: (i, 0))),
        out_shape=_out((R, C), F32),
        compiler_params=_params(("arbitrary",), 8 * tr * C * 4),
    )(chip, _hbm(part), _hbm(gathered), _hbm(gathered), _hbm(gathered))


def _place():
    x, y, c = lax.axis_index("x"), lax.axis_index("y"), lax.axis_index("c")
    return x, y, c


ANY = pl.BlockSpec(memory_space=pl.ANY)


class _Gather:
    def __init__(self, shards):
        self.shards = list(shards)
        self.n = len(self.shards)
        self.out_shape = [_out((s.shape[0], N_DEV * s.shape[1], s.shape[2]), s.dtype)
                          for s in self.shards]
        self.scratch = [pltpu.SemaphoreType.DMA((7 * self.n,)), pltpu.SemaphoreType.DMA((7 * self.n,)),
                        pltpu.SemaphoreType.DMA((self.n,))]
        self.operands = [_hbm(s) for s in self.shards]

    def _bind(self, refs):
        n = self.n
        ins, outs = refs[:n], refs[n:2 * n]
        send_sems, recv_sems, local_sems = refs[2 * n:]
        x, y, c = _place()
        me, sib = (x, y, c), (x, y, 1 - c)
        chips = [(1 - x, y), (x, 1 - y), (1 - x, 1 - y)]

        def rows(w, p):
            r = self.shards[w].shape[1]
            idx = 4 * p[0] + 2 * p[1] + p[2]
            return outs[w].at[:, pl.ds(pl.multiple_of(idx * r, 8), r), :]

        def copy(w, k, block, to, src=None):
            return pltpu.make_async_remote_copy(
                src_ref=rows(w, block) if src is None else src, dst_ref=rows(w, block),
                send_sem=send_sems.at[w * 7 + k], recv_sem=recv_sems.at[w * 7 + k],
                device_id=to, device_id_type=MESH)

        def mine():
            return [pltpu.make_async_copy(ins[w], rows(w, me), local_sems.at[w]) for w in range(n)]

        def first():
            out = []
            for w in range(n):
                out.append(copy(w, 0, me, sib, src=ins[w]))
                out += [copy(w, 1 + j, me, (*chip, c), src=ins[w]) for j, chip in enumerate(chips)]
            return out

        def passed():
            return [copy(w, 4 + j, (*chip, c), sib) for j, chip in enumerate(chips) for w in range(n)]

        def landed():
            return [copy(w, 1 + j, (*chip, c), me) for j, chip in enumerate(chips) for w in range(n)]

        def last():
            out = []
            for w in range(n):
                out.append(copy(w, 0, sib, me))
                out += [copy(w, 4 + j, (*chip, 1 - c), me) for j, chip in enumerate(chips)]
            return out

        return mine, first, landed, passed, last

    def start(self, refs):
        mine, first, _, _, _ = self._bind(refs)
        for cp in mine() + first():
            cp.start()

    def forward(self, refs):
        _, _, landed, passed, _ = self._bind(refs)
        for arrived, fwd in zip(landed(), passed()):
            arrived.wait_recv()
            fwd.start()

    def finish(self, refs):
        mine, first, _, passed, last = self._bind(refs)
        for cp in last():
            cp.wait_recv()
        for cp in first() + passed():
            cp.wait_send()
        for cp in mine():
            cp.wait()


class _ChipExchange:
    def __init__(self, parts, used):
        self.ncl = len(parts)
        self.used = list(used)
        self.out_shape = [_out(p.shape, p.dtype) for p in parts]
        self.scratch = [pltpu.SemaphoreType.DMA((3 * self.ncl,)), pltpu.SemaphoreType.DMA((3 * self.ncl,))]
        self.operands = [_hbm(p) for p in parts]
        self.n = self.ncl

    def _bind(self, refs):
        ncl = self.ncl
        ins, outs = refs[:ncl], refs[ncl:2 * ncl]
        send_sems, recv_sems = refs[2 * ncl:]
        x, y, c = _place()
        chips = [(1 - x, y), (x, 1 - y), (1 - x, 1 - y)]
        here = 2 * x + y

        def copies(outgoing):
            out = []
            for k in range(ncl):
                rows = pl.ds(0, self.used[k])
                for j, (cx, cy) in enumerate(chips):
                    there = 2 * cx + cy
                    src, dst = (there, here) if outgoing else (here, there)
                    out.append(pltpu.make_async_remote_copy(
                        src_ref=ins[k].at[src, rows, :], dst_ref=outs[k].at[dst, rows, :],
                        send_sem=send_sems.at[3 * k + j], recv_sem=recv_sems.at[3 * k + j],
                        device_id=(cx, cy, c), device_id_type=MESH))
            return out

        return copies

    def start(self, refs):
        for cp in self._bind(refs)(True):
            cp.start()

    def forward(self, refs):
        pass

    def finish(self, refs):
        copies = self._bind(refs)
        for cp in copies(False):
            cp.wait_recv()
        for cp in copies(True):
            cp.wait_send()


class _Both:
    def __init__(self, members):
        self.members = list(members)
        self.n = sum(m.n for m in self.members)
        self.out_shape = [s for m in self.members for s in m.out_shape]
        self.scratch = [s for m in self.members for s in m.scratch]
        self.operands = [o for m in self.members for o in m.operands]

    def split(self, arrays):
        out, a = [], 0
        for m in self.members:
            out.append(list(arrays[a:a + m.n]))
            a += m.n
        return out

    def _refs(self, refs):
        ins, outs = self.split(refs[:self.n]), self.split(refs[self.n:2 * self.n])
        scr, b = [], 2 * self.n
        for m in self.members:
            scr.append(list(refs[b:b + len(m.scratch)]))
            b += len(m.scratch)
        return [(*i, *o, *s) for i, o, s in zip(ins, outs, scr)]

    def start(self, refs):
        for m, r in zip(self.members, self._refs(refs)):
            m.start(r)

    def forward(self, refs):
        for m, r in zip(self.members, self._refs(refs)):
            m.forward(r)

    def finish(self, refs):
        for m, r in zip(self.members, self._refs(refs)):
            m.finish(r)


def _exchange_alone(ex, *, name):
    def body(*refs):
        ex.start(refs)
        ex.forward(refs)
        ex.finish(refs)

    return pl.pallas_call(
        body, name=name, in_specs=[ANY] * ex.n, out_specs=[ANY] * ex.n,
        out_shape=ex.out_shape, scratch_shapes=ex.scratch,
    )(*ex.operands)


def _host_call(body, *, name, grid, in_specs, out_specs, out_shape, args, sem, est, ride=None, scratch=()):
    scratch = list(scratch)
    if ride is None:
        outs = pl.pallas_call(body, name=name, grid=grid, in_specs=in_specs, out_specs=out_specs,
                              out_shape=out_shape, scratch_shapes=scratch,
                              compiler_params=_params(sem, est))(*args)
        return list(outs), []
    n_in, n_out, n, n_scr = len(in_specs), len(out_specs), ride.n, len(scratch)

    def full(*refs):
        ins, rin = refs[:n_in], refs[n_in:n_in + n]
        outs, rout = refs[n_in + n:n_in + n + n_out], refs[n_in + n + n_out:n_in + 2 * n + n_out]
        own = refs[n_in + 2 * n + n_out:n_in + 2 * n + n_out + n_scr]
        rrefs = (*rin, *rout, *refs[n_in + 2 * n + n_out + n_scr:])
        step, total = _ride(ride, rrefs, grid)
        body(*ins, *outs, *own)
        _ride_end(ride, rrefs, step, total)

    outs = pl.pallas_call(
        full, name=name, grid=grid,
        in_specs=list(in_specs) + [ANY] * n, out_specs=list(out_specs) + [ANY] * n,
        out_shape=list(out_shape) + ride.out_shape, scratch_shapes=scratch + ride.scratch,
        compiler_params=_params(("arbitrary",) * len(grid), est),
    )(*args, *ride.operands)
    return list(outs[:n_out]), list(outs[n_out:])


def _ride(ex, refs, grid):
    total = math.prod(grid)
    step = pl.program_id(0)
    for axis in range(1, len(grid)):
        step = step * grid[axis] + pl.program_id(axis)
    pl.when(step == 0)(lambda: ex.start(refs))
    return step, total


def _ride_end(ex, refs, step, total):
    pl.when(step == (3 * total) // 4)(lambda: ex.forward(refs))
    pl.when(step == total - 1)(lambda: ex.finish(refs))


def _class_layout(grads, classes):
    used = [0] * len(set(classes))
    offs = []
    for g, cl in zip(grads, classes):
        offs.append(used[cl])
        used[cl] += g.shape[0] // N_DEV
    return offs, used


def _rs_to_sibling(grads, classes, *, name):
    n = len(grads)
    offs, used = _class_layout(grads, classes)
    heights = used
    ncl = len(heights)
    cols = [next(g.shape[1] for g, cl in zip(grads, classes) if cl == k) for k in range(ncl)]

    def body(*refs):
        gs, land = refs[:n], refs[n:n + ncl]
        send_sems, recv_sems = refs[n + ncl:]
        x, y, c = _place()
        sib = (x, y, 1 - c)
        for p in range(4):
            for w in range(n):
                r = grads[w].shape[0] // N_DEV
                cl = classes[w]
                there = gs[w].at[pl.ds(pl.multiple_of((2 * p + 1 - c) * r, 8), r), :]
                pltpu.make_async_remote_copy(
                    src_ref=there, dst_ref=land[cl].at[p, pl.ds(offs[w], r), :],
                    send_sem=send_sems.at[cl * 4 + p], recv_sem=recv_sems.at[cl * 4 + p],
                    device_id=sib, device_id_type=MESH).start()
        for cl in range(ncl):
            for p in range(4):
                rows_used = land[cl].at[p, pl.ds(0, used[cl]), :]
                slab = pltpu.make_async_remote_copy(
                    src_ref=rows_used, dst_ref=rows_used,
                    send_sem=send_sems.at[cl * 4 + p], recv_sem=recv_sems.at[cl * 4 + p],
                    device_id=sib, device_id_type=MESH)
                slab.wait_send()
                slab.wait_recv()

    return pl.pallas_call(
        body, name=name,
        in_specs=[ANY] * n, out_specs=[ANY] * ncl,
        out_shape=[_out((4, heights[k], cols[k]), BF16) for k in range(ncl)],
        scratch_shapes=[pltpu.SemaphoreType.DMA((4 * ncl,))] * 2,
    )(*[_hbm(g) for g in grads])


def _sum_devices(g, *, name):
    R = g.shape[1]

    def body(g_ref, o_ref):
        acc = g_ref[0]
        for d in range(1, N_DEV):
            acc = acc + g_ref[d]
        o_ref[...] = acc

    vm = pl.BlockSpec(memory_space=pltpu.VMEM)
    return pl.pallas_call(
        body, name=name, in_specs=[vm], out_specs=vm,
        out_shape=jax.ShapeDtypeStruct((R, LANE), F32),
        compiler_params=pltpu.CompilerParams(vmem_limit_bytes=VMEM_FLOOR),
    )(g)


def _rope_tables(positions):
    half = QK_ROPE // 2
    inv_freq = ROPE_BASE ** (-jnp.arange(half, dtype=F32) / half)
    ang = positions.astype(F32)[:, None] * inv_freq
    cos, sin = jnp.cos(ang), jnp.sin(ang)
    z = jnp.zeros_like(cos)
    z2 = jnp.zeros((positions.shape[0], LANE - QK_ROPE), F32)
    rc = jnp.concatenate([cos, cos, z2], axis=1)
    rs1 = jnp.concatenate([-sin, z, z2], axis=1)
    rs2 = jnp.concatenate([z, sin, z2], axis=1)
    return rc, rs1, rs2


def _block_diag(pool_w):
    G, pg, _ = pool_w.shape
    out = jnp.zeros((G * pg, G * pg), pool_w.dtype)
    for g in range(G):
        out = lax.dynamic_update_slice(out, pool_w[g], (g * pg, g * pg))
    return out


def kernel(x, mem, positions, ln_g, ln_b, ffn1_w13, ffn1_w2, w_in, pool_w, pool_scale, q_norm_g, w_uq, kv_norm_g, w_ukv, w_out, mem_wq, mem_wkv, mem_wo, ffn2_w13, ffn2_w2, loss_target, m_ln_g, m_ln_b, m_ffn1_w13, m_ffn1_w2, m_w_in, m_pool_w, m_pool_scale, m_q_norm_g, m_w_uq, m_kv_norm_g, m_w_ukv, m_w_out, m_mem_wq, m_mem_wkv, m_mem_wo, m_ffn2_w13, m_ffn2_w2, v_ln_g, v_ln_b, v_ffn1_w13, v_ffn1_w2, v_w_in, v_pool_w, v_pool_scale, v_q_norm_g, v_w_uq, v_kv_norm_g, v_w_ukv, v_w_out, v_mem_wq, v_mem_wkv, v_mem_wo, v_ffn2_w13, v_ffn2_w2):
    L = ln_g.shape[0]
    T, D = x.shape[1], x.shape[2]
    F = ffn1_w2.shape[1] * N_DEV
    PW = D // 4
    H = (D - PW) // V_HEAD
    DIN = w_in.shape[2]
    DINP = PW + Q_LORA + KV_LORA + LANE
    QW = QK_NOPE + QK_ROPE
    alpha = (2 * L) ** 0.25
    x2d = x.reshape(T, D)
    memb = mem.reshape(mem.shape[1], D).astype(BF16)
    target = loss_target.reshape(T, D)
    tabs = _rope_tables(positions.reshape(T))

    def shards_of(l):
        return dict(
            w13a=ffn1_w13[l].T[None].astype(BF16),
            w13b=ffn2_w13[l].T[None].astype(BF16),
            w2a=ffn1_w2[l][None].astype(BF16),
            w2b=ffn2_w2[l][None].astype(BF16),
            wsq=jnp.stack([w_out[l], mem_wq[l], mem_wo[l]]).astype(BF16),
            wkvT=mem_wkv[l].T[None].astype(BF16),
            winp=jnp.pad(w_in[l], ((0, 0), (0, DINP - DIN)))[None].astype(BF16),
            wuqT=w_uq[l].T[None].astype(BF16),
            wukvT=w_ukv[l].T[None].astype(BF16),
        )

    SMALL = ("winp", "wuqT", "wukvT")
    shards = [shards_of(l) for l in range(L)]
    W = [dict() for _ in range(L)]

    def rider(spec):
        return _Gather([shards[l][n] for l, n in spec]) if spec else None

    def arrived(spec, arrays):
        for (l, n), a in zip(spec, arrays):
            if n in ("w13a", "w13b"):
                a = _interleave(a, 1)
            elif n == "wuqT":
                a = jnp.pad(a.reshape(H, QW, Q_LORA), ((0, 0), (0, HEAD_PAD - QW), (0, 0)))
                a = a.reshape(1, H * HEAD_PAD, Q_LORA)
            elif n == "ln":
                a = jnp.moveaxis(a.reshape(N_DEV, 2, L, 4, D // N_DEV), 0, 3).reshape(2, L, 4, D)
                LN["g"], LN["b"] = a[0], a[1]
            W[l][n] = a

    LN = {}
    shards[0]["ln"] = jnp.concatenate([ln_g.reshape(1, 4 * L, -1), ln_b.reshape(1, 4 * L, -1)], axis=1)
    spec0 = [(0, "w13a")]
    arrived(spec0, _exchange_alone(rider(spec0), name="ag_first"))
    wbd = [_block_diag(pool_w[l]).astype(BF16) for l in range(L)]

    def ffn_fwd(l, which, xres, xb, k, spec):
        ab = "ab"[which]
        h13, a, rode = _ffn_up(xb, W[l]["w13" + ab], 0, name=f"l{l}_ffn{which}_up", ride=rider(spec))
        arrived(spec, rode)
        y, xo, xob = _mm_ln(a, W[l]["w2" + ab], 0, xres, LN["g"][l,k:k + 1], LN["b"][l,k:k + 1], alpha=alpha, s=0.5,
                            name=f"l{l}_ffn{which}_y_ln{k}")
        return dict(xres=xres, xb=xb, h13=h13, a=a, y=y), xo, xob

    saved = []
    xres, xb = x2d, x2d.astype(BF16)
    for l in range(L):
        sv = {}
        more = l + 1 < L
        Wl = W[l]
        spec = ([(0, "w2a"), (0, "ln"), *[(0, n) for n in SMALL], (0, "wkvT")] if l == 0
                else [(l, "wsq"), (l, "wkvT")])
        sv["ffn1"], x1, x1b = ffn_fwd(l, 0, xres, xb, 0, spec)
        hin = _mm(x1b, Wl["winp"], lead=0, name=f"l{l}_hin")
        pscale = pool_scale[l].reshape(1, PW)
        gq, gkv = q_norm_g[l].reshape(1, Q_LORA), kv_norm_g[l].reshape(1, KV_LORA)
        qh, kh, vh, cqn, ckvn = _heads_fwd(hin, gq, gkv, tabs, Wl["wuqT"][0], Wl["wukvT"][0], H=H, pw=PW,
                                           name=f"l{l}_heads")
        spec = [(l, "w13b"), (l, "w2b")] + ([(0, "wsq")] if l == 0 else []) + ([(l + 1, "w13a")] if more else [])
        cat, lse, rode = _flash_fwd(qh, kh, vh, H=H, pw=PW, name=f"l{l}_flash", ride=rider(spec))
        arrived(spec, rode)
        cat = _pool_fwd(hin, wbd[l], pscale, cat, name=f"l{l}_pool")
        ymix, x2, x2b = _mm_ln(cat, Wl["wsq"], 0, x1, LN["g"][l,1:2], LN["b"][l,1:2], alpha=alpha, s=1.0,
                               name=f"l{l}_ymix_ln1")
        qm = _mm(x2b, Wl["wsq"], lead=1, out_dtype=BF16, name=f"l{l}_qm")
        kvm = _mm(memb, Wl["wkvT"], lead=0, tb=True, name=f"l{l}_kvm")
        km, vm = kvm[:, :D], kvm[:, D:]
        om = _mem_fwd(qm, km, vm, name=f"l{l}_memattn")
        ymem, x3, x3b = _mm_ln(om, Wl["wsq"], 2, x2, LN["g"][l,2:3], LN["b"][l,2:3], alpha=alpha, s=1.0,
                               name=f"l{l}_ymem_ln2")
        spec = [(l + 1, n) for n in ("w2a", *SMALL)] if more else []
        sv["ffn2"], x4, x4b = ffn_fwd(l, 1, x3, x3b, 3, spec)
        sv.update(x1=x1, x1b=x1b, hin=hin, pscale=pscale, gq=gq, gkv=gkv, cqn=cqn, ckvn=ckvn,
                  qh=qh, kh=kh, vh=vh, lse=lse, cat=cat, ymix=ymix, x2=x2, x2b=x2b, qm=qm, km=km, vm=vm,
                  om=om, ymem=ymem)
        saved.append(sv)
        xres, xb = x4, x4b


    gW = {}
    gS = {}

    def ln_of(l, k):
        sv = saved[l]
        x, y, s = {0: (sv["ffn1"]["xres"], sv["ffn1"]["y"], 0.5), 1: (sv["x1"], sv["ymix"], 1.0),
                   2: (sv["x2"], sv["ymem"], 1.0), 3: (sv["ffn2"]["xres"], sv["ffn2"]["y"], 0.5)}[k]
        return x, y, LN["g"][l, k:k + 1], s

    def dx_through_ln(a, b, lead, tb, add, into, name):
        x, y, g, s = ln_of(*into)
        dxres, dyb, dg, db = _mm_ln_bwd(a, b, lead, tb, add, x, y, g, alpha=alpha, s=s, name=name)
        gS[("ln_g", *into)], gS[("ln_b", *into)] = dg, db
        return dxres, dyb

    def ffn_bwd(l, which, sv, dxres, dyb, ride, into):
        tag = f"l{l}_ffn{which}"
        gW[("w2", which, l)] = _mm(sv["a"], dyb, ta=True, out_dtype=BF16, name=f"{tag}_dw2", tn=D)
        dh, rode = _ffn_down_bwd(dyb, W[l]["w2" + "ab"[which]], 0, sv["h13"], name=f"{tag}_dh", ride=ride)
        dw13 = _mm(dh, sv["xb"], ta=True, out_dtype=BF16, name=f"{tag}_dw13", tn=D)
        gW[("w13", which, l)] = _deinterleave(dw13, 0)
        w13 = W[l]["w13" + "ab"[which]]
        if into is not None:
            return dx_through_ln(dh, w13, 0, False, dxres, into, f"{tag}_dx"), rode
        last = rs_first_level(l, "c")
        dxn, got = _mm(dh, w13, lead=0, add=dxres, name=f"{tag}_dx", tn=D, ride=last["ex"])
        rs_last_level(last, got)
        return dxn, rode

    core = lax.axis_index("c").astype(jnp.int32).reshape(1)
    chip = (2 * lax.axis_index("x") + lax.axis_index("y")).astype(jnp.int32).reshape(1)
    gsh = {}

    def rs_first_level(l, group):
        keys, classes = {
            "a": ([("w13", 1, l), ("w2", 1, l), ("mem_wkv", l), ("mem_wq", l), ("mem_wo", l)], [0] * 5),
            "b": ([("w_out", l), ("w_in", l), ("w_uq", l), ("w_ukv", l)], [0, 1, 2, 3]),
            "c": ([("w13", 0, l), ("w2", 0, l)], [0, 0]),
        }[group]
        tag = f"l{l}{group}"
        garrs = []
        for key in keys:
            g = gW[key]
            if key[0] == "w_uq":
                g = g.reshape(H, HEAD_PAD, Q_LORA)[:, :QW, :].reshape(H * QW, Q_LORA)
            garrs.append(g)
        offs, used = _class_layout(garrs, classes)
        parts = list(_rs_to_sibling(garrs, classes, name=f"{tag}_rs_sibling"))
        for cl in range(len(parts)):
            mine = [w for w, c in enumerate(classes) if c == cl]
            parts[cl] = _pair_sum(core, [garrs[w] for w in mine], parts[cl], [offs[w] for w in mine],
                                  name=f"{tag}_rs_pair_sum{cl}")
        return dict(tag=tag, keys=keys, garrs=garrs, classes=classes, offs=offs, used=used, parts=parts,
                    ex=_ChipExchange(parts, used))

    def rs_last_level(st, gathered):
        sums = [_quad_sum(chip, p, a, u, name=f"{st['tag']}_rs_quad_sum{k}")
                for k, (p, a, u) in enumerate(zip(st["parts"], gathered, st["used"]))]
        for key, g, cl, off in zip(st["keys"], st["garrs"], st["classes"], st["offs"]):
            gsh[key] = sums[cl][off:off + g.shape[0] // N_DEV, :]

    top = (L - 1, 3)
    x_top, y_top, g_top, s_top = ln_of(*top)
    dxres, dyb, gS[("ln_g", *top)], gS[("ln_b", *top)], loss_blk = _loss_ln_bwd(
        x_top, y_top, g_top, xres, target, alpha=alpha, s=s_top, name="loss_ln_top_bwd")
    above = None
    for l in reversed(range(L)):
        sv = saved[l]
        Wl = W[l]
        (dxres, dyb), _ = ffn_bwd(l, 1, sv["ffn2"], dxres, dyb, None, (l, 2))
        dom = _mm(dyb, Wl["wsq"], lead=2, tb=True, out_dtype=BF16, name=f"l{l}_dom")
        gW[("mem_wo", l)] = _mm(sv["om"], dyb, ta=True, out_dtype=BF16, name=f"l{l}_dwo", tn=D)
        dqm, dkm, dvm = _mem_bwd(sv["qm"], sv["km"], sv["vm"], dom, name=f"l{l}_memattn_bwd")
        dxres, dyb = dx_through_ln(dqm, Wl["wsq"], 1, True, dxres, (l, 1), f"l{l}_dx2")
        gW[("mem_wq", l)] = _mm(sv["x2b"], dqm, ta=True, out_dtype=BF16, name=f"l{l}_dwq", tn=D)
        dkvm = jnp.concatenate([dkm, dvm], axis=1).astype(BF16)
        gW[("mem_wkv", l)] = _mm(dkvm, memb, ta=True, out_dtype=BF16, name=f"l{l}_dwkv", tn=D)
        dcat = _mm(dyb, Wl["wsq"], lead=0, tb=True, name=f"l{l}_dcat", tn=D)
        gW[("w_out", l)] = _mm(sv["cat"], dyb, ta=True, out_dtype=BF16, name=f"l{l}_dwout", tn=D)
        riding = [rs_first_level(l, "a")] + ([above] if above else [])
        both = _Both([st["ex"] for st in riding])
        dqh, dkh, dvh, rode = _flash_bwd(sv["qh"], sv["kh"], sv["vh"], sv["cat"], dcat, sv["lse"], H=H, pw=PW,
                                         name=f"l{l}_flash_bwd", ride=both)
        for st, got in zip(riding, both.split(rode)):
            rs_last_level(st, got)
        dqraw, dkv, dkpe = _heads_bwd(dqh, dkh, dvh, tabs, H=H, name=f"l{l}_heads_bwd")
        dcq = _mm(dqraw, Wl["wuqT"], lead=0, name=f"l{l}_dcq")
        gW[("w_uq", l)] = _mm(dqraw, sv["cqn"], ta=True, out_dtype=BF16, name=f"l{l}_dwuq")
        dckv = _mm(dkv, Wl["wukvT"], lead=0, name=f"l{l}_dckv")
        gW[("w_ukv", l)] = _mm(dkv, sv["ckvn"], ta=True, out_dtype=BF16, name=f"l{l}_dwukv")
        du, dwbd, dps = _pool_bwd(sv["hin"], dcat, wbd[l], sv["pscale"], name=f"l{l}_pool_bwd")
        dhin, dgq, dgkv = _norms_bwd(sv["hin"], sv["gq"], sv["gkv"], dcq, dckv, dkpe, du, pw=PW,
                                     name=f"l{l}_norms_bwd")
        pg = PW // len(POOL_WINDOWS)
        gS[("pool_w", l)] = jnp.stack([dwbd[g * pg:(g + 1) * pg, g * pg:(g + 1) * pg]
                                       for g in range(len(POOL_WINDOWS))])
        gS[("pool_scale", l)], gS[("q_norm_g", l)], gS[("kv_norm_g", l)] = dps, dgq, dgkv
        dxres, dyb = dx_through_ln(dhin, Wl["winp"], 0, True, dxres, (l, 0), f"l{l}_dx1")
        gW[("w_in", l)] = _mm(sv["x1b"], dhin, ta=True, out_dtype=BF16, name=f"l{l}_dwin", tn=DINP)
        heads = rs_first_level(l, "b")
        riding = [heads["ex"]]
        if l == 0:
            small_keys = []
            for ll in range(L):
                small_keys += [("pool_w", ll), ("pool_scale", ll), ("q_norm_g", ll), ("kv_norm_g", ll)]
                small_keys += [("ln_g", ll, k) for k in range(4)] + [("ln_b", ll, k) for k in range(4)]
            flat = jnp.concatenate([loss_blk[0, :1]] + [gS[k].reshape(-1) for k in small_keys])
            n_small = flat.shape[0]
            rows = -(-n_small // (8 * LANE)) * 8
            flat = jnp.pad(flat, (0, rows * LANE - n_small)).reshape(1, rows, LANE)
            riding.append(_Gather([flat]))
        both = _Both(riding)
        below, rode = ffn_bwd(l, 0, sv["ffn1"], dxres, dyb, both, (l - 1, 3) if l > 0 else None)
        rode = both.split(rode)
        rs_last_level(heads, rode[0])
        if l > 0:
            dxres, dyb = below
            above = rs_first_level(l, "c")
    grad_x = below.reshape(1, T, D)

    red = _sum_devices(rode[1][0].reshape(N_DEV, rows, LANE), name="sum_small").reshape(-1)
    loss = red[0]
    gsm, pos = {}, 1
    for k in small_keys:
        size = math.prod(gS[k].shape)
        gsm[k] = red[pos:pos + size].reshape(gS[k].shape)
        pos += size

    me = 4 * lax.axis_index("x") + 2 * lax.axis_index("y") + lax.axis_index("c")
    dsh = D // N_DEV
    stack = lambda f: jnp.stack([f(l) for l in range(L)])
    g_ln_g = stack(lambda l: jnp.concatenate([gsm[("ln_g", l, k)] for k in range(4)], axis=0))
    g_ln_b = stack(lambda l: jnp.concatenate([gsm[("ln_b", l, k)] for k in range(4)], axis=0))
    swapped = {
        "ffn1_w13": stack(lambda l: gsh[("w13", 0, l)]),
        "ffn2_w13": stack(lambda l: gsh[("w13", 1, l)]),
        "w_in": stack(lambda l: gsh[("w_in", l)][:, :DIN].T),
        "w_uq": stack(lambda l: gsh[("w_uq", l)]),
        "w_ukv": stack(lambda l: gsh[("w_ukv", l)]),
    }
    swap = lambda a: jnp.swapaxes(a, 1, 2)
    grads = {
        "ln_g": lax.dynamic_slice_in_dim(g_ln_g, me * dsh, dsh, axis=2),
        "ln_b": lax.dynamic_slice_in_dim(g_ln_b, me * dsh, dsh, axis=2),
        "ffn1_w2": stack(lambda l: gsh[("w2", 0, l)]),
        "pool_w": stack(lambda l: gsm[("pool_w", l)]),
        "pool_scale": stack(lambda l: gsm[("pool_scale", l)].reshape(PW)),
        "q_norm_g": stack(lambda l: gsm[("q_norm_g", l)].reshape(Q_LORA)),
        "kv_norm_g": stack(lambda l: gsm[("kv_norm_g", l)].reshape(KV_LORA)),
        "w_out": stack(lambda l: gsh[("w_out", l)]),
        "mem_wq": stack(lambda l: gsh[("mem_wq", l)]),
        "mem_wkv": stack(lambda l: gsh[("mem_wkv", l)].T),
        "mem_wo": stack(lambda l: gsh[("mem_wo", l)]),
        "ffn2_w2": stack(lambda l: gsh[("w2", 1, l)]),
        **{nme: swap(g) for nme, g in swapped.items()},
    }

    names = ["ln_g", "ln_b", "ffn1_w13", "ffn1_w2", "w_in", "pool_w", "pool_scale", "q_norm_g", "w_uq",
             "kv_norm_g", "w_ukv", "w_out", "mem_wq", "mem_wkv", "mem_wo", "ffn2_w13", "ffn2_w2"]
    weights = dict(ln_g=ln_g, ln_b=ln_b, ffn1_w13=ffn1_w13, ffn1_w2=ffn1_w2, w_in=w_in, pool_w=pool_w,
                   pool_scale=pool_scale, q_norm_g=q_norm_g, w_uq=w_uq, kv_norm_g=kv_norm_g, w_ukv=w_ukv,
                   w_out=w_out, mem_wq=mem_wq, mem_wkv=mem_wkv, mem_wo=mem_wo, ffn2_w13=ffn2_w13,
                   ffn2_w2=ffn2_w2)
    ms = dict(ln_g=m_ln_g, ln_b=m_ln_b, ffn1_w13=m_ffn1_w13, ffn1_w2=m_ffn1_w2, w_in=m_w_in, pool_w=m_pool_w,
              pool_scale=m_pool_scale, q_norm_g=m_q_norm_g, w_uq=m_w_uq, kv_norm_g=m_kv_norm_g,
              w_ukv=m_w_ukv, w_out=m_w_out, mem_wq=m_mem_wq, mem_wkv=m_mem_wkv, mem_wo=m_mem_wo,
              ffn2_w13=m_ffn2_w13, ffn2_w2=m_ffn2_w2)
    vs = dict(ln_g=v_ln_g, ln_b=v_ln_b, ffn1_w13=v_ffn1_w13, ffn1_w2=v_ffn1_w2, w_in=v_w_in, pool_w=v_pool_w,
              pool_scale=v_pool_scale, q_norm_g=v_q_norm_g, w_uq=v_w_uq, kv_norm_g=v_kv_norm_g,
              w_ukv=v_w_ukv, w_out=v_w_out, mem_wq=v_mem_wq, mem_wkv=v_mem_wkv, mem_wo=v_mem_wo,
              ffn2_w13=v_ffn2_w13, ffn2_w2=v_ffn2_w2)
    deltas, new_m, new_v = [], [], []
    for nme in names:
        if nme in swapped:
            d, mn, vn = [swap(o) for o in _adamw(swap(weights[nme]), swapped[nme], swap(ms[nme]), swap(vs[nme]),
                                                 name=f"adamw_{nme}")]
        else:
            d, mn, vn = _adamw(weights[nme], grads[nme], ms[nme], vs[nme], name=f"adamw_{nme}")
        deltas.append(d)
        new_m.append(mn)
        new_v.append(vn)
    return (loss, grad_x, *[grads[nme] for nme in names], *deltas, *new_m, *new_v)
```

```python
import functools
import math

import jax
import jax.numpy as jnp
from jax import lax
from jax.experimental import pallas as pl
from jax.experimental.pallas import tpu as pltpu

F32 = jnp.float32
BF16 = jnp.bfloat16
MESH = pl.DeviceIdType.MESH

CHUNK = 64
MEM_HEADS = 4
POOL_WINDOWS = (2, 4, 8, 16)
QK_NOPE = 128
QK_ROPE = 64
V_HEAD = 128
Q_LORA = 256
KV_LORA = 128
ROPE_BASE = 10000.0
LN_EPS = 1e-5
RMS_EPS = 1e-6
NEG_INF = -1e30
ADAM_LR = 0.001
ADAM_B1 = 0.9
ADAM_B2 = 0.999
ADAM_EPS = 1e-08
ADAM_WD = 0.01
ADAM_STEP = 10

N_DEV = 8
LANE = 128
HEAD_PAD = 2 * LANE
POOL_HALO = 16
VMEM_CAP = 56 * 1024 * 1024
VMEM_FLOOR = 32 * 1024 * 1024


def _tile(n, pref, mult):
    t = (min(pref, n) // mult) * mult
    while t >= mult:
        if n % t == 0:
            return t
        t -= mult
    return n


def _params(sem, est_bytes):
    limit = int(min(max(2 * est_bytes + (8 << 20), VMEM_FLOOR), VMEM_CAP))
    return pltpu.CompilerParams(dimension_semantics=sem, vmem_limit_bytes=limit)


def _nbytes(shape, dtype):
    return math.prod(shape) * jnp.dtype(dtype).itemsize


def _hbm(x):
    return pltpu.with_memory_space_constraint(x, pltpu.HBM)


def _out(shape, dtype):
    return pltpu.HBM(tuple(shape), dtype)


def _dg(a, b, ca, cb):
    return lax.dot_general(a.astype(BF16), b.astype(BF16), (((ca,), (cb,)), ((), ())),
                           preferred_element_type=F32)


@jax.custom_vjp
def _bdot_nn(a, b):
    return _dg(a, b, 1, 0)


def _bdot_nn_fwd(a, b):
    return _dg(a, b, 1, 0), (a, b)


def _bdot_nn_bwd(res, ct):
    a, b = res
    return _dg(ct, b, 1, 1).astype(a.dtype), _dg(a, ct, 0, 0).astype(b.dtype)


_bdot_nn.defvjp(_bdot_nn_fwd, _bdot_nn_bwd)


@jax.custom_vjp
def _bdot_nt(a, b):
    return _dg(a, b, 1, 1)


def _bdot_nt_fwd(a, b):
    return _dg(a, b, 1, 1), (a, b)


def _bdot_nt_bwd(res, ct):
    a, b = res
    return _dg(ct, b, 1, 0).astype(a.dtype), _dg(ct, a, 0, 0).astype(b.dtype)


_bdot_nt.defvjp(_bdot_nt_fwd, _bdot_nt_bwd)


@functools.partial(jax.custom_vjp, nondiff_argnums=(1,))
def _lane_roll(x, shift):
    return pltpu.roll(x, shift % x.shape[1], axis=1)


def _lane_roll_fwd(x, shift):
    return _lane_roll(x, shift), None


def _lane_roll_bwd(shift, _, ct):
    return (_lane_roll(ct, -shift),)


_lane_roll.defvjp(_lane_roll_fwd, _lane_roll_bwd)


@functools.partial(jax.custom_vjp, nondiff_argnums=(1, 2))
def _cols(x, lo, hi):
    return x[:, lo:hi]


def _cols_fwd(x, lo, hi):
    return x[:, lo:hi], x.shape[1]


def _cols_bwd(lo, hi, width, ct):
    parts = []
    if lo > 0:
        parts.append(jnp.zeros((ct.shape[0], lo), ct.dtype))
    parts.append(ct)
    if hi < width:
        parts.append(jnp.zeros((ct.shape[0], width - hi), ct.dtype))
    return (jnp.concatenate(parts, axis=1) if len(parts) > 1 else ct,)


_cols.defvjp(_cols_fwd, _cols_bwd)


MM_VMEM_BUDGET = 22 * 1024 * 1024


def _mm(a, b, *, name, ta=False, tb=False, out_dtype=F32, lead=None, add=None, add_scale=1.0,
        tm=1024, tn=1024, tk=8192, ride=None):
    if ta:
        K, M = a.shape
    else:
        M, K = a.shape
    bshape = b.shape[1:] if lead is not None else b.shape
    if tb:
        N, Kb = bshape
    else:
        Kb, N = bshape
    assert K == Kb, (name, a.shape, b.shape)

    def blocks(tm, tn, tk):
        tm = _tile(M, tm, LANE if ta else 16)
        tn = _tile(N, tn, LANE)
        tk = _tile(K, tk, LANE)
        nbytes = (tm * tk * a.dtype.itemsize + tk * tn * b.dtype.itemsize
                  + tm * tn * (jnp.dtype(out_dtype).itemsize + (4 if K // tk > 1 else 0)
                               + (add.dtype.itemsize if add is not None else 0)))
        return tm, tn, tk, nbytes

    if ta:
        tm = min(tm, max(LANE, M // 4))
    tm, tn, tk, est = blocks(tm, tn, tk)
    for shrink in ("m", "k", "m", "k", "n"):
        if est <= MM_VMEM_BUDGET:
            break
        if shrink == "m":
            tm, tn, tk, est = blocks(max(tm // 2, LANE), tn, tk)
        elif shrink == "k":
            tm, tn, tk, est = blocks(tm, tn, max(tk // 2, LANE))
        else:
            tm, tn, tk, est = blocks(tm, max(tn // 2, LANE), tk)
    nk = K // tk
    ca = 0 if ta else 1
    cb = 1 if tb else 0

    def body(*refs):
        a_ref, b_ref = refs[0], refs[1]
        add_ref = refs[2] if add is not None else None
        o_ref = refs[3] if add is not None else refs[2]

        def finish(r):
            if add_ref is not None:
                r = r + add_scale * add_ref[...].astype(F32)
            o_ref[...] = r.astype(o_ref.dtype)

        if nk == 1:
            finish(_dg(a_ref[...], b_ref[...], ca, cb))
            return
        acc_ref = refs[-1]
        k = pl.program_id(2)

        @pl.when(k == 0)
        def _():
            acc_ref[...] = jnp.zeros_like(acc_ref)

        acc_ref[...] += _dg(a_ref[...], b_ref[...], ca, cb)

        @pl.when(k == nk - 1)
        def _():
            finish(acc_ref[...])

    a_blk = (tk, tm) if ta else (tm, tk)
    a_map = (lambda i, j, k: (k, i)) if ta else (lambda i, j, k: (i, k))
    b_blk = (tn, tk) if tb else (tk, tn)
    if lead is None:
        b_map = (lambda i, j, k: (j, k)) if tb else (lambda i, j, k: (k, j))
        b_spec = pl.BlockSpec(b_blk, b_map)
    else:
        b_map = (lambda i, j, k: (lead, j, k)) if tb else (lambda i, j, k: (lead, k, j))
        b_spec = pl.BlockSpec((None,) + b_blk, b_map)
    in_specs = [pl.BlockSpec(a_blk, a_map), b_spec]
    args = [a, b]
    if add is not None:
        in_specs.append(pl.BlockSpec((tm, tn), lambda i, j, k: (i, j)))
        args.append(add)
    (out,), rode = _host_call(
        body, name=name,
        grid=(M // tm, N // tn, nk),
        in_specs=in_specs,
        out_specs=[pl.BlockSpec((tm, tn), lambda i, j, k: (i, j))],
        out_shape=[_out((M, N), out_dtype)],
        scratch=[pltpu.VMEM((tm, tn), F32)] if nk > 1 else [],
        args=[_hbm(v) for v in args], sem=("parallel", "parallel", "arbitrary"), est=est + tm * tn * 4,
        ride=ride)
    return out if ride is None else (out, rode)


def _mm_pair(dy, w, lead, x, *, dx_dtype, name, wt=False):
    M, N = dy.shape
    Kx = x.shape[1]
    wshape = (N, Kx) if wt else (Kx, N)
    assert w.shape[1:] == wshape, (name, w.shape, x.shape, dy.shape)
    tm = _tile(M, 512, 16)
    nt = M // tm

    def body(dy_ref, w_ref, x_ref, dx_ref, dw_ref, acc_ref):
        i = pl.program_id(0)

        @pl.when(i == 0)
        def _():
            acc_ref[...] = jnp.zeros_like(acc_ref)

        dyv = dy_ref[...]
        dx_ref[...] = _dg(dyv, w_ref[...], 1, 0 if wt else 1).astype(dx_ref.dtype)
        acc_ref[...] += _dg(dyv, x_ref[...], 0, 0) if wt else _dg(x_ref[...], dyv, 0, 0)

        @pl.when(i == nt - 1)
        def _():
            dw_ref[...] = acc_ref[...].astype(dw_ref.dtype)

    est = tm * (N + 2 * Kx) * 4 + Kx * N * (2 + 4 + 2)
    return pl.pallas_call(
        body, name=name, grid=(nt,),
        in_specs=[pl.BlockSpec((tm, N), lambda i: (i, 0)),
                  pl.BlockSpec((None,) + wshape, lambda i: (lead, 0, 0)),
                  pl.BlockSpec((tm, Kx), lambda i: (i, 0))],
        out_specs=[pl.BlockSpec((tm, Kx), lambda i: (i, 0)), pl.BlockSpec(wshape, lambda i: (0, 0))],
        out_shape=[_out((M, Kx), dx_dtype), _out(wshape, BF16)],
        scratch_shapes=[pltpu.VMEM(wshape, F32)],
        compiler_params=_params(("arbitrary",), est),
    )(_hbm(dy), _hbm(w), _hbm(x))


def _rowwise(fn, tiles, params, tile_outs, acc_outs=(), *, tm, name):
    tile_arrays, tile_specs = [], []
    for t in tiles:
        if isinstance(t, tuple):
            tile_arrays.append(t[0])
            tile_specs.append(t[1])
        else:
            tile_arrays.append(t)
            tile_specs.append(pl.BlockSpec((tm, t.shape[1]), lambda i: (i, 0)))
    T = tile_arrays[0].shape[0]
    nt, np_, nto, nao = len(tile_arrays), len(params), len(tile_outs), len(acc_outs)

    def body(*refs):
        i = pl.program_id(0)
        tvals = [r[...] for r in refs[:nt]]
        pvals = [r[...] for r in refs[nt:nt + np_]]
        to_refs = refs[nt + np_:nt + np_ + nto]
        ao_refs = refs[nt + np_ + nto:]
        touts, aouts = fn(i, tvals, pvals)
        for r, v in zip(to_refs, touts):
            r[...] = v.astype(r.dtype)
        if nao:
            @pl.when(i == 0)
            def _():
                for r in ao_refs:
                    r[...] = jnp.zeros_like(r)
            for r, v in zip(ao_refs, aouts):
                r[...] += v.astype(r.dtype)

    in_specs = tile_specs + [pl.BlockSpec(p.shape, lambda i: (0, 0)) for p in params]
    out_specs = [pl.BlockSpec((tm, c), lambda i: (i, 0)) for c, _ in tile_outs]
    out_specs += [pl.BlockSpec(s, lambda i: (0, 0)) for s, _ in acc_outs]
    out_shape = [_out((T, c), d) for c, d in tile_outs]
    out_shape += [_out(s, d) for s, d in acc_outs]
    width = sum(s.block_shape[-1] for s in tile_specs) + sum(c for c, _ in tile_outs)
    est = 6 * tm * width * 4 + sum(_nbytes(p.shape, F32) for p in params) * 4
    return pl.pallas_call(
        body, name=name, grid=(T // tm,),
        in_specs=in_specs, out_specs=out_specs, out_shape=out_shape,
        compiler_params=_params(("arbitrary",) if nao else ("parallel",), est),
    )(*[_hbm(v) for v in tile_arrays], *[_hbm(p) for p in params])


def _ln_fn(alpha, s, xres, y, g, b):
    z = alpha * xres.astype(F32) + s * y.astype(F32)
    mu = jnp.mean(z, axis=-1, keepdims=True)
    zc = z - mu
    var = jnp.mean(zc * zc, axis=-1, keepdims=True)
    return zc * lax.rsqrt(var + LN_EPS) * g + b


def _mm_ln(a, b, lead, xres, g, bias, *, alpha, s, name):
    M, K = a.shape
    N = b.shape[2]
    tm = _tile(M, 512, 16)

    def body(a_ref, b_ref, x_ref, g_ref, bias_ref, y_ref, xo_ref, xb_ref):
        y = _dg(a_ref[...], b_ref[...], 1, 0)
        y_ref[...] = y.astype(y_ref.dtype)
        out = _ln_fn(alpha, s, x_ref[...], y, g_ref[...], bias_ref[...])
        xo_ref[...] = out
        xb_ref[...] = out.astype(BF16)

    row = pl.BlockSpec((tm, N), lambda i: (i, 0))
    vec = pl.BlockSpec((1, N), lambda i: (0, 0))
    est = tm * K * 2 + K * N * 2 + tm * N * (4 + 4 + 4 + 2 + 8)
    return pl.pallas_call(
        body, name=name, grid=(M // tm,),
        in_specs=[pl.BlockSpec((tm, K), lambda i: (i, 0)), pl.BlockSpec((None, K, N), lambda i: (lead, 0, 0)),
                  row, vec, vec],
        out_specs=[row, row, row],
        out_shape=[_out((M, N), BF16), _out((M, N), F32), _out((M, N), BF16)],
        compiler_params=_params(("parallel",), est),
    )(_hbm(a), _hbm(b), _hbm(xres), _hbm(g), _hbm(bias))


def _ln_bwd_math(alpha, s, x, y, g, d):
    z = alpha * x + s * y.astype(F32)
    zc = z - jnp.mean(z, axis=-1, keepdims=True)
    r = lax.rsqrt(jnp.mean(zc * zc, axis=-1, keepdims=True) + LN_EPS)
    xh = zc * r
    dxh = d * g
    dz = r * (dxh - jnp.mean(dxh, axis=-1, keepdims=True) - xh * jnp.mean(dxh * xh, axis=-1, keepdims=True))
    return alpha * dz, s * dz, jnp.sum(d * xh, axis=0, keepdims=True), jnp.sum(d, axis=0, keepdims=True)


def _mm_ln_bwd(a, b, lead, tb, add, xres, y, g, *, alpha, s, name):
    M, K = a.shape
    N = b.shape[1] if tb else b.shape[2]
    tk = K if K * N * 2 <= MM_VMEM_BUDGET * 3 // 5 else _tile(K, 2816, LANE)
    tm = _tile(M, 512 if K * N * 2 <= MM_VMEM_BUDGET // 4 else 256, 16)
    nk = K // tk
    cb = 1 if tb else 0

    def body(a_ref, b_ref, add_ref, x_ref, y_ref, g_ref, dx_ref, dy_ref, dg_ref, db_ref, *scratch):
        i, k = pl.program_id(0), pl.program_id(1)

        def finish(d):
            @pl.when(i == 0)
            def _():
                dg_ref[...] = jnp.zeros_like(dg_ref)
                db_ref[...] = jnp.zeros_like(db_ref)

            dx, dy, dg, db = _ln_bwd_math(alpha, s, x_ref[...], y_ref[...], g_ref[...], d + add_ref[...])
            dx_ref[...] = dx
            dy_ref[...] = dy.astype(dy_ref.dtype)
            dg_ref[...] += dg
            db_ref[...] += db

        if nk == 1:
            finish(_dg(a_ref[...], b_ref[...], 1, cb))
            return
        acc_ref = scratch[0]

        @pl.when(k == 0)
        def _():
            acc_ref[...] = jnp.zeros_like(acc_ref)

        acc_ref[...] += _dg(a_ref[...], b_ref[...], 1, cb)

        @pl.when(k == nk - 1)
        def _():
            finish(acc_ref[...])

    row = pl.BlockSpec((tm, N), lambda i, k: (i, 0))
    vec = pl.BlockSpec((1, N), lambda i, k: (0, 0))
    b_spec = (pl.BlockSpec((None, N, tk), lambda i, k: (lead, 0, k)) if tb
              else pl.BlockSpec((None, tk, N), lambda i, k: (lead, k, 0)))
    est = tm * tk * 2 + tk * N * 2 + tm * N * (4 + 4 + 2 + 4 + 2 + 4 + 12)
    return pl.pallas_call(
        body, name=name, grid=(M // tm, nk),
        in_specs=[pl.BlockSpec((tm, tk), lambda i, k: (i, k)), b_spec, row, row, row, vec],
        out_specs=[row, row, vec, vec],
        out_shape=[_out((M, N), F32), _out((M, N), BF16), _out((1, N), F32), _out((1, N), F32)],
        scratch_shapes=[pltpu.VMEM((tm, N), F32)] if nk > 1 else [],
        compiler_params=_params(("arbitrary", "arbitrary"), est),
    )(_hbm(a), _hbm(b), _hbm(add), _hbm(xres), _hbm(y), _hbm(g))


def _loss_ln_bwd(xres, y, g, out, target, *, alpha, s, name):
    T, D = xres.shape
    tm = _tile(T, 256, 16)

    def body(x_ref, y_ref, o_ref, t_ref, g_ref, dx_ref, dy_ref, dg_ref, db_ref, loss_ref):
        @pl.when(pl.program_id(0) == 0)
        def _():
            dg_ref[...] = jnp.zeros_like(dg_ref)
            db_ref[...] = jnp.zeros_like(db_ref)
            loss_ref[...] = jnp.zeros_like(loss_ref)

        err = o_ref[...] - t_ref[...]
        part = 0.5 * jnp.sum(jnp.sum(err * err, axis=1, keepdims=True) / D, axis=0, keepdims=True)
        loss_ref[...] += jnp.broadcast_to(part, loss_ref.shape)
        dx, dy, dg, db = _ln_bwd_math(alpha, s, x_ref[...], y_ref[...], g_ref[...], err / D)
        dx_ref[...] = dx
        dy_ref[...] = dy.astype(dy_ref.dtype)
        dg_ref[...] += dg
        db_ref[...] += db

    row = pl.BlockSpec((tm, D), lambda i: (i, 0))
    vec = pl.BlockSpec((1, D), lambda i: (0, 0))
    return pl.pallas_call(
        body, name=name, grid=(T // tm,),
        in_specs=[row, row, row, row, vec],
        out_specs=[row, row, vec, vec, pl.BlockSpec((8, LANE), lambda i: (0, 0))],
        out_shape=[_out((T, D), F32), _out((T, D), BF16), _out((1, D), F32), _out((1, D), F32),
                   _out((8, LANE), F32)],
        compiler_params=_params(("arbitrary",), 14 * tm * D * 4),
    )(_hbm(xres), _hbm(y), _hbm(out), _hbm(target), _hbm(g))


FFN_TILE = 256


def _interleave(w, axis):
    n = w.shape[axis] // (2 * FFN_TILE)
    shp = w.shape[:axis] + (2, n, FFN_TILE) + w.shape[axis + 1:]
    return jnp.swapaxes(w.reshape(shp), axis, axis + 1).reshape(w.shape)


def _deinterleave(w, axis):
    n = w.shape[axis] // (2 * FFN_TILE)
    shp = w.shape[:axis] + (n, 2, FFN_TILE) + w.shape[axis + 1:]
    return jnp.swapaxes(w.reshape(shp), axis, axis + 1).reshape(w.shape)


def _ffn_up(xb, w13t, lead, *, name, ride=None):
    T, D = xb.shape
    F = w13t.shape[1] // 2
    tc = FFN_TILE
    tm = _tile(T, 2048, 16)

    def body(x_ref, w_ref, h_ref, a_ref):
        h = _dg(x_ref[...], w_ref[...], 1, 1)
        g, u = h[:, :tc], h[:, tc:]
        h_ref[...] = h.astype(h_ref.dtype)
        a_ref[...] = (g * jax.nn.sigmoid(g) * u).astype(a_ref.dtype)

    est = (tm * D + 2 * tc * D + 3 * tm * tc) * 2 + 3 * tm * tc * 4
    (h13, a), gathered = _host_call(
        body, name=name, grid=(T // tm, F // tc),
        in_specs=[pl.BlockSpec((tm, D), lambda i, j: (i, 0)),
                  pl.BlockSpec((None, 2 * tc, D), lambda i, j: (lead, j, 0))],
        out_specs=[pl.BlockSpec((tm, 2 * tc), lambda i, j: (i, j)),
                   pl.BlockSpec((tm, tc), lambda i, j: (i, j))],
        out_shape=[_out((T, 2 * F), BF16), _out((T, F), BF16)],
        args=[_hbm(xb), _hbm(w13t)], sem=("parallel", "parallel"), est=est, ride=ride)
    return h13, a, gathered


def _ffn_down_bwd(dyb, w2, lead, h13, *, name, ride=None):
    T, D = dyb.shape
    F = w2.shape[1]
    tc = FFN_TILE
    tm = _tile(T, 2048, 16)

    def body(dy_ref, w_ref, h_ref, dh_ref):
        d = _dg(dy_ref[...], w_ref[...], 1, 1)
        h = h_ref[...].astype(F32)
        g, u = h[:, :tc], h[:, tc:]
        sig = jax.nn.sigmoid(g)
        gs = g * sig
        dh_ref[...] = jnp.concatenate([d * u * (sig + gs * (1.0 - sig)), d * gs], axis=1).astype(dh_ref.dtype)

    est = (tm * D + tc * D + 4 * tm * tc) * 2 + 6 * tm * tc * 4
    (dh,), rode = _host_call(
        body, name=name, grid=(T // tm, F // tc),
        in_specs=[pl.BlockSpec((tm, D), lambda i, j: (i, 0)),
                  pl.BlockSpec((None, tc, D), lambda i, j: (lead, j, 0)),
                  pl.BlockSpec((tm, 2 * tc), lambda i, j: (i, j))],
        out_specs=[pl.BlockSpec((tm, 2 * tc), lambda i, j: (i, j))],
        out_shape=[_out((T, 2 * F), BF16)],
        args=[_hbm(dyb), _hbm(w2), _hbm(h13)], sem=("parallel", "parallel"), est=est, ride=ride)
    return dh, rode


def _pool_select(parts, pw):
    pg = pw // len(POOL_WINDOWS)
    grp = lax.broadcasted_iota(jnp.int32, parts[0].shape, 1) // pg
    out = parts[3]
    for g in (2, 1, 0):
        out = jnp.where(grp == g, parts[g], out)
    return out


def _pool_count(t0, rows, pw):
    pg = pw // len(POOL_WINDOWS)
    grp = lax.broadcasted_iota(jnp.int32, (rows, pw), 1) // pg
    win = jnp.where(grp == 0, POOL_WINDOWS[0],
                    jnp.where(grp == 1, POOL_WINDOWS[1],
                              jnp.where(grp == 2, POOL_WINDOWS[2], POOL_WINDOWS[3])))
    t = t0 + lax.broadcasted_iota(jnp.int32, (rows, pw), 0)
    return jnp.minimum(t + 1, win).astype(F32)


def _window_sums(ext, up):
    n = ext.shape[0]
    sums, cur, k = [], ext, 1
    for _ in POOL_WINDOWS:
        cur = cur + pltpu.roll(cur, (n - k) if up else k, axis=0)
        sums.append(cur)
        k *= 2
    return sums


def _pool_delta(u, halo, t0):
    tm, pw = u.shape
    ext = jnp.concatenate([halo, u], axis=0)
    sums = [s[POOL_HALO:, :] for s in _window_sums(ext, up=False)]
    return _pool_select(sums, pw) / _pool_count(t0, tm, pw) - u


def _pool_fwd(hin, wbd, scale, cat, *, name):
    T = hin.shape[0]
    pw = wbd.shape[0]
    tm = _tile(T, 256, POOL_HALO)
    per = tm // POOL_HALO

    def body(u_ref, halo_ref, w_ref, s_ref, cat_ref, y_ref):
        i = pl.program_id(0)
        halo = jnp.where(i > 0, halo_ref[...], 0.0)
        d = _pool_delta(u_ref[...], halo, i * tm)
        y_ref[...] = (_dg(d, w_ref[...], 1, 0) * s_ref[...]).astype(y_ref.dtype)

    return pl.pallas_call(
        body, name=name, grid=(T // tm,),
        in_specs=[pl.BlockSpec((tm, pw), lambda i: (i, 0)),
                  pl.BlockSpec((POOL_HALO, pw), lambda i: (jnp.maximum(i * per - 1, 0), 0)),
                  pl.BlockSpec((pw, pw), lambda i: (0, 0)),
                  pl.BlockSpec((1, pw), lambda i: (0, 0)),
                  ANY],
        out_specs=pl.BlockSpec((tm, pw), lambda i: (i, 0)),
        out_shape=_out(cat.shape, cat.dtype),
        input_output_aliases={4: 0},
        compiler_params=_params(("parallel",), 16 * tm * pw * 4),
    )(_hbm(hin), _hbm(hin), _hbm(wbd), _hbm(scale), _hbm(cat))


def _pool_bwd(hin, dcat, wbd, scale, *, name):
    T = hin.shape[0]
    pw = wbd.shape[0]
    tm = _tile(T, 256, POOL_HALO)
    per = tm // POOL_HALO
    nt = T // tm

    def body(u_ref, halo_ref, dy_ref, dyn_ref, w_ref, s_ref, du_ref, dw_ref, ds_ref):
        i = pl.program_id(0)

        @pl.when(i == 0)
        def _():
            dw_ref[...] = jnp.zeros_like(dw_ref)
            ds_ref[...] = jnp.zeros_like(ds_ref)

        halo = jnp.where(i > 0, halo_ref[...], 0.0)
        d = _pool_delta(u_ref[...], halo, i * tm)
        w = w_ref[...]
        sc = s_ref[...]
        dy = dy_ref[...]
        dyn = jnp.where(i < nt - 1, dyn_ref[...], 0.0)
        ds_ref[...] += jnp.sum(dy * _dg(d, w, 1, 0), axis=0, keepdims=True)
        dys = dy * sc
        dw_ref[...] += _dg(d, dys, 0, 0)
        dys_ext = jnp.concatenate([dys, dyn * sc], axis=0)
        dd_ext = _dg(dys_ext, w, 1, 1)
        ddp = dd_ext / _pool_count(i * tm, tm + POOL_HALO, pw)
        sums = [s[:tm, :] for s in _window_sums(ddp, up=True)]
        du_ref[...] = _pool_select(sums, pw) - dd_ext[:tm, :]

    return pl.pallas_call(
        body, name=name, grid=(nt,),
        in_specs=[pl.BlockSpec((tm, pw), lambda i: (i, 0)),
                  pl.BlockSpec((POOL_HALO, pw), lambda i: (jnp.maximum(i * per - 1, 0), 0)),
                  pl.BlockSpec((tm, pw), lambda i: (i, 0)),
                  pl.BlockSpec((POOL_HALO, pw), lambda i: (jnp.minimum((i + 1) * per, nt * per - 1), 0)),
                  pl.BlockSpec((pw, pw), lambda i: (0, 0)),
                  pl.BlockSpec((1, pw), lambda i: (0, 0))],
        out_specs=[pl.BlockSpec((tm, pw), lambda i: (i, 0)),
                   pl.BlockSpec((pw, pw), lambda i: (0, 0)),
                   pl.BlockSpec((1, pw), lambda i: (0, 0))],
        out_shape=[_out((T, pw), F32),
                   _out((pw, pw), F32),
                   _out((1, pw), F32)],
        compiler_params=_params(("arbitrary",), 24 * tm * pw * 4),
    )(_hbm(hin), _hbm(hin), _hbm(dcat), _hbm(dcat), _hbm(wbd), _hbm(scale))


def _rms(x, g):
    return x * lax.rsqrt(jnp.mean(x * x, axis=-1, keepdims=True) + RMS_EPS) * g


def _norms_fn(pw, h, gq, gkv):
    o1 = pw + Q_LORA
    o2 = o1 + KV_LORA
    return (_rms(_cols(h, pw, o1), gq), _rms(_cols(h, o1, o2), gkv), _cols(h, o2, h.shape[1]))


def _norms_bwd(hin, gq, gkv, dcq, dckv, dkpe, du, *, pw, name):
    tm = _tile(hin.shape[0], 256, 16)
    dinp = hin.shape[1]

    def fn(i, tv, pv):
        _, vjp = jax.vjp(functools.partial(_norms_fn, pw), tv[0], pv[0], pv[1])
        dh, dgq, dgkv = vjp((tv[1].astype(F32), tv[2].astype(F32), tv[3].astype(F32)))
        dh = jnp.concatenate([tv[4], dh[:, pw:]], axis=1)
        return (dh,), (dgq, dgkv)

    return _rowwise(fn, [hin, dcq, dckv, dkpe, du], [gq, gkv], [(dinp, BF16)],
                    [((1, Q_LORA), F32), ((1, KV_LORA), F32)], tm=tm, name=name)


def _heads_fn(H, qraw, kv, kpe, rc, rs1, rs2):
    half = QK_ROPE // 2
    scale = (QK_NOPE + QK_ROPE) ** -0.5

    def rope(blk):
        return blk * rc + _lane_roll(blk, -half) * rs1 + _lane_roll(blk, half) * rs2

    krot = rope(kpe)
    qs, ks, vs = [], [], []
    for h in range(H):
        lo = h * HEAD_PAD
        qs += [_cols(qraw, lo, lo + LANE) * scale, rope(_cols(qraw, lo + LANE, lo + HEAD_PAD)) * scale]
        ks += [_cols(kv, lo, lo + LANE), krot]
        vs += [_cols(kv, lo + LANE, lo + HEAD_PAD)]
    return jnp.concatenate(qs, axis=1), jnp.concatenate(ks, axis=1), jnp.concatenate(vs, axis=1)


def _heads_fwd(hin, gq, gkv, tabs, wuq, wukv, *, H, pw, name):
    tm = _tile(hin.shape[0], 256, 16)

    def fn(i, tv, pv):
        cqn, ckvn, kpe = _norms_fn(pw, tv[0], pv[0], pv[1])
        qraw = _dg(cqn, pv[2], 1, 1)
        kv = _dg(ckvn, pv[3], 1, 1)
        return (*_heads_fn(H, qraw, kv, kpe, *tv[1:]), cqn, ckvn), ()

    return _rowwise(fn, [hin, *tabs], [gq, gkv, wuq, wukv],
                    [(H * HEAD_PAD, BF16), (H * HEAD_PAD, BF16), (H * V_HEAD, BF16), (Q_LORA, BF16),
                     (KV_LORA, BF16)], tm=tm, name=name)


def _heads_bwd(dq, dk, dv, tabs, *, H, name):
    tm = _tile(dq.shape[0], 256, 16)

    def fn(i, tv, pv):
        z = jnp.zeros((tm, H * HEAD_PAD), F32)
        zk = jnp.zeros((tm, LANE), F32)
        rc, rs1, rs2 = tv[3], tv[4], tv[5]
        _, vjp = jax.vjp(lambda a, b, c: _heads_fn(H, a, b, c, rc, rs1, rs2), z, z, zk)
        return vjp((tv[0].astype(F32), tv[1].astype(F32), tv[2].astype(F32))), ()

    return _rowwise(fn, [dq, dk, dv, *tabs], [],
                    [(H * HEAD_PAD, BF16), (H * HEAD_PAD, BF16), (LANE, F32)], tm=tm, name=name)


def _diag_mask(rows, cols, row0):
    r = (row0 + lax.broadcasted_iota(jnp.int32, (rows, cols), 0)) // CHUNK
    c = lax.broadcasted_iota(jnp.int32, (rows, cols), 1) // CHUNK
    return r >= c


def _flash_fwd(qh, kh, vh, *, H, pw, name, ride=None):
    T = qh.shape[0]
    t = _tile(T, 512, CHUNK)
    off = pw // V_HEAD


    half = t

    def body(q_ref, k_ref, v_ref, o_ref, lse_ref):
        i = pl.program_id(1)
        q = q_ref[...]

        def update(carry, s, v):
            m, l, acc = carry
            mn = jnp.maximum(m, jnp.max(s, axis=1, keepdims=True))
            p = jnp.exp(s - mn)
            corr = jnp.exp(m - mn)
            return mn, corr * l + jnp.sum(p, axis=1, keepdims=True), corr * acc + _dg(p, v, 1, 0)

        def blk(j, carry):
            rows = pl.ds(pl.multiple_of(j * t, t), t)
            return update(carry, _dg(q, k_ref[rows, :], 1, 1), v_ref[rows, :])

        init = (jnp.full((t, 1), NEG_INF, F32), jnp.zeros((t, 1), F32), jnp.zeros((t, V_HEAD), F32))
        carry = lax.fori_loop(0, i, blk, init)
        done = []
        for r0 in range(0, t, half):
            keys = pl.ds(pl.multiple_of(i * t, t), r0 + half)
            s = _dg(q[r0:r0 + half, :], k_ref[keys, :], 1, 1)
            s = jnp.where(_diag_mask(half, r0 + half, r0), s, NEG_INF)
            done.append(update(tuple(c[r0:r0 + half] for c in carry), s, v_ref[keys, :]))
        m, l, acc = (jnp.concatenate(parts, axis=0) for parts in zip(*done))
        o_ref[...] = (acc / l).astype(o_ref.dtype)
        lse_ref[...] = jnp.broadcast_to(m + jnp.log(l), (t, V_HEAD))

    est = 2 * T * (HEAD_PAD + V_HEAD) * 2 + 8 * t * t * 4
    (o, lse), gathered = _host_call(
        body, name=name, grid=(H, T // t),
        in_specs=[pl.BlockSpec((t, HEAD_PAD), lambda h, i: (i, h)),
                  pl.BlockSpec((T, HEAD_PAD), lambda h, i: (0, h)),
                  pl.BlockSpec((T, V_HEAD), lambda h, i: (0, h))],
        out_specs=[pl.BlockSpec((t, V_HEAD), lambda h, i: (i, off + h)),
                   pl.BlockSpec((t, V_HEAD), lambda h, i: (i, h))],
        out_shape=[_out((T, pw + H * V_HEAD), BF16),
                   _out((T, H * V_HEAD), F32)],
        args=[_hbm(qh), _hbm(kh), _hbm(vh)], sem=("parallel", "parallel"), est=est, ride=ride)
    return o, lse, gathered


def _flash_bwd(qh, kh, vh, cat, dcat, lse, *, H, pw, name, ride=None):
    T = qh.shape[0]
    t = _tile(T, 512, CHUNK)
    nb = T // t
    off = pw // V_HEAD
    half = t

    def body(q_ref, k_ref, v_ref, o_ref, do_ref, lse_ref, dq_out_ref, dk_ref, dv_ref, dq_ref):
        j = pl.program_id(1)

        @pl.when(j == 0)
        def _():
            dq_ref[...] = jnp.zeros_like(dq_ref)

        kj = k_ref[...]
        vj = v_ref[...]

        def pair(rows, kx, vx, mask):
            qi = q_ref[rows, :]
            doi = do_ref[rows, :]
            oi = o_ref[rows, :].astype(F32)
            lsei = lse_ref[rows, :][:, :1]
            s = _dg(qi, kx, 1, 1)
            if mask is not None:
                s = jnp.where(mask, s, NEG_INF)
            p = jnp.exp(s - lsei)
            dp = _dg(doi, vx, 1, 1)
            di = jnp.sum(doi * oi, axis=1, keepdims=True)
            ds = p * (dp - di)
            dq_ref[rows, :] += _dg(ds, kx, 1, 0)
            return _dg(ds, qi, 0, 0), _dg(p, doi, 0, 0)

        def blk(i, carry):
            dk, dv = pair(pl.ds(pl.multiple_of(i * t, t), t), kj, vj, None)
            return carry[0] + dk, carry[1] + dv

        dk, dv = jnp.zeros((t, HEAD_PAD), F32), jnp.zeros((t, V_HEAD), F32)
        for r0 in range(0, t, half):
            n = r0 + half
            dkp, dvp = pair(pl.ds(pl.multiple_of(j * t + r0, half), half), kj[:n], vj[:n],
                            _diag_mask(half, n, r0))
            if n < t:
                dkp = jnp.concatenate([dkp, jnp.zeros((t - n, HEAD_PAD), F32)], axis=0)
                dvp = jnp.concatenate([dvp, jnp.zeros((t - n, V_HEAD), F32)], axis=0)
            dk, dv = dk + dkp, dv + dvp
        dk, dv = lax.fori_loop(j + 1, nb, blk, (dk, dv))
        dk_ref[...] = dk.astype(dk_ref.dtype)
        dv_ref[...] = dv.astype(dv_ref.dtype)

        @pl.when(j == nb - 1)
        def _():
            dq_out_ref[...] = dq_ref[...].astype(dq_out_ref.dtype)

    est = T * (HEAD_PAD * 2 + V_HEAD * 2 + V_HEAD * 4 + V_HEAD * 4 + HEAD_PAD * 4) + 10 * t * t * 4
    (dq, dk, dv), gathered = _host_call(
        body, name=name, grid=(H, nb),
        in_specs=[pl.BlockSpec((T, HEAD_PAD), lambda h, j: (0, h)),
                  pl.BlockSpec((t, HEAD_PAD), lambda h, j: (j, h)),
                  pl.BlockSpec((t, V_HEAD), lambda h, j: (j, h)),
                  pl.BlockSpec((T, V_HEAD), lambda h, j: (0, off + h)),
                  pl.BlockSpec((T, V_HEAD), lambda h, j: (0, off + h)),
                  pl.BlockSpec((T, V_HEAD), lambda h, j: (0, h))],
        out_specs=[pl.BlockSpec((T, HEAD_PAD), lambda h, j: (0, h)),
                   pl.BlockSpec((t, HEAD_PAD), lambda h, j: (j, h)),
                   pl.BlockSpec((t, V_HEAD), lambda h, j: (j, h))],
        out_shape=[_out((T, H * HEAD_PAD), BF16),
                   _out((T, H * HEAD_PAD), BF16),
                   _out((T, H * V_HEAD), BF16)],
        scratch=[pltpu.VMEM((T, HEAD_PAD), F32)],
        args=[_hbm(v) for v in (qh, kh, vh, cat, dcat, lse)], sem=("arbitrary", "arbitrary"), est=est,
        ride=ride)
    return dq, dk, dv, gathered


def _mem_fn(q, k, v):
    hd = q.shape[1] // MEM_HEADS
    outs = []
    for h in range(MEM_HEADS):
        lo, hi = h * hd, (h + 1) * hd
        s = _bdot_nt(_cols(q, lo, hi), _cols(k, lo, hi)) * hd ** -0.5
        e = jnp.exp(s - lax.stop_gradient(jnp.max(s, axis=1, keepdims=True)))
        p = e / jnp.sum(e, axis=1, keepdims=True)
        outs.append(_bdot_nn(p, _cols(v, lo, hi)))
    return jnp.concatenate(outs, axis=1)


def _mem_fwd(q, k, v, *, name):
    T, D = q.shape
    tm = _tile(T, 256, 16)

    def fn(i, tv, pv):
        return (_mem_fn(tv[0], pv[0], pv[1]),), ()

    return _rowwise(fn, [q], [k, v], [(D, BF16)], tm=tm, name=name)[0]


def _mem_bwd(q, k, v, do, *, name):
    T, D = q.shape
    tm = _tile(T, 256, 16)

    def fn(i, tv, pv):
        _, vjp = jax.vjp(_mem_fn, tv[0], pv[0], pv[1])
        dq, dk, dv = vjp(tv[1].astype(F32))
        return (dq,), (dk, dv)

    return _rowwise(fn, [q, do], [k, v], [(D, BF16)], [(k.shape, F32), (v.shape, F32)], tm=tm, name=name)


def _adamw(w, g, m, v, *, name):
    shape = w.shape
    if w.ndim != 3:
        lead3 = (1, math.prod(shape[:-1]), shape[-1])
        return [o.reshape(shape) for o in _adamw(*[a.reshape(lead3) for a in (w, g, m, v)], name=name)]
    Lw, R, C = shape
    tr = _tile(R, 512, 8)
    b1c = 1.0 - ADAM_B1 ** ADAM_STEP
    b2c = 1.0 - ADAM_B2 ** ADAM_STEP

    def body(w_ref, g_ref, m_ref, v_ref, d_ref, mo_ref, vo_ref):
        gg = g_ref[...]
        mn = ADAM_B1 * m_ref[...] + (1.0 - ADAM_B1) * gg
        vn = ADAM_B2 * v_ref[...] + (1.0 - ADAM_B2) * (gg * gg)
        d_ref[...] = -ADAM_LR * ((mn / b1c) / (jnp.sqrt(vn / b2c) + ADAM_EPS) + ADAM_WD * w_ref[...])
        mo_ref[...] = mn
        vo_ref[...] = vn

    spec = pl.BlockSpec((None, tr, C), lambda l, i: (l, i, 0))
    return pl.pallas_call(
        body, name=name, grid=(Lw, R // tr),
        in_specs=[spec] * 4, out_specs=[spec] * 3,
        out_shape=[_out(shape, F32)] * 3,
        compiler_params=_params(("parallel", "parallel"), 7 * tr * C * 4),
    )(*[_hbm(a) for a in (w, g, m, v)])


def _pair_sum(core, gs, landed, offs, *, name):
    n = len(gs)
    _, R, C = landed.shape
    rows = [g.shape[0] // N_DEV for g in gs]

    def body(core_ref, *refs):
        g_refs, l_ref, o_ref = refs[:n], refs[n], refs[n + 1]
        for g_ref, off, r in zip(g_refs, offs, rows):
            o_ref[off:off + r, :] = (g_ref[...].astype(F32) + l_ref[off:off + r, :].astype(F32)).astype(o_ref.dtype)

    slab = pl.BlockSpec((None, R, C), lambda p, core_ref: (p, 0, 0))
    own = [pl.BlockSpec((r, C), lambda p, core_ref: (2 * p + core_ref[0], 0)) for r in rows]
    return pl.pallas_call(
        body, name=name,
        grid_spec=pltpu.PrefetchScalarGridSpec(
            num_scalar_prefetch=1, grid=(4,), in_specs=own + [slab], out_specs=slab),
        out_shape=_out(landed.shape, landed.dtype),
        input_output_aliases={n + 1: 0},
        compiler_params=_params(("arbitrary",), 3 * R * C * 2 + R * C * 8),
    )(core, *[_hbm(g) for g in gs], _hbm(landed))


def _quad_sum(chip, part, gathered, used, *, name):
    C = part.shape[2]
    R = used
    tr = _tile(R, 256, 16)

    def body(chip_ref, own_ref, a_ref, b_ref, c_ref, o_ref):
        o_ref[...] = ((own_ref[...].astype(F32) + a_ref[...].astype(F32)) + b_ref[...].astype(F32)) \
            + c_ref[...].astype(F32)

    def other(k):
        return pl.BlockSpec((None, tr, C), lambda i, chip_ref: (chip_ref[0] ^ k, i, 0))

    return pl.pallas_call(
        body, name=name,
        grid_spec=pltpu.PrefetchScalarGridSpec(
            num_scalar_prefetch=1, grid=(R // tr,),
            in_specs=[pl.BlockSpec((None, tr, C), lambda i, chip_ref: (chip_ref[0], i, 0)),
                      other(1), other(2), other(3)],
            out_specs=pl.BlockSpec((tr, C), lambda i, chip_ref: (i, 0))),
        out_shape=_out((R, C), F32),
        compiler_params=_params(("arbitrary",), 8 * tr * C * 4),
    )(chip, _hbm(part), _hbm(gathered), _hbm(gathered), _hbm(gathered))


def _place():
    x, y, c = lax.axis_index("x"), lax.axis_index("y"), lax.axis_index("c")
    return x, y, c


ANY = pl.BlockSpec(memory_space=pl.ANY)


class _Gather:
    def __init__(self, shards):
        self.shards = list(shards)
        self.n = len(self.shards)
        self.out_shape = [_out((s.shape[0], N_DEV * s.shape[1], s.shape[2]), s.dtype)
                          for s in self.shards]
        self.scratch = [pltpu.SemaphoreType.DMA((7 * self.n,)), pltpu.SemaphoreType.DMA((7 * self.n,)),
                        pltpu.SemaphoreType.DMA((self.n,))]
        self.operands = [_hbm(s) for s in self.shards]

    def _bind(self, refs):
        n = self.n
        ins, outs = refs[:n], refs[n:2 * n]
        send_sems, recv_sems, local_sems = refs[2 * n:]
        x, y, c = _place()
        me, sib = (x, y, c), (x, y, 1 - c)
        chips = [(1 - x, y), (x, 1 - y), (1 - x, 1 - y)]

        def rows(w, p):
            r = self.shards[w].shape[1]
            idx = 4 * p[0] + 2 * p[1] + p[2]
            return outs[w].at[:, pl.ds(pl.multiple_of(idx * r, 8), r), :]

        def copy(w, k, block, to, src=None):
            return pltpu.make_async_remote_copy(
                src_ref=rows(w, block) if src is None else src, dst_ref=rows(w, block),
                send_sem=send_sems.at[w * 7 + k], recv_sem=recv_sems.at[w * 7 + k],
                device_id=to, device_id_type=MESH)

        def mine():
            return [pltpu.make_async_copy(ins[w], rows(w, me), local_sems.at[w]) for w in range(n)]

        def first():
            out = []
            for w in range(n):
                out.append(copy(w, 0, me, sib, src=ins[w]))
                out += [copy(w, 1 + j, me, (*chip, c), src=ins[w]) for j, chip in enumerate(chips)]
            return out

        def passed():
            return [copy(w, 4 + j, (*chip, c), sib) for j, chip in enumerate(chips) for w in range(n)]

        def landed():
            return [copy(w, 1 + j, (*chip, c), me) for j, chip in enumerate(chips) for w in range(n)]

        def last():
            out = []
            for w in range(n):
                out.append(copy(w, 0, sib, me))
                out += [copy(w, 4 + j, (*chip, 1 - c), me) for j, chip in enumerate(chips)]
            return out

        return mine, first, landed, passed, last

    def start(self, refs):
        mine, first, _, _, _ = self._bind(refs)
        for cp in mine() + first():
            cp.start()

    def forward(self, refs):
        _, _, landed, passed, _ = self._bind(refs)
        for arrived, fwd in zip(landed(), passed()):
            arrived.wait_recv()
            fwd.start()

    def finish(self, refs):
        mine, first, _, passed, last = self._bind(refs)
        for cp in last():
            cp.wait_recv()
        for cp in first() + passed():
            cp.wait_send()
        for cp in mine():
            cp.wait()


class _ChipExchange:
    def __init__(self, parts, used):
        self.ncl = len(parts)
        self.used = list(used)
        self.out_shape = [_out(p.shape, p.dtype) for p in parts]
        self.scratch = [pltpu.SemaphoreType.DMA((3 * self.ncl,)), pltpu.SemaphoreType.DMA((3 * self.ncl,))]
        self.operands = [_hbm(p) for p in parts]
        self.n = self.ncl

    def _bind(self, refs):
        ncl = self.ncl
        ins, outs = refs[:ncl], refs[ncl:2 * ncl]
        send_sems, recv_sems = refs[2 * ncl:]
        x, y, c = _place()
        chips = [(1 - x, y), (x, 1 - y), (1 - x, 1 - y)]
        here = 2 * x + y

        def copies(outgoing):
            out = []
            for k in range(ncl):
                rows = pl.ds(0, self.used[k])
                for j, (cx, cy) in enumerate(chips):
                    there = 2 * cx + cy
                    src, dst = (there, here) if outgoing else (here, there)
                    out.append(pltpu.make_async_remote_copy(
                        src_ref=ins[k].at[src, rows, :], dst_ref=outs[k].at[dst, rows, :],
                        send_sem=send_sems.at[3 * k + j], recv_sem=recv_sems.at[3 * k + j],
                        device_id=(cx, cy, c), device_id_type=MESH))
            return out

        return copies

    def start(self, refs):
        for cp in self._bind(refs)(True):
            cp.start()

    def forward(self, refs):
        pass

    def finish(self, refs):
        copies = self._bind(refs)
        for cp in copies(False):
            cp.wait_recv()
        for cp in copies(True):
            cp.wait_send()


class _Both:
    def __init__(self, members):
        self.members = list(members)
        self.n = sum(m.n for m in self.members)
        self.out_shape = [s for m in self.members for s in m.out_shape]
        self.scratch = [s for m in self.members for s in m.scratch]
        self.operands = [o for m in self.members for o in m.operands]

    def split(self, arrays):
        out, a = [], 0
        for m in self.members:
            out.append(list(arrays[a:a + m.n]))
            a += m.n
        return out

    def _refs(self, refs):
        ins, outs = self.split(refs[:self.n]), self.split(refs[self.n:2 * self.n])
        scr, b = [], 2 * self.n
        for m in self.members:
            scr.append(list(refs[b:b + len(m.scratch)]))
            b += len(m.scratch)
        return [(*i, *o, *s) for i, o, s in zip(ins, outs, scr)]

    def start(self, refs):
        for m, r in zip(self.members, self._refs(refs)):
            m.start(r)

    def forward(self, refs):
        for m, r in zip(self.members, self._refs(refs)):
            m.forward(r)

    def finish(self, refs):
        for m, r in zip(self.members, self._refs(refs)):
            m.finish(r)


def _exchange_alone(ex, *, name):
    def body(*refs):
        ex.start(refs)
        ex.forward(refs)
        ex.finish(refs)

    return pl.pallas_call(
        body, name=name, in_specs=[ANY] * ex.n, out_specs=[ANY] * ex.n,
        out_shape=ex.out_shape, scratch_shapes=ex.scratch,
    )(*ex.operands)


def _host_call(body, *, name, grid, in_specs, out_specs, out_shape, args, sem, est, ride=None, scratch=()):
    scratch = list(scratch)
    if ride is None:
        outs = pl.pallas_call(body, name=name, grid=grid, in_specs=in_specs, out_specs=out_specs,
                              out_shape=out_shape, scratch_shapes=scratch,
                              compiler_params=_params(sem, est))(*args)
        return list(outs), []
    n_in, n_out, n, n_scr = len(in_specs), len(out_specs), ride.n, len(scratch)

    def full(*refs):
        ins, rin = refs[:n_in], refs[n_in:n_in + n]
        outs, rout = refs[n_in + n:n_in + n + n_out], refs[n_in + n + n_out:n_in + 2 * n + n_out]
        own = refs[n_in + 2 * n + n_out:n_in + 2 * n + n_out + n_scr]
        rrefs = (*rin, *rout, *refs[n_in + 2 * n + n_out + n_scr:])
        step, total = _ride(ride, rrefs, grid)
        body(*ins, *outs, *own)
        _ride_end(ride, rrefs, step, total)

    outs = pl.pallas_call(
        full, name=name, grid=grid,
        in_specs=list(in_specs) + [ANY] * n, out_specs=list(out_specs) + [ANY] * n,
        out_shape=list(out_shape) + ride.out_shape, scratch_shapes=scratch + ride.scratch,
        compiler_params=_params(("arbitrary",) * len(grid), est),
    )(*args, *ride.operands)
    return list(outs[:n_out]), list(outs[n_out:])


def _ride(ex, refs, grid):
    total = math.prod(grid)
    step = pl.program_id(0)
    for axis in range(1, len(grid)):
        step = step * grid[axis] + pl.program_id(axis)
    pl.when(step == 0)(lambda: ex.start(refs))
    return step, total


def _ride_end(ex, refs, step, total):
    pl.when(step == (3 * total) // 4)(lambda: ex.forward(refs))
    pl.when(step == total - 1)(lambda: ex.finish(refs))


def _class_layout(grads, classes):
    used = [0] * len(set(classes))
    offs = []
    for g, cl in zip(grads, classes):
        offs.append(used[cl])
        used[cl] += g.shape[0] // N_DEV
    return offs, used


def _rs_to_sibling(grads, classes, *, name):
    n = len(grads)
    offs, used = _class_layout(grads, classes)
    heights = used
    ncl = len(heights)
    cols = [next(g.shape[1] for g, cl in zip(grads, classes) if cl == k) for k in range(ncl)]

    def body(*refs):
        gs, land = refs[:n], refs[n:n + ncl]
        send_sems, recv_sems = refs[n + ncl:]
        x, y, c = _place()
        sib = (x, y, 1 - c)
        for p in range(4):
            for w in range(n):
                r = grads[w].shape[0] // N_DEV
                cl = classes[w]
                there = gs[w].at[pl.ds(pl.multiple_of((2 * p + 1 - c) * r, 8), r), :]
                pltpu.make_async_remote_copy(
                    src_ref=there, dst_ref=land[cl].at[p, pl.ds(offs[w], r), :],
                    send_sem=send_sems.at[cl * 4 + p], recv_sem=recv_sems.at[cl * 4 + p],
                    device_id=sib, device_id_type=MESH).start()
        for cl in range(ncl):
            for p in range(4):
                rows_used = land[cl].at[p, pl.ds(0, used[cl]), :]
                slab = pltpu.make_async_remote_copy(
                    src_ref=rows_used, dst_ref=rows_used,
                    send_sem=send_sems.at[cl * 4 + p], recv_sem=recv_sems.at[cl * 4 + p],
                    device_id=sib, device_id_type=MESH)
                slab.wait_send()
                slab.wait_recv()

    return pl.pallas_call(
        body, name=name,
        in_specs=[ANY] * n, out_specs=[ANY] * ncl,
        out_shape=[_out((4, heights[k], cols[k]), BF16) for k in range(ncl)],
        scratch_shapes=[pltpu.SemaphoreType.DMA((4 * ncl,))] * 2,
    )(*[_hbm(g) for g in grads])


def _sum_devices(g, *, name):
    R = g.shape[1]

    def body(g_ref, o_ref):
        acc = g_ref[0]
        for d in range(1, N_DEV):
            acc = acc + g_ref[d]
        o_ref[...] = acc

    vm = pl.BlockSpec(memory_space=pltpu.VMEM)
    return pl.pallas_call(
        body, name=name, in_specs=[vm], out_specs=vm,
        out_shape=jax.ShapeDtypeStruct((R, LANE), F32),
        compiler_params=pltpu.CompilerParams(vmem_limit_bytes=VMEM_FLOOR),
    )(g)


def _rope_tables(positions):
    half = QK_ROPE // 2
    inv_freq = ROPE_BASE ** (-jnp.arange(half, dtype=F32) / half)
    ang = positions.astype(F32)[:, None] * inv_freq
    cos, sin = jnp.cos(ang), jnp.sin(ang)
    z = jnp.zeros_like(cos)
    z2 = jnp.zeros((positions.shape[0], LANE - QK_ROPE), F32)
    rc = jnp.concatenate([cos, cos, z2], axis=1)
    rs1 = jnp.concatenate([-sin, z, z2], axis=1)
    rs2 = jnp.concatenate([z, sin, z2], axis=1)
    return rc, rs1, rs2


def _block_diag(pool_w):
    G, pg, _ = pool_w.shape
    out = jnp.zeros((G * pg, G * pg), pool_w.dtype)
    for g in range(G):
        out = lax.dynamic_update_slice(out, pool_w[g], (g * pg, g * pg))
    return out


def kernel(x, mem, positions, ln_g, ln_b, ffn1_w13, ffn1_w2, w_in, pool_w, pool_scale, q_norm_g, w_uq, kv_norm_g, w_ukv, w_out, mem_wq, mem_wkv, mem_wo, ffn2_w13, ffn2_w2, loss_target, m_ln_g, m_ln_b, m_ffn1_w13, m_ffn1_w2, m_w_in, m_pool_w, m_pool_scale, m_q_norm_g, m_w_uq, m_kv_norm_g, m_w_ukv, m_w_out, m_mem_wq, m_mem_wkv, m_mem_wo, m_ffn2_w13, m_ffn2_w2, v_ln_g, v_ln_b, v_ffn1_w13, v_ffn1_w2, v_w_in, v_pool_w, v_pool_scale, v_q_norm_g, v_w_uq, v_kv_norm_g, v_w_ukv, v_w_out, v_mem_wq, v_mem_wkv, v_mem_wo, v_ffn2_w13, v_ffn2_w2):
    L = ln_g.shape[0]
    T, D = x.shape[1], x.shape[2]
    F = ffn1_w2.shape[1] * N_DEV
    PW = D // 4
    H = (D - PW) // V_HEAD
    DIN = w_in.shape[2]
    DINP = PW + Q_LORA + KV_LORA + LANE
    QW = QK_NOPE + QK_ROPE
    alpha = (2 * L) ** 0.25
    x2d = x.reshape(T, D)
    memb = mem.reshape(mem.shape[1], D).astype(BF16)
    target = loss_target.reshape(T, D)
    tabs = _rope_tables(positions.reshape(T))

    def shards_of(l):
        return dict(
            w13a=ffn1_w13[l].T[None].astype(BF16),
            w13b=ffn2_w13[l].T[None].astype(BF16),
            w2a=ffn1_w2[l][None].astype(BF16),
            w2b=ffn2_w2[l][None].astype(BF16),
            wsq=jnp.stack([w_out[l], mem_wq[l], mem_wo[l]]).astype(BF16),
            wkvT=mem_wkv[l].T[None].astype(BF16),
            winp=jnp.pad(w_in[l], ((0, 0), (0, DINP - DIN)))[None].astype(BF16),
            wuqT=w_uq[l].T[None].astype(BF16),
            wukvT=w_ukv[l].T[None].astype(BF16),
        )

    SMALL = ("winp", "wuqT", "wukvT")
    shards = [shards_of(l) for l in range(L)]
    W = [dict() for _ in range(L)]

    def rider(spec):
        return _Gather([shards[l][n] for l, n in spec]) if spec else None

    def arrived(spec, arrays):
        for (l, n), a in zip(spec, arrays):
            if n in ("w13a", "w13b"):
                a = _interleave(a, 1)
            elif n == "wuqT":
                a = jnp.pad(a.reshape(H, QW, Q_LORA), ((0, 0), (0, HEAD_PAD - QW), (0, 0)))
                a = a.reshape(1, H * HEAD_PAD, Q_LORA)
            elif n == "ln":
                a = jnp.moveaxis(a.reshape(N_DEV, 2, L, 4, D // N_DEV), 0, 3).reshape(2, L, 4, D)
                LN["g"], LN["b"] = a[0], a[1]
            W[l][n] = a

    LN = {}
    shards[0]["ln"] = jnp.concatenate([ln_g.reshape(1, 4 * L, -1), ln_b.reshape(1, 4 * L, -1)], axis=1)
    spec0 = [(0, "w13a")]
    arrived(spec0, _exchange_alone(rider(spec0), name="ag_first"))
    wbd = [_block_diag(pool_w[l]).astype(BF16) for l in range(L)]

    def ffn_fwd(l, which, xres, xb, k, spec):
        ab = "ab"[which]
        h13, a, rode = _ffn_up(xb, W[l]["w13" + ab], 0, name=f"l{l}_ffn{which}_up", ride=rider(spec))
        arrived(spec, rode)
        y, xo, xob = _mm_ln(a, W[l]["w2" + ab], 0, xres, LN["g"][l,k:k + 1], LN["b"][l,k:k + 1], alpha=alpha, s=0.5,
                            name=f"l{l}_ffn{which}_y_ln{k}")
        return dict(xres=xres, xb=xb, h13=h13, a=a, y=y), xo, xob

    saved = []
    xres, xb = x2d, x2d.astype(BF16)
    for l in range(L):
        sv = {}
        more = l + 1 < L
        Wl = W[l]
        spec = ([(0, "w2a"), (0, "ln"), *[(0, n) for n in SMALL], (0, "wkvT")] if l == 0
                else [(l, "wsq"), (l, "wkvT")])
        sv["ffn1"], x1, x1b = ffn_fwd(l, 0, xres, xb, 0, spec)
        hin = _mm(x1b, Wl["winp"], lead=0, name=f"l{l}_hin")
        pscale = pool_scale[l].reshape(1, PW)
        gq, gkv = q_norm_g[l].reshape(1, Q_LORA), kv_norm_g[l].reshape(1, KV_LORA)
        qh, kh, vh, cqn, ckvn = _heads_fwd(hin, gq, gkv, tabs, Wl["wuqT"][0], Wl["wukvT"][0], H=H, pw=PW,
                                           name=f"l{l}_heads")
        spec = [(l, "w13b"), (l, "w2b")] + ([(0, "wsq")] if l == 0 else []) + ([(l + 1, "w13a")] if more else [])
        cat, lse, rode = _flash_fwd(qh, kh, vh, H=H, pw=PW, name=f"l{l}_flash", ride=rider(spec))
        arrived(spec, rode)
        cat = _pool_fwd(hin, wbd[l], pscale, cat, name=f"l{l}_pool")
        ymix, x2, x2b = _mm_ln(cat, Wl["wsq"], 0, x1, LN["g"][l,1:2], LN["b"][l,1:2], alpha=alpha, s=1.0,
                               name=f"l{l}_ymix_ln1")
        qm = _mm(x2b, Wl["wsq"], lead=1, out_dtype=BF16, name=f"l{l}_qm")
        kvm = _mm(memb, Wl["wkvT"], lead=0, tb=True, name=f"l{l}_kvm")
        km, vm = kvm[:, :D], kvm[:, D:]
        om = _mem_fwd(qm, km, vm, name=f"l{l}_memattn")
        ymem, x3, x3b = _mm_ln(om, Wl["wsq"], 2, x2, LN["g"][l,2:3], LN["b"][l,2:3], alpha=alpha, s=1.0,
                               name=f"l{l}_ymem_ln2")
        spec = [(l + 1, n) for n in ("w2a", *SMALL)] if more else []
        sv["ffn2"], x4, x4b = ffn_fwd(l, 1, x3, x3b, 3, spec)
        sv.update(x1=x1, x1b=x1b, hin=hin, pscale=pscale, gq=gq, gkv=gkv, cqn=cqn, ckvn=ckvn,
                  qh=qh, kh=kh, vh=vh, lse=lse, cat=cat, ymix=ymix, x2=x2, x2b=x2b, qm=qm, km=km, vm=vm,
                  om=om, ymem=ymem)
        saved.append(sv)
        xres, xb = x4, x4b


    gW = {}
    gS = {}

    def ln_of(l, k):
        sv = saved[l]
        x, y, s = {0: (sv["ffn1"]["xres"], sv["ffn1"]["y"], 0.5), 1: (sv["x1"], sv["ymix"], 1.0),
                   2: (sv["x2"], sv["ymem"], 1.0), 3: (sv["ffn2"]["xres"], sv["ffn2"]["y"], 0.5)}[k]
        return x, y, LN["g"][l, k:k + 1], s

    def dx_through_ln(a, b, lead, tb, add, into, name):
        x, y, g, s = ln_of(*into)
        dxres, dyb, dg, db = _mm_ln_bwd(a, b, lead, tb, add, x, y, g, alpha=alpha, s=s, name=name)
        gS[("ln_g", *into)], gS[("ln_b", *into)] = dg, db
        return dxres, dyb

    def ffn_bwd(l, which, sv, dxres, dyb, ride, into):
        tag = f"l{l}_ffn{which}"
        gW[("w2", which, l)] = _mm(sv["a"], dyb, ta=True, out_dtype=BF16, name=f"{tag}_dw2", tn=D)
        dh, rode = _ffn_down_bwd(dyb, W[l]["w2" + "ab"[which]], 0, sv["h13"], name=f"{tag}_dh", ride=ride)
        dw13 = _mm(dh, sv["xb"], ta=True, out_dtype=BF16, name=f"{tag}_dw13", tn=D)
        gW[("w13", which, l)] = _deinterleave(dw13, 0)
        w13 = W[l]["w13" + "ab"[which]]
        if into is not None:
            return dx_through_ln(dh, w13, 0, False, dxres, into, f"{tag}_dx"), rode
        last = rs_first_level(l, "c")
        dxn, got = _mm(dh, w13, lead=0, add=dxres, name=f"{tag}_dx", tn=D, ride=last["ex"])
        rs_last_level(last, got)
        return dxn, rode

    core = lax.axis_index("c").astype(jnp.int32).reshape(1)
    chip = (2 * lax.axis_index("x") + lax.axis_index("y")).astype(jnp.int32).reshape(1)
    gsh = {}

    def rs_first_level(l, group):
        keys, classes = {
            "a": ([("w13", 1, l), ("w2", 1, l), ("mem_wkv", l), ("mem_wq", l), ("mem_wo", l)], [0] * 5),
            "b": ([("w_out", l), ("w_in", l), ("w_uq", l), ("w_ukv", l)], [0, 1, 2, 3]),
            "c": ([("w13", 0, l), ("w2", 0, l)], [0, 0]),
        }[group]
        tag = f"l{l}{group}"
        garrs = []
        for key in keys:
            g = gW[key]
            if key[0] == "w_uq":
                g = g.reshape(H, HEAD_PAD, Q_LORA)[:, :QW, :].reshape(H * QW, Q_LORA)
            garrs.append(g)
        offs, used = _class_layout(garrs, classes)
        parts = list(_rs_to_sibling(garrs, classes, name=f"{tag}_rs_sibling"))
        for cl in range(len(parts)):
            mine = [w for w, c in enumerate(classes) if c == cl]
            parts[cl] = _pair_sum(core, [garrs[w] for w in mine], parts[cl], [offs[w] for w in mine],
                                  name=f"{tag}_rs_pair_sum{cl}")
        return dict(tag=tag, keys=keys, garrs=garrs, classes=classes, offs=offs, used=used, parts=parts,
                    ex=_ChipExchange(parts, used))

    def rs_last_level(st, gathered):
        sums = [_quad_sum(chip, p, a, u, name=f"{st['tag']}_rs_quad_sum{k}")
                for k, (p, a, u) in enumerate(zip(st["parts"], gathered, st["used"]))]
        for key, g, cl, off in zip(st["keys"], st["garrs"], st["classes"], st["offs"]):
            gsh[key] = sums[cl][off:off + g.shape[0] // N_DEV, :]

    top = (L - 1, 3)
    x_top, y_top, g_top, s_top = ln_of(*top)
    dxres, dyb, gS[("ln_g", *top)], gS[("ln_b", *top)], loss_blk = _loss_ln_bwd(
        x_top, y_top, g_top, xres, target, alpha=alpha, s=s_top, name="loss_ln_top_bwd")
    above = None
    for l in reversed(range(L)):
        sv = saved[l]
        Wl = W[l]
        (dxres, dyb), _ = ffn_bwd(l, 1, sv["ffn2"], dxres, dyb, None, (l, 2))
        dom, gW[("mem_wo", l)] = _mm_pair(dyb, Wl["wsq"], 2, sv["om"], dx_dtype=BF16, name=f"l{l}_dom_dwo")
        dqm, dkm, dvm = _mem_bwd(sv["qm"], sv["km"], sv["vm"], dom, name=f"l{l}_memattn_bwd")
        dxres, dyb = dx_through_ln(dqm, Wl["wsq"], 1, True, dxres, (l, 1), f"l{l}_dx2")
        gW[("mem_wq", l)] = _mm(sv["x2b"], dqm, ta=True, out_dtype=BF16, name=f"l{l}_dwq", tn=D)
        dkvm = jnp.concatenate([dkm, dvm], axis=1).astype(BF16)
        gW[("mem_wkv", l)] = _mm(dkvm, memb, ta=True, out_dtype=BF16, name=f"l{l}_dwkv", tn=D)
        dcat, gW[("w_out", l)] = _mm_pair(dyb, Wl["wsq"], 0, sv["cat"], dx_dtype=F32, name=f"l{l}_dcat_dwout")
        riding = [rs_first_level(l, "a")] + ([above] if above else [])
        both = _Both([st["ex"] for st in riding])
        dqh, dkh, dvh, rode = _flash_bwd(sv["qh"], sv["kh"], sv["vh"], sv["cat"], dcat, sv["lse"], H=H, pw=PW,
                                         name=f"l{l}_flash_bwd", ride=both)
        for st, got in zip(riding, both.split(rode)):
            rs_last_level(st, got)
        dqraw, dkv, dkpe = _heads_bwd(dqh, dkh, dvh, tabs, H=H, name=f"l{l}_heads_bwd")
        dcq, gW[("w_uq", l)] = _mm_pair(dqraw, Wl["wuqT"], 0, sv["cqn"], dx_dtype=F32, wt=True,
                                        name=f"l{l}_dcq_dwuq")
        dckv, gW[("w_ukv", l)] = _mm_pair(dkv, Wl["wukvT"], 0, sv["ckvn"], dx_dtype=F32, wt=True,
                                          name=f"l{l}_dckv_dwukv")
        du, dwbd, dps = _pool_bwd(sv["hin"], dcat, wbd[l], sv["pscale"], name=f"l{l}_pool_bwd")
        dhin, dgq, dgkv = _norms_bwd(sv["hin"], sv["gq"], sv["gkv"], dcq, dckv, dkpe, du, pw=PW,
                                     name=f"l{l}_norms_bwd")
        pg = PW // len(POOL_WINDOWS)
        gS[("pool_w", l)] = jnp.stack([dwbd[g * pg:(g + 1) * pg, g * pg:(g + 1) * pg]
                                       for g in range(len(POOL_WINDOWS))])
        gS[("pool_scale", l)], gS[("q_norm_g", l)], gS[("kv_norm_g", l)] = dps, dgq, dgkv
        dxres, dyb = dx_through_ln(dhin, Wl["winp"], 0, True, dxres, (l, 0), f"l{l}_dx1")
        gW[("w_in", l)] = _mm(sv["x1b"], dhin, ta=True, out_dtype=BF16, name=f"l{l}_dwin", tn=DINP)
        heads = rs_first_level(l, "b")
        riding = [heads["ex"]]
        if l == 0:
            small_keys = []
            for ll in range(L):
                small_keys += [("pool_w", ll), ("pool_scale", ll), ("q_norm_g", ll), ("kv_norm_g", ll)]
                small_keys += [("ln_g", ll, k) for k in range(4)] + [("ln_b", ll, k) for k in range(4)]
            flat = jnp.concatenate([loss_blk[0, :1]] + [gS[k].reshape(-1) for k in small_keys])
            n_small = flat.shape[0]
            rows = -(-n_small // (8 * LANE)) * 8
            flat = jnp.pad(flat, (0, rows * LANE - n_small)).reshape(1, rows, LANE)
            riding.append(_Gather([flat]))
        both = _Both(riding)
        below, rode = ffn_bwd(l, 0, sv["ffn1"], dxres, dyb, both, (l - 1, 3) if l > 0 else None)
        rode = both.split(rode)
        rs_last_level(heads, rode[0])
        if l > 0:
            dxres, dyb = below
            above = rs_first_level(l, "c")
    grad_x = below.reshape(1, T, D)

    red = _sum_devices(rode[1][0].reshape(N_DEV, rows, LANE), name="sum_small").reshape(-1)
    loss = red[0]
    gsm, pos = {}, 1
    for k in small_keys:
        size = math.prod(gS[k].shape)
        gsm[k] = red[pos:pos + size].reshape(gS[k].shape)
        pos += size

    me = 4 * lax.axis_index("x") + 2 * lax.axis_index("y") + lax.axis_index("c")
    dsh = D // N_DEV
    stack = lambda f: jnp.stack([f(l) for l in range(L)])
    g_ln_g = stack(lambda l: jnp.concatenate([gsm[("ln_g", l, k)] for k in range(4)], axis=0))
    g_ln_b = stack(lambda l: jnp.concatenate([gsm[("ln_b", l, k)] for k in range(4)], axis=0))
    swapped = {
        "ffn1_w13": stack(lambda l: gsh[("w13", 0, l)]),
        "ffn2_w13": stack(lambda l: gsh[("w13", 1, l)]),
        "w_in": stack(lambda l: gsh[("w_in", l)][:, :DIN].T),
        "w_uq": stack(lambda l: gsh[("w_uq", l)]),
        "w_ukv": stack(lambda l: gsh[("w_ukv", l)]),
    }
    swap = lambda a: jnp.swapaxes(a, 1, 2)
    grads = {
        "ln_g": lax.dynamic_slice_in_dim(g_ln_g, me * dsh, dsh, axis=2),
        "ln_b": lax.dynamic_slice_in_dim(g_ln_b, me * dsh, dsh, axis=2),
        "ffn1_w2": stack(lambda l: gsh[("w2", 0, l)]),
        "pool_w": stack(lambda l: gsm[("pool_w", l)]),
        "pool_scale": stack(lambda l: gsm[("pool_scale", l)].reshape(PW)),
        "q_norm_g": stack(lambda l: gsm[("q_norm_g", l)].reshape(Q_LORA)),
        "kv_norm_g": stack(lambda l: gsm[("kv_norm_g", l)].reshape(KV_LORA)),
        "w_out": stack(lambda l: gsh[("w_out", l)]),
        "mem_wq": stack(lambda l: gsh[("mem_wq", l)]),
        "mem_wkv": stack(lambda l: gsh[("mem_wkv", l)].T),
        "mem_wo": stack(lambda l: gsh[("mem_wo", l)]),
        "ffn2_w2": stack(lambda l: gsh[("w2", 1, l)]),
        **{nme: swap(g) for nme, g in swapped.items()},
    }

    names = ["ln_g", "ln_b", "ffn1_w13", "ffn1_w2", "w_in", "pool_w", "pool_scale", "q_norm_g", "w_uq",
             "kv_norm_g", "w_ukv", "w_out", "mem_wq", "mem_wkv", "mem_wo", "ffn2_w13", "ffn2_w2"]
    weights = dict(ln_g=ln_g, ln_b=ln_b, ffn1_w13=ffn1_w13, ffn1_w2=ffn1_w2, w_in=w_in, pool_w=pool_w,
                   pool_scale=pool_scale, q_norm_g=q_norm_g, w_uq=w_uq, kv_norm_g=kv_norm_g, w_ukv=w_ukv,
                   w_out=w_out, mem_wq=mem_wq, mem_wkv=mem_wkv, mem_wo=mem_wo, ffn2_w13=ffn2_w13,
                   ffn2_w2=ffn2_w2)
    ms = dict(ln_g=m_ln_g, ln_b=m_ln_b, ffn1_w13=m_ffn1_w13, ffn1_w2=m_ffn1_w2, w_in=m_w_in, pool_w=m_pool_w,
              pool_scale=m_pool_scale, q_norm_g=m_q_norm_g, w_uq=m_w_uq, kv_norm_g=m_kv_norm_g,
              w_ukv=m_w_ukv, w_out=m_w_out, mem_wq=m_mem_wq, mem_wkv=m_mem_wkv, mem_wo=m_mem_wo,
              ffn2_w13=m_ffn2_w13, ffn2_w2=m_ffn2_w2)
    vs = dict(ln_g=v_ln_g, ln_b=v_ln_b, ffn1_w13=v_ffn1_w13, ffn1_w2=v_ffn1_w2, w_in=v_w_in, pool_w=v_pool_w,
              pool_scale=v_pool_scale, q_norm_g=v_q_norm_g, w_uq=v_w_uq, kv_norm_g=v_kv_norm_g,
              w_ukv=v_w_ukv, w_out=v_w_out, mem_wq=v_mem_wq, mem_wkv=v_mem_wkv, mem_wo=v_mem_wo,
              ffn2_w13=v_ffn2_w13, ffn2_w2=v_ffn2_w2)
    deltas, new_m, new_v = [], [], []
    for nme in names:
        if nme in swapped:
            d, mn, vn = [swap(o) for o in _adamw(swap(weights[nme]), swapped[nme], swap(ms[nme]), swap(vs[nme]),
                                                 name=f"adamw_{nme}")]
        else:
            d, mn, vn = _adamw(weights[nme], grads[nme], ms[nme], vs[nme], name=f"adamw_{nme}")
        deltas.append(d)
        new_m.append(mn)
        new_v.append(vn)
    return (loss, grad_x, *[grads[nme] for nme in names], *deltas, *new_m, *new_v)
```

```python
import functools
import math

import jax
import jax.numpy as jnp
from jax import lax
from jax.experimental import pallas as pl
from jax.experimental.pallas import tpu as pltpu

F32 = jnp.float32
BF16 = jnp.bfloat16
MESH = pl.DeviceIdType.MESH

CHUNK = 64
MEM_HEADS = 4
POOL_WINDOWS = (2, 4, 8, 16)
QK_NOPE = 128
QK_ROPE = 64
V_HEAD = 128
Q_LORA = 256
KV_LORA = 128
ROPE_BASE = 10000.0
LN_EPS = 1e-5
RMS_EPS = 1e-6
NEG_INF = -1e30
ADAM_LR = 0.001
ADAM_B1 = 0.9
ADAM_B2 = 0.999
ADAM_EPS = 1e-08
ADAM_WD = 0.01
ADAM_STEP = 10

N_DEV = 8
LANE = 128
HEAD_PAD = 2 * LANE
POOL_HALO = 16
VMEM_CAP = 56 * 1024 * 1024
VMEM_FLOOR = 32 * 1024 * 1024


def _tile(n, pref, mult):
    t = (min(pref, n) // mult) * mult
    while t >= mult:
        if n % t == 0:
            return t
        t -= mult
    return n


def _params(sem, est_bytes):
    limit = int(min(max(2 * est_bytes + (8 << 20), VMEM_FLOOR), VMEM_CAP))
    return pltpu.CompilerParams(dimension_semantics=sem, vmem_limit_bytes=limit)


def _nbytes(shape, dtype):
    return math.prod(shape) * jnp.dtype(dtype).itemsize


def _hbm(x):
    return pltpu.with_memory_space_constraint(x, pltpu.HBM)


def _out(shape, dtype):
    return pltpu.HBM(tuple(shape), dtype)


def _dg(a, b, ca, cb):
    return lax.dot_general(a.astype(BF16), b.astype(BF16), (((ca,), (cb,)), ((), ())),
                           preferred_element_type=F32)


@jax.custom_vjp
def _bdot_nn(a, b):
    return _dg(a, b, 1, 0)


def _bdot_nn_fwd(a, b):
    return _dg(a, b, 1, 0), (a, b)


def _bdot_nn_bwd(res, ct):
    a, b = res
    return _dg(ct, b, 1, 1).astype(a.dtype), _dg(a, ct, 0, 0).astype(b.dtype)


_bdot_nn.defvjp(_bdot_nn_fwd, _bdot_nn_bwd)


@jax.custom_vjp
def _bdot_nt(a, b):
    return _dg(a, b, 1, 1)


def _bdot_nt_fwd(a, b):
    return _dg(a, b, 1, 1), (a, b)


def _bdot_nt_bwd(res, ct):
    a, b = res
    return _dg(ct, b, 1, 0).astype(a.dtype), _dg(ct, a, 0, 0).astype(b.dtype)


_bdot_nt.defvjp(_bdot_nt_fwd, _bdot_nt_bwd)


@functools.partial(jax.custom_vjp, nondiff_argnums=(1,))
def _lane_roll(x, shift):
    return pltpu.roll(x, shift % x.shape[1], axis=1)


def _lane_roll_fwd(x, shift):
    return _lane_roll(x, shift), None


def _lane_roll_bwd(shift, _, ct):
    return (_lane_roll(ct, -shift),)


_lane_roll.defvjp(_lane_roll_fwd, _lane_roll_bwd)


@functools.partial(jax.custom_vjp, nondiff_argnums=(1, 2))
def _cols(x, lo, hi):
    return x[:, lo:hi]


def _cols_fwd(x, lo, hi):
    return x[:, lo:hi], x.shape[1]


def _cols_bwd(lo, hi, width, ct):
    parts = []
    if lo > 0:
        parts.append(jnp.zeros((ct.shape[0], lo), ct.dtype))
    parts.append(ct)
    if hi < width:
        parts.append(jnp.zeros((ct.shape[0], width - hi), ct.dtype))
    return (jnp.concatenate(parts, axis=1) if len(parts) > 1 else ct,)


_cols.defvjp(_cols_fwd, _cols_bwd)


MM_VMEM_BUDGET = 22 * 1024 * 1024


def _mm(a, b, *, name, ta=False, tb=False, out_dtype=F32, lead=None, add=None, add_scale=1.0,
        tm=1024, tn=1024, tk=8192, ride=None):
    if ta:
        K, M = a.shape
    else:
        M, K = a.shape
    bshape = b.shape[1:] if lead is not None else b.shape
    if tb:
        N, Kb = bshape
    else:
        Kb, N = bshape
    assert K == Kb, (name, a.shape, b.shape)

    def blocks(tm, tn, tk):
        tm = _tile(M, tm, LANE if ta else 16)
        tn = _tile(N, tn, LANE)
        tk = _tile(K, tk, LANE)
        nbytes = (tm * tk * a.dtype.itemsize + tk * tn * b.dtype.itemsize
                  + tm * tn * (jnp.dtype(out_dtype).itemsize + (4 if K // tk > 1 else 0)
                               + (add.dtype.itemsize if add is not None else 0)))
        return tm, tn, tk, nbytes

    if ta:
        tm = min(tm, max(LANE, M // 4))
    tm, tn, tk, est = blocks(tm, tn, tk)
    for shrink in ("m", "k", "m", "k", "n"):
        if est <= MM_VMEM_BUDGET:
            break
        if shrink == "m":
            tm, tn, tk, est = blocks(max(tm // 2, LANE), tn, tk)
        elif shrink == "k":
            tm, tn, tk, est = blocks(tm, tn, max(tk // 2, LANE))
        else:
            tm, tn, tk, est = blocks(tm, max(tn // 2, LANE), tk)
    nk = K // tk
    ca = 0 if ta else 1
    cb = 1 if tb else 0

    def body(*refs):
        a_ref, b_ref = refs[0], refs[1]
        add_ref = refs[2] if add is not None else None
        o_ref = refs[3] if add is not None else refs[2]

        def finish(r):
            if add_ref is not None:
                r = r + add_scale * add_ref[...].astype(F32)
            o_ref[...] = r.astype(o_ref.dtype)

        if nk == 1:
            finish(_dg(a_ref[...], b_ref[...], ca, cb))
            return
        acc_ref = refs[-1]
        k = pl.program_id(2)

        @pl.when(k == 0)
        def _():
            acc_ref[...] = jnp.zeros_like(acc_ref)

        acc_ref[...] += _dg(a_ref[...], b_ref[...], ca, cb)

        @pl.when(k == nk - 1)
        def _():
            finish(acc_ref[...])

    a_blk = (tk, tm) if ta else (tm, tk)
    a_map = (lambda i, j, k: (k, i)) if ta else (lambda i, j, k: (i, k))
    b_blk = (tn, tk) if tb else (tk, tn)
    if lead is None:
        b_map = (lambda i, j, k: (j, k)) if tb else (lambda i, j, k: (k, j))
        b_spec = pl.BlockSpec(b_blk, b_map)
    else:
        b_map = (lambda i, j, k: (lead, j, k)) if tb else (lambda i, j, k: (lead, k, j))
        b_spec = pl.BlockSpec((None,) + b_blk, b_map)
    in_specs = [pl.BlockSpec(a_blk, a_map), b_spec]
    args = [a, b]
    if add is not None:
        in_specs.append(pl.BlockSpec((tm, tn), lambda i, j, k: (i, j)))
        args.append(add)
    (out,), rode = _host_call(
        body, name=name,
        grid=(M // tm, N // tn, nk),
        in_specs=in_specs,
        out_specs=[pl.BlockSpec((tm, tn), lambda i, j, k: (i, j))],
        out_shape=[_out((M, N), out_dtype)],
        scratch=[pltpu.VMEM((tm, tn), F32)] if nk > 1 else [],
        args=[_hbm(v) for v in args], sem=("parallel", "parallel", "arbitrary"), est=est + tm * tn * 4,
        ride=ride)
    return out if ride is None else (out, rode)


def _mm_pair(dy, w, lead, x, *, dx_dtype, name, wt=False):
    M, N = dy.shape
    Kx = x.shape[1]
    wshape = (N, Kx) if wt else (Kx, N)
    assert w.shape[1:] == wshape, (name, w.shape, x.shape, dy.shape)
    tm = _tile(M, 512, 16)
    nt = M // tm

    def body(dy_ref, w_ref, x_ref, dx_ref, dw_ref, acc_ref):
        i = pl.program_id(0)

        @pl.when(i == 0)
        def _():
            acc_ref[...] = jnp.zeros_like(acc_ref)

        dyv = dy_ref[...]
        dx_ref[...] = _dg(dyv, w_ref[...], 1, 0 if wt else 1).astype(dx_ref.dtype)
        acc_ref[...] += _dg(dyv, x_ref[...], 0, 0) if wt else _dg(x_ref[...], dyv, 0, 0)

        @pl.when(i == nt - 1)
        def _():
            dw_ref[...] = acc_ref[...].astype(dw_ref.dtype)

    est = tm * (N + 2 * Kx) * 4 + Kx * N * (2 + 4 + 2)
    return pl.pallas_call(
        body, name=name, grid=(nt,),
        in_specs=[pl.BlockSpec((tm, N), lambda i: (i, 0)),
                  pl.BlockSpec((None,) + wshape, lambda i: (lead, 0, 0)),
                  pl.BlockSpec((tm, Kx), lambda i: (i, 0))],
        out_specs=[pl.BlockSpec((tm, Kx), lambda i: (i, 0)), pl.BlockSpec(wshape, lambda i: (0, 0))],
        out_shape=[_out((M, Kx), dx_dtype), _out(wshape, BF16)],
        scratch_shapes=[pltpu.VMEM(wshape, F32)],
        compiler_params=_params(("arbitrary",), est),
    )(_hbm(dy), _hbm(w), _hbm(x))


def _rowwise(fn, tiles, params, tile_outs, acc_outs=(), *, tm, name):
    tile_arrays, tile_specs = [], []
    for t in tiles:
        if isinstance(t, tuple):
            tile_arrays.append(t[0])
            tile_specs.append(t[1])
        else:
            tile_arrays.append(t)
            tile_specs.append(pl.BlockSpec((tm, t.shape[1]), lambda i: (i, 0)))
    T = tile_arrays[0].shape[0]
    nt, np_, nto, nao = len(tile_arrays), len(params), len(tile_outs), len(acc_outs)

    def body(*refs):
        i = pl.program_id(0)
        tvals = [r[...] for r in refs[:nt]]
        pvals = [r[...] for r in refs[nt:nt + np_]]
        to_refs = refs[nt + np_:nt + np_ + nto]
        ao_refs = refs[nt + np_ + nto:]
        touts, aouts = fn(i, tvals, pvals)
        for r, v in zip(to_refs, touts):
            r[...] = v.astype(r.dtype)
        if nao:
            @pl.when(i == 0)
            def _():
                for r in ao_refs:
                    r[...] = jnp.zeros_like(r)
            for r, v in zip(ao_refs, aouts):
                r[...] += v.astype(r.dtype)

    in_specs = tile_specs + [pl.BlockSpec(p.shape, lambda i: (0, 0)) for p in params]
    out_specs = [pl.BlockSpec((tm, c), lambda i: (i, 0)) for c, _ in tile_outs]
    out_specs += [pl.BlockSpec(s, lambda i: (0, 0)) for s, _ in acc_outs]
    out_shape = [_out((T, c), d) for c, d in tile_outs]
    out_shape += [_out(s, d) for s, d in acc_outs]
    width = sum(s.block_shape[-1] for s in tile_specs) + sum(c for c, _ in tile_outs)
    est = 6 * tm * width * 4 + sum(_nbytes(p.shape, F32) for p in params) * 4
    return pl.pallas_call(
        body, name=name, grid=(T // tm,),
        in_specs=in_specs, out_specs=out_specs, out_shape=out_shape,
        compiler_params=_params(("arbitrary",) if nao else ("parallel",), est),
    )(*[_hbm(v) for v in tile_arrays], *[_hbm(p) for p in params])


def _ln_fn(alpha, s, xres, y, g, b):
    z = alpha * xres.astype(F32) + s * y.astype(F32)
    mu = jnp.mean(z, axis=-1, keepdims=True)
    zc = z - mu
    var = jnp.mean(zc * zc, axis=-1, keepdims=True)
    return zc * lax.rsqrt(var + LN_EPS) * g + b


def _mm_ln(a, b, lead, xres, g, bias, *, alpha, s, name):
    M, K = a.shape
    N = b.shape[2]
    tm = _tile(M, 512, 16)

    def body(a_ref, b_ref, x_ref, g_ref, bias_ref, y_ref, xo_ref, xb_ref):
        y = _dg(a_ref[...], b_ref[...], 1, 0)
        y_ref[...] = y.astype(y_ref.dtype)
        out = _ln_fn(alpha, s, x_ref[...], y, g_ref[...], bias_ref[...])
        xo_ref[...] = out
        xb_ref[...] = out.astype(BF16)

    row = pl.BlockSpec((tm, N), lambda i: (i, 0))
    vec = pl.BlockSpec((1, N), lambda i: (0, 0))
    est = tm * K * 2 + K * N * 2 + tm * N * (4 + 4 + 4 + 2 + 8)
    return pl.pallas_call(
        body, name=name, grid=(M // tm,),
        in_specs=[pl.BlockSpec((tm, K), lambda i: (i, 0)), pl.BlockSpec((None, K, N), lambda i: (lead, 0, 0)),
                  row, vec, vec],
        out_specs=[row, row, row],
        out_shape=[_out((M, N), BF16), _out((M, N), F32), _out((M, N), BF16)],
        compiler_params=_params(("parallel",), est),
    )(_hbm(a), _hbm(b), _hbm(xres), _hbm(g), _hbm(bias))


def _ln_bwd_math(alpha, s, x, y, g, d):
    z = alpha * x + s * y.astype(F32)
    zc = z - jnp.mean(z, axis=-1, keepdims=True)
    r = lax.rsqrt(jnp.mean(zc * zc, axis=-1, keepdims=True) + LN_EPS)
    xh = zc * r
    dxh = d * g
    dz = r * (dxh - jnp.mean(dxh, axis=-1, keepdims=True) - xh * jnp.mean(dxh * xh, axis=-1, keepdims=True))
    return alpha * dz, s * dz, jnp.sum(d * xh, axis=0, keepdims=True), jnp.sum(d, axis=0, keepdims=True)


def _mm_ln_bwd(a, b, lead, tb, add, xres, y, g, *, alpha, s, name, xin=None):
    M, K = a.shape
    N = b.shape[1] if tb else b.shape[2]
    tk = K if K * N * 2 <= MM_VMEM_BUDGET * 3 // 5 else _tile(K, 2816, LANE)
    tm = _tile(M, 512 if K * N * 2 <= MM_VMEM_BUDGET // 4 else 256, 16)
    nk = K // tk
    nt = M // tm
    cb = 1 if tb else 0
    assert xin is None or (tb and nk == 1), name

    def body(*refs):
        a_ref, b_ref, add_ref, x_ref, y_ref, g_ref = refs[:6]
        rest = refs[6:]
        if xin is not None:
            xin_ref, rest = rest[0], rest[1:]
        dx_ref, dy_ref, dg_ref, db_ref = rest[:4]
        scratch = rest[4:]
        i, k = pl.program_id(0), pl.program_id(1)

        def finish(d):
            @pl.when(i == 0)
            def _():
                dg_ref[...] = jnp.zeros_like(dg_ref)
                db_ref[...] = jnp.zeros_like(db_ref)

            dx, dy, dg, db = _ln_bwd_math(alpha, s, x_ref[...], y_ref[...], g_ref[...], d + add_ref[...])
            dx_ref[...] = dx
            dy_ref[...] = dy.astype(dy_ref.dtype)
            dg_ref[...] += dg
            db_ref[...] += db

        if xin is not None:
            dw_ref, accw_ref = scratch[0], scratch[1]

            @pl.when(i == 0)
            def _():
                accw_ref[...] = jnp.zeros_like(accw_ref)

            accw_ref[...] += _dg(xin_ref[...], a_ref[...], 0, 0)

            @pl.when(i == nt - 1)
            def _():
                dw_ref[...] = accw_ref[...].astype(dw_ref.dtype)

        if nk == 1:
            finish(_dg(a_ref[...], b_ref[...], 1, cb))
            return
        acc_ref = scratch[0]

        @pl.when(k == 0)
        def _():
            acc_ref[...] = jnp.zeros_like(acc_ref)

        acc_ref[...] += _dg(a_ref[...], b_ref[...], 1, cb)

        @pl.when(k == nk - 1)
        def _():
            finish(acc_ref[...])

    row = pl.BlockSpec((tm, N), lambda i, k: (i, 0))
    vec = pl.BlockSpec((1, N), lambda i, k: (0, 0))
    b_spec = (pl.BlockSpec((None, N, tk), lambda i, k: (lead, 0, k)) if tb
              else pl.BlockSpec((None, tk, N), lambda i, k: (lead, k, 0)))
    est = tm * tk * 2 + tk * N * 2 + tm * N * (4 + 4 + 2 + 4 + 2 + 4 + 12)
    in_specs = [pl.BlockSpec((tm, tk), lambda i, k: (i, k)), b_spec, row, row, row, vec]
    out_specs = [row, row, vec, vec]
    out_shape = [_out((M, N), F32), _out((M, N), BF16), _out((1, N), F32), _out((1, N), F32)]
    scratch = [pltpu.VMEM((tm, N), F32)] if nk > 1 else []
    args = [a, b, add, xres, y, g]
    if xin is not None:
        in_specs.append(row)
        args.append(xin)
        out_specs.append(pl.BlockSpec((N, K), lambda i, k: (0, 0)))
        out_shape.append(_out((N, K), BF16))
        scratch.append(pltpu.VMEM((N, K), F32))
        est += N * K * 8 + tm * N * 2
    return pl.pallas_call(
        body, name=name, grid=(nt, nk),
        in_specs=in_specs, out_specs=out_specs, out_shape=out_shape, scratch_shapes=scratch,
        compiler_params=_params(("arbitrary", "arbitrary"), est),
    )(*[_hbm(v) for v in args])


def _loss_ln_bwd(xres, y, g, out, target, *, alpha, s, name):
    T, D = xres.shape
    tm = _tile(T, 256, 16)

    def body(x_ref, y_ref, o_ref, t_ref, g_ref, dx_ref, dy_ref, dg_ref, db_ref, loss_ref):
        @pl.when(pl.program_id(0) == 0)
        def _():
            dg_ref[...] = jnp.zeros_like(dg_ref)
            db_ref[...] = jnp.zeros_like(db_ref)
            loss_ref[...] = jnp.zeros_like(loss_ref)

        err = o_ref[...] - t_ref[...]
        part = 0.5 * jnp.sum(jnp.sum(err * err, axis=1, keepdims=True) / D, axis=0, keepdims=True)
        loss_ref[...] += jnp.broadcast_to(part, loss_ref.shape)
        dx, dy, dg, db = _ln_bwd_math(alpha, s, x_ref[...], y_ref[...], g_ref[...], err / D)
        dx_ref[...] = dx
        dy_ref[...] = dy.astype(dy_ref.dtype)
        dg_ref[...] += dg
        db_ref[...] += db

    row = pl.BlockSpec((tm, D), lambda i: (i, 0))
    vec = pl.BlockSpec((1, D), lambda i: (0, 0))
    return pl.pallas_call(
        body, name=name, grid=(T // tm,),
        in_specs=[row, row, row, row, vec],
        out_specs=[row, row, vec, vec, pl.BlockSpec((8, LANE), lambda i: (0, 0))],
        out_shape=[_out((T, D), F32), _out((T, D), BF16), _out((1, D), F32), _out((1, D), F32),
                   _out((8, LANE), F32)],
        compiler_params=_params(("arbitrary",), 14 * tm * D * 4),
    )(_hbm(xres), _hbm(y), _hbm(out), _hbm(target), _hbm(g))


FFN_TILE = 256


def _interleave(w, axis):
    n = w.shape[axis] // (2 * FFN_TILE)
    shp = w.shape[:axis] + (2, n, FFN_TILE) + w.shape[axis + 1:]
    return jnp.swapaxes(w.reshape(shp), axis, axis + 1).reshape(w.shape)


def _deinterleave(w, axis):
    n = w.shape[axis] // (2 * FFN_TILE)
    shp = w.shape[:axis] + (n, 2, FFN_TILE) + w.shape[axis + 1:]
    return jnp.swapaxes(w.reshape(shp), axis, axis + 1).reshape(w.shape)


def _ffn_up(xb, w13t, lead, *, name, ride=None):
    T, D = xb.shape
    F = w13t.shape[1] // 2
    tc = FFN_TILE
    tm = _tile(T, 2048, 16)

    def body(x_ref, w_ref, h_ref, a_ref):
        h = _dg(x_ref[...], w_ref[...], 1, 1)
        g, u = h[:, :tc], h[:, tc:]
        h_ref[...] = h.astype(h_ref.dtype)
        a_ref[...] = (g * jax.nn.sigmoid(g) * u).astype(a_ref.dtype)

    est = (tm * D + 2 * tc * D + 3 * tm * tc) * 2 + 3 * tm * tc * 4
    (h13, a), gathered = _host_call(
        body, name=name, grid=(T // tm, F // tc),
        in_specs=[pl.BlockSpec((tm, D), lambda i, j: (i, 0)),
                  pl.BlockSpec((None, 2 * tc, D), lambda i, j: (lead, j, 0))],
        out_specs=[pl.BlockSpec((tm, 2 * tc), lambda i, j: (i, j)),
                   pl.BlockSpec((tm, tc), lambda i, j: (i, j))],
        out_shape=[_out((T, 2 * F), BF16), _out((T, F), BF16)],
        args=[_hbm(xb), _hbm(w13t)], sem=("parallel", "parallel"), est=est, ride=ride)
    return h13, a, gathered


def _ffn_down_bwd(dyb, w2, lead, h13, *, name, ride=None):
    T, D = dyb.shape
    F = w2.shape[1]
    tc = FFN_TILE
    tm = _tile(T, 2048, 16)

    def body(dy_ref, w_ref, h_ref, dh_ref):
        d = _dg(dy_ref[...], w_ref[...], 1, 1)
        h = h_ref[...].astype(F32)
        g, u = h[:, :tc], h[:, tc:]
        sig = jax.nn.sigmoid(g)
        gs = g * sig
        dh_ref[...] = jnp.concatenate([d * u * (sig + gs * (1.0 - sig)), d * gs], axis=1).astype(dh_ref.dtype)

    est = (tm * D + tc * D + 4 * tm * tc) * 2 + 6 * tm * tc * 4
    (dh,), rode = _host_call(
        body, name=name, grid=(T // tm, F // tc),
        in_specs=[pl.BlockSpec((tm, D), lambda i, j: (i, 0)),
                  pl.BlockSpec((None, tc, D), lambda i, j: (lead, j, 0)),
                  pl.BlockSpec((tm, 2 * tc), lambda i, j: (i, j))],
        out_specs=[pl.BlockSpec((tm, 2 * tc), lambda i, j: (i, j))],
        out_shape=[_out((T, 2 * F), BF16)],
        args=[_hbm(dyb), _hbm(w2), _hbm(h13)], sem=("parallel", "parallel"), est=est, ride=ride)
    return dh, rode


def _pool_select(parts, pw):
    pg = pw // len(POOL_WINDOWS)
    grp = lax.broadcasted_iota(jnp.int32, parts[0].shape, 1) // pg
    out = parts[3]
    for g in (2, 1, 0):
        out = jnp.where(grp == g, parts[g], out)
    return out


def _pool_count(t0, rows, pw):
    pg = pw // len(POOL_WINDOWS)
    grp = lax.broadcasted_iota(jnp.int32, (rows, pw), 1) // pg
    win = jnp.where(grp == 0, POOL_WINDOWS[0],
                    jnp.where(grp == 1, POOL_WINDOWS[1],
                              jnp.where(grp == 2, POOL_WINDOWS[2], POOL_WINDOWS[3])))
    t = t0 + lax.broadcasted_iota(jnp.int32, (rows, pw), 0)
    return jnp.minimum(t + 1, win).astype(F32)


def _window_sums(ext, up):
    n = ext.shape[0]
    sums, cur, k = [], ext, 1
    for _ in POOL_WINDOWS:
        cur = cur + pltpu.roll(cur, (n - k) if up else k, axis=0)
        sums.append(cur)
        k *= 2
    return sums


def _pool_delta(u, halo, t0):
    tm, pw = u.shape
    ext = jnp.concatenate([halo, u], axis=0)
    sums = [s[POOL_HALO:, :] for s in _window_sums(ext, up=False)]
    return _pool_select(sums, pw) / _pool_count(t0, tm, pw) - u


def _pool_fwd(hin, wbd, scale, cat, *, name):
    T = hin.shape[0]
    pw = wbd.shape[0]
    tm = _tile(T, 256, POOL_HALO)
    per = tm // POOL_HALO

    def body(u_ref, halo_ref, w_ref, s_ref, cat_ref, y_ref):
        i = pl.program_id(0)
        halo = jnp.where(i > 0, halo_ref[...], 0.0)
        d = _pool_delta(u_ref[...], halo, i * tm)
        y_ref[...] = (_dg(d, w_ref[...], 1, 0) * s_ref[...]).astype(y_ref.dtype)

    return pl.pallas_call(
        body, name=name, grid=(T // tm,),
        in_specs=[pl.BlockSpec((tm, pw), lambda i: (i, 0)),
                  pl.BlockSpec((POOL_HALO, pw), lambda i: (jnp.maximum(i * per - 1, 0), 0)),
                  pl.BlockSpec((pw, pw), lambda i: (0, 0)),
                  pl.BlockSpec((1, pw), lambda i: (0, 0)),
                  ANY],
        out_specs=pl.BlockSpec((tm, pw), lambda i: (i, 0)),
        out_shape=_out(cat.shape, cat.dtype),
        input_output_aliases={4: 0},
        compiler_params=_params(("parallel",), 16 * tm * pw * 4),
    )(_hbm(hin), _hbm(hin), _hbm(wbd), _hbm(scale), _hbm(cat))


def _pool_bwd(hin, dcat, wbd, scale, *, name):
    T = hin.shape[0]
    pw = wbd.shape[0]
    tm = _tile(T, 256, POOL_HALO)
    per = tm // POOL_HALO
    nt = T // tm

    def body(u_ref, halo_ref, dy_ref, dyn_ref, w_ref, s_ref, du_ref, dw_ref, ds_ref):
        i = pl.program_id(0)

        @pl.when(i == 0)
        def _():
            dw_ref[...] = jnp.zeros_like(dw_ref)
            ds_ref[...] = jnp.zeros_like(ds_ref)

        halo = jnp.where(i > 0, halo_ref[...], 0.0)
        d = _pool_delta(u_ref[...], halo, i * tm)
        w = w_ref[...]
        sc = s_ref[...]
        dy = dy_ref[...]
        dyn = jnp.where(i < nt - 1, dyn_ref[...], 0.0)
        ds_ref[...] += jnp.sum(dy * _dg(d, w, 1, 0), axis=0, keepdims=True)
        dys = dy * sc
        dw_ref[...] += _dg(d, dys, 0, 0)
        dys_ext = jnp.concatenate([dys, dyn * sc], axis=0)
        dd_ext = _dg(dys_ext, w, 1, 1)
        ddp = dd_ext / _pool_count(i * tm, tm + POOL_HALO, pw)
        sums = [s[:tm, :] for s in _window_sums(ddp, up=True)]
        du_ref[...] = _pool_select(sums, pw) - dd_ext[:tm, :]

    return pl.pallas_call(
        body, name=name, grid=(nt,),
        in_specs=[pl.BlockSpec((tm, pw), lambda i: (i, 0)),
                  pl.BlockSpec((POOL_HALO, pw), lambda i: (jnp.maximum(i * per - 1, 0), 0)),
                  pl.BlockSpec((tm, pw), lambda i: (i, 0)),
                  pl.BlockSpec((POOL_HALO, pw), lambda i: (jnp.minimum((i + 1) * per, nt * per - 1), 0)),
                  pl.BlockSpec((pw, pw), lambda i: (0, 0)),
                  pl.BlockSpec((1, pw), lambda i: (0, 0))],
        out_specs=[pl.BlockSpec((tm, pw), lambda i: (i, 0)),
                   pl.BlockSpec((pw, pw), lambda i: (0, 0)),
                   pl.BlockSpec((1, pw), lambda i: (0, 0))],
        out_shape=[_out((T, pw), F32),
                   _out((pw, pw), F32),
                   _out((1, pw), F32)],
        compiler_params=_params(("arbitrary",), 24 * tm * pw * 4),
    )(_hbm(hin), _hbm(hin), _hbm(dcat), _hbm(dcat), _hbm(wbd), _hbm(scale))


def _rms(x, g):
    return x * lax.rsqrt(jnp.mean(x * x, axis=-1, keepdims=True) + RMS_EPS) * g


def _norms_fn(pw, h, gq, gkv):
    o1 = pw + Q_LORA
    o2 = o1 + KV_LORA
    return (_rms(_cols(h, pw, o1), gq), _rms(_cols(h, o1, o2), gkv), _cols(h, o2, h.shape[1]))


def _norms_bwd(hin, gq, gkv, dcq, dckv, dkpe, du, *, pw, name):
    tm = _tile(hin.shape[0], 256, 16)
    dinp = hin.shape[1]

    def fn(i, tv, pv):
        _, vjp = jax.vjp(functools.partial(_norms_fn, pw), tv[0], pv[0], pv[1])
        dh, dgq, dgkv = vjp((tv[1].astype(F32), tv[2].astype(F32), tv[3].astype(F32)))
        dh = jnp.concatenate([tv[4], dh[:, pw:]], axis=1)
        return (dh,), (dgq, dgkv)

    return _rowwise(fn, [hin, dcq, dckv, dkpe, du], [gq, gkv], [(dinp, BF16)],
                    [((1, Q_LORA), F32), ((1, KV_LORA), F32)], tm=tm, name=name)


def _heads_fn(H, qraw, kv, kpe, rc, rs1, rs2):
    half = QK_ROPE // 2
    scale = (QK_NOPE + QK_ROPE) ** -0.5

    def rope(blk):
        return blk * rc + _lane_roll(blk, -half) * rs1 + _lane_roll(blk, half) * rs2

    krot = rope(kpe)
    qs, ks, vs = [], [], []
    for h in range(H):
        lo = h * HEAD_PAD
        qs += [_cols(qraw, lo, lo + LANE) * scale, rope(_cols(qraw, lo + LANE, lo + HEAD_PAD)) * scale]
        ks += [_cols(kv, lo, lo + LANE), krot]
        vs += [_cols(kv, lo + LANE, lo + HEAD_PAD)]
    return jnp.concatenate(qs, axis=1), jnp.concatenate(ks, axis=1), jnp.concatenate(vs, axis=1)


def _heads_fwd(hin, gq, gkv, tabs, wuq, wukv, *, H, pw, name):
    tm = _tile(hin.shape[0], 256, 16)

    def fn(i, tv, pv):
        cqn, ckvn, kpe = _norms_fn(pw, tv[0], pv[0], pv[1])
        qraw = _dg(cqn, pv[2], 1, 1)
        kv = _dg(ckvn, pv[3], 1, 1)
        return (*_heads_fn(H, qraw, kv, kpe, *tv[1:]), cqn, ckvn), ()

    return _rowwise(fn, [hin, *tabs], [gq, gkv, wuq, wukv],
                    [(H * HEAD_PAD, BF16), (H * HEAD_PAD, BF16), (H * V_HEAD, BF16), (Q_LORA, BF16),
                     (KV_LORA, BF16)], tm=tm, name=name)


def _heads_bwd(dq, dk, dv, tabs, *, H, name):
    tm = _tile(dq.shape[0], 256, 16)

    def fn(i, tv, pv):
        z = jnp.zeros((tm, H * HEAD_PAD), F32)
        zk = jnp.zeros((tm, LANE), F32)
        rc, rs1, rs2 = tv[3], tv[4], tv[5]
        _, vjp = jax.vjp(lambda a, b, c: _heads_fn(H, a, b, c, rc, rs1, rs2), z, z, zk)
        return vjp((tv[0].astype(F32), tv[1].astype(F32), tv[2].astype(F32))), ()

    return _rowwise(fn, [dq, dk, dv, *tabs], [],
                    [(H * HEAD_PAD, BF16), (H * HEAD_PAD, BF16), (LANE, F32)], tm=tm, name=name)


def _diag_mask(rows, cols, row0):
    r = (row0 + lax.broadcasted_iota(jnp.int32, (rows, cols), 0)) // CHUNK
    c = lax.broadcasted_iota(jnp.int32, (rows, cols), 1) // CHUNK
    return r >= c


def _flash_fwd(qh, kh, vh, *, H, pw, name, ride=None):
    T = qh.shape[0]
    t = _tile(T, 512, CHUNK)
    off = pw // V_HEAD


    half = t

    def body(q_ref, k_ref, v_ref, o_ref, lse_ref):
        i = pl.program_id(1)
        q = q_ref[...]

        def update(carry, s, v):
            m, l, acc = carry
            mn = jnp.maximum(m, jnp.max(s, axis=1, keepdims=True))
            p = jnp.exp(s - mn)
            corr = jnp.exp(m - mn)
            return mn, corr * l + jnp.sum(p, axis=1, keepdims=True), corr * acc + _dg(p, v, 1, 0)

        def blk(j, carry):
            rows = pl.ds(pl.multiple_of(j * t, t), t)
            return update(carry, _dg(q, k_ref[rows, :], 1, 1), v_ref[rows, :])

        init = (jnp.full((t, 1), NEG_INF, F32), jnp.zeros((t, 1), F32), jnp.zeros((t, V_HEAD), F32))
        carry = lax.fori_loop(0, i, blk, init)
        done = []
        for r0 in range(0, t, half):
            keys = pl.ds(pl.multiple_of(i * t, t), r0 + half)
            s = _dg(q[r0:r0 + half, :], k_ref[keys, :], 1, 1)
            s = jnp.where(_diag_mask(half, r0 + half, r0), s, NEG_INF)
            done.append(update(tuple(c[r0:r0 + half] for c in carry), s, v_ref[keys, :]))
        m, l, acc = (jnp.concatenate(parts, axis=0) for parts in zip(*done))
        o_ref[...] = (acc / l).astype(o_ref.dtype)
        lse_ref[...] = jnp.broadcast_to(m + jnp.log(l), (t, V_HEAD))

    est = 2 * T * (HEAD_PAD + V_HEAD) * 2 + 8 * t * t * 4
    (o, lse), gathered = _host_call(
        body, name=name, grid=(H, T // t),
        in_specs=[pl.BlockSpec((t, HEAD_PAD), lambda h, i: (i, h)),
                  pl.BlockSpec((T, HEAD_PAD), lambda h, i: (0, h)),
                  pl.BlockSpec((T, V_HEAD), lambda h, i: (0, h))],
        out_specs=[pl.BlockSpec((t, V_HEAD), lambda h, i: (i, off + h)),
                   pl.BlockSpec((t, V_HEAD), lambda h, i: (i, h))],
        out_shape=[_out((T, pw + H * V_HEAD), BF16),
                   _out((T, H * V_HEAD), F32)],
        args=[_hbm(qh), _hbm(kh), _hbm(vh)], sem=("parallel", "parallel"), est=est, ride=ride)
    return o, lse, gathered


def _flash_bwd(qh, kh, vh, cat, dcat, lse, *, H, pw, name, ride=None):
    T = qh.shape[0]
    t = _tile(T, 512, CHUNK)
    nb = T // t
    off = pw // V_HEAD
    half = t

    def body(q_ref, k_ref, v_ref, o_ref, do_ref, lse_ref, dq_out_ref, dk_ref, dv_ref, dq_ref):
        j = pl.program_id(1)

        @pl.when(j == 0)
        def _():
            dq_ref[...] = jnp.zeros_like(dq_ref)

        kj = k_ref[...]
        vj = v_ref[...]

        def pair(rows, kx, vx, mask):
            qi = q_ref[rows, :]
            doi = do_ref[rows, :]
            oi = o_ref[rows, :].astype(F32)
            lsei = lse_ref[rows, :][:, :1]
            s = _dg(qi, kx, 1, 1)
            if mask is not None:
                s = jnp.where(mask, s, NEG_INF)
            p = jnp.exp(s - lsei)
            dp = _dg(doi, vx, 1, 1)
            di = jnp.sum(doi * oi, axis=1, keepdims=True)
            ds = p * (dp - di)
            dq_ref[rows, :] += _dg(ds, kx, 1, 0)
            return _dg(ds, qi, 0, 0), _dg(p, doi, 0, 0)

        def blk(i, carry):
            dk, dv = pair(pl.ds(pl.multiple_of(i * t, t), t), kj, vj, None)
            return carry[0] + dk, carry[1] + dv

        dk, dv = jnp.zeros((t, HEAD_PAD), F32), jnp.zeros((t, V_HEAD), F32)
        for r0 in range(0, t, half):
            n = r0 + half
            dkp, dvp = pair(pl.ds(pl.multiple_of(j * t + r0, half), half), kj[:n], vj[:n],
                            _diag_mask(half, n, r0))
            if n < t:
                dkp = jnp.concatenate([dkp, jnp.zeros((t - n, HEAD_PAD), F32)], axis=0)
                dvp = jnp.concatenate([dvp, jnp.zeros((t - n, V_HEAD), F32)], axis=0)
            dk, dv = dk + dkp, dv + dvp
        dk, dv = lax.fori_loop(j + 1, nb, blk, (dk, dv))
        dk_ref[...] = dk.astype(dk_ref.dtype)
        dv_ref[...] = dv.astype(dv_ref.dtype)

        @pl.when(j == nb - 1)
        def _():
            dq_out_ref[...] = dq_ref[...].astype(dq_out_ref.dtype)

    est = T * (HEAD_PAD * 2 + V_HEAD * 2 + V_HEAD * 4 + V_HEAD * 4 + HEAD_PAD * 4) + 10 * t * t * 4
    (dq, dk, dv), gathered = _host_call(
        body, name=name, grid=(H, nb),
        in_specs=[pl.BlockSpec((T, HEAD_PAD), lambda h, j: (0, h)),
                  pl.BlockSpec((t, HEAD_PAD), lambda h, j: (j, h)),
                  pl.BlockSpec((t, V_HEAD), lambda h, j: (j, h)),
                  pl.BlockSpec((T, V_HEAD), lambda h, j: (0, off + h)),
                  pl.BlockSpec((T, V_HEAD), lambda h, j: (0, off + h)),
                  pl.BlockSpec((T, V_HEAD), lambda h, j: (0, h))],
        out_specs=[pl.BlockSpec((T, HEAD_PAD), lambda h, j: (0, h)),
                   pl.BlockSpec((t, HEAD_PAD), lambda h, j: (j, h)),
                   pl.BlockSpec((t, V_HEAD), lambda h, j: (j, h))],
        out_shape=[_out((T, H * HEAD_PAD), BF16),
                   _out((T, H * HEAD_PAD), BF16),
                   _out((T, H * V_HEAD), BF16)],
        scratch=[pltpu.VMEM((T, HEAD_PAD), F32)],
        args=[_hbm(v) for v in (qh, kh, vh, cat, dcat, lse)], sem=("arbitrary", "arbitrary"), est=est,
        ride=ride)
    return dq, dk, dv, gathered


def _mem_fn(q, k, v):
    hd = q.shape[1] // MEM_HEADS
    outs = []
    for h in range(MEM_HEADS):
        lo, hi = h * hd, (h + 1) * hd
        s = _bdot_nt(_cols(q, lo, hi), _cols(k, lo, hi)) * hd ** -0.5
        e = jnp.exp(s - lax.stop_gradient(jnp.max(s, axis=1, keepdims=True)))
        p = e / jnp.sum(e, axis=1, keepdims=True)
        outs.append(_bdot_nn(p, _cols(v, lo, hi)))
    return jnp.concatenate(outs, axis=1)


def _mem_fwd(q, k, v, *, name):
    T, D = q.shape
    tm = _tile(T, 256, 16)

    def fn(i, tv, pv):
        return (_mem_fn(tv[0], pv[0], pv[1]),), ()

    return _rowwise(fn, [q], [k, v], [(D, BF16)], tm=tm, name=name)[0]


def _mem_bwd(q, k, v, do, *, name):
    T, D = q.shape
    tm = _tile(T, 256, 16)

    def fn(i, tv, pv):
        _, vjp = jax.vjp(_mem_fn, tv[0], pv[0], pv[1])
        dq, dk, dv = vjp(tv[1].astype(F32))
        return (dq,), (dk, dv)

    return _rowwise(fn, [q, do], [k, v], [(D, BF16)], [(k.shape, F32), (v.shape, F32)], tm=tm, name=name)


def _adamw(w, g, m, v, *, name):
    shape = w.shape
    if w.ndim != 3:
        lead3 = (1, math.prod(shape[:-1]), shape[-1])
        return [o.reshape(shape) for o in _adamw(*[a.reshape(lead3) for a in (w, g, m, v)], name=name)]
    Lw, R, C = shape
    tr = _tile(R, 512, 8)
    b1c = 1.0 - ADAM_B1 ** ADAM_STEP
    b2c = 1.0 - ADAM_B2 ** ADAM_STEP

    def body(w_ref, g_ref, m_ref, v_ref, d_ref, mo_ref, vo_ref):
        gg = g_ref[...]
        mn = ADAM_B1 * m_ref[...] + (1.0 - ADAM_B1) * gg
        vn = ADAM_B2 * v_ref[...] + (1.0 - ADAM_B2) * (gg * gg)
        d_ref[...] = -ADAM_LR * ((mn / b1c) / (jnp.sqrt(vn / b2c) + ADAM_EPS) + ADAM_WD * w_ref[...])
        mo_ref[...] = mn
        vo_ref[...] = vn

    spec = pl.BlockSpec((None, tr, C), lambda l, i: (l, i, 0))
    return pl.pallas_call(
        body, name=name, grid=(Lw, R // tr),
        in_specs=[spec] * 4, out_specs=[spec] * 3,
        out_shape=[_out(shape, F32)] * 3,
        compiler_params=_params(("parallel", "parallel"), 7 * tr * C * 4),
    )(*[_hbm(a) for a in (w, g, m, v)])


def _pair_sum(core, gs, landed, offs, *, name):
    n = len(gs)
    _, R, C = landed.shape
    rows = [g.shape[0] // N_DEV for g in gs]

    def body(core_ref, *refs):
        g_refs, l_ref, o_ref = refs[:n], refs[n], refs[n + 1]
        for g_ref, off, r in zip(g_refs, offs, rows):
            o_ref[off:off + r, :] = (g_ref[...].astype(F32) + l_ref[off:off + r, :].astype(F32)).astype(o_ref.dtype)

    slab = pl.BlockSpec((None, R, C), lambda p, core_ref: (p, 0, 0))
    own = [pl.BlockSpec((r, C), lambda p, core_ref: (2 * p + core_ref[0], 0)) for r in rows]
    return pl.pallas_call(
        body, name=name,
        grid_spec=pltpu.PrefetchScalarGridSpec(
            num_scalar_prefetch=1, grid=(4,), in_specs=own + [slab], out_specs=slab),
        out_shape=_out(landed.shape, landed.dtype),
        input_output_aliases={n + 1: 0},
        compiler_params=_params(("arbitrary",), 3 * R * C * 2 + R * C * 8),
    )(core, *[_hbm(g) for g in gs], _hbm(landed))


def _quad_sum(chip, part, gathered, used, *, name):
    C = part.shape[2]
    R = used
    tr = _tile(R, 256, 16)

    def body(chip_ref, own_ref, a_ref, b_ref, c_ref, o_ref):
        o_ref[...] = ((own_ref[...].astype(F32) + a_ref[...].astype(F32)) + b_ref[...].astype(F32)) \
            + c_ref[...].astype(F32)

    def other(k):
        return pl.BlockSpec((None, tr, C), lambda i, chip_ref: (chip_ref[0] ^ k, i, 0))

    return pl.pallas_call(
        body, name=name,
        grid_spec=pltpu.PrefetchScalarGridSpec(
            num_scalar_prefetch=1, grid=(R // tr,),
            in_specs=[pl.BlockSpec((None, tr, C), lambda i, chip_ref: (chip_ref[0], i, 0)),
                      other(1), other(2), other(3)],
            out_specs=pl.BlockSpec((tr, C), lambda i, chip_ref: (i, 0))),
        out_shape=_out((R, C), F32),
        compiler_params=_params(("arbitrary",), 8 * tr * C * 4),
    )(chip, _hbm(part), _hbm(gathered), _hbm(gathered), _hbm(gathered))


def _place():
    x, y, c = lax.axis_index("x"), lax.axis_index("y"), lax.axis_index("c")
    return x, y, c


ANY = pl.BlockSpec(memory_space=pl.ANY)


class _Gather:
    def __init__(self, shards):
        self.shards = list(shards)
        self.n = len(self.shards)
        self.out_shape = [_out((s.shape[0], N_DEV * s.shape[1], s.shape[2]), s.dtype)
                          for s in self.shards]
        self.scratch = [pltpu.SemaphoreType.DMA((7 * self.n,)), pltpu.SemaphoreType.DMA((7 * self.n,)),
                        pltpu.SemaphoreType.DMA((self.n,))]
        self.operands = [_hbm(s) for s in self.shards]

    def _bind(self, refs):
        n = self.n
        ins, outs = refs[:n], refs[n:2 * n]
        send_sems, recv_sems, local_sems = refs[2 * n:]
        x, y, c = _place()
        me, sib = (x, y, c), (x, y, 1 - c)
        chips = [(1 - x, y), (x, 1 - y), (1 - x, 1 - y)]

        def rows(w, p):
            r = self.shards[w].shape[1]
            idx = 4 * p[0] + 2 * p[1] + p[2]
            return outs[w].at[:, pl.ds(pl.multiple_of(idx * r, 8), r), :]

        def copy(w, k, block, to, src=None):
            return pltpu.make_async_remote_copy(
                src_ref=rows(w, block) if src is None else src, dst_ref=rows(w, block),
                send_sem=send_sems.at[w * 7 + k], recv_sem=recv_sems.at[w * 7 + k],
                device_id=to, device_id_type=MESH)

        def mine():
            return [pltpu.make_async_copy(ins[w], rows(w, me), local_sems.at[w]) for w in range(n)]

        def first():
            out = []
            for w in range(n):
                out.append(copy(w, 0, me, sib, src=ins[w]))
                out += [copy(w, 1 + j, me, (*chip, c), src=ins[w]) for j, chip in enumerate(chips)]
            return out

        def passed():
            return [copy(w, 4 + j, (*chip, c), sib) for j, chip in enumerate(chips) for w in range(n)]

        def landed():
            return [copy(w, 1 + j, (*chip, c), me) for j, chip in enumerate(chips) for w in range(n)]

        def last():
            out = []
            for w in range(n):
                out.append(copy(w, 0, sib, me))
                out += [copy(w, 4 + j, (*chip, 1 - c), me) for j, chip in enumerate(chips)]
            return out

        return mine, first, landed, passed, last

    def start(self, refs):
        mine, first, _, _, _ = self._bind(refs)
        for cp in mine() + first():
            cp.start()

    def forward(self, refs):
        _, _, landed, passed, _ = self._bind(refs)
        for arrived, fwd in zip(landed(), passed()):
            arrived.wait_recv()
            fwd.start()

    def finish(self, refs):
        mine, first, _, passed, last = self._bind(refs)
        for cp in last():
            cp.wait_recv()
        for cp in first() + passed():
            cp.wait_send()
        for cp in mine():
            cp.wait()


class _ChipExchange:
    def __init__(self, parts, used):
        self.ncl = len(parts)
        self.used = list(used)
        self.out_shape = [_out(p.shape, p.dtype) for p in parts]
        self.scratch = [pltpu.SemaphoreType.DMA((3 * self.ncl,)), pltpu.SemaphoreType.DMA((3 * self.ncl,))]
        self.operands = [_hbm(p) for p in parts]
        self.n = self.ncl

    def _bind(self, refs):
        ncl = self.ncl
        ins, outs = refs[:ncl], refs[ncl:2 * ncl]
        send_sems, recv_sems = refs[2 * ncl:]
        x, y, c = _place()
        chips = [(1 - x, y), (x, 1 - y), (1 - x, 1 - y)]
        here = 2 * x + y

        def copies(outgoing):
            out = []
            for k in range(ncl):
                rows = pl.ds(0, self.used[k])
                for j, (cx, cy) in enumerate(chips):
                    there = 2 * cx + cy
                    src, dst = (there, here) if outgoing else (here, there)
                    out.append(pltpu.make_async_remote_copy(
                        src_ref=ins[k].at[src, rows, :], dst_ref=outs[k].at[dst, rows, :],
                        send_sem=send_sems.at[3 * k + j], recv_sem=recv_sems.at[3 * k + j],
                        device_id=(cx, cy, c), device_id_type=MESH))
            return out

        return copies

    def start(self, refs):
        for cp in self._bind(refs)(True):
            cp.start()

    def forward(self, refs):
        pass

    def finish(self, refs):
        copies = self._bind(refs)
        for cp in copies(False):
            cp.wait_recv()
        for cp in copies(True):
            cp.wait_send()


class _Both:
    def __init__(self, members):
        self.members = list(members)
        self.n = sum(m.n for m in self.members)
        self.out_shape = [s for m in self.members for s in m.out_shape]
        self.scratch = [s for m in self.members for s in m.scratch]
        self.operands = [o for m in self.members for o in m.operands]

    def split(self, arrays):
        out, a = [], 0
        for m in self.members:
            out.append(list(arrays[a:a + m.n]))
            a += m.n
        return out

    def _refs(self, refs):
        ins, outs = self.split(refs[:self.n]), self.split(refs[self.n:2 * self.n])
        scr, b = [], 2 * self.n
        for m in self.members:
            scr.append(list(refs[b:b + len(m.scratch)]))
            b += len(m.scratch)
        return [(*i, *o, *s) for i, o, s in zip(ins, outs, scr)]

    def start(self, refs):
        for m, r in zip(self.members, self._refs(refs)):
            m.start(r)

    def forward(self, refs):
        for m, r in zip(self.members, self._refs(refs)):
            m.forward(r)

    def finish(self, refs):
        for m, r in zip(self.members, self._refs(refs)):
            m.finish(r)


def _exchange_alone(ex, *, name):
    def body(*refs):
        ex.start(refs)
        ex.forward(refs)
        ex.finish(refs)

    return pl.pallas_call(
        body, name=name, in_specs=[ANY] * ex.n, out_specs=[ANY] * ex.n,
        out_shape=ex.out_shape, scratch_shapes=ex.scratch,
    )(*ex.operands)


def _host_call(body, *, name, grid, in_specs, out_specs, out_shape, args, sem, est, ride=None, scratch=()):
    scratch = list(scratch)
    if ride is None:
        outs = pl.pallas_call(body, name=name, grid=grid, in_specs=in_specs, out_specs=out_specs,
                              out_shape=out_shape, scratch_shapes=scratch,
                              compiler_params=_params(sem, est))(*args)
        return list(outs), []
    n_in, n_out, n, n_scr = len(in_specs), len(out_specs), ride.n, len(scratch)

    def full(*refs):
        ins, rin = refs[:n_in], refs[n_in:n_in + n]
        outs, rout = refs[n_in + n:n_in + n + n_out], refs[n_in + n + n_out:n_in + 2 * n + n_out]
        own = refs[n_in + 2 * n + n_out:n_in + 2 * n + n_out + n_scr]
        rrefs = (*rin, *rout, *refs[n_in + 2 * n + n_out + n_scr:])
        step, total = _ride(ride, rrefs, grid)
        body(*ins, *outs, *own)
        _ride_end(ride, rrefs, step, total)

    outs = pl.pallas_call(
        full, name=name, grid=grid,
        in_specs=list(in_specs) + [ANY] * n, out_specs=list(out_specs) + [ANY] * n,
        out_shape=list(out_shape) + ride.out_shape, scratch_shapes=scratch + ride.scratch,
        compiler_params=_params(("arbitrary",) * len(grid), est),
    )(*args, *ride.operands)
    return list(outs[:n_out]), list(outs[n_out:])


def _ride(ex, refs, grid):
    total = math.prod(grid)
    step = pl.program_id(0)
    for axis in range(1, len(grid)):
        step = step * grid[axis] + pl.program_id(axis)
    pl.when(step == 0)(lambda: ex.start(refs))
    return step, total


def _ride_end(ex, refs, step, total):
    pl.when(step == (3 * total) // 4)(lambda: ex.forward(refs))
    pl.when(step == total - 1)(lambda: ex.finish(refs))


def _class_layout(grads, classes):
    used = [0] * len(set(classes))
    offs = []
    for g, cl in zip(grads, classes):
        offs.append(used[cl])
        used[cl] += g.shape[0] // N_DEV
    return offs, used


def _rs_to_sibling(grads, classes, *, name):
    n = len(grads)
    offs, used = _class_layout(grads, classes)
    heights = used
    ncl = len(heights)
    cols = [next(g.shape[1] for g, cl in zip(grads, classes) if cl == k) for k in range(ncl)]

    def body(*refs):
        gs, land = refs[:n], refs[n:n + ncl]
        send_sems, recv_sems = refs[n + ncl:]
        x, y, c = _place()
        sib = (x, y, 1 - c)
        for p in range(4):
            for w in range(n):
                r = grads[w].shape[0] // N_DEV
                cl = classes[w]
                there = gs[w].at[pl.ds(pl.multiple_of((2 * p + 1 - c) * r, 8), r), :]
                pltpu.make_async_remote_copy(
                    src_ref=there, dst_ref=land[cl].at[p, pl.ds(offs[w], r), :],
                    send_sem=send_sems.at[cl * 4 + p], recv_sem=recv_sems.at[cl * 4 + p],
                    device_id=sib, device_id_type=MESH).start()
        for cl in range(ncl):
            for p in range(4):
                rows_used = land[cl].at[p, pl.ds(0, used[cl]), :]
                slab = pltpu.make_async_remote_copy(
                    src_ref=rows_used, dst_ref=rows_used,
                    send_sem=send_sems.at[cl * 4 + p], recv_sem=recv_sems.at[cl * 4 + p],
                    device_id=sib, device_id_type=MESH)
                slab.wait_send()
                slab.wait_recv()

    return pl.pallas_call(
        body, name=name,
        in_specs=[ANY] * n, out_specs=[ANY] * ncl,
        out_shape=[_out((4, heights[k], cols[k]), BF16) for k in range(ncl)],
        scratch_shapes=[pltpu.SemaphoreType.DMA((4 * ncl,))] * 2,
    )(*[_hbm(g) for g in grads])


def _sum_devices(g, *, name):
    R = g.shape[1]

    def body(g_ref, o_ref):
        acc = g_ref[0]
        for d in range(1, N_DEV):
            acc = acc + g_ref[d]
        o_ref[...] = acc

    vm = pl.BlockSpec(memory_space=pltpu.VMEM)
    return pl.pallas_call(
        body, name=name, in_specs=[vm], out_specs=vm,
        out_shape=jax.ShapeDtypeStruct((R, LANE), F32),
        compiler_params=pltpu.CompilerParams(vmem_limit_bytes=VMEM_FLOOR),
    )(g)


def _rope_tables(positions):
    half = QK_ROPE // 2
    inv_freq = ROPE_BASE ** (-jnp.arange(half, dtype=F32) / half)
    ang = positions.astype(F32)[:, None] * inv_freq
    cos, sin = jnp.cos(ang), jnp.sin(ang)
    z = jnp.zeros_like(cos)
    z2 = jnp.zeros((positions.shape[0], LANE - QK_ROPE), F32)
    rc = jnp.concatenate([cos, cos, z2], axis=1)
    rs1 = jnp.concatenate([-sin, z, z2], axis=1)
    rs2 = jnp.concatenate([z, sin, z2], axis=1)
    return rc, rs1, rs2


def _block_diag(pool_w):
    G, pg, _ = pool_w.shape
    out = jnp.zeros((G * pg, G * pg), pool_w.dtype)
    for g in range(G):
        out = lax.dynamic_update_slice(out, pool_w[g], (g * pg, g * pg))
    return out


def kernel(x, mem, positions, ln_g, ln_b, ffn1_w13, ffn1_w2, w_in, pool_w, pool_scale, q_norm_g, w_uq, kv_norm_g, w_ukv, w_out, mem_wq, mem_wkv, mem_wo, ffn2_w13, ffn2_w2, loss_target, m_ln_g, m_ln_b, m_ffn1_w13, m_ffn1_w2, m_w_in, m_pool_w, m_pool_scale, m_q_norm_g, m_w_uq, m_kv_norm_g, m_w_ukv, m_w_out, m_mem_wq, m_mem_wkv, m_mem_wo, m_ffn2_w13, m_ffn2_w2, v_ln_g, v_ln_b, v_ffn1_w13, v_ffn1_w2, v_w_in, v_pool_w, v_pool_scale, v_q_norm_g, v_w_uq, v_kv_norm_g, v_w_ukv, v_w_out, v_mem_wq, v_mem_wkv, v_mem_wo, v_ffn2_w13, v_ffn2_w2):
    L = ln_g.shape[0]
    T, D = x.shape[1], x.shape[2]
    F = ffn1_w2.shape[1] * N_DEV
    PW = D // 4
    H = (D - PW) // V_HEAD
    DIN = w_in.shape[2]
    DINP = PW + Q_LORA + KV_LORA + LANE
    QW = QK_NOPE + QK_ROPE
    alpha = (2 * L) ** 0.25
    x2d = x.reshape(T, D)
    memb = mem.reshape(mem.shape[1], D).astype(BF16)
    target = loss_target.reshape(T, D)
    tabs = _rope_tables(positions.reshape(T))

    def shards_of(l):
        return dict(
            w13a=ffn1_w13[l].T[None].astype(BF16),
            w13b=ffn2_w13[l].T[None].astype(BF16),
            w2a=ffn1_w2[l][None].astype(BF16),
            w2b=ffn2_w2[l][None].astype(BF16),
            wsq=jnp.stack([w_out[l], mem_wq[l], mem_wo[l]]).astype(BF16),
            wkvT=mem_wkv[l].T[None].astype(BF16),
            winp=jnp.pad(w_in[l], ((0, 0), (0, DINP - DIN)))[None].astype(BF16),
            wuqT=w_uq[l].T[None].astype(BF16),
            wukvT=w_ukv[l].T[None].astype(BF16),
        )

    SMALL = ("winp", "wuqT", "wukvT")
    shards = [shards_of(l) for l in range(L)]
    W = [dict() for _ in range(L)]

    def rider(spec):
        return _Gather([shards[l][n] for l, n in spec]) if spec else None

    def arrived(spec, arrays):
        for (l, n), a in zip(spec, arrays):
            if n in ("w13a", "w13b"):
                a = _interleave(a, 1)
            elif n == "wuqT":
                a = jnp.pad(a.reshape(H, QW, Q_LORA), ((0, 0), (0, HEAD_PAD - QW), (0, 0)))
                a = a.reshape(1, H * HEAD_PAD, Q_LORA)
            elif n == "ln":
                a = jnp.moveaxis(a.reshape(N_DEV, 2, L, 4, D // N_DEV), 0, 3).reshape(2, L, 4, D)
                LN["g"], LN["b"] = a[0], a[1]
            W[l][n] = a

    LN = {}
    shards[0]["ln"] = jnp.concatenate([ln_g.reshape(1, 4 * L, -1), ln_b.reshape(1, 4 * L, -1)], axis=1)
    spec0 = [(0, "w13a")]
    arrived(spec0, _exchange_alone(rider(spec0), name="ag_first"))
    wbd = [_block_diag(pool_w[l]).astype(BF16) for l in range(L)]

    def ffn_fwd(l, which, xres, xb, k, spec):
        ab = "ab"[which]
        h13, a, rode = _ffn_up(xb, W[l]["w13" + ab], 0, name=f"l{l}_ffn{which}_up", ride=rider(spec))
        arrived(spec, rode)
        y, xo, xob = _mm_ln(a, W[l]["w2" + ab], 0, xres, LN["g"][l,k:k + 1], LN["b"][l,k:k + 1], alpha=alpha, s=0.5,
                            name=f"l{l}_ffn{which}_y_ln{k}")
        return dict(xres=xres, xb=xb, h13=h13, a=a, y=y), xo, xob

    saved = []
    xres, xb = x2d, x2d.astype(BF16)
    for l in range(L):
        sv = {}
        more = l + 1 < L
        Wl = W[l]
        spec = ([(0, "w2a"), (0, "ln"), *[(0, n) for n in SMALL], (0, "wkvT")] if l == 0
                else [(l, "wsq"), (l, "wkvT")])
        sv["ffn1"], x1, x1b = ffn_fwd(l, 0, xres, xb, 0, spec)
        hin = _mm(x1b, Wl["winp"], lead=0, name=f"l{l}_hin")
        pscale = pool_scale[l].reshape(1, PW)
        gq, gkv = q_norm_g[l].reshape(1, Q_LORA), kv_norm_g[l].reshape(1, KV_LORA)
        qh, kh, vh, cqn, ckvn = _heads_fwd(hin, gq, gkv, tabs, Wl["wuqT"][0], Wl["wukvT"][0], H=H, pw=PW,
                                           name=f"l{l}_heads")
        spec = [(l, "w13b"), (l, "w2b")] + ([(0, "wsq")] if l == 0 else []) + ([(l + 1, "w13a")] if more else [])
        cat, lse, rode = _flash_fwd(qh, kh, vh, H=H, pw=PW, name=f"l{l}_flash", ride=rider(spec))
        arrived(spec, rode)
        cat = _pool_fwd(hin, wbd[l], pscale, cat, name=f"l{l}_pool")
        ymix, x2, x2b = _mm_ln(cat, Wl["wsq"], 0, x1, LN["g"][l,1:2], LN["b"][l,1:2], alpha=alpha, s=1.0,
                               name=f"l{l}_ymix_ln1")
        qm = _mm(x2b, Wl["wsq"], lead=1, out_dtype=BF16, name=f"l{l}_qm")
        kvm = _mm(memb, Wl["wkvT"], lead=0, tb=True, name=f"l{l}_kvm")
        km, vm = kvm[:, :D], kvm[:, D:]
        om = _mem_fwd(qm, km, vm, name=f"l{l}_memattn")
        ymem, x3, x3b = _mm_ln(om, Wl["wsq"], 2, x2, LN["g"][l,2:3], LN["b"][l,2:3], alpha=alpha, s=1.0,
                               name=f"l{l}_ymem_ln2")
        spec = [(l + 1, n) for n in ("w2a", *SMALL)] if more else []
        sv["ffn2"], x4, x4b = ffn_fwd(l, 1, x3, x3b, 3, spec)
        sv.update(x1=x1, x1b=x1b, hin=hin, pscale=pscale, gq=gq, gkv=gkv, cqn=cqn, ckvn=ckvn,
                  qh=qh, kh=kh, vh=vh, lse=lse, cat=cat, ymix=ymix, x2=x2, x2b=x2b, qm=qm, km=km, vm=vm,
                  om=om, ymem=ymem)
        saved.append(sv)
        xres, xb = x4, x4b


    gW = {}
    gS = {}

    def ln_of(l, k):
        sv = saved[l]
        x, y, s = {0: (sv["ffn1"]["xres"], sv["ffn1"]["y"], 0.5), 1: (sv["x1"], sv["ymix"], 1.0),
                   2: (sv["x2"], sv["ymem"], 1.0), 3: (sv["ffn2"]["xres"], sv["ffn2"]["y"], 0.5)}[k]
        return x, y, LN["g"][l, k:k + 1], s

    def dx_through_ln(a, b, lead, tb, add, into, name, xin=None, wkey=None):
        x, y, g, s = ln_of(*into)
        out = _mm_ln_bwd(a, b, lead, tb, add, x, y, g, alpha=alpha, s=s, name=name, xin=xin)
        gS[("ln_g", *into)], gS[("ln_b", *into)] = out[2], out[3]
        if xin is not None:
            gW[wkey] = out[4]
        return out[0], out[1]

    def ffn_bwd(l, which, sv, dxres, dyb, ride, into):
        tag = f"l{l}_ffn{which}"
        gW[("w2", which, l)] = _mm(sv["a"], dyb, ta=True, out_dtype=BF16, name=f"{tag}_dw2", tn=D)
        dh, rode = _ffn_down_bwd(dyb, W[l]["w2" + "ab"[which]], 0, sv["h13"], name=f"{tag}_dh", ride=ride)
        dw13 = _mm(dh, sv["xb"], ta=True, out_dtype=BF16, name=f"{tag}_dw13", tn=D)
        gW[("w13", which, l)] = _deinterleave(dw13, 0)
        w13 = W[l]["w13" + "ab"[which]]
        if into is not None:
            return dx_through_ln(dh, w13, 0, False, dxres, into, f"{tag}_dx"), rode
        last = rs_first_level(l, "c")
        dxn, got = _mm(dh, w13, lead=0, add=dxres, name=f"{tag}_dx", tn=D, ride=last["ex"])
        rs_last_level(last, got)
        return dxn, rode

    core = lax.axis_index("c").astype(jnp.int32).reshape(1)
    chip = (2 * lax.axis_index("x") + lax.axis_index("y")).astype(jnp.int32).reshape(1)
    gsh = {}

    def rs_first_level(l, group):
        keys, classes = {
            "a": ([("w13", 1, l), ("w2", 1, l), ("mem_wkv", l), ("mem_wq", l), ("mem_wo", l)], [0] * 5),
            "b": ([("w_out", l), ("w_in", l), ("w_uq", l), ("w_ukv", l)], [0, 1, 2, 3]),
            "c": ([("w13", 0, l), ("w2", 0, l)], [0, 0]),
        }[group]
        tag = f"l{l}{group}"
        garrs = []
        for key in keys:
            g = gW[key]
            if key[0] == "w_uq":
                g = g.reshape(H, HEAD_PAD, Q_LORA)[:, :QW, :].reshape(H * QW, Q_LORA)
            garrs.append(g)
        offs, used = _class_layout(garrs, classes)
        parts = list(_rs_to_sibling(garrs, classes, name=f"{tag}_rs_sibling"))
        for cl in range(len(parts)):
            mine = [w for w, c in enumerate(classes) if c == cl]
            parts[cl] = _pair_sum(core, [garrs[w] for w in mine], parts[cl], [offs[w] for w in mine],
                                  name=f"{tag}_rs_pair_sum{cl}")
        return dict(tag=tag, keys=keys, garrs=garrs, classes=classes, offs=offs, used=used, parts=parts,
                    ex=_ChipExchange(parts, used))

    def rs_last_level(st, gathered):
        sums = [_quad_sum(chip, p, a, u, name=f"{st['tag']}_rs_quad_sum{k}")
                for k, (p, a, u) in enumerate(zip(st["parts"], gathered, st["used"]))]
        for key, g, cl, off in zip(st["keys"], st["garrs"], st["classes"], st["offs"]):
            gsh[key] = sums[cl][off:off + g.shape[0] // N_DEV, :]

    top = (L - 1, 3)
    x_top, y_top, g_top, s_top = ln_of(*top)
    dxres, dyb, gS[("ln_g", *top)], gS[("ln_b", *top)], loss_blk = _loss_ln_bwd(
        x_top, y_top, g_top, xres, target, alpha=alpha, s=s_top, name="loss_ln_top_bwd")
    above = None
    for l in reversed(range(L)):
        sv = saved[l]
        Wl = W[l]
        (dxres, dyb), _ = ffn_bwd(l, 1, sv["ffn2"], dxres, dyb, None, (l, 2))
        dom, gW[("mem_wo", l)] = _mm_pair(dyb, Wl["wsq"], 2, sv["om"], dx_dtype=BF16, name=f"l{l}_dom_dwo")
        dqm, dkm, dvm = _mem_bwd(sv["qm"], sv["km"], sv["vm"], dom, name=f"l{l}_memattn_bwd")
        dxres, dyb = dx_through_ln(dqm, Wl["wsq"], 1, True, dxres, (l, 1), f"l{l}_dx2_dwq",
                                   xin=sv["x2b"], wkey=("mem_wq", l))
        dkvm = jnp.concatenate([dkm, dvm], axis=1).astype(BF16)
        gW[("mem_wkv", l)] = _mm(dkvm, memb, ta=True, out_dtype=BF16, name=f"l{l}_dwkv", tn=D)
        dcat, gW[("w_out", l)] = _mm_pair(dyb, Wl["wsq"], 0, sv["cat"], dx_dtype=F32, name=f"l{l}_dcat_dwout")
        riding = [rs_first_level(l, "a")] + ([above] if above else [])
        both = _Both([st["ex"] for st in riding])
        dqh, dkh, dvh, rode = _flash_bwd(sv["qh"], sv["kh"], sv["vh"], sv["cat"], dcat, sv["lse"], H=H, pw=PW,
                                         name=f"l{l}_flash_bwd", ride=both)
        for st, got in zip(riding, both.split(rode)):
            rs_last_level(st, got)
        dqraw, dkv, dkpe = _heads_bwd(dqh, dkh, dvh, tabs, H=H, name=f"l{l}_heads_bwd")
        dcq, gW[("w_uq", l)] = _mm_pair(dqraw, Wl["wuqT"], 0, sv["cqn"], dx_dtype=F32, wt=True,
                                        name=f"l{l}_dcq_dwuq")
        dckv, gW[("w_ukv", l)] = _mm_pair(dkv, Wl["wukvT"], 0, sv["ckvn"], dx_dtype=F32, wt=True,
                                          name=f"l{l}_dckv_dwukv")
        du, dwbd, dps = _pool_bwd(sv["hin"], dcat, wbd[l], sv["pscale"], name=f"l{l}_pool_bwd")
        dhin, dgq, dgkv = _norms_bwd(sv["hin"], sv["gq"], sv["gkv"], dcq, dckv, dkpe, du, pw=PW,
                                     name=f"l{l}_norms_bwd")
        pg = PW // len(POOL_WINDOWS)
        gS[("pool_w", l)] = jnp.stack([dwbd[g * pg:(g + 1) * pg, g * pg:(g + 1) * pg]
                                       for g in range(len(POOL_WINDOWS))])
        gS[("pool_scale", l)], gS[("q_norm_g", l)], gS[("kv_norm_g", l)] = dps, dgq, dgkv
        dxres, dyb = dx_through_ln(dhin, Wl["winp"], 0, True, dxres, (l, 0), f"l{l}_dx1_dwin",
                                   xin=sv["x1b"], wkey=("w_in", l))
        heads = rs_first_level(l, "b")
        riding = [heads["ex"]]
        if l == 0:
            small_keys = []
            for ll in range(L):
                small_keys += [("pool_w", ll), ("pool_scale", ll), ("q_norm_g", ll), ("kv_norm_g", ll)]
                small_keys += [("ln_g", ll, k) for k in range(4)] + [("ln_b", ll, k) for k in range(4)]
            flat = jnp.concatenate([loss_blk[0, :1]] + [gS[k].reshape(-1) for k in small_keys])
            n_small = flat.shape[0]
            rows = -(-n_small // (8 * LANE)) * 8
            flat = jnp.pad(flat, (0, rows * LANE - n_small)).reshape(1, rows, LANE)
            riding.append(_Gather([flat]))
        both = _Both(riding)
        below, rode = ffn_bwd(l, 0, sv["ffn1"], dxres, dyb, both, (l - 1, 3) if l > 0 else None)
        rode = both.split(rode)
        rs_last_level(heads, rode[0])
        if l > 0:
            dxres, dyb = below
            above = rs_first_level(l, "c")
    grad_x = below.reshape(1, T, D)

    red = _sum_devices(rode[1][0].reshape(N_DEV, rows, LANE), name="sum_small").reshape(-1)
    loss = red[0]
    gsm, pos = {}, 1
    for k in small_keys:
        size = math.prod(gS[k].shape)
        gsm[k] = red[pos:pos + size].reshape(gS[k].shape)
        pos += size

    me = 4 * lax.axis_index("x") + 2 * lax.axis_index("y") + lax.axis_index("c")
    dsh = D // N_DEV
    stack = lambda f: jnp.stack([f(l) for l in range(L)])
    g_ln_g = stack(lambda l: jnp.concatenate([gsm[("ln_g", l, k)] for k in range(4)], axis=0))
    g_ln_b = stack(lambda l: jnp.concatenate([gsm[("ln_b", l, k)] for k in range(4)], axis=0))
    swapped = {
        "ffn1_w13": stack(lambda l: gsh[("w13", 0, l)]),
        "ffn2_w13": stack(lambda l: gsh[("w13", 1, l)]),
        "w_in": stack(lambda l: gsh[("w_in", l)][:, :DIN].T),
        "w_uq": stack(lambda l: gsh[("w_uq", l)]),
        "w_ukv": stack(lambda l: gsh[("w_ukv", l)]),
    }
    swap = lambda a: jnp.swapaxes(a, 1, 2)
    grads = {
        "ln_g": lax.dynamic_slice_in_dim(g_ln_g, me * dsh, dsh, axis=2),
        "ln_b": lax.dynamic_slice_in_dim(g_ln_b, me * dsh, dsh, axis=2),
        "ffn1_w2": stack(lambda l: gsh[("w2", 0, l)]),
        "pool_w": stack(lambda l: gsm[("pool_w", l)]),
        "pool_scale": stack(lambda l: gsm[("pool_scale", l)].reshape(PW)),
        "q_norm_g": stack(lambda l: gsm[("q_norm_g", l)].reshape(Q_LORA)),
        "kv_norm_g": stack(lambda l: gsm[("kv_norm_g", l)].reshape(KV_LORA)),
        "w_out": stack(lambda l: gsh[("w_out", l)]),
        "mem_wq": stack(lambda l: gsh[("mem_wq", l)]),
        "mem_wkv": stack(lambda l: gsh[("mem_wkv", l)].T),
        "mem_wo": stack(lambda l: gsh[("mem_wo", l)]),
        "ffn2_w2": stack(lambda l: gsh[("w2", 1, l)]),
        **{nme: swap(g) for nme, g in swapped.items()},
    }

    names = ["ln_g", "ln_b", "ffn1_w13", "ffn1_w2", "w_in", "pool_w", "pool_scale", "q_norm_g", "w_uq",
             "kv_norm_g", "w_ukv", "w_out", "mem_wq", "mem_wkv", "mem_wo", "ffn2_w13", "ffn2_w2"]
    weights = dict(ln_g=ln_g, ln_b=ln_b, ffn1_w13=ffn1_w13, ffn1_w2=ffn1_w2, w_in=w_in, pool_w=pool_w,
                   pool_scale=pool_scale, q_norm_g=q_norm_g, w_uq=w_uq, kv_norm_g=kv_norm_g, w_ukv=w_ukv,
                   w_out=w_out, mem_wq=mem_wq, mem_wkv=mem_wkv, mem_wo=mem_wo, ffn2_w13=ffn2_w13,
                   ffn2_w2=ffn2_w2)
    ms = dict(ln_g=m_ln_g, ln_b=m_ln_b, ffn1_w13=m_ffn1_w13, ffn1_w2=m_ffn1_w2, w_in=m_w_in, pool_w=m_pool_w,
              pool_scale=m_pool_scale, q_norm_g=m_q_norm_g, w_uq=m_w_uq, kv_norm_g=m_kv_norm_g,
              w_ukv=m_w_ukv, w_out=m_w_out, mem_wq=m_mem_wq, mem_wkv=m_mem_wkv, mem_wo=m_mem_wo,
              ffn2_w13=m_ffn2_w13, ffn2_w2=m_ffn2_w2)
    vs = dict(ln_g=v_ln_g, ln_b=v_ln_b, ffn1_w13=v_ffn1_w13, ffn1_w2=v_ffn1_w2, w_in=v_w_in, pool_w=v_pool_w,
              pool_scale=v_pool_scale, q_norm_g=v_q_norm_g, w_uq=v_w_uq, kv_norm_g=v_kv_norm_g,
              w_ukv=v_w_ukv, w_out=v_w_out, mem_wq=v_mem_wq, mem_wkv=v_mem_wkv, mem_wo=v_mem_wo,
              ffn2_w13=v_ffn2_w13, ffn2_w2=v_ffn2_w2)
    deltas, new_m, new_v = [], [], []
    for nme in names:
        if nme in swapped:
            d, mn, vn = [swap(o) for o in _adamw(swap(weights[nme]), swapped[nme], swap(ms[nme]), swap(vs[nme]),
                                                 name=f"adamw_{nme}")]
        else:
            d, mn, vn = _adamw(weights[nme], grads[nme], ms[nme], vs[nme], name=f"adamw_{nme}")
        deltas.append(d)
        new_m.append(mn)
        new_v.append(vn)
    return (loss, grad_x, *[grads[nme] for nme in names], *deltas, *new_m, *new_v)
```

```python
import functools
import math

import jax
import jax.numpy as jnp
from jax import lax
from jax.experimental import pallas as pl
from jax.experimental.pallas import tpu as pltpu

F32 = jnp.float32
BF16 = jnp.bfloat16
MESH = pl.DeviceIdType.MESH

CHUNK = 64
MEM_HEADS = 4
POOL_WINDOWS = (2, 4, 8, 16)
QK_NOPE = 128
QK_ROPE = 64
V_HEAD = 128
Q_LORA = 256
KV_LORA = 128
ROPE_BASE = 10000.0
LN_EPS = 1e-5
RMS_EPS = 1e-6
NEG_INF = -1e30
ADAM_LR = 0.001
ADAM_B1 = 0.9
ADAM_B2 = 0.999
ADAM_EPS = 1e-08
ADAM_WD = 0.01
ADAM_STEP = 10

N_DEV = 8
LANE = 128
HEAD_PAD = 2 * LANE
POOL_HALO = 16
VMEM_CAP = 56 * 1024 * 1024
VMEM_FLOOR = 32 * 1024 * 1024


def _tile(n, pref, mult):
    t = (min(pref, n) // mult) * mult
    while t >= mult:
        if n % t == 0:
            return t
        t -= mult
    return n


def _params(sem, est_bytes):
    limit = int(min(max(2 * est_bytes + (8 << 20), VMEM_FLOOR), VMEM_CAP))
    return pltpu.CompilerParams(dimension_semantics=sem, vmem_limit_bytes=limit)


def _nbytes(shape, dtype):
    return math.prod(shape) * jnp.dtype(dtype).itemsize


def _hbm(x):
    return pltpu.with_memory_space_constraint(x, pltpu.HBM)


def _out(shape, dtype):
    return pltpu.HBM(tuple(shape), dtype)


def _dg(a, b, ca, cb):
    return lax.dot_general(a.astype(BF16), b.astype(BF16), (((ca,), (cb,)), ((), ())),
                           preferred_element_type=F32)


@jax.custom_vjp
def _bdot_nn(a, b):
    return _dg(a, b, 1, 0)


def _bdot_nn_fwd(a, b):
    return _dg(a, b, 1, 0), (a, b)


def _bdot_nn_bwd(res, ct):
    a, b = res
    return _dg(ct, b, 1, 1).astype(a.dtype), _dg(a, ct, 0, 0).astype(b.dtype)


_bdot_nn.defvjp(_bdot_nn_fwd, _bdot_nn_bwd)


@jax.custom_vjp
def _bdot_nt(a, b):
    return _dg(a, b, 1, 1)


def _bdot_nt_fwd(a, b):
    return _dg(a, b, 1, 1), (a, b)


def _bdot_nt_bwd(res, ct):
    a, b = res
    return _dg(ct, b, 1, 0).astype(a.dtype), _dg(ct, a, 0, 0).astype(b.dtype)


_bdot_nt.defvjp(_bdot_nt_fwd, _bdot_nt_bwd)


@functools.partial(jax.custom_vjp, nondiff_argnums=(1,))
def _lane_roll(x, shift):
    return pltpu.roll(x, shift % x.shape[1], axis=1)


def _lane_roll_fwd(x, shift):
    return _lane_roll(x, shift), None


def _lane_roll_bwd(shift, _, ct):
    return (_lane_roll(ct, -shift),)


_lane_roll.defvjp(_lane_roll_fwd, _lane_roll_bwd)


@functools.partial(jax.custom_vjp, nondiff_argnums=(1, 2))
def _cols(x, lo, hi):
    return x[:, lo:hi]


def _cols_fwd(x, lo, hi):
    return x[:, lo:hi], x.shape[1]


def _cols_bwd(lo, hi, width, ct):
    parts = []
    if lo > 0:
        parts.append(jnp.zeros((ct.shape[0], lo), ct.dtype))
    parts.append(ct)
    if hi < width:
        parts.append(jnp.zeros((ct.shape[0], width - hi), ct.dtype))
    return (jnp.concatenate(parts, axis=1) if len(parts) > 1 else ct,)


_cols.defvjp(_cols_fwd, _cols_bwd)


MM_VMEM_BUDGET = 22 * 1024 * 1024


def _mm(a, b, *, name, ta=False, tb=False, out_dtype=F32, lead=None, add=None, add_scale=1.0,
        tm=1024, tn=1024, tk=8192, ride=None):
    if ta:
        K, M = a.shape
    else:
        M, K = a.shape
    bshape = b.shape[1:] if lead is not None else b.shape
    if tb:
        N, Kb = bshape
    else:
        Kb, N = bshape
    assert K == Kb, (name, a.shape, b.shape)

    def blocks(tm, tn, tk):
        tm = _tile(M, tm, LANE if ta else 16)
        tn = _tile(N, tn, LANE)
        tk = _tile(K, tk, LANE)
        nbytes = (tm * tk * a.dtype.itemsize + tk * tn * b.dtype.itemsize
                  + tm * tn * (jnp.dtype(out_dtype).itemsize + (4 if K // tk > 1 else 0)
                               + (add.dtype.itemsize if add is not None else 0)))
        return tm, tn, tk, nbytes

    if ta:
        tm = min(tm, max(LANE, M // 4))
    tm, tn, tk, est = blocks(tm, tn, tk)
    for shrink in ("m", "k", "m", "k", "n"):
        if est <= MM_VMEM_BUDGET:
            break
        if shrink == "m":
            tm, tn, tk, est = blocks(max(tm // 2, LANE), tn, tk)
        elif shrink == "k":
            tm, tn, tk, est = blocks(tm, tn, max(tk // 2, LANE))
        else:
            tm, tn, tk, est = blocks(tm, max(tn // 2, LANE), tk)
    nk = K // tk
    ca = 0 if ta else 1
    cb = 1 if tb else 0

    def body(*refs):
        a_ref, b_ref = refs[0], refs[1]
        add_ref = refs[2] if add is not None else None
        o_ref = refs[3] if add is not None else refs[2]

        def finish(r):
            if add_ref is not None:
                r = r + add_scale * add_ref[...].astype(F32)
            o_ref[...] = r.astype(o_ref.dtype)

        if nk == 1:
            finish(_dg(a_ref[...], b_ref[...], ca, cb))
            return
        acc_ref = refs[-1]
        k = pl.program_id(2)

        @pl.when(k == 0)
        def _():
            acc_ref[...] = jnp.zeros_like(acc_ref)

        acc_ref[...] += _dg(a_ref[...], b_ref[...], ca, cb)

        @pl.when(k == nk - 1)
        def _():
            finish(acc_ref[...])

    a_blk = (tk, tm) if ta else (tm, tk)
    a_map = (lambda i, j, k: (k, i)) if ta else (lambda i, j, k: (i, k))
    b_blk = (tn, tk) if tb else (tk, tn)
    if lead is None:
        b_map = (lambda i, j, k: (j, k)) if tb else (lambda i, j, k: (k, j))
        b_spec = pl.BlockSpec(b_blk, b_map)
    else:
        b_map = (lambda i, j, k: (lead, j, k)) if tb else (lambda i, j, k: (lead, k, j))
        b_spec = pl.BlockSpec((None,) + b_blk, b_map)
    in_specs = [pl.BlockSpec(a_blk, a_map), b_spec]
    args = [a, b]
    if add is not None:
        in_specs.append(pl.BlockSpec((tm, tn), lambda i, j, k: (i, j)))
        args.append(add)
    (out,), rode = _host_call(
        body, name=name,
        grid=(M // tm, N // tn, nk),
        in_specs=in_specs,
        out_specs=[pl.BlockSpec((tm, tn), lambda i, j, k: (i, j))],
        out_shape=[_out((M, N), out_dtype)],
        scratch=[pltpu.VMEM((tm, tn), F32)] if nk > 1 else [],
        args=[_hbm(v) for v in args], sem=("parallel", "parallel", "arbitrary"), est=est + tm * tn * 4,
        ride=ride)
    return out if ride is None else (out, rode)


def _mm_pair(dy, w, lead, x, *, dx_dtype, name, wt=False):
    M, N = dy.shape
    Kx = x.shape[1]
    wshape = (N, Kx) if wt else (Kx, N)
    assert w.shape[1:] == wshape, (name, w.shape, x.shape, dy.shape)
    tm = _tile(M, 512, 16)
    nt = M // tm

    def body(dy_ref, w_ref, x_ref, dx_ref, dw_ref, acc_ref):
        i = pl.program_id(0)

        @pl.when(i == 0)
        def _():
            acc_ref[...] = jnp.zeros_like(acc_ref)

        dyv = dy_ref[...]
        dx_ref[...] = _dg(dyv, w_ref[...], 1, 0 if wt else 1).astype(dx_ref.dtype)
        acc_ref[...] += _dg(dyv, x_ref[...], 0, 0) if wt else _dg(x_ref[...], dyv, 0, 0)

        @pl.when(i == nt - 1)
        def _():
            dw_ref[...] = acc_ref[...].astype(dw_ref.dtype)

    est = tm * (N + 2 * Kx) * 4 + Kx * N * (2 + 4 + 2)
    return pl.pallas_call(
        body, name=name, grid=(nt,),
        in_specs=[pl.BlockSpec((tm, N), lambda i: (i, 0)),
                  pl.BlockSpec((None,) + wshape, lambda i: (lead, 0, 0)),
                  pl.BlockSpec((tm, Kx), lambda i: (i, 0))],
        out_specs=[pl.BlockSpec((tm, Kx), lambda i: (i, 0)), pl.BlockSpec(wshape, lambda i: (0, 0))],
        out_shape=[_out((M, Kx), dx_dtype), _out(wshape, BF16)],
        scratch_shapes=[pltpu.VMEM(wshape, F32)],
        compiler_params=_params(("arbitrary",), est),
    )(_hbm(dy), _hbm(w), _hbm(x))


def _rowwise(fn, tiles, params, tile_outs, acc_outs=(), *, tm, name):
    tile_arrays, tile_specs = [], []
    for t in tiles:
        if isinstance(t, tuple):
            tile_arrays.append(t[0])
            tile_specs.append(t[1])
        else:
            tile_arrays.append(t)
            tile_specs.append(pl.BlockSpec((tm, t.shape[1]), lambda i: (i, 0)))
    T = tile_arrays[0].shape[0]
    nt, np_, nto, nao = len(tile_arrays), len(params), len(tile_outs), len(acc_outs)

    def body(*refs):
        i = pl.program_id(0)
        tvals = [r[...] for r in refs[:nt]]
        pvals = [r[...] for r in refs[nt:nt + np_]]
        to_refs = refs[nt + np_:nt + np_ + nto]
        ao_refs = refs[nt + np_ + nto:]
        touts, aouts = fn(i, tvals, pvals)
        for r, v in zip(to_refs, touts):
            r[...] = v.astype(r.dtype)
        if nao:
            @pl.when(i == 0)
            def _():
                for r in ao_refs:
                    r[...] = jnp.zeros_like(r)
            for r, v in zip(ao_refs, aouts):
                r[...] += v.astype(r.dtype)

    in_specs = tile_specs + [pl.BlockSpec(p.shape, lambda i: (0, 0)) for p in params]
    out_specs = [pl.BlockSpec((tm, c), lambda i: (i, 0)) for c, _ in tile_outs]
    out_specs += [pl.BlockSpec(s, lambda i: (0, 0)) for s, _ in acc_outs]
    out_shape = [_out((T, c), d) for c, d in tile_outs]
    out_shape += [_out(s, d) for s, d in acc_outs]
    width = sum(s.block_shape[-1] for s in tile_specs) + sum(c for c, _ in tile_outs)
    est = 6 * tm * width * 4 + sum(_nbytes(p.shape, F32) for p in params) * 4
    return pl.pallas_call(
        body, name=name, grid=(T // tm,),
        in_specs=in_specs, out_specs=out_specs, out_shape=out_shape,
        compiler_params=_params(("arbitrary",) if nao else ("parallel",), est),
    )(*[_hbm(v) for v in tile_arrays], *[_hbm(p) for p in params])


def _ln_fn(alpha, s, xres, y, g, b):
    z = alpha * xres.astype(F32) + s * y.astype(F32)
    mu = jnp.mean(z, axis=-1, keepdims=True)
    zc = z - mu
    var = jnp.mean(zc * zc, axis=-1, keepdims=True)
    return zc * lax.rsqrt(var + LN_EPS) * g + b


def _mm_ln(a, b, lead, xres, g, bias, *, alpha, s, name):
    M, K = a.shape
    N = b.shape[2]
    tm = _tile(M, 512, 16)

    def body(a_ref, b_ref, x_ref, g_ref, bias_ref, y_ref, xo_ref, xb_ref):
        y = _dg(a_ref[...], b_ref[...], 1, 0)
        y_ref[...] = y.astype(y_ref.dtype)
        out = _ln_fn(alpha, s, x_ref[...], y, g_ref[...], bias_ref[...])
        xo_ref[...] = out
        xb_ref[...] = out.astype(BF16)

    row = pl.BlockSpec((tm, N), lambda i: (i, 0))
    vec = pl.BlockSpec((1, N), lambda i: (0, 0))
    est = tm * K * 2 + K * N * 2 + tm * N * (4 + 4 + 4 + 2 + 8)
    return pl.pallas_call(
        body, name=name, grid=(M // tm,),
        in_specs=[pl.BlockSpec((tm, K), lambda i: (i, 0)), pl.BlockSpec((None, K, N), lambda i: (lead, 0, 0)),
                  row, vec, vec],
        out_specs=[row, row, row],
        out_shape=[_out((M, N), BF16), _out((M, N), F32), _out((M, N), BF16)],
        compiler_params=_params(("parallel",), est),
    )(_hbm(a), _hbm(b), _hbm(xres), _hbm(g), _hbm(bias))


def _ln_bwd_math(alpha, s, x, y, g, d):
    z = alpha * x + s * y.astype(F32)
    zc = z - jnp.mean(z, axis=-1, keepdims=True)
    r = lax.rsqrt(jnp.mean(zc * zc, axis=-1, keepdims=True) + LN_EPS)
    xh = zc * r
    dxh = d * g
    dz = r * (dxh - jnp.mean(dxh, axis=-1, keepdims=True) - xh * jnp.mean(dxh * xh, axis=-1, keepdims=True))
    return alpha * dz, s * dz, jnp.sum(d * xh, axis=0, keepdims=True), jnp.sum(d, axis=0, keepdims=True)


def _mm_ln_bwd(a, b, lead, tb, add, xres, y, g, *, alpha, s, name, xin=None):
    M, K = a.shape
    N = b.shape[1] if tb else b.shape[2]
    tk = K if K * N * 2 <= MM_VMEM_BUDGET * 3 // 5 else _tile(K, 2816, LANE)
    tm = _tile(M, 512 if K * N * 2 <= MM_VMEM_BUDGET // 4 else 256, 16)
    nk = K // tk
    nt = M // tm
    cb = 1 if tb else 0
    assert xin is None or (tb and nk == 1), name

    def body(*refs):
        a_ref, b_ref, add_ref, x_ref, y_ref, g_ref = refs[:6]
        rest = refs[6:]
        if xin is not None:
            xin_ref, rest = rest[0], rest[1:]
        dx_ref, dy_ref, dg_ref, db_ref = rest[:4]
        scratch = rest[4:]
        i, k = pl.program_id(0), pl.program_id(1)

        def finish(d):
            @pl.when(i == 0)
            def _():
                dg_ref[...] = jnp.zeros_like(dg_ref)
                db_ref[...] = jnp.zeros_like(db_ref)

            dx, dy, dg, db = _ln_bwd_math(alpha, s, x_ref[...], y_ref[...], g_ref[...], d + add_ref[...])
            dx_ref[...] = dx
            dy_ref[...] = dy.astype(dy_ref.dtype)
            dg_ref[...] += dg
            db_ref[...] += db

        if xin is not None:
            dw_ref, accw_ref = scratch[0], scratch[1]

            @pl.when(i == 0)
            def _():
                accw_ref[...] = jnp.zeros_like(accw_ref)

            accw_ref[...] += _dg(xin_ref[...], a_ref[...], 0, 0)

            @pl.when(i == nt - 1)
            def _():
                dw_ref[...] = accw_ref[...].astype(dw_ref.dtype)

        if nk == 1:
            finish(_dg(a_ref[...], b_ref[...], 1, cb))
            return
        acc_ref = scratch[0]

        @pl.when(k == 0)
        def _():
            acc_ref[...] = jnp.zeros_like(acc_ref)

        acc_ref[...] += _dg(a_ref[...], b_ref[...], 1, cb)

        @pl.when(k == nk - 1)
        def _():
            finish(acc_ref[...])

    row = pl.BlockSpec((tm, N), lambda i, k: (i, 0))
    vec = pl.BlockSpec((1, N), lambda i, k: (0, 0))
    b_spec = (pl.BlockSpec((None, N, tk), lambda i, k: (lead, 0, k)) if tb
              else pl.BlockSpec((None, tk, N), lambda i, k: (lead, k, 0)))
    est = tm * tk * 2 + tk * N * 2 + tm * N * (4 + 4 + 2 + 4 + 2 + 4 + 12)
    in_specs = [pl.BlockSpec((tm, tk), lambda i, k: (i, k)), b_spec, row, row, row, vec]
    out_specs = [row, row, vec, vec]
    out_shape = [_out((M, N), F32), _out((M, N), BF16), _out((1, N), F32), _out((1, N), F32)]
    scratch = [pltpu.VMEM((tm, N), F32)] if nk > 1 else []
    args = [a, b, add, xres, y, g]
    if xin is not None:
        in_specs.append(row)
        args.append(xin)
        out_specs.append(pl.BlockSpec((N, K), lambda i, k: (0, 0)))
        out_shape.append(_out((N, K), BF16))
        scratch.append(pltpu.VMEM((N, K), F32))
        est += N * K * 8 + tm * N * 2
    return pl.pallas_call(
        body, name=name, grid=(nt, nk),
        in_specs=in_specs, out_specs=out_specs, out_shape=out_shape, scratch_shapes=scratch,
        compiler_params=_params(("arbitrary", "arbitrary"), est),
    )(*[_hbm(v) for v in args])


def _loss_ln_bwd(xres, y, g, out, target, *, alpha, s, name):
    T, D = xres.shape
    tm = _tile(T, 256, 16)

    def body(x_ref, y_ref, o_ref, t_ref, g_ref, dx_ref, dy_ref, dg_ref, db_ref, loss_ref):
        @pl.when(pl.program_id(0) == 0)
        def _():
            dg_ref[...] = jnp.zeros_like(dg_ref)
            db_ref[...] = jnp.zeros_like(db_ref)
            loss_ref[...] = jnp.zeros_like(loss_ref)

        err = o_ref[...] - t_ref[...]
        part = 0.5 * jnp.sum(jnp.sum(err * err, axis=1, keepdims=True) / D, axis=0, keepdims=True)
        loss_ref[...] += jnp.broadcast_to(part, loss_ref.shape)
        dx, dy, dg, db = _ln_bwd_math(alpha, s, x_ref[...], y_ref[...], g_ref[...], err / D)
        dx_ref[...] = dx
        dy_ref[...] = dy.astype(dy_ref.dtype)
        dg_ref[...] += dg
        db_ref[...] += db

    row = pl.BlockSpec((tm, D), lambda i: (i, 0))
    vec = pl.BlockSpec((1, D), lambda i: (0, 0))
    return pl.pallas_call(
        body, name=name, grid=(T // tm,),
        in_specs=[row, row, row, row, vec],
        out_specs=[row, row, vec, vec, pl.BlockSpec((8, LANE), lambda i: (0, 0))],
        out_shape=[_out((T, D), F32), _out((T, D), BF16), _out((1, D), F32), _out((1, D), F32),
                   _out((8, LANE), F32)],
        compiler_params=_params(("arbitrary",), 14 * tm * D * 4),
    )(_hbm(xres), _hbm(y), _hbm(out), _hbm(target), _hbm(g))


FFN_TILE = 256


def _interleave(w, axis):
    n = w.shape[axis] // (2 * FFN_TILE)
    shp = w.shape[:axis] + (2, n, FFN_TILE) + w.shape[axis + 1:]
    return jnp.swapaxes(w.reshape(shp), axis, axis + 1).reshape(w.shape)


def _deinterleave(w, axis):
    n = w.shape[axis] // (2 * FFN_TILE)
    shp = w.shape[:axis] + (n, 2, FFN_TILE) + w.shape[axis + 1:]
    return jnp.swapaxes(w.reshape(shp), axis, axis + 1).reshape(w.shape)


def _ffn_up(xb, w13t, lead, *, name, ride=None):
    T, D = xb.shape
    F = w13t.shape[1] // 2
    tc = FFN_TILE
    tm = _tile(T, 2048, 16)

    def body(x_ref, w_ref, h_ref, a_ref):
        h = _dg(x_ref[...], w_ref[...], 1, 1)
        g, u = h[:, :tc], h[:, tc:]
        h_ref[...] = h.astype(h_ref.dtype)
        a_ref[...] = (g * jax.nn.sigmoid(g) * u).astype(a_ref.dtype)

    est = (tm * D + 2 * tc * D + 3 * tm * tc) * 2 + 3 * tm * tc * 4
    (h13, a), gathered = _host_call(
        body, name=name, grid=(T // tm, F // tc),
        in_specs=[pl.BlockSpec((tm, D), lambda i, j: (i, 0)),
                  pl.BlockSpec((None, 2 * tc, D), lambda i, j: (lead, j, 0))],
        out_specs=[pl.BlockSpec((tm, 2 * tc), lambda i, j: (i, j)),
                   pl.BlockSpec((tm, tc), lambda i, j: (i, j))],
        out_shape=[_out((T, 2 * F), BF16), _out((T, F), BF16)],
        args=[_hbm(xb), _hbm(w13t)], sem=("parallel", "parallel"), est=est, ride=ride)
    return h13, a, gathered


def _ffn_down_bwd(dyb, w2, lead, h13, *, name, ride=None):
    T, D = dyb.shape
    F = w2.shape[1]
    tc = FFN_TILE
    tm = _tile(T, 2048, 16)

    def body(dy_ref, w_ref, h_ref, dh_ref):
        d = _dg(dy_ref[...], w_ref[...], 1, 1)
        h = h_ref[...].astype(F32)
        g, u = h[:, :tc], h[:, tc:]
        sig = jax.nn.sigmoid(g)
        gs = g * sig
        dh_ref[...] = jnp.concatenate([d * u * (sig + gs * (1.0 - sig)), d * gs], axis=1).astype(dh_ref.dtype)

    est = (tm * D + tc * D + 4 * tm * tc) * 2 + 6 * tm * tc * 4
    (dh,), rode = _host_call(
        body, name=name, grid=(T // tm, F // tc),
        in_specs=[pl.BlockSpec((tm, D), lambda i, j: (i, 0)),
                  pl.BlockSpec((None, tc, D), lambda i, j: (lead, j, 0)),
                  pl.BlockSpec((tm, 2 * tc), lambda i, j: (i, j))],
        out_specs=[pl.BlockSpec((tm, 2 * tc), lambda i, j: (i, j))],
        out_shape=[_out((T, 2 * F), BF16)],
        args=[_hbm(dyb), _hbm(w2), _hbm(h13)], sem=("parallel", "parallel"), est=est, ride=ride)
    return dh, rode


def _pool_select(parts, pw):
    pg = pw // len(POOL_WINDOWS)
    grp = lax.broadcasted_iota(jnp.int32, parts[0].shape, 1) // pg
    out = parts[3]
    for g in (2, 1, 0):
        out = jnp.where(grp == g, parts[g], out)
    return out


def _pool_count(t0, rows, pw):
    pg = pw // len(POOL_WINDOWS)
    grp = lax.broadcasted_iota(jnp.int32, (rows, pw), 1) // pg
    win = jnp.where(grp == 0, POOL_WINDOWS[0],
                    jnp.where(grp == 1, POOL_WINDOWS[1],
                              jnp.where(grp == 2, POOL_WINDOWS[2], POOL_WINDOWS[3])))
    t = t0 + lax.broadcasted_iota(jnp.int32, (rows, pw), 0)
    return jnp.minimum(t + 1, win).astype(F32)


def _window_sums(ext, up):
    n = ext.shape[0]
    sums, cur, k = [], ext, 1
    for _ in POOL_WINDOWS:
        cur = cur + pltpu.roll(cur, (n - k) if up else k, axis=0)
        sums.append(cur)
        k *= 2
    return sums


def _pool_delta(u, halo, t0):
    tm, pw = u.shape
    ext = jnp.concatenate([halo, u], axis=0)
    sums = [s[POOL_HALO:, :] for s in _window_sums(ext, up=False)]
    return _pool_select(sums, pw) / _pool_count(t0, tm, pw) - u


def _pool_fwd(hin, wbd, scale, cat, *, name):
    T = hin.shape[0]
    pw = wbd.shape[0]
    tm = _tile(T, 256, POOL_HALO)
    per = tm // POOL_HALO

    def body(u_ref, halo_ref, w_ref, s_ref, cat_ref, y_ref):
        i = pl.program_id(0)
        halo = jnp.where(i > 0, halo_ref[...], 0.0)
        d = _pool_delta(u_ref[...], halo, i * tm)
        y_ref[...] = (_dg(d, w_ref[...], 1, 0) * s_ref[...]).astype(y_ref.dtype)

    return pl.pallas_call(
        body, name=name, grid=(T // tm,),
        in_specs=[pl.BlockSpec((tm, pw), lambda i: (i, 0)),
                  pl.BlockSpec((POOL_HALO, pw), lambda i: (jnp.maximum(i * per - 1, 0), 0)),
                  pl.BlockSpec((pw, pw), lambda i: (0, 0)),
                  pl.BlockSpec((1, pw), lambda i: (0, 0)),
                  ANY],
        out_specs=pl.BlockSpec((tm, pw), lambda i: (i, 0)),
        out_shape=_out(cat.shape, cat.dtype),
        input_output_aliases={4: 0},
        compiler_params=_params(("parallel",), 16 * tm * pw * 4),
    )(_hbm(hin), _hbm(hin), _hbm(wbd), _hbm(scale), _hbm(cat))


def _pool_bwd(hin, dcat, wbd, scale, *, name):
    T = hin.shape[0]
    pw = wbd.shape[0]
    tm = _tile(T, 256, POOL_HALO)
    per = tm // POOL_HALO
    nt = T // tm

    def body(u_ref, halo_ref, dy_ref, dyn_ref, w_ref, s_ref, du_ref, dw_ref, ds_ref):
        i = pl.program_id(0)

        @pl.when(i == 0)
        def _():
            dw_ref[...] = jnp.zeros_like(dw_ref)
            ds_ref[...] = jnp.zeros_like(ds_ref)

        halo = jnp.where(i > 0, halo_ref[...], 0.0)
        d = _pool_delta(u_ref[...], halo, i * tm)
        w = w_ref[...]
        sc = s_ref[...]
        dy = dy_ref[...]
        dyn = jnp.where(i < nt - 1, dyn_ref[...], 0.0)
        ds_ref[...] += jnp.sum(dy * _dg(d, w, 1, 0), axis=0, keepdims=True)
        dys = dy * sc
        dw_ref[...] += _dg(d, dys, 0, 0)
        dys_ext = jnp.concatenate([dys, dyn * sc], axis=0)
        dd_ext = _dg(dys_ext, w, 1, 1)
        ddp = dd_ext / _pool_count(i * tm, tm + POOL_HALO, pw)
        sums = [s[:tm, :] for s in _window_sums(ddp, up=True)]
        du_ref[...] = _pool_select(sums, pw) - dd_ext[:tm, :]

    return pl.pallas_call(
        body, name=name, grid=(nt,),
        in_specs=[pl.BlockSpec((tm, pw), lambda i: (i, 0)),
                  pl.BlockSpec((POOL_HALO, pw), lambda i: (jnp.maximum(i * per - 1, 0), 0)),
                  pl.BlockSpec((tm, pw), lambda i: (i, 0)),
                  pl.BlockSpec((POOL_HALO, pw), lambda i: (jnp.minimum((i + 1) * per, nt * per - 1), 0)),
                  pl.BlockSpec((pw, pw), lambda i: (0, 0)),
                  pl.BlockSpec((1, pw), lambda i: (0, 0))],
        out_specs=[pl.BlockSpec((tm, pw), lambda i: (i, 0)),
                   pl.BlockSpec((pw, pw), lambda i: (0, 0)),
                   pl.BlockSpec((1, pw), lambda i: (0, 0))],
        out_shape=[_out((T, pw), F32),
                   _out((pw, pw), F32),
                   _out((1, pw), F32)],
        compiler_params=_params(("arbitrary",), 24 * tm * pw * 4),
    )(_hbm(hin), _hbm(hin), _hbm(dcat), _hbm(dcat), _hbm(wbd), _hbm(scale))


def _rms(x, g):
    return x * lax.rsqrt(jnp.mean(x * x, axis=-1, keepdims=True) + RMS_EPS) * g


def _norms_fn(pw, h, gq, gkv):
    o1 = pw + Q_LORA
    o2 = o1 + KV_LORA
    return (_rms(_cols(h, pw, o1), gq), _rms(_cols(h, o1, o2), gkv), _cols(h, o2, h.shape[1]))


def _norms_bwd(hin, gq, gkv, dcq, dckv, dkpe, du, *, pw, name):
    tm = _tile(hin.shape[0], 256, 16)
    dinp = hin.shape[1]

    def fn(i, tv, pv):
        _, vjp = jax.vjp(functools.partial(_norms_fn, pw), tv[0], pv[0], pv[1])
        dh, dgq, dgkv = vjp((tv[1].astype(F32), tv[2].astype(F32), tv[3].astype(F32)))
        dh = jnp.concatenate([tv[4], dh[:, pw:]], axis=1)
        return (dh,), (dgq, dgkv)

    return _rowwise(fn, [hin, dcq, dckv, dkpe, du], [gq, gkv], [(dinp, BF16)],
                    [((1, Q_LORA), F32), ((1, KV_LORA), F32)], tm=tm, name=name)


def _heads_fn(H, qraw, kv, kpe, rc, rs1, rs2):
    half = QK_ROPE // 2
    scale = (QK_NOPE + QK_ROPE) ** -0.5

    def rope(blk):
        return blk * rc + _lane_roll(blk, -half) * rs1 + _lane_roll(blk, half) * rs2

    krot = rope(kpe)
    qs, ks, vs = [], [], []
    for h in range(H):
        lo = h * HEAD_PAD
        qs += [_cols(qraw, lo, lo + LANE) * scale, rope(_cols(qraw, lo + LANE, lo + HEAD_PAD)) * scale]
        ks += [_cols(kv, lo, lo + LANE), krot]
        vs += [_cols(kv, lo + LANE, lo + HEAD_PAD)]
    return jnp.concatenate(qs, axis=1), jnp.concatenate(ks, axis=1), jnp.concatenate(vs, axis=1)


def _heads_fwd(hin, gq, gkv, tabs, wuq, wukv, *, H, pw, name):
    tm = _tile(hin.shape[0], 256, 16)

    def fn(i, tv, pv):
        cqn, ckvn, kpe = _norms_fn(pw, tv[0], pv[0], pv[1])
        qraw = _dg(cqn, pv[2], 1, 1)
        kv = _dg(ckvn, pv[3], 1, 1)
        return (*_heads_fn(H, qraw, kv, kpe, *tv[1:]), cqn, ckvn), ()

    return _rowwise(fn, [hin, *tabs], [gq, gkv, wuq, wukv],
                    [(H * HEAD_PAD, BF16), (H * HEAD_PAD, BF16), (H * V_HEAD, BF16), (Q_LORA, BF16),
                     (KV_LORA, BF16)], tm=tm, name=name)


def _heads_bwd(dq, dk, dv, tabs, *, H, name):
    tm = _tile(dq.shape[0], 256, 16)

    def fn(i, tv, pv):
        z = jnp.zeros((tm, H * HEAD_PAD), F32)
        zk = jnp.zeros((tm, LANE), F32)
        rc, rs1, rs2 = tv[3], tv[4], tv[5]
        _, vjp = jax.vjp(lambda a, b, c: _heads_fn(H, a, b, c, rc, rs1, rs2), z, z, zk)
        return vjp((tv[0].astype(F32), tv[1].astype(F32), tv[2].astype(F32))), ()

    return _rowwise(fn, [dq, dk, dv, *tabs], [],
                    [(H * HEAD_PAD, BF16), (H * HEAD_PAD, BF16), (LANE, F32)], tm=tm, name=name)


def _diag_mask(rows, cols, row0):
    r = (row0 + lax.broadcasted_iota(jnp.int32, (rows, cols), 0)) // CHUNK
    c = lax.broadcasted_iota(jnp.int32, (rows, cols), 1) // CHUNK
    return r >= c


def _flash_fwd(qh, kh, vh, *, H, pw, name, ride=None):
    T = qh.shape[0]
    t = _tile(T, 512, CHUNK)
    off = pw // V_HEAD


    half = t

    def body(q_ref, k_ref, v_ref, o_ref, lse_ref):
        i = pl.program_id(1)
        q = q_ref[...]

        def update(carry, s, v):
            m, l, acc = carry
            mn = jnp.maximum(m, jnp.max(s, axis=1, keepdims=True))
            p = jnp.exp(s - mn)
            corr = jnp.exp(m - mn)
            return mn, corr * l + jnp.sum(p, axis=1, keepdims=True), corr * acc + _dg(p, v, 1, 0)

        def blk(j, carry):
            rows = pl.ds(pl.multiple_of(j * t, t), t)
            return update(carry, _dg(q, k_ref[rows, :], 1, 1), v_ref[rows, :])

        init = (jnp.full((t, 1), NEG_INF, F32), jnp.zeros((t, 1), F32), jnp.zeros((t, V_HEAD), F32))
        carry = lax.fori_loop(0, i, blk, init)
        done = []
        for r0 in range(0, t, half):
            keys = pl.ds(pl.multiple_of(i * t, t), r0 + half)
            s = _dg(q[r0:r0 + half, :], k_ref[keys, :], 1, 1)
            s = jnp.where(_diag_mask(half, r0 + half, r0), s, NEG_INF)
            done.append(update(tuple(c[r0:r0 + half] for c in carry), s, v_ref[keys, :]))
        m, l, acc = (jnp.concatenate(parts, axis=0) for parts in zip(*done))
        o_ref[...] = (acc / l).astype(o_ref.dtype)
        lse_ref[...] = jnp.broadcast_to(m + jnp.log(l), (t, V_HEAD))

    est = 2 * T * (HEAD_PAD + V_HEAD) * 2 + 8 * t * t * 4
    (o, lse), gathered = _host_call(
        body, name=name, grid=(H, T // t),
        in_specs=[pl.BlockSpec((t, HEAD_PAD), lambda h, i: (i, h)),
                  pl.BlockSpec((T, HEAD_PAD), lambda h, i: (0, h)),
                  pl.BlockSpec((T, V_HEAD), lambda h, i: (0, h))],
        out_specs=[pl.BlockSpec((t, V_HEAD), lambda h, i: (i, off + h)),
                   pl.BlockSpec((t, V_HEAD), lambda h, i: (i, h))],
        out_shape=[_out((T, pw + H * V_HEAD), BF16),
                   _out((T, H * V_HEAD), F32)],
        args=[_hbm(qh), _hbm(kh), _hbm(vh)], sem=("parallel", "parallel"), est=est, ride=ride)
    return o, lse, gathered


def _flash_bwd(qh, kh, vh, cat, dcat, lse, *, H, pw, name, ride=None):
    T = qh.shape[0]
    t = _tile(T, 512, CHUNK)
    nb = T // t
    off = pw // V_HEAD
    half = t

    def body(q_ref, k_ref, v_ref, o_ref, do_ref, lse_ref, dq_out_ref, dk_ref, dv_ref, dq_ref):
        j = pl.program_id(1)

        @pl.when(j == 0)
        def _():
            dq_ref[...] = jnp.zeros_like(dq_ref)

        kj = k_ref[...]
        vj = v_ref[...]

        def pair(rows, kx, vx, mask):
            qi = q_ref[rows, :]
            doi = do_ref[rows, :]
            oi = o_ref[rows, :].astype(F32)
            lsei = lse_ref[rows, :][:, :1]
            s = _dg(qi, kx, 1, 1)
            if mask is not None:
                s = jnp.where(mask, s, NEG_INF)
            p = jnp.exp(s - lsei)
            dp = _dg(doi, vx, 1, 1)
            di = jnp.sum(doi * oi, axis=1, keepdims=True)
            ds = p * (dp - di)
            dq_ref[rows, :] += _dg(ds, kx, 1, 0)
            return _dg(ds, qi, 0, 0), _dg(p, doi, 0, 0)

        def blk(i, carry):
            dk, dv = pair(pl.ds(pl.multiple_of(i * t, t), t), kj, vj, None)
            return carry[0] + dk, carry[1] + dv

        dk, dv = jnp.zeros((t, HEAD_PAD), F32), jnp.zeros((t, V_HEAD), F32)
        for r0 in range(0, t, half):
            n = r0 + half
            dkp, dvp = pair(pl.ds(pl.multiple_of(j * t + r0, half), half), kj[:n], vj[:n],
                            _diag_mask(half, n, r0))
            if n < t:
                dkp = jnp.concatenate([dkp, jnp.zeros((t - n, HEAD_PAD), F32)], axis=0)
                dvp = jnp.concatenate([dvp, jnp.zeros((t - n, V_HEAD), F32)], axis=0)
            dk, dv = dk + dkp, dv + dvp
        dk, dv = lax.fori_loop(j + 1, nb, blk, (dk, dv))
        dk_ref[...] = dk.astype(dk_ref.dtype)
        dv_ref[...] = dv.astype(dv_ref.dtype)

        @pl.when(j == nb - 1)
        def _():
            dq_out_ref[...] = dq_ref[...].astype(dq_out_ref.dtype)

    est = T * (HEAD_PAD * 2 + V_HEAD * 2 + V_HEAD * 4 + V_HEAD * 4 + HEAD_PAD * 4) + 10 * t * t * 4
    (dq, dk, dv), gathered = _host_call(
        body, name=name, grid=(H, nb),
        in_specs=[pl.BlockSpec((T, HEAD_PAD), lambda h, j: (0, h)),
                  pl.BlockSpec((t, HEAD_PAD), lambda h, j: (j, h)),
                  pl.BlockSpec((t, V_HEAD), lambda h, j: (j, h)),
                  pl.BlockSpec((T, V_HEAD), lambda h, j: (0, off + h)),
                  pl.BlockSpec((T, V_HEAD), lambda h, j: (0, off + h)),
                  pl.BlockSpec((T, V_HEAD), lambda h, j: (0, h))],
        out_specs=[pl.BlockSpec((T, HEAD_PAD), lambda h, j: (0, h)),
                   pl.BlockSpec((t, HEAD_PAD), lambda h, j: (j, h)),
                   pl.BlockSpec((t, V_HEAD), lambda h, j: (j, h))],
        out_shape=[_out((T, H * HEAD_PAD), BF16),
                   _out((T, H * HEAD_PAD), BF16),
                   _out((T, H * V_HEAD), BF16)],
        scratch=[pltpu.VMEM((T, HEAD_PAD), F32)],
        args=[_hbm(v) for v in (qh, kh, vh, cat, dcat, lse)], sem=("arbitrary", "arbitrary"), est=est,
        ride=ride)
    return dq, dk, dv, gathered


def _mem_fn(q, k, v):
    hd = q.shape[1] // MEM_HEADS
    outs = []
    for h in range(MEM_HEADS):
        lo, hi = h * hd, (h + 1) * hd
        s = _bdot_nt(_cols(q, lo, hi), _cols(k, lo, hi)) * hd ** -0.5
        e = jnp.exp(s - lax.stop_gradient(jnp.max(s, axis=1, keepdims=True)))
        p = e / jnp.sum(e, axis=1, keepdims=True)
        outs.append(_bdot_nn(p, _cols(v, lo, hi)))
    return jnp.concatenate(outs, axis=1)


def _mem_fwd(xb, wq, k, v, *, name):
    T, D = xb.shape
    tm = _tile(T, 256, 16)

    def fn(i, tv, pv):
        q = _dg(tv[0], pv[0], 1, 0).astype(BF16)
        return (_mem_fn(q, pv[1], pv[2]), q), ()

    return _rowwise(fn, [xb], [wq, k, v], [(D, BF16), (D, BF16)], tm=tm, name=name)


def _mem_bwd(q, k, v, do, *, name):
    T, D = q.shape
    tm = _tile(T, 256, 16)

    def fn(i, tv, pv):
        _, vjp = jax.vjp(_mem_fn, tv[0], pv[0], pv[1])
        dq, dk, dv = vjp(tv[1].astype(F32))
        return (dq,), (dk, dv)

    return _rowwise(fn, [q, do], [k, v], [(D, BF16)], [(k.shape, F32), (v.shape, F32)], tm=tm, name=name)


def _adamw(w, g, m, v, *, name):
    shape = w.shape
    if w.ndim != 3:
        lead3 = (1, math.prod(shape[:-1]), shape[-1])
        return [o.reshape(shape) for o in _adamw(*[a.reshape(lead3) for a in (w, g, m, v)], name=name)]
    Lw, R, C = shape
    tr = _tile(R, 512, 8)
    b1c = 1.0 - ADAM_B1 ** ADAM_STEP
    b2c = 1.0 - ADAM_B2 ** ADAM_STEP

    def body(w_ref, g_ref, m_ref, v_ref, d_ref, mo_ref, vo_ref):
        gg = g_ref[...]
        mn = ADAM_B1 * m_ref[...] + (1.0 - ADAM_B1) * gg
        vn = ADAM_B2 * v_ref[...] + (1.0 - ADAM_B2) * (gg * gg)
        d_ref[...] = -ADAM_LR * ((mn / b1c) / (jnp.sqrt(vn / b2c) + ADAM_EPS) + ADAM_WD * w_ref[...])
        mo_ref[...] = mn
        vo_ref[...] = vn

    spec = pl.BlockSpec((None, tr, C), lambda l, i: (l, i, 0))
    return pl.pallas_call(
        body, name=name, grid=(Lw, R // tr),
        in_specs=[spec] * 4, out_specs=[spec] * 3,
        out_shape=[_out(shape, F32)] * 3,
        compiler_params=_params(("parallel", "parallel"), 7 * tr * C * 4),
    )(*[_hbm(a) for a in (w, g, m, v)])


def _pair_sum(core, gs, landed, offs, *, name):
    n = len(gs)
    _, R, C = landed.shape
    rows = [g.shape[0] // N_DEV for g in gs]

    def body(core_ref, *refs):
        g_refs, l_ref, o_ref = refs[:n], refs[n], refs[n + 1]
        for g_ref, off, r in zip(g_refs, offs, rows):
            o_ref[off:off + r, :] = (g_ref[...].astype(F32) + l_ref[off:off + r, :].astype(F32)).astype(o_ref.dtype)

    slab = pl.BlockSpec((None, R, C), lambda p, core_ref: (p, 0, 0))
    own = [pl.BlockSpec((r, C), lambda p, core_ref: (2 * p + core_ref[0], 0)) for r in rows]
    return pl.pallas_call(
        body, name=name,
        grid_spec=pltpu.PrefetchScalarGridSpec(
            num_scalar_prefetch=1, grid=(4,), in_specs=own + [slab], out_specs=slab),
        out_shape=_out(landed.shape, landed.dtype),
        input_output_aliases={n + 1: 0},
        compiler_params=_params(("arbitrary",), 3 * R * C * 2 + R * C * 8),
    )(core, *[_hbm(g) for g in gs], _hbm(landed))


def _quad_sum(chip, part, gathered, used, *, name):
    C = part.shape[2]
    R = used
    tr = _tile(R, 256, 16)

    def body(chip_ref, own_ref, a_ref, b_ref, c_ref, o_ref):
        o_ref[...] = ((own_ref[...].astype(F32) + a_ref[...].astype(F32)) + b_ref[...].astype(F32)) \
            + c_ref[...].astype(F32)

    def other(k):
        return pl.BlockSpec((None, tr, C), lambda i, chip_ref: (chip_ref[0] ^ k, i, 0))

    return pl.pallas_call(
        body, name=name,
        grid_spec=pltpu.PrefetchScalarGridSpec(
            num_scalar_prefetch=1, grid=(R // tr,),
            in_specs=[pl.BlockSpec((None, tr, C), lambda i, chip_ref: (chip_ref[0], i, 0)),
                      other(1), other(2), other(3)],
            out_specs=pl.BlockSpec((tr, C), lambda i, chip_ref: (i, 0))),
        out_shape=_out((R, C), F32),
        compiler_params=_params(("arbitrary",), 8 * tr * C * 4),
    )(chip, _hbm(part), _hbm(gathered), _hbm(gathered), _hbm(gathered))


def _place():
    x, y, c = lax.axis_index("x"), lax.axis_index("y"), lax.axis_index("c")
    return x, y, c


ANY = pl.BlockSpec(memory_space=pl.ANY)


class _Gather:
    def __init__(self, shards):
        self.shards = list(shards)
        self.n = len(self.shards)
        self.out_shape = [_out((s.shape[0], N_DEV * s.shape[1], s.shape[2]), s.dtype)
                          for s in self.shards]
        self.scratch = [pltpu.SemaphoreType.DMA((7 * self.n,)), pltpu.SemaphoreType.DMA((7 * self.n,)),
                        pltpu.SemaphoreType.DMA((self.n,))]
        self.operands = [_hbm(s) for s in self.shards]

    def _bind(self, refs):
        n = self.n
        ins, outs = refs[:n], refs[n:2 * n]
        send_sems, recv_sems, local_sems = refs[2 * n:]
        x, y, c = _place()
        me, sib = (x, y, c), (x, y, 1 - c)
        chips = [(1 - x, y), (x, 1 - y), (1 - x, 1 - y)]

        def rows(w, p):
            r = self.shards[w].shape[1]
            idx = 4 * p[0] + 2 * p[1] + p[2]
            return outs[w].at[:, pl.ds(pl.multiple_of(idx * r, 8), r), :]

        def copy(w, k, block, to, src=None):
            return pltpu.make_async_remote_copy(
                src_ref=rows(w, block) if src is None else src, dst_ref=rows(w, block),
                send_sem=send_sems.at[w * 7 + k], recv_sem=recv_sems.at[w * 7 + k],
                device_id=to, device_id_type=MESH)

        def mine():
            return [pltpu.make_async_copy(ins[w], rows(w, me), local_sems.at[w]) for w in range(n)]

        def first():
            out = []
            for w in range(n):
                out.append(copy(w, 0, me, sib, src=ins[w]))
                out += [copy(w, 1 + j, me, (*chip, c), src=ins[w]) for j, chip in enumerate(chips)]
            return out

        def passed():
            return [copy(w, 4 + j, (*chip, c), sib) for j, chip in enumerate(chips) for w in range(n)]

        def landed():
            return [copy(w, 1 + j, (*chip, c), me) for j, chip in enumerate(chips) for w in range(n)]

        def last():
            out = []
            for w in range(n):
                out.append(copy(w, 0, sib, me))
                out += [copy(w, 4 + j, (*chip, 1 - c), me) for j, chip in enumerate(chips)]
            return out

        return mine, first, landed, passed, last

    def start(self, refs):
        mine, first, _, _, _ = self._bind(refs)
        for cp in mine() + first():
            cp.start()

    def forward(self, refs):
        _, _, landed, passed, _ = self._bind(refs)
        for arrived, fwd in zip(landed(), passed()):
            arrived.wait_recv()
            fwd.start()

    def finish(self, refs):
        mine, first, _, passed, last = self._bind(refs)
        for cp in last():
            cp.wait_recv()
        for cp in first() + passed():
            cp.wait_send()
        for cp in mine():
            cp.wait()


class _ChipExchange:
    def __init__(self, parts, used):
        self.ncl = len(parts)
        self.used = list(used)
        self.out_shape = [_out(p.shape, p.dtype) for p in parts]
        self.scratch = [pltpu.SemaphoreType.DMA((3 * self.ncl,)), pltpu.SemaphoreType.DMA((3 * self.ncl,))]
        self.operands = [_hbm(p) for p in parts]
        self.n = self.ncl

    def _bind(self, refs):
        ncl = self.ncl
        ins, outs = refs[:ncl], refs[ncl:2 * ncl]
        send_sems, recv_sems = refs[2 * ncl:]
        x, y, c = _place()
        chips = [(1 - x, y), (x, 1 - y), (1 - x, 1 - y)]
        here = 2 * x + y

        def copies(outgoing):
            out = []
            for k in range(ncl):
                rows = pl.ds(0, self.used[k])
                for j, (cx, cy) in enumerate(chips):
                    there = 2 * cx + cy
                    src, dst = (there, here) if outgoing else (here, there)
                    out.append(pltpu.make_async_remote_copy(
                        src_ref=ins[k].at[src, rows, :], dst_ref=outs[k].at[dst, rows, :],
                        send_sem=send_sems.at[3 * k + j], recv_sem=recv_sems.at[3 * k + j],
                        device_id=(cx, cy, c), device_id_type=MESH))
            return out

        return copies

    def start(self, refs):
        for cp in self._bind(refs)(True):
            cp.start()

    def forward(self, refs):
        pass

    def finish(self, refs):
        copies = self._bind(refs)
        for cp in copies(False):
            cp.wait_recv()
        for cp in copies(True):
            cp.wait_send()


class _Both:
    def __init__(self, members):
        self.members = list(members)
        self.n = sum(m.n for m in self.members)
        self.out_shape = [s for m in self.members for s in m.out_shape]
        self.scratch = [s for m in self.members for s in m.scratch]
        self.operands = [o for m in self.members for o in m.operands]

    def split(self, arrays):
        out, a = [], 0
        for m in self.members:
            out.append(list(arrays[a:a + m.n]))
            a += m.n
        return out

    def _refs(self, refs):
        ins, outs = self.split(refs[:self.n]), self.split(refs[self.n:2 * self.n])
        scr, b = [], 2 * self.n
        for m in self.members:
            scr.append(list(refs[b:b + len(m.scratch)]))
            b += len(m.scratch)
        return [(*i, *o, *s) for i, o, s in zip(ins, outs, scr)]

    def start(self, refs):
        for m, r in zip(self.members, self._refs(refs)):
            m.start(r)

    def forward(self, refs):
        for m, r in zip(self.members, self._refs(refs)):
            m.forward(r)

    def finish(self, refs):
        for m, r in zip(self.members, self._refs(refs)):
            m.finish(r)


def _exchange_alone(ex, *, name):
    def body(*refs):
        ex.start(refs)
        ex.forward(refs)
        ex.finish(refs)

    return pl.pallas_call(
        body, name=name, in_specs=[ANY] * ex.n, out_specs=[ANY] * ex.n,
        out_shape=ex.out_shape, scratch_shapes=ex.scratch,
    )(*ex.operands)


def _host_call(body, *, name, grid, in_specs, out_specs, out_shape, args, sem, est, ride=None, scratch=()):
    scratch = list(scratch)
    if ride is None:
        outs = pl.pallas_call(body, name=name, grid=grid, in_specs=in_specs, out_specs=out_specs,
                              out_shape=out_shape, scratch_shapes=scratch,
                              compiler_params=_params(sem, est))(*args)
        return list(outs), []
    n_in, n_out, n, n_scr = len(in_specs), len(out_specs), ride.n, len(scratch)

    def full(*refs):
        ins, rin = refs[:n_in], refs[n_in:n_in + n]
        outs, rout = refs[n_in + n:n_in + n + n_out], refs[n_in + n + n_out:n_in + 2 * n + n_out]
        own = refs[n_in + 2 * n + n_out:n_in + 2 * n + n_out + n_scr]
        rrefs = (*rin, *rout, *refs[n_in + 2 * n + n_out + n_scr:])
        step, total = _ride(ride, rrefs, grid)
        body(*ins, *outs, *own)
        _ride_end(ride, rrefs, step, total)

    outs = pl.pallas_call(
        full, name=name, grid=grid,
        in_specs=list(in_specs) + [ANY] * n, out_specs=list(out_specs) + [ANY] * n,
        out_shape=list(out_shape) + ride.out_shape, scratch_shapes=scratch + ride.scratch,
        compiler_params=_params(("arbitrary",) * len(grid), est),
    )(*args, *ride.operands)
    return list(outs[:n_out]), list(outs[n_out:])


def _ride(ex, refs, grid):
    total = math.prod(grid)
    step = pl.program_id(0)
    for axis in range(1, len(grid)):
        step = step * grid[axis] + pl.program_id(axis)
    pl.when(step == 0)(lambda: ex.start(refs))
    return step, total


def _ride_end(ex, refs, step, total):
    pl.when(step == (3 * total) // 4)(lambda: ex.forward(refs))
    pl.when(step == total - 1)(lambda: ex.finish(refs))


def _class_layout(grads, classes):
    used = [0] * len(set(classes))
    offs = []
    for g, cl in zip(grads, classes):
        offs.append(used[cl])
        used[cl] += g.shape[0] // N_DEV
    return offs, used


def _rs_to_sibling(grads, classes, *, name):
    n = len(grads)
    offs, used = _class_layout(grads, classes)
    heights = used
    ncl = len(heights)
    cols = [next(g.shape[1] for g, cl in zip(grads, classes) if cl == k) for k in range(ncl)]

    def body(*refs):
        gs, land = refs[:n], refs[n:n + ncl]
        send_sems, recv_sems = refs[n + ncl:]
        x, y, c = _place()
        sib = (x, y, 1 - c)
        for p in range(4):
            for w in range(n):
                r = grads[w].shape[0] // N_DEV
                cl = classes[w]
                there = gs[w].at[pl.ds(pl.multiple_of((2 * p + 1 - c) * r, 8), r), :]
                pltpu.make_async_remote_copy(
                    src_ref=there, dst_ref=land[cl].at[p, pl.ds(offs[w], r), :],
                    send_sem=send_sems.at[cl * 4 + p], recv_sem=recv_sems.at[cl * 4 + p],
                    device_id=sib, device_id_type=MESH).start()
        for cl in range(ncl):
            for p in range(4):
                rows_used = land[cl].at[p, pl.ds(0, used[cl]), :]
                slab = pltpu.make_async_remote_copy(
                    src_ref=rows_used, dst_ref=rows_used,
                    send_sem=send_sems.at[cl * 4 + p], recv_sem=recv_sems.at[cl * 4 + p],
                    device_id=sib, device_id_type=MESH)
                slab.wait_send()
                slab.wait_recv()

    return pl.pallas_call(
        body, name=name,
        in_specs=[ANY] * n, out_specs=[ANY] * ncl,
        out_shape=[_out((4, heights[k], cols[k]), BF16) for k in range(ncl)],
        scratch_shapes=[pltpu.SemaphoreType.DMA((4 * ncl,))] * 2,
    )(*[_hbm(g) for g in grads])


def _sum_devices(g, *, name):
    R = g.shape[1]

    def body(g_ref, o_ref):
        acc = g_ref[0]
        for d in range(1, N_DEV):
            acc = acc + g_ref[d]
        o_ref[...] = acc

    vm = pl.BlockSpec(memory_space=pltpu.VMEM)
    return pl.pallas_call(
        body, name=name, in_specs=[vm], out_specs=vm,
        out_shape=jax.ShapeDtypeStruct((R, LANE), F32),
        compiler_params=pltpu.CompilerParams(vmem_limit_bytes=VMEM_FLOOR),
    )(g)


def _rope_tables(positions):
    half = QK_ROPE // 2
    inv_freq = ROPE_BASE ** (-jnp.arange(half, dtype=F32) / half)
    ang = positions.astype(F32)[:, None] * inv_freq
    cos, sin = jnp.cos(ang), jnp.sin(ang)
    z = jnp.zeros_like(cos)
    z2 = jnp.zeros((positions.shape[0], LANE - QK_ROPE), F32)
    rc = jnp.concatenate([cos, cos, z2], axis=1)
    rs1 = jnp.concatenate([-sin, z, z2], axis=1)
    rs2 = jnp.concatenate([z, sin, z2], axis=1)
    return rc, rs1, rs2


def _block_diag(pool_w):
    G, pg, _ = pool_w.shape
    out = jnp.zeros((G * pg, G * pg), pool_w.dtype)
    for g in range(G):
        out = lax.dynamic_update_slice(out, pool_w[g], (g * pg, g * pg))
    return out


def kernel(x, mem, positions, ln_g, ln_b, ffn1_w13, ffn1_w2, w_in, pool_w, pool_scale, q_norm_g, w_uq, kv_norm_g, w_ukv, w_out, mem_wq, mem_wkv, mem_wo, ffn2_w13, ffn2_w2, loss_target, m_ln_g, m_ln_b, m_ffn1_w13, m_ffn1_w2, m_w_in, m_pool_w, m_pool_scale, m_q_norm_g, m_w_uq, m_kv_norm_g, m_w_ukv, m_w_out, m_mem_wq, m_mem_wkv, m_mem_wo, m_ffn2_w13, m_ffn2_w2, v_ln_g, v_ln_b, v_ffn1_w13, v_ffn1_w2, v_w_in, v_pool_w, v_pool_scale, v_q_norm_g, v_w_uq, v_kv_norm_g, v_w_ukv, v_w_out, v_mem_wq, v_mem_wkv, v_mem_wo, v_ffn2_w13, v_ffn2_w2):
    L = ln_g.shape[0]
    T, D = x.shape[1], x.shape[2]
    F = ffn1_w2.shape[1] * N_DEV
    PW = D // 4
    H = (D - PW) // V_HEAD
    DIN = w_in.shape[2]
    DINP = PW + Q_LORA + KV_LORA + LANE
    QW = QK_NOPE + QK_ROPE
    alpha = (2 * L) ** 0.25
    x2d = x.reshape(T, D)
    memb = mem.reshape(mem.shape[1], D).astype(BF16)
    target = loss_target.reshape(T, D)
    tabs = _rope_tables(positions.reshape(T))

    def shards_of(l):
        return dict(
            w13a=ffn1_w13[l].T[None].astype(BF16),
            w13b=ffn2_w13[l].T[None].astype(BF16),
            w2a=ffn1_w2[l][None].astype(BF16),
            w2b=ffn2_w2[l][None].astype(BF16),
            wsq=jnp.stack([w_out[l], mem_wq[l], mem_wo[l]]).astype(BF16),
            wkvT=mem_wkv[l].T[None].astype(BF16),
            winp=jnp.pad(w_in[l], ((0, 0), (0, DINP - DIN)))[None].astype(BF16),
            wuqT=w_uq[l].T[None].astype(BF16),
            wukvT=w_ukv[l].T[None].astype(BF16),
        )

    SMALL = ("winp", "wuqT", "wukvT")
    shards = [shards_of(l) for l in range(L)]
    W = [dict() for _ in range(L)]

    def rider(spec):
        return _Gather([shards[l][n] for l, n in spec]) if spec else None

    def arrived(spec, arrays):
        for (l, n), a in zip(spec, arrays):
            if n in ("w13a", "w13b"):
                a = _interleave(a, 1)
            elif n == "wuqT":
                a = jnp.pad(a.reshape(H, QW, Q_LORA), ((0, 0), (0, HEAD_PAD - QW), (0, 0)))
                a = a.reshape(1, H * HEAD_PAD, Q_LORA)
            elif n == "ln":
                a = jnp.moveaxis(a.reshape(N_DEV, 2, L, 4, D // N_DEV), 0, 3).reshape(2, L, 4, D)
                LN["g"], LN["b"] = a[0], a[1]
            W[l][n] = a

    LN = {}
    shards[0]["ln"] = jnp.concatenate([ln_g.reshape(1, 4 * L, -1), ln_b.reshape(1, 4 * L, -1)], axis=1)
    spec0 = [(0, "w13a")]
    arrived(spec0, _exchange_alone(rider(spec0), name="ag_first"))
    wbd = [_block_diag(pool_w[l]).astype(BF16) for l in range(L)]

    def ffn_fwd(l, which, xres, xb, k, spec):
        ab = "ab"[which]
        h13, a, rode = _ffn_up(xb, W[l]["w13" + ab], 0, name=f"l{l}_ffn{which}_up", ride=rider(spec))
        arrived(spec, rode)
        y, xo, xob = _mm_ln(a, W[l]["w2" + ab], 0, xres, LN["g"][l,k:k + 1], LN["b"][l,k:k + 1], alpha=alpha, s=0.5,
                            name=f"l{l}_ffn{which}_y_ln{k}")
        return dict(xres=xres, xb=xb, h13=h13, a=a, y=y), xo, xob

    saved = []
    xres, xb = x2d, x2d.astype(BF16)
    for l in range(L):
        sv = {}
        more = l + 1 < L
        Wl = W[l]
        spec = ([(0, "w2a"), (0, "ln"), *[(0, n) for n in SMALL], (0, "wkvT")] if l == 0
                else [(l, "wsq"), (l, "wkvT")])
        sv["ffn1"], x1, x1b = ffn_fwd(l, 0, xres, xb, 0, spec)
        hin = _mm(x1b, Wl["winp"], lead=0, name=f"l{l}_hin")
        pscale = pool_scale[l].reshape(1, PW)
        gq, gkv = q_norm_g[l].reshape(1, Q_LORA), kv_norm_g[l].reshape(1, KV_LORA)
        qh, kh, vh, cqn, ckvn = _heads_fwd(hin, gq, gkv, tabs, Wl["wuqT"][0], Wl["wukvT"][0], H=H, pw=PW,
                                           name=f"l{l}_heads")
        spec = [(l, "w13b"), (l, "w2b")] + ([(0, "wsq")] if l == 0 else []) + ([(l + 1, "w13a")] if more else [])
        cat, lse, rode = _flash_fwd(qh, kh, vh, H=H, pw=PW, name=f"l{l}_flash", ride=rider(spec))
        arrived(spec, rode)
        cat = _pool_fwd(hin, wbd[l], pscale, cat, name=f"l{l}_pool")
        ymix, x2, x2b = _mm_ln(cat, Wl["wsq"], 0, x1, LN["g"][l,1:2], LN["b"][l,1:2], alpha=alpha, s=1.0,
                               name=f"l{l}_ymix_ln1")
        kvm = _mm(memb, Wl["wkvT"], lead=0, tb=True, name=f"l{l}_kvm")
        km, vm = kvm[:, :D], kvm[:, D:]
        om, qm = _mem_fwd(x2b, Wl["wsq"][1], km, vm, name=f"l{l}_memattn")
        ymem, x3, x3b = _mm_ln(om, Wl["wsq"], 2, x2, LN["g"][l,2:3], LN["b"][l,2:3], alpha=alpha, s=1.0,
                               name=f"l{l}_ymem_ln2")
        spec = [(l + 1, n) for n in ("w2a", *SMALL)] if more else []
        sv["ffn2"], x4, x4b = ffn_fwd(l, 1, x3, x3b, 3, spec)
        sv.update(x1=x1, x1b=x1b, hin=hin, pscale=pscale, gq=gq, gkv=gkv, cqn=cqn, ckvn=ckvn,
                  qh=qh, kh=kh, vh=vh, lse=lse, cat=cat, ymix=ymix, x2=x2, x2b=x2b, qm=qm, km=km, vm=vm,
                  om=om, ymem=ymem)
        saved.append(sv)
        xres, xb = x4, x4b


    gW = {}
    gS = {}

    def ln_of(l, k):
        sv = saved[l]
        x, y, s = {0: (sv["ffn1"]["xres"], sv["ffn1"]["y"], 0.5), 1: (sv["x1"], sv["ymix"], 1.0),
                   2: (sv["x2"], sv["ymem"], 1.0), 3: (sv["ffn2"]["xres"], sv["ffn2"]["y"], 0.5)}[k]
        return x, y, LN["g"][l, k:k + 1], s

    def dx_through_ln(a, b, lead, tb, add, into, name, xin=None, wkey=None):
        x, y, g, s = ln_of(*into)
        out = _mm_ln_bwd(a, b, lead, tb, add, x, y, g, alpha=alpha, s=s, name=name, xin=xin)
        gS[("ln_g", *into)], gS[("ln_b", *into)] = out[2], out[3]
        if xin is not None:
            gW[wkey] = out[4]
        return out[0], out[1]

    def ffn_bwd(l, which, sv, dxres, dyb, ride, into):
        tag = f"l{l}_ffn{which}"
        gW[("w2", which, l)] = _mm(sv["a"], dyb, ta=True, out_dtype=BF16, name=f"{tag}_dw2", tn=D)
        dh, rode = _ffn_down_bwd(dyb, W[l]["w2" + "ab"[which]], 0, sv["h13"], name=f"{tag}_dh", ride=ride)
        dw13 = _mm(dh, sv["xb"], ta=True, out_dtype=BF16, name=f"{tag}_dw13", tn=D)
        gW[("w13", which, l)] = _deinterleave(dw13, 0)
        w13 = W[l]["w13" + "ab"[which]]
        if into is not None:
            return dx_through_ln(dh, w13, 0, False, dxres, into, f"{tag}_dx"), rode
        last = rs_first_level(l, "c")
        dxn, got = _mm(dh, w13, lead=0, add=dxres, name=f"{tag}_dx", tn=D, ride=last["ex"])
        rs_last_level(last, got)
        return dxn, rode

    core = lax.axis_index("c").astype(jnp.int32).reshape(1)
    chip = (2 * lax.axis_index("x") + lax.axis_index("y")).astype(jnp.int32).reshape(1)
    gsh = {}

    def rs_first_level(l, group):
        keys, classes = {
            "a": ([("w13", 1, l), ("w2", 1, l), ("mem_wkv", l), ("mem_wq", l), ("mem_wo", l)], [0] * 5),
            "b": ([("w_out", l), ("w_in", l), ("w_uq", l), ("w_ukv", l)], [0, 1, 2, 3]),
            "c": ([("w13", 0, l), ("w2", 0, l)], [0, 0]),
        }[group]
        tag = f"l{l}{group}"
        garrs = []
        for key in keys:
            g = gW[key]
            if key[0] == "w_uq":
                g = g.reshape(H, HEAD_PAD, Q_LORA)[:, :QW, :].reshape(H * QW, Q_LORA)
            garrs.append(g)
        offs, used = _class_layout(garrs, classes)
        parts = list(_rs_to_sibling(garrs, classes, name=f"{tag}_rs_sibling"))
        for cl in range(len(parts)):
            mine = [w for w, c in enumerate(classes) if c == cl]
            parts[cl] = _pair_sum(core, [garrs[w] for w in mine], parts[cl], [offs[w] for w in mine],
                                  name=f"{tag}_rs_pair_sum{cl}")
        return dict(tag=tag, keys=keys, garrs=garrs, classes=classes, offs=offs, used=used, parts=parts,
                    ex=_ChipExchange(parts, used))

    def rs_last_level(st, gathered):
        sums = [_quad_sum(chip, p, a, u, name=f"{st['tag']}_rs_quad_sum{k}")
                for k, (p, a, u) in enumerate(zip(st["parts"], gathered, st["used"]))]
        for key, g, cl, off in zip(st["keys"], st["garrs"], st["classes"], st["offs"]):
            gsh[key] = sums[cl][off:off + g.shape[0] // N_DEV, :]

    top = (L - 1, 3)
    x_top, y_top, g_top, s_top = ln_of(*top)
    dxres, dyb, gS[("ln_g", *top)], gS[("ln_b", *top)], loss_blk = _loss_ln_bwd(
        x_top, y_top, g_top, xres, target, alpha=alpha, s=s_top, name="loss_ln_top_bwd")
    above = None
    for l in reversed(range(L)):
        sv = saved[l]
        Wl = W[l]
        (dxres, dyb), _ = ffn_bwd(l, 1, sv["ffn2"], dxres, dyb, None, (l, 2))
        dom, gW[("mem_wo", l)] = _mm_pair(dyb, Wl["wsq"], 2, sv["om"], dx_dtype=BF16, name=f"l{l}_dom_dwo")
        dqm, dkm, dvm = _mem_bwd(sv["qm"], sv["km"], sv["vm"], dom, name=f"l{l}_memattn_bwd")
        dxres, dyb = dx_through_ln(dqm, Wl["wsq"], 1, True, dxres, (l, 1), f"l{l}_dx2_dwq",
                                   xin=sv["x2b"], wkey=("mem_wq", l))
        dkvm = jnp.concatenate([dkm, dvm], axis=1).astype(BF16)
        gW[("mem_wkv", l)] = _mm(dkvm, memb, ta=True, out_dtype=BF16, name=f"l{l}_dwkv", tn=D)
        dcat, gW[("w_out", l)] = _mm_pair(dyb, Wl["wsq"], 0, sv["cat"], dx_dtype=F32, name=f"l{l}_dcat_dwout")
        riding = [rs_first_level(l, "a")] + ([above] if above else [])
        both = _Both([st["ex"] for st in riding])
        dqh, dkh, dvh, rode = _flash_bwd(sv["qh"], sv["kh"], sv["vh"], sv["cat"], dcat, sv["lse"], H=H, pw=PW,
                                         name=f"l{l}_flash_bwd", ride=both)
        for st, got in zip(riding, both.split(rode)):
            rs_last_level(st, got)
        dqraw, dkv, dkpe = _heads_bwd(dqh, dkh, dvh, tabs, H=H, name=f"l{l}_heads_bwd")
        dcq, gW[("w_uq", l)] = _mm_pair(dqraw, Wl["wuqT"], 0, sv["cqn"], dx_dtype=F32, wt=True,
                                        name=f"l{l}_dcq_dwuq")
        dckv, gW[("w_ukv", l)] = _mm_pair(dkv, Wl["wukvT"], 0, sv["ckvn"], dx_dtype=F32, wt=True,
                                          name=f"l{l}_dckv_dwukv")
        du, dwbd, dps = _pool_bwd(sv["hin"], dcat, wbd[l], sv["pscale"], name=f"l{l}_pool_bwd")
        dhin, dgq, dgkv = _norms_bwd(sv["hin"], sv["gq"], sv["gkv"], dcq, dckv, dkpe, du, pw=PW,
                                     name=f"l{l}_norms_bwd")
        pg = PW // len(POOL_WINDOWS)
        gS[("pool_w", l)] = jnp.stack([dwbd[g * pg:(g + 1) * pg, g * pg:(g + 1) * pg]
                                       for g in range(len(POOL_WINDOWS))])
        gS[("pool_scale", l)], gS[("q_norm_g", l)], gS[("kv_norm_g", l)] = dps, dgq, dgkv
        dxres, dyb = dx_through_ln(dhin, Wl["winp"], 0, True, dxres, (l, 0), f"l{l}_dx1_dwin",
                                   xin=sv["x1b"], wkey=("w_in", l))
        heads = rs_first_level(l, "b")
        riding = [heads["ex"]]
        if l == 0:
            small_keys = []
            for ll in range(L):
                small_keys += [("pool_w", ll), ("pool_scale", ll), ("q_norm_g", ll), ("kv_norm_g", ll)]
                small_keys += [("ln_g", ll, k) for k in range(4)] + [("ln_b", ll, k) for k in range(4)]
            flat = jnp.concatenate([loss_blk[0, :1]] + [gS[k].reshape(-1) for k in small_keys])
            n_small = flat.shape[0]
            rows = -(-n_small // (8 * LANE)) * 8
            flat = jnp.pad(flat, (0, rows * LANE - n_small)).reshape(1, rows, LANE)
            riding.append(_Gather([flat]))
        both = _Both(riding)
        below, rode = ffn_bwd(l, 0, sv["ffn1"], dxres, dyb, both, (l - 1, 3) if l > 0 else None)
        rode = both.split(rode)
        rs_last_level(heads, rode[0])
        if l > 0:
            dxres, dyb = below
            above = rs_first_level(l, "c")
    grad_x = below.reshape(1, T, D)

    red = _sum_devices(rode[1][0].reshape(N_DEV, rows, LANE), name="sum_small").reshape(-1)
    loss = red[0]
    gsm, pos = {}, 1
    for k in small_keys:
        size = math.prod(gS[k].shape)
        gsm[k] = red[pos:pos + size].reshape(gS[k].shape)
        pos += size

    me = 4 * lax.axis_index("x") + 2 * lax.axis_index("y") + lax.axis_index("c")
    dsh = D // N_DEV
    stack = lambda f: jnp.stack([f(l) for l in range(L)])
    g_ln_g = stack(lambda l: jnp.concatenate([gsm[("ln_g", l, k)] for k in range(4)], axis=0))
    g_ln_b = stack(lambda l: jnp.concatenate([gsm[("ln_b", l, k)] for k in range(4)], axis=0))
    swapped = {
        "ffn1_w13": stack(lambda l: gsh[("w13", 0, l)]),
        "ffn2_w13": stack(lambda l: gsh[("w13", 1, l)]),
        "w_in": stack(lambda l: gsh[("w_in", l)][:, :DIN].T),
        "w_uq": stack(lambda l: gsh[("w_uq", l)]),
        "w_ukv": stack(lambda l: gsh[("w_ukv", l)]),
    }
    swap = lambda a: jnp.swapaxes(a, 1, 2)
    grads = {
        "ln_g": lax.dynamic_slice_in_dim(g_ln_g, me * dsh, dsh, axis=2),
        "ln_b": lax.dynamic_slice_in_dim(g_ln_b, me * dsh, dsh, axis=2),
        "ffn1_w2": stack(lambda l: gsh[("w2", 0, l)]),
        "pool_w": stack(lambda l: gsm[("pool_w", l)]),
        "pool_scale": stack(lambda l: gsm[("pool_scale", l)].reshape(PW)),
        "q_norm_g": stack(lambda l: gsm[("q_norm_g", l)].reshape(Q_LORA)),
        "kv_norm_g": stack(lambda l: gsm[("kv_norm_g", l)].reshape(KV_LORA)),
        "w_out": stack(lambda l: gsh[("w_out", l)]),
        "mem_wq": stack(lambda l: gsh[("mem_wq", l)]),
        "mem_wkv": stack(lambda l: gsh[("mem_wkv", l)].T),
        "mem_wo": stack(lambda l: gsh[("mem_wo", l)]),
        "ffn2_w2": stack(lambda l: gsh[("w2", 1, l)]),
        **{nme: swap(g) for nme, g in swapped.items()},
    }

    names = ["ln_g", "ln_b", "ffn1_w13", "ffn1_w2", "w_in", "pool_w", "pool_scale", "q_norm_g", "w_uq",
             "kv_norm_g", "w_ukv", "w_out", "mem_wq", "mem_wkv", "mem_wo", "ffn2_w13", "ffn2_w2"]
    weights = dict(ln_g=ln_g, ln_b=ln_b, ffn1_w13=ffn1_w13, ffn1_w2=ffn1_w2, w_in=w_in, pool_w=pool_w,
                   pool_scale=pool_scale, q_norm_g=q_norm_g, w_uq=w_uq, kv_norm_g=kv_norm_g, w_ukv=w_ukv,
                   w_out=w_out, mem_wq=mem_wq, mem_wkv=mem_wkv, mem_wo=mem_wo, ffn2_w13=ffn2_w13,
                   ffn2_w2=ffn2_w2)
    ms = dict(ln_g=m_ln_g, ln_b=m_ln_b, ffn1_w13=m_ffn1_w13, ffn1_w2=m_ffn1_w2, w_in=m_w_in, pool_w=m_pool_w,
              pool_scale=m_pool_scale, q_norm_g=m_q_norm_g, w_uq=m_w_uq, kv_norm_g=m_kv_norm_g,
              w_ukv=m_w_ukv, w_out=m_w_out, mem_wq=m_mem_wq, mem_wkv=m_mem_wkv, mem_wo=m_mem_wo,
              ffn2_w13=m_ffn2_w13, ffn2_w2=m_ffn2_w2)
    vs = dict(ln_g=v_ln_g, ln_b=v_ln_b, ffn1_w13=v_ffn1_w13, ffn1_w2=v_ffn1_w2, w_in=v_w_in, pool_w=v_pool_w,
              pool_scale=v_pool_scale, q_norm_g=v_q_norm_g, w_uq=v_w_uq, kv_norm_g=v_kv_norm_g,
              w_ukv=v_w_ukv, w_out=v_w_out, mem_wq=v_mem_wq, mem_wkv=v_mem_wkv, mem_wo=v_mem_wo,
              ffn2_w13=v_ffn2_w13, ffn2_w2=v_ffn2_w2)
    deltas, new_m, new_v = [], [], []
    for nme in names:
        if nme in swapped:
            d, mn, vn = [swap(o) for o in _adamw(swap(weights[nme]), swapped[nme], swap(ms[nme]), swap(vs[nme]),
                                                 name=f"adamw_{nme}")]
        else:
            d, mn, vn = _adamw(weights[nme], grads[nme], ms[nme], vs[nme], name=f"adamw_{nme}")
        deltas.append(d)
        new_m.append(mn)
        new_v.append(vn)
    return (loss, grad_x, *[grads[nme] for nme in names], *deltas, *new_m, *new_v)
```

```python
import functools
import math

import jax
import jax.numpy as jnp
from jax import lax
from jax.experimental import pallas as pl
from jax.experimental.pallas import tpu as pltpu

F32 = jnp.float32
BF16 = jnp.bfloat16
MESH = pl.DeviceIdType.MESH

CHUNK = 64
MEM_HEADS = 4
POOL_WINDOWS = (2, 4, 8, 16)
QK_NOPE = 128
QK_ROPE = 64
V_HEAD = 128
Q_LORA = 256
KV_LORA = 128
ROPE_BASE = 10000.0
LN_EPS = 1e-5
RMS_EPS = 1e-6
NEG_INF = -1e30
ADAM_LR = 0.001
ADAM_B1 = 0.9
ADAM_B2 = 0.999
ADAM_EPS = 1e-08
ADAM_WD = 0.01
ADAM_STEP = 10

N_DEV = 8
LANE = 128
HEAD_PAD = 2 * LANE
POOL_HALO = 16
VMEM_CAP = 56 * 1024 * 1024
VMEM_FLOOR = 32 * 1024 * 1024


def _tile(n, pref, mult):
    t = (min(pref, n) // mult) * mult
    while t >= mult:
        if n % t == 0:
            return t
        t -= mult
    return n


def _params(sem, est_bytes):
    limit = int(min(max(2 * est_bytes + (8 << 20), VMEM_FLOOR), VMEM_CAP))
    return pltpu.CompilerParams(dimension_semantics=sem, vmem_limit_bytes=limit)


def _nbytes(shape, dtype):
    return math.prod(shape) * jnp.dtype(dtype).itemsize


def _hbm(x):
    return pltpu.with_memory_space_constraint(x, pltpu.HBM)


def _out(shape, dtype):
    return pltpu.HBM(tuple(shape), dtype)


def _dg(a, b, ca, cb):
    return lax.dot_general(a.astype(BF16), b.astype(BF16), (((ca,), (cb,)), ((), ())),
                           preferred_element_type=F32)


@jax.custom_vjp
def _bdot_nn(a, b):
    return _dg(a, b, 1, 0)


def _bdot_nn_fwd(a, b):
    return _dg(a, b, 1, 0), (a, b)


def _bdot_nn_bwd(res, ct):
    a, b = res
    return _dg(ct, b, 1, 1).astype(a.dtype), _dg(a, ct, 0, 0).astype(b.dtype)


_bdot_nn.defvjp(_bdot_nn_fwd, _bdot_nn_bwd)


@jax.custom_vjp
def _bdot_nt(a, b):
    return _dg(a, b, 1, 1)


def _bdot_nt_fwd(a, b):
    return _dg(a, b, 1, 1), (a, b)


def _bdot_nt_bwd(res, ct):
    a, b = res
    return _dg(ct, b, 1, 0).astype(a.dtype), _dg(ct, a, 0, 0).astype(b.dtype)


_bdot_nt.defvjp(_bdot_nt_fwd, _bdot_nt_bwd)


@functools.partial(jax.custom_vjp, nondiff_argnums=(1,))
def _lane_roll(x, shift):
    return pltpu.roll(x, shift % x.shape[1], axis=1)


def _lane_roll_fwd(x, shift):
    return _lane_roll(x, shift), None


def _lane_roll_bwd(shift, _, ct):
    return (_lane_roll(ct, -shift),)


_lane_roll.defvjp(_lane_roll_fwd, _lane_roll_bwd)


@functools.partial(jax.custom_vjp, nondiff_argnums=(1, 2))
def _cols(x, lo, hi):
    return x[:, lo:hi]


def _cols_fwd(x, lo, hi):
    return x[:, lo:hi], x.shape[1]


def _cols_bwd(lo, hi, width, ct):
    parts = []
    if lo > 0:
        parts.append(jnp.zeros((ct.shape[0], lo), ct.dtype))
    parts.append(ct)
    if hi < width:
        parts.append(jnp.zeros((ct.shape[0], width - hi), ct.dtype))
    return (jnp.concatenate(parts, axis=1) if len(parts) > 1 else ct,)


_cols.defvjp(_cols_fwd, _cols_bwd)


MM_VMEM_BUDGET = 22 * 1024 * 1024


def _mm(a, b, *, name, ta=False, tb=False, out_dtype=F32, lead=None, add=None, add_scale=1.0,
        tm=1024, tn=1024, tk=8192, ride=None):
    if ta:
        K, M = a.shape
    else:
        M, K = a.shape
    bshape = b.shape[1:] if lead is not None else b.shape
    if tb:
        N, Kb = bshape
    else:
        Kb, N = bshape
    assert K == Kb, (name, a.shape, b.shape)

    def blocks(tm, tn, tk):
        tm = _tile(M, tm, LANE if ta else 16)
        tn = _tile(N, tn, LANE)
        tk = _tile(K, tk, LANE)
        nbytes = (tm * tk * a.dtype.itemsize + tk * tn * b.dtype.itemsize
                  + tm * tn * (jnp.dtype(out_dtype).itemsize + (4 if K // tk > 1 else 0)
                               + (add.dtype.itemsize if add is not None else 0)))
        return tm, tn, tk, nbytes

    if ta:
        tm = min(tm, max(LANE, M // 4))
    tm, tn, tk, est = blocks(tm, tn, tk)
    for shrink in ("m", "k", "m", "k", "n"):
        if est <= MM_VMEM_BUDGET:
            break
        if shrink == "m":
            tm, tn, tk, est = blocks(max(tm // 2, LANE), tn, tk)
        elif shrink == "k":
            tm, tn, tk, est = blocks(tm, tn, max(tk // 2, LANE))
        else:
            tm, tn, tk, est = blocks(tm, max(tn // 2, LANE), tk)
    nk = K // tk
    ca = 0 if ta else 1
    cb = 1 if tb else 0

    def body(*refs):
        a_ref, b_ref = refs[0], refs[1]
        add_ref = refs[2] if add is not None else None
        o_ref = refs[3] if add is not None else refs[2]

        def finish(r):
            if add_ref is not None:
                r = r + add_scale * add_ref[...].astype(F32)
            o_ref[...] = r.astype(o_ref.dtype)

        if nk == 1:
            finish(_dg(a_ref[...], b_ref[...], ca, cb))
            return
        acc_ref = refs[-1]
        k = pl.program_id(2)

        @pl.when(k == 0)
        def _():
            acc_ref[...] = jnp.zeros_like(acc_ref)

        acc_ref[...] += _dg(a_ref[...], b_ref[...], ca, cb)

        @pl.when(k == nk - 1)
        def _():
            finish(acc_ref[...])

    a_blk = (tk, tm) if ta else (tm, tk)
    a_map = (lambda i, j, k: (k, i)) if ta else (lambda i, j, k: (i, k))
    b_blk = (tn, tk) if tb else (tk, tn)
    if lead is None:
        b_map = (lambda i, j, k: (j, k)) if tb else (lambda i, j, k: (k, j))
        b_spec = pl.BlockSpec(b_blk, b_map)
    else:
        b_map = (lambda i, j, k: (lead, j, k)) if tb else (lambda i, j, k: (lead, k, j))
        b_spec = pl.BlockSpec((None,) + b_blk, b_map)
    in_specs = [pl.BlockSpec(a_blk, a_map), b_spec]
    args = [a, b]
    if add is not None:
        in_specs.append(pl.BlockSpec((tm, tn), lambda i, j, k: (i, j)))
        args.append(add)
    (out,), rode = _host_call(
        body, name=name,
        grid=(M // tm, N // tn, nk),
        in_specs=in_specs,
        out_specs=[pl.BlockSpec((tm, tn), lambda i, j, k: (i, j))],
        out_shape=[_out((M, N), out_dtype)],
        scratch=[pltpu.VMEM((tm, tn), F32)] if nk > 1 else [],
        args=[_hbm(v) for v in args], sem=("parallel", "parallel", "arbitrary"), est=est + tm * tn * 4,
        ride=ride)
    return out if ride is None else (out, rode)


def _mm_pair(dy, w, lead, x, *, dx_dtype, name, wt=False):
    M, N = dy.shape
    Kx = x.shape[1]
    wshape = (N, Kx) if wt else (Kx, N)
    assert w.shape[1:] == wshape, (name, w.shape, x.shape, dy.shape)
    tm = _tile(M, 512, 16)
    nt = M // tm

    def body(dy_ref, w_ref, x_ref, dx_ref, dw_ref, acc_ref):
        i = pl.program_id(0)

        @pl.when(i == 0)
        def _():
            acc_ref[...] = jnp.zeros_like(acc_ref)

        dyv = dy_ref[...]
        dx_ref[...] = _dg(dyv, w_ref[...], 1, 0 if wt else 1).astype(dx_ref.dtype)
        acc_ref[...] += _dg(dyv, x_ref[...], 0, 0) if wt else _dg(x_ref[...], dyv, 0, 0)

        @pl.when(i == nt - 1)
        def _():
            dw_ref[...] = acc_ref[...].astype(dw_ref.dtype)

    est = tm * (N + 2 * Kx) * 4 + Kx * N * (2 + 4 + 2)
    return pl.pallas_call(
        body, name=name, grid=(nt,),
        in_specs=[pl.BlockSpec((tm, N), lambda i: (i, 0)),
                  pl.BlockSpec((None,) + wshape, lambda i: (lead, 0, 0)),
                  pl.BlockSpec((tm, Kx), lambda i: (i, 0))],
        out_specs=[pl.BlockSpec((tm, Kx), lambda i: (i, 0)), pl.BlockSpec(wshape, lambda i: (0, 0))],
        out_shape=[_out((M, Kx), dx_dtype), _out(wshape, BF16)],
        scratch_shapes=[pltpu.VMEM(wshape, F32)],
        compiler_params=_params(("arbitrary",), est),
    )(_hbm(dy), _hbm(w), _hbm(x))


def _rowwise(fn, tiles, params, tile_outs, acc_outs=(), *, tm, name):
    tile_arrays, tile_specs = [], []
    for t in tiles:
        if isinstance(t, tuple):
            tile_arrays.append(t[0])
            tile_specs.append(t[1])
        else:
            tile_arrays.append(t)
            tile_specs.append(pl.BlockSpec((tm, t.shape[1]), lambda i: (i, 0)))
    T = tile_arrays[0].shape[0]
    nt, np_, nto, nao = len(tile_arrays), len(params), len(tile_outs), len(acc_outs)

    def body(*refs):
        i = pl.program_id(0)
        tvals = [r[...] for r in refs[:nt]]
        pvals = [r[...] for r in refs[nt:nt + np_]]
        to_refs = refs[nt + np_:nt + np_ + nto]
        ao_refs = refs[nt + np_ + nto:]
        touts, aouts = fn(i, tvals, pvals)
        for r, v in zip(to_refs, touts):
            r[...] = v.astype(r.dtype)
        if nao:
            @pl.when(i == 0)
            def _():
                for r in ao_refs:
                    r[...] = jnp.zeros_like(r)
            for r, v in zip(ao_refs, aouts):
                r[...] += v.astype(r.dtype)

    in_specs = tile_specs + [pl.BlockSpec(p.shape, lambda i: (0, 0)) for p in params]
    out_specs = [pl.BlockSpec((tm, c), lambda i: (i, 0)) for c, _ in tile_outs]
    out_specs += [pl.BlockSpec(s, lambda i: (0, 0)) for s, _ in acc_outs]
    out_shape = [_out((T, c), d) for c, d in tile_outs]
    out_shape += [_out(s, d) for s, d in acc_outs]
    width = sum(s.block_shape[-1] for s in tile_specs) + sum(c for c, _ in tile_outs)
    est = 6 * tm * width * 4 + sum(_nbytes(p.shape, F32) for p in params) * 4
    return pl.pallas_call(
        body, name=name, grid=(T // tm,),
        in_specs=in_specs, out_specs=out_specs, out_shape=out_shape,
        compiler_params=_params(("arbitrary",) if nao else ("parallel",), est),
    )(*[_hbm(v) for v in tile_arrays], *[_hbm(p) for p in params])


def _ln_fn(alpha, s, xres, y, g, b):
    z = alpha * xres.astype(F32) + s * y.astype(F32)
    mu = jnp.mean(z, axis=-1, keepdims=True)
    zc = z - mu
    var = jnp.mean(zc * zc, axis=-1, keepdims=True)
    return zc * lax.rsqrt(var + LN_EPS) * g + b


def _mm_ln(a, b, lead, xres, g, bias, *, alpha, s, name, nxt=None):
    M, K = a.shape
    N = b.shape[2]
    tm = _tile(M, 512, 16)

    def body(*refs):
        a_ref, b_ref, x_ref, g_ref, bias_ref = refs[:5]
        y_ref, xo_ref, xb_ref = refs[-3 - (nxt is not None):][:3]
        y = _dg(a_ref[...], b_ref[...], 1, 0)
        y_ref[...] = y.astype(y_ref.dtype)
        out = _ln_fn(alpha, s, x_ref[...], y, g_ref[...], bias_ref[...])
        xo_ref[...] = out
        xb_ref[...] = out.astype(BF16)
        if nxt is not None:
            refs[-1][...] = _dg(out, refs[5][...], 1, 0)

    row = pl.BlockSpec((tm, N), lambda i: (i, 0))
    vec = pl.BlockSpec((1, N), lambda i: (0, 0))
    est = tm * K * 2 + K * N * 2 + tm * N * (4 + 4 + 4 + 2 + 8)
    in_specs = [pl.BlockSpec((tm, K), lambda i: (i, 0)), pl.BlockSpec((None, K, N), lambda i: (lead, 0, 0)),
                row, vec, vec]
    out_specs = [row, row, row]
    out_shape = [_out((M, N), BF16), _out((M, N), F32), _out((M, N), BF16)]
    args = [a, b, xres, g, bias]
    if nxt is not None:
        P = nxt.shape[1]
        in_specs.append(pl.BlockSpec((N, P), lambda i: (0, 0)))
        args.append(nxt)
        out_specs.append(pl.BlockSpec((tm, P), lambda i: (i, 0)))
        out_shape.append(_out((M, P), F32))
        est += N * P * 2 + tm * P * 8
    return pl.pallas_call(
        body, name=name, grid=(M // tm,),
        in_specs=in_specs, out_specs=out_specs, out_shape=out_shape,
        compiler_params=_params(("parallel",), est),
    )(*[_hbm(v) for v in args])


def _ln_bwd_math(alpha, s, x, y, g, d):
    z = alpha * x + s * y.astype(F32)
    zc = z - jnp.mean(z, axis=-1, keepdims=True)
    r = lax.rsqrt(jnp.mean(zc * zc, axis=-1, keepdims=True) + LN_EPS)
    xh = zc * r
    dxh = d * g
    dz = r * (dxh - jnp.mean(dxh, axis=-1, keepdims=True) - xh * jnp.mean(dxh * xh, axis=-1, keepdims=True))
    return alpha * dz, s * dz, jnp.sum(d * xh, axis=0, keepdims=True), jnp.sum(d, axis=0, keepdims=True)


def _mm_ln_bwd(a, b, lead, tb, add, xres, y, g, *, alpha, s, name, xin=None):
    M, K = a.shape
    N = b.shape[1] if tb else b.shape[2]
    tk = K if K * N * 2 <= MM_VMEM_BUDGET * 3 // 5 else _tile(K, 2816, LANE)
    tm = _tile(M, 512 if K * N * 2 <= MM_VMEM_BUDGET // 4 else 256, 16)
    nk = K // tk
    nt = M // tm
    cb = 1 if tb else 0
    assert xin is None or (tb and nk == 1), name

    def body(*refs):
        a_ref, b_ref, add_ref, x_ref, y_ref, g_ref = refs[:6]
        rest = refs[6:]
        if xin is not None:
            xin_ref, rest = rest[0], rest[1:]
        dx_ref, dy_ref, dg_ref, db_ref = rest[:4]
        scratch = rest[4:]
        i, k = pl.program_id(0), pl.program_id(1)

        def finish(d):
            @pl.when(i == 0)
            def _():
                dg_ref[...] = jnp.zeros_like(dg_ref)
                db_ref[...] = jnp.zeros_like(db_ref)

            dx, dy, dg, db = _ln_bwd_math(alpha, s, x_ref[...], y_ref[...], g_ref[...], d + add_ref[...])
            dx_ref[...] = dx
            dy_ref[...] = dy.astype(dy_ref.dtype)
            dg_ref[...] += dg
            db_ref[...] += db

        if xin is not None:
            dw_ref, accw_ref = scratch[0], scratch[1]

            @pl.when(i == 0)
            def _():
                accw_ref[...] = jnp.zeros_like(accw_ref)

            accw_ref[...] += _dg(xin_ref[...], a_ref[...], 0, 0)

            @pl.when(i == nt - 1)
            def _():
                dw_ref[...] = accw_ref[...].astype(dw_ref.dtype)

        if nk == 1:
            finish(_dg(a_ref[...], b_ref[...], 1, cb))
            return
        acc_ref = scratch[0]

        @pl.when(k == 0)
        def _():
            acc_ref[...] = jnp.zeros_like(acc_ref)

        acc_ref[...] += _dg(a_ref[...], b_ref[...], 1, cb)

        @pl.when(k == nk - 1)
        def _():
            finish(acc_ref[...])

    row = pl.BlockSpec((tm, N), lambda i, k: (i, 0))
    vec = pl.BlockSpec((1, N), lambda i, k: (0, 0))
    b_spec = (pl.BlockSpec((None, N, tk), lambda i, k: (lead, 0, k)) if tb
              else pl.BlockSpec((None, tk, N), lambda i, k: (lead, k, 0)))
    est = tm * tk * 2 + tk * N * 2 + tm * N * (4 + 4 + 2 + 4 + 2 + 4 + 12)
    in_specs = [pl.BlockSpec((tm, tk), lambda i, k: (i, k)), b_spec, row, row, row, vec]
    out_specs = [row, row, vec, vec]
    out_shape = [_out((M, N), F32), _out((M, N), BF16), _out((1, N), F32), _out((1, N), F32)]
    scratch = [pltpu.VMEM((tm, N), F32)] if nk > 1 else []
    args = [a, b, add, xres, y, g]
    if xin is not None:
        in_specs.append(row)
        args.append(xin)
        out_specs.append(pl.BlockSpec((N, K), lambda i, k: (0, 0)))
        out_shape.append(_out((N, K), BF16))
        scratch.append(pltpu.VMEM((N, K), F32))
        est += N * K * 8 + tm * N * 2
    return pl.pallas_call(
        body, name=name, grid=(nt, nk),
        in_specs=in_specs, out_specs=out_specs, out_shape=out_shape, scratch_shapes=scratch,
        compiler_params=_params(("arbitrary", "arbitrary"), est),
    )(*[_hbm(v) for v in args])


def _loss_ln_bwd(xres, y, g, out, target, *, alpha, s, name):
    T, D = xres.shape
    tm = _tile(T, 256, 16)

    def body(x_ref, y_ref, o_ref, t_ref, g_ref, dx_ref, dy_ref, dg_ref, db_ref, loss_ref):
        @pl.when(pl.program_id(0) == 0)
        def _():
            dg_ref[...] = jnp.zeros_like(dg_ref)
            db_ref[...] = jnp.zeros_like(db_ref)
            loss_ref[...] = jnp.zeros_like(loss_ref)

        err = o_ref[...] - t_ref[...]
        part = 0.5 * jnp.sum(jnp.sum(err * err, axis=1, keepdims=True) / D, axis=0, keepdims=True)
        loss_ref[...] += jnp.broadcast_to(part, loss_ref.shape)
        dx, dy, dg, db = _ln_bwd_math(alpha, s, x_ref[...], y_ref[...], g_ref[...], err / D)
        dx_ref[...] = dx
        dy_ref[...] = dy.astype(dy_ref.dtype)
        dg_ref[...] += dg
        db_ref[...] += db

    row = pl.BlockSpec((tm, D), lambda i: (i, 0))
    vec = pl.BlockSpec((1, D), lambda i: (0, 0))
    return pl.pallas_call(
        body, name=name, grid=(T // tm,),
        in_specs=[row, row, row, row, vec],
        out_specs=[row, row, vec, vec, pl.BlockSpec((8, LANE), lambda i: (0, 0))],
        out_shape=[_out((T, D), F32), _out((T, D), BF16), _out((1, D), F32), _out((1, D), F32),
                   _out((8, LANE), F32)],
        compiler_params=_params(("arbitrary",), 14 * tm * D * 4),
    )(_hbm(xres), _hbm(y), _hbm(out), _hbm(target), _hbm(g))


FFN_TILE = 256


def _interleave(w, axis):
    n = w.shape[axis] // (2 * FFN_TILE)
    shp = w.shape[:axis] + (2, n, FFN_TILE) + w.shape[axis + 1:]
    return jnp.swapaxes(w.reshape(shp), axis, axis + 1).reshape(w.shape)


def _deinterleave(w, axis):
    n = w.shape[axis] // (2 * FFN_TILE)
    shp = w.shape[:axis] + (n, 2, FFN_TILE) + w.shape[axis + 1:]
    return jnp.swapaxes(w.reshape(shp), axis, axis + 1).reshape(w.shape)


def _ffn_up(xb, w13t, lead, *, name, ride=None):
    T, D = xb.shape
    F = w13t.shape[1] // 2
    tc = FFN_TILE
    tm = _tile(T, 2048, 16)

    def body(x_ref, w_ref, h_ref, a_ref):
        h = _dg(x_ref[...], w_ref[...], 1, 1)
        g, u = h[:, :tc], h[:, tc:]
        h_ref[...] = h.astype(h_ref.dtype)
        a_ref[...] = (g * jax.nn.sigmoid(g) * u).astype(a_ref.dtype)

    est = (tm * D + 2 * tc * D + 3 * tm * tc) * 2 + 3 * tm * tc * 4
    (h13, a), gathered = _host_call(
        body, name=name, grid=(T // tm, F // tc),
        in_specs=[pl.BlockSpec((tm, D), lambda i, j: (i, 0)),
                  pl.BlockSpec((None, 2 * tc, D), lambda i, j: (lead, j, 0))],
        out_specs=[pl.BlockSpec((tm, 2 * tc), lambda i, j: (i, j)),
                   pl.BlockSpec((tm, tc), lambda i, j: (i, j))],
        out_shape=[_out((T, 2 * F), BF16), _out((T, F), BF16)],
        args=[_hbm(xb), _hbm(w13t)], sem=("parallel", "parallel"), est=est, ride=ride)
    return h13, a, gathered


def _ffn_down_bwd(dyb, w2, lead, h13, *, name, ride=None):
    T, D = dyb.shape
    F = w2.shape[1]
    tc = FFN_TILE
    tm = _tile(T, 2048, 16)

    def body(dy_ref, w_ref, h_ref, dh_ref):
        d = _dg(dy_ref[...], w_ref[...], 1, 1)
        h = h_ref[...].astype(F32)
        g, u = h[:, :tc], h[:, tc:]
        sig = jax.nn.sigmoid(g)
        gs = g * sig
        dh_ref[...] = jnp.concatenate([d * u * (sig + gs * (1.0 - sig)), d * gs], axis=1).astype(dh_ref.dtype)

    est = (tm * D + tc * D + 4 * tm * tc) * 2 + 6 * tm * tc * 4
    (dh,), rode = _host_call(
        body, name=name, grid=(T // tm, F // tc),
        in_specs=[pl.BlockSpec((tm, D), lambda i, j: (i, 0)),
                  pl.BlockSpec((None, tc, D), lambda i, j: (lead, j, 0)),
                  pl.BlockSpec((tm, 2 * tc), lambda i, j: (i, j))],
        out_specs=[pl.BlockSpec((tm, 2 * tc), lambda i, j: (i, j))],
        out_shape=[_out((T, 2 * F), BF16)],
        args=[_hbm(dyb), _hbm(w2), _hbm(h13)], sem=("parallel", "parallel"), est=est, ride=ride)
    return dh, rode


def _pool_select(parts, pw):
    pg = pw // len(POOL_WINDOWS)
    grp = lax.broadcasted_iota(jnp.int32, parts[0].shape, 1) // pg
    out = parts[3]
    for g in (2, 1, 0):
        out = jnp.where(grp == g, parts[g], out)
    return out


def _pool_count(t0, rows, pw):
    pg = pw // len(POOL_WINDOWS)
    grp = lax.broadcasted_iota(jnp.int32, (rows, pw), 1) // pg
    win = jnp.where(grp == 0, POOL_WINDOWS[0],
                    jnp.where(grp == 1, POOL_WINDOWS[1],
                              jnp.where(grp == 2, POOL_WINDOWS[2], POOL_WINDOWS[3])))
    t = t0 + lax.broadcasted_iota(jnp.int32, (rows, pw), 0)
    return jnp.minimum(t + 1, win).astype(F32)


def _window_sums(ext, up):
    n = ext.shape[0]
    sums, cur, k = [], ext, 1
    for _ in POOL_WINDOWS:
        cur = cur + pltpu.roll(cur, (n - k) if up else k, axis=0)
        sums.append(cur)
        k *= 2
    return sums


def _pool_delta(u, halo, t0):
    tm, pw = u.shape
    ext = jnp.concatenate([halo, u], axis=0)
    sums = [s[POOL_HALO:, :] for s in _window_sums(ext, up=False)]
    return _pool_select(sums, pw) / _pool_count(t0, tm, pw) - u


def _pool_fwd(hin, wbd, scale, cat, *, name):
    T = hin.shape[0]
    pw = wbd.shape[0]
    tm = _tile(T, 256, POOL_HALO)
    per = tm // POOL_HALO

    def body(u_ref, halo_ref, w_ref, s_ref, cat_ref, y_ref):
        i = pl.program_id(0)
        halo = jnp.where(i > 0, halo_ref[...], 0.0)
        d = _pool_delta(u_ref[...], halo, i * tm)
        y_ref[...] = (_dg(d, w_ref[...], 1, 0) * s_ref[...]).astype(y_ref.dtype)

    return pl.pallas_call(
        body, name=name, grid=(T // tm,),
        in_specs=[pl.BlockSpec((tm, pw), lambda i: (i, 0)),
                  pl.BlockSpec((POOL_HALO, pw), lambda i: (jnp.maximum(i * per - 1, 0), 0)),
                  pl.BlockSpec((pw, pw), lambda i: (0, 0)),
                  pl.BlockSpec((1, pw), lambda i: (0, 0)),
                  ANY],
        out_specs=pl.BlockSpec((tm, pw), lambda i: (i, 0)),
        out_shape=_out(cat.shape, cat.dtype),
        input_output_aliases={4: 0},
        compiler_params=_params(("parallel",), 16 * tm * pw * 4),
    )(_hbm(hin), _hbm(hin), _hbm(wbd), _hbm(scale), _hbm(cat))


def _pool_bwd(hin, dcat, wbd, scale, *, name):
    T = hin.shape[0]
    pw = wbd.shape[0]
    tm = _tile(T, 256, POOL_HALO)
    per = tm // POOL_HALO
    nt = T // tm

    def body(u_ref, halo_ref, dy_ref, dyn_ref, w_ref, s_ref, du_ref, dw_ref, ds_ref):
        i = pl.program_id(0)

        @pl.when(i == 0)
        def _():
            dw_ref[...] = jnp.zeros_like(dw_ref)
            ds_ref[...] = jnp.zeros_like(ds_ref)

        halo = jnp.where(i > 0, halo_ref[...], 0.0)
        d = _pool_delta(u_ref[...], halo, i * tm)
        w = w_ref[...]
        sc = s_ref[...]
        dy = dy_ref[...]
        dyn = jnp.where(i < nt - 1, dyn_ref[...], 0.0)
        ds_ref[...] += jnp.sum(dy * _dg(d, w, 1, 0), axis=0, keepdims=True)
        dys = dy * sc
        dw_ref[...] += _dg(d, dys, 0, 0)
        dys_ext = jnp.concatenate([dys, dyn * sc], axis=0)
        dd_ext = _dg(dys_ext, w, 1, 1)
        ddp = dd_ext / _pool_count(i * tm, tm + POOL_HALO, pw)
        sums = [s[:tm, :] for s in _window_sums(ddp, up=True)]
        du_ref[...] = _pool_select(sums, pw) - dd_ext[:tm, :]

    return pl.pallas_call(
        body, name=name, grid=(nt,),
        in_specs=[pl.BlockSpec((tm, pw), lambda i: (i, 0)),
                  pl.BlockSpec((POOL_HALO, pw), lambda i: (jnp.maximum(i * per - 1, 0), 0)),
                  pl.BlockSpec((tm, pw), lambda i: (i, 0)),
                  pl.BlockSpec((POOL_HALO, pw), lambda i: (jnp.minimum((i + 1) * per, nt * per - 1), 0)),
                  pl.BlockSpec((pw, pw), lambda i: (0, 0)),
                  pl.BlockSpec((1, pw), lambda i: (0, 0))],
        out_specs=[pl.BlockSpec((tm, pw), lambda i: (i, 0)),
                   pl.BlockSpec((pw, pw), lambda i: (0, 0)),
                   pl.BlockSpec((1, pw), lambda i: (0, 0))],
        out_shape=[_out((T, pw), F32),
                   _out((pw, pw), F32),
                   _out((1, pw), F32)],
        compiler_params=_params(("arbitrary",), 24 * tm * pw * 4),
    )(_hbm(hin), _hbm(hin), _hbm(dcat), _hbm(dcat), _hbm(wbd), _hbm(scale))


def _rms(x, g):
    return x * lax.rsqrt(jnp.mean(x * x, axis=-1, keepdims=True) + RMS_EPS) * g


def _norms_fn(pw, h, gq, gkv):
    o1 = pw + Q_LORA
    o2 = o1 + KV_LORA
    return (_rms(_cols(h, pw, o1), gq), _rms(_cols(h, o1, o2), gkv), _cols(h, o2, h.shape[1]))


def _norms_bwd(hin, gq, gkv, dcq, dckv, dkpe, du, *, pw, name):
    tm = _tile(hin.shape[0], 256, 16)
    dinp = hin.shape[1]

    def fn(i, tv, pv):
        _, vjp = jax.vjp(functools.partial(_norms_fn, pw), tv[0], pv[0], pv[1])
        dh, dgq, dgkv = vjp((tv[1].astype(F32), tv[2].astype(F32), tv[3].astype(F32)))
        dh = jnp.concatenate([tv[4], dh[:, pw:]], axis=1)
        return (dh,), (dgq, dgkv)

    return _rowwise(fn, [hin, dcq, dckv, dkpe, du], [gq, gkv], [(dinp, BF16)],
                    [((1, Q_LORA), F32), ((1, KV_LORA), F32)], tm=tm, name=name)


def _heads_fn(H, qraw, kv, kpe, rc, rs1, rs2):
    half = QK_ROPE // 2
    scale = (QK_NOPE + QK_ROPE) ** -0.5

    def rope(blk):
        return blk * rc + _lane_roll(blk, -half) * rs1 + _lane_roll(blk, half) * rs2

    krot = rope(kpe)
    qs, ks, vs = [], [], []
    for h in range(H):
        lo = h * HEAD_PAD
        qs += [_cols(qraw, lo, lo + LANE) * scale, rope(_cols(qraw, lo + LANE, lo + HEAD_PAD)) * scale]
        ks += [_cols(kv, lo, lo + LANE), krot]
        vs += [_cols(kv, lo + LANE, lo + HEAD_PAD)]
    return jnp.concatenate(qs, axis=1), jnp.concatenate(ks, axis=1), jnp.concatenate(vs, axis=1)


def _heads_fwd(hin, gq, gkv, tabs, wuq, wukv, *, H, pw, name):
    tm = _tile(hin.shape[0], 256, 16)

    def fn(i, tv, pv):
        cqn, ckvn, kpe = _norms_fn(pw, tv[0], pv[0], pv[1])
        qraw = _dg(cqn, pv[2], 1, 1)
        kv = _dg(ckvn, pv[3], 1, 1)
        return (*_heads_fn(H, qraw, kv, kpe, *tv[1:]), cqn, ckvn), ()

    return _rowwise(fn, [hin, *tabs], [gq, gkv, wuq, wukv],
                    [(H * HEAD_PAD, BF16), (H * HEAD_PAD, BF16), (H * V_HEAD, BF16), (Q_LORA, BF16),
                     (KV_LORA, BF16)], tm=tm, name=name)


def _heads_bwd(dq, dk, dv, tabs, *, H, name):
    tm = _tile(dq.shape[0], 256, 16)

    def fn(i, tv, pv):
        z = jnp.zeros((tm, H * HEAD_PAD), F32)
        zk = jnp.zeros((tm, LANE), F32)
        rc, rs1, rs2 = tv[3], tv[4], tv[5]
        _, vjp = jax.vjp(lambda a, b, c: _heads_fn(H, a, b, c, rc, rs1, rs2), z, z, zk)
        return vjp((tv[0].astype(F32), tv[1].astype(F32), tv[2].astype(F32))), ()

    return _rowwise(fn, [dq, dk, dv, *tabs], [],
                    [(H * HEAD_PAD, BF16), (H * HEAD_PAD, BF16), (LANE, F32)], tm=tm, name=name)


def _diag_mask(rows, cols, row0):
    r = (row0 + lax.broadcasted_iota(jnp.int32, (rows, cols), 0)) // CHUNK
    c = lax.broadcasted_iota(jnp.int32, (rows, cols), 1) // CHUNK
    return r >= c


def _flash_fwd(qh, kh, vh, *, H, pw, name, ride=None):
    T = qh.shape[0]
    t = _tile(T, 512, CHUNK)
    off = pw // V_HEAD


    half = t

    def body(q_ref, k_ref, v_ref, o_ref, lse_ref):
        i = pl.program_id(1)
        q = q_ref[...]

        def update(carry, s, v):
            m, l, acc = carry
            mn = jnp.maximum(m, jnp.max(s, axis=1, keepdims=True))
            p = jnp.exp(s - mn)
            corr = jnp.exp(m - mn)
            return mn, corr * l + jnp.sum(p, axis=1, keepdims=True), corr * acc + _dg(p, v, 1, 0)

        def blk(j, carry):
            rows = pl.ds(pl.multiple_of(j * t, t), t)
            return update(carry, _dg(q, k_ref[rows, :], 1, 1), v_ref[rows, :])

        init = (jnp.full((t, 1), NEG_INF, F32), jnp.zeros((t, 1), F32), jnp.zeros((t, V_HEAD), F32))
        carry = lax.fori_loop(0, i, blk, init)
        done = []
        for r0 in range(0, t, half):
            keys = pl.ds(pl.multiple_of(i * t, t), r0 + half)
            s = _dg(q[r0:r0 + half, :], k_ref[keys, :], 1, 1)
            s = jnp.where(_diag_mask(half, r0 + half, r0), s, NEG_INF)
            done.append(update(tuple(c[r0:r0 + half] for c in carry), s, v_ref[keys, :]))
        m, l, acc = (jnp.concatenate(parts, axis=0) for parts in zip(*done))
        o_ref[...] = (acc / l).astype(o_ref.dtype)
        lse_ref[...] = jnp.broadcast_to(m + jnp.log(l), (t, V_HEAD))

    est = 2 * T * (HEAD_PAD + V_HEAD) * 2 + 8 * t * t * 4
    (o, lse), gathered = _host_call(
        body, name=name, grid=(H, T // t),
        in_specs=[pl.BlockSpec((t, HEAD_PAD), lambda h, i: (i, h)),
                  pl.BlockSpec((T, HEAD_PAD), lambda h, i: (0, h)),
                  pl.BlockSpec((T, V_HEAD), lambda h, i: (0, h))],
        out_specs=[pl.BlockSpec((t, V_HEAD), lambda h, i: (i, off + h)),
                   pl.BlockSpec((t, V_HEAD), lambda h, i: (i, h))],
        out_shape=[_out((T, pw + H * V_HEAD), BF16),
                   _out((T, H * V_HEAD), F32)],
        args=[_hbm(qh), _hbm(kh), _hbm(vh)], sem=("parallel", "parallel"), est=est, ride=ride)
    return o, lse, gathered


def _flash_bwd(qh, kh, vh, cat, dcat, lse, *, H, pw, name, ride=None):
    T = qh.shape[0]
    t = _tile(T, 512, CHUNK)
    nb = T // t
    off = pw // V_HEAD
    half = t

    def body(q_ref, k_ref, v_ref, o_ref, do_ref, lse_ref, dq_out_ref, dk_ref, dv_ref, dq_ref):
        j = pl.program_id(1)

        @pl.when(j == 0)
        def _():
            dq_ref[...] = jnp.zeros_like(dq_ref)

        kj = k_ref[...]
        vj = v_ref[...]

        def pair(rows, kx, vx, mask):
            qi = q_ref[rows, :]
            doi = do_ref[rows, :]
            oi = o_ref[rows, :].astype(F32)
            lsei = lse_ref[rows, :][:, :1]
            s = _dg(qi, kx, 1, 1)
            if mask is not None:
                s = jnp.where(mask, s, NEG_INF)
            p = jnp.exp(s - lsei)
            dp = _dg(doi, vx, 1, 1)
            di = jnp.sum(doi * oi, axis=1, keepdims=True)
            ds = p * (dp - di)
            dq_ref[rows, :] += _dg(ds, kx, 1, 0)
            return _dg(ds, qi, 0, 0), _dg(p, doi, 0, 0)

        def blk(i, carry):
            dk, dv = pair(pl.ds(pl.multiple_of(i * t, t), t), kj, vj, None)
            return carry[0] + dk, carry[1] + dv

        dk, dv = jnp.zeros((t, HEAD_PAD), F32), jnp.zeros((t, V_HEAD), F32)
        for r0 in range(0, t, half):
            n = r0 + half
            dkp, dvp = pair(pl.ds(pl.multiple_of(j * t + r0, half), half), kj[:n], vj[:n],
                            _diag_mask(half, n, r0))
            if n < t:
                dkp = jnp.concatenate([dkp, jnp.zeros((t - n, HEAD_PAD), F32)], axis=0)
                dvp = jnp.concatenate([dvp, jnp.zeros((t - n, V_HEAD), F32)], axis=0)
            dk, dv = dk + dkp, dv + dvp
        dk, dv = lax.fori_loop(j + 1, nb, blk, (dk, dv))
        dk_ref[...] = dk.astype(dk_ref.dtype)
        dv_ref[...] = dv.astype(dv_ref.dtype)

        @pl.when(j == nb - 1)
        def _():
            dq_out_ref[...] = dq_ref[...].astype(dq_out_ref.dtype)

    est = T * (HEAD_PAD * 2 + V_HEAD * 2 + V_HEAD * 4 + V_HEAD * 4 + HEAD_PAD * 4) + 10 * t * t * 4
    (dq, dk, dv), gathered = _host_call(
        body, name=name, grid=(H, nb),
        in_specs=[pl.BlockSpec((T, HEAD_PAD), lambda h, j: (0, h)),
                  pl.BlockSpec((t, HEAD_PAD), lambda h, j: (j, h)),
                  pl.BlockSpec((t, V_HEAD), lambda h, j: (j, h)),
                  pl.BlockSpec((T, V_HEAD), lambda h, j: (0, off + h)),
                  pl.BlockSpec((T, V_HEAD), lambda h, j: (0, off + h)),
                  pl.BlockSpec((T, V_HEAD), lambda h, j: (0, h))],
        out_specs=[pl.BlockSpec((T, HEAD_PAD), lambda h, j: (0, h)),
                   pl.BlockSpec((t, HEAD_PAD), lambda h, j: (j, h)),
                   pl.BlockSpec((t, V_HEAD), lambda h, j: (j, h))],
        out_shape=[_out((T, H * HEAD_PAD), BF16),
                   _out((T, H * HEAD_PAD), BF16),
                   _out((T, H * V_HEAD), BF16)],
        scratch=[pltpu.VMEM((T, HEAD_PAD), F32)],
        args=[_hbm(v) for v in (qh, kh, vh, cat, dcat, lse)], sem=("arbitrary", "arbitrary"), est=est,
        ride=ride)
    return dq, dk, dv, gathered


def _mem_fn(q, k, v):
    hd = q.shape[1] // MEM_HEADS
    outs = []
    for h in range(MEM_HEADS):
        lo, hi = h * hd, (h + 1) * hd
        s = _bdot_nt(_cols(q, lo, hi), _cols(k, lo, hi)) * hd ** -0.5
        e = jnp.exp(s - lax.stop_gradient(jnp.max(s, axis=1, keepdims=True)))
        p = e / jnp.sum(e, axis=1, keepdims=True)
        outs.append(_bdot_nn(p, _cols(v, lo, hi)))
    return jnp.concatenate(outs, axis=1)


def _mem_fwd(xb, wq, k, v, *, name):
    T, D = xb.shape
    tm = _tile(T, 256, 16)

    def fn(i, tv, pv):
        q = _dg(tv[0], pv[0], 1, 0).astype(BF16)
        return (_mem_fn(q, pv[1], pv[2]), q), ()

    return _rowwise(fn, [xb], [wq, k, v], [(D, BF16), (D, BF16)], tm=tm, name=name)


def _mem_bwd(q, k, v, do, *, name):
    T, D = q.shape
    tm = _tile(T, 256, 16)

    def fn(i, tv, pv):
        _, vjp = jax.vjp(_mem_fn, tv[0], pv[0], pv[1])
        dq, dk, dv = vjp(tv[1].astype(F32))
        return (dq,), (dk, dv)

    return _rowwise(fn, [q, do], [k, v], [(D, BF16)], [(k.shape, F32), (v.shape, F32)], tm=tm, name=name)


def _adamw(w, g, m, v, *, name):
    shape = w.shape
    if w.ndim != 3:
        lead3 = (1, math.prod(shape[:-1]), shape[-1])
        return [o.reshape(shape) for o in _adamw(*[a.reshape(lead3) for a in (w, g, m, v)], name=name)]
    Lw, R, C = shape
    tr = _tile(R, 512, 8)
    b1c = 1.0 - ADAM_B1 ** ADAM_STEP
    b2c = 1.0 - ADAM_B2 ** ADAM_STEP

    def body(w_ref, g_ref, m_ref, v_ref, d_ref, mo_ref, vo_ref):
        gg = g_ref[...]
        mn = ADAM_B1 * m_ref[...] + (1.0 - ADAM_B1) * gg
        vn = ADAM_B2 * v_ref[...] + (1.0 - ADAM_B2) * (gg * gg)
        d_ref[...] = -ADAM_LR * ((mn / b1c) / (jnp.sqrt(vn / b2c) + ADAM_EPS) + ADAM_WD * w_ref[...])
        mo_ref[...] = mn
        vo_ref[...] = vn

    spec = pl.BlockSpec((None, tr, C), lambda l, i: (l, i, 0))
    return pl.pallas_call(
        body, name=name, grid=(Lw, R // tr),
        in_specs=[spec] * 4, out_specs=[spec] * 3,
        out_shape=[_out(shape, F32)] * 3,
        compiler_params=_params(("parallel", "parallel"), 7 * tr * C * 4),
    )(*[_hbm(a) for a in (w, g, m, v)])


def _pair_sum(core, gs, landed, offs, *, name):
    n = len(gs)
    _, R, C = landed.shape
    rows = [g.shape[0] // N_DEV for g in gs]

    def body(core_ref, *refs):
        g_refs, l_ref, o_ref = refs[:n], refs[n], refs[n + 1]
        for g_ref, off, r in zip(g_refs, offs, rows):
            o_ref[off:off + r, :] = (g_ref[...].astype(F32) + l_ref[off:off + r, :].astype(F32)).astype(o_ref.dtype)

    slab = pl.BlockSpec((None, R, C), lambda p, core_ref: (p, 0, 0))
    own = [pl.BlockSpec((r, C), lambda p, core_ref: (2 * p + core_ref[0], 0)) for r in rows]
    return pl.pallas_call(
        body, name=name,
        grid_spec=pltpu.PrefetchScalarGridSpec(
            num_scalar_prefetch=1, grid=(4,), in_specs=own + [slab], out_specs=slab),
        out_shape=_out(landed.shape, landed.dtype),
        input_output_aliases={n + 1: 0},
        compiler_params=_params(("arbitrary",), 3 * R * C * 2 + R * C * 8),
    )(core, *[_hbm(g) for g in gs], _hbm(landed))


def _quad_sum(chip, part, gathered, used, *, name):
    C = part.shape[2]
    R = used
    tr = _tile(R, 256, 16)

    def body(chip_ref, own_ref, a_ref, b_ref, c_ref, o_ref):
        o_ref[...] = ((own_ref[...].astype(F32) + a_ref[...].astype(F32)) + b_ref[...].astype(F32)) \
            + c_ref[...].astype(F32)

    def other(k):
        return pl.BlockSpec((None, tr, C), lambda i, chip_ref: (chip_ref[0] ^ k, i, 0))

    return pl.pallas_call(
        body, name=name,
        grid_spec=pltpu.PrefetchScalarGridSpec(
            num_scalar_prefetch=1, grid=(R // tr,),
            in_specs=[pl.BlockSpec((None, tr, C), lambda i, chip_ref: (chip_ref[0], i, 0)),
                      other(1), other(2), other(3)],
            out_specs=pl.BlockSpec((tr, C), lambda i, chip_ref: (i, 0))),
        out_shape=_out((R, C), F32),
        compiler_params=_params(("arbitrary",), 8 * tr * C * 4),
    )(chip, _hbm(part), _hbm(gathered), _hbm(gathered), _hbm(gathered))


def _place():
    x, y, c = lax.axis_index("x"), lax.axis_index("y"), lax.axis_index("c")
    return x, y, c


ANY = pl.BlockSpec(memory_space=pl.ANY)


class _Gather:
    def __init__(self, shards):
        self.shards = list(shards)
        self.n = len(self.shards)
        self.out_shape = [_out((s.shape[0], N_DEV * s.shape[1], s.shape[2]), s.dtype)
                          for s in self.shards]
        self.scratch = [pltpu.SemaphoreType.DMA((7 * self.n,)), pltpu.SemaphoreType.DMA((7 * self.n,)),
                        pltpu.SemaphoreType.DMA((self.n,))]
        self.operands = [_hbm(s) for s in self.shards]

    def _bind(self, refs):
        n = self.n
        ins, outs = refs[:n], refs[n:2 * n]
        send_sems, recv_sems, local_sems = refs[2 * n:]
        x, y, c = _place()
        me, sib = (x, y, c), (x, y, 1 - c)
        chips = [(1 - x, y), (x, 1 - y), (1 - x, 1 - y)]

        def rows(w, p):
            r = self.shards[w].shape[1]
            idx = 4 * p[0] + 2 * p[1] + p[2]
            return outs[w].at[:, pl.ds(pl.multiple_of(idx * r, 8), r), :]

        def copy(w, k, block, to, src=None):
            return pltpu.make_async_remote_copy(
                src_ref=rows(w, block) if src is None else src, dst_ref=rows(w, block),
                send_sem=send_sems.at[w * 7 + k], recv_sem=recv_sems.at[w * 7 + k],
                device_id=to, device_id_type=MESH)

        def mine():
            return [pltpu.make_async_copy(ins[w], rows(w, me), local_sems.at[w]) for w in range(n)]

        def first():
            out = []
            for w in range(n):
                out.append(copy(w, 0, me, sib, src=ins[w]))
                out += [copy(w, 1 + j, me, (*chip, c), src=ins[w]) for j, chip in enumerate(chips)]
            return out

        def passed():
            return [copy(w, 4 + j, (*chip, c), sib) for j, chip in enumerate(chips) for w in range(n)]

        def landed():
            return [copy(w, 1 + j, (*chip, c), me) for j, chip in enumerate(chips) for w in range(n)]

        def last():
            out = []
            for w in range(n):
                out.append(copy(w, 0, sib, me))
                out += [copy(w, 4 + j, (*chip, 1 - c), me) for j, chip in enumerate(chips)]
            return out

        return mine, first, landed, passed, last

    def start(self, refs):
        mine, first, _, _, _ = self._bind(refs)
        for cp in mine() + first():
            cp.start()

    def forward(self, refs):
        _, _, landed, passed, _ = self._bind(refs)
        for arrived, fwd in zip(landed(), passed()):
            arrived.wait_recv()
            fwd.start()

    def finish(self, refs):
        mine, first, _, passed, last = self._bind(refs)
        for cp in last():
            cp.wait_recv()
        for cp in first() + passed():
            cp.wait_send()
        for cp in mine():
            cp.wait()


class _ChipExchange:
    def __init__(self, parts, used):
        self.ncl = len(parts)
        self.used = list(used)
        self.out_shape = [_out(p.shape, p.dtype) for p in parts]
        self.scratch = [pltpu.SemaphoreType.DMA((3 * self.ncl,)), pltpu.SemaphoreType.DMA((3 * self.ncl,))]
        self.operands = [_hbm(p) for p in parts]
        self.n = self.ncl

    def _bind(self, refs):
        ncl = self.ncl
        ins, outs = refs[:ncl], refs[ncl:2 * ncl]
        send_sems, recv_sems = refs[2 * ncl:]
        x, y, c = _place()
        chips = [(1 - x, y), (x, 1 - y), (1 - x, 1 - y)]
        here = 2 * x + y

        def copies(outgoing):
            out = []
            for k in range(ncl):
                rows = pl.ds(0, self.used[k])
                for j, (cx, cy) in enumerate(chips):
                    there = 2 * cx + cy
                    src, dst = (there, here) if outgoing else (here, there)
                    out.append(pltpu.make_async_remote_copy(
                        src_ref=ins[k].at[src, rows, :], dst_ref=outs[k].at[dst, rows, :],
                        send_sem=send_sems.at[3 * k + j], recv_sem=recv_sems.at[3 * k + j],
                        device_id=(cx, cy, c), device_id_type=MESH))
            return out

        return copies

    def start(self, refs):
        for cp in self._bind(refs)(True):
            cp.start()

    def forward(self, refs):
        pass

    def finish(self, refs):
        copies = self._bind(refs)
        for cp in copies(False):
            cp.wait_recv()
        for cp in copies(True):
            cp.wait_send()


class _Both:
    def __init__(self, members):
        self.members = list(members)
        self.n = sum(m.n for m in self.members)
        self.out_shape = [s for m in self.members for s in m.out_shape]
        self.scratch = [s for m in self.members for s in m.scratch]
        self.operands = [o for m in self.members for o in m.operands]

    def split(self, arrays):
        out, a = [], 0
        for m in self.members:
            out.append(list(arrays[a:a + m.n]))
            a += m.n
        return out

    def _refs(self, refs):
        ins, outs = self.split(refs[:self.n]), self.split(refs[self.n:2 * self.n])
        scr, b = [], 2 * self.n
        for m in self.members:
            scr.append(list(refs[b:b + len(m.scratch)]))
            b += len(m.scratch)
        return [(*i, *o, *s) for i, o, s in zip(ins, outs, scr)]

    def start(self, refs):
        for m, r in zip(self.members, self._refs(refs)):
            m.start(r)

    def forward(self, refs):
        for m, r in zip(self.members, self._refs(refs)):
            m.forward(r)

    def finish(self, refs):
        for m, r in zip(self.members, self._refs(refs)):
            m.finish(r)


def _exchange_alone(ex, *, name):
    def body(*refs):
        ex.start(refs)
        ex.forward(refs)
        ex.finish(refs)

    return pl.pallas_call(
        body, name=name, in_specs=[ANY] * ex.n, out_specs=[ANY] * ex.n,
        out_shape=ex.out_shape, scratch_shapes=ex.scratch,
    )(*ex.operands)


def _host_call(body, *, name, grid, in_specs, out_specs, out_shape, args, sem, est, ride=None, scratch=()):
    scratch = list(scratch)
    if ride is None:
        outs = pl.pallas_call(body, name=name, grid=grid, in_specs=in_specs, out_specs=out_specs,
                              out_shape=out_shape, scratch_shapes=scratch,
                              compiler_params=_params(sem, est))(*args)
        return list(outs), []
    n_in, n_out, n, n_scr = len(in_specs), len(out_specs), ride.n, len(scratch)

    def full(*refs):
        ins, rin = refs[:n_in], refs[n_in:n_in + n]
        outs, rout = refs[n_in + n:n_in + n + n_out], refs[n_in + n + n_out:n_in + 2 * n + n_out]
        own = refs[n_in + 2 * n + n_out:n_in + 2 * n + n_out + n_scr]
        rrefs = (*rin, *rout, *refs[n_in + 2 * n + n_out + n_scr:])
        step, total = _ride(ride, rrefs, grid)
        body(*ins, *outs, *own)
        _ride_end(ride, rrefs, step, total)

    outs = pl.pallas_call(
        full, name=name, grid=grid,
        in_specs=list(in_specs) + [ANY] * n, out_specs=list(out_specs) + [ANY] * n,
        out_shape=list(out_shape) + ride.out_shape, scratch_shapes=scratch + ride.scratch,
        compiler_params=_params(("arbitrary",) * len(grid), est),
    )(*args, *ride.operands)
    return list(outs[:n_out]), list(outs[n_out:])


def _ride(ex, refs, grid):
    total = math.prod(grid)
    step = pl.program_id(0)
    for axis in range(1, len(grid)):
        step = step * grid[axis] + pl.program_id(axis)
    pl.when(step == 0)(lambda: ex.start(refs))
    return step, total


def _ride_end(ex, refs, step, total):
    pl.when(step == (3 * total) // 4)(lambda: ex.forward(refs))
    pl.when(step == total - 1)(lambda: ex.finish(refs))


def _class_layout(grads, classes):
    used = [0] * len(set(classes))
    offs = []
    for g, cl in zip(grads, classes):
        offs.append(used[cl])
        used[cl] += g.shape[0] // N_DEV
    return offs, used


def _rs_to_sibling(grads, classes, *, name):
    n = len(grads)
    offs, used = _class_layout(grads, classes)
    heights = used
    ncl = len(heights)
    cols = [next(g.shape[1] for g, cl in zip(grads, classes) if cl == k) for k in range(ncl)]

    def body(*refs):
        gs, land = refs[:n], refs[n:n + ncl]
        send_sems, recv_sems = refs[n + ncl:]
        x, y, c = _place()
        sib = (x, y, 1 - c)
        for p in range(4):
            for w in range(n):
                r = grads[w].shape[0] // N_DEV
                cl = classes[w]
                there = gs[w].at[pl.ds(pl.multiple_of((2 * p + 1 - c) * r, 8), r), :]
                pltpu.make_async_remote_copy(
                    src_ref=there, dst_ref=land[cl].at[p, pl.ds(offs[w], r), :],
                    send_sem=send_sems.at[cl * 4 + p], recv_sem=recv_sems.at[cl * 4 + p],
                    device_id=sib, device_id_type=MESH).start()
        for cl in range(ncl):
            for p in range(4):
                rows_used = land[cl].at[p, pl.ds(0, used[cl]), :]
                slab = pltpu.make_async_remote_copy(
                    src_ref=rows_used, dst_ref=rows_used,
                    send_sem=send_sems.at[cl * 4 + p], recv_sem=recv_sems.at[cl * 4 + p],
                    device_id=sib, device_id_type=MESH)
                slab.wait_send()
                slab.wait_recv()

    return pl.pallas_call(
        body, name=name,
        in_specs=[ANY] * n, out_specs=[ANY] * ncl,
        out_shape=[_out((4, heights[k], cols[k]), BF16) for k in range(ncl)],
        scratch_shapes=[pltpu.SemaphoreType.DMA((4 * ncl,))] * 2,
    )(*[_hbm(g) for g in grads])


def _sum_devices(g, *, name):
    R = g.shape[1]

    def body(g_ref, o_ref):
        acc = g_ref[0]
        for d in range(1, N_DEV):
            acc = acc + g_ref[d]
        o_ref[...] = acc

    vm = pl.BlockSpec(memory_space=pltpu.VMEM)
    return pl.pallas_call(
        body, name=name, in_specs=[vm], out_specs=vm,
        out_shape=jax.ShapeDtypeStruct((R, LANE), F32),
        compiler_params=pltpu.CompilerParams(vmem_limit_bytes=VMEM_FLOOR),
    )(g)


def _rope_tables(positions):
    half = QK_ROPE // 2
    inv_freq = ROPE_BASE ** (-jnp.arange(half, dtype=F32) / half)
    ang = positions.astype(F32)[:, None] * inv_freq
    cos, sin = jnp.cos(ang), jnp.sin(ang)
    z = jnp.zeros_like(cos)
    z2 = jnp.zeros((positions.shape[0], LANE - QK_ROPE), F32)
    rc = jnp.concatenate([cos, cos, z2], axis=1)
    rs1 = jnp.concatenate([-sin, z, z2], axis=1)
    rs2 = jnp.concatenate([z, sin, z2], axis=1)
    return rc, rs1, rs2


def _block_diag(pool_w):
    G, pg, _ = pool_w.shape
    out = jnp.zeros((G * pg, G * pg), pool_w.dtype)
    for g in range(G):
        out = lax.dynamic_update_slice(out, pool_w[g], (g * pg, g * pg))
    return out


def kernel(x, mem, positions, ln_g, ln_b, ffn1_w13, ffn1_w2, w_in, pool_w, pool_scale, q_norm_g, w_uq, kv_norm_g, w_ukv, w_out, mem_wq, mem_wkv, mem_wo, ffn2_w13, ffn2_w2, loss_target, m_ln_g, m_ln_b, m_ffn1_w13, m_ffn1_w2, m_w_in, m_pool_w, m_pool_scale, m_q_norm_g, m_w_uq, m_kv_norm_g, m_w_ukv, m_w_out, m_mem_wq, m_mem_wkv, m_mem_wo, m_ffn2_w13, m_ffn2_w2, v_ln_g, v_ln_b, v_ffn1_w13, v_ffn1_w2, v_w_in, v_pool_w, v_pool_scale, v_q_norm_g, v_w_uq, v_kv_norm_g, v_w_ukv, v_w_out, v_mem_wq, v_mem_wkv, v_mem_wo, v_ffn2_w13, v_ffn2_w2):
    L = ln_g.shape[0]
    T, D = x.shape[1], x.shape[2]
    F = ffn1_w2.shape[1] * N_DEV
    PW = D // 4
    H = (D - PW) // V_HEAD
    DIN = w_in.shape[2]
    DINP = PW + Q_LORA + KV_LORA + LANE
    QW = QK_NOPE + QK_ROPE
    alpha = (2 * L) ** 0.25
    x2d = x.reshape(T, D)
    memb = mem.reshape(mem.shape[1], D).astype(BF16)
    target = loss_target.reshape(T, D)
    tabs = _rope_tables(positions.reshape(T))

    def shards_of(l):
        return dict(
            w13a=ffn1_w13[l].T[None].astype(BF16),
            w13b=ffn2_w13[l].T[None].astype(BF16),
            w2a=ffn1_w2[l][None].astype(BF16),
            w2b=ffn2_w2[l][None].astype(BF16),
            wsq=jnp.stack([w_out[l], mem_wq[l], mem_wo[l]]).astype(BF16),
            wkvT=mem_wkv[l].T[None].astype(BF16),
            winp=jnp.pad(w_in[l], ((0, 0), (0, DINP - DIN)))[None].astype(BF16),
            wuqT=w_uq[l].T[None].astype(BF16),
            wukvT=w_ukv[l].T[None].astype(BF16),
        )

    SMALL = ("winp", "wuqT", "wukvT")
    shards = [shards_of(l) for l in range(L)]
    W = [dict() for _ in range(L)]

    def rider(spec):
        return _Gather([shards[l][n] for l, n in spec]) if spec else None

    def arrived(spec, arrays):
        for (l, n), a in zip(spec, arrays):
            if n in ("w13a", "w13b"):
                a = _interleave(a, 1)
            elif n == "wuqT":
                a = jnp.pad(a.reshape(H, QW, Q_LORA), ((0, 0), (0, HEAD_PAD - QW), (0, 0)))
                a = a.reshape(1, H * HEAD_PAD, Q_LORA)
            elif n == "ln":
                a = jnp.moveaxis(a.reshape(N_DEV, 2, L, 4, D // N_DEV), 0, 3).reshape(2, L, 4, D)
                LN["g"], LN["b"] = a[0], a[1]
            W[l][n] = a

    LN = {}
    shards[0]["ln"] = jnp.concatenate([ln_g.reshape(1, 4 * L, -1), ln_b.reshape(1, 4 * L, -1)], axis=1)
    spec0 = [(0, "w13a")]
    arrived(spec0, _exchange_alone(rider(spec0), name="ag_first"))
    wbd = [_block_diag(pool_w[l]).astype(BF16) for l in range(L)]

    def ffn_fwd(l, which, xres, xb, k, spec):
        ab = "ab"[which]
        h13, a, rode = _ffn_up(xb, W[l]["w13" + ab], 0, name=f"l{l}_ffn{which}_up", ride=rider(spec))
        arrived(spec, rode)
        y, xo, xob, *more_out = _mm_ln(a, W[l]["w2" + ab], 0, xres, LN["g"][l,k:k + 1], LN["b"][l,k:k + 1],
                                       alpha=alpha, s=0.5, name=f"l{l}_ffn{which}_y_ln{k}",
                                       nxt=W[l]["winp"][0] if which == 0 else None)
        return dict(xres=xres, xb=xb, h13=h13, a=a, y=y), xo, xob, *more_out

    saved = []
    xres, xb = x2d, x2d.astype(BF16)
    for l in range(L):
        sv = {}
        more = l + 1 < L
        Wl = W[l]
        spec = ([(0, "w2a"), (0, "ln"), *[(0, n) for n in SMALL], (0, "wkvT")] if l == 0
                else [(l, "wsq"), (l, "wkvT")])
        sv["ffn1"], x1, x1b, hin = ffn_fwd(l, 0, xres, xb, 0, spec)
        pscale = pool_scale[l].reshape(1, PW)
        gq, gkv = q_norm_g[l].reshape(1, Q_LORA), kv_norm_g[l].reshape(1, KV_LORA)
        qh, kh, vh, cqn, ckvn = _heads_fwd(hin, gq, gkv, tabs, Wl["wuqT"][0], Wl["wukvT"][0], H=H, pw=PW,
                                           name=f"l{l}_heads")
        spec = [(l, "w13b"), (l, "w2b")] + ([(0, "wsq")] if l == 0 else []) + ([(l + 1, "w13a")] if more else [])
        cat, lse, rode = _flash_fwd(qh, kh, vh, H=H, pw=PW, name=f"l{l}_flash", ride=rider(spec))
        arrived(spec, rode)
        cat = _pool_fwd(hin, wbd[l], pscale, cat, name=f"l{l}_pool")
        ymix, x2, x2b = _mm_ln(cat, Wl["wsq"], 0, x1, LN["g"][l,1:2], LN["b"][l,1:2], alpha=alpha, s=1.0,
                               name=f"l{l}_ymix_ln1")
        kvm = _mm(memb, Wl["wkvT"], lead=0, tb=True, name=f"l{l}_kvm")
        km, vm = kvm[:, :D], kvm[:, D:]
        om, qm = _mem_fwd(x2b, Wl["wsq"][1], km, vm, name=f"l{l}_memattn")
        ymem, x3, x3b = _mm_ln(om, Wl["wsq"], 2, x2, LN["g"][l,2:3], LN["b"][l,2:3], alpha=alpha, s=1.0,
                               name=f"l{l}_ymem_ln2")
        spec = [(l + 1, n) for n in ("w2a", *SMALL)] if more else []
        sv["ffn2"], x4, x4b = ffn_fwd(l, 1, x3, x3b, 3, spec)
        sv.update(x1=x1, x1b=x1b, hin=hin, pscale=pscale, gq=gq, gkv=gkv, cqn=cqn, ckvn=ckvn,
                  qh=qh, kh=kh, vh=vh, lse=lse, cat=cat, ymix=ymix, x2=x2, x2b=x2b, qm=qm, km=km, vm=vm,
                  om=om, ymem=ymem)
        saved.append(sv)
        xres, xb = x4, x4b


    gW = {}
    gS = {}

    def ln_of(l, k):
        sv = saved[l]
        x, y, s = {0: (sv["ffn1"]["xres"], sv["ffn1"]["y"], 0.5), 1: (sv["x1"], sv["ymix"], 1.0),
                   2: (sv["x2"], sv["ymem"], 1.0), 3: (sv["ffn2"]["xres"], sv["ffn2"]["y"], 0.5)}[k]
        return x, y, LN["g"][l, k:k + 1], s

    def dx_through_ln(a, b, lead, tb, add, into, name, xin=None, wkey=None):
        x, y, g, s = ln_of(*into)
        out = _mm_ln_bwd(a, b, lead, tb, add, x, y, g, alpha=alpha, s=s, name=name, xin=xin)
        gS[("ln_g", *into)], gS[("ln_b", *into)] = out[2], out[3]
        if xin is not None:
            gW[wkey] = out[4]
        return out[0], out[1]

    def ffn_bwd(l, which, sv, dxres, dyb, ride, into):
        tag = f"l{l}_ffn{which}"
        gW[("w2", which, l)] = _mm(sv["a"], dyb, ta=True, out_dtype=BF16, name=f"{tag}_dw2", tn=D)
        dh, rode = _ffn_down_bwd(dyb, W[l]["w2" + "ab"[which]], 0, sv["h13"], name=f"{tag}_dh", ride=ride)
        dw13 = _mm(dh, sv["xb"], ta=True, out_dtype=BF16, name=f"{tag}_dw13", tn=D)
        gW[("w13", which, l)] = _deinterleave(dw13, 0)
        w13 = W[l]["w13" + "ab"[which]]
        if into is not None:
            return dx_through_ln(dh, w13, 0, False, dxres, into, f"{tag}_dx"), rode
        last = rs_first_level(l, "c")
        dxn, got = _mm(dh, w13, lead=0, add=dxres, name=f"{tag}_dx", tn=D, ride=last["ex"])
        rs_last_level(last, got)
        return dxn, rode

    core = lax.axis_index("c").astype(jnp.int32).reshape(1)
    chip = (2 * lax.axis_index("x") + lax.axis_index("y")).astype(jnp.int32).reshape(1)
    gsh = {}

    def rs_first_level(l, group):
        keys, classes = {
            "a": ([("w13", 1, l), ("w2", 1, l), ("mem_wkv", l), ("mem_wq", l), ("mem_wo", l)], [0] * 5),
            "b": ([("w_out", l), ("w_in", l), ("w_uq", l), ("w_ukv", l)], [0, 1, 2, 3]),
            "c": ([("w13", 0, l), ("w2", 0, l)], [0, 0]),
        }[group]
        tag = f"l{l}{group}"
        garrs = []
        for key in keys:
            g = gW[key]
            if key[0] == "w_uq":
                g = g.reshape(H, HEAD_PAD, Q_LORA)[:, :QW, :].reshape(H * QW, Q_LORA)
            garrs.append(g)
        offs, used = _class_layout(garrs, classes)
        parts = list(_rs_to_sibling(garrs, classes, name=f"{tag}_rs_sibling"))
        for cl in range(len(parts)):
            mine = [w for w, c in enumerate(classes) if c == cl]
            parts[cl] = _pair_sum(core, [garrs[w] for w in mine], parts[cl], [offs[w] for w in mine],
                                  name=f"{tag}_rs_pair_sum{cl}")
        return dict(tag=tag, keys=keys, garrs=garrs, classes=classes, offs=offs, used=used, parts=parts,
                    ex=_ChipExchange(parts, used))

    def rs_last_level(st, gathered):
        sums = [_quad_sum(chip, p, a, u, name=f"{st['tag']}_rs_quad_sum{k}")
                for k, (p, a, u) in enumerate(zip(st["parts"], gathered, st["used"]))]
        for key, g, cl, off in zip(st["keys"], st["garrs"], st["classes"], st["offs"]):
            gsh[key] = sums[cl][off:off + g.shape[0] // N_DEV, :]

    top = (L - 1, 3)
    x_top, y_top, g_top, s_top = ln_of(*top)
    dxres, dyb, gS[("ln_g", *top)], gS[("ln_b", *top)], loss_blk = _loss_ln_bwd(
        x_top, y_top, g_top, xres, target, alpha=alpha, s=s_top, name="loss_ln_top_bwd")
    above = None
    for l in reversed(range(L)):
        sv = saved[l]
        Wl = W[l]
        (dxres, dyb), _ = ffn_bwd(l, 1, sv["ffn2"], dxres, dyb, None, (l, 2))
        dom, gW[("mem_wo", l)] = _mm_pair(dyb, Wl["wsq"], 2, sv["om"], dx_dtype=BF16, name=f"l{l}_dom_dwo")
        dqm, dkm, dvm = _mem_bwd(sv["qm"], sv["km"], sv["vm"], dom, name=f"l{l}_memattn_bwd")
        dxres, dyb = dx_through_ln(dqm, Wl["wsq"], 1, True, dxres, (l, 1), f"l{l}_dx2_dwq",
                                   xin=sv["x2b"], wkey=("mem_wq", l))
        dkvm = jnp.concatenate([dkm, dvm], axis=1).astype(BF16)
        gW[("mem_wkv", l)] = _mm(dkvm, memb, ta=True, out_dtype=BF16, name=f"l{l}_dwkv", tn=D)
        dcat, gW[("w_out", l)] = _mm_pair(dyb, Wl["wsq"], 0, sv["cat"], dx_dtype=F32, name=f"l{l}_dcat_dwout")
        riding = [rs_first_level(l, "a")] + ([above] if above else [])
        both = _Both([st["ex"] for st in riding])
        dqh, dkh, dvh, rode = _flash_bwd(sv["qh"], sv["kh"], sv["vh"], sv["cat"], dcat, sv["lse"], H=H, pw=PW,
                                         name=f"l{l}_flash_bwd", ride=both)
        for st, got in zip(riding, both.split(rode)):
            rs_last_level(st, got)
        dqraw, dkv, dkpe = _heads_bwd(dqh, dkh, dvh, tabs, H=H, name=f"l{l}_heads_bwd")
        dcq, gW[("w_uq", l)] = _mm_pair(dqraw, Wl["wuqT"], 0, sv["cqn"], dx_dtype=F32, wt=True,
                                        name=f"l{l}_dcq_dwuq")
        dckv, gW[("w_ukv", l)] = _mm_pair(dkv, Wl["wukvT"], 0, sv["ckvn"], dx_dtype=F32, wt=True,
                                          name=f"l{l}_dckv_dwukv")
        du, dwbd, dps = _pool_bwd(sv["hin"], dcat, wbd[l], sv["pscale"], name=f"l{l}_pool_bwd")
        dhin, dgq, dgkv = _norms_bwd(sv["hin"], sv["gq"], sv["gkv"], dcq, dckv, dkpe, du, pw=PW,
                                     name=f"l{l}_norms_bwd")
        pg = PW // len(POOL_WINDOWS)
        gS[("pool_w", l)] = jnp.stack([dwbd[g * pg:(g + 1) * pg, g * pg:(g + 1) * pg]
                                       for g in range(len(POOL_WINDOWS))])
        gS[("pool_scale", l)], gS[("q_norm_g", l)], gS[("kv_norm_g", l)] = dps, dgq, dgkv
        dxres, dyb = dx_through_ln(dhin, Wl["winp"], 0, True, dxres, (l, 0), f"l{l}_dx1_dwin",
                                   xin=sv["x1b"], wkey=("w_in", l))
        heads = rs_first_level(l, "b")
        riding = [heads["ex"]]
        if l == 0:
            small_keys = []
            for ll in range(L):
                small_keys += [("pool_w", ll), ("pool_scale", ll), ("q_norm_g", ll), ("kv_norm_g", ll)]
                small_keys += [("ln_g", ll, k) for k in range(4)] + [("ln_b", ll, k) for k in range(4)]
            flat = jnp.concatenate([loss_blk[0, :1]] + [gS[k].reshape(-1) for k in small_keys])
            n_small = flat.shape[0]
            rows = -(-n_small // (8 * LANE)) * 8
            flat = jnp.pad(flat, (0, rows * LANE - n_small)).reshape(1, rows, LANE)
            riding.append(_Gather([flat]))
        both = _Both(riding)
        below, rode = ffn_bwd(l, 0, sv["ffn1"], dxres, dyb, both, (l - 1, 3) if l > 0 else None)
        rode = both.split(rode)
        rs_last_level(heads, rode[0])
        if l > 0:
            dxres, dyb = below
            above = rs_first_level(l, "c")
    grad_x = below.reshape(1, T, D)

    red = _sum_devices(rode[1][0].reshape(N_DEV, rows, LANE), name="sum_small").reshape(-1)
    loss = red[0]
    gsm, pos = {}, 1
    for k in small_keys:
        size = math.prod(gS[k].shape)
        gsm[k] = red[pos:pos + size].reshape(gS[k].shape)
        pos += size

    me = 4 * lax.axis_index("x") + 2 * lax.axis_index("y") + lax.axis_index("c")
    dsh = D // N_DEV
    stack = lambda f: jnp.stack([f(l) for l in range(L)])
    g_ln_g = stack(lambda l: jnp.concatenate([gsm[("ln_g", l, k)] for k in range(4)], axis=0))
    g_ln_b = stack(lambda l: jnp.concatenate([gsm[("ln_b", l, k)] for k in range(4)], axis=0))
    swapped = {
        "ffn1_w13": stack(lambda l: gsh[("w13", 0, l)]),
        "ffn2_w13": stack(lambda l: gsh[("w13", 1, l)]),
        "w_in": stack(lambda l: gsh[("w_in", l)][:, :DIN].T),
        "w_uq": stack(lambda l: gsh[("w_uq", l)]),
        "w_ukv": stack(lambda l: gsh[("w_ukv", l)]),
    }
    swap = lambda a: jnp.swapaxes(a, 1, 2)
    grads = {
        "ln_g": lax.dynamic_slice_in_dim(g_ln_g, me * dsh, dsh, axis=2),
        "ln_b": lax.dynamic_slice_in_dim(g_ln_b, me * dsh, dsh, axis=2),
        "ffn1_w2": stack(lambda l: gsh[("w2", 0, l)]),
        "pool_w": stack(lambda l: gsm[("pool_w", l)]),
        "pool_scale": stack(lambda l: gsm[("pool_scale", l)].reshape(PW)),
        "q_norm_g": stack(lambda l: gsm[("q_norm_g", l)].reshape(Q_LORA)),
        "kv_norm_g": stack(lambda l: gsm[("kv_norm_g", l)].reshape(KV_LORA)),
        "w_out": stack(lambda l: gsh[("w_out", l)]),
        "mem_wq": stack(lambda l: gsh[("mem_wq", l)]),
        "mem_wkv": stack(lambda l: gsh[("mem_wkv", l)].T),
        "mem_wo": stack(lambda l: gsh[("mem_wo", l)]),
        "ffn2_w2": stack(lambda l: gsh[("w2", 1, l)]),
        **{nme: swap(g) for nme, g in swapped.items()},
    }

    names = ["ln_g", "ln_b", "ffn1_w13", "ffn1_w2", "w_in", "pool_w", "pool_scale", "q_norm_g", "w_uq",
             "kv_norm_g", "w_ukv", "w_out", "mem_wq", "mem_wkv", "mem_wo", "ffn2_w13", "ffn2_w2"]
    weights = dict(ln_g=ln_g, ln_b=ln_b, ffn1_w13=ffn1_w13, ffn1_w2=ffn1_w2, w_in=w_in, pool_w=pool_w,
                   pool_scale=pool_scale, q_norm_g=q_norm_g, w_uq=w_uq, kv_norm_g=kv_norm_g, w_ukv=w_ukv,
                   w_out=w_out, mem_wq=mem_wq, mem_wkv=mem_wkv, mem_wo=mem_wo, ffn2_w13=ffn2_w13,
                   ffn2_w2=ffn2_w2)
    ms = dict(ln_g=m_ln_g, ln_b=m_ln_b, ffn1_w13=m_ffn1_w13, ffn1_w2=m_ffn1_w2, w_in=m_w_in, pool_w=m_pool_w,
              pool_scale=m_pool_scale, q_norm_g=m_q_norm_g, w_uq=m_w_uq, kv_norm_g=m_kv_norm_g,
              w_ukv=m_w_ukv, w_out=m_w_out, mem_wq=m_mem_wq, mem_wkv=m_mem_wkv, mem_wo=m_mem_wo,
              ffn2_w13=m_ffn2_w13, ffn2_w2=m_ffn2_w2)
    vs = dict(ln_g=v_ln_g, ln_b=v_ln_b, ffn1_w13=v_ffn1_w13, ffn1_w2=v_ffn1_w2, w_in=v_w_in, pool_w=v_pool_w,
              pool_scale=v_pool_scale, q_norm_g=v_q_norm_g, w_uq=v_w_uq, kv_norm_g=v_kv_norm_g,
              w_ukv=v_w_ukv, w_out=v_w_out, mem_wq=v_mem_wq, mem_wkv=v_mem_wkv, mem_wo=v_mem_wo,
              ffn2_w13=v_ffn2_w13, ffn2_w2=v_ffn2_w2)
    deltas, new_m, new_v = [], [], []
    for nme in names:
        if nme in swapped:
            d, mn, vn = [swap(o) for o in _adamw(swap(weights[nme]), swapped[nme], swap(ms[nme]), swap(vs[nme]),
                                                 name=f"adamw_{nme}")]
        else:
            d, mn, vn = _adamw(weights[nme], grads[nme], ms[nme], vs[nme], name=f"adamw_{nme}")
        deltas.append(d)
        new_m.append(mn)
        new_v.append(vn)
    return (loss, grad_x, *[grads[nme] for nme in names], *deltas, *new_m, *new_v)
```

```python
import functools
import math

import jax
import jax.numpy as jnp
from jax import lax
from jax.experimental import pallas as pl
from jax.experimental.pallas import tpu as pltpu

F32 = jnp.float32
BF16 = jnp.bfloat16
MESH = pl.DeviceIdType.MESH

CHUNK = 64
MEM_HEADS = 4
POOL_WINDOWS = (2, 4, 8, 16)
QK_NOPE = 128
QK_ROPE = 64
V_HEAD = 128
Q_LORA = 256
KV_LORA = 128
ROPE_BASE = 10000.0
LN_EPS = 1e-5
RMS_EPS = 1e-6
NEG_INF = -1e30
ADAM_LR = 0.001
ADAM_B1 = 0.9
ADAM_B2 = 0.999
ADAM_EPS = 1e-08
ADAM_WD = 0.01
ADAM_STEP = 10

N_DEV = 8
LANE = 128
HEAD_PAD = 2 * LANE
POOL_HALO = 16
VMEM_CAP = 56 * 1024 * 1024
VMEM_FLOOR = 32 * 1024 * 1024


def _tile(n, pref, mult):
    t = (min(pref, n) // mult) * mult
    while t >= mult:
        if n % t == 0:
            return t
        t -= mult
    return n


def _params(sem, est_bytes):
    limit = int(min(max(2 * est_bytes + (8 << 20), VMEM_FLOOR), VMEM_CAP))
    return pltpu.CompilerParams(dimension_semantics=sem, vmem_limit_bytes=limit)


def _nbytes(shape, dtype):
    return math.prod(shape) * jnp.dtype(dtype).itemsize


def _hbm(x):
    return pltpu.with_memory_space_constraint(x, pltpu.HBM)


def _out(shape, dtype):
    return pltpu.HBM(tuple(shape), dtype)


def _dg(a, b, ca, cb):
    return lax.dot_general(a.astype(BF16), b.astype(BF16), (((ca,), (cb,)), ((), ())),
                           preferred_element_type=F32)


@jax.custom_vjp
def _bdot_nn(a, b):
    return _dg(a, b, 1, 0)


def _bdot_nn_fwd(a, b):
    return _dg(a, b, 1, 0), (a, b)


def _bdot_nn_bwd(res, ct):
    a, b = res
    return _dg(ct, b, 1, 1).astype(a.dtype), _dg(a, ct, 0, 0).astype(b.dtype)


_bdot_nn.defvjp(_bdot_nn_fwd, _bdot_nn_bwd)


@jax.custom_vjp
def _bdot_nt(a, b):
    return _dg(a, b, 1, 1)


def _bdot_nt_fwd(a, b):
    return _dg(a, b, 1, 1), (a, b)


def _bdot_nt_bwd(res, ct):
    a, b = res
    return _dg(ct, b, 1, 0).astype(a.dtype), _dg(ct, a, 0, 0).astype(b.dtype)


_bdot_nt.defvjp(_bdot_nt_fwd, _bdot_nt_bwd)


@functools.partial(jax.custom_vjp, nondiff_argnums=(1,))
def _lane_roll(x, shift):
    return pltpu.roll(x, shift % x.shape[1], axis=1)


def _lane_roll_fwd(x, shift):
    return _lane_roll(x, shift), None


def _lane_roll_bwd(shift, _, ct):
    return (_lane_roll(ct, -shift),)


_lane_roll.defvjp(_lane_roll_fwd, _lane_roll_bwd)


@functools.partial(jax.custom_vjp, nondiff_argnums=(1, 2))
def _cols(x, lo, hi):
    return x[:, lo:hi]


def _cols_fwd(x, lo, hi):
    return x[:, lo:hi], x.shape[1]


def _cols_bwd(lo, hi, width, ct):
    parts = []
    if lo > 0:
        parts.append(jnp.zeros((ct.shape[0], lo), ct.dtype))
    parts.append(ct)
    if hi < width:
        parts.append(jnp.zeros((ct.shape[0], width - hi), ct.dtype))
    return (jnp.concatenate(parts, axis=1) if len(parts) > 1 else ct,)


_cols.defvjp(_cols_fwd, _cols_bwd)


MM_VMEM_BUDGET = 22 * 1024 * 1024


def _mm(a, b, *, name, ta=False, tb=False, out_dtype=F32, lead=None, add=None, add_scale=1.0,
        tm=1024, tn=1024, tk=8192, ride=None):
    if ta:
        K, M = a.shape
    else:
        M, K = a.shape
    bshape = b.shape[1:] if lead is not None else b.shape
    if tb:
        N, Kb = bshape
    else:
        Kb, N = bshape
    assert K == Kb, (name, a.shape, b.shape)

    def blocks(tm, tn, tk):
        tm = _tile(M, tm, LANE if ta else 16)
        tn = _tile(N, tn, LANE)
        tk = _tile(K, tk, LANE)
        nbytes = (tm * tk * a.dtype.itemsize + tk * tn * b.dtype.itemsize
                  + tm * tn * (jnp.dtype(out_dtype).itemsize + (4 if K // tk > 1 else 0)
                               + (add.dtype.itemsize if add is not None else 0)))
        return tm, tn, tk, nbytes

    if ta:
        tm = min(tm, max(LANE, M // 4))
    tm, tn, tk, est = blocks(tm, tn, tk)
    for shrink in ("m", "k", "m", "k", "n"):
        if est <= MM_VMEM_BUDGET:
            break
        if shrink == "m":
            tm, tn, tk, est = blocks(max(tm // 2, LANE), tn, tk)
        elif shrink == "k":
            tm, tn, tk, est = blocks(tm, tn, max(tk // 2, LANE))
        else:
            tm, tn, tk, est = blocks(tm, max(tn // 2, LANE), tk)
    nk = K // tk
    ca = 0 if ta else 1
    cb = 1 if tb else 0

    def body(*refs):
        a_ref, b_ref = refs[0], refs[1]
        add_ref = refs[2] if add is not None else None
        o_ref = refs[3] if add is not None else refs[2]

        def finish(r):
            if add_ref is not None:
                r = r + add_scale * add_ref[...].astype(F32)
            o_ref[...] = r.astype(o_ref.dtype)

        if nk == 1:
            finish(_dg(a_ref[...], b_ref[...], ca, cb))
            return
        acc_ref = refs[-1]
        k = pl.program_id(2)

        @pl.when(k == 0)
        def _():
            acc_ref[...] = jnp.zeros_like(acc_ref)

        acc_ref[...] += _dg(a_ref[...], b_ref[...], ca, cb)

        @pl.when(k == nk - 1)
        def _():
            finish(acc_ref[...])

    a_blk = (tk, tm) if ta else (tm, tk)
    a_map = (lambda i, j, k: (k, i)) if ta else (lambda i, j, k: (i, k))
    b_blk = (tn, tk) if tb else (tk, tn)
    if lead is None:
        b_map = (lambda i, j, k: (j, k)) if tb else (lambda i, j, k: (k, j))
        b_spec = pl.BlockSpec(b_blk, b_map)
    else:
        b_map = (lambda i, j, k: (lead, j, k)) if tb else (lambda i, j, k: (lead, k, j))
        b_spec = pl.BlockSpec((None,) + b_blk, b_map)
    in_specs = [pl.BlockSpec(a_blk, a_map), b_spec]
    args = [a, b]
    if add is not None:
        in_specs.append(pl.BlockSpec((tm, tn), lambda i, j, k: (i, j)))
        args.append(add)
    (out,), rode = _host_call(
        body, name=name,
        grid=(M // tm, N // tn, nk),
        in_specs=in_specs,
        out_specs=[pl.BlockSpec((tm, tn), lambda i, j, k: (i, j))],
        out_shape=[_out((M, N), out_dtype)],
        scratch=[pltpu.VMEM((tm, tn), F32)] if nk > 1 else [],
        args=[_hbm(v) for v in args], sem=("parallel", "parallel", "arbitrary"), est=est + tm * tn * 4,
        ride=ride)
    return out if ride is None else (out, rode)


def _mm_pair(dy, w, lead, x, *, dx_dtype, name, wt=False):
    M, N = dy.shape
    Kx = x.shape[1]
    wshape = (N, Kx) if wt else (Kx, N)
    assert w.shape[1:] == wshape, (name, w.shape, x.shape, dy.shape)
    tm = _tile(M, 512, 16)
    nt = M // tm

    def body(dy_ref, w_ref, x_ref, dx_ref, dw_ref, acc_ref):
        i = pl.program_id(0)

        @pl.when(i == 0)
        def _():
            acc_ref[...] = jnp.zeros_like(acc_ref)

        dyv = dy_ref[...]
        dx_ref[...] = _dg(dyv, w_ref[...], 1, 0 if wt else 1).astype(dx_ref.dtype)
        acc_ref[...] += _dg(dyv, x_ref[...], 0, 0) if wt else _dg(x_ref[...], dyv, 0, 0)

        @pl.when(i == nt - 1)
        def _():
            dw_ref[...] = acc_ref[...].astype(dw_ref.dtype)

    est = tm * (N + 2 * Kx) * 4 + Kx * N * (2 + 4 + 2)
    return pl.pallas_call(
        body, name=name, grid=(nt,),
        in_specs=[pl.BlockSpec((tm, N), lambda i: (i, 0)),
                  pl.BlockSpec((None,) + wshape, lambda i: (lead, 0, 0)),
                  pl.BlockSpec((tm, Kx), lambda i: (i, 0))],
        out_specs=[pl.BlockSpec((tm, Kx), lambda i: (i, 0)), pl.BlockSpec(wshape, lambda i: (0, 0))],
        out_shape=[_out((M, Kx), dx_dtype), _out(wshape, BF16)],
        scratch_shapes=[pltpu.VMEM(wshape, F32)],
        compiler_params=_params(("arbitrary",), est),
    )(_hbm(dy), _hbm(w), _hbm(x))


def _rowwise(fn, tiles, params, tile_outs, acc_outs=(), *, tm, name):
    tile_arrays, tile_specs = [], []
    for t in tiles:
        if isinstance(t, tuple):
            tile_arrays.append(t[0])
            tile_specs.append(t[1])
        else:
            tile_arrays.append(t)
            tile_specs.append(pl.BlockSpec((tm, t.shape[1]), lambda i: (i, 0)))
    T = tile_arrays[0].shape[0]
    nt, np_, nto, nao = len(tile_arrays), len(params), len(tile_outs), len(acc_outs)

    def body(*refs):
        i = pl.program_id(0)
        tvals = [r[...] for r in refs[:nt]]
        pvals = [r[...] for r in refs[nt:nt + np_]]
        to_refs = refs[nt + np_:nt + np_ + nto]
        ao_refs = refs[nt + np_ + nto:]
        touts, aouts = fn(i, tvals, pvals)
        for r, v in zip(to_refs, touts):
            r[...] = v.astype(r.dtype)
        if nao:
            @pl.when(i == 0)
            def _():
                for r in ao_refs:
                    r[...] = jnp.zeros_like(r)
            for r, v in zip(ao_refs, aouts):
                r[...] += v.astype(r.dtype)

    in_specs = tile_specs + [pl.BlockSpec(p.shape, lambda i: (0, 0)) for p in params]
    out_specs = [pl.BlockSpec((tm, c), lambda i: (i, 0)) for c, _ in tile_outs]
    out_specs += [pl.BlockSpec(s, lambda i: (0, 0)) for s, _ in acc_outs]
    out_shape = [_out((T, c), d) for c, d in tile_outs]
    out_shape += [_out(s, d) for s, d in acc_outs]
    width = sum(s.block_shape[-1] for s in tile_specs) + sum(c for c, _ in tile_outs)
    est = 6 * tm * width * 4 + sum(_nbytes(p.shape, F32) for p in params) * 4
    return pl.pallas_call(
        body, name=name, grid=(T // tm,),
        in_specs=in_specs, out_specs=out_specs, out_shape=out_shape,
        compiler_params=_params(("arbitrary",) if nao else ("parallel",), est),
    )(*[_hbm(v) for v in tile_arrays], *[_hbm(p) for p in params])


def _ln_fn(alpha, s, xres, y, g, b):
    z = alpha * xres.astype(F32) + s * y.astype(F32)
    mu = jnp.mean(z, axis=-1, keepdims=True)
    zc = z - mu
    var = jnp.mean(zc * zc, axis=-1, keepdims=True)
    return zc * lax.rsqrt(var + LN_EPS) * g + b


def _mm_ln(a, b, lead, xres, g, bias, *, alpha, s, name):
    M, K = a.shape
    N = b.shape[2]
    tm = _tile(M, 512, 16)

    def body(a_ref, b_ref, x_ref, g_ref, bias_ref, y_ref, xo_ref, xb_ref):
        y = _dg(a_ref[...], b_ref[...], 1, 0)
        y_ref[...] = y.astype(y_ref.dtype)
        out = _ln_fn(alpha, s, x_ref[...], y, g_ref[...], bias_ref[...])
        xo_ref[...] = out
        xb_ref[...] = out.astype(BF16)

    row = pl.BlockSpec((tm, N), lambda i: (i, 0))
    vec = pl.BlockSpec((1, N), lambda i: (0, 0))
    est = tm * K * 2 + K * N * 2 + tm * N * (4 + 4 + 4 + 2 + 8)
    return pl.pallas_call(
        body, name=name, grid=(M // tm,),
        in_specs=[pl.BlockSpec((tm, K), lambda i: (i, 0)), pl.BlockSpec((None, K, N), lambda i: (lead, 0, 0)),
                  row, vec, vec],
        out_specs=[row, row, row],
        out_shape=[_out((M, N), BF16), _out((M, N), F32), _out((M, N), BF16)],
        compiler_params=_params(("parallel",), est),
    )(_hbm(a), _hbm(b), _hbm(xres), _hbm(g), _hbm(bias))


def _ln_bwd_math(alpha, s, x, y, g, d):
    z = alpha * x + s * y.astype(F32)
    zc = z - jnp.mean(z, axis=-1, keepdims=True)
    r = lax.rsqrt(jnp.mean(zc * zc, axis=-1, keepdims=True) + LN_EPS)
    xh = zc * r
    dxh = d * g
    dz = r * (dxh - jnp.mean(dxh, axis=-1, keepdims=True) - xh * jnp.mean(dxh * xh, axis=-1, keepdims=True))
    return alpha * dz, s * dz, jnp.sum(d * xh, axis=0, keepdims=True), jnp.sum(d, axis=0, keepdims=True)


def _mm_ln_bwd(a, b, lead, tb, add, xres, y, g, *, alpha, s, name, xin=None):
    M, K = a.shape
    N = b.shape[1] if tb else b.shape[2]
    tk = K if K * N * 2 <= MM_VMEM_BUDGET * 3 // 5 else _tile(K, 2816, LANE)
    tm = _tile(M, 512 if K * N * 2 <= MM_VMEM_BUDGET // 4 else 256, 16)
    nk = K // tk
    nt = M // tm
    cb = 1 if tb else 0
    assert xin is None or (tb and nk == 1), name

    def body(*refs):
        a_ref, b_ref, add_ref, x_ref, y_ref, g_ref = refs[:6]
        rest = refs[6:]
        if xin is not None:
            xin_ref, rest = rest[0], rest[1:]
        dx_ref, dy_ref, dg_ref, db_ref = rest[:4]
        scratch = rest[4:]
        i, k = pl.program_id(0), pl.program_id(1)

        def finish(d):
            @pl.when(i == 0)
            def _():
                dg_ref[...] = jnp.zeros_like(dg_ref)
                db_ref[...] = jnp.zeros_like(db_ref)

            dx, dy, dg, db = _ln_bwd_math(alpha, s, x_ref[...], y_ref[...], g_ref[...], d + add_ref[...])
            dx_ref[...] = dx
            dy_ref[...] = dy.astype(dy_ref.dtype)
            dg_ref[...] += dg
            db_ref[...] += db

        if xin is not None:
            dw_ref, accw_ref = scratch[0], scratch[1]

            @pl.when(i == 0)
            def _():
                accw_ref[...] = jnp.zeros_like(accw_ref)

            accw_ref[...] += _dg(xin_ref[...], a_ref[...], 0, 0)

            @pl.when(i == nt - 1)
            def _():
                dw_ref[...] = accw_ref[...].astype(dw_ref.dtype)

        if nk == 1:
            finish(_dg(a_ref[...], b_ref[...], 1, cb))
            return
        acc_ref = scratch[0]

        @pl.when(k == 0)
        def _():
            acc_ref[...] = jnp.zeros_like(acc_ref)

        acc_ref[...] += _dg(a_ref[...], b_ref[...], 1, cb)

        @pl.when(k == nk - 1)
        def _():
            finish(acc_ref[...])

    deep = None
    row = pl.BlockSpec((tm, N), lambda i, k: (i, 0))
    row_in = pl.BlockSpec((tm, N), lambda i, k: (i, 0), pipeline_mode=deep)
    vec = pl.BlockSpec((1, N), lambda i, k: (0, 0))
    b_spec = (pl.BlockSpec((None, N, tk), lambda i, k: (lead, 0, k), pipeline_mode=pl.Buffered(1) if nk == 1 else None)
              if tb else
              pl.BlockSpec((None, tk, N), lambda i, k: (lead, k, 0), pipeline_mode=pl.Buffered(1) if nk == 1 else None))
    est = tm * tk * 2 + tk * N * 2 + tm * N * (4 + 4 + 2 + 4 + 2 + 4 + 12)
    in_specs = [pl.BlockSpec((tm, tk), lambda i, k: (i, k), pipeline_mode=deep), b_spec, row_in, row_in, row_in, vec]
    out_specs = [row, row, vec, vec]
    out_shape = [_out((M, N), F32), _out((M, N), BF16), _out((1, N), F32), _out((1, N), F32)]
    scratch = [pltpu.VMEM((tm, N), F32)] if nk > 1 else []
    args = [a, b, add, xres, y, g]
    if xin is not None:
        in_specs.append(row)
        args.append(xin)
        out_specs.append(pl.BlockSpec((N, K), lambda i, k: (0, 0)))
        out_shape.append(_out((N, K), BF16))
        scratch.append(pltpu.VMEM((N, K), F32))
        est += N * K * 8 + tm * N * 2
    return pl.pallas_call(
        body, name=name, grid=(nt, nk),
        in_specs=in_specs, out_specs=out_specs, out_shape=out_shape, scratch_shapes=scratch,
        compiler_params=_params(("arbitrary", "arbitrary"), est),
    )(*[_hbm(v) for v in args])


def _loss_ln_bwd(xres, y, g, out, target, *, alpha, s, name):
    T, D = xres.shape
    tm = _tile(T, 256, 16)

    def body(x_ref, y_ref, o_ref, t_ref, g_ref, dx_ref, dy_ref, dg_ref, db_ref, loss_ref):
        @pl.when(pl.program_id(0) == 0)
        def _():
            dg_ref[...] = jnp.zeros_like(dg_ref)
            db_ref[...] = jnp.zeros_like(db_ref)
            loss_ref[...] = jnp.zeros_like(loss_ref)

        err = o_ref[...] - t_ref[...]
        part = 0.5 * jnp.sum(jnp.sum(err * err, axis=1, keepdims=True) / D, axis=0, keepdims=True)
        loss_ref[...] += jnp.broadcast_to(part, loss_ref.shape)
        dx, dy, dg, db = _ln_bwd_math(alpha, s, x_ref[...], y_ref[...], g_ref[...], err / D)
        dx_ref[...] = dx
        dy_ref[...] = dy.astype(dy_ref.dtype)
        dg_ref[...] += dg
        db_ref[...] += db

    row = pl.BlockSpec((tm, D), lambda i: (i, 0))
    vec = pl.BlockSpec((1, D), lambda i: (0, 0))
    return pl.pallas_call(
        body, name=name, grid=(T // tm,),
        in_specs=[row, row, row, row, vec],
        out_specs=[row, row, vec, vec, pl.BlockSpec((8, LANE), lambda i: (0, 0))],
        out_shape=[_out((T, D), F32), _out((T, D), BF16), _out((1, D), F32), _out((1, D), F32),
                   _out((8, LANE), F32)],
        compiler_params=_params(("arbitrary",), 14 * tm * D * 4),
    )(_hbm(xres), _hbm(y), _hbm(out), _hbm(target), _hbm(g))


FFN_TILE = 256


def _interleave(w, axis):
    n = w.shape[axis] // (2 * FFN_TILE)
    shp = w.shape[:axis] + (2, n, FFN_TILE) + w.shape[axis + 1:]
    return jnp.swapaxes(w.reshape(shp), axis, axis + 1).reshape(w.shape)


def _deinterleave(w, axis):
    n = w.shape[axis] // (2 * FFN_TILE)
    shp = w.shape[:axis] + (n, 2, FFN_TILE) + w.shape[axis + 1:]
    return jnp.swapaxes(w.reshape(shp), axis, axis + 1).reshape(w.shape)


def _ffn_up(xb, w13t, lead, *, name, ride=None):
    T, D = xb.shape
    F = w13t.shape[1] // 2
    tc = FFN_TILE
    tm = _tile(T, 2048, 16)

    def body(x_ref, w_ref, h_ref, a_ref):
        h = _dg(x_ref[...], w_ref[...], 1, 1)
        g, u = h[:, :tc], h[:, tc:]
        h_ref[...] = h.astype(h_ref.dtype)
        a_ref[...] = (g * jax.nn.sigmoid(g) * u).astype(a_ref.dtype)

    est = (tm * D + 2 * tc * D + 3 * tm * tc) * 2 + 3 * tm * tc * 4
    (h13, a), gathered = _host_call(
        body, name=name, grid=(T // tm, F // tc),
        in_specs=[pl.BlockSpec((tm, D), lambda i, j: (i, 0)),
                  pl.BlockSpec((None, 2 * tc, D), lambda i, j: (lead, j, 0))],
        out_specs=[pl.BlockSpec((tm, 2 * tc), lambda i, j: (i, j)),
                   pl.BlockSpec((tm, tc), lambda i, j: (i, j))],
        out_shape=[_out((T, 2 * F), BF16), _out((T, F), BF16)],
        args=[_hbm(xb), _hbm(w13t)], sem=("parallel", "parallel"), est=est, ride=ride)
    return h13, a, gathered


def _ffn_down_bwd(dyb, w2, lead, h13, *, name, ride=None):
    T, D = dyb.shape
    F = w2.shape[1]
    tc = FFN_TILE
    tm = _tile(T, 2048, 16)

    def body(dy_ref, w_ref, h_ref, dh_ref):
        d = _dg(dy_ref[...], w_ref[...], 1, 1)
        h = h_ref[...].astype(F32)
        g, u = h[:, :tc], h[:, tc:]
        sig = jax.nn.sigmoid(g)
        gs = g * sig
        dh_ref[...] = jnp.concatenate([d * u * (sig + gs * (1.0 - sig)), d * gs], axis=1).astype(dh_ref.dtype)

    est = (tm * D + tc * D + 4 * tm * tc) * 2 + 6 * tm * tc * 4
    (dh,), rode = _host_call(
        body, name=name, grid=(T // tm, F // tc),
        in_specs=[pl.BlockSpec((tm, D), lambda i, j: (i, 0)),
                  pl.BlockSpec((None, tc, D), lambda i, j: (lead, j, 0)),
                  pl.BlockSpec((tm, 2 * tc), lambda i, j: (i, j))],
        out_specs=[pl.BlockSpec((tm, 2 * tc), lambda i, j: (i, j))],
        out_shape=[_out((T, 2 * F), BF16)],
        args=[_hbm(dyb), _hbm(w2), _hbm(h13)], sem=("parallel", "parallel"), est=est, ride=ride)
    return dh, rode


def _pool_select(parts, pw):
    pg = pw // len(POOL_WINDOWS)
    grp = lax.broadcasted_iota(jnp.int32, parts[0].shape, 1) // pg
    out = parts[3]
    for g in (2, 1, 0):
        out = jnp.where(grp == g, parts[g], out)
    return out


def _pool_count(t0, rows, pw):
    pg = pw // len(POOL_WINDOWS)
    grp = lax.broadcasted_iota(jnp.int32, (rows, pw), 1) // pg
    win = jnp.where(grp == 0, POOL_WINDOWS[0],
                    jnp.where(grp == 1, POOL_WINDOWS[1],
                              jnp.where(grp == 2, POOL_WINDOWS[2], POOL_WINDOWS[3])))
    t = t0 + lax.broadcasted_iota(jnp.int32, (rows, pw), 0)
    return jnp.minimum(t + 1, win).astype(F32)


def _window_sums(ext, up):
    n = ext.shape[0]
    sums, cur, k = [], ext, 1
    for _ in POOL_WINDOWS:
        cur = cur + pltpu.roll(cur, (n - k) if up else k, axis=0)
        sums.append(cur)
        k *= 2
    return sums


def _pool_delta(u, halo, t0):
    tm, pw = u.shape
    ext = jnp.concatenate([halo, u], axis=0)
    sums = [s[POOL_HALO:, :] for s in _window_sums(ext, up=False)]
    return _pool_select(sums, pw) / _pool_count(t0, tm, pw) - u


def _pool_fwd(hin, wbd, scale, cat, *, name):
    T = hin.shape[0]
    pw = wbd.shape[0]
    tm = _tile(T, 256, POOL_HALO)
    per = tm // POOL_HALO

    def body(u_ref, halo_ref, w_ref, s_ref, cat_ref, y_ref):
        i = pl.program_id(0)
        halo = jnp.where(i > 0, halo_ref[...], 0.0)
        d = _pool_delta(u_ref[...], halo, i * tm)
        y_ref[...] = (_dg(d, w_ref[...], 1, 0) * s_ref[...]).astype(y_ref.dtype)

    return pl.pallas_call(
        body, name=name, grid=(T // tm,),
        in_specs=[pl.BlockSpec((tm, pw), lambda i: (i, 0)),
                  pl.BlockSpec((POOL_HALO, pw), lambda i: (jnp.maximum(i * per - 1, 0), 0)),
                  pl.BlockSpec((pw, pw), lambda i: (0, 0)),
                  pl.BlockSpec((1, pw), lambda i: (0, 0)),
                  ANY],
        out_specs=pl.BlockSpec((tm, pw), lambda i: (i, 0)),
        out_shape=_out(cat.shape, cat.dtype),
        input_output_aliases={4: 0},
        compiler_params=_params(("parallel",), 16 * tm * pw * 4),
    )(_hbm(hin), _hbm(hin), _hbm(wbd), _hbm(scale), _hbm(cat))


def _pool_bwd(hin, dcat, wbd, scale, *, name):
    T = hin.shape[0]
    pw = wbd.shape[0]
    tm = _tile(T, 256, POOL_HALO)
    per = tm // POOL_HALO
    nt = T // tm

    def body(u_ref, halo_ref, dy_ref, dyn_ref, w_ref, s_ref, du_ref, dw_ref, ds_ref):
        i = pl.program_id(0)

        @pl.when(i == 0)
        def _():
            dw_ref[...] = jnp.zeros_like(dw_ref)
            ds_ref[...] = jnp.zeros_like(ds_ref)

        halo = jnp.where(i > 0, halo_ref[...], 0.0)
        d = _pool_delta(u_ref[...], halo, i * tm)
        w = w_ref[...]
        sc = s_ref[...]
        dy = dy_ref[...]
        dyn = jnp.where(i < nt - 1, dyn_ref[...], 0.0)
        ds_ref[...] += jnp.sum(dy * _dg(d, w, 1, 0), axis=0, keepdims=True)
        dys = dy * sc
        dw_ref[...] += _dg(d, dys, 0, 0)
        dys_ext = jnp.concatenate([dys, dyn * sc], axis=0)
        dd_ext = _dg(dys_ext, w, 1, 1)
        ddp = dd_ext / _pool_count(i * tm, tm + POOL_HALO, pw)
        sums = [s[:tm, :] for s in _window_sums(ddp, up=True)]
        du_ref[...] = _pool_select(sums, pw) - dd_ext[:tm, :]

    return pl.pallas_call(
        body, name=name, grid=(nt,),
        in_specs=[pl.BlockSpec((tm, pw), lambda i: (i, 0)),
                  pl.BlockSpec((POOL_HALO, pw), lambda i: (jnp.maximum(i * per - 1, 0), 0)),
                  pl.BlockSpec((tm, pw), lambda i: (i, 0)),
                  pl.BlockSpec((POOL_HALO, pw), lambda i: (jnp.minimum((i + 1) * per, nt * per - 1), 0)),
                  pl.BlockSpec((pw, pw), lambda i: (0, 0)),
                  pl.BlockSpec((1, pw), lambda i: (0, 0))],
        out_specs=[pl.BlockSpec((tm, pw), lambda i: (i, 0)),
                   pl.BlockSpec((pw, pw), lambda i: (0, 0)),
                   pl.BlockSpec((1, pw), lambda i: (0, 0))],
        out_shape=[_out((T, pw), F32),
                   _out((pw, pw), F32),
                   _out((1, pw), F32)],
        compiler_params=_params(("arbitrary",), 24 * tm * pw * 4),
    )(_hbm(hin), _hbm(hin), _hbm(dcat), _hbm(dcat), _hbm(wbd), _hbm(scale))


def _rms(x, g):
    return x * lax.rsqrt(jnp.mean(x * x, axis=-1, keepdims=True) + RMS_EPS) * g


def _norms_fn(pw, h, gq, gkv):
    o1 = pw + Q_LORA
    o2 = o1 + KV_LORA
    return (_rms(_cols(h, pw, o1), gq), _rms(_cols(h, o1, o2), gkv), _cols(h, o2, h.shape[1]))


def _norms_bwd(hin, gq, gkv, dcq, dckv, dkpe, du, *, pw, name):
    tm = _tile(hin.shape[0], 256, 16)
    dinp = hin.shape[1]

    def fn(i, tv, pv):
        _, vjp = jax.vjp(functools.partial(_norms_fn, pw), tv[0], pv[0], pv[1])
        dh, dgq, dgkv = vjp((tv[1].astype(F32), tv[2].astype(F32), tv[3].astype(F32)))
        dh = jnp.concatenate([tv[4], dh[:, pw:]], axis=1)
        return (dh,), (dgq, dgkv)

    return _rowwise(fn, [hin, dcq, dckv, dkpe, du], [gq, gkv], [(dinp, BF16)],
                    [((1, Q_LORA), F32), ((1, KV_LORA), F32)], tm=tm, name=name)


def _heads_fn(H, qraw, kv, kpe, rc, rs1, rs2):
    half = QK_ROPE // 2
    scale = (QK_NOPE + QK_ROPE) ** -0.5

    def rope(blk):
        return blk * rc + _lane_roll(blk, -half) * rs1 + _lane_roll(blk, half) * rs2

    krot = rope(kpe)
    qs, ks, vs = [], [], []
    for h in range(H):
        lo = h * HEAD_PAD
        qs += [_cols(qraw, lo, lo + LANE) * scale, rope(_cols(qraw, lo + LANE, lo + HEAD_PAD)) * scale]
        ks += [_cols(kv, lo, lo + LANE), krot]
        vs += [_cols(kv, lo + LANE, lo + HEAD_PAD)]
    return jnp.concatenate(qs, axis=1), jnp.concatenate(ks, axis=1), jnp.concatenate(vs, axis=1)


def _heads_fwd(hin, gq, gkv, tabs, wuq, wukv, *, H, pw, name):
    tm = _tile(hin.shape[0], 256, 16)

    def fn(i, tv, pv):
        cqn, ckvn, kpe = _norms_fn(pw, tv[0], pv[0], pv[1])
        qraw = _dg(cqn, pv[2], 1, 1)
        kv = _dg(ckvn, pv[3], 1, 1)
        return (*_heads_fn(H, qraw, kv, kpe, *tv[1:]), cqn, ckvn), ()

    return _rowwise(fn, [hin, *tabs], [gq, gkv, wuq, wukv],
                    [(H * HEAD_PAD, BF16), (H * HEAD_PAD, BF16), (H * V_HEAD, BF16), (Q_LORA, BF16),
                     (KV_LORA, BF16)], tm=tm, name=name)


def _heads_bwd(dq, dk, dv, tabs, *, H, name):
    tm = _tile(dq.shape[0], 256, 16)

    def fn(i, tv, pv):
        z = jnp.zeros((tm, H * HEAD_PAD), F32)
        zk = jnp.zeros((tm, LANE), F32)
        rc, rs1, rs2 = tv[3], tv[4], tv[5]
        _, vjp = jax.vjp(lambda a, b, c: _heads_fn(H, a, b, c, rc, rs1, rs2), z, z, zk)
        return vjp((tv[0].astype(F32), tv[1].astype(F32), tv[2].astype(F32))), ()

    return _rowwise(fn, [dq, dk, dv, *tabs], [],
                    [(H * HEAD_PAD, BF16), (H * HEAD_PAD, BF16), (LANE, F32)], tm=tm, name=name)


def _diag_mask(rows, cols, row0):
    r = (row0 + lax.broadcasted_iota(jnp.int32, (rows, cols), 0)) // CHUNK
    c = lax.broadcasted_iota(jnp.int32, (rows, cols), 1) // CHUNK
    return r >= c


def _flash_fwd(qh, kh, vh, *, H, pw, name, ride=None):
    T = qh.shape[0]
    t = _tile(T, 512, CHUNK)
    off = pw // V_HEAD


    half = t

    def body(q_ref, k_ref, v_ref, o_ref, lse_ref):
        i = pl.program_id(1)
        q = q_ref[...]

        def update(carry, s, v):
            m, l, acc = carry
            mn = jnp.maximum(m, jnp.max(s, axis=1, keepdims=True))
            p = jnp.exp(s - mn)
            corr = jnp.exp(m - mn)
            return mn, corr * l + jnp.sum(p, axis=1, keepdims=True), corr * acc + _dg(p, v, 1, 0)

        def blk(j, carry):
            rows = pl.ds(pl.multiple_of(j * t, t), t)
            return update(carry, _dg(q, k_ref[rows, :], 1, 1), v_ref[rows, :])

        init = (jnp.full((t, 1), NEG_INF, F32), jnp.zeros((t, 1), F32), jnp.zeros((t, V_HEAD), F32))
        carry = lax.fori_loop(0, i, blk, init)
        done = []
        for r0 in range(0, t, half):
            keys = pl.ds(pl.multiple_of(i * t, t), r0 + half)
            s = _dg(q[r0:r0 + half, :], k_ref[keys, :], 1, 1)
            s = jnp.where(_diag_mask(half, r0 + half, r0), s, NEG_INF)
            done.append(update(tuple(c[r0:r0 + half] for c in carry), s, v_ref[keys, :]))
        m, l, acc = (jnp.concatenate(parts, axis=0) for parts in zip(*done))
        o_ref[...] = (acc / l).astype(o_ref.dtype)
        lse_ref[...] = jnp.broadcast_to(m + jnp.log(l), (t, V_HEAD))

    est = 2 * T * (HEAD_PAD + V_HEAD) * 2 + 8 * t * t * 4
    (o, lse), gathered = _host_call(
        body, name=name, grid=(H, T // t),
        in_specs=[pl.BlockSpec((t, HEAD_PAD), lambda h, i: (i, h)),
                  pl.BlockSpec((T, HEAD_PAD), lambda h, i: (0, h)),
                  pl.BlockSpec((T, V_HEAD), lambda h, i: (0, h))],
        out_specs=[pl.BlockSpec((t, V_HEAD), lambda h, i: (i, off + h)),
                   pl.BlockSpec((t, V_HEAD), lambda h, i: (i, h))],
        out_shape=[_out((T, pw + H * V_HEAD), BF16),
                   _out((T, H * V_HEAD), F32)],
        args=[_hbm(qh), _hbm(kh), _hbm(vh)], sem=("parallel", "parallel"), est=est, ride=ride)
    return o, lse, gathered


def _flash_bwd(qh, kh, vh, cat, dcat, lse, *, H, pw, name, ride=None):
    T = qh.shape[0]
    t = _tile(T, 512, CHUNK)
    nb = T // t
    off = pw // V_HEAD
    half = t

    def body(q_ref, k_ref, v_ref, o_ref, do_ref, lse_ref, dq_out_ref, dk_ref, dv_ref, dq_ref):
        j = pl.program_id(1)

        @pl.when(j == 0)
        def _():
            dq_ref[...] = jnp.zeros_like(dq_ref)

        kj = k_ref[...]
        vj = v_ref[...]

        def pair(rows, kx, vx, mask):
            qi = q_ref[rows, :]
            doi = do_ref[rows, :]
            oi = o_ref[rows, :].astype(F32)
            lsei = lse_ref[rows, :][:, :1]
            s = _dg(qi, kx, 1, 1)
            if mask is not None:
                s = jnp.where(mask, s, NEG_INF)
            p = jnp.exp(s - lsei)
            dp = _dg(doi, vx, 1, 1)
            di = jnp.sum(doi * oi, axis=1, keepdims=True)
            ds = p * (dp - di)
            dq_ref[rows, :] += _dg(ds, kx, 1, 0)
            return _dg(ds, qi, 0, 0), _dg(p, doi, 0, 0)

        def blk(i, carry):
            dk, dv = pair(pl.ds(pl.multiple_of(i * t, t), t), kj, vj, None)
            return carry[0] + dk, carry[1] + dv

        dk, dv = jnp.zeros((t, HEAD_PAD), F32), jnp.zeros((t, V_HEAD), F32)
        for r0 in range(0, t, half):
            n = r0 + half
            dkp, dvp = pair(pl.ds(pl.multiple_of(j * t + r0, half), half), kj[:n], vj[:n],
                            _diag_mask(half, n, r0))
            if n < t:
                dkp = jnp.concatenate([dkp, jnp.zeros((t - n, HEAD_PAD), F32)], axis=0)
                dvp = jnp.concatenate([dvp, jnp.zeros((t - n, V_HEAD), F32)], axis=0)
            dk, dv = dk + dkp, dv + dvp
        dk, dv = lax.fori_loop(j + 1, nb, blk, (dk, dv))
        dk_ref[...] = dk.astype(dk_ref.dtype)
        dv_ref[...] = dv.astype(dv_ref.dtype)

        @pl.when(j == nb - 1)
        def _():
            dq_out_ref[...] = dq_ref[...].astype(dq_out_ref.dtype)

    est = T * (HEAD_PAD * 2 + V_HEAD * 2 + V_HEAD * 4 + V_HEAD * 4 + HEAD_PAD * 4) + 10 * t * t * 4
    (dq, dk, dv), gathered = _host_call(
        body, name=name, grid=(H, nb),
        in_specs=[pl.BlockSpec((T, HEAD_PAD), lambda h, j: (0, h)),
                  pl.BlockSpec((t, HEAD_PAD), lambda h, j: (j, h)),
                  pl.BlockSpec((t, V_HEAD), lambda h, j: (j, h)),
                  pl.BlockSpec((T, V_HEAD), lambda h, j: (0, off + h)),
                  pl.BlockSpec((T, V_HEAD), lambda h, j: (0, off + h)),
                  pl.BlockSpec((T, V_HEAD), lambda h, j: (0, h))],
        out_specs=[pl.BlockSpec((T, HEAD_PAD), lambda h, j: (0, h)),
                   pl.BlockSpec((t, HEAD_PAD), lambda h, j: (j, h)),
                   pl.BlockSpec((t, V_HEAD), lambda h, j: (j, h))],
        out_shape=[_out((T, H * HEAD_PAD), BF16),
                   _out((T, H * HEAD_PAD), BF16),
                   _out((T, H * V_HEAD), BF16)],
        scratch=[pltpu.VMEM((T, HEAD_PAD), F32)],
        args=[_hbm(v) for v in (qh, kh, vh, cat, dcat, lse)], sem=("arbitrary", "arbitrary"), est=est,
        ride=ride)
    return dq, dk, dv, gathered


def _mem_fn(q, k, v):
    hd = q.shape[1] // MEM_HEADS
    outs = []
    for h in range(MEM_HEADS):
        lo, hi = h * hd, (h + 1) * hd
        s = _bdot_nt(_cols(q, lo, hi), _cols(k, lo, hi)) * hd ** -0.5
        e = jnp.exp(s - lax.stop_gradient(jnp.max(s, axis=1, keepdims=True)))
        p = e / jnp.sum(e, axis=1, keepdims=True)
        outs.append(_bdot_nn(p, _cols(v, lo, hi)))
    return jnp.concatenate(outs, axis=1)


def _mem_fwd(xb, wq, k, v, *, name):
    T, D = xb.shape
    tm = _tile(T, 256, 16)

    def fn(i, tv, pv):
        q = _dg(tv[0], pv[0], 1, 0).astype(BF16)
        return (_mem_fn(q, pv[1], pv[2]), q), ()

    return _rowwise(fn, [xb], [wq, k, v], [(D, BF16), (D, BF16)], tm=tm, name=name)


def _mem_bwd(q, k, v, do, *, name):
    T, D = q.shape
    tm = _tile(T, 256, 16)

    def fn(i, tv, pv):
        _, vjp = jax.vjp(_mem_fn, tv[0], pv[0], pv[1])
        dq, dk, dv = vjp(tv[1].astype(F32))
        return (dq,), (dk, dv)

    return _rowwise(fn, [q, do], [k, v], [(D, BF16)], [(k.shape, F32), (v.shape, F32)], tm=tm, name=name)


def _adamw(w, g, m, v, *, name):
    shape = w.shape
    if w.ndim != 3:
        lead3 = (1, math.prod(shape[:-1]), shape[-1])
        return [o.reshape(shape) for o in _adamw(*[a.reshape(lead3) for a in (w, g, m, v)], name=name)]
    Lw, R, C = shape
    tr = _tile(R, 512, 8)
    b1c = 1.0 - ADAM_B1 ** ADAM_STEP
    b2c = 1.0 - ADAM_B2 ** ADAM_STEP

    def body(w_ref, g_ref, m_ref, v_ref, d_ref, mo_ref, vo_ref):
        gg = g_ref[...]
        mn = ADAM_B1 * m_ref[...] + (1.0 - ADAM_B1) * gg
        vn = ADAM_B2 * v_ref[...] + (1.0 - ADAM_B2) * (gg * gg)
        d_ref[...] = -ADAM_LR * ((mn / b1c) / (jnp.sqrt(vn / b2c) + ADAM_EPS) + ADAM_WD * w_ref[...])
        mo_ref[...] = mn
        vo_ref[...] = vn

    spec = pl.BlockSpec((None, tr, C), lambda l, i: (l, i, 0))
    return pl.pallas_call(
        body, name=name, grid=(Lw, R // tr),
        in_specs=[spec] * 4, out_specs=[spec] * 3,
        out_shape=[_out(shape, F32)] * 3,
        compiler_params=_params(("parallel", "parallel"), 7 * tr * C * 4),
    )(*[_hbm(a) for a in (w, g, m, v)])


def _pair_sum(core, gs, landed, offs, *, name):
    n = len(gs)
    _, R, C = landed.shape
    rows = [g.shape[0] // N_DEV for g in gs]

    def body(core_ref, *refs):
        g_refs, l_ref, o_ref = refs[:n], refs[n], refs[n + 1]
        for g_ref, off, r in zip(g_refs, offs, rows):
            o_ref[off:off + r, :] = (g_ref[...].astype(F32) + l_ref[off:off + r, :].astype(F32)).astype(o_ref.dtype)

    slab = pl.BlockSpec((None, R, C), lambda p, core_ref: (p, 0, 0))
    own = [pl.BlockSpec((r, C), lambda p, core_ref: (2 * p + core_ref[0], 0)) for r in rows]
    return pl.pallas_call(
        body, name=name,
        grid_spec=pltpu.PrefetchScalarGridSpec(
            num_scalar_prefetch=1, grid=(4,), in_specs=own + [slab], out_specs=slab),
        out_shape=_out(landed.shape, landed.dtype),
        input_output_aliases={n + 1: 0},
        compiler_params=_params(("arbitrary",), 3 * R * C * 2 + R * C * 8),
    )(core, *[_hbm(g) for g in gs], _hbm(landed))


def _quad_sum(chip, part, gathered, used, *, name):
    C = part.shape[2]
    R = used
    tr = _tile(R, 256, 16)

    def body(chip_ref, own_ref, a_ref, b_ref, c_ref, o_ref):
        o_ref[...] = ((own_ref[...].astype(F32) + a_ref[...].astype(F32)) + b_ref[...].astype(F32)) \
            + c_ref[...].astype(F32)

    def other(k):
        return pl.BlockSpec((None, tr, C), lambda i, chip_ref: (chip_ref[0] ^ k, i, 0))

    return pl.pallas_call(
        body, name=name,
        grid_spec=pltpu.PrefetchScalarGridSpec(
            num_scalar_prefetch=1, grid=(R // tr,),
            in_specs=[pl.BlockSpec((None, tr, C), lambda i, chip_ref: (chip_ref[0], i, 0)),
                      other(1), other(2), other(3)],
            out_specs=pl.BlockSpec((tr, C), lambda i, chip_ref: (i, 0))),
        out_shape=_out((R, C), F32),
        compiler_params=_params(("arbitrary",), 8 * tr * C * 4),
    )(chip, _hbm(part), _hbm(gathered), _hbm(gathered), _hbm(gathered))


def _place():
    x, y, c = lax.axis_index("x"), lax.axis_index("y"), lax.axis_index("c")
    return x, y, c


ANY = pl.BlockSpec(memory_space=pl.ANY)


class _Gather:
    def __init__(self, shards):
        self.shards = list(shards)
        self.n = len(self.shards)
        self.out_shape = [_out((s.shape[0], N_DEV * s.shape[1], s.shape[2]), s.dtype)
                          for s in self.shards]
        self.scratch = [pltpu.SemaphoreType.DMA((7 * self.n,)), pltpu.SemaphoreType.DMA((7 * self.n,)),
                        pltpu.SemaphoreType.DMA((self.n,))]
        self.operands = [_hbm(s) for s in self.shards]

    def _bind(self, refs):
        n = self.n
        ins, outs = refs[:n], refs[n:2 * n]
        send_sems, recv_sems, local_sems = refs[2 * n:]
        x, y, c = _place()
        me, sib = (x, y, c), (x, y, 1 - c)
        chips = [(1 - x, y), (x, 1 - y), (1 - x, 1 - y)]

        def rows(w, p):
            r = self.shards[w].shape[1]
            idx = 4 * p[0] + 2 * p[1] + p[2]
            return outs[w].at[:, pl.ds(pl.multiple_of(idx * r, 8), r), :]

        def copy(w, k, block, to, src=None):
            return pltpu.make_async_remote_copy(
                src_ref=rows(w, block) if src is None else src, dst_ref=rows(w, block),
                send_sem=send_sems.at[w * 7 + k], recv_sem=recv_sems.at[w * 7 + k],
                device_id=to, device_id_type=MESH)

        def mine():
            return [pltpu.make_async_copy(ins[w], rows(w, me), local_sems.at[w]) for w in range(n)]

        def first():
            out = []
            for w in range(n):
                out.append(copy(w, 0, me, sib, src=ins[w]))
                out += [copy(w, 1 + j, me, (*chip, c), src=ins[w]) for j, chip in enumerate(chips)]
            return out

        def passed():
            return [copy(w, 4 + j, (*chip, c), sib) for j, chip in enumerate(chips) for w in range(n)]

        def landed():
            return [copy(w, 1 + j, (*chip, c), me) for j, chip in enumerate(chips) for w in range(n)]

        def last():
            out = []
            for w in range(n):
                out.append(copy(w, 0, sib, me))
                out += [copy(w, 4 + j, (*chip, 1 - c), me) for j, chip in enumerate(chips)]
            return out

        return mine, first, landed, passed, last

    def start(self, refs):
        mine, first, _, _, _ = self._bind(refs)
        for cp in mine() + first():
            cp.start()

    def forward(self, refs):
        _, _, landed, passed, _ = self._bind(refs)
        for arrived, fwd in zip(landed(), passed()):
            arrived.wait_recv()
            fwd.start()

    def finish(self, refs):
        mine, first, _, passed, last = self._bind(refs)
        for cp in last():
            cp.wait_recv()
        for cp in first() + passed():
            cp.wait_send()
        for cp in mine():
            cp.wait()


class _ChipExchange:
    def __init__(self, parts, used):
        self.ncl = len(parts)
        self.used = list(used)
        self.out_shape = [_out(p.shape, p.dtype) for p in parts]
        self.scratch = [pltpu.SemaphoreType.DMA((3 * self.ncl,)), pltpu.SemaphoreType.DMA((3 * self.ncl,))]
        self.operands = [_hbm(p) for p in parts]
        self.n = self.ncl

    def _bind(self, refs):
        ncl = self.ncl
        ins, outs = refs[:ncl], refs[ncl:2 * ncl]
        send_sems, recv_sems = refs[2 * ncl:]
        x, y, c = _place()
        chips = [(1 - x, y), (x, 1 - y), (1 - x, 1 - y)]
        here = 2 * x + y

        def copies(outgoing):
            out = []
            for k in range(ncl):
                rows = pl.ds(0, self.used[k])
                for j, (cx, cy) in enumerate(chips):
                    there = 2 * cx + cy
                    src, dst = (there, here) if outgoing else (here, there)
                    out.append(pltpu.make_async_remote_copy(
                        src_ref=ins[k].at[src, rows, :], dst_ref=outs[k].at[dst, rows, :],
                        send_sem=send_sems.at[3 * k + j], recv_sem=recv_sems.at[3 * k + j],
                        device_id=(cx, cy, c), device_id_type=MESH))
            return out

        return copies

    def start(self, refs):
        for cp in self._bind(refs)(True):
            cp.start()

    def forward(self, refs):
        pass

    def finish(self, refs):
        copies = self._bind(refs)
        for cp in copies(False):
            cp.wait_recv()
        for cp in copies(True):
            cp.wait_send()


class _Both:
    def __init__(self, members):
        self.members = list(members)
        self.n = sum(m.n for m in self.members)
        self.out_shape = [s for m in self.members for s in m.out_shape]
        self.scratch = [s for m in self.members for s in m.scratch]
        self.operands = [o for m in self.members for o in m.operands]

    def split(self, arrays):
        out, a = [], 0
        for m in self.members:
            out.append(list(arrays[a:a + m.n]))
            a += m.n
        return out

    def _refs(self, refs):
        ins, outs = self.split(refs[:self.n]), self.split(refs[self.n:2 * self.n])
        scr, b = [], 2 * self.n
        for m in self.members:
            scr.append(list(refs[b:b + len(m.scratch)]))
            b += len(m.scratch)
        return [(*i, *o, *s) for i, o, s in zip(ins, outs, scr)]

    def start(self, refs):
        for m, r in zip(self.members, self._refs(refs)):
            m.start(r)

    def forward(self, refs):
        for m, r in zip(self.members, self._refs(refs)):
            m.forward(r)

    def finish(self, refs):
        for m, r in zip(self.members, self._refs(refs)):
            m.finish(r)


def _exchange_alone(ex, *, name):
    def body(*refs):
        ex.start(refs)
        ex.forward(refs)
        ex.finish(refs)

    return pl.pallas_call(
        body, name=name, in_specs=[ANY] * ex.n, out_specs=[ANY] * ex.n,
        out_shape=ex.out_shape, scratch_shapes=ex.scratch,
    )(*ex.operands)


def _host_call(body, *, name, grid, in_specs, out_specs, out_shape, args, sem, est, ride=None, scratch=()):
    scratch = list(scratch)
    if ride is None:
        outs = pl.pallas_call(body, name=name, grid=grid, in_specs=in_specs, out_specs=out_specs,
                              out_shape=out_shape, scratch_shapes=scratch,
                              compiler_params=_params(sem, est))(*args)
        return list(outs), []
    n_in, n_out, n, n_scr = len(in_specs), len(out_specs), ride.n, len(scratch)

    def full(*refs):
        ins, rin = refs[:n_in], refs[n_in:n_in + n]
        outs, rout = refs[n_in + n:n_in + n + n_out], refs[n_in + n + n_out:n_in + 2 * n + n_out]
        own = refs[n_in + 2 * n + n_out:n_in + 2 * n + n_out + n_scr]
        rrefs = (*rin, *rout, *refs[n_in + 2 * n + n_out + n_scr:])
        step, total = _ride(ride, rrefs, grid)
        body(*ins, *outs, *own)
        _ride_end(ride, rrefs, step, total)

    outs = pl.pallas_call(
        full, name=name, grid=grid,
        in_specs=list(in_specs) + [ANY] * n, out_specs=list(out_specs) + [ANY] * n,
        out_shape=list(out_shape) + ride.out_shape, scratch_shapes=scratch + ride.scratch,
        compiler_params=_params(("arbitrary",) * len(grid), est),
    )(*args, *ride.operands)
    return list(outs[:n_out]), list(outs[n_out:])


def _ride(ex, refs, grid):
    total = math.prod(grid)
    step = pl.program_id(0)
    for axis in range(1, len(grid)):
        step = step * grid[axis] + pl.program_id(axis)
    pl.when(step == 0)(lambda: ex.start(refs))
    return step, total


def _ride_end(ex, refs, step, total):
    pl.when(step == (3 * total) // 4)(lambda: ex.forward(refs))
    pl.when(step == total - 1)(lambda: ex.finish(refs))


def _class_layout(grads, classes):
    used = [0] * len(set(classes))
    offs = []
    for g, cl in zip(grads, classes):
        offs.append(used[cl])
        used[cl] += g.shape[0] // N_DEV
    return offs, used


def _rs_to_sibling(grads, classes, *, name):
    n = len(grads)
    offs, used = _class_layout(grads, classes)
    heights = used
    ncl = len(heights)
    cols = [next(g.shape[1] for g, cl in zip(grads, classes) if cl == k) for k in range(ncl)]

    def body(*refs):
        gs, land = refs[:n], refs[n:n + ncl]
        send_sems, recv_sems = refs[n + ncl:]
        x, y, c = _place()
        sib = (x, y, 1 - c)
        for p in range(4):
            for w in range(n):
                r = grads[w].shape[0] // N_DEV
                cl = classes[w]
                there = gs[w].at[pl.ds(pl.multiple_of((2 * p + 1 - c) * r, 8), r), :]
                pltpu.make_async_remote_copy(
                    src_ref=there, dst_ref=land[cl].at[p, pl.ds(offs[w], r), :],
                    send_sem=send_sems.at[cl * 4 + p], recv_sem=recv_sems.at[cl * 4 + p],
                    device_id=sib, device_id_type=MESH).start()
        for cl in range(ncl):
            for p in range(4):
                rows_used = land[cl].at[p, pl.ds(0, used[cl]), :]
                slab = pltpu.make_async_remote_copy(
                    src_ref=rows_used, dst_ref=rows_used,
                    send_sem=send_sems.at[cl * 4 + p], recv_sem=recv_sems.at[cl * 4 + p],
                    device_id=sib, device_id_type=MESH)
                slab.wait_send()
                slab.wait_recv()

    return pl.pallas_call(
        body, name=name,
        in_specs=[ANY] * n, out_specs=[ANY] * ncl,
        out_shape=[_out((4, heights[k], cols[k]), BF16) for k in range(ncl)],
        scratch_shapes=[pltpu.SemaphoreType.DMA((4 * ncl,))] * 2,
    )(*[_hbm(g) for g in grads])


def _sum_devices(g, *, name):
    R = g.shape[1]

    def body(g_ref, o_ref):
        acc = g_ref[0]
        for d in range(1, N_DEV):
            acc = acc + g_ref[d]
        o_ref[...] = acc

    vm = pl.BlockSpec(memory_space=pltpu.VMEM)
    return pl.pallas_call(
        body, name=name, in_specs=[vm], out_specs=vm,
        out_shape=jax.ShapeDtypeStruct((R, LANE), F32),
        compiler_params=pltpu.CompilerParams(vmem_limit_bytes=VMEM_FLOOR),
    )(g)


def _rope_tables(positions):
    half = QK_ROPE // 2
    inv_freq = ROPE_BASE ** (-jnp.arange(half, dtype=F32) / half)
    ang = positions.astype(F32)[:, None] * inv_freq
    cos, sin = jnp.cos(ang), jnp.sin(ang)
    z = jnp.zeros_like(cos)
    z2 = jnp.zeros((positions.shape[0], LANE - QK_ROPE), F32)
    rc = jnp.concatenate([cos, cos, z2], axis=1)
    rs1 = jnp.concatenate([-sin, z, z2], axis=1)
    rs2 = jnp.concatenate([z, sin, z2], axis=1)
    return rc, rs1, rs2


def _block_diag(pool_w):
    G, pg, _ = pool_w.shape
    out = jnp.zeros((G * pg, G * pg), pool_w.dtype)
    for g in range(G):
        out = lax.dynamic_update_slice(out, pool_w[g], (g * pg, g * pg))
    return out


def kernel(x, mem, positions, ln_g, ln_b, ffn1_w13, ffn1_w2, w_in, pool_w, pool_scale, q_norm_g, w_uq, kv_norm_g, w_ukv, w_out, mem_wq, mem_wkv, mem_wo, ffn2_w13, ffn2_w2, loss_target, m_ln_g, m_ln_b, m_ffn1_w13, m_ffn1_w2, m_w_in, m_pool_w, m_pool_scale, m_q_norm_g, m_w_uq, m_kv_norm_g, m_w_ukv, m_w_out, m_mem_wq, m_mem_wkv, m_mem_wo, m_ffn2_w13, m_ffn2_w2, v_ln_g, v_ln_b, v_ffn1_w13, v_ffn1_w2, v_w_in, v_pool_w, v_pool_scale, v_q_norm_g, v_w_uq, v_kv_norm_g, v_w_ukv, v_w_out, v_mem_wq, v_mem_wkv, v_mem_wo, v_ffn2_w13, v_ffn2_w2):
    L = ln_g.shape[0]
    T, D = x.shape[1], x.shape[2]
    F = ffn1_w2.shape[1] * N_DEV
    PW = D // 4
    H = (D - PW) // V_HEAD
    DIN = w_in.shape[2]
    DINP = PW + Q_LORA + KV_LORA + LANE
    QW = QK_NOPE + QK_ROPE
    alpha = (2 * L) ** 0.25
    x2d = x.reshape(T, D)
    memb = mem.reshape(mem.shape[1], D).astype(BF16)
    target = loss_target.reshape(T, D)
    tabs = _rope_tables(positions.reshape(T))

    def shards_of(l):
        return dict(
            w13a=ffn1_w13[l].T[None].astype(BF16),
            w13b=ffn2_w13[l].T[None].astype(BF16),
            w2a=ffn1_w2[l][None].astype(BF16),
            w2b=ffn2_w2[l][None].astype(BF16),
            wsq=jnp.stack([w_out[l], mem_wq[l], mem_wo[l]]).astype(BF16),
            wkvT=mem_wkv[l].T[None].astype(BF16),
            winp=jnp.pad(w_in[l], ((0, 0), (0, DINP - DIN)))[None].astype(BF16),
            wuqT=w_uq[l].T[None].astype(BF16),
            wukvT=w_ukv[l].T[None].astype(BF16),
        )

    SMALL = ("winp", "wuqT", "wukvT")
    shards = [shards_of(l) for l in range(L)]
    W = [dict() for _ in range(L)]

    def rider(spec):
        return _Gather([shards[l][n] for l, n in spec]) if spec else None

    def arrived(spec, arrays):
        for (l, n), a in zip(spec, arrays):
            if n in ("w13a", "w13b"):
                a = _interleave(a, 1)
            elif n == "wuqT":
                a = jnp.pad(a.reshape(H, QW, Q_LORA), ((0, 0), (0, HEAD_PAD - QW), (0, 0)))
                a = a.reshape(1, H * HEAD_PAD, Q_LORA)
            elif n == "ln":
                a = jnp.moveaxis(a.reshape(N_DEV, 2, L, 4, D // N_DEV), 0, 3).reshape(2, L, 4, D)
                LN["g"], LN["b"] = a[0], a[1]
            W[l][n] = a

    LN = {}
    shards[0]["ln"] = jnp.concatenate([ln_g.reshape(1, 4 * L, -1), ln_b.reshape(1, 4 * L, -1)], axis=1)
    spec0 = [(0, "w13a")]
    arrived(spec0, _exchange_alone(rider(spec0), name="ag_first"))
    wbd = [_block_diag(pool_w[l]).astype(BF16) for l in range(L)]

    def ffn_fwd(l, which, xres, xb, k, spec):
        ab = "ab"[which]
        h13, a, rode = _ffn_up(xb, W[l]["w13" + ab], 0, name=f"l{l}_ffn{which}_up", ride=rider(spec))
        arrived(spec, rode)
        y, xo, xob = _mm_ln(a, W[l]["w2" + ab], 0, xres, LN["g"][l,k:k + 1], LN["b"][l,k:k + 1], alpha=alpha, s=0.5,
                            name=f"l{l}_ffn{which}_y_ln{k}")
        return dict(xres=xres, xb=xb, h13=h13, a=a, y=y), xo, xob

    saved = []
    xres, xb = x2d, x2d.astype(BF16)
    for l in range(L):
        sv = {}
        more = l + 1 < L
        Wl = W[l]
        spec = ([(0, "w2a"), (0, "ln"), *[(0, n) for n in SMALL], (0, "wkvT")] if l == 0
                else [(l, "wsq"), (l, "wkvT")])
        sv["ffn1"], x1, x1b = ffn_fwd(l, 0, xres, xb, 0, spec)
        hin = _mm(x1b, Wl["winp"], lead=0, name=f"l{l}_hin")
        pscale = pool_scale[l].reshape(1, PW)
        gq, gkv = q_norm_g[l].reshape(1, Q_LORA), kv_norm_g[l].reshape(1, KV_LORA)
        qh, kh, vh, cqn, ckvn = _heads_fwd(hin, gq, gkv, tabs, Wl["wuqT"][0], Wl["wukvT"][0], H=H, pw=PW,
                                           name=f"l{l}_heads")
        spec = [(l, "w13b"), (l, "w2b")] + ([(0, "wsq")] if l == 0 else []) + ([(l + 1, "w13a")] if more else [])
        cat, lse, rode = _flash_fwd(qh, kh, vh, H=H, pw=PW, name=f"l{l}_flash", ride=rider(spec))
        arrived(spec, rode)
        cat = _pool_fwd(hin, wbd[l], pscale, cat, name=f"l{l}_pool")
        ymix, x2, x2b = _mm_ln(cat, Wl["wsq"], 0, x1, LN["g"][l,1:2], LN["b"][l,1:2], alpha=alpha, s=1.0,
                               name=f"l{l}_ymix_ln1")
        kvm = _mm(memb, Wl["wkvT"], lead=0, tb=True, name=f"l{l}_kvm")
        km, vm = kvm[:, :D], kvm[:, D:]
        om, qm = _mem_fwd(x2b, Wl["wsq"][1], km, vm, name=f"l{l}_memattn")
        ymem, x3, x3b = _mm_ln(om, Wl["wsq"], 2, x2, LN["g"][l,2:3], LN["b"][l,2:3], alpha=alpha, s=1.0,
                               name=f"l{l}_ymem_ln2")
        spec = [(l + 1, n) for n in ("w2a", *SMALL)] if more else []
        sv["ffn2"], x4, x4b = ffn_fwd(l, 1, x3, x3b, 3, spec)
        sv.update(x1=x1, x1b=x1b, hin=hin, pscale=pscale, gq=gq, gkv=gkv, cqn=cqn, ckvn=ckvn,
                  qh=qh, kh=kh, vh=vh, lse=lse, cat=cat, ymix=ymix, x2=x2, x2b=x2b, qm=qm, km=km, vm=vm,
                  om=om, ymem=ymem)
        saved.append(sv)
        xres, xb = x4, x4b


    gW = {}
    gS = {}

    def ln_of(l, k):
        sv = saved[l]
        x, y, s = {0: (sv["ffn1"]["xres"], sv["ffn1"]["y"], 0.5), 1: (sv["x1"], sv["ymix"], 1.0),
                   2: (sv["x2"], sv["ymem"], 1.0), 3: (sv["ffn2"]["xres"], sv["ffn2"]["y"], 0.5)}[k]
        return x, y, LN["g"][l, k:k + 1], s

    def dx_through_ln(a, b, lead, tb, add, into, name, xin=None, wkey=None):
        x, y, g, s = ln_of(*into)
        out = _mm_ln_bwd(a, b, lead, tb, add, x, y, g, alpha=alpha, s=s, name=name, xin=xin)
        gS[("ln_g", *into)], gS[("ln_b", *into)] = out[2], out[3]
        if xin is not None:
            gW[wkey] = out[4]
        return out[0], out[1]

    def ffn_bwd(l, which, sv, dxres, dyb, ride, into):
        tag = f"l{l}_ffn{which}"
        gW[("w2", which, l)] = _mm(sv["a"], dyb, ta=True, out_dtype=BF16, name=f"{tag}_dw2", tn=D)
        dh, rode = _ffn_down_bwd(dyb, W[l]["w2" + "ab"[which]], 0, sv["h13"], name=f"{tag}_dh", ride=ride)
        dw13 = _mm(dh, sv["xb"], ta=True, out_dtype=BF16, name=f"{tag}_dw13", tn=D)
        gW[("w13", which, l)] = _deinterleave(dw13, 0)
        w13 = W[l]["w13" + "ab"[which]]
        if into is not None:
            return dx_through_ln(dh, w13, 0, False, dxres, into, f"{tag}_dx"), rode
        last = rs_first_level(l, "c")
        dxn, got = _mm(dh, w13, lead=0, add=dxres, name=f"{tag}_dx", tn=D, ride=last["ex"])
        rs_last_level(last, got)
        return dxn, rode

    core = lax.axis_index("c").astype(jnp.int32).reshape(1)
    chip = (2 * lax.axis_index("x") + lax.axis_index("y")).astype(jnp.int32).reshape(1)
    gsh = {}

    def rs_first_level(l, group):
        keys, classes = {
            "a": ([("w13", 1, l), ("w2", 1, l), ("mem_wkv", l), ("mem_wq", l), ("mem_wo", l)], [0] * 5),
            "b": ([("w_out", l), ("w_in", l), ("w_uq", l), ("w_ukv", l)], [0, 1, 2, 3]),
            "c": ([("w13", 0, l), ("w2", 0, l)], [0, 0]),
        }[group]
        tag = f"l{l}{group}"
        garrs = []
        for key in keys:
            g = gW[key]
            if key[0] == "w_uq":
                g = g.reshape(H, HEAD_PAD, Q_LORA)[:, :QW, :].reshape(H * QW, Q_LORA)
            garrs.append(g)
        offs, used = _class_layout(garrs, classes)
        parts = list(_rs_to_sibling(garrs, classes, name=f"{tag}_rs_sibling"))
        for cl in range(len(parts)):
            mine = [w for w, c in enumerate(classes) if c == cl]
            parts[cl] = _pair_sum(core, [garrs[w] for w in mine], parts[cl], [offs[w] for w in mine],
                                  name=f"{tag}_rs_pair_sum{cl}")
        return dict(tag=tag, keys=keys, garrs=garrs, classes=classes, offs=offs, used=used, parts=parts,
                    ex=_ChipExchange(parts, used))

    def rs_last_level(st, gathered):
        sums = [_quad_sum(chip, p, a, u, name=f"{st['tag']}_rs_quad_sum{k}")
                for k, (p, a, u) in enumerate(zip(st["parts"], gathered, st["used"]))]
        for key, g, cl, off in zip(st["keys"], st["garrs"], st["classes"], st["offs"]):
            gsh[key] = sums[cl][off:off + g.shape[0] // N_DEV, :]

    top = (L - 1, 3)
    x_top, y_top, g_top, s_top = ln_of(*top)
    dxres, dyb, gS[("ln_g", *top)], gS[("ln_b", *top)], loss_blk = _loss_ln_bwd(
        x_top, y_top, g_top, xres, target, alpha=alpha, s=s_top, name="loss_ln_top_bwd")
    above = None
    for l in reversed(range(L)):
        sv = saved[l]
        Wl = W[l]
        (dxres, dyb), _ = ffn_bwd(l, 1, sv["ffn2"], dxres, dyb, None, (l, 2))
        dom, gW[("mem_wo", l)] = _mm_pair(dyb, Wl["wsq"], 2, sv["om"], dx_dtype=BF16, name=f"l{l}_dom_dwo")
        dqm, dkm, dvm = _mem_bwd(sv["qm"], sv["km"], sv["vm"], dom, name=f"l{l}_memattn_bwd")
        dxres, dyb = dx_through_ln(dqm, Wl["wsq"], 1, True, dxres, (l, 1), f"l{l}_dx2_dwq",
                                   xin=sv["x2b"], wkey=("mem_wq", l))
        dkvm = jnp.concatenate([dkm, dvm], axis=1).astype(BF16)
        gW[("mem_wkv", l)] = _mm(dkvm, memb, ta=True, out_dtype=BF16, name=f"l{l}_dwkv", tn=D)
        dcat, gW[("w_out", l)] = _mm_pair(dyb, Wl["wsq"], 0, sv["cat"], dx_dtype=F32, name=f"l{l}_dcat_dwout")
        riding = [rs_first_level(l, "a")] + ([above] if above else [])
        both = _Both([st["ex"] for st in riding])
        dqh, dkh, dvh, rode = _flash_bwd(sv["qh"], sv["kh"], sv["vh"], sv["cat"], dcat, sv["lse"], H=H, pw=PW,
                                         name=f"l{l}_flash_bwd", ride=both)
        for st, got in zip(riding, both.split(rode)):
            rs_last_level(st, got)
        dqraw, dkv, dkpe = _heads_bwd(dqh, dkh, dvh, tabs, H=H, name=f"l{l}_heads_bwd")
        dcq, gW[("w_uq", l)] = _mm_pair(dqraw, Wl["wuqT"], 0, sv["cqn"], dx_dtype=F32, wt=True,
                                        name=f"l{l}_dcq_dwuq")
        dckv, gW[("w_ukv", l)] = _mm_pair(dkv, Wl["wukvT"], 0, sv["ckvn"], dx_dtype=F32, wt=True,
                                          name=f"l{l}_dckv_dwukv")
        du, dwbd, dps = _pool_bwd(sv["hin"], dcat, wbd[l], sv["pscale"], name=f"l{l}_pool_bwd")
        dhin, dgq, dgkv = _norms_bwd(sv["hin"], sv["gq"], sv["gkv"], dcq, dckv, dkpe, du, pw=PW,
                                     name=f"l{l}_norms_bwd")
        pg = PW // len(POOL_WINDOWS)
        gS[("pool_w", l)] = jnp.stack([dwbd[g * pg:(g + 1) * pg, g * pg:(g + 1) * pg]
                                       for g in range(len(POOL_WINDOWS))])
        gS[("pool_scale", l)], gS[("q_norm_g", l)], gS[("kv_norm_g", l)] = dps, dgq, dgkv
        dxres, dyb = dx_through_ln(dhin, Wl["winp"], 0, True, dxres, (l, 0), f"l{l}_dx1_dwin",
                                   xin=sv["x1b"], wkey=("w_in", l))
        heads = rs_first_level(l, "b")
        riding = [heads["ex"]]
        if l == 0:
            small_keys = []
            for ll in range(L):
                small_keys += [("pool_w", ll), ("pool_scale", ll), ("q_norm_g", ll), ("kv_norm_g", ll)]
                small_keys += [("ln_g", ll, k) for k in range(4)] + [("ln_b", ll, k) for k in range(4)]
            flat = jnp.concatenate([loss_blk[0, :1]] + [gS[k].reshape(-1) for k in small_keys])
            n_small = flat.shape[0]
            rows = -(-n_small // (8 * LANE)) * 8
            flat = jnp.pad(flat, (0, rows * LANE - n_small)).reshape(1, rows, LANE)
            riding.append(_Gather([flat]))
        both = _Both(riding)
        below, rode = ffn_bwd(l, 0, sv["ffn1"], dxres, dyb, both, (l - 1, 3) if l > 0 else None)
        rode = both.split(rode)
        rs_last_level(heads, rode[0])
        if l > 0:
            dxres, dyb = below
            above = rs_first_level(l, "c")
    grad_x = below.reshape(1, T, D)

    red = _sum_devices(rode[1][0].reshape(N_DEV, rows, LANE), name="sum_small").reshape(-1)
    loss = red[0]
    gsm, pos = {}, 1
    for k in small_keys:
        size = math.prod(gS[k].shape)
        gsm[k] = red[pos:pos + size].reshape(gS[k].shape)
        pos += size

    me = 4 * lax.axis_index("x") + 2 * lax.axis_index("y") + lax.axis_index("c")
    dsh = D // N_DEV
    stack = lambda f: jnp.stack([f(l) for l in range(L)])
    g_ln_g = stack(lambda l: jnp.concatenate([gsm[("ln_g", l, k)] for k in range(4)], axis=0))
    g_ln_b = stack(lambda l: jnp.concatenate([gsm[("ln_b", l, k)] for k in range(4)], axis=0))
    swapped = {
        "ffn1_w13": stack(lambda l: gsh[("w13", 0, l)]),
        "ffn2_w13": stack(lambda l: gsh[("w13", 1, l)]),
        "w_in": stack(lambda l: gsh[("w_in", l)][:, :DIN].T),
        "w_uq": stack(lambda l: gsh[("w_uq", l)]),
        "w_ukv": stack(lambda l: gsh[("w_ukv", l)]),
    }
    swap = lambda a: jnp.swapaxes(a, 1, 2)
    grads = {
        "ln_g": lax.dynamic_slice_in_dim(g_ln_g, me * dsh, dsh, axis=2),
        "ln_b": lax.dynamic_slice_in_dim(g_ln_b, me * dsh, dsh, axis=2),
        "ffn1_w2": stack(lambda l: gsh[("w2", 0, l)]),
        "pool_w": stack(lambda l: gsm[("pool_w", l)]),
        "pool_scale": stack(lambda l: gsm[("pool_scale", l)].reshape(PW)),
        "q_norm_g": stack(lambda l: gsm[("q_norm_g", l)].reshape(Q_LORA)),
        "kv_norm_g": stack(lambda l: gsm[("kv_norm_g", l)].reshape(KV_LORA)),
        "w_out": stack(lambda l: gsh[("w_out", l)]),
        "mem_wq": stack(lambda l: gsh[("mem_wq", l)]),
        "mem_wkv": stack(lambda l: gsh[("mem_wkv", l)].T),
        "mem_wo": stack(lambda l: gsh[("mem_wo", l)]),
        "ffn2_w2": stack(lambda l: gsh[("w2", 1, l)]),
        **{nme: swap(g) for nme, g in swapped.items()},
    }

    names = ["ln_g", "ln_b", "ffn1_w13", "ffn1_w2", "w_in", "pool_w", "pool_scale", "q_norm_g", "w_uq",
             "kv_norm_g", "w_ukv", "w_out", "mem_wq", "mem_wkv", "mem_wo", "ffn2_w13", "ffn2_w2"]
    weights = dict(ln_g=ln_g, ln_b=ln_b, ffn1_w13=ffn1_w13, ffn1_w2=ffn1_w2, w_in=w_in, pool_w=pool_w,
                   pool_scale=pool_scale, q_norm_g=q_norm_g, w_uq=w_uq, kv_norm_g=kv_norm_g, w_ukv=w_ukv,
                   w_out=w_out, mem_wq=mem_wq, mem_wkv=mem_wkv, mem_wo=mem_wo, ffn2_w13=ffn2_w13,
                   ffn2_w2=ffn2_w2)
    ms = dict(ln_g=m_ln_g, ln_b=m_ln_b, ffn1_w13=m_ffn1_w13, ffn1_w2=m_ffn1_w2, w_in=m_w_in, pool_w=m_pool_w,
              pool_scale=m_pool_scale, q_norm_g=m_q_norm_g, w_uq=m_w_uq, kv_norm_g=m_kv_norm_g,
              w_ukv=m_w_ukv, w_out=m_w_out, mem_wq=m_mem_wq, mem_wkv=m_mem_wkv, mem_wo=m_mem_wo,
              ffn2_w13=m_ffn2_w13, ffn2_w2=m_ffn2_w2)
    vs = dict(ln_g=v_ln_g, ln_b=v_ln_b, ffn1_w13=v_ffn1_w13, ffn1_w2=v_ffn1_w2, w_in=v_w_in, pool_w=v_pool_w,
              pool_scale=v_pool_scale, q_norm_g=v_q_norm_g, w_uq=v_w_uq, kv_norm_g=v_kv_norm_g,
              w_ukv=v_w_ukv, w_out=v_w_out, mem_wq=v_mem_wq, mem_wkv=v_mem_wkv, mem_wo=v_mem_wo,
              ffn2_w13=v_ffn2_w13, ffn2_w2=v_ffn2_w2)
    deltas, new_m, new_v = [], [], []
    for nme in names:
        if nme in swapped:
            d, mn, vn = [swap(o) for o in _adamw(swap(weights[nme]), swapped[nme], swap(ms[nme]), swap(vs[nme]),
                                                 name=f"adamw_{nme}")]
        else:
            d, mn, vn = _adamw(weights[nme], grads[nme], ms[nme], vs[nme], name=f"adamw_{nme}")
        deltas.append(d)
        new_m.append(mn)
        new_v.append(vn)
    return (loss, grad_x, *[grads[nme] for nme in names], *deltas, *new_m, *new_v)
```
